```python
import math
import jax, jax.numpy as jnp
from jax import lax
import numpy as np


D_MODEL = 1024
BATCH = 16
SEQ = 2048
DEPTH = 1

N_MEM = 256
MLA_HEADS = 4
MLA_NOPE = 128
MLA_ROPE = 64
MLA_V = 128
Q_LORA_RANK = 384
KV_LORA_RANK = 256
Q_BLOCK = 128
ROPE_THETA = 10000.0
GDN_HEADS = 4
GDN_DK = 128
GDN_DV = 128
GDN_CONV = 4
GDN_CHUNK = 64
MEM_HEADS = 4
MEM_DH = 128
D_MIX = MLA_HEADS * MLA_V + GDN_HEADS * GDN_DV + MEM_HEADS * MEM_DH
GDN_QKV = 2 * GDN_HEADS * GDN_DK + GDN_HEADS * GDN_DV
IN_SPLITS = (Q_LORA_RANK, KV_LORA_RANK, MLA_ROPE, GDN_QKV, GDN_HEADS, GDN_HEADS, MEM_HEADS * MEM_DH, D_MIX)
D_IN = sum(IN_SPLITS)
EPS = 1e-6

kernel_name = 'hybrid_mla_gdn_memory_parallel_heads'


def rms_norm(t, gain):
    tf = t.astype(jnp.float32)
    y = tf * lax.rsqrt(jnp.mean(tf * tf, axis=-1, keepdims=True) + EPS)
    return (y * gain.astype(jnp.float32)).astype(t.dtype)


def l2_normalize(t):
    return t * lax.rsqrt(jnp.sum(t * t, axis=-1, keepdims=True) + EPS)


def split_cols(t, sizes):
    idx = [int(i) for i in np.cumsum(sizes)[:-1]]
    return jnp.split(t, idx, axis=-1)


def rope_tables(positions):
    half = MLA_ROPE // 2
    inv_freq = 1.0 / (ROPE_THETA ** (jnp.arange(half, dtype=jnp.float32) / half))
    ang = positions.astype(jnp.float32)[..., None] * inv_freq
    return jnp.cos(ang), jnp.sin(ang)


def apply_rope(t, cos, sin):
    half = t.shape[-1] // 2
    tf = t.astype(jnp.float32)
    t1, t2 = tf[..., :half], tf[..., half:]
    return jnp.concatenate([t1 * cos - t2 * sin, t2 * cos + t1 * sin], axis=-1).astype(t.dtype)


def causal_mla(q_nope, q_rope, k_nope, k_rope, v):
    S = q_nope.shape[1]
    scale = (MLA_NOPE + MLA_ROPE) ** -0.5
    outs = []
    for i in range(S // Q_BLOCK):
        lo, hi = i * Q_BLOCK, (i + 1) * Q_BLOCK
        s = (jnp.einsum('bqhd,bkhd->bhqk', q_nope[:, lo:hi], k_nope[:, :hi])
             + jnp.einsum('bqhr,bkr->bhqk', q_rope[:, lo:hi], k_rope[:, :hi])).astype(jnp.float32) * scale
        mask = jnp.arange(lo, hi)[:, None] >= jnp.arange(hi)[None, :]
        p = jax.nn.softmax(jnp.where(mask, s, -jnp.inf), axis=-1)
        outs.append(jnp.einsum('bhqk,bkhd->bqhd', p.astype(v.dtype), v[:, :hi]))
    return jnp.concatenate(outs, axis=1)


def causal_depthwise_conv(t, w):
    K, C = w.shape
    return lax.conv_general_dilated(t, w[:, None, :].astype(t.dtype), window_strides=(1,),
                                    padding=[(K - 1, 0)], dimension_numbers=('NWC', 'WIO', 'NWC'),
                                    feature_group_count=C)


def gated_delta_rule_chunked(q, k, v, g, beta):
    B, S, H, Dk = q.shape
    Dv = v.shape[-1]
    C = GDN_CHUNK
    N = S // C

    def chunks(t):
        return t.reshape(B, N, C, H, -1).transpose(0, 3, 1, 2, 4)

    q = chunks(q) * (Dk ** -0.5)
    k = chunks(k)
    v = chunks(v)
    g = jnp.cumsum(g.reshape(B, N, C, H).transpose(0, 3, 1, 2), axis=-1)
    beta = beta.reshape(B, N, C, H).transpose(0, 3, 1, 2)[..., None]
    incl = jnp.tril(jnp.ones((C, C), dtype=bool))
    strict = jnp.tril(jnp.ones((C, C), dtype=bool), -1)
    decay = jnp.exp(jnp.where(incl, g[..., :, None] - g[..., None, :], -jnp.inf))
    k_beta = k * beta
    L = jnp.where(strict, jnp.einsum('bhncd,bhnjd->bhncj', k_beta, k) * decay, 0.0)
    eye = jnp.eye(C, dtype=jnp.float32)
    T = lax.linalg.triangular_solve(L + eye, jnp.broadcast_to(eye, L.shape), left_side=True,
                                    lower=True, unit_diagonal=True)
    u = jnp.einsum('bhncj,bhnjv->bhncv', T, v * beta)
    w = jnp.einsum('bhncj,bhnjk->bhnck', T, k_beta * jnp.exp(g)[..., None])
    a_intra = jnp.einsum('bhncd,bhnjd->bhncj', q, k) * decay

    def step(state, xs):
        q_c, k_c, u_c, w_c, g_c, a_c = xs
        v_new = u_c - jnp.einsum('bhck,bhkv->bhcv', w_c, state)
        o = (jnp.einsum('bhck,bhkv->bhcv', q_c * jnp.exp(g_c)[..., None], state)
             + jnp.einsum('bhcj,bhjv->bhcv', a_c, v_new))
        g_last = g_c[..., -1:]
        state = (state * jnp.exp(g_last)[..., None]
                 + jnp.einsum('bhck,bhcv->bhkv', k_c * jnp.exp(g_last - g_c)[..., None], v_new))
        return state, o

    xs = tuple(jnp.moveaxis(t, 2, 0) for t in (q, k, u, w, g, a_intra))
    state0 = jnp.zeros((B, H, Dk, Dv), jnp.float32)
    _, o = lax.scan(step, state0, xs)
    return o.transpose(1, 0, 3, 2, 4).reshape(B, S, H, Dv)


def memory_attention(q, k, v):
    s = jnp.einsum('bqhd,bmhd->bhqm', q, k).astype(jnp.float32) * (MEM_DH ** -0.5)
    p = jax.nn.softmax(s, axis=-1)
    return jnp.einsum('bhqm,bmhd->bqhd', p.astype(v.dtype), v)


def _fwd_setup_inputs(seed: int = 0) -> dict:
    key = jax.random.key(seed)
    ks = jax.random.split(key, 20)
    f32 = jnp.float32

    def normal(k, shape, scale):
        return jax.random.normal(k, shape, f32) * scale

    def gain(k, shape):
        return 1.0 + 0.02 * jax.random.normal(k, shape, f32)

    x = normal(ks[0], (BATCH, SEQ, D_MODEL), 1.0)
    mem = normal(ks[1], (BATCH, N_MEM, D_MODEL), 1.0)
    positions = (jax.random.randint(ks[2], (BATCH, 1), 0, 4096) + jnp.arange(SEQ)[None, :]).astype(jnp.int32)
    norm_in = gain(ks[3], (DEPTH, D_MODEL))
    w_in = normal(ks[4], (DEPTH, D_MODEL, D_IN), D_MODEL ** -0.5)
    q_a_norm = gain(ks[5], (DEPTH, Q_LORA_RANK))
    w_q_b = normal(ks[6], (DEPTH, Q_LORA_RANK, MLA_HEADS * (MLA_NOPE + MLA_ROPE)), Q_LORA_RANK ** -0.5)
    kv_a_norm = gain(ks[7], (DEPTH, KV_LORA_RANK))
    w_kv_b = normal(ks[8], (DEPTH, KV_LORA_RANK, MLA_HEADS * (MLA_NOPE + MLA_V)), KV_LORA_RANK ** -0.5)
    gdn_conv = normal(ks[9], (DEPTH, GDN_CONV, GDN_QKV), GDN_CONV ** -0.5)
    gdn_a_log = jnp.log(jax.random.uniform(ks[10], (DEPTH, GDN_HEADS), f32, minval=1.0, maxval=16.0))
    dt = jnp.exp(jax.random.uniform(ks[11], (DEPTH, GDN_HEADS), f32, minval=math.log(1e-3), maxval=math.log(1e-1)))
    gdn_dt_bias = dt + jnp.log(-jnp.expm1(-dt))
    gdn_norm = gain(ks[12], (DEPTH, GDN_DV))
    mem_norm = gain(ks[13], (DEPTH, D_MODEL))
    w_mem_kv = normal(ks[14], (DEPTH, D_MODEL, 2 * MEM_HEADS * MEM_DH), D_MODEL ** -0.5)
    w_out = normal(ks[15], (DEPTH, D_MIX, D_MODEL), D_MIX ** -0.5)
    norm_final = gain(ks[16], (D_MODEL,))
    return {'x': x, 'mem': mem, 'positions': positions, 'norm_in': norm_in, 'w_in': w_in,
            'q_a_norm': q_a_norm, 'w_q_b': w_q_b, 'kv_a_norm': kv_a_norm, 'w_kv_b': w_kv_b,
            'gdn_conv': gdn_conv, 'gdn_a_log': gdn_a_log, 'gdn_dt_bias': gdn_dt_bias,
            'gdn_norm': gdn_norm, 'mem_norm': mem_norm, 'w_mem_kv': w_mem_kv, 'w_out': w_out,
            'norm_final': norm_final}


def _fwd_reference(x, mem, positions, norm_in, w_in, q_a_norm, w_q_b, kv_a_norm, w_kv_b, gdn_conv,
              gdn_a_log, gdn_dt_bias, gdn_norm, mem_norm, w_mem_kv, w_out, norm_final):
    B, S, _ = x.shape
    M = mem.shape[1]
    cos, sin = rope_tables(positions)
    for l in range(DEPTH):
        h = rms_norm(x, norm_in[l])
        c_q, c_kv, k_rope, gdn_qkv, gdn_a, gdn_b, mem_q, gate = split_cols(h @ w_in[l], IN_SPLITS)

        q = (rms_norm(c_q, q_a_norm[l]) @ w_q_b[l]).reshape(B, S, MLA_HEADS, MLA_NOPE + MLA_ROPE)
        q_nope = q[..., :MLA_NOPE]
        q_rope = apply_rope(q[..., MLA_NOPE:], cos[:, :, None], sin[:, :, None])
        kv = (rms_norm(c_kv, kv_a_norm[l]) @ w_kv_b[l]).reshape(B, S, MLA_HEADS, MLA_NOPE + MLA_V)
        k_nope, v_mla = kv[..., :MLA_NOPE], kv[..., MLA_NOPE:]
        k_rope = apply_rope(k_rope, cos, sin)
        o_mla = causal_mla(q_nope, q_rope, k_nope, k_rope, v_mla).reshape(B, S, MLA_HEADS * MLA_V)

        qkv = jax.nn.silu(causal_depthwise_conv(gdn_qkv, gdn_conv[l])).astype(jnp.float32)
        gq, gk, gv = split_cols(qkv, (GDN_HEADS * GDN_DK, GDN_HEADS * GDN_DK, GDN_HEADS * GDN_DV))
        gq = l2_normalize(gq.reshape(B, S, GDN_HEADS, GDN_DK))
        gk = l2_normalize(gk.reshape(B, S, GDN_HEADS, GDN_DK))
        gv = gv.reshape(B, S, GDN_HEADS, GDN_DV)
        beta = jax.nn.sigmoid(gdn_b.astype(jnp.float32))
        g = -jnp.exp(gdn_a_log[l].astype(jnp.float32)) * jax.nn.softplus(
            gdn_a.astype(jnp.float32) + gdn_dt_bias[l].astype(jnp.float32))
        o_gdn = gated_delta_rule_chunked(gq, gk, gv, g, beta)
        o_gdn = rms_norm(o_gdn, gdn_norm[l]).astype(x.dtype).reshape(B, S, GDN_HEADS * GDN_DV)

        mk, mv = split_cols(rms_norm(mem, mem_norm[l]) @ w_mem_kv[l], (MEM_HEADS * MEM_DH, MEM_HEADS * MEM_DH))
        o_mem = memory_attention(mem_q.reshape(B, S, MEM_HEADS, MEM_DH),
                                 mk.reshape(B, M, MEM_HEADS, MEM_DH),
                                 mv.reshape(B, M, MEM_HEADS, MEM_DH)).reshape(B, S, MEM_HEADS * MEM_DH)

        mixed = jnp.concatenate([o_mla, o_gdn, o_mem], axis=-1) * jax.nn.silu(gate)
        x = x + mixed @ w_out[l]
    return rms_norm(x, norm_final)


import jax as _jax
import jax.numpy as _jnp

TWIN_FORMAT = 'train_step'
FWD_PARAMS = ['x', 'mem', 'positions', 'norm_in', 'w_in', 'q_a_norm', 'w_q_b', 'kv_a_norm', 'w_kv_b', 'gdn_conv', 'gdn_a_log', 'gdn_dt_bias', 'gdn_norm', 'mem_norm', 'w_mem_kv', 'w_out', 'norm_final']
TWIN_WEIGHTS = ['norm_in', 'w_in', 'q_a_norm', 'w_q_b', 'kv_a_norm', 'w_kv_b', 'gdn_conv', 'gdn_a_log', 'gdn_dt_bias', 'gdn_norm', 'mem_norm', 'w_mem_kv', 'w_out', 'norm_final']
TWIN_DIFF_INPUT = 'x'
TWIN_INPUTS = ['x', 'mem', 'positions', 'norm_in', 'w_in', 'q_a_norm', 'w_q_b', 'kv_a_norm', 'w_kv_b', 'gdn_conv', 'gdn_a_log', 'gdn_dt_bias', 'gdn_norm', 'mem_norm', 'w_mem_kv', 'w_out', 'norm_final', 'loss_target', 'm_norm_in', 'm_w_in', 'm_q_a_norm', 'm_w_q_b', 'm_kv_a_norm', 'm_w_kv_b', 'm_gdn_conv', 'm_gdn_a_log', 'm_gdn_dt_bias', 'm_gdn_norm', 'm_mem_norm', 'm_w_mem_kv', 'm_w_out', 'm_norm_final', 'v_norm_in', 'v_w_in', 'v_q_a_norm', 'v_w_q_b', 'v_kv_a_norm', 'v_w_kv_b', 'v_gdn_conv', 'v_gdn_a_log', 'v_gdn_dt_bias', 'v_gdn_norm', 'v_mem_norm', 'v_w_mem_kv', 'v_w_out', 'v_norm_final']
TWIN_OUTPUTS = ['loss', 'grad_x', 'grad_norm_in', 'grad_w_in', 'grad_q_a_norm', 'grad_w_q_b', 'grad_kv_a_norm', 'grad_w_kv_b', 'grad_gdn_conv', 'grad_gdn_a_log', 'grad_gdn_dt_bias', 'grad_gdn_norm', 'grad_mem_norm', 'grad_w_mem_kv', 'grad_w_out', 'grad_norm_final', 'delta_norm_in', 'delta_w_in', 'delta_q_a_norm', 'delta_w_q_b', 'delta_kv_a_norm', 'delta_w_kv_b', 'delta_gdn_conv', 'delta_gdn_a_log', 'delta_gdn_dt_bias', 'delta_gdn_norm', 'delta_mem_norm', 'delta_w_mem_kv', 'delta_w_out', 'delta_norm_final', 'new_m_norm_in', 'new_m_w_in', 'new_m_q_a_norm', 'new_m_w_q_b', 'new_m_kv_a_norm', 'new_m_w_kv_b', 'new_m_gdn_conv', 'new_m_gdn_a_log', 'new_m_gdn_dt_bias', 'new_m_gdn_norm', 'new_m_mem_norm', 'new_m_w_mem_kv', 'new_m_w_out', 'new_m_norm_final', 'new_v_norm_in', 'new_v_w_in', 'new_v_q_a_norm', 'new_v_w_q_b', 'new_v_kv_a_norm', 'new_v_w_kv_b', 'new_v_gdn_conv', 'new_v_gdn_a_log', 'new_v_gdn_dt_bias', 'new_v_gdn_norm', 'new_v_mem_norm', 'new_v_w_mem_kv', 'new_v_w_out', 'new_v_norm_final']
TWIN_LEAF_KINDS = {'loss': 'loss', 'grad_x': 'grad_x', 'grad_norm_in': 'grad_w', 'grad_w_in': 'grad_w', 'grad_q_a_norm': 'grad_w', 'grad_w_q_b': 'grad_w', 'grad_kv_a_norm': 'grad_w', 'grad_w_kv_b': 'grad_w', 'grad_gdn_conv': 'grad_w', 'grad_gdn_a_log': 'grad_w', 'grad_gdn_dt_bias': 'grad_w', 'grad_gdn_norm': 'grad_w', 'grad_mem_norm': 'grad_w', 'grad_w_mem_kv': 'grad_w', 'grad_w_out': 'grad_w', 'grad_norm_final': 'grad_w', 'delta_norm_in': 'delta_w', 'delta_w_in': 'delta_w', 'delta_q_a_norm': 'delta_w', 'delta_w_q_b': 'delta_w', 'delta_kv_a_norm': 'delta_w', 'delta_w_kv_b': 'delta_w', 'delta_gdn_conv': 'delta_w', 'delta_gdn_a_log': 'delta_w', 'delta_gdn_dt_bias': 'delta_w', 'delta_gdn_norm': 'delta_w', 'delta_mem_norm': 'delta_w', 'delta_w_mem_kv': 'delta_w', 'delta_w_out': 'delta_w', 'delta_norm_final': 'delta_w', 'new_m_norm_in': 'new_m', 'new_m_w_in': 'new_m', 'new_m_q_a_norm': 'new_m', 'new_m_w_q_b': 'new_m', 'new_m_kv_a_norm': 'new_m', 'new_m_w_kv_b': 'new_m', 'new_m_gdn_conv': 'new_m', 'new_m_gdn_a_log': 'new_m', 'new_m_gdn_dt_bias': 'new_m', 'new_m_gdn_norm': 'new_m', 'new_m_mem_norm': 'new_m', 'new_m_w_mem_kv': 'new_m', 'new_m_w_out': 'new_m', 'new_m_norm_final': 'new_m', 'new_v_norm_in': 'new_v', 'new_v_w_in': 'new_v', 'new_v_q_a_norm': 'new_v', 'new_v_w_q_b': 'new_v', 'new_v_kv_a_norm': 'new_v', 'new_v_w_kv_b': 'new_v', 'new_v_gdn_conv': 'new_v', 'new_v_gdn_a_log': 'new_v', 'new_v_gdn_dt_bias': 'new_v', 'new_v_gdn_norm': 'new_v', 'new_v_mem_norm': 'new_v', 'new_v_w_mem_kv': 'new_v', 'new_v_w_out': 'new_v', 'new_v_norm_final': 'new_v'}


def _forward(args):
    return _fwd_reference(*[args[k] for k in FWD_PARAMS])


def _output_shape():
    out = _jax.eval_shape(lambda: _forward(_fwd_setup_inputs(0)))
    return out.shape, out.dtype

N_MICROBATCH = 1
ADAM_LR = 0.001
ADAM_B1 = 0.9
ADAM_B2 = 0.999
ADAM_EPS = 1e-08
ADAM_WD = 0.01
ADAM_STEP = 10
PER_EXAMPLE_BATCH_AXIS = {'x': 0, 'mem': 0, 'positions': 0, 'loss_target': 0}
SHARED_INPUTS = []
_WEIGHT_DTYPES = {'norm_in': _jnp.float32, 'w_in': _jnp.float32, 'q_a_norm': _jnp.float32, 'w_q_b': _jnp.float32, 'kv_a_norm': _jnp.float32, 'w_kv_b': _jnp.float32, 'gdn_conv': _jnp.float32, 'gdn_a_log': _jnp.float32, 'gdn_dt_bias': _jnp.float32, 'gdn_norm': _jnp.float32, 'mem_norm': _jnp.float32, 'w_mem_kv': _jnp.float32, 'w_out': _jnp.float32, 'norm_final': _jnp.float32}
MOMENT_SCALE = {'norm_in': 9.602620e-02, 'w_in': 4.822954e-02, 'q_a_norm': 1.702435e-02, 'w_q_b': 1.197391e-02, 'kv_a_norm': 2.984707e-02, 'w_kv_b': 1.525736e-02, 'gdn_conv': 6.220012e-02, 'gdn_a_log': 3.655970e-01, 'gdn_dt_bias': 3.475570e-01, 'gdn_norm': 2.261859e-01, 'mem_norm': 9.051026e-03, 'w_mem_kv': 8.700823e-03, 'w_out': 5.776108e-02, 'norm_final': 3.201269e+01}


def _to_microbatches(a, axis):
    t = _jnp.moveaxis(a, axis, 0)
    t = t.reshape((N_MICROBATCH, t.shape[0] // N_MICROBATCH) + t.shape[1:])
    return _jnp.moveaxis(t, 1, axis + 1)


def setup_inputs(seed: int = 0) -> dict:
    inp = _fwd_setup_inputs(seed)
    key = _jax.random.fold_in(_jax.random.key(seed), 7919)
    shape, _ = _output_shape()
    out = dict(inp)
    out["loss_target"] = _jax.random.normal(_jax.random.fold_in(key, 0), shape, _jnp.float32)
    for i, name in enumerate(TWIN_WEIGHTS):
        w = inp[name].astype(_jnp.float32)
        if MOMENT_SCALE is None:
            s = _jnp.sqrt(_jnp.mean(_jnp.square(w)) + 1e-30)
        else:
            s = MOMENT_SCALE[name]
        km, kv = _jax.random.split(_jax.random.fold_in(key, i + 1))
        out[name] = w
        out["m_" + name] = s * _jax.random.normal(km, w.shape, _jnp.float32)
        out["v_" + name] = (s * s) * _jax.random.uniform(kv, w.shape, _jnp.float32, 0.5, 1.5)
    if N_MICROBATCH > 1:
        for name, axis in PER_EXAMPLE_BATCH_AXIS.items():
            out[name] = _to_microbatches(out[name], axis)
    return {'x': out['x'], 'mem': out['mem'], 'positions': out['positions'], 'norm_in': out['norm_in'], 'w_in': out['w_in'], 'q_a_norm': out['q_a_norm'], 'w_q_b': out['w_q_b'], 'kv_a_norm': out['kv_a_norm'], 'w_kv_b': out['w_kv_b'], 'gdn_conv': out['gdn_conv'], 'gdn_a_log': out['gdn_a_log'], 'gdn_dt_bias': out['gdn_dt_bias'], 'gdn_norm': out['gdn_norm'], 'mem_norm': out['mem_norm'], 'w_mem_kv': out['w_mem_kv'], 'w_out': out['w_out'], 'norm_final': out['norm_final'], 'loss_target': out['loss_target'], 'm_norm_in': out['m_norm_in'], 'm_w_in': out['m_w_in'], 'm_q_a_norm': out['m_q_a_norm'], 'm_w_q_b': out['m_w_q_b'], 'm_kv_a_norm': out['m_kv_a_norm'], 'm_w_kv_b': out['m_w_kv_b'], 'm_gdn_conv': out['m_gdn_conv'], 'm_gdn_a_log': out['m_gdn_a_log'], 'm_gdn_dt_bias': out['m_gdn_dt_bias'], 'm_gdn_norm': out['m_gdn_norm'], 'm_mem_norm': out['m_mem_norm'], 'm_w_mem_kv': out['m_w_mem_kv'], 'm_w_out': out['m_w_out'], 'm_norm_final': out['m_norm_final'], 'v_norm_in': out['v_norm_in'], 'v_w_in': out['v_w_in'], 'v_q_a_norm': out['v_q_a_norm'], 'v_w_q_b': out['v_w_q_b'], 'v_kv_a_norm': out['v_kv_a_norm'], 'v_w_kv_b': out['v_w_kv_b'], 'v_gdn_conv': out['v_gdn_conv'], 'v_gdn_a_log': out['v_gdn_a_log'], 'v_gdn_dt_bias': out['v_gdn_dt_bias'], 'v_gdn_norm': out['v_gdn_norm'], 'v_mem_norm': out['v_mem_norm'], 'v_w_mem_kv': out['v_w_mem_kv'], 'v_w_out': out['v_w_out'], 'v_norm_final': out['v_norm_final']}


def _loss(weights, diff, rest, loss_target):
    with _jax.named_scope("forward"):
        args = {**rest, TWIN_DIFF_INPUT: diff, **{k: w.astype(_WEIGHT_DTYPES[k]) for k, w in weights.items()}}
        y = _forward(args)
    with _jax.named_scope("loss_head"):
        err = _jnp.square(y.astype(_jnp.float32) - loss_target)
        return 0.5 * _jnp.sum(_jnp.mean(err, axis=-1)) if err.ndim else 0.5 * err


def _adamw(w, g, m, v):
    m = ADAM_B1 * m + (1.0 - ADAM_B1) * g
    v = ADAM_B2 * v + (1.0 - ADAM_B2) * _jnp.square(g)
    m_hat = m / (1.0 - ADAM_B1 ** ADAM_STEP)
    v_hat = v / (1.0 - ADAM_B2 ** ADAM_STEP)
    delta = -ADAM_LR * (m_hat / (_jnp.sqrt(v_hat) + ADAM_EPS) + ADAM_WD * w)
    return delta, m, v


def reference(x, mem, positions, norm_in, w_in, q_a_norm, w_q_b, kv_a_norm, w_kv_b, gdn_conv, gdn_a_log, gdn_dt_bias, gdn_norm, mem_norm, w_mem_kv, w_out, norm_final, loss_target, m_norm_in, m_w_in, m_q_a_norm, m_w_q_b, m_kv_a_norm, m_w_kv_b, m_gdn_conv, m_gdn_a_log, m_gdn_dt_bias, m_gdn_norm, m_mem_norm, m_w_mem_kv, m_w_out, m_norm_final, v_norm_in, v_w_in, v_q_a_norm, v_w_q_b, v_kv_a_norm, v_w_kv_b, v_gdn_conv, v_gdn_a_log, v_gdn_dt_bias, v_gdn_norm, v_mem_norm, v_w_mem_kv, v_w_out, v_norm_final):
    given = dict(x=x, mem=mem, positions=positions, norm_in=norm_in, w_in=w_in, q_a_norm=q_a_norm, w_q_b=w_q_b, kv_a_norm=kv_a_norm, w_kv_b=w_kv_b, gdn_conv=gdn_conv, gdn_a_log=gdn_a_log, gdn_dt_bias=gdn_dt_bias, gdn_norm=gdn_norm, mem_norm=mem_norm, w_mem_kv=w_mem_kv, w_out=w_out, norm_final=norm_final, loss_target=loss_target, m_norm_in=m_norm_in, m_w_in=m_w_in, m_q_a_norm=m_q_a_norm, m_w_q_b=m_w_q_b, m_kv_a_norm=m_kv_a_norm, m_w_kv_b=m_w_kv_b, m_gdn_conv=m_gdn_conv, m_gdn_a_log=m_gdn_a_log, m_gdn_dt_bias=m_gdn_dt_bias, m_gdn_norm=m_gdn_norm, m_mem_norm=m_mem_norm, m_w_mem_kv=m_w_mem_kv, m_w_out=m_w_out, m_norm_final=m_norm_final, v_norm_in=v_norm_in, v_w_in=v_w_in, v_q_a_norm=v_q_a_norm, v_w_q_b=v_w_q_b, v_kv_a_norm=v_kv_a_norm, v_w_kv_b=v_w_kv_b, v_gdn_conv=v_gdn_conv, v_gdn_a_log=v_gdn_a_log, v_gdn_dt_bias=v_gdn_dt_bias, v_gdn_norm=v_gdn_norm, v_mem_norm=v_mem_norm, v_w_mem_kv=v_w_mem_kv, v_w_out=v_w_out, v_norm_final=v_norm_final)
    weights = {n: given[n] for n in TWIN_WEIGHTS}
    shared = {n: given[n] for n in SHARED_INPUTS}
    per_example = {n: given[n] for n in ['x', 'mem', 'positions']}
    grad_fn = _jax.value_and_grad(_loss, argnums=(0, 1))

    def one_microbatch(ex, loss_target):
        ex = dict(ex)
        diff = ex.pop(TWIN_DIFF_INPUT)
        return grad_fn(weights, diff, {**shared, **ex}, loss_target)

    if N_MICROBATCH == 1:
        loss, (grad_w, grad_x) = one_microbatch(per_example, given["loss_target"])
    else:
        def body(carry, xs):
            loss_sum, grad_sum = carry
            l_k, (gw_k, gx_k) = one_microbatch(xs[0], xs[1])
            with _jax.named_scope("update"):
                return (loss_sum + l_k, _jax.tree.map(_jnp.add, grad_sum, gw_k)), gx_k

        init = (_jnp.zeros((), _jnp.float32), _jax.tree.map(_jnp.zeros_like, weights))
        (loss, grad_w), grad_x = _jax.lax.scan(body, init, (per_example, given["loss_target"]))
    with _jax.named_scope("update"):
        delta_w, new_m, new_v = {}, {}, {}
        for n in TWIN_WEIGHTS:
            delta_w[n], new_m[n], new_v[n] = _adamw(weights[n], grad_w[n], given["m_" + n], given["v_" + n])
    return (loss, grad_x, *[grad_w[n] for n in TWIN_WEIGHTS], *[delta_w[n] for n in TWIN_WEIGHTS],
            *[new_m[n] for n in TWIN_WEIGHTS], *[new_v[n] for n in TWIN_WEIGHTS])
```

```python
import functools
import math

import jax
import jax.numpy as jnp
import numpy as np
from jax import lax
from jax.experimental import pallas as pl
from jax.experimental.pallas import tpu as pltpu

F32 = jnp.float32
BF16 = jnp.bfloat16
BS = pl.BlockSpec

D_MODEL = 1024
N_HEADS = 4
MLA_NOPE, MLA_ROPE, MLA_V = 128, 64, 128
Q_LORA, KV_LORA = 384, 256
ROPE_THETA = 10000.0
GDN_DK = GDN_DV = 128
GDN_CONV = 4
CHUNK = 64
MEM_DH = 128
D_MIX = 1536
GDN_QKV = 1536
D_IN = 4296
EPS = 1e-6
ADAM_LR, ADAM_B1, ADAM_B2, ADAM_EPS, ADAM_WD, ADAM_STEP = 0.001, 0.9, 0.999, 1e-08, 0.01, 10

OFF_MLA = 0
OFF_MEMQ = 1024
OFF_GDN = 1536
OFF_GATE = 3072
N_PAD = 4608
HEAD_PAD = 256
MLA_SCALE = (MLA_NOPE + MLA_ROPE) ** -0.5
MEM_SCALE = MEM_DH ** -0.5
GDN_SCALE = GDN_DK ** -0.5
NEG = -1e30

NN = ((1,), (0,))
NT = ((1,), (1,))
TN = ((0,), (0,))


def _dot(a, b, dims):
    return lax.dot_general(a, b, (dims, ((), ())), preferred_element_type=F32)


def _bdot(spec, a, b, precision=None):
    return jnp.einsum(spec, a, b, preferred_element_type=F32, precision=precision)


def _arb(n):
    return pltpu.CompilerParams(dimension_semantics=("arbitrary",) * n)


def _sigmoid(x):
    return 1.0 / (1.0 + jnp.exp(-x))


def _softplus(z):
    return jnp.maximum(z, 0.0) + jnp.log(1.0 + jnp.exp(-jnp.abs(z)))


def _rope(t, cos_row, sin_row):
    return t * cos_row + pltpu.roll(t, 64, 1) * sin_row


def _rope_bwd(d, cos_row, sin_row):
    return d * cos_row + pltpu.roll(d * sin_row, 64, 1)


def rms_fwd(x, gain, name, tm=512):
    T, n = x.shape
    tm = min(tm, T)

    def body(x_ref, g_ref, o_ref):
        xv = x_ref[...]
        r = lax.rsqrt(jnp.mean(xv * xv, axis=-1, keepdims=True) + EPS)
        o_ref[...] = (xv * r * g_ref[...]).astype(BF16)

    return pl.pallas_call(
        body, name=name, grid=(T // tm,),
        in_specs=[BS((tm, n), lambda i: (i, 0)), BS((1, n), lambda i: (0, 0))],
        out_specs=BS((tm, n), lambda i: (i, 0)),
        out_shape=jax.ShapeDtypeStruct((T, n), BF16), compiler_params=_arb(1))(x, gain)


def mm(a, b, kind, out_dtype, name, bm=512, bn=512, bk=512):
    if kind == "nn":
        (M, K), (_, N) = a.shape, b.shape
    elif kind == "nt":
        (M, K), (N, _) = a.shape, b.shape
    else:
        (K, M), (_, N) = a.shape, b.shape
    bm, bn, bk = min(bm, M), min(bn, N), min(bk, K)
    assert M % bm == 0 and N % bn == 0 and K % bk == 0, (name, M, N, K)
    nk = K // bk
    a_spec = BS((bk, bm), lambda i, j, k: (k, i)) if kind == "tn" else BS((bm, bk), lambda i, j, k: (i, k))
    b_spec = BS((bn, bk), lambda i, j, k: (j, k)) if kind == "nt" else BS((bk, bn), lambda i, j, k: (k, j))
    dims = {"nn": NN, "nt": NT, "tn": TN}[kind]

    def body(a_ref, b_ref, o_ref, acc):
        k = pl.program_id(2)

        @pl.when(k == 0)
        def _():
            acc[...] = jnp.zeros_like(acc)

        acc[...] += _dot(a_ref[...].astype(BF16), b_ref[...].astype(BF16), dims)

        @pl.when(k == nk - 1)
        def _():
            o_ref[...] = acc[...].astype(out_dtype)

    return pl.pallas_call(
        body, name=name, grid=(M // bm, N // bn, nk),
        in_specs=[a_spec, b_spec], out_specs=BS((bm, bn), lambda i, j, k: (i, j)),
        out_shape=jax.ShapeDtypeStruct((M, N), out_dtype),
        scratch_shapes=[pltpu.VMEM((bm, bn), F32)], compiler_params=_arb(3))(a, b)


def rope_tables(pos_col, inv_row, sgn_row, msk_row, tm=512):
    T = pos_col.shape[0]
    tm = min(tm, T)

    def body(p_ref, inv_ref, sgn_ref, msk_ref, c_ref, s_ref):
        ang = p_ref[...].astype(F32) * inv_ref[...]
        c_ref[...] = jnp.cos(ang) * msk_ref[...]
        s_ref[...] = jnp.sin(ang) * sgn_ref[...]

    row = BS((1, 128), lambda i: (0, 0))
    return pl.pallas_call(
        body, name="rope_tables", grid=(T // tm,),
        in_specs=[BS((tm, 1), lambda i: (i, 0)), row, row, row],
        out_specs=[BS((tm, 128), lambda i: (i, 0))] * 2,
        out_shape=[jax.ShapeDtypeStruct((T, 128), F32)] * 2, compiler_params=_arb(1))(pos_col, inv_row, sgn_row, msk_row)


def mla_prep(P, gq, gkv, wq, wkv, cos_t, sin_t, tm=512):
    T = P.shape[0]
    tm = min(tm, T)

    def body(p_ref, gq_ref, gkv_ref, wq_ref, wkv_ref, c_ref, s_ref, q_ref, k_ref, v_ref, qn_ref, kvn_ref):
        p = p_ref[...]
        cq, ckv, kr = p[:, :Q_LORA], p[:, Q_LORA:Q_LORA + KV_LORA], p[:, 640:768]
        qn = (cq * lax.rsqrt(jnp.mean(cq * cq, axis=-1, keepdims=True) + EPS) * gq_ref[...]).astype(BF16)
        kvn = (ckv * lax.rsqrt(jnp.mean(ckv * ckv, axis=-1, keepdims=True) + EPS) * gkv_ref[...]).astype(BF16)
        qn_ref[...] = qn
        kvn_ref[...] = kvn
        q = _dot(qn, wq_ref[...], NN)
        kv = _dot(kvn, wkv_ref[...], NN)
        cos_row, sin_row = c_ref[...], s_ref[...]
        krr = _rope(kr, cos_row, sin_row).astype(BF16)
        for h in range(N_HEADS):
            lo = h * HEAD_PAD
            q_ref[:, lo:lo + 128] = (q[:, lo:lo + 128] * MLA_SCALE).astype(BF16)
            q_ref[:, lo + 128:lo + 256] = (_rope(q[:, lo + 128:lo + 256], cos_row, sin_row) * MLA_SCALE).astype(BF16)
            k_ref[:, lo:lo + 128] = kv[:, h * 128:(h + 1) * 128].astype(BF16)
            k_ref[:, lo + 128:lo + 256] = krr
        v_ref[...] = kv[:, 512:].astype(BF16)

    full = lambda r, c: BS((r, c), lambda i: (0, 0))
    rowb = lambda c: BS((tm, c), lambda i: (i, 0))
    return pl.pallas_call(
        body, name="mla_prep", grid=(T // tm,),
        in_specs=[rowb(1024), full(1, Q_LORA), full(1, KV_LORA), full(Q_LORA, 1024), full(KV_LORA, 1024), rowb(128), rowb(128)],
        out_specs=[rowb(1024), rowb(1024), rowb(512), rowb(Q_LORA), rowb(KV_LORA)],
        out_shape=[jax.ShapeDtypeStruct((T, 1024), BF16), jax.ShapeDtypeStruct((T, 1024), BF16),
                   jax.ShapeDtypeStruct((T, 512), BF16), jax.ShapeDtypeStruct((T, Q_LORA), BF16),
                   jax.ShapeDtypeStruct((T, KV_LORA), BF16)],
        compiler_params=_arb(1))(P, gq, gkv, wq, wkv, cos_t, sin_t)


def mla_attn_fwd(Q, K, V, B, S, tq=512):
    T = B * S
    tq = min(tq, S)
    nq = S // tq

    def body(q_ref, k_ref, v_ref, o_ref, lse_ref, m_s, l_s, acc_s):
        i = pl.program_id(2)
        q = q_ref[...]
        m_s[...] = jnp.full_like(m_s, NEG)
        l_s[...] = jnp.zeros_like(l_s)
        acc_s[...] = jnp.zeros_like(acc_s)

        def blk(j, masked):
            rows = pl.ds(pl.multiple_of(j * tq, tq), tq)
            s = _dot(q, k_ref[rows, :], NT)
            if masked:
                r = lax.broadcasted_iota(jnp.int32, (tq, tq), 0)
                c = lax.broadcasted_iota(jnp.int32, (tq, tq), 1)
                s = jnp.where(r >= c, s, NEG)
            m_prev = m_s[...]
            m_new = jnp.maximum(m_prev, jnp.max(s, axis=1, keepdims=True))
            p = jnp.exp(s - m_new)
            alpha = jnp.exp(m_prev - m_new)
            l_s[...] = alpha * l_s[...] + jnp.sum(p, axis=1, keepdims=True)
            acc_s[...] = alpha * acc_s[...] + _dot(p.astype(BF16), v_ref[rows, :], NN)
            m_s[...] = m_new

        def loop(j, c):
            blk(j, False)
            return c

        lax.fori_loop(0, i, loop, 0)
        blk(i, True)
        o_ref[...] = acc_s[...] / l_s[...]
        lse_ref[...] = m_s[...] + jnp.log(l_s[...])

    return pl.pallas_call(
        body, name="mla_attn_fwd", grid=(B, N_HEADS, nq),
        in_specs=[BS((tq, HEAD_PAD), lambda b, h, i: (b * nq + i, h)),
                  BS((S, HEAD_PAD), lambda b, h, i: (b, h)),
                  BS((S, 128), lambda b, h, i: (b, h))],
        out_specs=[BS((tq, 128), lambda b, h, i: (b * nq + i, h)),
                   BS((None, tq, 1), lambda b, h, i: (h, b * nq + i, 0))],
        out_shape=[jax.ShapeDtypeStruct((T, 512), F32), jax.ShapeDtypeStruct((N_HEADS, T, 1), F32)],
        scratch_shapes=[pltpu.VMEM((tq, 1), F32), pltpu.VMEM((tq, 1), F32), pltpu.VMEM((tq, 128), F32)],
        compiler_params=_arb(3))(Q, K, V)


def mla_attn_bwd(Q, K, V, O, dO, LSE, B, S, tq=512):
    T = B * S
    tq = min(tq, S)
    nq = S // tq

    def body(q_ref, k_ref, v_ref, o_ref, do_ref, lse_ref, dq_ref, dk_ref, dv_ref, delta_s, dk_s, dv_s):
        j = pl.program_id(2)

        @pl.when(j == 0)
        def _():
            dq_ref[...] = jnp.zeros_like(dq_ref)
            delta_s[...] = jnp.sum(do_ref[...] * o_ref[...], axis=1, keepdims=True)

        dk_s[...] = jnp.zeros_like(dk_s)
        dv_s[...] = jnp.zeros_like(dv_s)
        k = k_ref[...]
        v = v_ref[...]

        def step(i, c):
            rows = pl.ds(pl.multiple_of(i * tq, tq), tq)
            q = q_ref[rows, :]
            do = do_ref[rows, :].astype(BF16)
            s = _dot(q, k, NT)
            r = i * tq + lax.broadcasted_iota(jnp.int32, (tq, tq), 0)
            cc = j * tq + lax.broadcasted_iota(jnp.int32, (tq, tq), 1)
            p = jnp.where(r >= cc, jnp.exp(s - lse_ref[rows, :]), 0.0)
            dv_s[...] += _dot(p.astype(BF16), do, TN)
            dp = _dot(do, v, NT)
            ds = (p * (dp - delta_s[rows, :])).astype(BF16)
            dk_s[...] += _dot(ds, q, TN)
            dq_ref[rows, :] += _dot(ds, k, NN)
            return c

        lax.fori_loop(j, nq, step, 0)
        dk_ref[...] = dk_s[...]
        dv_ref[...] = dv_s[...]

    seq = lambda c: BS((S, c), lambda b, h, j: (b, h))
    return pl.pallas_call(
        body, name="mla_attn_bwd", grid=(B, N_HEADS, nq),
        in_specs=[seq(HEAD_PAD), BS((tq, HEAD_PAD), lambda b, h, j: (b * nq + j, h)),
                  BS((tq, 128), lambda b, h, j: (b * nq + j, h)), seq(128), seq(128),
                  BS((None, S, 1), lambda b, h, j: (h, b, 0))],
        out_specs=[seq(HEAD_PAD), BS((tq, HEAD_PAD), lambda b, h, j: (b * nq + j, h)),
                   BS((tq, 128), lambda b, h, j: (b * nq + j, h))],
        out_shape=[jax.ShapeDtypeStruct((T, 1024), F32), jax.ShapeDtypeStruct((T, 1024), F32),
                   jax.ShapeDtypeStruct((T, 512), F32)],
        scratch_shapes=[pltpu.VMEM((S, 1), F32), pltpu.VMEM((tq, HEAD_PAD), F32), pltpu.VMEM((tq, 128), F32)],
        compiler_params=_arb(3))(Q, K, V, O, dO, LSE)


def mla_post_bwd(dQ, dK, dV, cos_t, sin_t, tm=512):
    T = dQ.shape[0]
    tm = min(tm, T)

    def body(dq_ref, dk_ref, dv_ref, c_ref, s_ref, ql_ref, kvl_ref, kr_ref):
        cos_row, sin_row = c_ref[...], s_ref[...]
        kr = jnp.zeros((tm, 128), F32)
        for h in range(N_HEADS):
            lo = h * HEAD_PAD
            ql_ref[:, lo:lo + 128] = (dq_ref[:, lo:lo + 128] * MLA_SCALE).astype(BF16)
            ql_ref[:, lo + 128:lo + 256] = (_rope_bwd(dq_ref[:, lo + 128:lo + 256], cos_row, sin_row) * MLA_SCALE).astype(BF16)
            kvl_ref[:, h * 128:(h + 1) * 128] = dk_ref[:, lo:lo + 128].astype(BF16)
            kr = kr + dk_ref[:, lo + 128:lo + 256]
        kvl_ref[:, 512:] = dv_ref[...].astype(BF16)
        kr_ref[...] = _rope_bwd(kr, cos_row, sin_row)

    rowb = lambda c: BS((tm, c), lambda i: (i, 0))
    return pl.pallas_call(
        body, name="mla_post_bwd", grid=(T // tm,),
        in_specs=[rowb(1024), rowb(1024), rowb(512), rowb(128), rowb(128)],
        out_specs=[rowb(1024), rowb(1024), rowb(128)],
        out_shape=[jax.ShapeDtypeStruct((T, 1024), BF16), jax.ShapeDtypeStruct((T, 1024), BF16),
                   jax.ShapeDtypeStruct((T, 128), F32)],
        compiler_params=_arb(1))(dQ, dK, dV, cos_t, sin_t)


def mla_norm_bwd(P, dqn, dkvn, dkr, dab, gq, gkv, tm=512):
    T = P.shape[0]
    tm = min(tm, T)

    def norm_bwd(x, dy, g):
        r = lax.rsqrt(jnp.mean(x * x, axis=-1, keepdims=True) + EPS)
        xh = x * r
        dxh = dy * g
        return r * (dxh - xh * jnp.mean(dxh * xh, axis=-1, keepdims=True)), jnp.sum(dy * xh, axis=0, keepdims=True)

    def body(p_ref, dqn_ref, dkvn_ref, dkr_ref, dab_ref, gq_ref, gkv_ref, o_ref, aq_ref, akv_ref):
        @pl.when(pl.program_id(0) == 0)
        def _():
            aq_ref[...] = jnp.zeros_like(aq_ref)
            akv_ref[...] = jnp.zeros_like(akv_ref)

        dcq, ggq = norm_bwd(p_ref[:, :Q_LORA], dqn_ref[...], gq_ref[...])
        dckv, ggkv = norm_bwd(p_ref[:, Q_LORA:640], dkvn_ref[...], gkv_ref[...])
        aq_ref[...] += ggq
        akv_ref[...] += ggkv
        o_ref[:, :Q_LORA] = dcq.astype(BF16)
        o_ref[:, Q_LORA:640] = dckv.astype(BF16)
        o_ref[:, 640:768] = dkr_ref[...].astype(BF16)
        o_ref[:, 768:896] = dab_ref[...]
        o_ref[:, 896:1024] = jnp.zeros((tm, 128), BF16)

    rowb = lambda c: BS((tm, c), lambda i: (i, 0))
    full = lambda c: BS((1, c), lambda i: (0, 0))
    return pl.pallas_call(
        body, name="mla_norm_bwd", grid=(T // tm,),
        in_specs=[rowb(1024), rowb(Q_LORA), rowb(KV_LORA), rowb(128), rowb(128), full(Q_LORA), full(KV_LORA)],
        out_specs=[rowb(1024), full(Q_LORA), full(KV_LORA)],
        out_shape=[jax.ShapeDtypeStruct((T, 1024), BF16), jax.ShapeDtypeStruct((1, Q_LORA), F32),
                   jax.ShapeDtypeStruct((1, KV_LORA), F32)],
        compiler_params=_arb(1))(P, dqn, dkvn, dkr, dab, gq, gkv)


def _mem_probs(qh, kh):
    s = _dot(qh, kh, NT) * MEM_SCALE
    p = jnp.exp(s - jnp.max(s, axis=1, keepdims=True))
    return p / jnp.sum(p, axis=1, keepdims=True)


def mem_attn_fwd(P, MKV, B, S, M, tq=512):
    T = B * S
    tq = min(tq, S)
    nq = S // tq

    def body(q_ref, kv_ref, o_ref):
        for h in range(N_HEADS):
            sl = slice(h * 128, (h + 1) * 128)
            p = _mem_probs(q_ref[:, sl].astype(BF16), kv_ref[:, sl])
            o_ref[:, sl] = _dot(p.astype(BF16), kv_ref[:, 512 + h * 128:512 + (h + 1) * 128], NN)

    return pl.pallas_call(
        body, name="mem_attn_fwd", grid=(B, nq),
        in_specs=[BS((tq, 512), lambda b, i: (b * nq + i, OFF_MEMQ // 512)), BS((M, 1024), lambda b, i: (b, 0))],
        out_specs=BS((tq, 512), lambda b, i: (b * nq + i, 0)),
        out_shape=jax.ShapeDtypeStruct((T, 512), F32), compiler_params=_arb(2))(P, MKV)


def mem_attn_bwd(P, MKV, dO, B, S, M, tq=512):
    T = B * S
    tq = min(tq, S)
    nq = S // tq

    def body(q_ref, kv_ref, do_ref, dq_ref, dkv_ref):
        @pl.when(pl.program_id(1) == 0)
        def _():
            dkv_ref[...] = jnp.zeros_like(dkv_ref)

        for h in range(N_HEADS):
            sl = slice(h * 128, (h + 1) * 128)
            sv = slice(512 + h * 128, 512 + (h + 1) * 128)
            qh = q_ref[:, sl].astype(BF16)
            kh = kv_ref[:, sl]
            do = do_ref[:, sl].astype(BF16)
            p = _mem_probs(qh, kh)
            dkv_ref[:, sv] += _dot(p.astype(BF16), do, TN)
            dp = _dot(do, kv_ref[:, sv], NT)
            ds = (p * (dp - jnp.sum(dp * p, axis=1, keepdims=True)) * MEM_SCALE).astype(BF16)
            dq_ref[:, sl] = _dot(ds, kh, NN).astype(BF16)
            dkv_ref[:, sl] += _dot(ds, qh, TN)

    return pl.pallas_call(
        body, name="mem_attn_bwd", grid=(B, nq),
        in_specs=[BS((tq, 512), lambda b, i: (b * nq + i, OFF_MEMQ // 512)), BS((M, 1024), lambda b, i: (b, 0)),
                  BS((tq, 512), lambda b, i: (b * nq + i, 0))],
        out_specs=[BS((tq, 512), lambda b, i: (b * nq + i, 0)), BS((M, 1024), lambda b, i: (b, 0))],
        out_shape=[jax.ShapeDtypeStruct((T, 512), BF16), jax.ShapeDtypeStruct((B * M, 1024), F32)],
        compiler_params=_arb(2))(P, MKV, dO)


def gain_grad(x, dy, name, tm=256):
    T, n = x.shape
    tm = min(tm, T)

    def body(x_ref, dy_ref, o_ref):
        @pl.when(pl.program_id(0) == 0)
        def _():
            o_ref[...] = jnp.zeros_like(o_ref)

        xv = x_ref[...]
        xh = xv * lax.rsqrt(jnp.mean(xv * xv, axis=-1, keepdims=True) + EPS)
        o_ref[...] += jnp.sum(dy_ref[...] * xh, axis=0, keepdims=True)

    return pl.pallas_call(
        body, name=name, grid=(T // tm,),
        in_specs=[BS((tm, n), lambda i: (i, 0))] * 2, out_specs=BS((1, n), lambda i: (0, 0)),
        out_shape=jax.ShapeDtypeStruct((1, n), F32), compiler_params=_arb(1))(x, dy)


def _conv_silu(x, w, t):
    y = x * w[3:4, :]
    for s in range(1, GDN_CONV):
        y = y + jnp.where(t >= s, pltpu.roll(x, s, 0), 0.0) * w[3 - s:4 - s, :]
    return y, _sigmoid(y)


def gdn_prep_fwd(P, conv_w, B, S):
    T = B * S

    def body(x_ref, w_ref, o_ref):
        kind = pl.program_id(1)
        t = lax.broadcasted_iota(jnp.int32, (S, 1), 0)
        y, sg = _conv_silu(x_ref[...], w_ref[...], t)
        a = y * sg
        scale = jnp.where(kind == 0, GDN_SCALE, 1.0).astype(F32)
        for h in range(N_HEADS):
            sl = slice(h * 128, (h + 1) * 128)
            seg = a[:, sl]
            n = lax.rsqrt(jnp.sum(seg * seg, axis=-1, keepdims=True) + EPS)
            o_ref[:, sl] = jnp.where(kind < 2, seg * (n * scale), seg)

    return pl.pallas_call(
        body, name="gdn_prep_fwd", grid=(B, 3),
        in_specs=[BS((S, 512), lambda b, k: (b, OFF_GDN // 512 + k)), BS((GDN_CONV, 512), lambda b, k: (0, k))],
        out_specs=BS((S, 512), lambda b, k: (b, k)),
        out_shape=jax.ShapeDtypeStruct((T, GDN_QKV), F32), compiler_params=_arb(2))(P, conv_w)


def gdn_prep_bwd(P, dqkv, conv_w, B, S):
    T = B * S

    def body(x_ref, d_ref, w_ref, o_ref, gw_ref):
        kind = pl.program_id(0)

        @pl.when(pl.program_id(1) == 0)
        def _():
            gw_ref[...] = jnp.zeros_like(gw_ref)

        t = lax.broadcasted_iota(jnp.int32, (S, 1), 0)
        x = x_ref[...]
        w = w_ref[...]
        y, sg = _conv_silu(x, w, t)
        a = y * sg
        scale = jnp.where(kind == 0, GDN_SCALE, 1.0).astype(F32)
        das = []
        for h in range(N_HEADS):
            sl = slice(h * 128, (h + 1) * 128)
            seg, dseg = a[:, sl], d_ref[:, sl]
            n = lax.rsqrt(jnp.sum(seg * seg, axis=-1, keepdims=True) + EPS)
            dn = scale * (n * dseg - seg * (n * n * n) * jnp.sum(dseg * seg, axis=-1, keepdims=True))
            das.append(jnp.where(kind < 2, dn, dseg))
        dy = jnp.concatenate(das, axis=1) * (sg * (1.0 + y * (1.0 - sg)))
        dx = dy * w[3:4, :]
        gw_ref[3:4, :] += jnp.sum(dy * x, axis=0, keepdims=True)
        for s in range(1, GDN_CONV):
            dx = dx + jnp.where(t + s < S, pltpu.roll(dy, S - s, 0), 0.0) * w[3 - s:4 - s, :]
            gw_ref[3 - s:4 - s, :] += jnp.sum(dy * jnp.where(t >= s, pltpu.roll(x, s, 0), 0.0), axis=0, keepdims=True)
        o_ref[...] = dx.astype(BF16)

    return pl.pallas_call(
        body, name="gdn_prep_bwd", grid=(3, B),
        in_specs=[BS((S, 512), lambda k, b: (b, OFF_GDN // 512 + k)), BS((S, 512), lambda k, b: (b, k)),
                  BS((GDN_CONV, 512), lambda k, b: (0, k))],
        out_specs=[BS((S, 512), lambda k, b: (b, k)), BS((GDN_CONV, 512), lambda k, b: (0, k))],
        out_shape=[jax.ShapeDtypeStruct((T, GDN_QKV), BF16), jax.ShapeDtypeStruct((GDN_CONV, GDN_QKV), F32)],
        compiler_params=_arb(2))(P, dqkv, conv_w)


def _chunk_row(n_rows):
    return lax.broadcasted_iota(jnp.int32, (n_rows, 1), 0) % CHUNK


def gdn_gate_fwd(P, alog_row, dt_row, B, S):
    T = B * S

    def body(x_ref, al_ref, dt_ref, o_ref):
        x = x_ref[...]
        lane = lax.broadcasted_iota(jnp.int32, (1, 128), 1)
        g = jnp.where(lane < 4, -jnp.exp(al_ref[...]) * _softplus(x + dt_ref[...]), 0.0)
        t = _chunk_row(S)
        for s in (1, 2, 4, 8, 16, 32):
            g = g + jnp.where(t >= s, pltpu.roll(g, s, 0), 0.0)
        o_ref[...] = jnp.where(lane < 4, g, jnp.where(lane < 8, _sigmoid(x), 0.0))

    row = BS((1, 128), lambda b: (0, 0))
    return pl.pallas_call(
        body, name="gdn_gate_fwd", grid=(B,),
        in_specs=[BS((S, 128), lambda b: (b, 768 // 128)), row, row], out_specs=BS((S, 128), lambda b: (b, 0)),
        out_shape=jax.ShapeDtypeStruct((T, 128), F32), compiler_params=_arb(1))(P, alog_row, dt_row)


def gdn_gate_bwd(P, dGB, alog_row, dt_row, B, S):
    T = B * S

    def body(x_ref, d_ref, al_ref, dt_ref, o_ref, acc_ref):
        @pl.when(pl.program_id(0) == 0)
        def _():
            acc_ref[...] = jnp.zeros_like(acc_ref)

        x, d = x_ref[...], d_ref[...]
        lane = lax.broadcasted_iota(jnp.int32, (1, 128), 1)
        z = x + dt_ref[...]
        coef = -jnp.exp(al_ref[...])
        g = coef * _softplus(z)
        da = jnp.where(lane < 4, d * coef * _sigmoid(z), 0.0)
        beta = _sigmoid(x)
        o_ref[...] = jnp.where(lane < 4, da, jnp.where(lane < 8, d * beta * (1.0 - beta), 0.0)).astype(BF16)
        acc_ref[0:1, :] += jnp.sum(jnp.where(lane < 4, d * g, 0.0), axis=0, keepdims=True)
        acc_ref[1:2, :] += jnp.sum(da, axis=0, keepdims=True)

    row = BS((1, 128), lambda b: (0, 0))
    return pl.pallas_call(
        body, name="gdn_gate_bwd", grid=(B,),
        in_specs=[BS((S, 128), lambda b: (b, 768 // 128)), BS((S, 128), lambda b: (b, 0)), row, row],
        out_specs=[BS((S, 128), lambda b: (b, 0)), BS((8, 128), lambda b: (0, 0))],
        out_shape=[jax.ShapeDtypeStruct((T, 128), BF16), jax.ShapeDtypeStruct((8, 128), F32)],
        compiler_params=_arb(1))(P, dGB, alog_row, dt_row)


def _chunk_masks(nc):
    r = lax.broadcasted_iota(jnp.int32, (nc, CHUNK, CHUNK), 1)
    c = lax.broadcasted_iota(jnp.int32, (nc, CHUNK, CHUNK), 2)
    return r >= c, r > c


def _chunk_local(q, k, gc, gr, beta, incl, strict):
    decay = jnp.exp(jnp.where(incl, gc - gr, NEG))
    kb = k * beta
    kbf = k.astype(BF16)
    m_kk = _bdot("gcd,gjd->gcj", kb.astype(BF16), kbf)
    l_mat = jnp.where(strict, m_kk * decay, 0.0)
    a_mat = _bdot("gcd,gjd->gcj", q.astype(BF16), kbf) * decay
    return decay, kb, l_mat, a_mat


def gdn_chunk_fwd(qkv, GB, Grow, B, S, nc=8):
    T = B * S
    N = S // CHUNK
    nc = min(nc, N)
    nb = N // nc
    R = nc * CHUNK
    hi = lax.Precision.HIGHEST

    def body(q_ref, k_ref, v_ref, gb_ref, gr_ref, u_ref, w_ref, t_ref, a_ref):
        incl, strict = _chunk_masks(nc)
        eye = (lax.broadcasted_iota(jnp.int32, (nc, CHUNK, CHUNK), 1)
               == lax.broadcasted_iota(jnp.int32, (nc, CHUNK, CHUNK), 2)).astype(F32)
        for h in range(N_HEADS):
            sl = slice(h * 128, (h + 1) * 128)
            q = q_ref[:, sl].reshape(nc, CHUNK, 128)
            k = k_ref[:, sl].reshape(nc, CHUNK, 128)
            v = v_ref[:, sl].reshape(nc, CHUNK, 128)
            gc = gb_ref[:, h:h + 1].reshape(nc, CHUNK, 1)
            beta = gb_ref[:, 4 + h:5 + h].reshape(nc, CHUNK, 1)
            gr = gr_ref[h][:, None, :]
            _, kb, l_mat, a_mat = _chunk_local(q, k, gc, gr, beta, incl, strict)
            pw = -l_mat
            tinv = eye + pw
            for _ in range(5):
                pw = _bdot("gij,gjk->gik", pw, pw, hi)
                tinv = tinv + _bdot("gij,gjk->gik", tinv, pw, hi)
            tb = tinv.astype(BF16)
            u = _bdot("gcj,gjv->gcv", tb, (v * beta).astype(BF16))
            w = _bdot("gcj,gjk->gck", tb, (kb * jnp.exp(gc)).astype(BF16))
            u_ref[:, sl] = u.reshape(R, 128)
            w_ref[:, sl] = w.reshape(R, 128)
            t_ref[h] = tinv
            a_ref[h] = a_mat

    rowb = lambda c, j: BS((R, c), lambda b, n: (b * nb + n, j))
    mat = BS((None, N_HEADS, nc, CHUNK, CHUNK), lambda b, n: (b, 0, n, 0, 0))
    return pl.pallas_call(
        body, name="gdn_chunk_fwd", grid=(B, nb),
        in_specs=[rowb(512, 0), rowb(512, 1), rowb(512, 2), rowb(128, 0),
                  BS((None, N_HEADS, nc, CHUNK), lambda b, n: (b, 0, n, 0))],
        out_specs=[rowb(512, 0), rowb(512, 0), mat, mat],
        out_shape=[jax.ShapeDtypeStruct((T, 512), F32), jax.ShapeDtypeStruct((T, 512), F32),
                   jax.ShapeDtypeStruct((B, N_HEADS, N, CHUNK, CHUNK), F32),
                   jax.ShapeDtypeStruct((B, N_HEADS, N, CHUNK, CHUNK), F32)],
        compiler_params=_arb(2))(qkv, qkv, qkv, GB, Grow)


def gdn_scan_fwd(qkv3, U3, W3, GB3, A, B, S):
    N = S // CHUNK

    def body(q_ref, k_ref, u_ref, w_ref, gb_ref, a_ref, o_ref, vn_ref, st_ref, s_s):
        @pl.when(pl.program_id(0) == 0)
        def _():
            s_s[...] = jnp.zeros_like(s_s)

        for b in range(B):
            for h in range(N_HEADS):
                sl = slice(h * 128, (h + 1) * 128)
                st = s_s[b, h]
                st_ref[b, h] = st
                stb = st.astype(BF16)
                g = gb_ref[b, :, h:h + 1]
                gl = g[CHUNK - 1:CHUNK, :]
                vn = u_ref[b, :, sl] - _dot(w_ref[b, :, sl].astype(BF16), stb, NN)
                vnb = vn.astype(BF16)
                o = _dot((q_ref[b, :, sl] * jnp.exp(g)).astype(BF16), stb, NN) + _dot(a_ref[b, h].astype(BF16), vnb, NN)
                vn_ref[b, :, sl] = vn
                o_ref[b, :, sl] = o
                s_s[b, h] = st * jnp.exp(gl) + _dot((k_ref[b, :, sl] * jnp.exp(gl - g)).astype(BF16), vnb, TN)

    tok = lambda c, j: BS((B, CHUNK, c), lambda n: (0, n, j))
    return pl.pallas_call(
        body, name="gdn_scan_fwd", grid=(N,),
        in_specs=[tok(512, 0), tok(512, 1), tok(512, 0), tok(512, 0), tok(128, 0),
                  BS((B, N_HEADS, None, CHUNK, CHUNK), lambda n: (0, 0, n, 0, 0))],
        out_specs=[tok(512, 0), tok(512, 0), BS((B, N_HEADS, None, 128, 128), lambda n: (0, 0, n, 0, 0))],
        out_shape=[jax.ShapeDtypeStruct((B, S, 512), F32), jax.ShapeDtypeStruct((B, S, 512), F32),
                   jax.ShapeDtypeStruct((B, N_HEADS, N, 128, 128), F32)],
        scratch_shapes=[pltpu.VMEM((B, N_HEADS, 128, 128), F32)],
        compiler_params=_arb(1))(qkv3, qkv3, U3, W3, GB3, A)


def gdn_scan_bwd(dO3, qkv3, W3, Vn3, GB3, A, St, B, S):
    N = S // CHUNK

    def body(do_ref, q_ref, k_ref, w_ref, vn_ref, gb_ref, a_ref, st_ref,
             du_ref, dw_ref, dq_ref, dk_ref, da_ref, dg_ref, ds_s):
        @pl.when(pl.program_id(0) == 0)
        def _():
            ds_s[...] = jnp.zeros_like(ds_s)

        lane = lax.broadcasted_iota(jnp.int32, (1, 128), 1)
        last = lax.broadcasted_iota(jnp.int32, (CHUNK, 1), 0) == CHUNK - 1
        for b in range(B):
            dg_all = jnp.zeros((CHUNK, 128), F32)
            for h in range(N_HEADS):
                sl = slice(h * 128, (h + 1) * 128)
                st = st_ref[b, h]
                stb = st.astype(BF16)
                dsn = ds_s[b, h]
                dsnb = dsn.astype(BF16)
                g = gb_ref[b, :, h:h + 1]
                gl = g[CHUNK - 1:CHUNK, :]
                egl = jnp.exp(gl)
                ekd = jnp.exp(gl - g)
                eg = jnp.exp(g)
                q, k = q_ref[b, :, sl], k_ref[b, :, sl]
                kd = k * ekd
                qg = q * eg
                do = do_ref[b, :, sl].astype(BF16)
                vnb = vn_ref[b, :, sl].astype(BF16)
                dvn = _dot(a_ref[b, h].astype(BF16), do, TN) + _dot(kd.astype(BF16), dsnb, NN)
                dvnb = dvn.astype(BF16)
                da_ref[b, h] = _dot(do, vnb, NT)
                dqg = _dot(do, stb, NT)
                dkd = _dot(vnb, dsnb, NT)
                ds_s[b, h] = (_dot(qg.astype(BF16), do, TN) + egl * dsn - _dot(w_ref[b, :, sl].astype(BF16), dvnb, TN))
                du_ref[b, :, sl] = dvn
                dw_ref[b, :, sl] = -_dot(dvnb, stb, NT)
                dq_ref[b, :, sl] = dqg * eg
                dk_ref[b, :, sl] = dkd * ekd
                ddel = jnp.sum(dkd * kd, axis=1, keepdims=True)
                dgl = jnp.sum(ddel, axis=0, keepdims=True) + jnp.sum(jnp.sum(st * dsn, axis=1, keepdims=True), axis=0, keepdims=True) * egl
                col = jnp.sum(dqg * qg, axis=1, keepdims=True) - ddel + jnp.where(last, dgl, 0.0)
                dg_all = jnp.where(lane == h, col, dg_all)
            dg_ref[b] = dg_all

    tok = lambda c, j: BS((B, CHUNK, c), lambda n: (0, N - 1 - n, j))
    mat = lambda d: BS((B, N_HEADS, None, d, d), lambda n: (0, 0, N - 1 - n, 0, 0))
    return pl.pallas_call(
        body, name="gdn_scan_bwd", grid=(N,),
        in_specs=[tok(512, 0), tok(512, 0), tok(512, 1), tok(512, 0), tok(512, 0), tok(128, 0), mat(CHUNK), mat(128)],
        out_specs=[tok(512, 0), tok(512, 0), tok(512, 0), tok(512, 0), mat(CHUNK), tok(128, 0)],
        out_shape=[jax.ShapeDtypeStruct((B, S, 512), F32)] * 4
        + [jax.ShapeDtypeStruct((B, N_HEADS, N, CHUNK, CHUNK), F32), jax.ShapeDtypeStruct((B, S, 128), F32)],
        scratch_shapes=[pltpu.VMEM((B, N_HEADS, 128, 128), F32)],
        compiler_params=_arb(1))(dO3, qkv3, qkv3, W3, Vn3, GB3, A, St)


def gdn_chunk_bwd(qkv, GB, Grow, Tinv, dA, dU, dW, dQ1, dK1, dG1, B, S, nc=8):
    T = B * S
    N = S // CHUNK
    nc = min(nc, N)
    nb = N // nc
    R = nc * CHUNK

    def body(q_ref, k_ref, v_ref, gb_ref, gr_ref, t_ref, da_ref, du_ref, dw_ref, dq1_ref, dk1_ref, dg1_ref, o_ref, dgb_ref):
        incl, strict = _chunk_masks(nc)
        lane = lax.broadcasted_iota(jnp.int32, (1, 128), 1)
        dg_all = dg1_ref[...]
        db_all = jnp.zeros((R, 128), F32)
        for h in range(N_HEADS):
            sl = slice(h * 128, (h + 1) * 128)
            q = q_ref[:, sl].reshape(nc, CHUNK, 128)
            k = k_ref[:, sl].reshape(nc, CHUNK, 128)
            v = v_ref[:, sl].reshape(nc, CHUNK, 128)
            gc = gb_ref[:, h:h + 1].reshape(nc, CHUNK, 1)
            beta = gb_ref[:, 4 + h:5 + h].reshape(nc, CHUNK, 1)
            gr = gr_ref[h][:, None, :]
            decay, kb, l_mat, a_mat = _chunk_local(q, k, gc, gr, beta, incl, strict)
            eg = jnp.exp(gc)
            kbg = kb * eg
            vb = v * beta
            tb = t_ref[h].astype(BF16)
            du = du_ref[:, sl].reshape(nc, CHUNK, 128).astype(BF16)
            dw = dw_ref[:, sl].reshape(nc, CHUNK, 128).astype(BF16)
            dvb = _bdot("gcj,gcv->gjv", tb, du)
            dkbg = _bdot("gcj,gck->gjk", tb, dw)
            dt = _bdot("gcv,gjv->gcj", du, vb.astype(BF16)) + _bdot("gck,gjk->gcj", dw, kbg.astype(BF16))
            tmp = _bdot("gac,gab->gcb", tb, dt.astype(BF16))
            dl = jnp.where(strict, -_bdot("gcb,gdb->gcd", tmp.astype(BF16), tb), 0.0)
            da = da_ref[h]
            dm = (dl * decay).astype(BF16)
            dqk = (da * decay).astype(BF16)
            kbf = k.astype(BF16)
            dkb = _bdot("gcj,gjd->gcd", dm, kbf) + dkbg * eg
            dk = (_bdot("gcj,gcd->gjd", dm, kb.astype(BF16)) + _bdot("gcj,gcd->gjd", dqk, q.astype(BF16))
                  + dk1_ref[:, sl].reshape(nc, CHUNK, 128) + dkb * beta)
            dq = _bdot("gcj,gjd->gcd", dqk, kbf) + dq1_ref[:, sl].reshape(nc, CHUNK, 128)
            e = dl * l_mat + da * a_mat
            dgc = (jnp.sum(e, axis=2, keepdims=True) - jnp.sum(jnp.swapaxes(e, 1, 2), axis=2, keepdims=True)
                   + jnp.sum(dkbg * kbg, axis=2, keepdims=True))
            dbeta = jnp.sum(dkb * k, axis=2, keepdims=True) + jnp.sum(dvb * v, axis=2, keepdims=True)
            o_ref[:, sl] = dq.reshape(R, 128)
            o_ref[:, 512 + h * 128:512 + (h + 1) * 128] = dk.reshape(R, 128)
            o_ref[:, 1024 + h * 128:1024 + (h + 1) * 128] = (dvb * beta).reshape(R, 128)
            dg_all = dg_all + jnp.where(lane == h, dgc.reshape(R, 1), 0.0)
            db_all = jnp.where(lane == 4 + h, dbeta.reshape(R, 1), db_all)
        t = _chunk_row(R)
        for s in (1, 2, 4, 8, 16, 32):
            dg_all = dg_all + jnp.where(t + s < CHUNK, pltpu.roll(dg_all, R - s, 0), 0.0)
        dgb_ref[...] = jnp.where(lane < 4, dg_all, db_all)

    rowb = lambda c, j: BS((R, c), lambda b, n: (b * nb + n, j))
    mat = BS((None, N_HEADS, nc, CHUNK, CHUNK), lambda b, n: (b, 0, n, 0, 0))
    return pl.pallas_call(
        body, name="gdn_chunk_bwd", grid=(B, nb),
        in_specs=[rowb(512, 0), rowb(512, 1), rowb(512, 2), rowb(128, 0),
                  BS((None, N_HEADS, nc, CHUNK), lambda b, n: (b, 0, n, 0)), mat, mat,
                  rowb(512, 0), rowb(512, 0), rowb(512, 0), rowb(512, 0), rowb(128, 0)],
        out_specs=[rowb(GDN_QKV, 0), rowb(128, 0)],
        out_shape=[jax.ShapeDtypeStruct((T, GDN_QKV), F32), jax.ShapeDtypeStruct((T, 128), F32)],
        compiler_params=_arb(2))(qkv, qkv, qkv, GB, Grow, Tinv, dA, dU, dW, dQ1, dK1, dG1)


def _gdn_out_norm(og, gg):
    outs, xhs, rs = [], [], []
    for h in range(N_HEADS):
        seg = og[:, h * 128:(h + 1) * 128]
        r = lax.rsqrt(jnp.mean(seg * seg, axis=-1, keepdims=True) + EPS)
        xh = seg * r
        outs.append(xh * gg)
        xhs.append(xh)
        rs.append(r)
    return outs, xhs, rs


def merge_fwd(o_mla, o_gdn, o_mem, P, x, tgt, w_out, g_gdn, g_fin, tm=256):
    T = x.shape[0]
    tm = min(tm, T)

    def body(om_ref, og_ref, oc_ref, gate_ref, x_ref, t_ref, w_ref, gg_ref, gf_ref, mix_ref, dx_ref, sq_ref, gnf_ref):
        @pl.when(pl.program_id(0) == 0)
        def _():
            sq_ref[...] = jnp.zeros_like(sq_ref)
            gnf_ref[...] = jnp.zeros_like(gnf_ref)

        ogn, _, _ = _gdn_out_norm(og_ref[...], gg_ref[...])
        cat = jnp.concatenate([om_ref[...]] + ogn + [oc_ref[...]], axis=1)
        gt = gate_ref[...]
        mixed = (cat * (gt * _sigmoid(gt))).astype(BF16)
        mix_ref[...] = mixed
        x2 = x_ref[...] + _dot(mixed, w_ref[...], NN)
        r2 = lax.rsqrt(jnp.mean(x2 * x2, axis=-1, keepdims=True) + EPS)
        xh = x2 * r2
        gf = gf_ref[...]
        diff = xh * gf - t_ref[...]
        sq_ref[...] += jnp.sum(diff * diff, axis=0, keepdims=True)
        dy = diff * (1.0 / D_MODEL)
        gnf_ref[...] += jnp.sum(dy * xh, axis=0, keepdims=True)
        dxh = dy * gf
        dx_ref[...] = r2 * (dxh - xh * jnp.mean(dxh * xh, axis=-1, keepdims=True))

    rowb = lambda c, j=0: BS((tm, c), lambda i: (i, j))
    full = lambda r, c: BS((r, c), lambda i: (0, 0))
    return pl.pallas_call(
        body, name="merge_fwd", grid=(T // tm,),
        in_specs=[rowb(512), rowb(512), rowb(512), rowb(D_MIX, OFF_GATE // D_MIX), rowb(D_MODEL), rowb(D_MODEL),
                  full(D_MIX, D_MODEL), full(1, 128), full(1, D_MODEL)],
        out_specs=[rowb(D_MIX), rowb(D_MODEL), full(1, D_MODEL), full(1, D_MODEL)],
        out_shape=[jax.ShapeDtypeStruct((T, D_MIX), BF16), jax.ShapeDtypeStruct((T, D_MODEL), F32),
                   jax.ShapeDtypeStruct((1, D_MODEL), F32), jax.ShapeDtypeStruct((1, D_MODEL), F32)],
        compiler_params=_arb(1))(o_mla, o_gdn, o_mem, P, x, tgt, w_out, g_gdn, g_fin)


def merge_bwd(dx2, o_mla, o_gdn, o_mem, P, w_out, g_gdn, tm=256):
    T = dx2.shape[0]
    tm = min(tm, T)

    def body(dx_ref, om_ref, og_ref, oc_ref, gate_ref, w_ref, gg_ref, dgate_ref, dom_ref, dog_ref, doc_ref, ggn_ref):
        @pl.when(pl.program_id(0) == 0)
        def _():
            ggn_ref[...] = jnp.zeros_like(ggn_ref)

        gg = gg_ref[...]
        dmix = _dot(dx_ref[...].astype(BF16), w_ref[...], NT)
        ogn, xhs, rs = _gdn_out_norm(og_ref[...], gg)
        cat = jnp.concatenate([om_ref[...]] + ogn + [oc_ref[...]], axis=1)
        gt = gate_ref[...]
        sg = _sigmoid(gt)
        dgate_ref[...] = (dmix * cat * (sg * (1.0 + gt * (1.0 - sg)))).astype(BF16)
        dcat = dmix * (gt * sg)
        dom_ref[...] = dcat[:, :512]
        doc_ref[...] = dcat[:, 1024:]
        acc = jnp.zeros((1, 128), F32)
        for h in range(N_HEADS):
            dseg = dcat[:, 512 + h * 128:512 + (h + 1) * 128]
            acc = acc + jnp.sum(dseg * xhs[h], axis=0, keepdims=True)
            dxh = dseg * gg
            dog_ref[:, h * 128:(h + 1) * 128] = rs[h] * (dxh - xhs[h] * jnp.mean(dxh * xhs[h], axis=-1, keepdims=True))
        ggn_ref[...] += acc

    rowb = lambda c, j=0: BS((tm, c), lambda i: (i, j))
    full = lambda r, c: BS((r, c), lambda i: (0, 0))
    return pl.pallas_call(
        body, name="merge_bwd", grid=(T // tm,),
        in_specs=[rowb(D_MODEL), rowb(512), rowb(512), rowb(512), rowb(D_MIX, OFF_GATE // D_MIX),
                  full(D_MIX, D_MODEL), full(1, 128)],
        out_specs=[rowb(D_MIX), rowb(512), rowb(512), rowb(512), full(1, 128)],
        out_shape=[jax.ShapeDtypeStruct((T, D_MIX), BF16)] + [jax.ShapeDtypeStruct((T, 512), F32)] * 3
        + [jax.ShapeDtypeStruct((1, 128), F32)],
        compiler_params=_arb(1))(dx2, o_mla, o_gdn, o_mem, P, w_out, g_gdn)


def in_norm_bwd(x, dh, dx2, gain, tm=256):
    T, n = x.shape
    tm = min(tm, T)

    def body(x_ref, dh_ref, dx2_ref, g_ref, o_ref, acc_ref):
        @pl.when(pl.program_id(0) == 0)
        def _():
            acc_ref[...] = jnp.zeros_like(acc_ref)

        xv = x_ref[...]
        r = lax.rsqrt(jnp.mean(xv * xv, axis=-1, keepdims=True) + EPS)
        xh = xv * r
        dy = dh_ref[...]
        acc_ref[...] += jnp.sum(dy * xh, axis=0, keepdims=True)
        dxh = dy * g_ref[...]
        o_ref[...] = dx2_ref[...] + r * (dxh - xh * jnp.mean(dxh * xh, axis=-1, keepdims=True))

    rowb = BS((tm, n), lambda i: (i, 0))
    full = BS((1, n), lambda i: (0, 0))
    return pl.pallas_call(
        body, name="in_norm_bwd", grid=(T // tm,),
        in_specs=[rowb, rowb, rowb, full], out_specs=[rowb, full],
        out_shape=[jax.ShapeDtypeStruct((T, n), F32), jax.ShapeDtypeStruct((1, n), F32)],
        compiler_params=_arb(1))(x, dh, dx2, gain)


def _pad_w_in(w):
    z = lambda n: jnp.zeros((w.shape[0], n), w.dtype)
    return jnp.concatenate([w[:, 0:640], w[:, 640:672], z(32), w[:, 672:704], z(32), w[:, 2240:2248], z(248),
                            w[:, 2248:2760], w[:, 704:2240], w[:, 2760:4296]], axis=1)


def _unpad_w_in(g):
    return jnp.concatenate([g[:, 0:640], g[:, 640:672], g[:, 704:736], g[:, OFF_GDN:OFF_GDN + GDN_QKV], g[:, 768:776],
                            g[:, OFF_MEMQ:OFF_MEMQ + 512], g[:, OFF_GATE:OFF_GATE + D_MIX]], axis=1)


def _pad_w_q_b(w):
    z = jnp.zeros((w.shape[0], 32), w.dtype)
    parts = []
    for h in range(N_HEADS):
        lo = h * (MLA_NOPE + MLA_ROPE)
        parts += [w[:, lo:lo + 128], w[:, lo + 128:lo + 160], z, w[:, lo + 160:lo + 192], z]
    return jnp.concatenate(parts, axis=1)


def _unpad_w_q_b(g):
    parts = []
    for h in range(N_HEADS):
        lo = h * HEAD_PAD
        parts += [g[:, lo:lo + 128], g[:, lo + 128:lo + 160], g[:, lo + 192:lo + 224]]
    return jnp.concatenate(parts, axis=1)


def _perm_w_kv_b(w):
    return jnp.concatenate([w[:, h * 256:h * 256 + 128] for h in range(N_HEADS)]
                           + [w[:, h * 256 + 128:h * 256 + 256] for h in range(N_HEADS)], axis=1)


def _unperm_w_kv_b(g):
    parts = []
    for h in range(N_HEADS):
        parts += [g[:, h * 128:(h + 1) * 128], g[:, 512 + h * 128:512 + (h + 1) * 128]]
    return jnp.concatenate(parts, axis=1)


def _lane_row(v4):
    return jnp.pad(v4.reshape(1, -1).astype(F32), ((0, 0), (0, 128 - v4.size)))


def _pack(pieces, n_rows):
    flat = jnp.concatenate([p.reshape(-1) for p in pieces])
    return jnp.pad(flat, (0, n_rows * 1024 - flat.size)).reshape(n_rows, 1024)


def _unpack(block, shapes):
    flat = block.reshape(-1)
    out, off = [], 0
    for shp in shapes:
        n = int(np.prod(shp))
        out.append(flat[off:off + n].reshape(shp))
        off += n
    return out


N_CHIPS = 4
PACK_ROWS = 1856
HALF_ROWS = PACK_ROWS // 2
MESH = pl.DeviceIdType.MESH
ANY = BS(memory_space=pl.ANY)


def _place():
    return lax.axis_index("x"), lax.axis_index("y"), lax.axis_index("c")


def _other_chips(x, y):
    return [(1 - x, y), (x, 1 - y), (1 - x, 1 - y)]


def allgather_chips(shard, name):
    R, C = shard.shape
    hr = R // 2

    def body(s_ref, o_ref, send_sems, recv_sems, local_sem):
        x, y, c = _place()
        chips = _other_chips(x, y)
        rows = pl.ds(pl.multiple_of(c * hr, 16), hr)
        sib_rows = pl.ds(pl.multiple_of((1 - c) * hr, 16), hr)

        def copy(k, src, dst, to):
            return pltpu.make_async_remote_copy(src_ref=src, dst_ref=dst, send_sem=send_sems.at[k], recv_sem=recv_sems.at[k],
                                                device_id=to, device_id_type=MESH)

        own = pltpu.make_async_copy(s_ref, o_ref.at[2 * x + y], local_sem)
        own.start()
        first = [copy(j, s_ref.at[rows], o_ref.at[2 * x + y, rows], (px, py, c)) for j, (px, py) in enumerate(chips)]
        for cp in first:
            cp.start()
        passed = [copy(3 + j, o_ref.at[2 * px + py, rows], o_ref.at[2 * px + py, rows], (x, y, 1 - c))
                  for j, (px, py) in enumerate(chips)]
        for j, (px, py) in enumerate(chips):
            copy(j, s_ref.at[rows], o_ref.at[2 * px + py, rows], (px, py, c)).wait_recv()
            passed[j].start()
        for j, (px, py) in enumerate(chips):
            copy(3 + j, s_ref.at[sib_rows], o_ref.at[2 * px + py, sib_rows], (x, y, 1 - c)).wait_recv()
        for cp in first + passed:
            cp.wait_send()
        own.wait()

    return pl.pallas_call(
        body, name=name, in_specs=[ANY], out_specs=ANY,
        out_shape=jax.ShapeDtypeStruct((N_CHIPS, R, C), shard.dtype),
        scratch_shapes=[pltpu.SemaphoreType.DMA((6,)), pltpu.SemaphoreType.DMA((6,)), pltpu.SemaphoreType.DMA(())])(shard)


def allgather_devices(block, name):
    R, C = block.shape

    def body(b_ref, o_ref, send_sems, recv_sems, local_sem):
        x, y, c = _place()
        me = 4 * x + 2 * y + c
        own = pltpu.make_async_copy(b_ref, o_ref.at[me], local_sem)
        own.start()
        copies = []
        for r in range(1, 8):
            px = 1 - x if r & 4 else x
            py = 1 - y if r & 2 else y
            pc = 1 - c if r & 1 else c
            send = pltpu.make_async_remote_copy(src_ref=b_ref, dst_ref=o_ref.at[me], send_sem=send_sems.at[r - 1],
                                                recv_sem=recv_sems.at[r - 1], device_id=(px, py, pc), device_id_type=MESH)
            recv = pltpu.make_async_remote_copy(src_ref=b_ref, dst_ref=o_ref.at[4 * px + 2 * py + pc], send_sem=send_sems.at[r - 1],
                                                recv_sem=recv_sems.at[r - 1], device_id=(px, py, pc), device_id_type=MESH)
            send.start()
            copies.append((send, recv))
        for send, recv in copies:
            recv.wait_recv()
            send.wait_send()
        own.wait()

    return pl.pallas_call(
        body, name=name, in_specs=[ANY], out_specs=ANY, out_shape=jax.ShapeDtypeStruct((8, R, C), block.dtype),
        scratch_shapes=[pltpu.SemaphoreType.DMA((7,)), pltpu.SemaphoreType.DMA((7,)), pltpu.SemaphoreType.DMA(())])(block)


def swap_sibling(a, name):
    def body(a_ref, o_ref, send_sem, recv_sem):
        x, y, c = _place()
        cp = pltpu.make_async_remote_copy(src_ref=a_ref, dst_ref=o_ref, send_sem=send_sem, recv_sem=recv_sem,
                                          device_id=(x, y, 1 - c), device_id_type=MESH)
        cp.start()
        cp.wait()

    return pl.pallas_call(
        body, name=name, in_specs=[ANY], out_specs=ANY, out_shape=jax.ShapeDtypeStruct(a.shape, a.dtype),
        scratch_shapes=[pltpu.SemaphoreType.DMA(()), pltpu.SemaphoreType.DMA(())])(a)


def exchange_chips(parts, name):
    _, R, C = parts.shape

    def body(p_ref, o_ref, send_sems, recv_sems):
        x, y, c = _place()
        copies = [pltpu.make_async_remote_copy(src_ref=p_ref.at[2 * px + py], dst_ref=o_ref.at[j], send_sem=send_sems.at[j],
                                               recv_sem=recv_sems.at[j], device_id=(px, py, c), device_id_type=MESH)
                  for j, (px, py) in enumerate(_other_chips(x, y))]
        for cp in copies:
            cp.start()
        for cp in copies:
            cp.wait()

    return pl.pallas_call(
        body, name=name, in_specs=[ANY], out_specs=ANY, out_shape=jax.ShapeDtypeStruct((3, R, C), parts.dtype),
        scratch_shapes=[pltpu.SemaphoreType.DMA((3,)), pltpu.SemaphoreType.DMA((3,))])(parts)


def _row_tile(n_rows, cap=512):
    for t in range(min(cap, n_rows), 15, -1):
        if n_rows % t == 0 and t % 16 == 0:
            return t
    return n_rows


def add_pair(a, b, name):
    R, C = a.shape
    tr = _row_tile(R)

    def body(a_ref, b_ref, s_ref, sb_ref):
        s = a_ref[...] + b_ref[...].astype(F32)
        s_ref[...] = s
        sb_ref[...] = s.astype(BF16)

    blk = BS((tr, C), lambda i: (i, 0))
    return pl.pallas_call(
        body, name=name, grid=(R // tr,), in_specs=[blk, blk], out_specs=[blk, blk],
        out_shape=[jax.ShapeDtypeStruct((R, C), F32), jax.ShapeDtypeStruct((R, C), BF16)], compiler_params=_arb(1))(a, b)


def add_four(a, parts, name):
    R, C = a.shape
    tr = _row_tile(R)

    def body(a_ref, p_ref, o_ref):
        s = a_ref[...]
        for j in range(3):
            s = s + p_ref[j].astype(F32)
        o_ref[...] = s

    return pl.pallas_call(
        body, name=name, grid=(R // tr,),
        in_specs=[BS((tr, C), lambda i: (i, 0)), BS((3, tr, C), lambda i: (0, i, 0))], out_specs=BS((tr, C), lambda i: (i, 0)),
        out_shape=jax.ShapeDtypeStruct((R, C), F32), compiler_params=_arb(1))(a, parts)


def sum_leading(a, name):
    def body(a_ref, o_ref):
        s = a_ref[0]
        for j in range(1, a.shape[0]):
            s = s + a_ref[j]
        o_ref[...] = s

    return pl.pallas_call(body, name=name, out_shape=jax.ShapeDtypeStruct(a.shape[1:], a.dtype))(a)


def adamw(w, g, m, v, name):
    R, C = w.shape
    tr = R
    if R > 256:
        tr = next(t for t in range(256, 7, -1) if R % t == 0 and t % 8 == 0)

    def body(w_ref, g_ref, m_ref, v_ref, d_ref, mo_ref, vo_ref):
        gv = g_ref[...]
        mn = ADAM_B1 * m_ref[...] + (1.0 - ADAM_B1) * gv
        vn = ADAM_B2 * v_ref[...] + (1.0 - ADAM_B2) * (gv * gv)
        m_hat = mn / (1.0 - ADAM_B1 ** ADAM_STEP)
        v_hat = vn / (1.0 - ADAM_B2 ** ADAM_STEP)
        d_ref[...] = -ADAM_LR * (m_hat / (jnp.sqrt(v_hat) + ADAM_EPS) + ADAM_WD * w_ref[...])
        mo_ref[...] = mn
        vo_ref[...] = vn

    blk = BS((tr, C), lambda i: (i, 0))
    return pl.pallas_call(
        body, name=name, grid=(R // tr,), in_specs=[blk] * 4, out_specs=[blk] * 3,
        out_shape=[jax.ShapeDtypeStruct((R, C), F32)] * 3, compiler_params=_arb(1))(w, g, m, v)


def local_step(x, mem, positions, tgt, norm_in, w_in, q_a_norm, w_q_b, kv_a_norm, w_kv_b, gdn_conv, gdn_a_log,
               gdn_dt_bias, gdn_norm, mem_norm, w_mem_kv, w_out, norm_final):
    B, S, D = x.shape
    M = mem.shape[1]
    T = B * S
    N = S // CHUNK
    x2d = x.reshape(T, D)
    mem2d = mem.reshape(B * M, D)
    tgt2d = tgt.reshape(T, D)

    wp = _pad_w_in(w_in)
    wq = _pad_w_q_b(w_q_b)
    wkv = _perm_w_kv_b(w_kv_b)
    alog_row, dt_row = _lane_row(gdn_a_log), _lane_row(gdn_dt_bias)

    half = MLA_ROPE // 2
    inv_freq = 1.0 / (ROPE_THETA ** (jnp.arange(half, dtype=F32) / half))
    z32 = jnp.zeros((half,), F32)
    o32 = jnp.ones((half,), F32)
    inv_row = jnp.concatenate([inv_freq, z32, inv_freq, z32]).reshape(1, 128)
    sgn_row = jnp.concatenate([-o32, z32, o32, z32]).reshape(1, 128)
    msk_row = jnp.concatenate([o32, z32, o32, z32]).reshape(1, 128)
    cos_t, sin_t = rope_tables(positions.reshape(T, 1), inv_row, sgn_row, msk_row)

    h = rms_fwd(x2d, norm_in, "rms_in")
    P = mm(h, wp, "nn", F32, "in_proj", bm=512, bn=768, bk=1024)
    Q, K, V, qn, kvn = mla_prep(P, q_a_norm, kv_a_norm, wq, wkv, cos_t, sin_t)
    o_mla, lse = mla_attn_fwd(Q, K, V, B, S)
    memn = rms_fwd(mem2d, mem_norm, "rms_mem")
    MKV = mm(memn, w_mem_kv, "nn", BF16, "mem_kv_proj", bk=1024)
    o_mem = mem_attn_fwd(P, MKV, B, S, M)
    qkv = gdn_prep_fwd(P, gdn_conv, B, S)
    GB = gdn_gate_fwd(P, alog_row, dt_row, B, S)
    Grow = jnp.transpose(GB[:, :N_HEADS].reshape(B, N, CHUNK, N_HEADS), (0, 3, 1, 2))
    U, W, Tinv, A = gdn_chunk_fwd(qkv, GB, Grow, B, S)
    qkv3, GB3 = qkv.reshape(B, S, GDN_QKV), GB.reshape(B, S, 128)
    W3 = W.reshape(B, S, 512)
    o_gdn3, Vn3, St = gdn_scan_fwd(qkv3, U.reshape(B, S, 512), W3, GB3, A, B, S)
    o_gdn = o_gdn3.reshape(T, 512)
    mixed, dx2, sq, g_norm_final = merge_fwd(o_mla, o_gdn, o_mem, P, x2d, tgt2d, w_out, gdn_norm, norm_final.reshape(1, D))

    g_w_out = mm(mixed, dx2, "tn", F32, "grad_w_out")
    dgate, do_mla, do_gdn, do_mem, g_gdn_norm = merge_bwd(dx2, o_mla, o_gdn, o_mem, P, w_out, gdn_norm)

    dmemq, dMKV = mem_attn_bwd(P, MKV, do_mem, B, S, M)
    g_w_mem_kv = mm(memn, dMKV, "tn", F32, "grad_w_mem_kv")
    dmemn = mm(dMKV, w_mem_kv, "nt", F32, "d_memn", bk=1024)
    g_mem_norm = gain_grad(mem2d, dmemn, "grad_mem_norm")

    dU3, dW3, dQ13, dK13, dA, dG13 = gdn_scan_bwd(do_gdn.reshape(B, S, 512), qkv3, W3, Vn3, GB3, A, St, B, S)
    r2 = lambda a: a.reshape(T, a.shape[-1])
    dqkv, dGB = gdn_chunk_bwd(qkv, GB, Grow, Tinv, dA, r2(dU3), r2(dW3), r2(dQ13), r2(dK13), r2(dG13), B, S)
    dPg, g_conv = gdn_prep_bwd(P, dqkv, gdn_conv, B, S)
    dab, g_ab = gdn_gate_bwd(P, dGB, alog_row, dt_row, B, S)

    dQ, dK, dV = mla_attn_bwd(Q, K, V, o_mla, do_mla, lse, B, S)
    dq_lin, dkv_lin, dkr = mla_post_bwd(dQ, dK, dV, cos_t, sin_t)
    dqn = mm(dq_lin, wq, "nt", F32, "d_qn", bk=1024)
    dkvn = mm(dkv_lin, wkv, "nt", F32, "d_kvn", bk=1024)
    g_wq = mm(qn, dq_lin, "tn", F32, "grad_w_q_b")
    g_wkv = mm(kvn, dkv_lin, "tn", F32, "grad_w_kv_b")
    dPm, g_q_a_norm, g_kv_a_norm = mla_norm_bwd(P, dqn, dkvn, dkr, dab, q_a_norm, kv_a_norm)

    dP = jnp.concatenate([dPm, dmemq, dPg, dgate], axis=1)
    g_wp = mm(h, dP, "tn", F32, "grad_w_in")
    dh = mm(dP, wp, "nt", F32, "d_h", bn=1024, bk=768)
    grad_x, g_norm_in = in_norm_bwd(x2d, dh, dx2, norm_in)

    grads = dict(
        norm_in=g_norm_in, w_in=_unpad_w_in(g_wp), q_a_norm=g_q_a_norm, w_q_b=_unpad_w_q_b(g_wq),
        kv_a_norm=g_kv_a_norm, w_kv_b=_unperm_w_kv_b(g_wkv), gdn_conv=g_conv,
        gdn_a_log=g_ab[0:1, :N_HEADS], gdn_dt_bias=g_ab[1:2, :N_HEADS], gdn_norm=g_gdn_norm,
        mem_norm=g_mem_norm, w_mem_kv=g_w_mem_kv, w_out=g_w_out, norm_final=g_norm_final)
    return sq, grad_x.reshape(B, S, D), grads


def kernel(x, mem, positions, norm_in, w_in, q_a_norm, w_q_b, kv_a_norm, w_kv_b, gdn_conv, gdn_a_log, gdn_dt_bias, gdn_norm, mem_norm, w_mem_kv, w_out, norm_final, loss_target, m_norm_in, m_w_in, m_q_a_norm, m_w_q_b, m_kv_a_norm, m_w_kv_b, m_gdn_conv, m_gdn_a_log, m_gdn_dt_bias, m_gdn_norm, m_mem_norm, m_w_mem_kv, m_w_out, m_norm_final, v_norm_in, v_w_in, v_q_a_norm, v_w_q_b, v_kv_a_norm, v_w_kv_b, v_gdn_conv, v_gdn_a_log, v_gdn_dt_bias, v_gdn_norm, v_mem_norm, v_w_mem_kv, v_w_out, v_norm_final):
    B = x.shape[0]
    cx, cy, cc = lax.axis_index("x"), lax.axis_index("y"), lax.axis_index("c")
    chip = 2 * cx + cy

    big_names = ("w_in", "w_q_b", "w_kv_b", "w_mem_kv", "w_out")
    big = dict(w_in=w_in[0], w_q_b=w_q_b[0], w_kv_b=w_kv_b[0], w_mem_kv=w_mem_kv[0], w_out=w_out[0])
    shard_shapes = [big[n].shape for n in big_names]
    gathered = allgather_chips(_pack([big[n].astype(BF16) for n in big_names], PACK_ROWS), "allgather_weights")
    per_chip = [_unpack(gathered[q], shard_shapes) for q in range(N_CHIPS)]
    cat_axis = dict(w_in=1, w_q_b=1, w_kv_b=1, w_mem_kv=0, w_out=0)
    full = {n: jnp.concatenate([per_chip[q][i] for q in range(N_CHIPS)], axis=cat_axis[n]) for i, n in enumerate(big_names)}
    conv_all = allgather_devices(_pack([gdn_conv[0]], 8), "allgather_conv")
    conv_shape = gdn_conv[0].shape
    conv_full = jnp.concatenate([_unpack(conv_all[2 * q], [conv_shape])[0] for q in range(N_CHIPS)], axis=1)

    sq, grad_x, g = local_step(x, mem, positions, loss_target, norm_in, full["w_in"], q_a_norm, full["w_q_b"], kv_a_norm,
                               full["w_kv_b"], conv_full, gdn_a_log, gdn_dt_bias, gdn_norm, mem_norm, full["w_mem_kv"],
                               full["w_out"], norm_final)
    loss = lax.psum(0.5 * jnp.sum(sq) / D_MODEL, ("x", "y", "c"))

    grad_names = big_names + ("gdn_conv",)
    g_axis = dict(cat_axis, gdn_conv=1)
    owned = dict(big, gdn_conv=gdn_conv[0])
    g_shapes = [owned[n].shape for n in grad_names]

    def chip_part(n, q):
        ax = g_axis[n]
        size = owned[n].shape[ax]
        return lax.slice_in_dim(g[n], q * size, (q + 1) * size, axis=ax)

    packed = jnp.stack([_pack([chip_part(n, q) for n in grad_names], PACK_ROWS) for q in range(N_CHIPS)])
    halves = packed.reshape(N_CHIPS, 2, HALF_ROWS, 1024)
    mine = lax.dynamic_index_in_dim(halves, cc, axis=1, keepdims=False)
    theirs = lax.dynamic_index_in_dim(halves, 1 - cc, axis=1, keepdims=False)
    from_sibling = swap_sibling(theirs.astype(BF16), "rs_sibling_partial")
    chip_sum, chip_sum_bf = add_pair(mine.reshape(-1, 1024), from_sibling.reshape(-1, 1024), "rs_add_sibling")
    from_chips = exchange_chips(chip_sum_bf.reshape(N_CHIPS, HALF_ROWS, 1024), "rs_exchange_chips")
    own_part = lax.dynamic_index_in_dim(chip_sum.reshape(N_CHIPS, HALF_ROWS, 1024), chip, axis=0, keepdims=False)
    my_half = add_four(own_part, from_chips, "rs_add_chips")
    other_half = swap_sibling(my_half, "rs_sibling_final")
    reduced = jnp.where(cc == 0, jnp.concatenate([my_half, other_half]), jnp.concatenate([other_half, my_half]))
    g_big = dict(zip(grad_names, _unpack(reduced, g_shapes)))

    small_names = ("norm_in", "q_a_norm", "kv_a_norm", "gdn_a_log", "gdn_dt_bias", "gdn_norm", "mem_norm", "norm_final")
    small = dict(norm_in=norm_in, q_a_norm=q_a_norm, kv_a_norm=kv_a_norm, gdn_a_log=gdn_a_log, gdn_dt_bias=gdn_dt_bias,
                 gdn_norm=gdn_norm, mem_norm=mem_norm, norm_final=norm_final)
    m_small = dict(norm_in=m_norm_in, q_a_norm=m_q_a_norm, kv_a_norm=m_kv_a_norm, gdn_a_log=m_gdn_a_log,
                   gdn_dt_bias=m_gdn_dt_bias, gdn_norm=m_gdn_norm, mem_norm=m_mem_norm, norm_final=m_norm_final)
    v_small = dict(norm_in=v_norm_in, q_a_norm=v_q_a_norm, kv_a_norm=v_kv_a_norm, gdn_a_log=v_gdn_a_log,
                   gdn_dt_bias=v_gdn_dt_bias, gdn_norm=v_gdn_norm, mem_norm=v_mem_norm, norm_final=v_norm_final)
    rows = lambda d: jnp.stack([jnp.pad(d[n].reshape(-1), (0, 1024 - d[n].size)) for n in small_names])
    g_small_rows = sum_leading(allgather_devices(rows(g), "allgather_small_grads"), "sum_small_grads")
    d_s, m_s, v_s = adamw(rows(small), g_small_rows, rows(m_small), rows(v_small), "adamw_small")
    unrow = lambda r: {n: r[i, :small[n].size].reshape(small[n].shape) for i, n in enumerate(small_names)}
    g_sm, d_sm, m_sm, v_sm = unrow(g_small_rows), unrow(d_s), unrow(m_s), unrow(v_s)

    m_big = dict(w_in=m_w_in, w_q_b=m_w_q_b, w_kv_b=m_w_kv_b, w_mem_kv=m_w_mem_kv, w_out=m_w_out, gdn_conv=m_gdn_conv)
    v_big = dict(w_in=v_w_in, w_q_b=v_w_q_b, w_kv_b=v_w_kv_b, w_mem_kv=v_w_mem_kv, w_out=v_w_out, gdn_conv=v_gdn_conv)
    g_out, d_out, m_out, v_out = dict(g_sm), dict(d_sm), dict(m_sm), dict(v_sm)
    for n in grad_names:
        d_n, m_n, v_n = adamw(owned[n], g_big[n], m_big[n][0], v_big[n][0], "adamw_" + n)
        g_out[n], d_out[n], m_out[n], v_out[n] = g_big[n][None], d_n[None], m_n[None], v_n[None]

    order = ("norm_in", "w_in", "q_a_norm", "w_q_b", "kv_a_norm", "w_kv_b", "gdn_conv", "gdn_a_log", "gdn_dt_bias",
             "gdn_norm", "mem_norm", "w_mem_kv", "w_out", "norm_final")
    return (loss, grad_x, *[g_out[n] for n in order], *[d_out[n] for n in order], *[m_out[n] for n in order],
            *[v_out[n] for n in order])
```

```python
import functools
import math

import jax
import jax.numpy as jnp
import numpy as np
from jax import lax
from jax.experimental import pallas as pl
from jax.experimental.pallas import tpu as pltpu

F32 = jnp.float32
BF16 = jnp.bfloat16
BS = pl.BlockSpec

D_MODEL = 1024
N_HEADS = 4
MLA_NOPE, MLA_ROPE, MLA_V = 128, 64, 128
Q_LORA, KV_LORA = 384, 256
ROPE_THETA = 10000.0
GDN_DK = GDN_DV = 128
GDN_CONV = 4
CHUNK = 64
MEM_DH = 128
D_MIX = 1536
GDN_QKV = 1536
D_IN = 4296
EPS = 1e-6
ADAM_LR, ADAM_B1, ADAM_B2, ADAM_EPS, ADAM_WD, ADAM_STEP = 0.001, 0.9, 0.999, 1e-08, 0.01, 10

OFF_MLA = 0
OFF_MEMQ = 1024
OFF_GDN = 1536
OFF_GATE = 3072
N_PAD = 4608
HEAD_PAD = 256
MLA_SCALE = (MLA_NOPE + MLA_ROPE) ** -0.5
MEM_SCALE = MEM_DH ** -0.5
GDN_SCALE = GDN_DK ** -0.5
NEG = -1e30

NN = ((1,), (0,))
NT = ((1,), (1,))
TN = ((0,), (0,))


def _dot(a, b, dims):
    return lax.dot_general(a, b, (dims, ((), ())), preferred_element_type=F32)


def _bdot(spec, a, b, precision=None):
    return jnp.einsum(spec, a, b, preferred_element_type=F32, precision=precision)


def _arb(n):
    return pltpu.CompilerParams(dimension_semantics=("arbitrary",) * n)


def _sigmoid(x):
    return 1.0 / (1.0 + jnp.exp(-x))


def _softplus(z):
    return jnp.maximum(z, 0.0) + jnp.log(1.0 + jnp.exp(-jnp.abs(z)))


def _rope(t, cos_row, sin_row):
    return t * cos_row + pltpu.roll(t, 64, 1) * sin_row


def _rope_bwd(d, cos_row, sin_row):
    return d * cos_row + pltpu.roll(d * sin_row, 64, 1)


def rms_fwd(x, gain, name, tm=512):
    T, n = x.shape
    tm = min(tm, T)

    def body(x_ref, g_ref, o_ref):
        xv = x_ref[...]
        r = lax.rsqrt(jnp.mean(xv * xv, axis=-1, keepdims=True) + EPS)
        o_ref[...] = (xv * r * g_ref[...]).astype(BF16)

    return pl.pallas_call(
        body, name=name, grid=(T // tm,),
        in_specs=[BS((tm, n), lambda i: (i, 0)), BS((1, n), lambda i: (0, 0))],
        out_specs=BS((tm, n), lambda i: (i, 0)),
        out_shape=jax.ShapeDtypeStruct((T, n), BF16), compiler_params=_arb(1))(x, gain)


def mm(a, b, kind, out_dtype, name, bm=512, bn=512, bk=512):
    if kind == "nn":
        (M, K), (_, N) = a.shape, b.shape
    elif kind == "nt":
        (M, K), (N, _) = a.shape, b.shape
    else:
        (K, M), (_, N) = a.shape, b.shape
    bm, bn, bk = min(bm, M), min(bn, N), min(bk, K)
    assert M % bm == 0 and N % bn == 0 and K % bk == 0, (name, M, N, K)
    nk = K // bk
    a_spec = BS((bk, bm), lambda i, j, k: (k, i)) if kind == "tn" else BS((bm, bk), lambda i, j, k: (i, k))
    b_spec = BS((bn, bk), lambda i, j, k: (j, k)) if kind == "nt" else BS((bk, bn), lambda i, j, k: (k, j))
    dims = {"nn": NN, "nt": NT, "tn": TN}[kind]

    def body(a_ref, b_ref, o_ref, acc):
        k = pl.program_id(2)

        @pl.when(k == 0)
        def _():
            acc[...] = jnp.zeros_like(acc)

        acc[...] += _dot(a_ref[...].astype(BF16), b_ref[...].astype(BF16), dims)

        @pl.when(k == nk - 1)
        def _():
            o_ref[...] = acc[...].astype(out_dtype)

    return pl.pallas_call(
        body, name=name, grid=(M // bm, N // bn, nk),
        in_specs=[a_spec, b_spec], out_specs=BS((bm, bn), lambda i, j, k: (i, j)),
        out_shape=jax.ShapeDtypeStruct((M, N), out_dtype),
        scratch_shapes=[pltpu.VMEM((bm, bn), F32)], compiler_params=_arb(3))(a, b)


def rope_tables(pos_col, inv_row, sgn_row, msk_row, tm=512):
    T = pos_col.shape[0]
    tm = min(tm, T)

    def body(p_ref, inv_ref, sgn_ref, msk_ref, c_ref, s_ref):
        ang = p_ref[...].astype(F32) * inv_ref[...]
        c_ref[...] = jnp.cos(ang) * msk_ref[...]
        s_ref[...] = jnp.sin(ang) * sgn_ref[...]

    row = BS((1, 128), lambda i: (0, 0))
    return pl.pallas_call(
        body, name="rope_tables", grid=(T // tm,),
        in_specs=[BS((tm, 1), lambda i: (i, 0)), row, row, row],
        out_specs=[BS((tm, 128), lambda i: (i, 0))] * 2,
        out_shape=[jax.ShapeDtypeStruct((T, 128), F32)] * 2, compiler_params=_arb(1))(pos_col, inv_row, sgn_row, msk_row)


def mla_prep(P, gq, gkv, wq, wkv, cos_t, sin_t, tm=512):
    T = P.shape[0]
    tm = min(tm, T)

    def body(p_ref, gq_ref, gkv_ref, wq_ref, wkv_ref, c_ref, s_ref, q_ref, k_ref, v_ref, qn_ref, kvn_ref):
        p = p_ref[...]
        cq, ckv, kr = p[:, :Q_LORA], p[:, Q_LORA:Q_LORA + KV_LORA], p[:, 640:768]
        qn = (cq * lax.rsqrt(jnp.mean(cq * cq, axis=-1, keepdims=True) + EPS) * gq_ref[...]).astype(BF16)
        kvn = (ckv * lax.rsqrt(jnp.mean(ckv * ckv, axis=-1, keepdims=True) + EPS) * gkv_ref[...]).astype(BF16)
        qn_ref[...] = qn
        kvn_ref[...] = kvn
        q = _dot(qn, wq_ref[...], NN)
        kv = _dot(kvn, wkv_ref[...], NN)
        cos_row, sin_row = c_ref[...], s_ref[...]
        krr = _rope(kr, cos_row, sin_row).astype(BF16)
        for h in range(N_HEADS):
            lo = h * HEAD_PAD
            q_ref[:, lo:lo + 128] = (q[:, lo:lo + 128] * MLA_SCALE).astype(BF16)
            q_ref[:, lo + 128:lo + 256] = (_rope(q[:, lo + 128:lo + 256], cos_row, sin_row) * MLA_SCALE).astype(BF16)
            k_ref[:, lo:lo + 128] = kv[:, h * 128:(h + 1) * 128].astype(BF16)
            k_ref[:, lo + 128:lo + 256] = krr
        v_ref[...] = kv[:, 512:].astype(BF16)

    full = lambda r, c: BS((r, c), lambda i: (0, 0))
    rowb = lambda c: BS((tm, c), lambda i: (i, 0))
    return pl.pallas_call(
        body, name="mla_prep", grid=(T // tm,),
        in_specs=[rowb(1024), full(1, Q_LORA), full(1, KV_LORA), full(Q_LORA, 1024), full(KV_LORA, 1024), rowb(128), rowb(128)],
        out_specs=[rowb(1024), rowb(1024), rowb(512), rowb(Q_LORA), rowb(KV_LORA)],
        out_shape=[jax.ShapeDtypeStruct((T, 1024), BF16), jax.ShapeDtypeStruct((T, 1024), BF16),
                   jax.ShapeDtypeStruct((T, 512), BF16), jax.ShapeDtypeStruct((T, Q_LORA), BF16),
                   jax.ShapeDtypeStruct((T, KV_LORA), BF16)],
        compiler_params=_arb(1))(P, gq, gkv, wq, wkv, cos_t, sin_t)


def mla_attn_fwd(Q, K, V, B, S, tq=512):
    T = B * S
    tq = min(tq, S)
    nq = S // tq

    def body(q_ref, k_ref, v_ref, o_ref, lse_ref, m_s, l_s, acc_s):
        i = pl.program_id(2)
        q = q_ref[...]
        m_s[...] = jnp.full_like(m_s, NEG)
        l_s[...] = jnp.zeros_like(l_s)
        acc_s[...] = jnp.zeros_like(acc_s)

        def blk(j, masked):
            rows = pl.ds(pl.multiple_of(j * tq, tq), tq)
            s = _dot(q, k_ref[rows, :], NT)
            if masked:
                r = lax.broadcasted_iota(jnp.int32, (tq, tq), 0)
                c = lax.broadcasted_iota(jnp.int32, (tq, tq), 1)
                s = jnp.where(r >= c, s, NEG)
            m_prev = m_s[...]
            m_new = jnp.maximum(m_prev, jnp.max(s, axis=1, keepdims=True))
            p = jnp.exp(s - m_new)
            alpha = jnp.exp(m_prev - m_new)
            l_s[...] = alpha * l_s[...] + jnp.sum(p, axis=1, keepdims=True)
            acc_s[...] = alpha * acc_s[...] + _dot(p.astype(BF16), v_ref[rows, :], NN)
            m_s[...] = m_new

        def loop(j, c):
            blk(j, False)
            return c

        lax.fori_loop(0, i, loop, 0)
        blk(i, True)
        o_ref[...] = acc_s[...] / l_s[...]
        lse_ref[...] = m_s[...] + jnp.log(l_s[...])

    return pl.pallas_call(
        body, name="mla_attn_fwd", grid=(B, N_HEADS, nq),
        in_specs=[BS((tq, HEAD_PAD), lambda b, h, i: (b * nq + i, h)),
                  BS((S, HEAD_PAD), lambda b, h, i: (b, h)),
                  BS((S, 128), lambda b, h, i: (b, h))],
        out_specs=[BS((tq, 128), lambda b, h, i: (b * nq + i, h)),
                   BS((None, tq, 1), lambda b, h, i: (h, b * nq + i, 0))],
        out_shape=[jax.ShapeDtypeStruct((T, 512), F32), jax.ShapeDtypeStruct((N_HEADS, T, 1), F32)],
        scratch_shapes=[pltpu.VMEM((tq, 1), F32), pltpu.VMEM((tq, 1), F32), pltpu.VMEM((tq, 128), F32)],
        compiler_params=_arb(3))(Q, K, V)


def mla_attn_bwd(Q, K, V, O, dO, LSE, B, S, tq=512):
    T = B * S
    tq = min(tq, S)
    nq = S // tq

    def body(q_ref, k_ref, v_ref, o_ref, do_ref, lse_ref, dq_ref, dk_ref, dv_ref, delta_s, dk_s, dv_s):
        j = pl.program_id(2)

        @pl.when(j == 0)
        def _():
            dq_ref[...] = jnp.zeros_like(dq_ref)
            delta_s[...] = jnp.sum(do_ref[...] * o_ref[...], axis=1, keepdims=True)

        dk_s[...] = jnp.zeros_like(dk_s)
        dv_s[...] = jnp.zeros_like(dv_s)
        k = k_ref[...]
        v = v_ref[...]

        def step(i, c):
            rows = pl.ds(pl.multiple_of(i * tq, tq), tq)
            q = q_ref[rows, :]
            do = do_ref[rows, :].astype(BF16)
            s = _dot(q, k, NT)
            r = i * tq + lax.broadcasted_iota(jnp.int32, (tq, tq), 0)
            cc = j * tq + lax.broadcasted_iota(jnp.int32, (tq, tq), 1)
            p = jnp.where(r >= cc, jnp.exp(s - lse_ref[rows, :]), 0.0)
            dv_s[...] += _dot(p.astype(BF16), do, TN)
            dp = _dot(do, v, NT)
            ds = (p * (dp - delta_s[rows, :])).astype(BF16)
            dk_s[...] += _dot(ds, q, TN)
            dq_ref[rows, :] += _dot(ds, k, NN)
            return c

        lax.fori_loop(j, nq, step, 0)
        dk_ref[...] = dk_s[...]
        dv_ref[...] = dv_s[...]

    seq = lambda c: BS((S, c), lambda b, h, j: (b, h))
    return pl.pallas_call(
        body, name="mla_attn_bwd", grid=(B, N_HEADS, nq),
        in_specs=[seq(HEAD_PAD), BS((tq, HEAD_PAD), lambda b, h, j: (b * nq + j, h)),
                  BS((tq, 128), lambda b, h, j: (b * nq + j, h)), seq(128), seq(128),
                  BS((None, S, 1), lambda b, h, j: (h, b, 0))],
        out_specs=[seq(HEAD_PAD), BS((tq, HEAD_PAD), lambda b, h, j: (b * nq + j, h)),
                   BS((tq, 128), lambda b, h, j: (b * nq + j, h))],
        out_shape=[jax.ShapeDtypeStruct((T, 1024), F32), jax.ShapeDtypeStruct((T, 1024), F32),
                   jax.ShapeDtypeStruct((T, 512), F32)],
        scratch_shapes=[pltpu.VMEM((S, 1), F32), pltpu.VMEM((tq, HEAD_PAD), F32), pltpu.VMEM((tq, 128), F32)],
        compiler_params=_arb(3))(Q, K, V, O, dO, LSE)


def mla_post_bwd(dQ, dK, dV, cos_t, sin_t, tm=512):
    T = dQ.shape[0]
    tm = min(tm, T)

    def body(dq_ref, dk_ref, dv_ref, c_ref, s_ref, ql_ref, kvl_ref, kr_ref):
        cos_row, sin_row = c_ref[...], s_ref[...]
        kr = jnp.zeros((tm, 128), F32)
        for h in range(N_HEADS):
            lo = h * HEAD_PAD
            ql_ref[:, lo:lo + 128] = (dq_ref[:, lo:lo + 128] * MLA_SCALE).astype(BF16)
            ql_ref[:, lo + 128:lo + 256] = (_rope_bwd(dq_ref[:, lo + 128:lo + 256], cos_row, sin_row) * MLA_SCALE).astype(BF16)
            kvl_ref[:, h * 128:(h + 1) * 128] = dk_ref[:, lo:lo + 128].astype(BF16)
            kr = kr + dk_ref[:, lo + 128:lo + 256]
        kvl_ref[:, 512:] = dv_ref[...].astype(BF16)
        kr_ref[...] = _rope_bwd(kr, cos_row, sin_row)

    rowb = lambda c: BS((tm, c), lambda i: (i, 0))
    return pl.pallas_call(
        body, name="mla_post_bwd", grid=(T // tm,),
        in_specs=[rowb(1024), rowb(1024), rowb(512), rowb(128), rowb(128)],
        out_specs=[rowb(1024), rowb(1024), rowb(128)],
        out_shape=[jax.ShapeDtypeStruct((T, 1024), BF16), jax.ShapeDtypeStruct((T, 1024), BF16),
                   jax.ShapeDtypeStruct((T, 128), F32)],
        compiler_params=_arb(1))(dQ, dK, dV, cos_t, sin_t)


def mla_norm_bwd(P, dqn, dkvn, dkr, dab, gq, gkv, tm=512):
    T = P.shape[0]
    tm = min(tm, T)

    def norm_bwd(x, dy, g):
        r = lax.rsqrt(jnp.mean(x * x, axis=-1, keepdims=True) + EPS)
        xh = x * r
        dxh = dy * g
        return r * (dxh - xh * jnp.mean(dxh * xh, axis=-1, keepdims=True)), jnp.sum(dy * xh, axis=0, keepdims=True)

    def body(p_ref, dqn_ref, dkvn_ref, dkr_ref, dab_ref, gq_ref, gkv_ref, o_ref, aq_ref, akv_ref):
        @pl.when(pl.program_id(0) == 0)
        def _():
            aq_ref[...] = jnp.zeros_like(aq_ref)
            akv_ref[...] = jnp.zeros_like(akv_ref)

        dcq, ggq = norm_bwd(p_ref[:, :Q_LORA], dqn_ref[...], gq_ref[...])
        dckv, ggkv = norm_bwd(p_ref[:, Q_LORA:640], dkvn_ref[...], gkv_ref[...])
        aq_ref[...] += ggq
        akv_ref[...] += ggkv
        o_ref[:, :Q_LORA] = dcq.astype(BF16)
        o_ref[:, Q_LORA:640] = dckv.astype(BF16)
        o_ref[:, 640:768] = dkr_ref[...].astype(BF16)
        o_ref[:, 768:896] = dab_ref[...]
        o_ref[:, 896:1024] = jnp.zeros((tm, 128), BF16)

    rowb = lambda c: BS((tm, c), lambda i: (i, 0))
    full = lambda c: BS((1, c), lambda i: (0, 0))
    return pl.pallas_call(
        body, name="mla_norm_bwd", grid=(T // tm,),
        in_specs=[rowb(1024), rowb(Q_LORA), rowb(KV_LORA), rowb(128), rowb(128), full(Q_LORA), full(KV_LORA)],
        out_specs=[rowb(1024), full(Q_LORA), full(KV_LORA)],
        out_shape=[jax.ShapeDtypeStruct((T, 1024), BF16), jax.ShapeDtypeStruct((1, Q_LORA), F32),
                   jax.ShapeDtypeStruct((1, KV_LORA), F32)],
        compiler_params=_arb(1))(P, dqn, dkvn, dkr, dab, gq, gkv)


def _mem_probs(qh, kh):
    s = _dot(qh, kh, NT) * MEM_SCALE
    p = jnp.exp(s - jnp.max(s, axis=1, keepdims=True))
    return p / jnp.sum(p, axis=1, keepdims=True)


def mem_attn_fwd(P, MKV, B, S, M, tq=512):
    T = B * S
    tq = min(tq, S)
    nq = S // tq

    def body(q_ref, kv_ref, o_ref):
        for h in range(N_HEADS):
            sl = slice(h * 128, (h + 1) * 128)
            p = _mem_probs(q_ref[:, sl].astype(BF16), kv_ref[:, sl])
            o_ref[:, sl] = _dot(p.astype(BF16), kv_ref[:, 512 + h * 128:512 + (h + 1) * 128], NN)

    return pl.pallas_call(
        body, name="mem_attn_fwd", grid=(B, nq),
        in_specs=[BS((tq, 512), lambda b, i: (b * nq + i, OFF_MEMQ // 512)), BS((M, 1024), lambda b, i: (b, 0))],
        out_specs=BS((tq, 512), lambda b, i: (b * nq + i, 0)),
        out_shape=jax.ShapeDtypeStruct((T, 512), F32), compiler_params=_arb(2))(P, MKV)


def mem_attn_bwd(P, MKV, dO, B, S, M, tq=512):
    T = B * S
    tq = min(tq, S)
    nq = S // tq

    def body(q_ref, kv_ref, do_ref, dq_ref, dkv_ref):
        @pl.when(pl.program_id(1) == 0)
        def _():
            dkv_ref[...] = jnp.zeros_like(dkv_ref)

        for h in range(N_HEADS):
            sl = slice(h * 128, (h + 1) * 128)
            sv = slice(512 + h * 128, 512 + (h + 1) * 128)
            qh = q_ref[:, sl].astype(BF16)
            kh = kv_ref[:, sl]
            do = do_ref[:, sl].astype(BF16)
            p = _mem_probs(qh, kh)
            dkv_ref[:, sv] += _dot(p.astype(BF16), do, TN)
            dp = _dot(do, kv_ref[:, sv], NT)
            ds = (p * (dp - jnp.sum(dp * p, axis=1, keepdims=True)) * MEM_SCALE).astype(BF16)
            dq_ref[:, sl] = _dot(ds, kh, NN).astype(BF16)
            dkv_ref[:, sl] += _dot(ds, qh, TN)

    return pl.pallas_call(
        body, name="mem_attn_bwd", grid=(B, nq),
        in_specs=[BS((tq, 512), lambda b, i: (b * nq + i, OFF_MEMQ // 512)), BS((M, 1024), lambda b, i: (b, 0)),
                  BS((tq, 512), lambda b, i: (b * nq + i, 0))],
        out_specs=[BS((tq, 512), lambda b, i: (b * nq + i, 0)), BS((M, 1024), lambda b, i: (b, 0))],
        out_shape=[jax.ShapeDtypeStruct((T, 512), BF16), jax.ShapeDtypeStruct((B * M, 1024), F32)],
        compiler_params=_arb(2))(P, MKV, dO)


def gain_grad(x, dy, name, tm=256):
    T, n = x.shape
    tm = min(tm, T)

    def body(x_ref, dy_ref, o_ref):
        @pl.when(pl.program_id(0) == 0)
        def _():
            o_ref[...] = jnp.zeros_like(o_ref)

        xv = x_ref[...]
        xh = xv * lax.rsqrt(jnp.mean(xv * xv, axis=-1, keepdims=True) + EPS)
        o_ref[...] += jnp.sum(dy_ref[...] * xh, axis=0, keepdims=True)

    return pl.pallas_call(
        body, name=name, grid=(T // tm,),
        in_specs=[BS((tm, n), lambda i: (i, 0))] * 2, out_specs=BS((1, n), lambda i: (0, 0)),
        out_shape=jax.ShapeDtypeStruct((1, n), F32), compiler_params=_arb(1))(x, dy)


def _conv_silu(x, w, t):
    y = x * w[3:4, :]
    for s in range(1, GDN_CONV):
        y = y + jnp.where(t >= s, pltpu.roll(x, s, 0), 0.0) * w[3 - s:4 - s, :]
    return y, _sigmoid(y)


def gdn_prep_fwd(P, conv_w, B, S):
    T = B * S

    def body(x_ref, w_ref, o_ref):
        kind = pl.program_id(1)
        t = lax.broadcasted_iota(jnp.int32, (S, 1), 0)
        y, sg = _conv_silu(x_ref[...], w_ref[...], t)
        a = y * sg
        scale = jnp.where(kind == 0, GDN_SCALE, 1.0).astype(F32)
        for h in range(N_HEADS):
            sl = slice(h * 128, (h + 1) * 128)
            seg = a[:, sl]
            n = lax.rsqrt(jnp.sum(seg * seg, axis=-1, keepdims=True) + EPS)
            o_ref[:, sl] = jnp.where(kind < 2, seg * (n * scale), seg)

    return pl.pallas_call(
        body, name="gdn_prep_fwd", grid=(B, 3),
        in_specs=[BS((S, 512), lambda b, k: (b, OFF_GDN // 512 + k)), BS((GDN_CONV, 512), lambda b, k: (0, k))],
        out_specs=BS((S, 512), lambda b, k: (b, k)),
        out_shape=jax.ShapeDtypeStruct((T, GDN_QKV), F32), compiler_params=_arb(2))(P, conv_w)


def gdn_prep_bwd(P, dqkv, conv_w, B, S):
    T = B * S

    def body(x_ref, d_ref, w_ref, o_ref, gw_ref):
        kind = pl.program_id(0)

        @pl.when(pl.program_id(1) == 0)
        def _():
            gw_ref[...] = jnp.zeros_like(gw_ref)

        t = lax.broadcasted_iota(jnp.int32, (S, 1), 0)
        x = x_ref[...]
        w = w_ref[...]
        y, sg = _conv_silu(x, w, t)
        a = y * sg
        scale = jnp.where(kind == 0, GDN_SCALE, 1.0).astype(F32)
        das = []
        for h in range(N_HEADS):
            sl = slice(h * 128, (h + 1) * 128)
            seg, dseg = a[:, sl], d_ref[:, sl]
            n = lax.rsqrt(jnp.sum(seg * seg, axis=-1, keepdims=True) + EPS)
            dn = scale * (n * dseg - seg * (n * n * n) * jnp.sum(dseg * seg, axis=-1, keepdims=True))
            das.append(jnp.where(kind < 2, dn, dseg))
        dy = jnp.concatenate(das, axis=1) * (sg * (1.0 + y * (1.0 - sg)))
        dx = dy * w[3:4, :]
        gw_ref[3:4, :] += jnp.sum(dy * x, axis=0, keepdims=True)
        for s in range(1, GDN_CONV):
            dx = dx + jnp.where(t + s < S, pltpu.roll(dy, S - s, 0), 0.0) * w[3 - s:4 - s, :]
            gw_ref[3 - s:4 - s, :] += jnp.sum(dy * jnp.where(t >= s, pltpu.roll(x, s, 0), 0.0), axis=0, keepdims=True)
        o_ref[...] = dx.astype(BF16)

    return pl.pallas_call(
        body, name="gdn_prep_bwd", grid=(3, B),
        in_specs=[BS((S, 512), lambda k, b: (b, OFF_GDN // 512 + k)), BS((S, 512), lambda k, b: (b, k)),
                  BS((GDN_CONV, 512), lambda k, b: (0, k))],
        out_specs=[BS((S, 512), lambda k, b: (b, k)), BS((GDN_CONV, 512), lambda k, b: (0, k))],
        out_shape=[jax.ShapeDtypeStruct((T, GDN_QKV), BF16), jax.ShapeDtypeStruct((GDN_CONV, GDN_QKV), F32)],
        compiler_params=_arb(2))(P, dqkv, conv_w)


def _chunk_row(n_rows):
    return lax.broadcasted_iota(jnp.int32, (n_rows, 1), 0) % CHUNK


def gdn_gate_fwd(P, alog_row, dt_row, B, S):
    T = B * S

    def body(x_ref, al_ref, dt_ref, o_ref):
        x = x_ref[...]
        lane = lax.broadcasted_iota(jnp.int32, (1, 128), 1)
        g = jnp.where(lane < 4, -jnp.exp(al_ref[...]) * _softplus(x + dt_ref[...]), 0.0)
        t = _chunk_row(S)
        for s in (1, 2, 4, 8, 16, 32):
            g = g + jnp.where(t >= s, pltpu.roll(g, s, 0), 0.0)
        o_ref[...] = jnp.where(lane < 4, g, jnp.where(lane < 8, _sigmoid(x), 0.0))

    row = BS((1, 128), lambda b: (0, 0))
    return pl.pallas_call(
        body, name="gdn_gate_fwd", grid=(B,),
        in_specs=[BS((S, 128), lambda b: (b, 768 // 128)), row, row], out_specs=BS((S, 128), lambda b: (b, 0)),
        out_shape=jax.ShapeDtypeStruct((T, 128), F32), compiler_params=_arb(1))(P, alog_row, dt_row)


def gdn_gate_bwd(P, dGB, alog_row, dt_row, B, S):
    T = B * S

    def body(x_ref, d_ref, al_ref, dt_ref, o_ref, acc_ref):
        @pl.when(pl.program_id(0) == 0)
        def _():
            acc_ref[...] = jnp.zeros_like(acc_ref)

        x, d = x_ref[...], d_ref[...]
        lane = lax.broadcasted_iota(jnp.int32, (1, 128), 1)
        z = x + dt_ref[...]
        coef = -jnp.exp(al_ref[...])
        g = coef * _softplus(z)
        da = jnp.where(lane < 4, d * coef * _sigmoid(z), 0.0)
        beta = _sigmoid(x)
        o_ref[...] = jnp.where(lane < 4, da, jnp.where(lane < 8, d * beta * (1.0 - beta), 0.0)).astype(BF16)
        acc_ref[0:1, :] += jnp.sum(jnp.where(lane < 4, d * g, 0.0), axis=0, keepdims=True)
        acc_ref[1:2, :] += jnp.sum(da, axis=0, keepdims=True)

    row = BS((1, 128), lambda b: (0, 0))
    return pl.pallas_call(
        body, name="gdn_gate_bwd", grid=(B,),
        in_specs=[BS((S, 128), lambda b: (b, 768 // 128)), BS((S, 128), lambda b: (b, 0)), row, row],
        out_specs=[BS((S, 128), lambda b: (b, 0)), BS((8, 128), lambda b: (0, 0))],
        out_shape=[jax.ShapeDtypeStruct((T, 128), BF16), jax.ShapeDtypeStruct((8, 128), F32)],
        compiler_params=_arb(1))(P, dGB, alog_row, dt_row)


def _chunk_masks(nc):
    r = lax.broadcasted_iota(jnp.int32, (nc, CHUNK, CHUNK), 1)
    c = lax.broadcasted_iota(jnp.int32, (nc, CHUNK, CHUNK), 2)
    return r >= c, r > c


def _chunk_local(q, k, gc, gr, beta, incl, strict):
    decay = jnp.exp(jnp.where(incl, gc - gr, NEG))
    kb = k * beta
    kbf = k.astype(BF16)
    m_kk = _bdot("gcd,gjd->gcj", kb.astype(BF16), kbf)
    l_mat = jnp.where(strict, m_kk * decay, 0.0)
    a_mat = _bdot("gcd,gjd->gcj", q.astype(BF16), kbf) * decay
    return decay, kb, l_mat, a_mat


def gdn_chunk_fwd(qkv, GB, Grow, B, S, nc=8):
    T = B * S
    N = S // CHUNK
    nc = min(nc, N)
    nb = N // nc
    R = nc * CHUNK
    hi = lax.Precision.HIGHEST

    def body(q_ref, k_ref, v_ref, gb_ref, gr_ref, u_ref, w_ref, t_ref, a_ref):
        incl, strict = _chunk_masks(nc)
        eye = (lax.broadcasted_iota(jnp.int32, (nc, CHUNK, CHUNK), 1)
               == lax.broadcasted_iota(jnp.int32, (nc, CHUNK, CHUNK), 2)).astype(F32)
        for h in range(N_HEADS):
            sl = slice(h * 128, (h + 1) * 128)
            q = q_ref[:, sl].reshape(nc, CHUNK, 128)
            k = k_ref[:, sl].reshape(nc, CHUNK, 128)
            v = v_ref[:, sl].reshape(nc, CHUNK, 128)
            gc = gb_ref[:, h:h + 1].reshape(nc, CHUNK, 1)
            beta = gb_ref[:, 4 + h:5 + h].reshape(nc, CHUNK, 1)
            gr = gr_ref[h][:, None, :]
            _, kb, l_mat, a_mat = _chunk_local(q, k, gc, gr, beta, incl, strict)
            pw = -l_mat
            tinv = eye + pw
            for _ in range(5):
                pw = _bdot("gij,gjk->gik", pw, pw, hi)
                tinv = tinv + _bdot("gij,gjk->gik", tinv, pw, hi)
            tb = tinv.astype(BF16)
            u = _bdot("gcj,gjv->gcv", tb, (v * beta).astype(BF16))
            w = _bdot("gcj,gjk->gck", tb, (kb * jnp.exp(gc)).astype(BF16))
            u_ref[:, sl] = u.reshape(R, 128)
            w_ref[:, sl] = w.reshape(R, 128)
            t_ref[h] = tinv
            a_ref[h] = a_mat

    rowb = lambda c, j: BS((R, c), lambda b, n: (b * nb + n, j))
    mat = BS((None, N_HEADS, nc, CHUNK, CHUNK), lambda b, n: (b, 0, n, 0, 0))
    return pl.pallas_call(
        body, name="gdn_chunk_fwd", grid=(B, nb),
        in_specs=[rowb(512, 0), rowb(512, 1), rowb(512, 2), rowb(128, 0),
                  BS((None, N_HEADS, nc, CHUNK), lambda b, n: (b, 0, n, 0))],
        out_specs=[rowb(512, 0), rowb(512, 0), mat, mat],
        out_shape=[jax.ShapeDtypeStruct((T, 512), F32), jax.ShapeDtypeStruct((T, 512), F32),
                   jax.ShapeDtypeStruct((B, N_HEADS, N, CHUNK, CHUNK), F32),
                   jax.ShapeDtypeStruct((B, N_HEADS, N, CHUNK, CHUNK), F32)],
        compiler_params=_arb(2))(qkv, qkv, qkv, GB, Grow)


def gdn_scan_fwd(qkv3, U3, W3, GB3, A, B, S):
    N = S // CHUNK

    def body(q_ref, k_ref, u_ref, w_ref, gb_ref, a_ref, o_ref, vn_ref, st_ref, s_s):
        @pl.when(pl.program_id(0) == 0)
        def _():
            s_s[...] = jnp.zeros_like(s_s)

        for b in range(B):
            for h in range(N_HEADS):
                sl = slice(h * 128, (h + 1) * 128)
                st = s_s[b, h]
                st_ref[b, h] = st
                stb = st.astype(BF16)
                g = gb_ref[b, :, h:h + 1]
                gl = g[CHUNK - 1:CHUNK, :]
                vn = u_ref[b, :, sl] - _dot(w_ref[b, :, sl].astype(BF16), stb, NN)
                vnb = vn.astype(BF16)
                o = _dot((q_ref[b, :, sl] * jnp.exp(g)).astype(BF16), stb, NN) + _dot(a_ref[b, h].astype(BF16), vnb, NN)
                vn_ref[b, :, sl] = vn
                o_ref[b, :, sl] = o
                s_s[b, h] = st * jnp.exp(gl) + _dot((k_ref[b, :, sl] * jnp.exp(gl - g)).astype(BF16), vnb, TN)

    tok = lambda c, j: BS((B, CHUNK, c), lambda n: (0, n, j))
    return pl.pallas_call(
        body, name="gdn_scan_fwd", grid=(N,),
        in_specs=[tok(512, 0), tok(512, 1), tok(512, 0), tok(512, 0), tok(128, 0),
                  BS((B, N_HEADS, None, CHUNK, CHUNK), lambda n: (0, 0, n, 0, 0))],
        out_specs=[tok(512, 0), tok(512, 0), BS((B, N_HEADS, None, 128, 128), lambda n: (0, 0, n, 0, 0))],
        out_shape=[jax.ShapeDtypeStruct((B, S, 512), F32), jax.ShapeDtypeStruct((B, S, 512), F32),
                   jax.ShapeDtypeStruct((B, N_HEADS, N, 128, 128), F32)],
        scratch_shapes=[pltpu.VMEM((B, N_HEADS, 128, 128), F32)],
        compiler_params=_arb(1))(qkv3, qkv3, U3, W3, GB3, A)


def gdn_scan_bwd(dO3, qkv3, W3, Vn3, GB3, A, St, B, S):
    N = S // CHUNK

    def body(do_ref, q_ref, k_ref, w_ref, vn_ref, gb_ref, a_ref, st_ref,
             du_ref, dw_ref, dq_ref, dk_ref, da_ref, dg_ref, ds_s):
        @pl.when(pl.program_id(0) == 0)
        def _():
            ds_s[...] = jnp.zeros_like(ds_s)

        lane = lax.broadcasted_iota(jnp.int32, (1, 128), 1)
        last = lax.broadcasted_iota(jnp.int32, (CHUNK, 1), 0) == CHUNK - 1
        for b in range(B):
            dg_all = jnp.zeros((CHUNK, 128), F32)
            for h in range(N_HEADS):
                sl = slice(h * 128, (h + 1) * 128)
                st = st_ref[b, h]
                stb = st.astype(BF16)
                dsn = ds_s[b, h]
                dsnb = dsn.astype(BF16)
                g = gb_ref[b, :, h:h + 1]
                gl = g[CHUNK - 1:CHUNK, :]
                egl = jnp.exp(gl)
                ekd = jnp.exp(gl - g)
                eg = jnp.exp(g)
                q, k = q_ref[b, :, sl], k_ref[b, :, sl]
                kd = k * ekd
                qg = q * eg
                do = do_ref[b, :, sl].astype(BF16)
                vnb = vn_ref[b, :, sl].astype(BF16)
                dvn = _dot(a_ref[b, h].astype(BF16), do, TN) + _dot(kd.astype(BF16), dsnb, NN)
                dvnb = dvn.astype(BF16)
                da_ref[b, h] = _dot(do, vnb, NT)
                dqg = _dot(do, stb, NT)
                dkd = _dot(vnb, dsnb, NT)
                ds_s[b, h] = (_dot(qg.astype(BF16), do, TN) + egl * dsn - _dot(w_ref[b, :, sl].astype(BF16), dvnb, TN))
                du_ref[b, :, sl] = dvn
                dw_ref[b, :, sl] = -_dot(dvnb, stb, NT)
                dq_ref[b, :, sl] = dqg * eg
                dk_ref[b, :, sl] = dkd * ekd
                ddel = jnp.sum(dkd * kd, axis=1, keepdims=True)
                dgl = jnp.sum(ddel, axis=0, keepdims=True) + jnp.sum(jnp.sum(st * dsn, axis=1, keepdims=True), axis=0, keepdims=True) * egl
                col = jnp.sum(dqg * qg, axis=1, keepdims=True) - ddel + jnp.where(last, dgl, 0.0)
                dg_all = jnp.where(lane == h, col, dg_all)
            dg_ref[b] = dg_all

    tok = lambda c, j: BS((B, CHUNK, c), lambda n: (0, N - 1 - n, j))
    mat = lambda d: BS((B, N_HEADS, None, d, d), lambda n: (0, 0, N - 1 - n, 0, 0))
    return pl.pallas_call(
        body, name="gdn_scan_bwd", grid=(N,),
        in_specs=[tok(512, 0), tok(512, 0), tok(512, 1), tok(512, 0), tok(512, 0), tok(128, 0), mat(CHUNK), mat(128)],
        out_specs=[tok(512, 0), tok(512, 0), tok(512, 0), tok(512, 0), mat(CHUNK), tok(128, 0)],
        out_shape=[jax.ShapeDtypeStruct((B, S, 512), F32)] * 4
        + [jax.ShapeDtypeStruct((B, N_HEADS, N, CHUNK, CHUNK), F32), jax.ShapeDtypeStruct((B, S, 128), F32)],
        scratch_shapes=[pltpu.VMEM((B, N_HEADS, 128, 128), F32)],
        compiler_params=_arb(1))(dO3, qkv3, qkv3, W3, Vn3, GB3, A, St)


def gdn_chunk_bwd(qkv, GB, Grow, Tinv, dA, dU, dW, dQ1, dK1, dG1, B, S, nc=8):
    T = B * S
    N = S // CHUNK
    nc = min(nc, N)
    nb = N // nc
    R = nc * CHUNK

    def body(q_ref, k_ref, v_ref, gb_ref, gr_ref, t_ref, da_ref, du_ref, dw_ref, dq1_ref, dk1_ref, dg1_ref, o_ref, dgb_ref):
        incl, strict = _chunk_masks(nc)
        lane = lax.broadcasted_iota(jnp.int32, (1, 128), 1)
        dg_all = dg1_ref[...]
        db_all = jnp.zeros((R, 128), F32)
        for h in range(N_HEADS):
            sl = slice(h * 128, (h + 1) * 128)
            q = q_ref[:, sl].reshape(nc, CHUNK, 128)
            k = k_ref[:, sl].reshape(nc, CHUNK, 128)
            v = v_ref[:, sl].reshape(nc, CHUNK, 128)
            gc = gb_ref[:, h:h + 1].reshape(nc, CHUNK, 1)
            beta = gb_ref[:, 4 + h:5 + h].reshape(nc, CHUNK, 1)
            gr = gr_ref[h][:, None, :]
            decay, kb, l_mat, a_mat = _chunk_local(q, k, gc, gr, beta, incl, strict)
            eg = jnp.exp(gc)
            kbg = kb * eg
            vb = v * beta
            tb = t_ref[h].astype(BF16)
            du = du_ref[:, sl].reshape(nc, CHUNK, 128).astype(BF16)
            dw = dw_ref[:, sl].reshape(nc, CHUNK, 128).astype(BF16)
            dvb = _bdot("gcj,gcv->gjv", tb, du)
            dkbg = _bdot("gcj,gck->gjk", tb, dw)
            dt = _bdot("gcv,gjv->gcj", du, vb.astype(BF16)) + _bdot("gck,gjk->gcj", dw, kbg.astype(BF16))
            tmp = _bdot("gac,gab->gcb", tb, dt.astype(BF16))
            dl = jnp.where(strict, -_bdot("gcb,gdb->gcd", tmp.astype(BF16), tb), 0.0)
            da = da_ref[h]
            dm = (dl * decay).astype(BF16)
            dqk = (da * decay).astype(BF16)
            kbf = k.astype(BF16)
            dkb = _bdot("gcj,gjd->gcd", dm, kbf) + dkbg * eg
            dk = (_bdot("gcj,gcd->gjd", dm, kb.astype(BF16)) + _bdot("gcj,gcd->gjd", dqk, q.astype(BF16))
                  + dk1_ref[:, sl].reshape(nc, CHUNK, 128) + dkb * beta)
            dq = _bdot("gcj,gjd->gcd", dqk, kbf) + dq1_ref[:, sl].reshape(nc, CHUNK, 128)
            e = dl * l_mat + da * a_mat
            dgc = (jnp.sum(e, axis=2, keepdims=True) - jnp.sum(jnp.swapaxes(e, 1, 2), axis=2, keepdims=True)
                   + jnp.sum(dkbg * kbg, axis=2, keepdims=True))
            dbeta = jnp.sum(dkb * k, axis=2, keepdims=True) + jnp.sum(dvb * v, axis=2, keepdims=True)
            o_ref[:, sl] = dq.reshape(R, 128)
            o_ref[:, 512 + h * 128:512 + (h + 1) * 128] = dk.reshape(R, 128)
            o_ref[:, 1024 + h * 128:1024 + (h + 1) * 128] = (dvb * beta).reshape(R, 128)
            dg_all = dg_all + jnp.where(lane == h, dgc.reshape(R, 1), 0.0)
            db_all = jnp.where(lane == 4 + h, dbeta.reshape(R, 1), db_all)
        t = _chunk_row(R)
        for s in (1, 2, 4, 8, 16, 32):
            dg_all = dg_all + jnp.where(t + s < CHUNK, pltpu.roll(dg_all, R - s, 0), 0.0)
        dgb_ref[...] = jnp.where(lane < 4, dg_all, db_all)

    rowb = lambda c, j: BS((R, c), lambda b, n: (b * nb + n, j))
    mat = BS((None, N_HEADS, nc, CHUNK, CHUNK), lambda b, n: (b, 0, n, 0, 0))
    return pl.pallas_call(
        body, name="gdn_chunk_bwd", grid=(B, nb),
        in_specs=[rowb(512, 0), rowb(512, 1), rowb(512, 2), rowb(128, 0),
                  BS((None, N_HEADS, nc, CHUNK), lambda b, n: (b, 0, n, 0)), mat, mat,
                  rowb(512, 0), rowb(512, 0), rowb(512, 0), rowb(512, 0), rowb(128, 0)],
        out_specs=[rowb(GDN_QKV, 0), rowb(128, 0)],
        out_shape=[jax.ShapeDtypeStruct((T, GDN_QKV), F32), jax.ShapeDtypeStruct((T, 128), F32)],
        compiler_params=_arb(2))(qkv, qkv, qkv, GB, Grow, Tinv, dA, dU, dW, dQ1, dK1, dG1)


def _gdn_out_norm(og, gg):
    outs, xhs, rs = [], [], []
    for h in range(N_HEADS):
        seg = og[:, h * 128:(h + 1) * 128]
        r = lax.rsqrt(jnp.mean(seg * seg, axis=-1, keepdims=True) + EPS)
        xh = seg * r
        outs.append(xh * gg)
        xhs.append(xh)
        rs.append(r)
    return outs, xhs, rs


def merge_fwd(o_mla, o_gdn, o_mem, P, x, tgt, w_out, g_gdn, g_fin, tm=256):
    T = x.shape[0]
    tm = min(tm, T)

    def body(om_ref, og_ref, oc_ref, gate_ref, x_ref, t_ref, w_ref, gg_ref, gf_ref, mix_ref, dx_ref, sq_ref, gnf_ref):
        @pl.when(pl.program_id(0) == 0)
        def _():
            sq_ref[...] = jnp.zeros_like(sq_ref)
            gnf_ref[...] = jnp.zeros_like(gnf_ref)

        ogn, _, _ = _gdn_out_norm(og_ref[...], gg_ref[...])
        cat = jnp.concatenate([om_ref[...]] + ogn + [oc_ref[...]], axis=1)
        gt = gate_ref[...]
        mixed = (cat * (gt * _sigmoid(gt))).astype(BF16)
        mix_ref[...] = mixed
        x2 = x_ref[...] + _dot(mixed, w_ref[...], NN)
        r2 = lax.rsqrt(jnp.mean(x2 * x2, axis=-1, keepdims=True) + EPS)
        xh = x2 * r2
        gf = gf_ref[...]
        diff = xh * gf - t_ref[...]
        sq_ref[...] += jnp.sum(diff * diff, axis=0, keepdims=True)
        dy = diff * (1.0 / D_MODEL)
        gnf_ref[...] += jnp.sum(dy * xh, axis=0, keepdims=True)
        dxh = dy * gf
        dx_ref[...] = r2 * (dxh - xh * jnp.mean(dxh * xh, axis=-1, keepdims=True))

    rowb = lambda c, j=0: BS((tm, c), lambda i: (i, j))
    full = lambda r, c: BS((r, c), lambda i: (0, 0))
    return pl.pallas_call(
        body, name="merge_fwd", grid=(T // tm,),
        in_specs=[rowb(512), rowb(512), rowb(512), rowb(D_MIX, OFF_GATE // D_MIX), rowb(D_MODEL), rowb(D_MODEL),
                  full(D_MIX, D_MODEL), full(1, 128), full(1, D_MODEL)],
        out_specs=[rowb(D_MIX), rowb(D_MODEL), full(1, D_MODEL), full(1, D_MODEL)],
        out_shape=[jax.ShapeDtypeStruct((T, D_MIX), BF16), jax.ShapeDtypeStruct((T, D_MODEL), F32),
                   jax.ShapeDtypeStruct((1, D_MODEL), F32), jax.ShapeDtypeStruct((1, D_MODEL), F32)],
        compiler_params=_arb(1))(o_mla, o_gdn, o_mem, P, x, tgt, w_out, g_gdn, g_fin)


def merge_bwd(dx2, o_mla, o_gdn, o_mem, P, w_out, g_gdn, tm=256):
    T = dx2.shape[0]
    tm = min(tm, T)

    def body(dx_ref, om_ref, og_ref, oc_ref, gate_ref, w_ref, gg_ref, dgate_ref, dom_ref, dog_ref, doc_ref, ggn_ref):
        @pl.when(pl.program_id(0) == 0)
        def _():
            ggn_ref[...] = jnp.zeros_like(ggn_ref)

        gg = gg_ref[...]
        dmix = _dot(dx_ref[...].astype(BF16), w_ref[...], NT)
        ogn, xhs, rs = _gdn_out_norm(og_ref[...], gg)
        cat = jnp.concatenate([om_ref[...]] + ogn + [oc_ref[...]], axis=1)
        gt = gate_ref[...]
        sg = _sigmoid(gt)
        dgate_ref[...] = (dmix * cat * (sg * (1.0 + gt * (1.0 - sg)))).astype(BF16)
        dcat = dmix * (gt * sg)
        dom_ref[...] = dcat[:, :512]
        doc_ref[...] = dcat[:, 1024:]
        acc = jnp.zeros((1, 128), F32)
        for h in range(N_HEADS):
            dseg = dcat[:, 512 + h * 128:512 + (h + 1) * 128]
            acc = acc + jnp.sum(dseg * xhs[h], axis=0, keepdims=True)
            dxh = dseg * gg
            dog_ref[:, h * 128:(h + 1) * 128] = rs[h] * (dxh - xhs[h] * jnp.mean(dxh * xhs[h], axis=-1, keepdims=True))
        ggn_ref[...] += acc

    rowb = lambda c, j=0: BS((tm, c), lambda i: (i, j))
    full = lambda r, c: BS((r, c), lambda i: (0, 0))
    return pl.pallas_call(
        body, name="merge_bwd", grid=(T // tm,),
        in_specs=[rowb(D_MODEL), rowb(512), rowb(512), rowb(512), rowb(D_MIX, OFF_GATE // D_MIX),
                  full(D_MIX, D_MODEL), full(1, 128)],
        out_specs=[rowb(D_MIX), rowb(512), rowb(512), rowb(512), full(1, 128)],
        out_shape=[jax.ShapeDtypeStruct((T, D_MIX), BF16)] + [jax.ShapeDtypeStruct((T, 512), F32)] * 3
        + [jax.ShapeDtypeStruct((1, 128), F32)],
        compiler_params=_arb(1))(dx2, o_mla, o_gdn, o_mem, P, w_out, g_gdn)


def in_norm_bwd(x, dh, dx2, gain, tm=256):
    T, n = x.shape
    tm = min(tm, T)

    def body(x_ref, dh_ref, dx2_ref, g_ref, o_ref, acc_ref):
        @pl.when(pl.program_id(0) == 0)
        def _():
            acc_ref[...] = jnp.zeros_like(acc_ref)

        xv = x_ref[...]
        r = lax.rsqrt(jnp.mean(xv * xv, axis=-1, keepdims=True) + EPS)
        xh = xv * r
        dy = dh_ref[...]
        acc_ref[...] += jnp.sum(dy * xh, axis=0, keepdims=True)
        dxh = dy * g_ref[...]
        o_ref[...] = dx2_ref[...] + r * (dxh - xh * jnp.mean(dxh * xh, axis=-1, keepdims=True))

    rowb = BS((tm, n), lambda i: (i, 0))
    full = BS((1, n), lambda i: (0, 0))
    return pl.pallas_call(
        body, name="in_norm_bwd", grid=(T // tm,),
        in_specs=[rowb, rowb, rowb, full], out_specs=[rowb, full],
        out_shape=[jax.ShapeDtypeStruct((T, n), F32), jax.ShapeDtypeStruct((1, n), F32)],
        compiler_params=_arb(1))(x, dh, dx2, gain)


def _pad_w_in(w):
    z = lambda n: jnp.zeros((w.shape[0], n), w.dtype)
    return jnp.concatenate([w[:, 0:640], w[:, 640:672], z(32), w[:, 672:704], z(32), w[:, 2240:2248], z(248),
                            w[:, 2248:2760], w[:, 704:2240], w[:, 2760:4296]], axis=1)


def _unpad_w_in(g):
    return jnp.concatenate([g[:, 0:640], g[:, 640:672], g[:, 704:736], g[:, OFF_GDN:OFF_GDN + GDN_QKV], g[:, 768:776],
                            g[:, OFF_MEMQ:OFF_MEMQ + 512], g[:, OFF_GATE:OFF_GATE + D_MIX]], axis=1)


def _pad_w_q_b(w):
    z = jnp.zeros((w.shape[0], 32), w.dtype)
    parts = []
    for h in range(N_HEADS):
        lo = h * (MLA_NOPE + MLA_ROPE)
        parts += [w[:, lo:lo + 128], w[:, lo + 128:lo + 160], z, w[:, lo + 160:lo + 192], z]
    return jnp.concatenate(parts, axis=1)


def _unpad_w_q_b(g):
    parts = []
    for h in range(N_HEADS):
        lo = h * HEAD_PAD
        parts += [g[:, lo:lo + 128], g[:, lo + 128:lo + 160], g[:, lo + 192:lo + 224]]
    return jnp.concatenate(parts, axis=1)


def _perm_w_kv_b(w):
    return jnp.concatenate([w[:, h * 256:h * 256 + 128] for h in range(N_HEADS)]
                           + [w[:, h * 256 + 128:h * 256 + 256] for h in range(N_HEADS)], axis=1)


def _unperm_w_kv_b(g):
    parts = []
    for h in range(N_HEADS):
        parts += [g[:, h * 128:(h + 1) * 128], g[:, 512 + h * 128:512 + (h + 1) * 128]]
    return jnp.concatenate(parts, axis=1)


def _lane_row(v4):
    return jnp.pad(v4.reshape(1, -1).astype(F32), ((0, 0), (0, 128 - v4.size)))


def _pack(pieces, n_rows):
    flat = jnp.concatenate([p.reshape(-1) for p in pieces])
    return jnp.pad(flat, (0, n_rows * 1024 - flat.size)).reshape(n_rows, 1024)


def _unpack(block, shapes):
    flat = block.reshape(-1)
    out, off = [], 0
    for shp in shapes:
        n = int(np.prod(shp))
        out.append(flat[off:off + n].reshape(shp))
        off += n
    return out


N_CHIPS = 4
MESH = pl.DeviceIdType.MESH
ANY = BS(memory_space=pl.ANY)


def _place():
    return lax.axis_index("x"), lax.axis_index("y"), lax.axis_index("c")


def _other_chips(x, y):
    return [(1 - x, y), (x, 1 - y), (1 - x, 1 - y)]


def allgather_chips(shards, name):
    n = len(shards)

    def body(*refs):
        s_refs, o_refs = refs[:n], refs[n:2 * n]
        send_sems, recv_sems, local_sems = refs[2 * n:]
        x, y, c = _place()
        chips = _other_chips(x, y)

        def copy(k, src, dst, to):
            return pltpu.make_async_remote_copy(src_ref=src, dst_ref=dst, send_sem=send_sems.at[k], recv_sem=recv_sems.at[k],
                                                device_id=to, device_id_type=MESH)

        owns, first, passed = [], [], []
        for i, (s_ref, o_ref) in enumerate(zip(s_refs, o_refs)):
            hr = s_ref.shape[0] // 2
            rows = pl.ds(pl.multiple_of(c * hr, 16), hr)
            owns.append(pltpu.make_async_copy(s_ref, o_ref.at[2 * x + y], local_sems.at[i]))
            owns[-1].start()
            for j, (px, py) in enumerate(chips):
                first.append(copy(6 * i + j, s_ref.at[rows], o_ref.at[2 * x + y, rows], (px, py, c)))
                first[-1].start()
        for i, (s_ref, o_ref) in enumerate(zip(s_refs, o_refs)):
            hr = s_ref.shape[0] // 2
            rows = pl.ds(pl.multiple_of(c * hr, 16), hr)
            for j, (px, py) in enumerate(chips):
                copy(6 * i + j, s_ref.at[rows], o_ref.at[2 * px + py, rows], (px, py, c)).wait_recv()
                passed.append(copy(6 * i + 3 + j, o_ref.at[2 * px + py, rows], o_ref.at[2 * px + py, rows], (x, y, 1 - c)))
                passed[-1].start()
        for i, (s_ref, o_ref) in enumerate(zip(s_refs, o_refs)):
            hr = s_ref.shape[0] // 2
            sib_rows = pl.ds(pl.multiple_of((1 - c) * hr, 16), hr)
            for j, (px, py) in enumerate(chips):
                copy(6 * i + 3 + j, s_ref.at[sib_rows], o_ref.at[2 * px + py, sib_rows], (x, y, 1 - c)).wait_recv()
        for cp in first + passed:
            cp.wait_send()
        for cp in owns:
            cp.wait()

    return pl.pallas_call(
        body, name=name, in_specs=[ANY] * n, out_specs=[ANY] * n,
        out_shape=[jax.ShapeDtypeStruct((N_CHIPS,) + s.shape, s.dtype) for s in shards],
        scratch_shapes=[pltpu.SemaphoreType.DMA((6 * n,)), pltpu.SemaphoreType.DMA((6 * n,)), pltpu.SemaphoreType.DMA((n,))])(*shards)


def allgather_devices(block, name):
    R, C = block.shape

    def body(b_ref, o_ref, send_sems, recv_sems, local_sem):
        x, y, c = _place()
        me = 4 * x + 2 * y + c
        own = pltpu.make_async_copy(b_ref, o_ref.at[me], local_sem)
        own.start()
        copies = []
        for r in range(1, 8):
            px = 1 - x if r & 4 else x
            py = 1 - y if r & 2 else y
            pc = 1 - c if r & 1 else c
            send = pltpu.make_async_remote_copy(src_ref=b_ref, dst_ref=o_ref.at[me], send_sem=send_sems.at[r - 1],
                                                recv_sem=recv_sems.at[r - 1], device_id=(px, py, pc), device_id_type=MESH)
            recv = pltpu.make_async_remote_copy(src_ref=b_ref, dst_ref=o_ref.at[4 * px + 2 * py + pc], send_sem=send_sems.at[r - 1],
                                                recv_sem=recv_sems.at[r - 1], device_id=(px, py, pc), device_id_type=MESH)
            send.start()
            copies.append((send, recv))
        for send, recv in copies:
            recv.wait_recv()
            send.wait_send()
        own.wait()

    return pl.pallas_call(
        body, name=name, in_specs=[ANY], out_specs=ANY, out_shape=jax.ShapeDtypeStruct((8, R, C), block.dtype),
        scratch_shapes=[pltpu.SemaphoreType.DMA((7,)), pltpu.SemaphoreType.DMA((7,)), pltpu.SemaphoreType.DMA(())])(block)


def swap_sibling(arrs, name):
    n = len(arrs)

    def body(*refs):
        a_refs, o_refs = refs[:n], refs[n:2 * n]
        send_sems, recv_sems = refs[2 * n:]
        x, y, c = _place()
        copies = [pltpu.make_async_remote_copy(src_ref=a_ref, dst_ref=o_ref, send_sem=send_sems.at[i], recv_sem=recv_sems.at[i],
                                               device_id=(x, y, 1 - c), device_id_type=MESH)
                  for i, (a_ref, o_ref) in enumerate(zip(a_refs, o_refs))]
        for cp in copies:
            cp.start()
        for cp in copies:
            cp.wait()

    return pl.pallas_call(
        body, name=name, in_specs=[ANY] * n, out_specs=[ANY] * n,
        out_shape=[jax.ShapeDtypeStruct(a.shape, a.dtype) for a in arrs],
        scratch_shapes=[pltpu.SemaphoreType.DMA((n,)), pltpu.SemaphoreType.DMA((n,))])(*arrs)


def exchange_chips(parts, name):
    n = len(parts)

    def body(*refs):
        p_refs, o_refs = refs[:n], refs[n:2 * n]
        send_sems, recv_sems = refs[2 * n:]
        x, y, c = _place()
        copies = [pltpu.make_async_remote_copy(src_ref=p_ref.at[2 * px + py], dst_ref=o_ref.at[j], send_sem=send_sems.at[3 * i + j],
                                               recv_sem=recv_sems.at[3 * i + j], device_id=(px, py, c), device_id_type=MESH)
                  for i, (p_ref, o_ref) in enumerate(zip(p_refs, o_refs)) for j, (px, py) in enumerate(_other_chips(x, y))]
        for cp in copies:
            cp.start()
        for cp in copies:
            cp.wait()

    return pl.pallas_call(
        body, name=name, in_specs=[ANY] * n, out_specs=[ANY] * n,
        out_shape=[jax.ShapeDtypeStruct((3,) + p.shape[1:], p.dtype) for p in parts],
        scratch_shapes=[pltpu.SemaphoreType.DMA((3 * n,)), pltpu.SemaphoreType.DMA((3 * n,))])(*parts)


def add_pairs(a_list, b_list, name):
    n = len(a_list)

    def body(*refs):
        for a_ref, b_ref, o_ref in zip(refs[:n], refs[n:2 * n], refs[2 * n:]):
            o_ref[...] = (a_ref[...].astype(F32) + b_ref[...].astype(F32)).astype(BF16)

    specs = [BS((None,) + a.shape[1:], lambda q: (q, 0, 0)) for a in a_list]
    return pl.pallas_call(
        body, name=name, grid=(N_CHIPS,), in_specs=specs + specs, out_specs=specs,
        out_shape=[jax.ShapeDtypeStruct(a.shape, BF16) for a in a_list], compiler_params=_arb(1))(*a_list, *b_list)


def add_fives(a_list, b_list, parts_list, name):
    n = len(a_list)

    def body(*refs):
        for a_ref, b_ref, p_ref, o_ref in zip(refs[:n], refs[n:2 * n], refs[2 * n:3 * n], refs[3 * n:]):
            s = a_ref[...].astype(F32) + b_ref[...].astype(F32)
            for j in range(3):
                s = s + p_ref[j].astype(F32)
            o_ref[...] = s

    return pl.pallas_call(
        body, name=name, out_shape=[jax.ShapeDtypeStruct(a.shape, F32) for a in a_list])(*a_list, *b_list, *parts_list)


def sum_leading(a, name):
    def body(a_ref, o_ref):
        s = a_ref[0]
        for j in range(1, a.shape[0]):
            s = s + a_ref[j]
        o_ref[...] = s

    return pl.pallas_call(body, name=name, out_shape=jax.ShapeDtypeStruct(a.shape[1:], a.dtype))(a)


def adamw(w, g, m, v, name):
    R, C = w.shape
    tr = R
    if R > 256:
        tr = next(t for t in range(256, 7, -1) if R % t == 0 and t % 8 == 0)

    def body(w_ref, g_ref, m_ref, v_ref, d_ref, mo_ref, vo_ref):
        gv = g_ref[...]
        mn = ADAM_B1 * m_ref[...] + (1.0 - ADAM_B1) * gv
        vn = ADAM_B2 * v_ref[...] + (1.0 - ADAM_B2) * (gv * gv)
        m_hat = mn / (1.0 - ADAM_B1 ** ADAM_STEP)
        v_hat = vn / (1.0 - ADAM_B2 ** ADAM_STEP)
        d_ref[...] = -ADAM_LR * (m_hat / (jnp.sqrt(v_hat) + ADAM_EPS) + ADAM_WD * w_ref[...])
        mo_ref[...] = mn
        vo_ref[...] = vn

    blk = BS((tr, C), lambda i: (i, 0))
    return pl.pallas_call(
        body, name=name, grid=(R // tr,), in_specs=[blk] * 4, out_specs=[blk] * 3,
        out_shape=[jax.ShapeDtypeStruct((R, C), F32)] * 3, compiler_params=_arb(1))(w, g, m, v)


def local_step(x, mem, positions, tgt, norm_in, w_in, q_a_norm, w_q_b, kv_a_norm, w_kv_b, gdn_conv, gdn_a_log,
               gdn_dt_bias, gdn_norm, mem_norm, w_mem_kv, w_out, norm_final):
    B, S, D = x.shape
    M = mem.shape[1]
    T = B * S
    N = S // CHUNK
    x2d = x.reshape(T, D)
    mem2d = mem.reshape(B * M, D)
    tgt2d = tgt.reshape(T, D)

    wp = _pad_w_in(w_in)
    wq = _pad_w_q_b(w_q_b)
    wkv = _perm_w_kv_b(w_kv_b)
    alog_row, dt_row = _lane_row(gdn_a_log), _lane_row(gdn_dt_bias)

    half = MLA_ROPE // 2
    inv_freq = 1.0 / (ROPE_THETA ** (jnp.arange(half, dtype=F32) / half))
    z32 = jnp.zeros((half,), F32)
    o32 = jnp.ones((half,), F32)
    inv_row = jnp.concatenate([inv_freq, z32, inv_freq, z32]).reshape(1, 128)
    sgn_row = jnp.concatenate([-o32, z32, o32, z32]).reshape(1, 128)
    msk_row = jnp.concatenate([o32, z32, o32, z32]).reshape(1, 128)
    cos_t, sin_t = rope_tables(positions.reshape(T, 1), inv_row, sgn_row, msk_row)

    h = rms_fwd(x2d, norm_in, "rms_in")
    P = mm(h, wp, "nn", F32, "in_proj", bm=512, bn=768, bk=1024)
    Q, K, V, qn, kvn = mla_prep(P, q_a_norm, kv_a_norm, wq, wkv, cos_t, sin_t)
    o_mla, lse = mla_attn_fwd(Q, K, V, B, S)
    memn = rms_fwd(mem2d, mem_norm, "rms_mem")
    MKV = mm(memn, w_mem_kv, "nn", BF16, "mem_kv_proj", bk=1024)
    o_mem = mem_attn_fwd(P, MKV, B, S, M)
    qkv = gdn_prep_fwd(P, gdn_conv, B, S)
    GB = gdn_gate_fwd(P, alog_row, dt_row, B, S)
    Grow = jnp.transpose(GB[:, :N_HEADS].reshape(B, N, CHUNK, N_HEADS), (0, 3, 1, 2))
    U, W, Tinv, A = gdn_chunk_fwd(qkv, GB, Grow, B, S)
    qkv3, GB3 = qkv.reshape(B, S, GDN_QKV), GB.reshape(B, S, 128)
    W3 = W.reshape(B, S, 512)
    o_gdn3, Vn3, St = gdn_scan_fwd(qkv3, U.reshape(B, S, 512), W3, GB3, A, B, S)
    o_gdn = o_gdn3.reshape(T, 512)
    mixed, dx2, sq, g_norm_final = merge_fwd(o_mla, o_gdn, o_mem, P, x2d, tgt2d, w_out, gdn_norm, norm_final.reshape(1, D))

    g_w_out = mm(mixed, dx2, "tn", BF16, "grad_w_out")
    dgate, do_mla, do_gdn, do_mem, g_gdn_norm = merge_bwd(dx2, o_mla, o_gdn, o_mem, P, w_out, gdn_norm)

    dmemq, dMKV = mem_attn_bwd(P, MKV, do_mem, B, S, M)
    g_w_mem_kv = mm(memn, dMKV, "tn", BF16, "grad_w_mem_kv")
    dmemn = mm(dMKV, w_mem_kv, "nt", F32, "d_memn", bk=1024)
    g_mem_norm = gain_grad(mem2d, dmemn, "grad_mem_norm")

    dU3, dW3, dQ13, dK13, dA, dG13 = gdn_scan_bwd(do_gdn.reshape(B, S, 512), qkv3, W3, Vn3, GB3, A, St, B, S)
    r2 = lambda a: a.reshape(T, a.shape[-1])
    dqkv, dGB = gdn_chunk_bwd(qkv, GB, Grow, Tinv, dA, r2(dU3), r2(dW3), r2(dQ13), r2(dK13), r2(dG13), B, S)
    dPg, g_conv = gdn_prep_bwd(P, dqkv, gdn_conv, B, S)
    dab, g_ab = gdn_gate_bwd(P, dGB, alog_row, dt_row, B, S)

    dQ, dK, dV = mla_attn_bwd(Q, K, V, o_mla, do_mla, lse, B, S)
    dq_lin, dkv_lin, dkr = mla_post_bwd(dQ, dK, dV, cos_t, sin_t)
    dqn = mm(dq_lin, wq, "nt", F32, "d_qn", bk=1024)
    dkvn = mm(dkv_lin, wkv, "nt", F32, "d_kvn", bk=1024)
    g_wq = mm(qn, dq_lin, "tn", BF16, "grad_w_q_b")
    g_wkv = mm(kvn, dkv_lin, "tn", BF16, "grad_w_kv_b")
    dPm, g_q_a_norm, g_kv_a_norm = mla_norm_bwd(P, dqn, dkvn, dkr, dab, q_a_norm, kv_a_norm)

    dP = jnp.concatenate([dPm, dmemq, dPg, dgate], axis=1)
    g_wp = mm(h, dP, "tn", BF16, "grad_w_in")
    dh = mm(dP, wp, "nt", F32, "d_h", bn=1024, bk=768)
    grad_x, g_norm_in = in_norm_bwd(x2d, dh, dx2, norm_in)

    grads = dict(
        norm_in=g_norm_in, w_in=_unpad_w_in(g_wp), q_a_norm=g_q_a_norm, w_q_b=_unpad_w_q_b(g_wq),
        kv_a_norm=g_kv_a_norm, w_kv_b=_unperm_w_kv_b(g_wkv), gdn_conv=g_conv,
        gdn_a_log=g_ab[0:1, :N_HEADS], gdn_dt_bias=g_ab[1:2, :N_HEADS], gdn_norm=g_gdn_norm,
        mem_norm=g_mem_norm, w_mem_kv=g_w_mem_kv, w_out=g_w_out, norm_final=g_norm_final)
    return sq, grad_x.reshape(B, S, D), grads


def kernel(x, mem, positions, norm_in, w_in, q_a_norm, w_q_b, kv_a_norm, w_kv_b, gdn_conv, gdn_a_log, gdn_dt_bias, gdn_norm, mem_norm, w_mem_kv, w_out, norm_final, loss_target, m_norm_in, m_w_in, m_q_a_norm, m_w_q_b, m_kv_a_norm, m_w_kv_b, m_gdn_conv, m_gdn_a_log, m_gdn_dt_bias, m_gdn_norm, m_mem_norm, m_w_mem_kv, m_w_out, m_norm_final, v_norm_in, v_w_in, v_q_a_norm, v_w_q_b, v_kv_a_norm, v_w_kv_b, v_gdn_conv, v_gdn_a_log, v_gdn_dt_bias, v_gdn_norm, v_mem_norm, v_w_mem_kv, v_w_out, v_norm_final):
    B = x.shape[0]
    cx, cy, cc = lax.axis_index("x"), lax.axis_index("y"), lax.axis_index("c")
    chip = 2 * cx + cy

    big_names = ("w_in", "w_q_b", "w_kv_b", "w_mem_kv", "w_out")
    big = dict(w_in=w_in[0], w_q_b=w_q_b[0], w_kv_b=w_kv_b[0], w_mem_kv=w_mem_kv[0], w_out=w_out[0])
    gathered = allgather_chips([big[n].astype(BF16) for n in big_names], "allgather_weights")
    cat_axis = dict(w_in=1, w_q_b=1, w_kv_b=1, w_mem_kv=0, w_out=0)
    full = {}
    for n, a in zip(big_names, gathered):
        R, C = big[n].shape
        full[n] = a.reshape(N_CHIPS * R, C) if cat_axis[n] == 0 else jnp.transpose(a, (1, 0, 2)).reshape(R, N_CHIPS * C)
    conv_all = allgather_devices(_pack([gdn_conv[0]], 8), "allgather_conv")
    conv_shape = gdn_conv[0].shape
    conv_full = jnp.concatenate([_unpack(conv_all[2 * q], [conv_shape])[0] for q in range(N_CHIPS)], axis=1)

    sq, grad_x, g = local_step(x, mem, positions, loss_target, norm_in, full["w_in"], q_a_norm, full["w_q_b"], kv_a_norm,
                               full["w_kv_b"], conv_full, gdn_a_log, gdn_dt_bias, gdn_norm, mem_norm, full["w_mem_kv"],
                               full["w_out"], norm_final)
    loss = lax.psum(0.5 * jnp.sum(sq) / D_MODEL, ("x", "y", "c"))

    def by_chip(n):
        R, C = big[n].shape
        if cat_axis[n] == 0:
            return g[n].reshape(N_CHIPS, R, C)
        return jnp.transpose(g[n].reshape(R, N_CHIPS, C), (1, 0, 2))

    def row_half(a, which):
        n4, R, C = a.shape
        return lax.dynamic_index_in_dim(a.reshape(n4, 2, R // 2, C), which, axis=1, keepdims=False)

    parts = [by_chip(n) for n in big_names]
    mine = [row_half(a, cc) for a in parts]
    from_sibling = swap_sibling([row_half(a, 1 - cc) for a in parts], "rs_sibling_partial")
    chip_sums = add_pairs(mine, from_sibling, "rs_add_sibling")
    from_chips = exchange_chips(chip_sums, "rs_exchange_chips")
    own = lambda arrs: [lax.dynamic_index_in_dim(a, chip, axis=0, keepdims=False) for a in arrs]
    my_half = add_fives(own(mine), own(from_sibling), from_chips, "rs_add_chips")
    other_half = swap_sibling(my_half, "rs_sibling_final")
    g_big = {n: jnp.where(cc == 0, jnp.concatenate([a, b]), jnp.concatenate([b, a]))
             for n, a, b in zip(big_names, my_half, other_half)}

    small_names = ("norm_in", "q_a_norm", "kv_a_norm", "gdn_a_log", "gdn_dt_bias", "gdn_norm", "mem_norm", "norm_final")
    small = dict(norm_in=norm_in, q_a_norm=q_a_norm, kv_a_norm=kv_a_norm, gdn_a_log=gdn_a_log, gdn_dt_bias=gdn_dt_bias,
                 gdn_norm=gdn_norm, mem_norm=mem_norm, norm_final=norm_final)
    m_small = dict(norm_in=m_norm_in, q_a_norm=m_q_a_norm, kv_a_norm=m_kv_a_norm, gdn_a_log=m_gdn_a_log,
                   gdn_dt_bias=m_gdn_dt_bias, gdn_norm=m_gdn_norm, mem_norm=m_mem_norm, norm_final=m_norm_final)
    v_small = dict(norm_in=v_norm_in, q_a_norm=v_q_a_norm, kv_a_norm=v_kv_a_norm, gdn_a_log=v_gdn_a_log,
                   gdn_dt_bias=v_gdn_dt_bias, gdn_norm=v_gdn_norm, mem_norm=v_mem_norm, norm_final=v_norm_final)
    rows = lambda d: jnp.stack([jnp.pad(d[n].reshape(-1), (0, 1024 - d[n].size)) for n in small_names])
    conv_rows = GDN_CONV * GDN_QKV // 1024
    g_block = jnp.concatenate([rows(g), g["gdn_conv"].reshape(conv_rows, 1024), jnp.zeros((16 - 8 - conv_rows, 1024), F32)])
    g_block = sum_leading(allgather_devices(g_block, "allgather_small_grads"), "sum_small_grads")
    g_small_rows = g_block[:8]
    conv_cols = gdn_conv.shape[2]
    g_big["gdn_conv"] = lax.dynamic_slice_in_dim(g_block[8:8 + conv_rows].reshape(GDN_CONV, GDN_QKV), chip * conv_cols, conv_cols, axis=1)
    d_s, m_s, v_s = adamw(rows(small), g_small_rows, rows(m_small), rows(v_small), "adamw_small")
    unrow = lambda r: {n: r[i, :small[n].size].reshape(small[n].shape) for i, n in enumerate(small_names)}
    g_sm, d_sm, m_sm, v_sm = unrow(g_small_rows), unrow(d_s), unrow(m_s), unrow(v_s)

    owned = dict(big, gdn_conv=gdn_conv[0])
    m_big = dict(w_in=m_w_in, w_q_b=m_w_q_b, w_kv_b=m_w_kv_b, w_mem_kv=m_w_mem_kv, w_out=m_w_out, gdn_conv=m_gdn_conv)
    v_big = dict(w_in=v_w_in, w_q_b=v_w_q_b, w_kv_b=v_w_kv_b, w_mem_kv=v_w_mem_kv, w_out=v_w_out, gdn_conv=v_gdn_conv)
    g_out, d_out, m_out, v_out = dict(g_sm), dict(d_sm), dict(m_sm), dict(v_sm)
    for n in big_names + ("gdn_conv",):
        d_n, m_n, v_n = adamw(owned[n], g_big[n], m_big[n][0], v_big[n][0], "adamw_" + n)
        g_out[n], d_out[n], m_out[n], v_out[n] = g_big[n][None], d_n[None], m_n[None], v_n[None]

    order = ("norm_in", "w_in", "q_a_norm", "w_q_b", "kv_a_norm", "w_kv_b", "gdn_conv", "gdn_a_log", "gdn_dt_bias",
             "gdn_norm", "mem_norm", "w_mem_kv", "w_out", "norm_final")
    return (loss, grad_x, *[g_out[n] for n in order], *[d_out[n] for n in order], *[m_out[n] for n in order],
            *[v_out[n] for n in order])
```

```python
import functools
import math

import jax
import jax.numpy as jnp
import numpy as np
from jax import lax
from jax.experimental import pallas as pl
from jax.experimental.pallas import tpu as pltpu

F32 = jnp.float32
BF16 = jnp.bfloat16
BS = pl.BlockSpec

D_MODEL = 1024
N_HEADS = 4
MLA_NOPE, MLA_ROPE, MLA_V = 128, 64, 128
Q_LORA, KV_LORA = 384, 256
ROPE_THETA = 10000.0
GDN_DK = GDN_DV = 128
GDN_CONV = 4
CHUNK = 64
MEM_DH = 128
D_MIX = 1536
GDN_QKV = 1536
D_IN = 4296
EPS = 1e-6
ADAM_LR, ADAM_B1, ADAM_B2, ADAM_EPS, ADAM_WD, ADAM_STEP = 0.001, 0.9, 0.999, 1e-08, 0.01, 10

OFF_MLA = 0
OFF_MEMQ = 1024
OFF_GDN = 1536
OFF_GATE = 3072
N_PAD = 4608
HEAD_PAD = 256
MLA_SCALE = (MLA_NOPE + MLA_ROPE) ** -0.5
MEM_SCALE = MEM_DH ** -0.5
GDN_SCALE = GDN_DK ** -0.5
NEG = -1e30

NN = ((1,), (0,))
NT = ((1,), (1,))
TN = ((0,), (0,))


def _dot(a, b, dims):
    return lax.dot_general(a, b, (dims, ((), ())), preferred_element_type=F32)


def _bdot(spec, a, b, precision=None):
    return jnp.einsum(spec, a, b, preferred_element_type=F32, precision=precision)


def _arb(n):
    return pltpu.CompilerParams(dimension_semantics=("arbitrary",) * n)


def _sigmoid(x):
    return 1.0 / (1.0 + jnp.exp(-x))


def _softplus(z):
    return jnp.maximum(z, 0.0) + jnp.log(1.0 + jnp.exp(-jnp.abs(z)))


def _rope(t, cos_row, sin_row):
    return t * cos_row + pltpu.roll(t, 64, 1) * sin_row


def _rope_bwd(d, cos_row, sin_row):
    return d * cos_row + pltpu.roll(d * sin_row, 64, 1)


def rms_fwd(x, gain, name, tm=512):
    T, n = x.shape
    tm = min(tm, T)

    def body(x_ref, g_ref, o_ref):
        xv = x_ref[...]
        r = lax.rsqrt(jnp.mean(xv * xv, axis=-1, keepdims=True) + EPS)
        o_ref[...] = (xv * r * g_ref[...]).astype(BF16)

    return pl.pallas_call(
        body, name=name, grid=(T // tm,),
        in_specs=[BS((tm, n), lambda i: (i, 0)), BS((1, n), lambda i: (0, 0))],
        out_specs=BS((tm, n), lambda i: (i, 0)),
        out_shape=jax.ShapeDtypeStruct((T, n), BF16), compiler_params=_arb(1))(x, gain)


def mm(a, b, kind, out_dtype, name, bm=512, bn=512, bk=512):
    if kind == "nn":
        (M, K), (_, N) = a.shape, b.shape
    elif kind == "nt":
        (M, K), (N, _) = a.shape, b.shape
    else:
        (K, M), (_, N) = a.shape, b.shape
    bm, bn, bk = min(bm, M), min(bn, N), min(bk, K)
    assert M % bm == 0 and N % bn == 0 and K % bk == 0, (name, M, N, K)
    nk = K // bk
    a_spec = BS((bk, bm), lambda i, j, k: (k, i)) if kind == "tn" else BS((bm, bk), lambda i, j, k: (i, k))
    b_spec = BS((bn, bk), lambda i, j, k: (j, k)) if kind == "nt" else BS((bk, bn), lambda i, j, k: (k, j))
    dims = {"nn": NN, "nt": NT, "tn": TN}[kind]

    def body(a_ref, b_ref, o_ref, acc):
        k = pl.program_id(2)

        @pl.when(k == 0)
        def _():
            acc[...] = jnp.zeros_like(acc)

        acc[...] += _dot(a_ref[...].astype(BF16), b_ref[...].astype(BF16), dims)

        @pl.when(k == nk - 1)
        def _():
            o_ref[...] = acc[...].astype(out_dtype)

    return pl.pallas_call(
        body, name=name, grid=(M // bm, N // bn, nk),
        in_specs=[a_spec, b_spec], out_specs=BS((bm, bn), lambda i, j, k: (i, j)),
        out_shape=jax.ShapeDtypeStruct((M, N), out_dtype),
        scratch_shapes=[pltpu.VMEM((bm, bn), F32)], compiler_params=_arb(3))(a, b)


def rope_tables(pos_col, inv_row, sgn_row, msk_row, tm=512):
    T = pos_col.shape[0]
    tm = min(tm, T)

    def body(p_ref, inv_ref, sgn_ref, msk_ref, c_ref, s_ref):
        ang = p_ref[...].astype(F32) * inv_ref[...]
        c_ref[...] = jnp.cos(ang) * msk_ref[...]
        s_ref[...] = jnp.sin(ang) * sgn_ref[...]

    row = BS((1, 128), lambda i: (0, 0))
    return pl.pallas_call(
        body, name="rope_tables", grid=(T // tm,),
        in_specs=[BS((tm, 1), lambda i: (i, 0)), row, row, row],
        out_specs=[BS((tm, 128), lambda i: (i, 0))] * 2,
        out_shape=[jax.ShapeDtypeStruct((T, 128), F32)] * 2, compiler_params=_arb(1))(pos_col, inv_row, sgn_row, msk_row)


def mla_prep(P, gq, gkv, wq, wkv, cos_t, sin_t, tm=512):
    T = P.shape[0]
    tm = min(tm, T)

    def body(p_ref, gq_ref, gkv_ref, wq_ref, wkv_ref, c_ref, s_ref, q_ref, k_ref, v_ref, qn_ref, kvn_ref):
        p = p_ref[...]
        cq, ckv, kr = p[:, :Q_LORA], p[:, Q_LORA:Q_LORA + KV_LORA], p[:, 640:768]
        qn = (cq * lax.rsqrt(jnp.mean(cq * cq, axis=-1, keepdims=True) + EPS) * gq_ref[...]).astype(BF16)
        kvn = (ckv * lax.rsqrt(jnp.mean(ckv * ckv, axis=-1, keepdims=True) + EPS) * gkv_ref[...]).astype(BF16)
        qn_ref[...] = qn
        kvn_ref[...] = kvn
        q = _dot(qn, wq_ref[...], NT)
        kv = _dot(kvn, wkv_ref[...], NN)
        cos_row, sin_row = c_ref[...], s_ref[...]
        krr = _rope(kr, cos_row, sin_row).astype(BF16)
        for h in range(N_HEADS):
            lo = h * HEAD_PAD
            q_ref[:, lo:lo + 128] = (q[:, lo:lo + 128] * MLA_SCALE).astype(BF16)
            q_ref[:, lo + 128:lo + 256] = (_rope(q[:, lo + 128:lo + 256], cos_row, sin_row) * MLA_SCALE).astype(BF16)
            k_ref[:, lo:lo + 128] = kv[:, h * 128:(h + 1) * 128].astype(BF16)
            k_ref[:, lo + 128:lo + 256] = krr
        v_ref[...] = kv[:, 512:].astype(BF16)

    full = lambda r, c: BS((r, c), lambda i: (0, 0))
    rowb = lambda c: BS((tm, c), lambda i: (i, 0))
    return pl.pallas_call(
        body, name="mla_prep", grid=(T // tm,),
        in_specs=[rowb(1024), full(1, Q_LORA), full(1, KV_LORA), full(1024, Q_LORA), full(KV_LORA, 1024), rowb(128), rowb(128)],
        out_specs=[rowb(1024), rowb(1024), rowb(512), rowb(Q_LORA), rowb(KV_LORA)],
        out_shape=[jax.ShapeDtypeStruct((T, 1024), BF16), jax.ShapeDtypeStruct((T, 1024), BF16),
                   jax.ShapeDtypeStruct((T, 512), BF16), jax.ShapeDtypeStruct((T, Q_LORA), BF16),
                   jax.ShapeDtypeStruct((T, KV_LORA), BF16)],
        compiler_params=_arb(1))(P, gq, gkv, wq, wkv, cos_t, sin_t)


def mla_attn_fwd(Q, K, V, B, S, tq=512):
    T = B * S
    tq = min(tq, S)
    nq = S // tq

    def body(q_ref, k_ref, v_ref, o_ref, lse_ref, m_s, l_s, acc_s):
        i = pl.program_id(2)
        q = q_ref[...]
        m_s[...] = jnp.full_like(m_s, NEG)
        l_s[...] = jnp.zeros_like(l_s)
        acc_s[...] = jnp.zeros_like(acc_s)

        def blk(j, masked):
            rows = pl.ds(pl.multiple_of(j * tq, tq), tq)
            s = _dot(q, k_ref[rows, :], NT)
            if masked:
                r = lax.broadcasted_iota(jnp.int32, (tq, tq), 0)
                c = lax.broadcasted_iota(jnp.int32, (tq, tq), 1)
                s = jnp.where(r >= c, s, NEG)
            m_prev = m_s[...]
            m_new = jnp.maximum(m_prev, jnp.max(s, axis=1, keepdims=True))
            p = jnp.exp(s - m_new)
            alpha = jnp.exp(m_prev - m_new)
            l_s[...] = alpha * l_s[...] + jnp.sum(p, axis=1, keepdims=True)
            acc_s[...] = alpha * acc_s[...] + _dot(p.astype(BF16), v_ref[rows, :], NN)
            m_s[...] = m_new

        def loop(j, c):
            blk(j, False)
            return c

        lax.fori_loop(0, i, loop, 0)
        blk(i, True)
        o_ref[...] = acc_s[...] / l_s[...]
        lse_ref[...] = m_s[...] + jnp.log(l_s[...])

    return pl.pallas_call(
        body, name="mla_attn_fwd", grid=(B, N_HEADS, nq),
        in_specs=[BS((tq, HEAD_PAD), lambda b, h, i: (b * nq + i, h)),
                  BS((S, HEAD_PAD), lambda b, h, i: (b, h)),
                  BS((S, 128), lambda b, h, i: (b, h))],
        out_specs=[BS((tq, 128), lambda b, h, i: (b * nq + i, h)),
                   BS((None, tq, 1), lambda b, h, i: (h, b * nq + i, 0))],
        out_shape=[jax.ShapeDtypeStruct((T, 512), F32), jax.ShapeDtypeStruct((N_HEADS, T, 1), F32)],
        scratch_shapes=[pltpu.VMEM((tq, 1), F32), pltpu.VMEM((tq, 1), F32), pltpu.VMEM((tq, 128), F32)],
        compiler_params=_arb(3))(Q, K, V)


def mla_attn_bwd(Q, K, V, O, dO, LSE, B, S, tq=512):
    T = B * S
    tq = min(tq, S)
    nq = S // tq

    def body(q_ref, k_ref, v_ref, o_ref, do_ref, lse_ref, dq_ref, dk_ref, dv_ref, delta_s, dk_s, dv_s):
        j = pl.program_id(2)

        @pl.when(j == 0)
        def _():
            dq_ref[...] = jnp.zeros_like(dq_ref)
            delta_s[...] = jnp.sum(do_ref[...] * o_ref[...], axis=1, keepdims=True)

        dk_s[...] = jnp.zeros_like(dk_s)
        dv_s[...] = jnp.zeros_like(dv_s)
        k = k_ref[...]
        v = v_ref[...]

        def step(i, c):
            rows = pl.ds(pl.multiple_of(i * tq, tq), tq)
            q = q_ref[rows, :]
            do = do_ref[rows, :].astype(BF16)
            s = _dot(q, k, NT)
            r = i * tq + lax.broadcasted_iota(jnp.int32, (tq, tq), 0)
            cc = j * tq + lax.broadcasted_iota(jnp.int32, (tq, tq), 1)
            p = jnp.where(r >= cc, jnp.exp(s - lse_ref[rows, :]), 0.0)
            dv_s[...] += _dot(p.astype(BF16), do, TN)
            dp = _dot(do, v, NT)
            ds = (p * (dp - delta_s[rows, :])).astype(BF16)
            dk_s[...] += _dot(ds, q, TN)
            dq_ref[rows, :] += _dot(ds, k, NN)
            return c

        lax.fori_loop(j, nq, step, 0)
        dk_ref[...] = dk_s[...]
        dv_ref[...] = dv_s[...]

    seq = lambda c: BS((S, c), lambda b, h, j: (b, h))
    return pl.pallas_call(
        body, name="mla_attn_bwd", grid=(B, N_HEADS, nq),
        in_specs=[seq(HEAD_PAD), BS((tq, HEAD_PAD), lambda b, h, j: (b * nq + j, h)),
                  BS((tq, 128), lambda b, h, j: (b * nq + j, h)), seq(128), seq(128),
                  BS((None, S, 1), lambda b, h, j: (h, b, 0))],
        out_specs=[seq(HEAD_PAD), BS((tq, HEAD_PAD), lambda b, h, j: (b * nq + j, h)),
                   BS((tq, 128), lambda b, h, j: (b * nq + j, h))],
        out_shape=[jax.ShapeDtypeStruct((T, 1024), F32), jax.ShapeDtypeStruct((T, 1024), F32),
                   jax.ShapeDtypeStruct((T, 512), F32)],
        scratch_shapes=[pltpu.VMEM((S, 1), F32), pltpu.VMEM((tq, HEAD_PAD), F32), pltpu.VMEM((tq, 128), F32)],
        compiler_params=_arb(3))(Q, K, V, O, dO, LSE)


def mla_post_bwd(dQ, dK, dV, cos_t, sin_t, tm=512):
    T = dQ.shape[0]
    tm = min(tm, T)

    def body(dq_ref, dk_ref, dv_ref, c_ref, s_ref, ql_ref, kvl_ref, kr_ref):
        cos_row, sin_row = c_ref[...], s_ref[...]
        kr = jnp.zeros((tm, 128), F32)
        for h in range(N_HEADS):
            lo = h * HEAD_PAD
            ql_ref[:, lo:lo + 128] = (dq_ref[:, lo:lo + 128] * MLA_SCALE).astype(BF16)
            ql_ref[:, lo + 128:lo + 256] = (_rope_bwd(dq_ref[:, lo + 128:lo + 256], cos_row, sin_row) * MLA_SCALE).astype(BF16)
            kvl_ref[:, h * 128:(h + 1) * 128] = dk_ref[:, lo:lo + 128].astype(BF16)
            kr = kr + dk_ref[:, lo + 128:lo + 256]
        kvl_ref[:, 512:] = dv_ref[...].astype(BF16)
        kr_ref[...] = _rope_bwd(kr, cos_row, sin_row)

    rowb = lambda c: BS((tm, c), lambda i: (i, 0))
    return pl.pallas_call(
        body, name="mla_post_bwd", grid=(T // tm,),
        in_specs=[rowb(1024), rowb(1024), rowb(512), rowb(128), rowb(128)],
        out_specs=[rowb(1024), rowb(1024), rowb(128)],
        out_shape=[jax.ShapeDtypeStruct((T, 1024), BF16), jax.ShapeDtypeStruct((T, 1024), BF16),
                   jax.ShapeDtypeStruct((T, 128), F32)],
        compiler_params=_arb(1))(dQ, dK, dV, cos_t, sin_t)


def mla_norm_bwd(P, dqn, dkvn, dkr, dab, gq, gkv, tm=512):
    T = P.shape[0]
    tm = min(tm, T)

    def norm_bwd(x, dy, g):
        r = lax.rsqrt(jnp.mean(x * x, axis=-1, keepdims=True) + EPS)
        xh = x * r
        dxh = dy * g
        return r * (dxh - xh * jnp.mean(dxh * xh, axis=-1, keepdims=True)), jnp.sum(dy * xh, axis=0, keepdims=True)

    def body(p_ref, dqn_ref, dkvn_ref, dkr_ref, dab_ref, gq_ref, gkv_ref, o_ref, aq_ref, akv_ref):
        @pl.when(pl.program_id(0) == 0)
        def _():
            aq_ref[...] = jnp.zeros_like(aq_ref)
            akv_ref[...] = jnp.zeros_like(akv_ref)

        dcq, ggq = norm_bwd(p_ref[:, :Q_LORA], dqn_ref[...], gq_ref[...])
        dckv, ggkv = norm_bwd(p_ref[:, Q_LORA:640], dkvn_ref[...], gkv_ref[...])
        aq_ref[...] += ggq
        akv_ref[...] += ggkv
        o_ref[:, :Q_LORA] = dcq.astype(BF16)
        o_ref[:, Q_LORA:640] = dckv.astype(BF16)
        o_ref[:, 640:768] = dkr_ref[...].astype(BF16)
        o_ref[:, 768:896] = dab_ref[...]
        o_ref[:, 896:1024] = jnp.zeros((tm, 128), BF16)

    rowb = lambda c: BS((tm, c), lambda i: (i, 0))
    full = lambda c: BS((1, c), lambda i: (0, 0))
    return pl.pallas_call(
        body, name="mla_norm_bwd", grid=(T // tm,),
        in_specs=[rowb(1024), rowb(Q_LORA), rowb(KV_LORA), rowb(128), rowb(128), full(Q_LORA), full(KV_LORA)],
        out_specs=[rowb(1024), full(Q_LORA), full(KV_LORA)],
        out_shape=[jax.ShapeDtypeStruct((T, 1024), BF16), jax.ShapeDtypeStruct((1, Q_LORA), F32),
                   jax.ShapeDtypeStruct((1, KV_LORA), F32)],
        compiler_params=_arb(1))(P, dqn, dkvn, dkr, dab, gq, gkv)


def _mem_probs(qh, kh):
    s = _dot(qh, kh, NT) * MEM_SCALE
    p = jnp.exp(s - jnp.max(s, axis=1, keepdims=True))
    return p / jnp.sum(p, axis=1, keepdims=True)


def mem_attn_fwd(P, MKV, B, S, M, tq=512):
    T = B * S
    tq = min(tq, S)
    nq = S // tq

    def body(q_ref, kv_ref, o_ref):
        for h in range(N_HEADS):
            sl = slice(h * 128, (h + 1) * 128)
            p = _mem_probs(q_ref[:, sl].astype(BF16), kv_ref[:, sl])
            o_ref[:, sl] = _dot(p.astype(BF16), kv_ref[:, 512 + h * 128:512 + (h + 1) * 128], NN)

    return pl.pallas_call(
        body, name="mem_attn_fwd", grid=(B, nq),
        in_specs=[BS((tq, 512), lambda b, i: (b * nq + i, OFF_MEMQ // 512)), BS((M, 1024), lambda b, i: (b, 0))],
        out_specs=BS((tq, 512), lambda b, i: (b * nq + i, 0)),
        out_shape=jax.ShapeDtypeStruct((T, 512), F32), compiler_params=_arb(2))(P, MKV)


def mem_attn_bwd(P, MKV, dO, B, S, M, tq=512):
    T = B * S
    tq = min(tq, S)
    nq = S // tq

    def body(q_ref, kv_ref, do_ref, dq_ref, dkv_ref):
        @pl.when(pl.program_id(1) == 0)
        def _():
            dkv_ref[...] = jnp.zeros_like(dkv_ref)

        for h in range(N_HEADS):
            sl = slice(h * 128, (h + 1) * 128)
            sv = slice(512 + h * 128, 512 + (h + 1) * 128)
            qh = q_ref[:, sl].astype(BF16)
            kh = kv_ref[:, sl]
            do = do_ref[:, sl].astype(BF16)
            p = _mem_probs(qh, kh)
            dkv_ref[:, sv] += _dot(p.astype(BF16), do, TN)
            dp = _dot(do, kv_ref[:, sv], NT)
            ds = (p * (dp - jnp.sum(dp * p, axis=1, keepdims=True)) * MEM_SCALE).astype(BF16)
            dq_ref[:, sl] = _dot(ds, kh, NN).astype(BF16)
            dkv_ref[:, sl] += _dot(ds, qh, TN)

    return pl.pallas_call(
        body, name="mem_attn_bwd", grid=(B, nq),
        in_specs=[BS((tq, 512), lambda b, i: (b * nq + i, OFF_MEMQ // 512)), BS((M, 1024), lambda b, i: (b, 0)),
                  BS((tq, 512), lambda b, i: (b * nq + i, 0))],
        out_specs=[BS((tq, 512), lambda b, i: (b * nq + i, 0)), BS((M, 1024), lambda b, i: (b, 0))],
        out_shape=[jax.ShapeDtypeStruct((T, 512), BF16), jax.ShapeDtypeStruct((B * M, 1024), F32)],
        compiler_params=_arb(2))(P, MKV, dO)


def gain_grad(x, dy, name, tm=256):
    T, n = x.shape
    tm = min(tm, T)

    def body(x_ref, dy_ref, o_ref):
        @pl.when(pl.program_id(0) == 0)
        def _():
            o_ref[...] = jnp.zeros_like(o_ref)

        xv = x_ref[...]
        xh = xv * lax.rsqrt(jnp.mean(xv * xv, axis=-1, keepdims=True) + EPS)
        o_ref[...] += jnp.sum(dy_ref[...] * xh, axis=0, keepdims=True)

    return pl.pallas_call(
        body, name=name, grid=(T // tm,),
        in_specs=[BS((tm, n), lambda i: (i, 0))] * 2, out_specs=BS((1, n), lambda i: (0, 0)),
        out_shape=jax.ShapeDtypeStruct((1, n), F32), compiler_params=_arb(1))(x, dy)


def _conv_silu(x, w, t):
    y = x * w[3:4, :]
    for s in range(1, GDN_CONV):
        y = y + jnp.where(t >= s, pltpu.roll(x, s, 0), 0.0) * w[3 - s:4 - s, :]
    return y, _sigmoid(y)


def gdn_prep_fwd(P, conv_w, B, S):
    T = B * S

    def body(x_ref, w_ref, o_ref):
        kind = pl.program_id(1)
        t = lax.broadcasted_iota(jnp.int32, (S, 1), 0)
        y, sg = _conv_silu(x_ref[...], w_ref[...], t)
        a = y * sg
        scale = jnp.where(kind == 0, GDN_SCALE, 1.0).astype(F32)
        for h in range(N_HEADS):
            sl = slice(h * 128, (h + 1) * 128)
            seg = a[:, sl]
            n = lax.rsqrt(jnp.sum(seg * seg, axis=-1, keepdims=True) + EPS)
            o_ref[:, sl] = jnp.where(kind < 2, seg * (n * scale), seg)

    return pl.pallas_call(
        body, name="gdn_prep_fwd", grid=(B, 3),
        in_specs=[BS((S, 512), lambda b, k: (b, OFF_GDN // 512 + k)), BS((GDN_CONV, 512), lambda b, k: (0, k))],
        out_specs=BS((S, 512), lambda b, k: (b, k)),
        out_shape=jax.ShapeDtypeStruct((T, GDN_QKV), F32), compiler_params=_arb(2))(P, conv_w)


def gdn_prep_bwd(P, dqkv, conv_w, B, S):
    T = B * S

    def body(x_ref, d_ref, w_ref, o_ref, gw_ref):
        kind = pl.program_id(0)

        @pl.when(pl.program_id(1) == 0)
        def _():
            gw_ref[...] = jnp.zeros_like(gw_ref)

        t = lax.broadcasted_iota(jnp.int32, (S, 1), 0)
        x = x_ref[...]
        w = w_ref[...]
        y, sg = _conv_silu(x, w, t)
        a = y * sg
        scale = jnp.where(kind == 0, GDN_SCALE, 1.0).astype(F32)
        das = []
        for h in range(N_HEADS):
            sl = slice(h * 128, (h + 1) * 128)
            seg, dseg = a[:, sl], d_ref[:, sl]
            n = lax.rsqrt(jnp.sum(seg * seg, axis=-1, keepdims=True) + EPS)
            dn = scale * (n * dseg - seg * (n * n * n) * jnp.sum(dseg * seg, axis=-1, keepdims=True))
            das.append(jnp.where(kind < 2, dn, dseg))
        dy = jnp.concatenate(das, axis=1) * (sg * (1.0 + y * (1.0 - sg)))
        dx = dy * w[3:4, :]
        gw_ref[3:4, :] += jnp.sum(dy * x, axis=0, keepdims=True)
        for s in range(1, GDN_CONV):
            dx = dx + jnp.where(t + s < S, pltpu.roll(dy, S - s, 0), 0.0) * w[3 - s:4 - s, :]
            gw_ref[3 - s:4 - s, :] += jnp.sum(dy * jnp.where(t >= s, pltpu.roll(x, s, 0), 0.0), axis=0, keepdims=True)
        o_ref[...] = dx.astype(BF16)

    return pl.pallas_call(
        body, name="gdn_prep_bwd", grid=(3, B),
        in_specs=[BS((S, 512), lambda k, b: (b, OFF_GDN // 512 + k)), BS((S, 512), lambda k, b: (b, k)),
                  BS((GDN_CONV, 512), lambda k, b: (0, k))],
        out_specs=[BS((S, 512), lambda k, b: (b, k)), BS((GDN_CONV, 512), lambda k, b: (0, k))],
        out_shape=[jax.ShapeDtypeStruct((T, GDN_QKV), BF16), jax.ShapeDtypeStruct((GDN_CONV, GDN_QKV), F32)],
        compiler_params=_arb(2))(P, dqkv, conv_w)


def _chunk_row(n_rows):
    return lax.broadcasted_iota(jnp.int32, (n_rows, 1), 0) % CHUNK


def gdn_gate_fwd(P, alog_row, dt_row, B, S):
    T = B * S

    def body(x_ref, al_ref, dt_ref, o_ref):
        x = x_ref[...]
        lane = lax.broadcasted_iota(jnp.int32, (1, 128), 1)
        g = jnp.where(lane < 4, -jnp.exp(al_ref[...]) * _softplus(x + dt_ref[...]), 0.0)
        t = _chunk_row(S)
        for s in (1, 2, 4, 8, 16, 32):
            g = g + jnp.where(t >= s, pltpu.roll(g, s, 0), 0.0)
        o_ref[...] = jnp.where(lane < 4, g, jnp.where(lane < 8, _sigmoid(x), 0.0))

    row = BS((1, 128), lambda b: (0, 0))
    return pl.pallas_call(
        body, name="gdn_gate_fwd", grid=(B,),
        in_specs=[BS((S, 128), lambda b: (b, 768 // 128)), row, row], out_specs=BS((S, 128), lambda b: (b, 0)),
        out_shape=jax.ShapeDtypeStruct((T, 128), F32), compiler_params=_arb(1))(P, alog_row, dt_row)


def gdn_gate_bwd(P, dGB, alog_row, dt_row, B, S):
    T = B * S

    def body(x_ref, d_ref, al_ref, dt_ref, o_ref, acc_ref):
        @pl.when(pl.program_id(0) == 0)
        def _():
            acc_ref[...] = jnp.zeros_like(acc_ref)

        x, d = x_ref[...], d_ref[...]
        lane = lax.broadcasted_iota(jnp.int32, (1, 128), 1)
        z = x + dt_ref[...]
        coef = -jnp.exp(al_ref[...])
        g = coef * _softplus(z)
        da = jnp.where(lane < 4, d * coef * _sigmoid(z), 0.0)
        beta = _sigmoid(x)
        o_ref[...] = jnp.where(lane < 4, da, jnp.where(lane < 8, d * beta * (1.0 - beta), 0.0)).astype(BF16)
        acc_ref[0:1, :] += jnp.sum(jnp.where(lane < 4, d * g, 0.0), axis=0, keepdims=True)
        acc_ref[1:2, :] += jnp.sum(da, axis=0, keepdims=True)

    row = BS((1, 128), lambda b: (0, 0))
    return pl.pallas_call(
        body, name="gdn_gate_bwd", grid=(B,),
        in_specs=[BS((S, 128), lambda b: (b, 768 // 128)), BS((S, 128), lambda b: (b, 0)), row, row],
        out_specs=[BS((S, 128), lambda b: (b, 0)), BS((8, 128), lambda b: (0, 0))],
        out_shape=[jax.ShapeDtypeStruct((T, 128), BF16), jax.ShapeDtypeStruct((8, 128), F32)],
        compiler_params=_arb(1))(P, dGB, alog_row, dt_row)


def _chunk_masks(nc):
    r = lax.broadcasted_iota(jnp.int32, (nc, CHUNK, CHUNK), 1)
    c = lax.broadcasted_iota(jnp.int32, (nc, CHUNK, CHUNK), 2)
    return r >= c, r > c


def _chunk_local(q, k, gc, gr, beta, incl, strict):
    decay = jnp.exp(jnp.where(incl, gc - gr, NEG))
    kb = k * beta
    kbf = k.astype(BF16)
    m_kk = _bdot("gcd,gjd->gcj", kb.astype(BF16), kbf)
    l_mat = jnp.where(strict, m_kk * decay, 0.0)
    a_mat = _bdot("gcd,gjd->gcj", q.astype(BF16), kbf) * decay
    return decay, kb, l_mat, a_mat


def gdn_chunk_fwd(qkv, GB, Grow, B, S, nc=8):
    T = B * S
    N = S // CHUNK
    nc = min(nc, N)
    nb = N // nc
    R = nc * CHUNK
    hi = lax.Precision.HIGHEST

    def body(q_ref, k_ref, v_ref, gb_ref, gr_ref, u_ref, w_ref, t_ref, a_ref):
        incl, strict = _chunk_masks(nc)
        eye = (lax.broadcasted_iota(jnp.int32, (nc, CHUNK, CHUNK), 1)
               == lax.broadcasted_iota(jnp.int32, (nc, CHUNK, CHUNK), 2)).astype(F32)
        for h in range(N_HEADS):
            sl = slice(h * 128, (h + 1) * 128)
            q = q_ref[:, sl].reshape(nc, CHUNK, 128)
            k = k_ref[:, sl].reshape(nc, CHUNK, 128)
            v = v_ref[:, sl].reshape(nc, CHUNK, 128)
            gc = gb_ref[:, h:h + 1].reshape(nc, CHUNK, 1)
            beta = gb_ref[:, 4 + h:5 + h].reshape(nc, CHUNK, 1)
            gr = gr_ref[h][:, None, :]
            _, kb, l_mat, a_mat = _chunk_local(q, k, gc, gr, beta, incl, strict)
            pw = -l_mat
            tinv = eye + pw
            for _ in range(5):
                pw = _bdot("gij,gjk->gik", pw, pw, hi)
                tinv = tinv + _bdot("gij,gjk->gik", tinv, pw, hi)
            tb = tinv.astype(BF16)
            u = _bdot("gcj,gjv->gcv", tb, (v * beta).astype(BF16))
            w = _bdot("gcj,gjk->gck", tb, (kb * jnp.exp(gc)).astype(BF16))
            u_ref[:, sl] = u.reshape(R, 128)
            w_ref[:, sl] = w.reshape(R, 128)
            t_ref[h] = tinv
            a_ref[h] = a_mat

    rowb = lambda c, j: BS((R, c), lambda b, n: (b * nb + n, j))
    mat = BS((None, N_HEADS, nc, CHUNK, CHUNK), lambda b, n: (b, 0, n, 0, 0))
    return pl.pallas_call(
        body, name="gdn_chunk_fwd", grid=(B, nb),
        in_specs=[rowb(512, 0), rowb(512, 1), rowb(512, 2), rowb(128, 0),
                  BS((None, N_HEADS, nc, CHUNK), lambda b, n: (b, 0, n, 0))],
        out_specs=[rowb(512, 0), rowb(512, 0), mat, mat],
        out_shape=[jax.ShapeDtypeStruct((T, 512), F32), jax.ShapeDtypeStruct((T, 512), F32),
                   jax.ShapeDtypeStruct((B, N_HEADS, N, CHUNK, CHUNK), F32),
                   jax.ShapeDtypeStruct((B, N_HEADS, N, CHUNK, CHUNK), F32)],
        compiler_params=_arb(2))(qkv, qkv, qkv, GB, Grow)


def gdn_scan_fwd(qkv3, U3, W3, GB3, A, B, S):
    N = S // CHUNK

    def body(q_ref, k_ref, u_ref, w_ref, gb_ref, a_ref, o_ref, vn_ref, st_ref, s_s):
        @pl.when(pl.program_id(0) == 0)
        def _():
            s_s[...] = jnp.zeros_like(s_s)

        for b in range(B):
            for h in range(N_HEADS):
                sl = slice(h * 128, (h + 1) * 128)
                st = s_s[b, h]
                st_ref[b, h] = st
                stb = st.astype(BF16)
                g = gb_ref[b, :, h:h + 1]
                gl = g[CHUNK - 1:CHUNK, :]
                vn = u_ref[b, :, sl] - _dot(w_ref[b, :, sl].astype(BF16), stb, NN)
                vnb = vn.astype(BF16)
                o = _dot((q_ref[b, :, sl] * jnp.exp(g)).astype(BF16), stb, NN) + _dot(a_ref[b, h].astype(BF16), vnb, NN)
                vn_ref[b, :, sl] = vn
                o_ref[b, :, sl] = o
                s_s[b, h] = st * jnp.exp(gl) + _dot((k_ref[b, :, sl] * jnp.exp(gl - g)).astype(BF16), vnb, TN)

    tok = lambda c, j: BS((B, CHUNK, c), lambda n: (0, n, j))
    return pl.pallas_call(
        body, name="gdn_scan_fwd", grid=(N,),
        in_specs=[tok(512, 0), tok(512, 1), tok(512, 0), tok(512, 0), tok(128, 0),
                  BS((B, N_HEADS, None, CHUNK, CHUNK), lambda n: (0, 0, n, 0, 0))],
        out_specs=[tok(512, 0), tok(512, 0), BS((B, N_HEADS, None, 128, 128), lambda n: (0, 0, n, 0, 0))],
        out_shape=[jax.ShapeDtypeStruct((B, S, 512), F32), jax.ShapeDtypeStruct((B, S, 512), F32),
                   jax.ShapeDtypeStruct((B, N_HEADS, N, 128, 128), F32)],
        scratch_shapes=[pltpu.VMEM((B, N_HEADS, 128, 128), F32)],
        compiler_params=_arb(1))(qkv3, qkv3, U3, W3, GB3, A)


def gdn_scan_bwd(dO3, qkv3, W3, Vn3, GB3, A, St, B, S):
    N = S // CHUNK

    def body(do_ref, q_ref, k_ref, w_ref, vn_ref, gb_ref, a_ref, st_ref,
             du_ref, dw_ref, dq_ref, dk_ref, da_ref, dg_ref, ds_s):
        @pl.when(pl.program_id(0) == 0)
        def _():
            ds_s[...] = jnp.zeros_like(ds_s)

        lane = lax.broadcasted_iota(jnp.int32, (1, 128), 1)
        last = lax.broadcasted_iota(jnp.int32, (CHUNK, 1), 0) == CHUNK - 1
        for b in range(B):
            dg_all = jnp.zeros((CHUNK, 128), F32)
            for h in range(N_HEADS):
                sl = slice(h * 128, (h + 1) * 128)
                st = st_ref[b, h]
                stb = st.astype(BF16)
                dsn = ds_s[b, h]
                dsnb = dsn.astype(BF16)
                g = gb_ref[b, :, h:h + 1]
                gl = g[CHUNK - 1:CHUNK, :]
                egl = jnp.exp(gl)
                ekd = jnp.exp(gl - g)
                eg = jnp.exp(g)
                q, k = q_ref[b, :, sl], k_ref[b, :, sl]
                kd = k * ekd
                qg = q * eg
                do = do_ref[b, :, sl].astype(BF16)
                vnb = vn_ref[b, :, sl].astype(BF16)
                dvn = _dot(a_ref[b, h].astype(BF16), do, TN) + _dot(kd.astype(BF16), dsnb, NN)
                dvnb = dvn.astype(BF16)
                da_ref[b, h] = _dot(do, vnb, NT)
                dqg = _dot(do, stb, NT)
                dkd = _dot(vnb, dsnb, NT)
                ds_s[b, h] = (_dot(qg.astype(BF16), do, TN) + egl * dsn - _dot(w_ref[b, :, sl].astype(BF16), dvnb, TN))
                du_ref[b, :, sl] = dvn
                dw_ref[b, :, sl] = -_dot(dvnb, stb, NT)
                dq_ref[b, :, sl] = dqg * eg
                dk_ref[b, :, sl] = dkd * ekd
                ddel = jnp.sum(dkd * kd, axis=1, keepdims=True)
                dgl = jnp.sum(ddel, axis=0, keepdims=True) + jnp.sum(jnp.sum(st * dsn, axis=1, keepdims=True), axis=0, keepdims=True) * egl
                col = jnp.sum(dqg * qg, axis=1, keepdims=True) - ddel + jnp.where(last, dgl, 0.0)
                dg_all = jnp.where(lane == h, col, dg_all)
            dg_ref[b] = dg_all

    tok = lambda c, j: BS((B, CHUNK, c), lambda n: (0, N - 1 - n, j))
    mat = lambda d: BS((B, N_HEADS, None, d, d), lambda n: (0, 0, N - 1 - n, 0, 0))
    return pl.pallas_call(
        body, name="gdn_scan_bwd", grid=(N,),
        in_specs=[tok(512, 0), tok(512, 0), tok(512, 1), tok(512, 0), tok(512, 0), tok(128, 0), mat(CHUNK), mat(128)],
        out_specs=[tok(512, 0), tok(512, 0), tok(512, 0), tok(512, 0), mat(CHUNK), tok(128, 0)],
        out_shape=[jax.ShapeDtypeStruct((B, S, 512), F32)] * 4
        + [jax.ShapeDtypeStruct((B, N_HEADS, N, CHUNK, CHUNK), F32), jax.ShapeDtypeStruct((B, S, 128), F32)],
        scratch_shapes=[pltpu.VMEM((B, N_HEADS, 128, 128), F32)],
        compiler_params=_arb(1))(dO3, qkv3, qkv3, W3, Vn3, GB3, A, St)


def gdn_chunk_bwd(qkv, GB, Grow, Tinv, dA, dU, dW, dQ1, dK1, dG1, B, S, nc=8):
    T = B * S
    N = S // CHUNK
    nc = min(nc, N)
    nb = N // nc
    R = nc * CHUNK

    def body(q_ref, k_ref, v_ref, gb_ref, gr_ref, t_ref, da_ref, du_ref, dw_ref, dq1_ref, dk1_ref, dg1_ref, o_ref, dgb_ref):
        incl, strict = _chunk_masks(nc)
        lane = lax.broadcasted_iota(jnp.int32, (1, 128), 1)
        dg_all = dg1_ref[...]
        db_all = jnp.zeros((R, 128), F32)
        for h in range(N_HEADS):
            sl = slice(h * 128, (h + 1) * 128)
            q = q_ref[:, sl].reshape(nc, CHUNK, 128)
            k = k_ref[:, sl].reshape(nc, CHUNK, 128)
            v = v_ref[:, sl].reshape(nc, CHUNK, 128)
            gc = gb_ref[:, h:h + 1].reshape(nc, CHUNK, 1)
            beta = gb_ref[:, 4 + h:5 + h].reshape(nc, CHUNK, 1)
            gr = gr_ref[h][:, None, :]
            decay, kb, l_mat, a_mat = _chunk_local(q, k, gc, gr, beta, incl, strict)
            eg = jnp.exp(gc)
            kbg = kb * eg
            vb = v * beta
            tb = t_ref[h].astype(BF16)
            du = du_ref[:, sl].reshape(nc, CHUNK, 128).astype(BF16)
            dw = dw_ref[:, sl].reshape(nc, CHUNK, 128).astype(BF16)
            dvb = _bdot("gcj,gcv->gjv", tb, du)
            dkbg = _bdot("gcj,gck->gjk", tb, dw)
            dt = _bdot("gcv,gjv->gcj", du, vb.astype(BF16)) + _bdot("gck,gjk->gcj", dw, kbg.astype(BF16))
            tmp = _bdot("gac,gab->gcb", tb, dt.astype(BF16))
            dl = jnp.where(strict, -_bdot("gcb,gdb->gcd", tmp.astype(BF16), tb), 0.0)
            da = da_ref[h]
            dm = (dl * decay).astype(BF16)
            dqk = (da * decay).astype(BF16)
            kbf = k.astype(BF16)
            dkb = _bdot("gcj,gjd->gcd", dm, kbf) + dkbg * eg
            dk = (_bdot("gcj,gcd->gjd", dm, kb.astype(BF16)) + _bdot("gcj,gcd->gjd", dqk, q.astype(BF16))
                  + dk1_ref[:, sl].reshape(nc, CHUNK, 128) + dkb * beta)
            dq = _bdot("gcj,gjd->gcd", dqk, kbf) + dq1_ref[:, sl].reshape(nc, CHUNK, 128)
            e = dl * l_mat + da * a_mat
            dgc = (jnp.sum(e, axis=2, keepdims=True) - jnp.sum(jnp.swapaxes(e, 1, 2), axis=2, keepdims=True)
                   + jnp.sum(dkbg * kbg, axis=2, keepdims=True))
            dbeta = jnp.sum(dkb * k, axis=2, keepdims=True) + jnp.sum(dvb * v, axis=2, keepdims=True)
            o_ref[:, sl] = dq.reshape(R, 128)
            o_ref[:, 512 + h * 128:512 + (h + 1) * 128] = dk.reshape(R, 128)
            o_ref[:, 1024 + h * 128:1024 + (h + 1) * 128] = (dvb * beta).reshape(R, 128)
            dg_all = dg_all + jnp.where(lane == h, dgc.reshape(R, 1), 0.0)
            db_all = jnp.where(lane == 4 + h, dbeta.reshape(R, 1), db_all)
        t = _chunk_row(R)
        for s in (1, 2, 4, 8, 16, 32):
            dg_all = dg_all + jnp.where(t + s < CHUNK, pltpu.roll(dg_all, R - s, 0), 0.0)
        dgb_ref[...] = jnp.where(lane < 4, dg_all, db_all)

    rowb = lambda c, j: BS((R, c), lambda b, n: (b * nb + n, j))
    mat = BS((None, N_HEADS, nc, CHUNK, CHUNK), lambda b, n: (b, 0, n, 0, 0))
    return pl.pallas_call(
        body, name="gdn_chunk_bwd", grid=(B, nb),
        in_specs=[rowb(512, 0), rowb(512, 1), rowb(512, 2), rowb(128, 0),
                  BS((None, N_HEADS, nc, CHUNK), lambda b, n: (b, 0, n, 0)), mat, mat,
                  rowb(512, 0), rowb(512, 0), rowb(512, 0), rowb(512, 0), rowb(128, 0)],
        out_specs=[rowb(GDN_QKV, 0), rowb(128, 0)],
        out_shape=[jax.ShapeDtypeStruct((T, GDN_QKV), F32), jax.ShapeDtypeStruct((T, 128), F32)],
        compiler_params=_arb(2))(qkv, qkv, qkv, GB, Grow, Tinv, dA, dU, dW, dQ1, dK1, dG1)


def _gdn_out_norm(og, gg):
    outs, xhs, rs = [], [], []
    for h in range(N_HEADS):
        seg = og[:, h * 128:(h + 1) * 128]
        r = lax.rsqrt(jnp.mean(seg * seg, axis=-1, keepdims=True) + EPS)
        xh = seg * r
        outs.append(xh * gg)
        xhs.append(xh)
        rs.append(r)
    return outs, xhs, rs


def merge_fwd(o_mla, o_gdn, o_mem, P, x, tgt, w_out, g_gdn, g_fin, tm=256):
    T = x.shape[0]
    tm = min(tm, T)

    def body(om_ref, og_ref, oc_ref, gate_ref, x_ref, t_ref, w_ref, gg_ref, gf_ref, mix_ref, dx_ref, sq_ref, gnf_ref):
        @pl.when(pl.program_id(0) == 0)
        def _():
            sq_ref[...] = jnp.zeros_like(sq_ref)
            gnf_ref[...] = jnp.zeros_like(gnf_ref)

        ogn, _, _ = _gdn_out_norm(og_ref[...], gg_ref[...])
        cat = jnp.concatenate([om_ref[...]] + ogn + [oc_ref[...]], axis=1)
        gt = gate_ref[...]
        mixed = (cat * (gt * _sigmoid(gt))).astype(BF16)
        mix_ref[...] = mixed
        x2 = x_ref[...] + _dot(mixed, w_ref[...], NN)
        r2 = lax.rsqrt(jnp.mean(x2 * x2, axis=-1, keepdims=True) + EPS)
        xh = x2 * r2
        gf = gf_ref[...]
        diff = xh * gf - t_ref[...]
        sq_ref[...] += jnp.sum(diff * diff, axis=0, keepdims=True)
        dy = diff * (1.0 / D_MODEL)
        gnf_ref[...] += jnp.sum(dy * xh, axis=0, keepdims=True)
        dxh = dy * gf
        dx_ref[...] = r2 * (dxh - xh * jnp.mean(dxh * xh, axis=-1, keepdims=True))

    rowb = lambda c, j=0: BS((tm, c), lambda i: (i, j))
    full = lambda r, c: BS((r, c), lambda i: (0, 0))
    return pl.pallas_call(
        body, name="merge_fwd", grid=(T // tm,),
        in_specs=[rowb(512), rowb(512), rowb(512), rowb(D_MIX, OFF_GATE // D_MIX), rowb(D_MODEL), rowb(D_MODEL),
                  full(D_MIX, D_MODEL), full(1, 128), full(1, D_MODEL)],
        out_specs=[rowb(D_MIX), rowb(D_MODEL), full(1, D_MODEL), full(1, D_MODEL)],
        out_shape=[jax.ShapeDtypeStruct((T, D_MIX), BF16), jax.ShapeDtypeStruct((T, D_MODEL), F32),
                   jax.ShapeDtypeStruct((1, D_MODEL), F32), jax.ShapeDtypeStruct((1, D_MODEL), F32)],
        compiler_params=_arb(1))(o_mla, o_gdn, o_mem, P, x, tgt, w_out, g_gdn, g_fin)


def merge_bwd(dx2, o_mla, o_gdn, o_mem, P, w_out, g_gdn, tm=256):
    T = dx2.shape[0]
    tm = min(tm, T)

    def body(dx_ref, om_ref, og_ref, oc_ref, gate_ref, w_ref, gg_ref, dgate_ref, dom_ref, dog_ref, doc_ref, ggn_ref):
        @pl.when(pl.program_id(0) == 0)
        def _():
            ggn_ref[...] = jnp.zeros_like(ggn_ref)

        gg = gg_ref[...]
        dmix = _dot(dx_ref[...].astype(BF16), w_ref[...], NT)
        ogn, xhs, rs = _gdn_out_norm(og_ref[...], gg)
        cat = jnp.concatenate([om_ref[...]] + ogn + [oc_ref[...]], axis=1)
        gt = gate_ref[...]
        sg = _sigmoid(gt)
        dgate_ref[...] = (dmix * cat * (sg * (1.0 + gt * (1.0 - sg)))).astype(BF16)
        dcat = dmix * (gt * sg)
        dom_ref[...] = dcat[:, :512]
        doc_ref[...] = dcat[:, 1024:]
        acc = jnp.zeros((1, 128), F32)
        for h in range(N_HEADS):
            dseg = dcat[:, 512 + h * 128:512 + (h + 1) * 128]
            acc = acc + jnp.sum(dseg * xhs[h], axis=0, keepdims=True)
            dxh = dseg * gg
            dog_ref[:, h * 128:(h + 1) * 128] = rs[h] * (dxh - xhs[h] * jnp.mean(dxh * xhs[h], axis=-1, keepdims=True))
        ggn_ref[...] += acc

    rowb = lambda c, j=0: BS((tm, c), lambda i: (i, j))
    full = lambda r, c: BS((r, c), lambda i: (0, 0))
    return pl.pallas_call(
        body, name="merge_bwd", grid=(T // tm,),
        in_specs=[rowb(D_MODEL), rowb(512), rowb(512), rowb(512), rowb(D_MIX, OFF_GATE // D_MIX),
                  full(D_MIX, D_MODEL), full(1, 128)],
        out_specs=[rowb(D_MIX), rowb(512), rowb(512), rowb(512), full(1, 128)],
        out_shape=[jax.ShapeDtypeStruct((T, D_MIX), BF16)] + [jax.ShapeDtypeStruct((T, 512), F32)] * 3
        + [jax.ShapeDtypeStruct((1, 128), F32)],
        compiler_params=_arb(1))(dx2, o_mla, o_gdn, o_mem, P, w_out, g_gdn)


def in_norm_bwd(x, dh, dx2, gain, tm=256):
    T, n = x.shape
    tm = min(tm, T)

    def body(x_ref, dh_ref, dx2_ref, g_ref, o_ref, acc_ref):
        @pl.when(pl.program_id(0) == 0)
        def _():
            acc_ref[...] = jnp.zeros_like(acc_ref)

        xv = x_ref[...]
        r = lax.rsqrt(jnp.mean(xv * xv, axis=-1, keepdims=True) + EPS)
        xh = xv * r
        dy = dh_ref[...]
        acc_ref[...] += jnp.sum(dy * xh, axis=0, keepdims=True)
        dxh = dy * g_ref[...]
        o_ref[...] = dx2_ref[...] + r * (dxh - xh * jnp.mean(dxh * xh, axis=-1, keepdims=True))

    rowb = BS((tm, n), lambda i: (i, 0))
    full = BS((1, n), lambda i: (0, 0))
    return pl.pallas_call(
        body, name="in_norm_bwd", grid=(T // tm,),
        in_specs=[rowb, rowb, rowb, full], out_specs=[rowb, full],
        out_shape=[jax.ShapeDtypeStruct((T, n), F32), jax.ShapeDtypeStruct((1, n), F32)],
        compiler_params=_arb(1))(x, dh, dx2, gain)


W_IN_SHARD = D_IN // 4
_GDN0 = Q_LORA + KV_LORA + MLA_ROPE
_AB0 = _GDN0 + GDN_QKV
_MEMQ0 = _AB0 + 2 * N_HEADS
_GATE0 = _MEMQ0 + N_HEADS * MEM_DH


def _pad_w_in_t(s):
    z = lambda n: jnp.zeros((n, s[0].shape[1]), s[0].dtype)
    a, m, gt = _AB0 - 2 * W_IN_SHARD, _MEMQ0 - 2 * W_IN_SHARD, _GATE0 - 2 * W_IN_SHARD
    return jnp.concatenate([s[0][:640], s[0][640:672], z(32), s[0][672:704], z(32), s[2][a:m], z(248), s[2][m:gt],
                            s[0][_GDN0:], s[1], s[2][:a], s[2][gt:], s[3]], axis=0)


def _unpad_w_in_t(g):
    e0 = OFF_GDN + W_IN_SHARD - _GDN0
    e1 = e0 + W_IN_SHARD
    gt = OFF_GATE + 3 * W_IN_SHARD - _GATE0
    return jnp.stack([jnp.concatenate([g[0:640], g[640:672], g[704:736], g[OFF_GDN:e0]]), g[e0:e1],
                      jnp.concatenate([g[e1:OFF_GATE], g[768:776], g[OFF_MEMQ:OFF_MEMQ + 512], g[OFF_GATE:gt]]), g[gt:]])


def _pad_w_q_b_t(s):
    z = jnp.zeros((32, s.shape[2]), s.dtype)
    parts = []
    for h in range(N_HEADS):
        parts += [s[h, :128], s[h, 128:160], z, s[h, 160:192], z]
    return jnp.concatenate(parts, axis=0)


def _unpad_w_q_b_t(g):
    return jnp.stack([jnp.concatenate([g[h * HEAD_PAD:h * HEAD_PAD + 128], g[h * HEAD_PAD + 128:h * HEAD_PAD + 160],
                                       g[h * HEAD_PAD + 192:h * HEAD_PAD + 224]]) for h in range(N_HEADS)])


def _perm_w_kv_b(s):
    return jnp.concatenate([s[h, :, :128] for h in range(N_HEADS)] + [s[h, :, 128:] for h in range(N_HEADS)], axis=1)


def _unperm_w_kv_b(g):
    return jnp.stack([jnp.concatenate([g[:, h * 128:(h + 1) * 128], g[:, 512 + h * 128:512 + (h + 1) * 128]], axis=1)
                      for h in range(N_HEADS)])


def _lane_row(v4):
    return jnp.pad(v4.reshape(1, -1).astype(F32), ((0, 0), (0, 128 - v4.size)))


def _pack(pieces, n_rows):
    flat = jnp.concatenate([p.reshape(-1) for p in pieces])
    return jnp.pad(flat, (0, n_rows * 1024 - flat.size)).reshape(n_rows, 1024)


def _unpack(block, shapes):
    flat = block.reshape(-1)
    out, off = [], 0
    for shp in shapes:
        n = int(np.prod(shp))
        out.append(flat[off:off + n].reshape(shp))
        off += n
    return out


N_CHIPS = 4
MESH = pl.DeviceIdType.MESH
ANY = BS(memory_space=pl.ANY)


def _place():
    return lax.axis_index("x"), lax.axis_index("y"), lax.axis_index("c")


def _other_chips(x, y):
    return [(1 - x, y), (x, 1 - y), (1 - x, 1 - y)]


def _half(split, which):
    axis, size = split
    ds = pl.ds(pl.multiple_of(which * size, 16 if axis == 0 else 128), size)
    return (ds, slice(None)) if axis == 0 else (slice(None), ds)


def allgather_chips(shards, splits, name):
    n = len(shards)

    def body(*refs):
        s_refs, o_refs = refs[:n], refs[n:2 * n]
        send_sems, recv_sems, local_sems = refs[2 * n:]
        x, y, c = _place()
        chips = _other_chips(x, y)

        def copy(k, src, dst, to):
            return pltpu.make_async_remote_copy(src_ref=src, dst_ref=dst, send_sem=send_sems.at[k], recv_sem=recv_sems.at[k],
                                                device_id=to, device_id_type=MESH)

        owns, first, passed = [], [], []
        for i, (s_ref, o_ref) in enumerate(zip(s_refs, o_refs)):
            mine = _half(splits[i], c)
            owns.append(pltpu.make_async_copy(s_ref, o_ref.at[2 * x + y], local_sems.at[i]))
            owns[-1].start()
            for j, (px, py) in enumerate(chips):
                first.append(copy(6 * i + j, s_ref.at[mine], o_ref.at[(2 * x + y,) + mine], (px, py, c)))
                first[-1].start()
        for i, (s_ref, o_ref) in enumerate(zip(s_refs, o_refs)):
            mine = _half(splits[i], c)
            for j, (px, py) in enumerate(chips):
                landed = o_ref.at[(2 * px + py,) + mine]
                copy(6 * i + j, s_ref.at[mine], landed, (px, py, c)).wait_recv()
                passed.append(copy(6 * i + 3 + j, landed, landed, (x, y, 1 - c)))
                passed[-1].start()
        for i, (s_ref, o_ref) in enumerate(zip(s_refs, o_refs)):
            theirs = _half(splits[i], 1 - c)
            for j, (px, py) in enumerate(chips):
                copy(6 * i + 3 + j, s_ref.at[theirs], o_ref.at[(2 * px + py,) + theirs], (x, y, 1 - c)).wait_recv()
        for cp in first + passed:
            cp.wait_send()
        for cp in owns:
            cp.wait()

    return pl.pallas_call(
        body, name=name, in_specs=[ANY] * n, out_specs=[ANY] * n,
        out_shape=[jax.ShapeDtypeStruct((N_CHIPS,) + s.shape, s.dtype) for s in shards],
        scratch_shapes=[pltpu.SemaphoreType.DMA((6 * n,)), pltpu.SemaphoreType.DMA((6 * n,)), pltpu.SemaphoreType.DMA((n,))])(*shards)


def allgather_devices(block, name):
    R, C = block.shape

    def body(b_ref, o_ref, send_sems, recv_sems, local_sem):
        x, y, c = _place()
        me = 4 * x + 2 * y + c
        own = pltpu.make_async_copy(b_ref, o_ref.at[me], local_sem)
        own.start()
        copies = []
        for r in range(1, 8):
            px = 1 - x if r & 4 else x
            py = 1 - y if r & 2 else y
            pc = 1 - c if r & 1 else c
            send = pltpu.make_async_remote_copy(src_ref=b_ref, dst_ref=o_ref.at[me], send_sem=send_sems.at[r - 1],
                                                recv_sem=recv_sems.at[r - 1], device_id=(px, py, pc), device_id_type=MESH)
            recv = pltpu.make_async_remote_copy(src_ref=b_ref, dst_ref=o_ref.at[4 * px + 2 * py + pc], send_sem=send_sems.at[r - 1],
                                                recv_sem=recv_sems.at[r - 1], device_id=(px, py, pc), device_id_type=MESH)
            send.start()
            copies.append((send, recv))
        for send, recv in copies:
            recv.wait_recv()
            send.wait_send()
        own.wait()

    return pl.pallas_call(
        body, name=name, in_specs=[ANY], out_specs=ANY, out_shape=jax.ShapeDtypeStruct((8, R, C), block.dtype),
        scratch_shapes=[pltpu.SemaphoreType.DMA((7,)), pltpu.SemaphoreType.DMA((7,)), pltpu.SemaphoreType.DMA(())])(block)


def swap_sibling(arrs, name, splits=None):
    n = len(arrs)

    def sent(a_ref, i, c):
        return a_ref if splits is None else a_ref.at[(slice(None),) + _half(splits[i], 1 - c)]

    def out_shape(a, i):
        if splits is None:
            return a.shape
        axis, size = splits[i]
        return (a.shape[0], size, a.shape[2]) if axis == 0 else (a.shape[0], a.shape[1], size)

    def body(*refs):
        a_refs, o_refs = refs[:n], refs[n:2 * n]
        send_sems, recv_sems = refs[2 * n:]
        x, y, c = _place()
        copies = [pltpu.make_async_remote_copy(src_ref=sent(a_ref, i, c), dst_ref=o_ref, send_sem=send_sems.at[i],
                                               recv_sem=recv_sems.at[i], device_id=(x, y, 1 - c), device_id_type=MESH)
                  for i, (a_ref, o_ref) in enumerate(zip(a_refs, o_refs))]
        for cp in copies:
            cp.start()
        for cp in copies:
            cp.wait()

    return pl.pallas_call(
        body, name=name, in_specs=[ANY] * n, out_specs=[ANY] * n,
        out_shape=[jax.ShapeDtypeStruct(out_shape(a, i), a.dtype) for i, a in enumerate(arrs)],
        scratch_shapes=[pltpu.SemaphoreType.DMA((n,)), pltpu.SemaphoreType.DMA((n,))])(*arrs)


def exchange_chips(parts, name):
    n = len(parts)

    def body(*refs):
        p_refs, o_refs = refs[:n], refs[n:2 * n]
        send_sems, recv_sems = refs[2 * n:]
        x, y, c = _place()
        copies = [pltpu.make_async_remote_copy(src_ref=p_ref.at[2 * px + py], dst_ref=o_ref.at[j], send_sem=send_sems.at[3 * i + j],
                                               recv_sem=recv_sems.at[3 * i + j], device_id=(px, py, c), device_id_type=MESH)
                  for i, (p_ref, o_ref) in enumerate(zip(p_refs, o_refs)) for j, (px, py) in enumerate(_other_chips(x, y))]
        for cp in copies:
            cp.start()
        for cp in copies:
            cp.wait()

    return pl.pallas_call(
        body, name=name, in_specs=[ANY] * n, out_specs=[ANY] * n,
        out_shape=[jax.ShapeDtypeStruct((3,) + p.shape[1:], p.dtype) for p in parts],
        scratch_shapes=[pltpu.SemaphoreType.DMA((3 * n,)), pltpu.SemaphoreType.DMA((3 * n,))])(*parts)


def _half_block(shape2, split):
    axis, size = split
    return (size, shape2[1]) if axis == 0 else (shape2[0], size)


def add_pairs(parts, halves, splits, core, name):
    n = len(parts)

    def body(s_ref, *refs):
        for a_ref, b_ref, o_ref in zip(refs[:n], refs[n:2 * n], refs[2 * n:]):
            o_ref[...] = (a_ref[...].astype(F32) + b_ref[...].astype(F32)).astype(BF16)

    def mine(i):
        blk = (None,) + _half_block(parts[i].shape[1:], splits[i])
        if splits[i][0] == 0:
            return BS(blk, lambda q, s: (q, s[0], 0))
        return BS(blk, lambda q, s: (q, 0, s[0]))

    half_specs = [BS((None,) + h.shape[1:], lambda q, s: (q, 0, 0)) for h in halves]
    return pl.pallas_call(
        body, name=name,
        grid_spec=pltpu.PrefetchScalarGridSpec(num_scalar_prefetch=1, grid=(N_CHIPS,),
                                               in_specs=[mine(i) for i in range(n)] + half_specs, out_specs=half_specs),
        out_shape=[jax.ShapeDtypeStruct(h.shape, BF16) for h in halves], compiler_params=_arb(1))(core, *parts, *halves)


def add_fives(parts, halves, from_chips, splits, chip_core, name):
    n = len(parts)

    def body(s_ref, *refs):
        for a_ref, b_ref, p_ref, o_ref in zip(refs[:n], refs[n:2 * n], refs[2 * n:3 * n], refs[3 * n:]):
            s = a_ref[...].astype(F32) + b_ref[...].astype(F32)
            for j in range(3):
                s = s + p_ref[j].astype(F32)
            o_ref[...] = s

    def mine(i):
        blk = (None,) + _half_block(parts[i].shape[1:], splits[i])
        if splits[i][0] == 0:
            return BS(blk, lambda g, s: (s[0], s[1], 0))
        return BS(blk, lambda g, s: (s[0], 0, s[1]))

    half_specs = [BS((None,) + h.shape[1:], lambda g, s: (s[0], 0, 0)) for h in halves]
    chip_specs = [BS(p.shape, lambda g, s: (0, 0, 0)) for p in from_chips]
    out_specs = [BS(h.shape[1:], lambda g, s: (0, 0)) for h in halves]
    return pl.pallas_call(
        body, name=name,
        grid_spec=pltpu.PrefetchScalarGridSpec(num_scalar_prefetch=1, grid=(1,),
                                               in_specs=[mine(i) for i in range(n)] + half_specs + chip_specs, out_specs=out_specs),
        out_shape=[jax.ShapeDtypeStruct(h.shape[1:], F32) for h in halves], compiler_params=_arb(1))(chip_core, *parts, *halves, *from_chips)


def sum_leading(a, name):
    def body(a_ref, o_ref):
        s = a_ref[0]
        for j in range(1, a.shape[0]):
            s = s + a_ref[j]
        o_ref[...] = s

    return pl.pallas_call(body, name=name, out_shape=jax.ShapeDtypeStruct(a.shape[1:], a.dtype))(a)


def _adamw_math(w, g, m, v):
    mn = ADAM_B1 * m + (1.0 - ADAM_B1) * g
    vn = ADAM_B2 * v + (1.0 - ADAM_B2) * (g * g)
    m_hat = mn / (1.0 - ADAM_B1 ** ADAM_STEP)
    v_hat = vn / (1.0 - ADAM_B2 ** ADAM_STEP)
    return -ADAM_LR * (m_hat / (jnp.sqrt(v_hat) + ADAM_EPS) + ADAM_WD * w), mn, vn


def adamw(w, g, m, v, name):
    R, C = g.shape
    lead = (None,) * (w.ndim - 2)

    def body(w_ref, g_ref, m_ref, v_ref, d_ref, mo_ref, vo_ref):
        d_ref[...], mo_ref[...], vo_ref[...] = _adamw_math(w_ref[...], g_ref[...], m_ref[...], v_ref[...])

    wblk = BS(lead + (R, C), lambda i: (0,) * w.ndim)
    gblk = BS((R, C), lambda i: (0, 0))
    return pl.pallas_call(
        body, name=name, grid=(1,), in_specs=[wblk, gblk, wblk, wblk], out_specs=[wblk] * 3,
        out_shape=[jax.ShapeDtypeStruct(w.shape, F32)] * 3, compiler_params=_arb(1))(w, g, m, v)


def adamw_halves(w, mine, other, m, v, split, core, name):
    R, C = w.shape[-2:]
    axis, size = split
    lead = (None,) * (w.ndim - 2)
    zeros = (0,) * (w.ndim - 2)
    if axis == 0:
        tr = size if size <= 256 else next(t for t in range(256, 7, -1) if size % t == 0 and t % 8 == 0)
        nb = size // tr
        whole = BS(lead + (tr, C), lambda hi, j, s: zeros + (hi * nb + j, 0))
        part = BS((tr, C), lambda hi, j, s: (j, 0))
    else:
        nb = size // 128
        whole = BS(lead + (R, 128), lambda hi, j, s: zeros + (0, hi * nb + j))
        part = BS((R, 128), lambda hi, j, s: (0, j))

    def body(s_ref, w_ref, a_ref, b_ref, m_ref, v_ref, g_ref, d_ref, mo_ref, vo_ref):
        g = jnp.where(pl.program_id(0) == s_ref[0], a_ref[...], b_ref[...])
        g_ref[...] = g
        d_ref[...], mo_ref[...], vo_ref[...] = _adamw_math(w_ref[...], g, m_ref[...], v_ref[...])

    return pl.pallas_call(
        body, name=name,
        grid_spec=pltpu.PrefetchScalarGridSpec(num_scalar_prefetch=1, grid=(2, nb),
                                               in_specs=[whole, part, part, whole, whole], out_specs=[whole] * 4),
        out_shape=[jax.ShapeDtypeStruct(w.shape, F32)] * 4, compiler_params=_arb(2))(core, w, mine, other, m, v)


def local_step(x, mem, positions, tgt, norm_in, w_in, q_a_norm, w_q_b, kv_a_norm, w_kv_b, gdn_conv, gdn_a_log,
               gdn_dt_bias, gdn_norm, mem_norm, w_mem_kv, w_out, norm_final):
    B, S, D = x.shape
    M = mem.shape[1]
    T = B * S
    N = S // CHUNK
    x2d = x.reshape(T, D)
    mem2d = mem.reshape(B * M, D)
    tgt2d = tgt.reshape(T, D)

    wp, wq, wkv = w_in, w_q_b, w_kv_b
    alog_row, dt_row = _lane_row(gdn_a_log), _lane_row(gdn_dt_bias)

    half = MLA_ROPE // 2
    inv_freq = 1.0 / (ROPE_THETA ** (jnp.arange(half, dtype=F32) / half))
    z32 = jnp.zeros((half,), F32)
    o32 = jnp.ones((half,), F32)
    inv_row = jnp.concatenate([inv_freq, z32, inv_freq, z32]).reshape(1, 128)
    sgn_row = jnp.concatenate([-o32, z32, o32, z32]).reshape(1, 128)
    msk_row = jnp.concatenate([o32, z32, o32, z32]).reshape(1, 128)
    cos_t, sin_t = rope_tables(positions.reshape(T, 1), inv_row, sgn_row, msk_row)

    h = rms_fwd(x2d, norm_in, "rms_in")
    P = mm(h, wp, "nt", F32, "in_proj", bm=512, bn=768, bk=1024)
    Q, K, V, qn, kvn = mla_prep(P, q_a_norm, kv_a_norm, wq, wkv, cos_t, sin_t)
    o_mla, lse = mla_attn_fwd(Q, K, V, B, S)
    memn = rms_fwd(mem2d, mem_norm, "rms_mem")
    MKV = mm(memn, w_mem_kv, "nn", BF16, "mem_kv_proj", bk=1024)
    o_mem = mem_attn_fwd(P, MKV, B, S, M)
    qkv = gdn_prep_fwd(P, gdn_conv, B, S)
    GB = gdn_gate_fwd(P, alog_row, dt_row, B, S)
    Grow = jnp.transpose(GB[:, :N_HEADS].reshape(B, N, CHUNK, N_HEADS), (0, 3, 1, 2))
    U, W, Tinv, A = gdn_chunk_fwd(qkv, GB, Grow, B, S)
    qkv3, GB3 = qkv.reshape(B, S, GDN_QKV), GB.reshape(B, S, 128)
    W3 = W.reshape(B, S, 512)
    o_gdn3, Vn3, St = gdn_scan_fwd(qkv3, U.reshape(B, S, 512), W3, GB3, A, B, S)
    o_gdn = o_gdn3.reshape(T, 512)
    mixed, dx2, sq, g_norm_final = merge_fwd(o_mla, o_gdn, o_mem, P, x2d, tgt2d, w_out, gdn_norm, norm_final.reshape(1, D))

    g_w_out = mm(mixed, dx2, "tn", BF16, "grad_w_out")
    dgate, do_mla, do_gdn, do_mem, g_gdn_norm = merge_bwd(dx2, o_mla, o_gdn, o_mem, P, w_out, gdn_norm)

    dmemq, dMKV = mem_attn_bwd(P, MKV, do_mem, B, S, M)
    g_w_mem_kv = mm(memn, dMKV, "tn", BF16, "grad_w_mem_kv")
    dmemn = mm(dMKV, w_mem_kv, "nt", F32, "d_memn", bk=1024)
    g_mem_norm = gain_grad(mem2d, dmemn, "grad_mem_norm")

    dU3, dW3, dQ13, dK13, dA, dG13 = gdn_scan_bwd(do_gdn.reshape(B, S, 512), qkv3, W3, Vn3, GB3, A, St, B, S)
    r2 = lambda a: a.reshape(T, a.shape[-1])
    dqkv, dGB = gdn_chunk_bwd(qkv, GB, Grow, Tinv, dA, r2(dU3), r2(dW3), r2(dQ13), r2(dK13), r2(dG13), B, S)
    dPg, g_conv = gdn_prep_bwd(P, dqkv, gdn_conv, B, S)
    dab, g_ab = gdn_gate_bwd(P, dGB, alog_row, dt_row, B, S)

    dQ, dK, dV = mla_attn_bwd(Q, K, V, o_mla, do_mla, lse, B, S)
    dq_lin, dkv_lin, dkr = mla_post_bwd(dQ, dK, dV, cos_t, sin_t)
    dqn = mm(dq_lin, wq, "nn", F32, "d_qn", bk=1024)
    dkvn = mm(dkv_lin, wkv, "nt", F32, "d_kvn", bk=1024)
    g_wq = mm(dq_lin, qn, "tn", BF16, "grad_w_q_b")
    g_wkv = mm(kvn, dkv_lin, "tn", BF16, "grad_w_kv_b")
    dPm, g_q_a_norm, g_kv_a_norm = mla_norm_bwd(P, dqn, dkvn, dkr, dab, q_a_norm, kv_a_norm)

    dP = jnp.concatenate([dPm, dmemq, dPg, dgate], axis=1)
    g_wp = mm(dP, h, "tn", BF16, "grad_w_in")
    dh = mm(dP, wp, "nn", F32, "d_h", bn=1024, bk=768)
    grad_x, g_norm_in = in_norm_bwd(x2d, dh, dx2, norm_in)

    grads = dict(
        norm_in=g_norm_in, w_in=g_wp, q_a_norm=g_q_a_norm, w_q_b=g_wq, kv_a_norm=g_kv_a_norm, w_kv_b=g_wkv, gdn_conv=g_conv,
        gdn_a_log=g_ab[0:1, :N_HEADS], gdn_dt_bias=g_ab[1:2, :N_HEADS], gdn_norm=g_gdn_norm,
        mem_norm=g_mem_norm, w_mem_kv=g_w_mem_kv, w_out=g_w_out, norm_final=g_norm_final)
    return sq, grad_x.reshape(B, S, D), grads


def kernel(x, mem, positions, norm_in, w_in, q_a_norm, w_q_b, kv_a_norm, w_kv_b, gdn_conv, gdn_a_log, gdn_dt_bias, gdn_norm, mem_norm, w_mem_kv, w_out, norm_final, loss_target, m_norm_in, m_w_in, m_q_a_norm, m_w_q_b, m_kv_a_norm, m_w_kv_b, m_gdn_conv, m_gdn_a_log, m_gdn_dt_bias, m_gdn_norm, m_mem_norm, m_w_mem_kv, m_w_out, m_norm_final, v_norm_in, v_w_in, v_q_a_norm, v_w_q_b, v_kv_a_norm, v_w_kv_b, v_gdn_conv, v_gdn_a_log, v_gdn_dt_bias, v_gdn_norm, v_mem_norm, v_w_mem_kv, v_w_out, v_norm_final):
    B = x.shape[0]
    cx, cy, cc = lax.axis_index("x"), lax.axis_index("y"), lax.axis_index("c")
    chip = 2 * cx + cy

    big_names = ("w_in", "w_q_b", "w_kv_b", "w_mem_kv", "w_out")
    transposed = ("w_in", "w_q_b")
    view = lambda n, a: jnp.transpose(a[0]) if n in transposed else a
    w_big = {n: view(n, a) for n, a in zip(big_names, (w_in, w_q_b, w_kv_b, w_mem_kv, w_out))}
    m_big = {n: view(n, a) for n, a in zip(big_names, (m_w_in, m_w_q_b, m_w_kv_b, m_w_mem_kv, m_w_out))}
    v_big = {n: view(n, a) for n, a in zip(big_names, (v_w_in, v_w_q_b, v_w_kv_b, v_w_mem_kv, v_w_out))}
    shard2d = lambda n: w_big[n] if n in transposed else w_big[n][0]
    splits = [(1, D_MODEL // 2)] + [(0, shard2d(n).shape[0] // 2) for n in big_names[1:]]
    g_in, g_qb, g_kvb, g_mem, g_out_w = allgather_chips([shard2d(n).astype(BF16) for n in big_names], splits, "allgather_weights")
    conv_all = allgather_devices(_pack([gdn_conv[0]], 8), "allgather_conv")
    conv_shape = gdn_conv[0].shape
    conv_full = jnp.concatenate([_unpack(conv_all[2 * q], [conv_shape])[0] for q in range(N_CHIPS)], axis=1)

    sq, grad_x, g = local_step(x, mem, positions, loss_target, norm_in, _pad_w_in_t([g_in[q] for q in range(N_CHIPS)]), q_a_norm,
                               _pad_w_q_b_t(g_qb), kv_a_norm, _perm_w_kv_b(g_kvb), conv_full, gdn_a_log, gdn_dt_bias, gdn_norm,
                               mem_norm, g_mem.reshape(-1, g_mem.shape[2]), g_out_w.reshape(-1, g_out_w.shape[2]), norm_final)
    loss = lax.psum(0.5 * jnp.sum(sq) / D_MODEL, ("x", "y", "c"))

    core = jnp.stack([cc]).astype(jnp.int32)
    chip_core = jnp.stack([chip, cc]).astype(jnp.int32)
    parts = [_unpad_w_in_t(g["w_in"]), _unpad_w_q_b_t(g["w_q_b"]), _unperm_w_kv_b(g["w_kv_b"]),
             g["w_mem_kv"].reshape(g_mem.shape), g["w_out"].reshape(g_out_w.shape)]
    from_sibling = swap_sibling(parts, "rs_sibling_partial", splits)
    chip_sums = add_pairs(parts, from_sibling, splits, core, "rs_add_sibling")
    from_chips = exchange_chips(chip_sums, "rs_exchange_chips")
    my_half = add_fives(parts, from_sibling, from_chips, splits, chip_core, "rs_add_chips")
    other_half = swap_sibling(my_half, "rs_sibling_final")

    small_names = ("norm_in", "q_a_norm", "kv_a_norm", "gdn_a_log", "gdn_dt_bias", "gdn_norm", "mem_norm", "norm_final")
    small = dict(norm_in=norm_in, q_a_norm=q_a_norm, kv_a_norm=kv_a_norm, gdn_a_log=gdn_a_log, gdn_dt_bias=gdn_dt_bias,
                 gdn_norm=gdn_norm, mem_norm=mem_norm, norm_final=norm_final)
    m_small = dict(norm_in=m_norm_in, q_a_norm=m_q_a_norm, kv_a_norm=m_kv_a_norm, gdn_a_log=m_gdn_a_log,
                   gdn_dt_bias=m_gdn_dt_bias, gdn_norm=m_gdn_norm, mem_norm=m_mem_norm, norm_final=m_norm_final)
    v_small = dict(norm_in=v_norm_in, q_a_norm=v_q_a_norm, kv_a_norm=v_kv_a_norm, gdn_a_log=v_gdn_a_log,
                   gdn_dt_bias=v_gdn_dt_bias, gdn_norm=v_gdn_norm, mem_norm=v_mem_norm, norm_final=v_norm_final)
    rows = lambda d: jnp.stack([jnp.pad(d[n].reshape(-1), (0, 1024 - d[n].size)) for n in small_names])
    conv_rows = GDN_CONV * GDN_QKV // 1024
    g_block = jnp.concatenate([rows(g), g["gdn_conv"].reshape(conv_rows, 1024), jnp.zeros((16 - 8 - conv_rows, 1024), F32)])
    g_block = sum_leading(allgather_devices(g_block, "allgather_small_grads"), "sum_small_grads")
    g_small_rows = g_block[:8]
    conv_cols = gdn_conv.shape[2]
    g_conv = lax.dynamic_slice_in_dim(g_block[8:8 + conv_rows].reshape(GDN_CONV, GDN_QKV), chip * conv_cols, conv_cols, axis=1)
    d_s, m_s, v_s = adamw(rows(small), g_small_rows, rows(m_small), rows(v_small), "adamw_small")
    unrow = lambda r: {n: r[i, :small[n].size].reshape(small[n].shape) for i, n in enumerate(small_names)}
    g_out, d_out, m_out, v_out = unrow(g_small_rows), unrow(d_s), unrow(m_s), unrow(v_s)

    d_out["gdn_conv"], m_out["gdn_conv"], v_out["gdn_conv"] = adamw(gdn_conv, g_conv, m_gdn_conv, v_gdn_conv, "adamw_gdn_conv")
    g_out["gdn_conv"] = g_conv[None]
    for i, n in enumerate(big_names):
        res = adamw_halves(w_big[n], my_half[i], other_half[i], m_big[n], v_big[n], splits[i], core, "adamw_" + n)
        if n in transposed:
            res = [jnp.transpose(r)[None] for r in res]
        g_out[n], d_out[n], m_out[n], v_out[n] = res

    order = ("norm_in", "w_in", "q_a_norm", "w_q_b", "kv_a_norm", "w_kv_b", "gdn_conv", "gdn_a_log", "gdn_dt_bias",
             "gdn_norm", "mem_norm", "w_mem_kv", "w_out", "norm_final")
    return (loss, grad_x, *[g_out[n] for n in order], *[d_out[n] for n in order], *[m_out[n] for n in order],
            *[v_out[n] for n in order])
```

```python
import functools
import math

import jax
import jax.numpy as jnp
import numpy as np
from jax import lax
from jax.experimental import pallas as pl
from jax.experimental.pallas import tpu as pltpu

F32 = jnp.float32
BF16 = jnp.bfloat16
BS = pl.BlockSpec

D_MODEL = 1024
N_HEADS = 4
MLA_NOPE, MLA_ROPE, MLA_V = 128, 64, 128
Q_LORA, KV_LORA = 384, 256
ROPE_THETA = 10000.0
GDN_DK = GDN_DV = 128
GDN_CONV = 4
CHUNK = 64
MEM_DH = 128
D_MIX = 1536
GDN_QKV = 1536
D_IN = 4296
EPS = 1e-6
ADAM_LR, ADAM_B1, ADAM_B2, ADAM_EPS, ADAM_WD, ADAM_STEP = 0.001, 0.9, 0.999, 1e-08, 0.01, 10

OFF_MLA = 0
OFF_MEMQ = 1024
OFF_GDN = 1536
OFF_GATE = 3072
N_PAD = 4608
HEAD_PAD = 256
MLA_SCALE = (MLA_NOPE + MLA_ROPE) ** -0.5
MEM_SCALE = MEM_DH ** -0.5
GDN_SCALE = GDN_DK ** -0.5
NEG = -1e30

NN = ((1,), (0,))
NT = ((1,), (1,))
TN = ((0,), (0,))


def _dot(a, b, dims):
    return lax.dot_general(a, b, (dims, ((), ())), preferred_element_type=F32)


def _bdot(spec, a, b, precision=None):
    return jnp.einsum(spec, a, b, preferred_element_type=F32, precision=precision)


def _arb(n):
    return pltpu.CompilerParams(dimension_semantics=("arbitrary",) * n)


def _sigmoid(x):
    return 1.0 / (1.0 + jnp.exp(-x))


def _softplus(z):
    return jnp.maximum(z, 0.0) + jnp.log(1.0 + jnp.exp(-jnp.abs(z)))


def _rope(t, cos_row, sin_row):
    return t * cos_row + pltpu.roll(t, 64, 1) * sin_row


def _rope_bwd(d, cos_row, sin_row):
    return d * cos_row + pltpu.roll(d * sin_row, 64, 1)


def rms_fwd(x, gain, name, tm=512):
    T, n = x.shape
    tm = min(tm, T)

    def body(x_ref, g_ref, o_ref):
        xv = x_ref[...]
        r = lax.rsqrt(jnp.mean(xv * xv, axis=-1, keepdims=True) + EPS)
        o_ref[...] = (xv * r * g_ref[...]).astype(BF16)

    return pl.pallas_call(
        body, name=name, grid=(T // tm,),
        in_specs=[BS((tm, n), lambda i: (i, 0)), BS((1, n), lambda i: (0, 0))],
        out_specs=BS((tm, n), lambda i: (i, 0)),
        out_shape=jax.ShapeDtypeStruct((T, n), BF16), compiler_params=_arb(1))(x, gain)


def mm(a, b, kind, out_dtype, name, bm=512, bn=512, bk=512):
    if kind == "nn":
        (M, K), (_, N) = a.shape, b.shape
    elif kind == "nt":
        (M, K), (N, _) = a.shape, b.shape
    else:
        (K, M), (_, N) = a.shape, b.shape
    bm, bn, bk = min(bm, M), min(bn, N), min(bk, K)
    assert M % bm == 0 and N % bn == 0 and K % bk == 0, (name, M, N, K)
    nk = K // bk
    a_spec = BS((bk, bm), lambda i, j, k: (k, i)) if kind == "tn" else BS((bm, bk), lambda i, j, k: (i, k))
    b_spec = BS((bn, bk), lambda i, j, k: (j, k)) if kind == "nt" else BS((bk, bn), lambda i, j, k: (k, j))
    dims = {"nn": NN, "nt": NT, "tn": TN}[kind]

    def body(a_ref, b_ref, o_ref, acc):
        k = pl.program_id(2)

        @pl.when(k == 0)
        def _():
            acc[...] = jnp.zeros_like(acc)

        acc[...] += _dot(a_ref[...].astype(BF16), b_ref[...].astype(BF16), dims)

        @pl.when(k == nk - 1)
        def _():
            o_ref[...] = acc[...].astype(out_dtype)

    return pl.pallas_call(
        body, name=name, grid=(M // bm, N // bn, nk),
        in_specs=[a_spec, b_spec], out_specs=BS((bm, bn), lambda i, j, k: (i, j)),
        out_shape=jax.ShapeDtypeStruct((M, N), out_dtype),
        scratch_shapes=[pltpu.VMEM((bm, bn), F32)], compiler_params=_arb(3))(a, b)


def rope_tables(pos_col, inv_row, sgn_row, msk_row, tm=512):
    T = pos_col.shape[0]
    tm = min(tm, T)

    def body(p_ref, inv_ref, sgn_ref, msk_ref, c_ref, s_ref):
        ang = p_ref[...].astype(F32) * inv_ref[...]
        c_ref[...] = jnp.cos(ang) * msk_ref[...]
        s_ref[...] = jnp.sin(ang) * sgn_ref[...]

    row = BS((1, 128), lambda i: (0, 0))
    return pl.pallas_call(
        body, name="rope_tables", grid=(T // tm,),
        in_specs=[BS((tm, 1), lambda i: (i, 0)), row, row, row],
        out_specs=[BS((tm, 128), lambda i: (i, 0))] * 2,
        out_shape=[jax.ShapeDtypeStruct((T, 128), F32)] * 2, compiler_params=_arb(1))(pos_col, inv_row, sgn_row, msk_row)


def mla_prep(P, gq, gkv, wq, wkv, cos_t, sin_t, tm=512):
    T = P.shape[0]
    tm = min(tm, T)

    def body(p_ref, gq_ref, gkv_ref, wq_ref, wkv_ref, c_ref, s_ref, q_ref, k_ref, v_ref, qn_ref, kvn_ref):
        p = p_ref[...]
        cq, ckv, kr = p[:, :Q_LORA], p[:, Q_LORA:Q_LORA + KV_LORA], p[:, 640:768]
        qn = (cq * lax.rsqrt(jnp.mean(cq * cq, axis=-1, keepdims=True) + EPS) * gq_ref[...]).astype(BF16)
        kvn = (ckv * lax.rsqrt(jnp.mean(ckv * ckv, axis=-1, keepdims=True) + EPS) * gkv_ref[...]).astype(BF16)
        qn_ref[...] = qn
        kvn_ref[...] = kvn
        q = _dot(qn, wq_ref[...], NT)
        kv = _dot(kvn, wkv_ref[...], NN)
        cos_row, sin_row = c_ref[...], s_ref[...]
        krr = _rope(kr, cos_row, sin_row).astype(BF16)
        for h in range(N_HEADS):
            lo = h * HEAD_PAD
            q_ref[:, lo:lo + 128] = (q[:, lo:lo + 128] * MLA_SCALE).astype(BF16)
            q_ref[:, lo + 128:lo + 256] = (_rope(q[:, lo + 128:lo + 256], cos_row, sin_row) * MLA_SCALE).astype(BF16)
            k_ref[:, lo:lo + 128] = kv[:, h * 128:(h + 1) * 128].astype(BF16)
            k_ref[:, lo + 128:lo + 256] = krr
        v_ref[...] = kv[:, 512:].astype(BF16)

    full = lambda r, c: BS((r, c), lambda i: (0, 0))
    rowb = lambda c: BS((tm, c), lambda i: (i, 0))
    return pl.pallas_call(
        body, name="mla_prep", grid=(T // tm,),
        in_specs=[rowb(1024), full(1, Q_LORA), full(1, KV_LORA), full(1024, Q_LORA), full(KV_LORA, 1024), rowb(128), rowb(128)],
        out_specs=[rowb(1024), rowb(1024), rowb(512), rowb(Q_LORA), rowb(KV_LORA)],
        out_shape=[jax.ShapeDtypeStruct((T, 1024), BF16), jax.ShapeDtypeStruct((T, 1024), BF16),
                   jax.ShapeDtypeStruct((T, 512), BF16), jax.ShapeDtypeStruct((T, Q_LORA), BF16),
                   jax.ShapeDtypeStruct((T, KV_LORA), BF16)],
        compiler_params=_arb(1))(P, gq, gkv, wq, wkv, cos_t, sin_t)


def mla_attn_fwd(Q, K, V, B, S, tq=512):
    T = B * S
    tq = min(tq, S)
    nq = S // tq

    def body(q_ref, k_ref, v_ref, o_ref, lse_ref, m_s, l_s, acc_s):
        i = pl.program_id(2)
        q = q_ref[...]
        m_s[...] = jnp.full_like(m_s, NEG)
        l_s[...] = jnp.zeros_like(l_s)
        acc_s[...] = jnp.zeros_like(acc_s)

        def blk(j, masked):
            rows = pl.ds(pl.multiple_of(j * tq, tq), tq)
            s = _dot(q, k_ref[rows, :], NT)
            if masked:
                r = lax.broadcasted_iota(jnp.int32, (tq, tq), 0)
                c = lax.broadcasted_iota(jnp.int32, (tq, tq), 1)
                s = jnp.where(r >= c, s, NEG)
            m_prev = m_s[...]
            m_new = jnp.maximum(m_prev, jnp.max(s, axis=1, keepdims=True))
            p = jnp.exp(s - m_new)
            alpha = jnp.exp(m_prev - m_new)
            l_s[...] = alpha * l_s[...] + jnp.sum(p, axis=1, keepdims=True)
            acc_s[...] = alpha * acc_s[...] + _dot(p.astype(BF16), v_ref[rows, :], NN)
            m_s[...] = m_new

        def loop(j, c):
            blk(j, False)
            return c

        lax.fori_loop(0, i, loop, 0)
        blk(i, True)
        o_ref[...] = acc_s[...] / l_s[...]
        lse_ref[...] = m_s[...] + jnp.log(l_s[...])

    return pl.pallas_call(
        body, name="mla_attn_fwd", grid=(B, N_HEADS, nq),
        in_specs=[BS((tq, HEAD_PAD), lambda b, h, i: (b * nq + i, h)),
                  BS((S, HEAD_PAD), lambda b, h, i: (b, h)),
                  BS((S, 128), lambda b, h, i: (b, h))],
        out_specs=[BS((tq, 128), lambda b, h, i: (b * nq + i, h)),
                   BS((None, tq, 1), lambda b, h, i: (h, b * nq + i, 0))],
        out_shape=[jax.ShapeDtypeStruct((T, 512), F32), jax.ShapeDtypeStruct((N_HEADS, T, 1), F32)],
        scratch_shapes=[pltpu.VMEM((tq, 1), F32), pltpu.VMEM((tq, 1), F32), pltpu.VMEM((tq, 128), F32)],
        compiler_params=_arb(3))(Q, K, V)


def mla_attn_bwd(Q, K, V, O, dO, LSE, B, S, tq=512):
    T = B * S
    tq = min(tq, S)
    nq = S // tq

    def body(q_ref, k_ref, v_ref, o_ref, do_ref, lse_ref, dq_ref, dk_ref, dv_ref, delta_s, dk_s, dv_s):
        j = pl.program_id(2)

        @pl.when(j == 0)
        def _():
            dq_ref[...] = jnp.zeros_like(dq_ref)
            delta_s[...] = jnp.sum(do_ref[...] * o_ref[...], axis=1, keepdims=True)

        dk_s[...] = jnp.zeros_like(dk_s)
        dv_s[...] = jnp.zeros_like(dv_s)
        k = k_ref[...]
        v = v_ref[...]

        def step(i, c):
            rows = pl.ds(pl.multiple_of(i * tq, tq), tq)
            q = q_ref[rows, :]
            do = do_ref[rows, :].astype(BF16)
            s = _dot(q, k, NT)
            r = i * tq + lax.broadcasted_iota(jnp.int32, (tq, tq), 0)
            cc = j * tq + lax.broadcasted_iota(jnp.int32, (tq, tq), 1)
            p = jnp.where(r >= cc, jnp.exp(s - lse_ref[rows, :]), 0.0)
            dv_s[...] += _dot(p.astype(BF16), do, TN)
            dp = _dot(do, v, NT)
            ds = (p * (dp - delta_s[rows, :])).astype(BF16)
            dk_s[...] += _dot(ds, q, TN)
            dq_ref[rows, :] += _dot(ds, k, NN)
            return c

        lax.fori_loop(j, nq, step, 0)
        dk_ref[...] = dk_s[...]
        dv_ref[...] = dv_s[...]

    seq = lambda c: BS((S, c), lambda b, h, j: (b, h))
    return pl.pallas_call(
        body, name="mla_attn_bwd", grid=(B, N_HEADS, nq),
        in_specs=[seq(HEAD_PAD), BS((tq, HEAD_PAD), lambda b, h, j: (b * nq + j, h)),
                  BS((tq, 128), lambda b, h, j: (b * nq + j, h)), seq(128), seq(128),
                  BS((None, S, 1), lambda b, h, j: (h, b, 0))],
        out_specs=[seq(HEAD_PAD), BS((tq, HEAD_PAD), lambda b, h, j: (b * nq + j, h)),
                   BS((tq, 128), lambda b, h, j: (b * nq + j, h))],
        out_shape=[jax.ShapeDtypeStruct((T, 1024), F32), jax.ShapeDtypeStruct((T, 1024), F32),
                   jax.ShapeDtypeStruct((T, 512), F32)],
        scratch_shapes=[pltpu.VMEM((S, 1), F32), pltpu.VMEM((tq, HEAD_PAD), F32), pltpu.VMEM((tq, 128), F32)],
        compiler_params=_arb(3))(Q, K, V, O, dO, LSE)


def mla_post_bwd(dQ, dK, dV, cos_t, sin_t, tm=512):
    T = dQ.shape[0]
    tm = min(tm, T)

    def body(dq_ref, dk_ref, dv_ref, c_ref, s_ref, ql_ref, kvl_ref, kr_ref):
        cos_row, sin_row = c_ref[...], s_ref[...]
        kr = jnp.zeros((tm, 128), F32)
        for h in range(N_HEADS):
            lo = h * HEAD_PAD
            ql_ref[:, lo:lo + 128] = (dq_ref[:, lo:lo + 128] * MLA_SCALE).astype(BF16)
            ql_ref[:, lo + 128:lo + 256] = (_rope_bwd(dq_ref[:, lo + 128:lo + 256], cos_row, sin_row) * MLA_SCALE).astype(BF16)
            kvl_ref[:, h * 128:(h + 1) * 128] = dk_ref[:, lo:lo + 128].astype(BF16)
            kr = kr + dk_ref[:, lo + 128:lo + 256]
        kvl_ref[:, 512:] = dv_ref[...].astype(BF16)
        kr_ref[...] = _rope_bwd(kr, cos_row, sin_row)

    rowb = lambda c: BS((tm, c), lambda i: (i, 0))
    return pl.pallas_call(
        body, name="mla_post_bwd", grid=(T // tm,),
        in_specs=[rowb(1024), rowb(1024), rowb(512), rowb(128), rowb(128)],
        out_specs=[rowb(1024), rowb(1024), rowb(128)],
        out_shape=[jax.ShapeDtypeStruct((T, 1024), BF16), jax.ShapeDtypeStruct((T, 1024), BF16),
                   jax.ShapeDtypeStruct((T, 128), F32)],
        compiler_params=_arb(1))(dQ, dK, dV, cos_t, sin_t)


def mla_norm_bwd(P, dqn, dkvn, dkr, dab, gq, gkv, tm=512):
    T = P.shape[0]
    tm = min(tm, T)

    def norm_bwd(x, dy, g):
        r = lax.rsqrt(jnp.mean(x * x, axis=-1, keepdims=True) + EPS)
        xh = x * r
        dxh = dy * g
        return r * (dxh - xh * jnp.mean(dxh * xh, axis=-1, keepdims=True)), jnp.sum(dy * xh, axis=0, keepdims=True)

    def body(p_ref, dqn_ref, dkvn_ref, dkr_ref, dab_ref, gq_ref, gkv_ref, o_ref, aq_ref, akv_ref):
        @pl.when(pl.program_id(0) == 0)
        def _():
            aq_ref[...] = jnp.zeros_like(aq_ref)
            akv_ref[...] = jnp.zeros_like(akv_ref)

        dcq, ggq = norm_bwd(p_ref[:, :Q_LORA], dqn_ref[...], gq_ref[...])
        dckv, ggkv = norm_bwd(p_ref[:, Q_LORA:640], dkvn_ref[...], gkv_ref[...])
        aq_ref[...] += ggq
        akv_ref[...] += ggkv
        o_ref[:, :Q_LORA] = dcq.astype(BF16)
        o_ref[:, Q_LORA:640] = dckv.astype(BF16)
        o_ref[:, 640:768] = dkr_ref[...].astype(BF16)
        o_ref[:, 768:896] = dab_ref[...]
        o_ref[:, 896:1024] = jnp.zeros((tm, 128), BF16)

    rowb = lambda c: BS((tm, c), lambda i: (i, 0))
    full = lambda c: BS((1, c), lambda i: (0, 0))
    return pl.pallas_call(
        body, name="mla_norm_bwd", grid=(T // tm,),
        in_specs=[rowb(1024), rowb(Q_LORA), rowb(KV_LORA), rowb(128), rowb(128), full(Q_LORA), full(KV_LORA)],
        out_specs=[rowb(1024), full(Q_LORA), full(KV_LORA)],
        out_shape=[jax.ShapeDtypeStruct((T, 1024), BF16), jax.ShapeDtypeStruct((1, Q_LORA), F32),
                   jax.ShapeDtypeStruct((1, KV_LORA), F32)],
        compiler_params=_arb(1))(P, dqn, dkvn, dkr, dab, gq, gkv)


def _mem_probs(qh, kh):
    s = _dot(qh, kh, NT) * MEM_SCALE
    p = jnp.exp(s - jnp.max(s, axis=1, keepdims=True))
    return p / jnp.sum(p, axis=1, keepdims=True)


def mem_attn_fwd(P, MKV, B, S, M, tq=512):
    T = B * S
    tq = min(tq, S)
    nq = S // tq

    def body(q_ref, kv_ref, o_ref):
        for h in range(N_HEADS):
            sl = slice(h * 128, (h + 1) * 128)
            p = _mem_probs(q_ref[:, sl].astype(BF16), kv_ref[:, sl])
            o_ref[:, sl] = _dot(p.astype(BF16), kv_ref[:, 512 + h * 128:512 + (h + 1) * 128], NN)

    return pl.pallas_call(
        body, name="mem_attn_fwd", grid=(B, nq),
        in_specs=[BS((tq, 512), lambda b, i: (b * nq + i, OFF_MEMQ // 512)), BS((M, 1024), lambda b, i: (b, 0))],
        out_specs=BS((tq, 512), lambda b, i: (b * nq + i, 0)),
        out_shape=jax.ShapeDtypeStruct((T, 512), F32), compiler_params=_arb(2))(P, MKV)


def mem_attn_bwd(P, MKV, dO, B, S, M, tq=512):
    T = B * S
    tq = min(tq, S)
    nq = S // tq

    def body(q_ref, kv_ref, do_ref, dq_ref, dkv_ref):
        @pl.when(pl.program_id(1) == 0)
        def _():
            dkv_ref[...] = jnp.zeros_like(dkv_ref)

        for h in range(N_HEADS):
            sl = slice(h * 128, (h + 1) * 128)
            sv = slice(512 + h * 128, 512 + (h + 1) * 128)
            qh = q_ref[:, sl].astype(BF16)
            kh = kv_ref[:, sl]
            do = do_ref[:, sl].astype(BF16)
            p = _mem_probs(qh, kh)
            dkv_ref[:, sv] += _dot(p.astype(BF16), do, TN)
            dp = _dot(do, kv_ref[:, sv], NT)
            ds = (p * (dp - jnp.sum(dp * p, axis=1, keepdims=True)) * MEM_SCALE).astype(BF16)
            dq_ref[:, sl] = _dot(ds, kh, NN).astype(BF16)
            dkv_ref[:, sl] += _dot(ds, qh, TN)

    return pl.pallas_call(
        body, name="mem_attn_bwd", grid=(B, nq),
        in_specs=[BS((tq, 512), lambda b, i: (b * nq + i, OFF_MEMQ // 512)), BS((M, 1024), lambda b, i: (b, 0)),
                  BS((tq, 512), lambda b, i: (b * nq + i, 0))],
        out_specs=[BS((tq, 512), lambda b, i: (b * nq + i, 0)), BS((M, 1024), lambda b, i: (b, 0))],
        out_shape=[jax.ShapeDtypeStruct((T, 512), BF16), jax.ShapeDtypeStruct((B * M, 1024), F32)],
        compiler_params=_arb(2))(P, MKV, dO)


def gain_grad(x, dy, name, tm=256):
    T, n = x.shape
    tm = min(tm, T)

    def body(x_ref, dy_ref, o_ref):
        @pl.when(pl.program_id(0) == 0)
        def _():
            o_ref[...] = jnp.zeros_like(o_ref)

        xv = x_ref[...]
        xh = xv * lax.rsqrt(jnp.mean(xv * xv, axis=-1, keepdims=True) + EPS)
        o_ref[...] += jnp.sum(dy_ref[...] * xh, axis=0, keepdims=True)

    return pl.pallas_call(
        body, name=name, grid=(T // tm,),
        in_specs=[BS((tm, n), lambda i: (i, 0))] * 2, out_specs=BS((1, n), lambda i: (0, 0)),
        out_shape=jax.ShapeDtypeStruct((1, n), F32), compiler_params=_arb(1))(x, dy)


def _conv_silu(x, w, t):
    y = x * w[3:4, :]
    for s in range(1, GDN_CONV):
        y = y + jnp.where(t >= s, pltpu.roll(x, s, 0), 0.0) * w[3 - s:4 - s, :]
    return y, _sigmoid(y)


def gdn_prep_fwd(P, conv_w, B, S):
    T = B * S

    def body(x_ref, w_ref, o_ref):
        kind = pl.program_id(1)
        t = lax.broadcasted_iota(jnp.int32, (S, 1), 0)
        y, sg = _conv_silu(x_ref[...], w_ref[...], t)
        a = y * sg
        scale = jnp.where(kind == 0, GDN_SCALE, 1.0).astype(F32)
        for h in range(N_HEADS):
            sl = slice(h * 128, (h + 1) * 128)
            seg = a[:, sl]
            n = lax.rsqrt(jnp.sum(seg * seg, axis=-1, keepdims=True) + EPS)
            o_ref[:, sl] = jnp.where(kind < 2, seg * (n * scale), seg)

    return pl.pallas_call(
        body, name="gdn_prep_fwd", grid=(B, 3),
        in_specs=[BS((S, 512), lambda b, k: (b, OFF_GDN // 512 + k)), BS((GDN_CONV, 512), lambda b, k: (0, k))],
        out_specs=BS((S, 512), lambda b, k: (b, k)),
        out_shape=jax.ShapeDtypeStruct((T, GDN_QKV), F32), compiler_params=_arb(2))(P, conv_w)


def gdn_prep_bwd(P, dqkv, conv_w, B, S):
    T = B * S

    def body(x_ref, d_ref, w_ref, o_ref, gw_ref):
        kind = pl.program_id(0)

        @pl.when(pl.program_id(1) == 0)
        def _():
            gw_ref[...] = jnp.zeros_like(gw_ref)

        t = lax.broadcasted_iota(jnp.int32, (S, 1), 0)
        x = x_ref[...]
        w = w_ref[...]
        y, sg = _conv_silu(x, w, t)
        a = y * sg
        scale = jnp.where(kind == 0, GDN_SCALE, 1.0).astype(F32)
        das = []
        for h in range(N_HEADS):
            sl = slice(h * 128, (h + 1) * 128)
            seg, dseg = a[:, sl], d_ref[:, sl]
            n = lax.rsqrt(jnp.sum(seg * seg, axis=-1, keepdims=True) + EPS)
            dn = scale * (n * dseg - seg * (n * n * n) * jnp.sum(dseg * seg, axis=-1, keepdims=True))
            das.append(jnp.where(kind < 2, dn, dseg))
        dy = jnp.concatenate(das, axis=1) * (sg * (1.0 + y * (1.0 - sg)))
        dx = dy * w[3:4, :]
        gw_ref[3:4, :] += jnp.sum(dy * x, axis=0, keepdims=True)
        for s in range(1, GDN_CONV):
            dx = dx + jnp.where(t + s < S, pltpu.roll(dy, S - s, 0), 0.0) * w[3 - s:4 - s, :]
            gw_ref[3 - s:4 - s, :] += jnp.sum(dy * jnp.where(t >= s, pltpu.roll(x, s, 0), 0.0), axis=0, keepdims=True)
        o_ref[...] = dx.astype(BF16)

    return pl.pallas_call(
        body, name="gdn_prep_bwd", grid=(3, B),
        in_specs=[BS((S, 512), lambda k, b: (b, OFF_GDN // 512 + k)), BS((S, 512), lambda k, b: (b, k)),
                  BS((GDN_CONV, 512), lambda k, b: (0, k))],
        out_specs=[BS((S, 512), lambda k, b: (b, k)), BS((GDN_CONV, 512), lambda k, b: (0, k))],
        out_shape=[jax.ShapeDtypeStruct((T, GDN_QKV), BF16), jax.ShapeDtypeStruct((GDN_CONV, GDN_QKV), F32)],
        compiler_params=_arb(2))(P, dqkv, conv_w)


def _chunk_row(n_rows):
    return lax.broadcasted_iota(jnp.int32, (n_rows, 1), 0) % CHUNK


def gdn_gate_fwd(P, alog_row, dt_row, B, S):
    T = B * S

    def body(x_ref, al_ref, dt_ref, o_ref):
        x = x_ref[...]
        lane = lax.broadcasted_iota(jnp.int32, (1, 128), 1)
        g = jnp.where(lane < 4, -jnp.exp(al_ref[...]) * _softplus(x + dt_ref[...]), 0.0)
        t = _chunk_row(S)
        for s in (1, 2, 4, 8, 16, 32):
            g = g + jnp.where(t >= s, pltpu.roll(g, s, 0), 0.0)
        o_ref[...] = jnp.where(lane < 4, g, jnp.where(lane < 8, _sigmoid(x), 0.0))

    row = BS((1, 128), lambda b: (0, 0))
    return pl.pallas_call(
        body, name="gdn_gate_fwd", grid=(B,),
        in_specs=[BS((S, 128), lambda b: (b, 768 // 128)), row, row], out_specs=BS((S, 128), lambda b: (b, 0)),
        out_shape=jax.ShapeDtypeStruct((T, 128), F32), compiler_params=_arb(1))(P, alog_row, dt_row)


def gdn_gate_bwd(P, dGB, alog_row, dt_row, B, S):
    T = B * S

    def body(x_ref, d_ref, al_ref, dt_ref, o_ref, acc_ref):
        @pl.when(pl.program_id(0) == 0)
        def _():
            acc_ref[...] = jnp.zeros_like(acc_ref)

        x, d = x_ref[...], d_ref[...]
        lane = lax.broadcasted_iota(jnp.int32, (1, 128), 1)
        z = x + dt_ref[...]
        coef = -jnp.exp(al_ref[...])
        g = coef * _softplus(z)
        da = jnp.where(lane < 4, d * coef * _sigmoid(z), 0.0)
        beta = _sigmoid(x)
        o_ref[...] = jnp.where(lane < 4, da, jnp.where(lane < 8, d * beta * (1.0 - beta), 0.0)).astype(BF16)
        acc_ref[0:1, :] += jnp.sum(jnp.where(lane < 4, d * g, 0.0), axis=0, keepdims=True)
        acc_ref[1:2, :] += jnp.sum(da, axis=0, keepdims=True)

    row = BS((1, 128), lambda b: (0, 0))
    return pl.pallas_call(
        body, name="gdn_gate_bwd", grid=(B,),
        in_specs=[BS((S, 128), lambda b: (b, 768 // 128)), BS((S, 128), lambda b: (b, 0)), row, row],
        out_specs=[BS((S, 128), lambda b: (b, 0)), BS((8, 128), lambda b: (0, 0))],
        out_shape=[jax.ShapeDtypeStruct((T, 128), BF16), jax.ShapeDtypeStruct((8, 128), F32)],
        compiler_params=_arb(1))(P, dGB, alog_row, dt_row)


def _chunk_masks(nc):
    r = lax.broadcasted_iota(jnp.int32, (nc, CHUNK, CHUNK), 1)
    c = lax.broadcasted_iota(jnp.int32, (nc, CHUNK, CHUNK), 2)
    return r >= c, r > c


def _chunk_local(q, k, gc, gr, beta, incl, strict):
    decay = jnp.exp(jnp.where(incl, gc - gr, NEG))
    kb = k * beta
    kbf = k.astype(BF16)
    m_kk = _bdot("gcd,gjd->gcj", kb.astype(BF16), kbf)
    l_mat = jnp.where(strict, m_kk * decay, 0.0)
    a_mat = _bdot("gcd,gjd->gcj", q.astype(BF16), kbf) * decay
    return decay, kb, l_mat, a_mat


def gdn_chunk_fwd(qkv, GB, Grow, B, S, nc=8):
    T = B * S
    N = S // CHUNK
    nc = min(nc, N)
    nb = N // nc
    R = nc * CHUNK
    hi = lax.Precision.HIGHEST

    def body(q_ref, k_ref, v_ref, gb_ref, gr_ref, u_ref, w_ref, t_ref, a_ref):
        incl, strict = _chunk_masks(nc)
        eye = (lax.broadcasted_iota(jnp.int32, (nc, CHUNK, CHUNK), 1)
               == lax.broadcasted_iota(jnp.int32, (nc, CHUNK, CHUNK), 2)).astype(F32)
        for h in range(N_HEADS):
            sl = slice(h * 128, (h + 1) * 128)
            q = q_ref[:, sl].reshape(nc, CHUNK, 128)
            k = k_ref[:, sl].reshape(nc, CHUNK, 128)
            v = v_ref[:, sl].reshape(nc, CHUNK, 128)
            gc = gb_ref[:, h:h + 1].reshape(nc, CHUNK, 1)
            beta = gb_ref[:, 4 + h:5 + h].reshape(nc, CHUNK, 1)
            gr = gr_ref[h][:, None, :]
            _, kb, l_mat, a_mat = _chunk_local(q, k, gc, gr, beta, incl, strict)
            pw = -l_mat
            tinv = eye + pw
            for _ in range(5):
                pw = _bdot("gij,gjk->gik", pw, pw, hi)
                tinv = tinv + _bdot("gij,gjk->gik", tinv, pw, hi)
            tb = tinv.astype(BF16)
            u = _bdot("gcj,gjv->gcv", tb, (v * beta).astype(BF16))
            w = _bdot("gcj,gjk->gck", tb, (kb * jnp.exp(gc)).astype(BF16))
            u_ref[:, sl] = u.reshape(R, 128)
            w_ref[:, sl] = w.reshape(R, 128)
            t_ref[h] = tinv
            a_ref[h] = a_mat

    rowb = lambda c, j: BS((R, c), lambda b, n: (b * nb + n, j))
    mat = BS((None, N_HEADS, nc, CHUNK, CHUNK), lambda b, n: (b, 0, n, 0, 0))
    return pl.pallas_call(
        body, name="gdn_chunk_fwd", grid=(B, nb),
        in_specs=[rowb(512, 0), rowb(512, 1), rowb(512, 2), rowb(128, 0),
                  BS((None, N_HEADS, nc, CHUNK), lambda b, n: (b, 0, n, 0))],
        out_specs=[rowb(512, 0), rowb(512, 0), mat, mat],
        out_shape=[jax.ShapeDtypeStruct((T, 512), F32), jax.ShapeDtypeStruct((T, 512), F32),
                   jax.ShapeDtypeStruct((B, N_HEADS, N, CHUNK, CHUNK), F32),
                   jax.ShapeDtypeStruct((B, N_HEADS, N, CHUNK, CHUNK), F32)],
        compiler_params=_arb(2))(qkv, qkv, qkv, GB, Grow)


def gdn_scan_fwd(qkv3, U3, W3, GB3, A, B, S):
    N = S // CHUNK

    def body(q_ref, k_ref, u_ref, w_ref, gb_ref, a_ref, o_ref, vn_ref, st_ref, s_s):
        @pl.when(pl.program_id(0) == 0)
        def _():
            s_s[...] = jnp.zeros_like(s_s)

        for b in range(B):
            for h in range(N_HEADS):
                sl = slice(h * 128, (h + 1) * 128)
                st = s_s[b, h]
                st_ref[b, h] = st
                stb = st.astype(BF16)
                g = gb_ref[b, :, h:h + 1]
                gl = g[CHUNK - 1:CHUNK, :]
                vn = u_ref[b, :, sl] - _dot(w_ref[b, :, sl].astype(BF16), stb, NN)
                vnb = vn.astype(BF16)
                o = _dot((q_ref[b, :, sl] * jnp.exp(g)).astype(BF16), stb, NN) + _dot(a_ref[b, h].astype(BF16), vnb, NN)
                vn_ref[b, :, sl] = vn
                o_ref[b, :, sl] = o
                s_s[b, h] = st * jnp.exp(gl) + _dot((k_ref[b, :, sl] * jnp.exp(gl - g)).astype(BF16), vnb, TN)

    tok = lambda c, j: BS((B, CHUNK, c), lambda n: (0, n, j))
    return pl.pallas_call(
        body, name="gdn_scan_fwd", grid=(N,),
        in_specs=[tok(512, 0), tok(512, 1), tok(512, 0), tok(512, 0), tok(128, 0),
                  BS((B, N_HEADS, None, CHUNK, CHUNK), lambda n: (0, 0, n, 0, 0))],
        out_specs=[tok(512, 0), tok(512, 0), BS((B, N_HEADS, None, 128, 128), lambda n: (0, 0, n, 0, 0))],
        out_shape=[jax.ShapeDtypeStruct((B, S, 512), F32), jax.ShapeDtypeStruct((B, S, 512), F32),
                   jax.ShapeDtypeStruct((B, N_HEADS, N, 128, 128), F32)],
        scratch_shapes=[pltpu.VMEM((B, N_HEADS, 128, 128), F32)],
        compiler_params=_arb(1))(qkv3, qkv3, U3, W3, GB3, A)


def gdn_scan_bwd(dO3, qkv3, W3, Vn3, GB3, A, St, B, S):
    N = S // CHUNK

    def body(do_ref, q_ref, k_ref, w_ref, vn_ref, gb_ref, a_ref, st_ref,
             du_ref, dw_ref, dq_ref, dk_ref, da_ref, dg_ref, ds_s):
        @pl.when(pl.program_id(0) == 0)
        def _():
            ds_s[...] = jnp.zeros_like(ds_s)

        lane = lax.broadcasted_iota(jnp.int32, (1, 128), 1)
        last = lax.broadcasted_iota(jnp.int32, (CHUNK, 1), 0) == CHUNK - 1
        for b in range(B):
            dg_all = jnp.zeros((CHUNK, 128), F32)
            for h in range(N_HEADS):
                sl = slice(h * 128, (h + 1) * 128)
                st = st_ref[b, h]
                stb = st.astype(BF16)
                dsn = ds_s[b, h]
                dsnb = dsn.astype(BF16)
                g = gb_ref[b, :, h:h + 1]
                gl = g[CHUNK - 1:CHUNK, :]
                egl = jnp.exp(gl)
                ekd = jnp.exp(gl - g)
                eg = jnp.exp(g)
                q, k = q_ref[b, :, sl], k_ref[b, :, sl]
                kd = k * ekd
                qg = q * eg
                do = do_ref[b, :, sl].astype(BF16)
                vnb = vn_ref[b, :, sl].astype(BF16)
                dvn = _dot(a_ref[b, h].astype(BF16), do, TN) + _dot(kd.astype(BF16), dsnb, NN)
                dvnb = dvn.astype(BF16)
                da_ref[b, h] = _dot(do, vnb, NT)
                dqg = _dot(do, stb, NT)
                dkd = _dot(vnb, dsnb, NT)
                ds_s[b, h] = (_dot(qg.astype(BF16), do, TN) + egl * dsn - _dot(w_ref[b, :, sl].astype(BF16), dvnb, TN))
                du_ref[b, :, sl] = dvn
                dw_ref[b, :, sl] = -_dot(dvnb, stb, NT)
                dq_ref[b, :, sl] = dqg * eg
                dk_ref[b, :, sl] = dkd * ekd
                ddel = jnp.sum(dkd * kd, axis=1, keepdims=True)
                dgl = jnp.sum(ddel, axis=0, keepdims=True) + jnp.sum(jnp.sum(st * dsn, axis=1, keepdims=True), axis=0, keepdims=True) * egl
                col = jnp.sum(dqg * qg, axis=1, keepdims=True) - ddel + jnp.where(last, dgl, 0.0)
                dg_all = jnp.where(lane == h, col, dg_all)
            dg_ref[b] = dg_all

    tok = lambda c, j: BS((B, CHUNK, c), lambda n: (0, N - 1 - n, j))
    mat = lambda d: BS((B, N_HEADS, None, d, d), lambda n: (0, 0, N - 1 - n, 0, 0))
    return pl.pallas_call(
        body, name="gdn_scan_bwd", grid=(N,),
        in_specs=[tok(512, 0), tok(512, 0), tok(512, 1), tok(512, 0), tok(512, 0), tok(128, 0), mat(CHUNK), mat(128)],
        out_specs=[tok(512, 0), tok(512, 0), tok(512, 0), tok(512, 0), mat(CHUNK), tok(128, 0)],
        out_shape=[jax.ShapeDtypeStruct((B, S, 512), F32)] * 4
        + [jax.ShapeDtypeStruct((B, N_HEADS, N, CHUNK, CHUNK), F32), jax.ShapeDtypeStruct((B, S, 128), F32)],
        scratch_shapes=[pltpu.VMEM((B, N_HEADS, 128, 128), F32)],
        compiler_params=_arb(1))(dO3, qkv3, qkv3, W3, Vn3, GB3, A, St)


def gdn_chunk_bwd(qkv, GB, Grow, Tinv, dA, dU, dW, dQ1, dK1, dG1, B, S, nc=8):
    T = B * S
    N = S // CHUNK
    nc = min(nc, N)
    nb = N // nc
    R = nc * CHUNK

    def body(q_ref, k_ref, v_ref, gb_ref, gr_ref, t_ref, da_ref, du_ref, dw_ref, dq1_ref, dk1_ref, dg1_ref, o_ref, dgb_ref):
        incl, strict = _chunk_masks(nc)
        lane = lax.broadcasted_iota(jnp.int32, (1, 128), 1)
        dg_all = dg1_ref[...]
        db_all = jnp.zeros((R, 128), F32)
        for h in range(N_HEADS):
            sl = slice(h * 128, (h + 1) * 128)
            q = q_ref[:, sl].reshape(nc, CHUNK, 128)
            k = k_ref[:, sl].reshape(nc, CHUNK, 128)
            v = v_ref[:, sl].reshape(nc, CHUNK, 128)
            gc = gb_ref[:, h:h + 1].reshape(nc, CHUNK, 1)
            beta = gb_ref[:, 4 + h:5 + h].reshape(nc, CHUNK, 1)
            gr = gr_ref[h][:, None, :]
            decay, kb, l_mat, a_mat = _chunk_local(q, k, gc, gr, beta, incl, strict)
            eg = jnp.exp(gc)
            kbg = kb * eg
            vb = v * beta
            tb = t_ref[h].astype(BF16)
            du = du_ref[:, sl].reshape(nc, CHUNK, 128).astype(BF16)
            dw = dw_ref[:, sl].reshape(nc, CHUNK, 128).astype(BF16)
            dvb = _bdot("gcj,gcv->gjv", tb, du)
            dkbg = _bdot("gcj,gck->gjk", tb, dw)
            dt = _bdot("gcv,gjv->gcj", du, vb.astype(BF16)) + _bdot("gck,gjk->gcj", dw, kbg.astype(BF16))
            tmp = _bdot("gac,gab->gcb", tb, dt.astype(BF16))
            dl = jnp.where(strict, -_bdot("gcb,gdb->gcd", tmp.astype(BF16), tb), 0.0)
            da = da_ref[h]
            dm = (dl * decay).astype(BF16)
            dqk = (da * decay).astype(BF16)
            kbf = k.astype(BF16)
            dkb = _bdot("gcj,gjd->gcd", dm, kbf) + dkbg * eg
            dk = (_bdot("gcj,gcd->gjd", dm, kb.astype(BF16)) + _bdot("gcj,gcd->gjd", dqk, q.astype(BF16))
                  + dk1_ref[:, sl].reshape(nc, CHUNK, 128) + dkb * beta)
            dq = _bdot("gcj,gjd->gcd", dqk, kbf) + dq1_ref[:, sl].reshape(nc, CHUNK, 128)
            e = dl * l_mat + da * a_mat
            dgc = (jnp.sum(e, axis=2, keepdims=True) - jnp.sum(jnp.swapaxes(e, 1, 2), axis=2, keepdims=True)
                   + jnp.sum(dkbg * kbg, axis=2, keepdims=True))
            dbeta = jnp.sum(dkb * k, axis=2, keepdims=True) + jnp.sum(dvb * v, axis=2, keepdims=True)
            o_ref[:, sl] = dq.reshape(R, 128)
            o_ref[:, 512 + h * 128:512 + (h + 1) * 128] = dk.reshape(R, 128)
            o_ref[:, 1024 + h * 128:1024 + (h + 1) * 128] = (dvb * beta).reshape(R, 128)
            dg_all = dg_all + jnp.where(lane == h, dgc.reshape(R, 1), 0.0)
            db_all = jnp.where(lane == 4 + h, dbeta.reshape(R, 1), db_all)
        t = _chunk_row(R)
        for s in (1, 2, 4, 8, 16, 32):
            dg_all = dg_all + jnp.where(t + s < CHUNK, pltpu.roll(dg_all, R - s, 0), 0.0)
        dgb_ref[...] = jnp.where(lane < 4, dg_all, db_all)

    rowb = lambda c, j: BS((R, c), lambda b, n: (b * nb + n, j))
    mat = BS((None, N_HEADS, nc, CHUNK, CHUNK), lambda b, n: (b, 0, n, 0, 0))
    return pl.pallas_call(
        body, name="gdn_chunk_bwd", grid=(B, nb),
        in_specs=[rowb(512, 0), rowb(512, 1), rowb(512, 2), rowb(128, 0),
                  BS((None, N_HEADS, nc, CHUNK), lambda b, n: (b, 0, n, 0)), mat, mat,
                  rowb(512, 0), rowb(512, 0), rowb(512, 0), rowb(512, 0), rowb(128, 0)],
        out_specs=[rowb(GDN_QKV, 0), rowb(128, 0)],
        out_shape=[jax.ShapeDtypeStruct((T, GDN_QKV), F32), jax.ShapeDtypeStruct((T, 128), F32)],
        compiler_params=_arb(2))(qkv, qkv, qkv, GB, Grow, Tinv, dA, dU, dW, dQ1, dK1, dG1)


def _gdn_out_norm(og, gg):
    outs, xhs, rs = [], [], []
    for h in range(N_HEADS):
        seg = og[:, h * 128:(h + 1) * 128]
        r = lax.rsqrt(jnp.mean(seg * seg, axis=-1, keepdims=True) + EPS)
        xh = seg * r
        outs.append(xh * gg)
        xhs.append(xh)
        rs.append(r)
    return outs, xhs, rs


def merge_fwd(o_mla, o_gdn, o_mem, P, x, tgt, w_out, g_gdn, g_fin, tm=256):
    T = x.shape[0]
    tm = min(tm, T)

    def body(om_ref, og_ref, oc_ref, gate_ref, x_ref, t_ref, w_ref, gg_ref, gf_ref, mix_ref, dx_ref, sq_ref, gnf_ref):
        @pl.when(pl.program_id(0) == 0)
        def _():
            sq_ref[...] = jnp.zeros_like(sq_ref)
            gnf_ref[...] = jnp.zeros_like(gnf_ref)

        ogn, _, _ = _gdn_out_norm(og_ref[...], gg_ref[...])
        cat = jnp.concatenate([om_ref[...]] + ogn + [oc_ref[...]], axis=1)
        gt = gate_ref[...]
        mixed = (cat * (gt * _sigmoid(gt))).astype(BF16)
        mix_ref[...] = mixed
        x2 = x_ref[...] + _dot(mixed, w_ref[...], NN)
        r2 = lax.rsqrt(jnp.mean(x2 * x2, axis=-1, keepdims=True) + EPS)
        xh = x2 * r2
        gf = gf_ref[...]
        diff = xh * gf - t_ref[...]
        sq_ref[...] += jnp.sum(diff * diff, axis=0, keepdims=True)
        dy = diff * (1.0 / D_MODEL)
        gnf_ref[...] += jnp.sum(dy * xh, axis=0, keepdims=True)
        dxh = dy * gf
        dx_ref[...] = r2 * (dxh - xh * jnp.mean(dxh * xh, axis=-1, keepdims=True))

    rowb = lambda c, j=0: BS((tm, c), lambda i: (i, j))
    full = lambda r, c: BS((r, c), lambda i: (0, 0))
    return pl.pallas_call(
        body, name="merge_fwd", grid=(T // tm,),
        in_specs=[rowb(512), rowb(512), rowb(512), rowb(D_MIX, OFF_GATE // D_MIX), rowb(D_MODEL), rowb(D_MODEL),
                  full(D_MIX, D_MODEL), full(1, 128), full(1, D_MODEL)],
        out_specs=[rowb(D_MIX), rowb(D_MODEL), full(1, D_MODEL), full(1, D_MODEL)],
        out_shape=[jax.ShapeDtypeStruct((T, D_MIX), BF16), jax.ShapeDtypeStruct((T, D_MODEL), F32),
                   jax.ShapeDtypeStruct((1, D_MODEL), F32), jax.ShapeDtypeStruct((1, D_MODEL), F32)],
        compiler_params=_arb(1))(o_mla, o_gdn, o_mem, P, x, tgt, w_out, g_gdn, g_fin)


def merge_bwd(dx2, o_mla, o_gdn, o_mem, P, w_out, g_gdn, tm=256):
    T = dx2.shape[0]
    tm = min(tm, T)

    def body(dx_ref, om_ref, og_ref, oc_ref, gate_ref, w_ref, gg_ref, dgate_ref, dom_ref, dog_ref, doc_ref, ggn_ref):
        @pl.when(pl.program_id(0) == 0)
        def _():
            ggn_ref[...] = jnp.zeros_like(ggn_ref)

        gg = gg_ref[...]
        dmix = _dot(dx_ref[...].astype(BF16), w_ref[...], NT)
        ogn, xhs, rs = _gdn_out_norm(og_ref[...], gg)
        cat = jnp.concatenate([om_ref[...]] + ogn + [oc_ref[...]], axis=1)
        gt = gate_ref[...]
        sg = _sigmoid(gt)
        dgate_ref[...] = (dmix * cat * (sg * (1.0 + gt * (1.0 - sg)))).astype(BF16)
        dcat = dmix * (gt * sg)
        dom_ref[...] = dcat[:, :512]
        doc_ref[...] = dcat[:, 1024:]
        acc = jnp.zeros((1, 128), F32)
        for h in range(N_HEADS):
            dseg = dcat[:, 512 + h * 128:512 + (h + 1) * 128]
            acc = acc + jnp.sum(dseg * xhs[h], axis=0, keepdims=True)
            dxh = dseg * gg
            dog_ref[:, h * 128:(h + 1) * 128] = rs[h] * (dxh - xhs[h] * jnp.mean(dxh * xhs[h], axis=-1, keepdims=True))
        ggn_ref[...] += acc

    rowb = lambda c, j=0: BS((tm, c), lambda i: (i, j))
    full = lambda r, c: BS((r, c), lambda i: (0, 0))
    return pl.pallas_call(
        body, name="merge_bwd", grid=(T // tm,),
        in_specs=[rowb(D_MODEL), rowb(512), rowb(512), rowb(512), rowb(D_MIX, OFF_GATE // D_MIX),
                  full(D_MIX, D_MODEL), full(1, 128)],
        out_specs=[rowb(D_MIX), rowb(512), rowb(512), rowb(512), full(1, 128)],
        out_shape=[jax.ShapeDtypeStruct((T, D_MIX), BF16)] + [jax.ShapeDtypeStruct((T, 512), F32)] * 3
        + [jax.ShapeDtypeStruct((1, 128), F32)],
        compiler_params=_arb(1))(dx2, o_mla, o_gdn, o_mem, P, w_out, g_gdn)


def in_norm_bwd(x, dh, dx2, gain, tm=256):
    T, n = x.shape
    tm = min(tm, T)

    def body(x_ref, dh_ref, dx2_ref, g_ref, o_ref, acc_ref):
        @pl.when(pl.program_id(0) == 0)
        def _():
            acc_ref[...] = jnp.zeros_like(acc_ref)

        xv = x_ref[...]
        r = lax.rsqrt(jnp.mean(xv * xv, axis=-1, keepdims=True) + EPS)
        xh = xv * r
        dy = dh_ref[...]
        acc_ref[...] += jnp.sum(dy * xh, axis=0, keepdims=True)
        dxh = dy * g_ref[...]
        o_ref[...] = dx2_ref[...] + r * (dxh - xh * jnp.mean(dxh * xh, axis=-1, keepdims=True))

    rowb = BS((tm, n), lambda i: (i, 0))
    full = BS((1, n), lambda i: (0, 0))
    return pl.pallas_call(
        body, name="in_norm_bwd", grid=(T // tm,),
        in_specs=[rowb, rowb, rowb, full], out_specs=[rowb, full],
        out_shape=[jax.ShapeDtypeStruct((T, n), F32), jax.ShapeDtypeStruct((1, n), F32)],
        compiler_params=_arb(1))(x, dh, dx2, gain)


W_IN_SHARD = D_IN // 4
_GDN0 = Q_LORA + KV_LORA + MLA_ROPE
_AB0 = _GDN0 + GDN_QKV
_MEMQ0 = _AB0 + 2 * N_HEADS
_GATE0 = _MEMQ0 + N_HEADS * MEM_DH


def _pad_w_in_t(s):
    z = lambda n: jnp.zeros((n, s[0].shape[1]), s[0].dtype)
    a, m, gt = _AB0 - 2 * W_IN_SHARD, _MEMQ0 - 2 * W_IN_SHARD, _GATE0 - 2 * W_IN_SHARD
    return jnp.concatenate([s[0][:640], s[0][640:672], z(32), s[0][672:704], z(32), s[2][a:m], z(248), s[2][m:gt],
                            s[0][_GDN0:], s[1], s[2][:a], s[2][gt:], s[3]], axis=0)


def _unpad_w_in_t(g):
    e0 = OFF_GDN + W_IN_SHARD - _GDN0
    e1 = e0 + W_IN_SHARD
    gt = OFF_GATE + 3 * W_IN_SHARD - _GATE0
    return jnp.stack([jnp.concatenate([g[0:640], g[640:672], g[704:736], g[OFF_GDN:e0]]), g[e0:e1],
                      jnp.concatenate([g[e1:OFF_GATE], g[768:776], g[OFF_MEMQ:OFF_MEMQ + 512], g[OFF_GATE:gt]]), g[gt:]])


def _pad_w_q_b_t(s):
    z = jnp.zeros((32, s.shape[2]), s.dtype)
    parts = []
    for h in range(N_HEADS):
        parts += [s[h, :128], s[h, 128:160], z, s[h, 160:192], z]
    return jnp.concatenate(parts, axis=0)


def _unpad_w_q_b_t(g):
    return jnp.stack([jnp.concatenate([g[h * HEAD_PAD:h * HEAD_PAD + 128], g[h * HEAD_PAD + 128:h * HEAD_PAD + 160],
                                       g[h * HEAD_PAD + 192:h * HEAD_PAD + 224]]) for h in range(N_HEADS)])


def _perm_w_kv_b(s):
    return jnp.concatenate([s[h, :, :128] for h in range(N_HEADS)] + [s[h, :, 128:] for h in range(N_HEADS)], axis=1)


def _unperm_w_kv_b(g):
    return jnp.stack([jnp.concatenate([g[:, h * 128:(h + 1) * 128], g[:, 512 + h * 128:512 + (h + 1) * 128]], axis=1)
                      for h in range(N_HEADS)])


def _lane_row(v4):
    return jnp.pad(v4.reshape(1, -1).astype(F32), ((0, 0), (0, 128 - v4.size)))


def _pack(pieces, n_rows):
    flat = jnp.concatenate([p.reshape(-1) for p in pieces])
    return jnp.pad(flat, (0, n_rows * 1024 - flat.size)).reshape(n_rows, 1024)


def _unpack(block, shapes):
    flat = block.reshape(-1)
    out, off = [], 0
    for shp in shapes:
        n = int(np.prod(shp))
        out.append(flat[off:off + n].reshape(shp))
        off += n
    return out


N_CHIPS = 4
MESH = pl.DeviceIdType.MESH
ANY = BS(memory_space=pl.ANY)


def _place():
    return lax.axis_index("x"), lax.axis_index("y"), lax.axis_index("c")


def _other_chips(x, y):
    return [(1 - x, y), (x, 1 - y), (1 - x, 1 - y)]


def _half(split, which):
    axis, size = split
    ds = pl.ds(pl.multiple_of(which * size, 16 if axis == 0 else 128), size)
    return (ds, slice(None)) if axis == 0 else (slice(None), ds)


def allgather_chips(shards, splits, name):
    n = len(shards)

    def body(*refs):
        s_refs, o_refs = refs[:n], refs[n:2 * n]
        send_sems, recv_sems, local_sems = refs[2 * n:]
        x, y, c = _place()
        chips = _other_chips(x, y)

        def copy(k, src, dst, to):
            return pltpu.make_async_remote_copy(src_ref=src, dst_ref=dst, send_sem=send_sems.at[k], recv_sem=recv_sems.at[k],
                                                device_id=to, device_id_type=MESH)

        owns, first, passed = [], [], []
        for i, (s_ref, o_ref) in enumerate(zip(s_refs, o_refs)):
            mine = _half(splits[i], c)
            owns.append(pltpu.make_async_copy(s_ref, o_ref.at[2 * x + y], local_sems.at[i]))
            owns[-1].start()
            for j, (px, py) in enumerate(chips):
                first.append(copy(6 * i + j, s_ref.at[mine], o_ref.at[(2 * x + y,) + mine], (px, py, c)))
                first[-1].start()
        for i, (s_ref, o_ref) in enumerate(zip(s_refs, o_refs)):
            mine = _half(splits[i], c)
            for j, (px, py) in enumerate(chips):
                landed = o_ref.at[(2 * px + py,) + mine]
                copy(6 * i + j, s_ref.at[mine], landed, (px, py, c)).wait_recv()
                passed.append(copy(6 * i + 3 + j, landed, landed, (x, y, 1 - c)))
                passed[-1].start()
        for i, (s_ref, o_ref) in enumerate(zip(s_refs, o_refs)):
            theirs = _half(splits[i], 1 - c)
            for j, (px, py) in enumerate(chips):
                copy(6 * i + 3 + j, s_ref.at[theirs], o_ref.at[(2 * px + py,) + theirs], (x, y, 1 - c)).wait_recv()
        for cp in first + passed:
            cp.wait_send()
        for cp in owns:
            cp.wait()

    return pl.pallas_call(
        body, name=name, in_specs=[ANY] * n, out_specs=[ANY] * n,
        out_shape=[jax.ShapeDtypeStruct((N_CHIPS,) + s.shape, s.dtype) for s in shards],
        scratch_shapes=[pltpu.SemaphoreType.DMA((6 * n,)), pltpu.SemaphoreType.DMA((6 * n,)), pltpu.SemaphoreType.DMA((n,))])(*shards)


def allgather_devices(block, name):
    R, C = block.shape

    def body(b_ref, o_ref, send_sems, recv_sems, local_sem):
        x, y, c = _place()
        me = 4 * x + 2 * y + c
        own = pltpu.make_async_copy(b_ref, o_ref.at[me], local_sem)
        own.start()
        copies = []
        for r in range(1, 8):
            px = 1 - x if r & 4 else x
            py = 1 - y if r & 2 else y
            pc = 1 - c if r & 1 else c
            send = pltpu.make_async_remote_copy(src_ref=b_ref, dst_ref=o_ref.at[me], send_sem=send_sems.at[r - 1],
                                                recv_sem=recv_sems.at[r - 1], device_id=(px, py, pc), device_id_type=MESH)
            recv = pltpu.make_async_remote_copy(src_ref=b_ref, dst_ref=o_ref.at[4 * px + 2 * py + pc], send_sem=send_sems.at[r - 1],
                                                recv_sem=recv_sems.at[r - 1], device_id=(px, py, pc), device_id_type=MESH)
            send.start()
            copies.append((send, recv))
        for send, recv in copies:
            recv.wait_recv()
            send.wait_send()
        own.wait()

    return pl.pallas_call(
        body, name=name, in_specs=[ANY], out_specs=ANY, out_shape=jax.ShapeDtypeStruct((8, R, C), block.dtype),
        scratch_shapes=[pltpu.SemaphoreType.DMA((7,)), pltpu.SemaphoreType.DMA((7,)), pltpu.SemaphoreType.DMA(())])(block)


def swap_sibling(arrs, name, splits=None):
    n = len(arrs)

    def sent(a_ref, i, c):
        return a_ref if splits is None else a_ref.at[(slice(None),) + _half(splits[i], 1 - c)]

    def out_shape(a, i):
        if splits is None:
            return a.shape
        axis, size = splits[i]
        return (a.shape[0], size, a.shape[2]) if axis == 0 else (a.shape[0], a.shape[1], size)

    def body(*refs):
        a_refs, o_refs = refs[:n], refs[n:2 * n]
        send_sems, recv_sems = refs[2 * n:]
        x, y, c = _place()
        copies = [pltpu.make_async_remote_copy(src_ref=sent(a_ref, i, c), dst_ref=o_ref, send_sem=send_sems.at[i],
                                               recv_sem=recv_sems.at[i], device_id=(x, y, 1 - c), device_id_type=MESH)
                  for i, (a_ref, o_ref) in enumerate(zip(a_refs, o_refs))]
        for cp in copies:
            cp.start()
        for cp in copies:
            cp.wait()

    return pl.pallas_call(
        body, name=name, in_specs=[ANY] * n, out_specs=[ANY] * n,
        out_shape=[jax.ShapeDtypeStruct(out_shape(a, i), a.dtype) for i, a in enumerate(arrs)],
        scratch_shapes=[pltpu.SemaphoreType.DMA((n,)), pltpu.SemaphoreType.DMA((n,))])(*arrs)


def exchange_chips(parts, name):
    n = len(parts)

    def body(*refs):
        p_refs, o_refs = refs[:n], refs[n:2 * n]
        send_sems, recv_sems = refs[2 * n:]
        x, y, c = _place()
        copies = [pltpu.make_async_remote_copy(src_ref=p_ref.at[2 * px + py], dst_ref=o_ref.at[j], send_sem=send_sems.at[3 * i + j],
                                               recv_sem=recv_sems.at[3 * i + j], device_id=(px, py, c), device_id_type=MESH)
                  for i, (p_ref, o_ref) in enumerate(zip(p_refs, o_refs)) for j, (px, py) in enumerate(_other_chips(x, y))]
        for cp in copies:
            cp.start()
        for cp in copies:
            cp.wait()

    return pl.pallas_call(
        body, name=name, in_specs=[ANY] * n, out_specs=[ANY] * n,
        out_shape=[jax.ShapeDtypeStruct((3,) + p.shape[1:], p.dtype) for p in parts],
        scratch_shapes=[pltpu.SemaphoreType.DMA((3 * n,)), pltpu.SemaphoreType.DMA((3 * n,))])(*parts)


def _half_block(shape2, split):
    axis, size = split
    return (size, shape2[1]) if axis == 0 else (shape2[0], size)


def add_pairs(parts, halves, splits, core, name):
    n = len(parts)

    def body(s_ref, *refs):
        for a_ref, b_ref, o_ref in zip(refs[:n], refs[n:2 * n], refs[2 * n:]):
            o_ref[...] = (a_ref[...].astype(F32) + b_ref[...].astype(F32)).astype(BF16)

    def mine(i):
        blk = (None,) + _half_block(parts[i].shape[1:], splits[i])
        if splits[i][0] == 0:
            return BS(blk, lambda q, s: (q, s[0], 0))
        return BS(blk, lambda q, s: (q, 0, s[0]))

    half_specs = [BS((None,) + h.shape[1:], lambda q, s: (q, 0, 0)) for h in halves]
    return pl.pallas_call(
        body, name=name,
        grid_spec=pltpu.PrefetchScalarGridSpec(num_scalar_prefetch=1, grid=(N_CHIPS,),
                                               in_specs=[mine(i) for i in range(n)] + half_specs, out_specs=half_specs),
        out_shape=[jax.ShapeDtypeStruct(h.shape, BF16) for h in halves], compiler_params=_arb(1))(core, *parts, *halves)


def add_fives(parts, halves, from_chips, splits, chip_core, name):
    n = len(parts)

    def body(s_ref, *refs):
        for a_ref, b_ref, p_ref, o_ref in zip(refs[:n], refs[n:2 * n], refs[2 * n:3 * n], refs[3 * n:]):
            s = a_ref[...].astype(F32) + b_ref[...].astype(F32)
            for j in range(3):
                s = s + p_ref[j].astype(F32)
            o_ref[...] = s

    def mine(i):
        blk = (None,) + _half_block(parts[i].shape[1:], splits[i])
        if splits[i][0] == 0:
            return BS(blk, lambda g, s: (s[0], s[1], 0))
        return BS(blk, lambda g, s: (s[0], 0, s[1]))

    half_specs = [BS((None,) + h.shape[1:], lambda g, s: (s[0], 0, 0)) for h in halves]
    chip_specs = [BS(p.shape, lambda g, s: (0, 0, 0)) for p in from_chips]
    out_specs = [BS(h.shape[1:], lambda g, s: (0, 0)) for h in halves]
    return pl.pallas_call(
        body, name=name,
        grid_spec=pltpu.PrefetchScalarGridSpec(num_scalar_prefetch=1, grid=(1,),
                                               in_specs=[mine(i) for i in range(n)] + half_specs + chip_specs, out_specs=out_specs),
        out_shape=[jax.ShapeDtypeStruct(h.shape[1:], F32) for h in halves], compiler_params=_arb(1))(chip_core, *parts, *halves, *from_chips)


def sum_leading(a, name):
    def body(a_ref, o_ref):
        s = a_ref[0]
        for j in range(1, a.shape[0]):
            s = s + a_ref[j]
        o_ref[...] = s

    return pl.pallas_call(body, name=name, out_shape=jax.ShapeDtypeStruct(a.shape[1:], a.dtype))(a)


def _adamw_math(w, g, m, v):
    mn = ADAM_B1 * m + (1.0 - ADAM_B1) * g
    vn = ADAM_B2 * v + (1.0 - ADAM_B2) * (g * g)
    m_hat = mn / (1.0 - ADAM_B1 ** ADAM_STEP)
    v_hat = vn / (1.0 - ADAM_B2 ** ADAM_STEP)
    return -ADAM_LR * (m_hat / (jnp.sqrt(v_hat) + ADAM_EPS) + ADAM_WD * w), mn, vn


def adamw(w, g, m, v, name):
    R, C = g.shape
    lead = (None,) * (w.ndim - 2)

    def body(w_ref, g_ref, m_ref, v_ref, d_ref, mo_ref, vo_ref):
        d_ref[...], mo_ref[...], vo_ref[...] = _adamw_math(w_ref[...], g_ref[...], m_ref[...], v_ref[...])

    wblk = BS(lead + (R, C), lambda i: (0,) * w.ndim)
    gblk = BS((R, C), lambda i: (0, 0))
    return pl.pallas_call(
        body, name=name, grid=(1,), in_specs=[wblk, gblk, wblk, wblk], out_specs=[wblk] * 3,
        out_shape=[jax.ShapeDtypeStruct(w.shape, F32)] * 3, compiler_params=_arb(1))(w, g, m, v)


def adamw_halves(w, mine, other, m, v, split, core, name):
    R, C = w.shape[-2:]
    axis, size = split
    lead = (None,) * (w.ndim - 2)
    zeros = (0,) * (w.ndim - 2)
    if axis == 0:
        tr = size if size <= 256 else next(t for t in range(256, 7, -1) if size % t == 0 and t % 8 == 0)
        nb = size // tr
        whole = BS(lead + (tr, C), lambda hi, j, s: zeros + (hi * nb + j, 0))
        part = BS((tr, C), lambda hi, j, s: (j, 0))
    else:
        nb = size // 128
        whole = BS(lead + (R, 128), lambda hi, j, s: zeros + (0, hi * nb + j))
        part = BS((R, 128), lambda hi, j, s: (0, j))

    def body(s_ref, w_ref, a_ref, b_ref, m_ref, v_ref, g_ref, d_ref, mo_ref, vo_ref):
        g = jnp.where(pl.program_id(0) == s_ref[0], a_ref[...], b_ref[...])
        g_ref[...] = g
        d_ref[...], mo_ref[...], vo_ref[...] = _adamw_math(w_ref[...], g, m_ref[...], v_ref[...])

    return pl.pallas_call(
        body, name=name,
        grid_spec=pltpu.PrefetchScalarGridSpec(num_scalar_prefetch=1, grid=(2, nb),
                                               in_specs=[whole, part, part, whole, whole], out_specs=[whole] * 4),
        out_shape=[jax.ShapeDtypeStruct(w.shape, F32)] * 4, compiler_params=_arb(2))(core, w, mine, other, m, v)


def dense_bf16(w3, name):
    R, _, K = w3.shape

    def body(w_hbm, o_ref, buf, sem):
        cp = pltpu.make_async_copy(w_hbm.at[:, 0], buf, sem)
        cp.start()
        cp.wait()
        o_ref[...] = buf[...].astype(BF16)

    return pl.pallas_call(
        body, name=name, in_specs=[ANY], out_specs=BS(memory_space=pltpu.VMEM), out_shape=jax.ShapeDtypeStruct((R, K), BF16),
        scratch_shapes=[pltpu.VMEM((R, K), F32), pltpu.SemaphoreType.DMA(())])(w3)


ROW_BLOCK = 184


def adamw_untiled_rows(w3, mine, other, m3, v3, name):
    R, _, K = w3.shape
    kh = K // 2
    starts = list(range(0, R, ROW_BLOCK))
    sizes = [min(ROW_BLOCK, R - s) for s in starts]
    nblk = len(starts)

    def body(w_hbm, a_ref, b_ref, m_hbm, v_hbm, g_hbm, d_hbm, mo_hbm, vo_hbm,
             wbuf, mbuf, vbuf, gbuf, dbuf, mobuf, vobuf, in_sems, out_sems):
        first = lax.axis_index("c") == 0
        ins = []
        for k, (r0, n) in enumerate(zip(starts, sizes)):
            rows = pl.ds(r0, n)
            cps = [pltpu.make_async_copy(src.at[rows, 0], dst.at[rows], in_sems.at[3 * k + i])
                   for i, (src, dst) in enumerate(((w_hbm, wbuf), (m_hbm, mbuf), (v_hbm, vbuf)))]
            for cp in cps:
                cp.start()
            ins.append(cps)

        def update(rows):
            a, b = a_ref[rows, :], b_ref[rows, :]
            g = jnp.concatenate([jnp.where(first, a, b), jnp.where(first, b, a)], axis=1)
            gbuf[rows, :] = g
            dbuf[rows, :], mobuf[rows, :], vobuf[rows, :] = _adamw_math(wbuf[rows, :], g, mbuf[rows, :], vbuf[rows, :])

        outs = []
        for k, (r0, n) in enumerate(zip(starts, sizes)):
            for cp in ins[k]:
                cp.wait()
            groups, tail = n // 8, n % 8

            def group(i, carry, r0=r0):
                update(pl.ds(pl.multiple_of(r0 + i * 8, 8), 8))
                return carry

            lax.fori_loop(0, groups, group, 0)
            if tail:
                update(pl.ds(r0 + groups * 8, tail))
            rows = pl.ds(r0, n)
            cps = [pltpu.make_async_copy(src.at[rows], dst.at[rows, 0], out_sems.at[4 * k + i])
                   for i, (src, dst) in enumerate(((gbuf, g_hbm), (dbuf, d_hbm), (mobuf, mo_hbm), (vobuf, vo_hbm)))]
            for cp in cps:
                cp.start()
            outs += cps
        for cp in outs:
            cp.wait()

    vmem = BS(memory_space=pltpu.VMEM)
    return pl.pallas_call(
        body, name=name, in_specs=[ANY, vmem, vmem, ANY, ANY], out_specs=[ANY] * 4,
        out_shape=[jax.ShapeDtypeStruct(w3.shape, F32)] * 4,
        scratch_shapes=[pltpu.VMEM((R, K), F32)] * 7 + [pltpu.SemaphoreType.DMA((3 * nblk,)), pltpu.SemaphoreType.DMA((4 * nblk,))])(
            w3, mine, other, m3, v3)


def adamw_w_q_b(w, mine, other, m, v, name):
    def body(w_ref, a_ref, b_ref, m_ref, v_ref, g_ref, d_ref, mo_ref, vo_ref):
        first = lax.axis_index("c") == 0
        lo = jnp.where(first, a_ref[...], b_ref[...])
        hi = jnp.where(first, b_ref[...], a_ref[...])
        g = jnp.concatenate([lo, hi[0:32], hi[64:96]], axis=0)
        g_ref[...] = g
        d_ref[...], mo_ref[...], vo_ref[...] = _adamw_math(w_ref[...], g, m_ref[...], v_ref[...])

    return pl.pallas_call(body, name=name, out_shape=[jax.ShapeDtypeStruct(w.shape, F32)] * 4)(w, mine, other, m, v)


def local_step(x, mem, positions, tgt, norm_in, w_in, q_a_norm, w_q_b, kv_a_norm, w_kv_b, gdn_conv, gdn_a_log,
               gdn_dt_bias, gdn_norm, mem_norm, w_mem_kv, w_out, norm_final):
    B, S, D = x.shape
    M = mem.shape[1]
    T = B * S
    N = S // CHUNK
    x2d = x.reshape(T, D)
    mem2d = mem.reshape(B * M, D)
    tgt2d = tgt.reshape(T, D)

    wp, wq, wkv = w_in, w_q_b, w_kv_b
    alog_row, dt_row = _lane_row(gdn_a_log), _lane_row(gdn_dt_bias)

    half = MLA_ROPE // 2
    inv_freq = 1.0 / (ROPE_THETA ** (jnp.arange(half, dtype=F32) / half))
    z32 = jnp.zeros((half,), F32)
    o32 = jnp.ones((half,), F32)
    inv_row = jnp.concatenate([inv_freq, z32, inv_freq, z32]).reshape(1, 128)
    sgn_row = jnp.concatenate([-o32, z32, o32, z32]).reshape(1, 128)
    msk_row = jnp.concatenate([o32, z32, o32, z32]).reshape(1, 128)
    cos_t, sin_t = rope_tables(positions.reshape(T, 1), inv_row, sgn_row, msk_row)

    h = rms_fwd(x2d, norm_in, "rms_in")
    P = mm(h, wp, "nt", F32, "in_proj", bm=512, bn=768, bk=1024)
    Q, K, V, qn, kvn = mla_prep(P, q_a_norm, kv_a_norm, wq, wkv, cos_t, sin_t)
    o_mla, lse = mla_attn_fwd(Q, K, V, B, S)
    memn = rms_fwd(mem2d, mem_norm, "rms_mem")
    MKV = mm(memn, w_mem_kv, "nn", BF16, "mem_kv_proj", bk=1024)
    o_mem = mem_attn_fwd(P, MKV, B, S, M)
    qkv = gdn_prep_fwd(P, gdn_conv, B, S)
    GB = gdn_gate_fwd(P, alog_row, dt_row, B, S)
    Grow = jnp.transpose(GB[:, :N_HEADS].reshape(B, N, CHUNK, N_HEADS), (0, 3, 1, 2))
    U, W, Tinv, A = gdn_chunk_fwd(qkv, GB, Grow, B, S)
    qkv3, GB3 = qkv.reshape(B, S, GDN_QKV), GB.reshape(B, S, 128)
    W3 = W.reshape(B, S, 512)
    o_gdn3, Vn3, St = gdn_scan_fwd(qkv3, U.reshape(B, S, 512), W3, GB3, A, B, S)
    o_gdn = o_gdn3.reshape(T, 512)
    mixed, dx2, sq, g_norm_final = merge_fwd(o_mla, o_gdn, o_mem, P, x2d, tgt2d, w_out, gdn_norm, norm_final.reshape(1, D))

    g_w_out = mm(mixed, dx2, "tn", BF16, "grad_w_out")
    dgate, do_mla, do_gdn, do_mem, g_gdn_norm = merge_bwd(dx2, o_mla, o_gdn, o_mem, P, w_out, gdn_norm)

    dmemq, dMKV = mem_attn_bwd(P, MKV, do_mem, B, S, M)
    g_w_mem_kv = mm(memn, dMKV, "tn", BF16, "grad_w_mem_kv")
    dmemn = mm(dMKV, w_mem_kv, "nt", F32, "d_memn", bk=1024)
    g_mem_norm = gain_grad(mem2d, dmemn, "grad_mem_norm")

    dU3, dW3, dQ13, dK13, dA, dG13 = gdn_scan_bwd(do_gdn.reshape(B, S, 512), qkv3, W3, Vn3, GB3, A, St, B, S)
    r2 = lambda a: a.reshape(T, a.shape[-1])
    dqkv, dGB = gdn_chunk_bwd(qkv, GB, Grow, Tinv, dA, r2(dU3), r2(dW3), r2(dQ13), r2(dK13), r2(dG13), B, S)
    dPg, g_conv = gdn_prep_bwd(P, dqkv, gdn_conv, B, S)
    dab, g_ab = gdn_gate_bwd(P, dGB, alog_row, dt_row, B, S)

    dQ, dK, dV = mla_attn_bwd(Q, K, V, o_mla, do_mla, lse, B, S)
    dq_lin, dkv_lin, dkr = mla_post_bwd(dQ, dK, dV, cos_t, sin_t)
    dqn = mm(dq_lin, wq, "nn", F32, "d_qn", bk=1024)
    dkvn = mm(dkv_lin, wkv, "nt", F32, "d_kvn", bk=1024)
    g_wq = mm(dq_lin, qn, "tn", BF16, "grad_w_q_b")
    g_wkv = mm(kvn, dkv_lin, "tn", BF16, "grad_w_kv_b")
    dPm, g_q_a_norm, g_kv_a_norm = mla_norm_bwd(P, dqn, dkvn, dkr, dab, q_a_norm, kv_a_norm)

    dP = jnp.concatenate([dPm, dmemq, dPg, dgate], axis=1)
    g_wp = mm(dP, h, "tn", BF16, "grad_w_in")
    dh = mm(dP, wp, "nn", F32, "d_h", bn=1024, bk=768)
    grad_x, g_norm_in = in_norm_bwd(x2d, dh, dx2, norm_in)

    grads = dict(
        norm_in=g_norm_in, w_in=g_wp, q_a_norm=g_q_a_norm, w_q_b=g_wq, kv_a_norm=g_kv_a_norm, w_kv_b=g_wkv, gdn_conv=g_conv,
        gdn_a_log=g_ab[0:1, :N_HEADS], gdn_dt_bias=g_ab[1:2, :N_HEADS], gdn_norm=g_gdn_norm,
        mem_norm=g_mem_norm, w_mem_kv=g_w_mem_kv, w_out=g_w_out, norm_final=g_norm_final)
    return sq, grad_x.reshape(B, S, D), grads


def kernel(x, mem, positions, norm_in, w_in, q_a_norm, w_q_b, kv_a_norm, w_kv_b, gdn_conv, gdn_a_log, gdn_dt_bias, gdn_norm, mem_norm, w_mem_kv, w_out, norm_final, loss_target, m_norm_in, m_w_in, m_q_a_norm, m_w_q_b, m_kv_a_norm, m_w_kv_b, m_gdn_conv, m_gdn_a_log, m_gdn_dt_bias, m_gdn_norm, m_mem_norm, m_w_mem_kv, m_w_out, m_norm_final, v_norm_in, v_w_in, v_q_a_norm, v_w_q_b, v_kv_a_norm, v_w_kv_b, v_gdn_conv, v_gdn_a_log, v_gdn_dt_bias, v_gdn_norm, v_mem_norm, v_w_mem_kv, v_w_out, v_norm_final):
    B = x.shape[0]
    cx, cy, cc = lax.axis_index("x"), lax.axis_index("y"), lax.axis_index("c")
    chip = 2 * cx + cy

    big_names = ("w_in", "w_q_b", "w_kv_b", "w_mem_kv", "w_out")
    rows_major = lambda a: jnp.transpose(a, (2, 0, 1))
    w_in3, m_in3, v_in3 = rows_major(w_in), rows_major(m_w_in), rows_major(v_w_in)
    w_qb_t, m_qb_t, v_qb_t = jnp.transpose(w_q_b[0]), jnp.transpose(m_w_q_b[0]), jnp.transpose(v_w_q_b[0])
    z32 = jnp.zeros((32, Q_LORA), BF16)
    qb_bf = w_qb_t.astype(BF16)
    qb_padded = jnp.concatenate([qb_bf[:160], z32, qb_bf[160:], z32])
    shards = [dense_bf16(w_in3, "w_in_bf16"), qb_padded, w_kv_b[0].astype(BF16), w_mem_kv[0].astype(BF16), w_out[0].astype(BF16)]
    splits = [(1, D_MODEL // 2)] + [(0, s.shape[0] // 2) for s in shards[1:]]
    g_in, g_qb, g_kvb, g_mem, g_out_w = allgather_chips(shards, splits, "allgather_weights")
    conv_all = allgather_devices(_pack([gdn_conv[0]], 8), "allgather_conv")
    conv_shape = gdn_conv[0].shape
    conv_full = jnp.concatenate([_unpack(conv_all[2 * q], [conv_shape])[0] for q in range(N_CHIPS)], axis=1)

    sq, grad_x, g = local_step(x, mem, positions, loss_target, norm_in, _pad_w_in_t([g_in[q] for q in range(N_CHIPS)]), q_a_norm,
                               g_qb.reshape(-1, Q_LORA), kv_a_norm, _perm_w_kv_b(g_kvb), conv_full, gdn_a_log, gdn_dt_bias,
                               gdn_norm, mem_norm, g_mem.reshape(-1, g_mem.shape[2]), g_out_w.reshape(-1, g_out_w.shape[2]),
                               norm_final)
    loss = lax.psum(0.5 * jnp.sum(sq) / D_MODEL, ("x", "y", "c"))

    core = jnp.stack([cc]).astype(jnp.int32)
    chip_core = jnp.stack([chip, cc]).astype(jnp.int32)
    parts = [_unpad_w_in_t(g["w_in"]), g["w_q_b"].reshape(g_qb.shape), _unperm_w_kv_b(g["w_kv_b"]),
             g["w_mem_kv"].reshape(g_mem.shape), g["w_out"].reshape(g_out_w.shape)]
    from_sibling = swap_sibling(parts, "rs_sibling_partial", splits)
    chip_sums = add_pairs(parts, from_sibling, splits, core, "rs_add_sibling")
    from_chips = exchange_chips(chip_sums, "rs_exchange_chips")
    my_half = add_fives(parts, from_sibling, from_chips, splits, chip_core, "rs_add_chips")
    other_half = swap_sibling(my_half, "rs_sibling_final")

    small_names = ("norm_in", "q_a_norm", "kv_a_norm", "gdn_a_log", "gdn_dt_bias", "gdn_norm", "mem_norm", "norm_final")
    small = dict(norm_in=norm_in, q_a_norm=q_a_norm, kv_a_norm=kv_a_norm, gdn_a_log=gdn_a_log, gdn_dt_bias=gdn_dt_bias,
                 gdn_norm=gdn_norm, mem_norm=mem_norm, norm_final=norm_final)
    m_small = dict(norm_in=m_norm_in, q_a_norm=m_q_a_norm, kv_a_norm=m_kv_a_norm, gdn_a_log=m_gdn_a_log,
                   gdn_dt_bias=m_gdn_dt_bias, gdn_norm=m_gdn_norm, mem_norm=m_mem_norm, norm_final=m_norm_final)
    v_small = dict(norm_in=v_norm_in, q_a_norm=v_q_a_norm, kv_a_norm=v_kv_a_norm, gdn_a_log=v_gdn_a_log,
                   gdn_dt_bias=v_gdn_dt_bias, gdn_norm=v_gdn_norm, mem_norm=v_mem_norm, norm_final=v_norm_final)
    rows = lambda d: jnp.stack([jnp.pad(d[n].reshape(-1), (0, 1024 - d[n].size)) for n in small_names])
    conv_rows = GDN_CONV * GDN_QKV // 1024
    g_block = jnp.concatenate([rows(g), g["gdn_conv"].reshape(conv_rows, 1024), jnp.zeros((16 - 8 - conv_rows, 1024), F32)])
    g_block = sum_leading(allgather_devices(g_block, "allgather_small_grads"), "sum_small_grads")
    g_small_rows = g_block[:8]
    conv_cols = gdn_conv.shape[2]
    g_conv = lax.dynamic_slice_in_dim(g_block[8:8 + conv_rows].reshape(GDN_CONV, GDN_QKV), chip * conv_cols, conv_cols, axis=1)
    d_s, m_s, v_s = adamw(rows(small), g_small_rows, rows(m_small), rows(v_small), "adamw_small")
    unrow = lambda r: {n: r[i, :small[n].size].reshape(small[n].shape) for i, n in enumerate(small_names)}
    g_out, d_out, m_out, v_out = unrow(g_small_rows), unrow(d_s), unrow(m_s), unrow(v_s)

    d_out["gdn_conv"], m_out["gdn_conv"], v_out["gdn_conv"] = adamw(gdn_conv, g_conv, m_gdn_conv, v_gdn_conv, "adamw_gdn_conv")
    g_out["gdn_conv"] = g_conv[None]
    res = adamw_untiled_rows(w_in3, my_half[0], other_half[0], m_in3, v_in3, "adamw_w_in")
    g_out["w_in"], d_out["w_in"], m_out["w_in"], v_out["w_in"] = [jnp.transpose(r, (1, 2, 0)) for r in res]
    res = adamw_w_q_b(w_qb_t, my_half[1], other_half[1], m_qb_t, v_qb_t, "adamw_w_q_b")
    g_out["w_q_b"], d_out["w_q_b"], m_out["w_q_b"], v_out["w_q_b"] = [jnp.transpose(r)[None] for r in res]
    rest = dict(w_kv_b=(w_kv_b, m_w_kv_b, v_w_kv_b), w_mem_kv=(w_mem_kv, m_w_mem_kv, v_w_mem_kv), w_out=(w_out, m_w_out, v_w_out))
    for i, n in enumerate(big_names):
        if n in rest:
            w_n, m_n, v_n = rest[n]
            g_out[n], d_out[n], m_out[n], v_out[n] = adamw_halves(w_n, my_half[i], other_half[i], m_n, v_n, splits[i], core, "adamw_" + n)

    order = ("norm_in", "w_in", "q_a_norm", "w_q_b", "kv_a_norm", "w_kv_b", "gdn_conv", "gdn_a_log", "gdn_dt_bias",
             "gdn_norm", "mem_norm", "w_mem_kv", "w_out", "norm_final")
    return (loss, grad_x, *[g_out[n] for n in order], *[d_out[n] for n in order], *[m_out[n] for n in order],
            *[v_out[n] for n in order])
```

```python
import functools
import math

import jax
import jax.numpy as jnp
import numpy as np
from jax import lax
from jax.experimental import pallas as pl
from jax.experimental.pallas import tpu as pltpu

F32 = jnp.float32
BF16 = jnp.bfloat16
BS = pl.BlockSpec

D_MODEL = 1024
N_HEADS = 4
MLA_NOPE, MLA_ROPE, MLA_V = 128, 64, 128
Q_LORA, KV_LORA = 384, 256
ROPE_THETA = 10000.0
GDN_DK = GDN_DV = 128
GDN_CONV = 4
CHUNK = 64
MEM_DH = 128
D_MIX = 1536
GDN_QKV = 1536
D_IN = 4296
EPS = 1e-6
ADAM_LR, ADAM_B1, ADAM_B2, ADAM_EPS, ADAM_WD, ADAM_STEP = 0.001, 0.9, 0.999, 1e-08, 0.01, 10

OFF_MLA = 0
OFF_MEMQ = 1024
OFF_GDN = 1536
OFF_GATE = 3072
N_PAD = 4608
HEAD_PAD = 256
MLA_SCALE = (MLA_NOPE + MLA_ROPE) ** -0.5
MEM_SCALE = MEM_DH ** -0.5
GDN_SCALE = GDN_DK ** -0.5
NEG = -1e30

NN = ((1,), (0,))
NT = ((1,), (1,))
TN = ((0,), (0,))


def _dot(a, b, dims):
    return lax.dot_general(a, b, (dims, ((), ())), preferred_element_type=F32)


def _bdot(spec, a, b, precision=None):
    return jnp.einsum(spec, a, b, preferred_element_type=F32, precision=precision)


def _arb(n):
    return pltpu.CompilerParams(dimension_semantics=("arbitrary",) * n)


def _sigmoid(x):
    return 1.0 / (1.0 + jnp.exp(-x))


def _softplus(z):
    return jnp.maximum(z, 0.0) + jnp.log(1.0 + jnp.exp(-jnp.abs(z)))


def _rope(t, cos_row, sin_row):
    return t * cos_row + pltpu.roll(t, 64, 1) * sin_row


def _rope_bwd(d, cos_row, sin_row):
    return d * cos_row + pltpu.roll(d * sin_row, 64, 1)


def rms_fwd(x, gain, name, tm=512):
    T, n = x.shape
    tm = min(tm, T)

    def body(x_ref, g_ref, o_ref):
        xv = x_ref[...]
        r = lax.rsqrt(jnp.mean(xv * xv, axis=-1, keepdims=True) + EPS)
        o_ref[...] = (xv * r * g_ref[...]).astype(BF16)

    return pl.pallas_call(
        body, name=name, grid=(T // tm,),
        in_specs=[BS((tm, n), lambda i: (i, 0)), BS((1, n), lambda i: (0, 0))],
        out_specs=BS((tm, n), lambda i: (i, 0)),
        out_shape=jax.ShapeDtypeStruct((T, n), BF16), compiler_params=_arb(1))(x, gain)


def mm(a, b, kind, out_dtype, name, bm=512, bn=None, n_outer=False):
    if kind == "nn":
        (M, K), (_, N) = a.shape, b.shape
    elif kind == "nt":
        (M, K), (N, _) = a.shape, b.shape
    else:
        (K, M), (_, N) = a.shape, b.shape
    bm, bn = min(bm, M), min(bn or N, N)
    assert M % bm == 0 and N % bn == 0, (name, M, N, K)
    ij = (lambda g0, g1: (g1, g0)) if n_outer else (lambda g0, g1: (g0, g1))
    a_spec = BS((K, bm), lambda g0, g1: (0, ij(g0, g1)[0])) if kind == "tn" else BS((bm, K), lambda g0, g1: (ij(g0, g1)[0], 0))
    once = dict(pipeline_mode=pl.Buffered(1)) if bn == N else {}
    b_spec = (BS((bn, K), lambda g0, g1: (ij(g0, g1)[1], 0), **once) if kind == "nt"
              else BS((K, bn), lambda g0, g1: (0, ij(g0, g1)[1]), **once))
    dims = {"nn": NN, "nt": NT, "tn": TN}[kind]

    def body(a_ref, b_ref, o_ref):
        o_ref[...] = _dot(a_ref[...].astype(BF16), b_ref[...].astype(BF16), dims).astype(out_dtype)

    grid = (N // bn, M // bm) if n_outer else (M // bm, N // bn)
    return pl.pallas_call(
        body, name=name, grid=grid, in_specs=[a_spec, b_spec], out_specs=BS((bm, bn), lambda g0, g1: ij(g0, g1)),
        out_shape=jax.ShapeDtypeStruct((M, N), out_dtype), compiler_params=_arb(2))(a, b)


def rope_tables(pos_col, inv_row, sgn_row, msk_row, tm=512):
    T = pos_col.shape[0]
    tm = min(tm, T)

    def body(p_ref, inv_ref, sgn_ref, msk_ref, c_ref, s_ref):
        ang = p_ref[...].astype(F32) * inv_ref[...]
        c_ref[...] = jnp.cos(ang) * msk_ref[...]
        s_ref[...] = jnp.sin(ang) * sgn_ref[...]

    row = BS((1, 128), lambda i: (0, 0))
    return pl.pallas_call(
        body, name="rope_tables", grid=(T // tm,),
        in_specs=[BS((tm, 1), lambda i: (i, 0)), row, row, row],
        out_specs=[BS((tm, 128), lambda i: (i, 0))] * 2,
        out_shape=[jax.ShapeDtypeStruct((T, 128), F32)] * 2, compiler_params=_arb(1))(pos_col, inv_row, sgn_row, msk_row)


def mla_prep(P, gq, gkv, wq, wkv, cos_t, sin_t, tm=512):
    T = P.shape[0]
    tm = min(tm, T)

    def body(p_ref, gq_ref, gkv_ref, wq_ref, wkv_ref, c_ref, s_ref, q_ref, k_ref, v_ref, qn_ref, kvn_ref):
        p = p_ref[...]
        cq, ckv, kr = p[:, :Q_LORA], p[:, Q_LORA:Q_LORA + KV_LORA], p[:, 640:768]
        qn = (cq * lax.rsqrt(jnp.mean(cq * cq, axis=-1, keepdims=True) + EPS) * gq_ref[...]).astype(BF16)
        kvn = (ckv * lax.rsqrt(jnp.mean(ckv * ckv, axis=-1, keepdims=True) + EPS) * gkv_ref[...]).astype(BF16)
        qn_ref[...] = qn
        kvn_ref[...] = kvn
        q = _dot(qn, wq_ref[...], NT)
        kv = _dot(kvn, wkv_ref[...], NN)
        cos_row, sin_row = c_ref[...], s_ref[...]
        krr = _rope(kr, cos_row, sin_row).astype(BF16)
        for h in range(N_HEADS):
            lo = h * HEAD_PAD
            q_ref[:, lo:lo + 128] = (q[:, lo:lo + 128] * MLA_SCALE).astype(BF16)
            q_ref[:, lo + 128:lo + 256] = (_rope(q[:, lo + 128:lo + 256], cos_row, sin_row) * MLA_SCALE).astype(BF16)
            k_ref[:, lo:lo + 128] = kv[:, h * 128:(h + 1) * 128].astype(BF16)
            k_ref[:, lo + 128:lo + 256] = krr
        v_ref[...] = kv[:, 512:].astype(BF16)

    full = lambda r, c: BS((r, c), lambda i: (0, 0))
    rowb = lambda c: BS((tm, c), lambda i: (i, 0))
    return pl.pallas_call(
        body, name="mla_prep", grid=(T // tm,),
        in_specs=[rowb(1024), full(1, Q_LORA), full(1, KV_LORA), full(1024, Q_LORA), full(KV_LORA, 1024), rowb(128), rowb(128)],
        out_specs=[rowb(1024), rowb(1024), rowb(512), rowb(Q_LORA), rowb(KV_LORA)],
        out_shape=[jax.ShapeDtypeStruct((T, 1024), BF16), jax.ShapeDtypeStruct((T, 1024), BF16),
                   jax.ShapeDtypeStruct((T, 512), BF16), jax.ShapeDtypeStruct((T, Q_LORA), BF16),
                   jax.ShapeDtypeStruct((T, KV_LORA), BF16)],
        compiler_params=_arb(1))(P, gq, gkv, wq, wkv, cos_t, sin_t)


ATTN_HEADS_PER_STEP = 2


def mla_attn_fwd(Q, K, V, B, S, tq=512, hp=ATTN_HEADS_PER_STEP):
    T = B * S
    tq = min(tq, S)
    nq = S // tq

    def body(q_ref, k_ref, v_ref, o_ref, lse_ref, m_s, l_s, acc_s):
        i = pl.program_id(2)
        m_s[...] = jnp.full_like(m_s, NEG)
        l_s[...] = jnp.zeros_like(l_s)
        acc_s[...] = jnp.zeros_like(acc_s)

        def blk(j, masked):
            rows = pl.ds(pl.multiple_of(j * tq, tq), tq)
            for h in range(hp):
                s = _dot(q_ref[:, h * HEAD_PAD:(h + 1) * HEAD_PAD], k_ref[rows, h * HEAD_PAD:(h + 1) * HEAD_PAD], NT)
                if masked:
                    r = lax.broadcasted_iota(jnp.int32, (tq, tq), 0)
                    c = lax.broadcasted_iota(jnp.int32, (tq, tq), 1)
                    s = jnp.where(r >= c, s, NEG)
                m_prev = m_s[h]
                m_new = jnp.maximum(m_prev, jnp.max(s, axis=1, keepdims=True))
                p = jnp.exp(s - m_new)
                alpha = jnp.exp(m_prev - m_new)
                l_s[h] = alpha * l_s[h] + jnp.sum(p, axis=1, keepdims=True)
                acc_s[h] = alpha * acc_s[h] + _dot(p.astype(BF16), v_ref[rows, h * 128:(h + 1) * 128], NN)
                m_s[h] = m_new

        def loop(j, c):
            blk(j, False)
            return c

        lax.fori_loop(0, i, loop, 0)
        blk(i, True)
        for h in range(hp):
            o_ref[:, h * 128:(h + 1) * 128] = acc_s[h] / l_s[h]
            lse_ref[h] = m_s[h] + jnp.log(l_s[h])

    return pl.pallas_call(
        body, name="mla_attn_fwd", grid=(B, N_HEADS // hp, nq),
        in_specs=[BS((tq, hp * HEAD_PAD), lambda b, h, i: (b * nq + i, h)),
                  BS((S, hp * HEAD_PAD), lambda b, h, i: (b, h)),
                  BS((S, hp * 128), lambda b, h, i: (b, h))],
        out_specs=[BS((tq, hp * 128), lambda b, h, i: (b * nq + i, h)),
                   BS((hp, tq, 1), lambda b, h, i: (h, b * nq + i, 0))],
        out_shape=[jax.ShapeDtypeStruct((T, 512), F32), jax.ShapeDtypeStruct((N_HEADS, T, 1), F32)],
        scratch_shapes=[pltpu.VMEM((hp, tq, 1), F32), pltpu.VMEM((hp, tq, 1), F32), pltpu.VMEM((hp, tq, 128), F32)],
        compiler_params=_arb(3))(Q, K, V)


def mla_attn_bwd(Q, K, V, O, dO, LSE, B, S, tq=512, hp=ATTN_HEADS_PER_STEP):
    T = B * S
    tq = min(tq, S)
    nq = S // tq

    def body(q_ref, k_ref, v_ref, o_ref, do_ref, lse_ref, dq_ref, dk_ref, dv_ref, delta_s, dk_s, dv_s):
        j = pl.program_id(2)

        @pl.when(j == 0)
        def _():
            dq_ref[...] = jnp.zeros_like(dq_ref)
            for h in range(hp):
                sl = slice(h * 128, (h + 1) * 128)
                delta_s[h] = jnp.sum(do_ref[:, sl] * o_ref[:, sl], axis=1, keepdims=True)

        dk_s[...] = jnp.zeros_like(dk_s)
        dv_s[...] = jnp.zeros_like(dv_s)

        def step(i, c):
            rows = pl.ds(pl.multiple_of(i * tq, tq), tq)
            r = i * tq + lax.broadcasted_iota(jnp.int32, (tq, tq), 0)
            cc = j * tq + lax.broadcasted_iota(jnp.int32, (tq, tq), 1)
            causal = r >= cc
            for h in range(hp):
                sq, sv = slice(h * HEAD_PAD, (h + 1) * HEAD_PAD), slice(h * 128, (h + 1) * 128)
                q = q_ref[rows, sq]
                k = k_ref[:, sq]
                do = do_ref[rows, sv].astype(BF16)
                s = _dot(q, k, NT)
                p = jnp.where(causal, jnp.exp(s - lse_ref[h, rows, :]), 0.0)
                dv_s[:, sv] += _dot(p.astype(BF16), do, TN)
                dp = _dot(do, v_ref[:, sv], NT)
                ds = (p * (dp - delta_s[h, rows, :])).astype(BF16)
                dk_s[:, sq] += _dot(ds, q, TN)
                dq_ref[rows, sq] += _dot(ds, k, NN)
            return c

        lax.fori_loop(j, nq, step, 0)
        dk_ref[...] = dk_s[...]
        dv_ref[...] = dv_s[...]

    seq = lambda c: BS((S, c), lambda b, h, j: (b, h))
    blk = lambda c: BS((tq, c), lambda b, h, j: (b * nq + j, h))
    return pl.pallas_call(
        body, name="mla_attn_bwd", grid=(B, N_HEADS // hp, nq),
        in_specs=[seq(hp * HEAD_PAD), blk(hp * HEAD_PAD), blk(hp * 128), seq(hp * 128), seq(hp * 128),
                  BS((hp, S, 1), lambda b, h, j: (h, b, 0))],
        out_specs=[seq(hp * HEAD_PAD), blk(hp * HEAD_PAD), blk(hp * 128)],
        out_shape=[jax.ShapeDtypeStruct((T, 1024), F32), jax.ShapeDtypeStruct((T, 1024), F32),
                   jax.ShapeDtypeStruct((T, 512), F32)],
        scratch_shapes=[pltpu.VMEM((hp, S, 1), F32), pltpu.VMEM((tq, hp * HEAD_PAD), F32), pltpu.VMEM((tq, hp * 128), F32)],
        compiler_params=_arb(3))(Q, K, V, O, dO, LSE)


def mla_post_bwd(dQ, dK, dV, cos_t, sin_t, tm=512):
    T = dQ.shape[0]
    tm = min(tm, T)

    def body(dq_ref, dk_ref, dv_ref, c_ref, s_ref, ql_ref, kvl_ref, kr_ref):
        cos_row, sin_row = c_ref[...], s_ref[...]
        kr = jnp.zeros((tm, 128), F32)
        for h in range(N_HEADS):
            lo = h * HEAD_PAD
            ql_ref[:, lo:lo + 128] = (dq_ref[:, lo:lo + 128] * MLA_SCALE).astype(BF16)
            ql_ref[:, lo + 128:lo + 256] = (_rope_bwd(dq_ref[:, lo + 128:lo + 256], cos_row, sin_row) * MLA_SCALE).astype(BF16)
            kvl_ref[:, h * 128:(h + 1) * 128] = dk_ref[:, lo:lo + 128].astype(BF16)
            kr = kr + dk_ref[:, lo + 128:lo + 256]
        kvl_ref[:, 512:] = dv_ref[...].astype(BF16)
        kr_ref[...] = _rope_bwd(kr, cos_row, sin_row)

    rowb = lambda c: BS((tm, c), lambda i: (i, 0))
    return pl.pallas_call(
        body, name="mla_post_bwd", grid=(T // tm,),
        in_specs=[rowb(1024), rowb(1024), rowb(512), rowb(128), rowb(128)],
        out_specs=[rowb(1024), rowb(1024), rowb(128)],
        out_shape=[jax.ShapeDtypeStruct((T, 1024), BF16), jax.ShapeDtypeStruct((T, 1024), BF16),
                   jax.ShapeDtypeStruct((T, 128), F32)],
        compiler_params=_arb(1))(dQ, dK, dV, cos_t, sin_t)


def mla_norm_bwd(P, dqn, dkvn, dkr, dab, gq, gkv, tm=512):
    T = P.shape[0]
    tm = min(tm, T)

    def norm_bwd(x, dy, g):
        r = lax.rsqrt(jnp.mean(x * x, axis=-1, keepdims=True) + EPS)
        xh = x * r
        dxh = dy * g
        return r * (dxh - xh * jnp.mean(dxh * xh, axis=-1, keepdims=True)), jnp.sum(dy * xh, axis=0, keepdims=True)

    def body(p_ref, dqn_ref, dkvn_ref, dkr_ref, dab_ref, gq_ref, gkv_ref, o_ref, aq_ref, akv_ref):
        @pl.when(pl.program_id(0) == 0)
        def _():
            aq_ref[...] = jnp.zeros_like(aq_ref)
            akv_ref[...] = jnp.zeros_like(akv_ref)

        dcq, ggq = norm_bwd(p_ref[:, :Q_LORA], dqn_ref[...], gq_ref[...])
        dckv, ggkv = norm_bwd(p_ref[:, Q_LORA:640], dkvn_ref[...], gkv_ref[...])
        aq_ref[...] += ggq
        akv_ref[...] += ggkv
        o_ref[:, :Q_LORA] = dcq.astype(BF16)
        o_ref[:, Q_LORA:640] = dckv.astype(BF16)
        o_ref[:, 640:768] = dkr_ref[...].astype(BF16)
        o_ref[:, 768:896] = dab_ref[...]
        o_ref[:, 896:1024] = jnp.zeros((tm, 128), BF16)

    rowb = lambda c: BS((tm, c), lambda i: (i, 0))
    full = lambda c: BS((1, c), lambda i: (0, 0))
    return pl.pallas_call(
        body, name="mla_norm_bwd", grid=(T // tm,),
        in_specs=[rowb(1024), rowb(Q_LORA), rowb(KV_LORA), rowb(128), rowb(128), full(Q_LORA), full(KV_LORA)],
        out_specs=[rowb(1024), full(Q_LORA), full(KV_LORA)],
        out_shape=[jax.ShapeDtypeStruct((T, 1024), BF16), jax.ShapeDtypeStruct((1, Q_LORA), F32),
                   jax.ShapeDtypeStruct((1, KV_LORA), F32)],
        compiler_params=_arb(1))(P, dqn, dkvn, dkr, dab, gq, gkv)


def _mem_probs(qh, kh):
    s = _dot(qh, kh, NT) * MEM_SCALE
    p = jnp.exp(s - jnp.max(s, axis=1, keepdims=True))
    return p / jnp.sum(p, axis=1, keepdims=True)


def mem_attn_fwd(P, MKV, B, S, M, tq=512):
    T = B * S
    tq = min(tq, S)
    nq = S // tq

    def body(q_ref, kv_ref, o_ref):
        for h in range(N_HEADS):
            sl = slice(h * 128, (h + 1) * 128)
            p = _mem_probs(q_ref[:, sl].astype(BF16), kv_ref[:, sl])
            o_ref[:, sl] = _dot(p.astype(BF16), kv_ref[:, 512 + h * 128:512 + (h + 1) * 128], NN)

    return pl.pallas_call(
        body, name="mem_attn_fwd", grid=(B, nq),
        in_specs=[BS((tq, 512), lambda b, i: (b * nq + i, OFF_MEMQ // 512)), BS((M, 1024), lambda b, i: (b, 0))],
        out_specs=BS((tq, 512), lambda b, i: (b * nq + i, 0)),
        out_shape=jax.ShapeDtypeStruct((T, 512), F32), compiler_params=_arb(2))(P, MKV)


def mem_attn_bwd(P, MKV, dO, B, S, M, tq=512):
    T = B * S
    tq = min(tq, S)
    nq = S // tq

    def body(q_ref, kv_ref, do_ref, dq_ref, dkv_ref):
        @pl.when(pl.program_id(1) == 0)
        def _():
            dkv_ref[...] = jnp.zeros_like(dkv_ref)

        for h in range(N_HEADS):
            sl = slice(h * 128, (h + 1) * 128)
            sv = slice(512 + h * 128, 512 + (h + 1) * 128)
            qh = q_ref[:, sl].astype(BF16)
            kh = kv_ref[:, sl]
            do = do_ref[:, sl].astype(BF16)
            p = _mem_probs(qh, kh)
            dkv_ref[:, sv] += _dot(p.astype(BF16), do, TN)
            dp = _dot(do, kv_ref[:, sv], NT)
            ds = (p * (dp - jnp.sum(dp * p, axis=1, keepdims=True)) * MEM_SCALE).astype(BF16)
            dq_ref[:, sl] = _dot(ds, kh, NN).astype(BF16)
            dkv_ref[:, sl] += _dot(ds, qh, TN)

    return pl.pallas_call(
        body, name="mem_attn_bwd", grid=(B, nq),
        in_specs=[BS((tq, 512), lambda b, i: (b * nq + i, OFF_MEMQ // 512)), BS((M, 1024), lambda b, i: (b, 0)),
                  BS((tq, 512), lambda b, i: (b * nq + i, 0))],
        out_specs=[BS((tq, 512), lambda b, i: (b * nq + i, 0)), BS((M, 1024), lambda b, i: (b, 0))],
        out_shape=[jax.ShapeDtypeStruct((T, 512), BF16), jax.ShapeDtypeStruct((B * M, 1024), F32)],
        compiler_params=_arb(2))(P, MKV, dO)


def gain_grad(x, dy, name, tm=256):
    T, n = x.shape
    tm = min(tm, T)

    def body(x_ref, dy_ref, o_ref):
        @pl.when(pl.program_id(0) == 0)
        def _():
            o_ref[...] = jnp.zeros_like(o_ref)

        xv = x_ref[...]
        xh = xv * lax.rsqrt(jnp.mean(xv * xv, axis=-1, keepdims=True) + EPS)
        o_ref[...] += jnp.sum(dy_ref[...] * xh, axis=0, keepdims=True)

    return pl.pallas_call(
        body, name=name, grid=(T // tm,),
        in_specs=[BS((tm, n), lambda i: (i, 0))] * 2, out_specs=BS((1, n), lambda i: (0, 0)),
        out_shape=jax.ShapeDtypeStruct((1, n), F32), compiler_params=_arb(1))(x, dy)


def _conv_silu(x, w, t):
    y = x * w[3:4, :]
    for s in range(1, GDN_CONV):
        y = y + jnp.where(t >= s, pltpu.roll(x, s, 0), 0.0) * w[3 - s:4 - s, :]
    return y, _sigmoid(y)


def gdn_prep_fwd(P, conv_w, B, S):
    T = B * S

    def body(x_ref, w_ref, o_ref):
        kind = pl.program_id(1)
        t = lax.broadcasted_iota(jnp.int32, (S, 1), 0)
        y, sg = _conv_silu(x_ref[...], w_ref[...], t)
        a = y * sg
        scale = jnp.where(kind == 0, GDN_SCALE, 1.0).astype(F32)
        for h in range(N_HEADS):
            sl = slice(h * 128, (h + 1) * 128)
            seg = a[:, sl]
            n = lax.rsqrt(jnp.sum(seg * seg, axis=-1, keepdims=True) + EPS)
            o_ref[:, sl] = jnp.where(kind < 2, seg * (n * scale), seg)

    return pl.pallas_call(
        body, name="gdn_prep_fwd", grid=(B, 3),
        in_specs=[BS((S, 512), lambda b, k: (b, OFF_GDN // 512 + k)), BS((GDN_CONV, 512), lambda b, k: (0, k))],
        out_specs=BS((S, 512), lambda b, k: (b, k)),
        out_shape=jax.ShapeDtypeStruct((T, GDN_QKV), F32), compiler_params=_arb(2))(P, conv_w)


def gdn_prep_bwd(P, dqkv, conv_w, B, S):
    T = B * S

    def body(x_ref, d_ref, w_ref, o_ref, gw_ref):
        kind = pl.program_id(0)

        @pl.when(pl.program_id(1) == 0)
        def _():
            gw_ref[...] = jnp.zeros_like(gw_ref)

        t = lax.broadcasted_iota(jnp.int32, (S, 1), 0)
        x = x_ref[...]
        w = w_ref[...]
        y, sg = _conv_silu(x, w, t)
        a = y * sg
        scale = jnp.where(kind == 0, GDN_SCALE, 1.0).astype(F32)
        das = []
        for h in range(N_HEADS):
            sl = slice(h * 128, (h + 1) * 128)
            seg, dseg = a[:, sl], d_ref[:, sl]
            n = lax.rsqrt(jnp.sum(seg * seg, axis=-1, keepdims=True) + EPS)
            dn = scale * (n * dseg - seg * (n * n * n) * jnp.sum(dseg * seg, axis=-1, keepdims=True))
            das.append(jnp.where(kind < 2, dn, dseg))
        dy = jnp.concatenate(das, axis=1) * (sg * (1.0 + y * (1.0 - sg)))
        dx = dy * w[3:4, :]
        gw_ref[3:4, :] += jnp.sum(dy * x, axis=0, keepdims=True)
        for s in range(1, GDN_CONV):
            dx = dx + jnp.where(t + s < S, pltpu.roll(dy, S - s, 0), 0.0) * w[3 - s:4 - s, :]
            gw_ref[3 - s:4 - s, :] += jnp.sum(dy * jnp.where(t >= s, pltpu.roll(x, s, 0), 0.0), axis=0, keepdims=True)
        o_ref[...] = dx.astype(BF16)

    return pl.pallas_call(
        body, name="gdn_prep_bwd", grid=(3, B),
        in_specs=[BS((S, 512), lambda k, b: (b, OFF_GDN // 512 + k)), BS((S, 512), lambda k, b: (b, k)),
                  BS((GDN_CONV, 512), lambda k, b: (0, k))],
        out_specs=[BS((S, 512), lambda k, b: (b, k)), BS((GDN_CONV, 512), lambda k, b: (0, k))],
        out_shape=[jax.ShapeDtypeStruct((T, GDN_QKV), BF16), jax.ShapeDtypeStruct((GDN_CONV, GDN_QKV), F32)],
        compiler_params=_arb(2))(P, dqkv, conv_w)


def _chunk_row(n_rows):
    return lax.broadcasted_iota(jnp.int32, (n_rows, 1), 0) % CHUNK


def gdn_gate_fwd(P, alog_row, dt_row, B, S):
    T = B * S

    def body(x_ref, al_ref, dt_ref, o_ref):
        x = x_ref[...]
        lane = lax.broadcasted_iota(jnp.int32, (1, 128), 1)
        g = jnp.where(lane < 4, -jnp.exp(al_ref[...]) * _softplus(x + dt_ref[...]), 0.0)
        t = _chunk_row(S)
        for s in (1, 2, 4, 8, 16, 32):
            g = g + jnp.where(t >= s, pltpu.roll(g, s, 0), 0.0)
        o_ref[...] = jnp.where(lane < 4, g, jnp.where(lane < 8, _sigmoid(x), 0.0))

    row = BS((1, 128), lambda b: (0, 0))
    return pl.pallas_call(
        body, name="gdn_gate_fwd", grid=(B,),
        in_specs=[BS((S, 128), lambda b: (b, 768 // 128)), row, row], out_specs=BS((S, 128), lambda b: (b, 0)),
        out_shape=jax.ShapeDtypeStruct((T, 128), F32), compiler_params=_arb(1))(P, alog_row, dt_row)


def gdn_gate_bwd(P, dGB, alog_row, dt_row, B, S):
    T = B * S

    def body(x_ref, d_ref, al_ref, dt_ref, o_ref, acc_ref):
        @pl.when(pl.program_id(0) == 0)
        def _():
            acc_ref[...] = jnp.zeros_like(acc_ref)

        x, d = x_ref[...], d_ref[...]
        lane = lax.broadcasted_iota(jnp.int32, (1, 128), 1)
        z = x + dt_ref[...]
        coef = -jnp.exp(al_ref[...])
        g = coef * _softplus(z)
        da = jnp.where(lane < 4, d * coef * _sigmoid(z), 0.0)
        beta = _sigmoid(x)
        o_ref[...] = jnp.where(lane < 4, da, jnp.where(lane < 8, d * beta * (1.0 - beta), 0.0)).astype(BF16)
        acc_ref[0:1, :] += jnp.sum(jnp.where(lane < 4, d * g, 0.0), axis=0, keepdims=True)
        acc_ref[1:2, :] += jnp.sum(da, axis=0, keepdims=True)

    row = BS((1, 128), lambda b: (0, 0))
    return pl.pallas_call(
        body, name="gdn_gate_bwd", grid=(B,),
        in_specs=[BS((S, 128), lambda b: (b, 768 // 128)), BS((S, 128), lambda b: (b, 0)), row, row],
        out_specs=[BS((S, 128), lambda b: (b, 0)), BS((8, 128), lambda b: (0, 0))],
        out_shape=[jax.ShapeDtypeStruct((T, 128), BF16), jax.ShapeDtypeStruct((8, 128), F32)],
        compiler_params=_arb(1))(P, dGB, alog_row, dt_row)


def _chunk_masks(nc):
    r = lax.broadcasted_iota(jnp.int32, (nc, CHUNK, CHUNK), 1)
    c = lax.broadcasted_iota(jnp.int32, (nc, CHUNK, CHUNK), 2)
    return r >= c, r > c


def _chunk_local(q, k, gc, gr, beta, incl, strict):
    decay = jnp.exp(jnp.where(incl, gc - gr, NEG))
    kb = k * beta
    kbf = k.astype(BF16)
    m_kk = _bdot("gcd,gjd->gcj", kb.astype(BF16), kbf)
    l_mat = jnp.where(strict, m_kk * decay, 0.0)
    a_mat = _bdot("gcd,gjd->gcj", q.astype(BF16), kbf) * decay
    return decay, kb, l_mat, a_mat


def gdn_chunk_fwd(qkv, GB, Grow, B, S, nc=8):
    T = B * S
    N = S // CHUNK
    nc = min(nc, N)
    nb = N // nc
    R = nc * CHUNK
    hi = lax.Precision.HIGHEST

    def body(q_ref, k_ref, v_ref, gb_ref, gr_ref, u_ref, w_ref, t_ref, a_ref):
        incl, strict = _chunk_masks(nc)
        eye = (lax.broadcasted_iota(jnp.int32, (nc, CHUNK, CHUNK), 1)
               == lax.broadcasted_iota(jnp.int32, (nc, CHUNK, CHUNK), 2)).astype(F32)
        for h in range(N_HEADS):
            sl = slice(h * 128, (h + 1) * 128)
            q = q_ref[:, sl].reshape(nc, CHUNK, 128)
            k = k_ref[:, sl].reshape(nc, CHUNK, 128)
            v = v_ref[:, sl].reshape(nc, CHUNK, 128)
            gc = gb_ref[:, h:h + 1].reshape(nc, CHUNK, 1)
            beta = gb_ref[:, 4 + h:5 + h].reshape(nc, CHUNK, 1)
            gr = gr_ref[h][:, None, :]
            _, kb, l_mat, a_mat = _chunk_local(q, k, gc, gr, beta, incl, strict)
            pw = -l_mat
            tinv = eye + pw
            for _ in range(5):
                pw = _bdot("gij,gjk->gik", pw, pw, hi)
                tinv = tinv + _bdot("gij,gjk->gik", tinv, pw, hi)
            tb = tinv.astype(BF16)
            u = _bdot("gcj,gjv->gcv", tb, (v * beta).astype(BF16))
            w = _bdot("gcj,gjk->gck", tb, (kb * jnp.exp(gc)).astype(BF16))
            u_ref[:, sl] = u.reshape(R, 128)
            w_ref[:, sl] = w.reshape(R, 128)
            t_ref[h] = tinv
            a_ref[h] = a_mat

    rowb = lambda c, j: BS((R, c), lambda b, n: (b * nb + n, j))
    mat = BS((None, N_HEADS, nc, CHUNK, CHUNK), lambda b, n: (b, 0, n, 0, 0))
    return pl.pallas_call(
        body, name="gdn_chunk_fwd", grid=(B, nb),
        in_specs=[rowb(512, 0), rowb(512, 1), rowb(512, 2), rowb(128, 0),
                  BS((None, N_HEADS, nc, CHUNK), lambda b, n: (b, 0, n, 0))],
        out_specs=[rowb(512, 0), rowb(512, 0), mat, mat],
        out_shape=[jax.ShapeDtypeStruct((T, 512), F32), jax.ShapeDtypeStruct((T, 512), F32),
                   jax.ShapeDtypeStruct((B, N_HEADS, N, CHUNK, CHUNK), F32),
                   jax.ShapeDtypeStruct((B, N_HEADS, N, CHUNK, CHUNK), F32)],
        compiler_params=_arb(2))(qkv, qkv, qkv, GB, Grow)


def gdn_scan_fwd(qkv3, U3, W3, GB3, A, B, S):
    N = S // CHUNK

    def body(q_ref, k_ref, u_ref, w_ref, gb_ref, a_ref, o_ref, vn_ref, st_ref, s_s):
        @pl.when(pl.program_id(0) == 0)
        def _():
            s_s[...] = jnp.zeros_like(s_s)

        for b in range(B):
            for h in range(N_HEADS):
                sl = slice(h * 128, (h + 1) * 128)
                st = s_s[b, h]
                st_ref[b, h] = st
                stb = st.astype(BF16)
                g = gb_ref[b, :, h:h + 1]
                gl = g[CHUNK - 1:CHUNK, :]
                vn = u_ref[b, :, sl] - _dot(w_ref[b, :, sl].astype(BF16), stb, NN)
                vnb = vn.astype(BF16)
                o = _dot((q_ref[b, :, sl] * jnp.exp(g)).astype(BF16), stb, NN) + _dot(a_ref[b, h].astype(BF16), vnb, NN)
                vn_ref[b, :, sl] = vn
                o_ref[b, :, sl] = o
                s_s[b, h] = st * jnp.exp(gl) + _dot((k_ref[b, :, sl] * jnp.exp(gl - g)).astype(BF16), vnb, TN)

    tok = lambda c, j: BS((B, CHUNK, c), lambda n: (0, n, j))
    return pl.pallas_call(
        body, name="gdn_scan_fwd", grid=(N,),
        in_specs=[tok(512, 0), tok(512, 1), tok(512, 0), tok(512, 0), tok(128, 0),
                  BS((B, N_HEADS, None, CHUNK, CHUNK), lambda n: (0, 0, n, 0, 0))],
        out_specs=[tok(512, 0), tok(512, 0), BS((B, N_HEADS, None, 128, 128), lambda n: (0, 0, n, 0, 0))],
        out_shape=[jax.ShapeDtypeStruct((B, S, 512), F32), jax.ShapeDtypeStruct((B, S, 512), F32),
                   jax.ShapeDtypeStruct((B, N_HEADS, N, 128, 128), F32)],
        scratch_shapes=[pltpu.VMEM((B, N_HEADS, 128, 128), F32)],
        compiler_params=_arb(1))(qkv3, qkv3, U3, W3, GB3, A)


def gdn_scan_bwd(dO3, qkv3, W3, Vn3, GB3, A, St, B, S):
    N = S // CHUNK

    def body(do_ref, q_ref, k_ref, w_ref, vn_ref, gb_ref, a_ref, st_ref,
             du_ref, dw_ref, dq_ref, dk_ref, da_ref, dg_ref, ds_s):
        @pl.when(pl.program_id(0) == 0)
        def _():
            ds_s[...] = jnp.zeros_like(ds_s)

        lane = lax.broadcasted_iota(jnp.int32, (1, 128), 1)
        last = lax.broadcasted_iota(jnp.int32, (CHUNK, 1), 0) == CHUNK - 1
        for b in range(B):
            dg_all = jnp.zeros((CHUNK, 128), F32)
            for h in range(N_HEADS):
                sl = slice(h * 128, (h + 1) * 128)
                st = st_ref[b, h]
                stb = st.astype(BF16)
                dsn = ds_s[b, h]
                dsnb = dsn.astype(BF16)
                g = gb_ref[b, :, h:h + 1]
                gl = g[CHUNK - 1:CHUNK, :]
                egl = jnp.exp(gl)
                ekd = jnp.exp(gl - g)
                eg = jnp.exp(g)
                q, k = q_ref[b, :, sl], k_ref[b, :, sl]
                kd = k * ekd
                qg = q * eg
                do = do_ref[b, :, sl].astype(BF16)
                vnb = vn_ref[b, :, sl].astype(BF16)
                dvn = _dot(a_ref[b, h].astype(BF16), do, TN) + _dot(kd.astype(BF16), dsnb, NN)
                dvnb = dvn.astype(BF16)
                da_ref[b, h] = _dot(do, vnb, NT)
                dqg = _dot(do, stb, NT)
                dkd = _dot(vnb, dsnb, NT)
                ds_s[b, h] = (_dot(qg.astype(BF16), do, TN) + egl * dsn - _dot(w_ref[b, :, sl].astype(BF16), dvnb, TN))
                du_ref[b, :, sl] = dvn
                dw_ref[b, :, sl] = -_dot(dvnb, stb, NT)
                dq_ref[b, :, sl] = dqg * eg
                dk_ref[b, :, sl] = dkd * ekd
                ddel = jnp.sum(dkd * kd, axis=1, keepdims=True)
                dgl = jnp.sum(ddel, axis=0, keepdims=True) + jnp.sum(jnp.sum(st * dsn, axis=1, keepdims=True), axis=0, keepdims=True) * egl
                col = jnp.sum(dqg * qg, axis=1, keepdims=True) - ddel + jnp.where(last, dgl, 0.0)
                dg_all = jnp.where(lane == h, col, dg_all)
            dg_ref[b] = dg_all

    tok = lambda c, j: BS((B, CHUNK, c), lambda n: (0, N - 1 - n, j))
    mat = lambda d: BS((B, N_HEADS, None, d, d), lambda n: (0, 0, N - 1 - n, 0, 0))
    return pl.pallas_call(
        body, name="gdn_scan_bwd", grid=(N,),
        in_specs=[tok(512, 0), tok(512, 0), tok(512, 1), tok(512, 0), tok(512, 0), tok(128, 0), mat(CHUNK), mat(128)],
        out_specs=[tok(512, 0), tok(512, 0), tok(512, 0), tok(512, 0), mat(CHUNK), tok(128, 0)],
        out_shape=[jax.ShapeDtypeStruct((B, S, 512), F32)] * 4
        + [jax.ShapeDtypeStruct((B, N_HEADS, N, CHUNK, CHUNK), F32), jax.ShapeDtypeStruct((B, S, 128), F32)],
        scratch_shapes=[pltpu.VMEM((B, N_HEADS, 128, 128), F32)],
        compiler_params=_arb(1))(dO3, qkv3, qkv3, W3, Vn3, GB3, A, St)


def gdn_chunk_bwd(qkv, GB, Grow, Tinv, dA, dU, dW, dQ1, dK1, dG1, B, S, nc=8):
    T = B * S
    N = S // CHUNK
    nc = min(nc, N)
    nb = N // nc
    R = nc * CHUNK

    def body(q_ref, k_ref, v_ref, gb_ref, gr_ref, t_ref, da_ref, du_ref, dw_ref, dq1_ref, dk1_ref, dg1_ref, o_ref, dgb_ref):
        incl, strict = _chunk_masks(nc)
        lane = lax.broadcasted_iota(jnp.int32, (1, 128), 1)
        dg_all = dg1_ref[...]
        db_all = jnp.zeros((R, 128), F32)
        for h in range(N_HEADS):
            sl = slice(h * 128, (h + 1) * 128)
            q = q_ref[:, sl].reshape(nc, CHUNK, 128)
            k = k_ref[:, sl].reshape(nc, CHUNK, 128)
            v = v_ref[:, sl].reshape(nc, CHUNK, 128)
            gc = gb_ref[:, h:h + 1].reshape(nc, CHUNK, 1)
            beta = gb_ref[:, 4 + h:5 + h].reshape(nc, CHUNK, 1)
            gr = gr_ref[h][:, None, :]
            decay, kb, l_mat, a_mat = _chunk_local(q, k, gc, gr, beta, incl, strict)
            eg = jnp.exp(gc)
            kbg = kb * eg
            vb = v * beta
            tb = t_ref[h].astype(BF16)
            du = du_ref[:, sl].reshape(nc, CHUNK, 128).astype(BF16)
            dw = dw_ref[:, sl].reshape(nc, CHUNK, 128).astype(BF16)
            dvb = _bdot("gcj,gcv->gjv", tb, du)
            dkbg = _bdot("gcj,gck->gjk", tb, dw)
            dt = _bdot("gcv,gjv->gcj", du, vb.astype(BF16)) + _bdot("gck,gjk->gcj", dw, kbg.astype(BF16))
            tmp = _bdot("gac,gab->gcb", tb, dt.astype(BF16))
            dl = jnp.where(strict, -_bdot("gcb,gdb->gcd", tmp.astype(BF16), tb), 0.0)
            da = da_ref[h]
            dm = (dl * decay).astype(BF16)
            dqk = (da * decay).astype(BF16)
            kbf = k.astype(BF16)
            dkb = _bdot("gcj,gjd->gcd", dm, kbf) + dkbg * eg
            dk = (_bdot("gcj,gcd->gjd", dm, kb.astype(BF16)) + _bdot("gcj,gcd->gjd", dqk, q.astype(BF16))
                  + dk1_ref[:, sl].reshape(nc, CHUNK, 128) + dkb * beta)
            dq = _bdot("gcj,gjd->gcd", dqk, kbf) + dq1_ref[:, sl].reshape(nc, CHUNK, 128)
            e = dl * l_mat + da * a_mat
            dgc = (jnp.sum(e, axis=2, keepdims=True) - jnp.sum(jnp.swapaxes(e, 1, 2), axis=2, keepdims=True)
                   + jnp.sum(dkbg * kbg, axis=2, keepdims=True))
            dbeta = jnp.sum(dkb * k, axis=2, keepdims=True) + jnp.sum(dvb * v, axis=2, keepdims=True)
            o_ref[:, sl] = dq.reshape(R, 128)
            o_ref[:, 512 + h * 128:512 + (h + 1) * 128] = dk.reshape(R, 128)
            o_ref[:, 1024 + h * 128:1024 + (h + 1) * 128] = (dvb * beta).reshape(R, 128)
            dg_all = dg_all + jnp.where(lane == h, dgc.reshape(R, 1), 0.0)
            db_all = jnp.where(lane == 4 + h, dbeta.reshape(R, 1), db_all)
        t = _chunk_row(R)
        for s in (1, 2, 4, 8, 16, 32):
            dg_all = dg_all + jnp.where(t + s < CHUNK, pltpu.roll(dg_all, R - s, 0), 0.0)
        dgb_ref[...] = jnp.where(lane < 4, dg_all, db_all)

    rowb = lambda c, j: BS((R, c), lambda b, n: (b * nb + n, j))
    mat = BS((None, N_HEADS, nc, CHUNK, CHUNK), lambda b, n: (b, 0, n, 0, 0))
    return pl.pallas_call(
        body, name="gdn_chunk_bwd", grid=(B, nb),
        in_specs=[rowb(512, 0), rowb(512, 1), rowb(512, 2), rowb(128, 0),
                  BS((None, N_HEADS, nc, CHUNK), lambda b, n: (b, 0, n, 0)), mat, mat,
                  rowb(512, 0), rowb(512, 0), rowb(512, 0), rowb(512, 0), rowb(128, 0)],
        out_specs=[rowb(GDN_QKV, 0), rowb(128, 0)],
        out_shape=[jax.ShapeDtypeStruct((T, GDN_QKV), F32), jax.ShapeDtypeStruct((T, 128), F32)],
        compiler_params=_arb(2))(qkv, qkv, qkv, GB, Grow, Tinv, dA, dU, dW, dQ1, dK1, dG1)


def _gdn_out_norm(og, gg):
    outs, xhs, rs = [], [], []
    for h in range(N_HEADS):
        seg = og[:, h * 128:(h + 1) * 128]
        r = lax.rsqrt(jnp.mean(seg * seg, axis=-1, keepdims=True) + EPS)
        xh = seg * r
        outs.append(xh * gg)
        xhs.append(xh)
        rs.append(r)
    return outs, xhs, rs


def merge_fwd(o_mla, o_gdn, o_mem, P, x, tgt, w_out, g_gdn, g_fin, tm=256):
    T = x.shape[0]
    tm = min(tm, T)

    def body(om_ref, og_ref, oc_ref, gate_ref, x_ref, t_ref, w_ref, gg_ref, gf_ref, mix_ref, dx_ref, dxb_ref, sq_ref, gnf_ref):
        @pl.when(pl.program_id(0) == 0)
        def _():
            sq_ref[...] = jnp.zeros_like(sq_ref)
            gnf_ref[...] = jnp.zeros_like(gnf_ref)

        ogn, _, _ = _gdn_out_norm(og_ref[...], gg_ref[...])
        cat = jnp.concatenate([om_ref[...]] + ogn + [oc_ref[...]], axis=1)
        gt = gate_ref[...]
        mixed = (cat * (gt * _sigmoid(gt))).astype(BF16)
        mix_ref[...] = mixed
        x2 = x_ref[...] + _dot(mixed, w_ref[...], NN)
        r2 = lax.rsqrt(jnp.mean(x2 * x2, axis=-1, keepdims=True) + EPS)
        xh = x2 * r2
        gf = gf_ref[...]
        diff = xh * gf - t_ref[...]
        sq_ref[...] += jnp.sum(diff * diff, axis=0, keepdims=True)
        dy = diff * (1.0 / D_MODEL)
        gnf_ref[...] += jnp.sum(dy * xh, axis=0, keepdims=True)
        dxh = dy * gf
        dx = r2 * (dxh - xh * jnp.mean(dxh * xh, axis=-1, keepdims=True))
        dx_ref[...] = dx
        dxb_ref[...] = dx.astype(BF16)

    rowb = lambda c, j=0: BS((tm, c), lambda i: (i, j))
    full = lambda r, c: BS((r, c), lambda i: (0, 0))
    return pl.pallas_call(
        body, name="merge_fwd", grid=(T // tm,),
        in_specs=[rowb(512), rowb(512), rowb(512), rowb(D_MIX, OFF_GATE // D_MIX), rowb(D_MODEL), rowb(D_MODEL),
                  full(D_MIX, D_MODEL), full(1, 128), full(1, D_MODEL)],
        out_specs=[rowb(D_MIX), rowb(D_MODEL), rowb(D_MODEL), full(1, D_MODEL), full(1, D_MODEL)],
        out_shape=[jax.ShapeDtypeStruct((T, D_MIX), BF16), jax.ShapeDtypeStruct((T, D_MODEL), F32),
                   jax.ShapeDtypeStruct((T, D_MODEL), BF16),
                   jax.ShapeDtypeStruct((1, D_MODEL), F32), jax.ShapeDtypeStruct((1, D_MODEL), F32)],
        compiler_params=_arb(1))(o_mla, o_gdn, o_mem, P, x, tgt, w_out, g_gdn, g_fin)


def merge_bwd(dx2, o_mla, o_gdn, o_mem, P, w_out, g_gdn, tm=256):
    T = dx2.shape[0]
    tm = min(tm, T)

    def body(dx_ref, om_ref, og_ref, oc_ref, gate_ref, w_ref, gg_ref, dgate_ref, dom_ref, dog_ref, doc_ref, ggn_ref):
        @pl.when(pl.program_id(0) == 0)
        def _():
            ggn_ref[...] = jnp.zeros_like(ggn_ref)

        gg = gg_ref[...]
        dmix = _dot(dx_ref[...].astype(BF16), w_ref[...], NT)
        ogn, xhs, rs = _gdn_out_norm(og_ref[...], gg)
        cat = jnp.concatenate([om_ref[...]] + ogn + [oc_ref[...]], axis=1)
        gt = gate_ref[...]
        sg = _sigmoid(gt)
        dgate_ref[...] = (dmix * cat * (sg * (1.0 + gt * (1.0 - sg)))).astype(BF16)
        dcat = dmix * (gt * sg)
        dom_ref[...] = dcat[:, :512]
        doc_ref[...] = dcat[:, 1024:]
        acc = jnp.zeros((1, 128), F32)
        for h in range(N_HEADS):
            dseg = dcat[:, 512 + h * 128:512 + (h + 1) * 128]
            acc = acc + jnp.sum(dseg * xhs[h], axis=0, keepdims=True)
            dxh = dseg * gg
            dog_ref[:, h * 128:(h + 1) * 128] = rs[h] * (dxh - xhs[h] * jnp.mean(dxh * xhs[h], axis=-1, keepdims=True))
        ggn_ref[...] += acc

    rowb = lambda c, j=0: BS((tm, c), lambda i: (i, j))
    full = lambda r, c: BS((r, c), lambda i: (0, 0))
    return pl.pallas_call(
        body, name="merge_bwd", grid=(T // tm,),
        in_specs=[rowb(D_MODEL), rowb(512), rowb(512), rowb(512), rowb(D_MIX, OFF_GATE // D_MIX),
                  full(D_MIX, D_MODEL), full(1, 128)],
        out_specs=[rowb(D_MIX), rowb(512), rowb(512), rowb(512), full(1, 128)],
        out_shape=[jax.ShapeDtypeStruct((T, D_MIX), BF16)] + [jax.ShapeDtypeStruct((T, 512), F32)] * 3
        + [jax.ShapeDtypeStruct((1, 128), F32)],
        compiler_params=_arb(1))(dx2, o_mla, o_gdn, o_mem, P, w_out, g_gdn)


def in_norm_bwd(x, dh, dx2, gain, tm=256):
    T, n = x.shape
    tm = min(tm, T)

    def body(x_ref, dh_ref, dx2_ref, g_ref, o_ref, acc_ref):
        @pl.when(pl.program_id(0) == 0)
        def _():
            acc_ref[...] = jnp.zeros_like(acc_ref)

        xv = x_ref[...]
        r = lax.rsqrt(jnp.mean(xv * xv, axis=-1, keepdims=True) + EPS)
        xh = xv * r
        dy = dh_ref[...]
        acc_ref[...] += jnp.sum(dy * xh, axis=0, keepdims=True)
        dxh = dy * g_ref[...]
        o_ref[...] = dx2_ref[...] + r * (dxh - xh * jnp.mean(dxh * xh, axis=-1, keepdims=True))

    rowb = BS((tm, n), lambda i: (i, 0))
    full = BS((1, n), lambda i: (0, 0))
    return pl.pallas_call(
        body, name="in_norm_bwd", grid=(T // tm,),
        in_specs=[rowb, rowb, rowb, full], out_specs=[rowb, full],
        out_shape=[jax.ShapeDtypeStruct((T, n), F32), jax.ShapeDtypeStruct((1, n), F32)],
        compiler_params=_arb(1))(x, dh, dx2, gain)


W_IN_SHARD = D_IN // 4
_GDN0 = Q_LORA + KV_LORA + MLA_ROPE
_AB0 = _GDN0 + GDN_QKV
_MEMQ0 = _AB0 + 2 * N_HEADS
_GATE0 = _MEMQ0 + N_HEADS * MEM_DH


def _pad_w_in_t(s):
    z = lambda n: jnp.zeros((n, s[0].shape[1]), s[0].dtype)
    a, m, gt = _AB0 - 2 * W_IN_SHARD, _MEMQ0 - 2 * W_IN_SHARD, _GATE0 - 2 * W_IN_SHARD
    return jnp.concatenate([s[0][:640], s[0][640:672], z(32), s[0][672:704], z(32), s[2][a:m], z(248), s[2][m:gt],
                            s[0][_GDN0:], s[1], s[2][:a], s[2][gt:], s[3]], axis=0)


def _unpad_w_in_t(g):
    e0 = OFF_GDN + W_IN_SHARD - _GDN0
    e1 = e0 + W_IN_SHARD
    gt = OFF_GATE + 3 * W_IN_SHARD - _GATE0
    return jnp.stack([jnp.concatenate([g[0:640], g[640:672], g[704:736], g[OFF_GDN:e0]]), g[e0:e1],
                      jnp.concatenate([g[e1:OFF_GATE], g[768:776], g[OFF_MEMQ:OFF_MEMQ + 512], g[OFF_GATE:gt]]), g[gt:]])


def _pad_w_q_b_t(s):
    z = jnp.zeros((32, s.shape[2]), s.dtype)
    parts = []
    for h in range(N_HEADS):
        parts += [s[h, :128], s[h, 128:160], z, s[h, 160:192], z]
    return jnp.concatenate(parts, axis=0)


def _unpad_w_q_b_t(g):
    return jnp.stack([jnp.concatenate([g[h * HEAD_PAD:h * HEAD_PAD + 128], g[h * HEAD_PAD + 128:h * HEAD_PAD + 160],
                                       g[h * HEAD_PAD + 192:h * HEAD_PAD + 224]]) for h in range(N_HEADS)])


def _perm_w_kv_b(s):
    return jnp.concatenate([s[h, :, :128] for h in range(N_HEADS)] + [s[h, :, 128:] for h in range(N_HEADS)], axis=1)


def _unperm_w_kv_b(g):
    return jnp.stack([jnp.concatenate([g[:, h * 128:(h + 1) * 128], g[:, 512 + h * 128:512 + (h + 1) * 128]], axis=1)
                      for h in range(N_HEADS)])


def _lane_row(v4):
    return jnp.pad(v4.reshape(1, -1).astype(F32), ((0, 0), (0, 128 - v4.size)))


def _pack(pieces, n_rows):
    flat = jnp.concatenate([p.reshape(-1) for p in pieces])
    return jnp.pad(flat, (0, n_rows * 1024 - flat.size)).reshape(n_rows, 1024)


def _unpack(block, shapes):
    flat = block.reshape(-1)
    out, off = [], 0
    for shp in shapes:
        n = int(np.prod(shp))
        out.append(flat[off:off + n].reshape(shp))
        off += n
    return out


N_CHIPS = 4
MESH = pl.DeviceIdType.MESH
ANY = BS(memory_space=pl.ANY)


def _place():
    return lax.axis_index("x"), lax.axis_index("y"), lax.axis_index("c")


def _other_chips(x, y):
    return [(1 - x, y), (x, 1 - y), (1 - x, 1 - y)]


def _half(split, which):
    axis, size = split
    ds = pl.ds(pl.multiple_of(which * size, 16 if axis == 0 else 128), size)
    return (ds, slice(None)) if axis == 0 else (slice(None), ds)


def allgather_chips(shards, splits, name):
    n = len(shards)

    def body(*refs):
        s_refs, o_refs = refs[:n], refs[n:2 * n]
        send_sems, recv_sems, local_sems = refs[2 * n:]
        x, y, c = _place()
        chips = _other_chips(x, y)

        def copy(k, src, dst, to):
            return pltpu.make_async_remote_copy(src_ref=src, dst_ref=dst, send_sem=send_sems.at[k], recv_sem=recv_sems.at[k],
                                                device_id=to, device_id_type=MESH)

        owns, first, passed = [], [], []
        for i, (s_ref, o_ref) in enumerate(zip(s_refs, o_refs)):
            mine = _half(splits[i], c)
            owns.append(pltpu.make_async_copy(s_ref, o_ref.at[2 * x + y], local_sems.at[i]))
            owns[-1].start()
            for j, (px, py) in enumerate(chips):
                first.append(copy(6 * i + j, s_ref.at[mine], o_ref.at[(2 * x + y,) + mine], (px, py, c)))
                first[-1].start()
        for i, (s_ref, o_ref) in enumerate(zip(s_refs, o_refs)):
            mine = _half(splits[i], c)
            for j, (px, py) in enumerate(chips):
                landed = o_ref.at[(2 * px + py,) + mine]
                copy(6 * i + j, s_ref.at[mine], landed, (px, py, c)).wait_recv()
                passed.append(copy(6 * i + 3 + j, landed, landed, (x, y, 1 - c)))
                passed[-1].start()
        for i, (s_ref, o_ref) in enumerate(zip(s_refs, o_refs)):
            theirs = _half(splits[i], 1 - c)
            for j, (px, py) in enumerate(chips):
                copy(6 * i + 3 + j, s_ref.at[theirs], o_ref.at[(2 * px + py,) + theirs], (x, y, 1 - c)).wait_recv()
        for cp in first + passed:
            cp.wait_send()
        for cp in owns:
            cp.wait()

    return pl.pallas_call(
        body, name=name, in_specs=[ANY] * n, out_specs=[ANY] * n,
        out_shape=[jax.ShapeDtypeStruct((N_CHIPS,) + s.shape, s.dtype) for s in shards],
        scratch_shapes=[pltpu.SemaphoreType.DMA((6 * n,)), pltpu.SemaphoreType.DMA((6 * n,)), pltpu.SemaphoreType.DMA((n,))])(*shards)


def allgather_devices(block, name):
    R, C = block.shape

    def body(b_ref, o_ref, send_sems, recv_sems, local_sem):
        x, y, c = _place()
        me = 4 * x + 2 * y + c
        own = pltpu.make_async_copy(b_ref, o_ref.at[me], local_sem)
        own.start()
        copies = []
        for r in range(1, 8):
            px = 1 - x if r & 4 else x
            py = 1 - y if r & 2 else y
            pc = 1 - c if r & 1 else c
            send = pltpu.make_async_remote_copy(src_ref=b_ref, dst_ref=o_ref.at[me], send_sem=send_sems.at[r - 1],
                                                recv_sem=recv_sems.at[r - 1], device_id=(px, py, pc), device_id_type=MESH)
            recv = pltpu.make_async_remote_copy(src_ref=b_ref, dst_ref=o_ref.at[4 * px + 2 * py + pc], send_sem=send_sems.at[r - 1],
                                                recv_sem=recv_sems.at[r - 1], device_id=(px, py, pc), device_id_type=MESH)
            send.start()
            copies.append((send, recv))
        for send, recv in copies:
            recv.wait_recv()
            send.wait_send()
        own.wait()

    return pl.pallas_call(
        body, name=name, in_specs=[ANY], out_specs=ANY, out_shape=jax.ShapeDtypeStruct((8, R, C), block.dtype),
        scratch_shapes=[pltpu.SemaphoreType.DMA((7,)), pltpu.SemaphoreType.DMA((7,)), pltpu.SemaphoreType.DMA(())])(block)


def swap_sibling(arrs, name, splits=None):
    n = len(arrs)

    def sent(a_ref, i, c):
        return a_ref if splits is None else a_ref.at[(slice(None),) + _half(splits[i], 1 - c)]

    def out_shape(a, i):
        if splits is None:
            return a.shape
        axis, size = splits[i]
        return (a.shape[0], size, a.shape[2]) if axis == 0 else (a.shape[0], a.shape[1], size)

    def body(*refs):
        a_refs, o_refs = refs[:n], refs[n:2 * n]
        send_sems, recv_sems = refs[2 * n:]
        x, y, c = _place()
        copies = [pltpu.make_async_remote_copy(src_ref=sent(a_ref, i, c), dst_ref=o_ref, send_sem=send_sems.at[i],
                                               recv_sem=recv_sems.at[i], device_id=(x, y, 1 - c), device_id_type=MESH)
                  for i, (a_ref, o_ref) in enumerate(zip(a_refs, o_refs))]
        for cp in copies:
            cp.start()
        for cp in copies:
            cp.wait()

    return pl.pallas_call(
        body, name=name, in_specs=[ANY] * n, out_specs=[ANY] * n,
        out_shape=[jax.ShapeDtypeStruct(out_shape(a, i), a.dtype) for i, a in enumerate(arrs)],
        scratch_shapes=[pltpu.SemaphoreType.DMA((n,)), pltpu.SemaphoreType.DMA((n,))])(*arrs)


def exchange_chips(parts, name):
    n = len(parts)

    def body(*refs):
        p_refs, o_refs = refs[:n], refs[n:2 * n]
        send_sems, recv_sems = refs[2 * n:]
        x, y, c = _place()
        copies = [pltpu.make_async_remote_copy(src_ref=p_ref.at[2 * px + py], dst_ref=o_ref.at[j], send_sem=send_sems.at[3 * i + j],
                                               recv_sem=recv_sems.at[3 * i + j], device_id=(px, py, c), device_id_type=MESH)
                  for i, (p_ref, o_ref) in enumerate(zip(p_refs, o_refs)) for j, (px, py) in enumerate(_other_chips(x, y))]
        for cp in copies:
            cp.start()
        for cp in copies:
            cp.wait()

    return pl.pallas_call(
        body, name=name, in_specs=[ANY] * n, out_specs=[ANY] * n,
        out_shape=[jax.ShapeDtypeStruct((3,) + p.shape[1:], p.dtype) for p in parts],
        scratch_shapes=[pltpu.SemaphoreType.DMA((3 * n,)), pltpu.SemaphoreType.DMA((3 * n,))])(*parts)


def _half_block(shape2, split):
    axis, size = split
    return (size, shape2[1]) if axis == 0 else (shape2[0], size)


def add_pairs(parts, halves, splits, core, name):
    n = len(parts)

    def body(s_ref, *refs):
        for a_ref, b_ref, o_ref in zip(refs[:n], refs[n:2 * n], refs[2 * n:]):
            o_ref[...] = (a_ref[...].astype(F32) + b_ref[...].astype(F32)).astype(BF16)

    def mine(i):
        blk = (None,) + _half_block(parts[i].shape[1:], splits[i])
        if splits[i][0] == 0:
            return BS(blk, lambda q, s: (q, s[0], 0))
        return BS(blk, lambda q, s: (q, 0, s[0]))

    half_specs = [BS((None,) + h.shape[1:], lambda q, s: (q, 0, 0)) for h in halves]
    return pl.pallas_call(
        body, name=name,
        grid_spec=pltpu.PrefetchScalarGridSpec(num_scalar_prefetch=1, grid=(N_CHIPS,),
                                               in_specs=[mine(i) for i in range(n)] + half_specs, out_specs=half_specs),
        out_shape=[jax.ShapeDtypeStruct(h.shape, BF16) for h in halves], compiler_params=_arb(1))(core, *parts, *halves)


def add_fives(parts, halves, from_chips, splits, chip_core, name):
    n = len(parts)

    def body(s_ref, *refs):
        for a_ref, b_ref, p_ref, o_ref in zip(refs[:n], refs[n:2 * n], refs[2 * n:3 * n], refs[3 * n:]):
            s = a_ref[...].astype(F32) + b_ref[...].astype(F32)
            for j in range(3):
                s = s + p_ref[j].astype(F32)
            o_ref[...] = s

    def mine(i):
        blk = (None,) + _half_block(parts[i].shape[1:], splits[i])
        if splits[i][0] == 0:
            return BS(blk, lambda g, s: (s[0], s[1], 0))
        return BS(blk, lambda g, s: (s[0], 0, s[1]))

    half_specs = [BS((None,) + h.shape[1:], lambda g, s: (s[0], 0, 0)) for h in halves]
    chip_specs = [BS(p.shape, lambda g, s: (0, 0, 0)) for p in from_chips]
    out_specs = [BS(h.shape[1:], lambda g, s: (0, 0)) for h in halves]
    return pl.pallas_call(
        body, name=name,
        grid_spec=pltpu.PrefetchScalarGridSpec(num_scalar_prefetch=1, grid=(1,),
                                               in_specs=[mine(i) for i in range(n)] + half_specs + chip_specs, out_specs=out_specs),
        out_shape=[jax.ShapeDtypeStruct(h.shape[1:], F32) for h in halves], compiler_params=_arb(1))(chip_core, *parts, *halves, *from_chips)


def sum_leading(a, name):
    def body(a_ref, o_ref):
        s = a_ref[0]
        for j in range(1, a.shape[0]):
            s = s + a_ref[j]
        o_ref[...] = s

    return pl.pallas_call(body, name=name, out_shape=jax.ShapeDtypeStruct(a.shape[1:], a.dtype))(a)


def _adamw_math(w, g, m, v):
    mn = ADAM_B1 * m + (1.0 - ADAM_B1) * g
    vn = ADAM_B2 * v + (1.0 - ADAM_B2) * (g * g)
    m_hat = mn / (1.0 - ADAM_B1 ** ADAM_STEP)
    v_hat = vn / (1.0 - ADAM_B2 ** ADAM_STEP)
    return -ADAM_LR * (m_hat / (jnp.sqrt(v_hat) + ADAM_EPS) + ADAM_WD * w), mn, vn


def adamw(w, g, m, v, name):
    R, C = g.shape
    lead = (None,) * (w.ndim - 2)

    def body(w_ref, g_ref, m_ref, v_ref, d_ref, mo_ref, vo_ref):
        d_ref[...], mo_ref[...], vo_ref[...] = _adamw_math(w_ref[...], g_ref[...], m_ref[...], v_ref[...])

    wblk = BS(lead + (R, C), lambda i: (0,) * w.ndim)
    gblk = BS((R, C), lambda i: (0, 0))
    return pl.pallas_call(
        body, name=name, grid=(1,), in_specs=[wblk, gblk, wblk, wblk], out_specs=[wblk] * 3,
        out_shape=[jax.ShapeDtypeStruct(w.shape, F32)] * 3, compiler_params=_arb(1))(w, g, m, v)


def adamw_halves(w, mine, other, m, v, split, core, name):
    R, C = w.shape[-2:]
    axis, size = split
    lead = (None,) * (w.ndim - 2)
    zeros = (0,) * (w.ndim - 2)
    if axis == 0:
        tr = size if size <= 256 else next(t for t in range(256, 7, -1) if size % t == 0 and t % 8 == 0)
        nb = size // tr
        whole = BS(lead + (tr, C), lambda hi, j, s: zeros + (hi * nb + j, 0))
        part = BS((tr, C), lambda hi, j, s: (j, 0))
    else:
        nb = size // 128
        whole = BS(lead + (R, 128), lambda hi, j, s: zeros + (0, hi * nb + j))
        part = BS((R, 128), lambda hi, j, s: (0, j))

    def body(s_ref, w_ref, a_ref, b_ref, m_ref, v_ref, g_ref, d_ref, mo_ref, vo_ref):
        g = jnp.where(pl.program_id(0) == s_ref[0], a_ref[...], b_ref[...])
        g_ref[...] = g
        d_ref[...], mo_ref[...], vo_ref[...] = _adamw_math(w_ref[...], g, m_ref[...], v_ref[...])

    return pl.pallas_call(
        body, name=name,
        grid_spec=pltpu.PrefetchScalarGridSpec(num_scalar_prefetch=1, grid=(2, nb),
                                               in_specs=[whole, part, part, whole, whole], out_specs=[whole] * 4),
        out_shape=[jax.ShapeDtypeStruct(w.shape, F32)] * 4, compiler_params=_arb(2))(core, w, mine, other, m, v)


def dense_bf16(w3, name):
    R, _, K = w3.shape

    def body(w_hbm, o_ref, buf, sem):
        cp = pltpu.make_async_copy(w_hbm.at[:, 0], buf, sem)
        cp.start()
        cp.wait()
        o_ref[...] = buf[...].astype(BF16)

    return pl.pallas_call(
        body, name=name, in_specs=[ANY], out_specs=BS(memory_space=pltpu.VMEM), out_shape=jax.ShapeDtypeStruct((R, K), BF16),
        scratch_shapes=[pltpu.VMEM((R, K), F32), pltpu.SemaphoreType.DMA(())])(w3)


ROW_BLOCK = 184


def adamw_untiled_rows(w3, mine, other, m3, v3, name):
    R, _, K = w3.shape
    kh = K // 2
    starts = list(range(0, R, ROW_BLOCK))
    sizes = [min(ROW_BLOCK, R - s) for s in starts]
    nblk = len(starts)

    def body(w_hbm, a_ref, b_ref, m_hbm, v_hbm, g_hbm, d_hbm, mo_hbm, vo_hbm,
             wbuf, mbuf, vbuf, gbuf, dbuf, mobuf, vobuf, in_sems, out_sems):
        first = lax.axis_index("c") == 0
        ins = []
        for k, (r0, n) in enumerate(zip(starts, sizes)):
            rows = pl.ds(r0, n)
            cps = [pltpu.make_async_copy(src.at[rows, 0], dst.at[rows], in_sems.at[3 * k + i])
                   for i, (src, dst) in enumerate(((w_hbm, wbuf), (m_hbm, mbuf), (v_hbm, vbuf)))]
            for cp in cps:
                cp.start()
            ins.append(cps)

        def update(rows):
            a, b = a_ref[rows, :], b_ref[rows, :]
            g = jnp.concatenate([jnp.where(first, a, b), jnp.where(first, b, a)], axis=1)
            gbuf[rows, :] = g
            dbuf[rows, :], mobuf[rows, :], vobuf[rows, :] = _adamw_math(wbuf[rows, :], g, mbuf[rows, :], vbuf[rows, :])

        outs = []
        for k, (r0, n) in enumerate(zip(starts, sizes)):
            for cp in ins[k]:
                cp.wait()
            groups, tail = n // 8, n % 8

            def group(i, carry, r0=r0):
                update(pl.ds(pl.multiple_of(r0 + i * 8, 8), 8))
                return carry

            lax.fori_loop(0, groups, group, 0)
            if tail:
                update(pl.ds(r0 + groups * 8, tail))
            rows = pl.ds(r0, n)
            cps = [pltpu.make_async_copy(src.at[rows], dst.at[rows, 0], out_sems.at[4 * k + i])
                   for i, (src, dst) in enumerate(((gbuf, g_hbm), (dbuf, d_hbm), (mobuf, mo_hbm), (vobuf, vo_hbm)))]
            for cp in cps:
                cp.start()
            outs += cps
        for cp in outs:
            cp.wait()

    vmem = BS(memory_space=pltpu.VMEM)
    return pl.pallas_call(
        body, name=name, in_specs=[ANY, vmem, vmem, ANY, ANY], out_specs=[ANY] * 4,
        out_shape=[jax.ShapeDtypeStruct(w3.shape, F32)] * 4,
        scratch_shapes=[pltpu.VMEM((R, K), F32)] * 7 + [pltpu.SemaphoreType.DMA((3 * nblk,)), pltpu.SemaphoreType.DMA((4 * nblk,))])(
            w3, mine, other, m3, v3)


def adamw_w_q_b(w, mine, other, m, v, name):
    def body(w_ref, a_ref, b_ref, m_ref, v_ref, g_ref, d_ref, mo_ref, vo_ref):
        first = lax.axis_index("c") == 0
        lo = jnp.where(first, a_ref[...], b_ref[...])
        hi = jnp.where(first, b_ref[...], a_ref[...])
        g = jnp.concatenate([lo, hi[0:32], hi[64:96]], axis=0)
        g_ref[...] = g
        d_ref[...], mo_ref[...], vo_ref[...] = _adamw_math(w_ref[...], g, m_ref[...], v_ref[...])

    return pl.pallas_call(body, name=name, out_shape=[jax.ShapeDtypeStruct(w.shape, F32)] * 4)(w, mine, other, m, v)


def local_step(x, mem, positions, tgt, norm_in, w_in, q_a_norm, w_q_b, kv_a_norm, w_kv_b, gdn_conv, gdn_a_log,
               gdn_dt_bias, gdn_norm, mem_norm, w_mem_kv, w_out, norm_final):
    B, S, D = x.shape
    M = mem.shape[1]
    T = B * S
    N = S // CHUNK
    x2d = x.reshape(T, D)
    mem2d = mem.reshape(B * M, D)
    tgt2d = tgt.reshape(T, D)

    wp, wq, wkv = w_in, w_q_b, w_kv_b
    alog_row, dt_row = _lane_row(gdn_a_log), _lane_row(gdn_dt_bias)

    half = MLA_ROPE // 2
    inv_freq = 1.0 / (ROPE_THETA ** (jnp.arange(half, dtype=F32) / half))
    z32 = jnp.zeros((half,), F32)
    o32 = jnp.ones((half,), F32)
    inv_row = jnp.concatenate([inv_freq, z32, inv_freq, z32]).reshape(1, 128)
    sgn_row = jnp.concatenate([-o32, z32, o32, z32]).reshape(1, 128)
    msk_row = jnp.concatenate([o32, z32, o32, z32]).reshape(1, 128)
    cos_t, sin_t = rope_tables(positions.reshape(T, 1), inv_row, sgn_row, msk_row)

    h = rms_fwd(x2d, norm_in, "rms_in")
    P = mm(h, wp, "nt", F32, "in_proj", bm=512, bn=1536, n_outer=True)
    Q, K, V, qn, kvn = mla_prep(P, q_a_norm, kv_a_norm, wq, wkv, cos_t, sin_t)
    o_mla, lse = mla_attn_fwd(Q, K, V, B, S)
    memn = rms_fwd(mem2d, mem_norm, "rms_mem")
    MKV = mm(memn, w_mem_kv, "nn", BF16, "mem_kv_proj")
    o_mem = mem_attn_fwd(P, MKV, B, S, M)
    qkv = gdn_prep_fwd(P, gdn_conv, B, S)
    GB = gdn_gate_fwd(P, alog_row, dt_row, B, S)
    Grow = jnp.transpose(GB[:, :N_HEADS].reshape(B, N, CHUNK, N_HEADS), (0, 3, 1, 2))
    U, W, Tinv, A = gdn_chunk_fwd(qkv, GB, Grow, B, S)
    qkv3, GB3 = qkv.reshape(B, S, GDN_QKV), GB.reshape(B, S, 128)
    W3 = W.reshape(B, S, 512)
    o_gdn3, Vn3, St = gdn_scan_fwd(qkv3, U.reshape(B, S, 512), W3, GB3, A, B, S)
    o_gdn = o_gdn3.reshape(T, 512)
    mixed, dx2, dx2b, sq, g_norm_final = merge_fwd(o_mla, o_gdn, o_mem, P, x2d, tgt2d, w_out, gdn_norm, norm_final.reshape(1, D))

    g_w_out = mm(mixed, dx2b, "tn", BF16, "grad_w_out")
    dgate, do_mla, do_gdn, do_mem, g_gdn_norm = merge_bwd(dx2b, o_mla, o_gdn, o_mem, P, w_out, gdn_norm)

    dmemq, dMKV = mem_attn_bwd(P, MKV, do_mem, B, S, M)
    g_w_mem_kv = mm(memn, dMKV, "tn", BF16, "grad_w_mem_kv")
    dmemn = mm(dMKV, w_mem_kv, "nt", F32, "d_memn")
    g_mem_norm = gain_grad(mem2d, dmemn, "grad_mem_norm")

    dU3, dW3, dQ13, dK13, dA, dG13 = gdn_scan_bwd(do_gdn.reshape(B, S, 512), qkv3, W3, Vn3, GB3, A, St, B, S)
    r2 = lambda a: a.reshape(T, a.shape[-1])
    dqkv, dGB = gdn_chunk_bwd(qkv, GB, Grow, Tinv, dA, r2(dU3), r2(dW3), r2(dQ13), r2(dK13), r2(dG13), B, S)
    dPg, g_conv = gdn_prep_bwd(P, dqkv, gdn_conv, B, S)
    dab, g_ab = gdn_gate_bwd(P, dGB, alog_row, dt_row, B, S)

    dQ, dK, dV = mla_attn_bwd(Q, K, V, o_mla, do_mla, lse, B, S)
    dq_lin, dkv_lin, dkr = mla_post_bwd(dQ, dK, dV, cos_t, sin_t)
    dqn = mm(dq_lin, wq, "nn", F32, "d_qn")
    dkvn = mm(dkv_lin, wkv, "nt", F32, "d_kvn")
    g_wq = mm(dq_lin, qn, "tn", BF16, "grad_w_q_b")
    g_wkv = mm(kvn, dkv_lin, "tn", BF16, "grad_w_kv_b")
    dPm, g_q_a_norm, g_kv_a_norm = mla_norm_bwd(P, dqn, dkvn, dkr, dab, q_a_norm, kv_a_norm)

    dP = jnp.concatenate([dPm, dmemq, dPg, dgate], axis=1)
    g_wp = mm(dP, h, "tn", BF16, "grad_w_in")
    dh = mm(dP, wp, "nn", F32, "d_h")
    grad_x, g_norm_in = in_norm_bwd(x2d, dh, dx2, norm_in)

    grads = dict(
        norm_in=g_norm_in, w_in=g_wp, q_a_norm=g_q_a_norm, w_q_b=g_wq, kv_a_norm=g_kv_a_norm, w_kv_b=g_wkv, gdn_conv=g_conv,
        gdn_a_log=g_ab[0:1, :N_HEADS], gdn_dt_bias=g_ab[1:2, :N_HEADS], gdn_norm=g_gdn_norm,
        mem_norm=g_mem_norm, w_mem_kv=g_w_mem_kv, w_out=g_w_out, norm_final=g_norm_final)
    return sq, grad_x.reshape(B, S, D), grads


def kernel(x, mem, positions, norm_in, w_in, q_a_norm, w_q_b, kv_a_norm, w_kv_b, gdn_conv, gdn_a_log, gdn_dt_bias, gdn_norm, mem_norm, w_mem_kv, w_out, norm_final, loss_target, m_norm_in, m_w_in, m_q_a_norm, m_w_q_b, m_kv_a_norm, m_w_kv_b, m_gdn_conv, m_gdn_a_log, m_gdn_dt_bias, m_gdn_norm, m_mem_norm, m_w_mem_kv, m_w_out, m_norm_final, v_norm_in, v_w_in, v_q_a_norm, v_w_q_b, v_kv_a_norm, v_w_kv_b, v_gdn_conv, v_gdn_a_log, v_gdn_dt_bias, v_gdn_norm, v_mem_norm, v_w_mem_kv, v_w_out, v_norm_final):
    B = x.shape[0]
    cx, cy, cc = lax.axis_index("x"), lax.axis_index("y"), lax.axis_index("c")
    chip = 2 * cx + cy

    big_names = ("w_in", "w_q_b", "w_kv_b", "w_mem_kv", "w_out")
    rows_major = lambda a: jnp.transpose(a, (2, 0, 1))
    w_in3, m_in3, v_in3 = rows_major(w_in), rows_major(m_w_in), rows_major(v_w_in)
    w_qb_t, m_qb_t, v_qb_t = jnp.transpose(w_q_b[0]), jnp.transpose(m_w_q_b[0]), jnp.transpose(v_w_q_b[0])
    z32 = jnp.zeros((32, Q_LORA), BF16)
    qb_bf = w_qb_t.astype(BF16)
    qb_padded = jnp.concatenate([qb_bf[:160], z32, qb_bf[160:], z32])
    shards = [dense_bf16(w_in3, "w_in_bf16"), qb_padded, w_kv_b[0].astype(BF16), w_mem_kv[0].astype(BF16), w_out[0].astype(BF16)]
    splits = [(1, D_MODEL // 2)] + [(0, s.shape[0] // 2) for s in shards[1:]]
    g_in, g_qb, g_kvb, g_mem, g_out_w = allgather_chips(shards, splits, "allgather_weights")
    conv_all = allgather_devices(_pack([gdn_conv[0]], 8), "allgather_conv")
    conv_shape = gdn_conv[0].shape
    conv_full = jnp.concatenate([_unpack(conv_all[2 * q], [conv_shape])[0] for q in range(N_CHIPS)], axis=1)

    sq, grad_x, g = local_step(x, mem, positions, loss_target, norm_in, _pad_w_in_t([g_in[q] for q in range(N_CHIPS)]), q_a_norm,
                               g_qb.reshape(-1, Q_LORA), kv_a_norm, _perm_w_kv_b(g_kvb), conv_full, gdn_a_log, gdn_dt_bias,
                               gdn_norm, mem_norm, g_mem.reshape(-1, g_mem.shape[2]), g_out_w.reshape(-1, g_out_w.shape[2]),
                               norm_final)
    loss = lax.psum(0.5 * jnp.sum(sq) / D_MODEL, ("x", "y", "c"))

    core = jnp.stack([cc]).astype(jnp.int32)
    chip_core = jnp.stack([chip, cc]).astype(jnp.int32)
    parts = [_unpad_w_in_t(g["w_in"]), g["w_q_b"].reshape(g_qb.shape), _unperm_w_kv_b(g["w_kv_b"]),
             g["w_mem_kv"].reshape(g_mem.shape), g["w_out"].reshape(g_out_w.shape)]
    from_sibling = swap_sibling(parts, "rs_sibling_partial", splits)
    chip_sums = add_pairs(parts, from_sibling, splits, core, "rs_add_sibling")
    from_chips = exchange_chips(chip_sums, "rs_exchange_chips")
    my_half = add_fives(parts, from_sibling, from_chips, splits, chip_core, "rs_add_chips")
    other_half = swap_sibling(my_half, "rs_sibling_final")

    small_names = ("norm_in", "q_a_norm", "kv_a_norm", "gdn_a_log", "gdn_dt_bias", "gdn_norm", "mem_norm", "norm_final")
    small = dict(norm_in=norm_in, q_a_norm=q_a_norm, kv_a_norm=kv_a_norm, gdn_a_log=gdn_a_log, gdn_dt_bias=gdn_dt_bias,
                 gdn_norm=gdn_norm, mem_norm=mem_norm, norm_final=norm_final)
    m_small = dict(norm_in=m_norm_in, q_a_norm=m_q_a_norm, kv_a_norm=m_kv_a_norm, gdn_a_log=m_gdn_a_log,
                   gdn_dt_bias=m_gdn_dt_bias, gdn_norm=m_gdn_norm, mem_norm=m_mem_norm, norm_final=m_norm_final)
    v_small = dict(norm_in=v_norm_in, q_a_norm=v_q_a_norm, kv_a_norm=v_kv_a_norm, gdn_a_log=v_gdn_a_log,
                   gdn_dt_bias=v_gdn_dt_bias, gdn_norm=v_gdn_norm, mem_norm=v_mem_norm, norm_final=v_norm_final)
    rows = lambda d: jnp.stack([jnp.pad(d[n].reshape(-1), (0, 1024 - d[n].size)) for n in small_names])
    conv_rows = GDN_CONV * GDN_QKV // 1024
    g_block = jnp.concatenate([rows(g), g["gdn_conv"].reshape(conv_rows, 1024), jnp.zeros((16 - 8 - conv_rows, 1024), F32)])
    g_block = sum_leading(allgather_devices(g_block, "allgather_small_grads"), "sum_small_grads")
    g_small_rows = g_block[:8]
    conv_cols = gdn_conv.shape[2]
    g_conv = lax.dynamic_slice_in_dim(g_block[8:8 + conv_rows].reshape(GDN_CONV, GDN_QKV), chip * conv_cols, conv_cols, axis=1)
    d_s, m_s, v_s = adamw(rows(small), g_small_rows, rows(m_small), rows(v_small), "adamw_small")
    unrow = lambda r: {n: r[i, :small[n].size].reshape(small[n].shape) for i, n in enumerate(small_names)}
    g_out, d_out, m_out, v_out = unrow(g_small_rows), unrow(d_s), unrow(m_s), unrow(v_s)

    d_out["gdn_conv"], m_out["gdn_conv"], v_out["gdn_conv"] = adamw(gdn_conv, g_conv, m_gdn_conv, v_gdn_conv, "adamw_gdn_conv")
    g_out["gdn_conv"] = g_conv[None]
    res = adamw_untiled_rows(w_in3, my_half[0], other_half[0], m_in3, v_in3, "adamw_w_in")
    g_out["w_in"], d_out["w_in"], m_out["w_in"], v_out["w_in"] = [jnp.transpose(r, (1, 2, 0)) for r in res]
    res = adamw_w_q_b(w_qb_t, my_half[1], other_half[1], m_qb_t, v_qb_t, "adamw_w_q_b")
    g_out["w_q_b"], d_out["w_q_b"], m_out["w_q_b"], v_out["w_q_b"] = [jnp.transpose(r)[None] for r in res]
    rest = dict(w_kv_b=(w_kv_b, m_w_kv_b, v_w_kv_b), w_mem_kv=(w_mem_kv, m_w_mem_kv, v_w_mem_kv), w_out=(w_out, m_w_out, v_w_out))
    for i, n in enumerate(big_names):
        if n in rest:
            w_n, m_n, v_n = rest[n]
            g_out[n], d_out[n], m_out[n], v_out[n] = adamw_halves(w_n, my_half[i], other_half[i], m_n, v_n, splits[i], core, "adamw_" + n)

    order = ("norm_in", "w_in", "q_a_norm", "w_q_b", "kv_a_norm", "w_kv_b", "gdn_conv", "gdn_a_log", "gdn_dt_bias",
             "gdn_norm", "mem_norm", "w_mem_kv", "w_out", "norm_final")
    return (loss, grad_x, *[g_out[n] for n in order], *[d_out[n] for n in order], *[m_out[n] for n in order],
            *[v_out[n] for n in order])
```

```python
import functools
import math

import jax
import jax.numpy as jnp
import numpy as np
from jax import lax
from jax.experimental import pallas as pl
from jax.experimental.pallas import tpu as pltpu

F32 = jnp.float32
BF16 = jnp.bfloat16
BS = pl.BlockSpec

D_MODEL = 1024
N_HEADS = 4
MLA_NOPE, MLA_ROPE, MLA_V = 128, 64, 128
Q_LORA, KV_LORA = 384, 256
ROPE_THETA = 10000.0
GDN_DK = GDN_DV = 128
GDN_CONV = 4
CHUNK = 64
MEM_DH = 128
D_MIX = 1536
GDN_QKV = 1536
D_IN = 4296
EPS = 1e-6
ADAM_LR, ADAM_B1, ADAM_B2, ADAM_EPS, ADAM_WD, ADAM_STEP = 0.001, 0.9, 0.999, 1e-08, 0.01, 10

OFF_MLA = 0
OFF_MEMQ = 1024
OFF_GDN = 1536
OFF_GATE = 3072
N_PAD = 4608
HEAD_PAD = 256
MLA_SCALE = (MLA_NOPE + MLA_ROPE) ** -0.5
MEM_SCALE = MEM_DH ** -0.5
GDN_SCALE = GDN_DK ** -0.5
NEG = -1e30

NN = ((1,), (0,))
NT = ((1,), (1,))
TN = ((0,), (0,))


def _dot(a, b, dims):
    return lax.dot_general(a, b, (dims, ((), ())), preferred_element_type=F32)


def _bdot(spec, a, b, precision=None):
    return jnp.einsum(spec, a, b, preferred_element_type=F32, precision=precision)


def _arb(n):
    return pltpu.CompilerParams(dimension_semantics=("arbitrary",) * n)


def _sigmoid(x):
    return 1.0 / (1.0 + jnp.exp(-x))


def _softplus(z):
    return jnp.maximum(z, 0.0) + jnp.log(1.0 + jnp.exp(-jnp.abs(z)))


def _rope(t, cos_row, sin_row):
    return t * cos_row + pltpu.roll(t, 64, 1) * sin_row


def _rope_bwd(d, cos_row, sin_row):
    return d * cos_row + pltpu.roll(d * sin_row, 64, 1)


def rms_fwd(x, gain, name, tm=512):
    T, n = x.shape
    tm = min(tm, T)

    def body(x_ref, g_ref, o_ref):
        xv = x_ref[...]
        r = lax.rsqrt(jnp.mean(xv * xv, axis=-1, keepdims=True) + EPS)
        o_ref[...] = (xv * r * g_ref[...]).astype(BF16)

    return pl.pallas_call(
        body, name=name, grid=(T // tm,),
        in_specs=[BS((tm, n), lambda i: (i, 0)), BS((1, n), lambda i: (0, 0))],
        out_specs=BS((tm, n), lambda i: (i, 0)),
        out_shape=jax.ShapeDtypeStruct((T, n), BF16), compiler_params=_arb(1))(x, gain)


def mm(a, b, kind, out_dtype, name, bm=512, bn=None, n_outer=False):
    if kind == "nn":
        (M, K), (_, N) = a.shape, b.shape
    elif kind == "nt":
        (M, K), (N, _) = a.shape, b.shape
    else:
        (K, M), (_, N) = a.shape, b.shape
    bm, bn = min(bm, M), min(bn or N, N)
    assert M % bm == 0 and N % bn == 0, (name, M, N, K)
    ij = (lambda g0, g1: (g1, g0)) if n_outer else (lambda g0, g1: (g0, g1))
    a_spec = BS((K, bm), lambda g0, g1: (0, ij(g0, g1)[0])) if kind == "tn" else BS((bm, K), lambda g0, g1: (ij(g0, g1)[0], 0))
    once = dict(pipeline_mode=pl.Buffered(1)) if bn == N else {}
    b_spec = (BS((bn, K), lambda g0, g1: (ij(g0, g1)[1], 0), **once) if kind == "nt"
              else BS((K, bn), lambda g0, g1: (0, ij(g0, g1)[1]), **once))
    dims = {"nn": NN, "nt": NT, "tn": TN}[kind]

    def body(a_ref, b_ref, o_ref):
        o_ref[...] = _dot(a_ref[...].astype(BF16), b_ref[...].astype(BF16), dims).astype(out_dtype)

    grid = (N // bn, M // bm) if n_outer else (M // bm, N // bn)
    return pl.pallas_call(
        body, name=name, grid=grid, in_specs=[a_spec, b_spec], out_specs=BS((bm, bn), lambda g0, g1: ij(g0, g1)),
        out_shape=jax.ShapeDtypeStruct((M, N), out_dtype), compiler_params=_arb(2))(a, b)


def mm_cols_nn(pieces, b, out_dtype, name, bm=512):
    M, N = pieces[0].shape[0], b.shape[1]
    bm = min(bm, M)
    widths = [p.shape[1] for p in pieces]
    offs = [sum(widths[:i]) for i in range(len(widths))]

    def body(*refs):
        b_ref, o_ref = refs[-2], refs[-1]
        acc = None
        for a_ref, off, w in zip(refs[:-2], offs, widths):
            d = _dot(a_ref[...], b_ref[off:off + w, :], NN)
            acc = d if acc is None else acc + d
        o_ref[...] = acc.astype(out_dtype)

    return pl.pallas_call(
        body, name=name, grid=(M // bm,),
        in_specs=[BS((bm, w), lambda i: (i, 0)) for w in widths] + [BS(b.shape, lambda i: (0, 0), pipeline_mode=pl.Buffered(1))],
        out_specs=BS((bm, N), lambda i: (i, 0)), out_shape=jax.ShapeDtypeStruct((M, N), out_dtype),
        compiler_params=_arb(1))(*pieces, b)


def mm_cols_tn(pieces, b, out_dtype, name, bm=512):
    K, N = b.shape
    tiles = [p.shape[1] // bm for p in pieces]
    firsts = [sum(tiles[:i]) for i in range(len(tiles))]

    def body(*refs):
        b_ref, o_ref = refs[-2], refs[-1]
        i = pl.program_id(0)
        for a_ref, t0, n in zip(refs[:-2], firsts, tiles):
            @pl.when((i >= t0) & (i < t0 + n))
            def _(a_ref=a_ref):
                o_ref[...] = _dot(a_ref[...], b_ref[...], TN).astype(out_dtype)

    a_specs = [BS((K, bm), lambda i, t0=t0, n=n: (0, jnp.clip(i - t0, 0, n - 1))) for t0, n in zip(firsts, tiles)]
    return pl.pallas_call(
        body, name=name, grid=(sum(tiles),),
        in_specs=a_specs + [BS(b.shape, lambda i: (0, 0), pipeline_mode=pl.Buffered(1))],
        out_specs=BS((bm, N), lambda i: (i, 0)), out_shape=jax.ShapeDtypeStruct((sum(tiles) * bm, N), out_dtype),
        compiler_params=_arb(1))(*pieces, b)


def rope_tables(pos_col, inv_row, sgn_row, msk_row, tm=512):
    T = pos_col.shape[0]
    tm = min(tm, T)

    def body(p_ref, inv_ref, sgn_ref, msk_ref, c_ref, s_ref):
        ang = p_ref[...].astype(F32) * inv_ref[...]
        c_ref[...] = jnp.cos(ang) * msk_ref[...]
        s_ref[...] = jnp.sin(ang) * sgn_ref[...]

    row = BS((1, 128), lambda i: (0, 0))
    return pl.pallas_call(
        body, name="rope_tables", grid=(T // tm,),
        in_specs=[BS((tm, 1), lambda i: (i, 0)), row, row, row],
        out_specs=[BS((tm, 128), lambda i: (i, 0))] * 2,
        out_shape=[jax.ShapeDtypeStruct((T, 128), F32)] * 2, compiler_params=_arb(1))(pos_col, inv_row, sgn_row, msk_row)


def mla_prep(P, gq, gkv, wq, wkv, cos_t, sin_t, tm=512):
    T = P.shape[0]
    tm = min(tm, T)

    def body(p_ref, gq_ref, gkv_ref, wq_ref, wkv_ref, c_ref, s_ref, q_ref, k_ref, v_ref, qn_ref, kvn_ref):
        p = p_ref[...]
        cq, ckv, kr = p[:, :Q_LORA], p[:, Q_LORA:Q_LORA + KV_LORA], p[:, 640:768]
        qn = (cq * lax.rsqrt(jnp.mean(cq * cq, axis=-1, keepdims=True) + EPS) * gq_ref[...]).astype(BF16)
        kvn = (ckv * lax.rsqrt(jnp.mean(ckv * ckv, axis=-1, keepdims=True) + EPS) * gkv_ref[...]).astype(BF16)
        qn_ref[...] = qn
        kvn_ref[...] = kvn
        q = _dot(qn, wq_ref[...], NT)
        kv = _dot(kvn, wkv_ref[...], NN)
        cos_row, sin_row = c_ref[...], s_ref[...]
        krr = _rope(kr, cos_row, sin_row).astype(BF16)
        for h in range(N_HEADS):
            lo = h * HEAD_PAD
            q_ref[:, lo:lo + 128] = (q[:, lo:lo + 128] * MLA_SCALE).astype(BF16)
            q_ref[:, lo + 128:lo + 256] = (_rope(q[:, lo + 128:lo + 256], cos_row, sin_row) * MLA_SCALE).astype(BF16)
            k_ref[:, lo:lo + 128] = kv[:, h * 128:(h + 1) * 128].astype(BF16)
            k_ref[:, lo + 128:lo + 256] = krr
            v_ref[:, lo:lo + 128] = kv[:, 512 + h * 128:512 + (h + 1) * 128].astype(BF16)
            v_ref[:, lo + 128:lo + 256] = jnp.ones((tm, 128), BF16)

    full = lambda r, c: BS((r, c), lambda i: (0, 0))
    rowb = lambda c: BS((tm, c), lambda i: (i, 0))
    return pl.pallas_call(
        body, name="mla_prep", grid=(T // tm,),
        in_specs=[rowb(1024), full(1, Q_LORA), full(1, KV_LORA), full(1024, Q_LORA), full(KV_LORA, 1024), rowb(128), rowb(128)],
        out_specs=[rowb(1024), rowb(1024), rowb(1024), rowb(Q_LORA), rowb(KV_LORA)],
        out_shape=[jax.ShapeDtypeStruct((T, 1024), BF16), jax.ShapeDtypeStruct((T, 1024), BF16),
                   jax.ShapeDtypeStruct((T, 1024), BF16), jax.ShapeDtypeStruct((T, Q_LORA), BF16),
                   jax.ShapeDtypeStruct((T, KV_LORA), BF16)],
        compiler_params=_arb(1))(P, gq, gkv, wq, wkv, cos_t, sin_t)


ATTN_HEADS_PER_STEP = 2
ATTN_STRIP = 32


def mla_attn_fwd(Q, K, V, B, S, tq=512, hp=ATTN_HEADS_PER_STEP):
    T = B * S
    tq = min(tq, S)
    nq = S // tq

    rs = min(ATTN_STRIP, tq)

    def body(q_ref, k_ref, v_ref, o_ref, lse_ref, m_s, acc_s, s_s, p_s, a_s):
        i = pl.program_id(2)
        m_s[...] = jnp.full_like(m_s, NEG)
        acc_s[...] = jnp.zeros_like(acc_s)

        def blk(j, masked):
            rows = pl.ds(pl.multiple_of(j * tq, tq), tq)
            for h in range(hp):
                hq = slice(h * HEAD_PAD, (h + 1) * HEAD_PAD)
                s_s[h] = _dot(q_ref[:, hq], k_ref[rows, hq], NT)
            for h in range(hp):
                for r0 in range(0, tq, rs):
                    rr = slice(r0, r0 + rs)
                    sv = s_s[h, rr, :]
                    if masked:
                        r = r0 + lax.broadcasted_iota(jnp.int32, (rs, tq), 0)
                        c = lax.broadcasted_iota(jnp.int32, (rs, tq), 1)
                        sv = jnp.where(r >= c, sv, NEG)
                    m_prev = m_s[h, rr, :]
                    m_new = jnp.maximum(m_prev, jnp.max(sv, axis=1, keepdims=True))
                    p_s[h, rr, :] = jnp.exp(sv - m_new).astype(BF16)
                    a_s[h, rr, :] = jnp.exp(m_prev - m_new)
                    m_s[h, rr, :] = m_new
            for h in range(hp):
                hq = slice(h * HEAD_PAD, (h + 1) * HEAD_PAD)
                acc_s[h] = a_s[h] * acc_s[h] + _dot(p_s[h], v_ref[rows, hq], NN)

        def loop(j, c):
            blk(j, False)
            return c

        lax.fori_loop(0, i, loop, 0)
        blk(i, True)
        for h in range(hp):
            den = acc_s[h, :, 128:256]
            o_ref[:, h * 128:(h + 1) * 128] = acc_s[h, :, 0:128] / den
            lse_ref[h] = m_s[h] + jnp.log(den[:, 0:1])

    return pl.pallas_call(
        body, name="mla_attn_fwd", grid=(B, N_HEADS // hp, nq),
        in_specs=[BS((tq, hp * HEAD_PAD), lambda b, h, i: (b * nq + i, h)),
                  BS((S, hp * HEAD_PAD), lambda b, h, i: (b, h)),
                  BS((S, hp * HEAD_PAD), lambda b, h, i: (b, h))],
        out_specs=[BS((tq, hp * 128), lambda b, h, i: (b * nq + i, h)),
                   BS((hp, tq, 1), lambda b, h, i: (h, b * nq + i, 0))],
        out_shape=[jax.ShapeDtypeStruct((T, 512), F32), jax.ShapeDtypeStruct((N_HEADS, T, 1), F32)],
        scratch_shapes=[pltpu.VMEM((hp, tq, 1), F32), pltpu.VMEM((hp, tq, HEAD_PAD), F32), pltpu.VMEM((hp, tq, tq), F32),
                        pltpu.VMEM((hp, tq, tq), BF16), pltpu.VMEM((hp, tq, 1), F32)],
        compiler_params=_arb(3))(Q, K, V)


def mla_attn_bwd(Q, K, V, O, dO, LSE, B, S, tq=512, hp=ATTN_HEADS_PER_STEP):
    T = B * S
    tq = min(tq, S)
    nq = S // tq

    def body(q_ref, k_ref, v_ref, o_ref, do_ref, lse_ref, dq_ref, dk_ref, dv_ref, delta_s, dk_s, dv_s):
        j = pl.program_id(2)

        @pl.when(j == 0)
        def _():
            dq_ref[...] = jnp.zeros_like(dq_ref)
            for h in range(hp):
                sl = slice(h * 128, (h + 1) * 128)
                delta_s[h] = jnp.sum(do_ref[:, sl] * o_ref[:, sl], axis=1, keepdims=True)

        dk_s[...] = jnp.zeros_like(dk_s)
        dv_s[...] = jnp.zeros_like(dv_s)

        def step(i, c):
            rows = pl.ds(pl.multiple_of(i * tq, tq), tq)
            r = i * tq + lax.broadcasted_iota(jnp.int32, (tq, tq), 0)
            cc = j * tq + lax.broadcasted_iota(jnp.int32, (tq, tq), 1)
            causal = r >= cc
            for h in range(hp):
                sq, sv = slice(h * HEAD_PAD, (h + 1) * HEAD_PAD), slice(h * 128, (h + 1) * 128)
                q = q_ref[rows, sq]
                k = k_ref[:, sq]
                do = do_ref[rows, sv].astype(BF16)
                s = _dot(q, k, NT)
                p = jnp.where(causal, jnp.exp(s - lse_ref[h, rows, :]), 0.0)
                dv_s[:, sv] += _dot(p.astype(BF16), do, TN)
                dp = _dot(do, v_ref[:, h * HEAD_PAD:h * HEAD_PAD + 128], NT)
                ds = (p * (dp - delta_s[h, rows, :])).astype(BF16)
                dk_s[:, sq] += _dot(ds, q, TN)
                dq_ref[rows, sq] += _dot(ds, k, NN)
            return c

        lax.fori_loop(j, nq, step, 0)
        dk_ref[...] = dk_s[...]
        dv_ref[...] = dv_s[...]

    seq = lambda c: BS((S, c), lambda b, h, j: (b, h))
    blk = lambda c: BS((tq, c), lambda b, h, j: (b * nq + j, h))
    return pl.pallas_call(
        body, name="mla_attn_bwd", grid=(B, N_HEADS // hp, nq),
        in_specs=[seq(hp * HEAD_PAD), blk(hp * HEAD_PAD), blk(hp * HEAD_PAD), seq(hp * 128), seq(hp * 128),
                  BS((hp, S, 1), lambda b, h, j: (h, b, 0))],
        out_specs=[seq(hp * HEAD_PAD), blk(hp * HEAD_PAD), blk(hp * 128)],
        out_shape=[jax.ShapeDtypeStruct((T, 1024), F32), jax.ShapeDtypeStruct((T, 1024), F32),
                   jax.ShapeDtypeStruct((T, 512), F32)],
        scratch_shapes=[pltpu.VMEM((hp, S, 1), F32), pltpu.VMEM((tq, hp * HEAD_PAD), F32), pltpu.VMEM((tq, hp * 128), F32)],
        compiler_params=_arb(3))(Q, K, V, O, dO, LSE)


def mla_post_bwd(dQ, dK, dV, cos_t, sin_t, tm=512):
    T = dQ.shape[0]
    tm = min(tm, T)

    def body(dq_ref, dk_ref, dv_ref, c_ref, s_ref, ql_ref, kvl_ref, kr_ref):
        cos_row, sin_row = c_ref[...], s_ref[...]
        kr = jnp.zeros((tm, 128), F32)
        for h in range(N_HEADS):
            lo = h * HEAD_PAD
            ql_ref[:, lo:lo + 128] = (dq_ref[:, lo:lo + 128] * MLA_SCALE).astype(BF16)
            ql_ref[:, lo + 128:lo + 256] = (_rope_bwd(dq_ref[:, lo + 128:lo + 256], cos_row, sin_row) * MLA_SCALE).astype(BF16)
            kvl_ref[:, h * 128:(h + 1) * 128] = dk_ref[:, lo:lo + 128].astype(BF16)
            kr = kr + dk_ref[:, lo + 128:lo + 256]
        kvl_ref[:, 512:] = dv_ref[...].astype(BF16)
        kr_ref[...] = _rope_bwd(kr, cos_row, sin_row)

    rowb = lambda c: BS((tm, c), lambda i: (i, 0))
    return pl.pallas_call(
        body, name="mla_post_bwd", grid=(T // tm,),
        in_specs=[rowb(1024), rowb(1024), rowb(512), rowb(128), rowb(128)],
        out_specs=[rowb(1024), rowb(1024), rowb(128)],
        out_shape=[jax.ShapeDtypeStruct((T, 1024), BF16), jax.ShapeDtypeStruct((T, 1024), BF16),
                   jax.ShapeDtypeStruct((T, 128), F32)],
        compiler_params=_arb(1))(dQ, dK, dV, cos_t, sin_t)


def mla_norm_bwd(P, dqn, dkvn, dkr, dab, gq, gkv, tm=512):
    T = P.shape[0]
    tm = min(tm, T)

    def norm_bwd(x, dy, g):
        r = lax.rsqrt(jnp.mean(x * x, axis=-1, keepdims=True) + EPS)
        xh = x * r
        dxh = dy * g
        return r * (dxh - xh * jnp.mean(dxh * xh, axis=-1, keepdims=True)), jnp.sum(dy * xh, axis=0, keepdims=True)

    def body(p_ref, dqn_ref, dkvn_ref, dkr_ref, dab_ref, gq_ref, gkv_ref, o_ref, aq_ref, akv_ref):
        @pl.when(pl.program_id(0) == 0)
        def _():
            aq_ref[...] = jnp.zeros_like(aq_ref)
            akv_ref[...] = jnp.zeros_like(akv_ref)

        dcq, ggq = norm_bwd(p_ref[:, :Q_LORA], dqn_ref[...], gq_ref[...])
        dckv, ggkv = norm_bwd(p_ref[:, Q_LORA:640], dkvn_ref[...], gkv_ref[...])
        aq_ref[...] += ggq
        akv_ref[...] += ggkv
        o_ref[:, :Q_LORA] = dcq.astype(BF16)
        o_ref[:, Q_LORA:640] = dckv.astype(BF16)
        o_ref[:, 640:768] = dkr_ref[...].astype(BF16)
        o_ref[:, 768:896] = dab_ref[...]
        o_ref[:, 896:1024] = jnp.zeros((tm, 128), BF16)

    rowb = lambda c: BS((tm, c), lambda i: (i, 0))
    full = lambda c: BS((1, c), lambda i: (0, 0))
    return pl.pallas_call(
        body, name="mla_norm_bwd", grid=(T // tm,),
        in_specs=[rowb(1024), rowb(Q_LORA), rowb(KV_LORA), rowb(128), rowb(128), full(Q_LORA), full(KV_LORA)],
        out_specs=[rowb(1024), full(Q_LORA), full(KV_LORA)],
        out_shape=[jax.ShapeDtypeStruct((T, 1024), BF16), jax.ShapeDtypeStruct((1, Q_LORA), F32),
                   jax.ShapeDtypeStruct((1, KV_LORA), F32)],
        compiler_params=_arb(1))(P, dqn, dkvn, dkr, dab, gq, gkv)


def _mem_probs(qh, kh):
    s = _dot(qh, kh, NT) * MEM_SCALE
    p = jnp.exp(s - jnp.max(s, axis=1, keepdims=True))
    return p / jnp.sum(p, axis=1, keepdims=True)


def mem_attn_fwd(P, MKV, B, S, M, tq=512):
    T = B * S
    tq = min(tq, S)
    nq = S // tq

    def body(q_ref, kv_ref, o_ref):
        for h in range(N_HEADS):
            sl = slice(h * 128, (h + 1) * 128)
            p = _mem_probs(q_ref[:, sl].astype(BF16), kv_ref[:, sl])
            o_ref[:, sl] = _dot(p.astype(BF16), kv_ref[:, 512 + h * 128:512 + (h + 1) * 128], NN)

    return pl.pallas_call(
        body, name="mem_attn_fwd", grid=(B, nq),
        in_specs=[BS((tq, 512), lambda b, i: (b * nq + i, OFF_MEMQ // 512)), BS((M, 1024), lambda b, i: (b, 0))],
        out_specs=BS((tq, 512), lambda b, i: (b * nq + i, 0)),
        out_shape=jax.ShapeDtypeStruct((T, 512), F32), compiler_params=_arb(2))(P, MKV)


def mem_attn_bwd(P, MKV, dO, B, S, M, tq=512):
    T = B * S
    tq = min(tq, S)
    nq = S // tq

    def body(q_ref, kv_ref, do_ref, dq_ref, dkv_ref):
        @pl.when(pl.program_id(1) == 0)
        def _():
            dkv_ref[...] = jnp.zeros_like(dkv_ref)

        for h in range(N_HEADS):
            sl = slice(h * 128, (h + 1) * 128)
            sv = slice(512 + h * 128, 512 + (h + 1) * 128)
            qh = q_ref[:, sl].astype(BF16)
            kh = kv_ref[:, sl]
            do = do_ref[:, sl].astype(BF16)
            p = _mem_probs(qh, kh)
            dkv_ref[:, sv] += _dot(p.astype(BF16), do, TN)
            dp = _dot(do, kv_ref[:, sv], NT)
            ds = (p * (dp - jnp.sum(dp * p, axis=1, keepdims=True)) * MEM_SCALE).astype(BF16)
            dq_ref[:, sl] = _dot(ds, kh, NN).astype(BF16)
            dkv_ref[:, sl] += _dot(ds, qh, TN)

    return pl.pallas_call(
        body, name="mem_attn_bwd", grid=(B, nq),
        in_specs=[BS((tq, 512), lambda b, i: (b * nq + i, OFF_MEMQ // 512)), BS((M, 1024), lambda b, i: (b, 0)),
                  BS((tq, 512), lambda b, i: (b * nq + i, 0))],
        out_specs=[BS((tq, 512), lambda b, i: (b * nq + i, 0)), BS((M, 1024), lambda b, i: (b, 0))],
        out_shape=[jax.ShapeDtypeStruct((T, 512), BF16), jax.ShapeDtypeStruct((B * M, 1024), F32)],
        compiler_params=_arb(2))(P, MKV, dO)


def gain_grad(x, dy, name, tm=256):
    T, n = x.shape
    tm = min(tm, T)

    def body(x_ref, dy_ref, o_ref):
        @pl.when(pl.program_id(0) == 0)
        def _():
            o_ref[...] = jnp.zeros_like(o_ref)

        xv = x_ref[...]
        xh = xv * lax.rsqrt(jnp.mean(xv * xv, axis=-1, keepdims=True) + EPS)
        o_ref[...] += jnp.sum(dy_ref[...] * xh, axis=0, keepdims=True)

    return pl.pallas_call(
        body, name=name, grid=(T // tm,),
        in_specs=[BS((tm, n), lambda i: (i, 0))] * 2, out_specs=BS((1, n), lambda i: (0, 0)),
        out_shape=jax.ShapeDtypeStruct((1, n), F32), compiler_params=_arb(1))(x, dy)


def _conv_silu(x, w, t):
    y = x * w[3:4, :]
    for s in range(1, GDN_CONV):
        y = y + jnp.where(t >= s, pltpu.roll(x, s, 0), 0.0) * w[3 - s:4 - s, :]
    return y, _sigmoid(y)


def gdn_prep_fwd(P, conv_w, B, S):
    T = B * S

    def body(x_ref, w_ref, o_ref):
        kind = pl.program_id(1)
        t = lax.broadcasted_iota(jnp.int32, (S, 1), 0)
        y, sg = _conv_silu(x_ref[...], w_ref[...], t)
        a = y * sg
        scale = jnp.where(kind == 0, GDN_SCALE, 1.0).astype(F32)
        for h in range(N_HEADS):
            sl = slice(h * 128, (h + 1) * 128)
            seg = a[:, sl]
            n = lax.rsqrt(jnp.sum(seg * seg, axis=-1, keepdims=True) + EPS)
            o_ref[:, sl] = jnp.where(kind < 2, seg * (n * scale), seg)

    return pl.pallas_call(
        body, name="gdn_prep_fwd", grid=(B, 3),
        in_specs=[BS((S, 512), lambda b, k: (b, OFF_GDN // 512 + k)), BS((GDN_CONV, 512), lambda b, k: (0, k))],
        out_specs=BS((S, 512), lambda b, k: (b, k)),
        out_shape=jax.ShapeDtypeStruct((T, GDN_QKV), F32), compiler_params=_arb(2))(P, conv_w)


def gdn_prep_bwd(P, dqkv, conv_w, B, S):
    T = B * S

    def body(x_ref, d_ref, w_ref, o_ref, gw_ref):
        kind = pl.program_id(0)

        @pl.when(pl.program_id(1) == 0)
        def _():
            gw_ref[...] = jnp.zeros_like(gw_ref)

        t = lax.broadcasted_iota(jnp.int32, (S, 1), 0)
        x = x_ref[...]
        w = w_ref[...]
        y, sg = _conv_silu(x, w, t)
        a = y * sg
        scale = jnp.where(kind == 0, GDN_SCALE, 1.0).astype(F32)
        das = []
        for h in range(N_HEADS):
            sl = slice(h * 128, (h + 1) * 128)
            seg, dseg = a[:, sl], d_ref[:, sl]
            n = lax.rsqrt(jnp.sum(seg * seg, axis=-1, keepdims=True) + EPS)
            dn = scale * (n * dseg - seg * (n * n * n) * jnp.sum(dseg * seg, axis=-1, keepdims=True))
            das.append(jnp.where(kind < 2, dn, dseg))
        dy = jnp.concatenate(das, axis=1) * (sg * (1.0 + y * (1.0 - sg)))
        dx = dy * w[3:4, :]
        gw_ref[3:4, :] += jnp.sum(dy * x, axis=0, keepdims=True)
        for s in range(1, GDN_CONV):
            dx = dx + jnp.where(t + s < S, pltpu.roll(dy, S - s, 0), 0.0) * w[3 - s:4 - s, :]
            gw_ref[3 - s:4 - s, :] += jnp.sum(dy * jnp.where(t >= s, pltpu.roll(x, s, 0), 0.0), axis=0, keepdims=True)
        o_ref[...] = dx.astype(BF16)

    return pl.pallas_call(
        body, name="gdn_prep_bwd", grid=(3, B),
        in_specs=[BS((S, 512), lambda k, b: (b, OFF_GDN // 512 + k)), BS((S, 512), lambda k, b: (b, k)),
                  BS((GDN_CONV, 512), lambda k, b: (0, k))],
        out_specs=[BS((S, 512), lambda k, b: (b, k)), BS((GDN_CONV, 512), lambda k, b: (0, k))],
        out_shape=[jax.ShapeDtypeStruct((T, GDN_QKV), BF16), jax.ShapeDtypeStruct((GDN_CONV, GDN_QKV), F32)],
        compiler_params=_arb(2))(P, dqkv, conv_w)


def _chunk_row(n_rows):
    return lax.broadcasted_iota(jnp.int32, (n_rows, 1), 0) % CHUNK


def gdn_gate_fwd(P, alog_row, dt_row, B, S):
    T = B * S

    def body(x_ref, al_ref, dt_ref, o_ref):
        x = x_ref[...]
        lane = lax.broadcasted_iota(jnp.int32, (1, 128), 1)
        g = jnp.where(lane < 4, -jnp.exp(al_ref[...]) * _softplus(x + dt_ref[...]), 0.0)
        t = _chunk_row(S)
        for s in (1, 2, 4, 8, 16, 32):
            g = g + jnp.where(t >= s, pltpu.roll(g, s, 0), 0.0)
        o_ref[...] = jnp.where(lane < 4, g, jnp.where(lane < 8, _sigmoid(x), 0.0))

    row = BS((1, 128), lambda b: (0, 0))
    return pl.pallas_call(
        body, name="gdn_gate_fwd", grid=(B,),
        in_specs=[BS((S, 128), lambda b: (b, 768 // 128)), row, row], out_specs=BS((S, 128), lambda b: (b, 0)),
        out_shape=jax.ShapeDtypeStruct((T, 128), F32), compiler_params=_arb(1))(P, alog_row, dt_row)


def gdn_gate_bwd(P, dGB, alog_row, dt_row, B, S):
    T = B * S

    def body(x_ref, d_ref, al_ref, dt_ref, o_ref, acc_ref):
        @pl.when(pl.program_id(0) == 0)
        def _():
            acc_ref[...] = jnp.zeros_like(acc_ref)

        x, d = x_ref[...], d_ref[...]
        lane = lax.broadcasted_iota(jnp.int32, (1, 128), 1)
        z = x + dt_ref[...]
        coef = -jnp.exp(al_ref[...])
        g = coef * _softplus(z)
        da = jnp.where(lane < 4, d * coef * _sigmoid(z), 0.0)
        beta = _sigmoid(x)
        o_ref[...] = jnp.where(lane < 4, da, jnp.where(lane < 8, d * beta * (1.0 - beta), 0.0)).astype(BF16)
        acc_ref[0:1, :] += jnp.sum(jnp.where(lane < 4, d * g, 0.0), axis=0, keepdims=True)
        acc_ref[1:2, :] += jnp.sum(da, axis=0, keepdims=True)

    row = BS((1, 128), lambda b: (0, 0))
    return pl.pallas_call(
        body, name="gdn_gate_bwd", grid=(B,),
        in_specs=[BS((S, 128), lambda b: (b, 768 // 128)), BS((S, 128), lambda b: (b, 0)), row, row],
        out_specs=[BS((S, 128), lambda b: (b, 0)), BS((8, 128), lambda b: (0, 0))],
        out_shape=[jax.ShapeDtypeStruct((T, 128), BF16), jax.ShapeDtypeStruct((8, 128), F32)],
        compiler_params=_arb(1))(P, dGB, alog_row, dt_row)


def _chunk_masks(nc):
    r = lax.broadcasted_iota(jnp.int32, (nc, CHUNK, CHUNK), 1)
    c = lax.broadcasted_iota(jnp.int32, (nc, CHUNK, CHUNK), 2)
    return r >= c, r > c


def _chunk_local(q, k, gc, gr, beta, incl, strict):
    decay = jnp.exp(jnp.where(incl, gc - gr, NEG))
    kb = k * beta
    kbf = k.astype(BF16)
    m_kk = _bdot("gcd,gjd->gcj", kb.astype(BF16), kbf)
    l_mat = jnp.where(strict, m_kk * decay, 0.0)
    a_mat = _bdot("gcd,gjd->gcj", q.astype(BF16), kbf) * decay
    return decay, kb, l_mat, a_mat


def gdn_chunk_fwd(qkv, GB, Grow, B, S, nc=8):
    T = B * S
    N = S // CHUNK
    nc = min(nc, N)
    nb = N // nc
    R = nc * CHUNK
    hi = lax.Precision.HIGHEST

    def body(q_ref, k_ref, v_ref, gb_ref, gr_ref, u_ref, w_ref, t_ref, a_ref):
        incl, strict = _chunk_masks(nc)
        eye = (lax.broadcasted_iota(jnp.int32, (nc, CHUNK, CHUNK), 1)
               == lax.broadcasted_iota(jnp.int32, (nc, CHUNK, CHUNK), 2)).astype(F32)
        for h in range(N_HEADS):
            sl = slice(h * 128, (h + 1) * 128)
            q = q_ref[:, sl].reshape(nc, CHUNK, 128)
            k = k_ref[:, sl].reshape(nc, CHUNK, 128)
            v = v_ref[:, sl].reshape(nc, CHUNK, 128)
            gc = gb_ref[:, h:h + 1].reshape(nc, CHUNK, 1)
            beta = gb_ref[:, 4 + h:5 + h].reshape(nc, CHUNK, 1)
            gr = gr_ref[h][:, None, :]
            _, kb, l_mat, a_mat = _chunk_local(q, k, gc, gr, beta, incl, strict)
            pw = -l_mat
            tinv = eye + pw
            for _ in range(5):
                pw = _bdot("gij,gjk->gik", pw, pw, hi)
                tinv = tinv + _bdot("gij,gjk->gik", tinv, pw, hi)
            tb = tinv.astype(BF16)
            u = _bdot("gcj,gjv->gcv", tb, (v * beta).astype(BF16))
            w = _bdot("gcj,gjk->gck", tb, (kb * jnp.exp(gc)).astype(BF16))
            u_ref[:, sl] = u.reshape(R, 128)
            w_ref[:, sl] = w.reshape(R, 128)
            t_ref[h] = tinv
            a_ref[h] = a_mat

    rowb = lambda c, j: BS((R, c), lambda b, n: (b * nb + n, j))
    mat = BS((None, N_HEADS, nc, CHUNK, CHUNK), lambda b, n: (b, 0, n, 0, 0))
    return pl.pallas_call(
        body, name="gdn_chunk_fwd", grid=(B, nb),
        in_specs=[rowb(512, 0), rowb(512, 1), rowb(512, 2), rowb(128, 0),
                  BS((None, N_HEADS, nc, CHUNK), lambda b, n: (b, 0, n, 0))],
        out_specs=[rowb(512, 0), rowb(512, 0), mat, mat],
        out_shape=[jax.ShapeDtypeStruct((T, 512), F32), jax.ShapeDtypeStruct((T, 512), F32),
                   jax.ShapeDtypeStruct((B, N_HEADS, N, CHUNK, CHUNK), F32),
                   jax.ShapeDtypeStruct((B, N_HEADS, N, CHUNK, CHUNK), F32)],
        compiler_params=_arb(2))(qkv, qkv, qkv, GB, Grow)


def gdn_scan_fwd(qkv3, U3, W3, GB3, A, B, S):
    N = S // CHUNK

    def body(q_ref, k_ref, u_ref, w_ref, gb_ref, a_ref, o_ref, vn_ref, st_ref, s_s):
        @pl.when(pl.program_id(0) == 0)
        def _():
            s_s[...] = jnp.zeros_like(s_s)

        for b in range(B):
            for h in range(N_HEADS):
                sl = slice(h * 128, (h + 1) * 128)
                st = s_s[b, h]
                st_ref[b, h] = st
                stb = st.astype(BF16)
                g = gb_ref[b, :, h:h + 1]
                gl = g[CHUNK - 1:CHUNK, :]
                vn = u_ref[b, :, sl] - _dot(w_ref[b, :, sl].astype(BF16), stb, NN)
                vnb = vn.astype(BF16)
                o = _dot((q_ref[b, :, sl] * jnp.exp(g)).astype(BF16), stb, NN) + _dot(a_ref[b, h].astype(BF16), vnb, NN)
                vn_ref[b, :, sl] = vn
                o_ref[b, :, sl] = o
                s_s[b, h] = st * jnp.exp(gl) + _dot((k_ref[b, :, sl] * jnp.exp(gl - g)).astype(BF16), vnb, TN)

    tok = lambda c, j: BS((B, CHUNK, c), lambda n: (0, n, j))
    return pl.pallas_call(
        body, name="gdn_scan_fwd", grid=(N,),
        in_specs=[tok(512, 0), tok(512, 1), tok(512, 0), tok(512, 0), tok(128, 0),
                  BS((B, N_HEADS, None, CHUNK, CHUNK), lambda n: (0, 0, n, 0, 0))],
        out_specs=[tok(512, 0), tok(512, 0), BS((B, N_HEADS, None, 128, 128), lambda n: (0, 0, n, 0, 0))],
        out_shape=[jax.ShapeDtypeStruct((B, S, 512), F32), jax.ShapeDtypeStruct((B, S, 512), F32),
                   jax.ShapeDtypeStruct((B, N_HEADS, N, 128, 128), F32)],
        scratch_shapes=[pltpu.VMEM((B, N_HEADS, 128, 128), F32)],
        compiler_params=_arb(1))(qkv3, qkv3, U3, W3, GB3, A)


def gdn_scan_bwd(dO3, qkv3, W3, Vn3, GB3, A, St, B, S):
    N = S // CHUNK

    def body(do_ref, q_ref, k_ref, w_ref, vn_ref, gb_ref, a_ref, st_ref,
             du_ref, dw_ref, dq_ref, dk_ref, da_ref, dg_ref, ds_s):
        @pl.when(pl.program_id(0) == 0)
        def _():
            ds_s[...] = jnp.zeros_like(ds_s)

        lane = lax.broadcasted_iota(jnp.int32, (1, 128), 1)
        last = lax.broadcasted_iota(jnp.int32, (CHUNK, 1), 0) == CHUNK - 1
        for b in range(B):
            dg_all = jnp.zeros((CHUNK, 128), F32)
            for h in range(N_HEADS):
                sl = slice(h * 128, (h + 1) * 128)
                st = st_ref[b, h]
                stb = st.astype(BF16)
                dsn = ds_s[b, h]
                dsnb = dsn.astype(BF16)
                g = gb_ref[b, :, h:h + 1]
                gl = g[CHUNK - 1:CHUNK, :]
                egl = jnp.exp(gl)
                ekd = jnp.exp(gl - g)
                eg = jnp.exp(g)
                q, k = q_ref[b, :, sl], k_ref[b, :, sl]
                kd = k * ekd
                qg = q * eg
                do = do_ref[b, :, sl].astype(BF16)
                vnb = vn_ref[b, :, sl].astype(BF16)
                dvn = _dot(a_ref[b, h].astype(BF16), do, TN) + _dot(kd.astype(BF16), dsnb, NN)
                dvnb = dvn.astype(BF16)
                da_ref[b, h] = _dot(do, vnb, NT)
                dqg = _dot(do, stb, NT)
                dkd = _dot(vnb, dsnb, NT)
                ds_s[b, h] = (_dot(qg.astype(BF16), do, TN) + egl * dsn - _dot(w_ref[b, :, sl].astype(BF16), dvnb, TN))
                du_ref[b, :, sl] = dvn
                dw_ref[b, :, sl] = -_dot(dvnb, stb, NT)
                dq_ref[b, :, sl] = dqg * eg
                dk_ref[b, :, sl] = dkd * ekd
                ddel = jnp.sum(dkd * kd, axis=1, keepdims=True)
                dgl = jnp.sum(ddel, axis=0, keepdims=True) + jnp.sum(jnp.sum(st * dsn, axis=1, keepdims=True), axis=0, keepdims=True) * egl
                col = jnp.sum(dqg * qg, axis=1, keepdims=True) - ddel + jnp.where(last, dgl, 0.0)
                dg_all = jnp.where(lane == h, col, dg_all)
            dg_ref[b] = dg_all

    tok = lambda c, j: BS((B, CHUNK, c), lambda n: (0, N - 1 - n, j))
    mat = lambda d: BS((B, N_HEADS, None, d, d), lambda n: (0, 0, N - 1 - n, 0, 0))
    return pl.pallas_call(
        body, name="gdn_scan_bwd", grid=(N,),
        in_specs=[tok(512, 0), tok(512, 0), tok(512, 1), tok(512, 0), tok(512, 0), tok(128, 0), mat(CHUNK), mat(128)],
        out_specs=[tok(512, 0), tok(512, 0), tok(512, 0), tok(512, 0), mat(CHUNK), tok(128, 0)],
        out_shape=[jax.ShapeDtypeStruct((B, S, 512), F32)] * 4
        + [jax.ShapeDtypeStruct((B, N_HEADS, N, CHUNK, CHUNK), F32), jax.ShapeDtypeStruct((B, S, 128), F32)],
        scratch_shapes=[pltpu.VMEM((B, N_HEADS, 128, 128), F32)],
        compiler_params=_arb(1))(dO3, qkv3, qkv3, W3, Vn3, GB3, A, St)


def gdn_chunk_bwd(qkv, GB, Grow, Tinv, dA, dU, dW, dQ1, dK1, dG1, B, S, nc=8):
    T = B * S
    N = S // CHUNK
    nc = min(nc, N)
    nb = N // nc
    R = nc * CHUNK

    def body(q_ref, k_ref, v_ref, gb_ref, gr_ref, t_ref, da_ref, du_ref, dw_ref, dq1_ref, dk1_ref, dg1_ref, o_ref, dgb_ref):
        incl, strict = _chunk_masks(nc)
        lane = lax.broadcasted_iota(jnp.int32, (1, 128), 1)
        dg_all = dg1_ref[...]
        db_all = jnp.zeros((R, 128), F32)
        for h in range(N_HEADS):
            sl = slice(h * 128, (h + 1) * 128)
            q = q_ref[:, sl].reshape(nc, CHUNK, 128)
            k = k_ref[:, sl].reshape(nc, CHUNK, 128)
            v = v_ref[:, sl].reshape(nc, CHUNK, 128)
            gc = gb_ref[:, h:h + 1].reshape(nc, CHUNK, 1)
            beta = gb_ref[:, 4 + h:5 + h].reshape(nc, CHUNK, 1)
            gr = gr_ref[h][:, None, :]
            decay, kb, l_mat, a_mat = _chunk_local(q, k, gc, gr, beta, incl, strict)
            eg = jnp.exp(gc)
            kbg = kb * eg
            vb = v * beta
            tb = t_ref[h].astype(BF16)
            du = du_ref[:, sl].reshape(nc, CHUNK, 128).astype(BF16)
            dw = dw_ref[:, sl].reshape(nc, CHUNK, 128).astype(BF16)
            dvb = _bdot("gcj,gcv->gjv", tb, du)
            dkbg = _bdot("gcj,gck->gjk", tb, dw)
            dt = _bdot("gcv,gjv->gcj", du, vb.astype(BF16)) + _bdot("gck,gjk->gcj", dw, kbg.astype(BF16))
            tmp = _bdot("gac,gab->gcb", tb, dt.astype(BF16))
            dl = jnp.where(strict, -_bdot("gcb,gdb->gcd", tmp.astype(BF16), tb), 0.0)
            da = da_ref[h]
            dm = (dl * decay).astype(BF16)
            dqk = (da * decay).astype(BF16)
            kbf = k.astype(BF16)
            dkb = _bdot("gcj,gjd->gcd", dm, kbf) + dkbg * eg
            dk = (_bdot("gcj,gcd->gjd", dm, kb.astype(BF16)) + _bdot("gcj,gcd->gjd", dqk, q.astype(BF16))
                  + dk1_ref[:, sl].reshape(nc, CHUNK, 128) + dkb * beta)
            dq = _bdot("gcj,gjd->gcd", dqk, kbf) + dq1_ref[:, sl].reshape(nc, CHUNK, 128)
            e = dl * l_mat + da * a_mat
            dgc = (jnp.sum(e, axis=2, keepdims=True) - jnp.sum(jnp.swapaxes(e, 1, 2), axis=2, keepdims=True)
                   + jnp.sum(dkbg * kbg, axis=2, keepdims=True))
            dbeta = jnp.sum(dkb * k, axis=2, keepdims=True) + jnp.sum(dvb * v, axis=2, keepdims=True)
            o_ref[:, sl] = dq.reshape(R, 128)
            o_ref[:, 512 + h * 128:512 + (h + 1) * 128] = dk.reshape(R, 128)
            o_ref[:, 1024 + h * 128:1024 + (h + 1) * 128] = (dvb * beta).reshape(R, 128)
            dg_all = dg_all + jnp.where(lane == h, dgc.reshape(R, 1), 0.0)
            db_all = jnp.where(lane == 4 + h, dbeta.reshape(R, 1), db_all)
        t = _chunk_row(R)
        for s in (1, 2, 4, 8, 16, 32):
            dg_all = dg_all + jnp.where(t + s < CHUNK, pltpu.roll(dg_all, R - s, 0), 0.0)
        dgb_ref[...] = jnp.where(lane < 4, dg_all, db_all)

    rowb = lambda c, j: BS((R, c), lambda b, n: (b * nb + n, j))
    mat = BS((None, N_HEADS, nc, CHUNK, CHUNK), lambda b, n: (b, 0, n, 0, 0))
    return pl.pallas_call(
        body, name="gdn_chunk_bwd", grid=(B, nb),
        in_specs=[rowb(512, 0), rowb(512, 1), rowb(512, 2), rowb(128, 0),
                  BS((None, N_HEADS, nc, CHUNK), lambda b, n: (b, 0, n, 0)), mat, mat,
                  rowb(512, 0), rowb(512, 0), rowb(512, 0), rowb(512, 0), rowb(128, 0)],
        out_specs=[rowb(GDN_QKV, 0), rowb(128, 0)],
        out_shape=[jax.ShapeDtypeStruct((T, GDN_QKV), F32), jax.ShapeDtypeStruct((T, 128), F32)],
        compiler_params=_arb(2))(qkv, qkv, qkv, GB, Grow, Tinv, dA, dU, dW, dQ1, dK1, dG1)


def _gdn_out_norm(og, gg):
    outs, xhs, rs = [], [], []
    for h in range(N_HEADS):
        seg = og[:, h * 128:(h + 1) * 128]
        r = lax.rsqrt(jnp.mean(seg * seg, axis=-1, keepdims=True) + EPS)
        xh = seg * r
        outs.append(xh * gg)
        xhs.append(xh)
        rs.append(r)
    return outs, xhs, rs


def merge_fwd(o_mla, o_gdn, o_mem, P, x, tgt, w_out, g_gdn, g_fin, tm=256):
    T = x.shape[0]
    tm = min(tm, T)

    def body(om_ref, og_ref, oc_ref, gate_ref, x_ref, t_ref, w_ref, gg_ref, gf_ref, mix_ref, dx_ref, dxb_ref, sq_ref, gnf_ref):
        @pl.when(pl.program_id(0) == 0)
        def _():
            sq_ref[...] = jnp.zeros_like(sq_ref)
            gnf_ref[...] = jnp.zeros_like(gnf_ref)

        ogn, _, _ = _gdn_out_norm(og_ref[...], gg_ref[...])
        cat = jnp.concatenate([om_ref[...]] + ogn + [oc_ref[...]], axis=1)
        gt = gate_ref[...]
        mixed = (cat * (gt * _sigmoid(gt))).astype(BF16)
        mix_ref[...] = mixed
        x2 = x_ref[...] + _dot(mixed, w_ref[...], NN)
        r2 = lax.rsqrt(jnp.mean(x2 * x2, axis=-1, keepdims=True) + EPS)
        xh = x2 * r2
        gf = gf_ref[...]
        diff = xh * gf - t_ref[...]
        sq_ref[...] += jnp.sum(diff * diff, axis=0, keepdims=True)
        dy = diff * (1.0 / D_MODEL)
        gnf_ref[...] += jnp.sum(dy * xh, axis=0, keepdims=True)
        dxh = dy * gf
        dx = r2 * (dxh - xh * jnp.mean(dxh * xh, axis=-1, keepdims=True))
        dx_ref[...] = dx
        dxb_ref[...] = dx.astype(BF16)

    rowb = lambda c, j=0: BS((tm, c), lambda i: (i, j))
    full = lambda r, c: BS((r, c), lambda i: (0, 0))
    return pl.pallas_call(
        body, name="merge_fwd", grid=(T // tm,),
        in_specs=[rowb(512), rowb(512), rowb(512), rowb(D_MIX, OFF_GATE // D_MIX), rowb(D_MODEL), rowb(D_MODEL),
                  full(D_MIX, D_MODEL), full(1, 128), full(1, D_MODEL)],
        out_specs=[rowb(D_MIX), rowb(D_MODEL), rowb(D_MODEL), full(1, D_MODEL), full(1, D_MODEL)],
        out_shape=[jax.ShapeDtypeStruct((T, D_MIX), BF16), jax.ShapeDtypeStruct((T, D_MODEL), F32),
                   jax.ShapeDtypeStruct((T, D_MODEL), BF16),
                   jax.ShapeDtypeStruct((1, D_MODEL), F32), jax.ShapeDtypeStruct((1, D_MODEL), F32)],
        compiler_params=_arb(1))(o_mla, o_gdn, o_mem, P, x, tgt, w_out, g_gdn, g_fin)


def merge_bwd(dx2, o_mla, o_gdn, o_mem, P, w_out, g_gdn, tm=256):
    T = dx2.shape[0]
    tm = min(tm, T)

    def body(dx_ref, om_ref, og_ref, oc_ref, gate_ref, w_ref, gg_ref, dgate_ref, dom_ref, dog_ref, doc_ref, ggn_ref):
        @pl.when(pl.program_id(0) == 0)
        def _():
            ggn_ref[...] = jnp.zeros_like(ggn_ref)

        gg = gg_ref[...]
        dmix = _dot(dx_ref[...].astype(BF16), w_ref[...], NT)
        ogn, xhs, rs = _gdn_out_norm(og_ref[...], gg)
        cat = jnp.concatenate([om_ref[...]] + ogn + [oc_ref[...]], axis=1)
        gt = gate_ref[...]
        sg = _sigmoid(gt)
        dgate_ref[...] = (dmix * cat * (sg * (1.0 + gt * (1.0 - sg)))).astype(BF16)
        dcat = dmix * (gt * sg)
        dom_ref[...] = dcat[:, :512]
        doc_ref[...] = dcat[:, 1024:]
        acc = jnp.zeros((1, 128), F32)
        for h in range(N_HEADS):
            dseg = dcat[:, 512 + h * 128:512 + (h + 1) * 128]
            acc = acc + jnp.sum(dseg * xhs[h], axis=0, keepdims=True)
            dxh = dseg * gg
            dog_ref[:, h * 128:(h + 1) * 128] = rs[h] * (dxh - xhs[h] * jnp.mean(dxh * xhs[h], axis=-1, keepdims=True))
        ggn_ref[...] += acc

    rowb = lambda c, j=0: BS((tm, c), lambda i: (i, j))
    full = lambda r, c: BS((r, c), lambda i: (0, 0))
    return pl.pallas_call(
        body, name="merge_bwd", grid=(T // tm,),
        in_specs=[rowb(D_MODEL), rowb(512), rowb(512), rowb(512), rowb(D_MIX, OFF_GATE // D_MIX),
                  full(D_MIX, D_MODEL), full(1, 128)],
        out_specs=[rowb(D_MIX), rowb(512), rowb(512), rowb(512), full(1, 128)],
        out_shape=[jax.ShapeDtypeStruct((T, D_MIX), BF16)] + [jax.ShapeDtypeStruct((T, 512), F32)] * 3
        + [jax.ShapeDtypeStruct((1, 128), F32)],
        compiler_params=_arb(1))(dx2, o_mla, o_gdn, o_mem, P, w_out, g_gdn)


def in_norm_bwd(x, dh, dx2, gain, tm=256):
    T, n = x.shape
    tm = min(tm, T)

    def body(x_ref, dh_ref, dx2_ref, g_ref, o_ref, acc_ref):
        @pl.when(pl.program_id(0) == 0)
        def _():
            acc_ref[...] = jnp.zeros_like(acc_ref)

        xv = x_ref[...]
        r = lax.rsqrt(jnp.mean(xv * xv, axis=-1, keepdims=True) + EPS)
        xh = xv * r
        dy = dh_ref[...]
        acc_ref[...] += jnp.sum(dy * xh, axis=0, keepdims=True)
        dxh = dy * g_ref[...]
        o_ref[...] = dx2_ref[...] + r * (dxh - xh * jnp.mean(dxh * xh, axis=-1, keepdims=True))

    rowb = BS((tm, n), lambda i: (i, 0))
    full = BS((1, n), lambda i: (0, 0))
    return pl.pallas_call(
        body, name="in_norm_bwd", grid=(T // tm,),
        in_specs=[rowb, rowb, rowb, full], out_specs=[rowb, full],
        out_shape=[jax.ShapeDtypeStruct((T, n), F32), jax.ShapeDtypeStruct((1, n), F32)],
        compiler_params=_arb(1))(x, dh, dx2, gain)


W_IN_SHARD = D_IN // 4
_GDN0 = Q_LORA + KV_LORA + MLA_ROPE
_AB0 = _GDN0 + GDN_QKV
_MEMQ0 = _AB0 + 2 * N_HEADS
_GATE0 = _MEMQ0 + N_HEADS * MEM_DH


def _w_in_row_map():
    a, m, gt = _AB0 - 2 * W_IN_SHARD, _MEMQ0 - 2 * W_IN_SHARD, _GATE0 - 2 * W_IN_SHARD
    e0 = OFF_GDN + W_IN_SHARD - _GDN0
    e1 = e0 + W_IN_SHARD
    e2 = OFF_GATE + W_IN_SHARD - gt
    return [(0, 0, 0, 672), (0, 672, 704, 32), (2, a, 768, m - a), (2, m, OFF_MEMQ, gt - m), (0, _GDN0, OFF_GDN, W_IN_SHARD - _GDN0),
            (1, 0, e0, W_IN_SHARD), (2, 0, e1, a), (2, gt, OFF_GATE, W_IN_SHARD - gt), (3, 0, e2, W_IN_SHARD)]


_W_IN_ZERO_ROWS = [(672, 32), (736, 32), (776, 248)]
W_IN_LANES = 256


def pad_w_in_t(shards):
    def body(s_ref, o_ref):
        for r0, n in _W_IN_ZERO_ROWS:
            o_ref[r0:r0 + n, :] = jnp.zeros((n, W_IN_LANES), o_ref.dtype)
        for q, src, dst, n in _w_in_row_map():
            o_ref[dst:dst + n, :] = s_ref[q, src:src + n, :]

    return pl.pallas_call(
        body, name="pad_w_in_t", grid=(D_MODEL // W_IN_LANES,),
        in_specs=[BS((N_CHIPS, W_IN_SHARD, W_IN_LANES), lambda j: (0, 0, j))], out_specs=BS((N_PAD, W_IN_LANES), lambda j: (0, j)),
        out_shape=jax.ShapeDtypeStruct((N_PAD, D_MODEL), shards.dtype), compiler_params=_arb(1))(shards)


def unpad_w_in_t(g):
    def body(g_ref, o_ref):
        for q, src, dst, n in _w_in_row_map():
            o_ref[q, src:src + n, :] = g_ref[dst:dst + n, :]

    return pl.pallas_call(
        body, name="unpad_w_in_t", grid=(D_MODEL // W_IN_LANES,),
        in_specs=[BS((N_PAD, W_IN_LANES), lambda j: (0, j))], out_specs=BS((N_CHIPS, W_IN_SHARD, W_IN_LANES), lambda j: (0, 0, j)),
        out_shape=jax.ShapeDtypeStruct((N_CHIPS, W_IN_SHARD, D_MODEL), g.dtype), compiler_params=_arb(1))(g)


def _pad_w_q_b_t(s):
    z = jnp.zeros((32, s.shape[2]), s.dtype)
    parts = []
    for h in range(N_HEADS):
        parts += [s[h, :128], s[h, 128:160], z, s[h, 160:192], z]
    return jnp.concatenate(parts, axis=0)


def _unpad_w_q_b_t(g):
    return jnp.stack([jnp.concatenate([g[h * HEAD_PAD:h * HEAD_PAD + 128], g[h * HEAD_PAD + 128:h * HEAD_PAD + 160],
                                       g[h * HEAD_PAD + 192:h * HEAD_PAD + 224]]) for h in range(N_HEADS)])


def _perm_w_kv_b(s):
    return jnp.concatenate([s[h, :, :128] for h in range(N_HEADS)] + [s[h, :, 128:] for h in range(N_HEADS)], axis=1)


def _unperm_w_kv_b(g):
    return jnp.stack([jnp.concatenate([g[:, h * 128:(h + 1) * 128], g[:, 512 + h * 128:512 + (h + 1) * 128]], axis=1)
                      for h in range(N_HEADS)])


def _lane_row(v4):
    return jnp.pad(v4.reshape(1, -1).astype(F32), ((0, 0), (0, 128 - v4.size)))


def _pack(pieces, n_rows):
    flat = jnp.concatenate([p.reshape(-1) for p in pieces])
    return jnp.pad(flat, (0, n_rows * 1024 - flat.size)).reshape(n_rows, 1024)


def _unpack(block, shapes):
    flat = block.reshape(-1)
    out, off = [], 0
    for shp in shapes:
        n = int(np.prod(shp))
        out.append(flat[off:off + n].reshape(shp))
        off += n
    return out


N_CHIPS = 4
MESH = pl.DeviceIdType.MESH
ANY = BS(memory_space=pl.ANY)


def _place():
    return lax.axis_index("x"), lax.axis_index("y"), lax.axis_index("c")


def _other_chips(x, y):
    return [(1 - x, y), (x, 1 - y), (1 - x, 1 - y)]


def _half(split, which):
    axis, size = split
    ds = pl.ds(pl.multiple_of(which * size, 16 if axis == 0 else 128), size)
    return (ds, slice(None)) if axis == 0 else (slice(None), ds)


def allgather_chips(shards, splits, name):
    n = len(shards)

    def body(*refs):
        s_refs, o_refs = refs[:n], refs[n:2 * n]
        send_sems, recv_sems, local_sems = refs[2 * n:]
        x, y, c = _place()
        chips = _other_chips(x, y)

        def copy(k, src, dst, to):
            return pltpu.make_async_remote_copy(src_ref=src, dst_ref=dst, send_sem=send_sems.at[k], recv_sem=recv_sems.at[k],
                                                device_id=to, device_id_type=MESH)

        owns, first, passed = [], [], []
        for i, (s_ref, o_ref) in enumerate(zip(s_refs, o_refs)):
            mine = _half(splits[i], c)
            owns.append(pltpu.make_async_copy(s_ref, o_ref.at[2 * x + y], local_sems.at[i]))
            owns[-1].start()
            for j, (px, py) in enumerate(chips):
                first.append(copy(6 * i + j, s_ref.at[mine], o_ref.at[(2 * x + y,) + mine], (px, py, c)))
                first[-1].start()
        for i, (s_ref, o_ref) in enumerate(zip(s_refs, o_refs)):
            mine = _half(splits[i], c)
            for j, (px, py) in enumerate(chips):
                landed = o_ref.at[(2 * px + py,) + mine]
                copy(6 * i + j, s_ref.at[mine], landed, (px, py, c)).wait_recv()
                passed.append(copy(6 * i + 3 + j, landed, landed, (x, y, 1 - c)))
                passed[-1].start()
        for i, (s_ref, o_ref) in enumerate(zip(s_refs, o_refs)):
            theirs = _half(splits[i], 1 - c)
            for j, (px, py) in enumerate(chips):
                copy(6 * i + 3 + j, s_ref.at[theirs], o_ref.at[(2 * px + py,) + theirs], (x, y, 1 - c)).wait_recv()
        for cp in first + passed:
            cp.wait_send()
        for cp in owns:
            cp.wait()

    return pl.pallas_call(
        body, name=name, in_specs=[ANY] * n, out_specs=[ANY] * n,
        out_shape=[jax.ShapeDtypeStruct((N_CHIPS,) + s.shape, s.dtype) for s in shards],
        scratch_shapes=[pltpu.SemaphoreType.DMA((6 * n,)), pltpu.SemaphoreType.DMA((6 * n,)), pltpu.SemaphoreType.DMA((n,))])(*shards)


def allgather_devices(block, name):
    R, C = block.shape

    def body(b_ref, o_ref, send_sems, recv_sems, local_sem):
        x, y, c = _place()
        me = 4 * x + 2 * y + c
        own = pltpu.make_async_copy(b_ref, o_ref.at[me], local_sem)
        own.start()
        copies = []
        for r in range(1, 8):
            px = 1 - x if r & 4 else x
            py = 1 - y if r & 2 else y
            pc = 1 - c if r & 1 else c
            send = pltpu.make_async_remote_copy(src_ref=b_ref, dst_ref=o_ref.at[me], send_sem=send_sems.at[r - 1],
                                                recv_sem=recv_sems.at[r - 1], device_id=(px, py, pc), device_id_type=MESH)
            recv = pltpu.make_async_remote_copy(src_ref=b_ref, dst_ref=o_ref.at[4 * px + 2 * py + pc], send_sem=send_sems.at[r - 1],
                                                recv_sem=recv_sems.at[r - 1], device_id=(px, py, pc), device_id_type=MESH)
            send.start()
            copies.append((send, recv))
        for send, recv in copies:
            recv.wait_recv()
            send.wait_send()
        own.wait()

    return pl.pallas_call(
        body, name=name, in_specs=[ANY], out_specs=ANY, out_shape=jax.ShapeDtypeStruct((8, R, C), block.dtype),
        scratch_shapes=[pltpu.SemaphoreType.DMA((7,)), pltpu.SemaphoreType.DMA((7,)), pltpu.SemaphoreType.DMA(())])(block)


def swap_sibling(arrs, name, splits=None):
    n = len(arrs)

    def sent(a_ref, i, c):
        return a_ref if splits is None else a_ref.at[(slice(None),) + _half(splits[i], 1 - c)]

    def out_shape(a, i):
        if splits is None:
            return a.shape
        axis, size = splits[i]
        return (a.shape[0], size, a.shape[2]) if axis == 0 else (a.shape[0], a.shape[1], size)

    def body(*refs):
        a_refs, o_refs = refs[:n], refs[n:2 * n]
        send_sems, recv_sems = refs[2 * n:]
        x, y, c = _place()
        copies = [pltpu.make_async_remote_copy(src_ref=sent(a_ref, i, c), dst_ref=o_ref, send_sem=send_sems.at[i],
                                               recv_sem=recv_sems.at[i], device_id=(x, y, 1 - c), device_id_type=MESH)
                  for i, (a_ref, o_ref) in enumerate(zip(a_refs, o_refs))]
        for cp in copies:
            cp.start()
        for cp in copies:
            cp.wait()

    return pl.pallas_call(
        body, name=name, in_specs=[ANY] * n, out_specs=[ANY] * n,
        out_shape=[jax.ShapeDtypeStruct(out_shape(a, i), a.dtype) for i, a in enumerate(arrs)],
        scratch_shapes=[pltpu.SemaphoreType.DMA((n,)), pltpu.SemaphoreType.DMA((n,))])(*arrs)


def exchange_chips(parts, name):
    n = len(parts)

    def body(*refs):
        p_refs, o_refs = refs[:n], refs[n:2 * n]
        send_sems, recv_sems = refs[2 * n:]
        x, y, c = _place()
        copies = [pltpu.make_async_remote_copy(src_ref=p_ref.at[2 * px + py], dst_ref=o_ref.at[j], send_sem=send_sems.at[3 * i + j],
                                               recv_sem=recv_sems.at[3 * i + j], device_id=(px, py, c), device_id_type=MESH)
                  for i, (p_ref, o_ref) in enumerate(zip(p_refs, o_refs)) for j, (px, py) in enumerate(_other_chips(x, y))]
        for cp in copies:
            cp.start()
        for cp in copies:
            cp.wait()

    return pl.pallas_call(
        body, name=name, in_specs=[ANY] * n, out_specs=[ANY] * n,
        out_shape=[jax.ShapeDtypeStruct((3,) + p.shape[1:], p.dtype) for p in parts],
        scratch_shapes=[pltpu.SemaphoreType.DMA((3 * n,)), pltpu.SemaphoreType.DMA((3 * n,))])(*parts)


def _half_block(shape2, split):
    axis, size = split
    return (size, shape2[1]) if axis == 0 else (shape2[0], size)


def add_pairs(parts, halves, splits, core, name):
    n = len(parts)

    def body(s_ref, *refs):
        for a_ref, b_ref, o_ref in zip(refs[:n], refs[n:2 * n], refs[2 * n:]):
            o_ref[...] = (a_ref[...].astype(F32) + b_ref[...].astype(F32)).astype(BF16)

    def mine(i):
        blk = (None,) + _half_block(parts[i].shape[1:], splits[i])
        if splits[i][0] == 0:
            return BS(blk, lambda q, s: (q, s[0], 0))
        return BS(blk, lambda q, s: (q, 0, s[0]))

    half_specs = [BS((None,) + h.shape[1:], lambda q, s: (q, 0, 0)) for h in halves]
    return pl.pallas_call(
        body, name=name,
        grid_spec=pltpu.PrefetchScalarGridSpec(num_scalar_prefetch=1, grid=(N_CHIPS,),
                                               in_specs=[mine(i) for i in range(n)] + half_specs, out_specs=half_specs),
        out_shape=[jax.ShapeDtypeStruct(h.shape, BF16) for h in halves], compiler_params=_arb(1))(core, *parts, *halves)


def add_fives(parts, halves, from_chips, splits, chip_core, name):
    n = len(parts)

    def body(s_ref, *refs):
        for a_ref, b_ref, p_ref, o_ref in zip(refs[:n], refs[n:2 * n], refs[2 * n:3 * n], refs[3 * n:]):
            s = a_ref[...].astype(F32) + b_ref[...].astype(F32)
            for j in range(3):
                s = s + p_ref[j].astype(F32)
            o_ref[...] = s

    def mine(i):
        blk = (None,) + _half_block(parts[i].shape[1:], splits[i])
        if splits[i][0] == 0:
            return BS(blk, lambda g, s: (s[0], s[1], 0))
        return BS(blk, lambda g, s: (s[0], 0, s[1]))

    half_specs = [BS((None,) + h.shape[1:], lambda g, s: (s[0], 0, 0)) for h in halves]
    chip_specs = [BS(p.shape, lambda g, s: (0, 0, 0)) for p in from_chips]
    out_specs = [BS(h.shape[1:], lambda g, s: (0, 0)) for h in halves]
    return pl.pallas_call(
        body, name=name,
        grid_spec=pltpu.PrefetchScalarGridSpec(num_scalar_prefetch=1, grid=(1,),
                                               in_specs=[mine(i) for i in range(n)] + half_specs + chip_specs, out_specs=out_specs),
        out_shape=[jax.ShapeDtypeStruct(h.shape[1:], F32) for h in halves], compiler_params=_arb(1))(chip_core, *parts, *halves, *from_chips)


def sum_leading(a, name):
    def body(a_ref, o_ref):
        s = a_ref[0]
        for j in range(1, a.shape[0]):
            s = s + a_ref[j]
        o_ref[...] = s

    return pl.pallas_call(body, name=name, out_shape=jax.ShapeDtypeStruct(a.shape[1:], a.dtype))(a)


def _adamw_math(w, g, m, v):
    mn = ADAM_B1 * m + (1.0 - ADAM_B1) * g
    vn = ADAM_B2 * v + (1.0 - ADAM_B2) * (g * g)
    m_hat = mn / (1.0 - ADAM_B1 ** ADAM_STEP)
    v_hat = vn / (1.0 - ADAM_B2 ** ADAM_STEP)
    return -ADAM_LR * (m_hat / (jnp.sqrt(v_hat) + ADAM_EPS) + ADAM_WD * w), mn, vn


def adamw(w, g, m, v, name):
    R, C = g.shape
    lead = (None,) * (w.ndim - 2)

    def body(w_ref, g_ref, m_ref, v_ref, d_ref, mo_ref, vo_ref):
        d_ref[...], mo_ref[...], vo_ref[...] = _adamw_math(w_ref[...], g_ref[...], m_ref[...], v_ref[...])

    wblk = BS(lead + (R, C), lambda i: (0,) * w.ndim)
    gblk = BS((R, C), lambda i: (0, 0))
    return pl.pallas_call(
        body, name=name, grid=(1,), in_specs=[wblk, gblk, wblk, wblk], out_specs=[wblk] * 3,
        out_shape=[jax.ShapeDtypeStruct(w.shape, F32)] * 3, compiler_params=_arb(1))(w, g, m, v)


def adamw_halves(w, mine, other, m, v, split, core, name):
    R, C = w.shape[-2:]
    axis, size = split
    lead = (None,) * (w.ndim - 2)
    zeros = (0,) * (w.ndim - 2)
    if axis == 0:
        tr = size if size <= 256 else next(t for t in range(256, 7, -1) if size % t == 0 and t % 8 == 0)
        nb = size // tr
        whole = BS(lead + (tr, C), lambda hi, j, s: zeros + (hi * nb + j, 0))
        part = BS((tr, C), lambda hi, j, s: (j, 0))
    else:
        nb = size // 128
        whole = BS(lead + (R, 128), lambda hi, j, s: zeros + (0, hi * nb + j))
        part = BS((R, 128), lambda hi, j, s: (0, j))

    def body(s_ref, w_ref, a_ref, b_ref, m_ref, v_ref, g_ref, d_ref, mo_ref, vo_ref):
        g = jnp.where(pl.program_id(0) == s_ref[0], a_ref[...], b_ref[...])
        g_ref[...] = g
        d_ref[...], mo_ref[...], vo_ref[...] = _adamw_math(w_ref[...], g, m_ref[...], v_ref[...])

    return pl.pallas_call(
        body, name=name,
        grid_spec=pltpu.PrefetchScalarGridSpec(num_scalar_prefetch=1, grid=(2, nb),
                                               in_specs=[whole, part, part, whole, whole], out_specs=[whole] * 4),
        out_shape=[jax.ShapeDtypeStruct(w.shape, F32)] * 4, compiler_params=_arb(2))(core, w, mine, other, m, v)


def dense_bf16(w3, name):
    R, _, K = w3.shape

    def body(w_hbm, o_ref, buf, sem):
        cp = pltpu.make_async_copy(w_hbm.at[:, 0], buf, sem)
        cp.start()
        cp.wait()
        o_ref[...] = buf[...].astype(BF16)

    return pl.pallas_call(
        body, name=name, in_specs=[ANY], out_specs=BS(memory_space=pltpu.VMEM), out_shape=jax.ShapeDtypeStruct((R, K), BF16),
        scratch_shapes=[pltpu.VMEM((R, K), F32), pltpu.SemaphoreType.DMA(())])(w3)


ROW_BLOCK = 184


def adamw_untiled_rows(w3, mine, other, m3, v3, name):
    R, _, K = w3.shape
    kh = K // 2
    starts = list(range(0, R, ROW_BLOCK))
    sizes = [min(ROW_BLOCK, R - s) for s in starts]
    nblk = len(starts)

    def body(w_hbm, a_ref, b_ref, m_hbm, v_hbm, g_hbm, d_hbm, mo_hbm, vo_hbm,
             wbuf, mbuf, vbuf, gbuf, dbuf, mobuf, vobuf, in_sems, out_sems):
        first = lax.axis_index("c") == 0
        ins = []
        for k, (r0, n) in enumerate(zip(starts, sizes)):
            rows = pl.ds(r0, n)
            cps = [pltpu.make_async_copy(src.at[rows, 0], dst.at[rows], in_sems.at[3 * k + i])
                   for i, (src, dst) in enumerate(((w_hbm, wbuf), (m_hbm, mbuf), (v_hbm, vbuf)))]
            for cp in cps:
                cp.start()
            ins.append(cps)

        def update(rows):
            a, b = a_ref[rows, :], b_ref[rows, :]
            g = jnp.concatenate([jnp.where(first, a, b), jnp.where(first, b, a)], axis=1)
            gbuf[rows, :] = g
            dbuf[rows, :], mobuf[rows, :], vobuf[rows, :] = _adamw_math(wbuf[rows, :], g, mbuf[rows, :], vbuf[rows, :])

        outs = []
        for k, (r0, n) in enumerate(zip(starts, sizes)):
            for cp in ins[k]:
                cp.wait()
            groups, tail = n // 8, n % 8

            def group(i, carry, r0=r0):
                update(pl.ds(pl.multiple_of(r0 + i * 8, 8), 8))
                return carry

            lax.fori_loop(0, groups, group, 0)
            if tail:
                update(pl.ds(r0 + groups * 8, tail))
            rows = pl.ds(r0, n)
            cps = [pltpu.make_async_copy(src.at[rows], dst.at[rows, 0], out_sems.at[4 * k + i])
                   for i, (src, dst) in enumerate(((gbuf, g_hbm), (dbuf, d_hbm), (mobuf, mo_hbm), (vobuf, vo_hbm)))]
            for cp in cps:
                cp.start()
            outs += cps
        for cp in outs:
            cp.wait()

    vmem = BS(memory_space=pltpu.VMEM)
    return pl.pallas_call(
        body, name=name, in_specs=[ANY, vmem, vmem, ANY, ANY], out_specs=[ANY] * 4,
        out_shape=[jax.ShapeDtypeStruct(w3.shape, F32)] * 4,
        scratch_shapes=[pltpu.VMEM((R, K), F32)] * 7 + [pltpu.SemaphoreType.DMA((3 * nblk,)), pltpu.SemaphoreType.DMA((4 * nblk,))])(
            w3, mine, other, m3, v3)


def adamw_w_q_b(w, mine, other, m, v, name):
    def body(w_ref, a_ref, b_ref, m_ref, v_ref, g_ref, d_ref, mo_ref, vo_ref):
        first = lax.axis_index("c") == 0
        lo = jnp.where(first, a_ref[...], b_ref[...])
        hi = jnp.where(first, b_ref[...], a_ref[...])
        g = jnp.concatenate([lo, hi[0:32], hi[64:96]], axis=0)
        g_ref[...] = g
        d_ref[...], mo_ref[...], vo_ref[...] = _adamw_math(w_ref[...], g, m_ref[...], v_ref[...])

    return pl.pallas_call(body, name=name, out_shape=[jax.ShapeDtypeStruct(w.shape, F32)] * 4)(w, mine, other, m, v)


def local_step(x, mem, positions, tgt, norm_in, w_in, q_a_norm, w_q_b, kv_a_norm, w_kv_b, gdn_conv, gdn_a_log,
               gdn_dt_bias, gdn_norm, mem_norm, w_mem_kv, w_out, norm_final):
    B, S, D = x.shape
    M = mem.shape[1]
    T = B * S
    N = S // CHUNK
    x2d = x.reshape(T, D)
    mem2d = mem.reshape(B * M, D)
    tgt2d = tgt.reshape(T, D)

    wp, wq, wkv = w_in, w_q_b, w_kv_b
    alog_row, dt_row = _lane_row(gdn_a_log), _lane_row(gdn_dt_bias)

    half = MLA_ROPE // 2
    inv_freq = 1.0 / (ROPE_THETA ** (jnp.arange(half, dtype=F32) / half))
    z32 = jnp.zeros((half,), F32)
    o32 = jnp.ones((half,), F32)
    inv_row = jnp.concatenate([inv_freq, z32, inv_freq, z32]).reshape(1, 128)
    sgn_row = jnp.concatenate([-o32, z32, o32, z32]).reshape(1, 128)
    msk_row = jnp.concatenate([o32, z32, o32, z32]).reshape(1, 128)
    cos_t, sin_t = rope_tables(positions.reshape(T, 1), inv_row, sgn_row, msk_row)

    h = rms_fwd(x2d, norm_in, "rms_in")
    P = mm(h, wp, "nt", F32, "in_proj", bm=512, bn=1536, n_outer=True)
    Q, K, V, qn, kvn = mla_prep(P, q_a_norm, kv_a_norm, wq, wkv, cos_t, sin_t)
    o_mla, lse = mla_attn_fwd(Q, K, V, B, S)
    memn = rms_fwd(mem2d, mem_norm, "rms_mem")
    MKV = mm(memn, w_mem_kv, "nn", BF16, "mem_kv_proj")
    o_mem = mem_attn_fwd(P, MKV, B, S, M)
    qkv = gdn_prep_fwd(P, gdn_conv, B, S)
    GB = gdn_gate_fwd(P, alog_row, dt_row, B, S)
    Grow = jnp.transpose(GB[:, :N_HEADS].reshape(B, N, CHUNK, N_HEADS), (0, 3, 1, 2))
    U, W, Tinv, A = gdn_chunk_fwd(qkv, GB, Grow, B, S)
    qkv3, GB3 = qkv.reshape(B, S, GDN_QKV), GB.reshape(B, S, 128)
    W3 = W.reshape(B, S, 512)
    o_gdn3, Vn3, St = gdn_scan_fwd(qkv3, U.reshape(B, S, 512), W3, GB3, A, B, S)
    o_gdn = o_gdn3.reshape(T, 512)
    mixed, dx2, dx2b, sq, g_norm_final = merge_fwd(o_mla, o_gdn, o_mem, P, x2d, tgt2d, w_out, gdn_norm, norm_final.reshape(1, D))

    g_w_out = mm(mixed, dx2b, "tn", BF16, "grad_w_out")
    dgate, do_mla, do_gdn, do_mem, g_gdn_norm = merge_bwd(dx2b, o_mla, o_gdn, o_mem, P, w_out, gdn_norm)

    dmemq, dMKV = mem_attn_bwd(P, MKV, do_mem, B, S, M)
    g_w_mem_kv = mm(memn, dMKV, "tn", BF16, "grad_w_mem_kv")
    dmemn = mm(dMKV, w_mem_kv, "nt", F32, "d_memn")
    g_mem_norm = gain_grad(mem2d, dmemn, "grad_mem_norm")

    dU3, dW3, dQ13, dK13, dA, dG13 = gdn_scan_bwd(do_gdn.reshape(B, S, 512), qkv3, W3, Vn3, GB3, A, St, B, S)
    r2 = lambda a: a.reshape(T, a.shape[-1])
    dqkv, dGB = gdn_chunk_bwd(qkv, GB, Grow, Tinv, dA, r2(dU3), r2(dW3), r2(dQ13), r2(dK13), r2(dG13), B, S)
    dPg, g_conv = gdn_prep_bwd(P, dqkv, gdn_conv, B, S)
    dab, g_ab = gdn_gate_bwd(P, dGB, alog_row, dt_row, B, S)

    dQ, dK, dV = mla_attn_bwd(Q, K, V, o_mla, do_mla, lse, B, S)
    dq_lin, dkv_lin, dkr = mla_post_bwd(dQ, dK, dV, cos_t, sin_t)
    dqn = mm(dq_lin, wq, "nn", F32, "d_qn")
    dkvn = mm(dkv_lin, wkv, "nt", F32, "d_kvn")
    g_wq = mm(dq_lin, qn, "tn", BF16, "grad_w_q_b")
    g_wkv = mm(kvn, dkv_lin, "tn", BF16, "grad_w_kv_b")
    dPm, g_q_a_norm, g_kv_a_norm = mla_norm_bwd(P, dqn, dkvn, dkr, dab, q_a_norm, kv_a_norm)

    dP = [dPm, dmemq, dPg, dgate]
    g_wp = mm_cols_tn(dP, h, BF16, "grad_w_in")
    dh = mm_cols_nn(dP, wp, F32, "d_h")
    grad_x, g_norm_in = in_norm_bwd(x2d, dh, dx2, norm_in)

    grads = dict(
        norm_in=g_norm_in, w_in=g_wp, q_a_norm=g_q_a_norm, w_q_b=g_wq, kv_a_norm=g_kv_a_norm, w_kv_b=g_wkv, gdn_conv=g_conv,
        gdn_a_log=g_ab[0:1, :N_HEADS], gdn_dt_bias=g_ab[1:2, :N_HEADS], gdn_norm=g_gdn_norm,
        mem_norm=g_mem_norm, w_mem_kv=g_w_mem_kv, w_out=g_w_out, norm_final=g_norm_final)
    return sq, grad_x.reshape(B, S, D), grads


def kernel(x, mem, positions, norm_in, w_in, q_a_norm, w_q_b, kv_a_norm, w_kv_b, gdn_conv, gdn_a_log, gdn_dt_bias, gdn_norm, mem_norm, w_mem_kv, w_out, norm_final, loss_target, m_norm_in, m_w_in, m_q_a_norm, m_w_q_b, m_kv_a_norm, m_w_kv_b, m_gdn_conv, m_gdn_a_log, m_gdn_dt_bias, m_gdn_norm, m_mem_norm, m_w_mem_kv, m_w_out, m_norm_final, v_norm_in, v_w_in, v_q_a_norm, v_w_q_b, v_kv_a_norm, v_w_kv_b, v_gdn_conv, v_gdn_a_log, v_gdn_dt_bias, v_gdn_norm, v_mem_norm, v_w_mem_kv, v_w_out, v_norm_final):
    B = x.shape[0]
    cx, cy, cc = lax.axis_index("x"), lax.axis_index("y"), lax.axis_index("c")
    chip = 2 * cx + cy

    big_names = ("w_in", "w_q_b", "w_kv_b", "w_mem_kv", "w_out")
    rows_major = lambda a: jnp.transpose(a, (2, 0, 1))
    w_in3, m_in3, v_in3 = rows_major(w_in), rows_major(m_w_in), rows_major(v_w_in)
    w_qb_t, m_qb_t, v_qb_t = jnp.transpose(w_q_b[0]), jnp.transpose(m_w_q_b[0]), jnp.transpose(v_w_q_b[0])
    z32 = jnp.zeros((32, Q_LORA), BF16)
    qb_bf = w_qb_t.astype(BF16)
    qb_padded = jnp.concatenate([qb_bf[:160], z32, qb_bf[160:], z32])
    shards = [dense_bf16(w_in3, "w_in_bf16"), qb_padded, w_kv_b[0].astype(BF16), w_mem_kv[0].astype(BF16), w_out[0].astype(BF16)]
    splits = [(1, D_MODEL // 2)] + [(0, s.shape[0] // 2) for s in shards[1:]]
    g_in, g_qb, g_kvb, g_mem, g_out_w = allgather_chips(shards, splits, "allgather_weights")
    conv_all = allgather_devices(gdn_conv[0], "allgather_conv")
    conv_cols = gdn_conv.shape[2]
    conv_full = jnp.transpose(conv_all[0::2], (1, 0, 2)).reshape(GDN_CONV, N_CHIPS * conv_cols)

    sq, grad_x, g = local_step(x, mem, positions, loss_target, norm_in, pad_w_in_t(g_in), q_a_norm,
                               g_qb.reshape(-1, Q_LORA), kv_a_norm, _perm_w_kv_b(g_kvb), conv_full, gdn_a_log, gdn_dt_bias,
                               gdn_norm, mem_norm, g_mem.reshape(-1, g_mem.shape[2]), g_out_w.reshape(-1, g_out_w.shape[2]),
                               norm_final)

    core = jnp.stack([cc]).astype(jnp.int32)
    chip_core = jnp.stack([chip, cc]).astype(jnp.int32)
    parts = [unpad_w_in_t(g["w_in"]), g["w_q_b"].reshape(g_qb.shape), _unperm_w_kv_b(g["w_kv_b"]),
             g["w_mem_kv"].reshape(g_mem.shape), g["w_out"].reshape(g_out_w.shape)]
    from_sibling = swap_sibling(parts, "rs_sibling_partial", splits)
    chip_sums = add_pairs(parts, from_sibling, splits, core, "rs_add_sibling")
    from_chips = exchange_chips(chip_sums, "rs_exchange_chips")
    my_half = add_fives(parts, from_sibling, from_chips, splits, chip_core, "rs_add_chips")
    other_half = swap_sibling(my_half, "rs_sibling_final")

    small_names = ("norm_in", "q_a_norm", "kv_a_norm", "gdn_a_log", "gdn_dt_bias", "gdn_norm", "mem_norm", "norm_final")
    small = dict(norm_in=norm_in, q_a_norm=q_a_norm, kv_a_norm=kv_a_norm, gdn_a_log=gdn_a_log, gdn_dt_bias=gdn_dt_bias,
                 gdn_norm=gdn_norm, mem_norm=mem_norm, norm_final=norm_final)
    m_small = dict(norm_in=m_norm_in, q_a_norm=m_q_a_norm, kv_a_norm=m_kv_a_norm, gdn_a_log=m_gdn_a_log,
                   gdn_dt_bias=m_gdn_dt_bias, gdn_norm=m_gdn_norm, mem_norm=m_mem_norm, norm_final=m_norm_final)
    v_small = dict(norm_in=v_norm_in, q_a_norm=v_q_a_norm, kv_a_norm=v_kv_a_norm, gdn_a_log=v_gdn_a_log,
                   gdn_dt_bias=v_gdn_dt_bias, gdn_norm=v_gdn_norm, mem_norm=v_mem_norm, norm_final=v_norm_final)
    rows = lambda d: jnp.stack([jnp.pad(d[n].reshape(-1), (0, 1024 - d[n].size)) for n in small_names])
    conv_rows = GDN_CONV * GDN_QKV // 1024
    g_block = jnp.concatenate([rows(g), g["gdn_conv"].reshape(conv_rows, 1024), sq, jnp.zeros((16 - 9 - conv_rows, 1024), F32)])
    g_block = sum_leading(allgather_devices(g_block, "allgather_small_grads"), "sum_small_grads")
    g_small_rows = g_block[:8]
    loss = 0.5 * jnp.sum(g_block[8 + conv_rows]) / D_MODEL
    g_conv = lax.dynamic_slice_in_dim(g_block[8:8 + conv_rows].reshape(GDN_CONV, GDN_QKV), chip * conv_cols, conv_cols, axis=1)
    d_s, m_s, v_s = adamw(rows(small), g_small_rows, rows(m_small), rows(v_small), "adamw_small")
    unrow = lambda r: {n: r[i, :small[n].size].reshape(small[n].shape) for i, n in enumerate(small_names)}
    g_out, d_out, m_out, v_out = unrow(g_small_rows), unrow(d_s), unrow(m_s), unrow(v_s)

    d_out["gdn_conv"], m_out["gdn_conv"], v_out["gdn_conv"] = adamw(gdn_conv, g_conv, m_gdn_conv, v_gdn_conv, "adamw_gdn_conv")
    g_out["gdn_conv"] = g_conv[None]
    res = adamw_untiled_rows(w_in3, my_half[0], other_half[0], m_in3, v_in3, "adamw_w_in")
    g_out["w_in"], d_out["w_in"], m_out["w_in"], v_out["w_in"] = [jnp.transpose(r, (1, 2, 0)) for r in res]
    res = adamw_w_q_b(w_qb_t, my_half[1], other_half[1], m_qb_t, v_qb_t, "adamw_w_q_b")
    g_out["w_q_b"], d_out["w_q_b"], m_out["w_q_b"], v_out["w_q_b"] = [jnp.transpose(r)[None] for r in res]
    rest = dict(w_kv_b=(w_kv_b, m_w_kv_b, v_w_kv_b), w_mem_kv=(w_mem_kv, m_w_mem_kv, v_w_mem_kv), w_out=(w_out, m_w_out, v_w_out))
    for i, n in enumerate(big_names):
        if n in rest:
            w_n, m_n, v_n = rest[n]
            g_out[n], d_out[n], m_out[n], v_out[n] = adamw_halves(w_n, my_half[i], other_half[i], m_n, v_n, splits[i], core, "adamw_" + n)

    order = ("norm_in", "w_in", "q_a_norm", "w_q_b", "kv_a_norm", "w_kv_b", "gdn_conv", "gdn_a_log", "gdn_dt_bias",
             "gdn_norm", "mem_norm", "w_mem_kv", "w_out", "norm_final")
    return (loss, grad_x, *[g_out[n] for n in order], *[d_out[n] for n in order], *[m_out[n] for n in order],
            *[v_out[n] for n in order])
```

```python
import functools
import math

import jax
import jax.numpy as jnp
import numpy as np
from jax import lax
from jax.experimental import pallas as pl
from jax.experimental.pallas import tpu as pltpu

F32 = jnp.float32
BF16 = jnp.bfloat16
BS = pl.BlockSpec

D_MODEL = 1024
N_HEADS = 4
MLA_NOPE, MLA_ROPE, MLA_V = 128, 64, 128
Q_LORA, KV_LORA = 384, 256
ROPE_THETA = 10000.0
GDN_DK = GDN_DV = 128
GDN_CONV = 4
CHUNK = 64
MEM_DH = 128
D_MIX = 1536
GDN_QKV = 1536
D_IN = 4296
EPS = 1e-6
ADAM_LR, ADAM_B1, ADAM_B2, ADAM_EPS, ADAM_WD, ADAM_STEP = 0.001, 0.9, 0.999, 1e-08, 0.01, 10

OFF_MLA = 0
OFF_MEMQ = 1024
OFF_GDN = 1536
OFF_GATE = 3072
N_PAD = 4608
HEAD_PAD = 256
MLA_SCALE = (MLA_NOPE + MLA_ROPE) ** -0.5
MEM_SCALE = MEM_DH ** -0.5
GDN_SCALE = GDN_DK ** -0.5
NEG = -1e30

NN = ((1,), (0,))
NT = ((1,), (1,))
TN = ((0,), (0,))


def _dot(a, b, dims):
    return lax.dot_general(a, b, (dims, ((), ())), preferred_element_type=F32)


def _bdot(spec, a, b, precision=None):
    return jnp.einsum(spec, a, b, preferred_element_type=F32, precision=precision)


def _arb(n):
    return pltpu.CompilerParams(dimension_semantics=("arbitrary",) * n)


def _sigmoid(x):
    return 1.0 / (1.0 + jnp.exp(-x))


def _softplus(z):
    return jnp.maximum(z, 0.0) + jnp.log(1.0 + jnp.exp(-jnp.abs(z)))


def _rope(t, cos_row, sin_row):
    return t * cos_row + pltpu.roll(t, 64, 1) * sin_row


def _rope_bwd(d, cos_row, sin_row):
    return d * cos_row + pltpu.roll(d * sin_row, 64, 1)


def rms_fwd(x, gain, name, tm=512):
    T, n = x.shape
    tm = min(tm, T)

    def body(x_ref, g_ref, o_ref):
        xv = x_ref[...]
        r = lax.rsqrt(jnp.mean(xv * xv, axis=-1, keepdims=True) + EPS)
        o_ref[...] = (xv * r * g_ref[...]).astype(BF16)

    return pl.pallas_call(
        body, name=name, grid=(T // tm,),
        in_specs=[BS((tm, n), lambda i: (i, 0)), BS((1, n), lambda i: (0, 0))],
        out_specs=BS((tm, n), lambda i: (i, 0)),
        out_shape=jax.ShapeDtypeStruct((T, n), BF16), compiler_params=_arb(1))(x, gain)


def mm(a, b, kind, out_dtype, name, bm=512, bn=None, n_outer=False):
    if kind == "nn":
        (M, K), (_, N) = a.shape, b.shape
    elif kind == "nt":
        (M, K), (N, _) = a.shape, b.shape
    else:
        (K, M), (_, N) = a.shape, b.shape
    bm, bn = min(bm, M), min(bn or N, N)
    assert M % bm == 0 and N % bn == 0, (name, M, N, K)
    ij = (lambda g0, g1: (g1, g0)) if n_outer else (lambda g0, g1: (g0, g1))
    a_spec = BS((K, bm), lambda g0, g1: (0, ij(g0, g1)[0])) if kind == "tn" else BS((bm, K), lambda g0, g1: (ij(g0, g1)[0], 0))
    once = dict(pipeline_mode=pl.Buffered(1)) if bn == N else {}
    b_spec = (BS((bn, K), lambda g0, g1: (ij(g0, g1)[1], 0), **once) if kind == "nt"
              else BS((K, bn), lambda g0, g1: (0, ij(g0, g1)[1]), **once))
    dims = {"nn": NN, "nt": NT, "tn": TN}[kind]

    def body(a_ref, b_ref, o_ref):
        o_ref[...] = _dot(a_ref[...].astype(BF16), b_ref[...].astype(BF16), dims).astype(out_dtype)

    grid = (N // bn, M // bm) if n_outer else (M // bm, N // bn)
    return pl.pallas_call(
        body, name=name, grid=grid, in_specs=[a_spec, b_spec], out_specs=BS((bm, bn), lambda g0, g1: ij(g0, g1)),
        out_shape=jax.ShapeDtypeStruct((M, N), out_dtype), compiler_params=_arb(2))(a, b)


def mm_cols_nn(pieces, b, out_dtype, name, bm=512):
    M, N = pieces[0].shape[0], b.shape[1]
    bm = min(bm, M)
    widths = [p.shape[1] for p in pieces]
    offs = [sum(widths[:i]) for i in range(len(widths))]

    def body(*refs):
        b_ref, o_ref = refs[-2], refs[-1]
        acc = None
        for a_ref, off, w in zip(refs[:-2], offs, widths):
            d = _dot(a_ref[...], b_ref[off:off + w, :], NN)
            acc = d if acc is None else acc + d
        o_ref[...] = acc.astype(out_dtype)

    return pl.pallas_call(
        body, name=name, grid=(M // bm,),
        in_specs=[BS((bm, w), lambda i: (i, 0)) for w in widths] + [BS(b.shape, lambda i: (0, 0), pipeline_mode=pl.Buffered(1))],
        out_specs=BS((bm, N), lambda i: (i, 0)), out_shape=jax.ShapeDtypeStruct((M, N), out_dtype),
        compiler_params=_arb(1))(*pieces, b)


def mm_cols_tn(pieces, b, out_dtype, name, bm=512):
    K, N = b.shape
    tiles = [p.shape[1] // bm for p in pieces]
    firsts = [sum(tiles[:i]) for i in range(len(tiles))]

    def body(*refs):
        b_ref, o_ref = refs[-2], refs[-1]
        i = pl.program_id(0)
        for a_ref, t0, n in zip(refs[:-2], firsts, tiles):
            @pl.when((i >= t0) & (i < t0 + n))
            def _(a_ref=a_ref):
                o_ref[...] = _dot(a_ref[...], b_ref[...], TN).astype(out_dtype)

    a_specs = [BS((K, bm), lambda i, t0=t0, n=n: (0, jnp.clip(i - t0, 0, n - 1))) for t0, n in zip(firsts, tiles)]
    return pl.pallas_call(
        body, name=name, grid=(sum(tiles),),
        in_specs=a_specs + [BS(b.shape, lambda i: (0, 0), pipeline_mode=pl.Buffered(1))],
        out_specs=BS((bm, N), lambda i: (i, 0)), out_shape=jax.ShapeDtypeStruct((sum(tiles) * bm, N), out_dtype),
        compiler_params=_arb(1))(*pieces, b)


def rope_tables(pos_col, inv_row, sgn_row, msk_row, tm=512):
    T = pos_col.shape[0]
    tm = min(tm, T)

    def body(p_ref, inv_ref, sgn_ref, msk_ref, c_ref, s_ref):
        ang = p_ref[...].astype(F32) * inv_ref[...]
        c_ref[...] = jnp.cos(ang) * msk_ref[...]
        s_ref[...] = jnp.sin(ang) * sgn_ref[...]

    row = BS((1, 128), lambda i: (0, 0))
    return pl.pallas_call(
        body, name="rope_tables", grid=(T // tm,),
        in_specs=[BS((tm, 1), lambda i: (i, 0)), row, row, row],
        out_specs=[BS((tm, 128), lambda i: (i, 0))] * 2,
        out_shape=[jax.ShapeDtypeStruct((T, 128), F32)] * 2, compiler_params=_arb(1))(pos_col, inv_row, sgn_row, msk_row)


def mla_prep(P, gq, gkv, wq, wkv, cos_t, sin_t, tm=512):
    T = P.shape[0]
    tm = min(tm, T)

    def body(p_ref, gq_ref, gkv_ref, wq_ref, wkv_ref, c_ref, s_ref, q_ref, k_ref, v_ref, qn_ref, kvn_ref):
        p = p_ref[...]
        cq, ckv, kr = p[:, :Q_LORA], p[:, Q_LORA:Q_LORA + KV_LORA], p[:, 640:768]
        qn = (cq * lax.rsqrt(jnp.mean(cq * cq, axis=-1, keepdims=True) + EPS) * gq_ref[...]).astype(BF16)
        kvn = (ckv * lax.rsqrt(jnp.mean(ckv * ckv, axis=-1, keepdims=True) + EPS) * gkv_ref[...]).astype(BF16)
        qn_ref[...] = qn
        kvn_ref[...] = kvn
        q = _dot(qn, wq_ref[...], NT)
        kv = _dot(kvn, wkv_ref[...], NN)
        cos_row, sin_row = c_ref[...], s_ref[...]
        krr = _rope(kr, cos_row, sin_row).astype(BF16)
        for h in range(N_HEADS):
            lo = h * HEAD_PAD
            q_ref[:, lo:lo + 128] = (q[:, lo:lo + 128] * MLA_SCALE).astype(BF16)
            q_ref[:, lo + 128:lo + 256] = (_rope(q[:, lo + 128:lo + 256], cos_row, sin_row) * MLA_SCALE).astype(BF16)
            k_ref[:, lo:lo + 128] = kv[:, h * 128:(h + 1) * 128].astype(BF16)
            k_ref[:, lo + 128:lo + 256] = krr
            v_ref[:, lo:lo + 128] = kv[:, 512 + h * 128:512 + (h + 1) * 128].astype(BF16)
            v_ref[:, lo + 128:lo + 256] = jnp.ones((tm, 128), BF16)

    full = lambda r, c: BS((r, c), lambda i: (0, 0))
    rowb = lambda c: BS((tm, c), lambda i: (i, 0))
    return pl.pallas_call(
        body, name="mla_prep", grid=(T // tm,),
        in_specs=[rowb(1024), full(1, Q_LORA), full(1, KV_LORA), full(1024, Q_LORA), full(KV_LORA, 1024), rowb(128), rowb(128)],
        out_specs=[rowb(1024), rowb(1024), rowb(1024), rowb(Q_LORA), rowb(KV_LORA)],
        out_shape=[jax.ShapeDtypeStruct((T, 1024), BF16), jax.ShapeDtypeStruct((T, 1024), BF16),
                   jax.ShapeDtypeStruct((T, 1024), BF16), jax.ShapeDtypeStruct((T, Q_LORA), BF16),
                   jax.ShapeDtypeStruct((T, KV_LORA), BF16)],
        compiler_params=_arb(1))(P, gq, gkv, wq, wkv, cos_t, sin_t)


ATTN_HEADS_PER_STEP = 2
ATTN_STRIP = 32


def mla_attn_fwd(Q, K, V, B, S, tq=512, hp=ATTN_HEADS_PER_STEP):
    T = B * S
    tq = min(tq, S)
    nq = S // tq

    rs = min(ATTN_STRIP, tq)

    def body(q_ref, k_ref, v_ref, o_ref, lse_ref, m_s, acc_s, s_s, p_s, a_s):
        i = pl.program_id(2)
        m_s[...] = jnp.full_like(m_s, NEG)
        acc_s[...] = jnp.zeros_like(acc_s)

        def blk(j, masked):
            rows = pl.ds(pl.multiple_of(j * tq, tq), tq)
            for h in range(hp):
                hq = slice(h * HEAD_PAD, (h + 1) * HEAD_PAD)
                s_s[h] = _dot(q_ref[:, hq], k_ref[rows, hq], NT)
            for h in range(hp):
                for r0 in range(0, tq, rs):
                    rr = slice(r0, r0 + rs)
                    sv = s_s[h, rr, :]
                    if masked:
                        r = r0 + lax.broadcasted_iota(jnp.int32, (rs, tq), 0)
                        c = lax.broadcasted_iota(jnp.int32, (rs, tq), 1)
                        sv = jnp.where(r >= c, sv, NEG)
                    m_prev = m_s[h, rr, :]
                    m_new = jnp.maximum(m_prev, jnp.max(sv, axis=1, keepdims=True))
                    p_s[h, rr, :] = jnp.exp(sv - m_new).astype(BF16)
                    a_s[h, rr, :] = jnp.exp(m_prev - m_new)
                    m_s[h, rr, :] = m_new
            for h in range(hp):
                hq = slice(h * HEAD_PAD, (h + 1) * HEAD_PAD)
                acc_s[h] = a_s[h] * acc_s[h] + _dot(p_s[h], v_ref[rows, hq], NN)

        def loop(j, c):
            blk(j, False)
            return c

        lax.fori_loop(0, i, loop, 0)
        blk(i, True)
        for h in range(hp):
            den = acc_s[h, :, 128:256]
            o_ref[:, h * 128:(h + 1) * 128] = acc_s[h, :, 0:128] / den
            lse_ref[h] = m_s[h] + jnp.log(den[:, 0:1])

    return pl.pallas_call(
        body, name="mla_attn_fwd", grid=(B, N_HEADS // hp, nq),
        in_specs=[BS((tq, hp * HEAD_PAD), lambda b, h, i: (b * nq + i, h)),
                  BS((S, hp * HEAD_PAD), lambda b, h, i: (b, h)),
                  BS((S, hp * HEAD_PAD), lambda b, h, i: (b, h))],
        out_specs=[BS((tq, hp * 128), lambda b, h, i: (b * nq + i, h)),
                   BS((hp, tq, 1), lambda b, h, i: (h, b * nq + i, 0))],
        out_shape=[jax.ShapeDtypeStruct((T, 512), F32), jax.ShapeDtypeStruct((N_HEADS, T, 1), F32)],
        scratch_shapes=[pltpu.VMEM((hp, tq, 1), F32), pltpu.VMEM((hp, tq, HEAD_PAD), F32), pltpu.VMEM((hp, tq, tq), F32),
                        pltpu.VMEM((hp, tq, tq), BF16), pltpu.VMEM((hp, tq, 1), F32)],
        compiler_params=_arb(3))(Q, K, V)


def mla_attn_bwd(Q, K, V, O, dO, LSE, B, S, tq=512, hp=ATTN_HEADS_PER_STEP):
    T = B * S
    tq = min(tq, S)
    nq = S // tq

    def body(q_ref, k_ref, v_ref, o_ref, do_ref, lse_ref, dq_ref, dk_ref, dv_ref, delta_s, dk_s, dv_s):
        j = pl.program_id(2)

        @pl.when(j == 0)
        def _():
            dq_ref[...] = jnp.zeros_like(dq_ref)
            for h in range(hp):
                sl = slice(h * 128, (h + 1) * 128)
                delta_s[h] = jnp.sum(do_ref[:, sl] * o_ref[:, sl], axis=1, keepdims=True)

        dk_s[...] = jnp.zeros_like(dk_s)
        dv_s[...] = jnp.zeros_like(dv_s)

        def step(i, c):
            rows = pl.ds(pl.multiple_of(i * tq, tq), tq)
            r = i * tq + lax.broadcasted_iota(jnp.int32, (tq, tq), 0)
            cc = j * tq + lax.broadcasted_iota(jnp.int32, (tq, tq), 1)
            causal = r >= cc
            for h in range(hp):
                sq, sv = slice(h * HEAD_PAD, (h + 1) * HEAD_PAD), slice(h * 128, (h + 1) * 128)
                q = q_ref[rows, sq]
                k = k_ref[:, sq]
                do = do_ref[rows, sv].astype(BF16)
                s = _dot(q, k, NT)
                p = jnp.where(causal, jnp.exp(s - lse_ref[h, rows, :]), 0.0)
                dv_s[:, sv] += _dot(p.astype(BF16), do, TN)
                dp = _dot(do, v_ref[:, h * HEAD_PAD:h * HEAD_PAD + 128], NT)
                ds = (p * (dp - delta_s[h, rows, :])).astype(BF16)
                dk_s[:, sq] += _dot(ds, q, TN)
                dq_ref[rows, sq] += _dot(ds, k, NN)
            return c

        lax.fori_loop(j, nq, step, 0)
        dk_ref[...] = dk_s[...]
        dv_ref[...] = dv_s[...]

    seq = lambda c: BS((S, c), lambda b, h, j: (b, h))
    blk = lambda c: BS((tq, c), lambda b, h, j: (b * nq + j, h))
    return pl.pallas_call(
        body, name="mla_attn_bwd", grid=(B, N_HEADS // hp, nq),
        in_specs=[seq(hp * HEAD_PAD), blk(hp * HEAD_PAD), blk(hp * HEAD_PAD), seq(hp * 128), seq(hp * 128),
                  BS((hp, S, 1), lambda b, h, j: (h, b, 0))],
        out_specs=[seq(hp * HEAD_PAD), blk(hp * HEAD_PAD), blk(hp * 128)],
        out_shape=[jax.ShapeDtypeStruct((T, 1024), F32), jax.ShapeDtypeStruct((T, 1024), F32),
                   jax.ShapeDtypeStruct((T, 512), F32)],
        scratch_shapes=[pltpu.VMEM((hp, S, 1), F32), pltpu.VMEM((tq, hp * HEAD_PAD), F32), pltpu.VMEM((tq, hp * 128), F32)],
        compiler_params=_arb(3))(Q, K, V, O, dO, LSE)


def mla_post_bwd(dQ, dK, dV, cos_t, sin_t, tm=512):
    T = dQ.shape[0]
    tm = min(tm, T)

    def body(dq_ref, dk_ref, dv_ref, c_ref, s_ref, ql_ref, kvl_ref, kr_ref):
        cos_row, sin_row = c_ref[...], s_ref[...]
        kr = jnp.zeros((tm, 128), F32)
        for h in range(N_HEADS):
            lo = h * HEAD_PAD
            ql_ref[:, lo:lo + 128] = (dq_ref[:, lo:lo + 128] * MLA_SCALE).astype(BF16)
            ql_ref[:, lo + 128:lo + 256] = (_rope_bwd(dq_ref[:, lo + 128:lo + 256], cos_row, sin_row) * MLA_SCALE).astype(BF16)
            kvl_ref[:, h * 128:(h + 1) * 128] = dk_ref[:, lo:lo + 128].astype(BF16)
            kr = kr + dk_ref[:, lo + 128:lo + 256]
        kvl_ref[:, 512:] = dv_ref[...].astype(BF16)
        kr_ref[...] = _rope_bwd(kr, cos_row, sin_row)

    rowb = lambda c: BS((tm, c), lambda i: (i, 0))
    return pl.pallas_call(
        body, name="mla_post_bwd", grid=(T // tm,),
        in_specs=[rowb(1024), rowb(1024), rowb(512), rowb(128), rowb(128)],
        out_specs=[rowb(1024), rowb(1024), rowb(128)],
        out_shape=[jax.ShapeDtypeStruct((T, 1024), BF16), jax.ShapeDtypeStruct((T, 1024), BF16),
                   jax.ShapeDtypeStruct((T, 128), F32)],
        compiler_params=_arb(1))(dQ, dK, dV, cos_t, sin_t)


def mla_norm_bwd(P, dqn, dkvn, dkr, dab, gq, gkv, tm=512):
    T = P.shape[0]
    tm = min(tm, T)

    def norm_bwd(x, dy, g):
        r = lax.rsqrt(jnp.mean(x * x, axis=-1, keepdims=True) + EPS)
        xh = x * r
        dxh = dy * g
        return r * (dxh - xh * jnp.mean(dxh * xh, axis=-1, keepdims=True)), jnp.sum(dy * xh, axis=0, keepdims=True)

    def body(p_ref, dqn_ref, dkvn_ref, dkr_ref, dab_ref, gq_ref, gkv_ref, o_ref, aq_ref, akv_ref):
        @pl.when(pl.program_id(0) == 0)
        def _():
            aq_ref[...] = jnp.zeros_like(aq_ref)
            akv_ref[...] = jnp.zeros_like(akv_ref)

        dcq, ggq = norm_bwd(p_ref[:, :Q_LORA], dqn_ref[...], gq_ref[...])
        dckv, ggkv = norm_bwd(p_ref[:, Q_LORA:640], dkvn_ref[...], gkv_ref[...])
        aq_ref[...] += ggq
        akv_ref[...] += ggkv
        o_ref[:, :Q_LORA] = dcq.astype(BF16)
        o_ref[:, Q_LORA:640] = dckv.astype(BF16)
        o_ref[:, 640:768] = dkr_ref[...].astype(BF16)
        o_ref[:, 768:896] = dab_ref[...]
        o_ref[:, 896:1024] = jnp.zeros((tm, 128), BF16)

    rowb = lambda c: BS((tm, c), lambda i: (i, 0))
    full = lambda c: BS((1, c), lambda i: (0, 0))
    return pl.pallas_call(
        body, name="mla_norm_bwd", grid=(T // tm,),
        in_specs=[rowb(1024), rowb(Q_LORA), rowb(KV_LORA), rowb(128), rowb(128), full(Q_LORA), full(KV_LORA)],
        out_specs=[rowb(1024), full(Q_LORA), full(KV_LORA)],
        out_shape=[jax.ShapeDtypeStruct((T, 1024), BF16), jax.ShapeDtypeStruct((1, Q_LORA), F32),
                   jax.ShapeDtypeStruct((1, KV_LORA), F32)],
        compiler_params=_arb(1))(P, dqn, dkvn, dkr, dab, gq, gkv)


def _mem_probs(qh, kh):
    s = _dot(qh, kh, NT) * MEM_SCALE
    p = jnp.exp(s - jnp.max(s, axis=1, keepdims=True))
    return p / jnp.sum(p, axis=1, keepdims=True)


def mem_attn_fwd(P, MKV, B, S, M, tq=512):
    T = B * S
    tq = min(tq, S)
    nq = S // tq

    def body(q_ref, kv_ref, o_ref):
        for h in range(N_HEADS):
            sl = slice(h * 128, (h + 1) * 128)
            p = _mem_probs(q_ref[:, sl].astype(BF16), kv_ref[:, sl])
            o_ref[:, sl] = _dot(p.astype(BF16), kv_ref[:, 512 + h * 128:512 + (h + 1) * 128], NN)

    return pl.pallas_call(
        body, name="mem_attn_fwd", grid=(B, nq),
        in_specs=[BS((tq, 512), lambda b, i: (b * nq + i, OFF_MEMQ // 512)), BS((M, 1024), lambda b, i: (b, 0))],
        out_specs=BS((tq, 512), lambda b, i: (b * nq + i, 0)),
        out_shape=jax.ShapeDtypeStruct((T, 512), F32), compiler_params=_arb(2))(P, MKV)


def mem_attn_bwd(P, MKV, dO, B, S, M, tq=512):
    T = B * S
    tq = min(tq, S)
    nq = S // tq

    def body(q_ref, kv_ref, do_ref, dq_ref, dkv_ref):
        @pl.when(pl.program_id(1) == 0)
        def _():
            dkv_ref[...] = jnp.zeros_like(dkv_ref)

        for h in range(N_HEADS):
            sl = slice(h * 128, (h + 1) * 128)
            sv = slice(512 + h * 128, 512 + (h + 1) * 128)
            qh = q_ref[:, sl].astype(BF16)
            kh = kv_ref[:, sl]
            do = do_ref[:, sl].astype(BF16)
            p = _mem_probs(qh, kh)
            dkv_ref[:, sv] += _dot(p.astype(BF16), do, TN)
            dp = _dot(do, kv_ref[:, sv], NT)
            ds = (p * (dp - jnp.sum(dp * p, axis=1, keepdims=True)) * MEM_SCALE).astype(BF16)
            dq_ref[:, sl] = _dot(ds, kh, NN).astype(BF16)
            dkv_ref[:, sl] += _dot(ds, qh, TN)

    return pl.pallas_call(
        body, name="mem_attn_bwd", grid=(B, nq),
        in_specs=[BS((tq, 512), lambda b, i: (b * nq + i, OFF_MEMQ // 512)), BS((M, 1024), lambda b, i: (b, 0)),
                  BS((tq, 512), lambda b, i: (b * nq + i, 0))],
        out_specs=[BS((tq, 512), lambda b, i: (b * nq + i, 0)), BS((M, 1024), lambda b, i: (b, 0))],
        out_shape=[jax.ShapeDtypeStruct((T, 512), BF16), jax.ShapeDtypeStruct((B * M, 1024), F32)],
        compiler_params=_arb(2))(P, MKV, dO)


def gain_grad(x, dy, name, tm=256):
    T, n = x.shape
    tm = min(tm, T)

    def body(x_ref, dy_ref, o_ref):
        @pl.when(pl.program_id(0) == 0)
        def _():
            o_ref[...] = jnp.zeros_like(o_ref)

        xv = x_ref[...]
        xh = xv * lax.rsqrt(jnp.mean(xv * xv, axis=-1, keepdims=True) + EPS)
        o_ref[...] += jnp.sum(dy_ref[...] * xh, axis=0, keepdims=True)

    return pl.pallas_call(
        body, name=name, grid=(T // tm,),
        in_specs=[BS((tm, n), lambda i: (i, 0))] * 2, out_specs=BS((1, n), lambda i: (0, 0)),
        out_shape=jax.ShapeDtypeStruct((1, n), F32), compiler_params=_arb(1))(x, dy)


def _conv_silu(x, w, t):
    y = x * w[3:4, :]
    for s in range(1, GDN_CONV):
        y = y + jnp.where(t >= s, pltpu.roll(x, s, 0), 0.0) * w[3 - s:4 - s, :]
    return y, _sigmoid(y)


def gdn_prep_fwd(P, conv_w, B, S):
    T = B * S

    def body(x_ref, w_ref, o_ref):
        kind = pl.program_id(1)
        t = lax.broadcasted_iota(jnp.int32, (S, 1), 0)
        y, sg = _conv_silu(x_ref[...], w_ref[...], t)
        a = y * sg
        scale = jnp.where(kind == 0, GDN_SCALE, 1.0).astype(F32)
        for h in range(N_HEADS):
            sl = slice(h * 128, (h + 1) * 128)
            seg = a[:, sl]
            n = lax.rsqrt(jnp.sum(seg * seg, axis=-1, keepdims=True) + EPS)
            o_ref[:, sl] = jnp.where(kind < 2, seg * (n * scale), seg)

    return pl.pallas_call(
        body, name="gdn_prep_fwd", grid=(B, 3),
        in_specs=[BS((S, 512), lambda b, k: (b, OFF_GDN // 512 + k)), BS((GDN_CONV, 512), lambda b, k: (0, k))],
        out_specs=BS((S, 512), lambda b, k: (b, k)),
        out_shape=jax.ShapeDtypeStruct((T, GDN_QKV), F32), compiler_params=_arb(2))(P, conv_w)


def gdn_prep_bwd(P, dqkv, conv_w, B, S):
    T = B * S

    def body(x_ref, d_ref, w_ref, o_ref, gw_ref):
        kind = pl.program_id(0)

        @pl.when(pl.program_id(1) == 0)
        def _():
            gw_ref[...] = jnp.zeros_like(gw_ref)

        t = lax.broadcasted_iota(jnp.int32, (S, 1), 0)
        x = x_ref[...]
        w = w_ref[...]
        y, sg = _conv_silu(x, w, t)
        a = y * sg
        scale = jnp.where(kind == 0, GDN_SCALE, 1.0).astype(F32)
        das = []
        for h in range(N_HEADS):
            sl = slice(h * 128, (h + 1) * 128)
            seg, dseg = a[:, sl], d_ref[:, sl]
            n = lax.rsqrt(jnp.sum(seg * seg, axis=-1, keepdims=True) + EPS)
            dn = scale * (n * dseg - seg * (n * n * n) * jnp.sum(dseg * seg, axis=-1, keepdims=True))
            das.append(jnp.where(kind < 2, dn, dseg))
        dy = jnp.concatenate(das, axis=1) * (sg * (1.0 + y * (1.0 - sg)))
        dx = dy * w[3:4, :]
        gw_ref[3:4, :] += jnp.sum(dy * x, axis=0, keepdims=True)
        for s in range(1, GDN_CONV):
            dx = dx + jnp.where(t + s < S, pltpu.roll(dy, S - s, 0), 0.0) * w[3 - s:4 - s, :]
            gw_ref[3 - s:4 - s, :] += jnp.sum(dy * jnp.where(t >= s, pltpu.roll(x, s, 0), 0.0), axis=0, keepdims=True)
        o_ref[...] = dx.astype(BF16)

    return pl.pallas_call(
        body, name="gdn_prep_bwd", grid=(3, B),
        in_specs=[BS((S, 512), lambda k, b: (b, OFF_GDN // 512 + k)), BS((S, 512), lambda k, b: (b, k)),
                  BS((GDN_CONV, 512), lambda k, b: (0, k))],
        out_specs=[BS((S, 512), lambda k, b: (b, k)), BS((GDN_CONV, 512), lambda k, b: (0, k))],
        out_shape=[jax.ShapeDtypeStruct((T, GDN_QKV), BF16), jax.ShapeDtypeStruct((GDN_CONV, GDN_QKV), F32)],
        compiler_params=_arb(2))(P, dqkv, conv_w)


def _chunk_row(n_rows):
    return lax.broadcasted_iota(jnp.int32, (n_rows, 1), 0) % CHUNK


def gdn_gate_fwd(P, alog_row, dt_row, B, S):
    T = B * S

    def body(x_ref, al_ref, dt_ref, o_ref):
        x = x_ref[...]
        lane = lax.broadcasted_iota(jnp.int32, (1, 128), 1)
        g = jnp.where(lane < 4, -jnp.exp(al_ref[...]) * _softplus(x + dt_ref[...]), 0.0)
        t = _chunk_row(S)
        for s in (1, 2, 4, 8, 16, 32):
            g = g + jnp.where(t >= s, pltpu.roll(g, s, 0), 0.0)
        o_ref[...] = jnp.where(lane < 4, g, jnp.where(lane < 8, _sigmoid(x), 0.0))

    row = BS((1, 128), lambda b: (0, 0))
    return pl.pallas_call(
        body, name="gdn_gate_fwd", grid=(B,),
        in_specs=[BS((S, 128), lambda b: (b, 768 // 128)), row, row], out_specs=BS((S, 128), lambda b: (b, 0)),
        out_shape=jax.ShapeDtypeStruct((T, 128), F32), compiler_params=_arb(1))(P, alog_row, dt_row)


def gdn_gate_bwd(P, dGB, alog_row, dt_row, B, S):
    T = B * S

    def body(x_ref, d_ref, al_ref, dt_ref, o_ref, acc_ref):
        @pl.when(pl.program_id(0) == 0)
        def _():
            acc_ref[...] = jnp.zeros_like(acc_ref)

        x, d = x_ref[...], d_ref[...]
        lane = lax.broadcasted_iota(jnp.int32, (1, 128), 1)
        z = x + dt_ref[...]
        coef = -jnp.exp(al_ref[...])
        g = coef * _softplus(z)
        da = jnp.where(lane < 4, d * coef * _sigmoid(z), 0.0)
        beta = _sigmoid(x)
        o_ref[...] = jnp.where(lane < 4, da, jnp.where(lane < 8, d * beta * (1.0 - beta), 0.0)).astype(BF16)
        acc_ref[0:1, :] += jnp.sum(jnp.where(lane < 4, d * g, 0.0), axis=0, keepdims=True)
        acc_ref[1:2, :] += jnp.sum(da, axis=0, keepdims=True)

    row = BS((1, 128), lambda b: (0, 0))
    return pl.pallas_call(
        body, name="gdn_gate_bwd", grid=(B,),
        in_specs=[BS((S, 128), lambda b: (b, 768 // 128)), BS((S, 128), lambda b: (b, 0)), row, row],
        out_specs=[BS((S, 128), lambda b: (b, 0)), BS((8, 128), lambda b: (0, 0))],
        out_shape=[jax.ShapeDtypeStruct((T, 128), BF16), jax.ShapeDtypeStruct((8, 128), F32)],
        compiler_params=_arb(1))(P, dGB, alog_row, dt_row)


def _chunk_masks(nc):
    r = lax.broadcasted_iota(jnp.int32, (nc, CHUNK, CHUNK), 1)
    c = lax.broadcasted_iota(jnp.int32, (nc, CHUNK, CHUNK), 2)
    return r >= c, r > c


def _chunk_local(q, k, gc, gr, beta, incl, strict):
    decay = jnp.exp(jnp.where(incl, gc - gr, NEG))
    kb = k * beta
    kbf = k.astype(BF16)
    m_kk = _bdot("gcd,gjd->gcj", kb.astype(BF16), kbf)
    l_mat = jnp.where(strict, m_kk * decay, 0.0)
    a_mat = _bdot("gcd,gjd->gcj", q.astype(BF16), kbf) * decay
    return decay, kb, l_mat, a_mat


def _split_bf16(x):
    hi = x.astype(BF16)
    return hi, (x - hi.astype(F32)).astype(BF16)


def _mm_split(ah, al, bh, bl):
    spec = "gij,gjk->gik"
    return _bdot(spec, ah, bh) + (_bdot(spec, ah, bl) + _bdot(spec, al, bh))


def gdn_chunk_fwd(qkv, GB, Grow, B, S, nc=8):
    T = B * S
    N = S // CHUNK
    nc = min(nc, N)
    nb = N // nc
    R = nc * CHUNK

    def body(q_ref, k_ref, v_ref, gb_ref, gr_ref, u_ref, w_ref, t_ref, a_ref):
        incl, strict = _chunk_masks(nc)
        eye = (lax.broadcasted_iota(jnp.int32, (nc, CHUNK, CHUNK), 1)
               == lax.broadcasted_iota(jnp.int32, (nc, CHUNK, CHUNK), 2)).astype(F32)
        for h in range(N_HEADS):
            sl = slice(h * 128, (h + 1) * 128)
            q = q_ref[:, sl].reshape(nc, CHUNK, 128)
            k = k_ref[:, sl].reshape(nc, CHUNK, 128)
            v = v_ref[:, sl].reshape(nc, CHUNK, 128)
            gc = gb_ref[:, h:h + 1].reshape(nc, CHUNK, 1)
            beta = gb_ref[:, 4 + h:5 + h].reshape(nc, CHUNK, 1)
            gr = gr_ref[h][:, None, :]
            _, kb, l_mat, a_mat = _chunk_local(q, k, gc, gr, beta, incl, strict)
            pw = -l_mat
            tinv = eye + pw
            for _ in range(5):
                ph, pl_ = _split_bf16(pw)
                pw = _mm_split(ph, pl_, ph, pl_)
                ph, pl_ = _split_bf16(pw)
                th, tl = _split_bf16(tinv)
                tinv = tinv + _mm_split(th, tl, ph, pl_)
            tb = tinv.astype(BF16)
            u = _bdot("gcj,gjv->gcv", tb, (v * beta).astype(BF16))
            w = _bdot("gcj,gjk->gck", tb, (kb * jnp.exp(gc)).astype(BF16))
            u_ref[:, sl] = u.reshape(R, 128)
            w_ref[:, sl] = w.reshape(R, 128)
            t_ref[h] = tinv
            a_ref[h] = a_mat

    rowb = lambda c, j: BS((R, c), lambda b, n: (b * nb + n, j))
    mat = BS((None, N_HEADS, nc, CHUNK, CHUNK), lambda b, n: (b, 0, n, 0, 0))
    return pl.pallas_call(
        body, name="gdn_chunk_fwd", grid=(B, nb),
        in_specs=[rowb(512, 0), rowb(512, 1), rowb(512, 2), rowb(128, 0),
                  BS((None, N_HEADS, nc, CHUNK), lambda b, n: (b, 0, n, 0))],
        out_specs=[rowb(512, 0), rowb(512, 0), mat, mat],
        out_shape=[jax.ShapeDtypeStruct((T, 512), F32), jax.ShapeDtypeStruct((T, 512), F32),
                   jax.ShapeDtypeStruct((B, N_HEADS, N, CHUNK, CHUNK), F32),
                   jax.ShapeDtypeStruct((B, N_HEADS, N, CHUNK, CHUNK), F32)],
        compiler_params=_arb(2))(qkv, qkv, qkv, GB, Grow)


def gdn_scan_fwd(qkv3, U3, W3, GB3, A, B, S):
    N = S // CHUNK

    def body(q_ref, k_ref, u_ref, w_ref, gb_ref, a_ref, o_ref, vn_ref, st_ref, s_s):
        @pl.when(pl.program_id(0) == 0)
        def _():
            s_s[...] = jnp.zeros_like(s_s)

        for b in range(B):
            for h in range(N_HEADS):
                sl = slice(h * 128, (h + 1) * 128)
                st = s_s[b, h]
                st_ref[b, h] = st
                stb = st.astype(BF16)
                g = gb_ref[b, :, h:h + 1]
                gl = g[CHUNK - 1:CHUNK, :]
                vn = u_ref[b, :, sl] - _dot(w_ref[b, :, sl].astype(BF16), stb, NN)
                vnb = vn.astype(BF16)
                o = _dot((q_ref[b, :, sl] * jnp.exp(g)).astype(BF16), stb, NN) + _dot(a_ref[b, h].astype(BF16), vnb, NN)
                vn_ref[b, :, sl] = vn
                o_ref[b, :, sl] = o
                s_s[b, h] = st * jnp.exp(gl) + _dot((k_ref[b, :, sl] * jnp.exp(gl - g)).astype(BF16), vnb, TN)

    tok = lambda c, j: BS((B, CHUNK, c), lambda n: (0, n, j))
    return pl.pallas_call(
        body, name="gdn_scan_fwd", grid=(N,),
        in_specs=[tok(512, 0), tok(512, 1), tok(512, 0), tok(512, 0), tok(128, 0),
                  BS((B, N_HEADS, None, CHUNK, CHUNK), lambda n: (0, 0, n, 0, 0))],
        out_specs=[tok(512, 0), tok(512, 0), BS((B, N_HEADS, None, 128, 128), lambda n: (0, 0, n, 0, 0))],
        out_shape=[jax.ShapeDtypeStruct((B, S, 512), F32), jax.ShapeDtypeStruct((B, S, 512), F32),
                   jax.ShapeDtypeStruct((B, N_HEADS, N, 128, 128), F32)],
        scratch_shapes=[pltpu.VMEM((B, N_HEADS, 128, 128), F32)],
        compiler_params=_arb(1))(qkv3, qkv3, U3, W3, GB3, A)


def gdn_scan_bwd(dO3, qkv3, W3, Vn3, GB3, A, St, B, S):
    N = S // CHUNK

    def body(do_ref, q_ref, k_ref, w_ref, vn_ref, gb_ref, a_ref, st_ref,
             du_ref, dw_ref, dq_ref, dk_ref, da_ref, dg_ref, ds_s):
        @pl.when(pl.program_id(0) == 0)
        def _():
            ds_s[...] = jnp.zeros_like(ds_s)

        lane = lax.broadcasted_iota(jnp.int32, (1, 128), 1)
        last = lax.broadcasted_iota(jnp.int32, (CHUNK, 1), 0) == CHUNK - 1
        for b in range(B):
            dg_all = jnp.zeros((CHUNK, 128), F32)
            for h in range(N_HEADS):
                sl = slice(h * 128, (h + 1) * 128)
                st = st_ref[b, h]
                stb = st.astype(BF16)
                dsn = ds_s[b, h]
                dsnb = dsn.astype(BF16)
                g = gb_ref[b, :, h:h + 1]
                gl = g[CHUNK - 1:CHUNK, :]
                egl = jnp.exp(gl)
                ekd = jnp.exp(gl - g)
                eg = jnp.exp(g)
                q, k = q_ref[b, :, sl], k_ref[b, :, sl]
                kd = k * ekd
                qg = q * eg
                do = do_ref[b, :, sl].astype(BF16)
                vnb = vn_ref[b, :, sl].astype(BF16)
                dvn = _dot(a_ref[b, h].astype(BF16), do, TN) + _dot(kd.astype(BF16), dsnb, NN)
                dvnb = dvn.astype(BF16)
                da_ref[b, h] = _dot(do, vnb, NT)
                dqg = _dot(do, stb, NT)
                dkd = _dot(vnb, dsnb, NT)
                ds_s[b, h] = (_dot(qg.astype(BF16), do, TN) + egl * dsn - _dot(w_ref[b, :, sl].astype(BF16), dvnb, TN))
                du_ref[b, :, sl] = dvn
                dw_ref[b, :, sl] = -_dot(dvnb, stb, NT)
                dq_ref[b, :, sl] = dqg * eg
                dk_ref[b, :, sl] = dkd * ekd
                ddel = jnp.sum(dkd * kd, axis=1, keepdims=True)
                dgl = jnp.sum(ddel, axis=0, keepdims=True) + jnp.sum(jnp.sum(st * dsn, axis=1, keepdims=True), axis=0, keepdims=True) * egl
                col = jnp.sum(dqg * qg, axis=1, keepdims=True) - ddel + jnp.where(last, dgl, 0.0)
                dg_all = jnp.where(lane == h, col, dg_all)
            dg_ref[b] = dg_all

    tok = lambda c, j: BS((B, CHUNK, c), lambda n: (0, N - 1 - n, j))
    mat = lambda d: BS((B, N_HEADS, None, d, d), lambda n: (0, 0, N - 1 - n, 0, 0))
    return pl.pallas_call(
        body, name="gdn_scan_bwd", grid=(N,),
        in_specs=[tok(512, 0), tok(512, 0), tok(512, 1), tok(512, 0), tok(512, 0), tok(128, 0), mat(CHUNK), mat(128)],
        out_specs=[tok(512, 0), tok(512, 0), tok(512, 0), tok(512, 0), mat(CHUNK), tok(128, 0)],
        out_shape=[jax.ShapeDtypeStruct((B, S, 512), F32)] * 4
        + [jax.ShapeDtypeStruct((B, N_HEADS, N, CHUNK, CHUNK), F32), jax.ShapeDtypeStruct((B, S, 128), F32)],
        scratch_shapes=[pltpu.VMEM((B, N_HEADS, 128, 128), F32)],
        compiler_params=_arb(1))(dO3, qkv3, qkv3, W3, Vn3, GB3, A, St)


def gdn_chunk_bwd(qkv, GB, Grow, Tinv, dA, dU, dW, dQ1, dK1, dG1, B, S, nc=8):
    T = B * S
    N = S // CHUNK
    nc = min(nc, N)
    nb = N // nc
    R = nc * CHUNK

    def body(q_ref, k_ref, v_ref, gb_ref, gr_ref, t_ref, da_ref, du_ref, dw_ref, dq1_ref, dk1_ref, dg1_ref, o_ref, dgb_ref):
        incl, strict = _chunk_masks(nc)
        lane = lax.broadcasted_iota(jnp.int32, (1, 128), 1)
        dg_all = dg1_ref[...]
        db_all = jnp.zeros((R, 128), F32)
        for h in range(N_HEADS):
            sl = slice(h * 128, (h + 1) * 128)
            q = q_ref[:, sl].reshape(nc, CHUNK, 128)
            k = k_ref[:, sl].reshape(nc, CHUNK, 128)
            v = v_ref[:, sl].reshape(nc, CHUNK, 128)
            gc = gb_ref[:, h:h + 1].reshape(nc, CHUNK, 1)
            beta = gb_ref[:, 4 + h:5 + h].reshape(nc, CHUNK, 1)
            gr = gr_ref[h][:, None, :]
            decay, kb, l_mat, a_mat = _chunk_local(q, k, gc, gr, beta, incl, strict)
            eg = jnp.exp(gc)
            kbg = kb * eg
            vb = v * beta
            tb = t_ref[h].astype(BF16)
            du = du_ref[:, sl].reshape(nc, CHUNK, 128).astype(BF16)
            dw = dw_ref[:, sl].reshape(nc, CHUNK, 128).astype(BF16)
            dvb = _bdot("gcj,gcv->gjv", tb, du)
            dkbg = _bdot("gcj,gck->gjk", tb, dw)
            dt = _bdot("gcv,gjv->gcj", du, vb.astype(BF16)) + _bdot("gck,gjk->gcj", dw, kbg.astype(BF16))
            tmp = _bdot("gac,gab->gcb", tb, dt.astype(BF16))
            dl = jnp.where(strict, -_bdot("gcb,gdb->gcd", tmp.astype(BF16), tb), 0.0)
            da = da_ref[h]
            dm = (dl * decay).astype(BF16)
            dqk = (da * decay).astype(BF16)
            kbf = k.astype(BF16)
            dkb = _bdot("gcj,gjd->gcd", dm, kbf) + dkbg * eg
            dk = (_bdot("gcj,gcd->gjd", dm, kb.astype(BF16)) + _bdot("gcj,gcd->gjd", dqk, q.astype(BF16))
                  + dk1_ref[:, sl].reshape(nc, CHUNK, 128) + dkb * beta)
            dq = _bdot("gcj,gjd->gcd", dqk, kbf) + dq1_ref[:, sl].reshape(nc, CHUNK, 128)
            e = dl * l_mat + da * a_mat
            dgc = (jnp.sum(e, axis=2, keepdims=True) - jnp.sum(jnp.swapaxes(e, 1, 2), axis=2, keepdims=True)
                   + jnp.sum(dkbg * kbg, axis=2, keepdims=True))
            dbeta = jnp.sum(dkb * k, axis=2, keepdims=True) + jnp.sum(dvb * v, axis=2, keepdims=True)
            o_ref[:, sl] = dq.reshape(R, 128)
            o_ref[:, 512 + h * 128:512 + (h + 1) * 128] = dk.reshape(R, 128)
            o_ref[:, 1024 + h * 128:1024 + (h + 1) * 128] = (dvb * beta).reshape(R, 128)
            dg_all = dg_all + jnp.where(lane == h, dgc.reshape(R, 1), 0.0)
            db_all = jnp.where(lane == 4 + h, dbeta.reshape(R, 1), db_all)
        t = _chunk_row(R)
        for s in (1, 2, 4, 8, 16, 32):
            dg_all = dg_all + jnp.where(t + s < CHUNK, pltpu.roll(dg_all, R - s, 0), 0.0)
        dgb_ref[...] = jnp.where(lane < 4, dg_all, db_all)

    rowb = lambda c, j: BS((R, c), lambda b, n: (b * nb + n, j))
    mat = BS((None, N_HEADS, nc, CHUNK, CHUNK), lambda b, n: (b, 0, n, 0, 0))
    return pl.pallas_call(
        body, name="gdn_chunk_bwd", grid=(B, nb),
        in_specs=[rowb(512, 0), rowb(512, 1), rowb(512, 2), rowb(128, 0),
                  BS((None, N_HEADS, nc, CHUNK), lambda b, n: (b, 0, n, 0)), mat, mat,
                  rowb(512, 0), rowb(512, 0), rowb(512, 0), rowb(512, 0), rowb(128, 0)],
        out_specs=[rowb(GDN_QKV, 0), rowb(128, 0)],
        out_shape=[jax.ShapeDtypeStruct((T, GDN_QKV), F32), jax.ShapeDtypeStruct((T, 128), F32)],
        compiler_params=_arb(2))(qkv, qkv, qkv, GB, Grow, Tinv, dA, dU, dW, dQ1, dK1, dG1)


def _gdn_out_norm(og, gg):
    outs, xhs, rs = [], [], []
    for h in range(N_HEADS):
        seg = og[:, h * 128:(h + 1) * 128]
        r = lax.rsqrt(jnp.mean(seg * seg, axis=-1, keepdims=True) + EPS)
        xh = seg * r
        outs.append(xh * gg)
        xhs.append(xh)
        rs.append(r)
    return outs, xhs, rs


def merge_fwd(o_mla, o_gdn, o_mem, P, x, tgt, w_out, g_gdn, g_fin, tm=256):
    T = x.shape[0]
    tm = min(tm, T)

    def body(om_ref, og_ref, oc_ref, gate_ref, x_ref, t_ref, w_ref, gg_ref, gf_ref, mix_ref, dx_ref, dxb_ref, sq_ref, gnf_ref):
        @pl.when(pl.program_id(0) == 0)
        def _():
            sq_ref[...] = jnp.zeros_like(sq_ref)
            gnf_ref[...] = jnp.zeros_like(gnf_ref)

        ogn, _, _ = _gdn_out_norm(og_ref[...], gg_ref[...])
        cat = jnp.concatenate([om_ref[...]] + ogn + [oc_ref[...]], axis=1)
        gt = gate_ref[...]
        mixed = (cat * (gt * _sigmoid(gt))).astype(BF16)
        mix_ref[...] = mixed
        x2 = x_ref[...] + _dot(mixed, w_ref[...], NN)
        r2 = lax.rsqrt(jnp.mean(x2 * x2, axis=-1, keepdims=True) + EPS)
        xh = x2 * r2
        gf = gf_ref[...]
        diff = xh * gf - t_ref[...]
        sq_ref[...] += jnp.sum(diff * diff, axis=0, keepdims=True)
        dy = diff * (1.0 / D_MODEL)
        gnf_ref[...] += jnp.sum(dy * xh, axis=0, keepdims=True)
        dxh = dy * gf
        dx = r2 * (dxh - xh * jnp.mean(dxh * xh, axis=-1, keepdims=True))
        dx_ref[...] = dx
        dxb_ref[...] = dx.astype(BF16)

    rowb = lambda c, j=0: BS((tm, c), lambda i: (i, j))
    full = lambda r, c: BS((r, c), lambda i: (0, 0))
    return pl.pallas_call(
        body, name="merge_fwd", grid=(T // tm,),
        in_specs=[rowb(512), rowb(512), rowb(512), rowb(D_MIX, OFF_GATE // D_MIX), rowb(D_MODEL), rowb(D_MODEL),
                  full(D_MIX, D_MODEL), full(1, 128), full(1, D_MODEL)],
        out_specs=[rowb(D_MIX), rowb(D_MODEL), rowb(D_MODEL), full(1, D_MODEL), full(1, D_MODEL)],
        out_shape=[jax.ShapeDtypeStruct((T, D_MIX), BF16), jax.ShapeDtypeStruct((T, D_MODEL), F32),
                   jax.ShapeDtypeStruct((T, D_MODEL), BF16),
                   jax.ShapeDtypeStruct((1, D_MODEL), F32), jax.ShapeDtypeStruct((1, D_MODEL), F32)],
        compiler_params=_arb(1))(o_mla, o_gdn, o_mem, P, x, tgt, w_out, g_gdn, g_fin)


def merge_bwd(dx2, o_mla, o_gdn, o_mem, P, w_out, g_gdn, tm=256):
    T = dx2.shape[0]
    tm = min(tm, T)

    def body(dx_ref, om_ref, og_ref, oc_ref, gate_ref, w_ref, gg_ref, dgate_ref, dom_ref, dog_ref, doc_ref, ggn_ref):
        @pl.when(pl.program_id(0) == 0)
        def _():
            ggn_ref[...] = jnp.zeros_like(ggn_ref)

        gg = gg_ref[...]
        dmix = _dot(dx_ref[...].astype(BF16), w_ref[...], NT)
        ogn, xhs, rs = _gdn_out_norm(og_ref[...], gg)
        cat = jnp.concatenate([om_ref[...]] + ogn + [oc_ref[...]], axis=1)
        gt = gate_ref[...]
        sg = _sigmoid(gt)
        dgate_ref[...] = (dmix * cat * (sg * (1.0 + gt * (1.0 - sg)))).astype(BF16)
        dcat = dmix * (gt * sg)
        dom_ref[...] = dcat[:, :512]
        doc_ref[...] = dcat[:, 1024:]
        acc = jnp.zeros((1, 128), F32)
        for h in range(N_HEADS):
            dseg = dcat[:, 512 + h * 128:512 + (h + 1) * 128]
            acc = acc + jnp.sum(dseg * xhs[h], axis=0, keepdims=True)
            dxh = dseg * gg
            dog_ref[:, h * 128:(h + 1) * 128] = rs[h] * (dxh - xhs[h] * jnp.mean(dxh * xhs[h], axis=-1, keepdims=True))
        ggn_ref[...] += acc

    rowb = lambda c, j=0: BS((tm, c), lambda i: (i, j))
    full = lambda r, c: BS((r, c), lambda i: (0, 0))
    return pl.pallas_call(
        body, name="merge_bwd", grid=(T // tm,),
        in_specs=[rowb(D_MODEL), rowb(512), rowb(512), rowb(512), rowb(D_MIX, OFF_GATE // D_MIX),
                  full(D_MIX, D_MODEL), full(1, 128)],
        out_specs=[rowb(D_MIX), rowb(512), rowb(512), rowb(512), full(1, 128)],
        out_shape=[jax.ShapeDtypeStruct((T, D_MIX), BF16)] + [jax.ShapeDtypeStruct((T, 512), F32)] * 3
        + [jax.ShapeDtypeStruct((1, 128), F32)],
        compiler_params=_arb(1))(dx2, o_mla, o_gdn, o_mem, P, w_out, g_gdn)


def in_norm_bwd(x, dh, dx2, gain, tm=256):
    T, n = x.shape
    tm = min(tm, T)

    def body(x_ref, dh_ref, dx2_ref, g_ref, o_ref, acc_ref):
        @pl.when(pl.program_id(0) == 0)
        def _():
            acc_ref[...] = jnp.zeros_like(acc_ref)

        xv = x_ref[...]
        r = lax.rsqrt(jnp.mean(xv * xv, axis=-1, keepdims=True) + EPS)
        xh = xv * r
        dy = dh_ref[...]
        acc_ref[...] += jnp.sum(dy * xh, axis=0, keepdims=True)
        dxh = dy * g_ref[...]
        o_ref[...] = dx2_ref[...] + r * (dxh - xh * jnp.mean(dxh * xh, axis=-1, keepdims=True))

    rowb = BS((tm, n), lambda i: (i, 0))
    full = BS((1, n), lambda i: (0, 0))
    return pl.pallas_call(
        body, name="in_norm_bwd", grid=(T // tm,),
        in_specs=[rowb, rowb, rowb, full], out_specs=[rowb, full],
        out_shape=[jax.ShapeDtypeStruct((T, n), F32), jax.ShapeDtypeStruct((1, n), F32)],
        compiler_params=_arb(1))(x, dh, dx2, gain)


W_IN_SHARD = D_IN // 4
_GDN0 = Q_LORA + KV_LORA + MLA_ROPE
_AB0 = _GDN0 + GDN_QKV
_MEMQ0 = _AB0 + 2 * N_HEADS
_GATE0 = _MEMQ0 + N_HEADS * MEM_DH


def _w_in_row_map():
    a, m, gt = _AB0 - 2 * W_IN_SHARD, _MEMQ0 - 2 * W_IN_SHARD, _GATE0 - 2 * W_IN_SHARD
    e0 = OFF_GDN + W_IN_SHARD - _GDN0
    e1 = e0 + W_IN_SHARD
    e2 = OFF_GATE + W_IN_SHARD - gt
    return [(0, 0, 0, 672), (0, 672, 704, 32), (2, a, 768, m - a), (2, m, OFF_MEMQ, gt - m), (0, _GDN0, OFF_GDN, W_IN_SHARD - _GDN0),
            (1, 0, e0, W_IN_SHARD), (2, 0, e1, a), (2, gt, OFF_GATE, W_IN_SHARD - gt), (3, 0, e2, W_IN_SHARD)]


_W_IN_ZERO_ROWS = [(672, 32), (736, 32), (776, 248)]
W_IN_LANES = 256


def pad_w_in_t(shards):
    def body(s_ref, o_ref):
        for r0, n in _W_IN_ZERO_ROWS:
            o_ref[r0:r0 + n, :] = jnp.zeros((n, W_IN_LANES), o_ref.dtype)
        for q, src, dst, n in _w_in_row_map():
            o_ref[dst:dst + n, :] = s_ref[q, src:src + n, :]

    return pl.pallas_call(
        body, name="pad_w_in_t", grid=(D_MODEL // W_IN_LANES,),
        in_specs=[BS((N_CHIPS, W_IN_SHARD, W_IN_LANES), lambda j: (0, 0, j))], out_specs=BS((N_PAD, W_IN_LANES), lambda j: (0, j)),
        out_shape=jax.ShapeDtypeStruct((N_PAD, D_MODEL), shards.dtype), compiler_params=_arb(1))(shards)


def unpad_w_in_t(g):
    def body(g_ref, o_ref):
        for q, src, dst, n in _w_in_row_map():
            o_ref[q, src:src + n, :] = g_ref[dst:dst + n, :]

    return pl.pallas_call(
        body, name="unpad_w_in_t", grid=(D_MODEL // W_IN_LANES,),
        in_specs=[BS((N_PAD, W_IN_LANES), lambda j: (0, j))], out_specs=BS((N_CHIPS, W_IN_SHARD, W_IN_LANES), lambda j: (0, 0, j)),
        out_shape=jax.ShapeDtypeStruct((N_CHIPS, W_IN_SHARD, D_MODEL), g.dtype), compiler_params=_arb(1))(g)


def _pad_w_q_b_t(s):
    z = jnp.zeros((32, s.shape[2]), s.dtype)
    parts = []
    for h in range(N_HEADS):
        parts += [s[h, :128], s[h, 128:160], z, s[h, 160:192], z]
    return jnp.concatenate(parts, axis=0)


def _unpad_w_q_b_t(g):
    return jnp.stack([jnp.concatenate([g[h * HEAD_PAD:h * HEAD_PAD + 128], g[h * HEAD_PAD + 128:h * HEAD_PAD + 160],
                                       g[h * HEAD_PAD + 192:h * HEAD_PAD + 224]]) for h in range(N_HEADS)])


def _perm_w_kv_b(s):
    return jnp.concatenate([s[h, :, :128] for h in range(N_HEADS)] + [s[h, :, 128:] for h in range(N_HEADS)], axis=1)


def _unperm_w_kv_b(g):
    return jnp.stack([jnp.concatenate([g[:, h * 128:(h + 1) * 128], g[:, 512 + h * 128:512 + (h + 1) * 128]], axis=1)
                      for h in range(N_HEADS)])


def _lane_row(v4):
    return jnp.pad(v4.reshape(1, -1).astype(F32), ((0, 0), (0, 128 - v4.size)))


def _pack(pieces, n_rows):
    flat = jnp.concatenate([p.reshape(-1) for p in pieces])
    return jnp.pad(flat, (0, n_rows * 1024 - flat.size)).reshape(n_rows, 1024)


def _unpack(block, shapes):
    flat = block.reshape(-1)
    out, off = [], 0
    for shp in shapes:
        n = int(np.prod(shp))
        out.append(flat[off:off + n].reshape(shp))
        off += n
    return out


N_CHIPS = 4
MESH = pl.DeviceIdType.MESH
ANY = BS(memory_space=pl.ANY)


def _place():
    return lax.axis_index("x"), lax.axis_index("y"), lax.axis_index("c")


def _other_chips(x, y):
    return [(1 - x, y), (x, 1 - y), (1 - x, 1 - y)]


def _half(split, which):
    axis, size = split
    ds = pl.ds(pl.multiple_of(which * size, 16 if axis == 0 else 128), size)
    return (ds, slice(None)) if axis == 0 else (slice(None), ds)


def allgather_chips(shards, splits, name):
    n = len(shards)

    def body(*refs):
        s_refs, o_refs = refs[:n], refs[n:2 * n]
        send_sems, recv_sems, local_sems = refs[2 * n:]
        x, y, c = _place()
        chips = _other_chips(x, y)

        def copy(k, src, dst, to):
            return pltpu.make_async_remote_copy(src_ref=src, dst_ref=dst, send_sem=send_sems.at[k], recv_sem=recv_sems.at[k],
                                                device_id=to, device_id_type=MESH)

        owns, first, passed = [], [], []
        for i, (s_ref, o_ref) in enumerate(zip(s_refs, o_refs)):
            mine = _half(splits[i], c)
            owns.append(pltpu.make_async_copy(s_ref, o_ref.at[2 * x + y], local_sems.at[i]))
            owns[-1].start()
            for j, (px, py) in enumerate(chips):
                first.append(copy(6 * i + j, s_ref.at[mine], o_ref.at[(2 * x + y,) + mine], (px, py, c)))
                first[-1].start()
        for i, (s_ref, o_ref) in enumerate(zip(s_refs, o_refs)):
            mine = _half(splits[i], c)
            for j, (px, py) in enumerate(chips):
                landed = o_ref.at[(2 * px + py,) + mine]
                copy(6 * i + j, s_ref.at[mine], landed, (px, py, c)).wait_recv()
                passed.append(copy(6 * i + 3 + j, landed, landed, (x, y, 1 - c)))
                passed[-1].start()
        for i, (s_ref, o_ref) in enumerate(zip(s_refs, o_refs)):
            theirs = _half(splits[i], 1 - c)
            for j, (px, py) in enumerate(chips):
                copy(6 * i + 3 + j, s_ref.at[theirs], o_ref.at[(2 * px + py,) + theirs], (x, y, 1 - c)).wait_recv()
        for cp in first + passed:
            cp.wait_send()
        for cp in owns:
            cp.wait()

    return pl.pallas_call(
        body, name=name, in_specs=[ANY] * n, out_specs=[ANY] * n,
        out_shape=[jax.ShapeDtypeStruct((N_CHIPS,) + s.shape, s.dtype) for s in shards],
        scratch_shapes=[pltpu.SemaphoreType.DMA((6 * n,)), pltpu.SemaphoreType.DMA((6 * n,)), pltpu.SemaphoreType.DMA((n,))])(*shards)


SEM = BS(memory_space=pltpu.SEMAPHORE)
HBM = BS(memory_space=pltpu.HBM)
_IN_HBM = lambda a: pltpu.with_memory_space_constraint(a, pltpu.HBM)
_SIDE_EFFECT = pltpu.SideEffectType.DATAFLOW_SIDE_EFFECTING


def _late_gather_copies(s_refs, l_refs, send_sems, recv_sems, local_sems, with_arrivals):
    x, y, c = _place()
    sends, recvs, locals_ = [], [], []
    for i, (s_ref, l_ref) in enumerate(zip(s_refs, l_refs)):
        locals_.append(pltpu.make_async_copy(s_ref, l_ref.at[2 * x + y], local_sems.at[i]))
        for j, (px, py) in enumerate(_other_chips(x, y)):
            k = 3 * i + j
            sends.append(pltpu.make_async_remote_copy(src_ref=s_ref, dst_ref=l_ref.at[2 * x + y], send_sem=send_sems.at[k],
                                                      recv_sem=recv_sems.at[k], device_id=(px, py, c), device_id_type=MESH))
            if with_arrivals:
                recvs.append(pltpu.make_async_remote_copy(src_ref=s_ref, dst_ref=l_ref.at[2 * px + py], send_sem=send_sems.at[k],
                                                          recv_sem=recv_sems.at[k], device_id=(px, py, c), device_id_type=MESH))
    return sends, recvs, locals_


def late_gather_start(shards, name):
    n = len(shards)

    def body(*refs):
        s_refs, l_refs = refs[:n], refs[n:2 * n]
        send_sems, recv_sems, local_sems = refs[2 * n:2 * n + 3]
        token = refs[-1]
        sends, _, locals_ = _late_gather_copies(s_refs, l_refs, send_sems, recv_sems, local_sems, False)
        for cp in locals_ + sends:
            cp.start()
        token[...] = jnp.zeros_like(token)

    lands = [lax.empty((N_CHIPS,) + s.shape, s.dtype) for s in shards]
    hbm_like = lambda a: pltpu.HBM(a.shape, a.dtype)
    out = pl.pallas_call(
        body, name=name,
        out_shape=[pltpu.SemaphoreType.DMA((3 * n,)), pltpu.SemaphoreType.DMA((3 * n,)), pltpu.SemaphoreType.DMA((n,))]
        + [hbm_like(s) for s in shards] + [hbm_like(l) for l in lands] + [jax.ShapeDtypeStruct((8, 128), F32)],
        in_specs=[HBM] * (2 * n), out_specs=[SEM] * 3 + [HBM] * (2 * n) + [BS(memory_space=pltpu.VMEM)],
        input_output_aliases={i: 3 + i for i in range(2 * n)},
        compiler_params=pltpu.CompilerParams(has_side_effects=_SIDE_EFFECT))(*[_IN_HBM(s) for s in shards], *[_IN_HBM(l) for l in lands])
    return out[:3], out[3:3 + n], out[3 + n:3 + 2 * n], out[-1]


def late_gather_wait(sems, shards, lands, after, name):
    n = len(shards)

    def body(*refs):
        s_refs, l_refs = refs[:n], refs[n:2 * n]
        send_sems, recv_sems, local_sems = refs[2 * n:2 * n + 3]
        sends, recvs, locals_ = _late_gather_copies(s_refs, l_refs, send_sems, recv_sems, local_sems, True)
        for cp in locals_:
            cp.wait()
        for cp in sends:
            cp.wait_send()
        for cp in recvs:
            cp.wait_recv()

    hbm_like = lambda a: pltpu.HBM(a.shape, a.dtype)
    out = pl.pallas_call(
        body, name=name, out_shape=[hbm_like(s) for s in shards] + [hbm_like(l) for l in lands],
        in_specs=[HBM] * (2 * n) + [SEM] * 3 + [BS(memory_space=pl.ANY)], out_specs=[HBM] * (2 * n),
        input_output_aliases={i: i for i in range(2 * n)},
        compiler_params=pltpu.CompilerParams(has_side_effects=_SIDE_EFFECT))(*shards, *lands, *sems, after)
    return out[n:]


def allgather_devices(block, name):
    R, C = block.shape

    def body(b_ref, o_ref, send_sems, recv_sems, local_sem):
        x, y, c = _place()
        me = 4 * x + 2 * y + c
        own = pltpu.make_async_copy(b_ref, o_ref.at[me], local_sem)
        own.start()
        copies = []
        for r in range(1, 8):
            px = 1 - x if r & 4 else x
            py = 1 - y if r & 2 else y
            pc = 1 - c if r & 1 else c
            send = pltpu.make_async_remote_copy(src_ref=b_ref, dst_ref=o_ref.at[me], send_sem=send_sems.at[r - 1],
                                                recv_sem=recv_sems.at[r - 1], device_id=(px, py, pc), device_id_type=MESH)
            recv = pltpu.make_async_remote_copy(src_ref=b_ref, dst_ref=o_ref.at[4 * px + 2 * py + pc], send_sem=send_sems.at[r - 1],
                                                recv_sem=recv_sems.at[r - 1], device_id=(px, py, pc), device_id_type=MESH)
            send.start()
            copies.append((send, recv))
        for send, recv in copies:
            recv.wait_recv()
            send.wait_send()
        own.wait()

    return pl.pallas_call(
        body, name=name, in_specs=[ANY], out_specs=ANY, out_shape=jax.ShapeDtypeStruct((8, R, C), block.dtype),
        scratch_shapes=[pltpu.SemaphoreType.DMA((7,)), pltpu.SemaphoreType.DMA((7,)), pltpu.SemaphoreType.DMA(())])(block)


def swap_sibling(arrs, name, splits=None):
    n = len(arrs)

    def sent(a_ref, i, c):
        return a_ref if splits is None else a_ref.at[(slice(None),) + _half(splits[i], 1 - c)]

    def out_shape(a, i):
        if splits is None:
            return a.shape
        axis, size = splits[i]
        return (a.shape[0], size, a.shape[2]) if axis == 0 else (a.shape[0], a.shape[1], size)

    def body(*refs):
        a_refs, o_refs = refs[:n], refs[n:2 * n]
        send_sems, recv_sems = refs[2 * n:]
        x, y, c = _place()
        copies = [pltpu.make_async_remote_copy(src_ref=sent(a_ref, i, c), dst_ref=o_ref, send_sem=send_sems.at[i],
                                               recv_sem=recv_sems.at[i], device_id=(x, y, 1 - c), device_id_type=MESH)
                  for i, (a_ref, o_ref) in enumerate(zip(a_refs, o_refs))]
        for cp in copies:
            cp.start()
        for cp in copies:
            cp.wait()

    return pl.pallas_call(
        body, name=name, in_specs=[ANY] * n, out_specs=[ANY] * n,
        out_shape=[jax.ShapeDtypeStruct(out_shape(a, i), a.dtype) for i, a in enumerate(arrs)],
        scratch_shapes=[pltpu.SemaphoreType.DMA((n,)), pltpu.SemaphoreType.DMA((n,))])(*arrs)


def exchange_chips(parts, name):
    n = len(parts)

    def body(*refs):
        p_refs, o_refs = refs[:n], refs[n:2 * n]
        send_sems, recv_sems = refs[2 * n:]
        x, y, c = _place()
        copies = [pltpu.make_async_remote_copy(src_ref=p_ref.at[2 * px + py], dst_ref=o_ref.at[j], send_sem=send_sems.at[3 * i + j],
                                               recv_sem=recv_sems.at[3 * i + j], device_id=(px, py, c), device_id_type=MESH)
                  for i, (p_ref, o_ref) in enumerate(zip(p_refs, o_refs)) for j, (px, py) in enumerate(_other_chips(x, y))]
        for cp in copies:
            cp.start()
        for cp in copies:
            cp.wait()

    return pl.pallas_call(
        body, name=name, in_specs=[ANY] * n, out_specs=[ANY] * n,
        out_shape=[jax.ShapeDtypeStruct((3,) + p.shape[1:], p.dtype) for p in parts],
        scratch_shapes=[pltpu.SemaphoreType.DMA((3 * n,)), pltpu.SemaphoreType.DMA((3 * n,))])(*parts)


def _half_block(shape2, split):
    axis, size = split
    return (size, shape2[1]) if axis == 0 else (shape2[0], size)


def add_pairs(parts, halves, splits, core, name):
    n = len(parts)

    def body(s_ref, *refs):
        for a_ref, b_ref, o_ref in zip(refs[:n], refs[n:2 * n], refs[2 * n:]):
            o_ref[...] = (a_ref[...].astype(F32) + b_ref[...].astype(F32)).astype(BF16)

    def mine(i):
        blk = (None,) + _half_block(parts[i].shape[1:], splits[i])
        if splits[i][0] == 0:
            return BS(blk, lambda q, s: (q, s[0], 0))
        return BS(blk, lambda q, s: (q, 0, s[0]))

    half_specs = [BS((None,) + h.shape[1:], lambda q, s: (q, 0, 0)) for h in halves]
    return pl.pallas_call(
        body, name=name,
        grid_spec=pltpu.PrefetchScalarGridSpec(num_scalar_prefetch=1, grid=(N_CHIPS,),
                                               in_specs=[mine(i) for i in range(n)] + half_specs, out_specs=half_specs),
        out_shape=[jax.ShapeDtypeStruct(h.shape, BF16) for h in halves], compiler_params=_arb(1))(core, *parts, *halves)


def add_fives(parts, halves, from_chips, splits, chip_core, name):
    n = len(parts)

    def body(s_ref, *refs):
        for a_ref, b_ref, p_ref, o_ref in zip(refs[:n], refs[n:2 * n], refs[2 * n:3 * n], refs[3 * n:]):
            s = a_ref[...].astype(F32) + b_ref[...].astype(F32)
            for j in range(3):
                s = s + p_ref[j].astype(F32)
            o_ref[...] = s

    def mine(i):
        blk = (None,) + _half_block(parts[i].shape[1:], splits[i])
        if splits[i][0] == 0:
            return BS(blk, lambda g, s: (s[0], s[1], 0))
        return BS(blk, lambda g, s: (s[0], 0, s[1]))

    half_specs = [BS((None,) + h.shape[1:], lambda g, s: (s[0], 0, 0)) for h in halves]
    chip_specs = [BS(p.shape, lambda g, s: (0, 0, 0)) for p in from_chips]
    out_specs = [BS(h.shape[1:], lambda g, s: (0, 0)) for h in halves]
    return pl.pallas_call(
        body, name=name,
        grid_spec=pltpu.PrefetchScalarGridSpec(num_scalar_prefetch=1, grid=(1,),
                                               in_specs=[mine(i) for i in range(n)] + half_specs + chip_specs, out_specs=out_specs),
        out_shape=[jax.ShapeDtypeStruct(h.shape[1:], F32) for h in halves], compiler_params=_arb(1))(chip_core, *parts, *halves, *from_chips)


def sum_leading(a, name):
    def body(a_ref, o_ref):
        s = a_ref[0]
        for j in range(1, a.shape[0]):
            s = s + a_ref[j]
        o_ref[...] = s

    return pl.pallas_call(body, name=name, out_shape=jax.ShapeDtypeStruct(a.shape[1:], a.dtype))(a)


def _adamw_math(w, g, m, v):
    mn = ADAM_B1 * m + (1.0 - ADAM_B1) * g
    vn = ADAM_B2 * v + (1.0 - ADAM_B2) * (g * g)
    m_hat = mn / (1.0 - ADAM_B1 ** ADAM_STEP)
    v_hat = vn / (1.0 - ADAM_B2 ** ADAM_STEP)
    return -ADAM_LR * (m_hat / (jnp.sqrt(v_hat) + ADAM_EPS) + ADAM_WD * w), mn, vn


def adamw(w, g, m, v, name):
    R, C = g.shape
    lead = (None,) * (w.ndim - 2)

    def body(w_ref, g_ref, m_ref, v_ref, d_ref, mo_ref, vo_ref):
        d_ref[...], mo_ref[...], vo_ref[...] = _adamw_math(w_ref[...], g_ref[...], m_ref[...], v_ref[...])

    wblk = BS(lead + (R, C), lambda i: (0,) * w.ndim)
    gblk = BS((R, C), lambda i: (0, 0))
    return pl.pallas_call(
        body, name=name, grid=(1,), in_specs=[wblk, gblk, wblk, wblk], out_specs=[wblk] * 3,
        out_shape=[jax.ShapeDtypeStruct(w.shape, F32)] * 3, compiler_params=_arb(1))(w, g, m, v)


def adamw_halves(w, mine, other, m, v, split, core, name):
    R, C = w.shape[-2:]
    axis, size = split
    lead = (None,) * (w.ndim - 2)
    zeros = (0,) * (w.ndim - 2)
    if axis == 0:
        tr = size if size <= 256 else next(t for t in range(256, 7, -1) if size % t == 0 and t % 8 == 0)
        nb = size // tr
        whole = BS(lead + (tr, C), lambda hi, j, s: zeros + (hi * nb + j, 0))
        part = BS((tr, C), lambda hi, j, s: (j, 0))
    else:
        nb = size // 128
        whole = BS(lead + (R, 128), lambda hi, j, s: zeros + (0, hi * nb + j))
        part = BS((R, 128), lambda hi, j, s: (0, j))

    def body(s_ref, w_ref, a_ref, b_ref, m_ref, v_ref, g_ref, d_ref, mo_ref, vo_ref):
        g = jnp.where(pl.program_id(0) == s_ref[0], a_ref[...], b_ref[...])
        g_ref[...] = g
        d_ref[...], mo_ref[...], vo_ref[...] = _adamw_math(w_ref[...], g, m_ref[...], v_ref[...])

    return pl.pallas_call(
        body, name=name,
        grid_spec=pltpu.PrefetchScalarGridSpec(num_scalar_prefetch=1, grid=(2, nb),
                                               in_specs=[whole, part, part, whole, whole], out_specs=[whole] * 4),
        out_shape=[jax.ShapeDtypeStruct(w.shape, F32)] * 4, compiler_params=_arb(2))(core, w, mine, other, m, v)


def dense_bf16(w3, name):
    R, _, K = w3.shape

    def body(w_hbm, o_ref, buf, sem):
        cp = pltpu.make_async_copy(w_hbm.at[:, 0], buf, sem)
        cp.start()
        cp.wait()
        o_ref[...] = buf[...].astype(BF16)

    return pl.pallas_call(
        body, name=name, in_specs=[ANY], out_specs=BS(memory_space=pltpu.VMEM), out_shape=jax.ShapeDtypeStruct((R, K), BF16),
        scratch_shapes=[pltpu.VMEM((R, K), F32), pltpu.SemaphoreType.DMA(())])(w3)


ROW_BLOCK = 184


def adamw_untiled_rows(w3, mine, other, m3, v3, name):
    R, _, K = w3.shape
    kh = K // 2
    starts = list(range(0, R, ROW_BLOCK))
    sizes = [min(ROW_BLOCK, R - s) for s in starts]
    nblk = len(starts)

    def body(w_hbm, a_ref, b_ref, m_hbm, v_hbm, g_hbm, d_hbm, mo_hbm, vo_hbm,
             wbuf, mbuf, vbuf, gbuf, dbuf, mobuf, vobuf, in_sems, out_sems):
        first = lax.axis_index("c") == 0
        ins = []
        for k, (r0, n) in enumerate(zip(starts, sizes)):
            rows = pl.ds(r0, n)
            cps = [pltpu.make_async_copy(src.at[rows, 0], dst.at[rows], in_sems.at[3 * k + i])
                   for i, (src, dst) in enumerate(((w_hbm, wbuf), (m_hbm, mbuf), (v_hbm, vbuf)))]
            for cp in cps:
                cp.start()
            ins.append(cps)

        def update(rows):
            a, b = a_ref[rows, :], b_ref[rows, :]
            g = jnp.concatenate([jnp.where(first, a, b), jnp.where(first, b, a)], axis=1)
            gbuf[rows, :] = g
            dbuf[rows, :], mobuf[rows, :], vobuf[rows, :] = _adamw_math(wbuf[rows, :], g, mbuf[rows, :], vbuf[rows, :])

        outs = []
        for k, (r0, n) in enumerate(zip(starts, sizes)):
            for cp in ins[k]:
                cp.wait()
            groups, tail = n // 8, n % 8

            def group(i, carry, r0=r0):
                update(pl.ds(pl.multiple_of(r0 + i * 8, 8), 8))
                return carry

            lax.fori_loop(0, groups, group, 0)
            if tail:
                update(pl.ds(r0 + groups * 8, tail))
            rows = pl.ds(r0, n)
            cps = [pltpu.make_async_copy(src.at[rows], dst.at[rows, 0], out_sems.at[4 * k + i])
                   for i, (src, dst) in enumerate(((gbuf, g_hbm), (dbuf, d_hbm), (mobuf, mo_hbm), (vobuf, vo_hbm)))]
            for cp in cps:
                cp.start()
            outs += cps
        for cp in outs:
            cp.wait()

    vmem = BS(memory_space=pltpu.VMEM)
    return pl.pallas_call(
        body, name=name, in_specs=[ANY, vmem, vmem, ANY, ANY], out_specs=[ANY] * 4,
        out_shape=[jax.ShapeDtypeStruct(w3.shape, F32)] * 4,
        scratch_shapes=[pltpu.VMEM((R, K), F32)] * 7 + [pltpu.SemaphoreType.DMA((3 * nblk,)), pltpu.SemaphoreType.DMA((4 * nblk,))])(
            w3, mine, other, m3, v3)


def adamw_w_q_b(w, mine, other, m, v, name):
    def body(w_ref, a_ref, b_ref, m_ref, v_ref, g_ref, d_ref, mo_ref, vo_ref):
        first = lax.axis_index("c") == 0
        lo = jnp.where(first, a_ref[...], b_ref[...])
        hi = jnp.where(first, b_ref[...], a_ref[...])
        g = jnp.concatenate([lo, hi[0:32], hi[64:96]], axis=0)
        g_ref[...] = g
        d_ref[...], mo_ref[...], vo_ref[...] = _adamw_math(w_ref[...], g, m_ref[...], v_ref[...])

    return pl.pallas_call(body, name=name, out_shape=[jax.ShapeDtypeStruct(w.shape, F32)] * 4)(w, mine, other, m, v)


def local_step(x, mem, positions, tgt, norm_in, w_in, late_weights, q_a_norm, kv_a_norm, gdn_conv, gdn_a_log,
               gdn_dt_bias, gdn_norm, mem_norm, norm_final):
    B, S, D = x.shape
    M = mem.shape[1]
    T = B * S
    N = S // CHUNK
    x2d = x.reshape(T, D)
    mem2d = mem.reshape(B * M, D)
    tgt2d = tgt.reshape(T, D)

    wp = w_in
    alog_row, dt_row = _lane_row(gdn_a_log), _lane_row(gdn_dt_bias)

    half = MLA_ROPE // 2
    inv_freq = 1.0 / (ROPE_THETA ** (jnp.arange(half, dtype=F32) / half))
    z32 = jnp.zeros((half,), F32)
    o32 = jnp.ones((half,), F32)
    inv_row = jnp.concatenate([inv_freq, z32, inv_freq, z32]).reshape(1, 128)
    sgn_row = jnp.concatenate([-o32, z32, o32, z32]).reshape(1, 128)
    msk_row = jnp.concatenate([o32, z32, o32, z32]).reshape(1, 128)
    cos_t, sin_t = rope_tables(positions.reshape(T, 1), inv_row, sgn_row, msk_row)

    h = rms_fwd(x2d, norm_in, "rms_in")
    P = mm(h, wp, "nt", F32, "in_proj", bm=512, bn=1536, n_outer=True)
    wq, wkv, w_mem_kv, w_out = late_weights(P)
    Q, K, V, qn, kvn = mla_prep(P, q_a_norm, kv_a_norm, wq, wkv, cos_t, sin_t)
    o_mla, lse = mla_attn_fwd(Q, K, V, B, S)
    memn = rms_fwd(mem2d, mem_norm, "rms_mem")
    MKV = mm(memn, w_mem_kv, "nn", BF16, "mem_kv_proj")
    o_mem = mem_attn_fwd(P, MKV, B, S, M)
    qkv = gdn_prep_fwd(P, gdn_conv, B, S)
    GB = gdn_gate_fwd(P, alog_row, dt_row, B, S)
    Grow = jnp.transpose(GB[:, :N_HEADS].reshape(B, N, CHUNK, N_HEADS), (0, 3, 1, 2))
    U, W, Tinv, A = gdn_chunk_fwd(qkv, GB, Grow, B, S)
    qkv3, GB3 = qkv.reshape(B, S, GDN_QKV), GB.reshape(B, S, 128)
    W3 = W.reshape(B, S, 512)
    o_gdn3, Vn3, St = gdn_scan_fwd(qkv3, U.reshape(B, S, 512), W3, GB3, A, B, S)
    o_gdn = o_gdn3.reshape(T, 512)
    mixed, dx2, dx2b, sq, g_norm_final = merge_fwd(o_mla, o_gdn, o_mem, P, x2d, tgt2d, w_out, gdn_norm, norm_final.reshape(1, D))

    g_w_out = mm(mixed, dx2b, "tn", BF16, "grad_w_out")
    dgate, do_mla, do_gdn, do_mem, g_gdn_norm = merge_bwd(dx2b, o_mla, o_gdn, o_mem, P, w_out, gdn_norm)

    dmemq, dMKV = mem_attn_bwd(P, MKV, do_mem, B, S, M)
    g_w_mem_kv = mm(memn, dMKV, "tn", BF16, "grad_w_mem_kv")
    dmemn = mm(dMKV, w_mem_kv, "nt", F32, "d_memn")
    g_mem_norm = gain_grad(mem2d, dmemn, "grad_mem_norm")

    dU3, dW3, dQ13, dK13, dA, dG13 = gdn_scan_bwd(do_gdn.reshape(B, S, 512), qkv3, W3, Vn3, GB3, A, St, B, S)
    r2 = lambda a: a.reshape(T, a.shape[-1])
    dqkv, dGB = gdn_chunk_bwd(qkv, GB, Grow, Tinv, dA, r2(dU3), r2(dW3), r2(dQ13), r2(dK13), r2(dG13), B, S)
    dPg, g_conv = gdn_prep_bwd(P, dqkv, gdn_conv, B, S)
    dab, g_ab = gdn_gate_bwd(P, dGB, alog_row, dt_row, B, S)

    dQ, dK, dV = mla_attn_bwd(Q, K, V, o_mla, do_mla, lse, B, S)
    dq_lin, dkv_lin, dkr = mla_post_bwd(dQ, dK, dV, cos_t, sin_t)
    dqn = mm(dq_lin, wq, "nn", F32, "d_qn")
    dkvn = mm(dkv_lin, wkv, "nt", F32, "d_kvn")
    g_wq = mm(dq_lin, qn, "tn", BF16, "grad_w_q_b")
    g_wkv = mm(kvn, dkv_lin, "tn", BF16, "grad_w_kv_b")
    dPm, g_q_a_norm, g_kv_a_norm = mla_norm_bwd(P, dqn, dkvn, dkr, dab, q_a_norm, kv_a_norm)

    dP = [dPm, dmemq, dPg, dgate]
    g_wp = mm_cols_tn(dP, h, BF16, "grad_w_in")
    dh = mm_cols_nn(dP, wp, F32, "d_h")
    grad_x, g_norm_in = in_norm_bwd(x2d, dh, dx2, norm_in)

    grads = dict(
        norm_in=g_norm_in, w_in=g_wp, q_a_norm=g_q_a_norm, w_q_b=g_wq, kv_a_norm=g_kv_a_norm, w_kv_b=g_wkv, gdn_conv=g_conv,
        gdn_a_log=g_ab[0:1, :N_HEADS], gdn_dt_bias=g_ab[1:2, :N_HEADS], gdn_norm=g_gdn_norm,
        mem_norm=g_mem_norm, w_mem_kv=g_w_mem_kv, w_out=g_w_out, norm_final=g_norm_final)
    return sq, grad_x.reshape(B, S, D), grads


def kernel(x, mem, positions, norm_in, w_in, q_a_norm, w_q_b, kv_a_norm, w_kv_b, gdn_conv, gdn_a_log, gdn_dt_bias, gdn_norm, mem_norm, w_mem_kv, w_out, norm_final, loss_target, m_norm_in, m_w_in, m_q_a_norm, m_w_q_b, m_kv_a_norm, m_w_kv_b, m_gdn_conv, m_gdn_a_log, m_gdn_dt_bias, m_gdn_norm, m_mem_norm, m_w_mem_kv, m_w_out, m_norm_final, v_norm_in, v_w_in, v_q_a_norm, v_w_q_b, v_kv_a_norm, v_w_kv_b, v_gdn_conv, v_gdn_a_log, v_gdn_dt_bias, v_gdn_norm, v_mem_norm, v_w_mem_kv, v_w_out, v_norm_final):
    B = x.shape[0]
    cx, cy, cc = lax.axis_index("x"), lax.axis_index("y"), lax.axis_index("c")
    chip = 2 * cx + cy

    big_names = ("w_in", "w_q_b", "w_kv_b", "w_mem_kv", "w_out")
    rows_major = lambda a: jnp.transpose(a, (2, 0, 1))
    w_in3, m_in3, v_in3 = rows_major(w_in), rows_major(m_w_in), rows_major(v_w_in)
    w_qb_t, m_qb_t, v_qb_t = jnp.transpose(w_q_b[0]), jnp.transpose(m_w_q_b[0]), jnp.transpose(v_w_q_b[0])
    z32 = jnp.zeros((32, Q_LORA), BF16)
    qb_bf = w_qb_t.astype(BF16)
    qb_padded = jnp.concatenate([qb_bf[:160], z32, qb_bf[160:], z32])
    shards = [dense_bf16(w_in3, "w_in_bf16"), qb_padded, w_kv_b[0].astype(BF16), w_mem_kv[0].astype(BF16), w_out[0].astype(BF16)]
    splits = [(1, D_MODEL // 2)] + [(0, s.shape[0] // 2) for s in shards[1:]]
    (g_in,) = allgather_chips(shards[:1], splits[:1], "allgather_w_in")
    conv_all = allgather_devices(gdn_conv[0], "allgather_conv")
    conv_cols = gdn_conv.shape[2]
    conv_full = jnp.transpose(conv_all[0::2], (1, 0, 2)).reshape(GDN_CONV, N_CHIPS * conv_cols)
    late_sems, late_shards, late_lands, _ = late_gather_start(shards[1:], "late_gather_start")
    late_shapes = [(N_CHIPS,) + s.shape for s in shards[1:]]

    def late_weights(after):
        g_qb, g_kvb, g_mem, g_out_w = late_gather_wait(late_sems, late_shards, late_lands, after, "late_gather_wait")
        return (g_qb.reshape(-1, Q_LORA), _perm_w_kv_b(g_kvb), g_mem.reshape(-1, g_mem.shape[2]),
                g_out_w.reshape(-1, g_out_w.shape[2]))

    sq, grad_x, g = local_step(x, mem, positions, loss_target, norm_in, pad_w_in_t(g_in), late_weights, q_a_norm, kv_a_norm,
                               conv_full, gdn_a_log, gdn_dt_bias, gdn_norm, mem_norm, norm_final)

    core = jnp.stack([cc]).astype(jnp.int32)
    chip_core = jnp.stack([chip, cc]).astype(jnp.int32)
    parts = [unpad_w_in_t(g["w_in"]), g["w_q_b"].reshape(late_shapes[0]), _unperm_w_kv_b(g["w_kv_b"]),
             g["w_mem_kv"].reshape(late_shapes[2]), g["w_out"].reshape(late_shapes[3])]
    from_sibling = swap_sibling(parts, "rs_sibling_partial", splits)
    chip_sums = add_pairs(parts, from_sibling, splits, core, "rs_add_sibling")
    from_chips = exchange_chips(chip_sums, "rs_exchange_chips")
    my_half = add_fives(parts, from_sibling, from_chips, splits, chip_core, "rs_add_chips")
    other_half = swap_sibling(my_half, "rs_sibling_final")

    small_names = ("norm_in", "q_a_norm", "kv_a_norm", "gdn_a_log", "gdn_dt_bias", "gdn_norm", "mem_norm", "norm_final")
    small = dict(norm_in=norm_in, q_a_norm=q_a_norm, kv_a_norm=kv_a_norm, gdn_a_log=gdn_a_log, gdn_dt_bias=gdn_dt_bias,
                 gdn_norm=gdn_norm, mem_norm=mem_norm, norm_final=norm_final)
    m_small = dict(norm_in=m_norm_in, q_a_norm=m_q_a_norm, kv_a_norm=m_kv_a_norm, gdn_a_log=m_gdn_a_log,
                   gdn_dt_bias=m_gdn_dt_bias, gdn_norm=m_gdn_norm, mem_norm=m_mem_norm, norm_final=m_norm_final)
    v_small = dict(norm_in=v_norm_in, q_a_norm=v_q_a_norm, kv_a_norm=v_kv_a_norm, gdn_a_log=v_gdn_a_log,
                   gdn_dt_bias=v_gdn_dt_bias, gdn_norm=v_gdn_norm, mem_norm=v_mem_norm, norm_final=v_norm_final)
    rows = lambda d: jnp.stack([jnp.pad(d[n].reshape(-1), (0, 1024 - d[n].size)) for n in small_names])
    conv_rows = GDN_CONV * GDN_QKV // 1024
    g_block = jnp.concatenate([rows(g), g["gdn_conv"].reshape(conv_rows, 1024), sq, jnp.zeros((16 - 9 - conv_rows, 1024), F32)])
    g_block = sum_leading(allgather_devices(g_block, "allgather_small_grads"), "sum_small_grads")
    g_small_rows = g_block[:8]
    loss = 0.5 * jnp.sum(g_block[8 + conv_rows]) / D_MODEL
    g_conv = lax.dynamic_slice_in_dim(g_block[8:8 + conv_rows].reshape(GDN_CONV, GDN_QKV), chip * conv_cols, conv_cols, axis=1)
    d_s, m_s, v_s = adamw(rows(small), g_small_rows, rows(m_small), rows(v_small), "adamw_small")
    unrow = lambda r: {n: r[i, :small[n].size].reshape(small[n].shape) for i, n in enumerate(small_names)}
    g_out, d_out, m_out, v_out = unrow(g_small_rows), unrow(d_s), unrow(m_s), unrow(v_s)

    d_out["gdn_conv"], m_out["gdn_conv"], v_out["gdn_conv"] = adamw(gdn_conv, g_conv, m_gdn_conv, v_gdn_conv, "adamw_gdn_conv")
    g_out["gdn_conv"] = g_conv[None]
    res = adamw_untiled_rows(w_in3, my_half[0], other_half[0], m_in3, v_in3, "adamw_w_in")
    g_out["w_in"], d_out["w_in"], m_out["w_in"], v_out["w_in"] = [jnp.transpose(r, (1, 2, 0)) for r in res]
    res = adamw_w_q_b(w_qb_t, my_half[1], other_half[1], m_qb_t, v_qb_t, "adamw_w_q_b")
    g_out["w_q_b"], d_out["w_q_b"], m_out["w_q_b"], v_out["w_q_b"] = [jnp.transpose(r)[None] for r in res]
    rest = dict(w_kv_b=(w_kv_b, m_w_kv_b, v_w_kv_b), w_mem_kv=(w_mem_kv, m_w_mem_kv, v_w_mem_kv), w_out=(w_out, m_w_out, v_w_out))
    for i, n in enumerate(big_names):
        if n in rest:
            w_n, m_n, v_n = rest[n]
            g_out[n], d_out[n], m_out[n], v_out[n] = adamw_halves(w_n, my_half[i], other_half[i], m_n, v_n, splits[i], core, "adamw_" + n)

    order = ("norm_in", "w_in", "q_a_norm", "w_q_b", "kv_a_norm", "w_kv_b", "gdn_conv", "gdn_a_log", "gdn_dt_bias",
             "gdn_norm", "mem_norm", "w_mem_kv", "w_out", "norm_final")
    return (loss, grad_x, *[g_out[n] for n in order], *[d_out[n] for n in order], *[m_out[n] for n in order],
            *[v_out[n] for n in order])
```

```python
import functools
import math

import jax
import jax.numpy as jnp
import numpy as np
from jax import lax
from jax.experimental import pallas as pl
from jax.experimental.pallas import tpu as pltpu

F32 = jnp.float32
BF16 = jnp.bfloat16
BS = pl.BlockSpec

D_MODEL = 1024
N_HEADS = 4
MLA_NOPE, MLA_ROPE, MLA_V = 128, 64, 128
Q_LORA, KV_LORA = 384, 256
ROPE_THETA = 10000.0
GDN_DK = GDN_DV = 128
GDN_CONV = 4
CHUNK = 64
MEM_DH = 128
D_MIX = 1536
GDN_QKV = 1536
D_IN = 4296
EPS = 1e-6
ADAM_LR, ADAM_B1, ADAM_B2, ADAM_EPS, ADAM_WD, ADAM_STEP = 0.001, 0.9, 0.999, 1e-08, 0.01, 10

OFF_MLA = 0
OFF_MEMQ = 1024
OFF_GDN = 1536
OFF_GATE = 3072
N_PAD = 4608
HEAD_PAD = 256
MLA_SCALE = (MLA_NOPE + MLA_ROPE) ** -0.5
MEM_SCALE = MEM_DH ** -0.5
GDN_SCALE = GDN_DK ** -0.5
NEG = -1e30

NN = ((1,), (0,))
NT = ((1,), (1,))
TN = ((0,), (0,))


def _dot(a, b, dims):
    return lax.dot_general(a, b, (dims, ((), ())), preferred_element_type=F32)


def _bdot(spec, a, b, precision=None):
    return jnp.einsum(spec, a, b, preferred_element_type=F32, precision=precision)


def _arb(n):
    return pltpu.CompilerParams(dimension_semantics=("arbitrary",) * n)


def _sigmoid(x):
    return 1.0 / (1.0 + jnp.exp(-x))


def _softplus(z):
    return jnp.maximum(z, 0.0) + jnp.log(1.0 + jnp.exp(-jnp.abs(z)))


def _rope(t, cos_row, sin_row):
    return t * cos_row + pltpu.roll(t, 64, 1) * sin_row


def _rope_bwd(d, cos_row, sin_row):
    return d * cos_row + pltpu.roll(d * sin_row, 64, 1)


def rms_fwd(x, gain, name, tm=512):
    T, n = x.shape
    tm = min(tm, T)

    def body(x_ref, g_ref, o_ref):
        xv = x_ref[...]
        r = lax.rsqrt(jnp.mean(xv * xv, axis=-1, keepdims=True) + EPS)
        o_ref[...] = (xv * r * g_ref[...]).astype(BF16)

    return pl.pallas_call(
        body, name=name, grid=(T // tm,),
        in_specs=[BS((tm, n), lambda i: (i, 0)), BS((1, n), lambda i: (0, 0))],
        out_specs=BS((tm, n), lambda i: (i, 0)),
        out_shape=jax.ShapeDtypeStruct((T, n), BF16), compiler_params=_arb(1))(x, gain)


def mm(a, b, kind, out_dtype, name, bm=512, bn=None, n_outer=False):
    if kind == "nn":
        (M, K), (_, N) = a.shape, b.shape
    elif kind == "nt":
        (M, K), (N, _) = a.shape, b.shape
    else:
        (K, M), (_, N) = a.shape, b.shape
    bm, bn = min(bm, M), min(bn or N, N)
    assert M % bm == 0 and N % bn == 0, (name, M, N, K)
    ij = (lambda g0, g1: (g1, g0)) if n_outer else (lambda g0, g1: (g0, g1))
    a_spec = BS((K, bm), lambda g0, g1: (0, ij(g0, g1)[0])) if kind == "tn" else BS((bm, K), lambda g0, g1: (ij(g0, g1)[0], 0))
    once = dict(pipeline_mode=pl.Buffered(1)) if bn == N else {}
    b_spec = (BS((bn, K), lambda g0, g1: (ij(g0, g1)[1], 0), **once) if kind == "nt"
              else BS((K, bn), lambda g0, g1: (0, ij(g0, g1)[1]), **once))
    dims = {"nn": NN, "nt": NT, "tn": TN}[kind]

    def body(a_ref, b_ref, o_ref):
        o_ref[...] = _dot(a_ref[...].astype(BF16), b_ref[...].astype(BF16), dims).astype(out_dtype)

    grid = (N // bn, M // bm) if n_outer else (M // bm, N // bn)
    return pl.pallas_call(
        body, name=name, grid=grid, in_specs=[a_spec, b_spec], out_specs=BS((bm, bn), lambda g0, g1: ij(g0, g1)),
        out_shape=jax.ShapeDtypeStruct((M, N), out_dtype), compiler_params=_arb(2))(a, b)


def mm_cols_nn(pieces, b, out_dtype, name, after, bm=512):
    M, N = pieces[0].shape[0], b.shape[1]
    bm = min(bm, M)
    n = len(pieces)
    widths = [p.shape[1] for p in pieces]
    offs = [sum(widths[:i]) for i in range(n)]

    def body(*refs):
        b_ref, o_ref = refs[n], refs[-1]
        acc = None
        for a_ref, off, w in zip(refs[:n], offs, widths):
            d = _dot(a_ref[...], b_ref[off:off + w, :], NN)
            acc = d if acc is None else acc + d
        o_ref[...] = acc.astype(out_dtype)

    return pl.pallas_call(
        body, name=name, grid=(M // bm,),
        in_specs=[BS((bm, w), lambda i: (i, 0)) for w in widths]
        + [BS(b.shape, lambda i: (0, 0), pipeline_mode=pl.Buffered(1)), BS(memory_space=pl.ANY)],
        out_specs=BS((bm, N), lambda i: (i, 0)), out_shape=jax.ShapeDtypeStruct((M, N), out_dtype),
        compiler_params=_arb(1))(*pieces, b, after)


def mm_cols_tn(pieces, b, out_dtype, name, bm=512):
    K, N = b.shape
    tiles = [p.shape[1] // bm for p in pieces]
    firsts = [sum(tiles[:i]) for i in range(len(tiles))]

    def body(*refs):
        b_ref, o_ref = refs[-2], refs[-1]
        i = pl.program_id(0)
        for a_ref, t0, n in zip(refs[:-2], firsts, tiles):
            @pl.when((i >= t0) & (i < t0 + n))
            def _(a_ref=a_ref):
                o_ref[...] = _dot(a_ref[...], b_ref[...], TN).astype(out_dtype)

    a_specs = [BS((K, bm), lambda i, t0=t0, n=n: (0, jnp.clip(i - t0, 0, n - 1))) for t0, n in zip(firsts, tiles)]
    return pl.pallas_call(
        body, name=name, grid=(sum(tiles),),
        in_specs=a_specs + [BS(b.shape, lambda i: (0, 0), pipeline_mode=pl.Buffered(1))],
        out_specs=BS((bm, N), lambda i: (i, 0)), out_shape=jax.ShapeDtypeStruct((sum(tiles) * bm, N), out_dtype),
        compiler_params=_arb(1))(*pieces, b)


def rope_tables(pos_col, inv_row, sgn_row, msk_row, tm=512):
    T = pos_col.shape[0]
    tm = min(tm, T)

    def body(p_ref, inv_ref, sgn_ref, msk_ref, c_ref, s_ref):
        ang = p_ref[...].astype(F32) * inv_ref[...]
        c_ref[...] = jnp.cos(ang) * msk_ref[...]
        s_ref[...] = jnp.sin(ang) * sgn_ref[...]

    row = BS((1, 128), lambda i: (0, 0))
    return pl.pallas_call(
        body, name="rope_tables", grid=(T // tm,),
        in_specs=[BS((tm, 1), lambda i: (i, 0)), row, row, row],
        out_specs=[BS((tm, 128), lambda i: (i, 0))] * 2,
        out_shape=[jax.ShapeDtypeStruct((T, 128), F32)] * 2, compiler_params=_arb(1))(pos_col, inv_row, sgn_row, msk_row)


def mla_prep(P, gq, gkv, wq, wkv, cos_t, sin_t, tm=512):
    T = P.shape[0]
    tm = min(tm, T)

    def body(p_ref, gq_ref, gkv_ref, wq_ref, wkv_ref, c_ref, s_ref, q_ref, k_ref, v_ref, qn_ref, kvn_ref):
        p = p_ref[...]
        cq, ckv, kr = p[:, :Q_LORA], p[:, Q_LORA:Q_LORA + KV_LORA], p[:, 640:768]
        qn = (cq * lax.rsqrt(jnp.mean(cq * cq, axis=-1, keepdims=True) + EPS) * gq_ref[...]).astype(BF16)
        kvn = (ckv * lax.rsqrt(jnp.mean(ckv * ckv, axis=-1, keepdims=True) + EPS) * gkv_ref[...]).astype(BF16)
        qn_ref[...] = qn
        kvn_ref[...] = kvn
        q = _dot(qn, wq_ref[...], NT)
        kv = _dot(kvn, wkv_ref[...], NN)
        cos_row, sin_row = c_ref[...], s_ref[...]
        krr = _rope(kr, cos_row, sin_row).astype(BF16)
        for h in range(N_HEADS):
            lo = h * HEAD_PAD
            q_ref[:, lo:lo + 128] = (q[:, lo:lo + 128] * MLA_SCALE).astype(BF16)
            q_ref[:, lo + 128:lo + 256] = (_rope(q[:, lo + 128:lo + 256], cos_row, sin_row) * MLA_SCALE).astype(BF16)
            k_ref[:, lo:lo + 128] = kv[:, h * 128:(h + 1) * 128].astype(BF16)
            k_ref[:, lo + 128:lo + 256] = krr
            v_ref[:, lo:lo + 128] = kv[:, 512 + h * 128:512 + (h + 1) * 128].astype(BF16)
            v_ref[:, lo + 128:lo + 256] = jnp.ones((tm, 128), BF16)

    full = lambda r, c: BS((r, c), lambda i: (0, 0))
    rowb = lambda c: BS((tm, c), lambda i: (i, 0))
    return pl.pallas_call(
        body, name="mla_prep", grid=(T // tm,),
        in_specs=[rowb(1024), full(1, Q_LORA), full(1, KV_LORA), full(1024, Q_LORA), full(KV_LORA, 1024), rowb(128), rowb(128)],
        out_specs=[rowb(1024), rowb(1024), rowb(1024), rowb(Q_LORA), rowb(KV_LORA)],
        out_shape=[jax.ShapeDtypeStruct((T, 1024), BF16), jax.ShapeDtypeStruct((T, 1024), BF16),
                   jax.ShapeDtypeStruct((T, 1024), BF16), jax.ShapeDtypeStruct((T, Q_LORA), BF16),
                   jax.ShapeDtypeStruct((T, KV_LORA), BF16)],
        compiler_params=_arb(1))(P, gq, gkv, wq, wkv, cos_t, sin_t)


ATTN_HEADS_PER_STEP = 2
ATTN_STRIP = 32


def mla_attn_fwd(Q, K, V, B, S, tq=512, hp=ATTN_HEADS_PER_STEP):
    T = B * S
    tq = min(tq, S)
    nq = S // tq

    rs = min(ATTN_STRIP, tq)

    def body(q_ref, k_ref, v_ref, o_ref, lse_ref, m_s, acc_s, s_s, p_s, a_s):
        i = pl.program_id(2)
        m_s[...] = jnp.full_like(m_s, NEG)
        acc_s[...] = jnp.zeros_like(acc_s)

        def blk(j, masked):
            rows = pl.ds(pl.multiple_of(j * tq, tq), tq)
            for h in range(hp):
                hq = slice(h * HEAD_PAD, (h + 1) * HEAD_PAD)
                s_s[h] = _dot(q_ref[:, hq], k_ref[rows, hq], NT)
            for h in range(hp):
                for r0 in range(0, tq, rs):
                    rr = slice(r0, r0 + rs)
                    sv = s_s[h, rr, :]
                    if masked:
                        r = r0 + lax.broadcasted_iota(jnp.int32, (rs, tq), 0)
                        c = lax.broadcasted_iota(jnp.int32, (rs, tq), 1)
                        sv = jnp.where(r >= c, sv, NEG)
                    m_prev = m_s[h, rr, :]
                    m_new = jnp.maximum(m_prev, jnp.max(sv, axis=1, keepdims=True))
                    p_s[h, rr, :] = jnp.exp(sv - m_new).astype(BF16)
                    a_s[h, rr, :] = jnp.exp(m_prev - m_new)
                    m_s[h, rr, :] = m_new
            for h in range(hp):
                hq = slice(h * HEAD_PAD, (h + 1) * HEAD_PAD)
                acc_s[h] = a_s[h] * acc_s[h] + _dot(p_s[h], v_ref[rows, hq], NN)

        def loop(j, c):
            blk(j, False)
            return c

        lax.fori_loop(0, i, loop, 0)
        blk(i, True)
        for h in range(hp):
            den = acc_s[h, :, 128:256]
            o_ref[:, h * 128:(h + 1) * 128] = acc_s[h, :, 0:128] / den
            lse_ref[h] = m_s[h] + jnp.log(den[:, 0:1])

    return pl.pallas_call(
        body, name="mla_attn_fwd", grid=(B, N_HEADS // hp, nq),
        in_specs=[BS((tq, hp * HEAD_PAD), lambda b, h, i: (b * nq + i, h)),
                  BS((S, hp * HEAD_PAD), lambda b, h, i: (b, h)),
                  BS((S, hp * HEAD_PAD), lambda b, h, i: (b, h))],
        out_specs=[BS((tq, hp * 128), lambda b, h, i: (b * nq + i, h)),
                   BS((hp, tq, 1), lambda b, h, i: (h, b * nq + i, 0))],
        out_shape=[jax.ShapeDtypeStruct((T, 512), F32), jax.ShapeDtypeStruct((N_HEADS, T, 1), F32)],
        scratch_shapes=[pltpu.VMEM((hp, tq, 1), F32), pltpu.VMEM((hp, tq, HEAD_PAD), F32), pltpu.VMEM((hp, tq, tq), F32),
                        pltpu.VMEM((hp, tq, tq), BF16), pltpu.VMEM((hp, tq, 1), F32)],
        compiler_params=_arb(3))(Q, K, V)


def mla_attn_bwd(Q, K, V, O, dO, LSE, B, S, tq=512, hp=ATTN_HEADS_PER_STEP):
    T = B * S
    tq = min(tq, S)
    nq = S // tq

    def body(q_ref, k_ref, v_ref, o_ref, do_ref, lse_ref, dq_ref, dk_ref, dv_ref, delta_s, dk_s, dv_s):
        j = pl.program_id(2)

        @pl.when(j == 0)
        def _():
            dq_ref[...] = jnp.zeros_like(dq_ref)
            for h in range(hp):
                sl = slice(h * 128, (h + 1) * 128)
                delta_s[h] = jnp.sum(do_ref[:, sl] * o_ref[:, sl], axis=1, keepdims=True)

        dk_s[...] = jnp.zeros_like(dk_s)
        dv_s[...] = jnp.zeros_like(dv_s)

        def step(i, c):
            rows = pl.ds(pl.multiple_of(i * tq, tq), tq)
            r = i * tq + lax.broadcasted_iota(jnp.int32, (tq, tq), 0)
            cc = j * tq + lax.broadcasted_iota(jnp.int32, (tq, tq), 1)
            causal = r >= cc
            for h in range(hp):
                sq, sv = slice(h * HEAD_PAD, (h + 1) * HEAD_PAD), slice(h * 128, (h + 1) * 128)
                q = q_ref[rows, sq]
                k = k_ref[:, sq]
                do = do_ref[rows, sv].astype(BF16)
                s = _dot(q, k, NT)
                p = jnp.where(causal, jnp.exp(s - lse_ref[h, rows, :]), 0.0)
                dv_s[:, sv] += _dot(p.astype(BF16), do, TN)
                dp = _dot(do, v_ref[:, h * HEAD_PAD:h * HEAD_PAD + 128], NT)
                ds = (p * (dp - delta_s[h, rows, :])).astype(BF16)
                dk_s[:, sq] += _dot(ds, q, TN)
                dq_ref[rows, sq] += _dot(ds, k, NN)
            return c

        lax.fori_loop(j, nq, step, 0)
        dk_ref[...] = dk_s[...]
        dv_ref[...] = dv_s[...]

    seq = lambda c: BS((S, c), lambda b, h, j: (b, h))
    blk = lambda c: BS((tq, c), lambda b, h, j: (b * nq + j, h))
    return pl.pallas_call(
        body, name="mla_attn_bwd", grid=(B, N_HEADS // hp, nq),
        in_specs=[seq(hp * HEAD_PAD), blk(hp * HEAD_PAD), blk(hp * HEAD_PAD), seq(hp * 128), seq(hp * 128),
                  BS((hp, S, 1), lambda b, h, j: (h, b, 0))],
        out_specs=[seq(hp * HEAD_PAD), blk(hp * HEAD_PAD), blk(hp * 128)],
        out_shape=[jax.ShapeDtypeStruct((T, 1024), F32), jax.ShapeDtypeStruct((T, 1024), F32),
                   jax.ShapeDtypeStruct((T, 512), F32)],
        scratch_shapes=[pltpu.VMEM((hp, S, 1), F32), pltpu.VMEM((tq, hp * HEAD_PAD), F32), pltpu.VMEM((tq, hp * 128), F32)],
        compiler_params=_arb(3))(Q, K, V, O, dO, LSE)


def mla_post_bwd(dQ, dK, dV, cos_t, sin_t, tm=512):
    T = dQ.shape[0]
    tm = min(tm, T)

    def body(dq_ref, dk_ref, dv_ref, c_ref, s_ref, ql_ref, kvl_ref, kr_ref):
        cos_row, sin_row = c_ref[...], s_ref[...]
        kr = jnp.zeros((tm, 128), F32)
        for h in range(N_HEADS):
            lo = h * HEAD_PAD
            ql_ref[:, lo:lo + 128] = (dq_ref[:, lo:lo + 128] * MLA_SCALE).astype(BF16)
            ql_ref[:, lo + 128:lo + 256] = (_rope_bwd(dq_ref[:, lo + 128:lo + 256], cos_row, sin_row) * MLA_SCALE).astype(BF16)
            kvl_ref[:, h * 128:(h + 1) * 128] = dk_ref[:, lo:lo + 128].astype(BF16)
            kr = kr + dk_ref[:, lo + 128:lo + 256]
        kvl_ref[:, 512:] = dv_ref[...].astype(BF16)
        kr_ref[...] = _rope_bwd(kr, cos_row, sin_row)

    rowb = lambda c: BS((tm, c), lambda i: (i, 0))
    return pl.pallas_call(
        body, name="mla_post_bwd", grid=(T // tm,),
        in_specs=[rowb(1024), rowb(1024), rowb(512), rowb(128), rowb(128)],
        out_specs=[rowb(1024), rowb(1024), rowb(128)],
        out_shape=[jax.ShapeDtypeStruct((T, 1024), BF16), jax.ShapeDtypeStruct((T, 1024), BF16),
                   jax.ShapeDtypeStruct((T, 128), F32)],
        compiler_params=_arb(1))(dQ, dK, dV, cos_t, sin_t)


def mla_norm_bwd(P, dqn, dkvn, dkr, dab, gq, gkv, tm=512):
    T = P.shape[0]
    tm = min(tm, T)

    def norm_bwd(x, dy, g):
        r = lax.rsqrt(jnp.mean(x * x, axis=-1, keepdims=True) + EPS)
        xh = x * r
        dxh = dy * g
        return r * (dxh - xh * jnp.mean(dxh * xh, axis=-1, keepdims=True)), jnp.sum(dy * xh, axis=0, keepdims=True)

    def body(p_ref, dqn_ref, dkvn_ref, dkr_ref, dab_ref, gq_ref, gkv_ref, o_ref, aq_ref, akv_ref):
        @pl.when(pl.program_id(0) == 0)
        def _():
            aq_ref[...] = jnp.zeros_like(aq_ref)
            akv_ref[...] = jnp.zeros_like(akv_ref)

        dcq, ggq = norm_bwd(p_ref[:, :Q_LORA], dqn_ref[...], gq_ref[...])
        dckv, ggkv = norm_bwd(p_ref[:, Q_LORA:640], dkvn_ref[...], gkv_ref[...])
        aq_ref[...] += ggq
        akv_ref[...] += ggkv
        o_ref[:, :Q_LORA] = dcq.astype(BF16)
        o_ref[:, Q_LORA:640] = dckv.astype(BF16)
        o_ref[:, 640:768] = dkr_ref[...].astype(BF16)
        o_ref[:, 768:896] = dab_ref[...]
        o_ref[:, 896:1024] = jnp.zeros((tm, 128), BF16)

    rowb = lambda c: BS((tm, c), lambda i: (i, 0))
    full = lambda c: BS((1, c), lambda i: (0, 0))
    return pl.pallas_call(
        body, name="mla_norm_bwd", grid=(T // tm,),
        in_specs=[rowb(1024), rowb(Q_LORA), rowb(KV_LORA), rowb(128), rowb(128), full(Q_LORA), full(KV_LORA)],
        out_specs=[rowb(1024), full(Q_LORA), full(KV_LORA)],
        out_shape=[jax.ShapeDtypeStruct((T, 1024), BF16), jax.ShapeDtypeStruct((1, Q_LORA), F32),
                   jax.ShapeDtypeStruct((1, KV_LORA), F32)],
        compiler_params=_arb(1))(P, dqn, dkvn, dkr, dab, gq, gkv)


def _mem_probs(qh, kh):
    s = _dot(qh, kh, NT) * MEM_SCALE
    p = jnp.exp(s - jnp.max(s, axis=1, keepdims=True))
    return p / jnp.sum(p, axis=1, keepdims=True)


def mem_attn_fwd(P, MKV, B, S, M, tq=512):
    T = B * S
    tq = min(tq, S)
    nq = S // tq

    def body(q_ref, kv_ref, o_ref):
        for h in range(N_HEADS):
            sl = slice(h * 128, (h + 1) * 128)
            p = _mem_probs(q_ref[:, sl].astype(BF16), kv_ref[:, sl])
            o_ref[:, sl] = _dot(p.astype(BF16), kv_ref[:, 512 + h * 128:512 + (h + 1) * 128], NN)

    return pl.pallas_call(
        body, name="mem_attn_fwd", grid=(B, nq),
        in_specs=[BS((tq, 512), lambda b, i: (b * nq + i, OFF_MEMQ // 512)), BS((M, 1024), lambda b, i: (b, 0))],
        out_specs=BS((tq, 512), lambda b, i: (b * nq + i, 0)),
        out_shape=jax.ShapeDtypeStruct((T, 512), F32), compiler_params=_arb(2))(P, MKV)


def mem_attn_bwd(P, MKV, dO, B, S, M, tq=512):
    T = B * S
    tq = min(tq, S)
    nq = S // tq

    def body(q_ref, kv_ref, do_ref, dq_ref, dkv_ref):
        @pl.when(pl.program_id(1) == 0)
        def _():
            dkv_ref[...] = jnp.zeros_like(dkv_ref)

        for h in range(N_HEADS):
            sl = slice(h * 128, (h + 1) * 128)
            sv = slice(512 + h * 128, 512 + (h + 1) * 128)
            qh = q_ref[:, sl].astype(BF16)
            kh = kv_ref[:, sl]
            do = do_ref[:, sl].astype(BF16)
            p = _mem_probs(qh, kh)
            dkv_ref[:, sv] += _dot(p.astype(BF16), do, TN)
            dp = _dot(do, kv_ref[:, sv], NT)
            ds = (p * (dp - jnp.sum(dp * p, axis=1, keepdims=True)) * MEM_SCALE).astype(BF16)
            dq_ref[:, sl] = _dot(ds, kh, NN).astype(BF16)
            dkv_ref[:, sl] += _dot(ds, qh, TN)

    return pl.pallas_call(
        body, name="mem_attn_bwd", grid=(B, nq),
        in_specs=[BS((tq, 512), lambda b, i: (b * nq + i, OFF_MEMQ // 512)), BS((M, 1024), lambda b, i: (b, 0)),
                  BS((tq, 512), lambda b, i: (b * nq + i, 0))],
        out_specs=[BS((tq, 512), lambda b, i: (b * nq + i, 0)), BS((M, 1024), lambda b, i: (b, 0))],
        out_shape=[jax.ShapeDtypeStruct((T, 512), BF16), jax.ShapeDtypeStruct((B * M, 1024), F32)],
        compiler_params=_arb(2))(P, MKV, dO)


def gain_grad(x, dy, name, tm=256):
    T, n = x.shape
    tm = min(tm, T)

    def body(x_ref, dy_ref, o_ref):
        @pl.when(pl.program_id(0) == 0)
        def _():
            o_ref[...] = jnp.zeros_like(o_ref)

        xv = x_ref[...]
        xh = xv * lax.rsqrt(jnp.mean(xv * xv, axis=-1, keepdims=True) + EPS)
        o_ref[...] += jnp.sum(dy_ref[...] * xh, axis=0, keepdims=True)

    return pl.pallas_call(
        body, name=name, grid=(T // tm,),
        in_specs=[BS((tm, n), lambda i: (i, 0))] * 2, out_specs=BS((1, n), lambda i: (0, 0)),
        out_shape=jax.ShapeDtypeStruct((1, n), F32), compiler_params=_arb(1))(x, dy)


def _conv_silu(x, w, t):
    y = x * w[3:4, :]
    for s in range(1, GDN_CONV):
        y = y + jnp.where(t >= s, pltpu.roll(x, s, 0), 0.0) * w[3 - s:4 - s, :]
    return y, _sigmoid(y)


def gdn_prep_fwd(P, conv_w, B, S):
    T = B * S

    def body(x_ref, w_ref, o_ref):
        kind = pl.program_id(1)
        t = lax.broadcasted_iota(jnp.int32, (S, 1), 0)
        y, sg = _conv_silu(x_ref[...], w_ref[...], t)
        a = y * sg
        scale = jnp.where(kind == 0, GDN_SCALE, 1.0).astype(F32)
        for h in range(N_HEADS):
            sl = slice(h * 128, (h + 1) * 128)
            seg = a[:, sl]
            n = lax.rsqrt(jnp.sum(seg * seg, axis=-1, keepdims=True) + EPS)
            o_ref[:, sl] = jnp.where(kind < 2, seg * (n * scale), seg)

    return pl.pallas_call(
        body, name="gdn_prep_fwd", grid=(B, 3),
        in_specs=[BS((S, 512), lambda b, k: (b, OFF_GDN // 512 + k)), BS((GDN_CONV, 512), lambda b, k: (0, k))],
        out_specs=BS((S, 512), lambda b, k: (b, k)),
        out_shape=jax.ShapeDtypeStruct((T, GDN_QKV), F32), compiler_params=_arb(2))(P, conv_w)


def gdn_prep_bwd(P, dqkv, conv_w, B, S):
    T = B * S

    def body(x_ref, d_ref, w_ref, o_ref, gw_ref):
        kind = pl.program_id(0)

        @pl.when(pl.program_id(1) == 0)
        def _():
            gw_ref[...] = jnp.zeros_like(gw_ref)

        t = lax.broadcasted_iota(jnp.int32, (S, 1), 0)
        x = x_ref[...]
        w = w_ref[...]
        y, sg = _conv_silu(x, w, t)
        a = y * sg
        scale = jnp.where(kind == 0, GDN_SCALE, 1.0).astype(F32)
        das = []
        for h in range(N_HEADS):
            sl = slice(h * 128, (h + 1) * 128)
            seg, dseg = a[:, sl], d_ref[:, sl]
            n = lax.rsqrt(jnp.sum(seg * seg, axis=-1, keepdims=True) + EPS)
            dn = scale * (n * dseg - seg * (n * n * n) * jnp.sum(dseg * seg, axis=-1, keepdims=True))
            das.append(jnp.where(kind < 2, dn, dseg))
        dy = jnp.concatenate(das, axis=1) * (sg * (1.0 + y * (1.0 - sg)))
        dx = dy * w[3:4, :]
        gw_ref[3:4, :] += jnp.sum(dy * x, axis=0, keepdims=True)
        for s in range(1, GDN_CONV):
            dx = dx + jnp.where(t + s < S, pltpu.roll(dy, S - s, 0), 0.0) * w[3 - s:4 - s, :]
            gw_ref[3 - s:4 - s, :] += jnp.sum(dy * jnp.where(t >= s, pltpu.roll(x, s, 0), 0.0), axis=0, keepdims=True)
        o_ref[...] = dx.astype(BF16)

    return pl.pallas_call(
        body, name="gdn_prep_bwd", grid=(3, B),
        in_specs=[BS((S, 512), lambda k, b: (b, OFF_GDN // 512 + k)), BS((S, 512), lambda k, b: (b, k)),
                  BS((GDN_CONV, 512), lambda k, b: (0, k))],
        out_specs=[BS((S, 512), lambda k, b: (b, k)), BS((GDN_CONV, 512), lambda k, b: (0, k))],
        out_shape=[jax.ShapeDtypeStruct((T, GDN_QKV), BF16), jax.ShapeDtypeStruct((GDN_CONV, GDN_QKV), F32)],
        compiler_params=_arb(2))(P, dqkv, conv_w)


def _chunk_row(n_rows):
    return lax.broadcasted_iota(jnp.int32, (n_rows, 1), 0) % CHUNK


def gdn_gate_fwd(P, alog_row, dt_row, B, S):
    T = B * S

    def body(x_ref, al_ref, dt_ref, o_ref):
        x = x_ref[...]
        lane = lax.broadcasted_iota(jnp.int32, (1, 128), 1)
        g = jnp.where(lane < 4, -jnp.exp(al_ref[...]) * _softplus(x + dt_ref[...]), 0.0)
        t = _chunk_row(S)
        for s in (1, 2, 4, 8, 16, 32):
            g = g + jnp.where(t >= s, pltpu.roll(g, s, 0), 0.0)
        o_ref[...] = jnp.where(lane < 4, g, jnp.where(lane < 8, _sigmoid(x), 0.0))

    row = BS((1, 128), lambda b: (0, 0))
    return pl.pallas_call(
        body, name="gdn_gate_fwd", grid=(B,),
        in_specs=[BS((S, 128), lambda b: (b, 768 // 128)), row, row], out_specs=BS((S, 128), lambda b: (b, 0)),
        out_shape=jax.ShapeDtypeStruct((T, 128), F32), compiler_params=_arb(1))(P, alog_row, dt_row)


def gdn_gate_bwd(P, dGB, alog_row, dt_row, B, S):
    T = B * S

    def body(x_ref, d_ref, al_ref, dt_ref, o_ref, acc_ref):
        @pl.when(pl.program_id(0) == 0)
        def _():
            acc_ref[...] = jnp.zeros_like(acc_ref)

        x, d = x_ref[...], d_ref[...]
        lane = lax.broadcasted_iota(jnp.int32, (1, 128), 1)
        z = x + dt_ref[...]
        coef = -jnp.exp(al_ref[...])
        g = coef * _softplus(z)
        da = jnp.where(lane < 4, d * coef * _sigmoid(z), 0.0)
        beta = _sigmoid(x)
        o_ref[...] = jnp.where(lane < 4, da, jnp.where(lane < 8, d * beta * (1.0 - beta), 0.0)).astype(BF16)
        acc_ref[0:1, :] += jnp.sum(jnp.where(lane < 4, d * g, 0.0), axis=0, keepdims=True)
        acc_ref[1:2, :] += jnp.sum(da, axis=0, keepdims=True)

    row = BS((1, 128), lambda b: (0, 0))
    return pl.pallas_call(
        body, name="gdn_gate_bwd", grid=(B,),
        in_specs=[BS((S, 128), lambda b: (b, 768 // 128)), BS((S, 128), lambda b: (b, 0)), row, row],
        out_specs=[BS((S, 128), lambda b: (b, 0)), BS((8, 128), lambda b: (0, 0))],
        out_shape=[jax.ShapeDtypeStruct((T, 128), BF16), jax.ShapeDtypeStruct((8, 128), F32)],
        compiler_params=_arb(1))(P, dGB, alog_row, dt_row)


def _chunk_masks(nc):
    r = lax.broadcasted_iota(jnp.int32, (nc, CHUNK, CHUNK), 1)
    c = lax.broadcasted_iota(jnp.int32, (nc, CHUNK, CHUNK), 2)
    return r >= c, r > c


def _chunk_local(q, k, gc, gr, beta, incl, strict):
    decay = jnp.exp(jnp.where(incl, gc - gr, NEG))
    kb = k * beta
    kbf = k.astype(BF16)
    m_kk = _bdot("gcd,gjd->gcj", kb.astype(BF16), kbf)
    l_mat = jnp.where(strict, m_kk * decay, 0.0)
    a_mat = _bdot("gcd,gjd->gcj", q.astype(BF16), kbf) * decay
    return decay, kb, l_mat, a_mat


def _split_bf16(x):
    hi = x.astype(BF16)
    return hi, (x - hi.astype(F32)).astype(BF16)


def _mm_split(ah, al, bh, bl):
    spec = "gij,gjk->gik"
    return _bdot(spec, ah, bh) + (_bdot(spec, ah, bl) + _bdot(spec, al, bh))


def gdn_chunk_fwd(qkv, GB, Grow, B, S, nc=8):
    T = B * S
    N = S // CHUNK
    nc = min(nc, N)
    nb = N // nc
    R = nc * CHUNK

    def body(q_ref, k_ref, v_ref, gb_ref, gr_ref, u_ref, w_ref, t_ref, a_ref):
        incl, strict = _chunk_masks(nc)
        eye = (lax.broadcasted_iota(jnp.int32, (nc, CHUNK, CHUNK), 1)
               == lax.broadcasted_iota(jnp.int32, (nc, CHUNK, CHUNK), 2)).astype(F32)
        for h in range(N_HEADS):
            sl = slice(h * 128, (h + 1) * 128)
            q = q_ref[:, sl].reshape(nc, CHUNK, 128)
            k = k_ref[:, sl].reshape(nc, CHUNK, 128)
            v = v_ref[:, sl].reshape(nc, CHUNK, 128)
            gc = gb_ref[:, h:h + 1].reshape(nc, CHUNK, 1)
            beta = gb_ref[:, 4 + h:5 + h].reshape(nc, CHUNK, 1)
            gr = gr_ref[h][:, None, :]
            _, kb, l_mat, a_mat = _chunk_local(q, k, gc, gr, beta, incl, strict)
            pw = -l_mat
            tinv = eye + pw
            for _ in range(5):
                ph, pl_ = _split_bf16(pw)
                pw = _mm_split(ph, pl_, ph, pl_)
                ph, pl_ = _split_bf16(pw)
                th, tl = _split_bf16(tinv)
                tinv = tinv + _mm_split(th, tl, ph, pl_)
            tb = tinv.astype(BF16)
            u = _bdot("gcj,gjv->gcv", tb, (v * beta).astype(BF16))
            w = _bdot("gcj,gjk->gck", tb, (kb * jnp.exp(gc)).astype(BF16))
            u_ref[:, sl] = u.reshape(R, 128)
            w_ref[:, sl] = w.reshape(R, 128)
            t_ref[h] = tinv
            a_ref[h] = a_mat

    rowb = lambda c, j: BS((R, c), lambda b, n: (b * nb + n, j))
    mat = BS((None, N_HEADS, nc, CHUNK, CHUNK), lambda b, n: (b, 0, n, 0, 0))
    return pl.pallas_call(
        body, name="gdn_chunk_fwd", grid=(B, nb),
        in_specs=[rowb(512, 0), rowb(512, 1), rowb(512, 2), rowb(128, 0),
                  BS((None, N_HEADS, nc, CHUNK), lambda b, n: (b, 0, n, 0))],
        out_specs=[rowb(512, 0), rowb(512, 0), mat, mat],
        out_shape=[jax.ShapeDtypeStruct((T, 512), F32), jax.ShapeDtypeStruct((T, 512), F32),
                   jax.ShapeDtypeStruct((B, N_HEADS, N, CHUNK, CHUNK), F32),
                   jax.ShapeDtypeStruct((B, N_HEADS, N, CHUNK, CHUNK), F32)],
        compiler_params=_arb(2))(qkv, qkv, qkv, GB, Grow)


def gdn_scan_fwd(qkv3, U3, W3, GB3, A, B, S):
    N = S // CHUNK

    def body(q_ref, k_ref, u_ref, w_ref, gb_ref, a_ref, o_ref, vn_ref, st_ref, s_s):
        @pl.when(pl.program_id(0) == 0)
        def _():
            s_s[...] = jnp.zeros_like(s_s)

        for b in range(B):
            for h in range(N_HEADS):
                sl = slice(h * 128, (h + 1) * 128)
                st = s_s[b, h]
                st_ref[b, h] = st
                stb = st.astype(BF16)
                g = gb_ref[b, :, h:h + 1]
                gl = g[CHUNK - 1:CHUNK, :]
                vn = u_ref[b, :, sl] - _dot(w_ref[b, :, sl].astype(BF16), stb, NN)
                vnb = vn.astype(BF16)
                o = _dot((q_ref[b, :, sl] * jnp.exp(g)).astype(BF16), stb, NN) + _dot(a_ref[b, h].astype(BF16), vnb, NN)
                vn_ref[b, :, sl] = vn
                o_ref[b, :, sl] = o
                s_s[b, h] = st * jnp.exp(gl) + _dot((k_ref[b, :, sl] * jnp.exp(gl - g)).astype(BF16), vnb, TN)

    tok = lambda c, j: BS((B, CHUNK, c), lambda n: (0, n, j))
    return pl.pallas_call(
        body, name="gdn_scan_fwd", grid=(N,),
        in_specs=[tok(512, 0), tok(512, 1), tok(512, 0), tok(512, 0), tok(128, 0),
                  BS((B, N_HEADS, None, CHUNK, CHUNK), lambda n: (0, 0, n, 0, 0))],
        out_specs=[tok(512, 0), tok(512, 0), BS((B, N_HEADS, None, 128, 128), lambda n: (0, 0, n, 0, 0))],
        out_shape=[jax.ShapeDtypeStruct((B, S, 512), F32), jax.ShapeDtypeStruct((B, S, 512), F32),
                   jax.ShapeDtypeStruct((B, N_HEADS, N, 128, 128), F32)],
        scratch_shapes=[pltpu.VMEM((B, N_HEADS, 128, 128), F32)],
        compiler_params=_arb(1))(qkv3, qkv3, U3, W3, GB3, A)


def gdn_scan_bwd(dO3, qkv3, W3, Vn3, GB3, A, St, B, S):
    N = S // CHUNK

    def body(do_ref, q_ref, k_ref, w_ref, vn_ref, gb_ref, a_ref, st_ref,
             du_ref, dw_ref, dq_ref, dk_ref, da_ref, dg_ref, ds_s):
        @pl.when(pl.program_id(0) == 0)
        def _():
            ds_s[...] = jnp.zeros_like(ds_s)

        lane = lax.broadcasted_iota(jnp.int32, (1, 128), 1)
        last = lax.broadcasted_iota(jnp.int32, (CHUNK, 1), 0) == CHUNK - 1
        for b in range(B):
            dg_all = jnp.zeros((CHUNK, 128), F32)
            for h in range(N_HEADS):
                sl = slice(h * 128, (h + 1) * 128)
                st = st_ref[b, h]
                stb = st.astype(BF16)
                dsn = ds_s[b, h]
                dsnb = dsn.astype(BF16)
                g = gb_ref[b, :, h:h + 1]
                gl = g[CHUNK - 1:CHUNK, :]
                egl = jnp.exp(gl)
                ekd = jnp.exp(gl - g)
                eg = jnp.exp(g)
                q, k = q_ref[b, :, sl], k_ref[b, :, sl]
                kd = k * ekd
                qg = q * eg
                do = do_ref[b, :, sl].astype(BF16)
                vnb = vn_ref[b, :, sl].astype(BF16)
                dvn = _dot(a_ref[b, h].astype(BF16), do, TN) + _dot(kd.astype(BF16), dsnb, NN)
                dvnb = dvn.astype(BF16)
                da_ref[b, h] = _dot(do, vnb, NT)
                dqg = _dot(do, stb, NT)
                dkd = _dot(vnb, dsnb, NT)
                ds_s[b, h] = (_dot(qg.astype(BF16), do, TN) + egl * dsn - _dot(w_ref[b, :, sl].astype(BF16), dvnb, TN))
                du_ref[b, :, sl] = dvn
                dw_ref[b, :, sl] = -_dot(dvnb, stb, NT)
                dq_ref[b, :, sl] = dqg * eg
                dk_ref[b, :, sl] = dkd * ekd
                ddel = jnp.sum(dkd * kd, axis=1, keepdims=True)
                dgl = jnp.sum(ddel, axis=0, keepdims=True) + jnp.sum(jnp.sum(st * dsn, axis=1, keepdims=True), axis=0, keepdims=True) * egl
                col = jnp.sum(dqg * qg, axis=1, keepdims=True) - ddel + jnp.where(last, dgl, 0.0)
                dg_all = jnp.where(lane == h, col, dg_all)
            dg_ref[b] = dg_all

    tok = lambda c, j: BS((B, CHUNK, c), lambda n: (0, N - 1 - n, j))
    mat = lambda d: BS((B, N_HEADS, None, d, d), lambda n: (0, 0, N - 1 - n, 0, 0))
    return pl.pallas_call(
        body, name="gdn_scan_bwd", grid=(N,),
        in_specs=[tok(512, 0), tok(512, 0), tok(512, 1), tok(512, 0), tok(512, 0), tok(128, 0), mat(CHUNK), mat(128)],
        out_specs=[tok(512, 0), tok(512, 0), tok(512, 0), tok(512, 0), mat(CHUNK), tok(128, 0)],
        out_shape=[jax.ShapeDtypeStruct((B, S, 512), F32)] * 4
        + [jax.ShapeDtypeStruct((B, N_HEADS, N, CHUNK, CHUNK), F32), jax.ShapeDtypeStruct((B, S, 128), F32)],
        scratch_shapes=[pltpu.VMEM((B, N_HEADS, 128, 128), F32)],
        compiler_params=_arb(1))(dO3, qkv3, qkv3, W3, Vn3, GB3, A, St)


def gdn_chunk_bwd(qkv, GB, Grow, Tinv, dA, dU, dW, dQ1, dK1, dG1, B, S, nc=8):
    T = B * S
    N = S // CHUNK
    nc = min(nc, N)
    nb = N // nc
    R = nc * CHUNK

    def body(q_ref, k_ref, v_ref, gb_ref, gr_ref, t_ref, da_ref, du_ref, dw_ref, dq1_ref, dk1_ref, dg1_ref, o_ref, dgb_ref):
        incl, strict = _chunk_masks(nc)
        lane = lax.broadcasted_iota(jnp.int32, (1, 128), 1)
        dg_all = dg1_ref[...]
        db_all = jnp.zeros((R, 128), F32)
        for h in range(N_HEADS):
            sl = slice(h * 128, (h + 1) * 128)
            q = q_ref[:, sl].reshape(nc, CHUNK, 128)
            k = k_ref[:, sl].reshape(nc, CHUNK, 128)
            v = v_ref[:, sl].reshape(nc, CHUNK, 128)
            gc = gb_ref[:, h:h + 1].reshape(nc, CHUNK, 1)
            beta = gb_ref[:, 4 + h:5 + h].reshape(nc, CHUNK, 1)
            gr = gr_ref[h][:, None, :]
            decay, kb, l_mat, a_mat = _chunk_local(q, k, gc, gr, beta, incl, strict)
            eg = jnp.exp(gc)
            kbg = kb * eg
            vb = v * beta
            tb = t_ref[h].astype(BF16)
            du = du_ref[:, sl].reshape(nc, CHUNK, 128).astype(BF16)
            dw = dw_ref[:, sl].reshape(nc, CHUNK, 128).astype(BF16)
            dvb = _bdot("gcj,gcv->gjv", tb, du)
            dkbg = _bdot("gcj,gck->gjk", tb, dw)
            dt = _bdot("gcv,gjv->gcj", du, vb.astype(BF16)) + _bdot("gck,gjk->gcj", dw, kbg.astype(BF16))
            tmp = _bdot("gac,gab->gcb", tb, dt.astype(BF16))
            dl = jnp.where(strict, -_bdot("gcb,gdb->gcd", tmp.astype(BF16), tb), 0.0)
            da = da_ref[h]
            dm = (dl * decay).astype(BF16)
            dqk = (da * decay).astype(BF16)
            kbf = k.astype(BF16)
            dkb = _bdot("gcj,gjd->gcd", dm, kbf) + dkbg * eg
            dk = (_bdot("gcj,gcd->gjd", dm, kb.astype(BF16)) + _bdot("gcj,gcd->gjd", dqk, q.astype(BF16))
                  + dk1_ref[:, sl].reshape(nc, CHUNK, 128) + dkb * beta)
            dq = _bdot("gcj,gjd->gcd", dqk, kbf) + dq1_ref[:, sl].reshape(nc, CHUNK, 128)
            e = dl * l_mat + da * a_mat
            dgc = (jnp.sum(e, axis=2, keepdims=True) - jnp.sum(jnp.swapaxes(e, 1, 2), axis=2, keepdims=True)
                   + jnp.sum(dkbg * kbg, axis=2, keepdims=True))
            dbeta = jnp.sum(dkb * k, axis=2, keepdims=True) + jnp.sum(dvb * v, axis=2, keepdims=True)
            o_ref[:, sl] = dq.reshape(R, 128)
            o_ref[:, 512 + h * 128:512 + (h + 1) * 128] = dk.reshape(R, 128)
            o_ref[:, 1024 + h * 128:1024 + (h + 1) * 128] = (dvb * beta).reshape(R, 128)
            dg_all = dg_all + jnp.where(lane == h, dgc.reshape(R, 1), 0.0)
            db_all = jnp.where(lane == 4 + h, dbeta.reshape(R, 1), db_all)
        t = _chunk_row(R)
        for s in (1, 2, 4, 8, 16, 32):
            dg_all = dg_all + jnp.where(t + s < CHUNK, pltpu.roll(dg_all, R - s, 0), 0.0)
        dgb_ref[...] = jnp.where(lane < 4, dg_all, db_all)

    rowb = lambda c, j: BS((R, c), lambda b, n: (b * nb + n, j))
    mat = BS((None, N_HEADS, nc, CHUNK, CHUNK), lambda b, n: (b, 0, n, 0, 0))
    return pl.pallas_call(
        body, name="gdn_chunk_bwd", grid=(B, nb),
        in_specs=[rowb(512, 0), rowb(512, 1), rowb(512, 2), rowb(128, 0),
                  BS((None, N_HEADS, nc, CHUNK), lambda b, n: (b, 0, n, 0)), mat, mat,
                  rowb(512, 0), rowb(512, 0), rowb(512, 0), rowb(512, 0), rowb(128, 0)],
        out_specs=[rowb(GDN_QKV, 0), rowb(128, 0)],
        out_shape=[jax.ShapeDtypeStruct((T, GDN_QKV), F32), jax.ShapeDtypeStruct((T, 128), F32)],
        compiler_params=_arb(2))(qkv, qkv, qkv, GB, Grow, Tinv, dA, dU, dW, dQ1, dK1, dG1)


def _gdn_out_norm(og, gg):
    outs, xhs, rs = [], [], []
    for h in range(N_HEADS):
        seg = og[:, h * 128:(h + 1) * 128]
        r = lax.rsqrt(jnp.mean(seg * seg, axis=-1, keepdims=True) + EPS)
        xh = seg * r
        outs.append(xh * gg)
        xhs.append(xh)
        rs.append(r)
    return outs, xhs, rs


def merge_fwd(o_mla, o_gdn, o_mem, P, x, tgt, w_out, g_gdn, g_fin, tm=256):
    T = x.shape[0]
    tm = min(tm, T)

    def body(om_ref, og_ref, oc_ref, gate_ref, x_ref, t_ref, w_ref, gg_ref, gf_ref, mix_ref, dx_ref, dxb_ref, sq_ref, gnf_ref):
        @pl.when(pl.program_id(0) == 0)
        def _():
            sq_ref[...] = jnp.zeros_like(sq_ref)
            gnf_ref[...] = jnp.zeros_like(gnf_ref)

        ogn, _, _ = _gdn_out_norm(og_ref[...], gg_ref[...])
        cat = jnp.concatenate([om_ref[...]] + ogn + [oc_ref[...]], axis=1)
        gt = gate_ref[...]
        mixed = (cat * (gt * _sigmoid(gt))).astype(BF16)
        mix_ref[...] = mixed
        x2 = x_ref[...] + _dot(mixed, w_ref[...], NN)
        r2 = lax.rsqrt(jnp.mean(x2 * x2, axis=-1, keepdims=True) + EPS)
        xh = x2 * r2
        gf = gf_ref[...]
        diff = xh * gf - t_ref[...]
        sq_ref[...] += jnp.sum(diff * diff, axis=0, keepdims=True)
        dy = diff * (1.0 / D_MODEL)
        gnf_ref[...] += jnp.sum(dy * xh, axis=0, keepdims=True)
        dxh = dy * gf
        dx = r2 * (dxh - xh * jnp.mean(dxh * xh, axis=-1, keepdims=True))
        dx_ref[...] = dx
        dxb_ref[...] = dx.astype(BF16)

    rowb = lambda c, j=0: BS((tm, c), lambda i: (i, j))
    full = lambda r, c: BS((r, c), lambda i: (0, 0))
    return pl.pallas_call(
        body, name="merge_fwd", grid=(T // tm,),
        in_specs=[rowb(512), rowb(512), rowb(512), rowb(D_MIX, OFF_GATE // D_MIX), rowb(D_MODEL), rowb(D_MODEL),
                  full(D_MIX, D_MODEL), full(1, 128), full(1, D_MODEL)],
        out_specs=[rowb(D_MIX), rowb(D_MODEL), rowb(D_MODEL), full(1, D_MODEL), full(1, D_MODEL)],
        out_shape=[jax.ShapeDtypeStruct((T, D_MIX), BF16), jax.ShapeDtypeStruct((T, D_MODEL), F32),
                   jax.ShapeDtypeStruct((T, D_MODEL), BF16),
                   jax.ShapeDtypeStruct((1, D_MODEL), F32), jax.ShapeDtypeStruct((1, D_MODEL), F32)],
        compiler_params=_arb(1))(o_mla, o_gdn, o_mem, P, x, tgt, w_out, g_gdn, g_fin)


def merge_bwd(dx2, o_mla, o_gdn, o_mem, P, w_out, g_gdn, tm=256):
    T = dx2.shape[0]
    tm = min(tm, T)

    def body(dx_ref, om_ref, og_ref, oc_ref, gate_ref, w_ref, gg_ref, dgate_ref, dom_ref, dog_ref, doc_ref, ggn_ref):
        @pl.when(pl.program_id(0) == 0)
        def _():
            ggn_ref[...] = jnp.zeros_like(ggn_ref)

        gg = gg_ref[...]
        dmix = _dot(dx_ref[...].astype(BF16), w_ref[...], NT)
        ogn, xhs, rs = _gdn_out_norm(og_ref[...], gg)
        cat = jnp.concatenate([om_ref[...]] + ogn + [oc_ref[...]], axis=1)
        gt = gate_ref[...]
        sg = _sigmoid(gt)
        dgate_ref[...] = (dmix * cat * (sg * (1.0 + gt * (1.0 - sg)))).astype(BF16)
        dcat = dmix * (gt * sg)
        dom_ref[...] = dcat[:, :512]
        doc_ref[...] = dcat[:, 1024:]
        acc = jnp.zeros((1, 128), F32)
        for h in range(N_HEADS):
            dseg = dcat[:, 512 + h * 128:512 + (h + 1) * 128]
            acc = acc + jnp.sum(dseg * xhs[h], axis=0, keepdims=True)
            dxh = dseg * gg
            dog_ref[:, h * 128:(h + 1) * 128] = rs[h] * (dxh - xhs[h] * jnp.mean(dxh * xhs[h], axis=-1, keepdims=True))
        ggn_ref[...] += acc

    rowb = lambda c, j=0: BS((tm, c), lambda i: (i, j))
    full = lambda r, c: BS((r, c), lambda i: (0, 0))
    return pl.pallas_call(
        body, name="merge_bwd", grid=(T // tm,),
        in_specs=[rowb(D_MODEL), rowb(512), rowb(512), rowb(512), rowb(D_MIX, OFF_GATE // D_MIX),
                  full(D_MIX, D_MODEL), full(1, 128)],
        out_specs=[rowb(D_MIX), rowb(512), rowb(512), rowb(512), full(1, 128)],
        out_shape=[jax.ShapeDtypeStruct((T, D_MIX), BF16)] + [jax.ShapeDtypeStruct((T, 512), F32)] * 3
        + [jax.ShapeDtypeStruct((1, 128), F32)],
        compiler_params=_arb(1))(dx2, o_mla, o_gdn, o_mem, P, w_out, g_gdn)


def in_norm_bwd(x, dh, dx2, gain, tm=256):
    T, n = x.shape
    tm = min(tm, T)

    def body(x_ref, dh_ref, dx2_ref, g_ref, o_ref, acc_ref):
        @pl.when(pl.program_id(0) == 0)
        def _():
            acc_ref[...] = jnp.zeros_like(acc_ref)

        xv = x_ref[...]
        r = lax.rsqrt(jnp.mean(xv * xv, axis=-1, keepdims=True) + EPS)
        xh = xv * r
        dy = dh_ref[...]
        acc_ref[...] += jnp.sum(dy * xh, axis=0, keepdims=True)
        dxh = dy * g_ref[...]
        o_ref[...] = dx2_ref[...] + r * (dxh - xh * jnp.mean(dxh * xh, axis=-1, keepdims=True))

    rowb = BS((tm, n), lambda i: (i, 0))
    full = BS((1, n), lambda i: (0, 0))
    return pl.pallas_call(
        body, name="in_norm_bwd", grid=(T // tm,),
        in_specs=[rowb, rowb, rowb, full], out_specs=[rowb, full],
        out_shape=[jax.ShapeDtypeStruct((T, n), F32), jax.ShapeDtypeStruct((1, n), F32)],
        compiler_params=_arb(1))(x, dh, dx2, gain)


W_IN_SHARD = D_IN // 4
_GDN0 = Q_LORA + KV_LORA + MLA_ROPE
_AB0 = _GDN0 + GDN_QKV
_MEMQ0 = _AB0 + 2 * N_HEADS
_GATE0 = _MEMQ0 + N_HEADS * MEM_DH


def _w_in_row_map():
    a, m, gt = _AB0 - 2 * W_IN_SHARD, _MEMQ0 - 2 * W_IN_SHARD, _GATE0 - 2 * W_IN_SHARD
    e0 = OFF_GDN + W_IN_SHARD - _GDN0
    e1 = e0 + W_IN_SHARD
    e2 = OFF_GATE + W_IN_SHARD - gt
    return [(0, 0, 0, 672), (0, 672, 704, 32), (2, a, 768, m - a), (2, m, OFF_MEMQ, gt - m), (0, _GDN0, OFF_GDN, W_IN_SHARD - _GDN0),
            (1, 0, e0, W_IN_SHARD), (2, 0, e1, a), (2, gt, OFF_GATE, W_IN_SHARD - gt), (3, 0, e2, W_IN_SHARD)]


_W_IN_ZERO_ROWS = [(672, 32), (736, 32), (776, 248)]
W_IN_LANES = 256


def pad_w_in_t(shards):
    per_half = shards.shape[3] // W_IN_LANES

    def body(s_ref, o_ref):
        for r0, n in _W_IN_ZERO_ROWS:
            o_ref[r0:r0 + n, :] = jnp.zeros((n, W_IN_LANES), o_ref.dtype)
        for q, src, dst, n in _w_in_row_map():
            o_ref[dst:dst + n, :] = s_ref[q, src:src + n, :]

    return pl.pallas_call(
        body, name="pad_w_in_t", grid=(D_MODEL // W_IN_LANES,),
        in_specs=[BS((N_CHIPS, None, W_IN_SHARD, W_IN_LANES), lambda j: (0, j // per_half, 0, j % per_half))],
        out_specs=BS((N_PAD, W_IN_LANES), lambda j: (0, j)),
        out_shape=jax.ShapeDtypeStruct((N_PAD, D_MODEL), shards.dtype), compiler_params=_arb(1))(shards)


def unpad_w_in_t(g):
    def body(g_ref, o_ref):
        for q, src, dst, n in _w_in_row_map():
            o_ref[q, src:src + n, :] = g_ref[dst:dst + n, :]

    return pl.pallas_call(
        body, name="unpad_w_in_t", grid=(D_MODEL // W_IN_LANES,),
        in_specs=[BS((N_PAD, W_IN_LANES), lambda j: (0, j))], out_specs=BS((N_CHIPS, W_IN_SHARD, W_IN_LANES), lambda j: (0, 0, j)),
        out_shape=jax.ShapeDtypeStruct((N_CHIPS, W_IN_SHARD, D_MODEL), g.dtype), compiler_params=_arb(1))(g)


def _pad_w_q_b_t(s):
    z = jnp.zeros((32, s.shape[2]), s.dtype)
    parts = []
    for h in range(N_HEADS):
        parts += [s[h, :128], s[h, 128:160], z, s[h, 160:192], z]
    return jnp.concatenate(parts, axis=0)


def _unpad_w_q_b_t(g):
    return jnp.stack([jnp.concatenate([g[h * HEAD_PAD:h * HEAD_PAD + 128], g[h * HEAD_PAD + 128:h * HEAD_PAD + 160],
                                       g[h * HEAD_PAD + 192:h * HEAD_PAD + 224]]) for h in range(N_HEADS)])


def _perm_w_kv_b(s):
    return jnp.concatenate([s[h, :, :128] for h in range(N_HEADS)] + [s[h, :, 128:] for h in range(N_HEADS)], axis=1)


def _unperm_w_kv_b(g):
    return jnp.stack([jnp.concatenate([g[:, h * 128:(h + 1) * 128], g[:, 512 + h * 128:512 + (h + 1) * 128]], axis=1)
                      for h in range(N_HEADS)])


def _lane_row(v4):
    return jnp.pad(v4.reshape(1, -1).astype(F32), ((0, 0), (0, 128 - v4.size)))


def _pack(pieces, n_rows):
    flat = jnp.concatenate([p.reshape(-1) for p in pieces])
    return jnp.pad(flat, (0, n_rows * 1024 - flat.size)).reshape(n_rows, 1024)


def _unpack(block, shapes):
    flat = block.reshape(-1)
    out, off = [], 0
    for shp in shapes:
        n = int(np.prod(shp))
        out.append(flat[off:off + n].reshape(shp))
        off += n
    return out


N_CHIPS = 4
MESH = pl.DeviceIdType.MESH
ANY = BS(memory_space=pl.ANY)


def _place():
    return lax.axis_index("x"), lax.axis_index("y"), lax.axis_index("c")


def _other_chips(x, y):
    return [(1 - x, y), (x, 1 - y), (1 - x, 1 - y)]


def _half(split, which):
    if split == "lead":
        return (which,)
    axis, size = split
    ds = pl.ds(pl.multiple_of(which * size, 16 if axis == 0 else 128), size)
    return (ds, slice(None)) if axis == 0 else (slice(None), ds)


def allgather_chips(shards, splits, name):
    n = len(shards)

    def body(*refs):
        s_refs, o_refs = refs[:n], refs[n:2 * n]
        send_sems, recv_sems, local_sems = refs[2 * n:]
        x, y, c = _place()
        chips = _other_chips(x, y)

        def copy(k, src, dst, to):
            return pltpu.make_async_remote_copy(src_ref=src, dst_ref=dst, send_sem=send_sems.at[k], recv_sem=recv_sems.at[k],
                                                device_id=to, device_id_type=MESH)

        owns, first, passed = [], [], []
        for i, (s_ref, o_ref) in enumerate(zip(s_refs, o_refs)):
            mine = _half(splits[i], c)
            owns.append(pltpu.make_async_copy(s_ref, o_ref.at[2 * x + y], local_sems.at[i]))
            owns[-1].start()
            for j, (px, py) in enumerate(chips):
                first.append(copy(6 * i + j, s_ref.at[mine], o_ref.at[(2 * x + y,) + mine], (px, py, c)))
                first[-1].start()
        for i, (s_ref, o_ref) in enumerate(zip(s_refs, o_refs)):
            mine = _half(splits[i], c)
            for j, (px, py) in enumerate(chips):
                landed = o_ref.at[(2 * px + py,) + mine]
                copy(6 * i + j, s_ref.at[mine], landed, (px, py, c)).wait_recv()
                passed.append(copy(6 * i + 3 + j, landed, landed, (x, y, 1 - c)))
                passed[-1].start()
        for i, (s_ref, o_ref) in enumerate(zip(s_refs, o_refs)):
            theirs = _half(splits[i], 1 - c)
            for j, (px, py) in enumerate(chips):
                copy(6 * i + 3 + j, s_ref.at[theirs], o_ref.at[(2 * px + py,) + theirs], (x, y, 1 - c)).wait_recv()
        for cp in first + passed:
            cp.wait_send()
        for cp in owns:
            cp.wait()

    return pl.pallas_call(
        body, name=name, in_specs=[ANY] * n, out_specs=[ANY] * n,
        out_shape=[jax.ShapeDtypeStruct((N_CHIPS,) + s.shape, s.dtype) for s in shards],
        scratch_shapes=[pltpu.SemaphoreType.DMA((6 * n,)), pltpu.SemaphoreType.DMA((6 * n,)), pltpu.SemaphoreType.DMA((n,))])(*shards)


SEM = BS(memory_space=pltpu.SEMAPHORE)
HBM = BS(memory_space=pltpu.HBM)
_IN_HBM = lambda a: pltpu.with_memory_space_constraint(a, pltpu.HBM)
_SIDE_EFFECT = pltpu.SideEffectType.DATAFLOW_SIDE_EFFECTING


def _late_gather_copies(s_refs, l_refs, send_sems, recv_sems, local_sems, with_arrivals):
    x, y, c = _place()
    sends, recvs, locals_ = [], [], []
    for i, (s_ref, l_ref) in enumerate(zip(s_refs, l_refs)):
        locals_.append(pltpu.make_async_copy(s_ref, l_ref.at[2 * x + y], local_sems.at[i]))
        for j, (px, py) in enumerate(_other_chips(x, y)):
            k = 3 * i + j
            sends.append(pltpu.make_async_remote_copy(src_ref=s_ref, dst_ref=l_ref.at[2 * x + y], send_sem=send_sems.at[k],
                                                      recv_sem=recv_sems.at[k], device_id=(px, py, c), device_id_type=MESH))
            if with_arrivals:
                recvs.append(pltpu.make_async_remote_copy(src_ref=s_ref, dst_ref=l_ref.at[2 * px + py], send_sem=send_sems.at[k],
                                                          recv_sem=recv_sems.at[k], device_id=(px, py, c), device_id_type=MESH))
    return sends, recvs, locals_


def late_gather_start(shards, name):
    n = len(shards)

    def body(*refs):
        s_refs, l_refs = refs[:n], refs[n:2 * n]
        send_sems, recv_sems, local_sems = refs[2 * n:2 * n + 3]
        token = refs[-1]
        sends, _, locals_ = _late_gather_copies(s_refs, l_refs, send_sems, recv_sems, local_sems, False)
        for cp in locals_ + sends:
            cp.start()
        token[...] = jnp.zeros_like(token)

    lands = [lax.empty((N_CHIPS,) + s.shape, s.dtype) for s in shards]
    hbm_like = lambda a: pltpu.HBM(a.shape, a.dtype)
    out = pl.pallas_call(
        body, name=name,
        out_shape=[pltpu.SemaphoreType.DMA((3 * n,)), pltpu.SemaphoreType.DMA((3 * n,)), pltpu.SemaphoreType.DMA((n,))]
        + [hbm_like(s) for s in shards] + [hbm_like(l) for l in lands] + [jax.ShapeDtypeStruct((8, 128), F32)],
        in_specs=[HBM] * (2 * n), out_specs=[SEM] * 3 + [HBM] * (2 * n) + [BS(memory_space=pltpu.VMEM)],
        input_output_aliases={i: 3 + i for i in range(2 * n)},
        compiler_params=pltpu.CompilerParams(has_side_effects=_SIDE_EFFECT))(*[_IN_HBM(s) for s in shards], *[_IN_HBM(l) for l in lands])
    return out[:3], out[3:3 + n], out[3 + n:3 + 2 * n], out[-1]


def late_gather_wait(sems, shards, lands, after, name):
    n = len(shards)

    def body(*refs):
        s_refs, l_refs = refs[:n], refs[n:2 * n]
        send_sems, recv_sems, local_sems = refs[2 * n:2 * n + 3]
        sends, recvs, locals_ = _late_gather_copies(s_refs, l_refs, send_sems, recv_sems, local_sems, True)
        for cp in locals_:
            cp.wait()
        for cp in sends:
            cp.wait_send()
        for cp in recvs:
            cp.wait_recv()

    hbm_like = lambda a: pltpu.HBM(a.shape, a.dtype)
    out = pl.pallas_call(
        body, name=name, out_shape=[hbm_like(s) for s in shards] + [hbm_like(l) for l in lands],
        in_specs=[HBM] * (2 * n) + [SEM] * 3 + [BS(memory_space=pl.ANY)], out_specs=[HBM] * (2 * n),
        input_output_aliases={i: i for i in range(2 * n)},
        compiler_params=pltpu.CompilerParams(has_side_effects=_SIDE_EFFECT))(*shards, *lands, *sems, after)
    return out[n:]


def allgather_devices(block, name):
    R, C = block.shape

    def body(b_ref, o_ref, send_sems, recv_sems, local_sem):
        x, y, c = _place()
        me = 4 * x + 2 * y + c
        own = pltpu.make_async_copy(b_ref, o_ref.at[me], local_sem)
        own.start()
        copies = []
        for r in range(1, 8):
            px = 1 - x if r & 4 else x
            py = 1 - y if r & 2 else y
            pc = 1 - c if r & 1 else c
            send = pltpu.make_async_remote_copy(src_ref=b_ref, dst_ref=o_ref.at[me], send_sem=send_sems.at[r - 1],
                                                recv_sem=recv_sems.at[r - 1], device_id=(px, py, pc), device_id_type=MESH)
            recv = pltpu.make_async_remote_copy(src_ref=b_ref, dst_ref=o_ref.at[4 * px + 2 * py + pc], send_sem=send_sems.at[r - 1],
                                                recv_sem=recv_sems.at[r - 1], device_id=(px, py, pc), device_id_type=MESH)
            send.start()
            copies.append((send, recv))
        for send, recv in copies:
            recv.wait_recv()
            send.wait_send()
        own.wait()

    return pl.pallas_call(
        body, name=name, in_specs=[ANY], out_specs=ANY, out_shape=jax.ShapeDtypeStruct((8, R, C), block.dtype),
        scratch_shapes=[pltpu.SemaphoreType.DMA((7,)), pltpu.SemaphoreType.DMA((7,)), pltpu.SemaphoreType.DMA(())])(block)


def swap_sibling(arrs, name, splits=None):
    n = len(arrs)

    def sent(a_ref, i, c):
        return a_ref if splits is None else a_ref.at[(slice(None),) + _half(splits[i], 1 - c)]

    def out_shape(a, i):
        if splits is None:
            return a.shape
        axis, size = splits[i]
        return (a.shape[0], size, a.shape[2]) if axis == 0 else (a.shape[0], a.shape[1], size)

    def body(*refs):
        a_refs, o_refs = refs[:n], refs[n:2 * n]
        send_sems, recv_sems = refs[2 * n:]
        x, y, c = _place()
        copies = [pltpu.make_async_remote_copy(src_ref=sent(a_ref, i, c), dst_ref=o_ref, send_sem=send_sems.at[i],
                                               recv_sem=recv_sems.at[i], device_id=(x, y, 1 - c), device_id_type=MESH)
                  for i, (a_ref, o_ref) in enumerate(zip(a_refs, o_refs))]
        for cp in copies:
            cp.start()
        for cp in copies:
            cp.wait()

    return pl.pallas_call(
        body, name=name, in_specs=[ANY] * n, out_specs=[ANY] * n,
        out_shape=[jax.ShapeDtypeStruct(out_shape(a, i), a.dtype) for i, a in enumerate(arrs)],
        scratch_shapes=[pltpu.SemaphoreType.DMA((n,)), pltpu.SemaphoreType.DMA((n,))])(*arrs)


def _exchange_copies(p_refs, l_refs, send_sems, recv_sems):
    x, y, c = _place()
    return [pltpu.make_async_remote_copy(src_ref=p_ref.at[2 * px + py], dst_ref=l_ref.at[j], send_sem=send_sems.at[3 * i + j],
                                         recv_sem=recv_sems.at[3 * i + j], device_id=(px, py, c), device_id_type=MESH)
            for i, (p_ref, l_ref) in enumerate(zip(p_refs, l_refs)) for j, (px, py) in enumerate(_other_chips(x, y))]


def exchange_chips_start(parts, name):
    n = len(parts)

    def body(*refs):
        send_sems, recv_sems = refs[2 * n:2 * n + 2]
        for cp in _exchange_copies(refs[:n], refs[n:2 * n], send_sems, recv_sems):
            cp.start()
        refs[-1][...] = jnp.zeros_like(refs[-1])

    lands = [lax.empty((3,) + p.shape[1:], p.dtype) for p in parts]
    hbm_like = lambda a: pltpu.HBM(a.shape, a.dtype)
    out = pl.pallas_call(
        body, name=name,
        out_shape=[pltpu.SemaphoreType.DMA((3 * n,)), pltpu.SemaphoreType.DMA((3 * n,))]
        + [hbm_like(p) for p in parts] + [hbm_like(l) for l in lands] + [jax.ShapeDtypeStruct((8, 128), F32)],
        in_specs=[HBM] * (2 * n), out_specs=[SEM] * 2 + [HBM] * (2 * n) + [BS(memory_space=pltpu.VMEM)],
        input_output_aliases={i: 2 + i for i in range(2 * n)},
        compiler_params=pltpu.CompilerParams(has_side_effects=_SIDE_EFFECT))(*[_IN_HBM(p) for p in parts], *[_IN_HBM(l) for l in lands])
    return out[:2], out[2:2 + n], out[2 + n:2 + 2 * n], out[-1]


def exchange_chips_wait(sems, parts, lands, after, name):
    n = len(parts)

    def body(*refs):
        send_sems, recv_sems = refs[2 * n:2 * n + 2]
        for cp in _exchange_copies(refs[:n], refs[n:2 * n], send_sems, recv_sems):
            cp.wait_send()
            cp.wait_recv()

    hbm_like = lambda a: pltpu.HBM(a.shape, a.dtype)
    out = pl.pallas_call(
        body, name=name, out_shape=[hbm_like(p) for p in parts] + [hbm_like(l) for l in lands],
        in_specs=[HBM] * (2 * n) + [SEM] * 2 + [BS(memory_space=pl.ANY)], out_specs=[HBM] * (2 * n),
        input_output_aliases={i: i for i in range(2 * n)},
        compiler_params=pltpu.CompilerParams(has_side_effects=_SIDE_EFFECT))(*parts, *lands, *sems, after)
    return out[n:]


def _half_block(shape2, split):
    axis, size = split
    return (size, shape2[1]) if axis == 0 else (shape2[0], size)


def add_pairs(parts, halves, splits, core, name):
    n = len(parts)

    def body(s_ref, *refs):
        for a_ref, b_ref, o_ref in zip(refs[:n], refs[n:2 * n], refs[2 * n:]):
            o_ref[...] = (a_ref[...].astype(F32) + b_ref[...].astype(F32)).astype(BF16)

    def mine(i):
        blk = (None,) + _half_block(parts[i].shape[1:], splits[i])
        if splits[i][0] == 0:
            return BS(blk, lambda q, s: (q, s[0], 0))
        return BS(blk, lambda q, s: (q, 0, s[0]))

    half_specs = [BS((None,) + h.shape[1:], lambda q, s: (q, 0, 0)) for h in halves]
    return pl.pallas_call(
        body, name=name,
        grid_spec=pltpu.PrefetchScalarGridSpec(num_scalar_prefetch=1, grid=(N_CHIPS,),
                                               in_specs=[mine(i) for i in range(n)] + half_specs, out_specs=half_specs),
        out_shape=[jax.ShapeDtypeStruct(h.shape, BF16) for h in halves], compiler_params=_arb(1))(core, *parts, *halves)


def add_fives(parts, halves, from_chips, splits, chip_core, name):
    n = len(parts)

    def body(s_ref, *refs):
        for a_ref, b_ref, p_ref, o_ref in zip(refs[:n], refs[n:2 * n], refs[2 * n:3 * n], refs[3 * n:]):
            s = a_ref[...].astype(F32) + b_ref[...].astype(F32)
            for j in range(3):
                s = s + p_ref[j].astype(F32)
            o_ref[...] = s

    def mine(i):
        blk = (None,) + _half_block(parts[i].shape[1:], splits[i])
        if splits[i][0] == 0:
            return BS(blk, lambda g, s: (s[0], s[1], 0))
        return BS(blk, lambda g, s: (s[0], 0, s[1]))

    half_specs = [BS((None,) + h.shape[1:], lambda g, s: (s[0], 0, 0)) for h in halves]
    chip_specs = [BS(p.shape, lambda g, s: (0, 0, 0)) for p in from_chips]
    out_specs = [BS(h.shape[1:], lambda g, s: (0, 0)) for h in halves]
    return pl.pallas_call(
        body, name=name,
        grid_spec=pltpu.PrefetchScalarGridSpec(num_scalar_prefetch=1, grid=(1,),
                                               in_specs=[mine(i) for i in range(n)] + half_specs + chip_specs, out_specs=out_specs),
        out_shape=[jax.ShapeDtypeStruct(h.shape[1:], F32) for h in halves], compiler_params=_arb(1))(chip_core, *parts, *halves, *from_chips)


def sum_leading(a, name):
    def body(a_ref, o_ref):
        s = a_ref[0]
        for j in range(1, a.shape[0]):
            s = s + a_ref[j]
        o_ref[...] = s

    return pl.pallas_call(body, name=name, out_shape=jax.ShapeDtypeStruct(a.shape[1:], a.dtype))(a)


def _adamw_math(w, g, m, v):
    mn = ADAM_B1 * m + (1.0 - ADAM_B1) * g
    vn = ADAM_B2 * v + (1.0 - ADAM_B2) * (g * g)
    m_hat = mn / (1.0 - ADAM_B1 ** ADAM_STEP)
    v_hat = vn / (1.0 - ADAM_B2 ** ADAM_STEP)
    return -ADAM_LR * (m_hat / (jnp.sqrt(v_hat) + ADAM_EPS) + ADAM_WD * w), mn, vn


def adamw(w, g, m, v, name):
    R, C = g.shape
    lead = (None,) * (w.ndim - 2)

    def body(w_ref, g_ref, m_ref, v_ref, d_ref, mo_ref, vo_ref):
        d_ref[...], mo_ref[...], vo_ref[...] = _adamw_math(w_ref[...], g_ref[...], m_ref[...], v_ref[...])

    wblk = BS(lead + (R, C), lambda i: (0,) * w.ndim)
    gblk = BS((R, C), lambda i: (0, 0))
    return pl.pallas_call(
        body, name=name, grid=(1,), in_specs=[wblk, gblk, wblk, wblk], out_specs=[wblk] * 3,
        out_shape=[jax.ShapeDtypeStruct(w.shape, F32)] * 3, compiler_params=_arb(1))(w, g, m, v)


def adamw_halves(w, mine, other, m, v, split, core, name):
    R, C = w.shape[-2:]
    axis, size = split
    lead = (None,) * (w.ndim - 2)
    zeros = (0,) * (w.ndim - 2)
    if axis == 0:
        tr = size if size <= 256 else next(t for t in range(256, 7, -1) if size % t == 0 and t % 8 == 0)
        nb = size // tr
        whole = BS(lead + (tr, C), lambda hi, j, s: zeros + (hi * nb + j, 0))
        part = BS((tr, C), lambda hi, j, s: (j, 0))
    else:
        nb = size // 128
        whole = BS(lead + (R, 128), lambda hi, j, s: zeros + (0, hi * nb + j))
        part = BS((R, 128), lambda hi, j, s: (0, j))

    def body(s_ref, w_ref, a_ref, b_ref, m_ref, v_ref, g_ref, d_ref, mo_ref, vo_ref):
        g = jnp.where(pl.program_id(0) == s_ref[0], a_ref[...], b_ref[...])
        g_ref[...] = g
        d_ref[...], mo_ref[...], vo_ref[...] = _adamw_math(w_ref[...], g, m_ref[...], v_ref[...])

    return pl.pallas_call(
        body, name=name,
        grid_spec=pltpu.PrefetchScalarGridSpec(num_scalar_prefetch=1, grid=(2, nb),
                                               in_specs=[whole, part, part, whole, whole], out_specs=[whole] * 4),
        out_shape=[jax.ShapeDtypeStruct(w.shape, F32)] * 4, compiler_params=_arb(2))(core, w, mine, other, m, v)


def dense_bf16(w3, name):
    R, _, K = w3.shape
    kh = K // 2

    def body(w_hbm, o_ref, buf, sem):
        cp = pltpu.make_async_copy(w_hbm.at[:, 0], buf, sem)
        cp.start()
        cp.wait()
        o_ref[0] = buf[:, :kh].astype(BF16)
        o_ref[1] = buf[:, kh:].astype(BF16)

    return pl.pallas_call(
        body, name=name, in_specs=[ANY], out_specs=BS(memory_space=pltpu.VMEM), out_shape=jax.ShapeDtypeStruct((2, R, kh), BF16),
        scratch_shapes=[pltpu.VMEM((R, K), F32), pltpu.SemaphoreType.DMA(())])(w3)


ROW_BLOCK = 184


def adamw_untiled_rows(w3, mine, other, m3, v3, name):
    R, _, K = w3.shape
    kh = K // 2
    starts = list(range(0, R, ROW_BLOCK))
    sizes = [min(ROW_BLOCK, R - s) for s in starts]
    nblk = len(starts)

    def body(w_hbm, a_ref, b_ref, m_hbm, v_hbm, g_hbm, d_hbm, mo_hbm, vo_hbm,
             wbuf, mbuf, vbuf, gbuf, dbuf, mobuf, vobuf, in_sems, out_sems):
        first = lax.axis_index("c") == 0
        ins = []
        for k, (r0, n) in enumerate(zip(starts, sizes)):
            rows = pl.ds(r0, n)
            cps = [pltpu.make_async_copy(src.at[rows, 0], dst.at[rows], in_sems.at[3 * k + i])
                   for i, (src, dst) in enumerate(((w_hbm, wbuf), (m_hbm, mbuf), (v_hbm, vbuf)))]
            for cp in cps:
                cp.start()
            ins.append(cps)

        def update(rows):
            a, b = a_ref[rows, :], b_ref[rows, :]
            g = jnp.concatenate([jnp.where(first, a, b), jnp.where(first, b, a)], axis=1)
            gbuf[rows, :] = g
            dbuf[rows, :], mobuf[rows, :], vobuf[rows, :] = _adamw_math(wbuf[rows, :], g, mbuf[rows, :], vbuf[rows, :])

        outs = []
        for k, (r0, n) in enumerate(zip(starts, sizes)):
            for cp in ins[k]:
                cp.wait()
            groups, tail = n // 8, n % 8

            def group(i, carry, r0=r0):
                update(pl.ds(pl.multiple_of(r0 + i * 8, 8), 8))
                return carry

            lax.fori_loop(0, groups, group, 0)
            if tail:
                update(pl.ds(r0 + groups * 8, tail))
            rows = pl.ds(r0, n)
            cps = [pltpu.make_async_copy(src.at[rows], dst.at[rows, 0], out_sems.at[4 * k + i])
                   for i, (src, dst) in enumerate(((gbuf, g_hbm), (dbuf, d_hbm), (mobuf, mo_hbm), (vobuf, vo_hbm)))]
            for cp in cps:
                cp.start()
            outs += cps
        for cp in outs:
            cp.wait()

    vmem = BS(memory_space=pltpu.VMEM)
    return pl.pallas_call(
        body, name=name, in_specs=[ANY, vmem, vmem, ANY, ANY], out_specs=[ANY] * 4,
        out_shape=[jax.ShapeDtypeStruct(w3.shape, F32)] * 4,
        scratch_shapes=[pltpu.VMEM((R, K), F32)] * 7 + [pltpu.SemaphoreType.DMA((3 * nblk,)), pltpu.SemaphoreType.DMA((4 * nblk,))])(
            w3, mine, other, m3, v3)


def adamw_w_q_b(w, mine, other, m, v, name):
    def body(w_ref, a_ref, b_ref, m_ref, v_ref, g_ref, d_ref, mo_ref, vo_ref):
        first = lax.axis_index("c") == 0
        lo = jnp.where(first, a_ref[...], b_ref[...])
        hi = jnp.where(first, b_ref[...], a_ref[...])
        g = jnp.concatenate([lo, hi[0:32], hi[64:96]], axis=0)
        g_ref[...] = g
        d_ref[...], mo_ref[...], vo_ref[...] = _adamw_math(w_ref[...], g, m_ref[...], v_ref[...])

    return pl.pallas_call(body, name=name, out_shape=[jax.ShapeDtypeStruct(w.shape, F32)] * 4)(w, mine, other, m, v)


def local_step(x, mem, positions, tgt, norm_in, w_in, late_weights, big_grads_ready, q_a_norm, kv_a_norm, gdn_conv, gdn_a_log,
               gdn_dt_bias, gdn_norm, mem_norm, norm_final):
    B, S, D = x.shape
    M = mem.shape[1]
    T = B * S
    N = S // CHUNK
    x2d = x.reshape(T, D)
    mem2d = mem.reshape(B * M, D)
    tgt2d = tgt.reshape(T, D)

    wp = w_in
    alog_row, dt_row = _lane_row(gdn_a_log), _lane_row(gdn_dt_bias)

    half = MLA_ROPE // 2
    inv_freq = 1.0 / (ROPE_THETA ** (jnp.arange(half, dtype=F32) / half))
    z32 = jnp.zeros((half,), F32)
    o32 = jnp.ones((half,), F32)
    inv_row = jnp.concatenate([inv_freq, z32, inv_freq, z32]).reshape(1, 128)
    sgn_row = jnp.concatenate([-o32, z32, o32, z32]).reshape(1, 128)
    msk_row = jnp.concatenate([o32, z32, o32, z32]).reshape(1, 128)
    cos_t, sin_t = rope_tables(positions.reshape(T, 1), inv_row, sgn_row, msk_row)

    h = rms_fwd(x2d, norm_in, "rms_in")
    P = mm(h, wp, "nt", F32, "in_proj", bm=512, bn=1536, n_outer=True)
    wq, wkv, w_mem_kv, w_out = late_weights(P)
    Q, K, V, qn, kvn = mla_prep(P, q_a_norm, kv_a_norm, wq, wkv, cos_t, sin_t)
    o_mla, lse = mla_attn_fwd(Q, K, V, B, S)
    memn = rms_fwd(mem2d, mem_norm, "rms_mem")
    MKV = mm(memn, w_mem_kv, "nn", BF16, "mem_kv_proj")
    o_mem = mem_attn_fwd(P, MKV, B, S, M)
    qkv = gdn_prep_fwd(P, gdn_conv, B, S)
    GB = gdn_gate_fwd(P, alog_row, dt_row, B, S)
    Grow = jnp.transpose(GB[:, :N_HEADS].reshape(B, N, CHUNK, N_HEADS), (0, 3, 1, 2))
    U, W, Tinv, A = gdn_chunk_fwd(qkv, GB, Grow, B, S)
    qkv3, GB3 = qkv.reshape(B, S, GDN_QKV), GB.reshape(B, S, 128)
    W3 = W.reshape(B, S, 512)
    o_gdn3, Vn3, St = gdn_scan_fwd(qkv3, U.reshape(B, S, 512), W3, GB3, A, B, S)
    o_gdn = o_gdn3.reshape(T, 512)
    mixed, dx2, dx2b, sq, g_norm_final = merge_fwd(o_mla, o_gdn, o_mem, P, x2d, tgt2d, w_out, gdn_norm, norm_final.reshape(1, D))

    g_w_out = mm(mixed, dx2b, "tn", BF16, "grad_w_out")
    dgate, do_mla, do_gdn, do_mem, g_gdn_norm = merge_bwd(dx2b, o_mla, o_gdn, o_mem, P, w_out, gdn_norm)

    dmemq, dMKV = mem_attn_bwd(P, MKV, do_mem, B, S, M)
    g_w_mem_kv = mm(memn, dMKV, "tn", BF16, "grad_w_mem_kv")
    dmemn = mm(dMKV, w_mem_kv, "nt", F32, "d_memn")
    g_mem_norm = gain_grad(mem2d, dmemn, "grad_mem_norm")

    dU3, dW3, dQ13, dK13, dA, dG13 = gdn_scan_bwd(do_gdn.reshape(B, S, 512), qkv3, W3, Vn3, GB3, A, St, B, S)
    r2 = lambda a: a.reshape(T, a.shape[-1])
    dqkv, dGB = gdn_chunk_bwd(qkv, GB, Grow, Tinv, dA, r2(dU3), r2(dW3), r2(dQ13), r2(dK13), r2(dG13), B, S)
    dPg, g_conv = gdn_prep_bwd(P, dqkv, gdn_conv, B, S)
    dab, g_ab = gdn_gate_bwd(P, dGB, alog_row, dt_row, B, S)

    dQ, dK, dV = mla_attn_bwd(Q, K, V, o_mla, do_mla, lse, B, S)
    dq_lin, dkv_lin, dkr = mla_post_bwd(dQ, dK, dV, cos_t, sin_t)
    dqn = mm(dq_lin, wq, "nn", F32, "d_qn")
    dkvn = mm(dkv_lin, wkv, "nt", F32, "d_kvn")
    g_wq = mm(dq_lin, qn, "tn", BF16, "grad_w_q_b")
    g_wkv = mm(kvn, dkv_lin, "tn", BF16, "grad_w_kv_b")
    dPm, g_q_a_norm, g_kv_a_norm = mla_norm_bwd(P, dqn, dkvn, dkr, dab, q_a_norm, kv_a_norm)

    dP = [dPm, dmemq, dPg, dgate]
    g_wp = mm_cols_tn(dP, h, BF16, "grad_w_in")
    started = big_grads_ready(dict(w_in=g_wp, w_q_b=g_wq, w_kv_b=g_wkv, w_mem_kv=g_w_mem_kv, w_out=g_w_out))
    dh = mm_cols_nn(dP, wp, F32, "d_h", started)
    grad_x, g_norm_in = in_norm_bwd(x2d, dh, dx2, norm_in)

    grads = dict(
        norm_in=g_norm_in, q_a_norm=g_q_a_norm, kv_a_norm=g_kv_a_norm, gdn_conv=g_conv,
        gdn_a_log=g_ab[0:1, :N_HEADS], gdn_dt_bias=g_ab[1:2, :N_HEADS], gdn_norm=g_gdn_norm,
        mem_norm=g_mem_norm, norm_final=g_norm_final)
    return sq, grad_x.reshape(B, S, D), grads


def kernel(x, mem, positions, norm_in, w_in, q_a_norm, w_q_b, kv_a_norm, w_kv_b, gdn_conv, gdn_a_log, gdn_dt_bias, gdn_norm, mem_norm, w_mem_kv, w_out, norm_final, loss_target, m_norm_in, m_w_in, m_q_a_norm, m_w_q_b, m_kv_a_norm, m_w_kv_b, m_gdn_conv, m_gdn_a_log, m_gdn_dt_bias, m_gdn_norm, m_mem_norm, m_w_mem_kv, m_w_out, m_norm_final, v_norm_in, v_w_in, v_q_a_norm, v_w_q_b, v_kv_a_norm, v_w_kv_b, v_gdn_conv, v_gdn_a_log, v_gdn_dt_bias, v_gdn_norm, v_mem_norm, v_w_mem_kv, v_w_out, v_norm_final):
    B = x.shape[0]
    cx, cy, cc = lax.axis_index("x"), lax.axis_index("y"), lax.axis_index("c")
    chip = 2 * cx + cy

    big_names = ("w_in", "w_q_b", "w_kv_b", "w_mem_kv", "w_out")
    rows_major = lambda a: jnp.transpose(a, (2, 0, 1))
    w_in3, m_in3, v_in3 = rows_major(w_in), rows_major(m_w_in), rows_major(v_w_in)
    w_qb_t, m_qb_t, v_qb_t = jnp.transpose(w_q_b[0]), jnp.transpose(m_w_q_b[0]), jnp.transpose(v_w_q_b[0])
    z32 = jnp.zeros((32, Q_LORA), BF16)
    qb_bf = w_qb_t.astype(BF16)
    qb_padded = jnp.concatenate([qb_bf[:160], z32, qb_bf[160:], z32])
    shards = [dense_bf16(w_in3, "w_in_bf16"), qb_padded, w_kv_b[0].astype(BF16), w_mem_kv[0].astype(BF16), w_out[0].astype(BF16)]
    splits = [(1, D_MODEL // 2)] + [(0, s.shape[0] // 2) for s in shards[1:]]
    (g_in,) = allgather_chips(shards[:1], ["lead"], "allgather_w_in")
    conv_all = allgather_devices(gdn_conv[0], "allgather_conv")
    conv_cols = gdn_conv.shape[2]
    conv_full = jnp.transpose(conv_all[0::2], (1, 0, 2)).reshape(GDN_CONV, N_CHIPS * conv_cols)
    late_sems, late_shards, late_lands, _ = late_gather_start(shards[1:], "late_gather_start")
    late_shapes = [(N_CHIPS,) + s.shape for s in shards[1:]]

    def late_weights(after):
        g_qb, g_kvb, g_mem, g_out_w = late_gather_wait(late_sems, late_shards, late_lands, after, "late_gather_wait")
        return (g_qb.reshape(-1, Q_LORA), _perm_w_kv_b(g_kvb), g_mem.reshape(-1, g_mem.shape[2]),
                g_out_w.reshape(-1, g_out_w.shape[2]))

    core = jnp.stack([cc]).astype(jnp.int32)
    chip_core = jnp.stack([chip, cc]).astype(jnp.int32)
    exchange = {}

    def big_grads_ready(gb):
        parts = [unpad_w_in_t(gb["w_in"]), gb["w_q_b"].reshape(late_shapes[0]), _unperm_w_kv_b(gb["w_kv_b"]),
                 gb["w_mem_kv"].reshape(late_shapes[2]), gb["w_out"].reshape(late_shapes[3])]
        from_sibling = swap_sibling(parts, "rs_sibling_partial", splits)
        chip_sums = add_pairs(parts, from_sibling, splits, core, "rs_add_sibling")
        sems, sums_thru, lands, token = exchange_chips_start(chip_sums, "rs_exchange_start")
        exchange.update(parts=parts, from_sibling=from_sibling, sems=sems, sums=sums_thru, lands=lands)
        return token

    sq, grad_x, g = local_step(x, mem, positions, loss_target, norm_in, pad_w_in_t(g_in), late_weights, big_grads_ready, q_a_norm,
                               kv_a_norm, conv_full, gdn_a_log, gdn_dt_bias, gdn_norm, mem_norm, norm_final)

    small_names = ("norm_in", "q_a_norm", "kv_a_norm", "gdn_a_log", "gdn_dt_bias", "gdn_norm", "mem_norm", "norm_final")
    small = dict(norm_in=norm_in, q_a_norm=q_a_norm, kv_a_norm=kv_a_norm, gdn_a_log=gdn_a_log, gdn_dt_bias=gdn_dt_bias,
                 gdn_norm=gdn_norm, mem_norm=mem_norm, norm_final=norm_final)
    m_small = dict(norm_in=m_norm_in, q_a_norm=m_q_a_norm, kv_a_norm=m_kv_a_norm, gdn_a_log=m_gdn_a_log,
                   gdn_dt_bias=m_gdn_dt_bias, gdn_norm=m_gdn_norm, mem_norm=m_mem_norm, norm_final=m_norm_final)
    v_small = dict(norm_in=v_norm_in, q_a_norm=v_q_a_norm, kv_a_norm=v_kv_a_norm, gdn_a_log=v_gdn_a_log,
                   gdn_dt_bias=v_gdn_dt_bias, gdn_norm=v_gdn_norm, mem_norm=v_mem_norm, norm_final=v_norm_final)
    rows = lambda d: jnp.stack([jnp.pad(d[n].reshape(-1), (0, 1024 - d[n].size)) for n in small_names])
    conv_rows = GDN_CONV * GDN_QKV // 1024
    g_block = jnp.concatenate([rows(g), g["gdn_conv"].reshape(conv_rows, 1024), sq, jnp.zeros((16 - 9 - conv_rows, 1024), F32)])
    g_block = sum_leading(allgather_devices(g_block, "allgather_small_grads"), "sum_small_grads")
    g_small_rows = g_block[:8]
    loss = 0.5 * jnp.sum(g_block[8 + conv_rows]) / D_MODEL
    g_conv = lax.dynamic_slice_in_dim(g_block[8:8 + conv_rows].reshape(GDN_CONV, GDN_QKV), chip * conv_cols, conv_cols, axis=1)
    d_s, m_s, v_s = adamw(rows(small), g_small_rows, rows(m_small), rows(v_small), "adamw_small")
    unrow = lambda r: {n: r[i, :small[n].size].reshape(small[n].shape) for i, n in enumerate(small_names)}
    g_out, d_out, m_out, v_out = unrow(g_small_rows), unrow(d_s), unrow(m_s), unrow(v_s)

    from_chips = exchange_chips_wait(exchange["sems"], exchange["sums"], exchange["lands"], d_s, "rs_exchange_wait")
    my_half = add_fives(exchange["parts"], exchange["from_sibling"], from_chips, splits, chip_core, "rs_add_chips")
    other_half = swap_sibling(my_half, "rs_sibling_final")

    d_out["gdn_conv"], m_out["gdn_conv"], v_out["gdn_conv"] = adamw(gdn_conv, g_conv, m_gdn_conv, v_gdn_conv, "adamw_gdn_conv")
    g_out["gdn_conv"] = g_conv[None]
    res = adamw_untiled_rows(w_in3, my_half[0], other_half[0], m_in3, v_in3, "adamw_w_in")
    g_out["w_in"], d_out["w_in"], m_out["w_in"], v_out["w_in"] = [jnp.transpose(r, (1, 2, 0)) for r in res]
    res = adamw_w_q_b(w_qb_t, my_half[1], other_half[1], m_qb_t, v_qb_t, "adamw_w_q_b")
    g_out["w_q_b"], d_out["w_q_b"], m_out["w_q_b"], v_out["w_q_b"] = [jnp.transpose(r)[None] for r in res]
    rest = dict(w_kv_b=(w_kv_b, m_w_kv_b, v_w_kv_b), w_mem_kv=(w_mem_kv, m_w_mem_kv, v_w_mem_kv), w_out=(w_out, m_w_out, v_w_out))
    for i, n in enumerate(big_names):
        if n in rest:
            w_n, m_n, v_n = rest[n]
            g_out[n], d_out[n], m_out[n], v_out[n] = adamw_halves(w_n, my_half[i], other_half[i], m_n, v_n, splits[i], core, "adamw_" + n)

    order = ("norm_in", "w_in", "q_a_norm", "w_q_b", "kv_a_norm", "w_kv_b", "gdn_conv", "gdn_a_log", "gdn_dt_bias",
             "gdn_norm", "mem_norm", "w_mem_kv", "w_out", "norm_final")
    return (loss, grad_x, *[g_out[n] for n in order], *[d_out[n] for n in order], *[m_out[n] for n in order],
            *[v_out[n] for n in order])
```

```python
import functools
import math

import jax
import jax.numpy as jnp
import numpy as np
from jax import lax
from jax.experimental import pallas as pl
from jax.experimental.pallas import tpu as pltpu

F32 = jnp.float32
BF16 = jnp.bfloat16
BS = pl.BlockSpec

D_MODEL = 1024
N_HEADS = 4
MLA_NOPE, MLA_ROPE, MLA_V = 128, 64, 128
Q_LORA, KV_LORA = 384, 256
ROPE_THETA = 10000.0
GDN_DK = GDN_DV = 128
GDN_CONV = 4
CHUNK = 64
MEM_DH = 128
D_MIX = 1536
GDN_QKV = 1536
D_IN = 4296
EPS = 1e-6
ADAM_LR, ADAM_B1, ADAM_B2, ADAM_EPS, ADAM_WD, ADAM_STEP = 0.001, 0.9, 0.999, 1e-08, 0.01, 10

OFF_MLA = 0
OFF_MEMQ = 1024
OFF_GDN = 1536
OFF_GATE = 3072
N_PAD = 4608
HEAD_PAD = 256
MLA_SCALE = (MLA_NOPE + MLA_ROPE) ** -0.5
MEM_SCALE = MEM_DH ** -0.5
GDN_SCALE = GDN_DK ** -0.5
NEG = -1e30

NN = ((1,), (0,))
NT = ((1,), (1,))
TN = ((0,), (0,))


def _dot(a, b, dims):
    return lax.dot_general(a, b, (dims, ((), ())), preferred_element_type=F32)


def _bdot(spec, a, b, precision=None):
    return jnp.einsum(spec, a, b, preferred_element_type=F32, precision=precision)


def _arb(n):
    return pltpu.CompilerParams(dimension_semantics=("arbitrary",) * n)


def _sigmoid(x):
    return 1.0 / (1.0 + jnp.exp(-x))


def _softplus(z):
    return jnp.maximum(z, 0.0) + jnp.log(1.0 + jnp.exp(-jnp.abs(z)))


def _rope(t, cos_row, sin_row):
    return t * cos_row + pltpu.roll(t, 64, 1) * sin_row


def _rope_bwd(d, cos_row, sin_row):
    return d * cos_row + pltpu.roll(d * sin_row, 64, 1)


def rms_fwd(x, gain, name, tm=512):
    T, n = x.shape
    tm = min(tm, T)

    def body(x_ref, g_ref, o_ref):
        xv = x_ref[...]
        r = lax.rsqrt(jnp.mean(xv * xv, axis=-1, keepdims=True) + EPS)
        o_ref[...] = (xv * r * g_ref[...]).astype(BF16)

    return pl.pallas_call(
        body, name=name, grid=(T // tm,),
        in_specs=[BS((tm, n), lambda i: (i, 0)), BS((1, n), lambda i: (0, 0))],
        out_specs=BS((tm, n), lambda i: (i, 0)),
        out_shape=jax.ShapeDtypeStruct((T, n), BF16), compiler_params=_arb(1))(x, gain)


def mm(a, b, kind, out_dtype, name, bm=512, bn=None, n_outer=False):
    if kind == "nn":
        (M, K), (_, N) = a.shape, b.shape
    elif kind == "nt":
        (M, K), (N, _) = a.shape, b.shape
    else:
        (K, M), (_, N) = a.shape, b.shape
    bm, bn = min(bm, M), min(bn or N, N)
    assert M % bm == 0 and N % bn == 0, (name, M, N, K)
    ij = (lambda g0, g1: (g1, g0)) if n_outer else (lambda g0, g1: (g0, g1))
    a_spec = BS((K, bm), lambda g0, g1: (0, ij(g0, g1)[0])) if kind == "tn" else BS((bm, K), lambda g0, g1: (ij(g0, g1)[0], 0))
    once = dict(pipeline_mode=pl.Buffered(1)) if bn == N else {}
    b_spec = (BS((bn, K), lambda g0, g1: (ij(g0, g1)[1], 0), **once) if kind == "nt"
              else BS((K, bn), lambda g0, g1: (0, ij(g0, g1)[1]), **once))
    dims = {"nn": NN, "nt": NT, "tn": TN}[kind]

    def body(a_ref, b_ref, o_ref):
        o_ref[...] = _dot(a_ref[...].astype(BF16), b_ref[...].astype(BF16), dims).astype(out_dtype)

    grid = (N // bn, M // bm) if n_outer else (M // bm, N // bn)
    return pl.pallas_call(
        body, name=name, grid=grid, in_specs=[a_spec, b_spec], out_specs=BS((bm, bn), lambda g0, g1: ij(g0, g1)),
        out_shape=jax.ShapeDtypeStruct((M, N), out_dtype), compiler_params=_arb(2))(a, b)


def mm_cols_nn(pieces, b, out_dtype, name, after, bm=512):
    M, N = pieces[0].shape[0], b.shape[1]
    bm = min(bm, M)
    n = len(pieces)
    widths = [p.shape[1] for p in pieces]
    offs = [sum(widths[:i]) for i in range(n)]

    def body(*refs):
        b_ref, o_ref = refs[n], refs[-1]
        acc = None
        for a_ref, off, w in zip(refs[:n], offs, widths):
            d = _dot(a_ref[...], b_ref[off:off + w, :], NN)
            acc = d if acc is None else acc + d
        o_ref[...] = acc.astype(out_dtype)

    return pl.pallas_call(
        body, name=name, grid=(M // bm,),
        in_specs=[BS((bm, w), lambda i: (i, 0)) for w in widths]
        + [BS(b.shape, lambda i: (0, 0), pipeline_mode=pl.Buffered(1)), BS(memory_space=pl.ANY)],
        out_specs=BS((bm, N), lambda i: (i, 0)), out_shape=jax.ShapeDtypeStruct((M, N), out_dtype),
        compiler_params=_arb(1))(*pieces, b, after)


def mm_cols_tn(pieces, b, out_dtype, name, bm=512):
    K, N = b.shape
    tiles = [p.shape[1] // bm for p in pieces]
    firsts = [sum(tiles[:i]) for i in range(len(tiles))]

    def body(*refs):
        b_ref, o_ref = refs[-2], refs[-1]
        i = pl.program_id(0)
        for a_ref, t0, n in zip(refs[:-2], firsts, tiles):
            @pl.when((i >= t0) & (i < t0 + n))
            def _(a_ref=a_ref):
                o_ref[...] = _dot(a_ref[...], b_ref[...], TN).astype(out_dtype)

    a_specs = [BS((K, bm), lambda i, t0=t0, n=n: (0, jnp.clip(i - t0, 0, n - 1))) for t0, n in zip(firsts, tiles)]
    return pl.pallas_call(
        body, name=name, grid=(sum(tiles),),
        in_specs=a_specs + [BS(b.shape, lambda i: (0, 0), pipeline_mode=pl.Buffered(1))],
        out_specs=BS((bm, N), lambda i: (i, 0)), out_shape=jax.ShapeDtypeStruct((sum(tiles) * bm, N), out_dtype),
        compiler_params=_arb(1))(*pieces, b)


def rope_tables(pos_col, inv_row, sgn_row, msk_row, tm=512):
    T = pos_col.shape[0]
    tm = min(tm, T)

    def body(p_ref, inv_ref, sgn_ref, msk_ref, c_ref, s_ref):
        ang = p_ref[...].astype(F32) * inv_ref[...]
        c_ref[...] = jnp.cos(ang) * msk_ref[...]
        s_ref[...] = jnp.sin(ang) * sgn_ref[...]

    row = BS((1, 128), lambda i: (0, 0))
    return pl.pallas_call(
        body, name="rope_tables", grid=(T // tm,),
        in_specs=[BS((tm, 1), lambda i: (i, 0)), row, row, row],
        out_specs=[BS((tm, 128), lambda i: (i, 0))] * 2,
        out_shape=[jax.ShapeDtypeStruct((T, 128), F32)] * 2, compiler_params=_arb(1))(pos_col, inv_row, sgn_row, msk_row)


def mla_prep(P, gq, gkv, wq, wkv, cos_t, sin_t, tm=512):
    T = P.shape[0]
    tm = min(tm, T)

    def body(p_ref, gq_ref, gkv_ref, wq_ref, wkv_ref, c_ref, s_ref, q_ref, k_ref, v_ref, qn_ref, kvn_ref):
        p = p_ref[...]
        cq, ckv, kr = p[:, :Q_LORA], p[:, Q_LORA:Q_LORA + KV_LORA], p[:, 640:768]
        qn = (cq * lax.rsqrt(jnp.mean(cq * cq, axis=-1, keepdims=True) + EPS) * gq_ref[...]).astype(BF16)
        kvn = (ckv * lax.rsqrt(jnp.mean(ckv * ckv, axis=-1, keepdims=True) + EPS) * gkv_ref[...]).astype(BF16)
        qn_ref[...] = qn
        kvn_ref[...] = kvn
        q = _dot(qn, wq_ref[...], NT)
        kv = _dot(kvn, wkv_ref[...], NN)
        cos_row, sin_row = c_ref[...], s_ref[...]
        krr = _rope(kr, cos_row, sin_row).astype(BF16)
        for h in range(N_HEADS):
            lo = h * HEAD_PAD
            q_ref[:, lo:lo + 128] = (q[:, lo:lo + 128] * MLA_SCALE).astype(BF16)
            q_ref[:, lo + 128:lo + 256] = (_rope(q[:, lo + 128:lo + 256], cos_row, sin_row) * MLA_SCALE).astype(BF16)
            k_ref[:, lo:lo + 128] = kv[:, h * 128:(h + 1) * 128].astype(BF16)
            k_ref[:, lo + 128:lo + 256] = krr
            v_ref[:, lo:lo + 128] = kv[:, 512 + h * 128:512 + (h + 1) * 128].astype(BF16)
            v_ref[:, lo + 128:lo + 256] = jnp.ones((tm, 128), BF16)

    full = lambda r, c: BS((r, c), lambda i: (0, 0))
    rowb = lambda c: BS((tm, c), lambda i: (i, 0))
    return pl.pallas_call(
        body, name="mla_prep", grid=(T // tm,),
        in_specs=[rowb(1024), full(1, Q_LORA), full(1, KV_LORA), full(1024, Q_LORA), full(KV_LORA, 1024), rowb(128), rowb(128)],
        out_specs=[rowb(1024), rowb(1024), rowb(1024), rowb(Q_LORA), rowb(KV_LORA)],
        out_shape=[jax.ShapeDtypeStruct((T, 1024), BF16), jax.ShapeDtypeStruct((T, 1024), BF16),
                   jax.ShapeDtypeStruct((T, 1024), BF16), jax.ShapeDtypeStruct((T, Q_LORA), BF16),
                   jax.ShapeDtypeStruct((T, KV_LORA), BF16)],
        compiler_params=_arb(1))(P, gq, gkv, wq, wkv, cos_t, sin_t)


ATTN_HEADS_PER_STEP = 2
ATTN_STRIP = 32


def mla_attn_fwd(Q, K, V, B, S, tq=512, hp=ATTN_HEADS_PER_STEP):
    T = B * S
    tq = min(tq, S)
    nq = S // tq

    rs = min(ATTN_STRIP, tq)

    def body(q_ref, k_ref, v_ref, o_ref, lse_ref, m_s, acc_s, s_s, p_s, a_s):
        i = pl.program_id(2)
        m_s[...] = jnp.full_like(m_s, NEG)
        acc_s[...] = jnp.zeros_like(acc_s)

        def blk(j, masked):
            rows = pl.ds(pl.multiple_of(j * tq, tq), tq)
            for h in range(hp):
                hq = slice(h * HEAD_PAD, (h + 1) * HEAD_PAD)
                s_s[h] = _dot(q_ref[:, hq], k_ref[rows, hq], NT)
            for h in range(hp):
                for r0 in range(0, tq, rs):
                    rr = slice(r0, r0 + rs)
                    sv = s_s[h, rr, :]
                    if masked:
                        r = r0 + lax.broadcasted_iota(jnp.int32, (rs, tq), 0)
                        c = lax.broadcasted_iota(jnp.int32, (rs, tq), 1)
                        sv = jnp.where(r >= c, sv, NEG)
                    m_prev = m_s[h, rr, :]
                    m_new = jnp.maximum(m_prev, jnp.max(sv, axis=1, keepdims=True))
                    p_s[h, rr, :] = jnp.exp(sv - m_new).astype(BF16)
                    a_s[h, rr, :] = jnp.exp(m_prev - m_new)
                    m_s[h, rr, :] = m_new
            for h in range(hp):
                hq = slice(h * HEAD_PAD, (h + 1) * HEAD_PAD)
                acc_s[h] = a_s[h] * acc_s[h] + _dot(p_s[h], v_ref[rows, hq], NN)

        def loop(j, c):
            blk(j, False)
            return c

        lax.fori_loop(0, i, loop, 0)
        blk(i, True)
        for h in range(hp):
            den = acc_s[h, :, 128:256]
            o_ref[:, h * 128:(h + 1) * 128] = acc_s[h, :, 0:128] / den
            lse_ref[h] = m_s[h] + jnp.log(den[:, 0:1])

    return pl.pallas_call(
        body, name="mla_attn_fwd", grid=(B, N_HEADS // hp, nq),
        in_specs=[BS((tq, hp * HEAD_PAD), lambda b, h, i: (b * nq + i, h)),
                  BS((S, hp * HEAD_PAD), lambda b, h, i: (b, h)),
                  BS((S, hp * HEAD_PAD), lambda b, h, i: (b, h))],
        out_specs=[BS((tq, hp * 128), lambda b, h, i: (b * nq + i, h)),
                   BS((hp, tq, 1), lambda b, h, i: (h, b * nq + i, 0))],
        out_shape=[jax.ShapeDtypeStruct((T, 512), F32), jax.ShapeDtypeStruct((N_HEADS, T, 1), F32)],
        scratch_shapes=[pltpu.VMEM((hp, tq, 1), F32), pltpu.VMEM((hp, tq, HEAD_PAD), F32), pltpu.VMEM((hp, tq, tq), F32),
                        pltpu.VMEM((hp, tq, tq), BF16), pltpu.VMEM((hp, tq, 1), F32)],
        compiler_params=_arb(3))(Q, K, V)


def mla_attn_bwd(Q, K, V, O, dO, LSE, B, S, tq=512, hp=ATTN_HEADS_PER_STEP):
    T = B * S
    tq = min(tq, S)
    nq = S // tq

    def body(q_ref, k_ref, v_ref, o_ref, do_ref, lse_ref, dq_ref, dk_ref, dv_ref, delta_s, dk_s, dv_s):
        j = pl.program_id(2)

        @pl.when(j == 0)
        def _():
            dq_ref[...] = jnp.zeros_like(dq_ref)
            for h in range(hp):
                sl = slice(h * 128, (h + 1) * 128)
                delta_s[h] = jnp.sum(do_ref[:, sl] * o_ref[:, sl], axis=1, keepdims=True)

        dk_s[...] = jnp.zeros_like(dk_s)
        dv_s[...] = jnp.zeros_like(dv_s)

        def step(i, c):
            rows = pl.ds(pl.multiple_of(i * tq, tq), tq)
            r = i * tq + lax.broadcasted_iota(jnp.int32, (tq, tq), 0)
            cc = j * tq + lax.broadcasted_iota(jnp.int32, (tq, tq), 1)
            causal = r >= cc
            for h in range(hp):
                sq, sv = slice(h * HEAD_PAD, (h + 1) * HEAD_PAD), slice(h * 128, (h + 1) * 128)
                q = q_ref[rows, sq]
                k = k_ref[:, sq]
                do = do_ref[rows, sv].astype(BF16)
                s = _dot(q, k, NT)
                p = jnp.where(causal, jnp.exp(s - lse_ref[h, rows, :]), 0.0)
                dv_s[:, sv] += _dot(p.astype(BF16), do, TN)
                dp = _dot(do, v_ref[:, h * HEAD_PAD:h * HEAD_PAD + 128], NT)
                ds = (p * (dp - delta_s[h, rows, :])).astype(BF16)
                dk_s[:, sq] += _dot(ds, q, TN)
                dq_ref[rows, sq] += _dot(ds, k, NN)
            return c

        lax.fori_loop(j, nq, step, 0)
        dk_ref[...] = dk_s[...]
        dv_ref[...] = dv_s[...]

    seq = lambda c: BS((S, c), lambda b, h, j: (b, h))
    blk = lambda c: BS((tq, c), lambda b, h, j: (b * nq + j, h))
    return pl.pallas_call(
        body, name="mla_attn_bwd", grid=(B, N_HEADS // hp, nq),
        in_specs=[seq(hp * HEAD_PAD), blk(hp * HEAD_PAD), blk(hp * HEAD_PAD), seq(hp * 128), seq(hp * 128),
                  BS((hp, S, 1), lambda b, h, j: (h, b, 0))],
        out_specs=[seq(hp * HEAD_PAD), blk(hp * HEAD_PAD), blk(hp * 128)],
        out_shape=[jax.ShapeDtypeStruct((T, 1024), F32), jax.ShapeDtypeStruct((T, 1024), F32),
                   jax.ShapeDtypeStruct((T, 512), F32)],
        scratch_shapes=[pltpu.VMEM((hp, S, 1), F32), pltpu.VMEM((tq, hp * HEAD_PAD), F32), pltpu.VMEM((tq, hp * 128), F32)],
        compiler_params=_arb(3))(Q, K, V, O, dO, LSE)


def mla_post_bwd(dQ, dK, dV, cos_t, sin_t, tm=512):
    T = dQ.shape[0]
    tm = min(tm, T)

    def body(dq_ref, dk_ref, dv_ref, c_ref, s_ref, ql_ref, kvl_ref, kr_ref):
        cos_row, sin_row = c_ref[...], s_ref[...]
        kr = jnp.zeros((tm, 128), F32)
        for h in range(N_HEADS):
            lo = h * HEAD_PAD
            ql_ref[:, lo:lo + 128] = (dq_ref[:, lo:lo + 128] * MLA_SCALE).astype(BF16)
            ql_ref[:, lo + 128:lo + 256] = (_rope_bwd(dq_ref[:, lo + 128:lo + 256], cos_row, sin_row) * MLA_SCALE).astype(BF16)
            kvl_ref[:, h * 128:(h + 1) * 128] = dk_ref[:, lo:lo + 128].astype(BF16)
            kr = kr + dk_ref[:, lo + 128:lo + 256]
        kvl_ref[:, 512:] = dv_ref[...].astype(BF16)
        kr_ref[...] = _rope_bwd(kr, cos_row, sin_row)

    rowb = lambda c: BS((tm, c), lambda i: (i, 0))
    return pl.pallas_call(
        body, name="mla_post_bwd", grid=(T // tm,),
        in_specs=[rowb(1024), rowb(1024), rowb(512), rowb(128), rowb(128)],
        out_specs=[rowb(1024), rowb(1024), rowb(128)],
        out_shape=[jax.ShapeDtypeStruct((T, 1024), BF16), jax.ShapeDtypeStruct((T, 1024), BF16),
                   jax.ShapeDtypeStruct((T, 128), F32)],
        compiler_params=_arb(1))(dQ, dK, dV, cos_t, sin_t)


def mla_norm_bwd(P, dqn, dkvn, dkr, dab, gq, gkv, tm=512):
    T = P.shape[0]
    tm = min(tm, T)

    def norm_bwd(x, dy, g):
        r = lax.rsqrt(jnp.mean(x * x, axis=-1, keepdims=True) + EPS)
        xh = x * r
        dxh = dy * g
        return r * (dxh - xh * jnp.mean(dxh * xh, axis=-1, keepdims=True)), jnp.sum(dy * xh, axis=0, keepdims=True)

    def body(p_ref, dqn_ref, dkvn_ref, dkr_ref, dab_ref, gq_ref, gkv_ref, o_ref, aq_ref, akv_ref):
        @pl.when(pl.program_id(0) == 0)
        def _():
            aq_ref[...] = jnp.zeros_like(aq_ref)
            akv_ref[...] = jnp.zeros_like(akv_ref)

        dcq, ggq = norm_bwd(p_ref[:, :Q_LORA], dqn_ref[...], gq_ref[...])
        dckv, ggkv = norm_bwd(p_ref[:, Q_LORA:640], dkvn_ref[...], gkv_ref[...])
        aq_ref[...] += ggq
        akv_ref[...] += ggkv
        o_ref[:, :Q_LORA] = dcq.astype(BF16)
        o_ref[:, Q_LORA:640] = dckv.astype(BF16)
        o_ref[:, 640:768] = dkr_ref[...].astype(BF16)
        o_ref[:, 768:896] = dab_ref[...]
        o_ref[:, 896:1024] = jnp.zeros((tm, 128), BF16)

    rowb = lambda c: BS((tm, c), lambda i: (i, 0))
    full = lambda c: BS((1, c), lambda i: (0, 0))
    return pl.pallas_call(
        body, name="mla_norm_bwd", grid=(T // tm,),
        in_specs=[rowb(1024), rowb(Q_LORA), rowb(KV_LORA), rowb(128), rowb(128), full(Q_LORA), full(KV_LORA)],
        out_specs=[rowb(1024), full(Q_LORA), full(KV_LORA)],
        out_shape=[jax.ShapeDtypeStruct((T, 1024), BF16), jax.ShapeDtypeStruct((1, Q_LORA), F32),
                   jax.ShapeDtypeStruct((1, KV_LORA), F32)],
        compiler_params=_arb(1))(P, dqn, dkvn, dkr, dab, gq, gkv)


def _mem_probs(qh, kh):
    s = _dot(qh, kh, NT) * MEM_SCALE
    p = jnp.exp(s - jnp.max(s, axis=1, keepdims=True))
    return p / jnp.sum(p, axis=1, keepdims=True)


def mem_attn_fwd(P, MKV, B, S, M, tq=512):
    T = B * S
    tq = min(tq, S)
    nq = S // tq

    def body(q_ref, kv_ref, o_ref):
        for h in range(N_HEADS):
            sl = slice(h * 128, (h + 1) * 128)
            p = _mem_probs(q_ref[:, sl].astype(BF16), kv_ref[:, sl])
            o_ref[:, sl] = _dot(p.astype(BF16), kv_ref[:, 512 + h * 128:512 + (h + 1) * 128], NN)

    return pl.pallas_call(
        body, name="mem_attn_fwd", grid=(B, nq),
        in_specs=[BS((tq, 512), lambda b, i: (b * nq + i, OFF_MEMQ // 512)), BS((M, 1024), lambda b, i: (b, 0))],
        out_specs=BS((tq, 512), lambda b, i: (b * nq + i, 0)),
        out_shape=jax.ShapeDtypeStruct((T, 512), F32), compiler_params=_arb(2))(P, MKV)


def mem_attn_bwd(P, MKV, dO, B, S, M, tq=512):
    T = B * S
    tq = min(tq, S)
    nq = S // tq

    def body(q_ref, kv_ref, do_ref, dq_ref, dkv_ref):
        @pl.when(pl.program_id(1) == 0)
        def _():
            dkv_ref[...] = jnp.zeros_like(dkv_ref)

        for h in range(N_HEADS):
            sl = slice(h * 128, (h + 1) * 128)
            sv = slice(512 + h * 128, 512 + (h + 1) * 128)
            qh = q_ref[:, sl].astype(BF16)
            kh = kv_ref[:, sl]
            do = do_ref[:, sl].astype(BF16)
            p = _mem_probs(qh, kh)
            dkv_ref[:, sv] += _dot(p.astype(BF16), do, TN)
            dp = _dot(do, kv_ref[:, sv], NT)
            ds = (p * (dp - jnp.sum(dp * p, axis=1, keepdims=True)) * MEM_SCALE).astype(BF16)
            dq_ref[:, sl] = _dot(ds, kh, NN).astype(BF16)
            dkv_ref[:, sl] += _dot(ds, qh, TN)

    return pl.pallas_call(
        body, name="mem_attn_bwd", grid=(B, nq),
        in_specs=[BS((tq, 512), lambda b, i: (b * nq + i, OFF_MEMQ // 512)), BS((M, 1024), lambda b, i: (b, 0)),
                  BS((tq, 512), lambda b, i: (b * nq + i, 0))],
        out_specs=[BS((tq, 512), lambda b, i: (b * nq + i, 0)), BS((M, 1024), lambda b, i: (b, 0))],
        out_shape=[jax.ShapeDtypeStruct((T, 512), BF16), jax.ShapeDtypeStruct((B * M, 1024), F32)],
        compiler_params=_arb(2))(P, MKV, dO)


def gain_grad(x, dy, name, tm=256):
    T, n = x.shape
    tm = min(tm, T)

    def body(x_ref, dy_ref, o_ref):
        @pl.when(pl.program_id(0) == 0)
        def _():
            o_ref[...] = jnp.zeros_like(o_ref)

        xv = x_ref[...]
        xh = xv * lax.rsqrt(jnp.mean(xv * xv, axis=-1, keepdims=True) + EPS)
        o_ref[...] += jnp.sum(dy_ref[...] * xh, axis=0, keepdims=True)

    return pl.pallas_call(
        body, name=name, grid=(T // tm,),
        in_specs=[BS((tm, n), lambda i: (i, 0))] * 2, out_specs=BS((1, n), lambda i: (0, 0)),
        out_shape=jax.ShapeDtypeStruct((1, n), F32), compiler_params=_arb(1))(x, dy)


def _conv_silu(x, w, t):
    y = x * w[3:4, :]
    for s in range(1, GDN_CONV):
        y = y + jnp.where(t >= s, pltpu.roll(x, s, 0), 0.0) * w[3 - s:4 - s, :]
    return y, _sigmoid(y)


def gdn_prep_fwd(P, conv_w, B, S):
    T = B * S

    def body(x_ref, w_ref, o_ref):
        kind = pl.program_id(1)
        t = lax.broadcasted_iota(jnp.int32, (S, 1), 0)
        y, sg = _conv_silu(x_ref[...], w_ref[...], t)
        a = y * sg
        scale = jnp.where(kind == 0, GDN_SCALE, 1.0).astype(F32)
        for h in range(N_HEADS):
            sl = slice(h * 128, (h + 1) * 128)
            seg = a[:, sl]
            n = lax.rsqrt(jnp.sum(seg * seg, axis=-1, keepdims=True) + EPS)
            o_ref[:, sl] = jnp.where(kind < 2, seg * (n * scale), seg)

    return pl.pallas_call(
        body, name="gdn_prep_fwd", grid=(B, 3),
        in_specs=[BS((S, 512), lambda b, k: (b, OFF_GDN // 512 + k)), BS((GDN_CONV, 512), lambda b, k: (0, k))],
        out_specs=BS((S, 512), lambda b, k: (b, k)),
        out_shape=jax.ShapeDtypeStruct((T, GDN_QKV), F32), compiler_params=_arb(2))(P, conv_w)


def gdn_prep_bwd(P, dqkv, conv_w, B, S):
    T = B * S

    def body(x_ref, d_ref, w_ref, o_ref, gw_ref):
        kind = pl.program_id(0)

        @pl.when(pl.program_id(1) == 0)
        def _():
            gw_ref[...] = jnp.zeros_like(gw_ref)

        t = lax.broadcasted_iota(jnp.int32, (S, 1), 0)
        x = x_ref[...]
        w = w_ref[...]
        y, sg = _conv_silu(x, w, t)
        a = y * sg
        scale = jnp.where(kind == 0, GDN_SCALE, 1.0).astype(F32)
        das = []
        for h in range(N_HEADS):
            sl = slice(h * 128, (h + 1) * 128)
            seg, dseg = a[:, sl], d_ref[:, sl]
            n = lax.rsqrt(jnp.sum(seg * seg, axis=-1, keepdims=True) + EPS)
            dn = scale * (n * dseg - seg * (n * n * n) * jnp.sum(dseg * seg, axis=-1, keepdims=True))
            das.append(jnp.where(kind < 2, dn, dseg))
        dy = jnp.concatenate(das, axis=1) * (sg * (1.0 + y * (1.0 - sg)))
        dx = dy * w[3:4, :]
        gw_ref[3:4, :] += jnp.sum(dy * x, axis=0, keepdims=True)
        for s in range(1, GDN_CONV):
            dx = dx + jnp.where(t + s < S, pltpu.roll(dy, S - s, 0), 0.0) * w[3 - s:4 - s, :]
            gw_ref[3 - s:4 - s, :] += jnp.sum(dy * jnp.where(t >= s, pltpu.roll(x, s, 0), 0.0), axis=0, keepdims=True)
        o_ref[...] = dx.astype(BF16)

    return pl.pallas_call(
        body, name="gdn_prep_bwd", grid=(3, B),
        in_specs=[BS((S, 512), lambda k, b: (b, OFF_GDN // 512 + k)), BS((S, 512), lambda k, b: (b, k)),
                  BS((GDN_CONV, 512), lambda k, b: (0, k))],
        out_specs=[BS((S, 512), lambda k, b: (b, k)), BS((GDN_CONV, 512), lambda k, b: (0, k))],
        out_shape=[jax.ShapeDtypeStruct((T, GDN_QKV), BF16), jax.ShapeDtypeStruct((GDN_CONV, GDN_QKV), F32)],
        compiler_params=_arb(2))(P, dqkv, conv_w)


def _chunk_row(n_rows):
    return lax.broadcasted_iota(jnp.int32, (n_rows, 1), 0) % CHUNK


def gdn_gate_fwd(P, alog_row, dt_row, B, S):
    T = B * S

    def body(x_ref, al_ref, dt_ref, o_ref):
        x = x_ref[...]
        lane = lax.broadcasted_iota(jnp.int32, (1, 128), 1)
        g = jnp.where(lane < 4, -jnp.exp(al_ref[...]) * _softplus(x + dt_ref[...]), 0.0)
        t = _chunk_row(S)
        for s in (1, 2, 4, 8, 16, 32):
            g = g + jnp.where(t >= s, pltpu.roll(g, s, 0), 0.0)
        o_ref[...] = jnp.where(lane < 4, g, jnp.where(lane < 8, _sigmoid(x), 0.0))

    row = BS((1, 128), lambda b: (0, 0))
    return pl.pallas_call(
        body, name="gdn_gate_fwd", grid=(B,),
        in_specs=[BS((S, 128), lambda b: (b, 768 // 128)), row, row], out_specs=BS((S, 128), lambda b: (b, 0)),
        out_shape=jax.ShapeDtypeStruct((T, 128), F32), compiler_params=_arb(1))(P, alog_row, dt_row)


def gdn_gate_bwd(P, dGB, alog_row, dt_row, B, S):
    T = B * S

    def body(x_ref, d_ref, al_ref, dt_ref, o_ref, acc_ref):
        @pl.when(pl.program_id(0) == 0)
        def _():
            acc_ref[...] = jnp.zeros_like(acc_ref)

        x, d = x_ref[...], d_ref[...]
        lane = lax.broadcasted_iota(jnp.int32, (1, 128), 1)
        z = x + dt_ref[...]
        coef = -jnp.exp(al_ref[...])
        g = coef * _softplus(z)
        da = jnp.where(lane < 4, d * coef * _sigmoid(z), 0.0)
        beta = _sigmoid(x)
        o_ref[...] = jnp.where(lane < 4, da, jnp.where(lane < 8, d * beta * (1.0 - beta), 0.0)).astype(BF16)
        acc_ref[0:1, :] += jnp.sum(jnp.where(lane < 4, d * g, 0.0), axis=0, keepdims=True)
        acc_ref[1:2, :] += jnp.sum(da, axis=0, keepdims=True)

    row = BS((1, 128), lambda b: (0, 0))
    return pl.pallas_call(
        body, name="gdn_gate_bwd", grid=(B,),
        in_specs=[BS((S, 128), lambda b: (b, 768 // 128)), BS((S, 128), lambda b: (b, 0)), row, row],
        out_specs=[BS((S, 128), lambda b: (b, 0)), BS((8, 128), lambda b: (0, 0))],
        out_shape=[jax.ShapeDtypeStruct((T, 128), BF16), jax.ShapeDtypeStruct((8, 128), F32)],
        compiler_params=_arb(1))(P, dGB, alog_row, dt_row)


def _chunk_masks(nc):
    r = lax.broadcasted_iota(jnp.int32, (nc, CHUNK, CHUNK), 1)
    c = lax.broadcasted_iota(jnp.int32, (nc, CHUNK, CHUNK), 2)
    return r >= c, r > c


def _chunk_local(q, k, gc, gr, beta, incl, strict):
    decay = jnp.exp(jnp.where(incl, gc - gr, NEG))
    kb = k * beta
    kbf = k.astype(BF16)
    m_kk = _bdot("gcd,gjd->gcj", kb.astype(BF16), kbf)
    l_mat = jnp.where(strict, m_kk * decay, 0.0)
    a_mat = _bdot("gcd,gjd->gcj", q.astype(BF16), kbf) * decay
    return decay, kb, l_mat, a_mat


def _split_bf16(x):
    hi = x.astype(BF16)
    return hi, (x - hi.astype(F32)).astype(BF16)


def _mm_split(ah, al, bh, bl):
    spec = "gij,gjk->gik"
    return _bdot(spec, ah, bh) + (_bdot(spec, ah, bl) + _bdot(spec, al, bh))


def gdn_chunk_fwd(qkv, GB, Grow, B, S, nc=8):
    T = B * S
    N = S // CHUNK
    nc = min(nc, N)
    nb = N // nc
    R = nc * CHUNK

    def body(q_ref, k_ref, v_ref, gb_ref, gr_ref, u_ref, w_ref, t_ref, a_ref):
        incl, strict = _chunk_masks(nc)
        eye = (lax.broadcasted_iota(jnp.int32, (nc, CHUNK, CHUNK), 1)
               == lax.broadcasted_iota(jnp.int32, (nc, CHUNK, CHUNK), 2)).astype(F32)
        for h in range(N_HEADS):
            sl = slice(h * 128, (h + 1) * 128)
            q = q_ref[:, sl].reshape(nc, CHUNK, 128)
            k = k_ref[:, sl].reshape(nc, CHUNK, 128)
            v = v_ref[:, sl].reshape(nc, CHUNK, 128)
            gc = gb_ref[:, h:h + 1].reshape(nc, CHUNK, 1)
            beta = gb_ref[:, 4 + h:5 + h].reshape(nc, CHUNK, 1)
            gr = gr_ref[h][:, None, :]
            _, kb, l_mat, a_mat = _chunk_local(q, k, gc, gr, beta, incl, strict)
            pw = -l_mat
            tinv = eye + pw
            for _ in range(5):
                ph, pl_ = _split_bf16(pw)
                pw = _mm_split(ph, pl_, ph, pl_)
                ph, pl_ = _split_bf16(pw)
                th, tl = _split_bf16(tinv)
                tinv = tinv + _mm_split(th, tl, ph, pl_)
            tb = tinv.astype(BF16)
            u = _bdot("gcj,gjv->gcv", tb, (v * beta).astype(BF16))
            w = _bdot("gcj,gjk->gck", tb, (kb * jnp.exp(gc)).astype(BF16))
            u_ref[:, sl] = u.reshape(R, 128)
            w_ref[:, sl] = w.reshape(R, 128)
            t_ref[h] = tinv
            a_ref[h] = a_mat

    rowb = lambda c, j: BS((R, c), lambda b, n: (b * nb + n, j))
    mat = BS((None, N_HEADS, nc, CHUNK, CHUNK), lambda b, n: (b, 0, n, 0, 0))
    return pl.pallas_call(
        body, name="gdn_chunk_fwd", grid=(B, nb),
        in_specs=[rowb(512, 0), rowb(512, 1), rowb(512, 2), rowb(128, 0),
                  BS((None, N_HEADS, nc, CHUNK), lambda b, n: (b, 0, n, 0))],
        out_specs=[rowb(512, 0), rowb(512, 0), mat, mat],
        out_shape=[jax.ShapeDtypeStruct((T, 512), F32), jax.ShapeDtypeStruct((T, 512), F32),
                   jax.ShapeDtypeStruct((B, N_HEADS, N, CHUNK, CHUNK), F32),
                   jax.ShapeDtypeStruct((B, N_HEADS, N, CHUNK, CHUNK), F32)],
        compiler_params=_arb(2))(qkv, qkv, qkv, GB, Grow)


def gdn_scan_fwd(qkv3, U3, W3, GB3, A, B, S):
    N = S // CHUNK

    def body(q_ref, k_ref, u_ref, w_ref, gb_ref, a_ref, o_ref, vn_ref, st_ref, s_s):
        @pl.when(pl.program_id(0) == 0)
        def _():
            s_s[...] = jnp.zeros_like(s_s)

        for b in range(B):
            for h in range(N_HEADS):
                sl = slice(h * 128, (h + 1) * 128)
                st = s_s[b, h]
                st_ref[b, h] = st
                stb = st.astype(BF16)
                g = gb_ref[b, :, h:h + 1]
                gl = g[CHUNK - 1:CHUNK, :]
                vn = u_ref[b, :, sl] - _dot(w_ref[b, :, sl].astype(BF16), stb, NN)
                vnb = vn.astype(BF16)
                o = _dot((q_ref[b, :, sl] * jnp.exp(g)).astype(BF16), stb, NN) + _dot(a_ref[b, h].astype(BF16), vnb, NN)
                vn_ref[b, :, sl] = vn
                o_ref[b, :, sl] = o
                s_s[b, h] = st * jnp.exp(gl) + _dot((k_ref[b, :, sl] * jnp.exp(gl - g)).astype(BF16), vnb, TN)

    tok = lambda c, j: BS((B, CHUNK, c), lambda n: (0, n, j))
    return pl.pallas_call(
        body, name="gdn_scan_fwd", grid=(N,),
        in_specs=[tok(512, 0), tok(512, 1), tok(512, 0), tok(512, 0), tok(128, 0),
                  BS((B, N_HEADS, None, CHUNK, CHUNK), lambda n: (0, 0, n, 0, 0))],
        out_specs=[tok(512, 0), tok(512, 0), BS((B, N_HEADS, None, 128, 128), lambda n: (0, 0, n, 0, 0))],
        out_shape=[jax.ShapeDtypeStruct((B, S, 512), F32), jax.ShapeDtypeStruct((B, S, 512), F32),
                   jax.ShapeDtypeStruct((B, N_HEADS, N, 128, 128), F32)],
        scratch_shapes=[pltpu.VMEM((B, N_HEADS, 128, 128), F32)],
        compiler_params=_arb(1))(qkv3, qkv3, U3, W3, GB3, A)


def gdn_scan_bwd(dO3, qkv3, W3, Vn3, GB3, A, St, B, S):
    N = S // CHUNK

    def body(do_ref, q_ref, k_ref, w_ref, vn_ref, gb_ref, a_ref, st_ref,
             du_ref, dw_ref, dq_ref, dk_ref, da_ref, dg_ref, ds_s):
        @pl.when(pl.program_id(0) == 0)
        def _():
            ds_s[...] = jnp.zeros_like(ds_s)

        lane = lax.broadcasted_iota(jnp.int32, (1, 128), 1)
        last = lax.broadcasted_iota(jnp.int32, (CHUNK, 1), 0) == CHUNK - 1
        for b in range(B):
            dg_all = jnp.zeros((CHUNK, 128), F32)
            for h in range(N_HEADS):
                sl = slice(h * 128, (h + 1) * 128)
                st = st_ref[b, h]
                stb = st.astype(BF16)
                dsn = ds_s[b, h]
                dsnb = dsn.astype(BF16)
                g = gb_ref[b, :, h:h + 1]
                gl = g[CHUNK - 1:CHUNK, :]
                egl = jnp.exp(gl)
                ekd = jnp.exp(gl - g)
                eg = jnp.exp(g)
                q, k = q_ref[b, :, sl], k_ref[b, :, sl]
                kd = k * ekd
                qg = q * eg
                do = do_ref[b, :, sl].astype(BF16)
                vnb = vn_ref[b, :, sl].astype(BF16)
                dvn = _dot(a_ref[b, h].astype(BF16), do, TN) + _dot(kd.astype(BF16), dsnb, NN)
                dvnb = dvn.astype(BF16)
                da_ref[b, h] = _dot(do, vnb, NT)
                dqg = _dot(do, stb, NT)
                dkd = _dot(vnb, dsnb, NT)
                ds_s[b, h] = (_dot(qg.astype(BF16), do, TN) + egl * dsn - _dot(w_ref[b, :, sl].astype(BF16), dvnb, TN))
                du_ref[b, :, sl] = dvn
                dw_ref[b, :, sl] = -_dot(dvnb, stb, NT)
                dq_ref[b, :, sl] = dqg * eg
                dk_ref[b, :, sl] = dkd * ekd
                ddel = jnp.sum(dkd * kd, axis=1, keepdims=True)
                dgl = jnp.sum(ddel, axis=0, keepdims=True) + jnp.sum(jnp.sum(st * dsn, axis=1, keepdims=True), axis=0, keepdims=True) * egl
                col = jnp.sum(dqg * qg, axis=1, keepdims=True) - ddel + jnp.where(last, dgl, 0.0)
                dg_all = jnp.where(lane == h, col, dg_all)
            dg_ref[b] = dg_all

    tok = lambda c, j: BS((B, CHUNK, c), lambda n: (0, N - 1 - n, j))
    mat = lambda d: BS((B, N_HEADS, None, d, d), lambda n: (0, 0, N - 1 - n, 0, 0))
    return pl.pallas_call(
        body, name="gdn_scan_bwd", grid=(N,),
        in_specs=[tok(512, 0), tok(512, 0), tok(512, 1), tok(512, 0), tok(512, 0), tok(128, 0), mat(CHUNK), mat(128)],
        out_specs=[tok(512, 0), tok(512, 0), tok(512, 0), tok(512, 0), mat(CHUNK), tok(128, 0)],
        out_shape=[jax.ShapeDtypeStruct((B, S, 512), F32)] * 4
        + [jax.ShapeDtypeStruct((B, N_HEADS, N, CHUNK, CHUNK), F32), jax.ShapeDtypeStruct((B, S, 128), F32)],
        scratch_shapes=[pltpu.VMEM((B, N_HEADS, 128, 128), F32)],
        compiler_params=_arb(1))(dO3, qkv3, qkv3, W3, Vn3, GB3, A, St)


def gdn_chunk_bwd(qkv, GB, Grow, Tinv, dA, dU, dW, dQ1, dK1, dG1, B, S, nc=8):
    T = B * S
    N = S // CHUNK
    nc = min(nc, N)
    nb = N // nc
    R = nc * CHUNK

    def body(q_ref, k_ref, v_ref, gb_ref, gr_ref, t_ref, da_ref, du_ref, dw_ref, dq1_ref, dk1_ref, dg1_ref, o_ref, dgb_ref):
        incl, strict = _chunk_masks(nc)
        lane = lax.broadcasted_iota(jnp.int32, (1, 128), 1)
        dg_all = dg1_ref[...]
        db_all = jnp.zeros((R, 128), F32)
        for h in range(N_HEADS):
            sl = slice(h * 128, (h + 1) * 128)
            q = q_ref[:, sl].reshape(nc, CHUNK, 128)
            k = k_ref[:, sl].reshape(nc, CHUNK, 128)
            v = v_ref[:, sl].reshape(nc, CHUNK, 128)
            gc = gb_ref[:, h:h + 1].reshape(nc, CHUNK, 1)
            beta = gb_ref[:, 4 + h:5 + h].reshape(nc, CHUNK, 1)
            gr = gr_ref[h][:, None, :]
            decay, kb, l_mat, a_mat = _chunk_local(q, k, gc, gr, beta, incl, strict)
            eg = jnp.exp(gc)
            kbg = kb * eg
            vb = v * beta
            tb = t_ref[h].astype(BF16)
            du = du_ref[:, sl].reshape(nc, CHUNK, 128).astype(BF16)
            dw = dw_ref[:, sl].reshape(nc, CHUNK, 128).astype(BF16)
            dvb = _bdot("gcj,gcv->gjv", tb, du)
            dkbg = _bdot("gcj,gck->gjk", tb, dw)
            dt = _bdot("gcv,gjv->gcj", du, vb.astype(BF16)) + _bdot("gck,gjk->gcj", dw, kbg.astype(BF16))
            tmp = _bdot("gac,gab->gcb", tb, dt.astype(BF16))
            dl = jnp.where(strict, -_bdot("gcb,gdb->gcd", tmp.astype(BF16), tb), 0.0)
            da = da_ref[h]
            dm = (dl * decay).astype(BF16)
            dqk = (da * decay).astype(BF16)
            kbf = k.astype(BF16)
            dkb = _bdot("gcj,gjd->gcd", dm, kbf) + dkbg * eg
            dk = (_bdot("gcj,gcd->gjd", dm, kb.astype(BF16)) + _bdot("gcj,gcd->gjd", dqk, q.astype(BF16))
                  + dk1_ref[:, sl].reshape(nc, CHUNK, 128) + dkb * beta)
            dq = _bdot("gcj,gjd->gcd", dqk, kbf) + dq1_ref[:, sl].reshape(nc, CHUNK, 128)
            e = dl * l_mat + da * a_mat
            dgc = (jnp.sum(e, axis=2, keepdims=True) - jnp.sum(jnp.swapaxes(e, 1, 2), axis=2, keepdims=True)
                   + jnp.sum(dkbg * kbg, axis=2, keepdims=True))
            dbeta = jnp.sum(dkb * k, axis=2, keepdims=True) + jnp.sum(dvb * v, axis=2, keepdims=True)
            o_ref[:, sl] = dq.reshape(R, 128)
            o_ref[:, 512 + h * 128:512 + (h + 1) * 128] = dk.reshape(R, 128)
            o_ref[:, 1024 + h * 128:1024 + (h + 1) * 128] = (dvb * beta).reshape(R, 128)
            dg_all = dg_all + jnp.where(lane == h, dgc.reshape(R, 1), 0.0)
            db_all = jnp.where(lane == 4 + h, dbeta.reshape(R, 1), db_all)
        t = _chunk_row(R)
        for s in (1, 2, 4, 8, 16, 32):
            dg_all = dg_all + jnp.where(t + s < CHUNK, pltpu.roll(dg_all, R - s, 0), 0.0)
        dgb_ref[...] = jnp.where(lane < 4, dg_all, db_all)

    rowb = lambda c, j: BS((R, c), lambda b, n: (b * nb + n, j))
    mat = BS((None, N_HEADS, nc, CHUNK, CHUNK), lambda b, n: (b, 0, n, 0, 0))
    return pl.pallas_call(
        body, name="gdn_chunk_bwd", grid=(B, nb),
        in_specs=[rowb(512, 0), rowb(512, 1), rowb(512, 2), rowb(128, 0),
                  BS((None, N_HEADS, nc, CHUNK), lambda b, n: (b, 0, n, 0)), mat, mat,
                  rowb(512, 0), rowb(512, 0), rowb(512, 0), rowb(512, 0), rowb(128, 0)],
        out_specs=[rowb(GDN_QKV, 0), rowb(128, 0)],
        out_shape=[jax.ShapeDtypeStruct((T, GDN_QKV), F32), jax.ShapeDtypeStruct((T, 128), F32)],
        compiler_params=_arb(2))(qkv, qkv, qkv, GB, Grow, Tinv, dA, dU, dW, dQ1, dK1, dG1)


def _gdn_out_norm(og, gg):
    outs, xhs, rs = [], [], []
    for h in range(N_HEADS):
        seg = og[:, h * 128:(h + 1) * 128]
        r = lax.rsqrt(jnp.mean(seg * seg, axis=-1, keepdims=True) + EPS)
        xh = seg * r
        outs.append(xh * gg)
        xhs.append(xh)
        rs.append(r)
    return outs, xhs, rs


def merge_fwd(o_mla, o_gdn, o_mem, P, x, tgt, w_out, g_gdn, g_fin, tm=256):
    T = x.shape[0]
    tm = min(tm, T)

    def body(om_ref, og_ref, oc_ref, gate_ref, x_ref, t_ref, w_ref, gg_ref, gf_ref, mix_ref, dx_ref, dxb_ref, sq_ref, gnf_ref):
        @pl.when(pl.program_id(0) == 0)
        def _():
            sq_ref[...] = jnp.zeros_like(sq_ref)
            gnf_ref[...] = jnp.zeros_like(gnf_ref)

        ogn, _, _ = _gdn_out_norm(og_ref[...], gg_ref[...])
        cat = jnp.concatenate([om_ref[...]] + ogn + [oc_ref[...]], axis=1)
        gt = gate_ref[...]
        mixed = (cat * (gt * _sigmoid(gt))).astype(BF16)
        mix_ref[...] = mixed
        x2 = x_ref[...] + _dot(mixed, w_ref[...], NN)
        r2 = lax.rsqrt(jnp.mean(x2 * x2, axis=-1, keepdims=True) + EPS)
        xh = x2 * r2
        gf = gf_ref[...]
        diff = xh * gf - t_ref[...]
        sq_ref[...] += jnp.sum(diff * diff, axis=0, keepdims=True)
        dy = diff * (1.0 / D_MODEL)
        gnf_ref[...] += jnp.sum(dy * xh, axis=0, keepdims=True)
        dxh = dy * gf
        dx = r2 * (dxh - xh * jnp.mean(dxh * xh, axis=-1, keepdims=True))
        dx_ref[...] = dx
        dxb_ref[...] = dx.astype(BF16)

    rowb = lambda c, j=0: BS((tm, c), lambda i: (i, j))
    full = lambda r, c: BS((r, c), lambda i: (0, 0))
    return pl.pallas_call(
        body, name="merge_fwd", grid=(T // tm,),
        in_specs=[rowb(512), rowb(512), rowb(512), rowb(D_MIX, OFF_GATE // D_MIX), rowb(D_MODEL), rowb(D_MODEL),
                  full(D_MIX, D_MODEL), full(1, 128), full(1, D_MODEL)],
        out_specs=[rowb(D_MIX), rowb(D_MODEL), rowb(D_MODEL), full(1, D_MODEL), full(1, D_MODEL)],
        out_shape=[jax.ShapeDtypeStruct((T, D_MIX), BF16), jax.ShapeDtypeStruct((T, D_MODEL), F32),
                   jax.ShapeDtypeStruct((T, D_MODEL), BF16),
                   jax.ShapeDtypeStruct((1, D_MODEL), F32), jax.ShapeDtypeStruct((1, D_MODEL), F32)],
        compiler_params=_arb(1))(o_mla, o_gdn, o_mem, P, x, tgt, w_out, g_gdn, g_fin)


def merge_bwd(dx2, o_mla, o_gdn, o_mem, P, w_out, g_gdn, tm=256):
    T = dx2.shape[0]
    tm = min(tm, T)

    def body(dx_ref, om_ref, og_ref, oc_ref, gate_ref, w_ref, gg_ref, dgate_ref, dom_ref, dog_ref, doc_ref, ggn_ref):
        @pl.when(pl.program_id(0) == 0)
        def _():
            ggn_ref[...] = jnp.zeros_like(ggn_ref)

        gg = gg_ref[...]
        dmix = _dot(dx_ref[...].astype(BF16), w_ref[...], NT)
        ogn, xhs, rs = _gdn_out_norm(og_ref[...], gg)
        cat = jnp.concatenate([om_ref[...]] + ogn + [oc_ref[...]], axis=1)
        gt = gate_ref[...]
        sg = _sigmoid(gt)
        dgate_ref[...] = (dmix * cat * (sg * (1.0 + gt * (1.0 - sg)))).astype(BF16)
        dcat = dmix * (gt * sg)
        dom_ref[...] = dcat[:, :512]
        doc_ref[...] = dcat[:, 1024:]
        acc = jnp.zeros((1, 128), F32)
        for h in range(N_HEADS):
            dseg = dcat[:, 512 + h * 128:512 + (h + 1) * 128]
            acc = acc + jnp.sum(dseg * xhs[h], axis=0, keepdims=True)
            dxh = dseg * gg
            dog_ref[:, h * 128:(h + 1) * 128] = rs[h] * (dxh - xhs[h] * jnp.mean(dxh * xhs[h], axis=-1, keepdims=True))
        ggn_ref[...] += acc

    rowb = lambda c, j=0: BS((tm, c), lambda i: (i, j))
    full = lambda r, c: BS((r, c), lambda i: (0, 0))
    return pl.pallas_call(
        body, name="merge_bwd", grid=(T // tm,),
        in_specs=[rowb(D_MODEL), rowb(512), rowb(512), rowb(512), rowb(D_MIX, OFF_GATE // D_MIX),
                  full(D_MIX, D_MODEL), full(1, 128)],
        out_specs=[rowb(D_MIX), rowb(512), rowb(512), rowb(512), full(1, 128)],
        out_shape=[jax.ShapeDtypeStruct((T, D_MIX), BF16)] + [jax.ShapeDtypeStruct((T, 512), F32)] * 3
        + [jax.ShapeDtypeStruct((1, 128), F32)],
        compiler_params=_arb(1))(dx2, o_mla, o_gdn, o_mem, P, w_out, g_gdn)


def in_norm_bwd(x, dh, dx2, gain, tm=256):
    T, n = x.shape
    tm = min(tm, T)

    def body(x_ref, dh_ref, dx2_ref, g_ref, o_ref, acc_ref):
        @pl.when(pl.program_id(0) == 0)
        def _():
            acc_ref[...] = jnp.zeros_like(acc_ref)

        xv = x_ref[...]
        r = lax.rsqrt(jnp.mean(xv * xv, axis=-1, keepdims=True) + EPS)
        xh = xv * r
        dy = dh_ref[...]
        acc_ref[...] += jnp.sum(dy * xh, axis=0, keepdims=True)
        dxh = dy * g_ref[...]
        o_ref[...] = dx2_ref[...] + r * (dxh - xh * jnp.mean(dxh * xh, axis=-1, keepdims=True))

    rowb = BS((tm, n), lambda i: (i, 0))
    full = BS((1, n), lambda i: (0, 0))
    return pl.pallas_call(
        body, name="in_norm_bwd", grid=(T // tm,),
        in_specs=[rowb, rowb, rowb, full], out_specs=[rowb, full],
        out_shape=[jax.ShapeDtypeStruct((T, n), F32), jax.ShapeDtypeStruct((1, n), F32)],
        compiler_params=_arb(1))(x, dh, dx2, gain)


W_IN_SHARD = D_IN // 4
_GDN0 = Q_LORA + KV_LORA + MLA_ROPE
_AB0 = _GDN0 + GDN_QKV
_MEMQ0 = _AB0 + 2 * N_HEADS
_GATE0 = _MEMQ0 + N_HEADS * MEM_DH


def _w_in_row_map():
    a, m, gt = _AB0 - 2 * W_IN_SHARD, _MEMQ0 - 2 * W_IN_SHARD, _GATE0 - 2 * W_IN_SHARD
    e0 = OFF_GDN + W_IN_SHARD - _GDN0
    e1 = e0 + W_IN_SHARD
    e2 = OFF_GATE + W_IN_SHARD - gt
    return [(0, 0, 0, 672), (0, 672, 704, 32), (2, a, 768, m - a), (2, m, OFF_MEMQ, gt - m), (0, _GDN0, OFF_GDN, W_IN_SHARD - _GDN0),
            (1, 0, e0, W_IN_SHARD), (2, 0, e1, a), (2, gt, OFF_GATE, W_IN_SHARD - gt), (3, 0, e2, W_IN_SHARD)]


_W_IN_ZERO_ROWS = [(672, 32), (736, 32), (776, 248)]
W_IN_LANES = 256


def pad_w_in_t(shards):
    per_half = shards.shape[3] // W_IN_LANES

    def body(s_ref, o_ref):
        for r0, n in _W_IN_ZERO_ROWS:
            o_ref[r0:r0 + n, :] = jnp.zeros((n, W_IN_LANES), o_ref.dtype)
        for q, src, dst, n in _w_in_row_map():
            o_ref[dst:dst + n, :] = s_ref[q, src:src + n, :]

    return pl.pallas_call(
        body, name="pad_w_in_t", grid=(D_MODEL // W_IN_LANES,),
        in_specs=[BS((N_CHIPS, None, W_IN_SHARD, W_IN_LANES), lambda j: (0, j // per_half, 0, j % per_half))],
        out_specs=BS((N_PAD, W_IN_LANES), lambda j: (0, j)),
        out_shape=jax.ShapeDtypeStruct((N_PAD, D_MODEL), shards.dtype), compiler_params=_arb(1))(shards)


def unpad_w_in_t(g):
    def body(g_ref, o_ref):
        for q, src, dst, n in _w_in_row_map():
            o_ref[q, src:src + n, :] = g_ref[dst:dst + n, :]

    return pl.pallas_call(
        body, name="unpad_w_in_t", grid=(D_MODEL // W_IN_LANES,),
        in_specs=[BS((N_PAD, W_IN_LANES), lambda j: (0, j))], out_specs=BS((N_CHIPS, W_IN_SHARD, W_IN_LANES), lambda j: (0, 0, j)),
        out_shape=jax.ShapeDtypeStruct((N_CHIPS, W_IN_SHARD, D_MODEL), g.dtype), compiler_params=_arb(1))(g)


def _pad_w_q_b_t(s):
    z = jnp.zeros((32, s.shape[2]), s.dtype)
    parts = []
    for h in range(N_HEADS):
        parts += [s[h, :128], s[h, 128:160], z, s[h, 160:192], z]
    return jnp.concatenate(parts, axis=0)


def _unpad_w_q_b_t(g):
    return jnp.stack([jnp.concatenate([g[h * HEAD_PAD:h * HEAD_PAD + 128], g[h * HEAD_PAD + 128:h * HEAD_PAD + 160],
                                       g[h * HEAD_PAD + 192:h * HEAD_PAD + 224]]) for h in range(N_HEADS)])


def _perm_w_kv_b(s):
    return jnp.concatenate([s[h, :, :128] for h in range(N_HEADS)] + [s[h, :, 128:] for h in range(N_HEADS)], axis=1)


def _unperm_w_kv_b(g):
    return jnp.stack([jnp.concatenate([g[:, h * 128:(h + 1) * 128], g[:, 512 + h * 128:512 + (h + 1) * 128]], axis=1)
                      for h in range(N_HEADS)])


def _lane_row(v4):
    return jnp.pad(v4.reshape(1, -1).astype(F32), ((0, 0), (0, 128 - v4.size)))


def _pack(pieces, n_rows):
    flat = jnp.concatenate([p.reshape(-1) for p in pieces])
    return jnp.pad(flat, (0, n_rows * 1024 - flat.size)).reshape(n_rows, 1024)


def _unpack(block, shapes):
    flat = block.reshape(-1)
    out, off = [], 0
    for shp in shapes:
        n = int(np.prod(shp))
        out.append(flat[off:off + n].reshape(shp))
        off += n
    return out


N_CHIPS = 4
MESH = pl.DeviceIdType.MESH
ANY = BS(memory_space=pl.ANY)


def _place():
    return lax.axis_index("x"), lax.axis_index("y"), lax.axis_index("c")


def _other_chips(x, y):
    return [(1 - x, y), (x, 1 - y), (1 - x, 1 - y)]


def _half(split, which):
    if split == "lead":
        return (which,)
    axis, size = split
    ds = pl.ds(pl.multiple_of(which * size, 16 if axis == 0 else 128), size)
    return (ds, slice(None)) if axis == 0 else (slice(None), ds)


def allgather_chips(shards, splits, name):
    n = len(shards)

    def body(*refs):
        s_refs, o_refs = refs[:n], refs[n:2 * n]
        send_sems, recv_sems, local_sems = refs[2 * n:]
        x, y, c = _place()
        chips = _other_chips(x, y)

        def copy(k, src, dst, to):
            return pltpu.make_async_remote_copy(src_ref=src, dst_ref=dst, send_sem=send_sems.at[k], recv_sem=recv_sems.at[k],
                                                device_id=to, device_id_type=MESH)

        owns, first, passed = [], [], []
        for i, (s_ref, o_ref) in enumerate(zip(s_refs, o_refs)):
            mine = _half(splits[i], c)
            owns.append(pltpu.make_async_copy(s_ref, o_ref.at[2 * x + y], local_sems.at[i]))
            owns[-1].start()
            for j, (px, py) in enumerate(chips):
                first.append(copy(6 * i + j, s_ref.at[mine], o_ref.at[(2 * x + y,) + mine], (px, py, c)))
                first[-1].start()
        for i, (s_ref, o_ref) in enumerate(zip(s_refs, o_refs)):
            mine = _half(splits[i], c)
            for j, (px, py) in enumerate(chips):
                landed = o_ref.at[(2 * px + py,) + mine]
                copy(6 * i + j, s_ref.at[mine], landed, (px, py, c)).wait_recv()
                passed.append(copy(6 * i + 3 + j, landed, landed, (x, y, 1 - c)))
                passed[-1].start()
        for i, (s_ref, o_ref) in enumerate(zip(s_refs, o_refs)):
            theirs = _half(splits[i], 1 - c)
            for j, (px, py) in enumerate(chips):
                copy(6 * i + 3 + j, s_ref.at[theirs], o_ref.at[(2 * px + py,) + theirs], (x, y, 1 - c)).wait_recv()
        for cp in first + passed:
            cp.wait_send()
        for cp in owns:
            cp.wait()

    return pl.pallas_call(
        body, name=name, in_specs=[ANY] * n, out_specs=[ANY] * n,
        out_shape=[jax.ShapeDtypeStruct((N_CHIPS,) + s.shape, s.dtype) for s in shards],
        scratch_shapes=[pltpu.SemaphoreType.DMA((6 * n,)), pltpu.SemaphoreType.DMA((6 * n,)), pltpu.SemaphoreType.DMA((n,))])(*shards)


SEM = BS(memory_space=pltpu.SEMAPHORE)
HBM = BS(memory_space=pltpu.HBM)
_IN_HBM = lambda a: pltpu.with_memory_space_constraint(a, pltpu.HBM)
_SIDE_EFFECT = pltpu.SideEffectType.DATAFLOW_SIDE_EFFECTING


def _late_gather_copies(s_refs, l_refs, send_sems, recv_sems, local_sems, with_arrivals):
    x, y, c = _place()
    sends, recvs, locals_ = [], [], []
    for i, (s_ref, l_ref) in enumerate(zip(s_refs, l_refs)):
        locals_.append(pltpu.make_async_copy(s_ref, l_ref.at[2 * x + y], local_sems.at[i]))
        for j, (px, py) in enumerate(_other_chips(x, y)):
            k = 3 * i + j
            sends.append(pltpu.make_async_remote_copy(src_ref=s_ref, dst_ref=l_ref.at[2 * x + y], send_sem=send_sems.at[k],
                                                      recv_sem=recv_sems.at[k], device_id=(px, py, c), device_id_type=MESH))
            if with_arrivals:
                recvs.append(pltpu.make_async_remote_copy(src_ref=s_ref, dst_ref=l_ref.at[2 * px + py], send_sem=send_sems.at[k],
                                                          recv_sem=recv_sems.at[k], device_id=(px, py, c), device_id_type=MESH))
    return sends, recvs, locals_


def late_gather_start(shards, name):
    n = len(shards)

    def body(*refs):
        s_refs, l_refs = refs[:n], refs[n:2 * n]
        send_sems, recv_sems, local_sems = refs[2 * n:2 * n + 3]
        token = refs[-1]
        sends, _, locals_ = _late_gather_copies(s_refs, l_refs, send_sems, recv_sems, local_sems, False)
        for cp in locals_ + sends:
            cp.start()
        token[...] = jnp.zeros_like(token)

    lands = [lax.empty((N_CHIPS,) + s.shape, s.dtype) for s in shards]
    hbm_like = lambda a: pltpu.HBM(a.shape, a.dtype)
    out = pl.pallas_call(
        body, name=name,
        out_shape=[pltpu.SemaphoreType.DMA((3 * n,)), pltpu.SemaphoreType.DMA((3 * n,)), pltpu.SemaphoreType.DMA((n,))]
        + [hbm_like(s) for s in shards] + [hbm_like(l) for l in lands] + [jax.ShapeDtypeStruct((8, 128), F32)],
        in_specs=[HBM] * (2 * n), out_specs=[SEM] * 3 + [HBM] * (2 * n) + [BS(memory_space=pltpu.VMEM)],
        input_output_aliases={i: 3 + i for i in range(2 * n)},
        compiler_params=pltpu.CompilerParams(has_side_effects=_SIDE_EFFECT))(*[_IN_HBM(s) for s in shards], *[_IN_HBM(l) for l in lands])
    return out[:3], out[3:3 + n], out[3 + n:3 + 2 * n], out[-1]


def late_gather_wait(sems, shards, lands, after, name):
    n = len(shards)

    def body(*refs):
        s_refs, l_refs = refs[:n], refs[n:2 * n]
        send_sems, recv_sems, local_sems = refs[2 * n:2 * n + 3]
        sends, recvs, locals_ = _late_gather_copies(s_refs, l_refs, send_sems, recv_sems, local_sems, True)
        for cp in locals_:
            cp.wait()
        for cp in sends:
            cp.wait_send()
        for cp in recvs:
            cp.wait_recv()

    hbm_like = lambda a: pltpu.HBM(a.shape, a.dtype)
    out = pl.pallas_call(
        body, name=name, out_shape=[hbm_like(s) for s in shards] + [hbm_like(l) for l in lands],
        in_specs=[HBM] * (2 * n) + [SEM] * 3 + [BS(memory_space=pl.ANY)], out_specs=[HBM] * (2 * n),
        input_output_aliases={i: i for i in range(2 * n)},
        compiler_params=pltpu.CompilerParams(has_side_effects=_SIDE_EFFECT))(*shards, *lands, *sems, after)
    return out[n:]


def allgather_devices(block, name):
    R, C = block.shape

    def body(b_ref, o_ref, send_sems, recv_sems, local_sem):
        x, y, c = _place()
        me = 4 * x + 2 * y + c
        own = pltpu.make_async_copy(b_ref, o_ref.at[me], local_sem)
        own.start()
        copies = []
        for r in range(1, 8):
            px = 1 - x if r & 4 else x
            py = 1 - y if r & 2 else y
            pc = 1 - c if r & 1 else c
            send = pltpu.make_async_remote_copy(src_ref=b_ref, dst_ref=o_ref.at[me], send_sem=send_sems.at[r - 1],
                                                recv_sem=recv_sems.at[r - 1], device_id=(px, py, pc), device_id_type=MESH)
            recv = pltpu.make_async_remote_copy(src_ref=b_ref, dst_ref=o_ref.at[4 * px + 2 * py + pc], send_sem=send_sems.at[r - 1],
                                                recv_sem=recv_sems.at[r - 1], device_id=(px, py, pc), device_id_type=MESH)
            send.start()
            copies.append((send, recv))
        for send, recv in copies:
            recv.wait_recv()
            send.wait_send()
        own.wait()

    return pl.pallas_call(
        body, name=name, in_specs=[ANY], out_specs=ANY, out_shape=jax.ShapeDtypeStruct((8, R, C), block.dtype),
        scratch_shapes=[pltpu.SemaphoreType.DMA((7,)), pltpu.SemaphoreType.DMA((7,)), pltpu.SemaphoreType.DMA(())])(block)


def swap_sibling(arrs, name, splits=None):
    n = len(arrs)

    def sent(a_ref, i, c):
        return a_ref if splits is None else a_ref.at[(slice(None),) + _half(splits[i], 1 - c)]

    def out_shape(a, i):
        if splits is None:
            return a.shape
        axis, size = splits[i]
        return (a.shape[0], size, a.shape[2]) if axis == 0 else (a.shape[0], a.shape[1], size)

    def body(*refs):
        a_refs, o_refs = refs[:n], refs[n:2 * n]
        send_sems, recv_sems = refs[2 * n:]
        x, y, c = _place()
        copies = [pltpu.make_async_remote_copy(src_ref=sent(a_ref, i, c), dst_ref=o_ref, send_sem=send_sems.at[i],
                                               recv_sem=recv_sems.at[i], device_id=(x, y, 1 - c), device_id_type=MESH)
                  for i, (a_ref, o_ref) in enumerate(zip(a_refs, o_refs))]
        for cp in copies:
            cp.start()
        for cp in copies:
            cp.wait()

    return pl.pallas_call(
        body, name=name, in_specs=[ANY] * n, out_specs=[ANY] * n,
        out_shape=[jax.ShapeDtypeStruct(out_shape(a, i), a.dtype) for i, a in enumerate(arrs)],
        scratch_shapes=[pltpu.SemaphoreType.DMA((n,)), pltpu.SemaphoreType.DMA((n,))])(*arrs)


def _exchange_copies(p_refs, l_refs, send_sems, recv_sems):
    x, y, c = _place()
    return [pltpu.make_async_remote_copy(src_ref=p_ref.at[2 * px + py], dst_ref=l_ref.at[j], send_sem=send_sems.at[3 * i + j],
                                         recv_sem=recv_sems.at[3 * i + j], device_id=(px, py, c), device_id_type=MESH)
            for i, (p_ref, l_ref) in enumerate(zip(p_refs, l_refs)) for j, (px, py) in enumerate(_other_chips(x, y))]


def exchange_chips_start(parts, name):
    n = len(parts)

    def body(*refs):
        send_sems, recv_sems = refs[2 * n:2 * n + 2]
        for cp in _exchange_copies(refs[:n], refs[n:2 * n], send_sems, recv_sems):
            cp.start()
        refs[-1][...] = jnp.zeros_like(refs[-1])

    lands = [lax.empty((3,) + p.shape[1:], p.dtype) for p in parts]
    hbm_like = lambda a: pltpu.HBM(a.shape, a.dtype)
    out = pl.pallas_call(
        body, name=name,
        out_shape=[pltpu.SemaphoreType.DMA((3 * n,)), pltpu.SemaphoreType.DMA((3 * n,))]
        + [hbm_like(p) for p in parts] + [hbm_like(l) for l in lands] + [jax.ShapeDtypeStruct((8, 128), F32)],
        in_specs=[HBM] * (2 * n), out_specs=[SEM] * 2 + [HBM] * (2 * n) + [BS(memory_space=pltpu.VMEM)],
        input_output_aliases={i: 2 + i for i in range(2 * n)},
        compiler_params=pltpu.CompilerParams(has_side_effects=_SIDE_EFFECT))(*[_IN_HBM(p) for p in parts], *[_IN_HBM(l) for l in lands])
    return out[:2], out[2:2 + n], out[2 + n:2 + 2 * n], out[-1]


def exchange_chips_wait(sems, parts, lands, after, name):
    n = len(parts)

    def body(*refs):
        send_sems, recv_sems = refs[2 * n:2 * n + 2]
        for cp in _exchange_copies(refs[:n], refs[n:2 * n], send_sems, recv_sems):
            cp.wait_send()
            cp.wait_recv()

    hbm_like = lambda a: pltpu.HBM(a.shape, a.dtype)
    out = pl.pallas_call(
        body, name=name, out_shape=[hbm_like(p) for p in parts] + [hbm_like(l) for l in lands],
        in_specs=[HBM] * (2 * n) + [SEM] * 2 + [BS(memory_space=pl.ANY)], out_specs=[HBM] * (2 * n),
        input_output_aliases={i: i for i in range(2 * n)},
        compiler_params=pltpu.CompilerParams(has_side_effects=_SIDE_EFFECT))(*parts, *lands, *sems, after)
    return out[n:]


def _half_block(shape2, split):
    axis, size = split
    return (size, shape2[1]) if axis == 0 else (shape2[0], size)


def add_pairs(parts, halves, splits, core, name):
    n = len(parts)

    def body(s_ref, *refs):
        for a_ref, b_ref, o_ref in zip(refs[:n], refs[n:2 * n], refs[2 * n:]):
            o_ref[...] = (a_ref[...].astype(F32) + b_ref[...].astype(F32)).astype(BF16)

    def mine(i):
        blk = (None,) + _half_block(parts[i].shape[1:], splits[i])
        if splits[i][0] == 0:
            return BS(blk, lambda q, s: (q, s[0], 0))
        return BS(blk, lambda q, s: (q, 0, s[0]))

    half_specs = [BS((None,) + h.shape[1:], lambda q, s: (q, 0, 0)) for h in halves]
    return pl.pallas_call(
        body, name=name,
        grid_spec=pltpu.PrefetchScalarGridSpec(num_scalar_prefetch=1, grid=(N_CHIPS,),
                                               in_specs=[mine(i) for i in range(n)] + half_specs, out_specs=half_specs),
        out_shape=[jax.ShapeDtypeStruct(h.shape, BF16) for h in halves], compiler_params=_arb(1))(core, *parts, *halves)


def add_fives(parts, halves, from_chips, splits, chip_core, name):
    n = len(parts)

    def body(s_ref, *refs):
        for a_ref, b_ref, p_ref, o_ref in zip(refs[:n], refs[n:2 * n], refs[2 * n:3 * n], refs[3 * n:]):
            s = a_ref[...].astype(F32) + b_ref[...].astype(F32)
            for j in range(3):
                s = s + p_ref[j].astype(F32)
            o_ref[...] = s

    def mine(i):
        blk = (None,) + _half_block(parts[i].shape[1:], splits[i])
        if splits[i][0] == 0:
            return BS(blk, lambda g, s: (s[0], s[1], 0))
        return BS(blk, lambda g, s: (s[0], 0, s[1]))

    half_specs = [BS((None,) + h.shape[1:], lambda g, s: (s[0], 0, 0)) for h in halves]
    chip_specs = [BS(p.shape, lambda g, s: (0, 0, 0)) for p in from_chips]
    out_specs = [BS(h.shape[1:], lambda g, s: (0, 0)) for h in halves]
    return pl.pallas_call(
        body, name=name,
        grid_spec=pltpu.PrefetchScalarGridSpec(num_scalar_prefetch=1, grid=(1,),
                                               in_specs=[mine(i) for i in range(n)] + half_specs + chip_specs, out_specs=out_specs),
        out_shape=[jax.ShapeDtypeStruct(h.shape[1:], F32) for h in halves], compiler_params=_arb(1))(chip_core, *parts, *halves, *from_chips)


def sum_leading(a, name):
    def body(a_ref, o_ref):
        s = a_ref[0]
        for j in range(1, a.shape[0]):
            s = s + a_ref[j]
        o_ref[...] = s

    return pl.pallas_call(body, name=name, out_shape=jax.ShapeDtypeStruct(a.shape[1:], a.dtype))(a)


def _adamw_math(w, g, m, v):
    mn = ADAM_B1 * m + (1.0 - ADAM_B1) * g
    vn = ADAM_B2 * v + (1.0 - ADAM_B2) * (g * g)
    m_hat = mn / (1.0 - ADAM_B1 ** ADAM_STEP)
    v_hat = vn / (1.0 - ADAM_B2 ** ADAM_STEP)
    return -ADAM_LR * (m_hat / (jnp.sqrt(v_hat) + ADAM_EPS) + ADAM_WD * w), mn, vn


def adamw(w, g, m, v, name):
    R, C = g.shape
    lead = (None,) * (w.ndim - 2)

    def body(w_ref, g_ref, m_ref, v_ref, d_ref, mo_ref, vo_ref):
        d_ref[...], mo_ref[...], vo_ref[...] = _adamw_math(w_ref[...], g_ref[...], m_ref[...], v_ref[...])

    wblk = BS(lead + (R, C), lambda i: (0,) * w.ndim)
    gblk = BS((R, C), lambda i: (0, 0))
    return pl.pallas_call(
        body, name=name, grid=(1,), in_specs=[wblk, gblk, wblk, wblk], out_specs=[wblk] * 3,
        out_shape=[jax.ShapeDtypeStruct(w.shape, F32)] * 3, compiler_params=_arb(1))(w, g, m, v)


def adamw_halves(w, mine, other, m, v, split, core, name):
    R, C = w.shape[-2:]
    axis, size = split
    lead = (None,) * (w.ndim - 2)
    zeros = (0,) * (w.ndim - 2)
    if axis == 0:
        tr = size if size <= 256 else next(t for t in range(256, 7, -1) if size % t == 0 and t % 8 == 0)
        nb = size // tr
        whole = BS(lead + (tr, C), lambda hi, j, s: zeros + (hi * nb + j, 0))
        part = BS((tr, C), lambda hi, j, s: (j, 0))
    else:
        nb = size // 128
        whole = BS(lead + (R, 128), lambda hi, j, s: zeros + (0, hi * nb + j))
        part = BS((R, 128), lambda hi, j, s: (0, j))

    def body(s_ref, w_ref, a_ref, b_ref, m_ref, v_ref, g_ref, d_ref, mo_ref, vo_ref):
        g = jnp.where(pl.program_id(0) == s_ref[0], a_ref[...], b_ref[...])
        g_ref[...] = g
        d_ref[...], mo_ref[...], vo_ref[...] = _adamw_math(w_ref[...], g, m_ref[...], v_ref[...])

    return pl.pallas_call(
        body, name=name,
        grid_spec=pltpu.PrefetchScalarGridSpec(num_scalar_prefetch=1, grid=(2, nb),
                                               in_specs=[whole, part, part, whole, whole], out_specs=[whole] * 4),
        out_shape=[jax.ShapeDtypeStruct(w.shape, F32)] * 4, compiler_params=_arb(2))(core, w, mine, other, m, v)


def dense_bf16(w3, name):
    R, _, K = w3.shape
    kh = K // 2

    def body(w_hbm, o_ref, buf, sem):
        cp = pltpu.make_async_copy(w_hbm.at[:, 0], buf, sem)
        cp.start()
        cp.wait()
        o_ref[0] = buf[:, :kh].astype(BF16)
        o_ref[1] = buf[:, kh:].astype(BF16)

    return pl.pallas_call(
        body, name=name, in_specs=[ANY], out_specs=BS(memory_space=pltpu.VMEM), out_shape=jax.ShapeDtypeStruct((2, R, kh), BF16),
        scratch_shapes=[pltpu.VMEM((R, K), F32), pltpu.SemaphoreType.DMA(())])(w3)


ROW_BLOCK = 184


def adamw_untiled_rows(w3, mine, other, m3, v3, name):
    R, _, K = w3.shape
    kh = K // 2
    starts = list(range(0, R, ROW_BLOCK))
    sizes = [min(ROW_BLOCK, R - s) for s in starts]
    nblk = len(starts)

    def body(w_hbm, a_ref, b_ref, m_hbm, v_hbm, g_hbm, d_hbm, mo_hbm, vo_hbm,
             wbuf, mbuf, vbuf, gbuf, dbuf, mobuf, vobuf, in_sems, out_sems):
        first = lax.axis_index("c") == 0
        ins = []
        for k, (r0, n) in enumerate(zip(starts, sizes)):
            rows = pl.ds(r0, n)
            cps = [pltpu.make_async_copy(src.at[rows, 0], dst.at[rows], in_sems.at[3 * k + i])
                   for i, (src, dst) in enumerate(((w_hbm, wbuf), (m_hbm, mbuf), (v_hbm, vbuf)))]
            for cp in cps:
                cp.start()
            ins.append(cps)

        def update(rows):
            a, b = a_ref[rows, :], b_ref[rows, :]
            g = jnp.concatenate([jnp.where(first, a, b), jnp.where(first, b, a)], axis=1)
            gbuf[rows, :] = g
            dbuf[rows, :], mobuf[rows, :], vobuf[rows, :] = _adamw_math(wbuf[rows, :], g, mbuf[rows, :], vbuf[rows, :])

        outs = []
        for k, (r0, n) in enumerate(zip(starts, sizes)):
            for cp in ins[k]:
                cp.wait()
            groups, tail = n // 8, n % 8

            def group(i, carry, r0=r0):
                update(pl.ds(pl.multiple_of(r0 + i * 8, 8), 8))
                return carry

            lax.fori_loop(0, groups, group, 0)
            if tail:
                update(pl.ds(r0 + groups * 8, tail))
            rows = pl.ds(r0, n)
            cps = [pltpu.make_async_copy(src.at[rows], dst.at[rows, 0], out_sems.at[4 * k + i])
                   for i, (src, dst) in enumerate(((gbuf, g_hbm), (dbuf, d_hbm), (mobuf, mo_hbm), (vobuf, vo_hbm)))]
            for cp in cps:
                cp.start()
            outs += cps
        for cp in outs:
            cp.wait()

    vmem = BS(memory_space=pltpu.VMEM)
    return pl.pallas_call(
        body, name=name, in_specs=[ANY, vmem, vmem, ANY, ANY], out_specs=[ANY] * 4,
        out_shape=[jax.ShapeDtypeStruct(w3.shape, F32)] * 4,
        scratch_shapes=[pltpu.VMEM((R, K), F32)] * 7 + [pltpu.SemaphoreType.DMA((3 * nblk,)), pltpu.SemaphoreType.DMA((4 * nblk,))])(
            w3, mine, other, m3, v3)


def adamw_w_q_b(w, mine, other, m, v, name):
    def body(w_ref, a_ref, b_ref, m_ref, v_ref, g_ref, d_ref, mo_ref, vo_ref):
        first = lax.axis_index("c") == 0
        lo = jnp.where(first, a_ref[...], b_ref[...])
        hi = jnp.where(first, b_ref[...], a_ref[...])
        g = jnp.concatenate([lo, hi[0:32], hi[64:96]], axis=0)
        g_ref[...] = g
        d_ref[...], mo_ref[...], vo_ref[...] = _adamw_math(w_ref[...], g, m_ref[...], v_ref[...])

    return pl.pallas_call(body, name=name, out_shape=[jax.ShapeDtypeStruct(w.shape, F32)] * 4)(w, mine, other, m, v)


def local_step(x, mem, positions, tgt, norm_in, w_in, late_weights, big_grads_ready, q_a_norm, kv_a_norm, gdn_conv, gdn_a_log,
               gdn_dt_bias, gdn_norm, mem_norm, norm_final):
    B, S, D = x.shape
    M = mem.shape[1]
    T = B * S
    N = S // CHUNK
    x2d = x.reshape(T, D)
    mem2d = mem.reshape(B * M, D)
    tgt2d = tgt.reshape(T, D)

    wp = w_in
    alog_row, dt_row = _lane_row(gdn_a_log), _lane_row(gdn_dt_bias)

    half = MLA_ROPE // 2
    inv_freq = 1.0 / (ROPE_THETA ** (jnp.arange(half, dtype=F32) / half))
    z32 = jnp.zeros((half,), F32)
    o32 = jnp.ones((half,), F32)
    inv_row = jnp.concatenate([inv_freq, z32, inv_freq, z32]).reshape(1, 128)
    sgn_row = jnp.concatenate([-o32, z32, o32, z32]).reshape(1, 128)
    msk_row = jnp.concatenate([o32, z32, o32, z32]).reshape(1, 128)
    cos_t, sin_t = rope_tables(positions.reshape(T, 1), inv_row, sgn_row, msk_row)

    h = rms_fwd(x2d, norm_in, "rms_in")
    P = mm(h, wp, "nt", F32, "in_proj", bm=512, bn=1536, n_outer=True)
    wq, wkv = late_weights[0](P)
    Q, K, V, qn, kvn = mla_prep(P, q_a_norm, kv_a_norm, wq, wkv, cos_t, sin_t)
    o_mla, lse = mla_attn_fwd(Q, K, V, B, S)
    qkv = gdn_prep_fwd(P, gdn_conv, B, S)
    GB = gdn_gate_fwd(P, alog_row, dt_row, B, S)
    Grow = jnp.transpose(GB[:, :N_HEADS].reshape(B, N, CHUNK, N_HEADS), (0, 3, 1, 2))
    U, W, Tinv, A = gdn_chunk_fwd(qkv, GB, Grow, B, S)
    qkv3, GB3 = qkv.reshape(B, S, GDN_QKV), GB.reshape(B, S, 128)
    W3 = W.reshape(B, S, 512)
    o_gdn3, Vn3, St = gdn_scan_fwd(qkv3, U.reshape(B, S, 512), W3, GB3, A, B, S)
    o_gdn = o_gdn3.reshape(T, 512)
    w_mem_kv, w_out = late_weights[1](o_gdn)
    memn = rms_fwd(mem2d, mem_norm, "rms_mem")
    MKV = mm(memn, w_mem_kv, "nn", BF16, "mem_kv_proj")
    o_mem = mem_attn_fwd(P, MKV, B, S, M)
    mixed, dx2, dx2b, sq, g_norm_final = merge_fwd(o_mla, o_gdn, o_mem, P, x2d, tgt2d, w_out, gdn_norm, norm_final.reshape(1, D))

    g_w_out = mm(mixed, dx2b, "tn", BF16, "grad_w_out")
    dgate, do_mla, do_gdn, do_mem, g_gdn_norm = merge_bwd(dx2b, o_mla, o_gdn, o_mem, P, w_out, gdn_norm)

    dmemq, dMKV = mem_attn_bwd(P, MKV, do_mem, B, S, M)
    g_w_mem_kv = mm(memn, dMKV, "tn", BF16, "grad_w_mem_kv")
    dmemn = mm(dMKV, w_mem_kv, "nt", F32, "d_memn")
    g_mem_norm = gain_grad(mem2d, dmemn, "grad_mem_norm")

    dU3, dW3, dQ13, dK13, dA, dG13 = gdn_scan_bwd(do_gdn.reshape(B, S, 512), qkv3, W3, Vn3, GB3, A, St, B, S)
    r2 = lambda a: a.reshape(T, a.shape[-1])
    dqkv, dGB = gdn_chunk_bwd(qkv, GB, Grow, Tinv, dA, r2(dU3), r2(dW3), r2(dQ13), r2(dK13), r2(dG13), B, S)
    dPg, g_conv = gdn_prep_bwd(P, dqkv, gdn_conv, B, S)
    dab, g_ab = gdn_gate_bwd(P, dGB, alog_row, dt_row, B, S)

    dQ, dK, dV = mla_attn_bwd(Q, K, V, o_mla, do_mla, lse, B, S)
    dq_lin, dkv_lin, dkr = mla_post_bwd(dQ, dK, dV, cos_t, sin_t)
    dqn = mm(dq_lin, wq, "nn", F32, "d_qn")
    dkvn = mm(dkv_lin, wkv, "nt", F32, "d_kvn")
    g_wq = mm(dq_lin, qn, "tn", BF16, "grad_w_q_b")
    g_wkv = mm(kvn, dkv_lin, "tn", BF16, "grad_w_kv_b")
    dPm, g_q_a_norm, g_kv_a_norm = mla_norm_bwd(P, dqn, dkvn, dkr, dab, q_a_norm, kv_a_norm)

    dP = [dPm, dmemq, dPg, dgate]
    g_wp = mm_cols_tn(dP, h, BF16, "grad_w_in")
    started = big_grads_ready(dict(w_in=g_wp, w_q_b=g_wq, w_kv_b=g_wkv, w_mem_kv=g_w_mem_kv, w_out=g_w_out))
    dh = mm_cols_nn(dP, wp, F32, "d_h", started)
    grad_x, g_norm_in = in_norm_bwd(x2d, dh, dx2, norm_in)

    grads = dict(
        norm_in=g_norm_in, q_a_norm=g_q_a_norm, kv_a_norm=g_kv_a_norm, gdn_conv=g_conv,
        gdn_a_log=g_ab[0:1, :N_HEADS], gdn_dt_bias=g_ab[1:2, :N_HEADS], gdn_norm=g_gdn_norm,
        mem_norm=g_mem_norm, norm_final=g_norm_final)
    return sq, grad_x.reshape(B, S, D), grads


def kernel(x, mem, positions, norm_in, w_in, q_a_norm, w_q_b, kv_a_norm, w_kv_b, gdn_conv, gdn_a_log, gdn_dt_bias, gdn_norm, mem_norm, w_mem_kv, w_out, norm_final, loss_target, m_norm_in, m_w_in, m_q_a_norm, m_w_q_b, m_kv_a_norm, m_w_kv_b, m_gdn_conv, m_gdn_a_log, m_gdn_dt_bias, m_gdn_norm, m_mem_norm, m_w_mem_kv, m_w_out, m_norm_final, v_norm_in, v_w_in, v_q_a_norm, v_w_q_b, v_kv_a_norm, v_w_kv_b, v_gdn_conv, v_gdn_a_log, v_gdn_dt_bias, v_gdn_norm, v_mem_norm, v_w_mem_kv, v_w_out, v_norm_final):
    B = x.shape[0]
    cx, cy, cc = lax.axis_index("x"), lax.axis_index("y"), lax.axis_index("c")
    chip = 2 * cx + cy

    big_names = ("w_in", "w_q_b", "w_kv_b", "w_mem_kv", "w_out")
    rows_major = lambda a: jnp.transpose(a, (2, 0, 1))
    w_in3, m_in3, v_in3 = rows_major(w_in), rows_major(m_w_in), rows_major(v_w_in)
    w_qb_t, m_qb_t, v_qb_t = jnp.transpose(w_q_b[0]), jnp.transpose(m_w_q_b[0]), jnp.transpose(v_w_q_b[0])
    z32 = jnp.zeros((32, Q_LORA), BF16)
    qb_bf = w_qb_t.astype(BF16)
    qb_padded = jnp.concatenate([qb_bf[:160], z32, qb_bf[160:], z32])
    shards = [dense_bf16(w_in3, "w_in_bf16"), qb_padded, w_kv_b[0].astype(BF16), w_mem_kv[0].astype(BF16), w_out[0].astype(BF16)]
    splits = [(1, D_MODEL // 2)] + [(0, s.shape[0] // 2) for s in shards[1:]]
    (g_in,) = allgather_chips(shards[:1], ["lead"], "allgather_w_in")
    conv_all = allgather_devices(gdn_conv[0], "allgather_conv")
    conv_cols = gdn_conv.shape[2]
    conv_full = jnp.transpose(conv_all[0::2], (1, 0, 2)).reshape(GDN_CONV, N_CHIPS * conv_cols)
    late_a = late_gather_start(shards[1:3], "late_gather_qkv_start")[:3]
    late_b = late_gather_start(shards[3:], "late_gather_mem_out_start")[:3]
    late_shapes = [(N_CHIPS,) + s.shape for s in shards[1:]]

    def late_qkv(after):
        g_qb, g_kvb = late_gather_wait(*late_a, after, "late_gather_qkv_wait")
        return g_qb.reshape(-1, Q_LORA), _perm_w_kv_b(g_kvb)

    def late_mem_out(after):
        g_mem, g_out_w = late_gather_wait(*late_b, after, "late_gather_mem_out_wait")
        return g_mem.reshape(-1, g_mem.shape[2]), g_out_w.reshape(-1, g_out_w.shape[2])

    late_weights = (late_qkv, late_mem_out)

    core = jnp.stack([cc]).astype(jnp.int32)
    chip_core = jnp.stack([chip, cc]).astype(jnp.int32)
    exchange = {}

    def big_grads_ready(gb):
        parts = [unpad_w_in_t(gb["w_in"]), gb["w_q_b"].reshape(late_shapes[0]), _unperm_w_kv_b(gb["w_kv_b"]),
                 gb["w_mem_kv"].reshape(late_shapes[2]), gb["w_out"].reshape(late_shapes[3])]
        from_sibling = swap_sibling(parts, "rs_sibling_partial", splits)
        chip_sums = add_pairs(parts, from_sibling, splits, core, "rs_add_sibling")
        sems, sums_thru, lands, token = exchange_chips_start(chip_sums, "rs_exchange_start")
        exchange.update(parts=parts, from_sibling=from_sibling, sems=sems, sums=sums_thru, lands=lands)
        return token

    sq, grad_x, g = local_step(x, mem, positions, loss_target, norm_in, pad_w_in_t(g_in), late_weights, big_grads_ready, q_a_norm,
                               kv_a_norm, conv_full, gdn_a_log, gdn_dt_bias, gdn_norm, mem_norm, norm_final)

    small_names = ("norm_in", "q_a_norm", "kv_a_norm", "gdn_a_log", "gdn_dt_bias", "gdn_norm", "mem_norm", "norm_final")
    small = dict(norm_in=norm_in, q_a_norm=q_a_norm, kv_a_norm=kv_a_norm, gdn_a_log=gdn_a_log, gdn_dt_bias=gdn_dt_bias,
                 gdn_norm=gdn_norm, mem_norm=mem_norm, norm_final=norm_final)
    m_small = dict(norm_in=m_norm_in, q_a_norm=m_q_a_norm, kv_a_norm=m_kv_a_norm, gdn_a_log=m_gdn_a_log,
                   gdn_dt_bias=m_gdn_dt_bias, gdn_norm=m_gdn_norm, mem_norm=m_mem_norm, norm_final=m_norm_final)
    v_small = dict(norm_in=v_norm_in, q_a_norm=v_q_a_norm, kv_a_norm=v_kv_a_norm, gdn_a_log=v_gdn_a_log,
                   gdn_dt_bias=v_gdn_dt_bias, gdn_norm=v_gdn_norm, mem_norm=v_mem_norm, norm_final=v_norm_final)
    rows = lambda d: jnp.stack([jnp.pad(d[n].reshape(-1), (0, 1024 - d[n].size)) for n in small_names])
    conv_rows = GDN_CONV * GDN_QKV // 1024
    g_block = jnp.concatenate([rows(g), g["gdn_conv"].reshape(conv_rows, 1024), sq, jnp.zeros((16 - 9 - conv_rows, 1024), F32)])
    g_block = sum_leading(allgather_devices(g_block, "allgather_small_grads"), "sum_small_grads")
    g_small_rows = g_block[:8]
    loss = 0.5 * jnp.sum(g_block[8 + conv_rows]) / D_MODEL
    g_conv = lax.dynamic_slice_in_dim(g_block[8:8 + conv_rows].reshape(GDN_CONV, GDN_QKV), chip * conv_cols, conv_cols, axis=1)
    d_s, m_s, v_s = adamw(rows(small), g_small_rows, rows(m_small), rows(v_small), "adamw_small")
    unrow = lambda r: {n: r[i, :small[n].size].reshape(small[n].shape) for i, n in enumerate(small_names)}
    g_out, d_out, m_out, v_out = unrow(g_small_rows), unrow(d_s), unrow(m_s), unrow(v_s)

    from_chips = exchange_chips_wait(exchange["sems"], exchange["sums"], exchange["lands"], d_s, "rs_exchange_wait")
    my_half = add_fives(exchange["parts"], exchange["from_sibling"], from_chips, splits, chip_core, "rs_add_chips")
    other_half = swap_sibling(my_half, "rs_sibling_final")

    d_out["gdn_conv"], m_out["gdn_conv"], v_out["gdn_conv"] = adamw(gdn_conv, g_conv, m_gdn_conv, v_gdn_conv, "adamw_gdn_conv")
    g_out["gdn_conv"] = g_conv[None]
    res = adamw_untiled_rows(w_in3, my_half[0], other_half[0], m_in3, v_in3, "adamw_w_in")
    g_out["w_in"], d_out["w_in"], m_out["w_in"], v_out["w_in"] = [jnp.transpose(r, (1, 2, 0)) for r in res]
    res = adamw_w_q_b(w_qb_t, my_half[1], other_half[1], m_qb_t, v_qb_t, "adamw_w_q_b")
    g_out["w_q_b"], d_out["w_q_b"], m_out["w_q_b"], v_out["w_q_b"] = [jnp.transpose(r)[None] for r in res]
    rest = dict(w_kv_b=(w_kv_b, m_w_kv_b, v_w_kv_b), w_mem_kv=(w_mem_kv, m_w_mem_kv, v_w_mem_kv), w_out=(w_out, m_w_out, v_w_out))
    for i, n in enumerate(big_names):
        if n in rest:
            w_n, m_n, v_n = rest[n]
            g_out[n], d_out[n], m_out[n], v_out[n] = adamw_halves(w_n, my_half[i], other_half[i], m_n, v_n, splits[i], core, "adamw_" + n)

    order = ("norm_in", "w_in", "q_a_norm", "w_q_b", "kv_a_norm", "w_kv_b", "gdn_conv", "gdn_a_log", "gdn_dt_bias",
             "gdn_norm", "mem_norm", "w_mem_kv", "w_out", "norm_final")
    return (loss, grad_x, *[g_out[n] for n in order], *[d_out[n] for n in order], *[m_out[n] for n in order],
            *[v_out[n] for n in order])
```

```python
import functools
import math

import jax
import jax.numpy as jnp
import numpy as np
from jax import lax
from jax.experimental import pallas as pl
from jax.experimental.pallas import tpu as pltpu

F32 = jnp.float32
BF16 = jnp.bfloat16
BS = pl.BlockSpec

D_MODEL = 1024
N_HEADS = 4
MLA_NOPE, MLA_ROPE, MLA_V = 128, 64, 128
Q_LORA, KV_LORA = 384, 256
ROPE_THETA = 10000.0
GDN_DK = GDN_DV = 128
GDN_CONV = 4
CHUNK = 64
MEM_DH = 128
D_MIX = 1536
GDN_QKV = 1536
D_IN = 4296
EPS = 1e-6
ADAM_LR, ADAM_B1, ADAM_B2, ADAM_EPS, ADAM_WD, ADAM_STEP = 0.001, 0.9, 0.999, 1e-08, 0.01, 10

OFF_MLA = 0
OFF_MEMQ = 1024
OFF_GDN = 1536
OFF_GATE = 3072
N_PAD = 4608
HEAD_PAD = 256
MLA_SCALE = (MLA_NOPE + MLA_ROPE) ** -0.5
MEM_SCALE = MEM_DH ** -0.5
GDN_SCALE = GDN_DK ** -0.5
NEG = -1e30

NN = ((1,), (0,))
NT = ((1,), (1,))
TN = ((0,), (0,))


def _dot(a, b, dims):
    return lax.dot_general(a, b, (dims, ((), ())), preferred_element_type=F32)


def _bdot(spec, a, b, precision=None):
    return jnp.einsum(spec, a, b, preferred_element_type=F32, precision=precision)


def _arb(n):
    return pltpu.CompilerParams(dimension_semantics=("arbitrary",) * n)


def _sigmoid(x):
    return 1.0 / (1.0 + jnp.exp(-x))


def _softplus(z):
    return jnp.maximum(z, 0.0) + jnp.log(1.0 + jnp.exp(-jnp.abs(z)))


def _rope(t, cos_row, sin_row):
    return t * cos_row + pltpu.roll(t, 64, 1) * sin_row


def _rope_bwd(d, cos_row, sin_row):
    return d * cos_row + pltpu.roll(d * sin_row, 64, 1)


def rms_fwd(x, gain, name, tm=512):
    T, n = x.shape
    tm = min(tm, T)

    def body(x_ref, g_ref, o_ref):
        xv = x_ref[...]
        r = lax.rsqrt(jnp.mean(xv * xv, axis=-1, keepdims=True) + EPS)
        o_ref[...] = (xv * r * g_ref[...]).astype(BF16)

    return pl.pallas_call(
        body, name=name, grid=(T // tm,),
        in_specs=[BS((tm, n), lambda i: (i, 0)), BS((1, n), lambda i: (0, 0))],
        out_specs=BS((tm, n), lambda i: (i, 0)),
        out_shape=jax.ShapeDtypeStruct((T, n), BF16), compiler_params=_arb(1))(x, gain)


def mm(a, b, kind, out_dtype, name, bm=512, bn=None, n_outer=False, after=()):
    if kind == "nn":
        (M, K), (_, N) = a.shape, b.shape
    elif kind == "nt":
        (M, K), (N, _) = a.shape, b.shape
    else:
        (K, M), (_, N) = a.shape, b.shape
    bm, bn = min(bm, M), min(bn or N, N)
    assert M % bm == 0 and N % bn == 0, (name, M, N, K)
    ij = (lambda g0, g1: (g1, g0)) if n_outer else (lambda g0, g1: (g0, g1))
    a_spec = BS((K, bm), lambda g0, g1: (0, ij(g0, g1)[0])) if kind == "tn" else BS((bm, K), lambda g0, g1: (ij(g0, g1)[0], 0))
    once = dict(pipeline_mode=pl.Buffered(1)) if bn == N else {}
    b_spec = (BS((bn, K), lambda g0, g1: (ij(g0, g1)[1], 0), **once) if kind == "nt"
              else BS((K, bn), lambda g0, g1: (0, ij(g0, g1)[1]), **once))
    dims = {"nn": NN, "nt": NT, "tn": TN}[kind]

    def body(a_ref, b_ref, *rest):
        rest[-1][...] = _dot(a_ref[...].astype(BF16), b_ref[...].astype(BF16), dims).astype(out_dtype)

    grid = (N // bn, M // bm) if n_outer else (M // bm, N // bn)
    return pl.pallas_call(
        body, name=name, grid=grid, in_specs=[a_spec, b_spec] + [BS(memory_space=pl.ANY)] * len(after),
        out_specs=BS((bm, bn), lambda g0, g1: ij(g0, g1)),
        out_shape=jax.ShapeDtypeStruct((M, N), out_dtype), compiler_params=_arb(2))(a, b, *after)


def mm_cols_nn(pieces, b, out_dtype, name, after, bm=512):
    M, N = pieces[0].shape[0], b.shape[1]
    bm = min(bm, M)
    n = len(pieces)
    widths = [p.shape[1] for p in pieces]
    offs = [sum(widths[:i]) for i in range(n)]

    def body(*refs):
        b_ref, o_ref = refs[n], refs[-1]
        acc = None
        for a_ref, off, w in zip(refs[:n], offs, widths):
            d = _dot(a_ref[...], b_ref[off:off + w, :], NN)
            acc = d if acc is None else acc + d
        o_ref[...] = acc.astype(out_dtype)

    return pl.pallas_call(
        body, name=name, grid=(M // bm,),
        in_specs=[BS((bm, w), lambda i: (i, 0)) for w in widths]
        + [BS(b.shape, lambda i: (0, 0), pipeline_mode=pl.Buffered(1)), BS(memory_space=pl.ANY)],
        out_specs=BS((bm, N), lambda i: (i, 0)), out_shape=jax.ShapeDtypeStruct((M, N), out_dtype),
        compiler_params=_arb(1))(*pieces, b, after)


def mm_cols_tn(pieces, b, out_dtype, name, bm=512):
    K, N = b.shape
    tiles = [p.shape[1] // bm for p in pieces]
    firsts = [sum(tiles[:i]) for i in range(len(tiles))]

    def body(*refs):
        b_ref, o_ref = refs[-2], refs[-1]
        i = pl.program_id(0)
        for a_ref, t0, n in zip(refs[:-2], firsts, tiles):
            @pl.when((i >= t0) & (i < t0 + n))
            def _(a_ref=a_ref):
                o_ref[...] = _dot(a_ref[...], b_ref[...], TN).astype(out_dtype)

    a_specs = [BS((K, bm), lambda i, t0=t0, n=n: (0, jnp.clip(i - t0, 0, n - 1))) for t0, n in zip(firsts, tiles)]
    return pl.pallas_call(
        body, name=name, grid=(sum(tiles),),
        in_specs=a_specs + [BS(b.shape, lambda i: (0, 0), pipeline_mode=pl.Buffered(1))],
        out_specs=BS((bm, N), lambda i: (i, 0)), out_shape=jax.ShapeDtypeStruct((sum(tiles) * bm, N), out_dtype),
        compiler_params=_arb(1))(*pieces, b)


def rope_tables(pos_col, inv_row, sgn_row, msk_row, tm=512):
    T = pos_col.shape[0]
    tm = min(tm, T)

    def body(p_ref, inv_ref, sgn_ref, msk_ref, c_ref, s_ref):
        ang = p_ref[...].astype(F32) * inv_ref[...]
        c_ref[...] = jnp.cos(ang) * msk_ref[...]
        s_ref[...] = jnp.sin(ang) * sgn_ref[...]

    row = BS((1, 128), lambda i: (0, 0))
    return pl.pallas_call(
        body, name="rope_tables", grid=(T // tm,),
        in_specs=[BS((tm, 1), lambda i: (i, 0)), row, row, row],
        out_specs=[BS((tm, 128), lambda i: (i, 0))] * 2,
        out_shape=[jax.ShapeDtypeStruct((T, 128), F32)] * 2, compiler_params=_arb(1))(pos_col, inv_row, sgn_row, msk_row)


def mla_prep(P, gq, gkv, wq, wkv, cos_t, sin_t, tm=512):
    T = P.shape[0]
    tm = min(tm, T)

    def body(p_ref, gq_ref, gkv_ref, wq_ref, wkv_ref, c_ref, s_ref, q_ref, k_ref, v_ref, qn_ref, kvn_ref):
        p = p_ref[...]
        cq, ckv, kr = p[:, :Q_LORA], p[:, Q_LORA:Q_LORA + KV_LORA], p[:, 640:768]
        qn = (cq * lax.rsqrt(jnp.mean(cq * cq, axis=-1, keepdims=True) + EPS) * gq_ref[...]).astype(BF16)
        kvn = (ckv * lax.rsqrt(jnp.mean(ckv * ckv, axis=-1, keepdims=True) + EPS) * gkv_ref[...]).astype(BF16)
        qn_ref[...] = qn
        kvn_ref[...] = kvn
        q = _dot(qn, wq_ref[...], NT)
        kv = _dot(kvn, wkv_ref[...], NN)
        cos_row, sin_row = c_ref[...], s_ref[...]
        krr = _rope(kr, cos_row, sin_row).astype(BF16)
        for h in range(N_HEADS):
            lo = h * HEAD_PAD
            q_ref[:, lo:lo + 128] = (q[:, lo:lo + 128] * MLA_SCALE).astype(BF16)
            q_ref[:, lo + 128:lo + 256] = (_rope(q[:, lo + 128:lo + 256], cos_row, sin_row) * MLA_SCALE).astype(BF16)
            k_ref[:, lo:lo + 128] = kv[:, h * 128:(h + 1) * 128].astype(BF16)
            k_ref[:, lo + 128:lo + 256] = krr
            v_ref[:, lo:lo + 128] = kv[:, 512 + h * 128:512 + (h + 1) * 128].astype(BF16)
            v_ref[:, lo + 128:lo + 256] = jnp.ones((tm, 128), BF16)

    full = lambda r, c: BS((r, c), lambda i: (0, 0))
    rowb = lambda c: BS((tm, c), lambda i: (i, 0))
    return pl.pallas_call(
        body, name="mla_prep", grid=(T // tm,),
        in_specs=[rowb(1024), full(1, Q_LORA), full(1, KV_LORA), full(1024, Q_LORA), full(KV_LORA, 1024), rowb(128), rowb(128)],
        out_specs=[rowb(1024), rowb(1024), rowb(1024), rowb(Q_LORA), rowb(KV_LORA)],
        out_shape=[jax.ShapeDtypeStruct((T, 1024), BF16), jax.ShapeDtypeStruct((T, 1024), BF16),
                   jax.ShapeDtypeStruct((T, 1024), BF16), jax.ShapeDtypeStruct((T, Q_LORA), BF16),
                   jax.ShapeDtypeStruct((T, KV_LORA), BF16)],
        compiler_params=_arb(1))(P, gq, gkv, wq, wkv, cos_t, sin_t)


ATTN_HEADS_PER_STEP = 2
ATTN_STRIP = 32


def mla_attn_fwd(Q, K, V, B, S, tq=512, hp=ATTN_HEADS_PER_STEP):
    T = B * S
    tq = min(tq, S)
    nq = S // tq

    rs = min(ATTN_STRIP, tq)

    def body(q_ref, k_ref, v_ref, o_ref, lse_ref, m_s, acc_s, s_s, p_s, a_s):
        i = pl.program_id(2)
        m_s[...] = jnp.full_like(m_s, NEG)
        acc_s[...] = jnp.zeros_like(acc_s)

        def blk(j, masked):
            rows = pl.ds(pl.multiple_of(j * tq, tq), tq)
            for h in range(hp):
                hq = slice(h * HEAD_PAD, (h + 1) * HEAD_PAD)
                s_s[h] = _dot(q_ref[:, hq], k_ref[rows, hq], NT)
            for h in range(hp):
                for r0 in range(0, tq, rs):
                    rr = slice(r0, r0 + rs)
                    sv = s_s[h, rr, :]
                    if masked:
                        r = r0 + lax.broadcasted_iota(jnp.int32, (rs, tq), 0)
                        c = lax.broadcasted_iota(jnp.int32, (rs, tq), 1)
                        sv = jnp.where(r >= c, sv, NEG)
                    m_prev = m_s[h, rr, :]
                    m_new = jnp.maximum(m_prev, jnp.max(sv, axis=1, keepdims=True))
                    p_s[h, rr, :] = jnp.exp(sv - m_new).astype(BF16)
                    a_s[h, rr, :] = jnp.exp(m_prev - m_new)
                    m_s[h, rr, :] = m_new
            for h in range(hp):
                hq = slice(h * HEAD_PAD, (h + 1) * HEAD_PAD)
                acc_s[h] = a_s[h] * acc_s[h] + _dot(p_s[h], v_ref[rows, hq], NN)

        def loop(j, c):
            blk(j, False)
            return c

        lax.fori_loop(0, i, loop, 0)
        blk(i, True)
        for h in range(hp):
            den = acc_s[h, :, 128:256]
            o_ref[:, h * 128:(h + 1) * 128] = acc_s[h, :, 0:128] / den
            lse_ref[h] = m_s[h] + jnp.log(den[:, 0:1])

    return pl.pallas_call(
        body, name="mla_attn_fwd", grid=(B, N_HEADS // hp, nq),
        in_specs=[BS((tq, hp * HEAD_PAD), lambda b, h, i: (b * nq + i, h)),
                  BS((S, hp * HEAD_PAD), lambda b, h, i: (b, h)),
                  BS((S, hp * HEAD_PAD), lambda b, h, i: (b, h))],
        out_specs=[BS((tq, hp * 128), lambda b, h, i: (b * nq + i, h)),
                   BS((hp, tq, 1), lambda b, h, i: (h, b * nq + i, 0))],
        out_shape=[jax.ShapeDtypeStruct((T, 512), F32), jax.ShapeDtypeStruct((N_HEADS, T, 1), F32)],
        scratch_shapes=[pltpu.VMEM((hp, tq, 1), F32), pltpu.VMEM((hp, tq, HEAD_PAD), F32), pltpu.VMEM((hp, tq, tq), F32),
                        pltpu.VMEM((hp, tq, tq), BF16), pltpu.VMEM((hp, tq, 1), F32)],
        compiler_params=_arb(3))(Q, K, V)


def mla_attn_bwd(Q, K, V, O, dO, LSE, B, S, tq=512, hp=ATTN_HEADS_PER_STEP):
    T = B * S
    tq = min(tq, S)
    nq = S // tq

    def body(q_ref, k_ref, v_ref, o_ref, do_ref, lse_ref, dq_ref, dk_ref, dv_ref, delta_s, dk_s, dv_s):
        j = pl.program_id(2)

        @pl.when(j == 0)
        def _():
            dq_ref[...] = jnp.zeros_like(dq_ref)
            for h in range(hp):
                sl = slice(h * 128, (h + 1) * 128)
                delta_s[h] = jnp.sum(do_ref[:, sl] * o_ref[:, sl], axis=1, keepdims=True)

        dk_s[...] = jnp.zeros_like(dk_s)
        dv_s[...] = jnp.zeros_like(dv_s)

        def step(i, c):
            rows = pl.ds(pl.multiple_of(i * tq, tq), tq)
            r = i * tq + lax.broadcasted_iota(jnp.int32, (tq, tq), 0)
            cc = j * tq + lax.broadcasted_iota(jnp.int32, (tq, tq), 1)
            causal = r >= cc
            for h in range(hp):
                sq, sv = slice(h * HEAD_PAD, (h + 1) * HEAD_PAD), slice(h * 128, (h + 1) * 128)
                q = q_ref[rows, sq]
                k = k_ref[:, sq]
                do = do_ref[rows, sv].astype(BF16)
                s = _dot(q, k, NT)
                p = jnp.where(causal, jnp.exp(s - lse_ref[h, rows, :]), 0.0)
                dv_s[:, sv] += _dot(p.astype(BF16), do, TN)
                dp = _dot(do, v_ref[:, h * HEAD_PAD:h * HEAD_PAD + 128], NT)
                ds = (p * (dp - delta_s[h, rows, :])).astype(BF16)
                dk_s[:, sq] += _dot(ds, q, TN)
                dq_ref[rows, sq] += _dot(ds, k, NN)
            return c

        lax.fori_loop(j, nq, step, 0)
        dk_ref[...] = dk_s[...]
        dv_ref[...] = dv_s[...]

    seq = lambda c: BS((S, c), lambda b, h, j: (b, h))
    blk = lambda c: BS((tq, c), lambda b, h, j: (b * nq + j, h))
    return pl.pallas_call(
        body, name="mla_attn_bwd", grid=(B, N_HEADS // hp, nq),
        in_specs=[seq(hp * HEAD_PAD), blk(hp * HEAD_PAD), blk(hp * HEAD_PAD), seq(hp * 128), seq(hp * 128),
                  BS((hp, S, 1), lambda b, h, j: (h, b, 0))],
        out_specs=[seq(hp * HEAD_PAD), blk(hp * HEAD_PAD), blk(hp * 128)],
        out_shape=[jax.ShapeDtypeStruct((T, 1024), F32), jax.ShapeDtypeStruct((T, 1024), F32),
                   jax.ShapeDtypeStruct((T, 512), F32)],
        scratch_shapes=[pltpu.VMEM((hp, S, 1), F32), pltpu.VMEM((tq, hp * HEAD_PAD), F32), pltpu.VMEM((tq, hp * 128), F32)],
        compiler_params=_arb(3))(Q, K, V, O, dO, LSE)


def mla_post_bwd(dQ, dK, dV, cos_t, sin_t, tm=512):
    T = dQ.shape[0]
    tm = min(tm, T)

    def body(dq_ref, dk_ref, dv_ref, c_ref, s_ref, ql_ref, kvl_ref, kr_ref):
        cos_row, sin_row = c_ref[...], s_ref[...]
        kr = jnp.zeros((tm, 128), F32)
        for h in range(N_HEADS):
            lo = h * HEAD_PAD
            ql_ref[:, lo:lo + 128] = (dq_ref[:, lo:lo + 128] * MLA_SCALE).astype(BF16)
            ql_ref[:, lo + 128:lo + 256] = (_rope_bwd(dq_ref[:, lo + 128:lo + 256], cos_row, sin_row) * MLA_SCALE).astype(BF16)
            kvl_ref[:, h * 128:(h + 1) * 128] = dk_ref[:, lo:lo + 128].astype(BF16)
            kr = kr + dk_ref[:, lo + 128:lo + 256]
        kvl_ref[:, 512:] = dv_ref[...].astype(BF16)
        kr_ref[...] = _rope_bwd(kr, cos_row, sin_row)

    rowb = lambda c: BS((tm, c), lambda i: (i, 0))
    return pl.pallas_call(
        body, name="mla_post_bwd", grid=(T // tm,),
        in_specs=[rowb(1024), rowb(1024), rowb(512), rowb(128), rowb(128)],
        out_specs=[rowb(1024), rowb(1024), rowb(128)],
        out_shape=[jax.ShapeDtypeStruct((T, 1024), BF16), jax.ShapeDtypeStruct((T, 1024), BF16),
                   jax.ShapeDtypeStruct((T, 128), F32)],
        compiler_params=_arb(1))(dQ, dK, dV, cos_t, sin_t)


def mla_norm_bwd(P, dqn, dkvn, dkr, dab, gq, gkv, tm=512):
    T = P.shape[0]
    tm = min(tm, T)

    def norm_bwd(x, dy, g):
        r = lax.rsqrt(jnp.mean(x * x, axis=-1, keepdims=True) + EPS)
        xh = x * r
        dxh = dy * g
        return r * (dxh - xh * jnp.mean(dxh * xh, axis=-1, keepdims=True)), jnp.sum(dy * xh, axis=0, keepdims=True)

    def body(p_ref, dqn_ref, dkvn_ref, dkr_ref, dab_ref, gq_ref, gkv_ref, o_ref, aq_ref, akv_ref):
        @pl.when(pl.program_id(0) == 0)
        def _():
            aq_ref[...] = jnp.zeros_like(aq_ref)
            akv_ref[...] = jnp.zeros_like(akv_ref)

        dcq, ggq = norm_bwd(p_ref[:, :Q_LORA], dqn_ref[...], gq_ref[...])
        dckv, ggkv = norm_bwd(p_ref[:, Q_LORA:640], dkvn_ref[...], gkv_ref[...])
        aq_ref[...] += ggq
        akv_ref[...] += ggkv
        o_ref[:, :Q_LORA] = dcq.astype(BF16)
        o_ref[:, Q_LORA:640] = dckv.astype(BF16)
        o_ref[:, 640:768] = dkr_ref[...].astype(BF16)
        o_ref[:, 768:896] = dab_ref[...]
        o_ref[:, 896:1024] = jnp.zeros((tm, 128), BF16)

    rowb = lambda c: BS((tm, c), lambda i: (i, 0))
    full = lambda c: BS((1, c), lambda i: (0, 0))
    return pl.pallas_call(
        body, name="mla_norm_bwd", grid=(T // tm,),
        in_specs=[rowb(1024), rowb(Q_LORA), rowb(KV_LORA), rowb(128), rowb(128), full(Q_LORA), full(KV_LORA)],
        out_specs=[rowb(1024), full(Q_LORA), full(KV_LORA)],
        out_shape=[jax.ShapeDtypeStruct((T, 1024), BF16), jax.ShapeDtypeStruct((1, Q_LORA), F32),
                   jax.ShapeDtypeStruct((1, KV_LORA), F32)],
        compiler_params=_arb(1))(P, dqn, dkvn, dkr, dab, gq, gkv)


def _mem_probs(qh, kh):
    s = _dot(qh, kh, NT) * MEM_SCALE
    p = jnp.exp(s - jnp.max(s, axis=1, keepdims=True))
    return p / jnp.sum(p, axis=1, keepdims=True)


def mem_attn_fwd(P, MKV, B, S, M, tq=512):
    T = B * S
    tq = min(tq, S)
    nq = S // tq

    def body(q_ref, kv_ref, o_ref):
        for h in range(N_HEADS):
            sl = slice(h * 128, (h + 1) * 128)
            p = _mem_probs(q_ref[:, sl].astype(BF16), kv_ref[:, sl])
            o_ref[:, sl] = _dot(p.astype(BF16), kv_ref[:, 512 + h * 128:512 + (h + 1) * 128], NN)

    return pl.pallas_call(
        body, name="mem_attn_fwd", grid=(B, nq),
        in_specs=[BS((tq, 512), lambda b, i: (b * nq + i, OFF_MEMQ // 512)), BS((M, 1024), lambda b, i: (b, 0))],
        out_specs=BS((tq, 512), lambda b, i: (b * nq + i, 0)),
        out_shape=jax.ShapeDtypeStruct((T, 512), F32), compiler_params=_arb(2))(P, MKV)


def mem_attn_bwd(P, MKV, dO, B, S, M, tq=512):
    T = B * S
    tq = min(tq, S)
    nq = S // tq

    def body(q_ref, kv_ref, do_ref, dq_ref, dkv_ref):
        @pl.when(pl.program_id(1) == 0)
        def _():
            dkv_ref[...] = jnp.zeros_like(dkv_ref)

        for h in range(N_HEADS):
            sl = slice(h * 128, (h + 1) * 128)
            sv = slice(512 + h * 128, 512 + (h + 1) * 128)
            qh = q_ref[:, sl].astype(BF16)
            kh = kv_ref[:, sl]
            do = do_ref[:, sl].astype(BF16)
            p = _mem_probs(qh, kh)
            dkv_ref[:, sv] += _dot(p.astype(BF16), do, TN)
            dp = _dot(do, kv_ref[:, sv], NT)
            ds = (p * (dp - jnp.sum(dp * p, axis=1, keepdims=True)) * MEM_SCALE).astype(BF16)
            dq_ref[:, sl] = _dot(ds, kh, NN).astype(BF16)
            dkv_ref[:, sl] += _dot(ds, qh, TN)

    return pl.pallas_call(
        body, name="mem_attn_bwd", grid=(B, nq),
        in_specs=[BS((tq, 512), lambda b, i: (b * nq + i, OFF_MEMQ // 512)), BS((M, 1024), lambda b, i: (b, 0)),
                  BS((tq, 512), lambda b, i: (b * nq + i, 0))],
        out_specs=[BS((tq, 512), lambda b, i: (b * nq + i, 0)), BS((M, 1024), lambda b, i: (b, 0))],
        out_shape=[jax.ShapeDtypeStruct((T, 512), BF16), jax.ShapeDtypeStruct((B * M, 1024), F32)],
        compiler_params=_arb(2))(P, MKV, dO)


def gain_grad(x, dy, name, tm=256):
    T, n = x.shape
    tm = min(tm, T)

    def body(x_ref, dy_ref, o_ref):
        @pl.when(pl.program_id(0) == 0)
        def _():
            o_ref[...] = jnp.zeros_like(o_ref)

        xv = x_ref[...]
        xh = xv * lax.rsqrt(jnp.mean(xv * xv, axis=-1, keepdims=True) + EPS)
        o_ref[...] += jnp.sum(dy_ref[...] * xh, axis=0, keepdims=True)

    return pl.pallas_call(
        body, name=name, grid=(T // tm,),
        in_specs=[BS((tm, n), lambda i: (i, 0))] * 2, out_specs=BS((1, n), lambda i: (0, 0)),
        out_shape=jax.ShapeDtypeStruct((1, n), F32), compiler_params=_arb(1))(x, dy)


def _conv_silu(x, w, t):
    y = x * w[3:4, :]
    for s in range(1, GDN_CONV):
        y = y + jnp.where(t >= s, pltpu.roll(x, s, 0), 0.0) * w[3 - s:4 - s, :]
    return y, _sigmoid(y)


def gdn_prep_fwd(P, conv_w, B, S):
    T = B * S

    def body(x_ref, w_ref, o_ref):
        kind = pl.program_id(1)
        t = lax.broadcasted_iota(jnp.int32, (S, 1), 0)
        y, sg = _conv_silu(x_ref[...], w_ref[...], t)
        a = y * sg
        scale = jnp.where(kind == 0, GDN_SCALE, 1.0).astype(F32)
        for h in range(N_HEADS):
            sl = slice(h * 128, (h + 1) * 128)
            seg = a[:, sl]
            n = lax.rsqrt(jnp.sum(seg * seg, axis=-1, keepdims=True) + EPS)
            o_ref[:, sl] = jnp.where(kind < 2, seg * (n * scale), seg)

    return pl.pallas_call(
        body, name="gdn_prep_fwd", grid=(B, 3),
        in_specs=[BS((S, 512), lambda b, k: (b, OFF_GDN // 512 + k)), BS((GDN_CONV, 512), lambda b, k: (0, k))],
        out_specs=BS((S, 512), lambda b, k: (b, k)),
        out_shape=jax.ShapeDtypeStruct((T, GDN_QKV), F32), compiler_params=_arb(2))(P, conv_w)


def gdn_prep_bwd(P, dqkv, conv_w, B, S):
    T = B * S

    def body(x_ref, d_ref, w_ref, o_ref, gw_ref):
        kind = pl.program_id(0)

        @pl.when(pl.program_id(1) == 0)
        def _():
            gw_ref[...] = jnp.zeros_like(gw_ref)

        t = lax.broadcasted_iota(jnp.int32, (S, 1), 0)
        x = x_ref[...]
        w = w_ref[...]
        y, sg = _conv_silu(x, w, t)
        a = y * sg
        scale = jnp.where(kind == 0, GDN_SCALE, 1.0).astype(F32)
        das = []
        for h in range(N_HEADS):
            sl = slice(h * 128, (h + 1) * 128)
            seg, dseg = a[:, sl], d_ref[:, sl]
            n = lax.rsqrt(jnp.sum(seg * seg, axis=-1, keepdims=True) + EPS)
            dn = scale * (n * dseg - seg * (n * n * n) * jnp.sum(dseg * seg, axis=-1, keepdims=True))
            das.append(jnp.where(kind < 2, dn, dseg))
        dy = jnp.concatenate(das, axis=1) * (sg * (1.0 + y * (1.0 - sg)))
        dx = dy * w[3:4, :]
        gw_ref[3:4, :] += jnp.sum(dy * x, axis=0, keepdims=True)
        for s in range(1, GDN_CONV):
            dx = dx + jnp.where(t + s < S, pltpu.roll(dy, S - s, 0), 0.0) * w[3 - s:4 - s, :]
            gw_ref[3 - s:4 - s, :] += jnp.sum(dy * jnp.where(t >= s, pltpu.roll(x, s, 0), 0.0), axis=0, keepdims=True)
        o_ref[...] = dx.astype(BF16)

    return pl.pallas_call(
        body, name="gdn_prep_bwd", grid=(3, B),
        in_specs=[BS((S, 512), lambda k, b: (b, OFF_GDN // 512 + k)), BS((S, 512), lambda k, b: (b, k)),
                  BS((GDN_CONV, 512), lambda k, b: (0, k))],
        out_specs=[BS((S, 512), lambda k, b: (b, k)), BS((GDN_CONV, 512), lambda k, b: (0, k))],
        out_shape=[jax.ShapeDtypeStruct((T, GDN_QKV), BF16), jax.ShapeDtypeStruct((GDN_CONV, GDN_QKV), F32)],
        compiler_params=_arb(2))(P, dqkv, conv_w)


def _chunk_row(n_rows):
    return lax.broadcasted_iota(jnp.int32, (n_rows, 1), 0) % CHUNK


def gdn_gate_fwd(P, alog_row, dt_row, B, S):
    T = B * S

    def body(x_ref, al_ref, dt_ref, o_ref):
        x = x_ref[...]
        lane = lax.broadcasted_iota(jnp.int32, (1, 128), 1)
        g = jnp.where(lane < 4, -jnp.exp(al_ref[...]) * _softplus(x + dt_ref[...]), 0.0)
        t = _chunk_row(S)
        for s in (1, 2, 4, 8, 16, 32):
            g = g + jnp.where(t >= s, pltpu.roll(g, s, 0), 0.0)
        o_ref[...] = jnp.where(lane < 4, g, jnp.where(lane < 8, _sigmoid(x), 0.0))

    row = BS((1, 128), lambda b: (0, 0))
    return pl.pallas_call(
        body, name="gdn_gate_fwd", grid=(B,),
        in_specs=[BS((S, 128), lambda b: (b, 768 // 128)), row, row], out_specs=BS((S, 128), lambda b: (b, 0)),
        out_shape=jax.ShapeDtypeStruct((T, 128), F32), compiler_params=_arb(1))(P, alog_row, dt_row)


def gdn_gate_bwd(P, dGB, alog_row, dt_row, B, S):
    T = B * S

    def body(x_ref, d_ref, al_ref, dt_ref, o_ref, acc_ref):
        @pl.when(pl.program_id(0) == 0)
        def _():
            acc_ref[...] = jnp.zeros_like(acc_ref)

        x, d = x_ref[...], d_ref[...]
        lane = lax.broadcasted_iota(jnp.int32, (1, 128), 1)
        z = x + dt_ref[...]
        coef = -jnp.exp(al_ref[...])
        g = coef * _softplus(z)
        da = jnp.where(lane < 4, d * coef * _sigmoid(z), 0.0)
        beta = _sigmoid(x)
        o_ref[...] = jnp.where(lane < 4, da, jnp.where(lane < 8, d * beta * (1.0 - beta), 0.0)).astype(BF16)
        acc_ref[0:1, :] += jnp.sum(jnp.where(lane < 4, d * g, 0.0), axis=0, keepdims=True)
        acc_ref[1:2, :] += jnp.sum(da, axis=0, keepdims=True)

    row = BS((1, 128), lambda b: (0, 0))
    return pl.pallas_call(
        body, name="gdn_gate_bwd", grid=(B,),
        in_specs=[BS((S, 128), lambda b: (b, 768 // 128)), BS((S, 128), lambda b: (b, 0)), row, row],
        out_specs=[BS((S, 128), lambda b: (b, 0)), BS((8, 128), lambda b: (0, 0))],
        out_shape=[jax.ShapeDtypeStruct((T, 128), BF16), jax.ShapeDtypeStruct((8, 128), F32)],
        compiler_params=_arb(1))(P, dGB, alog_row, dt_row)


def _chunk_masks(nc):
    r = lax.broadcasted_iota(jnp.int32, (nc, CHUNK, CHUNK), 1)
    c = lax.broadcasted_iota(jnp.int32, (nc, CHUNK, CHUNK), 2)
    return r >= c, r > c


def _chunk_local(q, k, gc, gr, beta, incl, strict):
    decay = jnp.exp(jnp.where(incl, gc - gr, NEG))
    kb = k * beta
    kbf = k.astype(BF16)
    m_kk = _bdot("gcd,gjd->gcj", kb.astype(BF16), kbf)
    l_mat = jnp.where(strict, m_kk * decay, 0.0)
    a_mat = _bdot("gcd,gjd->gcj", q.astype(BF16), kbf) * decay
    return decay, kb, l_mat, a_mat


def _split_bf16(x):
    hi = x.astype(BF16)
    return hi, (x - hi.astype(F32)).astype(BF16)


def _mm_split(ah, al, bh, bl):
    spec = "gij,gjk->gik"
    return _bdot(spec, ah, bh) + (_bdot(spec, ah, bl) + _bdot(spec, al, bh))


def gdn_chunk_fwd(qkv, GB, Grow, B, S, nc=8):
    T = B * S
    N = S // CHUNK
    nc = min(nc, N)
    nb = N // nc
    R = nc * CHUNK

    def body(q_ref, k_ref, v_ref, gb_ref, gr_ref, u_ref, w_ref, t_ref, a_ref):
        incl, strict = _chunk_masks(nc)
        eye = (lax.broadcasted_iota(jnp.int32, (nc, CHUNK, CHUNK), 1)
               == lax.broadcasted_iota(jnp.int32, (nc, CHUNK, CHUNK), 2)).astype(F32)
        for h in range(N_HEADS):
            sl = slice(h * 128, (h + 1) * 128)
            q = q_ref[:, sl].reshape(nc, CHUNK, 128)
            k = k_ref[:, sl].reshape(nc, CHUNK, 128)
            v = v_ref[:, sl].reshape(nc, CHUNK, 128)
            gc = gb_ref[:, h:h + 1].reshape(nc, CHUNK, 1)
            beta = gb_ref[:, 4 + h:5 + h].reshape(nc, CHUNK, 1)
            gr = gr_ref[h][:, None, :]
            _, kb, l_mat, a_mat = _chunk_local(q, k, gc, gr, beta, incl, strict)
            pw = -l_mat
            tinv = eye + pw
            for _ in range(5):
                ph, pl_ = _split_bf16(pw)
                pw = _mm_split(ph, pl_, ph, pl_)
                ph, pl_ = _split_bf16(pw)
                th, tl = _split_bf16(tinv)
                tinv = tinv + _mm_split(th, tl, ph, pl_)
            tb = tinv.astype(BF16)
            u = _bdot("gcj,gjv->gcv", tb, (v * beta).astype(BF16))
            w = _bdot("gcj,gjk->gck", tb, (kb * jnp.exp(gc)).astype(BF16))
            u_ref[:, sl] = u.reshape(R, 128)
            w_ref[:, sl] = w.reshape(R, 128)
            t_ref[h] = tinv
            a_ref[h] = a_mat

    rowb = lambda c, j: BS((R, c), lambda b, n: (b * nb + n, j))
    mat = BS((None, N_HEADS, nc, CHUNK, CHUNK), lambda b, n: (b, 0, n, 0, 0))
    return pl.pallas_call(
        body, name="gdn_chunk_fwd", grid=(B, nb),
        in_specs=[rowb(512, 0), rowb(512, 1), rowb(512, 2), rowb(128, 0),
                  BS((None, N_HEADS, nc, CHUNK), lambda b, n: (b, 0, n, 0))],
        out_specs=[rowb(512, 0), rowb(512, 0), mat, mat],
        out_shape=[jax.ShapeDtypeStruct((T, 512), F32), jax.ShapeDtypeStruct((T, 512), F32),
                   jax.ShapeDtypeStruct((B, N_HEADS, N, CHUNK, CHUNK), F32),
                   jax.ShapeDtypeStruct((B, N_HEADS, N, CHUNK, CHUNK), F32)],
        compiler_params=_arb(2))(qkv, qkv, qkv, GB, Grow)


def gdn_scan_fwd(qkv3, U3, W3, GB3, A, B, S):
    N = S // CHUNK

    def body(q_ref, k_ref, u_ref, w_ref, gb_ref, a_ref, o_ref, vn_ref, st_ref, s_s):
        @pl.when(pl.program_id(0) == 0)
        def _():
            s_s[...] = jnp.zeros_like(s_s)

        for b in range(B):
            for h in range(N_HEADS):
                sl = slice(h * 128, (h + 1) * 128)
                st = s_s[b, h]
                st_ref[b, h] = st
                stb = st.astype(BF16)
                g = gb_ref[b, :, h:h + 1]
                gl = g[CHUNK - 1:CHUNK, :]
                vn = u_ref[b, :, sl] - _dot(w_ref[b, :, sl].astype(BF16), stb, NN)
                vnb = vn.astype(BF16)
                o = _dot((q_ref[b, :, sl] * jnp.exp(g)).astype(BF16), stb, NN) + _dot(a_ref[b, h].astype(BF16), vnb, NN)
                vn_ref[b, :, sl] = vn
                o_ref[b, :, sl] = o
                s_s[b, h] = st * jnp.exp(gl) + _dot((k_ref[b, :, sl] * jnp.exp(gl - g)).astype(BF16), vnb, TN)

    tok = lambda c, j: BS((B, CHUNK, c), lambda n: (0, n, j))
    return pl.pallas_call(
        body, name="gdn_scan_fwd", grid=(N,),
        in_specs=[tok(512, 0), tok(512, 1), tok(512, 0), tok(512, 0), tok(128, 0),
                  BS((B, N_HEADS, None, CHUNK, CHUNK), lambda n: (0, 0, n, 0, 0))],
        out_specs=[tok(512, 0), tok(512, 0), BS((B, N_HEADS, None, 128, 128), lambda n: (0, 0, n, 0, 0))],
        out_shape=[jax.ShapeDtypeStruct((B, S, 512), F32), jax.ShapeDtypeStruct((B, S, 512), F32),
                   jax.ShapeDtypeStruct((B, N_HEADS, N, 128, 128), F32)],
        scratch_shapes=[pltpu.VMEM((B, N_HEADS, 128, 128), F32)],
        compiler_params=_arb(1))(qkv3, qkv3, U3, W3, GB3, A)


def gdn_scan_bwd(dO3, qkv3, W3, Vn3, GB3, A, St, B, S):
    N = S // CHUNK

    def body(do_ref, q_ref, k_ref, w_ref, vn_ref, gb_ref, a_ref, st_ref,
             du_ref, dw_ref, dq_ref, dk_ref, da_ref, dg_ref, ds_s):
        @pl.when(pl.program_id(0) == 0)
        def _():
            ds_s[...] = jnp.zeros_like(ds_s)

        lane = lax.broadcasted_iota(jnp.int32, (1, 128), 1)
        last = lax.broadcasted_iota(jnp.int32, (CHUNK, 1), 0) == CHUNK - 1
        for b in range(B):
            dg_all = jnp.zeros((CHUNK, 128), F32)
            for h in range(N_HEADS):
                sl = slice(h * 128, (h + 1) * 128)
                st = st_ref[b, h]
                stb = st.astype(BF16)
                dsn = ds_s[b, h]
                dsnb = dsn.astype(BF16)
                g = gb_ref[b, :, h:h + 1]
                gl = g[CHUNK - 1:CHUNK, :]
                egl = jnp.exp(gl)
                ekd = jnp.exp(gl - g)
                eg = jnp.exp(g)
                q, k = q_ref[b, :, sl], k_ref[b, :, sl]
                kd = k * ekd
                qg = q * eg
                do = do_ref[b, :, sl].astype(BF16)
                vnb = vn_ref[b, :, sl].astype(BF16)
                dvn = _dot(a_ref[b, h].astype(BF16), do, TN) + _dot(kd.astype(BF16), dsnb, NN)
                dvnb = dvn.astype(BF16)
                da_ref[b, h] = _dot(do, vnb, NT)
                dqg = _dot(do, stb, NT)
                dkd = _dot(vnb, dsnb, NT)
                ds_s[b, h] = (_dot(qg.astype(BF16), do, TN) + egl * dsn - _dot(w_ref[b, :, sl].astype(BF16), dvnb, TN))
                du_ref[b, :, sl] = dvn
                dw_ref[b, :, sl] = -_dot(dvnb, stb, NT)
                dq_ref[b, :, sl] = dqg * eg
                dk_ref[b, :, sl] = dkd * ekd
                ddel = jnp.sum(dkd * kd, axis=1, keepdims=True)
                dgl = jnp.sum(ddel, axis=0, keepdims=True) + jnp.sum(jnp.sum(st * dsn, axis=1, keepdims=True), axis=0, keepdims=True) * egl
                col = jnp.sum(dqg * qg, axis=1, keepdims=True) - ddel + jnp.where(last, dgl, 0.0)
                dg_all = jnp.where(lane == h, col, dg_all)
            dg_ref[b] = dg_all

    tok = lambda c, j: BS((B, CHUNK, c), lambda n: (0, N - 1 - n, j))
    mat = lambda d: BS((B, N_HEADS, None, d, d), lambda n: (0, 0, N - 1 - n, 0, 0))
    return pl.pallas_call(
        body, name="gdn_scan_bwd", grid=(N,),
        in_specs=[tok(512, 0), tok(512, 0), tok(512, 1), tok(512, 0), tok(512, 0), tok(128, 0), mat(CHUNK), mat(128)],
        out_specs=[tok(512, 0), tok(512, 0), tok(512, 0), tok(512, 0), mat(CHUNK), tok(128, 0)],
        out_shape=[jax.ShapeDtypeStruct((B, S, 512), F32)] * 4
        + [jax.ShapeDtypeStruct((B, N_HEADS, N, CHUNK, CHUNK), F32), jax.ShapeDtypeStruct((B, S, 128), F32)],
        scratch_shapes=[pltpu.VMEM((B, N_HEADS, 128, 128), F32)],
        compiler_params=_arb(1))(dO3, qkv3, qkv3, W3, Vn3, GB3, A, St)


def gdn_chunk_bwd(qkv, GB, Grow, Tinv, dA, dU, dW, dQ1, dK1, dG1, B, S, nc=8):
    T = B * S
    N = S // CHUNK
    nc = min(nc, N)
    nb = N // nc
    R = nc * CHUNK

    def body(q_ref, k_ref, v_ref, gb_ref, gr_ref, t_ref, da_ref, du_ref, dw_ref, dq1_ref, dk1_ref, dg1_ref, o_ref, dgb_ref):
        incl, strict = _chunk_masks(nc)
        lane = lax.broadcasted_iota(jnp.int32, (1, 128), 1)
        dg_all = dg1_ref[...]
        db_all = jnp.zeros((R, 128), F32)
        for h in range(N_HEADS):
            sl = slice(h * 128, (h + 1) * 128)
            q = q_ref[:, sl].reshape(nc, CHUNK, 128)
            k = k_ref[:, sl].reshape(nc, CHUNK, 128)
            v = v_ref[:, sl].reshape(nc, CHUNK, 128)
            gc = gb_ref[:, h:h + 1].reshape(nc, CHUNK, 1)
            beta = gb_ref[:, 4 + h:5 + h].reshape(nc, CHUNK, 1)
            gr = gr_ref[h][:, None, :]
            decay, kb, l_mat, a_mat = _chunk_local(q, k, gc, gr, beta, incl, strict)
            eg = jnp.exp(gc)
            kbg = kb * eg
            vb = v * beta
            tb = t_ref[h].astype(BF16)
            du = du_ref[:, sl].reshape(nc, CHUNK, 128).astype(BF16)
            dw = dw_ref[:, sl].reshape(nc, CHUNK, 128).astype(BF16)
            dvb = _bdot("gcj,gcv->gjv", tb, du)
            dkbg = _bdot("gcj,gck->gjk", tb, dw)
            dt = _bdot("gcv,gjv->gcj", du, vb.astype(BF16)) + _bdot("gck,gjk->gcj", dw, kbg.astype(BF16))
            tmp = _bdot("gac,gab->gcb", tb, dt.astype(BF16))
            dl = jnp.where(strict, -_bdot("gcb,gdb->gcd", tmp.astype(BF16), tb), 0.0)
            da = da_ref[h]
            dm = (dl * decay).astype(BF16)
            dqk = (da * decay).astype(BF16)
            kbf = k.astype(BF16)
            dkb = _bdot("gcj,gjd->gcd", dm, kbf) + dkbg * eg
            dk = (_bdot("gcj,gcd->gjd", dm, kb.astype(BF16)) + _bdot("gcj,gcd->gjd", dqk, q.astype(BF16))
                  + dk1_ref[:, sl].reshape(nc, CHUNK, 128) + dkb * beta)
            dq = _bdot("gcj,gjd->gcd", dqk, kbf) + dq1_ref[:, sl].reshape(nc, CHUNK, 128)
            e = dl * l_mat + da * a_mat
            dgc = (jnp.sum(e, axis=2, keepdims=True) - jnp.sum(jnp.swapaxes(e, 1, 2), axis=2, keepdims=True)
                   + jnp.sum(dkbg * kbg, axis=2, keepdims=True))
            dbeta = jnp.sum(dkb * k, axis=2, keepdims=True) + jnp.sum(dvb * v, axis=2, keepdims=True)
            o_ref[:, sl] = dq.reshape(R, 128)
            o_ref[:, 512 + h * 128:512 + (h + 1) * 128] = dk.reshape(R, 128)
            o_ref[:, 1024 + h * 128:1024 + (h + 1) * 128] = (dvb * beta).reshape(R, 128)
            dg_all = dg_all + jnp.where(lane == h, dgc.reshape(R, 1), 0.0)
            db_all = jnp.where(lane == 4 + h, dbeta.reshape(R, 1), db_all)
        t = _chunk_row(R)
        for s in (1, 2, 4, 8, 16, 32):
            dg_all = dg_all + jnp.where(t + s < CHUNK, pltpu.roll(dg_all, R - s, 0), 0.0)
        dgb_ref[...] = jnp.where(lane < 4, dg_all, db_all)

    rowb = lambda c, j: BS((R, c), lambda b, n: (b * nb + n, j))
    mat = BS((None, N_HEADS, nc, CHUNK, CHUNK), lambda b, n: (b, 0, n, 0, 0))
    return pl.pallas_call(
        body, name="gdn_chunk_bwd", grid=(B, nb),
        in_specs=[rowb(512, 0), rowb(512, 1), rowb(512, 2), rowb(128, 0),
                  BS((None, N_HEADS, nc, CHUNK), lambda b, n: (b, 0, n, 0)), mat, mat,
                  rowb(512, 0), rowb(512, 0), rowb(512, 0), rowb(512, 0), rowb(128, 0)],
        out_specs=[rowb(GDN_QKV, 0), rowb(128, 0)],
        out_shape=[jax.ShapeDtypeStruct((T, GDN_QKV), F32), jax.ShapeDtypeStruct((T, 128), F32)],
        compiler_params=_arb(2))(qkv, qkv, qkv, GB, Grow, Tinv, dA, dU, dW, dQ1, dK1, dG1)


def _gdn_out_norm(og, gg):
    outs, xhs, rs = [], [], []
    for h in range(N_HEADS):
        seg = og[:, h * 128:(h + 1) * 128]
        r = lax.rsqrt(jnp.mean(seg * seg, axis=-1, keepdims=True) + EPS)
        xh = seg * r
        outs.append(xh * gg)
        xhs.append(xh)
        rs.append(r)
    return outs, xhs, rs


def merge_fwd(o_mla, o_gdn, o_mem, P, x, tgt, w_out, g_gdn, g_fin, tm=256):
    T = x.shape[0]
    tm = min(tm, T)

    def body(om_ref, og_ref, oc_ref, gate_ref, x_ref, t_ref, w_ref, gg_ref, gf_ref, mix_ref, dx_ref, dxb_ref, sq_ref, gnf_ref):
        @pl.when(pl.program_id(0) == 0)
        def _():
            sq_ref[...] = jnp.zeros_like(sq_ref)
            gnf_ref[...] = jnp.zeros_like(gnf_ref)

        ogn, _, _ = _gdn_out_norm(og_ref[...], gg_ref[...])
        cat = jnp.concatenate([om_ref[...]] + ogn + [oc_ref[...]], axis=1)
        gt = gate_ref[...]
        mixed = (cat * (gt * _sigmoid(gt))).astype(BF16)
        mix_ref[...] = mixed
        x2 = x_ref[...] + _dot(mixed, w_ref[...], NN)
        r2 = lax.rsqrt(jnp.mean(x2 * x2, axis=-1, keepdims=True) + EPS)
        xh = x2 * r2
        gf = gf_ref[...]
        diff = xh * gf - t_ref[...]
        sq_ref[...] += jnp.sum(diff * diff, axis=0, keepdims=True)
        dy = diff * (1.0 / D_MODEL)
        gnf_ref[...] += jnp.sum(dy * xh, axis=0, keepdims=True)
        dxh = dy * gf
        dx = r2 * (dxh - xh * jnp.mean(dxh * xh, axis=-1, keepdims=True))
        dx_ref[...] = dx
        dxb_ref[...] = dx.astype(BF16)

    rowb = lambda c, j=0: BS((tm, c), lambda i: (i, j))
    full = lambda r, c: BS((r, c), lambda i: (0, 0))
    return pl.pallas_call(
        body, name="merge_fwd", grid=(T // tm,),
        in_specs=[rowb(512), rowb(512), rowb(512), rowb(D_MIX, OFF_GATE // D_MIX), rowb(D_MODEL), rowb(D_MODEL),
                  full(D_MIX, D_MODEL), full(1, 128), full(1, D_MODEL)],
        out_specs=[rowb(D_MIX), rowb(D_MODEL), rowb(D_MODEL), full(1, D_MODEL), full(1, D_MODEL)],
        out_shape=[jax.ShapeDtypeStruct((T, D_MIX), BF16), jax.ShapeDtypeStruct((T, D_MODEL), F32),
                   jax.ShapeDtypeStruct((T, D_MODEL), BF16),
                   jax.ShapeDtypeStruct((1, D_MODEL), F32), jax.ShapeDtypeStruct((1, D_MODEL), F32)],
        compiler_params=_arb(1))(o_mla, o_gdn, o_mem, P, x, tgt, w_out, g_gdn, g_fin)


def merge_bwd(dx2, o_mla, o_gdn, o_mem, P, w_out, g_gdn, tm=256):
    T = dx2.shape[0]
    tm = min(tm, T)

    def body(dx_ref, om_ref, og_ref, oc_ref, gate_ref, w_ref, gg_ref, dgate_ref, dom_ref, dog_ref, doc_ref, ggn_ref):
        @pl.when(pl.program_id(0) == 0)
        def _():
            ggn_ref[...] = jnp.zeros_like(ggn_ref)

        gg = gg_ref[...]
        dmix = _dot(dx_ref[...].astype(BF16), w_ref[...], NT)
        ogn, xhs, rs = _gdn_out_norm(og_ref[...], gg)
        cat = jnp.concatenate([om_ref[...]] + ogn + [oc_ref[...]], axis=1)
        gt = gate_ref[...]
        sg = _sigmoid(gt)
        dgate_ref[...] = (dmix * cat * (sg * (1.0 + gt * (1.0 - sg)))).astype(BF16)
        dcat = dmix * (gt * sg)
        dom_ref[...] = dcat[:, :512]
        doc_ref[...] = dcat[:, 1024:]
        acc = jnp.zeros((1, 128), F32)
        for h in range(N_HEADS):
            dseg = dcat[:, 512 + h * 128:512 + (h + 1) * 128]
            acc = acc + jnp.sum(dseg * xhs[h], axis=0, keepdims=True)
            dxh = dseg * gg
            dog_ref[:, h * 128:(h + 1) * 128] = rs[h] * (dxh - xhs[h] * jnp.mean(dxh * xhs[h], axis=-1, keepdims=True))
        ggn_ref[...] += acc

    rowb = lambda c, j=0: BS((tm, c), lambda i: (i, j))
    full = lambda r, c: BS((r, c), lambda i: (0, 0))
    return pl.pallas_call(
        body, name="merge_bwd", grid=(T // tm,),
        in_specs=[rowb(D_MODEL), rowb(512), rowb(512), rowb(512), rowb(D_MIX, OFF_GATE // D_MIX),
                  full(D_MIX, D_MODEL), full(1, 128)],
        out_specs=[rowb(D_MIX), rowb(512), rowb(512), rowb(512), full(1, 128)],
        out_shape=[jax.ShapeDtypeStruct((T, D_MIX), BF16)] + [jax.ShapeDtypeStruct((T, 512), F32)] * 3
        + [jax.ShapeDtypeStruct((1, 128), F32)],
        compiler_params=_arb(1))(dx2, o_mla, o_gdn, o_mem, P, w_out, g_gdn)


def in_norm_bwd(x, dh, dx2, gain, tm=256):
    T, n = x.shape
    tm = min(tm, T)

    def body(x_ref, dh_ref, dx2_ref, g_ref, o_ref, acc_ref):
        @pl.when(pl.program_id(0) == 0)
        def _():
            acc_ref[...] = jnp.zeros_like(acc_ref)

        xv = x_ref[...]
        r = lax.rsqrt(jnp.mean(xv * xv, axis=-1, keepdims=True) + EPS)
        xh = xv * r
        dy = dh_ref[...]
        acc_ref[...] += jnp.sum(dy * xh, axis=0, keepdims=True)
        dxh = dy * g_ref[...]
        o_ref[...] = dx2_ref[...] + r * (dxh - xh * jnp.mean(dxh * xh, axis=-1, keepdims=True))

    rowb = BS((tm, n), lambda i: (i, 0))
    full = BS((1, n), lambda i: (0, 0))
    return pl.pallas_call(
        body, name="in_norm_bwd", grid=(T // tm,),
        in_specs=[rowb, rowb, rowb, full], out_specs=[rowb, full],
        out_shape=[jax.ShapeDtypeStruct((T, n), F32), jax.ShapeDtypeStruct((1, n), F32)],
        compiler_params=_arb(1))(x, dh, dx2, gain)


W_IN_SHARD = D_IN // 4
_GDN0 = Q_LORA + KV_LORA + MLA_ROPE
_AB0 = _GDN0 + GDN_QKV
_MEMQ0 = _AB0 + 2 * N_HEADS
_GATE0 = _MEMQ0 + N_HEADS * MEM_DH


def _w_in_row_map():
    a, m, gt = _AB0 - 2 * W_IN_SHARD, _MEMQ0 - 2 * W_IN_SHARD, _GATE0 - 2 * W_IN_SHARD
    e0 = OFF_GDN + W_IN_SHARD - _GDN0
    e1 = e0 + W_IN_SHARD
    e2 = OFF_GATE + W_IN_SHARD - gt
    return [(0, 0, 0, 672), (0, 672, 704, 32), (2, a, 768, m - a), (2, m, OFF_MEMQ, gt - m), (0, _GDN0, OFF_GDN, W_IN_SHARD - _GDN0),
            (1, 0, e0, W_IN_SHARD), (2, 0, e1, a), (2, gt, OFF_GATE, W_IN_SHARD - gt), (3, 0, e2, W_IN_SHARD)]


_W_IN_ZERO_ROWS = [(672, 32), (736, 32), (776, 248)]
W_IN_LANES = 256


def pad_w_in_t(shards):
    per_half = shards.shape[3] // W_IN_LANES

    def body(s_ref, o_ref):
        for r0, n in _W_IN_ZERO_ROWS:
            o_ref[r0:r0 + n, :] = jnp.zeros((n, W_IN_LANES), o_ref.dtype)
        for q, src, dst, n in _w_in_row_map():
            o_ref[dst:dst + n, :] = s_ref[q, src:src + n, :]

    return pl.pallas_call(
        body, name="pad_w_in_t", grid=(D_MODEL // W_IN_LANES,),
        in_specs=[BS((N_CHIPS, None, W_IN_SHARD, W_IN_LANES), lambda j: (0, j // per_half, 0, j % per_half))],
        out_specs=BS((N_PAD, W_IN_LANES), lambda j: (0, j)),
        out_shape=jax.ShapeDtypeStruct((N_PAD, D_MODEL), shards.dtype), compiler_params=_arb(1))(shards)


def unpad_w_in_t(g):
    def body(g_ref, o_ref):
        for q, src, dst, n in _w_in_row_map():
            o_ref[q, src:src + n, :] = g_ref[dst:dst + n, :]

    return pl.pallas_call(
        body, name="unpad_w_in_t", grid=(D_MODEL // W_IN_LANES,),
        in_specs=[BS((N_PAD, W_IN_LANES), lambda j: (0, j))], out_specs=BS((N_CHIPS, W_IN_SHARD, W_IN_LANES), lambda j: (0, 0, j)),
        out_shape=jax.ShapeDtypeStruct((N_CHIPS, W_IN_SHARD, D_MODEL), g.dtype), compiler_params=_arb(1))(g)


def _pad_w_q_b_t(s):
    z = jnp.zeros((32, s.shape[2]), s.dtype)
    parts = []
    for h in range(N_HEADS):
        parts += [s[h, :128], s[h, 128:160], z, s[h, 160:192], z]
    return jnp.concatenate(parts, axis=0)


def _unpad_w_q_b_t(g):
    return jnp.stack([jnp.concatenate([g[h * HEAD_PAD:h * HEAD_PAD + 128], g[h * HEAD_PAD + 128:h * HEAD_PAD + 160],
                                       g[h * HEAD_PAD + 192:h * HEAD_PAD + 224]]) for h in range(N_HEADS)])


def _perm_w_kv_b(s):
    return jnp.concatenate([s[h, :, :128] for h in range(N_HEADS)] + [s[h, :, 128:] for h in range(N_HEADS)], axis=1)


def _unperm_w_kv_b(g):
    return jnp.stack([jnp.concatenate([g[:, h * 128:(h + 1) * 128], g[:, 512 + h * 128:512 + (h + 1) * 128]], axis=1)
                      for h in range(N_HEADS)])


def _lane_row(v4):
    return jnp.pad(v4.reshape(1, -1).astype(F32), ((0, 0), (0, 128 - v4.size)))


def _pack(pieces, n_rows):
    flat = jnp.concatenate([p.reshape(-1) for p in pieces])
    return jnp.pad(flat, (0, n_rows * 1024 - flat.size)).reshape(n_rows, 1024)


def _unpack(block, shapes):
    flat = block.reshape(-1)
    out, off = [], 0
    for shp in shapes:
        n = int(np.prod(shp))
        out.append(flat[off:off + n].reshape(shp))
        off += n
    return out


N_CHIPS = 4
MESH = pl.DeviceIdType.MESH
ANY = BS(memory_space=pl.ANY)


def _place():
    return lax.axis_index("x"), lax.axis_index("y"), lax.axis_index("c")


def _other_chips(x, y):
    return [(1 - x, y), (x, 1 - y), (1 - x, 1 - y)]


def _half(split, which):
    if split == "lead":
        return (which,)
    axis, size = split
    ds = pl.ds(pl.multiple_of(which * size, 16 if axis == 0 else 128), size)
    return (ds, slice(None)) if axis == 0 else (slice(None), ds)


def allgather_chips(shards, splits, name):
    n = len(shards)

    def body(*refs):
        s_refs, o_refs = refs[:n], refs[n:2 * n]
        send_sems, recv_sems, local_sems = refs[2 * n:]
        x, y, c = _place()
        chips = _other_chips(x, y)

        def copy(k, src, dst, to):
            return pltpu.make_async_remote_copy(src_ref=src, dst_ref=dst, send_sem=send_sems.at[k], recv_sem=recv_sems.at[k],
                                                device_id=to, device_id_type=MESH)

        owns, first, passed = [], [], []
        for i, (s_ref, o_ref) in enumerate(zip(s_refs, o_refs)):
            mine = _half(splits[i], c)
            owns.append(pltpu.make_async_copy(s_ref, o_ref.at[2 * x + y], local_sems.at[i]))
            owns[-1].start()
            for j, (px, py) in enumerate(chips):
                first.append(copy(6 * i + j, s_ref.at[mine], o_ref.at[(2 * x + y,) + mine], (px, py, c)))
                first[-1].start()
        for i, (s_ref, o_ref) in enumerate(zip(s_refs, o_refs)):
            mine = _half(splits[i], c)
            for j, (px, py) in enumerate(chips):
                landed = o_ref.at[(2 * px + py,) + mine]
                copy(6 * i + j, s_ref.at[mine], landed, (px, py, c)).wait_recv()
                passed.append(copy(6 * i + 3 + j, landed, landed, (x, y, 1 - c)))
                passed[-1].start()
        for i, (s_ref, o_ref) in enumerate(zip(s_refs, o_refs)):
            theirs = _half(splits[i], 1 - c)
            for j, (px, py) in enumerate(chips):
                copy(6 * i + 3 + j, s_ref.at[theirs], o_ref.at[(2 * px + py,) + theirs], (x, y, 1 - c)).wait_recv()
        for cp in first + passed:
            cp.wait_send()
        for cp in owns:
            cp.wait()

    return pl.pallas_call(
        body, name=name, in_specs=[ANY] * n, out_specs=[ANY] * n,
        out_shape=[jax.ShapeDtypeStruct((N_CHIPS,) + s.shape, s.dtype) for s in shards],
        scratch_shapes=[pltpu.SemaphoreType.DMA((6 * n,)), pltpu.SemaphoreType.DMA((6 * n,)), pltpu.SemaphoreType.DMA((n,))])(*shards)


SEM = BS(memory_space=pltpu.SEMAPHORE)
HBM = BS(memory_space=pltpu.HBM)
_IN_HBM = lambda a: pltpu.with_memory_space_constraint(a, pltpu.HBM)
_SIDE_EFFECT = pltpu.SideEffectType.DATAFLOW_SIDE_EFFECTING


def _late_gather_copies(s_refs, l_refs, send_sems, recv_sems, local_sems, with_arrivals):
    x, y, c = _place()
    sends, recvs, locals_ = [], [], []
    for i, (s_ref, l_ref) in enumerate(zip(s_refs, l_refs)):
        locals_.append(pltpu.make_async_copy(s_ref, l_ref.at[2 * x + y], local_sems.at[i]))
        for j, (px, py) in enumerate(_other_chips(x, y)):
            k = 3 * i + j
            sends.append(pltpu.make_async_remote_copy(src_ref=s_ref, dst_ref=l_ref.at[2 * x + y], send_sem=send_sems.at[k],
                                                      recv_sem=recv_sems.at[k], device_id=(px, py, c), device_id_type=MESH))
            if with_arrivals:
                recvs.append(pltpu.make_async_remote_copy(src_ref=s_ref, dst_ref=l_ref.at[2 * px + py], send_sem=send_sems.at[k],
                                                          recv_sem=recv_sems.at[k], device_id=(px, py, c), device_id_type=MESH))
    return sends, recvs, locals_


def late_gather_start(shards, after, name):
    n = len(shards)

    def body(*refs):
        s_refs, l_refs = refs[:n], refs[n:2 * n]
        send_sems, recv_sems, local_sems = refs[2 * n + 1:2 * n + 4]
        token = refs[-1]
        sends, _, locals_ = _late_gather_copies(s_refs, l_refs, send_sems, recv_sems, local_sems, False)
        for cp in locals_ + sends:
            cp.start()
        token[...] = jnp.zeros_like(token)

    lands = [lax.empty((N_CHIPS,) + s.shape, s.dtype) for s in shards]
    hbm_like = lambda a: pltpu.HBM(a.shape, a.dtype)
    out = pl.pallas_call(
        body, name=name,
        out_shape=[pltpu.SemaphoreType.DMA((3 * n,)), pltpu.SemaphoreType.DMA((3 * n,)), pltpu.SemaphoreType.DMA((n,))]
        + [hbm_like(s) for s in shards] + [hbm_like(l) for l in lands] + [jax.ShapeDtypeStruct((8, 128), F32)],
        in_specs=[HBM] * (2 * n) + [BS(memory_space=pl.ANY)], out_specs=[SEM] * 3 + [HBM] * (2 * n) + [BS(memory_space=pltpu.VMEM)],
        input_output_aliases={i: 3 + i for i in range(2 * n)},
        compiler_params=pltpu.CompilerParams(has_side_effects=_SIDE_EFFECT))(
            *[_IN_HBM(s) for s in shards], *[_IN_HBM(l) for l in lands], after)
    return out[:3], out[3:3 + n], out[3 + n:3 + 2 * n], out[-1]


def late_gather_wait(sems, shards, lands, after, name):
    n = len(shards)

    def body(*refs):
        s_refs, l_refs = refs[:n], refs[n:2 * n]
        send_sems, recv_sems, local_sems = refs[2 * n:2 * n + 3]
        sends, recvs, locals_ = _late_gather_copies(s_refs, l_refs, send_sems, recv_sems, local_sems, True)
        for cp in locals_:
            cp.wait()
        for cp in sends:
            cp.wait_send()
        for cp in recvs:
            cp.wait_recv()

    hbm_like = lambda a: pltpu.HBM(a.shape, a.dtype)
    out = pl.pallas_call(
        body, name=name, out_shape=[hbm_like(s) for s in shards] + [hbm_like(l) for l in lands],
        in_specs=[HBM] * (2 * n) + [SEM] * 3 + [BS(memory_space=pl.ANY)], out_specs=[HBM] * (2 * n),
        input_output_aliases={i: i for i in range(2 * n)},
        compiler_params=pltpu.CompilerParams(has_side_effects=_SIDE_EFFECT))(*shards, *lands, *sems, after)
    return out[n:]


def allgather_devices(block, name):
    R, C = block.shape

    def body(b_ref, o_ref, send_sems, recv_sems, local_sem):
        x, y, c = _place()
        me = 4 * x + 2 * y + c
        own = pltpu.make_async_copy(b_ref, o_ref.at[me], local_sem)
        own.start()
        copies = []
        for r in range(1, 8):
            px = 1 - x if r & 4 else x
            py = 1 - y if r & 2 else y
            pc = 1 - c if r & 1 else c
            send = pltpu.make_async_remote_copy(src_ref=b_ref, dst_ref=o_ref.at[me], send_sem=send_sems.at[r - 1],
                                                recv_sem=recv_sems.at[r - 1], device_id=(px, py, pc), device_id_type=MESH)
            recv = pltpu.make_async_remote_copy(src_ref=b_ref, dst_ref=o_ref.at[4 * px + 2 * py + pc], send_sem=send_sems.at[r - 1],
                                                recv_sem=recv_sems.at[r - 1], device_id=(px, py, pc), device_id_type=MESH)
            send.start()
            copies.append((send, recv))
        for send, recv in copies:
            recv.wait_recv()
            send.wait_send()
        own.wait()

    return pl.pallas_call(
        body, name=name, in_specs=[ANY], out_specs=ANY, out_shape=jax.ShapeDtypeStruct((8, R, C), block.dtype),
        scratch_shapes=[pltpu.SemaphoreType.DMA((7,)), pltpu.SemaphoreType.DMA((7,)), pltpu.SemaphoreType.DMA(())])(block)


def swap_sibling(arrs, name, splits=None):
    n = len(arrs)

    def sent(a_ref, i, c):
        return a_ref if splits is None else a_ref.at[(slice(None),) + _half(splits[i], 1 - c)]

    def out_shape(a, i):
        if splits is None:
            return a.shape
        axis, size = splits[i]
        return (a.shape[0], size, a.shape[2]) if axis == 0 else (a.shape[0], a.shape[1], size)

    def body(*refs):
        a_refs, o_refs = refs[:n], refs[n:2 * n]
        send_sems, recv_sems = refs[2 * n:]
        x, y, c = _place()
        copies = [pltpu.make_async_remote_copy(src_ref=sent(a_ref, i, c), dst_ref=o_ref, send_sem=send_sems.at[i],
                                               recv_sem=recv_sems.at[i], device_id=(x, y, 1 - c), device_id_type=MESH)
                  for i, (a_ref, o_ref) in enumerate(zip(a_refs, o_refs))]
        for cp in copies:
            cp.start()
        for cp in copies:
            cp.wait()

    return pl.pallas_call(
        body, name=name, in_specs=[ANY] * n, out_specs=[ANY] * n,
        out_shape=[jax.ShapeDtypeStruct(out_shape(a, i), a.dtype) for i, a in enumerate(arrs)],
        scratch_shapes=[pltpu.SemaphoreType.DMA((n,)), pltpu.SemaphoreType.DMA((n,))])(*arrs)


def _exchange_copies(p_refs, l_refs, send_sems, recv_sems):
    x, y, c = _place()
    return [pltpu.make_async_remote_copy(src_ref=p_ref.at[2 * px + py], dst_ref=l_ref.at[j], send_sem=send_sems.at[3 * i + j],
                                         recv_sem=recv_sems.at[3 * i + j], device_id=(px, py, c), device_id_type=MESH)
            for i, (p_ref, l_ref) in enumerate(zip(p_refs, l_refs)) for j, (px, py) in enumerate(_other_chips(x, y))]


def exchange_chips_start(parts, name):
    n = len(parts)

    def body(*refs):
        send_sems, recv_sems = refs[2 * n:2 * n + 2]
        for cp in _exchange_copies(refs[:n], refs[n:2 * n], send_sems, recv_sems):
            cp.start()
        refs[-1][...] = jnp.zeros_like(refs[-1])

    lands = [lax.empty((3,) + p.shape[1:], p.dtype) for p in parts]
    hbm_like = lambda a: pltpu.HBM(a.shape, a.dtype)
    out = pl.pallas_call(
        body, name=name,
        out_shape=[pltpu.SemaphoreType.DMA((3 * n,)), pltpu.SemaphoreType.DMA((3 * n,))]
        + [hbm_like(p) for p in parts] + [hbm_like(l) for l in lands] + [jax.ShapeDtypeStruct((8, 128), F32)],
        in_specs=[HBM] * (2 * n), out_specs=[SEM] * 2 + [HBM] * (2 * n) + [BS(memory_space=pltpu.VMEM)],
        input_output_aliases={i: 2 + i for i in range(2 * n)},
        compiler_params=pltpu.CompilerParams(has_side_effects=_SIDE_EFFECT))(*[_IN_HBM(p) for p in parts], *[_IN_HBM(l) for l in lands])
    return out[:2], out[2:2 + n], out[2 + n:2 + 2 * n], out[-1]


def exchange_chips_wait(sems, parts, lands, after, name):
    n = len(parts)

    def body(*refs):
        send_sems, recv_sems = refs[2 * n:2 * n + 2]
        for cp in _exchange_copies(refs[:n], refs[n:2 * n], send_sems, recv_sems):
            cp.wait_send()
            cp.wait_recv()

    hbm_like = lambda a: pltpu.HBM(a.shape, a.dtype)
    out = pl.pallas_call(
        body, name=name, out_shape=[hbm_like(p) for p in parts] + [hbm_like(l) for l in lands],
        in_specs=[HBM] * (2 * n) + [SEM] * 2 + [BS(memory_space=pl.ANY)], out_specs=[HBM] * (2 * n),
        input_output_aliases={i: i for i in range(2 * n)},
        compiler_params=pltpu.CompilerParams(has_side_effects=_SIDE_EFFECT))(*parts, *lands, *sems, after)
    return out[n:]


def _half_block(shape2, split):
    axis, size = split
    return (size, shape2[1]) if axis == 0 else (shape2[0], size)


def add_pairs(parts, halves, splits, core, name):
    n = len(parts)

    def body(s_ref, *refs):
        for a_ref, b_ref, o_ref in zip(refs[:n], refs[n:2 * n], refs[2 * n:]):
            o_ref[...] = (a_ref[...].astype(F32) + b_ref[...].astype(F32)).astype(BF16)

    def mine(i):
        blk = (None,) + _half_block(parts[i].shape[1:], splits[i])
        if splits[i][0] == 0:
            return BS(blk, lambda q, s: (q, s[0], 0))
        return BS(blk, lambda q, s: (q, 0, s[0]))

    half_specs = [BS((None,) + h.shape[1:], lambda q, s: (q, 0, 0)) for h in halves]
    return pl.pallas_call(
        body, name=name,
        grid_spec=pltpu.PrefetchScalarGridSpec(num_scalar_prefetch=1, grid=(N_CHIPS,),
                                               in_specs=[mine(i) for i in range(n)] + half_specs, out_specs=half_specs),
        out_shape=[jax.ShapeDtypeStruct(h.shape, BF16) for h in halves], compiler_params=_arb(1))(core, *parts, *halves)


def add_fives(parts, halves, from_chips, splits, chip_core, name):
    n = len(parts)

    def body(s_ref, *refs):
        for a_ref, b_ref, p_ref, o_ref in zip(refs[:n], refs[n:2 * n], refs[2 * n:3 * n], refs[3 * n:]):
            s = a_ref[...].astype(F32) + b_ref[...].astype(F32)
            for j in range(3):
                s = s + p_ref[j].astype(F32)
            o_ref[...] = s

    def mine(i):
        blk = (None,) + _half_block(parts[i].shape[1:], splits[i])
        if splits[i][0] == 0:
            return BS(blk, lambda g, s: (s[0], s[1], 0))
        return BS(blk, lambda g, s: (s[0], 0, s[1]))

    half_specs = [BS((None,) + h.shape[1:], lambda g, s: (s[0], 0, 0)) for h in halves]
    chip_specs = [BS(p.shape, lambda g, s: (0, 0, 0)) for p in from_chips]
    out_specs = [BS(h.shape[1:], lambda g, s: (0, 0)) for h in halves]
    return pl.pallas_call(
        body, name=name,
        grid_spec=pltpu.PrefetchScalarGridSpec(num_scalar_prefetch=1, grid=(1,),
                                               in_specs=[mine(i) for i in range(n)] + half_specs + chip_specs, out_specs=out_specs),
        out_shape=[jax.ShapeDtypeStruct(h.shape[1:], F32) for h in halves], compiler_params=_arb(1))(chip_core, *parts, *halves, *from_chips)


def sum_leading(a, name):
    def body(a_ref, o_ref):
        s = a_ref[0]
        for j in range(1, a.shape[0]):
            s = s + a_ref[j]
        o_ref[...] = s

    return pl.pallas_call(body, name=name, out_shape=jax.ShapeDtypeStruct(a.shape[1:], a.dtype))(a)


def _adamw_math(w, g, m, v):
    mn = ADAM_B1 * m + (1.0 - ADAM_B1) * g
    vn = ADAM_B2 * v + (1.0 - ADAM_B2) * (g * g)
    m_hat = mn / (1.0 - ADAM_B1 ** ADAM_STEP)
    v_hat = vn / (1.0 - ADAM_B2 ** ADAM_STEP)
    return -ADAM_LR * (m_hat / (jnp.sqrt(v_hat) + ADAM_EPS) + ADAM_WD * w), mn, vn


def adamw(w, g, m, v, name):
    R, C = g.shape
    lead = (None,) * (w.ndim - 2)

    def body(w_ref, g_ref, m_ref, v_ref, d_ref, mo_ref, vo_ref):
        d_ref[...], mo_ref[...], vo_ref[...] = _adamw_math(w_ref[...], g_ref[...], m_ref[...], v_ref[...])

    wblk = BS(lead + (R, C), lambda i: (0,) * w.ndim)
    gblk = BS((R, C), lambda i: (0, 0))
    return pl.pallas_call(
        body, name=name, grid=(1,), in_specs=[wblk, gblk, wblk, wblk], out_specs=[wblk] * 3,
        out_shape=[jax.ShapeDtypeStruct(w.shape, F32)] * 3, compiler_params=_arb(1))(w, g, m, v)


def adamw_halves(w, mine, other, m, v, split, core, name):
    R, C = w.shape[-2:]
    axis, size = split
    lead = (None,) * (w.ndim - 2)
    zeros = (0,) * (w.ndim - 2)
    if axis == 0:
        tr = size if size <= 256 else next(t for t in range(256, 7, -1) if size % t == 0 and t % 8 == 0)
        nb = size // tr
        whole = BS(lead + (tr, C), lambda hi, j, s: zeros + (hi * nb + j, 0))
        part = BS((tr, C), lambda hi, j, s: (j, 0))
    else:
        nb = size // 128
        whole = BS(lead + (R, 128), lambda hi, j, s: zeros + (0, hi * nb + j))
        part = BS((R, 128), lambda hi, j, s: (0, j))

    def body(s_ref, w_ref, a_ref, b_ref, m_ref, v_ref, g_ref, d_ref, mo_ref, vo_ref):
        g = jnp.where(pl.program_id(0) == s_ref[0], a_ref[...], b_ref[...])
        g_ref[...] = g
        d_ref[...], mo_ref[...], vo_ref[...] = _adamw_math(w_ref[...], g, m_ref[...], v_ref[...])

    return pl.pallas_call(
        body, name=name,
        grid_spec=pltpu.PrefetchScalarGridSpec(num_scalar_prefetch=1, grid=(2, nb),
                                               in_specs=[whole, part, part, whole, whole], out_specs=[whole] * 4),
        out_shape=[jax.ShapeDtypeStruct(w.shape, F32)] * 4, compiler_params=_arb(2))(core, w, mine, other, m, v)


def dense_bf16(w3, name):
    R, _, K = w3.shape
    kh = K // 2

    def body(w_hbm, o_ref, buf, sem):
        cp = pltpu.make_async_copy(w_hbm.at[:, 0], buf, sem)
        cp.start()
        cp.wait()
        o_ref[0] = buf[:, :kh].astype(BF16)
        o_ref[1] = buf[:, kh:].astype(BF16)

    return pl.pallas_call(
        body, name=name, in_specs=[ANY], out_specs=BS(memory_space=pltpu.VMEM), out_shape=jax.ShapeDtypeStruct((2, R, kh), BF16),
        scratch_shapes=[pltpu.VMEM((R, K), F32), pltpu.SemaphoreType.DMA(())])(w3)


ROW_BLOCK = 184


def adamw_untiled_rows(w3, mine, other, m3, v3, name):
    R, _, K = w3.shape
    kh = K // 2
    starts = list(range(0, R, ROW_BLOCK))
    sizes = [min(ROW_BLOCK, R - s) for s in starts]
    nblk = len(starts)

    def body(w_hbm, a_ref, b_ref, m_hbm, v_hbm, g_hbm, d_hbm, mo_hbm, vo_hbm,
             wbuf, mbuf, vbuf, gbuf, dbuf, mobuf, vobuf, in_sems, out_sems):
        first = lax.axis_index("c") == 0
        ins = []
        for k, (r0, n) in enumerate(zip(starts, sizes)):
            rows = pl.ds(r0, n)
            cps = [pltpu.make_async_copy(src.at[rows, 0], dst.at[rows], in_sems.at[3 * k + i])
                   for i, (src, dst) in enumerate(((w_hbm, wbuf), (m_hbm, mbuf), (v_hbm, vbuf)))]
            for cp in cps:
                cp.start()
            ins.append(cps)

        def update(rows):
            a, b = a_ref[rows, :], b_ref[rows, :]
            g = jnp.concatenate([jnp.where(first, a, b), jnp.where(first, b, a)], axis=1)
            gbuf[rows, :] = g
            dbuf[rows, :], mobuf[rows, :], vobuf[rows, :] = _adamw_math(wbuf[rows, :], g, mbuf[rows, :], vbuf[rows, :])

        outs = []
        for k, (r0, n) in enumerate(zip(starts, sizes)):
            for cp in ins[k]:
                cp.wait()
            groups, tail = n // 8, n % 8

            def group(i, carry, r0=r0):
                update(pl.ds(pl.multiple_of(r0 + i * 8, 8), 8))
                return carry

            lax.fori_loop(0, groups, group, 0)
            if tail:
                update(pl.ds(r0 + groups * 8, tail))
            rows = pl.ds(r0, n)
            cps = [pltpu.make_async_copy(src.at[rows], dst.at[rows, 0], out_sems.at[4 * k + i])
                   for i, (src, dst) in enumerate(((gbuf, g_hbm), (dbuf, d_hbm), (mobuf, mo_hbm), (vobuf, vo_hbm)))]
            for cp in cps:
                cp.start()
            outs += cps
        for cp in outs:
            cp.wait()

    vmem = BS(memory_space=pltpu.VMEM)
    return pl.pallas_call(
        body, name=name, in_specs=[ANY, vmem, vmem, ANY, ANY], out_specs=[ANY] * 4,
        out_shape=[jax.ShapeDtypeStruct(w3.shape, F32)] * 4,
        scratch_shapes=[pltpu.VMEM((R, K), F32)] * 7 + [pltpu.SemaphoreType.DMA((3 * nblk,)), pltpu.SemaphoreType.DMA((4 * nblk,))])(
            w3, mine, other, m3, v3)


def adamw_w_q_b(w, mine, other, m, v, name):
    def body(w_ref, a_ref, b_ref, m_ref, v_ref, g_ref, d_ref, mo_ref, vo_ref):
        first = lax.axis_index("c") == 0
        lo = jnp.where(first, a_ref[...], b_ref[...])
        hi = jnp.where(first, b_ref[...], a_ref[...])
        g = jnp.concatenate([lo, hi[0:32], hi[64:96]], axis=0)
        g_ref[...] = g
        d_ref[...], mo_ref[...], vo_ref[...] = _adamw_math(w_ref[...], g, m_ref[...], v_ref[...])

    return pl.pallas_call(body, name=name, out_shape=[jax.ShapeDtypeStruct(w.shape, F32)] * 4)(w, mine, other, m, v)


def local_step(x, mem, positions, tgt, norm_in, w_in, late_weights, big_grads_ready, q_a_norm, kv_a_norm, gdn_conv, gdn_a_log,
               gdn_dt_bias, gdn_norm, mem_norm, norm_final):
    B, S, D = x.shape
    M = mem.shape[1]
    T = B * S
    N = S // CHUNK
    x2d = x.reshape(T, D)
    mem2d = mem.reshape(B * M, D)
    tgt2d = tgt.reshape(T, D)

    wp = w_in
    alog_row, dt_row = _lane_row(gdn_a_log), _lane_row(gdn_dt_bias)

    half = MLA_ROPE // 2
    inv_freq = 1.0 / (ROPE_THETA ** (jnp.arange(half, dtype=F32) / half))
    z32 = jnp.zeros((half,), F32)
    o32 = jnp.ones((half,), F32)
    inv_row = jnp.concatenate([inv_freq, z32, inv_freq, z32]).reshape(1, 128)
    sgn_row = jnp.concatenate([-o32, z32, o32, z32]).reshape(1, 128)
    msk_row = jnp.concatenate([o32, z32, o32, z32]).reshape(1, 128)
    cos_t, sin_t = rope_tables(positions.reshape(T, 1), inv_row, sgn_row, msk_row)

    h = rms_fwd(x2d, norm_in, "rms_in")
    P = mm(h, wp, "nt", F32, "in_proj", bm=512, bn=1536, n_outer=True, after=late_weights[2])
    wq, wkv = late_weights[0](P)
    Q, K, V, qn, kvn = mla_prep(P, q_a_norm, kv_a_norm, wq, wkv, cos_t, sin_t)
    o_mla, lse = mla_attn_fwd(Q, K, V, B, S)
    qkv = gdn_prep_fwd(P, gdn_conv, B, S)
    GB = gdn_gate_fwd(P, alog_row, dt_row, B, S)
    Grow = jnp.transpose(GB[:, :N_HEADS].reshape(B, N, CHUNK, N_HEADS), (0, 3, 1, 2))
    U, W, Tinv, A = gdn_chunk_fwd(qkv, GB, Grow, B, S)
    qkv3, GB3 = qkv.reshape(B, S, GDN_QKV), GB.reshape(B, S, 128)
    W3 = W.reshape(B, S, 512)
    o_gdn3, Vn3, St = gdn_scan_fwd(qkv3, U.reshape(B, S, 512), W3, GB3, A, B, S)
    o_gdn = o_gdn3.reshape(T, 512)
    w_mem_kv, w_out = late_weights[1](o_gdn)
    memn = rms_fwd(mem2d, mem_norm, "rms_mem")
    MKV = mm(memn, w_mem_kv, "nn", BF16, "mem_kv_proj")
    o_mem = mem_attn_fwd(P, MKV, B, S, M)
    mixed, dx2, dx2b, sq, g_norm_final = merge_fwd(o_mla, o_gdn, o_mem, P, x2d, tgt2d, w_out, gdn_norm, norm_final.reshape(1, D))

    g_w_out = mm(mixed, dx2b, "tn", BF16, "grad_w_out")
    dgate, do_mla, do_gdn, do_mem, g_gdn_norm = merge_bwd(dx2b, o_mla, o_gdn, o_mem, P, w_out, gdn_norm)

    dmemq, dMKV = mem_attn_bwd(P, MKV, do_mem, B, S, M)
    g_w_mem_kv = mm(memn, dMKV, "tn", BF16, "grad_w_mem_kv")
    dmemn = mm(dMKV, w_mem_kv, "nt", F32, "d_memn")
    g_mem_norm = gain_grad(mem2d, dmemn, "grad_mem_norm")

    dU3, dW3, dQ13, dK13, dA, dG13 = gdn_scan_bwd(do_gdn.reshape(B, S, 512), qkv3, W3, Vn3, GB3, A, St, B, S)
    r2 = lambda a: a.reshape(T, a.shape[-1])
    dqkv, dGB = gdn_chunk_bwd(qkv, GB, Grow, Tinv, dA, r2(dU3), r2(dW3), r2(dQ13), r2(dK13), r2(dG13), B, S)
    dPg, g_conv = gdn_prep_bwd(P, dqkv, gdn_conv, B, S)
    dab, g_ab = gdn_gate_bwd(P, dGB, alog_row, dt_row, B, S)

    dQ, dK, dV = mla_attn_bwd(Q, K, V, o_mla, do_mla, lse, B, S)
    dq_lin, dkv_lin, dkr = mla_post_bwd(dQ, dK, dV, cos_t, sin_t)
    dqn = mm(dq_lin, wq, "nn", F32, "d_qn")
    dkvn = mm(dkv_lin, wkv, "nt", F32, "d_kvn")
    g_wq = mm(dq_lin, qn, "tn", BF16, "grad_w_q_b")
    g_wkv = mm(kvn, dkv_lin, "tn", BF16, "grad_w_kv_b")
    dPm, g_q_a_norm, g_kv_a_norm = mla_norm_bwd(P, dqn, dkvn, dkr, dab, q_a_norm, kv_a_norm)

    dP = [dPm, dmemq, dPg, dgate]
    g_wp = mm_cols_tn(dP, h, BF16, "grad_w_in")
    started = big_grads_ready(dict(w_in=g_wp, w_q_b=g_wq, w_kv_b=g_wkv, w_mem_kv=g_w_mem_kv, w_out=g_w_out))
    dh = mm_cols_nn(dP, wp, F32, "d_h", started)
    grad_x, g_norm_in = in_norm_bwd(x2d, dh, dx2, norm_in)

    grads = dict(
        norm_in=g_norm_in, q_a_norm=g_q_a_norm, kv_a_norm=g_kv_a_norm, gdn_conv=g_conv,
        gdn_a_log=g_ab[0:1, :N_HEADS], gdn_dt_bias=g_ab[1:2, :N_HEADS], gdn_norm=g_gdn_norm,
        mem_norm=g_mem_norm, norm_final=g_norm_final)
    return sq, grad_x.reshape(B, S, D), grads


def kernel(x, mem, positions, norm_in, w_in, q_a_norm, w_q_b, kv_a_norm, w_kv_b, gdn_conv, gdn_a_log, gdn_dt_bias, gdn_norm, mem_norm, w_mem_kv, w_out, norm_final, loss_target, m_norm_in, m_w_in, m_q_a_norm, m_w_q_b, m_kv_a_norm, m_w_kv_b, m_gdn_conv, m_gdn_a_log, m_gdn_dt_bias, m_gdn_norm, m_mem_norm, m_w_mem_kv, m_w_out, m_norm_final, v_norm_in, v_w_in, v_q_a_norm, v_w_q_b, v_kv_a_norm, v_w_kv_b, v_gdn_conv, v_gdn_a_log, v_gdn_dt_bias, v_gdn_norm, v_mem_norm, v_w_mem_kv, v_w_out, v_norm_final):
    B = x.shape[0]
    cx, cy, cc = lax.axis_index("x"), lax.axis_index("y"), lax.axis_index("c")
    chip = 2 * cx + cy

    big_names = ("w_in", "w_q_b", "w_kv_b", "w_mem_kv", "w_out")
    rows_major = lambda a: jnp.transpose(a, (2, 0, 1))
    w_in3, m_in3, v_in3 = rows_major(w_in), rows_major(m_w_in), rows_major(v_w_in)
    w_qb_t, m_qb_t, v_qb_t = jnp.transpose(w_q_b[0]), jnp.transpose(m_w_q_b[0]), jnp.transpose(v_w_q_b[0])
    z32 = jnp.zeros((32, Q_LORA), BF16)
    qb_bf = w_qb_t.astype(BF16)
    qb_padded = jnp.concatenate([qb_bf[:160], z32, qb_bf[160:], z32])
    shards = [dense_bf16(w_in3, "w_in_bf16"), qb_padded, w_kv_b[0].astype(BF16), w_mem_kv[0].astype(BF16), w_out[0].astype(BF16)]
    splits = [(1, D_MODEL // 2)] + [(0, s.shape[0] // 2) for s in shards[1:]]
    (g_in,) = allgather_chips(shards[:1], ["lead"], "allgather_w_in")
    conv_all = allgather_devices(gdn_conv[0], "allgather_conv")
    conv_cols = gdn_conv.shape[2]
    conv_full = jnp.transpose(conv_all[0::2], (1, 0, 2)).reshape(GDN_CONV, N_CHIPS * conv_cols)
    *late_a, started_a = late_gather_start(shards[1:3], g_in, "late_gather_qkv_start")
    *late_b, started_b = late_gather_start(shards[3:], started_a, "late_gather_mem_out_start")
    late_shapes = [(N_CHIPS,) + s.shape for s in shards[1:]]

    def late_qkv(after):
        g_qb, g_kvb = late_gather_wait(*late_a, after, "late_gather_qkv_wait")
        return g_qb.reshape(-1, Q_LORA), _perm_w_kv_b(g_kvb)

    def late_mem_out(after):
        g_mem, g_out_w = late_gather_wait(*late_b, after, "late_gather_mem_out_wait")
        return g_mem.reshape(-1, g_mem.shape[2]), g_out_w.reshape(-1, g_out_w.shape[2])

    late_weights = (late_qkv, late_mem_out, (started_a, started_b))

    core = jnp.stack([cc]).astype(jnp.int32)
    chip_core = jnp.stack([chip, cc]).astype(jnp.int32)
    exchange = {}

    def big_grads_ready(gb):
        parts = [unpad_w_in_t(gb["w_in"]), gb["w_q_b"].reshape(late_shapes[0]), _unperm_w_kv_b(gb["w_kv_b"]),
                 gb["w_mem_kv"].reshape(late_shapes[2]), gb["w_out"].reshape(late_shapes[3])]
        from_sibling = swap_sibling(parts, "rs_sibling_partial", splits)
        chip_sums = add_pairs(parts, from_sibling, splits, core, "rs_add_sibling")
        sems, sums_thru, lands, token = exchange_chips_start(chip_sums, "rs_exchange_start")
        exchange.update(parts=parts, from_sibling=from_sibling, sems=sems, sums=sums_thru, lands=lands)
        return token

    sq, grad_x, g = local_step(x, mem, positions, loss_target, norm_in, pad_w_in_t(g_in), late_weights, big_grads_ready, q_a_norm,
                               kv_a_norm, conv_full, gdn_a_log, gdn_dt_bias, gdn_norm, mem_norm, norm_final)

    small_names = ("norm_in", "q_a_norm", "kv_a_norm", "gdn_a_log", "gdn_dt_bias", "gdn_norm", "mem_norm", "norm_final")
    small = dict(norm_in=norm_in, q_a_norm=q_a_norm, kv_a_norm=kv_a_norm, gdn_a_log=gdn_a_log, gdn_dt_bias=gdn_dt_bias,
                 gdn_norm=gdn_norm, mem_norm=mem_norm, norm_final=norm_final)
    m_small = dict(norm_in=m_norm_in, q_a_norm=m_q_a_norm, kv_a_norm=m_kv_a_norm, gdn_a_log=m_gdn_a_log,
                   gdn_dt_bias=m_gdn_dt_bias, gdn_norm=m_gdn_norm, mem_norm=m_mem_norm, norm_final=m_norm_final)
    v_small = dict(norm_in=v_norm_in, q_a_norm=v_q_a_norm, kv_a_norm=v_kv_a_norm, gdn_a_log=v_gdn_a_log,
                   gdn_dt_bias=v_gdn_dt_bias, gdn_norm=v_gdn_norm, mem_norm=v_mem_norm, norm_final=v_norm_final)
    rows = lambda d: jnp.stack([jnp.pad(d[n].reshape(-1), (0, 1024 - d[n].size)) for n in small_names])
    conv_rows = GDN_CONV * GDN_QKV // 1024
    g_block = jnp.concatenate([rows(g), g["gdn_conv"].reshape(conv_rows, 1024), sq, jnp.zeros((16 - 9 - conv_rows, 1024), F32)])
    g_block = sum_leading(allgather_devices(g_block, "allgather_small_grads"), "sum_small_grads")
    g_small_rows = g_block[:8]
    loss = 0.5 * jnp.sum(g_block[8 + conv_rows]) / D_MODEL
    g_conv = lax.dynamic_slice_in_dim(g_block[8:8 + conv_rows].reshape(GDN_CONV, GDN_QKV), chip * conv_cols, conv_cols, axis=1)
    d_s, m_s, v_s = adamw(rows(small), g_small_rows, rows(m_small), rows(v_small), "adamw_small")
    unrow = lambda r: {n: r[i, :small[n].size].reshape(small[n].shape) for i, n in enumerate(small_names)}
    g_out, d_out, m_out, v_out = unrow(g_small_rows), unrow(d_s), unrow(m_s), unrow(v_s)

    from_chips = exchange_chips_wait(exchange["sems"], exchange["sums"], exchange["lands"], d_s, "rs_exchange_wait")
    my_half = add_fives(exchange["parts"], exchange["from_sibling"], from_chips, splits, chip_core, "rs_add_chips")
    other_half = swap_sibling(my_half, "rs_sibling_final")

    d_out["gdn_conv"], m_out["gdn_conv"], v_out["gdn_conv"] = adamw(gdn_conv, g_conv, m_gdn_conv, v_gdn_conv, "adamw_gdn_conv")
    g_out["gdn_conv"] = g_conv[None]
    res = adamw_untiled_rows(w_in3, my_half[0], other_half[0], m_in3, v_in3, "adamw_w_in")
    g_out["w_in"], d_out["w_in"], m_out["w_in"], v_out["w_in"] = [jnp.transpose(r, (1, 2, 0)) for r in res]
    res = adamw_w_q_b(w_qb_t, my_half[1], other_half[1], m_qb_t, v_qb_t, "adamw_w_q_b")
    g_out["w_q_b"], d_out["w_q_b"], m_out["w_q_b"], v_out["w_q_b"] = [jnp.transpose(r)[None] for r in res]
    rest = dict(w_kv_b=(w_kv_b, m_w_kv_b, v_w_kv_b), w_mem_kv=(w_mem_kv, m_w_mem_kv, v_w_mem_kv), w_out=(w_out, m_w_out, v_w_out))
    for i, n in enumerate(big_names):
        if n in rest:
            w_n, m_n, v_n = rest[n]
            g_out[n], d_out[n], m_out[n], v_out[n] = adamw_halves(w_n, my_half[i], other_half[i], m_n, v_n, splits[i], core, "adamw_" + n)

    order = ("norm_in", "w_in", "q_a_norm", "w_q_b", "kv_a_norm", "w_kv_b", "gdn_conv", "gdn_a_log", "gdn_dt_bias",
             "gdn_norm", "mem_norm", "w_mem_kv", "w_out", "norm_final")
    return (loss, grad_x, *[g_out[n] for n in order], *[d_out[n] for n in order], *[m_out[n] for n in order],
            *[v_out[n] for n in order])
```

```python
import functools
import math

import jax
import jax.numpy as jnp
import numpy as np
from jax import lax
from jax.experimental import pallas as pl
from jax.experimental.pallas import tpu as pltpu

F32 = jnp.float32
BF16 = jnp.bfloat16
BS = pl.BlockSpec

D_MODEL = 1024
N_HEADS = 4
MLA_NOPE, MLA_ROPE, MLA_V = 128, 64, 128
Q_LORA, KV_LORA = 384, 256
ROPE_THETA = 10000.0
GDN_DK = GDN_DV = 128
GDN_CONV = 4
CHUNK = 64
MEM_DH = 128
D_MIX = 1536
GDN_QKV = 1536
D_IN = 4296
EPS = 1e-6
ADAM_LR, ADAM_B1, ADAM_B2, ADAM_EPS, ADAM_WD, ADAM_STEP = 0.001, 0.9, 0.999, 1e-08, 0.01, 10

OFF_MLA = 0
OFF_MEMQ = 1024
OFF_GDN = 1536
OFF_GATE = 3072
N_PAD = 4608
HEAD_PAD = 256
MLA_SCALE = (MLA_NOPE + MLA_ROPE) ** -0.5
MEM_SCALE = MEM_DH ** -0.5
GDN_SCALE = GDN_DK ** -0.5
NEG = -1e30

NN = ((1,), (0,))
NT = ((1,), (1,))
TN = ((0,), (0,))


def _dot(a, b, dims):
    return lax.dot_general(a, b, (dims, ((), ())), preferred_element_type=F32)


def _bdot(spec, a, b, precision=None):
    return jnp.einsum(spec, a, b, preferred_element_type=F32, precision=precision)


def _arb(n):
    return pltpu.CompilerParams(dimension_semantics=("arbitrary",) * n)


def _sigmoid(x):
    return 1.0 / (1.0 + jnp.exp(-x))


def _softplus(z):
    return jnp.maximum(z, 0.0) + jnp.log(1.0 + jnp.exp(-jnp.abs(z)))


def _rope(t, cos_row, sin_row):
    return t * cos_row + pltpu.roll(t, 64, 1) * sin_row


def _rope_bwd(d, cos_row, sin_row):
    return d * cos_row + pltpu.roll(d * sin_row, 64, 1)


def rms_fwd(x, gain, name, tm=512, after=()):
    T, n = x.shape
    tm = min(tm, T)

    def body(x_ref, g_ref, *rest):
        xv = x_ref[...]
        r = lax.rsqrt(jnp.mean(xv * xv, axis=-1, keepdims=True) + EPS)
        rest[-1][...] = (xv * r * g_ref[...]).astype(BF16)

    return pl.pallas_call(
        body, name=name, grid=(T // tm,),
        in_specs=[BS((tm, n), lambda i: (i, 0)), BS((1, n), lambda i: (0, 0))] + [BS(memory_space=pl.ANY)] * len(after),
        out_specs=BS((tm, n), lambda i: (i, 0)),
        out_shape=jax.ShapeDtypeStruct((T, n), BF16), compiler_params=_arb(1))(x, gain, *after)


def mm(a, b, kind, out_dtype, name, bm=512, bn=None, n_outer=False, after=()):
    if kind == "nn":
        (M, K), (_, N) = a.shape, b.shape
    elif kind == "nt":
        (M, K), (N, _) = a.shape, b.shape
    else:
        (K, M), (_, N) = a.shape, b.shape
    bm, bn = min(bm, M), min(bn or N, N)
    assert M % bm == 0 and N % bn == 0, (name, M, N, K)
    ij = (lambda g0, g1: (g1, g0)) if n_outer else (lambda g0, g1: (g0, g1))
    a_spec = BS((K, bm), lambda g0, g1: (0, ij(g0, g1)[0])) if kind == "tn" else BS((bm, K), lambda g0, g1: (ij(g0, g1)[0], 0))
    once = dict(pipeline_mode=pl.Buffered(1)) if bn == N else {}
    b_spec = (BS((bn, K), lambda g0, g1: (ij(g0, g1)[1], 0), **once) if kind == "nt"
              else BS((K, bn), lambda g0, g1: (0, ij(g0, g1)[1]), **once))
    dims = {"nn": NN, "nt": NT, "tn": TN}[kind]

    def body(a_ref, b_ref, *rest):
        rest[-1][...] = _dot(a_ref[...].astype(BF16), b_ref[...].astype(BF16), dims).astype(out_dtype)

    grid = (N // bn, M // bm) if n_outer else (M // bm, N // bn)
    return pl.pallas_call(
        body, name=name, grid=grid, in_specs=[a_spec, b_spec] + [BS(memory_space=pl.ANY)] * len(after),
        out_specs=BS((bm, bn), lambda g0, g1: ij(g0, g1)),
        out_shape=jax.ShapeDtypeStruct((M, N), out_dtype), compiler_params=_arb(2))(a, b, *after)


def mm_cols_nn(pieces, b, out_dtype, name, after, bm=512):
    M, N = pieces[0].shape[0], b.shape[1]
    bm = min(bm, M)
    n = len(pieces)
    widths = [p.shape[1] for p in pieces]
    offs = [sum(widths[:i]) for i in range(n)]

    def body(*refs):
        b_ref, o_ref = refs[n], refs[-1]
        acc = None
        for a_ref, off, w in zip(refs[:n], offs, widths):
            d = _dot(a_ref[...], b_ref[off:off + w, :], NN)
            acc = d if acc is None else acc + d
        o_ref[...] = acc.astype(out_dtype)

    return pl.pallas_call(
        body, name=name, grid=(M // bm,),
        in_specs=[BS((bm, w), lambda i: (i, 0)) for w in widths]
        + [BS(b.shape, lambda i: (0, 0), pipeline_mode=pl.Buffered(1)), BS(memory_space=pl.ANY)],
        out_specs=BS((bm, N), lambda i: (i, 0)), out_shape=jax.ShapeDtypeStruct((M, N), out_dtype),
        compiler_params=_arb(1))(*pieces, b, after)


def mm_cols_tn(pieces, b, out_dtype, name, bm=512):
    K, N = b.shape
    tiles = [p.shape[1] // bm for p in pieces]
    firsts = [sum(tiles[:i]) for i in range(len(tiles))]

    def body(*refs):
        b_ref, o_ref = refs[-2], refs[-1]
        i = pl.program_id(0)
        for a_ref, t0, n in zip(refs[:-2], firsts, tiles):
            @pl.when((i >= t0) & (i < t0 + n))
            def _(a_ref=a_ref):
                o_ref[...] = _dot(a_ref[...], b_ref[...], TN).astype(out_dtype)

    a_specs = [BS((K, bm), lambda i, t0=t0, n=n: (0, jnp.clip(i - t0, 0, n - 1))) for t0, n in zip(firsts, tiles)]
    return pl.pallas_call(
        body, name=name, grid=(sum(tiles),),
        in_specs=a_specs + [BS(b.shape, lambda i: (0, 0), pipeline_mode=pl.Buffered(1))],
        out_specs=BS((bm, N), lambda i: (i, 0)), out_shape=jax.ShapeDtypeStruct((sum(tiles) * bm, N), out_dtype),
        compiler_params=_arb(1))(*pieces, b)


def rope_tables(pos_col, inv_row, sgn_row, msk_row, tm=512, after=()):
    T = pos_col.shape[0]
    tm = min(tm, T)

    def body(p_ref, inv_ref, sgn_ref, msk_ref, *rest):
        c_ref, s_ref = rest[-2:]
        ang = p_ref[...].astype(F32) * inv_ref[...]
        c_ref[...] = jnp.cos(ang) * msk_ref[...]
        s_ref[...] = jnp.sin(ang) * sgn_ref[...]

    row = BS((1, 128), lambda i: (0, 0))
    return pl.pallas_call(
        body, name="rope_tables", grid=(T // tm,),
        in_specs=[BS((tm, 1), lambda i: (i, 0)), row, row, row] + [BS(memory_space=pl.ANY)] * len(after),
        out_specs=[BS((tm, 128), lambda i: (i, 0))] * 2,
        out_shape=[jax.ShapeDtypeStruct((T, 128), F32)] * 2, compiler_params=_arb(1))(pos_col, inv_row, sgn_row, msk_row, *after)


def mla_prep(P, gq, gkv, wq, wkv, cos_t, sin_t, tm=512):
    T = P.shape[0]
    tm = min(tm, T)

    def body(p_ref, gq_ref, gkv_ref, wq_ref, wkv_ref, c_ref, s_ref, q_ref, k_ref, v_ref, qn_ref, kvn_ref):
        p = p_ref[...]
        cq, ckv, kr = p[:, :Q_LORA], p[:, Q_LORA:Q_LORA + KV_LORA], p[:, 640:768]
        qn = (cq * lax.rsqrt(jnp.mean(cq * cq, axis=-1, keepdims=True) + EPS) * gq_ref[...]).astype(BF16)
        kvn = (ckv * lax.rsqrt(jnp.mean(ckv * ckv, axis=-1, keepdims=True) + EPS) * gkv_ref[...]).astype(BF16)
        qn_ref[...] = qn
        kvn_ref[...] = kvn
        q = _dot(qn, wq_ref[...], NT)
        kv = _dot(kvn, wkv_ref[...], NN)
        cos_row, sin_row = c_ref[...], s_ref[...]
        krr = _rope(kr, cos_row, sin_row).astype(BF16)
        for h in range(N_HEADS):
            lo = h * HEAD_PAD
            q_ref[:, lo:lo + 128] = (q[:, lo:lo + 128] * MLA_SCALE).astype(BF16)
            q_ref[:, lo + 128:lo + 256] = (_rope(q[:, lo + 128:lo + 256], cos_row, sin_row) * MLA_SCALE).astype(BF16)
            k_ref[:, lo:lo + 128] = kv[:, h * 128:(h + 1) * 128].astype(BF16)
            k_ref[:, lo + 128:lo + 256] = krr
            v_ref[:, lo:lo + 128] = kv[:, 512 + h * 128:512 + (h + 1) * 128].astype(BF16)
            v_ref[:, lo + 128:lo + 256] = jnp.ones((tm, 128), BF16)

    full = lambda r, c: BS((r, c), lambda i: (0, 0))
    rowb = lambda c: BS((tm, c), lambda i: (i, 0))
    return pl.pallas_call(
        body, name="mla_prep", grid=(T // tm,),
        in_specs=[rowb(1024), full(1, Q_LORA), full(1, KV_LORA), full(1024, Q_LORA), full(KV_LORA, 1024), rowb(128), rowb(128)],
        out_specs=[rowb(1024), rowb(1024), rowb(1024), rowb(Q_LORA), rowb(KV_LORA)],
        out_shape=[jax.ShapeDtypeStruct((T, 1024), BF16), jax.ShapeDtypeStruct((T, 1024), BF16),
                   jax.ShapeDtypeStruct((T, 1024), BF16), jax.ShapeDtypeStruct((T, Q_LORA), BF16),
                   jax.ShapeDtypeStruct((T, KV_LORA), BF16)],
        compiler_params=_arb(1))(P, gq, gkv, wq, wkv, cos_t, sin_t)


ATTN_HEADS_PER_STEP = 2
ATTN_STRIP = 32


def mla_attn_fwd(Q, K, V, B, S, tq=512, hp=ATTN_HEADS_PER_STEP):
    T = B * S
    tq = min(tq, S)
    nq = S // tq

    rs = min(ATTN_STRIP, tq)

    def body(q_ref, k_ref, v_ref, o_ref, lse_ref, m_s, acc_s, s_s, p_s, a_s):
        i = pl.program_id(2)
        m_s[...] = jnp.full_like(m_s, NEG)
        acc_s[...] = jnp.zeros_like(acc_s)

        def blk(j, masked):
            rows = pl.ds(pl.multiple_of(j * tq, tq), tq)
            for h in range(hp):
                hq = slice(h * HEAD_PAD, (h + 1) * HEAD_PAD)
                s_s[h] = _dot(q_ref[:, hq], k_ref[rows, hq], NT)
            for h in range(hp):
                for r0 in range(0, tq, rs):
                    rr = slice(r0, r0 + rs)
                    sv = s_s[h, rr, :]
                    if masked:
                        r = r0 + lax.broadcasted_iota(jnp.int32, (rs, tq), 0)
                        c = lax.broadcasted_iota(jnp.int32, (rs, tq), 1)
                        sv = jnp.where(r >= c, sv, NEG)
                    m_prev = m_s[h, rr, :]
                    m_new = jnp.maximum(m_prev, jnp.max(sv, axis=1, keepdims=True))
                    p_s[h, rr, :] = jnp.exp(sv - m_new).astype(BF16)
                    a_s[h, rr, :] = jnp.exp(m_prev - m_new)
                    m_s[h, rr, :] = m_new
            for h in range(hp):
                hq = slice(h * HEAD_PAD, (h + 1) * HEAD_PAD)
                acc_s[h] = a_s[h] * acc_s[h] + _dot(p_s[h], v_ref[rows, hq], NN)

        def loop(j, c):
            blk(j, False)
            return c

        lax.fori_loop(0, i, loop, 0)
        blk(i, True)
        for h in range(hp):
            den = acc_s[h, :, 128:256]
            o_ref[:, h * 128:(h + 1) * 128] = acc_s[h, :, 0:128] / den
            lse_ref[h] = m_s[h] + jnp.log(den[:, 0:1])

    return pl.pallas_call(
        body, name="mla_attn_fwd", grid=(B, N_HEADS // hp, nq),
        in_specs=[BS((tq, hp * HEAD_PAD), lambda b, h, i: (b * nq + i, h)),
                  BS((S, hp * HEAD_PAD), lambda b, h, i: (b, h)),
                  BS((S, hp * HEAD_PAD), lambda b, h, i: (b, h))],
        out_specs=[BS((tq, hp * 128), lambda b, h, i: (b * nq + i, h)),
                   BS((hp, tq, 1), lambda b, h, i: (h, b * nq + i, 0))],
        out_shape=[jax.ShapeDtypeStruct((T, 512), F32), jax.ShapeDtypeStruct((N_HEADS, T, 1), F32)],
        scratch_shapes=[pltpu.VMEM((hp, tq, 1), F32), pltpu.VMEM((hp, tq, HEAD_PAD), F32), pltpu.VMEM((hp, tq, tq), F32),
                        pltpu.VMEM((hp, tq, tq), BF16), pltpu.VMEM((hp, tq, 1), F32)],
        compiler_params=_arb(3))(Q, K, V)


def mla_attn_bwd(Q, K, V, O, dO, LSE, B, S, tq=512, hp=ATTN_HEADS_PER_STEP):
    T = B * S
    tq = min(tq, S)
    nq = S // tq

    rs = min(ATTN_STRIP, tq)

    def body(q_ref, k_ref, v_ref, o_ref, do_ref, lse_ref, dq_ref, dk_ref, dv_ref, delta_s, dk_s, dv_s, s_s, dp_s, p_s, ds_s):
        j = pl.program_id(2)

        @pl.when(j == 0)
        def _():
            dq_ref[...] = jnp.zeros_like(dq_ref)
            for h in range(hp):
                sl = slice(h * 128, (h + 1) * 128)
                delta_s[h] = jnp.sum(do_ref[:, sl] * o_ref[:, sl], axis=1, keepdims=True)

        dk_s[...] = jnp.zeros_like(dk_s)
        dv_s[...] = jnp.zeros_like(dv_s)

        def step(i, c):
            rows = pl.ds(pl.multiple_of(i * tq, tq), tq)
            for h in range(hp):
                sq, sv = slice(h * HEAD_PAD, (h + 1) * HEAD_PAD), slice(h * 128, (h + 1) * 128)
                s_s[h] = _dot(q_ref[rows, sq], k_ref[:, sq], NT)
                dp_s[h] = _dot(do_ref[rows, sv].astype(BF16), v_ref[:, h * HEAD_PAD:h * HEAD_PAD + 128], NT)
            for h in range(hp):
                for r0 in range(0, tq, rs):
                    rr = slice(r0, r0 + rs)
                    seq_rows = pl.ds(pl.multiple_of(i * tq + r0, rs), rs)
                    r = i * tq + r0 + lax.broadcasted_iota(jnp.int32, (rs, tq), 0)
                    cc = j * tq + lax.broadcasted_iota(jnp.int32, (rs, tq), 1)
                    p = jnp.where(r >= cc, jnp.exp(s_s[h, rr, :] - lse_ref[h, seq_rows, :]), 0.0)
                    p_s[h, rr, :] = p.astype(BF16)
                    ds_s[h, rr, :] = (p * (dp_s[h, rr, :] - delta_s[h, seq_rows, :])).astype(BF16)
            for h in range(hp):
                sq, sv = slice(h * HEAD_PAD, (h + 1) * HEAD_PAD), slice(h * 128, (h + 1) * 128)
                dv_s[:, sv] += _dot(p_s[h], do_ref[rows, sv].astype(BF16), TN)
                dk_s[:, sq] += _dot(ds_s[h], q_ref[rows, sq], TN)
                dq_ref[rows, sq] += _dot(ds_s[h], k_ref[:, sq], NN)
            return c

        lax.fori_loop(j, nq, step, 0)
        dk_ref[...] = dk_s[...]
        dv_ref[...] = dv_s[...]

    seq = lambda c: BS((S, c), lambda b, h, j: (b, h))
    blk = lambda c: BS((tq, c), lambda b, h, j: (b * nq + j, h))
    return pl.pallas_call(
        body, name="mla_attn_bwd", grid=(B, N_HEADS // hp, nq),
        in_specs=[seq(hp * HEAD_PAD), blk(hp * HEAD_PAD), blk(hp * HEAD_PAD), seq(hp * 128), seq(hp * 128),
                  BS((hp, S, 1), lambda b, h, j: (h, b, 0))],
        out_specs=[seq(hp * HEAD_PAD), blk(hp * HEAD_PAD), blk(hp * 128)],
        out_shape=[jax.ShapeDtypeStruct((T, 1024), F32), jax.ShapeDtypeStruct((T, 1024), F32),
                   jax.ShapeDtypeStruct((T, 512), F32)],
        scratch_shapes=[pltpu.VMEM((hp, S, 1), F32), pltpu.VMEM((tq, hp * HEAD_PAD), F32), pltpu.VMEM((tq, hp * 128), F32),
                        pltpu.VMEM((hp, tq, tq), F32), pltpu.VMEM((hp, tq, tq), F32),
                        pltpu.VMEM((hp, tq, tq), BF16), pltpu.VMEM((hp, tq, tq), BF16)],
        compiler_params=_arb(3))(Q, K, V, O, dO, LSE)


def mla_post_bwd(dQ, dK, dV, cos_t, sin_t, tm=512):
    T = dQ.shape[0]
    tm = min(tm, T)

    def body(dq_ref, dk_ref, dv_ref, c_ref, s_ref, ql_ref, kvl_ref, kr_ref):
        cos_row, sin_row = c_ref[...], s_ref[...]
        kr = jnp.zeros((tm, 128), F32)
        for h in range(N_HEADS):
            lo = h * HEAD_PAD
            ql_ref[:, lo:lo + 128] = (dq_ref[:, lo:lo + 128] * MLA_SCALE).astype(BF16)
            ql_ref[:, lo + 128:lo + 256] = (_rope_bwd(dq_ref[:, lo + 128:lo + 256], cos_row, sin_row) * MLA_SCALE).astype(BF16)
            kvl_ref[:, h * 128:(h + 1) * 128] = dk_ref[:, lo:lo + 128].astype(BF16)
            kr = kr + dk_ref[:, lo + 128:lo + 256]
        kvl_ref[:, 512:] = dv_ref[...].astype(BF16)
        kr_ref[...] = _rope_bwd(kr, cos_row, sin_row)

    rowb = lambda c: BS((tm, c), lambda i: (i, 0))
    return pl.pallas_call(
        body, name="mla_post_bwd", grid=(T // tm,),
        in_specs=[rowb(1024), rowb(1024), rowb(512), rowb(128), rowb(128)],
        out_specs=[rowb(1024), rowb(1024), rowb(128)],
        out_shape=[jax.ShapeDtypeStruct((T, 1024), BF16), jax.ShapeDtypeStruct((T, 1024), BF16),
                   jax.ShapeDtypeStruct((T, 128), F32)],
        compiler_params=_arb(1))(dQ, dK, dV, cos_t, sin_t)


def mla_norm_bwd(P, dqn, dkvn, dkr, dab, gq, gkv, tm=512):
    T = P.shape[0]
    tm = min(tm, T)

    def norm_bwd(x, dy, g):
        r = lax.rsqrt(jnp.mean(x * x, axis=-1, keepdims=True) + EPS)
        xh = x * r
        dxh = dy * g
        return r * (dxh - xh * jnp.mean(dxh * xh, axis=-1, keepdims=True)), jnp.sum(dy * xh, axis=0, keepdims=True)

    def body(p_ref, dqn_ref, dkvn_ref, dkr_ref, dab_ref, gq_ref, gkv_ref, o_ref, aq_ref, akv_ref):
        @pl.when(pl.program_id(0) == 0)
        def _():
            aq_ref[...] = jnp.zeros_like(aq_ref)
            akv_ref[...] = jnp.zeros_like(akv_ref)

        dcq, ggq = norm_bwd(p_ref[:, :Q_LORA], dqn_ref[...], gq_ref[...])
        dckv, ggkv = norm_bwd(p_ref[:, Q_LORA:640], dkvn_ref[...], gkv_ref[...])
        aq_ref[...] += ggq
        akv_ref[...] += ggkv
        o_ref[:, :Q_LORA] = dcq.astype(BF16)
        o_ref[:, Q_LORA:640] = dckv.astype(BF16)
        o_ref[:, 640:768] = dkr_ref[...].astype(BF16)
        o_ref[:, 768:896] = dab_ref[...]
        o_ref[:, 896:1024] = jnp.zeros((tm, 128), BF16)

    rowb = lambda c: BS((tm, c), lambda i: (i, 0))
    full = lambda c: BS((1, c), lambda i: (0, 0))
    return pl.pallas_call(
        body, name="mla_norm_bwd", grid=(T // tm,),
        in_specs=[rowb(1024), rowb(Q_LORA), rowb(KV_LORA), rowb(128), rowb(128), full(Q_LORA), full(KV_LORA)],
        out_specs=[rowb(1024), full(Q_LORA), full(KV_LORA)],
        out_shape=[jax.ShapeDtypeStruct((T, 1024), BF16), jax.ShapeDtypeStruct((1, Q_LORA), F32),
                   jax.ShapeDtypeStruct((1, KV_LORA), F32)],
        compiler_params=_arb(1))(P, dqn, dkvn, dkr, dab, gq, gkv)


def _mem_probs(qh, kh):
    s = _dot(qh, kh, NT) * MEM_SCALE
    p = jnp.exp(s - jnp.max(s, axis=1, keepdims=True))
    return p / jnp.sum(p, axis=1, keepdims=True)


def mem_attn_fwd(P, MKV, B, S, M, tq=512):
    T = B * S
    tq = min(tq, S)
    nq = S // tq

    def body(q_ref, kv_ref, o_ref):
        for h in range(N_HEADS):
            sl = slice(h * 128, (h + 1) * 128)
            p = _mem_probs(q_ref[:, sl].astype(BF16), kv_ref[:, sl])
            o_ref[:, sl] = _dot(p.astype(BF16), kv_ref[:, 512 + h * 128:512 + (h + 1) * 128], NN)

    return pl.pallas_call(
        body, name="mem_attn_fwd", grid=(B, nq),
        in_specs=[BS((tq, 512), lambda b, i: (b * nq + i, OFF_MEMQ // 512)), BS((M, 1024), lambda b, i: (b, 0))],
        out_specs=BS((tq, 512), lambda b, i: (b * nq + i, 0)),
        out_shape=jax.ShapeDtypeStruct((T, 512), F32), compiler_params=_arb(2))(P, MKV)


def mem_attn_bwd(P, MKV, dO, B, S, M, tq=512):
    T = B * S
    tq = min(tq, S)
    nq = S // tq

    def body(q_ref, kv_ref, do_ref, dq_ref, dkv_ref):
        @pl.when(pl.program_id(1) == 0)
        def _():
            dkv_ref[...] = jnp.zeros_like(dkv_ref)

        for h in range(N_HEADS):
            sl = slice(h * 128, (h + 1) * 128)
            sv = slice(512 + h * 128, 512 + (h + 1) * 128)
            qh = q_ref[:, sl].astype(BF16)
            kh = kv_ref[:, sl]
            do = do_ref[:, sl].astype(BF16)
            p = _mem_probs(qh, kh)
            dkv_ref[:, sv] += _dot(p.astype(BF16), do, TN)
            dp = _dot(do, kv_ref[:, sv], NT)
            ds = (p * (dp - jnp.sum(dp * p, axis=1, keepdims=True)) * MEM_SCALE).astype(BF16)
            dq_ref[:, sl] = _dot(ds, kh, NN).astype(BF16)
            dkv_ref[:, sl] += _dot(ds, qh, TN)

    return pl.pallas_call(
        body, name="mem_attn_bwd", grid=(B, nq),
        in_specs=[BS((tq, 512), lambda b, i: (b * nq + i, OFF_MEMQ // 512)), BS((M, 1024), lambda b, i: (b, 0)),
                  BS((tq, 512), lambda b, i: (b * nq + i, 0))],
        out_specs=[BS((tq, 512), lambda b, i: (b * nq + i, 0)), BS((M, 1024), lambda b, i: (b, 0))],
        out_shape=[jax.ShapeDtypeStruct((T, 512), BF16), jax.ShapeDtypeStruct((B * M, 1024), F32)],
        compiler_params=_arb(2))(P, MKV, dO)


def gain_grad(x, dy, name, tm=256):
    T, n = x.shape
    tm = min(tm, T)

    def body(x_ref, dy_ref, o_ref):
        @pl.when(pl.program_id(0) == 0)
        def _():
            o_ref[...] = jnp.zeros_like(o_ref)

        xv = x_ref[...]
        xh = xv * lax.rsqrt(jnp.mean(xv * xv, axis=-1, keepdims=True) + EPS)
        o_ref[...] += jnp.sum(dy_ref[...] * xh, axis=0, keepdims=True)

    return pl.pallas_call(
        body, name=name, grid=(T // tm,),
        in_specs=[BS((tm, n), lambda i: (i, 0))] * 2, out_specs=BS((1, n), lambda i: (0, 0)),
        out_shape=jax.ShapeDtypeStruct((1, n), F32), compiler_params=_arb(1))(x, dy)


def _conv_silu(x, w, t):
    y = x * w[3:4, :]
    for s in range(1, GDN_CONV):
        y = y + jnp.where(t >= s, pltpu.roll(x, s, 0), 0.0) * w[3 - s:4 - s, :]
    return y, _sigmoid(y)


def gdn_prep_fwd(P, conv_w, B, S):
    T = B * S

    def body(x_ref, w_ref, o_ref):
        kind = pl.program_id(1)
        t = lax.broadcasted_iota(jnp.int32, (S, 1), 0)
        y, sg = _conv_silu(x_ref[...], w_ref[...], t)
        a = y * sg
        scale = jnp.where(kind == 0, GDN_SCALE, 1.0).astype(F32)
        for h in range(N_HEADS):
            sl = slice(h * 128, (h + 1) * 128)
            seg = a[:, sl]
            n = lax.rsqrt(jnp.sum(seg * seg, axis=-1, keepdims=True) + EPS)
            o_ref[:, sl] = jnp.where(kind < 2, seg * (n * scale), seg)

    return pl.pallas_call(
        body, name="gdn_prep_fwd", grid=(B, 3),
        in_specs=[BS((S, 512), lambda b, k: (b, OFF_GDN // 512 + k)), BS((GDN_CONV, 512), lambda b, k: (0, k))],
        out_specs=BS((S, 512), lambda b, k: (b, k)),
        out_shape=jax.ShapeDtypeStruct((T, GDN_QKV), F32), compiler_params=_arb(2))(P, conv_w)


def gdn_prep_bwd(P, dqkv, conv_w, B, S):
    T = B * S

    def body(x_ref, d_ref, w_ref, o_ref, gw_ref):
        kind = pl.program_id(0)

        @pl.when(pl.program_id(1) == 0)
        def _():
            gw_ref[...] = jnp.zeros_like(gw_ref)

        t = lax.broadcasted_iota(jnp.int32, (S, 1), 0)
        x = x_ref[...]
        w = w_ref[...]
        y, sg = _conv_silu(x, w, t)
        a = y * sg
        scale = jnp.where(kind == 0, GDN_SCALE, 1.0).astype(F32)
        das = []
        for h in range(N_HEADS):
            sl = slice(h * 128, (h + 1) * 128)
            seg, dseg = a[:, sl], d_ref[:, sl]
            n = lax.rsqrt(jnp.sum(seg * seg, axis=-1, keepdims=True) + EPS)
            dn = scale * (n * dseg - seg * (n * n * n) * jnp.sum(dseg * seg, axis=-1, keepdims=True))
            das.append(jnp.where(kind < 2, dn, dseg))
        dy = jnp.concatenate(das, axis=1) * (sg * (1.0 + y * (1.0 - sg)))
        dx = dy * w[3:4, :]
        gw_ref[3:4, :] += jnp.sum(dy * x, axis=0, keepdims=True)
        for s in range(1, GDN_CONV):
            dx = dx + jnp.where(t + s < S, pltpu.roll(dy, S - s, 0), 0.0) * w[3 - s:4 - s, :]
            gw_ref[3 - s:4 - s, :] += jnp.sum(dy * jnp.where(t >= s, pltpu.roll(x, s, 0), 0.0), axis=0, keepdims=True)
        o_ref[...] = dx.astype(BF16)

    return pl.pallas_call(
        body, name="gdn_prep_bwd", grid=(3, B),
        in_specs=[BS((S, 512), lambda k, b: (b, OFF_GDN // 512 + k)), BS((S, 512), lambda k, b: (b, k)),
                  BS((GDN_CONV, 512), lambda k, b: (0, k))],
        out_specs=[BS((S, 512), lambda k, b: (b, k)), BS((GDN_CONV, 512), lambda k, b: (0, k))],
        out_shape=[jax.ShapeDtypeStruct((T, GDN_QKV), BF16), jax.ShapeDtypeStruct((GDN_CONV, GDN_QKV), F32)],
        compiler_params=_arb(2))(P, dqkv, conv_w)


def _chunk_row(n_rows):
    return lax.broadcasted_iota(jnp.int32, (n_rows, 1), 0) % CHUNK


def gdn_gate_fwd(P, alog_row, dt_row, B, S):
    T = B * S

    def body(x_ref, al_ref, dt_ref, o_ref):
        x = x_ref[...]
        lane = lax.broadcasted_iota(jnp.int32, (1, 128), 1)
        g = jnp.where(lane < 4, -jnp.exp(al_ref[...]) * _softplus(x + dt_ref[...]), 0.0)
        t = _chunk_row(S)
        for s in (1, 2, 4, 8, 16, 32):
            g = g + jnp.where(t >= s, pltpu.roll(g, s, 0), 0.0)
        o_ref[...] = jnp.where(lane < 4, g, jnp.where(lane < 8, _sigmoid(x), 0.0))

    row = BS((1, 128), lambda b: (0, 0))
    return pl.pallas_call(
        body, name="gdn_gate_fwd", grid=(B,),
        in_specs=[BS((S, 128), lambda b: (b, 768 // 128)), row, row], out_specs=BS((S, 128), lambda b: (b, 0)),
        out_shape=jax.ShapeDtypeStruct((T, 128), F32), compiler_params=_arb(1))(P, alog_row, dt_row)


def gdn_gate_bwd(P, dGB, alog_row, dt_row, B, S):
    T = B * S

    def body(x_ref, d_ref, al_ref, dt_ref, o_ref, acc_ref):
        @pl.when(pl.program_id(0) == 0)
        def _():
            acc_ref[...] = jnp.zeros_like(acc_ref)

        x, d = x_ref[...], d_ref[...]
        lane = lax.broadcasted_iota(jnp.int32, (1, 128), 1)
        z = x + dt_ref[...]
        coef = -jnp.exp(al_ref[...])
        g = coef * _softplus(z)
        da = jnp.where(lane < 4, d * coef * _sigmoid(z), 0.0)
        beta = _sigmoid(x)
        o_ref[...] = jnp.where(lane < 4, da, jnp.where(lane < 8, d * beta * (1.0 - beta), 0.0)).astype(BF16)
        acc_ref[0:1, :] += jnp.sum(jnp.where(lane < 4, d * g, 0.0), axis=0, keepdims=True)
        acc_ref[1:2, :] += jnp.sum(da, axis=0, keepdims=True)

    row = BS((1, 128), lambda b: (0, 0))
    return pl.pallas_call(
        body, name="gdn_gate_bwd", grid=(B,),
        in_specs=[BS((S, 128), lambda b: (b, 768 // 128)), BS((S, 128), lambda b: (b, 0)), row, row],
        out_specs=[BS((S, 128), lambda b: (b, 0)), BS((8, 128), lambda b: (0, 0))],
        out_shape=[jax.ShapeDtypeStruct((T, 128), BF16), jax.ShapeDtypeStruct((8, 128), F32)],
        compiler_params=_arb(1))(P, dGB, alog_row, dt_row)


def _chunk_masks(nc):
    r = lax.broadcasted_iota(jnp.int32, (nc, CHUNK, CHUNK), 1)
    c = lax.broadcasted_iota(jnp.int32, (nc, CHUNK, CHUNK), 2)
    return r >= c, r > c


def _chunk_local(q, k, gc, gr, beta, incl, strict):
    decay = jnp.exp(jnp.where(incl, gc - gr, NEG))
    kb = k * beta
    kbf = k.astype(BF16)
    m_kk = _bdot("gcd,gjd->gcj", kb.astype(BF16), kbf)
    l_mat = jnp.where(strict, m_kk * decay, 0.0)
    a_mat = _bdot("gcd,gjd->gcj", q.astype(BF16), kbf) * decay
    return decay, kb, l_mat, a_mat


def _split_bf16(x):
    hi = x.astype(BF16)
    return hi, (x - hi.astype(F32)).astype(BF16)


def _mm_split(ah, al, bh, bl):
    spec = "gij,gjk->gik"
    return _bdot(spec, ah, bh) + (_bdot(spec, ah, bl) + _bdot(spec, al, bh))


def gdn_chunk_fwd(qkv, GB, Grow, B, S, nc=8):
    T = B * S
    N = S // CHUNK
    nc = min(nc, N)
    nb = N // nc
    R = nc * CHUNK

    def body(q_ref, k_ref, v_ref, gb_ref, gr_ref, u_ref, w_ref, t_ref, a_ref):
        incl, strict = _chunk_masks(nc)
        eye = (lax.broadcasted_iota(jnp.int32, (nc, CHUNK, CHUNK), 1)
               == lax.broadcasted_iota(jnp.int32, (nc, CHUNK, CHUNK), 2)).astype(F32)
        for h in range(N_HEADS):
            sl = slice(h * 128, (h + 1) * 128)
            q = q_ref[:, sl].reshape(nc, CHUNK, 128)
            k = k_ref[:, sl].reshape(nc, CHUNK, 128)
            v = v_ref[:, sl].reshape(nc, CHUNK, 128)
            gc = gb_ref[:, h:h + 1].reshape(nc, CHUNK, 1)
            beta = gb_ref[:, 4 + h:5 + h].reshape(nc, CHUNK, 1)
            gr = gr_ref[h][:, None, :]
            _, kb, l_mat, a_mat = _chunk_local(q, k, gc, gr, beta, incl, strict)
            pw = -l_mat
            tinv = eye + pw
            for _ in range(5):
                ph, pl_ = _split_bf16(pw)
                pw = _mm_split(ph, pl_, ph, pl_)
                ph, pl_ = _split_bf16(pw)
                th, tl = _split_bf16(tinv)
                tinv = tinv + _mm_split(th, tl, ph, pl_)
            tb = tinv.astype(BF16)
            u = _bdot("gcj,gjv->gcv", tb, (v * beta).astype(BF16))
            w = _bdot("gcj,gjk->gck", tb, (kb * jnp.exp(gc)).astype(BF16))
            u_ref[:, sl] = u.reshape(R, 128)
            w_ref[:, sl] = w.reshape(R, 128)
            t_ref[h] = tinv
            a_ref[h] = a_mat

    rowb = lambda c, j: BS((R, c), lambda b, n: (b * nb + n, j))
    mat = BS((None, N_HEADS, nc, CHUNK, CHUNK), lambda b, n: (b, 0, n, 0, 0))
    return pl.pallas_call(
        body, name="gdn_chunk_fwd", grid=(B, nb),
        in_specs=[rowb(512, 0), rowb(512, 1), rowb(512, 2), rowb(128, 0),
                  BS((None, N_HEADS, nc, CHUNK), lambda b, n: (b, 0, n, 0))],
        out_specs=[rowb(512, 0), rowb(512, 0), mat, mat],
        out_shape=[jax.ShapeDtypeStruct((T, 512), F32), jax.ShapeDtypeStruct((T, 512), F32),
                   jax.ShapeDtypeStruct((B, N_HEADS, N, CHUNK, CHUNK), F32),
                   jax.ShapeDtypeStruct((B, N_HEADS, N, CHUNK, CHUNK), F32)],
        compiler_params=_arb(2))(qkv, qkv, qkv, GB, Grow)


def gdn_scan_fwd(qkv3, U3, W3, GB3, A, B, S):
    N = S // CHUNK

    def body(q_ref, k_ref, u_ref, w_ref, gb_ref, a_ref, o_ref, vn_ref, st_ref, s_s):
        @pl.when(pl.program_id(0) == 0)
        def _():
            s_s[...] = jnp.zeros_like(s_s)

        for b in range(B):
            for h in range(N_HEADS):
                sl = slice(h * 128, (h + 1) * 128)
                st = s_s[b, h]
                st_ref[b, h] = st
                stb = st.astype(BF16)
                g = gb_ref[b, :, h:h + 1]
                gl = g[CHUNK - 1:CHUNK, :]
                vn = u_ref[b, :, sl] - _dot(w_ref[b, :, sl].astype(BF16), stb, NN)
                vnb = vn.astype(BF16)
                o = _dot((q_ref[b, :, sl] * jnp.exp(g)).astype(BF16), stb, NN) + _dot(a_ref[b, h].astype(BF16), vnb, NN)
                vn_ref[b, :, sl] = vn
                o_ref[b, :, sl] = o
                s_s[b, h] = st * jnp.exp(gl) + _dot((k_ref[b, :, sl] * jnp.exp(gl - g)).astype(BF16), vnb, TN)

    tok = lambda c, j: BS((B, CHUNK, c), lambda n: (0, n, j))
    return pl.pallas_call(
        body, name="gdn_scan_fwd", grid=(N,),
        in_specs=[tok(512, 0), tok(512, 1), tok(512, 0), tok(512, 0), tok(128, 0),
                  BS((B, N_HEADS, None, CHUNK, CHUNK), lambda n: (0, 0, n, 0, 0))],
        out_specs=[tok(512, 0), tok(512, 0), BS((B, N_HEADS, None, 128, 128), lambda n: (0, 0, n, 0, 0))],
        out_shape=[jax.ShapeDtypeStruct((B, S, 512), F32), jax.ShapeDtypeStruct((B, S, 512), F32),
                   jax.ShapeDtypeStruct((B, N_HEADS, N, 128, 128), F32)],
        scratch_shapes=[pltpu.VMEM((B, N_HEADS, 128, 128), F32)],
        compiler_params=_arb(1))(qkv3, qkv3, U3, W3, GB3, A)


def gdn_scan_bwd(dO3, qkv3, W3, Vn3, GB3, A, St, B, S):
    N = S // CHUNK

    def body(do_ref, q_ref, k_ref, w_ref, vn_ref, gb_ref, a_ref, st_ref,
             du_ref, dw_ref, dq_ref, dk_ref, da_ref, dg_ref, ds_s):
        @pl.when(pl.program_id(0) == 0)
        def _():
            ds_s[...] = jnp.zeros_like(ds_s)

        lane = lax.broadcasted_iota(jnp.int32, (1, 128), 1)
        last = lax.broadcasted_iota(jnp.int32, (CHUNK, 1), 0) == CHUNK - 1
        for b in range(B):
            dg_all = jnp.zeros((CHUNK, 128), F32)
            for h in range(N_HEADS):
                sl = slice(h * 128, (h + 1) * 128)
                st = st_ref[b, h]
                stb = st.astype(BF16)
                dsn = ds_s[b, h]
                dsnb = dsn.astype(BF16)
                g = gb_ref[b, :, h:h + 1]
                gl = g[CHUNK - 1:CHUNK, :]
                egl = jnp.exp(gl)
                ekd = jnp.exp(gl - g)
                eg = jnp.exp(g)
                q, k = q_ref[b, :, sl], k_ref[b, :, sl]
                kd = k * ekd
                qg = q * eg
                do = do_ref[b, :, sl].astype(BF16)
                vnb = vn_ref[b, :, sl].astype(BF16)
                dvn = _dot(a_ref[b, h].astype(BF16), do, TN) + _dot(kd.astype(BF16), dsnb, NN)
                dvnb = dvn.astype(BF16)
                da_ref[b, h] = _dot(do, vnb, NT)
                dqg = _dot(do, stb, NT)
                dkd = _dot(vnb, dsnb, NT)
                ds_s[b, h] = (_dot(qg.astype(BF16), do, TN) + egl * dsn - _dot(w_ref[b, :, sl].astype(BF16), dvnb, TN))
                du_ref[b, :, sl] = dvn
                dw_ref[b, :, sl] = -_dot(dvnb, stb, NT)
                dq_ref[b, :, sl] = dqg * eg
                dk_ref[b, :, sl] = dkd * ekd
                ddel = jnp.sum(dkd * kd, axis=1, keepdims=True)
                dgl = jnp.sum(ddel, axis=0, keepdims=True) + jnp.sum(jnp.sum(st * dsn, axis=1, keepdims=True), axis=0, keepdims=True) * egl
                col = jnp.sum(dqg * qg, axis=1, keepdims=True) - ddel + jnp.where(last, dgl, 0.0)
                dg_all = jnp.where(lane == h, col, dg_all)
            dg_ref[b] = dg_all

    tok = lambda c, j: BS((B, CHUNK, c), lambda n: (0, N - 1 - n, j))
    mat = lambda d: BS((B, N_HEADS, None, d, d), lambda n: (0, 0, N - 1 - n, 0, 0))
    return pl.pallas_call(
        body, name="gdn_scan_bwd", grid=(N,),
        in_specs=[tok(512, 0), tok(512, 0), tok(512, 1), tok(512, 0), tok(512, 0), tok(128, 0), mat(CHUNK), mat(128)],
        out_specs=[tok(512, 0), tok(512, 0), tok(512, 0), tok(512, 0), mat(CHUNK), tok(128, 0)],
        out_shape=[jax.ShapeDtypeStruct((B, S, 512), F32)] * 4
        + [jax.ShapeDtypeStruct((B, N_HEADS, N, CHUNK, CHUNK), F32), jax.ShapeDtypeStruct((B, S, 128), F32)],
        scratch_shapes=[pltpu.VMEM((B, N_HEADS, 128, 128), F32)],
        compiler_params=_arb(1))(dO3, qkv3, qkv3, W3, Vn3, GB3, A, St)


def gdn_chunk_bwd(qkv, GB, Grow, Tinv, dA, dU, dW, dQ1, dK1, dG1, B, S, nc=8):
    T = B * S
    N = S // CHUNK
    nc = min(nc, N)
    nb = N // nc
    R = nc * CHUNK

    def body(q_ref, k_ref, v_ref, gb_ref, gr_ref, t_ref, da_ref, du_ref, dw_ref, dq1_ref, dk1_ref, dg1_ref, o_ref, dgb_ref):
        incl, strict = _chunk_masks(nc)
        lane = lax.broadcasted_iota(jnp.int32, (1, 128), 1)
        dg_all = dg1_ref[...]
        db_all = jnp.zeros((R, 128), F32)
        for h in range(N_HEADS):
            sl = slice(h * 128, (h + 1) * 128)
            q = q_ref[:, sl].reshape(nc, CHUNK, 128)
            k = k_ref[:, sl].reshape(nc, CHUNK, 128)
            v = v_ref[:, sl].reshape(nc, CHUNK, 128)
            gc = gb_ref[:, h:h + 1].reshape(nc, CHUNK, 1)
            beta = gb_ref[:, 4 + h:5 + h].reshape(nc, CHUNK, 1)
            gr = gr_ref[h][:, None, :]
            decay, kb, l_mat, a_mat = _chunk_local(q, k, gc, gr, beta, incl, strict)
            eg = jnp.exp(gc)
            kbg = kb * eg
            vb = v * beta
            tb = t_ref[h].astype(BF16)
            du = du_ref[:, sl].reshape(nc, CHUNK, 128).astype(BF16)
            dw = dw_ref[:, sl].reshape(nc, CHUNK, 128).astype(BF16)
            dvb = _bdot("gcj,gcv->gjv", tb, du)
            dkbg = _bdot("gcj,gck->gjk", tb, dw)
            dt = _bdot("gcv,gjv->gcj", du, vb.astype(BF16)) + _bdot("gck,gjk->gcj", dw, kbg.astype(BF16))
            tmp = _bdot("gac,gab->gcb", tb, dt.astype(BF16))
            dl = jnp.where(strict, -_bdot("gcb,gdb->gcd", tmp.astype(BF16), tb), 0.0)
            da = da_ref[h]
            dm = (dl * decay).astype(BF16)
            dqk = (da * decay).astype(BF16)
            kbf = k.astype(BF16)
            dkb = _bdot("gcj,gjd->gcd", dm, kbf) + dkbg * eg
            dk = (_bdot("gcj,gcd->gjd", dm, kb.astype(BF16)) + _bdot("gcj,gcd->gjd", dqk, q.astype(BF16))
                  + dk1_ref[:, sl].reshape(nc, CHUNK, 128) + dkb * beta)
            dq = _bdot("gcj,gjd->gcd", dqk, kbf) + dq1_ref[:, sl].reshape(nc, CHUNK, 128)
            e = dl * l_mat + da * a_mat
            dgc = (jnp.sum(e, axis=2, keepdims=True) - jnp.sum(jnp.swapaxes(e, 1, 2), axis=2, keepdims=True)
                   + jnp.sum(dkbg * kbg, axis=2, keepdims=True))
            dbeta = jnp.sum(dkb * k, axis=2, keepdims=True) + jnp.sum(dvb * v, axis=2, keepdims=True)
            o_ref[:, sl] = dq.reshape(R, 128)
            o_ref[:, 512 + h * 128:512 + (h + 1) * 128] = dk.reshape(R, 128)
            o_ref[:, 1024 + h * 128:1024 + (h + 1) * 128] = (dvb * beta).reshape(R, 128)
            dg_all = dg_all + jnp.where(lane == h, dgc.reshape(R, 1), 0.0)
            db_all = jnp.where(lane == 4 + h, dbeta.reshape(R, 1), db_all)
        t = _chunk_row(R)
        for s in (1, 2, 4, 8, 16, 32):
            dg_all = dg_all + jnp.where(t + s < CHUNK, pltpu.roll(dg_all, R - s, 0), 0.0)
        dgb_ref[...] = jnp.where(lane < 4, dg_all, db_all)

    rowb = lambda c, j: BS((R, c), lambda b, n: (b * nb + n, j))
    mat = BS((None, N_HEADS, nc, CHUNK, CHUNK), lambda b, n: (b, 0, n, 0, 0))
    return pl.pallas_call(
        body, name="gdn_chunk_bwd", grid=(B, nb),
        in_specs=[rowb(512, 0), rowb(512, 1), rowb(512, 2), rowb(128, 0),
                  BS((None, N_HEADS, nc, CHUNK), lambda b, n: (b, 0, n, 0)), mat, mat,
                  rowb(512, 0), rowb(512, 0), rowb(512, 0), rowb(512, 0), rowb(128, 0)],
        out_specs=[rowb(GDN_QKV, 0), rowb(128, 0)],
        out_shape=[jax.ShapeDtypeStruct((T, GDN_QKV), F32), jax.ShapeDtypeStruct((T, 128), F32)],
        compiler_params=_arb(2))(qkv, qkv, qkv, GB, Grow, Tinv, dA, dU, dW, dQ1, dK1, dG1)


def _gdn_out_norm(og, gg):
    outs, xhs, rs = [], [], []
    for h in range(N_HEADS):
        seg = og[:, h * 128:(h + 1) * 128]
        r = lax.rsqrt(jnp.mean(seg * seg, axis=-1, keepdims=True) + EPS)
        xh = seg * r
        outs.append(xh * gg)
        xhs.append(xh)
        rs.append(r)
    return outs, xhs, rs


def merge_fwd(o_mla, o_gdn, o_mem, P, x, tgt, w_out, g_gdn, g_fin, tm=256):
    T = x.shape[0]
    tm = min(tm, T)

    def body(om_ref, og_ref, oc_ref, gate_ref, x_ref, t_ref, w_ref, gg_ref, gf_ref, mix_ref, dx_ref, dxb_ref, sq_ref, gnf_ref):
        @pl.when(pl.program_id(0) == 0)
        def _():
            sq_ref[...] = jnp.zeros_like(sq_ref)
            gnf_ref[...] = jnp.zeros_like(gnf_ref)

        ogn, _, _ = _gdn_out_norm(og_ref[...], gg_ref[...])
        cat = jnp.concatenate([om_ref[...]] + ogn + [oc_ref[...]], axis=1)
        gt = gate_ref[...]
        mixed = (cat * (gt * _sigmoid(gt))).astype(BF16)
        mix_ref[...] = mixed
        x2 = x_ref[...] + _dot(mixed, w_ref[...], NN)
        r2 = lax.rsqrt(jnp.mean(x2 * x2, axis=-1, keepdims=True) + EPS)
        xh = x2 * r2
        gf = gf_ref[...]
        diff = xh * gf - t_ref[...]
        sq_ref[...] += jnp.sum(diff * diff, axis=0, keepdims=True)
        dy = diff * (1.0 / D_MODEL)
        gnf_ref[...] += jnp.sum(dy * xh, axis=0, keepdims=True)
        dxh = dy * gf
        dx = r2 * (dxh - xh * jnp.mean(dxh * xh, axis=-1, keepdims=True))
        dx_ref[...] = dx
        dxb_ref[...] = dx.astype(BF16)

    rowb = lambda c, j=0: BS((tm, c), lambda i: (i, j))
    full = lambda r, c: BS((r, c), lambda i: (0, 0))
    return pl.pallas_call(
        body, name="merge_fwd", grid=(T // tm,),
        in_specs=[rowb(512), rowb(512), rowb(512), rowb(D_MIX, OFF_GATE // D_MIX), rowb(D_MODEL), rowb(D_MODEL),
                  full(D_MIX, D_MODEL), full(1, 128), full(1, D_MODEL)],
        out_specs=[rowb(D_MIX), rowb(D_MODEL), rowb(D_MODEL), full(1, D_MODEL), full(1, D_MODEL)],
        out_shape=[jax.ShapeDtypeStruct((T, D_MIX), BF16), jax.ShapeDtypeStruct((T, D_MODEL), F32),
                   jax.ShapeDtypeStruct((T, D_MODEL), BF16),
                   jax.ShapeDtypeStruct((1, D_MODEL), F32), jax.ShapeDtypeStruct((1, D_MODEL), F32)],
        compiler_params=_arb(1))(o_mla, o_gdn, o_mem, P, x, tgt, w_out, g_gdn, g_fin)


def merge_bwd(dx2, o_mla, o_gdn, o_mem, P, w_out, g_gdn, tm=256):
    T = dx2.shape[0]
    tm = min(tm, T)

    def body(dx_ref, om_ref, og_ref, oc_ref, gate_ref, w_ref, gg_ref, dgate_ref, dom_ref, dog_ref, doc_ref, ggn_ref):
        @pl.when(pl.program_id(0) == 0)
        def _():
            ggn_ref[...] = jnp.zeros_like(ggn_ref)

        gg = gg_ref[...]
        dmix = _dot(dx_ref[...].astype(BF16), w_ref[...], NT)
        ogn, xhs, rs = _gdn_out_norm(og_ref[...], gg)
        cat = jnp.concatenate([om_ref[...]] + ogn + [oc_ref[...]], axis=1)
        gt = gate_ref[...]
        sg = _sigmoid(gt)
        dgate_ref[...] = (dmix * cat * (sg * (1.0 + gt * (1.0 - sg)))).astype(BF16)
        dcat = dmix * (gt * sg)
        dom_ref[...] = dcat[:, :512]
        doc_ref[...] = dcat[:, 1024:]
        acc = jnp.zeros((1, 128), F32)
        for h in range(N_HEADS):
            dseg = dcat[:, 512 + h * 128:512 + (h + 1) * 128]
            acc = acc + jnp.sum(dseg * xhs[h], axis=0, keepdims=True)
            dxh = dseg * gg
            dog_ref[:, h * 128:(h + 1) * 128] = rs[h] * (dxh - xhs[h] * jnp.mean(dxh * xhs[h], axis=-1, keepdims=True))
        ggn_ref[...] += acc

    rowb = lambda c, j=0: BS((tm, c), lambda i: (i, j))
    full = lambda r, c: BS((r, c), lambda i: (0, 0))
    return pl.pallas_call(
        body, name="merge_bwd", grid=(T // tm,),
        in_specs=[rowb(D_MODEL), rowb(512), rowb(512), rowb(512), rowb(D_MIX, OFF_GATE // D_MIX),
                  full(D_MIX, D_MODEL), full(1, 128)],
        out_specs=[rowb(D_MIX), rowb(512), rowb(512), rowb(512), full(1, 128)],
        out_shape=[jax.ShapeDtypeStruct((T, D_MIX), BF16)] + [jax.ShapeDtypeStruct((T, 512), F32)] * 3
        + [jax.ShapeDtypeStruct((1, 128), F32)],
        compiler_params=_arb(1))(dx2, o_mla, o_gdn, o_mem, P, w_out, g_gdn)


def in_norm_bwd(x, dh, dx2, gain, tm=256):
    T, n = x.shape
    tm = min(tm, T)

    def body(x_ref, dh_ref, dx2_ref, g_ref, o_ref, acc_ref):
        @pl.when(pl.program_id(0) == 0)
        def _():
            acc_ref[...] = jnp.zeros_like(acc_ref)

        xv = x_ref[...]
        r = lax.rsqrt(jnp.mean(xv * xv, axis=-1, keepdims=True) + EPS)
        xh = xv * r
        dy = dh_ref[...]
        acc_ref[...] += jnp.sum(dy * xh, axis=0, keepdims=True)
        dxh = dy * g_ref[...]
        o_ref[...] = dx2_ref[...] + r * (dxh - xh * jnp.mean(dxh * xh, axis=-1, keepdims=True))

    rowb = BS((tm, n), lambda i: (i, 0))
    full = BS((1, n), lambda i: (0, 0))
    return pl.pallas_call(
        body, name="in_norm_bwd", grid=(T // tm,),
        in_specs=[rowb, rowb, rowb, full], out_specs=[rowb, full],
        out_shape=[jax.ShapeDtypeStruct((T, n), F32), jax.ShapeDtypeStruct((1, n), F32)],
        compiler_params=_arb(1))(x, dh, dx2, gain)


W_IN_SHARD = D_IN // 4
_GDN0 = Q_LORA + KV_LORA + MLA_ROPE
_AB0 = _GDN0 + GDN_QKV
_MEMQ0 = _AB0 + 2 * N_HEADS
_GATE0 = _MEMQ0 + N_HEADS * MEM_DH


def _w_in_row_map():
    a, m, gt = _AB0 - 2 * W_IN_SHARD, _MEMQ0 - 2 * W_IN_SHARD, _GATE0 - 2 * W_IN_SHARD
    e0 = OFF_GDN + W_IN_SHARD - _GDN0
    e1 = e0 + W_IN_SHARD
    e2 = OFF_GATE + W_IN_SHARD - gt
    return [(0, 0, 0, 672), (0, 672, 704, 32), (2, a, 768, m - a), (2, m, OFF_MEMQ, gt - m), (0, _GDN0, OFF_GDN, W_IN_SHARD - _GDN0),
            (1, 0, e0, W_IN_SHARD), (2, 0, e1, a), (2, gt, OFF_GATE, W_IN_SHARD - gt), (3, 0, e2, W_IN_SHARD)]


_W_IN_ZERO_ROWS = [(672, 32), (736, 32), (776, 248)]
W_IN_LANES = 256


def pad_w_in_t(shards):
    per_half = shards.shape[3] // W_IN_LANES

    def body(s_ref, o_ref):
        for r0, n in _W_IN_ZERO_ROWS:
            o_ref[r0:r0 + n, :] = jnp.zeros((n, W_IN_LANES), o_ref.dtype)
        for q, src, dst, n in _w_in_row_map():
            o_ref[dst:dst + n, :] = s_ref[q, src:src + n, :]

    return pl.pallas_call(
        body, name="pad_w_in_t", grid=(D_MODEL // W_IN_LANES,),
        in_specs=[BS((N_CHIPS, None, W_IN_SHARD, W_IN_LANES), lambda j: (0, j // per_half, 0, j % per_half))],
        out_specs=BS((N_PAD, W_IN_LANES), lambda j: (0, j)),
        out_shape=jax.ShapeDtypeStruct((N_PAD, D_MODEL), shards.dtype), compiler_params=_arb(1))(shards)


def unpad_w_in_t(g):
    def body(g_ref, o_ref):
        for q, src, dst, n in _w_in_row_map():
            o_ref[q, src:src + n, :] = g_ref[dst:dst + n, :]

    return pl.pallas_call(
        body, name="unpad_w_in_t", grid=(D_MODEL // W_IN_LANES,),
        in_specs=[BS((N_PAD, W_IN_LANES), lambda j: (0, j))], out_specs=BS((N_CHIPS, W_IN_SHARD, W_IN_LANES), lambda j: (0, 0, j)),
        out_shape=jax.ShapeDtypeStruct((N_CHIPS, W_IN_SHARD, D_MODEL), g.dtype), compiler_params=_arb(1))(g)


def _pad_w_q_b_t(s):
    z = jnp.zeros((32, s.shape[2]), s.dtype)
    parts = []
    for h in range(N_HEADS):
        parts += [s[h, :128], s[h, 128:160], z, s[h, 160:192], z]
    return jnp.concatenate(parts, axis=0)


def _unpad_w_q_b_t(g):
    return jnp.stack([jnp.concatenate([g[h * HEAD_PAD:h * HEAD_PAD + 128], g[h * HEAD_PAD + 128:h * HEAD_PAD + 160],
                                       g[h * HEAD_PAD + 192:h * HEAD_PAD + 224]]) for h in range(N_HEADS)])


def _perm_w_kv_b(s):
    return jnp.concatenate([s[h, :, :128] for h in range(N_HEADS)] + [s[h, :, 128:] for h in range(N_HEADS)], axis=1)


def _unperm_w_kv_b(g):
    return jnp.stack([jnp.concatenate([g[:, h * 128:(h + 1) * 128], g[:, 512 + h * 128:512 + (h + 1) * 128]], axis=1)
                      for h in range(N_HEADS)])


def _lane_row(v4):
    return jnp.pad(v4.reshape(1, -1).astype(F32), ((0, 0), (0, 128 - v4.size)))


def _pack(pieces, n_rows):
    flat = jnp.concatenate([p.reshape(-1) for p in pieces])
    return jnp.pad(flat, (0, n_rows * 1024 - flat.size)).reshape(n_rows, 1024)


def _unpack(block, shapes):
    flat = block.reshape(-1)
    out, off = [], 0
    for shp in shapes:
        n = int(np.prod(shp))
        out.append(flat[off:off + n].reshape(shp))
        off += n
    return out


N_CHIPS = 4
MESH = pl.DeviceIdType.MESH
ANY = BS(memory_space=pl.ANY)


def _place():
    return lax.axis_index("x"), lax.axis_index("y"), lax.axis_index("c")


def _other_chips(x, y):
    return [(1 - x, y), (x, 1 - y), (1 - x, 1 - y)]


def _half(split, which):
    if split == "lead":
        return (which,)
    axis, size = split
    ds = pl.ds(pl.multiple_of(which * size, 16 if axis == 0 else 128), size)
    return (ds, slice(None)) if axis == 0 else (slice(None), ds)


SEM = BS(memory_space=pltpu.SEMAPHORE)
HBM = BS(memory_space=pltpu.HBM)
_IN_HBM = lambda a: pltpu.with_memory_space_constraint(a, pltpu.HBM)
_SIDE_EFFECT = pltpu.SideEffectType.DATAFLOW_SIDE_EFFECTING


def _late_gather_copies(s_refs, l_refs, send_sems, recv_sems, local_sems, with_arrivals):
    x, y, c = _place()
    sends, recvs, locals_ = [], [], []
    for i, (s_ref, l_ref) in enumerate(zip(s_refs, l_refs)):
        locals_.append(pltpu.make_async_copy(s_ref, l_ref.at[2 * x + y], local_sems.at[i]))
        for j, (px, py) in enumerate(_other_chips(x, y)):
            k = 3 * i + j
            sends.append(pltpu.make_async_remote_copy(src_ref=s_ref, dst_ref=l_ref.at[2 * x + y], send_sem=send_sems.at[k],
                                                      recv_sem=recv_sems.at[k], device_id=(px, py, c), device_id_type=MESH))
            if with_arrivals:
                recvs.append(pltpu.make_async_remote_copy(src_ref=s_ref, dst_ref=l_ref.at[2 * px + py], send_sem=send_sems.at[k],
                                                          recv_sem=recv_sems.at[k], device_id=(px, py, c), device_id_type=MESH))
    return sends, recvs, locals_


def late_gather_start(shards, after, name):
    n = len(shards)

    def body(*refs):
        s_refs, l_refs = refs[:n], refs[n:2 * n]
        send_sems, recv_sems, local_sems = refs[2 * n + 1:2 * n + 4]
        token = refs[-1]
        sends, _, locals_ = _late_gather_copies(s_refs, l_refs, send_sems, recv_sems, local_sems, False)
        for cp in locals_ + sends:
            cp.start()
        token[...] = jnp.zeros_like(token)

    lands = [lax.empty((N_CHIPS,) + s.shape, s.dtype) for s in shards]
    hbm_like = lambda a: pltpu.HBM(a.shape, a.dtype)
    out = pl.pallas_call(
        body, name=name,
        out_shape=[pltpu.SemaphoreType.DMA((3 * n,)), pltpu.SemaphoreType.DMA((3 * n,)), pltpu.SemaphoreType.DMA((n,))]
        + [hbm_like(s) for s in shards] + [hbm_like(l) for l in lands] + [jax.ShapeDtypeStruct((8, 128), F32)],
        in_specs=[HBM] * (2 * n) + [BS(memory_space=pl.ANY)], out_specs=[SEM] * 3 + [HBM] * (2 * n) + [BS(memory_space=pltpu.VMEM)],
        input_output_aliases={i: 3 + i for i in range(2 * n)},
        compiler_params=pltpu.CompilerParams(has_side_effects=_SIDE_EFFECT))(
            *[_IN_HBM(s) for s in shards], *[_IN_HBM(l) for l in lands], after)
    return out[:3], out[3:3 + n], out[3 + n:3 + 2 * n], out[-1]


def late_gather_wait(sems, shards, lands, after, name):
    n = len(shards)

    def body(*refs):
        s_refs, l_refs = refs[:n], refs[n:2 * n]
        send_sems, recv_sems, local_sems = refs[2 * n:2 * n + 3]
        sends, recvs, locals_ = _late_gather_copies(s_refs, l_refs, send_sems, recv_sems, local_sems, True)
        for cp in locals_:
            cp.wait()
        for cp in sends:
            cp.wait_send()
        for cp in recvs:
            cp.wait_recv()

    hbm_like = lambda a: pltpu.HBM(a.shape, a.dtype)
    out = pl.pallas_call(
        body, name=name, out_shape=[hbm_like(s) for s in shards] + [hbm_like(l) for l in lands],
        in_specs=[HBM] * (2 * n) + [SEM] * 3 + [BS(memory_space=pl.ANY)], out_specs=[HBM] * (2 * n),
        input_output_aliases={i: i for i in range(2 * n)},
        compiler_params=pltpu.CompilerParams(has_side_effects=_SIDE_EFFECT))(*shards, *lands, *sems, after)
    return out[n:]


def _half_gather_copies(s_ref, l_ref, send_sems, recv_sems, with_arrivals):
    x, y, c = _place()
    sends, recvs = [], []
    for j, (px, py) in enumerate(_other_chips(x, y)):
        sends.append(pltpu.make_async_remote_copy(src_ref=s_ref.at[c], dst_ref=l_ref.at[2 * x + y, c], send_sem=send_sems.at[j],
                                                  recv_sem=recv_sems.at[j], device_id=(px, py, c), device_id_type=MESH))
        if with_arrivals:
            recvs.append(pltpu.make_async_remote_copy(src_ref=s_ref.at[c], dst_ref=l_ref.at[2 * px + py, c], send_sem=send_sems.at[j],
                                                      recv_sem=recv_sems.at[j], device_id=(px, py, c), device_id_type=MESH))
    return sends, recvs


def half_gather_start(shard, name):
    def body(s_ref, l_ref, send_sems, recv_sems, local_sem, s_thru, l_thru, token):
        x, y, _ = _place()
        pltpu.make_async_copy(s_ref, l_ref.at[2 * x + y], local_sem.at[0]).start()
        for cp in _half_gather_copies(s_ref, l_ref, send_sems, recv_sems, False)[0]:
            cp.start()
        token[...] = jnp.zeros_like(token)

    land = lax.empty((N_CHIPS,) + shard.shape, shard.dtype)
    out = pl.pallas_call(
        body, name=name,
        out_shape=[pltpu.SemaphoreType.DMA((3,)), pltpu.SemaphoreType.DMA((3,)), pltpu.SemaphoreType.DMA((1,)),
                   pltpu.HBM(shard.shape, shard.dtype), pltpu.HBM(land.shape, land.dtype), jax.ShapeDtypeStruct((8, 128), F32)],
        in_specs=[HBM, HBM], out_specs=[SEM] * 3 + [HBM, HBM, BS(memory_space=pltpu.VMEM)],
        input_output_aliases={0: 3, 1: 4},
        compiler_params=pltpu.CompilerParams(has_side_effects=_SIDE_EFFECT))(_IN_HBM(shard), _IN_HBM(land))
    return out[:3], out[3], out[4], out[5]


def half_gather_wait(sems, shard, land, after, name):
    def body(s_ref, l_ref, send_sems, recv_sems, local_sem, *rest):
        x, y, _ = _place()
        pltpu.make_async_copy(s_ref, l_ref.at[2 * x + y], local_sem.at[0]).wait()
        sends, recvs = _half_gather_copies(s_ref, l_ref, send_sems, recv_sems, True)
        for cp in sends:
            cp.wait_send()
        for cp in recvs:
            cp.wait_recv()

    out = pl.pallas_call(
        body, name=name, out_shape=[pltpu.HBM(shard.shape, shard.dtype), pltpu.HBM(land.shape, land.dtype)],
        in_specs=[HBM, HBM] + [SEM] * 3 + [BS(memory_space=pl.ANY)] * len(after), out_specs=[HBM, HBM],
        input_output_aliases={0: 0, 1: 1},
        compiler_params=pltpu.CompilerParams(has_side_effects=_SIDE_EFFECT))(shard, land, *sems, *after)
    return out[1]


def pass_halves_to_sibling(land, name):
    def body(l_in, l_ref, send_sems, recv_sems):
        x, y, c = _place()
        copies = []
        for j, (px, py) in enumerate(_other_chips(x, y)):
            q = 2 * px + py
            give = pltpu.make_async_remote_copy(src_ref=l_ref.at[q, c], dst_ref=l_ref.at[q, c], send_sem=send_sems.at[j],
                                                recv_sem=recv_sems.at[j], device_id=(x, y, 1 - c), device_id_type=MESH)
            take = pltpu.make_async_remote_copy(src_ref=l_ref.at[q, c], dst_ref=l_ref.at[q, 1 - c], send_sem=send_sems.at[j],
                                                recv_sem=recv_sems.at[j], device_id=(x, y, 1 - c), device_id_type=MESH)
            give.start()
            copies.append((give, take))
        for give, take in copies:
            take.wait_recv()
            give.wait_send()

    return pl.pallas_call(
        body, name=name, in_specs=[ANY], out_specs=ANY, out_shape=jax.ShapeDtypeStruct(land.shape, land.dtype),
        input_output_aliases={0: 0},
        scratch_shapes=[pltpu.SemaphoreType.DMA((3,)), pltpu.SemaphoreType.DMA((3,))])(land)


def allgather_devices(block, name):
    R, C = block.shape

    def body(b_ref, o_ref, send_sems, recv_sems, local_sem):
        x, y, c = _place()
        me = 4 * x + 2 * y + c
        own = pltpu.make_async_copy(b_ref, o_ref.at[me], local_sem)
        own.start()
        copies = []
        for r in range(1, 8):
            px = 1 - x if r & 4 else x
            py = 1 - y if r & 2 else y
            pc = 1 - c if r & 1 else c
            send = pltpu.make_async_remote_copy(src_ref=b_ref, dst_ref=o_ref.at[me], send_sem=send_sems.at[r - 1],
                                                recv_sem=recv_sems.at[r - 1], device_id=(px, py, pc), device_id_type=MESH)
            recv = pltpu.make_async_remote_copy(src_ref=b_ref, dst_ref=o_ref.at[4 * px + 2 * py + pc], send_sem=send_sems.at[r - 1],
                                                recv_sem=recv_sems.at[r - 1], device_id=(px, py, pc), device_id_type=MESH)
            send.start()
            copies.append((send, recv))
        for send, recv in copies:
            recv.wait_recv()
            send.wait_send()
        own.wait()

    return pl.pallas_call(
        body, name=name, in_specs=[ANY], out_specs=ANY, out_shape=jax.ShapeDtypeStruct((8, R, C), block.dtype),
        scratch_shapes=[pltpu.SemaphoreType.DMA((7,)), pltpu.SemaphoreType.DMA((7,)), pltpu.SemaphoreType.DMA(())])(block)


def swap_sibling(arrs, name, splits=None):
    n = len(arrs)

    def sent(a_ref, i, c):
        return a_ref if splits is None else a_ref.at[(slice(None),) + _half(splits[i], 1 - c)]

    def out_shape(a, i):
        if splits is None:
            return a.shape
        axis, size = splits[i]
        return (a.shape[0], size, a.shape[2]) if axis == 0 else (a.shape[0], a.shape[1], size)

    def body(*refs):
        a_refs, o_refs = refs[:n], refs[n:2 * n]
        send_sems, recv_sems = refs[2 * n:]
        x, y, c = _place()
        copies = [pltpu.make_async_remote_copy(src_ref=sent(a_ref, i, c), dst_ref=o_ref, send_sem=send_sems.at[i],
                                               recv_sem=recv_sems.at[i], device_id=(x, y, 1 - c), device_id_type=MESH)
                  for i, (a_ref, o_ref) in enumerate(zip(a_refs, o_refs))]
        for cp in copies:
            cp.start()
        for cp in copies:
            cp.wait()

    return pl.pallas_call(
        body, name=name, in_specs=[ANY] * n, out_specs=[ANY] * n,
        out_shape=[jax.ShapeDtypeStruct(out_shape(a, i), a.dtype) for i, a in enumerate(arrs)],
        scratch_shapes=[pltpu.SemaphoreType.DMA((n,)), pltpu.SemaphoreType.DMA((n,))])(*arrs)


def _exchange_copies(p_refs, l_refs, send_sems, recv_sems):
    x, y, c = _place()
    return [pltpu.make_async_remote_copy(src_ref=p_ref.at[2 * px + py], dst_ref=l_ref.at[j], send_sem=send_sems.at[3 * i + j],
                                         recv_sem=recv_sems.at[3 * i + j], device_id=(px, py, c), device_id_type=MESH)
            for i, (p_ref, l_ref) in enumerate(zip(p_refs, l_refs)) for j, (px, py) in enumerate(_other_chips(x, y))]


def exchange_chips_start(parts, name):
    n = len(parts)

    def body(*refs):
        send_sems, recv_sems = refs[2 * n:2 * n + 2]
        for cp in _exchange_copies(refs[:n], refs[n:2 * n], send_sems, recv_sems):
            cp.start()
        refs[-1][...] = jnp.zeros_like(refs[-1])

    lands = [lax.empty((3,) + p.shape[1:], p.dtype) for p in parts]
    hbm_like = lambda a: pltpu.HBM(a.shape, a.dtype)
    out = pl.pallas_call(
        body, name=name,
        out_shape=[pltpu.SemaphoreType.DMA((3 * n,)), pltpu.SemaphoreType.DMA((3 * n,))]
        + [hbm_like(p) for p in parts] + [hbm_like(l) for l in lands] + [jax.ShapeDtypeStruct((8, 128), F32)],
        in_specs=[HBM] * (2 * n), out_specs=[SEM] * 2 + [HBM] * (2 * n) + [BS(memory_space=pltpu.VMEM)],
        input_output_aliases={i: 2 + i for i in range(2 * n)},
        compiler_params=pltpu.CompilerParams(has_side_effects=_SIDE_EFFECT))(*[_IN_HBM(p) for p in parts], *[_IN_HBM(l) for l in lands])
    return out[:2], out[2:2 + n], out[2 + n:2 + 2 * n], out[-1]


def exchange_chips_wait(sems, parts, lands, after, name):
    n = len(parts)

    def body(*refs):
        send_sems, recv_sems = refs[2 * n:2 * n + 2]
        for cp in _exchange_copies(refs[:n], refs[n:2 * n], send_sems, recv_sems):
            cp.wait_send()
            cp.wait_recv()

    hbm_like = lambda a: pltpu.HBM(a.shape, a.dtype)
    out = pl.pallas_call(
        body, name=name, out_shape=[hbm_like(p) for p in parts] + [hbm_like(l) for l in lands],
        in_specs=[HBM] * (2 * n) + [SEM] * 2 + [BS(memory_space=pl.ANY)], out_specs=[HBM] * (2 * n),
        input_output_aliases={i: i for i in range(2 * n)},
        compiler_params=pltpu.CompilerParams(has_side_effects=_SIDE_EFFECT))(*parts, *lands, *sems, after)
    return out[n:]


def _half_block(shape2, split):
    axis, size = split
    return (size, shape2[1]) if axis == 0 else (shape2[0], size)


def add_pairs(parts, halves, splits, core, name):
    n = len(parts)

    def body(s_ref, *refs):
        for a_ref, b_ref, o_ref in zip(refs[:n], refs[n:2 * n], refs[2 * n:]):
            o_ref[...] = (a_ref[...].astype(F32) + b_ref[...].astype(F32)).astype(BF16)

    def mine(i):
        blk = (None,) + _half_block(parts[i].shape[1:], splits[i])
        if splits[i][0] == 0:
            return BS(blk, lambda q, s: (q, s[0], 0))
        return BS(blk, lambda q, s: (q, 0, s[0]))

    half_specs = [BS((None,) + h.shape[1:], lambda q, s: (q, 0, 0)) for h in halves]
    return pl.pallas_call(
        body, name=name,
        grid_spec=pltpu.PrefetchScalarGridSpec(num_scalar_prefetch=1, grid=(N_CHIPS,),
                                               in_specs=[mine(i) for i in range(n)] + half_specs, out_specs=half_specs),
        out_shape=[jax.ShapeDtypeStruct(h.shape, BF16) for h in halves], compiler_params=_arb(1))(core, *parts, *halves)


def add_fives(parts, halves, from_chips, splits, chip_core, name):
    n = len(parts)

    def body(s_ref, *refs):
        for a_ref, b_ref, p_ref, o_ref in zip(refs[:n], refs[n:2 * n], refs[2 * n:3 * n], refs[3 * n:]):
            s = a_ref[...].astype(F32) + b_ref[...].astype(F32)
            for j in range(3):
                s = s + p_ref[j].astype(F32)
            o_ref[...] = s

    def mine(i):
        blk = (None,) + _half_block(parts[i].shape[1:], splits[i])
        if splits[i][0] == 0:
            return BS(blk, lambda g, s: (s[0], s[1], 0))
        return BS(blk, lambda g, s: (s[0], 0, s[1]))

    half_specs = [BS((None,) + h.shape[1:], lambda g, s: (s[0], 0, 0)) for h in halves]
    chip_specs = [BS(p.shape, lambda g, s: (0, 0, 0)) for p in from_chips]
    out_specs = [BS(h.shape[1:], lambda g, s: (0, 0)) for h in halves]
    return pl.pallas_call(
        body, name=name,
        grid_spec=pltpu.PrefetchScalarGridSpec(num_scalar_prefetch=1, grid=(1,),
                                               in_specs=[mine(i) for i in range(n)] + half_specs + chip_specs, out_specs=out_specs),
        out_shape=[jax.ShapeDtypeStruct(h.shape[1:], F32) for h in halves], compiler_params=_arb(1))(chip_core, *parts, *halves, *from_chips)


def sum_leading(a, name):
    def body(a_ref, o_ref):
        s = a_ref[0]
        for j in range(1, a.shape[0]):
            s = s + a_ref[j]
        o_ref[...] = s

    return pl.pallas_call(body, name=name, out_shape=jax.ShapeDtypeStruct(a.shape[1:], a.dtype))(a)


def _adamw_math(w, g, m, v):
    mn = ADAM_B1 * m + (1.0 - ADAM_B1) * g
    vn = ADAM_B2 * v + (1.0 - ADAM_B2) * (g * g)
    m_hat = mn / (1.0 - ADAM_B1 ** ADAM_STEP)
    v_hat = vn / (1.0 - ADAM_B2 ** ADAM_STEP)
    return -ADAM_LR * (m_hat / (jnp.sqrt(v_hat) + ADAM_EPS) + ADAM_WD * w), mn, vn


def adamw(w, g, m, v, name):
    R, C = g.shape
    lead = (None,) * (w.ndim - 2)

    def body(w_ref, g_ref, m_ref, v_ref, d_ref, mo_ref, vo_ref):
        d_ref[...], mo_ref[...], vo_ref[...] = _adamw_math(w_ref[...], g_ref[...], m_ref[...], v_ref[...])

    wblk = BS(lead + (R, C), lambda i: (0,) * w.ndim)
    gblk = BS((R, C), lambda i: (0, 0))
    return pl.pallas_call(
        body, name=name, grid=(1,), in_specs=[wblk, gblk, wblk, wblk], out_specs=[wblk] * 3,
        out_shape=[jax.ShapeDtypeStruct(w.shape, F32)] * 3, compiler_params=_arb(1))(w, g, m, v)


def adamw_halves(w, mine, other, m, v, split, core, name):
    R, C = w.shape[-2:]
    axis, size = split
    lead = (None,) * (w.ndim - 2)
    zeros = (0,) * (w.ndim - 2)
    if axis == 0:
        tr = size if size <= 256 else next(t for t in range(256, 7, -1) if size % t == 0 and t % 8 == 0)
        nb = size // tr
        whole = BS(lead + (tr, C), lambda hi, j, s: zeros + (hi * nb + j, 0))
        part = BS((tr, C), lambda hi, j, s: (j, 0))
    else:
        nb = size // 128
        whole = BS(lead + (R, 128), lambda hi, j, s: zeros + (0, hi * nb + j))
        part = BS((R, 128), lambda hi, j, s: (0, j))

    def body(s_ref, w_ref, a_ref, b_ref, m_ref, v_ref, g_ref, d_ref, mo_ref, vo_ref):
        g = jnp.where(pl.program_id(0) == s_ref[0], a_ref[...], b_ref[...])
        g_ref[...] = g
        d_ref[...], mo_ref[...], vo_ref[...] = _adamw_math(w_ref[...], g, m_ref[...], v_ref[...])

    return pl.pallas_call(
        body, name=name,
        grid_spec=pltpu.PrefetchScalarGridSpec(num_scalar_prefetch=1, grid=(2, nb),
                                               in_specs=[whole, part, part, whole, whole], out_specs=[whole] * 4),
        out_shape=[jax.ShapeDtypeStruct(w.shape, F32)] * 4, compiler_params=_arb(2))(core, w, mine, other, m, v)


def dense_bf16(w3, name):
    R, _, K = w3.shape
    kh = K // 2

    def body(w_hbm, o_ref, buf, sem):
        cp = pltpu.make_async_copy(w_hbm.at[:, 0], buf, sem)
        cp.start()
        cp.wait()
        o_ref[0] = buf[:, :kh].astype(BF16)
        o_ref[1] = buf[:, kh:].astype(BF16)

    return pl.pallas_call(
        body, name=name, in_specs=[ANY], out_specs=BS(memory_space=pltpu.VMEM), out_shape=jax.ShapeDtypeStruct((2, R, kh), BF16),
        scratch_shapes=[pltpu.VMEM((R, K), F32), pltpu.SemaphoreType.DMA(())])(w3)


ROW_BLOCK = 184


def adamw_untiled_rows(w3, mine, other, m3, v3, name):
    R, _, K = w3.shape
    kh = K // 2
    starts = list(range(0, R, ROW_BLOCK))
    sizes = [min(ROW_BLOCK, R - s) for s in starts]
    nblk = len(starts)

    def body(w_hbm, a_ref, b_ref, m_hbm, v_hbm, g_hbm, d_hbm, mo_hbm, vo_hbm,
             wbuf, mbuf, vbuf, gbuf, dbuf, mobuf, vobuf, in_sems, out_sems):
        first = lax.axis_index("c") == 0
        ins = []
        for k, (r0, n) in enumerate(zip(starts, sizes)):
            rows = pl.ds(r0, n)
            cps = [pltpu.make_async_copy(src.at[rows, 0], dst.at[rows], in_sems.at[3 * k + i])
                   for i, (src, dst) in enumerate(((w_hbm, wbuf), (m_hbm, mbuf), (v_hbm, vbuf)))]
            for cp in cps:
                cp.start()
            ins.append(cps)

        def update(rows):
            a, b = a_ref[rows, :], b_ref[rows, :]
            g = jnp.concatenate([jnp.where(first, a, b), jnp.where(first, b, a)], axis=1)
            gbuf[rows, :] = g
            dbuf[rows, :], mobuf[rows, :], vobuf[rows, :] = _adamw_math(wbuf[rows, :], g, mbuf[rows, :], vbuf[rows, :])

        outs = []
        for k, (r0, n) in enumerate(zip(starts, sizes)):
            for cp in ins[k]:
                cp.wait()
            groups, tail = n // 8, n % 8

            def group(i, carry, r0=r0):
                update(pl.ds(pl.multiple_of(r0 + i * 8, 8), 8))
                return carry

            lax.fori_loop(0, groups, group, 0)
            if tail:
                update(pl.ds(r0 + groups * 8, tail))
            rows = pl.ds(r0, n)
            cps = [pltpu.make_async_copy(src.at[rows], dst.at[rows, 0], out_sems.at[4 * k + i])
                   for i, (src, dst) in enumerate(((gbuf, g_hbm), (dbuf, d_hbm), (mobuf, mo_hbm), (vobuf, vo_hbm)))]
            for cp in cps:
                cp.start()
            outs += cps
        for cp in outs:
            cp.wait()

    vmem = BS(memory_space=pltpu.VMEM)
    return pl.pallas_call(
        body, name=name, in_specs=[ANY, vmem, vmem, ANY, ANY], out_specs=[ANY] * 4,
        out_shape=[jax.ShapeDtypeStruct(w3.shape, F32)] * 4,
        scratch_shapes=[pltpu.VMEM((R, K), F32)] * 7 + [pltpu.SemaphoreType.DMA((3 * nblk,)), pltpu.SemaphoreType.DMA((4 * nblk,))])(
            w3, mine, other, m3, v3)


def adamw_w_q_b(w, mine, other, m, v, name):
    def body(w_ref, a_ref, b_ref, m_ref, v_ref, g_ref, d_ref, mo_ref, vo_ref):
        first = lax.axis_index("c") == 0
        lo = jnp.where(first, a_ref[...], b_ref[...])
        hi = jnp.where(first, b_ref[...], a_ref[...])
        g = jnp.concatenate([lo, hi[0:32], hi[64:96]], axis=0)
        g_ref[...] = g
        d_ref[...], mo_ref[...], vo_ref[...] = _adamw_math(w_ref[...], g, m_ref[...], v_ref[...])

    return pl.pallas_call(body, name=name, out_shape=[jax.ShapeDtypeStruct(w.shape, F32)] * 4)(w, mine, other, m, v)


def local_step(x, mem, positions, tgt, norm_in, weights, big_grads_ready, q_a_norm, kv_a_norm, gdn_conv, gdn_a_log,
               gdn_dt_bias, gdn_norm, mem_norm, norm_final):
    B, S, D = x.shape
    M = mem.shape[1]
    T = B * S
    N = S // CHUNK
    x2d = x.reshape(T, D)
    mem2d = mem.reshape(B * M, D)
    tgt2d = tgt.reshape(T, D)

    alog_row, dt_row = _lane_row(gdn_a_log), _lane_row(gdn_dt_bias)

    half = MLA_ROPE // 2
    inv_freq = 1.0 / (ROPE_THETA ** (jnp.arange(half, dtype=F32) / half))
    z32 = jnp.zeros((half,), F32)
    o32 = jnp.ones((half,), F32)
    inv_row = jnp.concatenate([inv_freq, z32, inv_freq, z32]).reshape(1, 128)
    sgn_row = jnp.concatenate([-o32, z32, o32, z32]).reshape(1, 128)
    msk_row = jnp.concatenate([o32, z32, o32, z32]).reshape(1, 128)
    cos_t, sin_t = rope_tables(positions.reshape(T, 1), inv_row, sgn_row, msk_row, after=weights[3])

    h = rms_fwd(x2d, norm_in, "rms_in", after=weights[3])
    wp, behind = weights[0]((h, cos_t))
    P = mm(h, wp, "nt", F32, "in_proj", bm=512, bn=1536, n_outer=True, after=behind)
    wq, wkv = weights[1](P)
    Q, K, V, qn, kvn = mla_prep(P, q_a_norm, kv_a_norm, wq, wkv, cos_t, sin_t)
    o_mla, lse = mla_attn_fwd(Q, K, V, B, S)
    qkv = gdn_prep_fwd(P, gdn_conv, B, S)
    GB = gdn_gate_fwd(P, alog_row, dt_row, B, S)
    Grow = jnp.transpose(GB[:, :N_HEADS].reshape(B, N, CHUNK, N_HEADS), (0, 3, 1, 2))
    U, W, Tinv, A = gdn_chunk_fwd(qkv, GB, Grow, B, S)
    qkv3, GB3 = qkv.reshape(B, S, GDN_QKV), GB.reshape(B, S, 128)
    W3 = W.reshape(B, S, 512)
    o_gdn3, Vn3, St = gdn_scan_fwd(qkv3, U.reshape(B, S, 512), W3, GB3, A, B, S)
    o_gdn = o_gdn3.reshape(T, 512)
    w_mem_kv, w_out = weights[2](o_gdn)
    memn = rms_fwd(mem2d, mem_norm, "rms_mem")
    MKV = mm(memn, w_mem_kv, "nn", BF16, "mem_kv_proj")
    o_mem = mem_attn_fwd(P, MKV, B, S, M)
    mixed, dx2, dx2b, sq, g_norm_final = merge_fwd(o_mla, o_gdn, o_mem, P, x2d, tgt2d, w_out, gdn_norm, norm_final.reshape(1, D))

    g_w_out = mm(mixed, dx2b, "tn", BF16, "grad_w_out")
    dgate, do_mla, do_gdn, do_mem, g_gdn_norm = merge_bwd(dx2b, o_mla, o_gdn, o_mem, P, w_out, gdn_norm)

    dmemq, dMKV = mem_attn_bwd(P, MKV, do_mem, B, S, M)
    g_w_mem_kv = mm(memn, dMKV, "tn", BF16, "grad_w_mem_kv")
    dmemn = mm(dMKV, w_mem_kv, "nt", F32, "d_memn")
    g_mem_norm = gain_grad(mem2d, dmemn, "grad_mem_norm")

    dU3, dW3, dQ13, dK13, dA, dG13 = gdn_scan_bwd(do_gdn.reshape(B, S, 512), qkv3, W3, Vn3, GB3, A, St, B, S)
    r2 = lambda a: a.reshape(T, a.shape[-1])
    dqkv, dGB = gdn_chunk_bwd(qkv, GB, Grow, Tinv, dA, r2(dU3), r2(dW3), r2(dQ13), r2(dK13), r2(dG13), B, S)
    dPg, g_conv = gdn_prep_bwd(P, dqkv, gdn_conv, B, S)
    dab, g_ab = gdn_gate_bwd(P, dGB, alog_row, dt_row, B, S)

    dQ, dK, dV = mla_attn_bwd(Q, K, V, o_mla, do_mla, lse, B, S)
    dq_lin, dkv_lin, dkr = mla_post_bwd(dQ, dK, dV, cos_t, sin_t)
    dqn = mm(dq_lin, wq, "nn", F32, "d_qn")
    dkvn = mm(dkv_lin, wkv, "nt", F32, "d_kvn")
    g_wq = mm(dq_lin, qn, "tn", BF16, "grad_w_q_b")
    g_wkv = mm(kvn, dkv_lin, "tn", BF16, "grad_w_kv_b")
    dPm, g_q_a_norm, g_kv_a_norm = mla_norm_bwd(P, dqn, dkvn, dkr, dab, q_a_norm, kv_a_norm)

    dP = [dPm, dmemq, dPg, dgate]
    g_wp = mm_cols_tn(dP, h, BF16, "grad_w_in")
    started = big_grads_ready(dict(w_in=g_wp, w_q_b=g_wq, w_kv_b=g_wkv, w_mem_kv=g_w_mem_kv, w_out=g_w_out))
    dh = mm_cols_nn(dP, wp, F32, "d_h", started)
    grad_x, g_norm_in = in_norm_bwd(x2d, dh, dx2, norm_in)

    grads = dict(
        norm_in=g_norm_in, q_a_norm=g_q_a_norm, kv_a_norm=g_kv_a_norm, gdn_conv=g_conv,
        gdn_a_log=g_ab[0:1, :N_HEADS], gdn_dt_bias=g_ab[1:2, :N_HEADS], gdn_norm=g_gdn_norm,
        mem_norm=g_mem_norm, norm_final=g_norm_final)
    return sq, grad_x.reshape(B, S, D), grads


def kernel(x, mem, positions, norm_in, w_in, q_a_norm, w_q_b, kv_a_norm, w_kv_b, gdn_conv, gdn_a_log, gdn_dt_bias, gdn_norm, mem_norm, w_mem_kv, w_out, norm_final, loss_target, m_norm_in, m_w_in, m_q_a_norm, m_w_q_b, m_kv_a_norm, m_w_kv_b, m_gdn_conv, m_gdn_a_log, m_gdn_dt_bias, m_gdn_norm, m_mem_norm, m_w_mem_kv, m_w_out, m_norm_final, v_norm_in, v_w_in, v_q_a_norm, v_w_q_b, v_kv_a_norm, v_w_kv_b, v_gdn_conv, v_gdn_a_log, v_gdn_dt_bias, v_gdn_norm, v_mem_norm, v_w_mem_kv, v_w_out, v_norm_final):
    B = x.shape[0]
    cx, cy, cc = lax.axis_index("x"), lax.axis_index("y"), lax.axis_index("c")
    chip = 2 * cx + cy

    big_names = ("w_in", "w_q_b", "w_kv_b", "w_mem_kv", "w_out")
    rows_major = lambda a: jnp.transpose(a, (2, 0, 1))
    w_in3, m_in3, v_in3 = rows_major(w_in), rows_major(m_w_in), rows_major(v_w_in)
    w_qb_t, m_qb_t, v_qb_t = jnp.transpose(w_q_b[0]), jnp.transpose(m_w_q_b[0]), jnp.transpose(v_w_q_b[0])
    z32 = jnp.zeros((32, Q_LORA), BF16)
    qb_bf = w_qb_t.astype(BF16)
    qb_padded = jnp.concatenate([qb_bf[:160], z32, qb_bf[160:], z32])
    shards = [dense_bf16(w_in3, "w_in_bf16"), qb_padded, w_kv_b[0].astype(BF16), w_mem_kv[0].astype(BF16), w_out[0].astype(BF16)]
    splits = [(1, D_MODEL // 2)] + [(0, s.shape[0] // 2) for s in shards[1:]]
    *w_in_flight, w_in_started = half_gather_start(shards[0], "w_in_gather_start")
    conv_all = allgather_devices(gdn_conv[0], "allgather_conv")
    conv_cols = gdn_conv.shape[2]
    conv_full = jnp.transpose(conv_all[0::2], (1, 0, 2)).reshape(GDN_CONV, N_CHIPS * conv_cols)
    late_shapes = [(N_CHIPS,) + s.shape for s in shards[1:]]
    late = {}

    def w_in_ready(after):
        g_in = pass_halves_to_sibling(half_gather_wait(*w_in_flight, after, "w_in_gather_wait"), "w_in_gather_sibling")
        *late["a"], started_a = late_gather_start(shards[1:3], g_in, "late_gather_qkv_start")
        *late["b"], started_b = late_gather_start(shards[3:], started_a, "late_gather_mem_out_start")
        return pad_w_in_t(g_in), (started_a, started_b)

    def late_qkv(after):
        g_qb, g_kvb = late_gather_wait(*late["a"], after, "late_gather_qkv_wait")
        return g_qb.reshape(-1, Q_LORA), _perm_w_kv_b(g_kvb)

    def late_mem_out(after):
        g_mem, g_out_w = late_gather_wait(*late["b"], after, "late_gather_mem_out_wait")
        return g_mem.reshape(-1, g_mem.shape[2]), g_out_w.reshape(-1, g_out_w.shape[2])

    weights = (w_in_ready, late_qkv, late_mem_out, (w_in_started,))

    core = jnp.stack([cc]).astype(jnp.int32)
    chip_core = jnp.stack([chip, cc]).astype(jnp.int32)
    exchange = {}

    def big_grads_ready(gb):
        parts = [unpad_w_in_t(gb["w_in"]), gb["w_q_b"].reshape(late_shapes[0]), _unperm_w_kv_b(gb["w_kv_b"]),
                 gb["w_mem_kv"].reshape(late_shapes[2]), gb["w_out"].reshape(late_shapes[3])]
        from_sibling = swap_sibling(parts, "rs_sibling_partial", splits)
        chip_sums = add_pairs(parts, from_sibling, splits, core, "rs_add_sibling")
        sems, sums_thru, lands, token = exchange_chips_start(chip_sums, "rs_exchange_start")
        exchange.update(parts=parts, from_sibling=from_sibling, sems=sems, sums=sums_thru, lands=lands)
        return token

    sq, grad_x, g = local_step(x, mem, positions, loss_target, norm_in, weights, big_grads_ready, q_a_norm, kv_a_norm, conv_full,
                               gdn_a_log, gdn_dt_bias, gdn_norm, mem_norm, norm_final)

    small_names = ("norm_in", "q_a_norm", "kv_a_norm", "gdn_a_log", "gdn_dt_bias", "gdn_norm", "mem_norm", "norm_final")
    small = dict(norm_in=norm_in, q_a_norm=q_a_norm, kv_a_norm=kv_a_norm, gdn_a_log=gdn_a_log, gdn_dt_bias=gdn_dt_bias,
                 gdn_norm=gdn_norm, mem_norm=mem_norm, norm_final=norm_final)
    m_small = dict(norm_in=m_norm_in, q_a_norm=m_q_a_norm, kv_a_norm=m_kv_a_norm, gdn_a_log=m_gdn_a_log,
                   gdn_dt_bias=m_gdn_dt_bias, gdn_norm=m_gdn_norm, mem_norm=m_mem_norm, norm_final=m_norm_final)
    v_small = dict(norm_in=v_norm_in, q_a_norm=v_q_a_norm, kv_a_norm=v_kv_a_norm, gdn_a_log=v_gdn_a_log,
                   gdn_dt_bias=v_gdn_dt_bias, gdn_norm=v_gdn_norm, mem_norm=v_mem_norm, norm_final=v_norm_final)
    rows = lambda d: jnp.stack([jnp.pad(d[n].reshape(-1), (0, 1024 - d[n].size)) for n in small_names])
    conv_rows = GDN_CONV * GDN_QKV // 1024
    g_block = jnp.concatenate([rows(g), g["gdn_conv"].reshape(conv_rows, 1024), sq, jnp.zeros((16 - 9 - conv_rows, 1024), F32)])
    g_block = sum_leading(allgather_devices(g_block, "allgather_small_grads"), "sum_small_grads")
    g_small_rows = g_block[:8]
    loss = 0.5 * jnp.sum(g_block[8 + conv_rows]) / D_MODEL
    g_conv = lax.dynamic_slice_in_dim(g_block[8:8 + conv_rows].reshape(GDN_CONV, GDN_QKV), chip * conv_cols, conv_cols, axis=1)
    d_s, m_s, v_s = adamw(rows(small), g_small_rows, rows(m_small), rows(v_small), "adamw_small")
    unrow = lambda r: {n: r[i, :small[n].size].reshape(small[n].shape) for i, n in enumerate(small_names)}
    g_out, d_out, m_out, v_out = unrow(g_small_rows), unrow(d_s), unrow(m_s), unrow(v_s)

    from_chips = exchange_chips_wait(exchange["sems"], exchange["sums"], exchange["lands"], d_s, "rs_exchange_wait")
    my_half = add_fives(exchange["parts"], exchange["from_sibling"], from_chips, splits, chip_core, "rs_add_chips")
    other_half = swap_sibling(my_half, "rs_sibling_final")

    d_out["gdn_conv"], m_out["gdn_conv"], v_out["gdn_conv"] = adamw(gdn_conv, g_conv, m_gdn_conv, v_gdn_conv, "adamw_gdn_conv")
    g_out["gdn_conv"] = g_conv[None]
    res = adamw_untiled_rows(w_in3, my_half[0], other_half[0], m_in3, v_in3, "adamw_w_in")
    g_out["w_in"], d_out["w_in"], m_out["w_in"], v_out["w_in"] = [jnp.transpose(r, (1, 2, 0)) for r in res]
    res = adamw_w_q_b(w_qb_t, my_half[1], other_half[1], m_qb_t, v_qb_t, "adamw_w_q_b")
    g_out["w_q_b"], d_out["w_q_b"], m_out["w_q_b"], v_out["w_q_b"] = [jnp.transpose(r)[None] for r in res]
    rest = dict(w_kv_b=(w_kv_b, m_w_kv_b, v_w_kv_b), w_mem_kv=(w_mem_kv, m_w_mem_kv, v_w_mem_kv), w_out=(w_out, m_w_out, v_w_out))
    for i, n in enumerate(big_names):
        if n in rest:
            w_n, m_n, v_n = rest[n]
            g_out[n], d_out[n], m_out[n], v_out[n] = adamw_halves(w_n, my_half[i], other_half[i], m_n, v_n, splits[i], core, "adamw_" + n)

    order = ("norm_in", "w_in", "q_a_norm", "w_q_b", "kv_a_norm", "w_kv_b", "gdn_conv", "gdn_a_log", "gdn_dt_bias",
             "gdn_norm", "mem_norm", "w_mem_kv", "w_out", "norm_final")
    return (loss, grad_x, *[g_out[n] for n in order], *[d_out[n] for n in order], *[m_out[n] for n in order],
            *[v_out[n] for n in order])
```

```python
import functools
import math

import jax
import jax.numpy as jnp
import numpy as np
from jax import lax
from jax.experimental import pallas as pl
from jax.experimental.pallas import tpu as pltpu

F32 = jnp.float32
BF16 = jnp.bfloat16
BS = pl.BlockSpec

D_MODEL = 1024
N_HEADS = 4
MLA_NOPE, MLA_ROPE, MLA_V = 128, 64, 128
Q_LORA, KV_LORA = 384, 256
ROPE_THETA = 10000.0
GDN_DK = GDN_DV = 128
GDN_CONV = 4
CHUNK = 64
MEM_DH = 128
D_MIX = 1536
GDN_QKV = 1536
D_IN = 4296
EPS = 1e-6
ADAM_LR, ADAM_B1, ADAM_B2, ADAM_EPS, ADAM_WD, ADAM_STEP = 0.001, 0.9, 0.999, 1e-08, 0.01, 10

OFF_MLA = 0
OFF_MEMQ = 1024
OFF_GDN = 1536
OFF_GATE = 3072
N_PAD = 4608
HEAD_PAD = 256
MLA_SCALE = (MLA_NOPE + MLA_ROPE) ** -0.5
MEM_SCALE = MEM_DH ** -0.5
GDN_SCALE = GDN_DK ** -0.5
NEG = -1e30

NN = ((1,), (0,))
NT = ((1,), (1,))
TN = ((0,), (0,))


def _dot(a, b, dims):
    return lax.dot_general(a, b, (dims, ((), ())), preferred_element_type=F32)


def _bdot(spec, a, b, precision=None):
    return jnp.einsum(spec, a, b, preferred_element_type=F32, precision=precision)


def _arb(n):
    return pltpu.CompilerParams(dimension_semantics=("arbitrary",) * n)


def _sigmoid(x):
    return 1.0 / (1.0 + jnp.exp(-x))


def _softplus(z):
    return jnp.maximum(z, 0.0) + jnp.log(1.0 + jnp.exp(-jnp.abs(z)))


def _rope(t, cos_row, sin_row):
    return t * cos_row + pltpu.roll(t, 64, 1) * sin_row


def _rope_bwd(d, cos_row, sin_row):
    return d * cos_row + pltpu.roll(d * sin_row, 64, 1)


def rms_fwd(x, gain, name, tm=512, after=()):
    T, n = x.shape
    tm = min(tm, T)

    def body(x_ref, g_ref, *rest):
        xv = x_ref[...]
        r = lax.rsqrt(jnp.mean(xv * xv, axis=-1, keepdims=True) + EPS)
        rest[-1][...] = (xv * r * g_ref[...]).astype(BF16)

    return pl.pallas_call(
        body, name=name, grid=(T // tm,),
        in_specs=[BS((tm, n), lambda i: (i, 0)), BS((1, n), lambda i: (0, 0))] + [BS(memory_space=pl.ANY)] * len(after),
        out_specs=BS((tm, n), lambda i: (i, 0)),
        out_shape=jax.ShapeDtypeStruct((T, n), BF16), compiler_params=_arb(1))(x, gain, *after)


def mm(a, b, kind, out_dtype, name, bm=512, bn=None, n_outer=False, after=()):
    if kind == "nn":
        (M, K), (_, N) = a.shape, b.shape
    elif kind == "nt":
        (M, K), (N, _) = a.shape, b.shape
    else:
        (K, M), (_, N) = a.shape, b.shape
    bm, bn = min(bm, M), min(bn or N, N)
    assert M % bm == 0 and N % bn == 0, (name, M, N, K)
    ij = (lambda g0, g1: (g1, g0)) if n_outer else (lambda g0, g1: (g0, g1))
    a_spec = BS((K, bm), lambda g0, g1: (0, ij(g0, g1)[0])) if kind == "tn" else BS((bm, K), lambda g0, g1: (ij(g0, g1)[0], 0))
    once = dict(pipeline_mode=pl.Buffered(1)) if bn == N else {}
    b_spec = (BS((bn, K), lambda g0, g1: (ij(g0, g1)[1], 0), **once) if kind == "nt"
              else BS((K, bn), lambda g0, g1: (0, ij(g0, g1)[1]), **once))
    dims = {"nn": NN, "nt": NT, "tn": TN}[kind]

    def body(a_ref, b_ref, *rest):
        rest[-1][...] = _dot(a_ref[...].astype(BF16), b_ref[...].astype(BF16), dims).astype(out_dtype)

    grid = (N // bn, M // bm) if n_outer else (M // bm, N // bn)
    return pl.pallas_call(
        body, name=name, grid=grid, in_specs=[a_spec, b_spec] + [BS(memory_space=pl.ANY)] * len(after),
        out_specs=BS((bm, bn), lambda g0, g1: ij(g0, g1)),
        out_shape=jax.ShapeDtypeStruct((M, N), out_dtype), compiler_params=_arb(2))(a, b, *after)


def mm_cols_tn(pieces, b, out_dtype, name, bm=512):
    K, N = b.shape
    tiles = [p.shape[1] // bm for p in pieces]
    firsts = [sum(tiles[:i]) for i in range(len(tiles))]

    def body(*refs):
        b_ref, o_ref = refs[-2], refs[-1]
        i = pl.program_id(0)
        for a_ref, t0, n in zip(refs[:-2], firsts, tiles):
            @pl.when((i >= t0) & (i < t0 + n))
            def _(a_ref=a_ref):
                o_ref[...] = _dot(a_ref[...], b_ref[...], TN).astype(out_dtype)

    a_specs = [BS((K, bm), lambda i, t0=t0, n=n: (0, jnp.clip(i - t0, 0, n - 1))) for t0, n in zip(firsts, tiles)]
    return pl.pallas_call(
        body, name=name, grid=(sum(tiles),),
        in_specs=a_specs + [BS(b.shape, lambda i: (0, 0), pipeline_mode=pl.Buffered(1))],
        out_specs=BS((bm, N), lambda i: (i, 0)), out_shape=jax.ShapeDtypeStruct((sum(tiles) * bm, N), out_dtype),
        compiler_params=_arb(1))(*pieces, b)


def rope_tables(pos_col, inv_row, sgn_row, msk_row, tm=512, after=()):
    T = pos_col.shape[0]
    tm = min(tm, T)

    def body(p_ref, inv_ref, sgn_ref, msk_ref, *rest):
        c_ref, s_ref = rest[-2:]
        ang = p_ref[...].astype(F32) * inv_ref[...]
        c_ref[...] = jnp.cos(ang) * msk_ref[...]
        s_ref[...] = jnp.sin(ang) * sgn_ref[...]

    row = BS((1, 128), lambda i: (0, 0))
    return pl.pallas_call(
        body, name="rope_tables", grid=(T // tm,),
        in_specs=[BS((tm, 1), lambda i: (i, 0)), row, row, row] + [BS(memory_space=pl.ANY)] * len(after),
        out_specs=[BS((tm, 128), lambda i: (i, 0))] * 2,
        out_shape=[jax.ShapeDtypeStruct((T, 128), F32)] * 2, compiler_params=_arb(1))(pos_col, inv_row, sgn_row, msk_row, *after)


def mla_prep(P, gq, gkv, wq, wkv, cos_t, sin_t, tm=512):
    T = P.shape[0]
    tm = min(tm, T)

    def body(p_ref, gq_ref, gkv_ref, wq_ref, wkv_ref, c_ref, s_ref, q_ref, k_ref, v_ref, qn_ref, kvn_ref):
        p = p_ref[...]
        cq, ckv, kr = p[:, :Q_LORA], p[:, Q_LORA:Q_LORA + KV_LORA], p[:, 640:768]
        qn = (cq * lax.rsqrt(jnp.mean(cq * cq, axis=-1, keepdims=True) + EPS) * gq_ref[...]).astype(BF16)
        kvn = (ckv * lax.rsqrt(jnp.mean(ckv * ckv, axis=-1, keepdims=True) + EPS) * gkv_ref[...]).astype(BF16)
        qn_ref[...] = qn
        kvn_ref[...] = kvn
        q = _dot(qn, wq_ref[...], NT)
        kv = _dot(kvn, wkv_ref[...], NN)
        cos_row, sin_row = c_ref[...], s_ref[...]
        krr = _rope(kr, cos_row, sin_row).astype(BF16)
        for h in range(N_HEADS):
            lo = h * HEAD_PAD
            q_ref[:, lo:lo + 128] = (q[:, lo:lo + 128] * MLA_SCALE).astype(BF16)
            q_ref[:, lo + 128:lo + 256] = (_rope(q[:, lo + 128:lo + 256], cos_row, sin_row) * MLA_SCALE).astype(BF16)
            k_ref[:, lo:lo + 128] = kv[:, h * 128:(h + 1) * 128].astype(BF16)
            k_ref[:, lo + 128:lo + 256] = krr
            v_ref[:, lo:lo + 128] = kv[:, 512 + h * 128:512 + (h + 1) * 128].astype(BF16)
            v_ref[:, lo + 128:lo + 256] = jnp.ones((tm, 128), BF16)

    full = lambda r, c: BS((r, c), lambda i: (0, 0))
    rowb = lambda c: BS((tm, c), lambda i: (i, 0))
    return pl.pallas_call(
        body, name="mla_prep", grid=(T // tm,),
        in_specs=[rowb(1024), full(1, Q_LORA), full(1, KV_LORA), full(1024, Q_LORA), full(KV_LORA, 1024), rowb(128), rowb(128)],
        out_specs=[rowb(1024), rowb(1024), rowb(1024), rowb(Q_LORA), rowb(KV_LORA)],
        out_shape=[jax.ShapeDtypeStruct((T, 1024), BF16), jax.ShapeDtypeStruct((T, 1024), BF16),
                   jax.ShapeDtypeStruct((T, 1024), BF16), jax.ShapeDtypeStruct((T, Q_LORA), BF16),
                   jax.ShapeDtypeStruct((T, KV_LORA), BF16)],
        compiler_params=_arb(1))(P, gq, gkv, wq, wkv, cos_t, sin_t)


ATTN_HEADS_PER_STEP = 2
ATTN_STRIP = 32


def mla_attn_fwd(Q, K, V, B, S, tq=512, hp=ATTN_HEADS_PER_STEP):
    T = B * S
    tq = min(tq, S)
    nq = S // tq

    rs = min(ATTN_STRIP, tq)

    def body(q_ref, k_ref, v_ref, o_ref, lse_ref, m_s, acc_s, s_s, p_s, a_s):
        i = pl.program_id(2)
        m_s[...] = jnp.full_like(m_s, NEG)
        acc_s[...] = jnp.zeros_like(acc_s)

        def blk(j, masked):
            rows = pl.ds(pl.multiple_of(j * tq, tq), tq)
            for h in range(hp):
                hq = slice(h * HEAD_PAD, (h + 1) * HEAD_PAD)
                s_s[h] = _dot(q_ref[:, hq], k_ref[rows, hq], NT)
            for h in range(hp):
                for r0 in range(0, tq, rs):
                    rr = slice(r0, r0 + rs)
                    sv = s_s[h, rr, :]
                    if masked:
                        r = r0 + lax.broadcasted_iota(jnp.int32, (rs, tq), 0)
                        c = lax.broadcasted_iota(jnp.int32, (rs, tq), 1)
                        sv = jnp.where(r >= c, sv, NEG)
                    m_prev = m_s[h, rr, :]
                    m_new = jnp.maximum(m_prev, jnp.max(sv, axis=1, keepdims=True))
                    p_s[h, rr, :] = jnp.exp(sv - m_new).astype(BF16)
                    a_s[h, rr, :] = jnp.exp(m_prev - m_new)
                    m_s[h, rr, :] = m_new
            for h in range(hp):
                hq = slice(h * HEAD_PAD, (h + 1) * HEAD_PAD)
                acc_s[h] = a_s[h] * acc_s[h] + _dot(p_s[h], v_ref[rows, hq], NN)

        def loop(j, c):
            blk(j, False)
            return c

        lax.fori_loop(0, i, loop, 0)
        blk(i, True)
        for h in range(hp):
            den = acc_s[h, :, 128:256]
            o_ref[:, h * 128:(h + 1) * 128] = acc_s[h, :, 0:128] / den
            lse_ref[h] = m_s[h] + jnp.log(den[:, 0:1])

    return pl.pallas_call(
        body, name="mla_attn_fwd", grid=(B, N_HEADS // hp, nq),
        in_specs=[BS((tq, hp * HEAD_PAD), lambda b, h, i: (b * nq + i, h)),
                  BS((S, hp * HEAD_PAD), lambda b, h, i: (b, h)),
                  BS((S, hp * HEAD_PAD), lambda b, h, i: (b, h))],
        out_specs=[BS((tq, hp * 128), lambda b, h, i: (b * nq + i, h)),
                   BS((hp, tq, 1), lambda b, h, i: (h, b * nq + i, 0))],
        out_shape=[jax.ShapeDtypeStruct((T, 512), F32), jax.ShapeDtypeStruct((N_HEADS, T, 1), F32)],
        scratch_shapes=[pltpu.VMEM((hp, tq, 1), F32), pltpu.VMEM((hp, tq, HEAD_PAD), F32), pltpu.VMEM((hp, tq, tq), F32),
                        pltpu.VMEM((hp, tq, tq), BF16), pltpu.VMEM((hp, tq, 1), F32)],
        compiler_params=_arb(3))(Q, K, V)


def mla_attn_bwd(Q, K, V, O, dO, LSE, B, S, tq=512, hp=ATTN_HEADS_PER_STEP):
    T = B * S
    tq = min(tq, S)
    nq = S // tq

    rs = min(ATTN_STRIP, tq)

    def body(q_ref, k_ref, v_ref, o_ref, do_ref, lse_ref, dq_ref, dk_ref, dv_ref, delta_s, dk_s, dv_s, s_s, dp_s, p_s, ds_s):
        j = pl.program_id(2)

        @pl.when(j == 0)
        def _():
            dq_ref[...] = jnp.zeros_like(dq_ref)
            for h in range(hp):
                sl = slice(h * 128, (h + 1) * 128)
                delta_s[h] = jnp.sum(do_ref[:, sl] * o_ref[:, sl], axis=1, keepdims=True)

        dk_s[...] = jnp.zeros_like(dk_s)
        dv_s[...] = jnp.zeros_like(dv_s)

        def step(i, c):
            rows = pl.ds(pl.multiple_of(i * tq, tq), tq)
            for h in range(hp):
                sq, sv = slice(h * HEAD_PAD, (h + 1) * HEAD_PAD), slice(h * 128, (h + 1) * 128)
                s_s[h] = _dot(q_ref[rows, sq], k_ref[:, sq], NT)
                dp_s[h] = _dot(do_ref[rows, sv].astype(BF16), v_ref[:, h * HEAD_PAD:h * HEAD_PAD + 128], NT)
            for h in range(hp):
                for r0 in range(0, tq, rs):
                    rr = slice(r0, r0 + rs)
                    seq_rows = pl.ds(pl.multiple_of(i * tq + r0, rs), rs)
                    r = i * tq + r0 + lax.broadcasted_iota(jnp.int32, (rs, tq), 0)
                    cc = j * tq + lax.broadcasted_iota(jnp.int32, (rs, tq), 1)
                    p = jnp.where(r >= cc, jnp.exp(s_s[h, rr, :] - lse_ref[h, seq_rows, :]), 0.0)
                    p_s[h, rr, :] = p.astype(BF16)
                    ds_s[h, rr, :] = (p * (dp_s[h, rr, :] - delta_s[h, seq_rows, :])).astype(BF16)
            for h in range(hp):
                sq, sv = slice(h * HEAD_PAD, (h + 1) * HEAD_PAD), slice(h * 128, (h + 1) * 128)
                dv_s[:, sv] += _dot(p_s[h], do_ref[rows, sv].astype(BF16), TN)
                dk_s[:, sq] += _dot(ds_s[h], q_ref[rows, sq], TN)
                dq_ref[rows, sq] += _dot(ds_s[h], k_ref[:, sq], NN)
            return c

        lax.fori_loop(j, nq, step, 0)
        dk_ref[...] = dk_s[...]
        dv_ref[...] = dv_s[...]

    seq = lambda c: BS((S, c), lambda b, h, j: (b, h))
    blk = lambda c: BS((tq, c), lambda b, h, j: (b * nq + j, h))
    return pl.pallas_call(
        body, name="mla_attn_bwd", grid=(B, N_HEADS // hp, nq),
        in_specs=[seq(hp * HEAD_PAD), blk(hp * HEAD_PAD), blk(hp * HEAD_PAD), seq(hp * 128), seq(hp * 128),
                  BS((hp, S, 1), lambda b, h, j: (h, b, 0))],
        out_specs=[seq(hp * HEAD_PAD), blk(hp * HEAD_PAD), blk(hp * 128)],
        out_shape=[jax.ShapeDtypeStruct((T, 1024), F32), jax.ShapeDtypeStruct((T, 1024), F32),
                   jax.ShapeDtypeStruct((T, 512), F32)],
        scratch_shapes=[pltpu.VMEM((hp, S, 1), F32), pltpu.VMEM((tq, hp * HEAD_PAD), F32), pltpu.VMEM((tq, hp * 128), F32),
                        pltpu.VMEM((hp, tq, tq), F32), pltpu.VMEM((hp, tq, tq), F32),
                        pltpu.VMEM((hp, tq, tq), BF16), pltpu.VMEM((hp, tq, tq), BF16)],
        compiler_params=_arb(3))(Q, K, V, O, dO, LSE)


def mla_post_bwd(dQ, dK, dV, cos_t, sin_t, tm=512):
    T = dQ.shape[0]
    tm = min(tm, T)

    def body(dq_ref, dk_ref, dv_ref, c_ref, s_ref, ql_ref, kvl_ref, kr_ref):
        cos_row, sin_row = c_ref[...], s_ref[...]
        kr = jnp.zeros((tm, 128), F32)
        for h in range(N_HEADS):
            lo = h * HEAD_PAD
            ql_ref[:, lo:lo + 128] = (dq_ref[:, lo:lo + 128] * MLA_SCALE).astype(BF16)
            ql_ref[:, lo + 128:lo + 256] = (_rope_bwd(dq_ref[:, lo + 128:lo + 256], cos_row, sin_row) * MLA_SCALE).astype(BF16)
            kvl_ref[:, h * 128:(h + 1) * 128] = dk_ref[:, lo:lo + 128].astype(BF16)
            kr = kr + dk_ref[:, lo + 128:lo + 256]
        kvl_ref[:, 512:] = dv_ref[...].astype(BF16)
        kr_ref[...] = _rope_bwd(kr, cos_row, sin_row)

    rowb = lambda c: BS((tm, c), lambda i: (i, 0))
    return pl.pallas_call(
        body, name="mla_post_bwd", grid=(T // tm,),
        in_specs=[rowb(1024), rowb(1024), rowb(512), rowb(128), rowb(128)],
        out_specs=[rowb(1024), rowb(1024), rowb(128)],
        out_shape=[jax.ShapeDtypeStruct((T, 1024), BF16), jax.ShapeDtypeStruct((T, 1024), BF16),
                   jax.ShapeDtypeStruct((T, 128), F32)],
        compiler_params=_arb(1))(dQ, dK, dV, cos_t, sin_t)


def mla_norm_bwd(P, dqn, dkvn, dkr, dab, gq, gkv, tm=512):
    T = P.shape[0]
    tm = min(tm, T)

    def norm_bwd(x, dy, g):
        r = lax.rsqrt(jnp.mean(x * x, axis=-1, keepdims=True) + EPS)
        xh = x * r
        dxh = dy * g
        return r * (dxh - xh * jnp.mean(dxh * xh, axis=-1, keepdims=True)), jnp.sum(dy * xh, axis=0, keepdims=True)

    def body(p_ref, dqn_ref, dkvn_ref, dkr_ref, dab_ref, gq_ref, gkv_ref, o_ref, aq_ref, akv_ref):
        @pl.when(pl.program_id(0) == 0)
        def _():
            aq_ref[...] = jnp.zeros_like(aq_ref)
            akv_ref[...] = jnp.zeros_like(akv_ref)

        dcq, ggq = norm_bwd(p_ref[:, :Q_LORA], dqn_ref[...], gq_ref[...])
        dckv, ggkv = norm_bwd(p_ref[:, Q_LORA:640], dkvn_ref[...], gkv_ref[...])
        aq_ref[...] += ggq
        akv_ref[...] += ggkv
        o_ref[:, :Q_LORA] = dcq.astype(BF16)
        o_ref[:, Q_LORA:640] = dckv.astype(BF16)
        o_ref[:, 640:768] = dkr_ref[...].astype(BF16)
        o_ref[:, 768:896] = dab_ref[...]
        o_ref[:, 896:1024] = jnp.zeros((tm, 128), BF16)

    rowb = lambda c: BS((tm, c), lambda i: (i, 0))
    full = lambda c: BS((1, c), lambda i: (0, 0))
    return pl.pallas_call(
        body, name="mla_norm_bwd", grid=(T // tm,),
        in_specs=[rowb(1024), rowb(Q_LORA), rowb(KV_LORA), rowb(128), rowb(128), full(Q_LORA), full(KV_LORA)],
        out_specs=[rowb(1024), full(Q_LORA), full(KV_LORA)],
        out_shape=[jax.ShapeDtypeStruct((T, 1024), BF16), jax.ShapeDtypeStruct((1, Q_LORA), F32),
                   jax.ShapeDtypeStruct((1, KV_LORA), F32)],
        compiler_params=_arb(1))(P, dqn, dkvn, dkr, dab, gq, gkv)


def _mem_probs(qh, kh):
    s = _dot(qh, kh, NT) * MEM_SCALE
    p = jnp.exp(s - jnp.max(s, axis=1, keepdims=True))
    return p / jnp.sum(p, axis=1, keepdims=True)


def mem_attn_fwd(P, MKV, B, S, M, tq=512):
    T = B * S
    tq = min(tq, S)
    nq = S // tq

    def body(q_ref, kv_ref, o_ref):
        for h in range(N_HEADS):
            sl = slice(h * 128, (h + 1) * 128)
            p = _mem_probs(q_ref[:, sl].astype(BF16), kv_ref[:, sl])
            o_ref[:, sl] = _dot(p.astype(BF16), kv_ref[:, 512 + h * 128:512 + (h + 1) * 128], NN)

    return pl.pallas_call(
        body, name="mem_attn_fwd", grid=(B, nq),
        in_specs=[BS((tq, 512), lambda b, i: (b * nq + i, OFF_MEMQ // 512)), BS((M, 1024), lambda b, i: (b, 0))],
        out_specs=BS((tq, 512), lambda b, i: (b * nq + i, 0)),
        out_shape=jax.ShapeDtypeStruct((T, 512), F32), compiler_params=_arb(2))(P, MKV)


def mem_attn_bwd(P, MKV, dO, B, S, M, tq=512):
    T = B * S
    tq = min(tq, S)
    nq = S // tq

    def body(q_ref, kv_ref, do_ref, dq_ref, dkv_ref):
        @pl.when(pl.program_id(1) == 0)
        def _():
            dkv_ref[...] = jnp.zeros_like(dkv_ref)

        for h in range(N_HEADS):
            sl = slice(h * 128, (h + 1) * 128)
            sv = slice(512 + h * 128, 512 + (h + 1) * 128)
            qh = q_ref[:, sl].astype(BF16)
            kh = kv_ref[:, sl]
            do = do_ref[:, sl].astype(BF16)
            p = _mem_probs(qh, kh)
            dkv_ref[:, sv] += _dot(p.astype(BF16), do, TN)
            dp = _dot(do, kv_ref[:, sv], NT)
            ds = (p * (dp - jnp.sum(dp * p, axis=1, keepdims=True)) * MEM_SCALE).astype(BF16)
            dq_ref[:, sl] = _dot(ds, kh, NN).astype(BF16)
            dkv_ref[:, sl] += _dot(ds, qh, TN)

    return pl.pallas_call(
        body, name="mem_attn_bwd", grid=(B, nq),
        in_specs=[BS((tq, 512), lambda b, i: (b * nq + i, OFF_MEMQ // 512)), BS((M, 1024), lambda b, i: (b, 0)),
                  BS((tq, 512), lambda b, i: (b * nq + i, 0))],
        out_specs=[BS((tq, 512), lambda b, i: (b * nq + i, 0)), BS((M, 1024), lambda b, i: (b, 0))],
        out_shape=[jax.ShapeDtypeStruct((T, 512), BF16), jax.ShapeDtypeStruct((B * M, 1024), F32)],
        compiler_params=_arb(2))(P, MKV, dO)


def gain_grad(x, dy, name, tm=256):
    T, n = x.shape
    tm = min(tm, T)

    def body(x_ref, dy_ref, o_ref):
        @pl.when(pl.program_id(0) == 0)
        def _():
            o_ref[...] = jnp.zeros_like(o_ref)

        xv = x_ref[...]
        xh = xv * lax.rsqrt(jnp.mean(xv * xv, axis=-1, keepdims=True) + EPS)
        o_ref[...] += jnp.sum(dy_ref[...] * xh, axis=0, keepdims=True)

    return pl.pallas_call(
        body, name=name, grid=(T // tm,),
        in_specs=[BS((tm, n), lambda i: (i, 0))] * 2, out_specs=BS((1, n), lambda i: (0, 0)),
        out_shape=jax.ShapeDtypeStruct((1, n), F32), compiler_params=_arb(1))(x, dy)


def _conv_silu(x, w, t):
    y = x * w[3:4, :]
    for s in range(1, GDN_CONV):
        y = y + jnp.where(t >= s, pltpu.roll(x, s, 0), 0.0) * w[3 - s:4 - s, :]
    return y, _sigmoid(y)


def gdn_prep_fwd(P, conv_w, B, S):
    T = B * S

    def body(x_ref, w_ref, o_ref):
        kind = pl.program_id(1)
        t = lax.broadcasted_iota(jnp.int32, (S, 1), 0)
        y, sg = _conv_silu(x_ref[...], w_ref[...], t)
        a = y * sg
        scale = jnp.where(kind == 0, GDN_SCALE, 1.0).astype(F32)
        for h in range(N_HEADS):
            sl = slice(h * 128, (h + 1) * 128)
            seg = a[:, sl]
            n = lax.rsqrt(jnp.sum(seg * seg, axis=-1, keepdims=True) + EPS)
            o_ref[:, sl] = jnp.where(kind < 2, seg * (n * scale), seg)

    return pl.pallas_call(
        body, name="gdn_prep_fwd", grid=(B, 3),
        in_specs=[BS((S, 512), lambda b, k: (b, OFF_GDN // 512 + k)), BS((GDN_CONV, 512), lambda b, k: (0, k))],
        out_specs=BS((S, 512), lambda b, k: (b, k)),
        out_shape=jax.ShapeDtypeStruct((T, GDN_QKV), F32), compiler_params=_arb(2))(P, conv_w)


def gdn_prep_bwd(P, dqkv, conv_w, B, S):
    T = B * S

    def body(x_ref, d_ref, w_ref, o_ref, gw_ref):
        kind = pl.program_id(0)

        @pl.when(pl.program_id(1) == 0)
        def _():
            gw_ref[...] = jnp.zeros_like(gw_ref)

        t = lax.broadcasted_iota(jnp.int32, (S, 1), 0)
        x = x_ref[...]
        w = w_ref[...]
        y, sg = _conv_silu(x, w, t)
        a = y * sg
        scale = jnp.where(kind == 0, GDN_SCALE, 1.0).astype(F32)
        das = []
        for h in range(N_HEADS):
            sl = slice(h * 128, (h + 1) * 128)
            seg, dseg = a[:, sl], d_ref[:, sl]
            n = lax.rsqrt(jnp.sum(seg * seg, axis=-1, keepdims=True) + EPS)
            dn = scale * (n * dseg - seg * (n * n * n) * jnp.sum(dseg * seg, axis=-1, keepdims=True))
            das.append(jnp.where(kind < 2, dn, dseg))
        dy = jnp.concatenate(das, axis=1) * (sg * (1.0 + y * (1.0 - sg)))
        dx = dy * w[3:4, :]
        gw_ref[3:4, :] += jnp.sum(dy * x, axis=0, keepdims=True)
        for s in range(1, GDN_CONV):
            dx = dx + jnp.where(t + s < S, pltpu.roll(dy, S - s, 0), 0.0) * w[3 - s:4 - s, :]
            gw_ref[3 - s:4 - s, :] += jnp.sum(dy * jnp.where(t >= s, pltpu.roll(x, s, 0), 0.0), axis=0, keepdims=True)
        o_ref[...] = dx.astype(BF16)

    return pl.pallas_call(
        body, name="gdn_prep_bwd", grid=(3, B),
        in_specs=[BS((S, 512), lambda k, b: (b, OFF_GDN // 512 + k)), BS((S, 512), lambda k, b: (b, k)),
                  BS((GDN_CONV, 512), lambda k, b: (0, k))],
        out_specs=[BS((S, 512), lambda k, b: (b, k)), BS((GDN_CONV, 512), lambda k, b: (0, k))],
        out_shape=[jax.ShapeDtypeStruct((T, GDN_QKV), BF16), jax.ShapeDtypeStruct((GDN_CONV, GDN_QKV), F32)],
        compiler_params=_arb(2))(P, dqkv, conv_w)


def _chunk_row(n_rows):
    return lax.broadcasted_iota(jnp.int32, (n_rows, 1), 0) % CHUNK


def gdn_gate_fwd(P, alog_row, dt_row, B, S):
    T = B * S

    def body(x_ref, al_ref, dt_ref, o_ref):
        x = x_ref[...]
        lane = lax.broadcasted_iota(jnp.int32, (1, 128), 1)
        g = jnp.where(lane < 4, -jnp.exp(al_ref[...]) * _softplus(x + dt_ref[...]), 0.0)
        t = _chunk_row(S)
        for s in (1, 2, 4, 8, 16, 32):
            g = g + jnp.where(t >= s, pltpu.roll(g, s, 0), 0.0)
        o_ref[...] = jnp.where(lane < 4, g, jnp.where(lane < 8, _sigmoid(x), 0.0))

    row = BS((1, 128), lambda b: (0, 0))
    return pl.pallas_call(
        body, name="gdn_gate_fwd", grid=(B,),
        in_specs=[BS((S, 128), lambda b: (b, 768 // 128)), row, row], out_specs=BS((S, 128), lambda b: (b, 0)),
        out_shape=jax.ShapeDtypeStruct((T, 128), F32), compiler_params=_arb(1))(P, alog_row, dt_row)


def gdn_gate_bwd(P, dGB, alog_row, dt_row, B, S):
    T = B * S

    def body(x_ref, d_ref, al_ref, dt_ref, o_ref, acc_ref):
        @pl.when(pl.program_id(0) == 0)
        def _():
            acc_ref[...] = jnp.zeros_like(acc_ref)

        x, d = x_ref[...], d_ref[...]
        lane = lax.broadcasted_iota(jnp.int32, (1, 128), 1)
        z = x + dt_ref[...]
        coef = -jnp.exp(al_ref[...])
        g = coef * _softplus(z)
        da = jnp.where(lane < 4, d * coef * _sigmoid(z), 0.0)
        beta = _sigmoid(x)
        o_ref[...] = jnp.where(lane < 4, da, jnp.where(lane < 8, d * beta * (1.0 - beta), 0.0)).astype(BF16)
        acc_ref[0:1, :] += jnp.sum(jnp.where(lane < 4, d * g, 0.0), axis=0, keepdims=True)
        acc_ref[1:2, :] += jnp.sum(da, axis=0, keepdims=True)

    row = BS((1, 128), lambda b: (0, 0))
    return pl.pallas_call(
        body, name="gdn_gate_bwd", grid=(B,),
        in_specs=[BS((S, 128), lambda b: (b, 768 // 128)), BS((S, 128), lambda b: (b, 0)), row, row],
        out_specs=[BS((S, 128), lambda b: (b, 0)), BS((8, 128), lambda b: (0, 0))],
        out_shape=[jax.ShapeDtypeStruct((T, 128), BF16), jax.ShapeDtypeStruct((8, 128), F32)],
        compiler_params=_arb(1))(P, dGB, alog_row, dt_row)


def _chunk_masks(nc):
    r = lax.broadcasted_iota(jnp.int32, (nc, CHUNK, CHUNK), 1)
    c = lax.broadcasted_iota(jnp.int32, (nc, CHUNK, CHUNK), 2)
    return r >= c, r > c


def _chunk_local(q, k, gc, gr, beta, incl, strict):
    decay = jnp.exp(jnp.where(incl, gc - gr, NEG))
    kb = k * beta
    kbf = k.astype(BF16)
    m_kk = _bdot("gcd,gjd->gcj", kb.astype(BF16), kbf)
    l_mat = jnp.where(strict, m_kk * decay, 0.0)
    a_mat = _bdot("gcd,gjd->gcj", q.astype(BF16), kbf) * decay
    return decay, kb, l_mat, a_mat


def _split_bf16(x):
    hi = x.astype(BF16)
    return hi, (x - hi.astype(F32)).astype(BF16)


def _mm_split(ah, al, bh, bl):
    spec = "gij,gjk->gik"
    return _bdot(spec, ah, bh) + (_bdot(spec, ah, bl) + _bdot(spec, al, bh))


def gdn_chunk_fwd(qkv, GB, Grow, B, S, nc=8):
    T = B * S
    N = S // CHUNK
    nc = min(nc, N)
    nb = N // nc
    R = nc * CHUNK

    def body(q_ref, k_ref, v_ref, gb_ref, gr_ref, u_ref, w_ref, t_ref, a_ref):
        incl, strict = _chunk_masks(nc)
        eye = (lax.broadcasted_iota(jnp.int32, (nc, CHUNK, CHUNK), 1)
               == lax.broadcasted_iota(jnp.int32, (nc, CHUNK, CHUNK), 2)).astype(F32)
        for h in range(N_HEADS):
            sl = slice(h * 128, (h + 1) * 128)
            q = q_ref[:, sl].reshape(nc, CHUNK, 128)
            k = k_ref[:, sl].reshape(nc, CHUNK, 128)
            v = v_ref[:, sl].reshape(nc, CHUNK, 128)
            gc = gb_ref[:, h:h + 1].reshape(nc, CHUNK, 1)
            beta = gb_ref[:, 4 + h:5 + h].reshape(nc, CHUNK, 1)
            gr = gr_ref[h][:, None, :]
            _, kb, l_mat, a_mat = _chunk_local(q, k, gc, gr, beta, incl, strict)
            pw = -l_mat
            tinv = eye + pw
            for _ in range(5):
                ph, pl_ = _split_bf16(pw)
                pw = _mm_split(ph, pl_, ph, pl_)
                ph, pl_ = _split_bf16(pw)
                th, tl = _split_bf16(tinv)
                tinv = tinv + _mm_split(th, tl, ph, pl_)
            tb = tinv.astype(BF16)
            u = _bdot("gcj,gjv->gcv", tb, (v * beta).astype(BF16))
            w = _bdot("gcj,gjk->gck", tb, (kb * jnp.exp(gc)).astype(BF16))
            u_ref[:, sl] = u.reshape(R, 128)
            w_ref[:, sl] = w.reshape(R, 128).astype(BF16)
            t_ref[h] = tb
            a_ref[h] = a_mat.astype(BF16)

    rowb = lambda c, j: BS((R, c), lambda b, n: (b * nb + n, j))
    mat = BS((None, N_HEADS, nc, CHUNK, CHUNK), lambda b, n: (b, 0, n, 0, 0))
    return pl.pallas_call(
        body, name="gdn_chunk_fwd", grid=(B, nb),
        in_specs=[rowb(512, 0), rowb(512, 1), rowb(512, 2), rowb(128, 0),
                  BS((None, N_HEADS, nc, CHUNK), lambda b, n: (b, 0, n, 0))],
        out_specs=[rowb(512, 0), rowb(512, 0), mat, mat],
        out_shape=[jax.ShapeDtypeStruct((T, 512), F32), jax.ShapeDtypeStruct((T, 512), BF16),
                   jax.ShapeDtypeStruct((B, N_HEADS, N, CHUNK, CHUNK), BF16),
                   jax.ShapeDtypeStruct((B, N_HEADS, N, CHUNK, CHUNK), BF16)],
        compiler_params=_arb(2))(qkv, qkv, qkv, GB, Grow)


def gdn_scan_fwd(qkv3, U3, W3, GB3, A, B, S):
    N = S // CHUNK

    def body(q_ref, k_ref, u_ref, w_ref, gb_ref, a_ref, o_ref, vn_ref, st_ref, s_s):
        @pl.when(pl.program_id(0) == 0)
        def _():
            s_s[...] = jnp.zeros_like(s_s)

        for b in range(B):
            for h in range(N_HEADS):
                sl = slice(h * 128, (h + 1) * 128)
                st = s_s[b, h]
                st_ref[b, h] = st
                stb = st.astype(BF16)
                g = gb_ref[b, :, h:h + 1]
                gl = g[CHUNK - 1:CHUNK, :]
                vn = u_ref[b, :, sl] - _dot(w_ref[b, :, sl].astype(BF16), stb, NN)
                vnb = vn.astype(BF16)
                o = _dot((q_ref[b, :, sl] * jnp.exp(g)).astype(BF16), stb, NN) + _dot(a_ref[b, h].astype(BF16), vnb, NN)
                vn_ref[b, :, sl] = vnb
                o_ref[b, :, sl] = o
                s_s[b, h] = st * jnp.exp(gl) + _dot((k_ref[b, :, sl] * jnp.exp(gl - g)).astype(BF16), vnb, TN)

    tok = lambda c, j: BS((B, CHUNK, c), lambda n: (0, n, j))
    return pl.pallas_call(
        body, name="gdn_scan_fwd", grid=(N,),
        in_specs=[tok(512, 0), tok(512, 1), tok(512, 0), tok(512, 0), tok(128, 0),
                  BS((B, N_HEADS, None, CHUNK, CHUNK), lambda n: (0, 0, n, 0, 0))],
        out_specs=[tok(512, 0), tok(512, 0), BS((B, N_HEADS, None, 128, 128), lambda n: (0, 0, n, 0, 0))],
        out_shape=[jax.ShapeDtypeStruct((B, S, 512), F32), jax.ShapeDtypeStruct((B, S, 512), BF16),
                   jax.ShapeDtypeStruct((B, N_HEADS, N, 128, 128), F32)],
        scratch_shapes=[pltpu.VMEM((B, N_HEADS, 128, 128), F32)],
        compiler_params=_arb(1))(qkv3, qkv3, U3, W3, GB3, A)


def gdn_scan_bwd(dO3, qkv3, W3, Vn3, GB3, A, St, B, S):
    N = S // CHUNK

    def body(do_ref, q_ref, k_ref, w_ref, vn_ref, gb_ref, a_ref, st_ref,
             du_ref, dw_ref, dq_ref, dk_ref, da_ref, dg_ref, ds_s):
        @pl.when(pl.program_id(0) == 0)
        def _():
            ds_s[...] = jnp.zeros_like(ds_s)

        lane = lax.broadcasted_iota(jnp.int32, (1, 128), 1)
        last = lax.broadcasted_iota(jnp.int32, (CHUNK, 1), 0) == CHUNK - 1
        for b in range(B):
            dg_all = jnp.zeros((CHUNK, 128), F32)
            for h in range(N_HEADS):
                sl = slice(h * 128, (h + 1) * 128)
                st = st_ref[b, h]
                stb = st.astype(BF16)
                dsn = ds_s[b, h]
                dsnb = dsn.astype(BF16)
                g = gb_ref[b, :, h:h + 1]
                gl = g[CHUNK - 1:CHUNK, :]
                egl = jnp.exp(gl)
                ekd = jnp.exp(gl - g)
                eg = jnp.exp(g)
                q, k = q_ref[b, :, sl], k_ref[b, :, sl]
                kd = k * ekd
                qg = q * eg
                do = do_ref[b, :, sl].astype(BF16)
                vnb = vn_ref[b, :, sl].astype(BF16)
                dvn = _dot(a_ref[b, h].astype(BF16), do, TN) + _dot(kd.astype(BF16), dsnb, NN)
                dvnb = dvn.astype(BF16)
                da_ref[b, h] = _dot(do, vnb, NT)
                dqg = _dot(do, stb, NT)
                dkd = _dot(vnb, dsnb, NT)
                ds_s[b, h] = (_dot(qg.astype(BF16), do, TN) + egl * dsn - _dot(w_ref[b, :, sl].astype(BF16), dvnb, TN))
                du_ref[b, :, sl] = dvnb
                dw_ref[b, :, sl] = (-_dot(dvnb, stb, NT)).astype(BF16)
                dq_ref[b, :, sl] = dqg * eg
                dk_ref[b, :, sl] = dkd * ekd
                ddel = jnp.sum(dkd * kd, axis=1, keepdims=True)
                dgl = jnp.sum(ddel, axis=0, keepdims=True) + jnp.sum(jnp.sum(st * dsn, axis=1, keepdims=True), axis=0, keepdims=True) * egl
                col = jnp.sum(dqg * qg, axis=1, keepdims=True) - ddel + jnp.where(last, dgl, 0.0)
                dg_all = jnp.where(lane == h, col, dg_all)
            dg_ref[b] = dg_all

    tok = lambda c, j: BS((B, CHUNK, c), lambda n: (0, N - 1 - n, j))
    mat = lambda d: BS((B, N_HEADS, None, d, d), lambda n: (0, 0, N - 1 - n, 0, 0))
    return pl.pallas_call(
        body, name="gdn_scan_bwd", grid=(N,),
        in_specs=[tok(512, 0), tok(512, 0), tok(512, 1), tok(512, 0), tok(512, 0), tok(128, 0), mat(CHUNK), mat(128)],
        out_specs=[tok(512, 0), tok(512, 0), tok(512, 0), tok(512, 0), mat(CHUNK), tok(128, 0)],
        out_shape=[jax.ShapeDtypeStruct((B, S, 512), BF16)] * 2 + [jax.ShapeDtypeStruct((B, S, 512), F32)] * 2
        + [jax.ShapeDtypeStruct((B, N_HEADS, N, CHUNK, CHUNK), F32), jax.ShapeDtypeStruct((B, S, 128), F32)],
        scratch_shapes=[pltpu.VMEM((B, N_HEADS, 128, 128), F32)],
        compiler_params=_arb(1))(dO3, qkv3, qkv3, W3, Vn3, GB3, A, St)


def gdn_chunk_bwd(qkv, GB, Grow, Tinv, dA, dU, dW, dQ1, dK1, dG1, B, S, nc=8):
    T = B * S
    N = S // CHUNK
    nc = min(nc, N)
    nb = N // nc
    R = nc * CHUNK

    def body(q_ref, k_ref, v_ref, gb_ref, gr_ref, t_ref, da_ref, du_ref, dw_ref, dq1_ref, dk1_ref, dg1_ref, o_ref, dgb_ref):
        incl, strict = _chunk_masks(nc)
        lane = lax.broadcasted_iota(jnp.int32, (1, 128), 1)
        dg_all = dg1_ref[...]
        db_all = jnp.zeros((R, 128), F32)
        for h in range(N_HEADS):
            sl = slice(h * 128, (h + 1) * 128)
            q = q_ref[:, sl].reshape(nc, CHUNK, 128)
            k = k_ref[:, sl].reshape(nc, CHUNK, 128)
            v = v_ref[:, sl].reshape(nc, CHUNK, 128)
            gc = gb_ref[:, h:h + 1].reshape(nc, CHUNK, 1)
            beta = gb_ref[:, 4 + h:5 + h].reshape(nc, CHUNK, 1)
            gr = gr_ref[h][:, None, :]
            decay, kb, l_mat, a_mat = _chunk_local(q, k, gc, gr, beta, incl, strict)
            eg = jnp.exp(gc)
            kbg = kb * eg
            vb = v * beta
            tb = t_ref[h].astype(BF16)
            du = du_ref[:, sl].reshape(nc, CHUNK, 128).astype(BF16)
            dw = dw_ref[:, sl].reshape(nc, CHUNK, 128).astype(BF16)
            dvb = _bdot("gcj,gcv->gjv", tb, du)
            dkbg = _bdot("gcj,gck->gjk", tb, dw)
            dt = _bdot("gcv,gjv->gcj", du, vb.astype(BF16)) + _bdot("gck,gjk->gcj", dw, kbg.astype(BF16))
            tmp = _bdot("gac,gab->gcb", tb, dt.astype(BF16))
            dl = jnp.where(strict, -_bdot("gcb,gdb->gcd", tmp.astype(BF16), tb), 0.0)
            da = da_ref[h]
            dm = (dl * decay).astype(BF16)
            dqk = (da * decay).astype(BF16)
            kbf = k.astype(BF16)
            dkb = _bdot("gcj,gjd->gcd", dm, kbf) + dkbg * eg
            dk = (_bdot("gcj,gcd->gjd", dm, kb.astype(BF16)) + _bdot("gcj,gcd->gjd", dqk, q.astype(BF16))
                  + dk1_ref[:, sl].reshape(nc, CHUNK, 128) + dkb * beta)
            dq = _bdot("gcj,gjd->gcd", dqk, kbf) + dq1_ref[:, sl].reshape(nc, CHUNK, 128)
            e = dl * l_mat + da * a_mat
            dgc = (jnp.sum(e, axis=2, keepdims=True) - jnp.sum(jnp.swapaxes(e, 1, 2), axis=2, keepdims=True)
                   + jnp.sum(dkbg * kbg, axis=2, keepdims=True))
            dbeta = jnp.sum(dkb * k, axis=2, keepdims=True) + jnp.sum(dvb * v, axis=2, keepdims=True)
            o_ref[:, sl] = dq.reshape(R, 128)
            o_ref[:, 512 + h * 128:512 + (h + 1) * 128] = dk.reshape(R, 128)
            o_ref[:, 1024 + h * 128:1024 + (h + 1) * 128] = (dvb * beta).reshape(R, 128)
            dg_all = dg_all + jnp.where(lane == h, dgc.reshape(R, 1), 0.0)
            db_all = jnp.where(lane == 4 + h, dbeta.reshape(R, 1), db_all)
        t = _chunk_row(R)
        for s in (1, 2, 4, 8, 16, 32):
            dg_all = dg_all + jnp.where(t + s < CHUNK, pltpu.roll(dg_all, R - s, 0), 0.0)
        dgb_ref[...] = jnp.where(lane < 4, dg_all, db_all)

    rowb = lambda c, j: BS((R, c), lambda b, n: (b * nb + n, j))
    mat = BS((None, N_HEADS, nc, CHUNK, CHUNK), lambda b, n: (b, 0, n, 0, 0))
    return pl.pallas_call(
        body, name="gdn_chunk_bwd", grid=(B, nb),
        in_specs=[rowb(512, 0), rowb(512, 1), rowb(512, 2), rowb(128, 0),
                  BS((None, N_HEADS, nc, CHUNK), lambda b, n: (b, 0, n, 0)), mat, mat,
                  rowb(512, 0), rowb(512, 0), rowb(512, 0), rowb(512, 0), rowb(128, 0)],
        out_specs=[rowb(GDN_QKV, 0), rowb(128, 0)],
        out_shape=[jax.ShapeDtypeStruct((T, GDN_QKV), F32), jax.ShapeDtypeStruct((T, 128), F32)],
        compiler_params=_arb(2))(qkv, qkv, qkv, GB, Grow, Tinv, dA, dU, dW, dQ1, dK1, dG1)


def _gdn_out_norm(og, gg):
    outs, xhs, rs = [], [], []
    for h in range(N_HEADS):
        seg = og[:, h * 128:(h + 1) * 128]
        r = lax.rsqrt(jnp.mean(seg * seg, axis=-1, keepdims=True) + EPS)
        xh = seg * r
        outs.append(xh * gg)
        xhs.append(xh)
        rs.append(r)
    return outs, xhs, rs


def merge_fwd(o_mla, o_gdn, o_mem, P, x, tgt, w_out, g_gdn, g_fin, tm=256):
    T = x.shape[0]
    tm = min(tm, T)

    def body(om_ref, og_ref, oc_ref, gate_ref, x_ref, t_ref, w_ref, gg_ref, gf_ref, mix_ref, dx_ref, dxb_ref, sq_ref, gnf_ref):
        @pl.when(pl.program_id(0) == 0)
        def _():
            sq_ref[...] = jnp.zeros_like(sq_ref)
            gnf_ref[...] = jnp.zeros_like(gnf_ref)

        ogn, _, _ = _gdn_out_norm(og_ref[...], gg_ref[...])
        cat = jnp.concatenate([om_ref[...]] + ogn + [oc_ref[...]], axis=1)
        gt = gate_ref[...]
        mixed = (cat * (gt * _sigmoid(gt))).astype(BF16)
        mix_ref[...] = mixed
        x2 = x_ref[...] + _dot(mixed, w_ref[...], NN)
        r2 = lax.rsqrt(jnp.mean(x2 * x2, axis=-1, keepdims=True) + EPS)
        xh = x2 * r2
        gf = gf_ref[...]
        diff = xh * gf - t_ref[...]
        sq_ref[...] += jnp.sum(diff * diff, axis=0, keepdims=True)
        dy = diff * (1.0 / D_MODEL)
        gnf_ref[...] += jnp.sum(dy * xh, axis=0, keepdims=True)
        dxh = dy * gf
        dx = r2 * (dxh - xh * jnp.mean(dxh * xh, axis=-1, keepdims=True))
        dx_ref[...] = dx
        dxb_ref[...] = dx.astype(BF16)

    rowb = lambda c, j=0: BS((tm, c), lambda i: (i, j))
    full = lambda r, c: BS((r, c), lambda i: (0, 0))
    return pl.pallas_call(
        body, name="merge_fwd", grid=(T // tm,),
        in_specs=[rowb(512), rowb(512), rowb(512), rowb(D_MIX, OFF_GATE // D_MIX), rowb(D_MODEL), rowb(D_MODEL),
                  full(D_MIX, D_MODEL), full(1, 128), full(1, D_MODEL)],
        out_specs=[rowb(D_MIX), rowb(D_MODEL), rowb(D_MODEL), full(1, D_MODEL), full(1, D_MODEL)],
        out_shape=[jax.ShapeDtypeStruct((T, D_MIX), BF16), jax.ShapeDtypeStruct((T, D_MODEL), F32),
                   jax.ShapeDtypeStruct((T, D_MODEL), BF16),
                   jax.ShapeDtypeStruct((1, D_MODEL), F32), jax.ShapeDtypeStruct((1, D_MODEL), F32)],
        compiler_params=_arb(1))(o_mla, o_gdn, o_mem, P, x, tgt, w_out, g_gdn, g_fin)


def merge_bwd(dx2, o_mla, o_gdn, o_mem, P, w_out, g_gdn, tm=256):
    T = dx2.shape[0]
    tm = min(tm, T)

    def body(dx_ref, om_ref, og_ref, oc_ref, gate_ref, w_ref, gg_ref, dgate_ref, dom_ref, dog_ref, doc_ref, ggn_ref):
        @pl.when(pl.program_id(0) == 0)
        def _():
            ggn_ref[...] = jnp.zeros_like(ggn_ref)

        gg = gg_ref[...]
        dmix = _dot(dx_ref[...].astype(BF16), w_ref[...], NT)
        ogn, xhs, rs = _gdn_out_norm(og_ref[...], gg)
        cat = jnp.concatenate([om_ref[...]] + ogn + [oc_ref[...]], axis=1)
        gt = gate_ref[...]
        sg = _sigmoid(gt)
        dgate_ref[...] = (dmix * cat * (sg * (1.0 + gt * (1.0 - sg)))).astype(BF16)
        dcat = dmix * (gt * sg)
        dom_ref[...] = dcat[:, :512]
        doc_ref[...] = dcat[:, 1024:]
        acc = jnp.zeros((1, 128), F32)
        for h in range(N_HEADS):
            dseg = dcat[:, 512 + h * 128:512 + (h + 1) * 128]
            acc = acc + jnp.sum(dseg * xhs[h], axis=0, keepdims=True)
            dxh = dseg * gg
            dog_ref[:, h * 128:(h + 1) * 128] = rs[h] * (dxh - xhs[h] * jnp.mean(dxh * xhs[h], axis=-1, keepdims=True))
        ggn_ref[...] += acc

    rowb = lambda c, j=0: BS((tm, c), lambda i: (i, j))
    full = lambda r, c: BS((r, c), lambda i: (0, 0))
    return pl.pallas_call(
        body, name="merge_bwd", grid=(T // tm,),
        in_specs=[rowb(D_MODEL), rowb(512), rowb(512), rowb(512), rowb(D_MIX, OFF_GATE // D_MIX),
                  full(D_MIX, D_MODEL), full(1, 128)],
        out_specs=[rowb(D_MIX), rowb(512), rowb(512), rowb(512), full(1, 128)],
        out_shape=[jax.ShapeDtypeStruct((T, D_MIX), BF16)] + [jax.ShapeDtypeStruct((T, 512), F32)] * 3
        + [jax.ShapeDtypeStruct((1, 128), F32)],
        compiler_params=_arb(1))(dx2, o_mla, o_gdn, o_mem, P, w_out, g_gdn)


def in_proj_bwd(dP, wp, x, dx2, gain, after, tm=512):
    T, n = x.shape
    tm = min(tm, T)
    k = len(dP)
    widths = [p.shape[1] for p in dP]
    offs = [sum(widths[:i]) for i in range(k)]

    def body(*refs):
        w_ref, x_ref, dx2_ref, g_ref = refs[k:k + 4]
        o_ref, acc_ref = refs[-2:]

        @pl.when(pl.program_id(0) == 0)
        def _():
            acc_ref[...] = jnp.zeros_like(acc_ref)

        dy = None
        for a_ref, off, w in zip(refs[:k], offs, widths):
            d = _dot(a_ref[...], w_ref[off:off + w, :], NN)
            dy = d if dy is None else dy + d
        xv = x_ref[...]
        r = lax.rsqrt(jnp.mean(xv * xv, axis=-1, keepdims=True) + EPS)
        xh = xv * r
        acc_ref[...] += jnp.sum(dy * xh, axis=0, keepdims=True)
        dxh = dy * g_ref[...]
        o_ref[...] = dx2_ref[...] + r * (dxh - xh * jnp.mean(dxh * xh, axis=-1, keepdims=True))

    rowb = BS((tm, n), lambda i: (i, 0))
    full = BS((1, n), lambda i: (0, 0))
    return pl.pallas_call(
        body, name="in_proj_bwd", grid=(T // tm,),
        in_specs=[BS((tm, w), lambda i: (i, 0)) for w in widths]
        + [BS(wp.shape, lambda i: (0, 0), pipeline_mode=pl.Buffered(1)), rowb, rowb, full, BS(memory_space=pl.ANY)],
        out_specs=[rowb, full], out_shape=[jax.ShapeDtypeStruct((T, n), F32), jax.ShapeDtypeStruct((1, n), F32)],
        compiler_params=_arb(1))(*dP, wp, x, dx2, gain, after)


W_IN_SHARD = D_IN // 4
_GDN0 = Q_LORA + KV_LORA + MLA_ROPE
_AB0 = _GDN0 + GDN_QKV
_MEMQ0 = _AB0 + 2 * N_HEADS
_GATE0 = _MEMQ0 + N_HEADS * MEM_DH


def _w_in_row_map():
    a, m, gt = _AB0 - 2 * W_IN_SHARD, _MEMQ0 - 2 * W_IN_SHARD, _GATE0 - 2 * W_IN_SHARD
    e0 = OFF_GDN + W_IN_SHARD - _GDN0
    e1 = e0 + W_IN_SHARD
    e2 = OFF_GATE + W_IN_SHARD - gt
    return [(0, 0, 0, 672), (0, 672, 704, 32), (2, a, 768, m - a), (2, m, OFF_MEMQ, gt - m), (0, _GDN0, OFF_GDN, W_IN_SHARD - _GDN0),
            (1, 0, e0, W_IN_SHARD), (2, 0, e1, a), (2, gt, OFF_GATE, W_IN_SHARD - gt), (3, 0, e2, W_IN_SHARD)]


_W_IN_ZERO_ROWS = [(672, 32), (736, 32), (776, 248)]
W_IN_LANES = 256


def pad_w_in_t(shards):
    per_half = shards.shape[3] // W_IN_LANES

    def body(s_ref, o_ref):
        for r0, n in _W_IN_ZERO_ROWS:
            o_ref[r0:r0 + n, :] = jnp.zeros((n, W_IN_LANES), o_ref.dtype)
        for q, src, dst, n in _w_in_row_map():
            o_ref[dst:dst + n, :] = s_ref[q, src:src + n, :]

    return pl.pallas_call(
        body, name="pad_w_in_t", grid=(D_MODEL // W_IN_LANES,),
        in_specs=[BS((N_CHIPS, None, W_IN_SHARD, W_IN_LANES), lambda j: (0, j // per_half, 0, j % per_half))],
        out_specs=BS((N_PAD, W_IN_LANES), lambda j: (0, j)),
        out_shape=jax.ShapeDtypeStruct((N_PAD, D_MODEL), shards.dtype), compiler_params=_arb(1))(shards)


def unpad_w_in_t(g):
    def body(g_ref, o_ref):
        for q, src, dst, n in _w_in_row_map():
            o_ref[q, src:src + n, :] = g_ref[dst:dst + n, :]

    return pl.pallas_call(
        body, name="unpad_w_in_t", grid=(D_MODEL // W_IN_LANES,),
        in_specs=[BS((N_PAD, W_IN_LANES), lambda j: (0, j))], out_specs=BS((N_CHIPS, W_IN_SHARD, W_IN_LANES), lambda j: (0, 0, j)),
        out_shape=jax.ShapeDtypeStruct((N_CHIPS, W_IN_SHARD, D_MODEL), g.dtype), compiler_params=_arb(1))(g)


def _pad_w_q_b_t(s):
    z = jnp.zeros((32, s.shape[2]), s.dtype)
    parts = []
    for h in range(N_HEADS):
        parts += [s[h, :128], s[h, 128:160], z, s[h, 160:192], z]
    return jnp.concatenate(parts, axis=0)


def _unpad_w_q_b_t(g):
    return jnp.stack([jnp.concatenate([g[h * HEAD_PAD:h * HEAD_PAD + 128], g[h * HEAD_PAD + 128:h * HEAD_PAD + 160],
                                       g[h * HEAD_PAD + 192:h * HEAD_PAD + 224]]) for h in range(N_HEADS)])


def _perm_w_kv_b(s):
    return jnp.concatenate([s[h, :, :128] for h in range(N_HEADS)] + [s[h, :, 128:] for h in range(N_HEADS)], axis=1)


def _unperm_w_kv_b(g):
    return jnp.stack([jnp.concatenate([g[:, h * 128:(h + 1) * 128], g[:, 512 + h * 128:512 + (h + 1) * 128]], axis=1)
                      for h in range(N_HEADS)])


def _lane_row(v4):
    return jnp.pad(v4.reshape(1, -1).astype(F32), ((0, 0), (0, 128 - v4.size)))


def _pack(pieces, n_rows):
    flat = jnp.concatenate([p.reshape(-1) for p in pieces])
    return jnp.pad(flat, (0, n_rows * 1024 - flat.size)).reshape(n_rows, 1024)


def _unpack(block, shapes):
    flat = block.reshape(-1)
    out, off = [], 0
    for shp in shapes:
        n = int(np.prod(shp))
        out.append(flat[off:off + n].reshape(shp))
        off += n
    return out


N_CHIPS = 4
MESH = pl.DeviceIdType.MESH
ANY = BS(memory_space=pl.ANY)


def _place():
    return lax.axis_index("x"), lax.axis_index("y"), lax.axis_index("c")


def _other_chips(x, y):
    return [(1 - x, y), (x, 1 - y), (1 - x, 1 - y)]


def _half(split, which):
    axis, size = split
    ds = pl.ds(pl.multiple_of(which * size, 16 if axis == 0 else 128), size)
    return (ds, slice(None)) if axis == 0 else (slice(None), ds)


SEM = BS(memory_space=pltpu.SEMAPHORE)
HBM = BS(memory_space=pltpu.HBM)
_IN_HBM = lambda a: pltpu.with_memory_space_constraint(a, pltpu.HBM)
_SIDE_EFFECT = pltpu.SideEffectType.DATAFLOW_SIDE_EFFECTING


def _late_gather_copies(s_refs, l_refs, send_sems, recv_sems, local_sems, with_arrivals):
    x, y, c = _place()
    sends, recvs, locals_ = [], [], []
    for i, (s_ref, l_ref) in enumerate(zip(s_refs, l_refs)):
        locals_.append(pltpu.make_async_copy(s_ref, l_ref.at[2 * x + y], local_sems.at[i]))
        for j, (px, py) in enumerate(_other_chips(x, y)):
            k = 3 * i + j
            sends.append(pltpu.make_async_remote_copy(src_ref=s_ref, dst_ref=l_ref.at[2 * x + y], send_sem=send_sems.at[k],
                                                      recv_sem=recv_sems.at[k], device_id=(px, py, c), device_id_type=MESH))
            if with_arrivals:
                recvs.append(pltpu.make_async_remote_copy(src_ref=s_ref, dst_ref=l_ref.at[2 * px + py], send_sem=send_sems.at[k],
                                                          recv_sem=recv_sems.at[k], device_id=(px, py, c), device_id_type=MESH))
    return sends, recvs, locals_


def late_gather_start(shards, after, name):
    n = len(shards)

    def body(*refs):
        s_refs, l_refs = refs[:n], refs[n:2 * n]
        send_sems, recv_sems, local_sems = refs[2 * n + 1:2 * n + 4]
        token = refs[-1]
        sends, _, locals_ = _late_gather_copies(s_refs, l_refs, send_sems, recv_sems, local_sems, False)
        for cp in locals_ + sends:
            cp.start()
        token[...] = jnp.zeros_like(token)

    lands = [lax.empty((N_CHIPS,) + s.shape, s.dtype) for s in shards]
    hbm_like = lambda a: pltpu.HBM(a.shape, a.dtype)
    out = pl.pallas_call(
        body, name=name,
        out_shape=[pltpu.SemaphoreType.DMA((3 * n,)), pltpu.SemaphoreType.DMA((3 * n,)), pltpu.SemaphoreType.DMA((n,))]
        + [hbm_like(s) for s in shards] + [hbm_like(l) for l in lands] + [jax.ShapeDtypeStruct((8, 128), F32)],
        in_specs=[HBM] * (2 * n) + [BS(memory_space=pl.ANY)], out_specs=[SEM] * 3 + [HBM] * (2 * n) + [BS(memory_space=pltpu.VMEM)],
        input_output_aliases={i: 3 + i for i in range(2 * n)},
        compiler_params=pltpu.CompilerParams(has_side_effects=_SIDE_EFFECT))(
            *[_IN_HBM(s) for s in shards], *[_IN_HBM(l) for l in lands], after)
    return out[:3], out[3:3 + n], out[3 + n:3 + 2 * n], out[-1]


def late_gather_wait(sems, shards, lands, after, name):
    n = len(shards)

    def body(*refs):
        s_refs, l_refs = refs[:n], refs[n:2 * n]
        send_sems, recv_sems, local_sems = refs[2 * n:2 * n + 3]
        sends, recvs, locals_ = _late_gather_copies(s_refs, l_refs, send_sems, recv_sems, local_sems, True)
        for cp in locals_:
            cp.wait()
        for cp in sends:
            cp.wait_send()
        for cp in recvs:
            cp.wait_recv()

    hbm_like = lambda a: pltpu.HBM(a.shape, a.dtype)
    out = pl.pallas_call(
        body, name=name, out_shape=[hbm_like(s) for s in shards] + [hbm_like(l) for l in lands],
        in_specs=[HBM] * (2 * n) + [SEM] * 3 + [BS(memory_space=pl.ANY)], out_specs=[HBM] * (2 * n),
        input_output_aliases={i: i for i in range(2 * n)},
        compiler_params=pltpu.CompilerParams(has_side_effects=_SIDE_EFFECT))(*shards, *lands, *sems, after)
    return out[n:]


def _half_gather_copies(s_ref, l_ref, send_sems, recv_sems, with_arrivals):
    x, y, c = _place()
    sends, recvs = [], []
    for j, (px, py) in enumerate(_other_chips(x, y)):
        sends.append(pltpu.make_async_remote_copy(src_ref=s_ref.at[c], dst_ref=l_ref.at[2 * x + y, c], send_sem=send_sems.at[j],
                                                  recv_sem=recv_sems.at[j], device_id=(px, py, c), device_id_type=MESH))
        if with_arrivals:
            recvs.append(pltpu.make_async_remote_copy(src_ref=s_ref.at[c], dst_ref=l_ref.at[2 * px + py, c], send_sem=send_sems.at[j],
                                                      recv_sem=recv_sems.at[j], device_id=(px, py, c), device_id_type=MESH))
    return sends, recvs


def half_gather_start(shard, name):
    def body(s_ref, l_ref, send_sems, recv_sems, local_sem, s_thru, l_thru, token):
        x, y, _ = _place()
        pltpu.make_async_copy(s_ref, l_ref.at[2 * x + y], local_sem.at[0]).start()
        for cp in _half_gather_copies(s_ref, l_ref, send_sems, recv_sems, False)[0]:
            cp.start()
        token[...] = jnp.zeros_like(token)

    land = lax.empty((N_CHIPS,) + shard.shape, shard.dtype)
    out = pl.pallas_call(
        body, name=name,
        out_shape=[pltpu.SemaphoreType.DMA((3,)), pltpu.SemaphoreType.DMA((3,)), pltpu.SemaphoreType.DMA((1,)),
                   pltpu.HBM(shard.shape, shard.dtype), pltpu.HBM(land.shape, land.dtype), jax.ShapeDtypeStruct((8, 128), F32)],
        in_specs=[HBM, HBM], out_specs=[SEM] * 3 + [HBM, HBM, BS(memory_space=pltpu.VMEM)],
        input_output_aliases={0: 3, 1: 4},
        compiler_params=pltpu.CompilerParams(has_side_effects=_SIDE_EFFECT))(_IN_HBM(shard), _IN_HBM(land))
    return out[:3], out[3], out[4], out[5]


def half_gather_wait(sems, shard, land, after, name):
    def body(s_ref, l_ref, send_sems, recv_sems, local_sem, *rest):
        x, y, _ = _place()
        pltpu.make_async_copy(s_ref, l_ref.at[2 * x + y], local_sem.at[0]).wait()
        sends, recvs = _half_gather_copies(s_ref, l_ref, send_sems, recv_sems, True)
        for cp in sends:
            cp.wait_send()
        for cp in recvs:
            cp.wait_recv()

    out = pl.pallas_call(
        body, name=name, out_shape=[pltpu.HBM(shard.shape, shard.dtype), pltpu.HBM(land.shape, land.dtype)],
        in_specs=[HBM, HBM] + [SEM] * 3 + [BS(memory_space=pl.ANY)] * len(after), out_specs=[HBM, HBM],
        input_output_aliases={0: 0, 1: 1},
        compiler_params=pltpu.CompilerParams(has_side_effects=_SIDE_EFFECT))(shard, land, *sems, *after)
    return out[1]


def pass_halves_to_sibling(land, name):
    def body(l_in, l_ref, send_sems, recv_sems):
        x, y, c = _place()
        copies = []
        for j, (px, py) in enumerate(_other_chips(x, y)):
            q = 2 * px + py
            give = pltpu.make_async_remote_copy(src_ref=l_ref.at[q, c], dst_ref=l_ref.at[q, c], send_sem=send_sems.at[j],
                                                recv_sem=recv_sems.at[j], device_id=(x, y, 1 - c), device_id_type=MESH)
            take = pltpu.make_async_remote_copy(src_ref=l_ref.at[q, c], dst_ref=l_ref.at[q, 1 - c], send_sem=send_sems.at[j],
                                                recv_sem=recv_sems.at[j], device_id=(x, y, 1 - c), device_id_type=MESH)
            give.start()
            copies.append((give, take))
        for give, take in copies:
            take.wait_recv()
            give.wait_send()

    return pl.pallas_call(
        body, name=name, in_specs=[ANY], out_specs=ANY, out_shape=jax.ShapeDtypeStruct(land.shape, land.dtype),
        input_output_aliases={0: 0},
        scratch_shapes=[pltpu.SemaphoreType.DMA((3,)), pltpu.SemaphoreType.DMA((3,))])(land)


def allgather_devices(block, name):
    R, C = block.shape

    def body(b_ref, o_ref, send_sems, recv_sems, local_sem):
        x, y, c = _place()
        me = 4 * x + 2 * y + c
        own = pltpu.make_async_copy(b_ref, o_ref.at[me], local_sem)
        own.start()
        copies = []
        for r in range(1, 8):
            px = 1 - x if r & 4 else x
            py = 1 - y if r & 2 else y
            pc = 1 - c if r & 1 else c
            send = pltpu.make_async_remote_copy(src_ref=b_ref, dst_ref=o_ref.at[me], send_sem=send_sems.at[r - 1],
                                                recv_sem=recv_sems.at[r - 1], device_id=(px, py, pc), device_id_type=MESH)
            recv = pltpu.make_async_remote_copy(src_ref=b_ref, dst_ref=o_ref.at[4 * px + 2 * py + pc], send_sem=send_sems.at[r - 1],
                                                recv_sem=recv_sems.at[r - 1], device_id=(px, py, pc), device_id_type=MESH)
            send.start()
            copies.append((send, recv))
        for send, recv in copies:
            recv.wait_recv()
            send.wait_send()
        own.wait()

    return pl.pallas_call(
        body, name=name, in_specs=[ANY], out_specs=ANY, out_shape=jax.ShapeDtypeStruct((8, R, C), block.dtype),
        scratch_shapes=[pltpu.SemaphoreType.DMA((7,)), pltpu.SemaphoreType.DMA((7,)), pltpu.SemaphoreType.DMA(())])(block)


def swap_sibling(arrs, name, splits=None):
    n = len(arrs)

    def sent(a_ref, i, c):
        return a_ref if splits is None else a_ref.at[(slice(None),) + _half(splits[i], 1 - c)]

    def out_shape(a, i):
        if splits is None:
            return a.shape
        axis, size = splits[i]
        return (a.shape[0], size, a.shape[2]) if axis == 0 else (a.shape[0], a.shape[1], size)

    def body(*refs):
        a_refs, o_refs = refs[:n], refs[n:2 * n]
        send_sems, recv_sems = refs[2 * n:]
        x, y, c = _place()
        copies = [pltpu.make_async_remote_copy(src_ref=sent(a_ref, i, c), dst_ref=o_ref, send_sem=send_sems.at[i],
                                               recv_sem=recv_sems.at[i], device_id=(x, y, 1 - c), device_id_type=MESH)
                  for i, (a_ref, o_ref) in enumerate(zip(a_refs, o_refs))]
        for cp in copies:
            cp.start()
        for cp in copies:
            cp.wait()

    return pl.pallas_call(
        body, name=name, in_specs=[ANY] * n, out_specs=[ANY] * n,
        out_shape=[jax.ShapeDtypeStruct(out_shape(a, i), a.dtype) for i, a in enumerate(arrs)],
        scratch_shapes=[pltpu.SemaphoreType.DMA((n,)), pltpu.SemaphoreType.DMA((n,))])(*arrs)


def _exchange_copies(p_refs, l_refs, send_sems, recv_sems):
    x, y, c = _place()
    return [pltpu.make_async_remote_copy(src_ref=p_ref.at[2 * px + py], dst_ref=l_ref.at[j], send_sem=send_sems.at[3 * i + j],
                                         recv_sem=recv_sems.at[3 * i + j], device_id=(px, py, c), device_id_type=MESH)
            for i, (p_ref, l_ref) in enumerate(zip(p_refs, l_refs)) for j, (px, py) in enumerate(_other_chips(x, y))]


def exchange_chips_start(parts, name):
    n = len(parts)

    def body(*refs):
        send_sems, recv_sems = refs[2 * n:2 * n + 2]
        for cp in _exchange_copies(refs[:n], refs[n:2 * n], send_sems, recv_sems):
            cp.start()
        refs[-1][...] = jnp.zeros_like(refs[-1])

    lands = [lax.empty((3,) + p.shape[1:], p.dtype) for p in parts]
    hbm_like = lambda a: pltpu.HBM(a.shape, a.dtype)
    out = pl.pallas_call(
        body, name=name,
        out_shape=[pltpu.SemaphoreType.DMA((3 * n,)), pltpu.SemaphoreType.DMA((3 * n,))]
        + [hbm_like(p) for p in parts] + [hbm_like(l) for l in lands] + [jax.ShapeDtypeStruct((8, 128), F32)],
        in_specs=[HBM] * (2 * n), out_specs=[SEM] * 2 + [HBM] * (2 * n) + [BS(memory_space=pltpu.VMEM)],
        input_output_aliases={i: 2 + i for i in range(2 * n)},
        compiler_params=pltpu.CompilerParams(has_side_effects=_SIDE_EFFECT))(*[_IN_HBM(p) for p in parts], *[_IN_HBM(l) for l in lands])
    return out[:2], out[2:2 + n], out[2 + n:2 + 2 * n], out[-1]


def exchange_chips_wait(sems, parts, lands, after, name):
    n = len(parts)

    def body(*refs):
        send_sems, recv_sems = refs[2 * n:2 * n + 2]
        for cp in _exchange_copies(refs[:n], refs[n:2 * n], send_sems, recv_sems):
            cp.wait_send()
            cp.wait_recv()

    hbm_like = lambda a: pltpu.HBM(a.shape, a.dtype)
    out = pl.pallas_call(
        body, name=name, out_shape=[hbm_like(p) for p in parts] + [hbm_like(l) for l in lands],
        in_specs=[HBM] * (2 * n) + [SEM] * 2 + [BS(memory_space=pl.ANY)], out_specs=[HBM] * (2 * n),
        input_output_aliases={i: i for i in range(2 * n)},
        compiler_params=pltpu.CompilerParams(has_side_effects=_SIDE_EFFECT))(*parts, *lands, *sems, after)
    return out[n:]


def _half_block(shape2, split):
    axis, size = split
    return (size, shape2[1]) if axis == 0 else (shape2[0], size)


def add_pairs(parts, halves, splits, core, name):
    n = len(parts)

    def body(s_ref, *refs):
        for a_ref, b_ref, o_ref in zip(refs[:n], refs[n:2 * n], refs[2 * n:]):
            o_ref[...] = (a_ref[...].astype(F32) + b_ref[...].astype(F32)).astype(BF16)

    def mine(i):
        blk = (None,) + _half_block(parts[i].shape[1:], splits[i])
        if splits[i][0] == 0:
            return BS(blk, lambda q, s: (q, s[0], 0))
        return BS(blk, lambda q, s: (q, 0, s[0]))

    half_specs = [BS((None,) + h.shape[1:], lambda q, s: (q, 0, 0)) for h in halves]
    return pl.pallas_call(
        body, name=name,
        grid_spec=pltpu.PrefetchScalarGridSpec(num_scalar_prefetch=1, grid=(N_CHIPS,),
                                               in_specs=[mine(i) for i in range(n)] + half_specs, out_specs=half_specs),
        out_shape=[jax.ShapeDtypeStruct(h.shape, BF16) for h in halves], compiler_params=_arb(1))(core, *parts, *halves)


def add_fives(parts, halves, from_chips, splits, chip_core, name):
    n = len(parts)

    def body(s_ref, *refs):
        for a_ref, b_ref, p_ref, o_ref in zip(refs[:n], refs[n:2 * n], refs[2 * n:3 * n], refs[3 * n:]):
            s = a_ref[...].astype(F32) + b_ref[...].astype(F32)
            for j in range(3):
                s = s + p_ref[j].astype(F32)
            o_ref[...] = s

    def mine(i):
        blk = (None,) + _half_block(parts[i].shape[1:], splits[i])
        if splits[i][0] == 0:
            return BS(blk, lambda g, s: (s[0], s[1], 0))
        return BS(blk, lambda g, s: (s[0], 0, s[1]))

    half_specs = [BS((None,) + h.shape[1:], lambda g, s: (s[0], 0, 0)) for h in halves]
    chip_specs = [BS(p.shape, lambda g, s: (0, 0, 0)) for p in from_chips]
    out_specs = [BS(h.shape[1:], lambda g, s: (0, 0)) for h in halves]
    return pl.pallas_call(
        body, name=name,
        grid_spec=pltpu.PrefetchScalarGridSpec(num_scalar_prefetch=1, grid=(1,),
                                               in_specs=[mine(i) for i in range(n)] + half_specs + chip_specs, out_specs=out_specs),
        out_shape=[jax.ShapeDtypeStruct(h.shape[1:], F32) for h in halves], compiler_params=_arb(1))(chip_core, *parts, *halves, *from_chips)


def sum_leading(a, name):
    def body(a_ref, o_ref):
        s = a_ref[0]
        for j in range(1, a.shape[0]):
            s = s + a_ref[j]
        o_ref[...] = s

    return pl.pallas_call(body, name=name, out_shape=jax.ShapeDtypeStruct(a.shape[1:], a.dtype))(a)


def _adamw_math(w, g, m, v):
    mn = ADAM_B1 * m + (1.0 - ADAM_B1) * g
    vn = ADAM_B2 * v + (1.0 - ADAM_B2) * (g * g)
    m_hat = mn / (1.0 - ADAM_B1 ** ADAM_STEP)
    v_hat = vn / (1.0 - ADAM_B2 ** ADAM_STEP)
    return -ADAM_LR * (m_hat / (jnp.sqrt(v_hat) + ADAM_EPS) + ADAM_WD * w), mn, vn


def adamw(w, g, m, v, name):
    R, C = g.shape
    lead = (None,) * (w.ndim - 2)

    def body(w_ref, g_ref, m_ref, v_ref, d_ref, mo_ref, vo_ref):
        d_ref[...], mo_ref[...], vo_ref[...] = _adamw_math(w_ref[...], g_ref[...], m_ref[...], v_ref[...])

    wblk = BS(lead + (R, C), lambda i: (0,) * w.ndim)
    gblk = BS((R, C), lambda i: (0, 0))
    return pl.pallas_call(
        body, name=name, grid=(1,), in_specs=[wblk, gblk, wblk, wblk], out_specs=[wblk] * 3,
        out_shape=[jax.ShapeDtypeStruct(w.shape, F32)] * 3, compiler_params=_arb(1))(w, g, m, v)


def adamw_halves(w, mine, other, m, v, split, core, name):
    R, C = w.shape[-2:]
    axis, size = split
    lead = (None,) * (w.ndim - 2)
    zeros = (0,) * (w.ndim - 2)
    if axis == 0:
        tr = size if size <= 256 else next(t for t in range(256, 7, -1) if size % t == 0 and t % 8 == 0)
        nb = size // tr
        whole = BS(lead + (tr, C), lambda hi, j, s: zeros + (hi * nb + j, 0))
        part = BS((tr, C), lambda hi, j, s: (j, 0))
    else:
        nb = size // 128
        whole = BS(lead + (R, 128), lambda hi, j, s: zeros + (0, hi * nb + j))
        part = BS((R, 128), lambda hi, j, s: (0, j))

    def body(s_ref, w_ref, a_ref, b_ref, m_ref, v_ref, g_ref, d_ref, mo_ref, vo_ref):
        g = jnp.where(pl.program_id(0) == s_ref[0], a_ref[...], b_ref[...])
        g_ref[...] = g
        d_ref[...], mo_ref[...], vo_ref[...] = _adamw_math(w_ref[...], g, m_ref[...], v_ref[...])

    return pl.pallas_call(
        body, name=name,
        grid_spec=pltpu.PrefetchScalarGridSpec(num_scalar_prefetch=1, grid=(2, nb),
                                               in_specs=[whole, part, part, whole, whole], out_specs=[whole] * 4),
        out_shape=[jax.ShapeDtypeStruct(w.shape, F32)] * 4, compiler_params=_arb(2))(core, w, mine, other, m, v)


def dense_bf16(w3, name):
    R, _, K = w3.shape
    kh = K // 2

    def body(w_hbm, o_ref, buf, sem):
        cp = pltpu.make_async_copy(w_hbm.at[:, 0], buf, sem)
        cp.start()
        cp.wait()
        o_ref[0] = buf[:, :kh].astype(BF16)
        o_ref[1] = buf[:, kh:].astype(BF16)

    return pl.pallas_call(
        body, name=name, in_specs=[ANY], out_specs=BS(memory_space=pltpu.VMEM), out_shape=jax.ShapeDtypeStruct((2, R, kh), BF16),
        scratch_shapes=[pltpu.VMEM((R, K), F32), pltpu.SemaphoreType.DMA(())])(w3)


ROW_BLOCK = 184


def adamw_untiled_rows(w3, mine, other, m3, v3, name):
    R, _, K = w3.shape
    kh = K // 2
    starts = list(range(0, R, ROW_BLOCK))
    sizes = [min(ROW_BLOCK, R - s) for s in starts]
    nblk = len(starts)

    def body(w_hbm, a_ref, b_ref, m_hbm, v_hbm, g_hbm, d_hbm, mo_hbm, vo_hbm,
             wbuf, mbuf, vbuf, gbuf, dbuf, mobuf, vobuf, in_sems, out_sems):
        first = lax.axis_index("c") == 0
        ins = []
        for k, (r0, n) in enumerate(zip(starts, sizes)):
            rows = pl.ds(r0, n)
            cps = [pltpu.make_async_copy(src.at[rows, 0], dst.at[rows], in_sems.at[3 * k + i])
                   for i, (src, dst) in enumerate(((w_hbm, wbuf), (m_hbm, mbuf), (v_hbm, vbuf)))]
            for cp in cps:
                cp.start()
            ins.append(cps)

        def update(rows):
            a, b = a_ref[rows, :], b_ref[rows, :]
            g = jnp.concatenate([jnp.where(first, a, b), jnp.where(first, b, a)], axis=1)
            gbuf[rows, :] = g
            dbuf[rows, :], mobuf[rows, :], vobuf[rows, :] = _adamw_math(wbuf[rows, :], g, mbuf[rows, :], vbuf[rows, :])

        outs = []
        for k, (r0, n) in enumerate(zip(starts, sizes)):
            for cp in ins[k]:
                cp.wait()
            groups, tail = n // 8, n % 8

            def group(i, carry, r0=r0):
                update(pl.ds(pl.multiple_of(r0 + i * 8, 8), 8))
                return carry

            lax.fori_loop(0, groups, group, 0)
            if tail:
                update(pl.ds(r0 + groups * 8, tail))
            rows = pl.ds(r0, n)
            cps = [pltpu.make_async_copy(src.at[rows], dst.at[rows, 0], out_sems.at[4 * k + i])
                   for i, (src, dst) in enumerate(((gbuf, g_hbm), (dbuf, d_hbm), (mobuf, mo_hbm), (vobuf, vo_hbm)))]
            for cp in cps:
                cp.start()
            outs += cps
        for cp in outs:
            cp.wait()

    vmem = BS(memory_space=pltpu.VMEM)
    return pl.pallas_call(
        body, name=name, in_specs=[ANY, vmem, vmem, ANY, ANY], out_specs=[ANY] * 4,
        out_shape=[jax.ShapeDtypeStruct(w3.shape, F32)] * 4,
        scratch_shapes=[pltpu.VMEM((R, K), F32)] * 7 + [pltpu.SemaphoreType.DMA((3 * nblk,)), pltpu.SemaphoreType.DMA((4 * nblk,))])(
            w3, mine, other, m3, v3)


def adamw_w_q_b(w, mine, other, m, v, name):
    def body(w_ref, a_ref, b_ref, m_ref, v_ref, g_ref, d_ref, mo_ref, vo_ref):
        first = lax.axis_index("c") == 0
        lo = jnp.where(first, a_ref[...], b_ref[...])
        hi = jnp.where(first, b_ref[...], a_ref[...])
        g = jnp.concatenate([lo, hi[0:32], hi[64:96]], axis=0)
        g_ref[...] = g
        d_ref[...], mo_ref[...], vo_ref[...] = _adamw_math(w_ref[...], g, m_ref[...], v_ref[...])

    return pl.pallas_call(body, name=name, out_shape=[jax.ShapeDtypeStruct(w.shape, F32)] * 4)(w, mine, other, m, v)


def local_step(x, mem, positions, tgt, norm_in, weights, big_grads_ready, q_a_norm, kv_a_norm, gdn_conv, gdn_a_log,
               gdn_dt_bias, gdn_norm, mem_norm, norm_final):
    B, S, D = x.shape
    M = mem.shape[1]
    T = B * S
    N = S // CHUNK
    x2d = x.reshape(T, D)
    mem2d = mem.reshape(B * M, D)
    tgt2d = tgt.reshape(T, D)

    alog_row, dt_row = _lane_row(gdn_a_log), _lane_row(gdn_dt_bias)

    half = MLA_ROPE // 2
    inv_freq = 1.0 / (ROPE_THETA ** (jnp.arange(half, dtype=F32) / half))
    z32 = jnp.zeros((half,), F32)
    o32 = jnp.ones((half,), F32)
    inv_row = jnp.concatenate([inv_freq, z32, inv_freq, z32]).reshape(1, 128)
    sgn_row = jnp.concatenate([-o32, z32, o32, z32]).reshape(1, 128)
    msk_row = jnp.concatenate([o32, z32, o32, z32]).reshape(1, 128)
    cos_t, sin_t = rope_tables(positions.reshape(T, 1), inv_row, sgn_row, msk_row, after=weights[3])

    h = rms_fwd(x2d, norm_in, "rms_in", after=weights[3])
    wp, behind = weights[0]((h, cos_t))
    P = mm(h, wp, "nt", F32, "in_proj", bm=512, bn=1536, n_outer=True, after=behind)
    wq, wkv = weights[1](P)
    Q, K, V, qn, kvn = mla_prep(P, q_a_norm, kv_a_norm, wq, wkv, cos_t, sin_t)
    o_mla, lse = mla_attn_fwd(Q, K, V, B, S)
    qkv = gdn_prep_fwd(P, gdn_conv, B, S)
    GB = gdn_gate_fwd(P, alog_row, dt_row, B, S)
    Grow = jnp.transpose(GB[:, :N_HEADS].reshape(B, N, CHUNK, N_HEADS), (0, 3, 1, 2))
    U, W, Tinv, A = gdn_chunk_fwd(qkv, GB, Grow, B, S)
    qkv3, GB3 = qkv.reshape(B, S, GDN_QKV), GB.reshape(B, S, 128)
    W3 = W.reshape(B, S, 512)
    o_gdn3, Vn3, St = gdn_scan_fwd(qkv3, U.reshape(B, S, 512), W3, GB3, A, B, S)
    o_gdn = o_gdn3.reshape(T, 512)
    w_mem_kv, w_out = weights[2](o_gdn)
    memn = rms_fwd(mem2d, mem_norm, "rms_mem")
    MKV = mm(memn, w_mem_kv, "nn", BF16, "mem_kv_proj")
    o_mem = mem_attn_fwd(P, MKV, B, S, M)
    mixed, dx2, dx2b, sq, g_norm_final = merge_fwd(o_mla, o_gdn, o_mem, P, x2d, tgt2d, w_out, gdn_norm, norm_final.reshape(1, D))

    g_w_out = mm(mixed, dx2b, "tn", BF16, "grad_w_out")
    dgate, do_mla, do_gdn, do_mem, g_gdn_norm = merge_bwd(dx2b, o_mla, o_gdn, o_mem, P, w_out, gdn_norm)

    dmemq, dMKV = mem_attn_bwd(P, MKV, do_mem, B, S, M)
    g_w_mem_kv = mm(memn, dMKV, "tn", BF16, "grad_w_mem_kv")
    dmemn = mm(dMKV, w_mem_kv, "nt", F32, "d_memn")
    g_mem_norm = gain_grad(mem2d, dmemn, "grad_mem_norm")

    dU3, dW3, dQ13, dK13, dA, dG13 = gdn_scan_bwd(do_gdn.reshape(B, S, 512), qkv3, W3, Vn3, GB3, A, St, B, S)
    r2 = lambda a: a.reshape(T, a.shape[-1])
    dqkv, dGB = gdn_chunk_bwd(qkv, GB, Grow, Tinv, dA, r2(dU3), r2(dW3), r2(dQ13), r2(dK13), r2(dG13), B, S)
    dPg, g_conv = gdn_prep_bwd(P, dqkv, gdn_conv, B, S)
    dab, g_ab = gdn_gate_bwd(P, dGB, alog_row, dt_row, B, S)

    dQ, dK, dV = mla_attn_bwd(Q, K, V, o_mla, do_mla, lse, B, S)
    dq_lin, dkv_lin, dkr = mla_post_bwd(dQ, dK, dV, cos_t, sin_t)
    dqn = mm(dq_lin, wq, "nn", F32, "d_qn")
    dkvn = mm(dkv_lin, wkv, "nt", F32, "d_kvn")
    g_wq = mm(dq_lin, qn, "tn", BF16, "grad_w_q_b")
    g_wkv = mm(kvn, dkv_lin, "tn", BF16, "grad_w_kv_b")
    dPm, g_q_a_norm, g_kv_a_norm = mla_norm_bwd(P, dqn, dkvn, dkr, dab, q_a_norm, kv_a_norm)

    dP = [dPm, dmemq, dPg, dgate]
    g_wp = mm_cols_tn(dP, h, BF16, "grad_w_in")
    started = big_grads_ready(dict(w_in=g_wp, w_q_b=g_wq, w_kv_b=g_wkv, w_mem_kv=g_w_mem_kv, w_out=g_w_out))
    grad_x, g_norm_in = in_proj_bwd(dP, wp, x2d, dx2, norm_in, started)

    grads = dict(
        norm_in=g_norm_in, q_a_norm=g_q_a_norm, kv_a_norm=g_kv_a_norm, gdn_conv=g_conv,
        gdn_a_log=g_ab[0:1, :N_HEADS], gdn_dt_bias=g_ab[1:2, :N_HEADS], gdn_norm=g_gdn_norm,
        mem_norm=g_mem_norm, norm_final=g_norm_final)
    return sq, grad_x.reshape(B, S, D), grads


def kernel(x, mem, positions, norm_in, w_in, q_a_norm, w_q_b, kv_a_norm, w_kv_b, gdn_conv, gdn_a_log, gdn_dt_bias, gdn_norm, mem_norm, w_mem_kv, w_out, norm_final, loss_target, m_norm_in, m_w_in, m_q_a_norm, m_w_q_b, m_kv_a_norm, m_w_kv_b, m_gdn_conv, m_gdn_a_log, m_gdn_dt_bias, m_gdn_norm, m_mem_norm, m_w_mem_kv, m_w_out, m_norm_final, v_norm_in, v_w_in, v_q_a_norm, v_w_q_b, v_kv_a_norm, v_w_kv_b, v_gdn_conv, v_gdn_a_log, v_gdn_dt_bias, v_gdn_norm, v_mem_norm, v_w_mem_kv, v_w_out, v_norm_final):
    B = x.shape[0]
    cx, cy, cc = lax.axis_index("x"), lax.axis_index("y"), lax.axis_index("c")
    chip = 2 * cx + cy

    big_names = ("w_in", "w_q_b", "w_kv_b", "w_mem_kv", "w_out")
    rows_major = lambda a: jnp.transpose(a, (2, 0, 1))
    w_in3, m_in3, v_in3 = rows_major(w_in), rows_major(m_w_in), rows_major(v_w_in)
    w_qb_t, m_qb_t, v_qb_t = jnp.transpose(w_q_b[0]), jnp.transpose(m_w_q_b[0]), jnp.transpose(v_w_q_b[0])
    z32 = jnp.zeros((32, Q_LORA), BF16)
    qb_bf = w_qb_t.astype(BF16)
    qb_padded = jnp.concatenate([qb_bf[:160], z32, qb_bf[160:], z32])
    shards = [dense_bf16(w_in3, "w_in_bf16"), qb_padded, w_kv_b[0].astype(BF16), w_mem_kv[0].astype(BF16), w_out[0].astype(BF16)]
    splits = [(1, D_MODEL // 2)] + [(0, s.shape[0] // 2) for s in shards[1:]]
    *w_in_flight, w_in_started = half_gather_start(shards[0], "w_in_gather_start")
    conv_all = allgather_devices(gdn_conv[0], "allgather_conv")
    conv_cols = gdn_conv.shape[2]
    conv_full = jnp.transpose(conv_all[0::2], (1, 0, 2)).reshape(GDN_CONV, N_CHIPS * conv_cols)
    late_shapes = [(N_CHIPS,) + s.shape for s in shards[1:]]
    late = {}

    def w_in_ready(after):
        g_in = pass_halves_to_sibling(half_gather_wait(*w_in_flight, after, "w_in_gather_wait"), "w_in_gather_sibling")
        *late["a"], started_a = late_gather_start(shards[1:3], g_in, "late_gather_qkv_start")
        *late["b"], started_b = late_gather_start(shards[3:], started_a, "late_gather_mem_out_start")
        return pad_w_in_t(g_in), (started_a, started_b)

    def late_qkv(after):
        g_qb, g_kvb = late_gather_wait(*late["a"], after, "late_gather_qkv_wait")
        return g_qb.reshape(-1, Q_LORA), _perm_w_kv_b(g_kvb)

    def late_mem_out(after):
        g_mem, g_out_w = late_gather_wait(*late["b"], after, "late_gather_mem_out_wait")
        return g_mem.reshape(-1, g_mem.shape[2]), g_out_w.reshape(-1, g_out_w.shape[2])

    weights = (w_in_ready, late_qkv, late_mem_out, (w_in_started,))

    core = jnp.stack([cc]).astype(jnp.int32)
    chip_core = jnp.stack([chip, cc]).astype(jnp.int32)
    exchange = {}

    def big_grads_ready(gb):
        parts = [unpad_w_in_t(gb["w_in"]), gb["w_q_b"].reshape(late_shapes[0]), _unperm_w_kv_b(gb["w_kv_b"]),
                 gb["w_mem_kv"].reshape(late_shapes[2]), gb["w_out"].reshape(late_shapes[3])]
        from_sibling = swap_sibling(parts, "rs_sibling_partial", splits)
        chip_sums = add_pairs(parts, from_sibling, splits, core, "rs_add_sibling")
        sems, sums_thru, lands, token = exchange_chips_start(chip_sums, "rs_exchange_start")
        exchange.update(parts=parts, from_sibling=from_sibling, sems=sems, sums=sums_thru, lands=lands)
        return token

    sq, grad_x, g = local_step(x, mem, positions, loss_target, norm_in, weights, big_grads_ready, q_a_norm, kv_a_norm, conv_full,
                               gdn_a_log, gdn_dt_bias, gdn_norm, mem_norm, norm_final)

    small_names = ("norm_in", "q_a_norm", "kv_a_norm", "gdn_a_log", "gdn_dt_bias", "gdn_norm", "mem_norm", "norm_final")
    small = dict(norm_in=norm_in, q_a_norm=q_a_norm, kv_a_norm=kv_a_norm, gdn_a_log=gdn_a_log, gdn_dt_bias=gdn_dt_bias,
                 gdn_norm=gdn_norm, mem_norm=mem_norm, norm_final=norm_final)
    m_small = dict(norm_in=m_norm_in, q_a_norm=m_q_a_norm, kv_a_norm=m_kv_a_norm, gdn_a_log=m_gdn_a_log,
                   gdn_dt_bias=m_gdn_dt_bias, gdn_norm=m_gdn_norm, mem_norm=m_mem_norm, norm_final=m_norm_final)
    v_small = dict(norm_in=v_norm_in, q_a_norm=v_q_a_norm, kv_a_norm=v_kv_a_norm, gdn_a_log=v_gdn_a_log,
                   gdn_dt_bias=v_gdn_dt_bias, gdn_norm=v_gdn_norm, mem_norm=v_mem_norm, norm_final=v_norm_final)
    rows = lambda d: jnp.stack([jnp.pad(d[n].reshape(-1), (0, 1024 - d[n].size)) for n in small_names])
    conv_rows = GDN_CONV * GDN_QKV // 1024
    g_block = jnp.concatenate([rows(g), g["gdn_conv"].reshape(conv_rows, 1024), sq, jnp.zeros((16 - 9 - conv_rows, 1024), F32)])
    g_block = sum_leading(allgather_devices(g_block, "allgather_small_grads"), "sum_small_grads")
    g_small_rows = g_block[:8]
    loss = 0.5 * jnp.sum(g_block[8 + conv_rows]) / D_MODEL
    g_conv = lax.dynamic_slice_in_dim(g_block[8:8 + conv_rows].reshape(GDN_CONV, GDN_QKV), chip * conv_cols, conv_cols, axis=1)
    d_s, m_s, v_s = adamw(rows(small), g_small_rows, rows(m_small), rows(v_small), "adamw_small")
    unrow = lambda r: {n: r[i, :small[n].size].reshape(small[n].shape) for i, n in enumerate(small_names)}
    g_out, d_out, m_out, v_out = unrow(g_small_rows), unrow(d_s), unrow(m_s), unrow(v_s)

    from_chips = exchange_chips_wait(exchange["sems"], exchange["sums"], exchange["lands"], d_s, "rs_exchange_wait")
    my_half = add_fives(exchange["parts"], exchange["from_sibling"], from_chips, splits, chip_core, "rs_add_chips")
    other_half = swap_sibling(my_half, "rs_sibling_final")

    d_out["gdn_conv"], m_out["gdn_conv"], v_out["gdn_conv"] = adamw(gdn_conv, g_conv, m_gdn_conv, v_gdn_conv, "adamw_gdn_conv")
    g_out["gdn_conv"] = g_conv[None]
    res = adamw_untiled_rows(w_in3, my_half[0], other_half[0], m_in3, v_in3, "adamw_w_in")
    g_out["w_in"], d_out["w_in"], m_out["w_in"], v_out["w_in"] = [jnp.transpose(r, (1, 2, 0)) for r in res]
    res = adamw_w_q_b(w_qb_t, my_half[1], other_half[1], m_qb_t, v_qb_t, "adamw_w_q_b")
    g_out["w_q_b"], d_out["w_q_b"], m_out["w_q_b"], v_out["w_q_b"] = [jnp.transpose(r)[None] for r in res]
    rest = dict(w_kv_b=(w_kv_b, m_w_kv_b, v_w_kv_b), w_mem_kv=(w_mem_kv, m_w_mem_kv, v_w_mem_kv), w_out=(w_out, m_w_out, v_w_out))
    for i, n in enumerate(big_names):
        if n in rest:
            w_n, m_n, v_n = rest[n]
            g_out[n], d_out[n], m_out[n], v_out[n] = adamw_halves(w_n, my_half[i], other_half[i], m_n, v_n, splits[i], core, "adamw_" + n)

    order = ("norm_in", "w_in", "q_a_norm", "w_q_b", "kv_a_norm", "w_kv_b", "gdn_conv", "gdn_a_log", "gdn_dt_bias",
             "gdn_norm", "mem_norm", "w_mem_kv", "w_out", "norm_final")
    return (loss, grad_x, *[g_out[n] for n in order], *[d_out[n] for n in order], *[m_out[n] for n in order],
            *[v_out[n] for n in order])
```

```python
import functools
import math

import jax
import jax.numpy as jnp
import numpy as np
from jax import lax
from jax.experimental import pallas as pl
from jax.experimental.pallas import tpu as pltpu

F32 = jnp.float32
BF16 = jnp.bfloat16
BS = pl.BlockSpec

D_MODEL = 1024
N_HEADS = 4
MLA_NOPE, MLA_ROPE, MLA_V = 128, 64, 128
Q_LORA, KV_LORA = 384, 256
ROPE_THETA = 10000.0
GDN_DK = GDN_DV = 128
GDN_CONV = 4
CHUNK = 64
MEM_DH = 128
D_MIX = 1536
GDN_QKV = 1536
D_IN = 4296
EPS = 1e-6
ADAM_LR, ADAM_B1, ADAM_B2, ADAM_EPS, ADAM_WD, ADAM_STEP = 0.001, 0.9, 0.999, 1e-08, 0.01, 10

OFF_MLA = 0
OFF_MEMQ = 1024
OFF_GDN = 1536
OFF_GATE = 3072
N_PAD = 4608
HEAD_PAD = 256
MLA_SCALE = (MLA_NOPE + MLA_ROPE) ** -0.5
MEM_SCALE = MEM_DH ** -0.5
GDN_SCALE = GDN_DK ** -0.5
NEG = -1e30

NN = ((1,), (0,))
NT = ((1,), (1,))
TN = ((0,), (0,))


def _dot(a, b, dims):
    return lax.dot_general(a, b, (dims, ((), ())), preferred_element_type=F32)


def _bdot(spec, a, b, precision=None):
    return jnp.einsum(spec, a, b, preferred_element_type=F32, precision=precision)


def _arb(n):
    return pltpu.CompilerParams(dimension_semantics=("arbitrary",) * n)


def _sigmoid(x):
    return 1.0 / (1.0 + jnp.exp(-x))


def _softplus(z):
    return jnp.maximum(z, 0.0) + jnp.log(1.0 + jnp.exp(-jnp.abs(z)))


def _rope(t, cos_row, sin_row):
    return t * cos_row + pltpu.roll(t, 64, 1) * sin_row


def _rope_bwd(d, cos_row, sin_row):
    return d * cos_row + pltpu.roll(d * sin_row, 64, 1)


def rms_fwd(x, gain, name, tm=512, after=()):
    T, n = x.shape
    tm = min(tm, T)

    def body(x_ref, g_ref, *rest):
        xv = x_ref[...]
        r = lax.rsqrt(jnp.mean(xv * xv, axis=-1, keepdims=True) + EPS)
        rest[-1][...] = (xv * r * g_ref[...]).astype(BF16)

    return pl.pallas_call(
        body, name=name, grid=(T // tm,),
        in_specs=[BS((tm, n), lambda i: (i, 0)), BS((1, n), lambda i: (0, 0))] + [BS(memory_space=pl.ANY)] * len(after),
        out_specs=BS((tm, n), lambda i: (i, 0)),
        out_shape=jax.ShapeDtypeStruct((T, n), BF16), compiler_params=_arb(1))(x, gain, *after)


def mm(a, b, kind, out_dtype, name, bm=512, bn=None, n_outer=False, after=()):
    if kind == "nn":
        (M, K), (_, N) = a.shape, b.shape
    elif kind == "nt":
        (M, K), (N, _) = a.shape, b.shape
    else:
        (K, M), (_, N) = a.shape, b.shape
    bm, bn = min(bm, M), min(bn or N, N)
    assert M % bm == 0 and N % bn == 0, (name, M, N, K)
    ij = (lambda g0, g1: (g1, g0)) if n_outer else (lambda g0, g1: (g0, g1))
    a_spec = BS((K, bm), lambda g0, g1: (0, ij(g0, g1)[0])) if kind == "tn" else BS((bm, K), lambda g0, g1: (ij(g0, g1)[0], 0))
    once = dict(pipeline_mode=pl.Buffered(1)) if bn == N else {}
    b_spec = (BS((bn, K), lambda g0, g1: (ij(g0, g1)[1], 0), **once) if kind == "nt"
              else BS((K, bn), lambda g0, g1: (0, ij(g0, g1)[1]), **once))
    dims = {"nn": NN, "nt": NT, "tn": TN}[kind]

    def body(a_ref, b_ref, *rest):
        rest[-1][...] = _dot(a_ref[...].astype(BF16), b_ref[...].astype(BF16), dims).astype(out_dtype)

    grid = (N // bn, M // bm) if n_outer else (M // bm, N // bn)
    return pl.pallas_call(
        body, name=name, grid=grid, in_specs=[a_spec, b_spec] + [BS(memory_space=pl.ANY)] * len(after),
        out_specs=BS((bm, bn), lambda g0, g1: ij(g0, g1)),
        out_shape=jax.ShapeDtypeStruct((M, N), out_dtype), compiler_params=_arb(2))(a, b, *after)


def mm_cols_tn(pieces, b, out_dtype, name, bm=512):
    K, N = b.shape
    tiles = [p.shape[1] // bm for p in pieces]
    firsts = [sum(tiles[:i]) for i in range(len(tiles))]

    def body(*refs):
        b_ref, o_ref = refs[-2], refs[-1]
        i = pl.program_id(0)
        for a_ref, t0, n in zip(refs[:-2], firsts, tiles):
            @pl.when((i >= t0) & (i < t0 + n))
            def _(a_ref=a_ref):
                o_ref[...] = _dot(a_ref[...], b_ref[...], TN).astype(out_dtype)

    a_specs = [BS((K, bm), lambda i, t0=t0, n=n: (0, jnp.clip(i - t0, 0, n - 1))) for t0, n in zip(firsts, tiles)]
    return pl.pallas_call(
        body, name=name, grid=(sum(tiles),),
        in_specs=a_specs + [BS(b.shape, lambda i: (0, 0), pipeline_mode=pl.Buffered(1))],
        out_specs=BS((bm, N), lambda i: (i, 0)), out_shape=jax.ShapeDtypeStruct((sum(tiles) * bm, N), out_dtype),
        compiler_params=_arb(1))(*pieces, b)


def rope_tables(pos_col, inv_row, sgn_row, msk_row, tm=512, after=()):
    T = pos_col.shape[0]
    tm = min(tm, T)

    def body(p_ref, inv_ref, sgn_ref, msk_ref, *rest):
        c_ref, s_ref = rest[-2:]
        ang = p_ref[...].astype(F32) * inv_ref[...]
        c_ref[...] = jnp.cos(ang) * msk_ref[...]
        s_ref[...] = jnp.sin(ang) * sgn_ref[...]

    row = BS((1, 128), lambda i: (0, 0))
    return pl.pallas_call(
        body, name="rope_tables", grid=(T // tm,),
        in_specs=[BS((tm, 1), lambda i: (i, 0)), row, row, row] + [BS(memory_space=pl.ANY)] * len(after),
        out_specs=[BS((tm, 128), lambda i: (i, 0))] * 2,
        out_shape=[jax.ShapeDtypeStruct((T, 128), F32)] * 2, compiler_params=_arb(1))(pos_col, inv_row, sgn_row, msk_row, *after)


def mla_prep(P, gq, gkv, wq, wkv, cos_t, sin_t, tm=512):
    T = P.shape[0]
    tm = min(tm, T)

    def body(p_ref, gq_ref, gkv_ref, wq_ref, wkv_ref, c_ref, s_ref, q_ref, k_ref, v_ref, qn_ref, kvn_ref):
        p = p_ref[...]
        cq, ckv, kr = p[:, :Q_LORA], p[:, Q_LORA:Q_LORA + KV_LORA], p[:, 640:768]
        qn = (cq * lax.rsqrt(jnp.mean(cq * cq, axis=-1, keepdims=True) + EPS) * gq_ref[...]).astype(BF16)
        kvn = (ckv * lax.rsqrt(jnp.mean(ckv * ckv, axis=-1, keepdims=True) + EPS) * gkv_ref[...]).astype(BF16)
        qn_ref[...] = qn
        kvn_ref[...] = kvn
        q = _dot(qn, wq_ref[...], NT)
        kv = _dot(kvn, wkv_ref[...], NN)
        cos_row, sin_row = c_ref[...], s_ref[...]
        krr = _rope(kr, cos_row, sin_row).astype(BF16)
        for h in range(N_HEADS):
            lo = h * HEAD_PAD
            q_ref[:, lo:lo + 128] = (q[:, lo:lo + 128] * MLA_SCALE).astype(BF16)
            q_ref[:, lo + 128:lo + 256] = (_rope(q[:, lo + 128:lo + 256], cos_row, sin_row) * MLA_SCALE).astype(BF16)
            k_ref[:, lo:lo + 128] = kv[:, h * 128:(h + 1) * 128].astype(BF16)
            k_ref[:, lo + 128:lo + 256] = krr
            v_ref[:, lo:lo + 128] = kv[:, 512 + h * 128:512 + (h + 1) * 128].astype(BF16)
            v_ref[:, lo + 128:lo + 256] = jnp.ones((tm, 128), BF16)

    full = lambda r, c: BS((r, c), lambda i: (0, 0))
    rowb = lambda c: BS((tm, c), lambda i: (i, 0))
    return pl.pallas_call(
        body, name="mla_prep", grid=(T // tm,),
        in_specs=[rowb(1024), full(1, Q_LORA), full(1, KV_LORA), full(1024, Q_LORA), full(KV_LORA, 1024), rowb(128), rowb(128)],
        out_specs=[rowb(1024), rowb(1024), rowb(1024), rowb(Q_LORA), rowb(KV_LORA)],
        out_shape=[jax.ShapeDtypeStruct((T, 1024), BF16), jax.ShapeDtypeStruct((T, 1024), BF16),
                   jax.ShapeDtypeStruct((T, 1024), BF16), jax.ShapeDtypeStruct((T, Q_LORA), BF16),
                   jax.ShapeDtypeStruct((T, KV_LORA), BF16)],
        compiler_params=_arb(1))(P, gq, gkv, wq, wkv, cos_t, sin_t)


ATTN_HEADS_PER_STEP = 2
ATTN_STRIP = 32


def mla_attn_fwd(Q, K, V, B, S, tq=512, hp=ATTN_HEADS_PER_STEP):
    T = B * S
    tq = min(tq, S)
    nq = S // tq

    rs = min(ATTN_STRIP, tq)

    def body(q_ref, k_ref, v_ref, o_ref, lse_ref, m_s, acc_s, s_s, p_s, a_s):
        i = pl.program_id(2)
        m_s[...] = jnp.full_like(m_s, NEG)
        acc_s[...] = jnp.zeros_like(acc_s)

        def blk(j, masked):
            rows = pl.ds(pl.multiple_of(j * tq, tq), tq)
            for h in range(hp):
                hq = slice(h * HEAD_PAD, (h + 1) * HEAD_PAD)
                s_s[h] = _dot(q_ref[:, hq], k_ref[rows, hq], NT)
            for h in range(hp):
                for r0 in range(0, tq, rs):
                    rr = slice(r0, r0 + rs)
                    sv = s_s[h, rr, :]
                    if masked:
                        r = r0 + lax.broadcasted_iota(jnp.int32, (rs, tq), 0)
                        c = lax.broadcasted_iota(jnp.int32, (rs, tq), 1)
                        sv = jnp.where(r >= c, sv, NEG)
                    m_prev = m_s[h, rr, :]
                    m_new = jnp.maximum(m_prev, jnp.max(sv, axis=1, keepdims=True))
                    p_s[h, rr, :] = jnp.exp(sv - m_new).astype(BF16)
                    a_s[h, rr, :] = jnp.exp(m_prev - m_new)
                    m_s[h, rr, :] = m_new
            for h in range(hp):
                hq = slice(h * HEAD_PAD, (h + 1) * HEAD_PAD)
                acc_s[h] = a_s[h] * acc_s[h] + _dot(p_s[h], v_ref[rows, hq], NN)

        def loop(j, c):
            blk(j, False)
            return c

        lax.fori_loop(0, i, loop, 0)
        blk(i, True)
        for h in range(hp):
            den = acc_s[h, :, 128:256]
            o_ref[:, h * 128:(h + 1) * 128] = acc_s[h, :, 0:128] / den
            lse_ref[h] = m_s[h] + jnp.log(den[:, 0:1])

    return pl.pallas_call(
        body, name="mla_attn_fwd", grid=(B, N_HEADS // hp, nq),
        in_specs=[BS((tq, hp * HEAD_PAD), lambda b, h, i: (b * nq + i, h)),
                  BS((S, hp * HEAD_PAD), lambda b, h, i: (b, h)),
                  BS((S, hp * HEAD_PAD), lambda b, h, i: (b, h))],
        out_specs=[BS((tq, hp * 128), lambda b, h, i: (b * nq + i, h)),
                   BS((hp, tq, 1), lambda b, h, i: (h, b * nq + i, 0))],
        out_shape=[jax.ShapeDtypeStruct((T, 512), F32), jax.ShapeDtypeStruct((N_HEADS, T, 1), F32)],
        scratch_shapes=[pltpu.VMEM((hp, tq, 1), F32), pltpu.VMEM((hp, tq, HEAD_PAD), F32), pltpu.VMEM((hp, tq, tq), F32),
                        pltpu.VMEM((hp, tq, tq), BF16), pltpu.VMEM((hp, tq, 1), F32)],
        compiler_params=_arb(3))(Q, K, V)


def mla_attn_bwd(Q, K, V, O, dO, LSE, B, S, tq=512, hp=ATTN_HEADS_PER_STEP):
    T = B * S
    tq = min(tq, S)
    nq = S // tq

    rs = min(ATTN_STRIP, tq)

    def body(q_ref, k_ref, v_ref, o_ref, do_ref, lse_ref, dq_ref, dk_ref, dv_ref, delta_s, dk_s, dv_s, s_s, dp_s, p_s, ds_s):
        j = pl.program_id(2)

        @pl.when(j == 0)
        def _():
            dq_ref[...] = jnp.zeros_like(dq_ref)
            for h in range(hp):
                sl = slice(h * 128, (h + 1) * 128)
                delta_s[h] = jnp.sum(do_ref[:, sl] * o_ref[:, sl], axis=1, keepdims=True)

        dk_s[...] = jnp.zeros_like(dk_s)
        dv_s[...] = jnp.zeros_like(dv_s)

        def step(i, c):
            rows = pl.ds(pl.multiple_of(i * tq, tq), tq)
            for h in range(hp):
                sq, sv = slice(h * HEAD_PAD, (h + 1) * HEAD_PAD), slice(h * 128, (h + 1) * 128)
                s_s[h] = _dot(q_ref[rows, sq], k_ref[:, sq], NT)
                dp_s[h] = _dot(do_ref[rows, sv].astype(BF16), v_ref[:, h * HEAD_PAD:h * HEAD_PAD + 128], NT)
            for h in range(hp):
                for r0 in range(0, tq, rs):
                    rr = slice(r0, r0 + rs)
                    seq_rows = pl.ds(pl.multiple_of(i * tq + r0, rs), rs)
                    r = i * tq + r0 + lax.broadcasted_iota(jnp.int32, (rs, tq), 0)
                    cc = j * tq + lax.broadcasted_iota(jnp.int32, (rs, tq), 1)
                    p = jnp.where(r >= cc, jnp.exp(s_s[h, rr, :] - lse_ref[h, seq_rows, :]), 0.0)
                    p_s[h, rr, :] = p.astype(BF16)
                    ds_s[h, rr, :] = (p * (dp_s[h, rr, :] - delta_s[h, seq_rows, :])).astype(BF16)
            for h in range(hp):
                sq, sv = slice(h * HEAD_PAD, (h + 1) * HEAD_PAD), slice(h * 128, (h + 1) * 128)
                dv_s[:, sv] += _dot(p_s[h], do_ref[rows, sv].astype(BF16), TN)
                dk_s[:, sq] += _dot(ds_s[h], q_ref[rows, sq], TN)
                dq_ref[rows, sq] += _dot(ds_s[h], k_ref[:, sq], NN)
            return c

        lax.fori_loop(j, nq, step, 0)
        dk_ref[...] = dk_s[...]
        dv_ref[...] = dv_s[...]

    seq = lambda c: BS((S, c), lambda b, h, j: (b, h))
    blk = lambda c: BS((tq, c), lambda b, h, j: (b * nq + j, h))
    return pl.pallas_call(
        body, name="mla_attn_bwd", grid=(B, N_HEADS // hp, nq),
        in_specs=[seq(hp * HEAD_PAD), blk(hp * HEAD_PAD), blk(hp * HEAD_PAD), seq(hp * 128), seq(hp * 128),
                  BS((hp, S, 1), lambda b, h, j: (h, b, 0))],
        out_specs=[seq(hp * HEAD_PAD), blk(hp * HEAD_PAD), blk(hp * 128)],
        out_shape=[jax.ShapeDtypeStruct((T, 1024), F32), jax.ShapeDtypeStruct((T, 1024), F32),
                   jax.ShapeDtypeStruct((T, 512), F32)],
        scratch_shapes=[pltpu.VMEM((hp, S, 1), F32), pltpu.VMEM((tq, hp * HEAD_PAD), F32), pltpu.VMEM((tq, hp * 128), F32),
                        pltpu.VMEM((hp, tq, tq), F32), pltpu.VMEM((hp, tq, tq), F32),
                        pltpu.VMEM((hp, tq, tq), BF16), pltpu.VMEM((hp, tq, tq), BF16)],
        compiler_params=_arb(3))(Q, K, V, O, dO, LSE)


def mla_post_bwd(dQ, dK, dV, cos_t, sin_t, tm=512):
    T = dQ.shape[0]
    tm = min(tm, T)

    def body(dq_ref, dk_ref, dv_ref, c_ref, s_ref, ql_ref, kvl_ref, kr_ref):
        cos_row, sin_row = c_ref[...], s_ref[...]
        kr = jnp.zeros((tm, 128), F32)
        for h in range(N_HEADS):
            lo = h * HEAD_PAD
            ql_ref[:, lo:lo + 128] = (dq_ref[:, lo:lo + 128] * MLA_SCALE).astype(BF16)
            ql_ref[:, lo + 128:lo + 256] = (_rope_bwd(dq_ref[:, lo + 128:lo + 256], cos_row, sin_row) * MLA_SCALE).astype(BF16)
            kvl_ref[:, h * 128:(h + 1) * 128] = dk_ref[:, lo:lo + 128].astype(BF16)
            kr = kr + dk_ref[:, lo + 128:lo + 256]
        kvl_ref[:, 512:] = dv_ref[...].astype(BF16)
        kr_ref[...] = _rope_bwd(kr, cos_row, sin_row)

    rowb = lambda c: BS((tm, c), lambda i: (i, 0))
    return pl.pallas_call(
        body, name="mla_post_bwd", grid=(T // tm,),
        in_specs=[rowb(1024), rowb(1024), rowb(512), rowb(128), rowb(128)],
        out_specs=[rowb(1024), rowb(1024), rowb(128)],
        out_shape=[jax.ShapeDtypeStruct((T, 1024), BF16), jax.ShapeDtypeStruct((T, 1024), BF16),
                   jax.ShapeDtypeStruct((T, 128), F32)],
        compiler_params=_arb(1))(dQ, dK, dV, cos_t, sin_t)


def mla_norm_bwd(P, dqn, dkvn, dkr, dab, gq, gkv, tm=512):
    T = P.shape[0]
    tm = min(tm, T)

    def norm_bwd(x, dy, g):
        r = lax.rsqrt(jnp.mean(x * x, axis=-1, keepdims=True) + EPS)
        xh = x * r
        dxh = dy * g
        return r * (dxh - xh * jnp.mean(dxh * xh, axis=-1, keepdims=True)), jnp.sum(dy * xh, axis=0, keepdims=True)

    def body(p_ref, dqn_ref, dkvn_ref, dkr_ref, dab_ref, gq_ref, gkv_ref, o_ref, aq_ref, akv_ref):
        @pl.when(pl.program_id(0) == 0)
        def _():
            aq_ref[...] = jnp.zeros_like(aq_ref)
            akv_ref[...] = jnp.zeros_like(akv_ref)

        dcq, ggq = norm_bwd(p_ref[:, :Q_LORA], dqn_ref[...], gq_ref[...])
        dckv, ggkv = norm_bwd(p_ref[:, Q_LORA:640], dkvn_ref[...], gkv_ref[...])
        aq_ref[...] += ggq
        akv_ref[...] += ggkv
        o_ref[:, :Q_LORA] = dcq.astype(BF16)
        o_ref[:, Q_LORA:640] = dckv.astype(BF16)
        o_ref[:, 640:768] = dkr_ref[...].astype(BF16)
        o_ref[:, 768:896] = dab_ref[...]
        o_ref[:, 896:1024] = jnp.zeros((tm, 128), BF16)

    rowb = lambda c: BS((tm, c), lambda i: (i, 0))
    full = lambda c: BS((1, c), lambda i: (0, 0))
    return pl.pallas_call(
        body, name="mla_norm_bwd", grid=(T // tm,),
        in_specs=[rowb(1024), rowb(Q_LORA), rowb(KV_LORA), rowb(128), rowb(128), full(Q_LORA), full(KV_LORA)],
        out_specs=[rowb(1024), full(Q_LORA), full(KV_LORA)],
        out_shape=[jax.ShapeDtypeStruct((T, 1024), BF16), jax.ShapeDtypeStruct((1, Q_LORA), F32),
                   jax.ShapeDtypeStruct((1, KV_LORA), F32)],
        compiler_params=_arb(1))(P, dqn, dkvn, dkr, dab, gq, gkv)


def _mem_probs(qh, kh):
    s = _dot(qh, kh, NT) * MEM_SCALE
    p = jnp.exp(s - jnp.max(s, axis=1, keepdims=True))
    return p / jnp.sum(p, axis=1, keepdims=True)


def mem_attn_fwd(P, MKV, B, S, M, tq=512):
    T = B * S
    tq = min(tq, S)
    nq = S // tq

    def body(q_ref, kv_ref, o_ref):
        for h in range(N_HEADS):
            sl = slice(h * 128, (h + 1) * 128)
            p = _mem_probs(q_ref[:, sl].astype(BF16), kv_ref[:, sl])
            o_ref[:, sl] = _dot(p.astype(BF16), kv_ref[:, 512 + h * 128:512 + (h + 1) * 128], NN)

    return pl.pallas_call(
        body, name="mem_attn_fwd", grid=(B, nq),
        in_specs=[BS((tq, 512), lambda b, i: (b * nq + i, OFF_MEMQ // 512)), BS((M, 1024), lambda b, i: (b, 0))],
        out_specs=BS((tq, 512), lambda b, i: (b * nq + i, 0)),
        out_shape=jax.ShapeDtypeStruct((T, 512), F32), compiler_params=_arb(2))(P, MKV)


def mem_attn_bwd(P, MKV, dO, B, S, M, tq=512):
    T = B * S
    tq = min(tq, S)
    nq = S // tq

    def body(q_ref, kv_ref, do_ref, dq_ref, dkv_ref):
        @pl.when(pl.program_id(1) == 0)
        def _():
            dkv_ref[...] = jnp.zeros_like(dkv_ref)

        for h in range(N_HEADS):
            sl = slice(h * 128, (h + 1) * 128)
            sv = slice(512 + h * 128, 512 + (h + 1) * 128)
            qh = q_ref[:, sl].astype(BF16)
            kh = kv_ref[:, sl]
            do = do_ref[:, sl].astype(BF16)
            p = _mem_probs(qh, kh)
            dkv_ref[:, sv] += _dot(p.astype(BF16), do, TN)
            dp = _dot(do, kv_ref[:, sv], NT)
            ds = (p * (dp - jnp.sum(dp * p, axis=1, keepdims=True)) * MEM_SCALE).astype(BF16)
            dq_ref[:, sl] = _dot(ds, kh, NN).astype(BF16)
            dkv_ref[:, sl] += _dot(ds, qh, TN)

    return pl.pallas_call(
        body, name="mem_attn_bwd", grid=(B, nq),
        in_specs=[BS((tq, 512), lambda b, i: (b * nq + i, OFF_MEMQ // 512)), BS((M, 1024), lambda b, i: (b, 0)),
                  BS((tq, 512), lambda b, i: (b * nq + i, 0))],
        out_specs=[BS((tq, 512), lambda b, i: (b * nq + i, 0)), BS((M, 1024), lambda b, i: (b, 0))],
        out_shape=[jax.ShapeDtypeStruct((T, 512), BF16), jax.ShapeDtypeStruct((B * M, 1024), F32)],
        compiler_params=_arb(2))(P, MKV, dO)


def gain_grad(x, dy, name, tm=256):
    T, n = x.shape
    tm = min(tm, T)

    def body(x_ref, dy_ref, o_ref):
        @pl.when(pl.program_id(0) == 0)
        def _():
            o_ref[...] = jnp.zeros_like(o_ref)

        xv = x_ref[...]
        xh = xv * lax.rsqrt(jnp.mean(xv * xv, axis=-1, keepdims=True) + EPS)
        o_ref[...] += jnp.sum(dy_ref[...] * xh, axis=0, keepdims=True)

    return pl.pallas_call(
        body, name=name, grid=(T // tm,),
        in_specs=[BS((tm, n), lambda i: (i, 0))] * 2, out_specs=BS((1, n), lambda i: (0, 0)),
        out_shape=jax.ShapeDtypeStruct((1, n), F32), compiler_params=_arb(1))(x, dy)


def _conv_silu(x, w, t):
    y = x * w[3:4, :]
    for s in range(1, GDN_CONV):
        y = y + jnp.where(t >= s, pltpu.roll(x, s, 0), 0.0) * w[3 - s:4 - s, :]
    return y, _sigmoid(y)


def gdn_prep_fwd(P, conv_w, B, S):
    T = B * S

    def body(x_ref, w_ref, o_ref):
        kind = pl.program_id(1)
        t = lax.broadcasted_iota(jnp.int32, (S, 1), 0)
        y, sg = _conv_silu(x_ref[...], w_ref[...], t)
        a = y * sg
        scale = jnp.where(kind == 0, GDN_SCALE, 1.0).astype(F32)
        for h in range(N_HEADS):
            sl = slice(h * 128, (h + 1) * 128)
            seg = a[:, sl]
            n = lax.rsqrt(jnp.sum(seg * seg, axis=-1, keepdims=True) + EPS)
            o_ref[:, sl] = jnp.where(kind < 2, seg * (n * scale), seg)

    return pl.pallas_call(
        body, name="gdn_prep_fwd", grid=(B, 3),
        in_specs=[BS((S, 512), lambda b, k: (b, OFF_GDN // 512 + k)), BS((GDN_CONV, 512), lambda b, k: (0, k))],
        out_specs=BS((S, 512), lambda b, k: (b, k)),
        out_shape=jax.ShapeDtypeStruct((T, GDN_QKV), F32), compiler_params=_arb(2))(P, conv_w)


def gdn_prep_bwd(P, dqkv, conv_w, B, S):
    T = B * S

    def body(x_ref, d_ref, w_ref, o_ref, gw_ref):
        kind = pl.program_id(0)

        @pl.when(pl.program_id(1) == 0)
        def _():
            gw_ref[...] = jnp.zeros_like(gw_ref)

        t = lax.broadcasted_iota(jnp.int32, (S, 1), 0)
        x = x_ref[...]
        w = w_ref[...]
        y, sg = _conv_silu(x, w, t)
        a = y * sg
        scale = jnp.where(kind == 0, GDN_SCALE, 1.0).astype(F32)
        das = []
        for h in range(N_HEADS):
            sl = slice(h * 128, (h + 1) * 128)
            seg, dseg = a[:, sl], d_ref[:, sl]
            n = lax.rsqrt(jnp.sum(seg * seg, axis=-1, keepdims=True) + EPS)
            dn = scale * (n * dseg - seg * (n * n * n) * jnp.sum(dseg * seg, axis=-1, keepdims=True))
            das.append(jnp.where(kind < 2, dn, dseg))
        dy = jnp.concatenate(das, axis=1) * (sg * (1.0 + y * (1.0 - sg)))
        dx = dy * w[3:4, :]
        gw_ref[3:4, :] += jnp.sum(dy * x, axis=0, keepdims=True)
        for s in range(1, GDN_CONV):
            dx = dx + jnp.where(t + s < S, pltpu.roll(dy, S - s, 0), 0.0) * w[3 - s:4 - s, :]
            gw_ref[3 - s:4 - s, :] += jnp.sum(dy * jnp.where(t >= s, pltpu.roll(x, s, 0), 0.0), axis=0, keepdims=True)
        o_ref[...] = dx.astype(BF16)

    return pl.pallas_call(
        body, name="gdn_prep_bwd", grid=(3, B),
        in_specs=[BS((S, 512), lambda k, b: (b, OFF_GDN // 512 + k)), BS((S, 512), lambda k, b: (b, k)),
                  BS((GDN_CONV, 512), lambda k, b: (0, k))],
        out_specs=[BS((S, 512), lambda k, b: (b, k)), BS((GDN_CONV, 512), lambda k, b: (0, k))],
        out_shape=[jax.ShapeDtypeStruct((T, GDN_QKV), BF16), jax.ShapeDtypeStruct((GDN_CONV, GDN_QKV), F32)],
        compiler_params=_arb(2))(P, dqkv, conv_w)


def _chunk_row(n_rows):
    return lax.broadcasted_iota(jnp.int32, (n_rows, 1), 0) % CHUNK


def gdn_gate_fwd(P, alog_row, dt_row, B, S):
    T = B * S

    def body(x_ref, al_ref, dt_ref, o_ref):
        x = x_ref[...]
        lane = lax.broadcasted_iota(jnp.int32, (1, 128), 1)
        g = jnp.where(lane < 4, -jnp.exp(al_ref[...]) * _softplus(x + dt_ref[...]), 0.0)
        t = _chunk_row(S)
        for s in (1, 2, 4, 8, 16, 32):
            g = g + jnp.where(t >= s, pltpu.roll(g, s, 0), 0.0)
        o_ref[...] = jnp.where(lane < 4, g, jnp.where(lane < 8, _sigmoid(x), 0.0))

    row = BS((1, 128), lambda b: (0, 0))
    return pl.pallas_call(
        body, name="gdn_gate_fwd", grid=(B,),
        in_specs=[BS((S, 128), lambda b: (b, 768 // 128)), row, row], out_specs=BS((S, 128), lambda b: (b, 0)),
        out_shape=jax.ShapeDtypeStruct((T, 128), F32), compiler_params=_arb(1))(P, alog_row, dt_row)


def gdn_gate_bwd(P, dGB, alog_row, dt_row, B, S):
    T = B * S

    def body(x_ref, d_ref, al_ref, dt_ref, o_ref, acc_ref):
        @pl.when(pl.program_id(0) == 0)
        def _():
            acc_ref[...] = jnp.zeros_like(acc_ref)

        x, d = x_ref[...], d_ref[...]
        lane = lax.broadcasted_iota(jnp.int32, (1, 128), 1)
        z = x + dt_ref[...]
        coef = -jnp.exp(al_ref[...])
        g = coef * _softplus(z)
        da = jnp.where(lane < 4, d * coef * _sigmoid(z), 0.0)
        beta = _sigmoid(x)
        o_ref[...] = jnp.where(lane < 4, da, jnp.where(lane < 8, d * beta * (1.0 - beta), 0.0)).astype(BF16)
        acc_ref[0:1, :] += jnp.sum(jnp.where(lane < 4, d * g, 0.0), axis=0, keepdims=True)
        acc_ref[1:2, :] += jnp.sum(da, axis=0, keepdims=True)

    row = BS((1, 128), lambda b: (0, 0))
    return pl.pallas_call(
        body, name="gdn_gate_bwd", grid=(B,),
        in_specs=[BS((S, 128), lambda b: (b, 768 // 128)), BS((S, 128), lambda b: (b, 0)), row, row],
        out_specs=[BS((S, 128), lambda b: (b, 0)), BS((8, 128), lambda b: (0, 0))],
        out_shape=[jax.ShapeDtypeStruct((T, 128), BF16), jax.ShapeDtypeStruct((8, 128), F32)],
        compiler_params=_arb(1))(P, dGB, alog_row, dt_row)


def _chunk_masks(nc):
    r = lax.broadcasted_iota(jnp.int32, (nc, CHUNK, CHUNK), 1)
    c = lax.broadcasted_iota(jnp.int32, (nc, CHUNK, CHUNK), 2)
    return r >= c, r > c


def _chunk_local(q, k, gc, gr, beta, incl, strict):
    decay = jnp.exp(jnp.where(incl, gc - gr, NEG))
    kb = k * beta
    kbf = k.astype(BF16)
    m_kk = _bdot("gcd,gjd->gcj", kb.astype(BF16), kbf)
    l_mat = jnp.where(strict, m_kk * decay, 0.0)
    a_mat = _bdot("gcd,gjd->gcj", q.astype(BF16), kbf) * decay
    return decay, kb, l_mat, a_mat


def _split_bf16(x):
    hi = x.astype(BF16)
    return hi, (x - hi.astype(F32)).astype(BF16)


def _mm_split(ah, al, bh, bl):
    spec = "gij,gjk->gik"
    return _bdot(spec, ah, bh) + (_bdot(spec, ah, bl) + _bdot(spec, al, bh))


def gdn_chunk_fwd(qkv, GB, Grow, B, S, nc=8):
    T = B * S
    N = S // CHUNK
    nc = min(nc, N)
    nb = N // nc
    R = nc * CHUNK

    def body(q_ref, k_ref, v_ref, gb_ref, gr_ref, u_ref, w_ref, t_ref, a_ref):
        incl, strict = _chunk_masks(nc)
        eye = (lax.broadcasted_iota(jnp.int32, (nc, CHUNK, CHUNK), 1)
               == lax.broadcasted_iota(jnp.int32, (nc, CHUNK, CHUNK), 2)).astype(F32)
        for h in range(N_HEADS):
            sl = slice(h * 128, (h + 1) * 128)
            q = q_ref[:, sl].reshape(nc, CHUNK, 128)
            k = k_ref[:, sl].reshape(nc, CHUNK, 128)
            v = v_ref[:, sl].reshape(nc, CHUNK, 128)
            gc = gb_ref[:, h:h + 1].reshape(nc, CHUNK, 1)
            beta = gb_ref[:, 4 + h:5 + h].reshape(nc, CHUNK, 1)
            gr = gr_ref[h][:, None, :]
            _, kb, l_mat, a_mat = _chunk_local(q, k, gc, gr, beta, incl, strict)
            pw = -l_mat
            tinv = eye + pw
            for _ in range(5):
                ph, pl_ = _split_bf16(pw)
                pw = _mm_split(ph, pl_, ph, pl_)
                ph, pl_ = _split_bf16(pw)
                th, tl = _split_bf16(tinv)
                tinv = tinv + _mm_split(th, tl, ph, pl_)
            tb = tinv.astype(BF16)
            u = _bdot("gcj,gjv->gcv", tb, (v * beta).astype(BF16))
            w = _bdot("gcj,gjk->gck", tb, (kb * jnp.exp(gc)).astype(BF16))
            u_ref[:, sl] = u.reshape(R, 128)
            w_ref[:, sl] = w.reshape(R, 128).astype(BF16)
            t_ref[h] = tb
            a_ref[h] = a_mat.astype(BF16)

    rowb = lambda c, j: BS((R, c), lambda b, n: (b * nb + n, j))
    mat = BS((None, N_HEADS, nc, CHUNK, CHUNK), lambda b, n: (b, 0, n, 0, 0))
    return pl.pallas_call(
        body, name="gdn_chunk_fwd", grid=(B, nb),
        in_specs=[rowb(512, 0), rowb(512, 1), rowb(512, 2), rowb(128, 0),
                  BS((None, N_HEADS, nc, CHUNK), lambda b, n: (b, 0, n, 0))],
        out_specs=[rowb(512, 0), rowb(512, 0), mat, mat],
        out_shape=[jax.ShapeDtypeStruct((T, 512), F32), jax.ShapeDtypeStruct((T, 512), BF16),
                   jax.ShapeDtypeStruct((B, N_HEADS, N, CHUNK, CHUNK), BF16),
                   jax.ShapeDtypeStruct((B, N_HEADS, N, CHUNK, CHUNK), BF16)],
        compiler_params=_arb(2))(qkv, qkv, qkv, GB, Grow)


def gdn_scan_fwd(qkv3, U3, W3, GB3, A, B, S):
    N = S // CHUNK

    def body(q_ref, k_ref, u_ref, w_ref, gb_ref, a_ref, o_ref, vn_ref, st_ref, s_s):
        @pl.when(pl.program_id(0) == 0)
        def _():
            s_s[...] = jnp.zeros_like(s_s)

        for b in range(B):
            for h in range(N_HEADS):
                sl = slice(h * 128, (h + 1) * 128)
                st = s_s[b, h]
                st_ref[b, h] = st
                stb = st.astype(BF16)
                g = gb_ref[b, :, h:h + 1]
                gl = g[CHUNK - 1:CHUNK, :]
                vn = u_ref[b, :, sl] - _dot(w_ref[b, :, sl].astype(BF16), stb, NN)
                vnb = vn.astype(BF16)
                o = _dot((q_ref[b, :, sl] * jnp.exp(g)).astype(BF16), stb, NN) + _dot(a_ref[b, h].astype(BF16), vnb, NN)
                vn_ref[b, :, sl] = vnb
                o_ref[b, :, sl] = o
                s_s[b, h] = st * jnp.exp(gl) + _dot((k_ref[b, :, sl] * jnp.exp(gl - g)).astype(BF16), vnb, TN)

    tok = lambda c, j: BS((B, CHUNK, c), lambda n: (0, n, j))
    return pl.pallas_call(
        body, name="gdn_scan_fwd", grid=(N,),
        in_specs=[tok(512, 0), tok(512, 1), tok(512, 0), tok(512, 0), tok(128, 0),
                  BS((B, N_HEADS, None, CHUNK, CHUNK), lambda n: (0, 0, n, 0, 0))],
        out_specs=[tok(512, 0), tok(512, 0), BS((B, N_HEADS, None, 128, 128), lambda n: (0, 0, n, 0, 0))],
        out_shape=[jax.ShapeDtypeStruct((B, S, 512), F32), jax.ShapeDtypeStruct((B, S, 512), BF16),
                   jax.ShapeDtypeStruct((B, N_HEADS, N, 128, 128), F32)],
        scratch_shapes=[pltpu.VMEM((B, N_HEADS, 128, 128), F32)],
        compiler_params=_arb(1))(qkv3, qkv3, U3, W3, GB3, A)


def gdn_scan_bwd(dO3, qkv3, W3, Vn3, GB3, A, St, B, S):
    N = S // CHUNK

    def body(do_ref, q_ref, k_ref, w_ref, vn_ref, gb_ref, a_ref, st_ref,
             du_ref, dw_ref, dq_ref, dk_ref, da_ref, dg_ref, ds_s):
        @pl.when(pl.program_id(0) == 0)
        def _():
            ds_s[...] = jnp.zeros_like(ds_s)

        lane = lax.broadcasted_iota(jnp.int32, (1, 128), 1)
        last = lax.broadcasted_iota(jnp.int32, (CHUNK, 1), 0) == CHUNK - 1
        for b in range(B):
            dg_all = jnp.zeros((CHUNK, 128), F32)
            for h in range(N_HEADS):
                sl = slice(h * 128, (h + 1) * 128)
                st = st_ref[b, h]
                stb = st.astype(BF16)
                dsn = ds_s[b, h]
                dsnb = dsn.astype(BF16)
                g = gb_ref[b, :, h:h + 1]
                gl = g[CHUNK - 1:CHUNK, :]
                egl = jnp.exp(gl)
                ekd = jnp.exp(gl - g)
                eg = jnp.exp(g)
                q, k = q_ref[b, :, sl], k_ref[b, :, sl]
                kd = k * ekd
                qg = q * eg
                do = do_ref[b, :, sl].astype(BF16)
                vnb = vn_ref[b, :, sl].astype(BF16)
                dvn = _dot(a_ref[b, h].astype(BF16), do, TN) + _dot(kd.astype(BF16), dsnb, NN)
                dvnb = dvn.astype(BF16)
                da_ref[b, h] = _dot(do, vnb, NT)
                dqg = _dot(do, stb, NT)
                dkd = _dot(vnb, dsnb, NT)
                ds_s[b, h] = (_dot(qg.astype(BF16), do, TN) + egl * dsn - _dot(w_ref[b, :, sl].astype(BF16), dvnb, TN))
                du_ref[b, :, sl] = dvnb
                dw_ref[b, :, sl] = (-_dot(dvnb, stb, NT)).astype(BF16)
                dq_ref[b, :, sl] = dqg * eg
                dk_ref[b, :, sl] = dkd * ekd
                ddel = jnp.sum(dkd * kd, axis=1, keepdims=True)
                dgl = jnp.sum(ddel, axis=0, keepdims=True) + jnp.sum(jnp.sum(st * dsn, axis=1, keepdims=True), axis=0, keepdims=True) * egl
                col = jnp.sum(dqg * qg, axis=1, keepdims=True) - ddel + jnp.where(last, dgl, 0.0)
                dg_all = jnp.where(lane == h, col, dg_all)
            dg_ref[b] = dg_all

    tok = lambda c, j: BS((B, CHUNK, c), lambda n: (0, N - 1 - n, j))
    mat = lambda d: BS((B, N_HEADS, None, d, d), lambda n: (0, 0, N - 1 - n, 0, 0))
    return pl.pallas_call(
        body, name="gdn_scan_bwd", grid=(N,),
        in_specs=[tok(512, 0), tok(512, 0), tok(512, 1), tok(512, 0), tok(512, 0), tok(128, 0), mat(CHUNK), mat(128)],
        out_specs=[tok(512, 0), tok(512, 0), tok(512, 0), tok(512, 0), mat(CHUNK), tok(128, 0)],
        out_shape=[jax.ShapeDtypeStruct((B, S, 512), BF16)] * 2 + [jax.ShapeDtypeStruct((B, S, 512), F32)] * 2
        + [jax.ShapeDtypeStruct((B, N_HEADS, N, CHUNK, CHUNK), F32), jax.ShapeDtypeStruct((B, S, 128), F32)],
        scratch_shapes=[pltpu.VMEM((B, N_HEADS, 128, 128), F32)],
        compiler_params=_arb(1))(dO3, qkv3, qkv3, W3, Vn3, GB3, A, St)


def gdn_chunk_bwd(qkv, GB, Grow, Tinv, dA, dU, dW, dQ1, dK1, dG1, B, S, nc=8):
    T = B * S
    N = S // CHUNK
    nc = min(nc, N)
    nb = N // nc
    R = nc * CHUNK

    def body(q_ref, k_ref, v_ref, gb_ref, gr_ref, t_ref, da_ref, du_ref, dw_ref, dq1_ref, dk1_ref, dg1_ref, o_ref, dgb_ref):
        incl, strict = _chunk_masks(nc)
        lane = lax.broadcasted_iota(jnp.int32, (1, 128), 1)
        dg_all = dg1_ref[...]
        db_all = jnp.zeros((R, 128), F32)
        for h in range(N_HEADS):
            sl = slice(h * 128, (h + 1) * 128)
            q = q_ref[:, sl].reshape(nc, CHUNK, 128)
            k = k_ref[:, sl].reshape(nc, CHUNK, 128)
            v = v_ref[:, sl].reshape(nc, CHUNK, 128)
            gc = gb_ref[:, h:h + 1].reshape(nc, CHUNK, 1)
            beta = gb_ref[:, 4 + h:5 + h].reshape(nc, CHUNK, 1)
            gr = gr_ref[h][:, None, :]
            decay, kb, l_mat, a_mat = _chunk_local(q, k, gc, gr, beta, incl, strict)
            eg = jnp.exp(gc)
            kbg = kb * eg
            vb = v * beta
            tb = t_ref[h].astype(BF16)
            du = du_ref[:, sl].reshape(nc, CHUNK, 128).astype(BF16)
            dw = dw_ref[:, sl].reshape(nc, CHUNK, 128).astype(BF16)
            dvb = _bdot("gcj,gcv->gjv", tb, du)
            dkbg = _bdot("gcj,gck->gjk", tb, dw)
            dt = _bdot("gcv,gjv->gcj", du, vb.astype(BF16)) + _bdot("gck,gjk->gcj", dw, kbg.astype(BF16))
            tmp = _bdot("gac,gab->gcb", tb, dt.astype(BF16))
            dl = jnp.where(strict, -_bdot("gcb,gdb->gcd", tmp.astype(BF16), tb), 0.0)
            da = da_ref[h]
            dm = (dl * decay).astype(BF16)
            dqk = (da * decay).astype(BF16)
            kbf = k.astype(BF16)
            dkb = _bdot("gcj,gjd->gcd", dm, kbf) + dkbg * eg
            dk = (_bdot("gcj,gcd->gjd", dm, kb.astype(BF16)) + _bdot("gcj,gcd->gjd", dqk, q.astype(BF16))
                  + dk1_ref[:, sl].reshape(nc, CHUNK, 128) + dkb * beta)
            dq = _bdot("gcj,gjd->gcd", dqk, kbf) + dq1_ref[:, sl].reshape(nc, CHUNK, 128)
            e = dl * l_mat + da * a_mat
            dgc = (jnp.sum(e, axis=2, keepdims=True) - jnp.sum(jnp.swapaxes(e, 1, 2), axis=2, keepdims=True)
                   + jnp.sum(dkbg * kbg, axis=2, keepdims=True))
            dbeta = jnp.sum(dkb * k, axis=2, keepdims=True) + jnp.sum(dvb * v, axis=2, keepdims=True)
            o_ref[:, sl] = dq.reshape(R, 128)
            o_ref[:, 512 + h * 128:512 + (h + 1) * 128] = dk.reshape(R, 128)
            o_ref[:, 1024 + h * 128:1024 + (h + 1) * 128] = (dvb * beta).reshape(R, 128)
            dg_all = dg_all + jnp.where(lane == h, dgc.reshape(R, 1), 0.0)
            db_all = jnp.where(lane == 4 + h, dbeta.reshape(R, 1), db_all)
        t = _chunk_row(R)
        for s in (1, 2, 4, 8, 16, 32):
            dg_all = dg_all + jnp.where(t + s < CHUNK, pltpu.roll(dg_all, R - s, 0), 0.0)
        dgb_ref[...] = jnp.where(lane < 4, dg_all, db_all)

    rowb = lambda c, j: BS((R, c), lambda b, n: (b * nb + n, j))
    mat = BS((None, N_HEADS, nc, CHUNK, CHUNK), lambda b, n: (b, 0, n, 0, 0))
    return pl.pallas_call(
        body, name="gdn_chunk_bwd", grid=(B, nb),
        in_specs=[rowb(512, 0), rowb(512, 1), rowb(512, 2), rowb(128, 0),
                  BS((None, N_HEADS, nc, CHUNK), lambda b, n: (b, 0, n, 0)), mat, mat,
                  rowb(512, 0), rowb(512, 0), rowb(512, 0), rowb(512, 0), rowb(128, 0)],
        out_specs=[rowb(GDN_QKV, 0), rowb(128, 0)],
        out_shape=[jax.ShapeDtypeStruct((T, GDN_QKV), F32), jax.ShapeDtypeStruct((T, 128), F32)],
        compiler_params=_arb(2))(qkv, qkv, qkv, GB, Grow, Tinv, dA, dU, dW, dQ1, dK1, dG1)


def _gdn_out_norm(og, gg):
    outs, xhs, rs = [], [], []
    for h in range(N_HEADS):
        seg = og[:, h * 128:(h + 1) * 128]
        r = lax.rsqrt(jnp.mean(seg * seg, axis=-1, keepdims=True) + EPS)
        xh = seg * r
        outs.append(xh * gg)
        xhs.append(xh)
        rs.append(r)
    return outs, xhs, rs


def merge_fwd(o_mla, o_gdn, o_mem, P, x, tgt, w_out, g_gdn, g_fin, tm=256):
    T = x.shape[0]
    tm = min(tm, T)

    def body(om_ref, og_ref, oc_ref, gate_ref, x_ref, t_ref, w_ref, gg_ref, gf_ref, mix_ref, dx_ref, dxb_ref, sq_ref, gnf_ref):
        @pl.when(pl.program_id(0) == 0)
        def _():
            sq_ref[...] = jnp.zeros_like(sq_ref)
            gnf_ref[...] = jnp.zeros_like(gnf_ref)

        ogn, _, _ = _gdn_out_norm(og_ref[...], gg_ref[...])
        cat = jnp.concatenate([om_ref[...]] + ogn + [oc_ref[...]], axis=1)
        gt = gate_ref[...]
        mixed = (cat * (gt * _sigmoid(gt))).astype(BF16)
        mix_ref[...] = mixed
        x2 = x_ref[...] + _dot(mixed, w_ref[...], NN)
        r2 = lax.rsqrt(jnp.mean(x2 * x2, axis=-1, keepdims=True) + EPS)
        xh = x2 * r2
        gf = gf_ref[...]
        diff = xh * gf - t_ref[...]
        sq_ref[...] += jnp.sum(diff * diff, axis=0, keepdims=True)
        dy = diff * (1.0 / D_MODEL)
        gnf_ref[...] += jnp.sum(dy * xh, axis=0, keepdims=True)
        dxh = dy * gf
        dx = r2 * (dxh - xh * jnp.mean(dxh * xh, axis=-1, keepdims=True))
        dx_ref[...] = dx
        dxb_ref[...] = dx.astype(BF16)

    rowb = lambda c, j=0: BS((tm, c), lambda i: (i, j))
    full = lambda r, c: BS((r, c), lambda i: (0, 0))
    return pl.pallas_call(
        body, name="merge_fwd", grid=(T // tm,),
        in_specs=[rowb(512), rowb(512), rowb(512), rowb(D_MIX, OFF_GATE // D_MIX), rowb(D_MODEL), rowb(D_MODEL),
                  full(D_MIX, D_MODEL), full(1, 128), full(1, D_MODEL)],
        out_specs=[rowb(D_MIX), rowb(D_MODEL), rowb(D_MODEL), full(1, D_MODEL), full(1, D_MODEL)],
        out_shape=[jax.ShapeDtypeStruct((T, D_MIX), BF16), jax.ShapeDtypeStruct((T, D_MODEL), F32),
                   jax.ShapeDtypeStruct((T, D_MODEL), BF16),
                   jax.ShapeDtypeStruct((1, D_MODEL), F32), jax.ShapeDtypeStruct((1, D_MODEL), F32)],
        compiler_params=_arb(1))(o_mla, o_gdn, o_mem, P, x, tgt, w_out, g_gdn, g_fin)


def merge_bwd(dx2, o_mla, o_gdn, o_mem, P, w_out, g_gdn, tm=256):
    T = dx2.shape[0]
    tm = min(tm, T)

    def body(dx_ref, om_ref, og_ref, oc_ref, gate_ref, w_ref, gg_ref, dgate_ref, dom_ref, dog_ref, doc_ref, ggn_ref):
        @pl.when(pl.program_id(0) == 0)
        def _():
            ggn_ref[...] = jnp.zeros_like(ggn_ref)

        gg = gg_ref[...]
        dmix = _dot(dx_ref[...].astype(BF16), w_ref[...], NT)
        ogn, xhs, rs = _gdn_out_norm(og_ref[...], gg)
        cat = jnp.concatenate([om_ref[...]] + ogn + [oc_ref[...]], axis=1)
        gt = gate_ref[...]
        sg = _sigmoid(gt)
        dgate_ref[...] = (dmix * cat * (sg * (1.0 + gt * (1.0 - sg)))).astype(BF16)
        dcat = dmix * (gt * sg)
        dom_ref[...] = dcat[:, :512]
        doc_ref[...] = dcat[:, 1024:]
        acc = jnp.zeros((1, 128), F32)
        for h in range(N_HEADS):
            dseg = dcat[:, 512 + h * 128:512 + (h + 1) * 128]
            acc = acc + jnp.sum(dseg * xhs[h], axis=0, keepdims=True)
            dxh = dseg * gg
            dog_ref[:, h * 128:(h + 1) * 128] = rs[h] * (dxh - xhs[h] * jnp.mean(dxh * xhs[h], axis=-1, keepdims=True))
        ggn_ref[...] += acc

    rowb = lambda c, j=0: BS((tm, c), lambda i: (i, j))
    full = lambda r, c: BS((r, c), lambda i: (0, 0))
    return pl.pallas_call(
        body, name="merge_bwd", grid=(T // tm,),
        in_specs=[rowb(D_MODEL), rowb(512), rowb(512), rowb(512), rowb(D_MIX, OFF_GATE // D_MIX),
                  full(D_MIX, D_MODEL), full(1, 128)],
        out_specs=[rowb(D_MIX), rowb(512), rowb(512), rowb(512), full(1, 128)],
        out_shape=[jax.ShapeDtypeStruct((T, D_MIX), BF16)] + [jax.ShapeDtypeStruct((T, 512), F32)] * 3
        + [jax.ShapeDtypeStruct((1, 128), F32)],
        compiler_params=_arb(1))(dx2, o_mla, o_gdn, o_mem, P, w_out, g_gdn)


def in_proj_bwd(dP, wp, x, dx2, gain, after, tm=512):
    T, n = x.shape
    tm = min(tm, T)
    k = len(dP)
    widths = [p.shape[1] for p in dP]
    offs = [sum(widths[:i]) for i in range(k)]

    def body(*refs):
        w_ref, x_ref, dx2_ref, g_ref = refs[k:k + 4]
        o_ref, acc_ref = refs[-2:]

        @pl.when(pl.program_id(0) == 0)
        def _():
            acc_ref[...] = jnp.zeros_like(acc_ref)

        dy = None
        for a_ref, off, w in zip(refs[:k], offs, widths):
            d = _dot(a_ref[...], w_ref[off:off + w, :], NN)
            dy = d if dy is None else dy + d
        xv = x_ref[...]
        r = lax.rsqrt(jnp.mean(xv * xv, axis=-1, keepdims=True) + EPS)
        xh = xv * r
        acc_ref[...] += jnp.sum(dy * xh, axis=0, keepdims=True)
        dxh = dy * g_ref[...]
        o_ref[...] = dx2_ref[...] + r * (dxh - xh * jnp.mean(dxh * xh, axis=-1, keepdims=True))

    rowb = BS((tm, n), lambda i: (i, 0))
    full = BS((1, n), lambda i: (0, 0))
    return pl.pallas_call(
        body, name="in_proj_bwd", grid=(T // tm,),
        in_specs=[BS((tm, w), lambda i: (i, 0)) for w in widths]
        + [BS(wp.shape, lambda i: (0, 0), pipeline_mode=pl.Buffered(1)), rowb, rowb, full, BS(memory_space=pl.ANY)],
        out_specs=[rowb, full], out_shape=[jax.ShapeDtypeStruct((T, n), F32), jax.ShapeDtypeStruct((1, n), F32)],
        compiler_params=_arb(1))(*dP, wp, x, dx2, gain, after)


W_IN_SHARD = D_IN // 4
_GDN0 = Q_LORA + KV_LORA + MLA_ROPE
_AB0 = _GDN0 + GDN_QKV
_MEMQ0 = _AB0 + 2 * N_HEADS
_GATE0 = _MEMQ0 + N_HEADS * MEM_DH


def _w_in_row_map():
    a, m, gt = _AB0 - 2 * W_IN_SHARD, _MEMQ0 - 2 * W_IN_SHARD, _GATE0 - 2 * W_IN_SHARD
    e0 = OFF_GDN + W_IN_SHARD - _GDN0
    e1 = e0 + W_IN_SHARD
    e2 = OFF_GATE + W_IN_SHARD - gt
    return [(0, 0, 0, 672), (0, 672, 704, 32), (2, a, 768, m - a), (2, m, OFF_MEMQ, gt - m), (0, _GDN0, OFF_GDN, W_IN_SHARD - _GDN0),
            (1, 0, e0, W_IN_SHARD), (2, 0, e1, a), (2, gt, OFF_GATE, W_IN_SHARD - gt), (3, 0, e2, W_IN_SHARD)]


_W_IN_ZERO_ROWS = [(672, 32), (736, 32), (776, 248)]
W_IN_LANES = 256


def pad_w_in_t(shards):
    per_half = shards.shape[3] // W_IN_LANES

    def body(s_ref, o_ref):
        for r0, n in _W_IN_ZERO_ROWS:
            o_ref[r0:r0 + n, :] = jnp.zeros((n, W_IN_LANES), o_ref.dtype)
        for q, src, dst, n in _w_in_row_map():
            o_ref[dst:dst + n, :] = s_ref[q, src:src + n, :]

    return pl.pallas_call(
        body, name="pad_w_in_t", grid=(D_MODEL // W_IN_LANES,),
        in_specs=[BS((N_CHIPS, None, W_IN_SHARD, W_IN_LANES), lambda j: (0, j // per_half, 0, j % per_half))],
        out_specs=BS((N_PAD, W_IN_LANES), lambda j: (0, j)),
        out_shape=jax.ShapeDtypeStruct((N_PAD, D_MODEL), shards.dtype), compiler_params=_arb(1))(shards)


def unpad_w_in_t(g):
    def body(g_ref, o_ref):
        for q, src, dst, n in _w_in_row_map():
            o_ref[q, src:src + n, :] = g_ref[dst:dst + n, :]

    return pl.pallas_call(
        body, name="unpad_w_in_t", grid=(D_MODEL // W_IN_LANES,),
        in_specs=[BS((N_PAD, W_IN_LANES), lambda j: (0, j))], out_specs=BS((N_CHIPS, W_IN_SHARD, W_IN_LANES), lambda j: (0, 0, j)),
        out_shape=jax.ShapeDtypeStruct((N_CHIPS, W_IN_SHARD, D_MODEL), g.dtype), compiler_params=_arb(1))(g)


def _pad_w_q_b_t(s):
    z = jnp.zeros((32, s.shape[2]), s.dtype)
    parts = []
    for h in range(N_HEADS):
        parts += [s[h, :128], s[h, 128:160], z, s[h, 160:192], z]
    return jnp.concatenate(parts, axis=0)


def _unpad_w_q_b_t(g):
    return jnp.stack([jnp.concatenate([g[h * HEAD_PAD:h * HEAD_PAD + 128], g[h * HEAD_PAD + 128:h * HEAD_PAD + 160],
                                       g[h * HEAD_PAD + 192:h * HEAD_PAD + 224]]) for h in range(N_HEADS)])


def _perm_w_kv_b(s):
    return jnp.concatenate([s[h, :, :128] for h in range(N_HEADS)] + [s[h, :, 128:] for h in range(N_HEADS)], axis=1)


def _unperm_w_kv_b(g):
    return jnp.stack([jnp.concatenate([g[:, h * 128:(h + 1) * 128], g[:, 512 + h * 128:512 + (h + 1) * 128]], axis=1)
                      for h in range(N_HEADS)])


def _lane_row(v4):
    return jnp.pad(v4.reshape(1, -1).astype(F32), ((0, 0), (0, 128 - v4.size)))


def _pack(pieces, n_rows):
    flat = jnp.concatenate([p.reshape(-1) for p in pieces])
    return jnp.pad(flat, (0, n_rows * 1024 - flat.size)).reshape(n_rows, 1024)


def _unpack(block, shapes):
    flat = block.reshape(-1)
    out, off = [], 0
    for shp in shapes:
        n = int(np.prod(shp))
        out.append(flat[off:off + n].reshape(shp))
        off += n
    return out


N_CHIPS = 4
MESH = pl.DeviceIdType.MESH
ANY = BS(memory_space=pl.ANY)


def _place():
    return lax.axis_index("x"), lax.axis_index("y"), lax.axis_index("c")


def _other_chips(x, y):
    return [(1 - x, y), (x, 1 - y), (1 - x, 1 - y)]


def _half(split, which):
    axis, size = split
    ds = pl.ds(pl.multiple_of(which * size, 16 if axis == 0 else 128), size)
    return (ds, slice(None)) if axis == 0 else (slice(None), ds)


SEM = BS(memory_space=pltpu.SEMAPHORE)
HBM = BS(memory_space=pltpu.HBM)
_IN_HBM = lambda a: pltpu.with_memory_space_constraint(a, pltpu.HBM)
_SIDE_EFFECT = pltpu.SideEffectType.DATAFLOW_SIDE_EFFECTING


def _late_gather_copies(s_refs, l_refs, send_sems, recv_sems, local_sems, with_arrivals):
    x, y, c = _place()
    sends, recvs, locals_ = [], [], []
    for i, (s_ref, l_ref) in enumerate(zip(s_refs, l_refs)):
        locals_.append(pltpu.make_async_copy(s_ref, l_ref.at[2 * x + y], local_sems.at[i]))
        for j, (px, py) in enumerate(_other_chips(x, y)):
            k = 3 * i + j
            sends.append(pltpu.make_async_remote_copy(src_ref=s_ref, dst_ref=l_ref.at[2 * x + y], send_sem=send_sems.at[k],
                                                      recv_sem=recv_sems.at[k], device_id=(px, py, c), device_id_type=MESH))
            if with_arrivals:
                recvs.append(pltpu.make_async_remote_copy(src_ref=s_ref, dst_ref=l_ref.at[2 * px + py], send_sem=send_sems.at[k],
                                                          recv_sem=recv_sems.at[k], device_id=(px, py, c), device_id_type=MESH))
    return sends, recvs, locals_


def late_gather_start(shards, after, name):
    n = len(shards)

    def body(*refs):
        s_refs, l_refs = refs[:n], refs[n:2 * n]
        send_sems, recv_sems, local_sems = refs[2 * n + 1:2 * n + 4]
        token = refs[-1]
        sends, _, locals_ = _late_gather_copies(s_refs, l_refs, send_sems, recv_sems, local_sems, False)
        for cp in locals_ + sends:
            cp.start()
        token[...] = jnp.zeros_like(token)

    lands = [lax.empty((N_CHIPS,) + s.shape, s.dtype) for s in shards]
    hbm_like = lambda a: pltpu.HBM(a.shape, a.dtype)
    out = pl.pallas_call(
        body, name=name,
        out_shape=[pltpu.SemaphoreType.DMA((3 * n,)), pltpu.SemaphoreType.DMA((3 * n,)), pltpu.SemaphoreType.DMA((n,))]
        + [hbm_like(s) for s in shards] + [hbm_like(l) for l in lands] + [jax.ShapeDtypeStruct((8, 128), F32)],
        in_specs=[HBM] * (2 * n) + [BS(memory_space=pl.ANY)], out_specs=[SEM] * 3 + [HBM] * (2 * n) + [BS(memory_space=pltpu.VMEM)],
        input_output_aliases={i: 3 + i for i in range(2 * n)},
        compiler_params=pltpu.CompilerParams(has_side_effects=_SIDE_EFFECT))(
            *[_IN_HBM(s) for s in shards], *[_IN_HBM(l) for l in lands], after)
    return out[:3], out[3:3 + n], out[3 + n:3 + 2 * n], out[-1]


def late_gather_wait(sems, shards, lands, after, name):
    n = len(shards)

    def body(*refs):
        s_refs, l_refs = refs[:n], refs[n:2 * n]
        send_sems, recv_sems, local_sems = refs[2 * n:2 * n + 3]
        sends, recvs, locals_ = _late_gather_copies(s_refs, l_refs, send_sems, recv_sems, local_sems, True)
        for cp in locals_:
            cp.wait()
        for cp in sends:
            cp.wait_send()
        for cp in recvs:
            cp.wait_recv()

    hbm_like = lambda a: pltpu.HBM(a.shape, a.dtype)
    out = pl.pallas_call(
        body, name=name, out_shape=[hbm_like(s) for s in shards] + [hbm_like(l) for l in lands],
        in_specs=[HBM] * (2 * n) + [SEM] * 3 + [BS(memory_space=pl.ANY)], out_specs=[HBM] * (2 * n),
        input_output_aliases={i: i for i in range(2 * n)},
        compiler_params=pltpu.CompilerParams(has_side_effects=_SIDE_EFFECT))(*shards, *lands, *sems, after)
    return out[n:]


def _half_gather_copies(s_ref, l_ref, send_sems, recv_sems, with_arrivals):
    x, y, c = _place()
    sends, recvs = [], []
    for j, (px, py) in enumerate(_other_chips(x, y)):
        sends.append(pltpu.make_async_remote_copy(src_ref=s_ref.at[c], dst_ref=l_ref.at[2 * x + y, c], send_sem=send_sems.at[j],
                                                  recv_sem=recv_sems.at[j], device_id=(px, py, c), device_id_type=MESH))
        if with_arrivals:
            recvs.append(pltpu.make_async_remote_copy(src_ref=s_ref.at[c], dst_ref=l_ref.at[2 * px + py, c], send_sem=send_sems.at[j],
                                                      recv_sem=recv_sems.at[j], device_id=(px, py, c), device_id_type=MESH))
    return sends, recvs


def half_gather_start(shard, name):
    def body(s_ref, l_ref, send_sems, recv_sems, local_sem, s_thru, l_thru, token):
        x, y, _ = _place()
        pltpu.make_async_copy(s_ref, l_ref.at[2 * x + y], local_sem.at[0]).start()
        for cp in _half_gather_copies(s_ref, l_ref, send_sems, recv_sems, False)[0]:
            cp.start()
        token[...] = jnp.zeros_like(token)

    land = lax.empty((N_CHIPS,) + shard.shape, shard.dtype)
    out = pl.pallas_call(
        body, name=name,
        out_shape=[pltpu.SemaphoreType.DMA((3,)), pltpu.SemaphoreType.DMA((3,)), pltpu.SemaphoreType.DMA((1,)),
                   pltpu.HBM(shard.shape, shard.dtype), pltpu.HBM(land.shape, land.dtype), jax.ShapeDtypeStruct((8, 128), F32)],
        in_specs=[HBM, HBM], out_specs=[SEM] * 3 + [HBM, HBM, BS(memory_space=pltpu.VMEM)],
        input_output_aliases={0: 3, 1: 4},
        compiler_params=pltpu.CompilerParams(has_side_effects=_SIDE_EFFECT))(_IN_HBM(shard), _IN_HBM(land))
    return out[:3], out[3], out[4], out[5]


def half_gather_wait(sems, shard, land, after, name):
    def body(s_ref, l_ref, send_sems, recv_sems, local_sem, *rest):
        x, y, _ = _place()
        pltpu.make_async_copy(s_ref, l_ref.at[2 * x + y], local_sem.at[0]).wait()
        sends, recvs = _half_gather_copies(s_ref, l_ref, send_sems, recv_sems, True)
        for cp in sends:
            cp.wait_send()
        for cp in recvs:
            cp.wait_recv()

    out = pl.pallas_call(
        body, name=name, out_shape=[pltpu.HBM(shard.shape, shard.dtype), pltpu.HBM(land.shape, land.dtype)],
        in_specs=[HBM, HBM] + [SEM] * 3 + [BS(memory_space=pl.ANY)] * len(after), out_specs=[HBM, HBM],
        input_output_aliases={0: 0, 1: 1},
        compiler_params=pltpu.CompilerParams(has_side_effects=_SIDE_EFFECT))(shard, land, *sems, *after)
    return out[1]


def pass_halves_to_sibling(land, name):
    def body(l_in, l_ref, send_sems, recv_sems):
        x, y, c = _place()
        copies = []
        for j, (px, py) in enumerate(_other_chips(x, y)):
            q = 2 * px + py
            give = pltpu.make_async_remote_copy(src_ref=l_ref.at[q, c], dst_ref=l_ref.at[q, c], send_sem=send_sems.at[j],
                                                recv_sem=recv_sems.at[j], device_id=(x, y, 1 - c), device_id_type=MESH)
            take = pltpu.make_async_remote_copy(src_ref=l_ref.at[q, c], dst_ref=l_ref.at[q, 1 - c], send_sem=send_sems.at[j],
                                                recv_sem=recv_sems.at[j], device_id=(x, y, 1 - c), device_id_type=MESH)
            give.start()
            copies.append((give, take))
        for give, take in copies:
            take.wait_recv()
            give.wait_send()

    return pl.pallas_call(
        body, name=name, in_specs=[ANY], out_specs=ANY, out_shape=jax.ShapeDtypeStruct(land.shape, land.dtype),
        input_output_aliases={0: 0},
        scratch_shapes=[pltpu.SemaphoreType.DMA((3,)), pltpu.SemaphoreType.DMA((3,))])(land)


def allgather_devices(block, name):
    R, C = block.shape

    def body(b_ref, o_ref, send_sems, recv_sems, local_sem):
        x, y, c = _place()
        me = 4 * x + 2 * y + c
        own = pltpu.make_async_copy(b_ref, o_ref.at[me], local_sem)
        own.start()
        copies = []
        for r in range(1, 8):
            px = 1 - x if r & 4 else x
            py = 1 - y if r & 2 else y
            pc = 1 - c if r & 1 else c
            send = pltpu.make_async_remote_copy(src_ref=b_ref, dst_ref=o_ref.at[me], send_sem=send_sems.at[r - 1],
                                                recv_sem=recv_sems.at[r - 1], device_id=(px, py, pc), device_id_type=MESH)
            recv = pltpu.make_async_remote_copy(src_ref=b_ref, dst_ref=o_ref.at[4 * px + 2 * py + pc], send_sem=send_sems.at[r - 1],
                                                recv_sem=recv_sems.at[r - 1], device_id=(px, py, pc), device_id_type=MESH)
            send.start()
            copies.append((send, recv))
        for send, recv in copies:
            recv.wait_recv()
            send.wait_send()
        own.wait()

    return pl.pallas_call(
        body, name=name, in_specs=[ANY], out_specs=ANY, out_shape=jax.ShapeDtypeStruct((8, R, C), block.dtype),
        scratch_shapes=[pltpu.SemaphoreType.DMA((7,)), pltpu.SemaphoreType.DMA((7,)), pltpu.SemaphoreType.DMA(())])(block)


def swap_sibling(arrs, name, splits=None):
    n = len(arrs)

    def sent(a_ref, i, c):
        return a_ref if splits is None else a_ref.at[(slice(None),) + _half(splits[i], 1 - c)]

    def out_shape(a, i):
        if splits is None:
            return a.shape
        axis, size = splits[i]
        return (a.shape[0], size, a.shape[2]) if axis == 0 else (a.shape[0], a.shape[1], size)

    def body(*refs):
        a_refs, o_refs = refs[:n], refs[n:2 * n]
        send_sems, recv_sems = refs[2 * n:]
        x, y, c = _place()
        copies = [pltpu.make_async_remote_copy(src_ref=sent(a_ref, i, c), dst_ref=o_ref, send_sem=send_sems.at[i],
                                               recv_sem=recv_sems.at[i], device_id=(x, y, 1 - c), device_id_type=MESH)
                  for i, (a_ref, o_ref) in enumerate(zip(a_refs, o_refs))]
        for cp in copies:
            cp.start()
        for cp in copies:
            cp.wait()

    return pl.pallas_call(
        body, name=name, in_specs=[ANY] * n, out_specs=[ANY] * n,
        out_shape=[jax.ShapeDtypeStruct(out_shape(a, i), a.dtype) for i, a in enumerate(arrs)],
        scratch_shapes=[pltpu.SemaphoreType.DMA((n,)), pltpu.SemaphoreType.DMA((n,))])(*arrs)


def _exchange_copies(p_refs, l_refs, send_sems, recv_sems):
    x, y, c = _place()
    return [pltpu.make_async_remote_copy(src_ref=p_ref.at[2 * px + py], dst_ref=l_ref.at[j], send_sem=send_sems.at[3 * i + j],
                                         recv_sem=recv_sems.at[3 * i + j], device_id=(px, py, c), device_id_type=MESH)
            for i, (p_ref, l_ref) in enumerate(zip(p_refs, l_refs)) for j, (px, py) in enumerate(_other_chips(x, y))]


def exchange_chips_start(parts, name):
    n = len(parts)

    def body(*refs):
        send_sems, recv_sems = refs[2 * n:2 * n + 2]
        for cp in _exchange_copies(refs[:n], refs[n:2 * n], send_sems, recv_sems):
            cp.start()
        refs[-1][...] = jnp.zeros_like(refs[-1])

    lands = [lax.empty((3,) + p.shape[1:], p.dtype) for p in parts]
    hbm_like = lambda a: pltpu.HBM(a.shape, a.dtype)
    out = pl.pallas_call(
        body, name=name,
        out_shape=[pltpu.SemaphoreType.DMA((3 * n,)), pltpu.SemaphoreType.DMA((3 * n,))]
        + [hbm_like(p) for p in parts] + [hbm_like(l) for l in lands] + [jax.ShapeDtypeStruct((8, 128), F32)],
        in_specs=[HBM] * (2 * n), out_specs=[SEM] * 2 + [HBM] * (2 * n) + [BS(memory_space=pltpu.VMEM)],
        input_output_aliases={i: 2 + i for i in range(2 * n)},
        compiler_params=pltpu.CompilerParams(has_side_effects=_SIDE_EFFECT))(*[_IN_HBM(p) for p in parts], *[_IN_HBM(l) for l in lands])
    return out[:2], out[2:2 + n], out[2 + n:2 + 2 * n], out[-1]


def exchange_chips_wait(sems, parts, lands, after, name):
    n = len(parts)

    def body(*refs):
        send_sems, recv_sems = refs[2 * n:2 * n + 2]
        for cp in _exchange_copies(refs[:n], refs[n:2 * n], send_sems, recv_sems):
            cp.wait_send()
            cp.wait_recv()

    hbm_like = lambda a: pltpu.HBM(a.shape, a.dtype)
    out = pl.pallas_call(
        body, name=name, out_shape=[hbm_like(p) for p in parts] + [hbm_like(l) for l in lands],
        in_specs=[HBM] * (2 * n) + [SEM] * 2 + [BS(memory_space=pl.ANY)], out_specs=[HBM] * (2 * n),
        input_output_aliases={i: i for i in range(2 * n)},
        compiler_params=pltpu.CompilerParams(has_side_effects=_SIDE_EFFECT))(*parts, *lands, *sems, after)
    return out[n:]


def _half_block(shape2, split):
    axis, size = split
    return (size, shape2[1]) if axis == 0 else (shape2[0], size)


def add_pairs(parts, halves, splits, core, name):
    n = len(parts)

    def body(s_ref, *refs):
        for a_ref, b_ref, o_ref in zip(refs[:n], refs[n:2 * n], refs[2 * n:]):
            o_ref[...] = (a_ref[...].astype(F32) + b_ref[...].astype(F32)).astype(BF16)

    def mine(i):
        blk = (None,) + _half_block(parts[i].shape[1:], splits[i])
        if splits[i][0] == 0:
            return BS(blk, lambda q, s: (q, s[0], 0))
        return BS(blk, lambda q, s: (q, 0, s[0]))

    half_specs = [BS((None,) + h.shape[1:], lambda q, s: (q, 0, 0)) for h in halves]
    return pl.pallas_call(
        body, name=name,
        grid_spec=pltpu.PrefetchScalarGridSpec(num_scalar_prefetch=1, grid=(N_CHIPS,),
                                               in_specs=[mine(i) for i in range(n)] + half_specs, out_specs=half_specs),
        out_shape=[jax.ShapeDtypeStruct(h.shape, BF16) for h in halves], compiler_params=_arb(1))(core, *parts, *halves)


def add_fives(parts, halves, from_chips, splits, chip_core, name):
    n = len(parts)

    def body(s_ref, *refs):
        for a_ref, b_ref, p_ref, o_ref in zip(refs[:n], refs[n:2 * n], refs[2 * n:3 * n], refs[3 * n:]):
            s = a_ref[...].astype(F32) + b_ref[...].astype(F32)
            for j in range(3):
                s = s + p_ref[j].astype(F32)
            o_ref[...] = s

    def mine(i):
        blk = (None,) + _half_block(parts[i].shape[1:], splits[i])
        if splits[i][0] == 0:
            return BS(blk, lambda g, s: (s[0], s[1], 0))
        return BS(blk, lambda g, s: (s[0], 0, s[1]))

    half_specs = [BS((None,) + h.shape[1:], lambda g, s: (s[0], 0, 0)) for h in halves]
    chip_specs = [BS(p.shape, lambda g, s: (0, 0, 0)) for p in from_chips]
    out_specs = [BS(h.shape[1:], lambda g, s: (0, 0)) for h in halves]
    return pl.pallas_call(
        body, name=name,
        grid_spec=pltpu.PrefetchScalarGridSpec(num_scalar_prefetch=1, grid=(1,),
                                               in_specs=[mine(i) for i in range(n)] + half_specs + chip_specs, out_specs=out_specs),
        out_shape=[jax.ShapeDtypeStruct(h.shape[1:], F32) for h in halves], compiler_params=_arb(1))(chip_core, *parts, *halves, *from_chips)


def sum_leading(a, name):
    def body(a_ref, o_ref):
        s = a_ref[0]
        for j in range(1, a.shape[0]):
            s = s + a_ref[j]
        o_ref[...] = s

    return pl.pallas_call(body, name=name, out_shape=jax.ShapeDtypeStruct(a.shape[1:], a.dtype))(a)


def _adamw_math(w, g, m, v):
    mn = ADAM_B1 * m + (1.0 - ADAM_B1) * g
    vn = ADAM_B2 * v + (1.0 - ADAM_B2) * (g * g)
    m_hat = mn / (1.0 - ADAM_B1 ** ADAM_STEP)
    v_hat = vn / (1.0 - ADAM_B2 ** ADAM_STEP)
    return -ADAM_LR * (m_hat / (jnp.sqrt(v_hat) + ADAM_EPS) + ADAM_WD * w), mn, vn


def adamw(w, g, m, v, name):
    R, C = g.shape
    lead = (None,) * (w.ndim - 2)

    def body(w_ref, g_ref, m_ref, v_ref, d_ref, mo_ref, vo_ref):
        d_ref[...], mo_ref[...], vo_ref[...] = _adamw_math(w_ref[...], g_ref[...], m_ref[...], v_ref[...])

    wblk = BS(lead + (R, C), lambda i: (0,) * w.ndim)
    gblk = BS((R, C), lambda i: (0, 0))
    return pl.pallas_call(
        body, name=name, grid=(1,), in_specs=[wblk, gblk, wblk, wblk], out_specs=[wblk] * 3,
        out_shape=[jax.ShapeDtypeStruct(w.shape, F32)] * 3, compiler_params=_arb(1))(w, g, m, v)


def adamw_halves(w, mine, other, m, v, split, core, name):
    R, C = w.shape[-2:]
    axis, size = split
    lead = (None,) * (w.ndim - 2)
    zeros = (0,) * (w.ndim - 2)
    if axis == 0:
        tr = size if size <= 256 else next(t for t in range(256, 7, -1) if size % t == 0 and t % 8 == 0)
        nb = size // tr
        whole = BS(lead + (tr, C), lambda hi, j, s: zeros + (hi * nb + j, 0))
        part = BS((tr, C), lambda hi, j, s: (j, 0))
    else:
        nb = size // 128
        whole = BS(lead + (R, 128), lambda hi, j, s: zeros + (0, hi * nb + j))
        part = BS((R, 128), lambda hi, j, s: (0, j))

    def body(s_ref, w_ref, a_ref, b_ref, m_ref, v_ref, g_ref, d_ref, mo_ref, vo_ref):
        g = jnp.where(pl.program_id(0) == s_ref[0], a_ref[...], b_ref[...])
        g_ref[...] = g
        d_ref[...], mo_ref[...], vo_ref[...] = _adamw_math(w_ref[...], g, m_ref[...], v_ref[...])

    return pl.pallas_call(
        body, name=name,
        grid_spec=pltpu.PrefetchScalarGridSpec(num_scalar_prefetch=1, grid=(2, nb),
                                               in_specs=[whole, part, part, whole, whole], out_specs=[whole] * 4),
        out_shape=[jax.ShapeDtypeStruct(w.shape, F32)] * 4, compiler_params=_arb(2))(core, w, mine, other, m, v)


def dense_bf16(w3, name):
    R, _, K = w3.shape
    kh = K // 2

    def body(w_hbm, o_ref, buf, sem):
        cp = pltpu.make_async_copy(w_hbm.at[:, 0], buf, sem)
        cp.start()
        cp.wait()
        o_ref[0] = buf[:, :kh].astype(BF16)
        o_ref[1] = buf[:, kh:].astype(BF16)

    return pl.pallas_call(
        body, name=name, in_specs=[ANY], out_specs=BS(memory_space=pltpu.VMEM), out_shape=jax.ShapeDtypeStruct((2, R, kh), BF16),
        scratch_shapes=[pltpu.VMEM((R, K), F32), pltpu.SemaphoreType.DMA(())])(w3)


ROW_BLOCK = 184


def adamw_untiled_rows(w3, mine, other, m3, v3, name):
    R, _, K = w3.shape
    kh = K // 2
    starts = list(range(0, R, ROW_BLOCK))
    sizes = [min(ROW_BLOCK, R - s) for s in starts]
    nblk = len(starts)

    def body(w_hbm, a_ref, b_ref, m_hbm, v_hbm, g_hbm, d_hbm, mo_hbm, vo_hbm,
             wbuf, mbuf, vbuf, gbuf, dbuf, mobuf, vobuf, in_sems, out_sems):
        first = lax.axis_index("c") == 0
        ins = []
        for k, (r0, n) in enumerate(zip(starts, sizes)):
            rows = pl.ds(r0, n)
            cps = [pltpu.make_async_copy(src.at[rows, 0], dst.at[rows], in_sems.at[3 * k + i])
                   for i, (src, dst) in enumerate(((w_hbm, wbuf), (m_hbm, mbuf), (v_hbm, vbuf)))]
            for cp in cps:
                cp.start()
            ins.append(cps)

        def update(rows):
            a, b = a_ref[rows, :], b_ref[rows, :]
            g = jnp.concatenate([jnp.where(first, a, b), jnp.where(first, b, a)], axis=1)
            gbuf[rows, :] = g
            dbuf[rows, :], mobuf[rows, :], vobuf[rows, :] = _adamw_math(wbuf[rows, :], g, mbuf[rows, :], vbuf[rows, :])

        outs = []
        for k, (r0, n) in enumerate(zip(starts, sizes)):
            for cp in ins[k]:
                cp.wait()
            groups, tail = n // 8, n % 8

            def group(i, carry, r0=r0):
                update(pl.ds(pl.multiple_of(r0 + i * 8, 8), 8))
                return carry

            lax.fori_loop(0, groups, group, 0)
            if tail:
                update(pl.ds(r0 + groups * 8, tail))
            rows = pl.ds(r0, n)
            cps = [pltpu.make_async_copy(src.at[rows], dst.at[rows, 0], out_sems.at[4 * k + i])
                   for i, (src, dst) in enumerate(((gbuf, g_hbm), (dbuf, d_hbm), (mobuf, mo_hbm), (vobuf, vo_hbm)))]
            for cp in cps:
                cp.start()
            outs += cps
        for cp in outs:
            cp.wait()

    vmem = BS(memory_space=pltpu.VMEM)
    return pl.pallas_call(
        body, name=name, in_specs=[ANY, vmem, vmem, ANY, ANY], out_specs=[ANY] * 4,
        out_shape=[jax.ShapeDtypeStruct(w3.shape, F32)] * 4,
        scratch_shapes=[pltpu.VMEM((R, K), F32)] * 7 + [pltpu.SemaphoreType.DMA((3 * nblk,)), pltpu.SemaphoreType.DMA((4 * nblk,))])(
            w3, mine, other, m3, v3)


def adamw_w_q_b(w, mine, other, m, v, name):
    def body(w_ref, a_ref, b_ref, m_ref, v_ref, g_ref, d_ref, mo_ref, vo_ref):
        first = lax.axis_index("c") == 0
        lo = jnp.where(first, a_ref[...], b_ref[...])
        hi = jnp.where(first, b_ref[...], a_ref[...])
        g = jnp.concatenate([lo, hi[0:32], hi[64:96]], axis=0)
        g_ref[...] = g
        d_ref[...], mo_ref[...], vo_ref[...] = _adamw_math(w_ref[...], g, m_ref[...], v_ref[...])

    return pl.pallas_call(body, name=name, out_shape=[jax.ShapeDtypeStruct(w.shape, F32)] * 4)(w, mine, other, m, v)


def local_step(x, mem, positions, tgt, norm_in, weights, big_grads_ready, q_a_norm, kv_a_norm, gdn_conv, gdn_a_log,
               gdn_dt_bias, gdn_norm, mem_norm, norm_final):
    B, S, D = x.shape
    M = mem.shape[1]
    T = B * S
    N = S // CHUNK
    x2d = x.reshape(T, D)
    mem2d = mem.reshape(B * M, D)
    tgt2d = tgt.reshape(T, D)

    alog_row, dt_row = _lane_row(gdn_a_log), _lane_row(gdn_dt_bias)

    half = MLA_ROPE // 2
    inv_freq = 1.0 / (ROPE_THETA ** (jnp.arange(half, dtype=F32) / half))
    z32 = jnp.zeros((half,), F32)
    o32 = jnp.ones((half,), F32)
    inv_row = jnp.concatenate([inv_freq, z32, inv_freq, z32]).reshape(1, 128)
    sgn_row = jnp.concatenate([-o32, z32, o32, z32]).reshape(1, 128)
    msk_row = jnp.concatenate([o32, z32, o32, z32]).reshape(1, 128)
    cos_t, sin_t = rope_tables(positions.reshape(T, 1), inv_row, sgn_row, msk_row, after=weights[3])

    h = rms_fwd(x2d, norm_in, "rms_in", after=weights[3])
    wp, behind = weights[0]((h, cos_t))
    P = mm(h, wp, "nt", F32, "in_proj", bm=512, bn=1536, n_outer=True, after=behind)
    wq, wkv = weights[1](P)
    Q, K, V, qn, kvn = mla_prep(P, q_a_norm, kv_a_norm, wq, wkv, cos_t, sin_t)
    o_mla, lse = mla_attn_fwd(Q, K, V, B, S)
    qkv = gdn_prep_fwd(P, gdn_conv, B, S)
    GB = gdn_gate_fwd(P, alog_row, dt_row, B, S)
    Grow = jnp.transpose(GB[:, :N_HEADS].reshape(B, N, CHUNK, N_HEADS), (0, 3, 1, 2))
    U, W, Tinv, A = gdn_chunk_fwd(qkv, GB, Grow, B, S)
    qkv3, GB3 = qkv.reshape(B, S, GDN_QKV), GB.reshape(B, S, 128)
    W3 = W.reshape(B, S, 512)
    o_gdn3, Vn3, St = gdn_scan_fwd(qkv3, U.reshape(B, S, 512), W3, GB3, A, B, S)
    o_gdn = o_gdn3.reshape(T, 512)
    w_mem_kv, w_out = weights[2](o_gdn)
    memn = rms_fwd(mem2d, mem_norm, "rms_mem")
    MKV = mm(memn, w_mem_kv, "nn", BF16, "mem_kv_proj")
    o_mem = mem_attn_fwd(P, MKV, B, S, M)
    mixed, dx2, dx2b, sq, g_norm_final = merge_fwd(o_mla, o_gdn, o_mem, P, x2d, tgt2d, w_out, gdn_norm, norm_final.reshape(1, D))

    g_w_out = mm(mixed, dx2b, "tn", BF16, "grad_w_out")
    dgate, do_mla, do_gdn, do_mem, g_gdn_norm = merge_bwd(dx2b, o_mla, o_gdn, o_mem, P, w_out, gdn_norm)

    dmemq, dMKV = mem_attn_bwd(P, MKV, do_mem, B, S, M)
    g_w_mem_kv = mm(memn, dMKV, "tn", BF16, "grad_w_mem_kv")
    started_early = big_grads_ready(dict(w_mem_kv=g_w_mem_kv, w_out=g_w_out), "early")
    dmemn = mm(dMKV, w_mem_kv, "nt", F32, "d_memn", after=(started_early,))
    g_mem_norm = gain_grad(mem2d, dmemn, "grad_mem_norm")

    dU3, dW3, dQ13, dK13, dA, dG13 = gdn_scan_bwd(do_gdn.reshape(B, S, 512), qkv3, W3, Vn3, GB3, A, St, B, S)
    r2 = lambda a: a.reshape(T, a.shape[-1])
    dqkv, dGB = gdn_chunk_bwd(qkv, GB, Grow, Tinv, dA, r2(dU3), r2(dW3), r2(dQ13), r2(dK13), r2(dG13), B, S)
    dPg, g_conv = gdn_prep_bwd(P, dqkv, gdn_conv, B, S)
    dab, g_ab = gdn_gate_bwd(P, dGB, alog_row, dt_row, B, S)

    dQ, dK, dV = mla_attn_bwd(Q, K, V, o_mla, do_mla, lse, B, S)
    dq_lin, dkv_lin, dkr = mla_post_bwd(dQ, dK, dV, cos_t, sin_t)
    dqn = mm(dq_lin, wq, "nn", F32, "d_qn")
    dkvn = mm(dkv_lin, wkv, "nt", F32, "d_kvn")
    g_wq = mm(dq_lin, qn, "tn", BF16, "grad_w_q_b")
    g_wkv = mm(kvn, dkv_lin, "tn", BF16, "grad_w_kv_b")
    dPm, g_q_a_norm, g_kv_a_norm = mla_norm_bwd(P, dqn, dkvn, dkr, dab, q_a_norm, kv_a_norm)

    dP = [dPm, dmemq, dPg, dgate]
    g_wp = mm_cols_tn(dP, h, BF16, "grad_w_in")
    started = big_grads_ready(dict(w_in=g_wp, w_q_b=g_wq, w_kv_b=g_wkv), "late")
    grad_x, g_norm_in = in_proj_bwd(dP, wp, x2d, dx2, norm_in, started)

    grads = dict(
        norm_in=g_norm_in, q_a_norm=g_q_a_norm, kv_a_norm=g_kv_a_norm, gdn_conv=g_conv,
        gdn_a_log=g_ab[0:1, :N_HEADS], gdn_dt_bias=g_ab[1:2, :N_HEADS], gdn_norm=g_gdn_norm,
        mem_norm=g_mem_norm, norm_final=g_norm_final)
    return sq, grad_x.reshape(B, S, D), grads


def kernel(x, mem, positions, norm_in, w_in, q_a_norm, w_q_b, kv_a_norm, w_kv_b, gdn_conv, gdn_a_log, gdn_dt_bias, gdn_norm, mem_norm, w_mem_kv, w_out, norm_final, loss_target, m_norm_in, m_w_in, m_q_a_norm, m_w_q_b, m_kv_a_norm, m_w_kv_b, m_gdn_conv, m_gdn_a_log, m_gdn_dt_bias, m_gdn_norm, m_mem_norm, m_w_mem_kv, m_w_out, m_norm_final, v_norm_in, v_w_in, v_q_a_norm, v_w_q_b, v_kv_a_norm, v_w_kv_b, v_gdn_conv, v_gdn_a_log, v_gdn_dt_bias, v_gdn_norm, v_mem_norm, v_w_mem_kv, v_w_out, v_norm_final):
    B = x.shape[0]
    cx, cy, cc = lax.axis_index("x"), lax.axis_index("y"), lax.axis_index("c")
    chip = 2 * cx + cy

    big_names = ("w_in", "w_q_b", "w_kv_b", "w_mem_kv", "w_out")
    rows_major = lambda a: jnp.transpose(a, (2, 0, 1))
    w_in3, m_in3, v_in3 = rows_major(w_in), rows_major(m_w_in), rows_major(v_w_in)
    w_qb_t, m_qb_t, v_qb_t = jnp.transpose(w_q_b[0]), jnp.transpose(m_w_q_b[0]), jnp.transpose(v_w_q_b[0])
    z32 = jnp.zeros((32, Q_LORA), BF16)
    qb_bf = w_qb_t.astype(BF16)
    qb_padded = jnp.concatenate([qb_bf[:160], z32, qb_bf[160:], z32])
    shards = [dense_bf16(w_in3, "w_in_bf16"), qb_padded, w_kv_b[0].astype(BF16), w_mem_kv[0].astype(BF16), w_out[0].astype(BF16)]
    splits = [(1, D_MODEL // 2)] + [(0, s.shape[0] // 2) for s in shards[1:]]
    *w_in_flight, w_in_started = half_gather_start(shards[0], "w_in_gather_start")
    conv_all = allgather_devices(gdn_conv[0], "allgather_conv")
    conv_cols = gdn_conv.shape[2]
    conv_full = jnp.transpose(conv_all[0::2], (1, 0, 2)).reshape(GDN_CONV, N_CHIPS * conv_cols)
    late_shapes = [(N_CHIPS,) + s.shape for s in shards[1:]]
    late = {}

    def w_in_ready(after):
        g_in = pass_halves_to_sibling(half_gather_wait(*w_in_flight, after, "w_in_gather_wait"), "w_in_gather_sibling")
        *late["a"], started_a = late_gather_start(shards[1:3], g_in, "late_gather_qkv_start")
        *late["b"], started_b = late_gather_start(shards[3:], started_a, "late_gather_mem_out_start")
        return pad_w_in_t(g_in), (started_a, started_b)

    def late_qkv(after):
        g_qb, g_kvb = late_gather_wait(*late["a"], after, "late_gather_qkv_wait")
        return g_qb.reshape(-1, Q_LORA), _perm_w_kv_b(g_kvb)

    def late_mem_out(after):
        g_mem, g_out_w = late_gather_wait(*late["b"], after, "late_gather_mem_out_wait")
        return g_mem.reshape(-1, g_mem.shape[2]), g_out_w.reshape(-1, g_out_w.shape[2])

    weights = (w_in_ready, late_qkv, late_mem_out, (w_in_started,))

    core = jnp.stack([cc]).astype(jnp.int32)
    chip_core = jnp.stack([chip, cc]).astype(jnp.int32)
    exchanges = {}
    by_chip = dict(w_in=unpad_w_in_t, w_q_b=lambda a: a.reshape(late_shapes[0]), w_kv_b=_unperm_w_kv_b,
                   w_mem_kv=lambda a: a.reshape(late_shapes[2]), w_out=lambda a: a.reshape(late_shapes[3]))

    def big_grads_ready(gb, group):
        idx = [big_names.index(n) for n in gb]
        parts = [by_chip[n](a) for n, a in gb.items()]
        sp = [splits[i] for i in idx]
        from_sibling = swap_sibling(parts, "rs_sibling_partial_" + group, sp)
        chip_sums = add_pairs(parts, from_sibling, sp, core, "rs_add_sibling_" + group)
        sems, sums_thru, lands, token = exchange_chips_start(chip_sums, "rs_exchange_start_" + group)
        exchanges[group] = dict(idx=idx, parts=parts, from_sibling=from_sibling, sems=sems, sums=sums_thru, lands=lands)
        return token

    sq, grad_x, g = local_step(x, mem, positions, loss_target, norm_in, weights, big_grads_ready, q_a_norm, kv_a_norm, conv_full,
                               gdn_a_log, gdn_dt_bias, gdn_norm, mem_norm, norm_final)

    small_names = ("norm_in", "q_a_norm", "kv_a_norm", "gdn_a_log", "gdn_dt_bias", "gdn_norm", "mem_norm", "norm_final")
    small = dict(norm_in=norm_in, q_a_norm=q_a_norm, kv_a_norm=kv_a_norm, gdn_a_log=gdn_a_log, gdn_dt_bias=gdn_dt_bias,
                 gdn_norm=gdn_norm, mem_norm=mem_norm, norm_final=norm_final)
    m_small = dict(norm_in=m_norm_in, q_a_norm=m_q_a_norm, kv_a_norm=m_kv_a_norm, gdn_a_log=m_gdn_a_log,
                   gdn_dt_bias=m_gdn_dt_bias, gdn_norm=m_gdn_norm, mem_norm=m_mem_norm, norm_final=m_norm_final)
    v_small = dict(norm_in=v_norm_in, q_a_norm=v_q_a_norm, kv_a_norm=v_kv_a_norm, gdn_a_log=v_gdn_a_log,
                   gdn_dt_bias=v_gdn_dt_bias, gdn_norm=v_gdn_norm, mem_norm=v_mem_norm, norm_final=v_norm_final)
    rows = lambda d: jnp.stack([jnp.pad(d[n].reshape(-1), (0, 1024 - d[n].size)) for n in small_names])
    conv_rows = GDN_CONV * GDN_QKV // 1024
    g_block = jnp.concatenate([rows(g), g["gdn_conv"].reshape(conv_rows, 1024), sq, jnp.zeros((16 - 9 - conv_rows, 1024), F32)])
    g_block = sum_leading(allgather_devices(g_block, "allgather_small_grads"), "sum_small_grads")
    g_small_rows = g_block[:8]
    loss = 0.5 * jnp.sum(g_block[8 + conv_rows]) / D_MODEL
    g_conv = lax.dynamic_slice_in_dim(g_block[8:8 + conv_rows].reshape(GDN_CONV, GDN_QKV), chip * conv_cols, conv_cols, axis=1)
    d_s, m_s, v_s = adamw(rows(small), g_small_rows, rows(m_small), rows(v_small), "adamw_small")
    unrow = lambda r: {n: r[i, :small[n].size].reshape(small[n].shape) for i, n in enumerate(small_names)}
    g_out, d_out, m_out, v_out = unrow(g_small_rows), unrow(d_s), unrow(m_s), unrow(v_s)

    my_half = [None] * len(big_names)
    for group, e in exchanges.items():
        from_chips = exchange_chips_wait(e["sems"], e["sums"], e["lands"], d_s, "rs_exchange_wait_" + group)
        halves = add_fives(e["parts"], e["from_sibling"], from_chips, [splits[i] for i in e["idx"]], chip_core, "rs_add_chips_" + group)
        for i, a in zip(e["idx"], halves):
            my_half[i] = a
    other_half = swap_sibling(my_half, "rs_sibling_final")

    d_out["gdn_conv"], m_out["gdn_conv"], v_out["gdn_conv"] = adamw(gdn_conv, g_conv, m_gdn_conv, v_gdn_conv, "adamw_gdn_conv")
    g_out["gdn_conv"] = g_conv[None]
    res = adamw_untiled_rows(w_in3, my_half[0], other_half[0], m_in3, v_in3, "adamw_w_in")
    g_out["w_in"], d_out["w_in"], m_out["w_in"], v_out["w_in"] = [jnp.transpose(r, (1, 2, 0)) for r in res]
    res = adamw_w_q_b(w_qb_t, my_half[1], other_half[1], m_qb_t, v_qb_t, "adamw_w_q_b")
    g_out["w_q_b"], d_out["w_q_b"], m_out["w_q_b"], v_out["w_q_b"] = [jnp.transpose(r)[None] for r in res]
    rest = dict(w_kv_b=(w_kv_b, m_w_kv_b, v_w_kv_b), w_mem_kv=(w_mem_kv, m_w_mem_kv, v_w_mem_kv), w_out=(w_out, m_w_out, v_w_out))
    for i, n in enumerate(big_names):
        if n in rest:
            w_n, m_n, v_n = rest[n]
            g_out[n], d_out[n], m_out[n], v_out[n] = adamw_halves(w_n, my_half[i], other_half[i], m_n, v_n, splits[i], core, "adamw_" + n)

    order = ("norm_in", "w_in", "q_a_norm", "w_q_b", "kv_a_norm", "w_kv_b", "gdn_conv", "gdn_a_log", "gdn_dt_bias",
             "gdn_norm", "mem_norm", "w_mem_kv", "w_out", "norm_final")
    return (loss, grad_x, *[g_out[n] for n in order], *[d_out[n] for n in order], *[m_out[n] for n in order],
            *[v_out[n] for n in order])
```

```python
import functools
import math

import jax
import jax.numpy as jnp
import numpy as np
from jax import lax
from jax.experimental import pallas as pl
from jax.experimental.pallas import tpu as pltpu

F32 = jnp.float32
BF16 = jnp.bfloat16
BS = pl.BlockSpec

D_MODEL = 1024
N_HEADS = 4
MLA_NOPE, MLA_ROPE, MLA_V = 128, 64, 128
Q_LORA, KV_LORA = 384, 256
ROPE_THETA = 10000.0
GDN_DK = GDN_DV = 128
GDN_CONV = 4
CHUNK = 64
MEM_DH = 128
D_MIX = 1536
GDN_QKV = 1536
D_IN = 4296
EPS = 1e-6
ADAM_LR, ADAM_B1, ADAM_B2, ADAM_EPS, ADAM_WD, ADAM_STEP = 0.001, 0.9, 0.999, 1e-08, 0.01, 10

OFF_MLA = 0
OFF_MEMQ = 1024
OFF_GDN = 1536
OFF_GATE = 3072
N_PAD = 4608
HEAD_PAD = 256
MLA_SCALE = (MLA_NOPE + MLA_ROPE) ** -0.5
MEM_SCALE = MEM_DH ** -0.5
GDN_SCALE = GDN_DK ** -0.5
NEG = -1e30

NN = ((1,), (0,))
NT = ((1,), (1,))
TN = ((0,), (0,))


def _dot(a, b, dims):
    return lax.dot_general(a, b, (dims, ((), ())), preferred_element_type=F32)


def _bdot(spec, a, b, precision=None):
    return jnp.einsum(spec, a, b, preferred_element_type=F32, precision=precision)


def _arb(n):
    return pltpu.CompilerParams(dimension_semantics=("arbitrary",) * n)


def _sigmoid(x):
    return 1.0 / (1.0 + jnp.exp(-x))


def _softplus(z):
    return jnp.maximum(z, 0.0) + jnp.log(1.0 + jnp.exp(-jnp.abs(z)))


def _rope(t, cos_row, sin_row):
    return t * cos_row + pltpu.roll(t, 64, 1) * sin_row


def _rope_bwd(d, cos_row, sin_row):
    return d * cos_row + pltpu.roll(d * sin_row, 64, 1)


def rms_fwd(x, gain, name, tm=512, after=()):
    T, n = x.shape
    tm = min(tm, T)

    def body(x_ref, g_ref, *rest):
        xv = x_ref[...]
        r = lax.rsqrt(jnp.mean(xv * xv, axis=-1, keepdims=True) + EPS)
        rest[-1][...] = (xv * r * g_ref[...]).astype(BF16)

    return pl.pallas_call(
        body, name=name, grid=(T // tm,),
        in_specs=[BS((tm, n), lambda i: (i, 0)), BS((1, n), lambda i: (0, 0))] + [BS(memory_space=pl.ANY)] * len(after),
        out_specs=BS((tm, n), lambda i: (i, 0)),
        out_shape=jax.ShapeDtypeStruct((T, n), BF16), compiler_params=_arb(1))(x, gain, *after)


def mm(a, b, kind, out_dtype, name, bm=512, bn=None, n_outer=False, after=()):
    if kind == "nn":
        (M, K), (_, N) = a.shape, b.shape
    elif kind == "nt":
        (M, K), (N, _) = a.shape, b.shape
    else:
        (K, M), (_, N) = a.shape, b.shape
    bm, bn = min(bm, M), min(bn or N, N)
    assert M % bm == 0 and N % bn == 0, (name, M, N, K)
    ij = (lambda g0, g1: (g1, g0)) if n_outer else (lambda g0, g1: (g0, g1))
    a_spec = BS((K, bm), lambda g0, g1: (0, ij(g0, g1)[0])) if kind == "tn" else BS((bm, K), lambda g0, g1: (ij(g0, g1)[0], 0))
    once = dict(pipeline_mode=pl.Buffered(1)) if bn == N else {}
    b_spec = (BS((bn, K), lambda g0, g1: (ij(g0, g1)[1], 0), **once) if kind == "nt"
              else BS((K, bn), lambda g0, g1: (0, ij(g0, g1)[1]), **once))
    dims = {"nn": NN, "nt": NT, "tn": TN}[kind]

    def body(a_ref, b_ref, *rest):
        rest[-1][...] = _dot(a_ref[...].astype(BF16), b_ref[...].astype(BF16), dims).astype(out_dtype)

    grid = (N // bn, M // bm) if n_outer else (M // bm, N // bn)
    return pl.pallas_call(
        body, name=name, grid=grid, in_specs=[a_spec, b_spec] + [BS(memory_space=pl.ANY)] * len(after),
        out_specs=BS((bm, bn), lambda g0, g1: ij(g0, g1)),
        out_shape=jax.ShapeDtypeStruct((M, N), out_dtype), compiler_params=_arb(2))(a, b, *after)


def mm_cols_tn(pieces, b, out_dtype, name, bm=512):
    K, N = b.shape
    tiles = [p.shape[1] // bm for p in pieces]
    firsts = [sum(tiles[:i]) for i in range(len(tiles))]

    def body(*refs):
        b_ref, o_ref = refs[-2], refs[-1]
        i = pl.program_id(0)
        for a_ref, t0, n in zip(refs[:-2], firsts, tiles):
            @pl.when((i >= t0) & (i < t0 + n))
            def _(a_ref=a_ref):
                o_ref[...] = _dot(a_ref[...], b_ref[...], TN).astype(out_dtype)

    a_specs = [BS((K, bm), lambda i, t0=t0, n=n: (0, jnp.clip(i - t0, 0, n - 1))) for t0, n in zip(firsts, tiles)]
    return pl.pallas_call(
        body, name=name, grid=(sum(tiles),),
        in_specs=a_specs + [BS(b.shape, lambda i: (0, 0), pipeline_mode=pl.Buffered(1))],
        out_specs=BS((bm, N), lambda i: (i, 0)), out_shape=jax.ShapeDtypeStruct((sum(tiles) * bm, N), out_dtype),
        compiler_params=_arb(1))(*pieces, b)


def rope_tables(pos_col, inv_row, sgn_row, msk_row, tm=512, after=()):
    T = pos_col.shape[0]
    tm = min(tm, T)

    def body(p_ref, inv_ref, sgn_ref, msk_ref, *rest):
        c_ref, s_ref = rest[-2:]
        ang = p_ref[...].astype(F32) * inv_ref[...]
        c_ref[...] = jnp.cos(ang) * msk_ref[...]
        s_ref[...] = jnp.sin(ang) * sgn_ref[...]

    row = BS((1, 128), lambda i: (0, 0))
    return pl.pallas_call(
        body, name="rope_tables", grid=(T // tm,),
        in_specs=[BS((tm, 1), lambda i: (i, 0)), row, row, row] + [BS(memory_space=pl.ANY)] * len(after),
        out_specs=[BS((tm, 128), lambda i: (i, 0))] * 2,
        out_shape=[jax.ShapeDtypeStruct((T, 128), F32)] * 2, compiler_params=_arb(1))(pos_col, inv_row, sgn_row, msk_row, *after)


def mla_prep(P, gq, gkv, wq, wkv, cos_t, sin_t, tm=512):
    T = P.shape[0]
    tm = min(tm, T)

    def body(p_ref, gq_ref, gkv_ref, wq_ref, wkv_ref, c_ref, s_ref, q_ref, k_ref, v_ref, qn_ref, kvn_ref):
        p = p_ref[...].astype(F32)
        cq, ckv, kr = p[:, :Q_LORA], p[:, Q_LORA:Q_LORA + KV_LORA], p[:, 640:768]
        qn = (cq * lax.rsqrt(jnp.mean(cq * cq, axis=-1, keepdims=True) + EPS) * gq_ref[...]).astype(BF16)
        kvn = (ckv * lax.rsqrt(jnp.mean(ckv * ckv, axis=-1, keepdims=True) + EPS) * gkv_ref[...]).astype(BF16)
        qn_ref[...] = qn
        kvn_ref[...] = kvn
        q = _dot(qn, wq_ref[...], NT)
        kv = _dot(kvn, wkv_ref[...], NN)
        cos_row, sin_row = c_ref[...], s_ref[...]
        krr = _rope(kr, cos_row, sin_row).astype(BF16)
        for h in range(N_HEADS):
            lo = h * HEAD_PAD
            q_ref[:, lo:lo + 128] = (q[:, lo:lo + 128] * MLA_SCALE).astype(BF16)
            q_ref[:, lo + 128:lo + 256] = (_rope(q[:, lo + 128:lo + 256], cos_row, sin_row) * MLA_SCALE).astype(BF16)
            k_ref[:, lo:lo + 128] = kv[:, h * 128:(h + 1) * 128].astype(BF16)
            k_ref[:, lo + 128:lo + 256] = krr
            v_ref[:, lo:lo + 128] = kv[:, 512 + h * 128:512 + (h + 1) * 128].astype(BF16)
            v_ref[:, lo + 128:lo + 256] = jnp.ones((tm, 128), BF16)

    full = lambda r, c: BS((r, c), lambda i: (0, 0))
    rowb = lambda c: BS((tm, c), lambda i: (i, 0))
    return pl.pallas_call(
        body, name="mla_prep", grid=(T // tm,),
        in_specs=[rowb(1024), full(1, Q_LORA), full(1, KV_LORA), full(1024, Q_LORA), full(KV_LORA, 1024), rowb(128), rowb(128)],
        out_specs=[rowb(1024), rowb(1024), rowb(1024), rowb(Q_LORA), rowb(KV_LORA)],
        out_shape=[jax.ShapeDtypeStruct((T, 1024), BF16), jax.ShapeDtypeStruct((T, 1024), BF16),
                   jax.ShapeDtypeStruct((T, 1024), BF16), jax.ShapeDtypeStruct((T, Q_LORA), BF16),
                   jax.ShapeDtypeStruct((T, KV_LORA), BF16)],
        compiler_params=_arb(1))(P, gq, gkv, wq, wkv, cos_t, sin_t)


ATTN_HEADS_PER_STEP = 2
ATTN_STRIP = 32


def mla_attn_fwd(Q, K, V, B, S, tq=512, hp=ATTN_HEADS_PER_STEP):
    T = B * S
    tq = min(tq, S)
    nq = S // tq

    rs = min(ATTN_STRIP, tq)

    def body(q_ref, k_ref, v_ref, o_ref, lse_ref, m_s, acc_s, s_s, p_s, a_s):
        i = pl.program_id(2)
        m_s[...] = jnp.full_like(m_s, NEG)
        acc_s[...] = jnp.zeros_like(acc_s)

        def blk(j, masked):
            rows = pl.ds(pl.multiple_of(j * tq, tq), tq)
            for h in range(hp):
                hq = slice(h * HEAD_PAD, (h + 1) * HEAD_PAD)
                s_s[h] = _dot(q_ref[:, hq], k_ref[rows, hq], NT)
            for h in range(hp):
                for r0 in range(0, tq, rs):
                    rr = slice(r0, r0 + rs)
                    sv = s_s[h, rr, :]
                    if masked:
                        r = r0 + lax.broadcasted_iota(jnp.int32, (rs, tq), 0)
                        c = lax.broadcasted_iota(jnp.int32, (rs, tq), 1)
                        sv = jnp.where(r >= c, sv, NEG)
                    m_prev = m_s[h, rr, :]
                    m_new = jnp.maximum(m_prev, jnp.max(sv, axis=1, keepdims=True))
                    p_s[h, rr, :] = jnp.exp(sv - m_new).astype(BF16)
                    a_s[h, rr, :] = jnp.exp(m_prev - m_new)
                    m_s[h, rr, :] = m_new
            for h in range(hp):
                hq = slice(h * HEAD_PAD, (h + 1) * HEAD_PAD)
                acc_s[h] = a_s[h] * acc_s[h] + _dot(p_s[h], v_ref[rows, hq], NN)

        def loop(j, c):
            blk(j, False)
            return c

        lax.fori_loop(0, i, loop, 0)
        blk(i, True)
        for h in range(hp):
            den = acc_s[h, :, 128:256]
            o_ref[:, h * 128:(h + 1) * 128] = acc_s[h, :, 0:128] / den
            lse_ref[h] = m_s[h] + jnp.log(den[:, 0:1])

    return pl.pallas_call(
        body, name="mla_attn_fwd", grid=(B, N_HEADS // hp, nq),
        in_specs=[BS((tq, hp * HEAD_PAD), lambda b, h, i: (b * nq + i, h)),
                  BS((S, hp * HEAD_PAD), lambda b, h, i: (b, h)),
                  BS((S, hp * HEAD_PAD), lambda b, h, i: (b, h))],
        out_specs=[BS((tq, hp * 128), lambda b, h, i: (b * nq + i, h)),
                   BS((hp, tq, 1), lambda b, h, i: (h, b * nq + i, 0))],
        out_shape=[jax.ShapeDtypeStruct((T, 512), F32), jax.ShapeDtypeStruct((N_HEADS, T, 1), F32)],
        scratch_shapes=[pltpu.VMEM((hp, tq, 1), F32), pltpu.VMEM((hp, tq, HEAD_PAD), F32), pltpu.VMEM((hp, tq, tq), F32),
                        pltpu.VMEM((hp, tq, tq), BF16), pltpu.VMEM((hp, tq, 1), F32)],
        compiler_params=_arb(3))(Q, K, V)


def mla_attn_bwd(Q, K, V, O, dO, LSE, B, S, tq=512, hp=ATTN_HEADS_PER_STEP):
    T = B * S
    tq = min(tq, S)
    nq = S // tq

    rs = min(ATTN_STRIP, tq)

    def body(q_ref, k_ref, v_ref, o_ref, do_ref, lse_ref, dq_ref, dk_ref, dv_ref, delta_s, dk_s, dv_s, s_s, dp_s, p_s, ds_s):
        j = pl.program_id(2)

        @pl.when(j == 0)
        def _():
            dq_ref[...] = jnp.zeros_like(dq_ref)
            for h in range(hp):
                sl = slice(h * 128, (h + 1) * 128)
                delta_s[h] = jnp.sum(do_ref[:, sl] * o_ref[:, sl], axis=1, keepdims=True)

        dk_s[...] = jnp.zeros_like(dk_s)
        dv_s[...] = jnp.zeros_like(dv_s)

        def step(i, c):
            rows = pl.ds(pl.multiple_of(i * tq, tq), tq)
            for h in range(hp):
                sq, sv = slice(h * HEAD_PAD, (h + 1) * HEAD_PAD), slice(h * 128, (h + 1) * 128)
                s_s[h] = _dot(q_ref[rows, sq], k_ref[:, sq], NT)
                dp_s[h] = _dot(do_ref[rows, sv].astype(BF16), v_ref[:, h * HEAD_PAD:h * HEAD_PAD + 128], NT)
            for h in range(hp):
                for r0 in range(0, tq, rs):
                    rr = slice(r0, r0 + rs)
                    seq_rows = pl.ds(pl.multiple_of(i * tq + r0, rs), rs)
                    r = i * tq + r0 + lax.broadcasted_iota(jnp.int32, (rs, tq), 0)
                    cc = j * tq + lax.broadcasted_iota(jnp.int32, (rs, tq), 1)
                    p = jnp.where(r >= cc, jnp.exp(s_s[h, rr, :] - lse_ref[h, seq_rows, :]), 0.0)
                    p_s[h, rr, :] = p.astype(BF16)
                    ds_s[h, rr, :] = (p * (dp_s[h, rr, :] - delta_s[h, seq_rows, :])).astype(BF16)
            for h in range(hp):
                sq, sv = slice(h * HEAD_PAD, (h + 1) * HEAD_PAD), slice(h * 128, (h + 1) * 128)
                dv_s[:, sv] += _dot(p_s[h], do_ref[rows, sv].astype(BF16), TN)
                dk_s[:, sq] += _dot(ds_s[h], q_ref[rows, sq], TN)
                dq_ref[rows, sq] += _dot(ds_s[h], k_ref[:, sq], NN)
            return c

        lax.fori_loop(j, nq, step, 0)
        dk_ref[...] = dk_s[...]
        dv_ref[...] = dv_s[...]

    seq = lambda c: BS((S, c), lambda b, h, j: (b, h))
    blk = lambda c: BS((tq, c), lambda b, h, j: (b * nq + j, h))
    return pl.pallas_call(
        body, name="mla_attn_bwd", grid=(B, N_HEADS // hp, nq),
        in_specs=[seq(hp * HEAD_PAD), blk(hp * HEAD_PAD), blk(hp * HEAD_PAD), seq(hp * 128), seq(hp * 128),
                  BS((hp, S, 1), lambda b, h, j: (h, b, 0))],
        out_specs=[seq(hp * HEAD_PAD), blk(hp * HEAD_PAD), blk(hp * 128)],
        out_shape=[jax.ShapeDtypeStruct((T, 1024), F32), jax.ShapeDtypeStruct((T, 1024), F32),
                   jax.ShapeDtypeStruct((T, 512), F32)],
        scratch_shapes=[pltpu.VMEM((hp, S, 1), F32), pltpu.VMEM((tq, hp * HEAD_PAD), F32), pltpu.VMEM((tq, hp * 128), F32),
                        pltpu.VMEM((hp, tq, tq), F32), pltpu.VMEM((hp, tq, tq), F32),
                        pltpu.VMEM((hp, tq, tq), BF16), pltpu.VMEM((hp, tq, tq), BF16)],
        compiler_params=_arb(3))(Q, K, V, O, dO, LSE)


def mla_post_bwd(dQ, dK, dV, cos_t, sin_t, tm=512):
    T = dQ.shape[0]
    tm = min(tm, T)

    def body(dq_ref, dk_ref, dv_ref, c_ref, s_ref, ql_ref, kvl_ref, kr_ref):
        cos_row, sin_row = c_ref[...], s_ref[...]
        kr = jnp.zeros((tm, 128), F32)
        for h in range(N_HEADS):
            lo = h * HEAD_PAD
            ql_ref[:, lo:lo + 128] = (dq_ref[:, lo:lo + 128] * MLA_SCALE).astype(BF16)
            ql_ref[:, lo + 128:lo + 256] = (_rope_bwd(dq_ref[:, lo + 128:lo + 256], cos_row, sin_row) * MLA_SCALE).astype(BF16)
            kvl_ref[:, h * 128:(h + 1) * 128] = dk_ref[:, lo:lo + 128].astype(BF16)
            kr = kr + dk_ref[:, lo + 128:lo + 256]
        kvl_ref[:, 512:] = dv_ref[...].astype(BF16)
        kr_ref[...] = _rope_bwd(kr, cos_row, sin_row)

    rowb = lambda c: BS((tm, c), lambda i: (i, 0))
    return pl.pallas_call(
        body, name="mla_post_bwd", grid=(T // tm,),
        in_specs=[rowb(1024), rowb(1024), rowb(512), rowb(128), rowb(128)],
        out_specs=[rowb(1024), rowb(1024), rowb(128)],
        out_shape=[jax.ShapeDtypeStruct((T, 1024), BF16), jax.ShapeDtypeStruct((T, 1024), BF16),
                   jax.ShapeDtypeStruct((T, 128), F32)],
        compiler_params=_arb(1))(dQ, dK, dV, cos_t, sin_t)


def mla_norm_bwd(P, dqn, dkvn, dkr, dab, gq, gkv, tm=512):
    T = P.shape[0]
    tm = min(tm, T)

    def norm_bwd(x, dy, g):
        r = lax.rsqrt(jnp.mean(x * x, axis=-1, keepdims=True) + EPS)
        xh = x * r
        dxh = dy * g
        return r * (dxh - xh * jnp.mean(dxh * xh, axis=-1, keepdims=True)), jnp.sum(dy * xh, axis=0, keepdims=True)

    def body(p_ref, dqn_ref, dkvn_ref, dkr_ref, dab_ref, gq_ref, gkv_ref, o_ref, aq_ref, akv_ref):
        @pl.when(pl.program_id(0) == 0)
        def _():
            aq_ref[...] = jnp.zeros_like(aq_ref)
            akv_ref[...] = jnp.zeros_like(akv_ref)

        dcq, ggq = norm_bwd(p_ref[:, :Q_LORA].astype(F32), dqn_ref[...], gq_ref[...])
        dckv, ggkv = norm_bwd(p_ref[:, Q_LORA:640].astype(F32), dkvn_ref[...], gkv_ref[...])
        aq_ref[...] += ggq
        akv_ref[...] += ggkv
        o_ref[:, :Q_LORA] = dcq.astype(BF16)
        o_ref[:, Q_LORA:640] = dckv.astype(BF16)
        o_ref[:, 640:768] = dkr_ref[...].astype(BF16)
        o_ref[:, 768:896] = dab_ref[...]
        o_ref[:, 896:1024] = jnp.zeros((tm, 128), BF16)

    rowb = lambda c: BS((tm, c), lambda i: (i, 0))
    full = lambda c: BS((1, c), lambda i: (0, 0))
    return pl.pallas_call(
        body, name="mla_norm_bwd", grid=(T // tm,),
        in_specs=[rowb(1024), rowb(Q_LORA), rowb(KV_LORA), rowb(128), rowb(128), full(Q_LORA), full(KV_LORA)],
        out_specs=[rowb(1024), full(Q_LORA), full(KV_LORA)],
        out_shape=[jax.ShapeDtypeStruct((T, 1024), BF16), jax.ShapeDtypeStruct((1, Q_LORA), F32),
                   jax.ShapeDtypeStruct((1, KV_LORA), F32)],
        compiler_params=_arb(1))(P, dqn, dkvn, dkr, dab, gq, gkv)


def _mem_probs(qh, kh):
    s = _dot(qh, kh, NT) * MEM_SCALE
    p = jnp.exp(s - jnp.max(s, axis=1, keepdims=True))
    return p / jnp.sum(p, axis=1, keepdims=True)


def mem_attn_fwd(P, MKV, B, S, M, tq=512):
    T = B * S
    tq = min(tq, S)
    nq = S // tq

    def body(q_ref, kv_ref, o_ref):
        for h in range(N_HEADS):
            sl = slice(h * 128, (h + 1) * 128)
            p = _mem_probs(q_ref[:, sl].astype(BF16), kv_ref[:, sl])
            o_ref[:, sl] = _dot(p.astype(BF16), kv_ref[:, 512 + h * 128:512 + (h + 1) * 128], NN)

    return pl.pallas_call(
        body, name="mem_attn_fwd", grid=(B, nq),
        in_specs=[BS((tq, 512), lambda b, i: (b * nq + i, OFF_MEMQ // 512)), BS((M, 1024), lambda b, i: (b, 0))],
        out_specs=BS((tq, 512), lambda b, i: (b * nq + i, 0)),
        out_shape=jax.ShapeDtypeStruct((T, 512), F32), compiler_params=_arb(2))(P, MKV)


def mem_attn_bwd(P, MKV, dO, B, S, M, tq=512):
    T = B * S
    tq = min(tq, S)
    nq = S // tq

    def body(q_ref, kv_ref, do_ref, dq_ref, dkv_ref):
        @pl.when(pl.program_id(1) == 0)
        def _():
            dkv_ref[...] = jnp.zeros_like(dkv_ref)

        for h in range(N_HEADS):
            sl = slice(h * 128, (h + 1) * 128)
            sv = slice(512 + h * 128, 512 + (h + 1) * 128)
            qh = q_ref[:, sl].astype(BF16)
            kh = kv_ref[:, sl]
            do = do_ref[:, sl].astype(BF16)
            p = _mem_probs(qh, kh)
            dkv_ref[:, sv] += _dot(p.astype(BF16), do, TN)
            dp = _dot(do, kv_ref[:, sv], NT)
            ds = (p * (dp - jnp.sum(dp * p, axis=1, keepdims=True)) * MEM_SCALE).astype(BF16)
            dq_ref[:, sl] = _dot(ds, kh, NN).astype(BF16)
            dkv_ref[:, sl] += _dot(ds, qh, TN)

    return pl.pallas_call(
        body, name="mem_attn_bwd", grid=(B, nq),
        in_specs=[BS((tq, 512), lambda b, i: (b * nq + i, OFF_MEMQ // 512)), BS((M, 1024), lambda b, i: (b, 0)),
                  BS((tq, 512), lambda b, i: (b * nq + i, 0))],
        out_specs=[BS((tq, 512), lambda b, i: (b * nq + i, 0)), BS((M, 1024), lambda b, i: (b, 0))],
        out_shape=[jax.ShapeDtypeStruct((T, 512), BF16), jax.ShapeDtypeStruct((B * M, 1024), F32)],
        compiler_params=_arb(2))(P, MKV, dO)


def gain_grad(x, dy, name, tm=256):
    T, n = x.shape
    tm = min(tm, T)

    def body(x_ref, dy_ref, o_ref):
        @pl.when(pl.program_id(0) == 0)
        def _():
            o_ref[...] = jnp.zeros_like(o_ref)

        xv = x_ref[...]
        xh = xv * lax.rsqrt(jnp.mean(xv * xv, axis=-1, keepdims=True) + EPS)
        o_ref[...] += jnp.sum(dy_ref[...] * xh, axis=0, keepdims=True)

    return pl.pallas_call(
        body, name=name, grid=(T // tm,),
        in_specs=[BS((tm, n), lambda i: (i, 0))] * 2, out_specs=BS((1, n), lambda i: (0, 0)),
        out_shape=jax.ShapeDtypeStruct((1, n), F32), compiler_params=_arb(1))(x, dy)


def _conv_silu(x, w, t):
    y = x * w[3:4, :]
    for s in range(1, GDN_CONV):
        y = y + jnp.where(t >= s, pltpu.roll(x, s, 0), 0.0) * w[3 - s:4 - s, :]
    return y, _sigmoid(y)


def gdn_prep_fwd(P, conv_w, B, S):
    T = B * S

    def body(x_ref, w_ref, o_ref):
        kind = pl.program_id(1)
        t = lax.broadcasted_iota(jnp.int32, (S, 1), 0)
        y, sg = _conv_silu(x_ref[...].astype(F32), w_ref[...], t)
        a = y * sg
        scale = jnp.where(kind == 0, GDN_SCALE, 1.0).astype(F32)
        for h in range(N_HEADS):
            sl = slice(h * 128, (h + 1) * 128)
            seg = a[:, sl]
            n = lax.rsqrt(jnp.sum(seg * seg, axis=-1, keepdims=True) + EPS)
            o_ref[:, sl] = jnp.where(kind < 2, seg * (n * scale), seg)

    return pl.pallas_call(
        body, name="gdn_prep_fwd", grid=(B, 3),
        in_specs=[BS((S, 512), lambda b, k: (b, OFF_GDN // 512 + k)), BS((GDN_CONV, 512), lambda b, k: (0, k))],
        out_specs=BS((S, 512), lambda b, k: (b, k)),
        out_shape=jax.ShapeDtypeStruct((T, GDN_QKV), F32), compiler_params=_arb(2))(P, conv_w)


def gdn_prep_bwd(P, dqkv, conv_w, B, S):
    T = B * S

    def body(x_ref, d_ref, w_ref, o_ref, gw_ref):
        kind = pl.program_id(0)

        @pl.when(pl.program_id(1) == 0)
        def _():
            gw_ref[...] = jnp.zeros_like(gw_ref)

        t = lax.broadcasted_iota(jnp.int32, (S, 1), 0)
        x = x_ref[...].astype(F32)
        w = w_ref[...]
        y, sg = _conv_silu(x, w, t)
        a = y * sg
        scale = jnp.where(kind == 0, GDN_SCALE, 1.0).astype(F32)
        das = []
        for h in range(N_HEADS):
            sl = slice(h * 128, (h + 1) * 128)
            seg, dseg = a[:, sl], d_ref[:, sl]
            n = lax.rsqrt(jnp.sum(seg * seg, axis=-1, keepdims=True) + EPS)
            dn = scale * (n * dseg - seg * (n * n * n) * jnp.sum(dseg * seg, axis=-1, keepdims=True))
            das.append(jnp.where(kind < 2, dn, dseg))
        dy = jnp.concatenate(das, axis=1) * (sg * (1.0 + y * (1.0 - sg)))
        dx = dy * w[3:4, :]
        gw_ref[3:4, :] += jnp.sum(dy * x, axis=0, keepdims=True)
        for s in range(1, GDN_CONV):
            dx = dx + jnp.where(t + s < S, pltpu.roll(dy, S - s, 0), 0.0) * w[3 - s:4 - s, :]
            gw_ref[3 - s:4 - s, :] += jnp.sum(dy * jnp.where(t >= s, pltpu.roll(x, s, 0), 0.0), axis=0, keepdims=True)
        o_ref[...] = dx.astype(BF16)

    return pl.pallas_call(
        body, name="gdn_prep_bwd", grid=(3, B),
        in_specs=[BS((S, 512), lambda k, b: (b, OFF_GDN // 512 + k)), BS((S, 512), lambda k, b: (b, k)),
                  BS((GDN_CONV, 512), lambda k, b: (0, k))],
        out_specs=[BS((S, 512), lambda k, b: (b, k)), BS((GDN_CONV, 512), lambda k, b: (0, k))],
        out_shape=[jax.ShapeDtypeStruct((T, GDN_QKV), BF16), jax.ShapeDtypeStruct((GDN_CONV, GDN_QKV), F32)],
        compiler_params=_arb(2))(P, dqkv, conv_w)


def _chunk_row(n_rows):
    return lax.broadcasted_iota(jnp.int32, (n_rows, 1), 0) % CHUNK


def gdn_gate_fwd(P, alog_row, dt_row, B, S):
    T = B * S

    def body(x_ref, al_ref, dt_ref, o_ref):
        x = x_ref[...].astype(F32)
        lane = lax.broadcasted_iota(jnp.int32, (1, 128), 1)
        g = jnp.where(lane < 4, -jnp.exp(al_ref[...]) * _softplus(x + dt_ref[...]), 0.0)
        t = _chunk_row(S)
        for s in (1, 2, 4, 8, 16, 32):
            g = g + jnp.where(t >= s, pltpu.roll(g, s, 0), 0.0)
        o_ref[...] = jnp.where(lane < 4, g, jnp.where(lane < 8, _sigmoid(x), 0.0))

    row = BS((1, 128), lambda b: (0, 0))
    return pl.pallas_call(
        body, name="gdn_gate_fwd", grid=(B,),
        in_specs=[BS((S, 128), lambda b: (b, 768 // 128)), row, row], out_specs=BS((S, 128), lambda b: (b, 0)),
        out_shape=jax.ShapeDtypeStruct((T, 128), F32), compiler_params=_arb(1))(P, alog_row, dt_row)


def gdn_gate_bwd(P, dGB, alog_row, dt_row, B, S):
    T = B * S

    def body(x_ref, d_ref, al_ref, dt_ref, o_ref, acc_ref):
        @pl.when(pl.program_id(0) == 0)
        def _():
            acc_ref[...] = jnp.zeros_like(acc_ref)

        x, d = x_ref[...].astype(F32), d_ref[...]
        lane = lax.broadcasted_iota(jnp.int32, (1, 128), 1)
        z = x + dt_ref[...]
        coef = -jnp.exp(al_ref[...])
        g = coef * _softplus(z)
        da = jnp.where(lane < 4, d * coef * _sigmoid(z), 0.0)
        beta = _sigmoid(x)
        o_ref[...] = jnp.where(lane < 4, da, jnp.where(lane < 8, d * beta * (1.0 - beta), 0.0)).astype(BF16)
        acc_ref[0:1, :] += jnp.sum(jnp.where(lane < 4, d * g, 0.0), axis=0, keepdims=True)
        acc_ref[1:2, :] += jnp.sum(da, axis=0, keepdims=True)

    row = BS((1, 128), lambda b: (0, 0))
    return pl.pallas_call(
        body, name="gdn_gate_bwd", grid=(B,),
        in_specs=[BS((S, 128), lambda b: (b, 768 // 128)), BS((S, 128), lambda b: (b, 0)), row, row],
        out_specs=[BS((S, 128), lambda b: (b, 0)), BS((8, 128), lambda b: (0, 0))],
        out_shape=[jax.ShapeDtypeStruct((T, 128), BF16), jax.ShapeDtypeStruct((8, 128), F32)],
        compiler_params=_arb(1))(P, dGB, alog_row, dt_row)


def _chunk_masks(nc):
    r = lax.broadcasted_iota(jnp.int32, (nc, CHUNK, CHUNK), 1)
    c = lax.broadcasted_iota(jnp.int32, (nc, CHUNK, CHUNK), 2)
    return r >= c, r > c


def _chunk_local(q, k, gc, gr, beta, incl, strict):
    decay = jnp.exp(jnp.where(incl, gc - gr, NEG))
    kb = k * beta
    kbf = k.astype(BF16)
    m_kk = _bdot("gcd,gjd->gcj", kb.astype(BF16), kbf)
    l_mat = jnp.where(strict, m_kk * decay, 0.0)
    a_mat = _bdot("gcd,gjd->gcj", q.astype(BF16), kbf) * decay
    return decay, kb, l_mat, a_mat


def _split_bf16(x):
    hi = x.astype(BF16)
    return hi, (x - hi.astype(F32)).astype(BF16)


def _mm_split(ah, al, bh, bl):
    spec = "gij,gjk->gik"
    return _bdot(spec, ah, bh) + (_bdot(spec, ah, bl) + _bdot(spec, al, bh))


def gdn_chunk_fwd(qkv, GB, Grow, B, S, nc=8):
    T = B * S
    N = S // CHUNK
    nc = min(nc, N)
    nb = N // nc
    R = nc * CHUNK

    def body(q_ref, k_ref, v_ref, gb_ref, gr_ref, u_ref, w_ref, t_ref, a_ref):
        incl, strict = _chunk_masks(nc)
        eye = (lax.broadcasted_iota(jnp.int32, (nc, CHUNK, CHUNK), 1)
               == lax.broadcasted_iota(jnp.int32, (nc, CHUNK, CHUNK), 2)).astype(F32)
        for h in range(N_HEADS):
            sl = slice(h * 128, (h + 1) * 128)
            q = q_ref[:, sl].reshape(nc, CHUNK, 128)
            k = k_ref[:, sl].reshape(nc, CHUNK, 128)
            v = v_ref[:, sl].reshape(nc, CHUNK, 128)
            gc = gb_ref[:, h:h + 1].reshape(nc, CHUNK, 1)
            beta = gb_ref[:, 4 + h:5 + h].reshape(nc, CHUNK, 1)
            gr = gr_ref[h][:, None, :]
            _, kb, l_mat, a_mat = _chunk_local(q, k, gc, gr, beta, incl, strict)
            pw = -l_mat
            tinv = eye + pw
            for _ in range(5):
                ph, pl_ = _split_bf16(pw)
                pw = _mm_split(ph, pl_, ph, pl_)
                ph, pl_ = _split_bf16(pw)
                th, tl = _split_bf16(tinv)
                tinv = tinv + _mm_split(th, tl, ph, pl_)
            tb = tinv.astype(BF16)
            u = _bdot("gcj,gjv->gcv", tb, (v * beta).astype(BF16))
            w = _bdot("gcj,gjk->gck", tb, (kb * jnp.exp(gc)).astype(BF16))
            u_ref[:, sl] = u.reshape(R, 128)
            w_ref[:, sl] = w.reshape(R, 128).astype(BF16)
            t_ref[h] = tb
            a_ref[h] = a_mat.astype(BF16)

    rowb = lambda c, j: BS((R, c), lambda b, n: (b * nb + n, j))
    mat = BS((None, N_HEADS, nc, CHUNK, CHUNK), lambda b, n: (b, 0, n, 0, 0))
    return pl.pallas_call(
        body, name="gdn_chunk_fwd", grid=(B, nb),
        in_specs=[rowb(512, 0), rowb(512, 1), rowb(512, 2), rowb(128, 0),
                  BS((None, N_HEADS, nc, CHUNK), lambda b, n: (b, 0, n, 0))],
        out_specs=[rowb(512, 0), rowb(512, 0), mat, mat],
        out_shape=[jax.ShapeDtypeStruct((T, 512), F32), jax.ShapeDtypeStruct((T, 512), BF16),
                   jax.ShapeDtypeStruct((B, N_HEADS, N, CHUNK, CHUNK), BF16),
                   jax.ShapeDtypeStruct((B, N_HEADS, N, CHUNK, CHUNK), BF16)],
        compiler_params=_arb(2))(qkv, qkv, qkv, GB, Grow)


def gdn_scan_fwd(qkv3, U3, W3, GB3, A, B, S):
    N = S // CHUNK

    def body(q_ref, k_ref, u_ref, w_ref, gb_ref, a_ref, o_ref, vn_ref, st_ref, s_s):
        @pl.when(pl.program_id(0) == 0)
        def _():
            s_s[...] = jnp.zeros_like(s_s)

        for b in range(B):
            for h in range(N_HEADS):
                sl = slice(h * 128, (h + 1) * 128)
                st = s_s[b, h]
                st_ref[b, h] = st
                stb = st.astype(BF16)
                g = gb_ref[b, :, h:h + 1]
                gl = g[CHUNK - 1:CHUNK, :]
                vn = u_ref[b, :, sl] - _dot(w_ref[b, :, sl].astype(BF16), stb, NN)
                vnb = vn.astype(BF16)
                o = _dot((q_ref[b, :, sl] * jnp.exp(g)).astype(BF16), stb, NN) + _dot(a_ref[b, h].astype(BF16), vnb, NN)
                vn_ref[b, :, sl] = vnb
                o_ref[b, :, sl] = o
                s_s[b, h] = st * jnp.exp(gl) + _dot((k_ref[b, :, sl] * jnp.exp(gl - g)).astype(BF16), vnb, TN)

    tok = lambda c, j: BS((B, CHUNK, c), lambda n: (0, n, j))
    return pl.pallas_call(
        body, name="gdn_scan_fwd", grid=(N,),
        in_specs=[tok(512, 0), tok(512, 1), tok(512, 0), tok(512, 0), tok(128, 0),
                  BS((B, N_HEADS, None, CHUNK, CHUNK), lambda n: (0, 0, n, 0, 0))],
        out_specs=[tok(512, 0), tok(512, 0), BS((B, N_HEADS, None, 128, 128), lambda n: (0, 0, n, 0, 0))],
        out_shape=[jax.ShapeDtypeStruct((B, S, 512), F32), jax.ShapeDtypeStruct((B, S, 512), BF16),
                   jax.ShapeDtypeStruct((B, N_HEADS, N, 128, 128), F32)],
        scratch_shapes=[pltpu.VMEM((B, N_HEADS, 128, 128), F32)],
        compiler_params=_arb(1))(qkv3, qkv3, U3, W3, GB3, A)


def gdn_scan_bwd(dO3, qkv3, W3, Vn3, GB3, A, St, B, S):
    N = S // CHUNK

    def body(do_ref, q_ref, k_ref, w_ref, vn_ref, gb_ref, a_ref, st_ref,
             du_ref, dw_ref, dq_ref, dk_ref, da_ref, dg_ref, ds_s):
        @pl.when(pl.program_id(0) == 0)
        def _():
            ds_s[...] = jnp.zeros_like(ds_s)

        lane = lax.broadcasted_iota(jnp.int32, (1, 128), 1)
        last = lax.broadcasted_iota(jnp.int32, (CHUNK, 1), 0) == CHUNK - 1
        for b in range(B):
            dg_all = jnp.zeros((CHUNK, 128), F32)
            for h in range(N_HEADS):
                sl = slice(h * 128, (h + 1) * 128)
                st = st_ref[b, h]
                stb = st.astype(BF16)
                dsn = ds_s[b, h]
                dsnb = dsn.astype(BF16)
                g = gb_ref[b, :, h:h + 1]
                gl = g[CHUNK - 1:CHUNK, :]
                egl = jnp.exp(gl)
                ekd = jnp.exp(gl - g)
                eg = jnp.exp(g)
                q, k = q_ref[b, :, sl], k_ref[b, :, sl]
                kd = k * ekd
                qg = q * eg
                do = do_ref[b, :, sl].astype(BF16)
                vnb = vn_ref[b, :, sl].astype(BF16)
                dvn = _dot(a_ref[b, h].astype(BF16), do, TN) + _dot(kd.astype(BF16), dsnb, NN)
                dvnb = dvn.astype(BF16)
                da_ref[b, h] = _dot(do, vnb, NT)
                dqg = _dot(do, stb, NT)
                dkd = _dot(vnb, dsnb, NT)
                ds_s[b, h] = (_dot(qg.astype(BF16), do, TN) + egl * dsn - _dot(w_ref[b, :, sl].astype(BF16), dvnb, TN))
                du_ref[b, :, sl] = dvnb
                dw_ref[b, :, sl] = (-_dot(dvnb, stb, NT)).astype(BF16)
                dq_ref[b, :, sl] = dqg * eg
                dk_ref[b, :, sl] = dkd * ekd
                ddel = jnp.sum(dkd * kd, axis=1, keepdims=True)
                dgl = jnp.sum(ddel, axis=0, keepdims=True) + jnp.sum(jnp.sum(st * dsn, axis=1, keepdims=True), axis=0, keepdims=True) * egl
                col = jnp.sum(dqg * qg, axis=1, keepdims=True) - ddel + jnp.where(last, dgl, 0.0)
                dg_all = jnp.where(lane == h, col, dg_all)
            dg_ref[b] = dg_all

    tok = lambda c, j: BS((B, CHUNK, c), lambda n: (0, N - 1 - n, j))
    mat = lambda d: BS((B, N_HEADS, None, d, d), lambda n: (0, 0, N - 1 - n, 0, 0))
    return pl.pallas_call(
        body, name="gdn_scan_bwd", grid=(N,),
        in_specs=[tok(512, 0), tok(512, 0), tok(512, 1), tok(512, 0), tok(512, 0), tok(128, 0), mat(CHUNK), mat(128)],
        out_specs=[tok(512, 0), tok(512, 0), tok(512, 0), tok(512, 0), mat(CHUNK), tok(128, 0)],
        out_shape=[jax.ShapeDtypeStruct((B, S, 512), BF16)] * 2 + [jax.ShapeDtypeStruct((B, S, 512), F32)] * 2
        + [jax.ShapeDtypeStruct((B, N_HEADS, N, CHUNK, CHUNK), F32), jax.ShapeDtypeStruct((B, S, 128), F32)],
        scratch_shapes=[pltpu.VMEM((B, N_HEADS, 128, 128), F32)],
        compiler_params=_arb(1))(dO3, qkv3, qkv3, W3, Vn3, GB3, A, St)


def gdn_chunk_bwd(qkv, GB, Grow, Tinv, dA, dU, dW, dQ1, dK1, dG1, B, S, nc=8):
    T = B * S
    N = S // CHUNK
    nc = min(nc, N)
    nb = N // nc
    R = nc * CHUNK

    def body(q_ref, k_ref, v_ref, gb_ref, gr_ref, t_ref, da_ref, du_ref, dw_ref, dq1_ref, dk1_ref, dg1_ref, o_ref, dgb_ref):
        incl, strict = _chunk_masks(nc)
        lane = lax.broadcasted_iota(jnp.int32, (1, 128), 1)
        dg_all = dg1_ref[...]
        db_all = jnp.zeros((R, 128), F32)
        for h in range(N_HEADS):
            sl = slice(h * 128, (h + 1) * 128)
            q = q_ref[:, sl].reshape(nc, CHUNK, 128)
            k = k_ref[:, sl].reshape(nc, CHUNK, 128)
            v = v_ref[:, sl].reshape(nc, CHUNK, 128)
            gc = gb_ref[:, h:h + 1].reshape(nc, CHUNK, 1)
            beta = gb_ref[:, 4 + h:5 + h].reshape(nc, CHUNK, 1)
            gr = gr_ref[h][:, None, :]
            decay, kb, l_mat, a_mat = _chunk_local(q, k, gc, gr, beta, incl, strict)
            eg = jnp.exp(gc)
            kbg = kb * eg
            vb = v * beta
            tb = t_ref[h].astype(BF16)
            du = du_ref[:, sl].reshape(nc, CHUNK, 128).astype(BF16)
            dw = dw_ref[:, sl].reshape(nc, CHUNK, 128).astype(BF16)
            dvb = _bdot("gcj,gcv->gjv", tb, du)
            dkbg = _bdot("gcj,gck->gjk", tb, dw)
            dt = _bdot("gcv,gjv->gcj", du, vb.astype(BF16)) + _bdot("gck,gjk->gcj", dw, kbg.astype(BF16))
            tmp = _bdot("gac,gab->gcb", tb, dt.astype(BF16))
            dl = jnp.where(strict, -_bdot("gcb,gdb->gcd", tmp.astype(BF16), tb), 0.0)
            da = da_ref[h]
            dm = (dl * decay).astype(BF16)
            dqk = (da * decay).astype(BF16)
            kbf = k.astype(BF16)
            dkb = _bdot("gcj,gjd->gcd", dm, kbf) + dkbg * eg
            dk = (_bdot("gcj,gcd->gjd", dm, kb.astype(BF16)) + _bdot("gcj,gcd->gjd", dqk, q.astype(BF16))
                  + dk1_ref[:, sl].reshape(nc, CHUNK, 128) + dkb * beta)
            dq = _bdot("gcj,gjd->gcd", dqk, kbf) + dq1_ref[:, sl].reshape(nc, CHUNK, 128)
            e = dl * l_mat + da * a_mat
            dgc = (jnp.sum(e, axis=2, keepdims=True) - jnp.sum(jnp.swapaxes(e, 1, 2), axis=2, keepdims=True)
                   + jnp.sum(dkbg * kbg, axis=2, keepdims=True))
            dbeta = jnp.sum(dkb * k, axis=2, keepdims=True) + jnp.sum(dvb * v, axis=2, keepdims=True)
            o_ref[:, sl] = dq.reshape(R, 128)
            o_ref[:, 512 + h * 128:512 + (h + 1) * 128] = dk.reshape(R, 128)
            o_ref[:, 1024 + h * 128:1024 + (h + 1) * 128] = (dvb * beta).reshape(R, 128)
            dg_all = dg_all + jnp.where(lane == h, dgc.reshape(R, 1), 0.0)
            db_all = jnp.where(lane == 4 + h, dbeta.reshape(R, 1), db_all)
        t = _chunk_row(R)
        for s in (1, 2, 4, 8, 16, 32):
            dg_all = dg_all + jnp.where(t + s < CHUNK, pltpu.roll(dg_all, R - s, 0), 0.0)
        dgb_ref[...] = jnp.where(lane < 4, dg_all, db_all)

    rowb = lambda c, j: BS((R, c), lambda b, n: (b * nb + n, j))
    mat = BS((None, N_HEADS, nc, CHUNK, CHUNK), lambda b, n: (b, 0, n, 0, 0))
    return pl.pallas_call(
        body, name="gdn_chunk_bwd", grid=(B, nb),
        in_specs=[rowb(512, 0), rowb(512, 1), rowb(512, 2), rowb(128, 0),
                  BS((None, N_HEADS, nc, CHUNK), lambda b, n: (b, 0, n, 0)), mat, mat,
                  rowb(512, 0), rowb(512, 0), rowb(512, 0), rowb(512, 0), rowb(128, 0)],
        out_specs=[rowb(GDN_QKV, 0), rowb(128, 0)],
        out_shape=[jax.ShapeDtypeStruct((T, GDN_QKV), F32), jax.ShapeDtypeStruct((T, 128), F32)],
        compiler_params=_arb(2))(qkv, qkv, qkv, GB, Grow, Tinv, dA, dU, dW, dQ1, dK1, dG1)


def _gdn_out_norm(og, gg):
    outs, xhs, rs = [], [], []
    for h in range(N_HEADS):
        seg = og[:, h * 128:(h + 1) * 128]
        r = lax.rsqrt(jnp.mean(seg * seg, axis=-1, keepdims=True) + EPS)
        xh = seg * r
        outs.append(xh * gg)
        xhs.append(xh)
        rs.append(r)
    return outs, xhs, rs


def merge_fwd(o_mla, o_gdn, o_mem, P, x, tgt, w_out, g_gdn, g_fin, tm=256):
    T = x.shape[0]
    tm = min(tm, T)

    def body(om_ref, og_ref, oc_ref, gate_ref, x_ref, t_ref, w_ref, gg_ref, gf_ref, mix_ref, dx_ref, dxb_ref, sq_ref, gnf_ref):
        @pl.when(pl.program_id(0) == 0)
        def _():
            sq_ref[...] = jnp.zeros_like(sq_ref)
            gnf_ref[...] = jnp.zeros_like(gnf_ref)

        ogn, _, _ = _gdn_out_norm(og_ref[...], gg_ref[...])
        cat = jnp.concatenate([om_ref[...]] + ogn + [oc_ref[...]], axis=1)
        gt = gate_ref[...].astype(F32)
        mixed = (cat * (gt * _sigmoid(gt))).astype(BF16)
        mix_ref[...] = mixed
        x2 = x_ref[...] + _dot(mixed, w_ref[...], NN)
        r2 = lax.rsqrt(jnp.mean(x2 * x2, axis=-1, keepdims=True) + EPS)
        xh = x2 * r2
        gf = gf_ref[...]
        diff = xh * gf - t_ref[...]
        sq_ref[...] += jnp.sum(diff * diff, axis=0, keepdims=True)
        dy = diff * (1.0 / D_MODEL)
        gnf_ref[...] += jnp.sum(dy * xh, axis=0, keepdims=True)
        dxh = dy * gf
        dx = r2 * (dxh - xh * jnp.mean(dxh * xh, axis=-1, keepdims=True))
        dx_ref[...] = dx
        dxb_ref[...] = dx.astype(BF16)

    rowb = lambda c, j=0: BS((tm, c), lambda i: (i, j))
    full = lambda r, c: BS((r, c), lambda i: (0, 0))
    return pl.pallas_call(
        body, name="merge_fwd", grid=(T // tm,),
        in_specs=[rowb(512), rowb(512), rowb(512), rowb(D_MIX, OFF_GATE // D_MIX), rowb(D_MODEL), rowb(D_MODEL),
                  full(D_MIX, D_MODEL), full(1, 128), full(1, D_MODEL)],
        out_specs=[rowb(D_MIX), rowb(D_MODEL), rowb(D_MODEL), full(1, D_MODEL), full(1, D_MODEL)],
        out_shape=[jax.ShapeDtypeStruct((T, D_MIX), BF16), jax.ShapeDtypeStruct((T, D_MODEL), F32),
                   jax.ShapeDtypeStruct((T, D_MODEL), BF16),
                   jax.ShapeDtypeStruct((1, D_MODEL), F32), jax.ShapeDtypeStruct((1, D_MODEL), F32)],
        compiler_params=_arb(1))(o_mla, o_gdn, o_mem, P, x, tgt, w_out, g_gdn, g_fin)


def merge_bwd(dx2, o_mla, o_gdn, o_mem, P, w_out, g_gdn, tm=256):
    T = dx2.shape[0]
    tm = min(tm, T)

    def body(dx_ref, om_ref, og_ref, oc_ref, gate_ref, w_ref, gg_ref, dgate_ref, dom_ref, dog_ref, doc_ref, ggn_ref):
        @pl.when(pl.program_id(0) == 0)
        def _():
            ggn_ref[...] = jnp.zeros_like(ggn_ref)

        gg = gg_ref[...]
        dmix = _dot(dx_ref[...].astype(BF16), w_ref[...], NT)
        ogn, xhs, rs = _gdn_out_norm(og_ref[...], gg)
        cat = jnp.concatenate([om_ref[...]] + ogn + [oc_ref[...]], axis=1)
        gt = gate_ref[...].astype(F32)
        sg = _sigmoid(gt)
        dgate_ref[...] = (dmix * cat * (sg * (1.0 + gt * (1.0 - sg)))).astype(BF16)
        dcat = dmix * (gt * sg)
        dom_ref[...] = dcat[:, :512]
        doc_ref[...] = dcat[:, 1024:]
        acc = jnp.zeros((1, 128), F32)
        for h in range(N_HEADS):
            dseg = dcat[:, 512 + h * 128:512 + (h + 1) * 128]
            acc = acc + jnp.sum(dseg * xhs[h], axis=0, keepdims=True)
            dxh = dseg * gg
            dog_ref[:, h * 128:(h + 1) * 128] = rs[h] * (dxh - xhs[h] * jnp.mean(dxh * xhs[h], axis=-1, keepdims=True))
        ggn_ref[...] += acc

    rowb = lambda c, j=0: BS((tm, c), lambda i: (i, j))
    full = lambda r, c: BS((r, c), lambda i: (0, 0))
    return pl.pallas_call(
        body, name="merge_bwd", grid=(T // tm,),
        in_specs=[rowb(D_MODEL), rowb(512), rowb(512), rowb(512), rowb(D_MIX, OFF_GATE // D_MIX),
                  full(D_MIX, D_MODEL), full(1, 128)],
        out_specs=[rowb(D_MIX), rowb(512), rowb(512), rowb(512), full(1, 128)],
        out_shape=[jax.ShapeDtypeStruct((T, D_MIX), BF16)] + [jax.ShapeDtypeStruct((T, 512), F32)] * 3
        + [jax.ShapeDtypeStruct((1, 128), F32)],
        compiler_params=_arb(1))(dx2, o_mla, o_gdn, o_mem, P, w_out, g_gdn)


def in_proj_bwd(dP, wp, x, dx2, gain, after, tm=512):
    T, n = x.shape
    tm = min(tm, T)
    k = len(dP)
    widths = [p.shape[1] for p in dP]
    offs = [sum(widths[:i]) for i in range(k)]

    def body(*refs):
        w_ref, x_ref, dx2_ref, g_ref = refs[k:k + 4]
        o_ref, acc_ref = refs[-2:]

        @pl.when(pl.program_id(0) == 0)
        def _():
            acc_ref[...] = jnp.zeros_like(acc_ref)

        dy = None
        for a_ref, off, w in zip(refs[:k], offs, widths):
            d = _dot(a_ref[...], w_ref[off:off + w, :], NN)
            dy = d if dy is None else dy + d
        xv = x_ref[...]
        r = lax.rsqrt(jnp.mean(xv * xv, axis=-1, keepdims=True) + EPS)
        xh = xv * r
        acc_ref[...] += jnp.sum(dy * xh, axis=0, keepdims=True)
        dxh = dy * g_ref[...]
        o_ref[...] = dx2_ref[...] + r * (dxh - xh * jnp.mean(dxh * xh, axis=-1, keepdims=True))

    rowb = BS((tm, n), lambda i: (i, 0))
    full = BS((1, n), lambda i: (0, 0))
    return pl.pallas_call(
        body, name="in_proj_bwd", grid=(T // tm,),
        in_specs=[BS((tm, w), lambda i: (i, 0)) for w in widths]
        + [BS(wp.shape, lambda i: (0, 0), pipeline_mode=pl.Buffered(1)), rowb, rowb, full, BS(memory_space=pl.ANY)],
        out_specs=[rowb, full], out_shape=[jax.ShapeDtypeStruct((T, n), F32), jax.ShapeDtypeStruct((1, n), F32)],
        compiler_params=_arb(1))(*dP, wp, x, dx2, gain, after)


W_IN_SHARD = D_IN // 4
_GDN0 = Q_LORA + KV_LORA + MLA_ROPE
_AB0 = _GDN0 + GDN_QKV
_MEMQ0 = _AB0 + 2 * N_HEADS
_GATE0 = _MEMQ0 + N_HEADS * MEM_DH


def _w_in_row_map():
    a, m, gt = _AB0 - 2 * W_IN_SHARD, _MEMQ0 - 2 * W_IN_SHARD, _GATE0 - 2 * W_IN_SHARD
    e0 = OFF_GDN + W_IN_SHARD - _GDN0
    e1 = e0 + W_IN_SHARD
    e2 = OFF_GATE + W_IN_SHARD - gt
    return [(0, 0, 0, 672), (0, 672, 704, 32), (2, a, 768, m - a), (2, m, OFF_MEMQ, gt - m), (0, _GDN0, OFF_GDN, W_IN_SHARD - _GDN0),
            (1, 0, e0, W_IN_SHARD), (2, 0, e1, a), (2, gt, OFF_GATE, W_IN_SHARD - gt), (3, 0, e2, W_IN_SHARD)]


_W_IN_ZERO_ROWS = [(672, 32), (736, 32), (776, 248)]
W_IN_LANES = 256


def pad_w_in_t(shards):
    per_half = shards.shape[3] // W_IN_LANES

    def body(s_ref, o_ref):
        for r0, n in _W_IN_ZERO_ROWS:
            o_ref[r0:r0 + n, :] = jnp.zeros((n, W_IN_LANES), o_ref.dtype)
        for q, src, dst, n in _w_in_row_map():
            o_ref[dst:dst + n, :] = s_ref[q, src:src + n, :]

    return pl.pallas_call(
        body, name="pad_w_in_t", grid=(D_MODEL // W_IN_LANES,),
        in_specs=[BS((N_CHIPS, None, W_IN_SHARD, W_IN_LANES), lambda j: (0, j // per_half, 0, j % per_half))],
        out_specs=BS((N_PAD, W_IN_LANES), lambda j: (0, j)),
        out_shape=jax.ShapeDtypeStruct((N_PAD, D_MODEL), shards.dtype), compiler_params=_arb(1))(shards)


def unpad_w_in_t(g):
    def body(g_ref, o_ref):
        for q, src, dst, n in _w_in_row_map():
            o_ref[q, src:src + n, :] = g_ref[dst:dst + n, :]

    return pl.pallas_call(
        body, name="unpad_w_in_t", grid=(D_MODEL // W_IN_LANES,),
        in_specs=[BS((N_PAD, W_IN_LANES), lambda j: (0, j))], out_specs=BS((N_CHIPS, W_IN_SHARD, W_IN_LANES), lambda j: (0, 0, j)),
        out_shape=jax.ShapeDtypeStruct((N_CHIPS, W_IN_SHARD, D_MODEL), g.dtype), compiler_params=_arb(1))(g)


def _pad_w_q_b_t(s):
    z = jnp.zeros((32, s.shape[2]), s.dtype)
    parts = []
    for h in range(N_HEADS):
        parts += [s[h, :128], s[h, 128:160], z, s[h, 160:192], z]
    return jnp.concatenate(parts, axis=0)


def _unpad_w_q_b_t(g):
    return jnp.stack([jnp.concatenate([g[h * HEAD_PAD:h * HEAD_PAD + 128], g[h * HEAD_PAD + 128:h * HEAD_PAD + 160],
                                       g[h * HEAD_PAD + 192:h * HEAD_PAD + 224]]) for h in range(N_HEADS)])


def _perm_w_kv_b(s):
    return jnp.concatenate([s[h, :, :128] for h in range(N_HEADS)] + [s[h, :, 128:] for h in range(N_HEADS)], axis=1)


def _unperm_w_kv_b(g):
    return jnp.stack([jnp.concatenate([g[:, h * 128:(h + 1) * 128], g[:, 512 + h * 128:512 + (h + 1) * 128]], axis=1)
                      for h in range(N_HEADS)])


def _lane_row(v4):
    return jnp.pad(v4.reshape(1, -1).astype(F32), ((0, 0), (0, 128 - v4.size)))


def _pack(pieces, n_rows):
    flat = jnp.concatenate([p.reshape(-1) for p in pieces])
    return jnp.pad(flat, (0, n_rows * 1024 - flat.size)).reshape(n_rows, 1024)


def _unpack(block, shapes):
    flat = block.reshape(-1)
    out, off = [], 0
    for shp in shapes:
        n = int(np.prod(shp))
        out.append(flat[off:off + n].reshape(shp))
        off += n
    return out


N_CHIPS = 4
MESH = pl.DeviceIdType.MESH
ANY = BS(memory_space=pl.ANY)


def _place():
    return lax.axis_index("x"), lax.axis_index("y"), lax.axis_index("c")


def _other_chips(x, y):
    return [(1 - x, y), (x, 1 - y), (1 - x, 1 - y)]


def _half(split, which):
    axis, size = split
    ds = pl.ds(pl.multiple_of(which * size, 16 if axis == 0 else 128), size)
    return (ds, slice(None)) if axis == 0 else (slice(None), ds)


SEM = BS(memory_space=pltpu.SEMAPHORE)
HBM = BS(memory_space=pltpu.HBM)
_IN_HBM = lambda a: pltpu.with_memory_space_constraint(a, pltpu.HBM)
_SIDE_EFFECT = pltpu.SideEffectType.DATAFLOW_SIDE_EFFECTING


def _late_gather_copies(s_refs, l_refs, send_sems, recv_sems, local_sems, with_arrivals):
    x, y, c = _place()
    sends, recvs, locals_ = [], [], []
    for i, (s_ref, l_ref) in enumerate(zip(s_refs, l_refs)):
        locals_.append(pltpu.make_async_copy(s_ref, l_ref.at[2 * x + y], local_sems.at[i]))
        for j, (px, py) in enumerate(_other_chips(x, y)):
            k = 3 * i + j
            sends.append(pltpu.make_async_remote_copy(src_ref=s_ref, dst_ref=l_ref.at[2 * x + y], send_sem=send_sems.at[k],
                                                      recv_sem=recv_sems.at[k], device_id=(px, py, c), device_id_type=MESH))
            if with_arrivals:
                recvs.append(pltpu.make_async_remote_copy(src_ref=s_ref, dst_ref=l_ref.at[2 * px + py], send_sem=send_sems.at[k],
                                                          recv_sem=recv_sems.at[k], device_id=(px, py, c), device_id_type=MESH))
    return sends, recvs, locals_


def late_gather_start(shards, after, name):
    n = len(shards)

    def body(*refs):
        s_refs, l_refs = refs[:n], refs[n:2 * n]
        send_sems, recv_sems, local_sems = refs[2 * n + 1:2 * n + 4]
        token = refs[-1]
        sends, _, locals_ = _late_gather_copies(s_refs, l_refs, send_sems, recv_sems, local_sems, False)
        for cp in locals_ + sends:
            cp.start()
        token[...] = jnp.zeros_like(token)

    lands = [lax.empty((N_CHIPS,) + s.shape, s.dtype) for s in shards]
    hbm_like = lambda a: pltpu.HBM(a.shape, a.dtype)
    out = pl.pallas_call(
        body, name=name,
        out_shape=[pltpu.SemaphoreType.DMA((3 * n,)), pltpu.SemaphoreType.DMA((3 * n,)), pltpu.SemaphoreType.DMA((n,))]
        + [hbm_like(s) for s in shards] + [hbm_like(l) for l in lands] + [jax.ShapeDtypeStruct((8, 128), F32)],
        in_specs=[HBM] * (2 * n) + [BS(memory_space=pl.ANY)], out_specs=[SEM] * 3 + [HBM] * (2 * n) + [BS(memory_space=pltpu.VMEM)],
        input_output_aliases={i: 3 + i for i in range(2 * n)},
        compiler_params=pltpu.CompilerParams(has_side_effects=_SIDE_EFFECT))(
            *[_IN_HBM(s) for s in shards], *[_IN_HBM(l) for l in lands], after)
    return out[:3], out[3:3 + n], out[3 + n:3 + 2 * n], out[-1]


def late_gather_wait(sems, shards, lands, after, name):
    n = len(shards)

    def body(*refs):
        s_refs, l_refs = refs[:n], refs[n:2 * n]
        send_sems, recv_sems, local_sems = refs[2 * n:2 * n + 3]
        sends, recvs, locals_ = _late_gather_copies(s_refs, l_refs, send_sems, recv_sems, local_sems, True)
        for cp in locals_:
            cp.wait()
        for cp in sends:
            cp.wait_send()
        for cp in recvs:
            cp.wait_recv()

    hbm_like = lambda a: pltpu.HBM(a.shape, a.dtype)
    out = pl.pallas_call(
        body, name=name, out_shape=[hbm_like(s) for s in shards] + [hbm_like(l) for l in lands],
        in_specs=[HBM] * (2 * n) + [SEM] * 3 + [BS(memory_space=pl.ANY)], out_specs=[HBM] * (2 * n),
        input_output_aliases={i: i for i in range(2 * n)},
        compiler_params=pltpu.CompilerParams(has_side_effects=_SIDE_EFFECT))(*shards, *lands, *sems, after)
    return out[n:]


def _half_gather_copies(s_ref, l_ref, send_sems, recv_sems, with_arrivals):
    x, y, c = _place()
    sends, recvs = [], []
    for j, (px, py) in enumerate(_other_chips(x, y)):
        sends.append(pltpu.make_async_remote_copy(src_ref=s_ref.at[c], dst_ref=l_ref.at[2 * x + y, c], send_sem=send_sems.at[j],
                                                  recv_sem=recv_sems.at[j], device_id=(px, py, c), device_id_type=MESH))
        if with_arrivals:
            recvs.append(pltpu.make_async_remote_copy(src_ref=s_ref.at[c], dst_ref=l_ref.at[2 * px + py, c], send_sem=send_sems.at[j],
                                                      recv_sem=recv_sems.at[j], device_id=(px, py, c), device_id_type=MESH))
    return sends, recvs


def half_gather_start(shard, name):
    def body(s_ref, l_ref, send_sems, recv_sems, local_sem, s_thru, l_thru, token):
        x, y, _ = _place()
        pltpu.make_async_copy(s_ref, l_ref.at[2 * x + y], local_sem.at[0]).start()
        for cp in _half_gather_copies(s_ref, l_ref, send_sems, recv_sems, False)[0]:
            cp.start()
        token[...] = jnp.zeros_like(token)

    land = lax.empty((N_CHIPS,) + shard.shape, shard.dtype)
    out = pl.pallas_call(
        body, name=name,
        out_shape=[pltpu.SemaphoreType.DMA((3,)), pltpu.SemaphoreType.DMA((3,)), pltpu.SemaphoreType.DMA((1,)),
                   pltpu.HBM(shard.shape, shard.dtype), pltpu.HBM(land.shape, land.dtype), jax.ShapeDtypeStruct((8, 128), F32)],
        in_specs=[HBM, HBM], out_specs=[SEM] * 3 + [HBM, HBM, BS(memory_space=pltpu.VMEM)],
        input_output_aliases={0: 3, 1: 4},
        compiler_params=pltpu.CompilerParams(has_side_effects=_SIDE_EFFECT))(_IN_HBM(shard), _IN_HBM(land))
    return out[:3], out[3], out[4], out[5]


def half_gather_wait(sems, shard, land, after, name):
    def body(s_ref, l_ref, send_sems, recv_sems, local_sem, *rest):
        x, y, _ = _place()
        pltpu.make_async_copy(s_ref, l_ref.at[2 * x + y], local_sem.at[0]).wait()
        sends, recvs = _half_gather_copies(s_ref, l_ref, send_sems, recv_sems, True)
        for cp in sends:
            cp.wait_send()
        for cp in recvs:
            cp.wait_recv()

    out = pl.pallas_call(
        body, name=name, out_shape=[pltpu.HBM(shard.shape, shard.dtype), pltpu.HBM(land.shape, land.dtype)],
        in_specs=[HBM, HBM] + [SEM] * 3 + [BS(memory_space=pl.ANY)] * len(after), out_specs=[HBM, HBM],
        input_output_aliases={0: 0, 1: 1},
        compiler_params=pltpu.CompilerParams(has_side_effects=_SIDE_EFFECT))(shard, land, *sems, *after)
    return out[1]


def pass_halves_to_sibling(land, name):
    def body(l_in, l_ref, send_sems, recv_sems):
        x, y, c = _place()
        copies = []
        for j, (px, py) in enumerate(_other_chips(x, y)):
            q = 2 * px + py
            give = pltpu.make_async_remote_copy(src_ref=l_ref.at[q, c], dst_ref=l_ref.at[q, c], send_sem=send_sems.at[j],
                                                recv_sem=recv_sems.at[j], device_id=(x, y, 1 - c), device_id_type=MESH)
            take = pltpu.make_async_remote_copy(src_ref=l_ref.at[q, c], dst_ref=l_ref.at[q, 1 - c], send_sem=send_sems.at[j],
                                                recv_sem=recv_sems.at[j], device_id=(x, y, 1 - c), device_id_type=MESH)
            give.start()
            copies.append((give, take))
        for give, take in copies:
            take.wait_recv()
            give.wait_send()

    return pl.pallas_call(
        body, name=name, in_specs=[ANY], out_specs=ANY, out_shape=jax.ShapeDtypeStruct(land.shape, land.dtype),
        input_output_aliases={0: 0},
        scratch_shapes=[pltpu.SemaphoreType.DMA((3,)), pltpu.SemaphoreType.DMA((3,))])(land)


def allgather_devices(block, name):
    R, C = block.shape

    def body(b_ref, o_ref, send_sems, recv_sems, local_sem):
        x, y, c = _place()
        me = 4 * x + 2 * y + c
        own = pltpu.make_async_copy(b_ref, o_ref.at[me], local_sem)
        own.start()
        copies = []
        for r in range(1, 8):
            px = 1 - x if r & 4 else x
            py = 1 - y if r & 2 else y
            pc = 1 - c if r & 1 else c
            send = pltpu.make_async_remote_copy(src_ref=b_ref, dst_ref=o_ref.at[me], send_sem=send_sems.at[r - 1],
                                                recv_sem=recv_sems.at[r - 1], device_id=(px, py, pc), device_id_type=MESH)
            recv = pltpu.make_async_remote_copy(src_ref=b_ref, dst_ref=o_ref.at[4 * px + 2 * py + pc], send_sem=send_sems.at[r - 1],
                                                recv_sem=recv_sems.at[r - 1], device_id=(px, py, pc), device_id_type=MESH)
            send.start()
            copies.append((send, recv))
        for send, recv in copies:
            recv.wait_recv()
            send.wait_send()
        own.wait()

    return pl.pallas_call(
        body, name=name, in_specs=[ANY], out_specs=ANY, out_shape=jax.ShapeDtypeStruct((8, R, C), block.dtype),
        scratch_shapes=[pltpu.SemaphoreType.DMA((7,)), pltpu.SemaphoreType.DMA((7,)), pltpu.SemaphoreType.DMA(())])(block)


def swap_sibling(arrs, name, splits=None):
    n = len(arrs)

    def sent(a_ref, i, c):
        return a_ref if splits is None else a_ref.at[(slice(None),) + _half(splits[i], 1 - c)]

    def out_shape(a, i):
        if splits is None:
            return a.shape
        axis, size = splits[i]
        return (a.shape[0], size, a.shape[2]) if axis == 0 else (a.shape[0], a.shape[1], size)

    def body(*refs):
        a_refs, o_refs = refs[:n], refs[n:2 * n]
        send_sems, recv_sems = refs[2 * n:]
        x, y, c = _place()
        copies = [pltpu.make_async_remote_copy(src_ref=sent(a_ref, i, c), dst_ref=o_ref, send_sem=send_sems.at[i],
                                               recv_sem=recv_sems.at[i], device_id=(x, y, 1 - c), device_id_type=MESH)
                  for i, (a_ref, o_ref) in enumerate(zip(a_refs, o_refs))]
        for cp in copies:
            cp.start()
        for cp in copies:
            cp.wait()

    return pl.pallas_call(
        body, name=name, in_specs=[ANY] * n, out_specs=[ANY] * n,
        out_shape=[jax.ShapeDtypeStruct(out_shape(a, i), a.dtype) for i, a in enumerate(arrs)],
        scratch_shapes=[pltpu.SemaphoreType.DMA((n,)), pltpu.SemaphoreType.DMA((n,))])(*arrs)


def _exchange_copies(p_refs, l_refs, send_sems, recv_sems):
    x, y, c = _place()
    return [pltpu.make_async_remote_copy(src_ref=p_ref.at[2 * px + py], dst_ref=l_ref.at[j], send_sem=send_sems.at[3 * i + j],
                                         recv_sem=recv_sems.at[3 * i + j], device_id=(px, py, c), device_id_type=MESH)
            for i, (p_ref, l_ref) in enumerate(zip(p_refs, l_refs)) for j, (px, py) in enumerate(_other_chips(x, y))]


def exchange_chips_start(parts, name):
    n = len(parts)

    def body(*refs):
        send_sems, recv_sems = refs[2 * n:2 * n + 2]
        for cp in _exchange_copies(refs[:n], refs[n:2 * n], send_sems, recv_sems):
            cp.start()
        refs[-1][...] = jnp.zeros_like(refs[-1])

    lands = [lax.empty((3,) + p.shape[1:], p.dtype) for p in parts]
    hbm_like = lambda a: pltpu.HBM(a.shape, a.dtype)
    out = pl.pallas_call(
        body, name=name,
        out_shape=[pltpu.SemaphoreType.DMA((3 * n,)), pltpu.SemaphoreType.DMA((3 * n,))]
        + [hbm_like(p) for p in parts] + [hbm_like(l) for l in lands] + [jax.ShapeDtypeStruct((8, 128), F32)],
        in_specs=[HBM] * (2 * n), out_specs=[SEM] * 2 + [HBM] * (2 * n) + [BS(memory_space=pltpu.VMEM)],
        input_output_aliases={i: 2 + i for i in range(2 * n)},
        compiler_params=pltpu.CompilerParams(has_side_effects=_SIDE_EFFECT))(*[_IN_HBM(p) for p in parts], *[_IN_HBM(l) for l in lands])
    return out[:2], out[2:2 + n], out[2 + n:2 + 2 * n], out[-1]


def exchange_chips_wait(sems, parts, lands, after, name):
    n = len(parts)

    def body(*refs):
        send_sems, recv_sems = refs[2 * n:2 * n + 2]
        for cp in _exchange_copies(refs[:n], refs[n:2 * n], send_sems, recv_sems):
            cp.wait_send()
            cp.wait_recv()

    hbm_like = lambda a: pltpu.HBM(a.shape, a.dtype)
    out = pl.pallas_call(
        body, name=name, out_shape=[hbm_like(p) for p in parts] + [hbm_like(l) for l in lands],
        in_specs=[HBM] * (2 * n) + [SEM] * 2 + [BS(memory_space=pl.ANY)], out_specs=[HBM] * (2 * n),
        input_output_aliases={i: i for i in range(2 * n)},
        compiler_params=pltpu.CompilerParams(has_side_effects=_SIDE_EFFECT))(*parts, *lands, *sems, after)
    return out[n:]


def _half_block(shape2, split):
    axis, size = split
    return (size, shape2[1]) if axis == 0 else (shape2[0], size)


def add_pairs(parts, halves, splits, core, name):
    n = len(parts)

    def body(s_ref, *refs):
        for a_ref, b_ref, o_ref in zip(refs[:n], refs[n:2 * n], refs[2 * n:]):
            o_ref[...] = (a_ref[...].astype(F32) + b_ref[...].astype(F32)).astype(BF16)

    def mine(i):
        blk = (None,) + _half_block(parts[i].shape[1:], splits[i])
        if splits[i][0] == 0:
            return BS(blk, lambda q, s: (q, s[0], 0))
        return BS(blk, lambda q, s: (q, 0, s[0]))

    half_specs = [BS((None,) + h.shape[1:], lambda q, s: (q, 0, 0)) for h in halves]
    return pl.pallas_call(
        body, name=name,
        grid_spec=pltpu.PrefetchScalarGridSpec(num_scalar_prefetch=1, grid=(N_CHIPS,),
                                               in_specs=[mine(i) for i in range(n)] + half_specs, out_specs=half_specs),
        out_shape=[jax.ShapeDtypeStruct(h.shape, BF16) for h in halves], compiler_params=_arb(1))(core, *parts, *halves)


def add_fives(parts, halves, from_chips, splits, chip_core, name):
    n = len(parts)

    def body(s_ref, *refs):
        for a_ref, b_ref, p_ref, o_ref in zip(refs[:n], refs[n:2 * n], refs[2 * n:3 * n], refs[3 * n:]):
            s = a_ref[...].astype(F32) + b_ref[...].astype(F32)
            for j in range(3):
                s = s + p_ref[j].astype(F32)
            o_ref[...] = s

    def mine(i):
        blk = (None,) + _half_block(parts[i].shape[1:], splits[i])
        if splits[i][0] == 0:
            return BS(blk, lambda g, s: (s[0], s[1], 0))
        return BS(blk, lambda g, s: (s[0], 0, s[1]))

    half_specs = [BS((None,) + h.shape[1:], lambda g, s: (s[0], 0, 0)) for h in halves]
    chip_specs = [BS(p.shape, lambda g, s: (0, 0, 0)) for p in from_chips]
    out_specs = [BS(h.shape[1:], lambda g, s: (0, 0)) for h in halves]
    return pl.pallas_call(
        body, name=name,
        grid_spec=pltpu.PrefetchScalarGridSpec(num_scalar_prefetch=1, grid=(1,),
                                               in_specs=[mine(i) for i in range(n)] + half_specs + chip_specs, out_specs=out_specs),
        out_shape=[jax.ShapeDtypeStruct(h.shape[1:], F32) for h in halves], compiler_params=_arb(1))(chip_core, *parts, *halves, *from_chips)


def sum_leading(a, name):
    def body(a_ref, o_ref):
        s = a_ref[0]
        for j in range(1, a.shape[0]):
            s = s + a_ref[j]
        o_ref[...] = s

    return pl.pallas_call(body, name=name, out_shape=jax.ShapeDtypeStruct(a.shape[1:], a.dtype))(a)


def _adamw_math(w, g, m, v):
    mn = ADAM_B1 * m + (1.0 - ADAM_B1) * g
    vn = ADAM_B2 * v + (1.0 - ADAM_B2) * (g * g)
    m_hat = mn / (1.0 - ADAM_B1 ** ADAM_STEP)
    v_hat = vn / (1.0 - ADAM_B2 ** ADAM_STEP)
    return -ADAM_LR * (m_hat / (jnp.sqrt(v_hat) + ADAM_EPS) + ADAM_WD * w), mn, vn


def adamw(w, g, m, v, name):
    R, C = g.shape
    lead = (None,) * (w.ndim - 2)

    def body(w_ref, g_ref, m_ref, v_ref, d_ref, mo_ref, vo_ref):
        d_ref[...], mo_ref[...], vo_ref[...] = _adamw_math(w_ref[...], g_ref[...], m_ref[...], v_ref[...])

    wblk = BS(lead + (R, C), lambda i: (0,) * w.ndim)
    gblk = BS((R, C), lambda i: (0, 0))
    return pl.pallas_call(
        body, name=name, grid=(1,), in_specs=[wblk, gblk, wblk, wblk], out_specs=[wblk] * 3,
        out_shape=[jax.ShapeDtypeStruct(w.shape, F32)] * 3, compiler_params=_arb(1))(w, g, m, v)


def adamw_halves(w, mine, other, m, v, split, core, name):
    R, C = w.shape[-2:]
    axis, size = split
    lead = (None,) * (w.ndim - 2)
    zeros = (0,) * (w.ndim - 2)
    if axis == 0:
        tr = size if size <= 256 else next(t for t in range(256, 7, -1) if size % t == 0 and t % 8 == 0)
        nb = size // tr
        whole = BS(lead + (tr, C), lambda hi, j, s: zeros + (hi * nb + j, 0))
        part = BS((tr, C), lambda hi, j, s: (j, 0))
    else:
        nb = size // 128
        whole = BS(lead + (R, 128), lambda hi, j, s: zeros + (0, hi * nb + j))
        part = BS((R, 128), lambda hi, j, s: (0, j))

    def body(s_ref, w_ref, a_ref, b_ref, m_ref, v_ref, g_ref, d_ref, mo_ref, vo_ref):
        g = jnp.where(pl.program_id(0) == s_ref[0], a_ref[...], b_ref[...])
        g_ref[...] = g
        d_ref[...], mo_ref[...], vo_ref[...] = _adamw_math(w_ref[...], g, m_ref[...], v_ref[...])

    return pl.pallas_call(
        body, name=name,
        grid_spec=pltpu.PrefetchScalarGridSpec(num_scalar_prefetch=1, grid=(2, nb),
                                               in_specs=[whole, part, part, whole, whole], out_specs=[whole] * 4),
        out_shape=[jax.ShapeDtypeStruct(w.shape, F32)] * 4, compiler_params=_arb(2))(core, w, mine, other, m, v)


def dense_bf16(w3, name):
    R, _, K = w3.shape
    kh = K // 2

    def body(w_hbm, o_ref, buf, sem):
        cp = pltpu.make_async_copy(w_hbm.at[:, 0], buf, sem)
        cp.start()
        cp.wait()
        o_ref[0] = buf[:, :kh].astype(BF16)
        o_ref[1] = buf[:, kh:].astype(BF16)

    return pl.pallas_call(
        body, name=name, in_specs=[ANY], out_specs=BS(memory_space=pltpu.VMEM), out_shape=jax.ShapeDtypeStruct((2, R, kh), BF16),
        scratch_shapes=[pltpu.VMEM((R, K), F32), pltpu.SemaphoreType.DMA(())])(w3)


ROW_BLOCK = 184


def adamw_untiled_rows(w3, mine, other, m3, v3, name):
    R, _, K = w3.shape
    kh = K // 2
    starts = list(range(0, R, ROW_BLOCK))
    sizes = [min(ROW_BLOCK, R - s) for s in starts]
    nblk = len(starts)

    def body(w_hbm, a_ref, b_ref, m_hbm, v_hbm, g_hbm, d_hbm, mo_hbm, vo_hbm,
             wbuf, mbuf, vbuf, gbuf, dbuf, mobuf, vobuf, in_sems, out_sems):
        first = lax.axis_index("c") == 0
        ins = []
        for k, (r0, n) in enumerate(zip(starts, sizes)):
            rows = pl.ds(r0, n)
            cps = [pltpu.make_async_copy(src.at[rows, 0], dst.at[rows], in_sems.at[3 * k + i])
                   for i, (src, dst) in enumerate(((w_hbm, wbuf), (m_hbm, mbuf), (v_hbm, vbuf)))]
            for cp in cps:
                cp.start()
            ins.append(cps)

        def update(rows):
            a, b = a_ref[rows, :], b_ref[rows, :]
            g = jnp.concatenate([jnp.where(first, a, b), jnp.where(first, b, a)], axis=1)
            gbuf[rows, :] = g
            dbuf[rows, :], mobuf[rows, :], vobuf[rows, :] = _adamw_math(wbuf[rows, :], g, mbuf[rows, :], vbuf[rows, :])

        outs = []
        for k, (r0, n) in enumerate(zip(starts, sizes)):
            for cp in ins[k]:
                cp.wait()
            groups, tail = n // 8, n % 8

            def group(i, carry, r0=r0):
                update(pl.ds(pl.multiple_of(r0 + i * 8, 8), 8))
                return carry

            lax.fori_loop(0, groups, group, 0)
            if tail:
                update(pl.ds(r0 + groups * 8, tail))
            rows = pl.ds(r0, n)
            cps = [pltpu.make_async_copy(src.at[rows], dst.at[rows, 0], out_sems.at[4 * k + i])
                   for i, (src, dst) in enumerate(((gbuf, g_hbm), (dbuf, d_hbm), (mobuf, mo_hbm), (vobuf, vo_hbm)))]
            for cp in cps:
                cp.start()
            outs += cps
        for cp in outs:
            cp.wait()

    vmem = BS(memory_space=pltpu.VMEM)
    return pl.pallas_call(
        body, name=name, in_specs=[ANY, vmem, vmem, ANY, ANY], out_specs=[ANY] * 4,
        out_shape=[jax.ShapeDtypeStruct(w3.shape, F32)] * 4,
        scratch_shapes=[pltpu.VMEM((R, K), F32)] * 7 + [pltpu.SemaphoreType.DMA((3 * nblk,)), pltpu.SemaphoreType.DMA((4 * nblk,))])(
            w3, mine, other, m3, v3)


def adamw_w_q_b(w, mine, other, m, v, name):
    def body(w_ref, a_ref, b_ref, m_ref, v_ref, g_ref, d_ref, mo_ref, vo_ref):
        first = lax.axis_index("c") == 0
        lo = jnp.where(first, a_ref[...], b_ref[...])
        hi = jnp.where(first, b_ref[...], a_ref[...])
        g = jnp.concatenate([lo, hi[0:32], hi[64:96]], axis=0)
        g_ref[...] = g
        d_ref[...], mo_ref[...], vo_ref[...] = _adamw_math(w_ref[...], g, m_ref[...], v_ref[...])

    return pl.pallas_call(body, name=name, out_shape=[jax.ShapeDtypeStruct(w.shape, F32)] * 4)(w, mine, other, m, v)


def local_step(x, mem, positions, tgt, norm_in, weights, big_grads_ready, q_a_norm, kv_a_norm, gdn_conv, gdn_a_log,
               gdn_dt_bias, gdn_norm, mem_norm, norm_final):
    B, S, D = x.shape
    M = mem.shape[1]
    T = B * S
    N = S // CHUNK
    x2d = x.reshape(T, D)
    mem2d = mem.reshape(B * M, D)
    tgt2d = tgt.reshape(T, D)

    alog_row, dt_row = _lane_row(gdn_a_log), _lane_row(gdn_dt_bias)

    half = MLA_ROPE // 2
    inv_freq = 1.0 / (ROPE_THETA ** (jnp.arange(half, dtype=F32) / half))
    z32 = jnp.zeros((half,), F32)
    o32 = jnp.ones((half,), F32)
    inv_row = jnp.concatenate([inv_freq, z32, inv_freq, z32]).reshape(1, 128)
    sgn_row = jnp.concatenate([-o32, z32, o32, z32]).reshape(1, 128)
    msk_row = jnp.concatenate([o32, z32, o32, z32]).reshape(1, 128)
    cos_t, sin_t = rope_tables(positions.reshape(T, 1), inv_row, sgn_row, msk_row, after=weights[3])

    h = rms_fwd(x2d, norm_in, "rms_in", after=weights[3])
    wp, behind = weights[0]((h, cos_t))
    P = mm(h, wp, "nt", BF16, "in_proj", bm=512, bn=1536, n_outer=True, after=behind)
    wq, wkv = weights[1](P)
    Q, K, V, qn, kvn = mla_prep(P, q_a_norm, kv_a_norm, wq, wkv, cos_t, sin_t)
    o_mla, lse = mla_attn_fwd(Q, K, V, B, S)
    qkv = gdn_prep_fwd(P, gdn_conv, B, S)
    GB = gdn_gate_fwd(P, alog_row, dt_row, B, S)
    Grow = jnp.transpose(GB[:, :N_HEADS].reshape(B, N, CHUNK, N_HEADS), (0, 3, 1, 2))
    U, W, Tinv, A = gdn_chunk_fwd(qkv, GB, Grow, B, S)
    qkv3, GB3 = qkv.reshape(B, S, GDN_QKV), GB.reshape(B, S, 128)
    W3 = W.reshape(B, S, 512)
    o_gdn3, Vn3, St = gdn_scan_fwd(qkv3, U.reshape(B, S, 512), W3, GB3, A, B, S)
    o_gdn = o_gdn3.reshape(T, 512)
    w_mem_kv, w_out = weights[2](o_gdn)
    memn = rms_fwd(mem2d, mem_norm, "rms_mem")
    MKV = mm(memn, w_mem_kv, "nn", BF16, "mem_kv_proj")
    o_mem = mem_attn_fwd(P, MKV, B, S, M)
    mixed, dx2, dx2b, sq, g_norm_final = merge_fwd(o_mla, o_gdn, o_mem, P, x2d, tgt2d, w_out, gdn_norm, norm_final.reshape(1, D))

    g_w_out = mm(mixed, dx2b, "tn", BF16, "grad_w_out")
    dgate, do_mla, do_gdn, do_mem, g_gdn_norm = merge_bwd(dx2b, o_mla, o_gdn, o_mem, P, w_out, gdn_norm)

    dmemq, dMKV = mem_attn_bwd(P, MKV, do_mem, B, S, M)
    g_w_mem_kv = mm(memn, dMKV, "tn", BF16, "grad_w_mem_kv")
    started_early = big_grads_ready(dict(w_mem_kv=g_w_mem_kv, w_out=g_w_out), "early")
    dmemn = mm(dMKV, w_mem_kv, "nt", F32, "d_memn", after=(started_early,))
    g_mem_norm = gain_grad(mem2d, dmemn, "grad_mem_norm")

    dU3, dW3, dQ13, dK13, dA, dG13 = gdn_scan_bwd(do_gdn.reshape(B, S, 512), qkv3, W3, Vn3, GB3, A, St, B, S)
    r2 = lambda a: a.reshape(T, a.shape[-1])
    dqkv, dGB = gdn_chunk_bwd(qkv, GB, Grow, Tinv, dA, r2(dU3), r2(dW3), r2(dQ13), r2(dK13), r2(dG13), B, S)
    dPg, g_conv = gdn_prep_bwd(P, dqkv, gdn_conv, B, S)
    dab, g_ab = gdn_gate_bwd(P, dGB, alog_row, dt_row, B, S)

    dQ, dK, dV = mla_attn_bwd(Q, K, V, o_mla, do_mla, lse, B, S)
    dq_lin, dkv_lin, dkr = mla_post_bwd(dQ, dK, dV, cos_t, sin_t)
    dqn = mm(dq_lin, wq, "nn", F32, "d_qn")
    dkvn = mm(dkv_lin, wkv, "nt", F32, "d_kvn")
    g_wq = mm(dq_lin, qn, "tn", BF16, "grad_w_q_b")
    g_wkv = mm(kvn, dkv_lin, "tn", BF16, "grad_w_kv_b")
    dPm, g_q_a_norm, g_kv_a_norm = mla_norm_bwd(P, dqn, dkvn, dkr, dab, q_a_norm, kv_a_norm)

    dP = [dPm, dmemq, dPg, dgate]
    g_wp = mm_cols_tn(dP, h, BF16, "grad_w_in")
    started = big_grads_ready(dict(w_in=g_wp, w_q_b=g_wq, w_kv_b=g_wkv), "late")
    grad_x, g_norm_in = in_proj_bwd(dP, wp, x2d, dx2, norm_in, started)

    grads = dict(
        norm_in=g_norm_in, q_a_norm=g_q_a_norm, kv_a_norm=g_kv_a_norm, gdn_conv=g_conv,
        gdn_a_log=g_ab[0:1, :N_HEADS], gdn_dt_bias=g_ab[1:2, :N_HEADS], gdn_norm=g_gdn_norm,
        mem_norm=g_mem_norm, norm_final=g_norm_final)
    return sq, grad_x.reshape(B, S, D), grads


def kernel(x, mem, positions, norm_in, w_in, q_a_norm, w_q_b, kv_a_norm, w_kv_b, gdn_conv, gdn_a_log, gdn_dt_bias, gdn_norm, mem_norm, w_mem_kv, w_out, norm_final, loss_target, m_norm_in, m_w_in, m_q_a_norm, m_w_q_b, m_kv_a_norm, m_w_kv_b, m_gdn_conv, m_gdn_a_log, m_gdn_dt_bias, m_gdn_norm, m_mem_norm, m_w_mem_kv, m_w_out, m_norm_final, v_norm_in, v_w_in, v_q_a_norm, v_w_q_b, v_kv_a_norm, v_w_kv_b, v_gdn_conv, v_gdn_a_log, v_gdn_dt_bias, v_gdn_norm, v_mem_norm, v_w_mem_kv, v_w_out, v_norm_final):
    B = x.shape[0]
    cx, cy, cc = lax.axis_index("x"), lax.axis_index("y"), lax.axis_index("c")
    chip = 2 * cx + cy

    big_names = ("w_in", "w_q_b", "w_kv_b", "w_mem_kv", "w_out")
    rows_major = lambda a: jnp.transpose(a, (2, 0, 1))
    w_in3, m_in3, v_in3 = rows_major(w_in), rows_major(m_w_in), rows_major(v_w_in)
    w_qb_t, m_qb_t, v_qb_t = jnp.transpose(w_q_b[0]), jnp.transpose(m_w_q_b[0]), jnp.transpose(v_w_q_b[0])
    z32 = jnp.zeros((32, Q_LORA), BF16)
    qb_bf = w_qb_t.astype(BF16)
    qb_padded = jnp.concatenate([qb_bf[:160], z32, qb_bf[160:], z32])
    shards = [dense_bf16(w_in3, "w_in_bf16"), qb_padded, w_kv_b[0].astype(BF16), w_mem_kv[0].astype(BF16), w_out[0].astype(BF16)]
    splits = [(1, D_MODEL // 2)] + [(0, s.shape[0] // 2) for s in shards[1:]]
    *w_in_flight, w_in_started = half_gather_start(shards[0], "w_in_gather_start")
    conv_all = allgather_devices(gdn_conv[0], "allgather_conv")
    conv_cols = gdn_conv.shape[2]
    conv_full = jnp.transpose(conv_all[0::2], (1, 0, 2)).reshape(GDN_CONV, N_CHIPS * conv_cols)
    late_shapes = [(N_CHIPS,) + s.shape for s in shards[1:]]
    late = {}

    def w_in_ready(after):
        g_in = pass_halves_to_sibling(half_gather_wait(*w_in_flight, after, "w_in_gather_wait"), "w_in_gather_sibling")
        *late["a"], started_a = late_gather_start(shards[1:3], g_in, "late_gather_qkv_start")
        *late["b"], started_b = late_gather_start(shards[3:], started_a, "late_gather_mem_out_start")
        return pad_w_in_t(g_in), (started_a, started_b)

    def late_qkv(after):
        g_qb, g_kvb = late_gather_wait(*late["a"], after, "late_gather_qkv_wait")
        return g_qb.reshape(-1, Q_LORA), _perm_w_kv_b(g_kvb)

    def late_mem_out(after):
        g_mem, g_out_w = late_gather_wait(*late["b"], after, "late_gather_mem_out_wait")
        return g_mem.reshape(-1, g_mem.shape[2]), g_out_w.reshape(-1, g_out_w.shape[2])

    weights = (w_in_ready, late_qkv, late_mem_out, (w_in_started,))

    core = jnp.stack([cc]).astype(jnp.int32)
    chip_core = jnp.stack([chip, cc]).astype(jnp.int32)
    exchanges = {}
    by_chip = dict(w_in=unpad_w_in_t, w_q_b=lambda a: a.reshape(late_shapes[0]), w_kv_b=_unperm_w_kv_b,
                   w_mem_kv=lambda a: a.reshape(late_shapes[2]), w_out=lambda a: a.reshape(late_shapes[3]))

    def big_grads_ready(gb, group):
        idx = [big_names.index(n) for n in gb]
        parts = [by_chip[n](a) for n, a in gb.items()]
        sp = [splits[i] for i in idx]
        from_sibling = swap_sibling(parts, "rs_sibling_partial_" + group, sp)
        chip_sums = add_pairs(parts, from_sibling, sp, core, "rs_add_sibling_" + group)
        sems, sums_thru, lands, token = exchange_chips_start(chip_sums, "rs_exchange_start_" + group)
        exchanges[group] = dict(idx=idx, parts=parts, from_sibling=from_sibling, sems=sems, sums=sums_thru, lands=lands)
        return token

    sq, grad_x, g = local_step(x, mem, positions, loss_target, norm_in, weights, big_grads_ready, q_a_norm, kv_a_norm, conv_full,
                               gdn_a_log, gdn_dt_bias, gdn_norm, mem_norm, norm_final)

    small_names = ("norm_in", "q_a_norm", "kv_a_norm", "gdn_a_log", "gdn_dt_bias", "gdn_norm", "mem_norm", "norm_final")
    small = dict(norm_in=norm_in, q_a_norm=q_a_norm, kv_a_norm=kv_a_norm, gdn_a_log=gdn_a_log, gdn_dt_bias=gdn_dt_bias,
                 gdn_norm=gdn_norm, mem_norm=mem_norm, norm_final=norm_final)
    m_small = dict(norm_in=m_norm_in, q_a_norm=m_q_a_norm, kv_a_norm=m_kv_a_norm, gdn_a_log=m_gdn_a_log,
                   gdn_dt_bias=m_gdn_dt_bias, gdn_norm=m_gdn_norm, mem_norm=m_mem_norm, norm_final=m_norm_final)
    v_small = dict(norm_in=v_norm_in, q_a_norm=v_q_a_norm, kv_a_norm=v_kv_a_norm, gdn_a_log=v_gdn_a_log,
                   gdn_dt_bias=v_gdn_dt_bias, gdn_norm=v_gdn_norm, mem_norm=v_mem_norm, norm_final=v_norm_final)
    rows = lambda d: jnp.stack([jnp.pad(d[n].reshape(-1), (0, 1024 - d[n].size)) for n in small_names])
    conv_rows = GDN_CONV * GDN_QKV // 1024
    g_block = jnp.concatenate([rows(g), g["gdn_conv"].reshape(conv_rows, 1024), sq, jnp.zeros((16 - 9 - conv_rows, 1024), F32)])
    g_block = sum_leading(allgather_devices(g_block, "allgather_small_grads"), "sum_small_grads")
    g_small_rows = g_block[:8]
    loss = 0.5 * jnp.sum(g_block[8 + conv_rows]) / D_MODEL
    g_conv = lax.dynamic_slice_in_dim(g_block[8:8 + conv_rows].reshape(GDN_CONV, GDN_QKV), chip * conv_cols, conv_cols, axis=1)
    d_s, m_s, v_s = adamw(rows(small), g_small_rows, rows(m_small), rows(v_small), "adamw_small")
    unrow = lambda r: {n: r[i, :small[n].size].reshape(small[n].shape) for i, n in enumerate(small_names)}
    g_out, d_out, m_out, v_out = unrow(g_small_rows), unrow(d_s), unrow(m_s), unrow(v_s)

    my_half = [None] * len(big_names)
    for group, e in exchanges.items():
        from_chips = exchange_chips_wait(e["sems"], e["sums"], e["lands"], d_s, "rs_exchange_wait_" + group)
        halves = add_fives(e["parts"], e["from_sibling"], from_chips, [splits[i] for i in e["idx"]], chip_core, "rs_add_chips_" + group)
        for i, a in zip(e["idx"], halves):
            my_half[i] = a
    other_half = swap_sibling(my_half, "rs_sibling_final")

    d_out["gdn_conv"], m_out["gdn_conv"], v_out["gdn_conv"] = adamw(gdn_conv, g_conv, m_gdn_conv, v_gdn_conv, "adamw_gdn_conv")
    g_out["gdn_conv"] = g_conv[None]
    res = adamw_untiled_rows(w_in3, my_half[0], other_half[0], m_in3, v_in3, "adamw_w_in")
    g_out["w_in"], d_out["w_in"], m_out["w_in"], v_out["w_in"] = [jnp.transpose(r, (1, 2, 0)) for r in res]
    res = adamw_w_q_b(w_qb_t, my_half[1], other_half[1], m_qb_t, v_qb_t, "adamw_w_q_b")
    g_out["w_q_b"], d_out["w_q_b"], m_out["w_q_b"], v_out["w_q_b"] = [jnp.transpose(r)[None] for r in res]
    rest = dict(w_kv_b=(w_kv_b, m_w_kv_b, v_w_kv_b), w_mem_kv=(w_mem_kv, m_w_mem_kv, v_w_mem_kv), w_out=(w_out, m_w_out, v_w_out))
    for i, n in enumerate(big_names):
        if n in rest:
            w_n, m_n, v_n = rest[n]
            g_out[n], d_out[n], m_out[n], v_out[n] = adamw_halves(w_n, my_half[i], other_half[i], m_n, v_n, splits[i], core, "adamw_" + n)

    order = ("norm_in", "w_in", "q_a_norm", "w_q_b", "kv_a_norm", "w_kv_b", "gdn_conv", "gdn_a_log", "gdn_dt_bias",
             "gdn_norm", "mem_norm", "w_mem_kv", "w_out", "norm_final")
    return (loss, grad_x, *[g_out[n] for n in order], *[d_out[n] for n in order], *[m_out[n] for n in order],
            *[v_out[n] for n in order])
```

```python
import functools
import math

import jax
import jax.numpy as jnp
import numpy as np
from jax import lax
from jax.experimental import pallas as pl
from jax.experimental.pallas import tpu as pltpu

F32 = jnp.float32
BF16 = jnp.bfloat16
BS = pl.BlockSpec

D_MODEL = 1024
N_HEADS = 4
MLA_NOPE, MLA_ROPE, MLA_V = 128, 64, 128
Q_LORA, KV_LORA = 384, 256
ROPE_THETA = 10000.0
GDN_DK = GDN_DV = 128
GDN_CONV = 4
CHUNK = 64
MEM_DH = 128
D_MIX = 1536
GDN_QKV = 1536
D_IN = 4296
EPS = 1e-6
ADAM_LR, ADAM_B1, ADAM_B2, ADAM_EPS, ADAM_WD, ADAM_STEP = 0.001, 0.9, 0.999, 1e-08, 0.01, 10

OFF_MLA = 0
OFF_MEMQ = 1024
OFF_GDN = 1536
OFF_GATE = 3072
N_PAD = 4608
HEAD_PAD = 256
MLA_SCALE = (MLA_NOPE + MLA_ROPE) ** -0.5
MEM_SCALE = MEM_DH ** -0.5
GDN_SCALE = GDN_DK ** -0.5
NEG = -1e30

NN = ((1,), (0,))
NT = ((1,), (1,))
TN = ((0,), (0,))


def _dot(a, b, dims):
    return lax.dot_general(a, b, (dims, ((), ())), preferred_element_type=F32)


def _bdot(spec, a, b, precision=None):
    return jnp.einsum(spec, a, b, preferred_element_type=F32, precision=precision)


def _arb(n):
    return pltpu.CompilerParams(dimension_semantics=("arbitrary",) * n)


def _sigmoid(x):
    return 1.0 / (1.0 + jnp.exp(-x))


def _softplus(z):
    return jnp.maximum(z, 0.0) + jnp.log(1.0 + jnp.exp(-jnp.abs(z)))


def _rope(t, cos_row, sin_row):
    return t * cos_row + pltpu.roll(t, 64, 1) * sin_row


def _rope_bwd(d, cos_row, sin_row):
    return d * cos_row + pltpu.roll(d * sin_row, 64, 1)


def rms_fwd(x, gain, name, tm=512, after=()):
    T, n = x.shape
    tm = min(tm, T)

    def body(x_ref, g_ref, *rest):
        xv = x_ref[...]
        r = lax.rsqrt(jnp.mean(xv * xv, axis=-1, keepdims=True) + EPS)
        rest[-1][...] = (xv * r * g_ref[...]).astype(BF16)

    return pl.pallas_call(
        body, name=name, grid=(T // tm,),
        in_specs=[BS((tm, n), lambda i: (i, 0)), BS((1, n), lambda i: (0, 0))] + [BS(memory_space=pl.ANY)] * len(after),
        out_specs=BS((tm, n), lambda i: (i, 0)),
        out_shape=jax.ShapeDtypeStruct((T, n), BF16), compiler_params=_arb(1))(x, gain, *after)


def mm(a, b, kind, out_dtype, name, bm=512, bn=None, n_outer=False, after=()):
    if kind == "nn":
        (M, K), (_, N) = a.shape, b.shape
    elif kind == "nt":
        (M, K), (N, _) = a.shape, b.shape
    else:
        (K, M), (_, N) = a.shape, b.shape
    bm, bn = min(bm, M), min(bn or N, N)
    assert M % bm == 0 and N % bn == 0, (name, M, N, K)
    ij = (lambda g0, g1: (g1, g0)) if n_outer else (lambda g0, g1: (g0, g1))
    a_spec = BS((K, bm), lambda g0, g1: (0, ij(g0, g1)[0])) if kind == "tn" else BS((bm, K), lambda g0, g1: (ij(g0, g1)[0], 0))
    once = dict(pipeline_mode=pl.Buffered(1)) if bn == N else {}
    b_spec = (BS((bn, K), lambda g0, g1: (ij(g0, g1)[1], 0), **once) if kind == "nt"
              else BS((K, bn), lambda g0, g1: (0, ij(g0, g1)[1]), **once))
    dims = {"nn": NN, "nt": NT, "tn": TN}[kind]

    def body(a_ref, b_ref, *rest):
        rest[-1][...] = _dot(a_ref[...].astype(BF16), b_ref[...].astype(BF16), dims).astype(out_dtype)

    grid = (N // bn, M // bm) if n_outer else (M // bm, N // bn)
    return pl.pallas_call(
        body, name=name, grid=grid, in_specs=[a_spec, b_spec] + [BS(memory_space=pl.ANY)] * len(after),
        out_specs=BS((bm, bn), lambda g0, g1: ij(g0, g1)),
        out_shape=jax.ShapeDtypeStruct((M, N), out_dtype), compiler_params=_arb(2))(a, b, *after)


def mm_cols_tn(pieces, b, out_dtype, name, bm=512):
    K, N = b.shape
    tiles = [p.shape[1] // bm for p in pieces]
    firsts = [sum(tiles[:i]) for i in range(len(tiles))]

    def body(*refs):
        b_ref, o_ref = refs[-2], refs[-1]
        i = pl.program_id(0)
        for a_ref, t0, n in zip(refs[:-2], firsts, tiles):
            @pl.when((i >= t0) & (i < t0 + n))
            def _(a_ref=a_ref):
                o_ref[...] = _dot(a_ref[...], b_ref[...], TN).astype(out_dtype)

    a_specs = [BS((K, bm), lambda i, t0=t0, n=n: (0, jnp.clip(i - t0, 0, n - 1))) for t0, n in zip(firsts, tiles)]
    return pl.pallas_call(
        body, name=name, grid=(sum(tiles),),
        in_specs=a_specs + [BS(b.shape, lambda i: (0, 0), pipeline_mode=pl.Buffered(1))],
        out_specs=BS((bm, N), lambda i: (i, 0)), out_shape=jax.ShapeDtypeStruct((sum(tiles) * bm, N), out_dtype),
        compiler_params=_arb(1))(*pieces, b)


def rope_tables(pos_col, inv_row, sgn_row, msk_row, tm=512, after=()):
    T = pos_col.shape[0]
    tm = min(tm, T)

    def body(p_ref, inv_ref, sgn_ref, msk_ref, *rest):
        c_ref, s_ref = rest[-2:]
        ang = p_ref[...].astype(F32) * inv_ref[...]
        c_ref[...] = jnp.cos(ang) * msk_ref[...]
        s_ref[...] = jnp.sin(ang) * sgn_ref[...]

    row = BS((1, 128), lambda i: (0, 0))
    return pl.pallas_call(
        body, name="rope_tables", grid=(T // tm,),
        in_specs=[BS((tm, 1), lambda i: (i, 0)), row, row, row] + [BS(memory_space=pl.ANY)] * len(after),
        out_specs=[BS((tm, 128), lambda i: (i, 0))] * 2,
        out_shape=[jax.ShapeDtypeStruct((T, 128), F32)] * 2, compiler_params=_arb(1))(pos_col, inv_row, sgn_row, msk_row, *after)


def mla_prep(P, gq, gkv, wq, wkv, cos_t, sin_t, tm=512):
    T = P.shape[0]
    tm = min(tm, T)

    def body(p_ref, gq_ref, gkv_ref, wq_ref, wkv_ref, c_ref, s_ref, q_ref, k_ref, v_ref, qn_ref, kvn_ref):
        p = p_ref[...].astype(F32)
        cq, ckv, kr = p[:, :Q_LORA], p[:, Q_LORA:Q_LORA + KV_LORA], p[:, 640:768]
        qn = (cq * lax.rsqrt(jnp.mean(cq * cq, axis=-1, keepdims=True) + EPS) * gq_ref[...]).astype(BF16)
        kvn = (ckv * lax.rsqrt(jnp.mean(ckv * ckv, axis=-1, keepdims=True) + EPS) * gkv_ref[...]).astype(BF16)
        qn_ref[...] = qn
        kvn_ref[...] = kvn
        q = _dot(qn, wq_ref[...], NT)
        kv = _dot(kvn, wkv_ref[...], NN)
        cos_row, sin_row = c_ref[...], s_ref[...]
        krr = _rope(kr, cos_row, sin_row).astype(BF16)
        for h in range(N_HEADS):
            lo = h * HEAD_PAD
            q_ref[:, lo:lo + 128] = (q[:, lo:lo + 128] * MLA_SCALE).astype(BF16)
            q_ref[:, lo + 128:lo + 256] = (_rope(q[:, lo + 128:lo + 256], cos_row, sin_row) * MLA_SCALE).astype(BF16)
            k_ref[:, lo:lo + 128] = kv[:, h * 128:(h + 1) * 128].astype(BF16)
            k_ref[:, lo + 128:lo + 256] = krr
            v_ref[:, lo:lo + 128] = kv[:, 512 + h * 128:512 + (h + 1) * 128].astype(BF16)
            v_ref[:, lo + 128:lo + 256] = jnp.ones((tm, 128), BF16)

    full = lambda r, c: BS((r, c), lambda i: (0, 0))
    rowb = lambda c: BS((tm, c), lambda i: (i, 0))
    return pl.pallas_call(
        body, name="mla_prep", grid=(T // tm,),
        in_specs=[rowb(1024), full(1, Q_LORA), full(1, KV_LORA), full(1024, Q_LORA), full(KV_LORA, 1024), rowb(128), rowb(128)],
        out_specs=[rowb(1024), rowb(1024), rowb(1024), rowb(Q_LORA), rowb(KV_LORA)],
        out_shape=[jax.ShapeDtypeStruct((T, 1024), BF16), jax.ShapeDtypeStruct((T, 1024), BF16),
                   jax.ShapeDtypeStruct((T, 1024), BF16), jax.ShapeDtypeStruct((T, Q_LORA), BF16),
                   jax.ShapeDtypeStruct((T, KV_LORA), BF16)],
        compiler_params=_arb(1))(P, gq, gkv, wq, wkv, cos_t, sin_t)


ATTN_HEADS_PER_STEP = 2
ATTN_STRIP = 32


def mla_attn_fwd(Q, K, V, B, S, tq=512, hp=ATTN_HEADS_PER_STEP):
    T = B * S
    tq = min(tq, S)
    nq = S // tq

    rs = min(ATTN_STRIP, tq)

    def body(q_ref, k_ref, v_ref, o_ref, lse_ref, m_s, acc_s, s_s, p_s, a_s):
        i = pl.program_id(2)
        m_s[...] = jnp.full_like(m_s, NEG)
        acc_s[...] = jnp.zeros_like(acc_s)

        def blk(j, masked):
            rows = pl.ds(pl.multiple_of(j * tq, tq), tq)
            for h in range(hp):
                hq = slice(h * HEAD_PAD, (h + 1) * HEAD_PAD)
                s_s[h] = _dot(q_ref[:, hq], k_ref[rows, hq], NT)
            for h in range(hp):
                for r0 in range(0, tq, rs):
                    rr = slice(r0, r0 + rs)
                    sv = s_s[h, rr, :]
                    if masked:
                        r = r0 + lax.broadcasted_iota(jnp.int32, (rs, tq), 0)
                        c = lax.broadcasted_iota(jnp.int32, (rs, tq), 1)
                        sv = jnp.where(r >= c, sv, NEG)
                    m_prev = m_s[h, rr, :]
                    m_new = jnp.maximum(m_prev, jnp.max(sv, axis=1, keepdims=True))
                    p_s[h, rr, :] = jnp.exp(sv - m_new).astype(BF16)
                    a_s[h, rr, :] = jnp.exp(m_prev - m_new)
                    m_s[h, rr, :] = m_new
            for h in range(hp):
                hq = slice(h * HEAD_PAD, (h + 1) * HEAD_PAD)
                acc_s[h] = a_s[h] * acc_s[h] + _dot(p_s[h], v_ref[rows, hq], NN)

        def loop(j, c):
            blk(j, False)
            return c

        lax.fori_loop(0, i, loop, 0)
        blk(i, True)
        for h in range(hp):
            den = acc_s[h, :, 128:256]
            o_ref[:, h * 128:(h + 1) * 128] = acc_s[h, :, 0:128] / den
            lse_ref[h] = m_s[h] + jnp.log(den[:, 0:1])

    return pl.pallas_call(
        body, name="mla_attn_fwd", grid=(B, N_HEADS // hp, nq),
        in_specs=[BS((tq, hp * HEAD_PAD), lambda b, h, i: (b * nq + i, h)),
                  BS((S, hp * HEAD_PAD), lambda b, h, i: (b, h)),
                  BS((S, hp * HEAD_PAD), lambda b, h, i: (b, h))],
        out_specs=[BS((tq, hp * 128), lambda b, h, i: (b * nq + i, h)),
                   BS((hp, tq, 1), lambda b, h, i: (h, b * nq + i, 0))],
        out_shape=[jax.ShapeDtypeStruct((T, 512), F32), jax.ShapeDtypeStruct((N_HEADS, T, 1), F32)],
        scratch_shapes=[pltpu.VMEM((hp, tq, 1), F32), pltpu.VMEM((hp, tq, HEAD_PAD), F32), pltpu.VMEM((hp, tq, tq), F32),
                        pltpu.VMEM((hp, tq, tq), BF16), pltpu.VMEM((hp, tq, 1), F32)],
        compiler_params=_arb(3))(Q, K, V)


def mla_attn_bwd(Q, K, V, O, dO, LSE, B, S, tq=512, hp=ATTN_HEADS_PER_STEP):
    T = B * S
    tq = min(tq, S)
    nq = S // tq

    rs = min(ATTN_STRIP, tq)

    def body(q_ref, k_ref, v_ref, o_ref, do_ref, lse_ref, dq_ref, dk_ref, dv_ref, delta_s, dk_s, dv_s, s_s, dp_s, p_s, ds_s):
        j = pl.program_id(2)

        @pl.when(j == 0)
        def _():
            dq_ref[...] = jnp.zeros_like(dq_ref)
            for h in range(hp):
                sl = slice(h * 128, (h + 1) * 128)
                delta_s[h] = jnp.sum(do_ref[:, sl] * o_ref[:, sl], axis=1, keepdims=True)

        dk_s[...] = jnp.zeros_like(dk_s)
        dv_s[...] = jnp.zeros_like(dv_s)

        def step(i, c):
            rows = pl.ds(pl.multiple_of(i * tq, tq), tq)
            for h in range(hp):
                sq, sv = slice(h * HEAD_PAD, (h + 1) * HEAD_PAD), slice(h * 128, (h + 1) * 128)
                s_s[h] = _dot(q_ref[rows, sq], k_ref[:, sq], NT)
                dp_s[h] = _dot(do_ref[rows, sv].astype(BF16), v_ref[:, h * HEAD_PAD:h * HEAD_PAD + 128], NT)
            for h in range(hp):
                for r0 in range(0, tq, rs):
                    rr = slice(r0, r0 + rs)
                    seq_rows = pl.ds(pl.multiple_of(i * tq + r0, rs), rs)
                    r = i * tq + r0 + lax.broadcasted_iota(jnp.int32, (rs, tq), 0)
                    cc = j * tq + lax.broadcasted_iota(jnp.int32, (rs, tq), 1)
                    p = jnp.where(r >= cc, jnp.exp(s_s[h, rr, :] - lse_ref[h, seq_rows, :]), 0.0)
                    p_s[h, rr, :] = p.astype(BF16)
                    ds_s[h, rr, :] = (p * (dp_s[h, rr, :] - delta_s[h, seq_rows, :])).astype(BF16)
            for h in range(hp):
                sq, sv = slice(h * HEAD_PAD, (h + 1) * HEAD_PAD), slice(h * 128, (h + 1) * 128)
                dv_s[:, sv] += _dot(p_s[h], do_ref[rows, sv].astype(BF16), TN)
                dk_s[:, sq] += _dot(ds_s[h], q_ref[rows, sq], TN)
                dq_ref[rows, sq] += _dot(ds_s[h], k_ref[:, sq], NN)
            return c

        lax.fori_loop(j, nq, step, 0)
        dk_ref[...] = dk_s[...]
        dv_ref[...] = dv_s[...]

    seq = lambda c: BS((S, c), lambda b, h, j: (b, h))
    blk = lambda c: BS((tq, c), lambda b, h, j: (b * nq + j, h))
    return pl.pallas_call(
        body, name="mla_attn_bwd", grid=(B, N_HEADS // hp, nq),
        in_specs=[seq(hp * HEAD_PAD), blk(hp * HEAD_PAD), blk(hp * HEAD_PAD), seq(hp * 128), seq(hp * 128),
                  BS((hp, S, 1), lambda b, h, j: (h, b, 0))],
        out_specs=[seq(hp * HEAD_PAD), blk(hp * HEAD_PAD), blk(hp * 128)],
        out_shape=[jax.ShapeDtypeStruct((T, 1024), F32), jax.ShapeDtypeStruct((T, 1024), F32),
                   jax.ShapeDtypeStruct((T, 512), F32)],
        scratch_shapes=[pltpu.VMEM((hp, S, 1), F32), pltpu.VMEM((tq, hp * HEAD_PAD), F32), pltpu.VMEM((tq, hp * 128), F32),
                        pltpu.VMEM((hp, tq, tq), F32), pltpu.VMEM((hp, tq, tq), F32),
                        pltpu.VMEM((hp, tq, tq), BF16), pltpu.VMEM((hp, tq, tq), BF16)],
        compiler_params=_arb(3))(Q, K, V, O, dO, LSE)


def mla_proj_bwd(dQ, dK, dV, cos_t, sin_t, P, dab, wq, wkv, gq, gkv, tm=512):
    T = P.shape[0]
    tm = min(tm, T)

    def norm_bwd(x, dy, g):
        r = lax.rsqrt(jnp.mean(x * x, axis=-1, keepdims=True) + EPS)
        xh = x * r
        dxh = dy * g
        return r * (dxh - xh * jnp.mean(dxh * xh, axis=-1, keepdims=True)), jnp.sum(dy * xh, axis=0, keepdims=True)

    def body(dq_ref, dk_ref, dv_ref, c_ref, s_ref, p_ref, dab_ref, wq_ref, wkv_ref, gq_ref, gkv_ref,
             ql_ref, kvl_ref, o_ref, aq_ref, akv_ref):
        @pl.when(pl.program_id(0) == 0)
        def _():
            aq_ref[...] = jnp.zeros_like(aq_ref)
            akv_ref[...] = jnp.zeros_like(akv_ref)

        cos_row, sin_row = c_ref[...], s_ref[...]
        kr = jnp.zeros((tm, 128), F32)
        for h in range(N_HEADS):
            lo = h * HEAD_PAD
            ql_ref[:, lo:lo + 128] = (dq_ref[:, lo:lo + 128] * MLA_SCALE).astype(BF16)
            ql_ref[:, lo + 128:lo + 256] = (_rope_bwd(dq_ref[:, lo + 128:lo + 256], cos_row, sin_row) * MLA_SCALE).astype(BF16)
            kvl_ref[:, h * 128:(h + 1) * 128] = dk_ref[:, lo:lo + 128].astype(BF16)
            kr = kr + dk_ref[:, lo + 128:lo + 256]
        kvl_ref[:, 512:] = dv_ref[...].astype(BF16)
        dqn = _dot(ql_ref[...], wq_ref[...], NN)
        dkvn = _dot(kvl_ref[...], wkv_ref[...], NT)
        dcq, ggq = norm_bwd(p_ref[:, :Q_LORA].astype(F32), dqn, gq_ref[...])
        dckv, ggkv = norm_bwd(p_ref[:, Q_LORA:640].astype(F32), dkvn, gkv_ref[...])
        aq_ref[...] += ggq
        akv_ref[...] += ggkv
        o_ref[:, :Q_LORA] = dcq.astype(BF16)
        o_ref[:, Q_LORA:640] = dckv.astype(BF16)
        o_ref[:, 640:768] = _rope_bwd(kr, cos_row, sin_row).astype(BF16)
        o_ref[:, 768:896] = dab_ref[...]
        o_ref[:, 896:1024] = jnp.zeros((tm, 128), BF16)

    rowb = lambda c: BS((tm, c), lambda i: (i, 0))
    full = lambda r, c: BS((r, c), lambda i: (0, 0))
    return pl.pallas_call(
        body, name="mla_proj_bwd", grid=(T // tm,),
        in_specs=[rowb(1024), rowb(1024), rowb(512), rowb(128), rowb(128), rowb(1024), rowb(128),
                  full(1024, Q_LORA), full(KV_LORA, 1024), full(1, Q_LORA), full(1, KV_LORA)],
        out_specs=[rowb(1024), rowb(1024), rowb(1024), full(1, Q_LORA), full(1, KV_LORA)],
        out_shape=[jax.ShapeDtypeStruct((T, 1024), BF16)] * 3
        + [jax.ShapeDtypeStruct((1, Q_LORA), F32), jax.ShapeDtypeStruct((1, KV_LORA), F32)],
        compiler_params=_arb(1))(dQ, dK, dV, cos_t, sin_t, P, dab, wq, wkv, gq, gkv)


def _mem_probs(qh, kh):
    s = _dot(qh, kh, NT) * MEM_SCALE
    p = jnp.exp(s - jnp.max(s, axis=1, keepdims=True))
    return p / jnp.sum(p, axis=1, keepdims=True)


def mem_attn_fwd(P, MKV, B, S, M, tq=512):
    T = B * S
    tq = min(tq, S)
    nq = S // tq

    def body(q_ref, kv_ref, o_ref):
        for h in range(N_HEADS):
            sl = slice(h * 128, (h + 1) * 128)
            p = _mem_probs(q_ref[:, sl].astype(BF16), kv_ref[:, sl])
            o_ref[:, sl] = _dot(p.astype(BF16), kv_ref[:, 512 + h * 128:512 + (h + 1) * 128], NN)

    return pl.pallas_call(
        body, name="mem_attn_fwd", grid=(B, nq),
        in_specs=[BS((tq, 512), lambda b, i: (b * nq + i, OFF_MEMQ // 512)), BS((M, 1024), lambda b, i: (b, 0))],
        out_specs=BS((tq, 512), lambda b, i: (b * nq + i, 0)),
        out_shape=jax.ShapeDtypeStruct((T, 512), F32), compiler_params=_arb(2))(P, MKV)


def mem_attn_bwd(P, MKV, dO, B, S, M, tq=512):
    T = B * S
    tq = min(tq, S)
    nq = S // tq

    def body(q_ref, kv_ref, do_ref, dq_ref, dkv_ref):
        @pl.when(pl.program_id(1) == 0)
        def _():
            dkv_ref[...] = jnp.zeros_like(dkv_ref)

        for h in range(N_HEADS):
            sl = slice(h * 128, (h + 1) * 128)
            sv = slice(512 + h * 128, 512 + (h + 1) * 128)
            qh = q_ref[:, sl].astype(BF16)
            kh = kv_ref[:, sl]
            do = do_ref[:, sl].astype(BF16)
            p = _mem_probs(qh, kh)
            dkv_ref[:, sv] += _dot(p.astype(BF16), do, TN)
            dp = _dot(do, kv_ref[:, sv], NT)
            ds = (p * (dp - jnp.sum(dp * p, axis=1, keepdims=True)) * MEM_SCALE).astype(BF16)
            dq_ref[:, sl] = _dot(ds, kh, NN).astype(BF16)
            dkv_ref[:, sl] += _dot(ds, qh, TN)

    return pl.pallas_call(
        body, name="mem_attn_bwd", grid=(B, nq),
        in_specs=[BS((tq, 512), lambda b, i: (b * nq + i, OFF_MEMQ // 512)), BS((M, 1024), lambda b, i: (b, 0)),
                  BS((tq, 512), lambda b, i: (b * nq + i, 0))],
        out_specs=[BS((tq, 512), lambda b, i: (b * nq + i, 0)), BS((M, 1024), lambda b, i: (b, 0))],
        out_shape=[jax.ShapeDtypeStruct((T, 512), BF16), jax.ShapeDtypeStruct((B * M, 1024), F32)],
        compiler_params=_arb(2))(P, MKV, dO)


def gain_grad(x, dy, name, tm=256):
    T, n = x.shape
    tm = min(tm, T)

    def body(x_ref, dy_ref, o_ref):
        @pl.when(pl.program_id(0) == 0)
        def _():
            o_ref[...] = jnp.zeros_like(o_ref)

        xv = x_ref[...]
        xh = xv * lax.rsqrt(jnp.mean(xv * xv, axis=-1, keepdims=True) + EPS)
        o_ref[...] += jnp.sum(dy_ref[...] * xh, axis=0, keepdims=True)

    return pl.pallas_call(
        body, name=name, grid=(T // tm,),
        in_specs=[BS((tm, n), lambda i: (i, 0))] * 2, out_specs=BS((1, n), lambda i: (0, 0)),
        out_shape=jax.ShapeDtypeStruct((1, n), F32), compiler_params=_arb(1))(x, dy)


def _conv_silu(x, w, t):
    y = x * w[3:4, :]
    for s in range(1, GDN_CONV):
        y = y + jnp.where(t >= s, pltpu.roll(x, s, 0), 0.0) * w[3 - s:4 - s, :]
    return y, _sigmoid(y)


def gdn_prep_fwd(P, conv_w, B, S):
    T = B * S

    def body(x_ref, w_ref, o_ref):
        kind = pl.program_id(1)
        t = lax.broadcasted_iota(jnp.int32, (S, 1), 0)
        y, sg = _conv_silu(x_ref[...].astype(F32), w_ref[...], t)
        a = y * sg
        scale = jnp.where(kind == 0, GDN_SCALE, 1.0).astype(F32)
        for h in range(N_HEADS):
            sl = slice(h * 128, (h + 1) * 128)
            seg = a[:, sl]
            n = lax.rsqrt(jnp.sum(seg * seg, axis=-1, keepdims=True) + EPS)
            o_ref[:, sl] = jnp.where(kind < 2, seg * (n * scale), seg)

    return pl.pallas_call(
        body, name="gdn_prep_fwd", grid=(B, 3),
        in_specs=[BS((S, 512), lambda b, k: (b, OFF_GDN // 512 + k)), BS((GDN_CONV, 512), lambda b, k: (0, k))],
        out_specs=BS((S, 512), lambda b, k: (b, k)),
        out_shape=jax.ShapeDtypeStruct((T, GDN_QKV), F32), compiler_params=_arb(2))(P, conv_w)


def gdn_prep_bwd(P, dqkv, conv_w, B, S):
    T = B * S

    def body(x_ref, d_ref, w_ref, o_ref, gw_ref):
        kind = pl.program_id(0)

        @pl.when(pl.program_id(1) == 0)
        def _():
            gw_ref[...] = jnp.zeros_like(gw_ref)

        t = lax.broadcasted_iota(jnp.int32, (S, 1), 0)
        x = x_ref[...].astype(F32)
        w = w_ref[...]
        y, sg = _conv_silu(x, w, t)
        a = y * sg
        scale = jnp.where(kind == 0, GDN_SCALE, 1.0).astype(F32)
        das = []
        for h in range(N_HEADS):
            sl = slice(h * 128, (h + 1) * 128)
            seg, dseg = a[:, sl], d_ref[:, sl]
            n = lax.rsqrt(jnp.sum(seg * seg, axis=-1, keepdims=True) + EPS)
            dn = scale * (n * dseg - seg * (n * n * n) * jnp.sum(dseg * seg, axis=-1, keepdims=True))
            das.append(jnp.where(kind < 2, dn, dseg))
        dy = jnp.concatenate(das, axis=1) * (sg * (1.0 + y * (1.0 - sg)))
        dx = dy * w[3:4, :]
        gw_ref[3:4, :] += jnp.sum(dy * x, axis=0, keepdims=True)
        for s in range(1, GDN_CONV):
            dx = dx + jnp.where(t + s < S, pltpu.roll(dy, S - s, 0), 0.0) * w[3 - s:4 - s, :]
            gw_ref[3 - s:4 - s, :] += jnp.sum(dy * jnp.where(t >= s, pltpu.roll(x, s, 0), 0.0), axis=0, keepdims=True)
        o_ref[...] = dx.astype(BF16)

    return pl.pallas_call(
        body, name="gdn_prep_bwd", grid=(3, B),
        in_specs=[BS((S, 512), lambda k, b: (b, OFF_GDN // 512 + k)), BS((S, 512), lambda k, b: (b, k)),
                  BS((GDN_CONV, 512), lambda k, b: (0, k))],
        out_specs=[BS((S, 512), lambda k, b: (b, k)), BS((GDN_CONV, 512), lambda k, b: (0, k))],
        out_shape=[jax.ShapeDtypeStruct((T, GDN_QKV), BF16), jax.ShapeDtypeStruct((GDN_CONV, GDN_QKV), F32)],
        compiler_params=_arb(2))(P, dqkv, conv_w)


def _chunk_row(n_rows):
    return lax.broadcasted_iota(jnp.int32, (n_rows, 1), 0) % CHUNK


def gdn_gate_fwd(P, alog_row, dt_row, B, S):
    T = B * S

    def body(x_ref, al_ref, dt_ref, o_ref):
        x = x_ref[...].astype(F32)
        lane = lax.broadcasted_iota(jnp.int32, (1, 128), 1)
        g = jnp.where(lane < 4, -jnp.exp(al_ref[...]) * _softplus(x + dt_ref[...]), 0.0)
        t = _chunk_row(S)
        for s in (1, 2, 4, 8, 16, 32):
            g = g + jnp.where(t >= s, pltpu.roll(g, s, 0), 0.0)
        o_ref[...] = jnp.where(lane < 4, g, jnp.where(lane < 8, _sigmoid(x), 0.0))

    row = BS((1, 128), lambda b: (0, 0))
    return pl.pallas_call(
        body, name="gdn_gate_fwd", grid=(B,),
        in_specs=[BS((S, 128), lambda b: (b, 768 // 128)), row, row], out_specs=BS((S, 128), lambda b: (b, 0)),
        out_shape=jax.ShapeDtypeStruct((T, 128), F32), compiler_params=_arb(1))(P, alog_row, dt_row)


def gdn_gate_bwd(P, dGB, alog_row, dt_row, B, S):
    T = B * S

    def body(x_ref, d_ref, al_ref, dt_ref, o_ref, acc_ref):
        @pl.when(pl.program_id(0) == 0)
        def _():
            acc_ref[...] = jnp.zeros_like(acc_ref)

        x, d = x_ref[...].astype(F32), d_ref[...]
        lane = lax.broadcasted_iota(jnp.int32, (1, 128), 1)
        z = x + dt_ref[...]
        coef = -jnp.exp(al_ref[...])
        g = coef * _softplus(z)
        da = jnp.where(lane < 4, d * coef * _sigmoid(z), 0.0)
        beta = _sigmoid(x)
        o_ref[...] = jnp.where(lane < 4, da, jnp.where(lane < 8, d * beta * (1.0 - beta), 0.0)).astype(BF16)
        acc_ref[0:1, :] += jnp.sum(jnp.where(lane < 4, d * g, 0.0), axis=0, keepdims=True)
        acc_ref[1:2, :] += jnp.sum(da, axis=0, keepdims=True)

    row = BS((1, 128), lambda b: (0, 0))
    return pl.pallas_call(
        body, name="gdn_gate_bwd", grid=(B,),
        in_specs=[BS((S, 128), lambda b: (b, 768 // 128)), BS((S, 128), lambda b: (b, 0)), row, row],
        out_specs=[BS((S, 128), lambda b: (b, 0)), BS((8, 128), lambda b: (0, 0))],
        out_shape=[jax.ShapeDtypeStruct((T, 128), BF16), jax.ShapeDtypeStruct((8, 128), F32)],
        compiler_params=_arb(1))(P, dGB, alog_row, dt_row)


def _chunk_masks(nc):
    r = lax.broadcasted_iota(jnp.int32, (nc, CHUNK, CHUNK), 1)
    c = lax.broadcasted_iota(jnp.int32, (nc, CHUNK, CHUNK), 2)
    return r >= c, r > c


def _chunk_local(q, k, gc, gr, beta, incl, strict):
    decay = jnp.exp(jnp.where(incl, gc - gr, NEG))
    kb = k * beta
    kbf = k.astype(BF16)
    m_kk = _bdot("gcd,gjd->gcj", kb.astype(BF16), kbf)
    l_mat = jnp.where(strict, m_kk * decay, 0.0)
    a_mat = _bdot("gcd,gjd->gcj", q.astype(BF16), kbf) * decay
    return decay, kb, l_mat, a_mat


WY_SPLIT_LEVELS = 2


def _split_bf16(x):
    hi = x.astype(BF16)
    return hi, (x - hi.astype(F32)).astype(BF16)


def _mm_split(ah, al, bh, bl):
    spec = "gij,gjk->gik"
    return _bdot(spec, ah, bh) + (_bdot(spec, ah, bl) + _bdot(spec, al, bh))


def gdn_chunk_fwd(qkv, GB, Grow, B, S, nc=8):
    T = B * S
    N = S // CHUNK
    nc = min(nc, N)
    nb = N // nc
    R = nc * CHUNK

    def body(q_ref, k_ref, v_ref, gb_ref, gr_ref, u_ref, w_ref, t_ref, a_ref):
        incl, strict = _chunk_masks(nc)
        eye = (lax.broadcasted_iota(jnp.int32, (nc, CHUNK, CHUNK), 1)
               == lax.broadcasted_iota(jnp.int32, (nc, CHUNK, CHUNK), 2)).astype(F32)
        for h in range(N_HEADS):
            sl = slice(h * 128, (h + 1) * 128)
            q = q_ref[:, sl].reshape(nc, CHUNK, 128)
            k = k_ref[:, sl].reshape(nc, CHUNK, 128)
            v = v_ref[:, sl].reshape(nc, CHUNK, 128)
            gc = gb_ref[:, h:h + 1].reshape(nc, CHUNK, 1)
            beta = gb_ref[:, 4 + h:5 + h].reshape(nc, CHUNK, 1)
            gr = gr_ref[h][:, None, :]
            _, kb, l_mat, a_mat = _chunk_local(q, k, gc, gr, beta, incl, strict)
            pw = -l_mat
            tinv = eye + pw
            for level in range(5):
                if level < WY_SPLIT_LEVELS:
                    ph, pl_ = _split_bf16(pw)
                    pw = _mm_split(ph, pl_, ph, pl_)
                    ph, pl_ = _split_bf16(pw)
                    th, tl = _split_bf16(tinv)
                    tinv = tinv + _mm_split(th, tl, ph, pl_)
                else:
                    ph = pw.astype(BF16)
                    pw = _bdot("gij,gjk->gik", ph, ph)
                    tinv = tinv + _bdot("gij,gjk->gik", tinv.astype(BF16), pw.astype(BF16))
            tb = tinv.astype(BF16)
            u = _bdot("gcj,gjv->gcv", tb, (v * beta).astype(BF16))
            w = _bdot("gcj,gjk->gck", tb, (kb * jnp.exp(gc)).astype(BF16))
            u_ref[:, sl] = u.reshape(R, 128)
            w_ref[:, sl] = w.reshape(R, 128).astype(BF16)
            t_ref[h] = tb
            a_ref[h] = a_mat.astype(BF16)

    rowb = lambda c, j: BS((R, c), lambda b, n: (b * nb + n, j))
    mat = BS((None, N_HEADS, nc, CHUNK, CHUNK), lambda b, n: (b, 0, n, 0, 0))
    return pl.pallas_call(
        body, name="gdn_chunk_fwd", grid=(B, nb),
        in_specs=[rowb(512, 0), rowb(512, 1), rowb(512, 2), rowb(128, 0),
                  BS((None, N_HEADS, nc, CHUNK), lambda b, n: (b, 0, n, 0))],
        out_specs=[rowb(512, 0), rowb(512, 0), mat, mat],
        out_shape=[jax.ShapeDtypeStruct((T, 512), F32), jax.ShapeDtypeStruct((T, 512), BF16),
                   jax.ShapeDtypeStruct((B, N_HEADS, N, CHUNK, CHUNK), BF16),
                   jax.ShapeDtypeStruct((B, N_HEADS, N, CHUNK, CHUNK), BF16)],
        compiler_params=_arb(2))(qkv, qkv, qkv, GB, Grow)


def gdn_scan_fwd(qkv3, U3, W3, GB3, A, B, S):
    N = S // CHUNK

    def body(q_ref, k_ref, u_ref, w_ref, gb_ref, a_ref, o_ref, vn_ref, st_ref, s_s):
        @pl.when(pl.program_id(0) == 0)
        def _():
            s_s[...] = jnp.zeros_like(s_s)

        for b in range(B):
            for h in range(N_HEADS):
                sl = slice(h * 128, (h + 1) * 128)
                st = s_s[b, h]
                st_ref[b, h] = st
                stb = st.astype(BF16)
                g = gb_ref[b, :, h:h + 1]
                gl = g[CHUNK - 1:CHUNK, :]
                vn = u_ref[b, :, sl] - _dot(w_ref[b, :, sl].astype(BF16), stb, NN)
                vnb = vn.astype(BF16)
                o = _dot((q_ref[b, :, sl] * jnp.exp(g)).astype(BF16), stb, NN) + _dot(a_ref[b, h].astype(BF16), vnb, NN)
                vn_ref[b, :, sl] = vnb
                o_ref[b, :, sl] = o
                s_s[b, h] = st * jnp.exp(gl) + _dot((k_ref[b, :, sl] * jnp.exp(gl - g)).astype(BF16), vnb, TN)

    tok = lambda c, j: BS((B, CHUNK, c), lambda n: (0, n, j))
    return pl.pallas_call(
        body, name="gdn_scan_fwd", grid=(N,),
        in_specs=[tok(512, 0), tok(512, 1), tok(512, 0), tok(512, 0), tok(128, 0),
                  BS((B, N_HEADS, None, CHUNK, CHUNK), lambda n: (0, 0, n, 0, 0))],
        out_specs=[tok(512, 0), tok(512, 0), BS((B, N_HEADS, None, 128, 128), lambda n: (0, 0, n, 0, 0))],
        out_shape=[jax.ShapeDtypeStruct((B, S, 512), F32), jax.ShapeDtypeStruct((B, S, 512), BF16),
                   jax.ShapeDtypeStruct((B, N_HEADS, N, 128, 128), F32)],
        scratch_shapes=[pltpu.VMEM((B, N_HEADS, 128, 128), F32)],
        compiler_params=_arb(1))(qkv3, qkv3, U3, W3, GB3, A)


def gdn_scan_bwd(dO3, qkv3, W3, Vn3, GB3, A, St, B, S):
    N = S // CHUNK

    def body(do_ref, q_ref, k_ref, w_ref, vn_ref, gb_ref, a_ref, st_ref,
             du_ref, dw_ref, dq_ref, dk_ref, da_ref, dg_ref, ds_s):
        @pl.when(pl.program_id(0) == 0)
        def _():
            ds_s[...] = jnp.zeros_like(ds_s)

        lane = lax.broadcasted_iota(jnp.int32, (1, 128), 1)
        last = lax.broadcasted_iota(jnp.int32, (CHUNK, 1), 0) == CHUNK - 1
        for b in range(B):
            dg_all = jnp.zeros((CHUNK, 128), F32)
            for h in range(N_HEADS):
                sl = slice(h * 128, (h + 1) * 128)
                st = st_ref[b, h]
                stb = st.astype(BF16)
                dsn = ds_s[b, h]
                dsnb = dsn.astype(BF16)
                g = gb_ref[b, :, h:h + 1]
                gl = g[CHUNK - 1:CHUNK, :]
                egl = jnp.exp(gl)
                ekd = jnp.exp(gl - g)
                eg = jnp.exp(g)
                q, k = q_ref[b, :, sl], k_ref[b, :, sl]
                kd = k * ekd
                qg = q * eg
                do = do_ref[b, :, sl].astype(BF16)
                vnb = vn_ref[b, :, sl].astype(BF16)
                dvn = _dot(a_ref[b, h].astype(BF16), do, TN) + _dot(kd.astype(BF16), dsnb, NN)
                dvnb = dvn.astype(BF16)
                da_ref[b, h] = _dot(do, vnb, NT)
                dqg = _dot(do, stb, NT)
                dkd = _dot(vnb, dsnb, NT)
                ds_s[b, h] = (_dot(qg.astype(BF16), do, TN) + egl * dsn - _dot(w_ref[b, :, sl].astype(BF16), dvnb, TN))
                du_ref[b, :, sl] = dvnb
                dw_ref[b, :, sl] = (-_dot(dvnb, stb, NT)).astype(BF16)
                dq_ref[b, :, sl] = dqg * eg
                dk_ref[b, :, sl] = dkd * ekd
                ddel = jnp.sum(dkd * kd, axis=1, keepdims=True)
                dgl = jnp.sum(ddel, axis=0, keepdims=True) + jnp.sum(jnp.sum(st * dsn, axis=1, keepdims=True), axis=0, keepdims=True) * egl
                col = jnp.sum(dqg * qg, axis=1, keepdims=True) - ddel + jnp.where(last, dgl, 0.0)
                dg_all = jnp.where(lane == h, col, dg_all)
            dg_ref[b] = dg_all

    tok = lambda c, j: BS((B, CHUNK, c), lambda n: (0, N - 1 - n, j))
    mat = lambda d: BS((B, N_HEADS, None, d, d), lambda n: (0, 0, N - 1 - n, 0, 0))
    return pl.pallas_call(
        body, name="gdn_scan_bwd", grid=(N,),
        in_specs=[tok(512, 0), tok(512, 0), tok(512, 1), tok(512, 0), tok(512, 0), tok(128, 0), mat(CHUNK), mat(128)],
        out_specs=[tok(512, 0), tok(512, 0), tok(512, 0), tok(512, 0), mat(CHUNK), tok(128, 0)],
        out_shape=[jax.ShapeDtypeStruct((B, S, 512), BF16)] * 2 + [jax.ShapeDtypeStruct((B, S, 512), F32)] * 2
        + [jax.ShapeDtypeStruct((B, N_HEADS, N, CHUNK, CHUNK), F32), jax.ShapeDtypeStruct((B, S, 128), F32)],
        scratch_shapes=[pltpu.VMEM((B, N_HEADS, 128, 128), F32)],
        compiler_params=_arb(1))(dO3, qkv3, qkv3, W3, Vn3, GB3, A, St)


def gdn_chunk_bwd(qkv, GB, Grow, Tinv, dA, dU, dW, dQ1, dK1, dG1, B, S, nc=8):
    T = B * S
    N = S // CHUNK
    nc = min(nc, N)
    nb = N // nc
    R = nc * CHUNK

    def body(q_ref, k_ref, v_ref, gb_ref, gr_ref, t_ref, da_ref, du_ref, dw_ref, dq1_ref, dk1_ref, dg1_ref, o_ref, dgb_ref):
        incl, strict = _chunk_masks(nc)
        lane = lax.broadcasted_iota(jnp.int32, (1, 128), 1)
        dg_all = dg1_ref[...]
        db_all = jnp.zeros((R, 128), F32)
        for h in range(N_HEADS):
            sl = slice(h * 128, (h + 1) * 128)
            q = q_ref[:, sl].reshape(nc, CHUNK, 128)
            k = k_ref[:, sl].reshape(nc, CHUNK, 128)
            v = v_ref[:, sl].reshape(nc, CHUNK, 128)
            gc = gb_ref[:, h:h + 1].reshape(nc, CHUNK, 1)
            beta = gb_ref[:, 4 + h:5 + h].reshape(nc, CHUNK, 1)
            gr = gr_ref[h][:, None, :]
            decay, kb, l_mat, a_mat = _chunk_local(q, k, gc, gr, beta, incl, strict)
            eg = jnp.exp(gc)
            kbg = kb * eg
            vb = v * beta
            tb = t_ref[h].astype(BF16)
            du = du_ref[:, sl].reshape(nc, CHUNK, 128).astype(BF16)
            dw = dw_ref[:, sl].reshape(nc, CHUNK, 128).astype(BF16)
            dvb = _bdot("gcj,gcv->gjv", tb, du)
            dkbg = _bdot("gcj,gck->gjk", tb, dw)
            dt = _bdot("gcv,gjv->gcj", du, vb.astype(BF16)) + _bdot("gck,gjk->gcj", dw, kbg.astype(BF16))
            tmp = _bdot("gac,gab->gcb", tb, dt.astype(BF16))
            dl = jnp.where(strict, -_bdot("gcb,gdb->gcd", tmp.astype(BF16), tb), 0.0)
            da = da_ref[h]
            dm = (dl * decay).astype(BF16)
            dqk = (da * decay).astype(BF16)
            kbf = k.astype(BF16)
            dkb = _bdot("gcj,gjd->gcd", dm, kbf) + dkbg * eg
            dk = (_bdot("gcj,gcd->gjd", dm, kb.astype(BF16)) + _bdot("gcj,gcd->gjd", dqk, q.astype(BF16))
                  + dk1_ref[:, sl].reshape(nc, CHUNK, 128) + dkb * beta)
            dq = _bdot("gcj,gjd->gcd", dqk, kbf) + dq1_ref[:, sl].reshape(nc, CHUNK, 128)
            e = dl * l_mat + da * a_mat
            dgc = (jnp.sum(e, axis=2, keepdims=True) - jnp.sum(jnp.swapaxes(e, 1, 2), axis=2, keepdims=True)
                   + jnp.sum(dkbg * kbg, axis=2, keepdims=True))
            dbeta = jnp.sum(dkb * k, axis=2, keepdims=True) + jnp.sum(dvb * v, axis=2, keepdims=True)
            o_ref[:, sl] = dq.reshape(R, 128)
            o_ref[:, 512 + h * 128:512 + (h + 1) * 128] = dk.reshape(R, 128)
            o_ref[:, 1024 + h * 128:1024 + (h + 1) * 128] = (dvb * beta).reshape(R, 128)
            dg_all = dg_all + jnp.where(lane == h, dgc.reshape(R, 1), 0.0)
            db_all = jnp.where(lane == 4 + h, dbeta.reshape(R, 1), db_all)
        t = _chunk_row(R)
        for s in (1, 2, 4, 8, 16, 32):
            dg_all = dg_all + jnp.where(t + s < CHUNK, pltpu.roll(dg_all, R - s, 0), 0.0)
        dgb_ref[...] = jnp.where(lane < 4, dg_all, db_all)

    rowb = lambda c, j: BS((R, c), lambda b, n: (b * nb + n, j))
    mat = BS((None, N_HEADS, nc, CHUNK, CHUNK), lambda b, n: (b, 0, n, 0, 0))
    return pl.pallas_call(
        body, name="gdn_chunk_bwd", grid=(B, nb),
        in_specs=[rowb(512, 0), rowb(512, 1), rowb(512, 2), rowb(128, 0),
                  BS((None, N_HEADS, nc, CHUNK), lambda b, n: (b, 0, n, 0)), mat, mat,
                  rowb(512, 0), rowb(512, 0), rowb(512, 0), rowb(512, 0), rowb(128, 0)],
        out_specs=[rowb(GDN_QKV, 0), rowb(128, 0)],
        out_shape=[jax.ShapeDtypeStruct((T, GDN_QKV), F32), jax.ShapeDtypeStruct((T, 128), F32)],
        compiler_params=_arb(2))(qkv, qkv, qkv, GB, Grow, Tinv, dA, dU, dW, dQ1, dK1, dG1)


def _gdn_out_norm(og, gg):
    outs, xhs, rs = [], [], []
    for h in range(N_HEADS):
        seg = og[:, h * 128:(h + 1) * 128]
        r = lax.rsqrt(jnp.mean(seg * seg, axis=-1, keepdims=True) + EPS)
        xh = seg * r
        outs.append(xh * gg)
        xhs.append(xh)
        rs.append(r)
    return outs, xhs, rs


def merge_fwd(o_mla, o_gdn, o_mem, P, x, tgt, w_out, g_gdn, g_fin, tm=256):
    T = x.shape[0]
    tm = min(tm, T)

    def body(om_ref, og_ref, oc_ref, gate_ref, x_ref, t_ref, w_ref, gg_ref, gf_ref, mix_ref, dx_ref, dxb_ref, sq_ref, gnf_ref):
        @pl.when(pl.program_id(0) == 0)
        def _():
            sq_ref[...] = jnp.zeros_like(sq_ref)
            gnf_ref[...] = jnp.zeros_like(gnf_ref)

        ogn, _, _ = _gdn_out_norm(og_ref[...], gg_ref[...])
        cat = jnp.concatenate([om_ref[...]] + ogn + [oc_ref[...]], axis=1)
        gt = gate_ref[...].astype(F32)
        mixed = (cat * (gt * _sigmoid(gt))).astype(BF16)
        mix_ref[...] = mixed
        x2 = x_ref[...] + _dot(mixed, w_ref[...], NN)
        r2 = lax.rsqrt(jnp.mean(x2 * x2, axis=-1, keepdims=True) + EPS)
        xh = x2 * r2
        gf = gf_ref[...]
        diff = xh * gf - t_ref[...]
        sq_ref[...] += jnp.sum(diff * diff, axis=0, keepdims=True)
        dy = diff * (1.0 / D_MODEL)
        gnf_ref[...] += jnp.sum(dy * xh, axis=0, keepdims=True)
        dxh = dy * gf
        dx = r2 * (dxh - xh * jnp.mean(dxh * xh, axis=-1, keepdims=True))
        dx_ref[...] = dx
        dxb_ref[...] = dx.astype(BF16)

    rowb = lambda c, j=0: BS((tm, c), lambda i: (i, j))
    full = lambda r, c: BS((r, c), lambda i: (0, 0))
    return pl.pallas_call(
        body, name="merge_fwd", grid=(T // tm,),
        in_specs=[rowb(512), rowb(512), rowb(512), rowb(D_MIX, OFF_GATE // D_MIX), rowb(D_MODEL), rowb(D_MODEL),
                  full(D_MIX, D_MODEL), full(1, 128), full(1, D_MODEL)],
        out_specs=[rowb(D_MIX), rowb(D_MODEL), rowb(D_MODEL), full(1, D_MODEL), full(1, D_MODEL)],
        out_shape=[jax.ShapeDtypeStruct((T, D_MIX), BF16), jax.ShapeDtypeStruct((T, D_MODEL), F32),
                   jax.ShapeDtypeStruct((T, D_MODEL), BF16),
                   jax.ShapeDtypeStruct((1, D_MODEL), F32), jax.ShapeDtypeStruct((1, D_MODEL), F32)],
        compiler_params=_arb(1))(o_mla, o_gdn, o_mem, P, x, tgt, w_out, g_gdn, g_fin)


def merge_bwd(dx2, o_mla, o_gdn, o_mem, P, w_out, g_gdn, tm=256):
    T = dx2.shape[0]
    tm = min(tm, T)

    def body(dx_ref, om_ref, og_ref, oc_ref, gate_ref, w_ref, gg_ref, dgate_ref, dom_ref, dog_ref, doc_ref, ggn_ref):
        @pl.when(pl.program_id(0) == 0)
        def _():
            ggn_ref[...] = jnp.zeros_like(ggn_ref)

        gg = gg_ref[...]
        dmix = _dot(dx_ref[...].astype(BF16), w_ref[...], NT)
        ogn, xhs, rs = _gdn_out_norm(og_ref[...], gg)
        cat = jnp.concatenate([om_ref[...]] + ogn + [oc_ref[...]], axis=1)
        gt = gate_ref[...].astype(F32)
        sg = _sigmoid(gt)
        dgate_ref[...] = (dmix * cat * (sg * (1.0 + gt * (1.0 - sg)))).astype(BF16)
        dcat = dmix * (gt * sg)
        dom_ref[...] = dcat[:, :512]
        doc_ref[...] = dcat[:, 1024:]
        acc = jnp.zeros((1, 128), F32)
        for h in range(N_HEADS):
            dseg = dcat[:, 512 + h * 128:512 + (h + 1) * 128]
            acc = acc + jnp.sum(dseg * xhs[h], axis=0, keepdims=True)
            dxh = dseg * gg
            dog_ref[:, h * 128:(h + 1) * 128] = rs[h] * (dxh - xhs[h] * jnp.mean(dxh * xhs[h], axis=-1, keepdims=True))
        ggn_ref[...] += acc

    rowb = lambda c, j=0: BS((tm, c), lambda i: (i, j))
    full = lambda r, c: BS((r, c), lambda i: (0, 0))
    return pl.pallas_call(
        body, name="merge_bwd", grid=(T // tm,),
        in_specs=[rowb(D_MODEL), rowb(512), rowb(512), rowb(512), rowb(D_MIX, OFF_GATE // D_MIX),
                  full(D_MIX, D_MODEL), full(1, 128)],
        out_specs=[rowb(D_MIX), rowb(512), rowb(512), rowb(512), full(1, 128)],
        out_shape=[jax.ShapeDtypeStruct((T, D_MIX), BF16)] + [jax.ShapeDtypeStruct((T, 512), F32)] * 3
        + [jax.ShapeDtypeStruct((1, 128), F32)],
        compiler_params=_arb(1))(dx2, o_mla, o_gdn, o_mem, P, w_out, g_gdn)


def in_proj_bwd(dP, wp, x, dx2, gain, after, tm=512):
    T, n = x.shape
    tm = min(tm, T)
    k = len(dP)
    widths = [p.shape[1] for p in dP]
    offs = [sum(widths[:i]) for i in range(k)]

    def body(*refs):
        w_ref, x_ref, dx2_ref, g_ref = refs[k:k + 4]
        o_ref, acc_ref = refs[-2:]

        @pl.when(pl.program_id(0) == 0)
        def _():
            acc_ref[...] = jnp.zeros_like(acc_ref)

        dy = None
        for a_ref, off, w in zip(refs[:k], offs, widths):
            d = _dot(a_ref[...], w_ref[off:off + w, :], NN)
            dy = d if dy is None else dy + d
        xv = x_ref[...]
        r = lax.rsqrt(jnp.mean(xv * xv, axis=-1, keepdims=True) + EPS)
        xh = xv * r
        acc_ref[...] += jnp.sum(dy * xh, axis=0, keepdims=True)
        dxh = dy * g_ref[...]
        o_ref[...] = dx2_ref[...] + r * (dxh - xh * jnp.mean(dxh * xh, axis=-1, keepdims=True))

    rowb = BS((tm, n), lambda i: (i, 0))
    full = BS((1, n), lambda i: (0, 0))
    return pl.pallas_call(
        body, name="in_proj_bwd", grid=(T // tm,),
        in_specs=[BS((tm, w), lambda i: (i, 0)) for w in widths]
        + [BS(wp.shape, lambda i: (0, 0), pipeline_mode=pl.Buffered(1)), rowb, rowb, full, BS(memory_space=pl.ANY)],
        out_specs=[rowb, full], out_shape=[jax.ShapeDtypeStruct((T, n), F32), jax.ShapeDtypeStruct((1, n), F32)],
        compiler_params=_arb(1))(*dP, wp, x, dx2, gain, after)


W_IN_SHARD = D_IN // 4
_GDN0 = Q_LORA + KV_LORA + MLA_ROPE
_AB0 = _GDN0 + GDN_QKV
_MEMQ0 = _AB0 + 2 * N_HEADS
_GATE0 = _MEMQ0 + N_HEADS * MEM_DH


def _w_in_row_map():
    a, m, gt = _AB0 - 2 * W_IN_SHARD, _MEMQ0 - 2 * W_IN_SHARD, _GATE0 - 2 * W_IN_SHARD
    e0 = OFF_GDN + W_IN_SHARD - _GDN0
    e1 = e0 + W_IN_SHARD
    e2 = OFF_GATE + W_IN_SHARD - gt
    return [(0, 0, 0, 672), (0, 672, 704, 32), (2, a, 768, m - a), (2, m, OFF_MEMQ, gt - m), (0, _GDN0, OFF_GDN, W_IN_SHARD - _GDN0),
            (1, 0, e0, W_IN_SHARD), (2, 0, e1, a), (2, gt, OFF_GATE, W_IN_SHARD - gt), (3, 0, e2, W_IN_SHARD)]


_W_IN_ZERO_ROWS = [(672, 32), (736, 32), (776, 248)]
W_IN_LANES = 256


def pad_w_in_t(shards):
    per_half = shards.shape[3] // W_IN_LANES

    def body(s_ref, o_ref):
        for r0, n in _W_IN_ZERO_ROWS:
            o_ref[r0:r0 + n, :] = jnp.zeros((n, W_IN_LANES), o_ref.dtype)
        for q, src, dst, n in _w_in_row_map():
            o_ref[dst:dst + n, :] = s_ref[q, src:src + n, :]

    return pl.pallas_call(
        body, name="pad_w_in_t", grid=(D_MODEL // W_IN_LANES,),
        in_specs=[BS((N_CHIPS, None, W_IN_SHARD, W_IN_LANES), lambda j: (0, j // per_half, 0, j % per_half))],
        out_specs=BS((N_PAD, W_IN_LANES), lambda j: (0, j)),
        out_shape=jax.ShapeDtypeStruct((N_PAD, D_MODEL), shards.dtype), compiler_params=_arb(1))(shards)


def unpad_w_in_t(g):
    def body(g_ref, o_ref):
        for q, src, dst, n in _w_in_row_map():
            o_ref[q, src:src + n, :] = g_ref[dst:dst + n, :]

    return pl.pallas_call(
        body, name="unpad_w_in_t", grid=(D_MODEL // W_IN_LANES,),
        in_specs=[BS((N_PAD, W_IN_LANES), lambda j: (0, j))], out_specs=BS((N_CHIPS, W_IN_SHARD, W_IN_LANES), lambda j: (0, 0, j)),
        out_shape=jax.ShapeDtypeStruct((N_CHIPS, W_IN_SHARD, D_MODEL), g.dtype), compiler_params=_arb(1))(g)


def _pad_w_q_b_t(s):
    z = jnp.zeros((32, s.shape[2]), s.dtype)
    parts = []
    for h in range(N_HEADS):
        parts += [s[h, :128], s[h, 128:160], z, s[h, 160:192], z]
    return jnp.concatenate(parts, axis=0)


def _unpad_w_q_b_t(g):
    return jnp.stack([jnp.concatenate([g[h * HEAD_PAD:h * HEAD_PAD + 128], g[h * HEAD_PAD + 128:h * HEAD_PAD + 160],
                                       g[h * HEAD_PAD + 192:h * HEAD_PAD + 224]]) for h in range(N_HEADS)])


def _perm_w_kv_b(s):
    return jnp.concatenate([s[h, :, :128] for h in range(N_HEADS)] + [s[h, :, 128:] for h in range(N_HEADS)], axis=1)


def _unperm_w_kv_b(g):
    return jnp.stack([jnp.concatenate([g[:, h * 128:(h + 1) * 128], g[:, 512 + h * 128:512 + (h + 1) * 128]], axis=1)
                      for h in range(N_HEADS)])


def _lane_row(v4):
    return jnp.pad(v4.reshape(1, -1).astype(F32), ((0, 0), (0, 128 - v4.size)))


def _pack(pieces, n_rows):
    flat = jnp.concatenate([p.reshape(-1) for p in pieces])
    return jnp.pad(flat, (0, n_rows * 1024 - flat.size)).reshape(n_rows, 1024)


def _unpack(block, shapes):
    flat = block.reshape(-1)
    out, off = [], 0
    for shp in shapes:
        n = int(np.prod(shp))
        out.append(flat[off:off + n].reshape(shp))
        off += n
    return out


N_CHIPS = 4
MESH = pl.DeviceIdType.MESH
ANY = BS(memory_space=pl.ANY)


def _place():
    return lax.axis_index("x"), lax.axis_index("y"), lax.axis_index("c")


def _other_chips(x, y):
    return [(1 - x, y), (x, 1 - y), (1 - x, 1 - y)]


def _half(split, which):
    axis, size = split
    ds = pl.ds(pl.multiple_of(which * size, 16 if axis == 0 else 128), size)
    return (ds, slice(None)) if axis == 0 else (slice(None), ds)


SEM = BS(memory_space=pltpu.SEMAPHORE)
HBM = BS(memory_space=pltpu.HBM)
_IN_HBM = lambda a: pltpu.with_memory_space_constraint(a, pltpu.HBM)
_SIDE_EFFECT = pltpu.SideEffectType.DATAFLOW_SIDE_EFFECTING


def _late_gather_copies(s_refs, l_refs, send_sems, recv_sems, local_sems, with_arrivals):
    x, y, c = _place()
    sends, recvs, locals_ = [], [], []
    for i, (s_ref, l_ref) in enumerate(zip(s_refs, l_refs)):
        locals_.append(pltpu.make_async_copy(s_ref, l_ref.at[2 * x + y], local_sems.at[i]))
        for j, (px, py) in enumerate(_other_chips(x, y)):
            k = 3 * i + j
            sends.append(pltpu.make_async_remote_copy(src_ref=s_ref, dst_ref=l_ref.at[2 * x + y], send_sem=send_sems.at[k],
                                                      recv_sem=recv_sems.at[k], device_id=(px, py, c), device_id_type=MESH))
            if with_arrivals:
                recvs.append(pltpu.make_async_remote_copy(src_ref=s_ref, dst_ref=l_ref.at[2 * px + py], send_sem=send_sems.at[k],
                                                          recv_sem=recv_sems.at[k], device_id=(px, py, c), device_id_type=MESH))
    return sends, recvs, locals_


def late_gather_start(shards, after, name):
    n = len(shards)

    def body(*refs):
        s_refs, l_refs = refs[:n], refs[n:2 * n]
        send_sems, recv_sems, local_sems = refs[2 * n + 1:2 * n + 4]
        token = refs[-1]
        sends, _, locals_ = _late_gather_copies(s_refs, l_refs, send_sems, recv_sems, local_sems, False)
        for cp in locals_ + sends:
            cp.start()
        token[...] = jnp.zeros_like(token)

    lands = [lax.empty((N_CHIPS,) + s.shape, s.dtype) for s in shards]
    hbm_like = lambda a: pltpu.HBM(a.shape, a.dtype)
    out = pl.pallas_call(
        body, name=name,
        out_shape=[pltpu.SemaphoreType.DMA((3 * n,)), pltpu.SemaphoreType.DMA((3 * n,)), pltpu.SemaphoreType.DMA((n,))]
        + [hbm_like(s) for s in shards] + [hbm_like(l) for l in lands] + [jax.ShapeDtypeStruct((8, 128), F32)],
        in_specs=[HBM] * (2 * n) + [BS(memory_space=pl.ANY)], out_specs=[SEM] * 3 + [HBM] * (2 * n) + [BS(memory_space=pltpu.VMEM)],
        input_output_aliases={i: 3 + i for i in range(2 * n)},
        compiler_params=pltpu.CompilerParams(has_side_effects=_SIDE_EFFECT))(
            *[_IN_HBM(s) for s in shards], *[_IN_HBM(l) for l in lands], after)
    return out[:3], out[3:3 + n], out[3 + n:3 + 2 * n], out[-1]


def late_gather_wait(sems, shards, lands, after, name):
    n = len(shards)

    def body(*refs):
        s_refs, l_refs = refs[:n], refs[n:2 * n]
        send_sems, recv_sems, local_sems = refs[2 * n:2 * n + 3]
        sends, recvs, locals_ = _late_gather_copies(s_refs, l_refs, send_sems, recv_sems, local_sems, True)
        for cp in locals_:
            cp.wait()
        for cp in sends:
            cp.wait_send()
        for cp in recvs:
            cp.wait_recv()

    hbm_like = lambda a: pltpu.HBM(a.shape, a.dtype)
    out = pl.pallas_call(
        body, name=name, out_shape=[hbm_like(s) for s in shards] + [hbm_like(l) for l in lands],
        in_specs=[HBM] * (2 * n) + [SEM] * 3 + [BS(memory_space=pl.ANY)], out_specs=[HBM] * (2 * n),
        input_output_aliases={i: i for i in range(2 * n)},
        compiler_params=pltpu.CompilerParams(has_side_effects=_SIDE_EFFECT))(*shards, *lands, *sems, after)
    return out[n:]


def _half_gather_copies(s_ref, l_ref, send_sems, recv_sems, with_arrivals):
    x, y, c = _place()
    sends, recvs = [], []
    for j, (px, py) in enumerate(_other_chips(x, y)):
        sends.append(pltpu.make_async_remote_copy(src_ref=s_ref.at[c], dst_ref=l_ref.at[2 * x + y, c], send_sem=send_sems.at[j],
                                                  recv_sem=recv_sems.at[j], device_id=(px, py, c), device_id_type=MESH))
        if with_arrivals:
            recvs.append(pltpu.make_async_remote_copy(src_ref=s_ref.at[c], dst_ref=l_ref.at[2 * px + py, c], send_sem=send_sems.at[j],
                                                      recv_sem=recv_sems.at[j], device_id=(px, py, c), device_id_type=MESH))
    return sends, recvs


def half_gather_start(shard, name):
    def body(s_ref, l_ref, send_sems, recv_sems, local_sem, s_thru, l_thru, token):
        x, y, _ = _place()
        pltpu.make_async_copy(s_ref, l_ref.at[2 * x + y], local_sem.at[0]).start()
        for cp in _half_gather_copies(s_ref, l_ref, send_sems, recv_sems, False)[0]:
            cp.start()
        token[...] = jnp.zeros_like(token)

    land = lax.empty((N_CHIPS,) + shard.shape, shard.dtype)
    out = pl.pallas_call(
        body, name=name,
        out_shape=[pltpu.SemaphoreType.DMA((3,)), pltpu.SemaphoreType.DMA((3,)), pltpu.SemaphoreType.DMA((1,)),
                   pltpu.HBM(shard.shape, shard.dtype), pltpu.HBM(land.shape, land.dtype), jax.ShapeDtypeStruct((8, 128), F32)],
        in_specs=[HBM, HBM], out_specs=[SEM] * 3 + [HBM, HBM, BS(memory_space=pltpu.VMEM)],
        input_output_aliases={0: 3, 1: 4},
        compiler_params=pltpu.CompilerParams(has_side_effects=_SIDE_EFFECT))(_IN_HBM(shard), _IN_HBM(land))
    return out[:3], out[3], out[4], out[5]


def half_gather_wait(sems, shard, land, after, name):
    def body(s_ref, l_ref, send_sems, recv_sems, local_sem, *rest):
        x, y, _ = _place()
        pltpu.make_async_copy(s_ref, l_ref.at[2 * x + y], local_sem.at[0]).wait()
        sends, recvs = _half_gather_copies(s_ref, l_ref, send_sems, recv_sems, True)
        for cp in sends:
            cp.wait_send()
        for cp in recvs:
            cp.wait_recv()

    out = pl.pallas_call(
        body, name=name, out_shape=[pltpu.HBM(shard.shape, shard.dtype), pltpu.HBM(land.shape, land.dtype)],
        in_specs=[HBM, HBM] + [SEM] * 3 + [BS(memory_space=pl.ANY)] * len(after), out_specs=[HBM, HBM],
        input_output_aliases={0: 0, 1: 1},
        compiler_params=pltpu.CompilerParams(has_side_effects=_SIDE_EFFECT))(shard, land, *sems, *after)
    return out[1]


def pass_halves_to_sibling(land, name):
    def body(l_in, l_ref, send_sems, recv_sems):
        x, y, c = _place()
        copies = []
        for j, (px, py) in enumerate(_other_chips(x, y)):
            q = 2 * px + py
            give = pltpu.make_async_remote_copy(src_ref=l_ref.at[q, c], dst_ref=l_ref.at[q, c], send_sem=send_sems.at[j],
                                                recv_sem=recv_sems.at[j], device_id=(x, y, 1 - c), device_id_type=MESH)
            take = pltpu.make_async_remote_copy(src_ref=l_ref.at[q, c], dst_ref=l_ref.at[q, 1 - c], send_sem=send_sems.at[j],
                                                recv_sem=recv_sems.at[j], device_id=(x, y, 1 - c), device_id_type=MESH)
            give.start()
            copies.append((give, take))
        for give, take in copies:
            take.wait_recv()
            give.wait_send()

    return pl.pallas_call(
        body, name=name, in_specs=[ANY], out_specs=ANY, out_shape=jax.ShapeDtypeStruct(land.shape, land.dtype),
        input_output_aliases={0: 0},
        scratch_shapes=[pltpu.SemaphoreType.DMA((3,)), pltpu.SemaphoreType.DMA((3,))])(land)


def allgather_devices(block, name):
    R, C = block.shape

    def body(b_ref, o_ref, send_sems, recv_sems, local_sem):
        x, y, c = _place()
        me = 4 * x + 2 * y + c
        own = pltpu.make_async_copy(b_ref, o_ref.at[me], local_sem)
        own.start()
        copies = []
        for r in range(1, 8):
            px = 1 - x if r & 4 else x
            py = 1 - y if r & 2 else y
            pc = 1 - c if r & 1 else c
            send = pltpu.make_async_remote_copy(src_ref=b_ref, dst_ref=o_ref.at[me], send_sem=send_sems.at[r - 1],
                                                recv_sem=recv_sems.at[r - 1], device_id=(px, py, pc), device_id_type=MESH)
            recv = pltpu.make_async_remote_copy(src_ref=b_ref, dst_ref=o_ref.at[4 * px + 2 * py + pc], send_sem=send_sems.at[r - 1],
                                                recv_sem=recv_sems.at[r - 1], device_id=(px, py, pc), device_id_type=MESH)
            send.start()
            copies.append((send, recv))
        for send, recv in copies:
            recv.wait_recv()
            send.wait_send()
        own.wait()

    return pl.pallas_call(
        body, name=name, in_specs=[ANY], out_specs=ANY, out_shape=jax.ShapeDtypeStruct((8, R, C), block.dtype),
        scratch_shapes=[pltpu.SemaphoreType.DMA((7,)), pltpu.SemaphoreType.DMA((7,)), pltpu.SemaphoreType.DMA(())])(block)


def swap_sibling(arrs, name, splits=None):
    n = len(arrs)

    def sent(a_ref, i, c):
        return a_ref if splits is None else a_ref.at[(slice(None),) + _half(splits[i], 1 - c)]

    def out_shape(a, i):
        if splits is None:
            return a.shape
        axis, size = splits[i]
        return (a.shape[0], size, a.shape[2]) if axis == 0 else (a.shape[0], a.shape[1], size)

    def body(*refs):
        a_refs, o_refs = refs[:n], refs[n:2 * n]
        send_sems, recv_sems = refs[2 * n:]
        x, y, c = _place()
        copies = [pltpu.make_async_remote_copy(src_ref=sent(a_ref, i, c), dst_ref=o_ref, send_sem=send_sems.at[i],
                                               recv_sem=recv_sems.at[i], device_id=(x, y, 1 - c), device_id_type=MESH)
                  for i, (a_ref, o_ref) in enumerate(zip(a_refs, o_refs))]
        for cp in copies:
            cp.start()
        for cp in copies:
            cp.wait()

    return pl.pallas_call(
        body, name=name, in_specs=[ANY] * n, out_specs=[ANY] * n,
        out_shape=[jax.ShapeDtypeStruct(out_shape(a, i), a.dtype) for i, a in enumerate(arrs)],
        scratch_shapes=[pltpu.SemaphoreType.DMA((n,)), pltpu.SemaphoreType.DMA((n,))])(*arrs)


def _exchange_copies(p_refs, l_refs, send_sems, recv_sems):
    x, y, c = _place()
    return [pltpu.make_async_remote_copy(src_ref=p_ref.at[2 * px + py], dst_ref=l_ref.at[j], send_sem=send_sems.at[3 * i + j],
                                         recv_sem=recv_sems.at[3 * i + j], device_id=(px, py, c), device_id_type=MESH)
            for i, (p_ref, l_ref) in enumerate(zip(p_refs, l_refs)) for j, (px, py) in enumerate(_other_chips(x, y))]


def exchange_chips_start(parts, name):
    n = len(parts)

    def body(*refs):
        send_sems, recv_sems = refs[2 * n:2 * n + 2]
        for cp in _exchange_copies(refs[:n], refs[n:2 * n], send_sems, recv_sems):
            cp.start()
        refs[-1][...] = jnp.zeros_like(refs[-1])

    lands = [lax.empty((3,) + p.shape[1:], p.dtype) for p in parts]
    hbm_like = lambda a: pltpu.HBM(a.shape, a.dtype)
    out = pl.pallas_call(
        body, name=name,
        out_shape=[pltpu.SemaphoreType.DMA((3 * n,)), pltpu.SemaphoreType.DMA((3 * n,))]
        + [hbm_like(p) for p in parts] + [hbm_like(l) for l in lands] + [jax.ShapeDtypeStruct((8, 128), F32)],
        in_specs=[HBM] * (2 * n), out_specs=[SEM] * 2 + [HBM] * (2 * n) + [BS(memory_space=pltpu.VMEM)],
        input_output_aliases={i: 2 + i for i in range(2 * n)},
        compiler_params=pltpu.CompilerParams(has_side_effects=_SIDE_EFFECT))(*[_IN_HBM(p) for p in parts], *[_IN_HBM(l) for l in lands])
    return out[:2], out[2:2 + n], out[2 + n:2 + 2 * n], out[-1]


def exchange_chips_wait(sems, parts, lands, after, name):
    n = len(parts)

    def body(*refs):
        send_sems, recv_sems = refs[2 * n:2 * n + 2]
        for cp in _exchange_copies(refs[:n], refs[n:2 * n], send_sems, recv_sems):
            cp.wait_send()
            cp.wait_recv()

    hbm_like = lambda a: pltpu.HBM(a.shape, a.dtype)
    out = pl.pallas_call(
        body, name=name, out_shape=[hbm_like(p) for p in parts] + [hbm_like(l) for l in lands],
        in_specs=[HBM] * (2 * n) + [SEM] * 2 + [BS(memory_space=pl.ANY)], out_specs=[HBM] * (2 * n),
        input_output_aliases={i: i for i in range(2 * n)},
        compiler_params=pltpu.CompilerParams(has_side_effects=_SIDE_EFFECT))(*parts, *lands, *sems, after)
    return out[n:]


def _half_block(shape2, split):
    axis, size = split
    return (size, shape2[1]) if axis == 0 else (shape2[0], size)


def add_pairs(parts, halves, splits, core, name):
    n = len(parts)

    def body(s_ref, *refs):
        for a_ref, b_ref, o_ref in zip(refs[:n], refs[n:2 * n], refs[2 * n:]):
            o_ref[...] = (a_ref[...].astype(F32) + b_ref[...].astype(F32)).astype(BF16)

    def mine(i):
        blk = (None,) + _half_block(parts[i].shape[1:], splits[i])
        if splits[i][0] == 0:
            return BS(blk, lambda q, s: (q, s[0], 0))
        return BS(blk, lambda q, s: (q, 0, s[0]))

    half_specs = [BS((None,) + h.shape[1:], lambda q, s: (q, 0, 0)) for h in halves]
    return pl.pallas_call(
        body, name=name,
        grid_spec=pltpu.PrefetchScalarGridSpec(num_scalar_prefetch=1, grid=(N_CHIPS,),
                                               in_specs=[mine(i) for i in range(n)] + half_specs, out_specs=half_specs),
        out_shape=[jax.ShapeDtypeStruct(h.shape, BF16) for h in halves], compiler_params=_arb(1))(core, *parts, *halves)


def add_fives(parts, halves, from_chips, splits, chip_core, name):
    n = len(parts)

    def body(s_ref, *refs):
        for a_ref, b_ref, p_ref, o_ref in zip(refs[:n], refs[n:2 * n], refs[2 * n:3 * n], refs[3 * n:]):
            s = a_ref[...].astype(F32) + b_ref[...].astype(F32)
            for j in range(3):
                s = s + p_ref[j].astype(F32)
            o_ref[...] = s

    def mine(i):
        blk = (None,) + _half_block(parts[i].shape[1:], splits[i])
        if splits[i][0] == 0:
            return BS(blk, lambda g, s: (s[0], s[1], 0))
        return BS(blk, lambda g, s: (s[0], 0, s[1]))

    half_specs = [BS((None,) + h.shape[1:], lambda g, s: (s[0], 0, 0)) for h in halves]
    chip_specs = [BS(p.shape, lambda g, s: (0, 0, 0)) for p in from_chips]
    out_specs = [BS(h.shape[1:], lambda g, s: (0, 0)) for h in halves]
    return pl.pallas_call(
        body, name=name,
        grid_spec=pltpu.PrefetchScalarGridSpec(num_scalar_prefetch=1, grid=(1,),
                                               in_specs=[mine(i) for i in range(n)] + half_specs + chip_specs, out_specs=out_specs),
        out_shape=[jax.ShapeDtypeStruct(h.shape[1:], F32) for h in halves], compiler_params=_arb(1))(chip_core, *parts, *halves, *from_chips)


def sum_leading(a, name):
    def body(a_ref, o_ref):
        s = a_ref[0]
        for j in range(1, a.shape[0]):
            s = s + a_ref[j]
        o_ref[...] = s

    return pl.pallas_call(body, name=name, out_shape=jax.ShapeDtypeStruct(a.shape[1:], a.dtype))(a)


def _adamw_math(w, g, m, v):
    mn = ADAM_B1 * m + (1.0 - ADAM_B1) * g
    vn = ADAM_B2 * v + (1.0 - ADAM_B2) * (g * g)
    m_hat = mn / (1.0 - ADAM_B1 ** ADAM_STEP)
    v_hat = vn / (1.0 - ADAM_B2 ** ADAM_STEP)
    return -ADAM_LR * (m_hat / (jnp.sqrt(v_hat) + ADAM_EPS) + ADAM_WD * w), mn, vn


def adamw(w, g, m, v, name):
    R, C = g.shape
    lead = (None,) * (w.ndim - 2)

    def body(w_ref, g_ref, m_ref, v_ref, d_ref, mo_ref, vo_ref):
        d_ref[...], mo_ref[...], vo_ref[...] = _adamw_math(w_ref[...], g_ref[...], m_ref[...], v_ref[...])

    wblk = BS(lead + (R, C), lambda i: (0,) * w.ndim)
    gblk = BS((R, C), lambda i: (0, 0))
    return pl.pallas_call(
        body, name=name, grid=(1,), in_specs=[wblk, gblk, wblk, wblk], out_specs=[wblk] * 3,
        out_shape=[jax.ShapeDtypeStruct(w.shape, F32)] * 3, compiler_params=_arb(1))(w, g, m, v)


def adamw_halves(w, mine, other, m, v, split, core, name):
    R, C = w.shape[-2:]
    axis, size = split
    lead = (None,) * (w.ndim - 2)
    zeros = (0,) * (w.ndim - 2)
    if axis == 0:
        tr = size if size <= 256 else next(t for t in range(256, 7, -1) if size % t == 0 and t % 8 == 0)
        nb = size // tr
        whole = BS(lead + (tr, C), lambda hi, j, s: zeros + (hi * nb + j, 0))
        part = BS((tr, C), lambda hi, j, s: (j, 0))
    else:
        nb = size // 128
        whole = BS(lead + (R, 128), lambda hi, j, s: zeros + (0, hi * nb + j))
        part = BS((R, 128), lambda hi, j, s: (0, j))

    def body(s_ref, w_ref, a_ref, b_ref, m_ref, v_ref, g_ref, d_ref, mo_ref, vo_ref):
        g = jnp.where(pl.program_id(0) == s_ref[0], a_ref[...], b_ref[...])
        g_ref[...] = g
        d_ref[...], mo_ref[...], vo_ref[...] = _adamw_math(w_ref[...], g, m_ref[...], v_ref[...])

    return pl.pallas_call(
        body, name=name,
        grid_spec=pltpu.PrefetchScalarGridSpec(num_scalar_prefetch=1, grid=(2, nb),
                                               in_specs=[whole, part, part, whole, whole], out_specs=[whole] * 4),
        out_shape=[jax.ShapeDtypeStruct(w.shape, F32)] * 4, compiler_params=_arb(2))(core, w, mine, other, m, v)


def dense_bf16(w3, name):
    R, _, K = w3.shape
    kh = K // 2

    def body(w_hbm, o_ref, buf, sem):
        cp = pltpu.make_async_copy(w_hbm.at[:, 0], buf, sem)
        cp.start()
        cp.wait()
        o_ref[0] = buf[:, :kh].astype(BF16)
        o_ref[1] = buf[:, kh:].astype(BF16)

    return pl.pallas_call(
        body, name=name, in_specs=[ANY], out_specs=BS(memory_space=pltpu.VMEM), out_shape=jax.ShapeDtypeStruct((2, R, kh), BF16),
        scratch_shapes=[pltpu.VMEM((R, K), F32), pltpu.SemaphoreType.DMA(())])(w3)


ROW_BLOCK = 184


def adamw_untiled_rows(w3, mine, other, m3, v3, name):
    R, _, K = w3.shape
    kh = K // 2
    starts = list(range(0, R, ROW_BLOCK))
    sizes = [min(ROW_BLOCK, R - s) for s in starts]
    nblk = len(starts)

    def body(w_hbm, a_ref, b_ref, m_hbm, v_hbm, g_hbm, d_hbm, mo_hbm, vo_hbm,
             wbuf, mbuf, vbuf, gbuf, dbuf, mobuf, vobuf, in_sems, out_sems):
        first = lax.axis_index("c") == 0
        ins = []
        for k, (r0, n) in enumerate(zip(starts, sizes)):
            rows = pl.ds(r0, n)
            cps = [pltpu.make_async_copy(src.at[rows, 0], dst.at[rows], in_sems.at[3 * k + i])
                   for i, (src, dst) in enumerate(((w_hbm, wbuf), (m_hbm, mbuf), (v_hbm, vbuf)))]
            for cp in cps:
                cp.start()
            ins.append(cps)

        def update(rows):
            a, b = a_ref[rows, :], b_ref[rows, :]
            g = jnp.concatenate([jnp.where(first, a, b), jnp.where(first, b, a)], axis=1)
            gbuf[rows, :] = g
            dbuf[rows, :], mobuf[rows, :], vobuf[rows, :] = _adamw_math(wbuf[rows, :], g, mbuf[rows, :], vbuf[rows, :])

        outs = []
        for k, (r0, n) in enumerate(zip(starts, sizes)):
            for cp in ins[k]:
                cp.wait()
            groups, tail = n // 8, n % 8

            def group(i, carry, r0=r0):
                update(pl.ds(pl.multiple_of(r0 + i * 8, 8), 8))
                return carry

            lax.fori_loop(0, groups, group, 0)
            if tail:
                update(pl.ds(r0 + groups * 8, tail))
            rows = pl.ds(r0, n)
            cps = [pltpu.make_async_copy(src.at[rows], dst.at[rows, 0], out_sems.at[4 * k + i])
                   for i, (src, dst) in enumerate(((gbuf, g_hbm), (dbuf, d_hbm), (mobuf, mo_hbm), (vobuf, vo_hbm)))]
            for cp in cps:
                cp.start()
            outs += cps
        for cp in outs:
            cp.wait()

    vmem = BS(memory_space=pltpu.VMEM)
    return pl.pallas_call(
        body, name=name, in_specs=[ANY, vmem, vmem, ANY, ANY], out_specs=[ANY] * 4,
        out_shape=[jax.ShapeDtypeStruct(w3.shape, F32)] * 4,
        scratch_shapes=[pltpu.VMEM((R, K), F32)] * 7 + [pltpu.SemaphoreType.DMA((3 * nblk,)), pltpu.SemaphoreType.DMA((4 * nblk,))])(
            w3, mine, other, m3, v3)


def adamw_w_q_b(w, mine, other, m, v, name):
    def body(w_ref, a_ref, b_ref, m_ref, v_ref, g_ref, d_ref, mo_ref, vo_ref):
        first = lax.axis_index("c") == 0
        lo = jnp.where(first, a_ref[...], b_ref[...])
        hi = jnp.where(first, b_ref[...], a_ref[...])
        g = jnp.concatenate([lo, hi[0:32], hi[64:96]], axis=0)
        g_ref[...] = g
        d_ref[...], mo_ref[...], vo_ref[...] = _adamw_math(w_ref[...], g, m_ref[...], v_ref[...])

    return pl.pallas_call(body, name=name, out_shape=[jax.ShapeDtypeStruct(w.shape, F32)] * 4)(w, mine, other, m, v)


def local_step(x, mem, positions, tgt, norm_in, weights, big_grads_ready, q_a_norm, kv_a_norm, gdn_conv, gdn_a_log,
               gdn_dt_bias, gdn_norm, mem_norm, norm_final):
    B, S, D = x.shape
    M = mem.shape[1]
    T = B * S
    N = S // CHUNK
    x2d = x.reshape(T, D)
    mem2d = mem.reshape(B * M, D)
    tgt2d = tgt.reshape(T, D)

    alog_row, dt_row = _lane_row(gdn_a_log), _lane_row(gdn_dt_bias)

    half = MLA_ROPE // 2
    inv_freq = 1.0 / (ROPE_THETA ** (jnp.arange(half, dtype=F32) / half))
    z32 = jnp.zeros((half,), F32)
    o32 = jnp.ones((half,), F32)
    inv_row = jnp.concatenate([inv_freq, z32, inv_freq, z32]).reshape(1, 128)
    sgn_row = jnp.concatenate([-o32, z32, o32, z32]).reshape(1, 128)
    msk_row = jnp.concatenate([o32, z32, o32, z32]).reshape(1, 128)
    cos_t, sin_t = rope_tables(positions.reshape(T, 1), inv_row, sgn_row, msk_row, after=weights[3])

    h = rms_fwd(x2d, norm_in, "rms_in", after=weights[3])
    wp, behind = weights[0]((h, cos_t))
    P = mm(h, wp, "nt", BF16, "in_proj", bm=512, bn=1536, n_outer=True, after=behind)
    wq, wkv = weights[1](P)
    Q, K, V, qn, kvn = mla_prep(P, q_a_norm, kv_a_norm, wq, wkv, cos_t, sin_t)
    o_mla, lse = mla_attn_fwd(Q, K, V, B, S)
    qkv = gdn_prep_fwd(P, gdn_conv, B, S)
    GB = gdn_gate_fwd(P, alog_row, dt_row, B, S)
    Grow = jnp.transpose(GB[:, :N_HEADS].reshape(B, N, CHUNK, N_HEADS), (0, 3, 1, 2))
    U, W, Tinv, A = gdn_chunk_fwd(qkv, GB, Grow, B, S)
    qkv3, GB3 = qkv.reshape(B, S, GDN_QKV), GB.reshape(B, S, 128)
    W3 = W.reshape(B, S, 512)
    o_gdn3, Vn3, St = gdn_scan_fwd(qkv3, U.reshape(B, S, 512), W3, GB3, A, B, S)
    o_gdn = o_gdn3.reshape(T, 512)
    w_mem_kv, w_out = weights[2](o_gdn)
    memn = rms_fwd(mem2d, mem_norm, "rms_mem")
    MKV = mm(memn, w_mem_kv, "nn", BF16, "mem_kv_proj")
    o_mem = mem_attn_fwd(P, MKV, B, S, M)
    mixed, dx2, dx2b, sq, g_norm_final = merge_fwd(o_mla, o_gdn, o_mem, P, x2d, tgt2d, w_out, gdn_norm, norm_final.reshape(1, D))

    g_w_out = mm(mixed, dx2b, "tn", BF16, "grad_w_out")
    dgate, do_mla, do_gdn, do_mem, g_gdn_norm = merge_bwd(dx2b, o_mla, o_gdn, o_mem, P, w_out, gdn_norm)

    dmemq, dMKV = mem_attn_bwd(P, MKV, do_mem, B, S, M)
    g_w_mem_kv = mm(memn, dMKV, "tn", BF16, "grad_w_mem_kv")
    started_early = big_grads_ready(dict(w_mem_kv=g_w_mem_kv, w_out=g_w_out), "early")
    dmemn = mm(dMKV, w_mem_kv, "nt", F32, "d_memn", after=(started_early,))
    g_mem_norm = gain_grad(mem2d, dmemn, "grad_mem_norm")

    dU3, dW3, dQ13, dK13, dA, dG13 = gdn_scan_bwd(do_gdn.reshape(B, S, 512), qkv3, W3, Vn3, GB3, A, St, B, S)
    r2 = lambda a: a.reshape(T, a.shape[-1])
    dqkv, dGB = gdn_chunk_bwd(qkv, GB, Grow, Tinv, dA, r2(dU3), r2(dW3), r2(dQ13), r2(dK13), r2(dG13), B, S)
    dPg, g_conv = gdn_prep_bwd(P, dqkv, gdn_conv, B, S)
    dab, g_ab = gdn_gate_bwd(P, dGB, alog_row, dt_row, B, S)

    dQ, dK, dV = mla_attn_bwd(Q, K, V, o_mla, do_mla, lse, B, S)
    dq_lin, dkv_lin, dPm, g_q_a_norm, g_kv_a_norm = mla_proj_bwd(dQ, dK, dV, cos_t, sin_t, P, dab, wq, wkv, q_a_norm, kv_a_norm)
    g_wq = mm(dq_lin, qn, "tn", BF16, "grad_w_q_b")
    g_wkv = mm(kvn, dkv_lin, "tn", BF16, "grad_w_kv_b")

    dP = [dPm, dmemq, dPg, dgate]
    g_wp = mm_cols_tn(dP, h, BF16, "grad_w_in")
    started = big_grads_ready(dict(w_in=g_wp, w_q_b=g_wq, w_kv_b=g_wkv), "late")
    grad_x, g_norm_in = in_proj_bwd(dP, wp, x2d, dx2, norm_in, started)

    grads = dict(
        norm_in=g_norm_in, q_a_norm=g_q_a_norm, kv_a_norm=g_kv_a_norm, gdn_conv=g_conv,
        gdn_a_log=g_ab[0:1, :N_HEADS], gdn_dt_bias=g_ab[1:2, :N_HEADS], gdn_norm=g_gdn_norm,
        mem_norm=g_mem_norm, norm_final=g_norm_final)
    return sq, grad_x.reshape(B, S, D), grads


def kernel(x, mem, positions, norm_in, w_in, q_a_norm, w_q_b, kv_a_norm, w_kv_b, gdn_conv, gdn_a_log, gdn_dt_bias, gdn_norm, mem_norm, w_mem_kv, w_out, norm_final, loss_target, m_norm_in, m_w_in, m_q_a_norm, m_w_q_b, m_kv_a_norm, m_w_kv_b, m_gdn_conv, m_gdn_a_log, m_gdn_dt_bias, m_gdn_norm, m_mem_norm, m_w_mem_kv, m_w_out, m_norm_final, v_norm_in, v_w_in, v_q_a_norm, v_w_q_b, v_kv_a_norm, v_w_kv_b, v_gdn_conv, v_gdn_a_log, v_gdn_dt_bias, v_gdn_norm, v_mem_norm, v_w_mem_kv, v_w_out, v_norm_final):
    B = x.shape[0]
    cx, cy, cc = lax.axis_index("x"), lax.axis_index("y"), lax.axis_index("c")
    chip = 2 * cx + cy

    big_names = ("w_in", "w_q_b", "w_kv_b", "w_mem_kv", "w_out")
    rows_major = lambda a: jnp.transpose(a, (2, 0, 1))
    w_in3, m_in3, v_in3 = rows_major(w_in), rows_major(m_w_in), rows_major(v_w_in)
    w_qb_t, m_qb_t, v_qb_t = jnp.transpose(w_q_b[0]), jnp.transpose(m_w_q_b[0]), jnp.transpose(v_w_q_b[0])
    z32 = jnp.zeros((32, Q_LORA), BF16)
    qb_bf = w_qb_t.astype(BF16)
    qb_padded = jnp.concatenate([qb_bf[:160], z32, qb_bf[160:], z32])
    shards = [dense_bf16(w_in3, "w_in_bf16"), qb_padded, w_kv_b[0].astype(BF16), w_mem_kv[0].astype(BF16), w_out[0].astype(BF16)]
    splits = [(1, D_MODEL // 2)] + [(0, s.shape[0] // 2) for s in shards[1:]]
    *w_in_flight, w_in_started = half_gather_start(shards[0], "w_in_gather_start")
    conv_all = allgather_devices(gdn_conv[0], "allgather_conv")
    conv_cols = gdn_conv.shape[2]
    conv_full = jnp.transpose(conv_all[0::2], (1, 0, 2)).reshape(GDN_CONV, N_CHIPS * conv_cols)
    late_shapes = [(N_CHIPS,) + s.shape for s in shards[1:]]
    late = {}

    def w_in_ready(after):
        g_in = pass_halves_to_sibling(half_gather_wait(*w_in_flight, after, "w_in_gather_wait"), "w_in_gather_sibling")
        *late["a"], started_a = late_gather_start(shards[1:3], g_in, "late_gather_qkv_start")
        *late["b"], started_b = late_gather_start(shards[3:], started_a, "late_gather_mem_out_start")
        return pad_w_in_t(g_in), (started_a, started_b)

    def late_qkv(after):
        g_qb, g_kvb = late_gather_wait(*late["a"], after, "late_gather_qkv_wait")
        return g_qb.reshape(-1, Q_LORA), _perm_w_kv_b(g_kvb)

    def late_mem_out(after):
        g_mem, g_out_w = late_gather_wait(*late["b"], after, "late_gather_mem_out_wait")
        return g_mem.reshape(-1, g_mem.shape[2]), g_out_w.reshape(-1, g_out_w.shape[2])

    weights = (w_in_ready, late_qkv, late_mem_out, (w_in_started,))

    core = jnp.stack([cc]).astype(jnp.int32)
    chip_core = jnp.stack([chip, cc]).astype(jnp.int32)
    exchanges = {}
    by_chip = dict(w_in=unpad_w_in_t, w_q_b=lambda a: a.reshape(late_shapes[0]), w_kv_b=_unperm_w_kv_b,
                   w_mem_kv=lambda a: a.reshape(late_shapes[2]), w_out=lambda a: a.reshape(late_shapes[3]))

    def big_grads_ready(gb, group):
        idx = [big_names.index(n) for n in gb]
        parts = [by_chip[n](a) for n, a in gb.items()]
        sp = [splits[i] for i in idx]
        from_sibling = swap_sibling(parts, "rs_sibling_partial_" + group, sp)
        chip_sums = add_pairs(parts, from_sibling, sp, core, "rs_add_sibling_" + group)
        sems, sums_thru, lands, token = exchange_chips_start(chip_sums, "rs_exchange_start_" + group)
        exchanges[group] = dict(idx=idx, parts=parts, from_sibling=from_sibling, sems=sems, sums=sums_thru, lands=lands)
        return token

    sq, grad_x, g = local_step(x, mem, positions, loss_target, norm_in, weights, big_grads_ready, q_a_norm, kv_a_norm, conv_full,
                               gdn_a_log, gdn_dt_bias, gdn_norm, mem_norm, norm_final)

    small_names = ("norm_in", "q_a_norm", "kv_a_norm", "gdn_a_log", "gdn_dt_bias", "gdn_norm", "mem_norm", "norm_final")
    small = dict(norm_in=norm_in, q_a_norm=q_a_norm, kv_a_norm=kv_a_norm, gdn_a_log=gdn_a_log, gdn_dt_bias=gdn_dt_bias,
                 gdn_norm=gdn_norm, mem_norm=mem_norm, norm_final=norm_final)
    m_small = dict(norm_in=m_norm_in, q_a_norm=m_q_a_norm, kv_a_norm=m_kv_a_norm, gdn_a_log=m_gdn_a_log,
                   gdn_dt_bias=m_gdn_dt_bias, gdn_norm=m_gdn_norm, mem_norm=m_mem_norm, norm_final=m_norm_final)
    v_small = dict(norm_in=v_norm_in, q_a_norm=v_q_a_norm, kv_a_norm=v_kv_a_norm, gdn_a_log=v_gdn_a_log,
                   gdn_dt_bias=v_gdn_dt_bias, gdn_norm=v_gdn_norm, mem_norm=v_mem_norm, norm_final=v_norm_final)
    rows = lambda d: jnp.stack([jnp.pad(d[n].reshape(-1), (0, 1024 - d[n].size)) for n in small_names])
    conv_rows = GDN_CONV * GDN_QKV // 1024
    g_block = jnp.concatenate([rows(g), g["gdn_conv"].reshape(conv_rows, 1024), sq, jnp.zeros((16 - 9 - conv_rows, 1024), F32)])
    g_block = sum_leading(allgather_devices(g_block, "allgather_small_grads"), "sum_small_grads")
    g_small_rows = g_block[:8]
    loss = 0.5 * jnp.sum(g_block[8 + conv_rows]) / D_MODEL
    g_conv = lax.dynamic_slice_in_dim(g_block[8:8 + conv_rows].reshape(GDN_CONV, GDN_QKV), chip * conv_cols, conv_cols, axis=1)
    d_s, m_s, v_s = adamw(rows(small), g_small_rows, rows(m_small), rows(v_small), "adamw_small")
    unrow = lambda r: {n: r[i, :small[n].size].reshape(small[n].shape) for i, n in enumerate(small_names)}
    g_out, d_out, m_out, v_out = unrow(g_small_rows), unrow(d_s), unrow(m_s), unrow(v_s)

    my_half = [None] * len(big_names)
    for group, e in exchanges.items():
        from_chips = exchange_chips_wait(e["sems"], e["sums"], e["lands"], d_s, "rs_exchange_wait_" + group)
        halves = add_fives(e["parts"], e["from_sibling"], from_chips, [splits[i] for i in e["idx"]], chip_core, "rs_add_chips_" + group)
        for i, a in zip(e["idx"], halves):
            my_half[i] = a
    other_half = swap_sibling(my_half, "rs_sibling_final")

    d_out["gdn_conv"], m_out["gdn_conv"], v_out["gdn_conv"] = adamw(gdn_conv, g_conv, m_gdn_conv, v_gdn_conv, "adamw_gdn_conv")
    g_out["gdn_conv"] = g_conv[None]
    res = adamw_untiled_rows(w_in3, my_half[0], other_half[0], m_in3, v_in3, "adamw_w_in")
    g_out["w_in"], d_out["w_in"], m_out["w_in"], v_out["w_in"] = [jnp.transpose(r, (1, 2, 0)) for r in res]
    res = adamw_w_q_b(w_qb_t, my_half[1], other_half[1], m_qb_t, v_qb_t, "adamw_w_q_b")
    g_out["w_q_b"], d_out["w_q_b"], m_out["w_q_b"], v_out["w_q_b"] = [jnp.transpose(r)[None] for r in res]
    rest = dict(w_kv_b=(w_kv_b, m_w_kv_b, v_w_kv_b), w_mem_kv=(w_mem_kv, m_w_mem_kv, v_w_mem_kv), w_out=(w_out, m_w_out, v_w_out))
    for i, n in enumerate(big_names):
        if n in rest:
            w_n, m_n, v_n = rest[n]
            g_out[n], d_out[n], m_out[n], v_out[n] = adamw_halves(w_n, my_half[i], other_half[i], m_n, v_n, splits[i], core, "adamw_" + n)

    order = ("norm_in", "w_in", "q_a_norm", "w_q_b", "kv_a_norm", "w_kv_b", "gdn_conv", "gdn_a_log", "gdn_dt_bias",
             "gdn_norm", "mem_norm", "w_mem_kv", "w_out", "norm_final")
    return (loss, grad_x, *[g_out[n] for n in order], *[d_out[n] for n in order], *[m_out[n] for n in order],
            *[v_out[n] for n in order])
```

```python
import jax
import jax.numpy as jnp
from jax import lax
from jax.experimental import pallas as pl
from jax.experimental.pallas import tpu as pltpu

F32 = jnp.float32
BF16 = jnp.bfloat16
BS = pl.BlockSpec

D_MODEL = 1024
N_HEADS = 4
MLA_NOPE, MLA_ROPE, MLA_V = 128, 64, 128
Q_LORA, KV_LORA = 384, 256
ROPE_THETA = 10000.0
GDN_DK = GDN_DV = 128
GDN_CONV = 4
CHUNK = 64
MEM_DH = 128
D_MIX = 1536
GDN_QKV = 1536
D_IN = 4296
EPS = 1e-6
ADAM_LR, ADAM_B1, ADAM_B2, ADAM_EPS, ADAM_WD, ADAM_STEP = 0.001, 0.9, 0.999, 1e-08, 0.01, 10

OFF_MLA = 0
OFF_MEMQ = 1024
OFF_GDN = 1536
OFF_GATE = 3072
N_PAD = 4608
HEAD_PAD = 256
MLA_SCALE = (MLA_NOPE + MLA_ROPE) ** -0.5
MEM_SCALE = MEM_DH ** -0.5
GDN_SCALE = GDN_DK ** -0.5
NEG = -1e30

NN = ((1,), (0,))
NT = ((1,), (1,))
TN = ((0,), (0,))


def _dot(a, b, dims):
    return lax.dot_general(a, b, (dims, ((), ())), preferred_element_type=F32)


def _bdot(spec, a, b, precision=None):
    return jnp.einsum(spec, a, b, preferred_element_type=F32, precision=precision)


def _arb(n):
    return pltpu.CompilerParams(dimension_semantics=("arbitrary",) * n)


def _sigmoid(x):
    return 1.0 / (1.0 + jnp.exp(-x))


def _softplus(z):
    return jnp.maximum(z, 0.0) + jnp.log(1.0 + jnp.exp(-jnp.abs(z)))


def _rope(t, cos_row, sin_row):
    return t * cos_row + pltpu.roll(t, 64, 1) * sin_row


def _rope_bwd(d, cos_row, sin_row):
    return d * cos_row + pltpu.roll(d * sin_row, 64, 1)


def rms_fwd(x, gain, name, tm=512, after=()):
    T, n = x.shape
    tm = min(tm, T)

    def body(x_ref, g_ref, *rest):
        xv = x_ref[...]
        r = lax.rsqrt(jnp.mean(xv * xv, axis=-1, keepdims=True) + EPS)
        rest[-1][...] = (xv * r * g_ref[...]).astype(BF16)

    return pl.pallas_call(
        body, name=name, grid=(T // tm,),
        in_specs=[BS((tm, n), lambda i: (i, 0)), BS((1, n), lambda i: (0, 0))] + [BS(memory_space=pl.ANY)] * len(after),
        out_specs=BS((tm, n), lambda i: (i, 0)),
        out_shape=jax.ShapeDtypeStruct((T, n), BF16), compiler_params=_arb(1))(x, gain, *after)


def mm(a, b, kind, out_dtype, name, bm=512, bn=None, n_outer=False, after=()):
    if kind == "nn":
        (M, K), (_, N) = a.shape, b.shape
    elif kind == "nt":
        (M, K), (N, _) = a.shape, b.shape
    else:
        (K, M), (_, N) = a.shape, b.shape
    bm, bn = min(bm, M), min(bn or N, N)
    assert M % bm == 0 and N % bn == 0, (name, M, N, K)
    ij = (lambda g0, g1: (g1, g0)) if n_outer else (lambda g0, g1: (g0, g1))
    a_spec = BS((K, bm), lambda g0, g1: (0, ij(g0, g1)[0])) if kind == "tn" else BS((bm, K), lambda g0, g1: (ij(g0, g1)[0], 0))
    once = dict(pipeline_mode=pl.Buffered(1)) if bn == N else {}
    b_spec = (BS((bn, K), lambda g0, g1: (ij(g0, g1)[1], 0), **once) if kind == "nt"
              else BS((K, bn), lambda g0, g1: (0, ij(g0, g1)[1]), **once))
    dims = {"nn": NN, "nt": NT, "tn": TN}[kind]

    def body(a_ref, b_ref, *rest):
        rest[-1][...] = _dot(a_ref[...].astype(BF16), b_ref[...].astype(BF16), dims).astype(out_dtype)

    grid = (N // bn, M // bm) if n_outer else (M // bm, N // bn)
    return pl.pallas_call(
        body, name=name, grid=grid, in_specs=[a_spec, b_spec] + [BS(memory_space=pl.ANY)] * len(after),
        out_specs=BS((bm, bn), lambda g0, g1: ij(g0, g1)),
        out_shape=jax.ShapeDtypeStruct((M, N), out_dtype), compiler_params=_arb(2))(a, b, *after)


def mm_cols_tn(pieces, b, out_dtype, name, bm=512):
    K, N = b.shape
    tiles = [p.shape[1] // bm for p in pieces]
    firsts = [sum(tiles[:i]) for i in range(len(tiles))]

    def body(*refs):
        b_ref, o_ref = refs[-2], refs[-1]
        i = pl.program_id(0)
        for a_ref, t0, n in zip(refs[:-2], firsts, tiles):
            @pl.when((i >= t0) & (i < t0 + n))
            def _(a_ref=a_ref):
                o_ref[...] = _dot(a_ref[...], b_ref[...], TN).astype(out_dtype)

    a_specs = [BS((K, bm), lambda i, t0=t0, n=n: (0, jnp.clip(i - t0, 0, n - 1))) for t0, n in zip(firsts, tiles)]
    return pl.pallas_call(
        body, name=name, grid=(sum(tiles),),
        in_specs=a_specs + [BS(b.shape, lambda i: (0, 0), pipeline_mode=pl.Buffered(1))],
        out_specs=BS((bm, N), lambda i: (i, 0)), out_shape=jax.ShapeDtypeStruct((sum(tiles) * bm, N), out_dtype),
        compiler_params=_arb(1))(*pieces, b)


def rope_tables(pos_col, inv_row, sgn_row, msk_row, tm=512, after=()):
    T = pos_col.shape[0]
    tm = min(tm, T)

    def body(p_ref, inv_ref, sgn_ref, msk_ref, *rest):
        c_ref, s_ref = rest[-2:]
        ang = p_ref[...].astype(F32) * inv_ref[...]
        c_ref[...] = jnp.cos(ang) * msk_ref[...]
        s_ref[...] = jnp.sin(ang) * sgn_ref[...]

    row = BS((1, 128), lambda i: (0, 0))
    return pl.pallas_call(
        body, name="rope_tables", grid=(T // tm,),
        in_specs=[BS((tm, 1), lambda i: (i, 0)), row, row, row] + [BS(memory_space=pl.ANY)] * len(after),
        out_specs=[BS((tm, 128), lambda i: (i, 0))] * 2,
        out_shape=[jax.ShapeDtypeStruct((T, 128), F32)] * 2, compiler_params=_arb(1))(pos_col, inv_row, sgn_row, msk_row, *after)


def mla_prep(P, gq, gkv, wq, wkv, cos_t, sin_t, tm=512):
    T = P.shape[0]
    tm = min(tm, T)

    def body(p_ref, gq_ref, gkv_ref, wq_ref, wkv_ref, c_ref, s_ref, q_ref, k_ref, v_ref, qn_ref, kvn_ref):
        p = p_ref[...].astype(F32)
        cq, ckv, kr = p[:, :Q_LORA], p[:, Q_LORA:Q_LORA + KV_LORA], p[:, 640:768]
        qn = (cq * lax.rsqrt(jnp.mean(cq * cq, axis=-1, keepdims=True) + EPS) * gq_ref[...]).astype(BF16)
        kvn = (ckv * lax.rsqrt(jnp.mean(ckv * ckv, axis=-1, keepdims=True) + EPS) * gkv_ref[...]).astype(BF16)
        qn_ref[...] = qn
        kvn_ref[...] = kvn
        q = _dot(qn, wq_ref[...], NT)
        kv = _dot(kvn, wkv_ref[...], NN)
        cos_row, sin_row = c_ref[...], s_ref[...]
        krr = _rope(kr, cos_row, sin_row).astype(BF16)
        for h in range(N_HEADS):
            lo = h * HEAD_PAD
            q_ref[:, lo:lo + 128] = (q[:, lo:lo + 128] * MLA_SCALE).astype(BF16)
            q_ref[:, lo + 128:lo + 256] = (_rope(q[:, lo + 128:lo + 256], cos_row, sin_row) * MLA_SCALE).astype(BF16)
            k_ref[:, lo:lo + 128] = kv[:, h * 128:(h + 1) * 128].astype(BF16)
            k_ref[:, lo + 128:lo + 256] = krr
            v_ref[:, lo:lo + 128] = kv[:, 512 + h * 128:512 + (h + 1) * 128].astype(BF16)
            v_ref[:, lo + 128:lo + 256] = jnp.ones((tm, 128), BF16)

    full = lambda r, c: BS((r, c), lambda i: (0, 0))
    rowb = lambda c: BS((tm, c), lambda i: (i, 0))
    return pl.pallas_call(
        body, name="mla_prep", grid=(T // tm,),
        in_specs=[rowb(1024), full(1, Q_LORA), full(1, KV_LORA), full(1024, Q_LORA), full(KV_LORA, 1024), rowb(128), rowb(128)],
        out_specs=[rowb(1024), rowb(1024), rowb(1024), rowb(Q_LORA), rowb(KV_LORA)],
        out_shape=[jax.ShapeDtypeStruct((T, 1024), BF16), jax.ShapeDtypeStruct((T, 1024), BF16),
                   jax.ShapeDtypeStruct((T, 1024), BF16), jax.ShapeDtypeStruct((T, Q_LORA), BF16),
                   jax.ShapeDtypeStruct((T, KV_LORA), BF16)],
        compiler_params=_arb(1))(P, gq, gkv, wq, wkv, cos_t, sin_t)


ATTN_HEADS_PER_STEP = 2
ATTN_STRIP = 32


def mla_attn_fwd(Q, K, V, B, S, tq=512, hp=ATTN_HEADS_PER_STEP):
    T = B * S
    tq = min(tq, S)
    nq = S // tq

    rs = min(ATTN_STRIP, tq)

    def body(q_ref, k_ref, v_ref, o_ref, lse_ref, m_s, acc_s, s_s, p_s, a_s):
        i = pl.program_id(2)
        m_s[...] = jnp.full_like(m_s, NEG)
        acc_s[...] = jnp.zeros_like(acc_s)

        def blk(j, masked):
            rows = pl.ds(pl.multiple_of(j * tq, tq), tq)
            for h in range(hp):
                hq = slice(h * HEAD_PAD, (h + 1) * HEAD_PAD)
                s_s[h] = _dot(q_ref[:, hq], k_ref[rows, hq], NT)
            for h in range(hp):
                for r0 in range(0, tq, rs):
                    rr = slice(r0, r0 + rs)
                    sv = s_s[h, rr, :]
                    if masked:
                        r = r0 + lax.broadcasted_iota(jnp.int32, (rs, tq), 0)
                        c = lax.broadcasted_iota(jnp.int32, (rs, tq), 1)
                        sv = jnp.where(r >= c, sv, NEG)
                    m_prev = m_s[h, rr, :]
                    m_new = jnp.maximum(m_prev, jnp.max(sv, axis=1, keepdims=True))
                    p_s[h, rr, :] = jnp.exp(sv - m_new).astype(BF16)
                    a_s[h, rr, :] = jnp.exp(m_prev - m_new)
                    m_s[h, rr, :] = m_new
            for h in range(hp):
                hq = slice(h * HEAD_PAD, (h + 1) * HEAD_PAD)
                acc_s[h] = a_s[h] * acc_s[h] + _dot(p_s[h], v_ref[rows, hq], NN)

        def loop(j, c):
            blk(j, False)
            return c

        lax.fori_loop(0, i, loop, 0)
        blk(i, True)
        for h in range(hp):
            den = acc_s[h, :, 128:256]
            o_ref[:, h * 128:(h + 1) * 128] = (acc_s[h, :, 0:128] / den).astype(BF16)
            lse_ref[h] = m_s[h] + jnp.log(den[:, 0:1])

    return pl.pallas_call(
        body, name="mla_attn_fwd", grid=(B, N_HEADS // hp, nq),
        in_specs=[BS((tq, hp * HEAD_PAD), lambda b, h, i: (b * nq + i, h)),
                  BS((S, hp * HEAD_PAD), lambda b, h, i: (b, h)),
                  BS((S, hp * HEAD_PAD), lambda b, h, i: (b, h))],
        out_specs=[BS((tq, hp * 128), lambda b, h, i: (b * nq + i, h)),
                   BS((hp, tq, 1), lambda b, h, i: (h, b * nq + i, 0))],
        out_shape=[jax.ShapeDtypeStruct((T, 512), BF16), jax.ShapeDtypeStruct((N_HEADS, T, 1), F32)],
        scratch_shapes=[pltpu.VMEM((hp, tq, 1), F32), pltpu.VMEM((hp, tq, HEAD_PAD), F32), pltpu.VMEM((hp, tq, tq), F32),
                        pltpu.VMEM((hp, tq, tq), BF16), pltpu.VMEM((hp, tq, 1), F32)],
        compiler_params=_arb(3))(Q, K, V)


def mla_attn_bwd(Q, K, V, O, dO, LSE, B, S, tq=512, hp=ATTN_HEADS_PER_STEP):
    T = B * S
    tq = min(tq, S)
    nq = S // tq

    rs = min(ATTN_STRIP, tq)

    def body(q_ref, k_ref, v_ref, o_ref, do_ref, lse_ref, dq_ref, dk_ref, dv_ref, delta_s, dq_s, dk_s, dv_s, s_s, dp_s, p_s, ds_s):
        j = pl.program_id(2)

        @pl.when(j == 0)
        def _():
            dq_s[...] = jnp.zeros_like(dq_s)
            for h in range(hp):
                sl = slice(h * 128, (h + 1) * 128)
                delta_s[h] = jnp.sum(do_ref[:, sl] * o_ref[:, sl].astype(F32), axis=1, keepdims=True)

        dk_s[...] = jnp.zeros_like(dk_s)
        dv_s[...] = jnp.zeros_like(dv_s)

        def step(i, c):
            rows = pl.ds(pl.multiple_of(i * tq, tq), tq)
            for h in range(hp):
                sq, sv = slice(h * HEAD_PAD, (h + 1) * HEAD_PAD), slice(h * 128, (h + 1) * 128)
                s_s[h] = _dot(q_ref[rows, sq], k_ref[:, sq], NT)
                dp_s[h] = _dot(do_ref[rows, sv].astype(BF16), v_ref[:, h * HEAD_PAD:h * HEAD_PAD + 128], NT)
            for h in range(hp):
                for r0 in range(0, tq, rs):
                    rr = slice(r0, r0 + rs)
                    seq_rows = pl.ds(pl.multiple_of(i * tq + r0, rs), rs)
                    r = i * tq + r0 + lax.broadcasted_iota(jnp.int32, (rs, tq), 0)
                    cc = j * tq + lax.broadcasted_iota(jnp.int32, (rs, tq), 1)
                    p = jnp.where(r >= cc, jnp.exp(s_s[h, rr, :] - lse_ref[h, seq_rows, :]), 0.0)
                    p_s[h, rr, :] = p.astype(BF16)
                    ds_s[h, rr, :] = (p * (dp_s[h, rr, :] - delta_s[h, seq_rows, :])).astype(BF16)
            for h in range(hp):
                sq, sv = slice(h * HEAD_PAD, (h + 1) * HEAD_PAD), slice(h * 128, (h + 1) * 128)
                dv_s[:, sv] += _dot(p_s[h], do_ref[rows, sv].astype(BF16), TN)
                dk_s[:, sq] += _dot(ds_s[h], q_ref[rows, sq], TN)
                dq_s[rows, sq] += _dot(ds_s[h], k_ref[:, sq], NN)
            return c

        lax.fori_loop(j, nq, step, 0)
        dk_ref[...] = dk_s[...].astype(BF16)
        dv_ref[...] = dv_s[...].astype(BF16)

        @pl.when(j == nq - 1)
        def _():
            dq_ref[...] = dq_s[...].astype(BF16)

    seq = lambda c: BS((S, c), lambda b, h, j: (b, h))
    blk = lambda c: BS((tq, c), lambda b, h, j: (b * nq + j, h))
    return pl.pallas_call(
        body, name="mla_attn_bwd", grid=(B, N_HEADS // hp, nq),
        in_specs=[seq(hp * HEAD_PAD), blk(hp * HEAD_PAD), blk(hp * HEAD_PAD), seq(hp * 128), seq(hp * 128),
                  BS((hp, S, 1), lambda b, h, j: (h, b, 0))],
        out_specs=[seq(hp * HEAD_PAD), blk(hp * HEAD_PAD), blk(hp * 128)],
        out_shape=[jax.ShapeDtypeStruct((T, 1024), BF16), jax.ShapeDtypeStruct((T, 1024), BF16),
                   jax.ShapeDtypeStruct((T, 512), BF16)],
        scratch_shapes=[pltpu.VMEM((hp, S, 1), F32), pltpu.VMEM((S, hp * HEAD_PAD), F32),
                        pltpu.VMEM((tq, hp * HEAD_PAD), F32), pltpu.VMEM((tq, hp * 128), F32),
                        pltpu.VMEM((hp, tq, tq), F32), pltpu.VMEM((hp, tq, tq), F32),
                        pltpu.VMEM((hp, tq, tq), BF16), pltpu.VMEM((hp, tq, tq), BF16)],
        compiler_params=_arb(3))(Q, K, V, O, dO, LSE)


def mla_proj_bwd(dQ, dK, dV, cos_t, sin_t, P, dab, wq, wkv, gq, gkv, tm=512):
    T = P.shape[0]
    tm = min(tm, T)

    def norm_bwd(x, dy, g):
        r = lax.rsqrt(jnp.mean(x * x, axis=-1, keepdims=True) + EPS)
        xh = x * r
        dxh = dy * g
        return r * (dxh - xh * jnp.mean(dxh * xh, axis=-1, keepdims=True)), jnp.sum(dy * xh, axis=0, keepdims=True)

    def body(dq_ref, dk_ref, dv_ref, c_ref, s_ref, p_ref, dab_ref, wq_ref, wkv_ref, gq_ref, gkv_ref,
             ql_ref, kvl_ref, o_ref, aq_ref, akv_ref):
        @pl.when(pl.program_id(0) == 0)
        def _():
            aq_ref[...] = jnp.zeros_like(aq_ref)
            akv_ref[...] = jnp.zeros_like(akv_ref)

        cos_row, sin_row = c_ref[...], s_ref[...]
        kr = jnp.zeros((tm, 128), F32)
        for h in range(N_HEADS):
            lo = h * HEAD_PAD
            ql_ref[:, lo:lo + 128] = (dq_ref[:, lo:lo + 128].astype(F32) * MLA_SCALE).astype(BF16)
            ql_ref[:, lo + 128:lo + 256] = (_rope_bwd(dq_ref[:, lo + 128:lo + 256].astype(F32), cos_row, sin_row) * MLA_SCALE).astype(BF16)
            kvl_ref[:, h * 128:(h + 1) * 128] = dk_ref[:, lo:lo + 128]
            kr = kr + dk_ref[:, lo + 128:lo + 256].astype(F32)
        kvl_ref[:, 512:] = dv_ref[...]
        dqn = _dot(ql_ref[...], wq_ref[...], NN)
        dkvn = _dot(kvl_ref[...], wkv_ref[...], NT)
        dcq, ggq = norm_bwd(p_ref[:, :Q_LORA].astype(F32), dqn, gq_ref[...])
        dckv, ggkv = norm_bwd(p_ref[:, Q_LORA:640].astype(F32), dkvn, gkv_ref[...])
        aq_ref[...] += ggq
        akv_ref[...] += ggkv
        o_ref[:, :Q_LORA] = dcq.astype(BF16)
        o_ref[:, Q_LORA:640] = dckv.astype(BF16)
        o_ref[:, 640:768] = _rope_bwd(kr, cos_row, sin_row).astype(BF16)
        o_ref[:, 768:896] = dab_ref[...]
        o_ref[:, 896:1024] = jnp.zeros((tm, 128), BF16)

    rowb = lambda c: BS((tm, c), lambda i: (i, 0))
    full = lambda r, c: BS((r, c), lambda i: (0, 0))
    return pl.pallas_call(
        body, name="mla_proj_bwd", grid=(T // tm,),
        in_specs=[rowb(1024), rowb(1024), rowb(512), rowb(128), rowb(128), rowb(1024), rowb(128),
                  full(1024, Q_LORA), full(KV_LORA, 1024), full(1, Q_LORA), full(1, KV_LORA)],
        out_specs=[rowb(1024), rowb(1024), rowb(1024), full(1, Q_LORA), full(1, KV_LORA)],
        out_shape=[jax.ShapeDtypeStruct((T, 1024), BF16)] * 3
        + [jax.ShapeDtypeStruct((1, Q_LORA), F32), jax.ShapeDtypeStruct((1, KV_LORA), F32)],
        compiler_params=_arb(1))(dQ, dK, dV, cos_t, sin_t, P, dab, wq, wkv, gq, gkv)


def _mem_probs(qh, kh):
    s = _dot(qh, kh, NT) * MEM_SCALE
    p = jnp.exp(s - jnp.max(s, axis=1, keepdims=True))
    return p / jnp.sum(p, axis=1, keepdims=True)


def mem_attn_fwd(P, MKV, B, S, M, tq=512):
    T = B * S
    tq = min(tq, S)
    nq = S // tq

    def body(q_ref, kv_ref, o_ref):
        for h in range(N_HEADS):
            sl = slice(h * 128, (h + 1) * 128)
            p = _mem_probs(q_ref[:, sl].astype(BF16), kv_ref[:, sl])
            o_ref[:, sl] = _dot(p.astype(BF16), kv_ref[:, 512 + h * 128:512 + (h + 1) * 128], NN).astype(BF16)

    return pl.pallas_call(
        body, name="mem_attn_fwd", grid=(B, nq),
        in_specs=[BS((tq, 512), lambda b, i: (b * nq + i, OFF_MEMQ // 512)), BS((M, 1024), lambda b, i: (b, 0))],
        out_specs=BS((tq, 512), lambda b, i: (b * nq + i, 0)),
        out_shape=jax.ShapeDtypeStruct((T, 512), BF16), compiler_params=_arb(2))(P, MKV)


def mem_attn_bwd(P, MKV, dO, B, S, M, tq=512):
    T = B * S
    tq = min(tq, S)
    nq = S // tq

    def body(q_ref, kv_ref, do_ref, dq_ref, dkv_ref):
        @pl.when(pl.program_id(1) == 0)
        def _():
            dkv_ref[...] = jnp.zeros_like(dkv_ref)

        for h in range(N_HEADS):
            sl = slice(h * 128, (h + 1) * 128)
            sv = slice(512 + h * 128, 512 + (h + 1) * 128)
            qh = q_ref[:, sl].astype(BF16)
            kh = kv_ref[:, sl]
            do = do_ref[:, sl].astype(BF16)
            p = _mem_probs(qh, kh)
            dkv_ref[:, sv] += _dot(p.astype(BF16), do, TN)
            dp = _dot(do, kv_ref[:, sv], NT)
            ds = (p * (dp - jnp.sum(dp * p, axis=1, keepdims=True)) * MEM_SCALE).astype(BF16)
            dq_ref[:, sl] = _dot(ds, kh, NN).astype(BF16)
            dkv_ref[:, sl] += _dot(ds, qh, TN)

    return pl.pallas_call(
        body, name="mem_attn_bwd", grid=(B, nq),
        in_specs=[BS((tq, 512), lambda b, i: (b * nq + i, OFF_MEMQ // 512)), BS((M, 1024), lambda b, i: (b, 0)),
                  BS((tq, 512), lambda b, i: (b * nq + i, 0))],
        out_specs=[BS((tq, 512), lambda b, i: (b * nq + i, 0)), BS((M, 1024), lambda b, i: (b, 0))],
        out_shape=[jax.ShapeDtypeStruct((T, 512), BF16), jax.ShapeDtypeStruct((B * M, 1024), F32)],
        compiler_params=_arb(2))(P, MKV, dO)


def gain_grad(x, dy, name, tm=256):
    T, n = x.shape
    tm = min(tm, T)

    def body(x_ref, dy_ref, o_ref):
        @pl.when(pl.program_id(0) == 0)
        def _():
            o_ref[...] = jnp.zeros_like(o_ref)

        xv = x_ref[...]
        xh = xv * lax.rsqrt(jnp.mean(xv * xv, axis=-1, keepdims=True) + EPS)
        o_ref[...] += jnp.sum(dy_ref[...] * xh, axis=0, keepdims=True)

    return pl.pallas_call(
        body, name=name, grid=(T // tm,),
        in_specs=[BS((tm, n), lambda i: (i, 0))] * 2, out_specs=BS((1, n), lambda i: (0, 0)),
        out_shape=jax.ShapeDtypeStruct((1, n), F32), compiler_params=_arb(1))(x, dy)


def _conv_silu(x, w, t):
    y = x * w[3:4, :]
    for s in range(1, GDN_CONV):
        y = y + jnp.where(t >= s, pltpu.roll(x, s, 0), 0.0) * w[3 - s:4 - s, :]
    return y, _sigmoid(y)


def gdn_prep_fwd(P, conv_w, B, S):
    T = B * S

    def body(x_ref, w_ref, o_ref):
        kind = pl.program_id(1)
        t = lax.broadcasted_iota(jnp.int32, (S, 1), 0)
        y, sg = _conv_silu(x_ref[...].astype(F32), w_ref[...], t)
        a = y * sg
        scale = jnp.where(kind == 0, GDN_SCALE, 1.0).astype(F32)
        for h in range(N_HEADS):
            sl = slice(h * 128, (h + 1) * 128)
            seg = a[:, sl]
            n = lax.rsqrt(jnp.sum(seg * seg, axis=-1, keepdims=True) + EPS)
            o_ref[:, sl] = jnp.where(kind < 2, seg * (n * scale), seg)

    return pl.pallas_call(
        body, name="gdn_prep_fwd", grid=(B, 3),
        in_specs=[BS((S, 512), lambda b, k: (b, OFF_GDN // 512 + k)), BS((GDN_CONV, 512), lambda b, k: (0, k))],
        out_specs=BS((S, 512), lambda b, k: (b, k)),
        out_shape=jax.ShapeDtypeStruct((T, GDN_QKV), F32), compiler_params=_arb(2))(P, conv_w)


def gdn_prep_bwd(P, dqkv, conv_w, B, S):
    T = B * S

    def body(x_ref, d_ref, w_ref, o_ref, gw_ref):
        kind = pl.program_id(0)

        @pl.when(pl.program_id(1) == 0)
        def _():
            gw_ref[...] = jnp.zeros_like(gw_ref)

        t = lax.broadcasted_iota(jnp.int32, (S, 1), 0)
        x = x_ref[...].astype(F32)
        w = w_ref[...]
        y, sg = _conv_silu(x, w, t)
        a = y * sg
        scale = jnp.where(kind == 0, GDN_SCALE, 1.0).astype(F32)
        das = []
        for h in range(N_HEADS):
            sl = slice(h * 128, (h + 1) * 128)
            seg, dseg = a[:, sl], d_ref[:, sl]
            n = lax.rsqrt(jnp.sum(seg * seg, axis=-1, keepdims=True) + EPS)
            dn = scale * (n * dseg - seg * (n * n * n) * jnp.sum(dseg * seg, axis=-1, keepdims=True))
            das.append(jnp.where(kind < 2, dn, dseg))
        dy = jnp.concatenate(das, axis=1) * (sg * (1.0 + y * (1.0 - sg)))
        dx = dy * w[3:4, :]
        gw_ref[3:4, :] += jnp.sum(dy * x, axis=0, keepdims=True)
        for s in range(1, GDN_CONV):
            dx = dx + jnp.where(t + s < S, pltpu.roll(dy, S - s, 0), 0.0) * w[3 - s:4 - s, :]
            gw_ref[3 - s:4 - s, :] += jnp.sum(dy * jnp.where(t >= s, pltpu.roll(x, s, 0), 0.0), axis=0, keepdims=True)
        o_ref[...] = dx.astype(BF16)

    return pl.pallas_call(
        body, name="gdn_prep_bwd", grid=(3, B),
        in_specs=[BS((S, 512), lambda k, b: (b, OFF_GDN // 512 + k)), BS((S, 512), lambda k, b: (b, k)),
                  BS((GDN_CONV, 512), lambda k, b: (0, k))],
        out_specs=[BS((S, 512), lambda k, b: (b, k)), BS((GDN_CONV, 512), lambda k, b: (0, k))],
        out_shape=[jax.ShapeDtypeStruct((T, GDN_QKV), BF16), jax.ShapeDtypeStruct((GDN_CONV, GDN_QKV), F32)],
        compiler_params=_arb(2))(P, dqkv, conv_w)


def _chunk_row(n_rows):
    return lax.broadcasted_iota(jnp.int32, (n_rows, 1), 0) % CHUNK


def gdn_gate_fwd(P, alog_row, dt_row, B, S):
    T = B * S

    def body(x_ref, al_ref, dt_ref, o_ref):
        x = x_ref[...].astype(F32)
        lane = lax.broadcasted_iota(jnp.int32, (1, 128), 1)
        g = jnp.where(lane < 4, -jnp.exp(al_ref[...]) * _softplus(x + dt_ref[...]), 0.0)
        t = _chunk_row(S)
        for s in (1, 2, 4, 8, 16, 32):
            g = g + jnp.where(t >= s, pltpu.roll(g, s, 0), 0.0)
        o_ref[...] = jnp.where(lane < 4, g, jnp.where(lane < 8, _sigmoid(x), 0.0))

    row = BS((1, 128), lambda b: (0, 0))
    return pl.pallas_call(
        body, name="gdn_gate_fwd", grid=(B,),
        in_specs=[BS((S, 128), lambda b: (b, 768 // 128)), row, row], out_specs=BS((S, 128), lambda b: (b, 0)),
        out_shape=jax.ShapeDtypeStruct((T, 128), F32), compiler_params=_arb(1))(P, alog_row, dt_row)


def gdn_gate_bwd(P, dGB, alog_row, dt_row, B, S):
    T = B * S

    def body(x_ref, d_ref, al_ref, dt_ref, o_ref, acc_ref):
        @pl.when(pl.program_id(0) == 0)
        def _():
            acc_ref[...] = jnp.zeros_like(acc_ref)

        x, d = x_ref[...].astype(F32), d_ref[...]
        lane = lax.broadcasted_iota(jnp.int32, (1, 128), 1)
        z = x + dt_ref[...]
        coef = -jnp.exp(al_ref[...])
        g = coef * _softplus(z)
        da = jnp.where(lane < 4, d * coef * _sigmoid(z), 0.0)
        beta = _sigmoid(x)
        o_ref[...] = jnp.where(lane < 4, da, jnp.where(lane < 8, d * beta * (1.0 - beta), 0.0)).astype(BF16)
        acc_ref[0:1, :] += jnp.sum(jnp.where(lane < 4, d * g, 0.0), axis=0, keepdims=True)
        acc_ref[1:2, :] += jnp.sum(da, axis=0, keepdims=True)

    row = BS((1, 128), lambda b: (0, 0))
    return pl.pallas_call(
        body, name="gdn_gate_bwd", grid=(B,),
        in_specs=[BS((S, 128), lambda b: (b, 768 // 128)), BS((S, 128), lambda b: (b, 0)), row, row],
        out_specs=[BS((S, 128), lambda b: (b, 0)), BS((8, 128), lambda b: (0, 0))],
        out_shape=[jax.ShapeDtypeStruct((T, 128), BF16), jax.ShapeDtypeStruct((8, 128), F32)],
        compiler_params=_arb(1))(P, dGB, alog_row, dt_row)


def _chunk_masks(nc):
    r = lax.broadcasted_iota(jnp.int32, (nc, CHUNK, CHUNK), 1)
    c = lax.broadcasted_iota(jnp.int32, (nc, CHUNK, CHUNK), 2)
    return r >= c, r > c


def _chunk_local(q, k, gc, gr, beta, incl, strict):
    decay = jnp.exp(jnp.where(incl, gc - gr, NEG))
    kb = k * beta
    kbf = k.astype(BF16)
    m_kk = _bdot("gcd,gjd->gcj", kb.astype(BF16), kbf)
    l_mat = jnp.where(strict, m_kk * decay, 0.0)
    a_mat = _bdot("gcd,gjd->gcj", q.astype(BF16), kbf) * decay
    return decay, kb, l_mat, a_mat


WY_SPLIT_LEVELS = 2


def _split_bf16(x):
    hi = x.astype(BF16)
    return hi, (x - hi.astype(F32)).astype(BF16)


def _mm_split(ah, al, bh, bl):
    spec = "gij,gjk->gik"
    return _bdot(spec, ah, bh) + (_bdot(spec, ah, bl) + _bdot(spec, al, bh))


def gdn_chunk_fwd(qkv, GB, Grow, B, S, nc=8):
    T = B * S
    N = S // CHUNK
    nc = min(nc, N)
    nb = N // nc
    R = nc * CHUNK

    def body(q_ref, k_ref, v_ref, gb_ref, gr_ref, u_ref, w_ref, t_ref, a_ref):
        incl, strict = _chunk_masks(nc)
        eye = (lax.broadcasted_iota(jnp.int32, (nc, CHUNK, CHUNK), 1)
               == lax.broadcasted_iota(jnp.int32, (nc, CHUNK, CHUNK), 2)).astype(F32)
        for h in range(N_HEADS):
            sl = slice(h * 128, (h + 1) * 128)
            q = q_ref[:, sl].reshape(nc, CHUNK, 128)
            k = k_ref[:, sl].reshape(nc, CHUNK, 128)
            v = v_ref[:, sl].reshape(nc, CHUNK, 128)
            gc = gb_ref[:, h:h + 1].reshape(nc, CHUNK, 1)
            beta = gb_ref[:, 4 + h:5 + h].reshape(nc, CHUNK, 1)
            gr = gr_ref[h][:, None, :]
            _, kb, l_mat, a_mat = _chunk_local(q, k, gc, gr, beta, incl, strict)
            pw = -l_mat
            tinv = eye + pw
            for level in range(5):
                if level < WY_SPLIT_LEVELS:
                    ph, pl_ = _split_bf16(pw)
                    pw = _mm_split(ph, pl_, ph, pl_)
                    ph, pl_ = _split_bf16(pw)
                    th, tl = _split_bf16(tinv)
                    tinv = tinv + _mm_split(th, tl, ph, pl_)
                else:
                    ph = pw.astype(BF16)
                    pw = _bdot("gij,gjk->gik", ph, ph)
                    tinv = tinv + _bdot("gij,gjk->gik", tinv.astype(BF16), pw.astype(BF16))
            tb = tinv.astype(BF16)
            u = _bdot("gcj,gjv->gcv", tb, (v * beta).astype(BF16))
            w = _bdot("gcj,gjk->gck", tb, (kb * jnp.exp(gc)).astype(BF16))
            u_ref[:, sl] = u.reshape(R, 128)
            w_ref[:, sl] = w.reshape(R, 128).astype(BF16)
            t_ref[h] = tb
            a_ref[h] = a_mat.astype(BF16)

    rowb = lambda c, j: BS((R, c), lambda b, n: (b * nb + n, j))
    mat = BS((None, N_HEADS, nc, CHUNK, CHUNK), lambda b, n: (b, 0, n, 0, 0))
    return pl.pallas_call(
        body, name="gdn_chunk_fwd", grid=(B, nb),
        in_specs=[rowb(512, 0), rowb(512, 1), rowb(512, 2), rowb(128, 0),
                  BS((None, N_HEADS, nc, CHUNK), lambda b, n: (b, 0, n, 0))],
        out_specs=[rowb(512, 0), rowb(512, 0), mat, mat],
        out_shape=[jax.ShapeDtypeStruct((T, 512), F32), jax.ShapeDtypeStruct((T, 512), BF16),
                   jax.ShapeDtypeStruct((B, N_HEADS, N, CHUNK, CHUNK), BF16),
                   jax.ShapeDtypeStruct((B, N_HEADS, N, CHUNK, CHUNK), BF16)],
        compiler_params=_arb(2))(qkv, qkv, qkv, GB, Grow)


def gdn_scan_fwd(qkv3, U3, W3, GB3, A, B, S):
    N = S // CHUNK

    def body(q_ref, k_ref, u_ref, w_ref, gb_ref, a_ref, o_ref, vn_ref, st_ref, s_s):
        @pl.when(pl.program_id(0) == 0)
        def _():
            s_s[...] = jnp.zeros_like(s_s)

        for b in range(B):
            for h in range(N_HEADS):
                sl = slice(h * 128, (h + 1) * 128)
                st = s_s[b, h]
                st_ref[b, h] = st
                stb = st.astype(BF16)
                g = gb_ref[b, :, h:h + 1]
                gl = g[CHUNK - 1:CHUNK, :]
                vn = u_ref[b, :, sl] - _dot(w_ref[b, :, sl].astype(BF16), stb, NN)
                vnb = vn.astype(BF16)
                o = _dot((q_ref[b, :, sl] * jnp.exp(g)).astype(BF16), stb, NN) + _dot(a_ref[b, h].astype(BF16), vnb, NN)
                vn_ref[b, :, sl] = vnb
                o_ref[b, :, sl] = o.astype(BF16)
                s_s[b, h] = st * jnp.exp(gl) + _dot((k_ref[b, :, sl] * jnp.exp(gl - g)).astype(BF16), vnb, TN)

    tok = lambda c, j: BS((B, CHUNK, c), lambda n: (0, n, j))
    return pl.pallas_call(
        body, name="gdn_scan_fwd", grid=(N,),
        in_specs=[tok(512, 0), tok(512, 1), tok(512, 0), tok(512, 0), tok(128, 0),
                  BS((B, N_HEADS, None, CHUNK, CHUNK), lambda n: (0, 0, n, 0, 0))],
        out_specs=[tok(512, 0), tok(512, 0), BS((B, N_HEADS, None, 128, 128), lambda n: (0, 0, n, 0, 0))],
        out_shape=[jax.ShapeDtypeStruct((B, S, 512), BF16), jax.ShapeDtypeStruct((B, S, 512), BF16),
                   jax.ShapeDtypeStruct((B, N_HEADS, N, 128, 128), F32)],
        scratch_shapes=[pltpu.VMEM((B, N_HEADS, 128, 128), F32)],
        compiler_params=_arb(1))(qkv3, qkv3, U3, W3, GB3, A)


def gdn_scan_bwd(dO3, qkv3, W3, Vn3, GB3, A, St, B, S):
    N = S // CHUNK

    def body(do_ref, q_ref, k_ref, w_ref, vn_ref, gb_ref, a_ref, st_ref,
             du_ref, dw_ref, dq_ref, dk_ref, da_ref, dg_ref, ds_s):
        @pl.when(pl.program_id(0) == 0)
        def _():
            ds_s[...] = jnp.zeros_like(ds_s)

        lane = lax.broadcasted_iota(jnp.int32, (1, 128), 1)
        last = lax.broadcasted_iota(jnp.int32, (CHUNK, 1), 0) == CHUNK - 1
        for b in range(B):
            dg_all = jnp.zeros((CHUNK, 128), F32)
            for h in range(N_HEADS):
                sl = slice(h * 128, (h + 1) * 128)
                st = st_ref[b, h]
                stb = st.astype(BF16)
                dsn = ds_s[b, h]
                dsnb = dsn.astype(BF16)
                g = gb_ref[b, :, h:h + 1]
                gl = g[CHUNK - 1:CHUNK, :]
                egl = jnp.exp(gl)
                ekd = jnp.exp(gl - g)
                eg = jnp.exp(g)
                q, k = q_ref[b, :, sl], k_ref[b, :, sl]
                kd = k * ekd
                qg = q * eg
                do = do_ref[b, :, sl].astype(BF16)
                vnb = vn_ref[b, :, sl].astype(BF16)
                dvn = _dot(a_ref[b, h].astype(BF16), do, TN) + _dot(kd.astype(BF16), dsnb, NN)
                dvnb = dvn.astype(BF16)
                da_ref[b, h] = _dot(do, vnb, NT)
                dqg = _dot(do, stb, NT)
                dkd = _dot(vnb, dsnb, NT)
                ds_s[b, h] = (_dot(qg.astype(BF16), do, TN) + egl * dsn - _dot(w_ref[b, :, sl].astype(BF16), dvnb, TN))
                du_ref[b, :, sl] = dvnb
                dw_ref[b, :, sl] = (-_dot(dvnb, stb, NT)).astype(BF16)
                dq_ref[b, :, sl] = dqg * eg
                dk_ref[b, :, sl] = dkd * ekd
                ddel = jnp.sum(dkd * kd, axis=1, keepdims=True)
                dgl = jnp.sum(ddel, axis=0, keepdims=True) + jnp.sum(jnp.sum(st * dsn, axis=1, keepdims=True), axis=0, keepdims=True) * egl
                col = jnp.sum(dqg * qg, axis=1, keepdims=True) - ddel + jnp.where(last, dgl, 0.0)
                dg_all = jnp.where(lane == h, col, dg_all)
            dg_ref[b] = dg_all

    tok = lambda c, j: BS((B, CHUNK, c), lambda n: (0, N - 1 - n, j))
    mat = lambda d: BS((B, N_HEADS, None, d, d), lambda n: (0, 0, N - 1 - n, 0, 0))
    return pl.pallas_call(
        body, name="gdn_scan_bwd", grid=(N,),
        in_specs=[tok(512, 0), tok(512, 0), tok(512, 1), tok(512, 0), tok(512, 0), tok(128, 0), mat(CHUNK), mat(128)],
        out_specs=[tok(512, 0), tok(512, 0), tok(512, 0), tok(512, 0), mat(CHUNK), tok(128, 0)],
        out_shape=[jax.ShapeDtypeStruct((B, S, 512), BF16)] * 2 + [jax.ShapeDtypeStruct((B, S, 512), F32)] * 2
        + [jax.ShapeDtypeStruct((B, N_HEADS, N, CHUNK, CHUNK), F32), jax.ShapeDtypeStruct((B, S, 128), F32)],
        scratch_shapes=[pltpu.VMEM((B, N_HEADS, 128, 128), F32)],
        compiler_params=_arb(1))(dO3, qkv3, qkv3, W3, Vn3, GB3, A, St)


def gdn_chunk_bwd(qkv, GB, Grow, Tinv, dA, dU, dW, dQ1, dK1, dG1, B, S, nc=8):
    T = B * S
    N = S // CHUNK
    nc = min(nc, N)
    nb = N // nc
    R = nc * CHUNK

    def body(q_ref, k_ref, v_ref, gb_ref, gr_ref, t_ref, da_ref, du_ref, dw_ref, dq1_ref, dk1_ref, dg1_ref, o_ref, dgb_ref):
        incl, strict = _chunk_masks(nc)
        lane = lax.broadcasted_iota(jnp.int32, (1, 128), 1)
        dg_all = dg1_ref[...]
        db_all = jnp.zeros((R, 128), F32)
        for h in range(N_HEADS):
            sl = slice(h * 128, (h + 1) * 128)
            q = q_ref[:, sl].reshape(nc, CHUNK, 128)
            k = k_ref[:, sl].reshape(nc, CHUNK, 128)
            v = v_ref[:, sl].reshape(nc, CHUNK, 128)
            gc = gb_ref[:, h:h + 1].reshape(nc, CHUNK, 1)
            beta = gb_ref[:, 4 + h:5 + h].reshape(nc, CHUNK, 1)
            gr = gr_ref[h][:, None, :]
            decay, kb, l_mat, a_mat = _chunk_local(q, k, gc, gr, beta, incl, strict)
            eg = jnp.exp(gc)
            kbg = kb * eg
            vb = v * beta
            tb = t_ref[h].astype(BF16)
            du = du_ref[:, sl].reshape(nc, CHUNK, 128).astype(BF16)
            dw = dw_ref[:, sl].reshape(nc, CHUNK, 128).astype(BF16)
            dvb = _bdot("gcj,gcv->gjv", tb, du)
            dkbg = _bdot("gcj,gck->gjk", tb, dw)
            dt = _bdot("gcv,gjv->gcj", du, vb.astype(BF16)) + _bdot("gck,gjk->gcj", dw, kbg.astype(BF16))
            tmp = _bdot("gac,gab->gcb", tb, dt.astype(BF16))
            dl = jnp.where(strict, -_bdot("gcb,gdb->gcd", tmp.astype(BF16), tb), 0.0)
            da = da_ref[h]
            dm = (dl * decay).astype(BF16)
            dqk = (da * decay).astype(BF16)
            kbf = k.astype(BF16)
            dkb = _bdot("gcj,gjd->gcd", dm, kbf) + dkbg * eg
            dk = (_bdot("gcj,gcd->gjd", dm, kb.astype(BF16)) + _bdot("gcj,gcd->gjd", dqk, q.astype(BF16))
                  + dk1_ref[:, sl].reshape(nc, CHUNK, 128) + dkb * beta)
            dq = _bdot("gcj,gjd->gcd", dqk, kbf) + dq1_ref[:, sl].reshape(nc, CHUNK, 128)
            e = dl * l_mat + da * a_mat
            dgc = (jnp.sum(e, axis=2, keepdims=True) - jnp.sum(jnp.swapaxes(e, 1, 2), axis=2, keepdims=True)
                   + jnp.sum(dkbg * kbg, axis=2, keepdims=True))
            dbeta = jnp.sum(dkb * k, axis=2, keepdims=True) + jnp.sum(dvb * v, axis=2, keepdims=True)
            o_ref[:, sl] = dq.reshape(R, 128)
            o_ref[:, 512 + h * 128:512 + (h + 1) * 128] = dk.reshape(R, 128)
            o_ref[:, 1024 + h * 128:1024 + (h + 1) * 128] = (dvb * beta).reshape(R, 128)
            dg_all = dg_all + jnp.where(lane == h, dgc.reshape(R, 1), 0.0)
            db_all = jnp.where(lane == 4 + h, dbeta.reshape(R, 1), db_all)
        t = _chunk_row(R)
        for s in (1, 2, 4, 8, 16, 32):
            dg_all = dg_all + jnp.where(t + s < CHUNK, pltpu.roll(dg_all, R - s, 0), 0.0)
        dgb_ref[...] = jnp.where(lane < 4, dg_all, db_all)

    rowb = lambda c, j: BS((R, c), lambda b, n: (b * nb + n, j))
    mat = BS((None, N_HEADS, nc, CHUNK, CHUNK), lambda b, n: (b, 0, n, 0, 0))
    return pl.pallas_call(
        body, name="gdn_chunk_bwd", grid=(B, nb),
        in_specs=[rowb(512, 0), rowb(512, 1), rowb(512, 2), rowb(128, 0),
                  BS((None, N_HEADS, nc, CHUNK), lambda b, n: (b, 0, n, 0)), mat, mat,
                  rowb(512, 0), rowb(512, 0), rowb(512, 0), rowb(512, 0), rowb(128, 0)],
        out_specs=[rowb(GDN_QKV, 0), rowb(128, 0)],
        out_shape=[jax.ShapeDtypeStruct((T, GDN_QKV), F32), jax.ShapeDtypeStruct((T, 128), F32)],
        compiler_params=_arb(2))(qkv, qkv, qkv, GB, Grow, Tinv, dA, dU, dW, dQ1, dK1, dG1)


def _gdn_out_norm(og, gg):
    outs, xhs, rs = [], [], []
    for h in range(N_HEADS):
        seg = og[:, h * 128:(h + 1) * 128]
        r = lax.rsqrt(jnp.mean(seg * seg, axis=-1, keepdims=True) + EPS)
        xh = seg * r
        outs.append(xh * gg)
        xhs.append(xh)
        rs.append(r)
    return outs, xhs, rs


def merge_fwd(o_mla, o_gdn, o_mem, P, x, tgt, w_out, g_gdn, g_fin, tm=256):
    T = x.shape[0]
    tm = min(tm, T)

    def body(om_ref, og_ref, oc_ref, gate_ref, x_ref, t_ref, w_ref, gg_ref, gf_ref, mix_ref, dx_ref, dxb_ref, sq_ref, gnf_ref):
        @pl.when(pl.program_id(0) == 0)
        def _():
            sq_ref[...] = jnp.zeros_like(sq_ref)
            gnf_ref[...] = jnp.zeros_like(gnf_ref)

        ogn, _, _ = _gdn_out_norm(og_ref[...].astype(F32), gg_ref[...])
        cat = jnp.concatenate([om_ref[...].astype(F32)] + ogn + [oc_ref[...].astype(F32)], axis=1)
        gt = gate_ref[...].astype(F32)
        mixed = (cat * (gt * _sigmoid(gt))).astype(BF16)
        mix_ref[...] = mixed
        x2 = x_ref[...] + _dot(mixed, w_ref[...], NN)
        r2 = lax.rsqrt(jnp.mean(x2 * x2, axis=-1, keepdims=True) + EPS)
        xh = x2 * r2
        gf = gf_ref[...]
        diff = xh * gf - t_ref[...]
        sq_ref[...] += jnp.sum(diff * diff, axis=0, keepdims=True)
        dy = diff * (1.0 / D_MODEL)
        gnf_ref[...] += jnp.sum(dy * xh, axis=0, keepdims=True)
        dxh = dy * gf
        dx = r2 * (dxh - xh * jnp.mean(dxh * xh, axis=-1, keepdims=True))
        dx_ref[...] = dx
        dxb_ref[...] = dx.astype(BF16)

    rowb = lambda c, j=0: BS((tm, c), lambda i: (i, j))
    full = lambda r, c: BS((r, c), lambda i: (0, 0))
    return pl.pallas_call(
        body, name="merge_fwd", grid=(T // tm,),
        in_specs=[rowb(512), rowb(512), rowb(512), rowb(D_MIX, OFF_GATE // D_MIX), rowb(D_MODEL), rowb(D_MODEL),
                  full(D_MIX, D_MODEL), full(1, 128), full(1, D_MODEL)],
        out_specs=[rowb(D_MIX), rowb(D_MODEL), rowb(D_MODEL), full(1, D_MODEL), full(1, D_MODEL)],
        out_shape=[jax.ShapeDtypeStruct((T, D_MIX), BF16), jax.ShapeDtypeStruct((T, D_MODEL), F32),
                   jax.ShapeDtypeStruct((T, D_MODEL), BF16),
                   jax.ShapeDtypeStruct((1, D_MODEL), F32), jax.ShapeDtypeStruct((1, D_MODEL), F32)],
        compiler_params=_arb(1))(o_mla, o_gdn, o_mem, P, x, tgt, w_out, g_gdn, g_fin)


def merge_bwd(dx2, o_mla, o_gdn, o_mem, P, w_out, g_gdn, tm=256):
    T = dx2.shape[0]
    tm = min(tm, T)

    def body(dx_ref, om_ref, og_ref, oc_ref, gate_ref, w_ref, gg_ref, dgate_ref, dom_ref, dog_ref, doc_ref, ggn_ref):
        @pl.when(pl.program_id(0) == 0)
        def _():
            ggn_ref[...] = jnp.zeros_like(ggn_ref)

        gg = gg_ref[...]
        dmix = _dot(dx_ref[...].astype(BF16), w_ref[...], NT)
        ogn, xhs, rs = _gdn_out_norm(og_ref[...].astype(F32), gg)
        cat = jnp.concatenate([om_ref[...].astype(F32)] + ogn + [oc_ref[...].astype(F32)], axis=1)
        gt = gate_ref[...].astype(F32)
        sg = _sigmoid(gt)
        dgate_ref[...] = (dmix * cat * (sg * (1.0 + gt * (1.0 - sg)))).astype(BF16)
        dcat = dmix * (gt * sg)
        dom_ref[...] = dcat[:, :512].astype(BF16)
        doc_ref[...] = dcat[:, 1024:].astype(BF16)
        acc = jnp.zeros((1, 128), F32)
        for h in range(N_HEADS):
            dseg = dcat[:, 512 + h * 128:512 + (h + 1) * 128]
            acc = acc + jnp.sum(dseg * xhs[h], axis=0, keepdims=True)
            dxh = dseg * gg
            dog_ref[:, h * 128:(h + 1) * 128] = (rs[h] * (dxh - xhs[h] * jnp.mean(dxh * xhs[h], axis=-1, keepdims=True))).astype(BF16)
        ggn_ref[...] += acc

    rowb = lambda c, j=0: BS((tm, c), lambda i: (i, j))
    full = lambda r, c: BS((r, c), lambda i: (0, 0))
    return pl.pallas_call(
        body, name="merge_bwd", grid=(T // tm,),
        in_specs=[rowb(D_MODEL), rowb(512), rowb(512), rowb(512), rowb(D_MIX, OFF_GATE // D_MIX),
                  full(D_MIX, D_MODEL), full(1, 128)],
        out_specs=[rowb(D_MIX), rowb(512), rowb(512), rowb(512), full(1, 128)],
        out_shape=[jax.ShapeDtypeStruct((T, D_MIX), BF16)] + [jax.ShapeDtypeStruct((T, 512), BF16)] * 3
        + [jax.ShapeDtypeStruct((1, 128), F32)],
        compiler_params=_arb(1))(dx2, o_mla, o_gdn, o_mem, P, w_out, g_gdn)


def in_proj_bwd(dP, wp, x, dx2, gain, after, tm=512):
    T, n = x.shape
    tm = min(tm, T)
    k = len(dP)
    widths = [p.shape[1] for p in dP]
    offs = [sum(widths[:i]) for i in range(k)]

    def body(*refs):
        w_ref, x_ref, dx2_ref, g_ref = refs[k:k + 4]
        o_ref, acc_ref = refs[-2:]

        @pl.when(pl.program_id(0) == 0)
        def _():
            acc_ref[...] = jnp.zeros_like(acc_ref)

        dy = None
        for a_ref, off, w in zip(refs[:k], offs, widths):
            d = _dot(a_ref[...], w_ref[off:off + w, :], NN)
            dy = d if dy is None else dy + d
        xv = x_ref[...]
        r = lax.rsqrt(jnp.mean(xv * xv, axis=-1, keepdims=True) + EPS)
        xh = xv * r
        acc_ref[...] += jnp.sum(dy * xh, axis=0, keepdims=True)
        dxh = dy * g_ref[...]
        o_ref[...] = dx2_ref[...] + r * (dxh - xh * jnp.mean(dxh * xh, axis=-1, keepdims=True))

    rowb = BS((tm, n), lambda i: (i, 0))
    full = BS((1, n), lambda i: (0, 0))
    return pl.pallas_call(
        body, name="in_proj_bwd", grid=(T // tm,),
        in_specs=[BS((tm, w), lambda i: (i, 0)) for w in widths]
        + [BS(wp.shape, lambda i: (0, 0), pipeline_mode=pl.Buffered(1)), rowb, rowb, full, BS(memory_space=pl.ANY)],
        out_specs=[rowb, full], out_shape=[jax.ShapeDtypeStruct((T, n), F32), jax.ShapeDtypeStruct((1, n), F32)],
        compiler_params=_arb(1))(*dP, wp, x, dx2, gain, after)


W_IN_SHARD = D_IN // 4
_GDN0 = Q_LORA + KV_LORA + MLA_ROPE
_AB0 = _GDN0 + GDN_QKV
_MEMQ0 = _AB0 + 2 * N_HEADS
_GATE0 = _MEMQ0 + N_HEADS * MEM_DH


def _w_in_row_map():
    a, m, gt = _AB0 - 2 * W_IN_SHARD, _MEMQ0 - 2 * W_IN_SHARD, _GATE0 - 2 * W_IN_SHARD
    e0 = OFF_GDN + W_IN_SHARD - _GDN0
    e1 = e0 + W_IN_SHARD
    e2 = OFF_GATE + W_IN_SHARD - gt
    return [(0, 0, 0, 672), (0, 672, 704, 32), (2, a, 768, m - a), (2, m, OFF_MEMQ, gt - m), (0, _GDN0, OFF_GDN, W_IN_SHARD - _GDN0),
            (1, 0, e0, W_IN_SHARD), (2, 0, e1, a), (2, gt, OFF_GATE, W_IN_SHARD - gt), (3, 0, e2, W_IN_SHARD)]


_W_IN_ZERO_ROWS = [(672, 32), (736, 32), (776, 248)]
W_IN_LANES = 256


def pad_w_in_t(shards):
    per_half = shards.shape[3] // W_IN_LANES

    def body(s_ref, o_ref):
        for r0, n in _W_IN_ZERO_ROWS:
            o_ref[r0:r0 + n, :] = jnp.zeros((n, W_IN_LANES), o_ref.dtype)
        for q, src, dst, n in _w_in_row_map():
            o_ref[dst:dst + n, :] = s_ref[q, src:src + n, :]

    return pl.pallas_call(
        body, name="pad_w_in_t", grid=(D_MODEL // W_IN_LANES,),
        in_specs=[BS((N_CHIPS, None, W_IN_SHARD, W_IN_LANES), lambda j: (0, j // per_half, 0, j % per_half))],
        out_specs=BS((N_PAD, W_IN_LANES), lambda j: (0, j)),
        out_shape=jax.ShapeDtypeStruct((N_PAD, D_MODEL), shards.dtype), compiler_params=_arb(1))(shards)


def unpad_w_in_t(g):
    def body(g_ref, o_ref):
        for q, src, dst, n in _w_in_row_map():
            o_ref[q, src:src + n, :] = g_ref[dst:dst + n, :]

    return pl.pallas_call(
        body, name="unpad_w_in_t", grid=(D_MODEL // W_IN_LANES,),
        in_specs=[BS((N_PAD, W_IN_LANES), lambda j: (0, j))], out_specs=BS((N_CHIPS, W_IN_SHARD, W_IN_LANES), lambda j: (0, 0, j)),
        out_shape=jax.ShapeDtypeStruct((N_CHIPS, W_IN_SHARD, D_MODEL), g.dtype), compiler_params=_arb(1))(g)


def _perm_w_kv_b(s):
    return jnp.concatenate([s[h, :, :128] for h in range(N_HEADS)] + [s[h, :, 128:] for h in range(N_HEADS)], axis=1)


def _unperm_w_kv_b(g):
    return jnp.stack([jnp.concatenate([g[:, h * 128:(h + 1) * 128], g[:, 512 + h * 128:512 + (h + 1) * 128]], axis=1)
                      for h in range(N_HEADS)])


def _lane_row(v4):
    return jnp.pad(v4.reshape(1, -1).astype(F32), ((0, 0), (0, 128 - v4.size)))


N_CHIPS = 4
MESH = pl.DeviceIdType.MESH
ANY = BS(memory_space=pl.ANY)


def _place():
    return lax.axis_index("x"), lax.axis_index("y"), lax.axis_index("c")


def _other_chips(x, y):
    return [(1 - x, y), (x, 1 - y), (1 - x, 1 - y)]


def _half(split, which):
    axis, size = split
    ds = pl.ds(pl.multiple_of(which * size, 16 if axis == 0 else 128), size)
    return (ds, slice(None)) if axis == 0 else (slice(None), ds)


SEM = BS(memory_space=pltpu.SEMAPHORE)
HBM = BS(memory_space=pltpu.HBM)
_IN_HBM = lambda a: pltpu.with_memory_space_constraint(a, pltpu.HBM)
_SIDE_EFFECT = pltpu.SideEffectType.DATAFLOW_SIDE_EFFECTING


def _late_gather_copies(s_refs, l_refs, send_sems, recv_sems, local_sems, with_arrivals):
    x, y, c = _place()
    sends, recvs, locals_ = [], [], []
    for i, (s_ref, l_ref) in enumerate(zip(s_refs, l_refs)):
        locals_.append(pltpu.make_async_copy(s_ref, l_ref.at[2 * x + y], local_sems.at[i]))
        for j, (px, py) in enumerate(_other_chips(x, y)):
            k = 3 * i + j
            sends.append(pltpu.make_async_remote_copy(src_ref=s_ref, dst_ref=l_ref.at[2 * x + y], send_sem=send_sems.at[k],
                                                      recv_sem=recv_sems.at[k], device_id=(px, py, c), device_id_type=MESH))
            if with_arrivals:
                recvs.append(pltpu.make_async_remote_copy(src_ref=s_ref, dst_ref=l_ref.at[2 * px + py], send_sem=send_sems.at[k],
                                                          recv_sem=recv_sems.at[k], device_id=(px, py, c), device_id_type=MESH))
    return sends, recvs, locals_


def late_gather_start(shards, after, name):
    n = len(shards)

    def body(*refs):
        s_refs, l_refs = refs[:n], refs[n:2 * n]
        send_sems, recv_sems, local_sems = refs[2 * n + 1:2 * n + 4]
        token = refs[-1]
        sends, _, locals_ = _late_gather_copies(s_refs, l_refs, send_sems, recv_sems, local_sems, False)
        for cp in locals_ + sends:
            cp.start()
        token[...] = jnp.zeros_like(token)

    lands = [lax.empty((N_CHIPS,) + s.shape, s.dtype) for s in shards]
    hbm_like = lambda a: pltpu.HBM(a.shape, a.dtype)
    out = pl.pallas_call(
        body, name=name,
        out_shape=[pltpu.SemaphoreType.DMA((3 * n,)), pltpu.SemaphoreType.DMA((3 * n,)), pltpu.SemaphoreType.DMA((n,))]
        + [hbm_like(s) for s in shards] + [hbm_like(l) for l in lands] + [jax.ShapeDtypeStruct((8, 128), F32)],
        in_specs=[HBM] * (2 * n) + [BS(memory_space=pl.ANY)], out_specs=[SEM] * 3 + [HBM] * (2 * n) + [BS(memory_space=pltpu.VMEM)],
        input_output_aliases={i: 3 + i for i in range(2 * n)},
        compiler_params=pltpu.CompilerParams(has_side_effects=_SIDE_EFFECT))(
            *[_IN_HBM(s) for s in shards], *[_IN_HBM(l) for l in lands], after)
    return out[:3], out[3:3 + n], out[3 + n:3 + 2 * n], out[-1]


def late_gather_wait(sems, shards, lands, after, name):
    n = len(shards)

    def body(*refs):
        s_refs, l_refs = refs[:n], refs[n:2 * n]
        send_sems, recv_sems, local_sems = refs[2 * n:2 * n + 3]
        sends, recvs, locals_ = _late_gather_copies(s_refs, l_refs, send_sems, recv_sems, local_sems, True)
        for cp in locals_:
            cp.wait()
        for cp in sends:
            cp.wait_send()
        for cp in recvs:
            cp.wait_recv()

    hbm_like = lambda a: pltpu.HBM(a.shape, a.dtype)
    out = pl.pallas_call(
        body, name=name, out_shape=[hbm_like(s) for s in shards] + [hbm_like(l) for l in lands],
        in_specs=[HBM] * (2 * n) + [SEM] * 3 + [BS(memory_space=pl.ANY)], out_specs=[HBM] * (2 * n),
        input_output_aliases={i: i for i in range(2 * n)},
        compiler_params=pltpu.CompilerParams(has_side_effects=_SIDE_EFFECT))(*shards, *lands, *sems, after)
    return out[n:]


def _half_gather_copies(s_ref, l_ref, send_sems, recv_sems, with_arrivals):
    x, y, c = _place()
    sends, recvs = [], []
    for j, (px, py) in enumerate(_other_chips(x, y)):
        sends.append(pltpu.make_async_remote_copy(src_ref=s_ref.at[c], dst_ref=l_ref.at[2 * x + y, c], send_sem=send_sems.at[j],
                                                  recv_sem=recv_sems.at[j], device_id=(px, py, c), device_id_type=MESH))
        if with_arrivals:
            recvs.append(pltpu.make_async_remote_copy(src_ref=s_ref.at[c], dst_ref=l_ref.at[2 * px + py, c], send_sem=send_sems.at[j],
                                                      recv_sem=recv_sems.at[j], device_id=(px, py, c), device_id_type=MESH))
    return sends, recvs


def half_gather_start(shard, name):
    def body(s_ref, l_ref, send_sems, recv_sems, local_sem, s_thru, l_thru, token):
        x, y, _ = _place()
        pltpu.make_async_copy(s_ref, l_ref.at[2 * x + y], local_sem.at[0]).start()
        for cp in _half_gather_copies(s_ref, l_ref, send_sems, recv_sems, False)[0]:
            cp.start()
        token[...] = jnp.zeros_like(token)

    land = lax.empty((N_CHIPS,) + shard.shape, shard.dtype)
    out = pl.pallas_call(
        body, name=name,
        out_shape=[pltpu.SemaphoreType.DMA((3,)), pltpu.SemaphoreType.DMA((3,)), pltpu.SemaphoreType.DMA((1,)),
                   pltpu.HBM(shard.shape, shard.dtype), pltpu.HBM(land.shape, land.dtype), jax.ShapeDtypeStruct((8, 128), F32)],
        in_specs=[HBM, HBM], out_specs=[SEM] * 3 + [HBM, HBM, BS(memory_space=pltpu.VMEM)],
        input_output_aliases={0: 3, 1: 4},
        compiler_params=pltpu.CompilerParams(has_side_effects=_SIDE_EFFECT))(_IN_HBM(shard), _IN_HBM(land))
    return out[:3], out[3], out[4], out[5]


def half_gather_wait(sems, shard, land, after, name):
    def body(s_ref, l_ref, send_sems, recv_sems, local_sem, *rest):
        x, y, _ = _place()
        pltpu.make_async_copy(s_ref, l_ref.at[2 * x + y], local_sem.at[0]).wait()
        sends, recvs = _half_gather_copies(s_ref, l_ref, send_sems, recv_sems, True)
        for cp in sends:
            cp.wait_send()
        for cp in recvs:
            cp.wait_recv()

    out = pl.pallas_call(
        body, name=name, out_shape=[pltpu.HBM(shard.shape, shard.dtype), pltpu.HBM(land.shape, land.dtype)],
        in_specs=[HBM, HBM] + [SEM] * 3 + [BS(memory_space=pl.ANY)] * len(after), out_specs=[HBM, HBM],
        input_output_aliases={0: 0, 1: 1},
        compiler_params=pltpu.CompilerParams(has_side_effects=_SIDE_EFFECT))(shard, land, *sems, *after)
    return out[1]


def pass_halves_to_sibling(land, name):
    def body(l_in, l_ref, send_sems, recv_sems):
        x, y, c = _place()
        copies = []
        for j, (px, py) in enumerate(_other_chips(x, y)):
            q = 2 * px + py
            give = pltpu.make_async_remote_copy(src_ref=l_ref.at[q, c], dst_ref=l_ref.at[q, c], send_sem=send_sems.at[j],
                                                recv_sem=recv_sems.at[j], device_id=(x, y, 1 - c), device_id_type=MESH)
            take = pltpu.make_async_remote_copy(src_ref=l_ref.at[q, c], dst_ref=l_ref.at[q, 1 - c], send_sem=send_sems.at[j],
                                                recv_sem=recv_sems.at[j], device_id=(x, y, 1 - c), device_id_type=MESH)
            give.start()
            copies.append((give, take))
        for give, take in copies:
            take.wait_recv()
            give.wait_send()

    return pl.pallas_call(
        body, name=name, in_specs=[ANY], out_specs=ANY, out_shape=jax.ShapeDtypeStruct(land.shape, land.dtype),
        input_output_aliases={0: 0},
        scratch_shapes=[pltpu.SemaphoreType.DMA((3,)), pltpu.SemaphoreType.DMA((3,))])(land)


def allgather_devices(block, name):
    R, C = block.shape

    def body(b_ref, o_ref, send_sems, recv_sems, local_sem):
        x, y, c = _place()
        me = 4 * x + 2 * y + c
        own = pltpu.make_async_copy(b_ref, o_ref.at[me], local_sem)
        own.start()
        copies = []
        for r in range(1, 8):
            px = 1 - x if r & 4 else x
            py = 1 - y if r & 2 else y
            pc = 1 - c if r & 1 else c
            send = pltpu.make_async_remote_copy(src_ref=b_ref, dst_ref=o_ref.at[me], send_sem=send_sems.at[r - 1],
                                                recv_sem=recv_sems.at[r - 1], device_id=(px, py, pc), device_id_type=MESH)
            recv = pltpu.make_async_remote_copy(src_ref=b_ref, dst_ref=o_ref.at[4 * px + 2 * py + pc], send_sem=send_sems.at[r - 1],
                                                recv_sem=recv_sems.at[r - 1], device_id=(px, py, pc), device_id_type=MESH)
            send.start()
            copies.append((send, recv))
        for send, recv in copies:
            recv.wait_recv()
            send.wait_send()
        own.wait()

    return pl.pallas_call(
        body, name=name, in_specs=[ANY], out_specs=ANY, out_shape=jax.ShapeDtypeStruct((8, R, C), block.dtype),
        scratch_shapes=[pltpu.SemaphoreType.DMA((7,)), pltpu.SemaphoreType.DMA((7,)), pltpu.SemaphoreType.DMA(())])(block)


def swap_sibling(arrs, name, splits=None):
    n = len(arrs)

    def sent(a_ref, i, c):
        return a_ref if splits is None else a_ref.at[(slice(None),) + _half(splits[i], 1 - c)]

    def out_shape(a, i):
        if splits is None:
            return a.shape
        axis, size = splits[i]
        return (a.shape[0], size, a.shape[2]) if axis == 0 else (a.shape[0], a.shape[1], size)

    def body(*refs):
        a_refs, o_refs = refs[:n], refs[n:2 * n]
        send_sems, recv_sems = refs[2 * n:]
        x, y, c = _place()
        copies = [pltpu.make_async_remote_copy(src_ref=sent(a_ref, i, c), dst_ref=o_ref, send_sem=send_sems.at[i],
                                               recv_sem=recv_sems.at[i], device_id=(x, y, 1 - c), device_id_type=MESH)
                  for i, (a_ref, o_ref) in enumerate(zip(a_refs, o_refs))]
        for cp in copies:
            cp.start()
        for cp in copies:
            cp.wait()

    return pl.pallas_call(
        body, name=name, in_specs=[ANY] * n, out_specs=[ANY] * n,
        out_shape=[jax.ShapeDtypeStruct(out_shape(a, i), a.dtype) for i, a in enumerate(arrs)],
        scratch_shapes=[pltpu.SemaphoreType.DMA((n,)), pltpu.SemaphoreType.DMA((n,))])(*arrs)


def _exchange_copies(p_refs, l_refs, send_sems, recv_sems):
    x, y, c = _place()
    return [pltpu.make_async_remote_copy(src_ref=p_ref.at[2 * px + py], dst_ref=l_ref.at[j], send_sem=send_sems.at[3 * i + j],
                                         recv_sem=recv_sems.at[3 * i + j], device_id=(px, py, c), device_id_type=MESH)
            for i, (p_ref, l_ref) in enumerate(zip(p_refs, l_refs)) for j, (px, py) in enumerate(_other_chips(x, y))]


def exchange_chips_start(parts, name):
    n = len(parts)

    def body(*refs):
        send_sems, recv_sems = refs[2 * n:2 * n + 2]
        for cp in _exchange_copies(refs[:n], refs[n:2 * n], send_sems, recv_sems):
            cp.start()
        refs[-1][...] = jnp.zeros_like(refs[-1])

    lands = [lax.empty((3,) + p.shape[1:], p.dtype) for p in parts]
    hbm_like = lambda a: pltpu.HBM(a.shape, a.dtype)
    out = pl.pallas_call(
        body, name=name,
        out_shape=[pltpu.SemaphoreType.DMA((3 * n,)), pltpu.SemaphoreType.DMA((3 * n,))]
        + [hbm_like(p) for p in parts] + [hbm_like(l) for l in lands] + [jax.ShapeDtypeStruct((8, 128), F32)],
        in_specs=[HBM] * (2 * n), out_specs=[SEM] * 2 + [HBM] * (2 * n) + [BS(memory_space=pltpu.VMEM)],
        input_output_aliases={i: 2 + i for i in range(2 * n)},
        compiler_params=pltpu.CompilerParams(has_side_effects=_SIDE_EFFECT))(*[_IN_HBM(p) for p in parts], *[_IN_HBM(l) for l in lands])
    return out[:2], out[2:2 + n], out[2 + n:2 + 2 * n], out[-1]


def exchange_chips_wait(sems, parts, lands, after, name):
    n = len(parts)

    def body(*refs):
        send_sems, recv_sems = refs[2 * n:2 * n + 2]
        for cp in _exchange_copies(refs[:n], refs[n:2 * n], send_sems, recv_sems):
            cp.wait_send()
            cp.wait_recv()

    hbm_like = lambda a: pltpu.HBM(a.shape, a.dtype)
    out = pl.pallas_call(
        body, name=name, out_shape=[hbm_like(p) for p in parts] + [hbm_like(l) for l in lands],
        in_specs=[HBM] * (2 * n) + [SEM] * 2 + [BS(memory_space=pl.ANY)], out_specs=[HBM] * (2 * n),
        input_output_aliases={i: i for i in range(2 * n)},
        compiler_params=pltpu.CompilerParams(has_side_effects=_SIDE_EFFECT))(*parts, *lands, *sems, after)
    return out[n:]


def _half_block(shape2, split):
    axis, size = split
    return (size, shape2[1]) if axis == 0 else (shape2[0], size)


def add_pairs(parts, halves, splits, core, name):
    n = len(parts)

    def body(s_ref, *refs):
        for a_ref, b_ref, o_ref in zip(refs[:n], refs[n:2 * n], refs[2 * n:]):
            o_ref[...] = (a_ref[...].astype(F32) + b_ref[...].astype(F32)).astype(BF16)

    def mine(i):
        blk = (None,) + _half_block(parts[i].shape[1:], splits[i])
        if splits[i][0] == 0:
            return BS(blk, lambda q, s: (q, s[0], 0))
        return BS(blk, lambda q, s: (q, 0, s[0]))

    half_specs = [BS((None,) + h.shape[1:], lambda q, s: (q, 0, 0)) for h in halves]
    return pl.pallas_call(
        body, name=name,
        grid_spec=pltpu.PrefetchScalarGridSpec(num_scalar_prefetch=1, grid=(N_CHIPS,),
                                               in_specs=[mine(i) for i in range(n)] + half_specs, out_specs=half_specs),
        out_shape=[jax.ShapeDtypeStruct(h.shape, BF16) for h in halves], compiler_params=_arb(1))(core, *parts, *halves)


def add_fives(parts, halves, from_chips, splits, chip_core, name):
    n = len(parts)

    def body(s_ref, *refs):
        for a_ref, b_ref, p_ref, o_ref in zip(refs[:n], refs[n:2 * n], refs[2 * n:3 * n], refs[3 * n:]):
            s = a_ref[...].astype(F32) + b_ref[...].astype(F32)
            for j in range(3):
                s = s + p_ref[j].astype(F32)
            o_ref[...] = s

    def mine(i):
        blk = (None,) + _half_block(parts[i].shape[1:], splits[i])
        if splits[i][0] == 0:
            return BS(blk, lambda g, s: (s[0], s[1], 0))
        return BS(blk, lambda g, s: (s[0], 0, s[1]))

    half_specs = [BS((None,) + h.shape[1:], lambda g, s: (s[0], 0, 0)) for h in halves]
    chip_specs = [BS(p.shape, lambda g, s: (0, 0, 0)) for p in from_chips]
    out_specs = [BS(h.shape[1:], lambda g, s: (0, 0)) for h in halves]
    return pl.pallas_call(
        body, name=name,
        grid_spec=pltpu.PrefetchScalarGridSpec(num_scalar_prefetch=1, grid=(1,),
                                               in_specs=[mine(i) for i in range(n)] + half_specs + chip_specs, out_specs=out_specs),
        out_shape=[jax.ShapeDtypeStruct(h.shape[1:], F32) for h in halves], compiler_params=_arb(1))(chip_core, *parts, *halves, *from_chips)


def sum_leading(a, name):
    def body(a_ref, o_ref):
        s = a_ref[0]
        for j in range(1, a.shape[0]):
            s = s + a_ref[j]
        o_ref[...] = s

    return pl.pallas_call(body, name=name, out_shape=jax.ShapeDtypeStruct(a.shape[1:], a.dtype))(a)


def _adamw_math(w, g, m, v):
    mn = ADAM_B1 * m + (1.0 - ADAM_B1) * g
    vn = ADAM_B2 * v + (1.0 - ADAM_B2) * (g * g)
    m_hat = mn / (1.0 - ADAM_B1 ** ADAM_STEP)
    v_hat = vn / (1.0 - ADAM_B2 ** ADAM_STEP)
    return -ADAM_LR * (m_hat / (jnp.sqrt(v_hat) + ADAM_EPS) + ADAM_WD * w), mn, vn


def adamw(w, g, m, v, name):
    R, C = g.shape
    lead = (None,) * (w.ndim - 2)

    def body(w_ref, g_ref, m_ref, v_ref, d_ref, mo_ref, vo_ref):
        d_ref[...], mo_ref[...], vo_ref[...] = _adamw_math(w_ref[...], g_ref[...], m_ref[...], v_ref[...])

    wblk = BS(lead + (R, C), lambda i: (0,) * w.ndim)
    gblk = BS((R, C), lambda i: (0, 0))
    return pl.pallas_call(
        body, name=name, grid=(1,), in_specs=[wblk, gblk, wblk, wblk], out_specs=[wblk] * 3,
        out_shape=[jax.ShapeDtypeStruct(w.shape, F32)] * 3, compiler_params=_arb(1))(w, g, m, v)


def adamw_halves(w, mine, other, m, v, split, core, name):
    R, C = w.shape[-2:]
    axis, size = split
    lead = (None,) * (w.ndim - 2)
    zeros = (0,) * (w.ndim - 2)
    if axis == 0:
        tr = size if size <= 256 else next(t for t in range(256, 7, -1) if size % t == 0 and t % 8 == 0)
        nb = size // tr
        whole = BS(lead + (tr, C), lambda hi, j, s: zeros + (hi * nb + j, 0))
        part = BS((tr, C), lambda hi, j, s: (j, 0))
    else:
        nb = size // 128
        whole = BS(lead + (R, 128), lambda hi, j, s: zeros + (0, hi * nb + j))
        part = BS((R, 128), lambda hi, j, s: (0, j))

    def body(s_ref, w_ref, a_ref, b_ref, m_ref, v_ref, g_ref, d_ref, mo_ref, vo_ref):
        g = jnp.where(pl.program_id(0) == s_ref[0], a_ref[...], b_ref[...])
        g_ref[...] = g
        d_ref[...], mo_ref[...], vo_ref[...] = _adamw_math(w_ref[...], g, m_ref[...], v_ref[...])

    return pl.pallas_call(
        body, name=name,
        grid_spec=pltpu.PrefetchScalarGridSpec(num_scalar_prefetch=1, grid=(2, nb),
                                               in_specs=[whole, part, part, whole, whole], out_specs=[whole] * 4),
        out_shape=[jax.ShapeDtypeStruct(w.shape, F32)] * 4, compiler_params=_arb(2))(core, w, mine, other, m, v)


def dense_bf16(w3, name):
    R, _, K = w3.shape
    kh = K // 2

    def body(w_hbm, o_ref, buf, sem):
        cp = pltpu.make_async_copy(w_hbm.at[:, 0], buf, sem)
        cp.start()
        cp.wait()
        o_ref[0] = buf[:, :kh].astype(BF16)
        o_ref[1] = buf[:, kh:].astype(BF16)

    return pl.pallas_call(
        body, name=name, in_specs=[ANY], out_specs=BS(memory_space=pltpu.VMEM), out_shape=jax.ShapeDtypeStruct((2, R, kh), BF16),
        scratch_shapes=[pltpu.VMEM((R, K), F32), pltpu.SemaphoreType.DMA(())])(w3)


ROW_BLOCK = 184


def adamw_untiled_rows(w3, mine, other, m3, v3, name):
    R, _, K = w3.shape
    kh = K // 2
    starts = list(range(0, R, ROW_BLOCK))
    sizes = [min(ROW_BLOCK, R - s) for s in starts]
    nblk = len(starts)

    def body(w_hbm, a_ref, b_ref, m_hbm, v_hbm, g_hbm, d_hbm, mo_hbm, vo_hbm,
             wbuf, mbuf, vbuf, gbuf, dbuf, mobuf, vobuf, in_sems, out_sems):
        first = lax.axis_index("c") == 0
        ins = []
        for k, (r0, n) in enumerate(zip(starts, sizes)):
            rows = pl.ds(r0, n)
            cps = [pltpu.make_async_copy(src.at[rows, 0], dst.at[rows], in_sems.at[3 * k + i])
                   for i, (src, dst) in enumerate(((w_hbm, wbuf), (m_hbm, mbuf), (v_hbm, vbuf)))]
            for cp in cps:
                cp.start()
            ins.append(cps)

        def update(rows):
            a, b = a_ref[rows, :], b_ref[rows, :]
            g = jnp.concatenate([jnp.where(first, a, b), jnp.where(first, b, a)], axis=1)
            gbuf[rows, :] = g
            dbuf[rows, :], mobuf[rows, :], vobuf[rows, :] = _adamw_math(wbuf[rows, :], g, mbuf[rows, :], vbuf[rows, :])

        outs = []
        for k, (r0, n) in enumerate(zip(starts, sizes)):
            for cp in ins[k]:
                cp.wait()
            groups, tail = n // 8, n % 8

            def group(i, carry, r0=r0):
                update(pl.ds(pl.multiple_of(r0 + i * 8, 8), 8))
                return carry

            lax.fori_loop(0, groups, group, 0)
            if tail:
                update(pl.ds(r0 + groups * 8, tail))
            rows = pl.ds(r0, n)
            cps = [pltpu.make_async_copy(src.at[rows], dst.at[rows, 0], out_sems.at[4 * k + i])
                   for i, (src, dst) in enumerate(((gbuf, g_hbm), (dbuf, d_hbm), (mobuf, mo_hbm), (vobuf, vo_hbm)))]
            for cp in cps:
                cp.start()
            outs += cps
        for cp in outs:
            cp.wait()

    vmem = BS(memory_space=pltpu.VMEM)
    return pl.pallas_call(
        body, name=name, in_specs=[ANY, vmem, vmem, ANY, ANY], out_specs=[ANY] * 4,
        out_shape=[jax.ShapeDtypeStruct(w3.shape, F32)] * 4,
        scratch_shapes=[pltpu.VMEM((R, K), F32)] * 7 + [pltpu.SemaphoreType.DMA((3 * nblk,)), pltpu.SemaphoreType.DMA((4 * nblk,))])(
            w3, mine, other, m3, v3)


def adamw_w_q_b(w, mine, other, m, v, name):
    def body(w_ref, a_ref, b_ref, m_ref, v_ref, g_ref, d_ref, mo_ref, vo_ref):
        first = lax.axis_index("c") == 0
        lo = jnp.where(first, a_ref[...], b_ref[...])
        hi = jnp.where(first, b_ref[...], a_ref[...])
        g = jnp.concatenate([lo, hi[0:32], hi[64:96]], axis=0)
        g_ref[...] = g
        d_ref[...], mo_ref[...], vo_ref[...] = _adamw_math(w_ref[...], g, m_ref[...], v_ref[...])

    return pl.pallas_call(body, name=name, out_shape=[jax.ShapeDtypeStruct(w.shape, F32)] * 4)(w, mine, other, m, v)


def local_step(x, mem, positions, tgt, norm_in, weights, big_grads_ready, q_a_norm, kv_a_norm, gdn_conv, gdn_a_log,
               gdn_dt_bias, gdn_norm, mem_norm, norm_final):
    B, S, D = x.shape
    M = mem.shape[1]
    T = B * S
    N = S // CHUNK
    x2d = x.reshape(T, D)
    mem2d = mem.reshape(B * M, D)
    tgt2d = tgt.reshape(T, D)

    alog_row, dt_row = _lane_row(gdn_a_log), _lane_row(gdn_dt_bias)

    half = MLA_ROPE // 2
    inv_freq = 1.0 / (ROPE_THETA ** (jnp.arange(half, dtype=F32) / half))
    z32 = jnp.zeros((half,), F32)
    o32 = jnp.ones((half,), F32)
    inv_row = jnp.concatenate([inv_freq, z32, inv_freq, z32]).reshape(1, 128)
    sgn_row = jnp.concatenate([-o32, z32, o32, z32]).reshape(1, 128)
    msk_row = jnp.concatenate([o32, z32, o32, z32]).reshape(1, 128)
    cos_t, sin_t = rope_tables(positions.reshape(T, 1), inv_row, sgn_row, msk_row, after=weights[3])

    h = rms_fwd(x2d, norm_in, "rms_in", after=weights[3])
    wp, behind = weights[0]((h, cos_t))
    P = mm(h, wp, "nt", BF16, "in_proj", bm=512, bn=1536, n_outer=True, after=behind)
    wq, wkv = weights[1](P)
    Q, K, V, qn, kvn = mla_prep(P, q_a_norm, kv_a_norm, wq, wkv, cos_t, sin_t)
    o_mla, lse = mla_attn_fwd(Q, K, V, B, S)
    qkv = gdn_prep_fwd(P, gdn_conv, B, S)
    GB = gdn_gate_fwd(P, alog_row, dt_row, B, S)
    Grow = jnp.transpose(GB[:, :N_HEADS].reshape(B, N, CHUNK, N_HEADS), (0, 3, 1, 2))
    U, W, Tinv, A = gdn_chunk_fwd(qkv, GB, Grow, B, S)
    qkv3, GB3 = qkv.reshape(B, S, GDN_QKV), GB.reshape(B, S, 128)
    W3 = W.reshape(B, S, 512)
    o_gdn3, Vn3, St = gdn_scan_fwd(qkv3, U.reshape(B, S, 512), W3, GB3, A, B, S)
    o_gdn = o_gdn3.reshape(T, 512)
    w_mem_kv, w_out = weights[2](o_gdn)
    memn = rms_fwd(mem2d, mem_norm, "rms_mem")
    MKV = mm(memn, w_mem_kv, "nn", BF16, "mem_kv_proj")
    o_mem = mem_attn_fwd(P, MKV, B, S, M)
    mixed, dx2, dx2b, sq, g_norm_final = merge_fwd(o_mla, o_gdn, o_mem, P, x2d, tgt2d, w_out, gdn_norm, norm_final.reshape(1, D))

    g_w_out = mm(mixed, dx2b, "tn", BF16, "grad_w_out")
    dgate, do_mla, do_gdn, do_mem, g_gdn_norm = merge_bwd(dx2b, o_mla, o_gdn, o_mem, P, w_out, gdn_norm)

    dmemq, dMKV = mem_attn_bwd(P, MKV, do_mem, B, S, M)
    g_w_mem_kv = mm(memn, dMKV, "tn", BF16, "grad_w_mem_kv")
    started_early = big_grads_ready(dict(w_mem_kv=g_w_mem_kv, w_out=g_w_out), "early")
    dmemn = mm(dMKV, w_mem_kv, "nt", F32, "d_memn", after=(started_early,))
    g_mem_norm = gain_grad(mem2d, dmemn, "grad_mem_norm")

    dU3, dW3, dQ13, dK13, dA, dG13 = gdn_scan_bwd(do_gdn.reshape(B, S, 512), qkv3, W3, Vn3, GB3, A, St, B, S)
    r2 = lambda a: a.reshape(T, a.shape[-1])
    dqkv, dGB = gdn_chunk_bwd(qkv, GB, Grow, Tinv, dA, r2(dU3), r2(dW3), r2(dQ13), r2(dK13), r2(dG13), B, S)
    dPg, g_conv = gdn_prep_bwd(P, dqkv, gdn_conv, B, S)
    dab, g_ab = gdn_gate_bwd(P, dGB, alog_row, dt_row, B, S)

    dQ, dK, dV = mla_attn_bwd(Q, K, V, o_mla, do_mla, lse, B, S)
    dq_lin, dkv_lin, dPm, g_q_a_norm, g_kv_a_norm = mla_proj_bwd(dQ, dK, dV, cos_t, sin_t, P, dab, wq, wkv, q_a_norm, kv_a_norm)
    g_wq = mm(dq_lin, qn, "tn", BF16, "grad_w_q_b")
    g_wkv = mm(kvn, dkv_lin, "tn", BF16, "grad_w_kv_b")

    dP = [dPm, dmemq, dPg, dgate]
    g_wp = mm_cols_tn(dP, h, BF16, "grad_w_in")
    started = big_grads_ready(dict(w_in=g_wp, w_q_b=g_wq, w_kv_b=g_wkv), "late")
    grad_x, g_norm_in = in_proj_bwd(dP, wp, x2d, dx2, norm_in, started)

    grads = dict(
        norm_in=g_norm_in, q_a_norm=g_q_a_norm, kv_a_norm=g_kv_a_norm, gdn_conv=g_conv,
        gdn_a_log=g_ab[0:1, :N_HEADS], gdn_dt_bias=g_ab[1:2, :N_HEADS], gdn_norm=g_gdn_norm,
        mem_norm=g_mem_norm, norm_final=g_norm_final)
    return sq, grad_x.reshape(B, S, D), grads


def kernel(x, mem, positions, norm_in, w_in, q_a_norm, w_q_b, kv_a_norm, w_kv_b, gdn_conv, gdn_a_log, gdn_dt_bias, gdn_norm, mem_norm, w_mem_kv, w_out, norm_final, loss_target, m_norm_in, m_w_in, m_q_a_norm, m_w_q_b, m_kv_a_norm, m_w_kv_b, m_gdn_conv, m_gdn_a_log, m_gdn_dt_bias, m_gdn_norm, m_mem_norm, m_w_mem_kv, m_w_out, m_norm_final, v_norm_in, v_w_in, v_q_a_norm, v_w_q_b, v_kv_a_norm, v_w_kv_b, v_gdn_conv, v_gdn_a_log, v_gdn_dt_bias, v_gdn_norm, v_mem_norm, v_w_mem_kv, v_w_out, v_norm_final):
    B = x.shape[0]
    cx, cy, cc = lax.axis_index("x"), lax.axis_index("y"), lax.axis_index("c")
    chip = 2 * cx + cy

    big_names = ("w_in", "w_q_b", "w_kv_b", "w_mem_kv", "w_out")
    rows_major = lambda a: jnp.transpose(a, (2, 0, 1))
    w_in3, m_in3, v_in3 = rows_major(w_in), rows_major(m_w_in), rows_major(v_w_in)
    w_qb_t, m_qb_t, v_qb_t = jnp.transpose(w_q_b[0]), jnp.transpose(m_w_q_b[0]), jnp.transpose(v_w_q_b[0])
    z32 = jnp.zeros((32, Q_LORA), BF16)
    qb_bf = w_qb_t.astype(BF16)
    qb_padded = jnp.concatenate([qb_bf[:160], z32, qb_bf[160:], z32])
    shards = [dense_bf16(w_in3, "w_in_bf16"), qb_padded, w_kv_b[0].astype(BF16), w_mem_kv[0].astype(BF16), w_out[0].astype(BF16)]
    splits = [(1, D_MODEL // 2)] + [(0, s.shape[0] // 2) for s in shards[1:]]
    *w_in_flight, w_in_started = half_gather_start(shards[0], "w_in_gather_start")
    conv_all = allgather_devices(gdn_conv[0], "allgather_conv")
    conv_cols = gdn_conv.shape[2]
    conv_full = jnp.transpose(conv_all[0::2], (1, 0, 2)).reshape(GDN_CONV, N_CHIPS * conv_cols)
    late_shapes = [(N_CHIPS,) + s.shape for s in shards[1:]]
    late = {}

    def w_in_ready(after):
        g_in = pass_halves_to_sibling(half_gather_wait(*w_in_flight, after, "w_in_gather_wait"), "w_in_gather_sibling")
        *late["a"], started_a = late_gather_start(shards[1:3], g_in, "late_gather_qkv_start")
        *late["b"], started_b = late_gather_start(shards[3:], started_a, "late_gather_mem_out_start")
        return pad_w_in_t(g_in), (started_a, started_b)

    def late_qkv(after):
        g_qb, g_kvb = late_gather_wait(*late["a"], after, "late_gather_qkv_wait")
        return g_qb.reshape(-1, Q_LORA), _perm_w_kv_b(g_kvb)

    def late_mem_out(after):
        g_mem, g_out_w = late_gather_wait(*late["b"], after, "late_gather_mem_out_wait")
        return g_mem.reshape(-1, g_mem.shape[2]), g_out_w.reshape(-1, g_out_w.shape[2])

    weights = (w_in_ready, late_qkv, late_mem_out, (w_in_started,))

    core = jnp.stack([cc]).astype(jnp.int32)
    chip_core = jnp.stack([chip, cc]).astype(jnp.int32)
    exchanges = {}
    by_chip = dict(w_in=unpad_w_in_t, w_q_b=lambda a: a.reshape(late_shapes[0]), w_kv_b=_unperm_w_kv_b,
                   w_mem_kv=lambda a: a.reshape(late_shapes[2]), w_out=lambda a: a.reshape(late_shapes[3]))

    def big_grads_ready(gb, group):
        idx = [big_names.index(n) for n in gb]
        parts = [by_chip[n](a) for n, a in gb.items()]
        sp = [splits[i] for i in idx]
        from_sibling = swap_sibling(parts, "rs_sibling_partial_" + group, sp)
        chip_sums = add_pairs(parts, from_sibling, sp, core, "rs_add_sibling_" + group)
        sems, sums_thru, lands, token = exchange_chips_start(chip_sums, "rs_exchange_start_" + group)
        exchanges[group] = dict(idx=idx, parts=parts, from_sibling=from_sibling, sems=sems, sums=sums_thru, lands=lands)
        return token

    sq, grad_x, g = local_step(x, mem, positions, loss_target, norm_in, weights, big_grads_ready, q_a_norm, kv_a_norm, conv_full,
                               gdn_a_log, gdn_dt_bias, gdn_norm, mem_norm, norm_final)

    small_names = ("norm_in", "q_a_norm", "kv_a_norm", "gdn_a_log", "gdn_dt_bias", "gdn_norm", "mem_norm", "norm_final")
    small = dict(norm_in=norm_in, q_a_norm=q_a_norm, kv_a_norm=kv_a_norm, gdn_a_log=gdn_a_log, gdn_dt_bias=gdn_dt_bias,
                 gdn_norm=gdn_norm, mem_norm=mem_norm, norm_final=norm_final)
    m_small = dict(norm_in=m_norm_in, q_a_norm=m_q_a_norm, kv_a_norm=m_kv_a_norm, gdn_a_log=m_gdn_a_log,
                   gdn_dt_bias=m_gdn_dt_bias, gdn_norm=m_gdn_norm, mem_norm=m_mem_norm, norm_final=m_norm_final)
    v_small = dict(norm_in=v_norm_in, q_a_norm=v_q_a_norm, kv_a_norm=v_kv_a_norm, gdn_a_log=v_gdn_a_log,
                   gdn_dt_bias=v_gdn_dt_bias, gdn_norm=v_gdn_norm, mem_norm=v_mem_norm, norm_final=v_norm_final)
    rows = lambda d: jnp.stack([jnp.pad(d[n].reshape(-1), (0, 1024 - d[n].size)) for n in small_names])
    conv_rows = GDN_CONV * GDN_QKV // 1024
    g_block = jnp.concatenate([rows(g), g["gdn_conv"].reshape(conv_rows, 1024), sq, jnp.zeros((16 - 9 - conv_rows, 1024), F32)])
    g_block = sum_leading(allgather_devices(g_block, "allgather_small_grads"), "sum_small_grads")
    g_small_rows = g_block[:8]
    loss = 0.5 * jnp.sum(g_block[8 + conv_rows]) / D_MODEL
    g_conv = lax.dynamic_slice_in_dim(g_block[8:8 + conv_rows].reshape(GDN_CONV, GDN_QKV), chip * conv_cols, conv_cols, axis=1)
    d_s, m_s, v_s = adamw(rows(small), g_small_rows, rows(m_small), rows(v_small), "adamw_small")
    unrow = lambda r: {n: r[i, :small[n].size].reshape(small[n].shape) for i, n in enumerate(small_names)}
    g_out, d_out, m_out, v_out = unrow(g_small_rows), unrow(d_s), unrow(m_s), unrow(v_s)

    my_half = [None] * len(big_names)
    for group, e in exchanges.items():
        from_chips = exchange_chips_wait(e["sems"], e["sums"], e["lands"], d_s, "rs_exchange_wait_" + group)
        halves = add_fives(e["parts"], e["from_sibling"], from_chips, [splits[i] for i in e["idx"]], chip_core, "rs_add_chips_" + group)
        for i, a in zip(e["idx"], halves):
            my_half[i] = a
    other_half = swap_sibling(my_half, "rs_sibling_final")

    d_out["gdn_conv"], m_out["gdn_conv"], v_out["gdn_conv"] = adamw(gdn_conv, g_conv, m_gdn_conv, v_gdn_conv, "adamw_gdn_conv")
    g_out["gdn_conv"] = g_conv[None]
    res = adamw_untiled_rows(w_in3, my_half[0], other_half[0], m_in3, v_in3, "adamw_w_in")
    g_out["w_in"], d_out["w_in"], m_out["w_in"], v_out["w_in"] = [jnp.transpose(r, (1, 2, 0)) for r in res]
    res = adamw_w_q_b(w_qb_t, my_half[1], other_half[1], m_qb_t, v_qb_t, "adamw_w_q_b")
    g_out["w_q_b"], d_out["w_q_b"], m_out["w_q_b"], v_out["w_q_b"] = [jnp.transpose(r)[None] for r in res]
    rest = dict(w_kv_b=(w_kv_b, m_w_kv_b, v_w_kv_b), w_mem_kv=(w_mem_kv, m_w_mem_kv, v_w_mem_kv), w_out=(w_out, m_w_out, v_w_out))
    for i, n in enumerate(big_names):
        if n in rest:
            w_n, m_n, v_n = rest[n]
            g_out[n], d_out[n], m_out[n], v_out[n] = adamw_halves(w_n, my_half[i], other_half[i], m_n, v_n, splits[i], core, "adamw_" + n)

    order = ("norm_in", "w_in", "q_a_norm", "w_q_b", "kv_a_norm", "w_kv_b", "gdn_conv", "gdn_a_log", "gdn_dt_bias",
             "gdn_norm", "mem_norm", "w_mem_kv", "w_out", "norm_final")
    return (loss, grad_x, *[g_out[n] for n in order], *[d_out[n] for n in order], *[m_out[n] for n in order],
            *[v_out[n] for n in order])
```

```python
import jax
import jax.numpy as jnp
from jax import lax
from jax.experimental import pallas as pl
from jax.experimental.pallas import tpu as pltpu

F32 = jnp.float32
BF16 = jnp.bfloat16
BS = pl.BlockSpec

D_MODEL = 1024
N_HEADS = 4
MLA_NOPE, MLA_ROPE, MLA_V = 128, 64, 128
Q_LORA, KV_LORA = 384, 256
ROPE_THETA = 10000.0
GDN_DK = GDN_DV = 128
GDN_CONV = 4
CHUNK = 64
MEM_DH = 128
D_MIX = 1536
GDN_QKV = 1536
D_IN = 4296
EPS = 1e-6
ADAM_LR, ADAM_B1, ADAM_B2, ADAM_EPS, ADAM_WD, ADAM_STEP = 0.001, 0.9, 0.999, 1e-08, 0.01, 10

OFF_MLA = 0
OFF_MEMQ = 1024
OFF_GDN = 1536
OFF_GATE = 3072
N_PAD = 4608
HEAD_PAD = 256
MLA_SCALE = (MLA_NOPE + MLA_ROPE) ** -0.5
MEM_SCALE = MEM_DH ** -0.5
GDN_SCALE = GDN_DK ** -0.5
NEG = -1e30

NN = ((1,), (0,))
NT = ((1,), (1,))
TN = ((0,), (0,))


def _dot(a, b, dims):
    return lax.dot_general(a, b, (dims, ((), ())), preferred_element_type=F32)


def _bdot(spec, a, b, precision=None):
    return jnp.einsum(spec, a, b, preferred_element_type=F32, precision=precision)


def _arb(n):
    return pltpu.CompilerParams(dimension_semantics=("arbitrary",) * n)


def _sigmoid(x):
    return 1.0 / (1.0 + jnp.exp(-x))


def _softplus(z):
    return jnp.maximum(z, 0.0) + jnp.log(1.0 + jnp.exp(-jnp.abs(z)))


def _rope(t, cos_row, sin_row):
    return t * cos_row + pltpu.roll(t, 64, 1) * sin_row


def _rope_bwd(d, cos_row, sin_row):
    return d * cos_row + pltpu.roll(d * sin_row, 64, 1)


def rms_fwd(x, gain, name, tm=512, after=()):
    T, n = x.shape
    tm = min(tm, T)

    def body(x_ref, g_ref, *rest):
        xv = x_ref[...]
        r = lax.rsqrt(jnp.mean(xv * xv, axis=-1, keepdims=True) + EPS)
        rest[-1][...] = (xv * r * g_ref[...]).astype(BF16)

    return pl.pallas_call(
        body, name=name, grid=(T // tm,),
        in_specs=[BS((tm, n), lambda i: (i, 0)), BS((1, n), lambda i: (0, 0))] + [BS(memory_space=pl.ANY)] * len(after),
        out_specs=BS((tm, n), lambda i: (i, 0)),
        out_shape=jax.ShapeDtypeStruct((T, n), BF16), compiler_params=_arb(1))(x, gain, *after)


def mm(a, b, kind, out_dtype, name, bm=512, bn=None, n_outer=False, after=()):
    if kind == "nn":
        (M, K), (_, N) = a.shape, b.shape
    elif kind == "nt":
        (M, K), (N, _) = a.shape, b.shape
    else:
        (K, M), (_, N) = a.shape, b.shape
    bm, bn = min(bm, M), min(bn or N, N)
    assert M % bm == 0 and N % bn == 0, (name, M, N, K)
    ij = (lambda g0, g1: (g1, g0)) if n_outer else (lambda g0, g1: (g0, g1))
    a_spec = BS((K, bm), lambda g0, g1: (0, ij(g0, g1)[0])) if kind == "tn" else BS((bm, K), lambda g0, g1: (ij(g0, g1)[0], 0))
    once = dict(pipeline_mode=pl.Buffered(1)) if bn == N else {}
    b_spec = (BS((bn, K), lambda g0, g1: (ij(g0, g1)[1], 0), **once) if kind == "nt"
              else BS((K, bn), lambda g0, g1: (0, ij(g0, g1)[1]), **once))
    dims = {"nn": NN, "nt": NT, "tn": TN}[kind]

    def body(a_ref, b_ref, *rest):
        rest[-1][...] = _dot(a_ref[...].astype(BF16), b_ref[...].astype(BF16), dims).astype(out_dtype)

    grid = (N // bn, M // bm) if n_outer else (M // bm, N // bn)
    return pl.pallas_call(
        body, name=name, grid=grid, in_specs=[a_spec, b_spec] + [BS(memory_space=pl.ANY)] * len(after),
        out_specs=BS((bm, bn), lambda g0, g1: ij(g0, g1)),
        out_shape=jax.ShapeDtypeStruct((M, N), out_dtype), compiler_params=_arb(2))(a, b, *after)


def mm_cols_tn(pieces, b, out_dtype, name, bm=512):
    K, N = b.shape
    tiles = [p.shape[1] // bm for p in pieces]
    firsts = [sum(tiles[:i]) for i in range(len(tiles))]

    def body(*refs):
        b_ref, o_ref = refs[-2], refs[-1]
        i = pl.program_id(0)
        for a_ref, t0, n in zip(refs[:-2], firsts, tiles):
            @pl.when((i >= t0) & (i < t0 + n))
            def _(a_ref=a_ref):
                o_ref[...] = _dot(a_ref[...], b_ref[...], TN).astype(out_dtype)

    a_specs = [BS((K, bm), lambda i, t0=t0, n=n: (0, jnp.clip(i - t0, 0, n - 1))) for t0, n in zip(firsts, tiles)]
    return pl.pallas_call(
        body, name=name, grid=(sum(tiles),),
        in_specs=a_specs + [BS(b.shape, lambda i: (0, 0), pipeline_mode=pl.Buffered(1))],
        out_specs=BS((bm, N), lambda i: (i, 0)), out_shape=jax.ShapeDtypeStruct((sum(tiles) * bm, N), out_dtype),
        compiler_params=_arb(1))(*pieces, b)


def rope_tables(pos_col, inv_row, sgn_row, msk_row, tm=512, after=()):
    T = pos_col.shape[0]
    tm = min(tm, T)

    def body(p_ref, inv_ref, sgn_ref, msk_ref, *rest):
        c_ref, s_ref = rest[-2:]
        ang = p_ref[...].astype(F32) * inv_ref[...]
        c_ref[...] = jnp.cos(ang) * msk_ref[...]
        s_ref[...] = jnp.sin(ang) * sgn_ref[...]

    row = BS((1, 128), lambda i: (0, 0))
    return pl.pallas_call(
        body, name="rope_tables", grid=(T // tm,),
        in_specs=[BS((tm, 1), lambda i: (i, 0)), row, row, row] + [BS(memory_space=pl.ANY)] * len(after),
        out_specs=[BS((tm, 128), lambda i: (i, 0))] * 2,
        out_shape=[jax.ShapeDtypeStruct((T, 128), F32)] * 2, compiler_params=_arb(1))(pos_col, inv_row, sgn_row, msk_row, *after)


def mla_prep(P, gq, gkv, wq, wkv, cos_t, sin_t, tm=512):
    T = P.shape[0]
    tm = min(tm, T)

    def body(p_ref, gq_ref, gkv_ref, wq_ref, wkv_ref, c_ref, s_ref, q_ref, k_ref, v_ref, qn_ref, kvn_ref):
        p = p_ref[...].astype(F32)
        cq, ckv, kr = p[:, :Q_LORA], p[:, Q_LORA:Q_LORA + KV_LORA], p[:, 640:768]
        qn = (cq * lax.rsqrt(jnp.mean(cq * cq, axis=-1, keepdims=True) + EPS) * gq_ref[...]).astype(BF16)
        kvn = (ckv * lax.rsqrt(jnp.mean(ckv * ckv, axis=-1, keepdims=True) + EPS) * gkv_ref[...]).astype(BF16)
        qn_ref[...] = qn
        kvn_ref[...] = kvn
        q = _dot(qn, wq_ref[...], NT)
        kv = _dot(kvn, wkv_ref[...], NN)
        cos_row, sin_row = c_ref[...], s_ref[...]
        krr = _rope(kr, cos_row, sin_row).astype(BF16)
        for h in range(N_HEADS):
            lo = h * HEAD_PAD
            q_ref[:, lo:lo + 128] = (q[:, lo:lo + 128] * MLA_SCALE).astype(BF16)
            q_ref[:, lo + 128:lo + 256] = (_rope(q[:, lo + 128:lo + 256], cos_row, sin_row) * MLA_SCALE).astype(BF16)
            k_ref[:, lo:lo + 128] = kv[:, h * 128:(h + 1) * 128].astype(BF16)
            k_ref[:, lo + 128:lo + 256] = krr
            v_ref[:, lo:lo + 128] = kv[:, 512 + h * 128:512 + (h + 1) * 128].astype(BF16)
            v_ref[:, lo + 128:lo + 256] = jnp.ones((tm, 128), BF16)

    full = lambda r, c: BS((r, c), lambda i: (0, 0))
    rowb = lambda c: BS((tm, c), lambda i: (i, 0))
    return pl.pallas_call(
        body, name="mla_prep", grid=(T // tm,),
        in_specs=[rowb(1024), full(1, Q_LORA), full(1, KV_LORA), full(1024, Q_LORA), full(KV_LORA, 1024), rowb(128), rowb(128)],
        out_specs=[rowb(1024), rowb(1024), rowb(1024), rowb(Q_LORA), rowb(KV_LORA)],
        out_shape=[jax.ShapeDtypeStruct((T, 1024), BF16), jax.ShapeDtypeStruct((T, 1024), BF16),
                   jax.ShapeDtypeStruct((T, 1024), BF16), jax.ShapeDtypeStruct((T, Q_LORA), BF16),
                   jax.ShapeDtypeStruct((T, KV_LORA), BF16)],
        compiler_params=_arb(1))(P, gq, gkv, wq, wkv, cos_t, sin_t)


ATTN_HEADS_PER_STEP = 2
ATTN_STRIP = 32


def mla_attn_fwd(Q, K, V, B, S, tq=512, hp=ATTN_HEADS_PER_STEP):
    T = B * S
    tq = min(tq, S)
    nq = S // tq

    rs = min(ATTN_STRIP, tq)

    def body(q_ref, k_ref, v_ref, o_ref, lse_ref, m_s, acc_s, s_s, p_s, a_s):
        i = pl.program_id(2)
        m_s[...] = jnp.full_like(m_s, NEG)
        acc_s[...] = jnp.zeros_like(acc_s)

        def blk(j, masked):
            rows = pl.ds(pl.multiple_of(j * tq, tq), tq)
            for h in range(hp):
                hq = slice(h * HEAD_PAD, (h + 1) * HEAD_PAD)
                s_s[h] = _dot(q_ref[:, hq], k_ref[rows, hq], NT)
            for h in range(hp):
                for r0 in range(0, tq, rs):
                    rr = slice(r0, r0 + rs)
                    sv = s_s[h, rr, :]
                    if masked:
                        r = r0 + lax.broadcasted_iota(jnp.int32, (rs, tq), 0)
                        c = lax.broadcasted_iota(jnp.int32, (rs, tq), 1)
                        sv = jnp.where(r >= c, sv, NEG)
                    m_prev = m_s[h, rr, :]
                    m_new = jnp.maximum(m_prev, jnp.max(sv, axis=1, keepdims=True))
                    p_s[h, rr, :] = jnp.exp(sv - m_new).astype(BF16)
                    a_s[h, rr, :] = jnp.exp(m_prev - m_new)
                    m_s[h, rr, :] = m_new
            for h in range(hp):
                hq = slice(h * HEAD_PAD, (h + 1) * HEAD_PAD)
                acc_s[h] = a_s[h] * acc_s[h] + _dot(p_s[h], v_ref[rows, hq], NN)

        def loop(j, c):
            blk(j, False)
            return c

        lax.fori_loop(0, i, loop, 0)
        blk(i, True)
        for h in range(hp):
            den = acc_s[h, :, 128:256]
            o_ref[:, h * 128:(h + 1) * 128] = (acc_s[h, :, 0:128] / den).astype(BF16)
            lse_ref[h] = m_s[h] + jnp.log(den[:, 0:1])

    return pl.pallas_call(
        body, name="mla_attn_fwd", grid=(B, N_HEADS // hp, nq),
        in_specs=[BS((tq, hp * HEAD_PAD), lambda b, h, i: (b * nq + i, h)),
                  BS((S, hp * HEAD_PAD), lambda b, h, i: (b, h)),
                  BS((S, hp * HEAD_PAD), lambda b, h, i: (b, h))],
        out_specs=[BS((tq, hp * 128), lambda b, h, i: (b * nq + i, h)),
                   BS((hp, tq, 1), lambda b, h, i: (h, b * nq + i, 0))],
        out_shape=[jax.ShapeDtypeStruct((T, 512), BF16), jax.ShapeDtypeStruct((N_HEADS, T, 1), F32)],
        scratch_shapes=[pltpu.VMEM((hp, tq, 1), F32), pltpu.VMEM((hp, tq, HEAD_PAD), F32), pltpu.VMEM((hp, tq, tq), F32),
                        pltpu.VMEM((hp, tq, tq), BF16), pltpu.VMEM((hp, tq, 1), F32)],
        compiler_params=_arb(3))(Q, K, V)


def mla_attn_bwd(Q, K, V, O, dO, LSE, B, S, tq=512, hp=ATTN_HEADS_PER_STEP):
    T = B * S
    tq = min(tq, S)
    nq = S // tq

    rs = min(ATTN_STRIP, tq)

    def body(q_ref, k_ref, v_ref, o_ref, do_ref, lse_ref, dq_ref, dk_ref, dv_ref, delta_s, dq_s, dk_s, dv_s, s_s, dp_s, p_s, ds_s):
        j = pl.program_id(2)

        @pl.when(j == 0)
        def _():
            dq_s[...] = jnp.zeros_like(dq_s)
            for h in range(hp):
                sl = slice(h * 128, (h + 1) * 128)
                delta_s[h] = jnp.sum(do_ref[:, sl] * o_ref[:, sl].astype(F32), axis=1, keepdims=True)

        dk_s[...] = jnp.zeros_like(dk_s)
        dv_s[...] = jnp.zeros_like(dv_s)

        def step(i, c):
            rows = pl.ds(pl.multiple_of(i * tq, tq), tq)
            for h in range(hp):
                sq, sv = slice(h * HEAD_PAD, (h + 1) * HEAD_PAD), slice(h * 128, (h + 1) * 128)
                s_s[h] = _dot(q_ref[rows, sq], k_ref[:, sq], NT)
                dp_s[h] = _dot(do_ref[rows, sv].astype(BF16), v_ref[:, h * HEAD_PAD:h * HEAD_PAD + 128], NT)
            for h in range(hp):
                for r0 in range(0, tq, rs):
                    rr = slice(r0, r0 + rs)
                    seq_rows = pl.ds(pl.multiple_of(i * tq + r0, rs), rs)
                    r = i * tq + r0 + lax.broadcasted_iota(jnp.int32, (rs, tq), 0)
                    cc = j * tq + lax.broadcasted_iota(jnp.int32, (rs, tq), 1)
                    p = jnp.where(r >= cc, jnp.exp(s_s[h, rr, :] - lse_ref[h, seq_rows, :]), 0.0)
                    p_s[h, rr, :] = p.astype(BF16)
                    ds_s[h, rr, :] = (p * (dp_s[h, rr, :] - delta_s[h, seq_rows, :])).astype(BF16)
            for h in range(hp):
                sq, sv = slice(h * HEAD_PAD, (h + 1) * HEAD_PAD), slice(h * 128, (h + 1) * 128)
                dv_s[:, sv] += _dot(p_s[h], do_ref[rows, sv].astype(BF16), TN)
                dk_s[:, sq] += _dot(ds_s[h], q_ref[rows, sq], TN)
                dq_s[rows, sq] += _dot(ds_s[h], k_ref[:, sq], NN)
            return c

        lax.fori_loop(j, nq, step, 0)
        dk_ref[...] = dk_s[...].astype(BF16)
        dv_ref[...] = dv_s[...].astype(BF16)

        @pl.when(j == nq - 1)
        def _():
            dq_ref[...] = dq_s[...].astype(BF16)

    seq = lambda c: BS((S, c), lambda b, h, j: (b, h))
    blk = lambda c: BS((tq, c), lambda b, h, j: (b * nq + j, h))
    return pl.pallas_call(
        body, name="mla_attn_bwd", grid=(B, N_HEADS // hp, nq),
        in_specs=[seq(hp * HEAD_PAD), blk(hp * HEAD_PAD), blk(hp * HEAD_PAD), seq(hp * 128), seq(hp * 128),
                  BS((hp, S, 1), lambda b, h, j: (h, b, 0))],
        out_specs=[seq(hp * HEAD_PAD), blk(hp * HEAD_PAD), blk(hp * 128)],
        out_shape=[jax.ShapeDtypeStruct((T, 1024), BF16), jax.ShapeDtypeStruct((T, 1024), BF16),
                   jax.ShapeDtypeStruct((T, 512), BF16)],
        scratch_shapes=[pltpu.VMEM((hp, S, 1), F32), pltpu.VMEM((S, hp * HEAD_PAD), F32),
                        pltpu.VMEM((tq, hp * HEAD_PAD), F32), pltpu.VMEM((tq, hp * 128), F32),
                        pltpu.VMEM((hp, tq, tq), F32), pltpu.VMEM((hp, tq, tq), F32),
                        pltpu.VMEM((hp, tq, tq), BF16), pltpu.VMEM((hp, tq, tq), BF16)],
        compiler_params=_arb(3))(Q, K, V, O, dO, LSE)


def mla_proj_bwd(dQ, dK, dV, cos_t, sin_t, P, dab, wq, wkv, gq, gkv, tm=512):
    T = P.shape[0]
    tm = min(tm, T)

    def norm_bwd(x, dy, g):
        r = lax.rsqrt(jnp.mean(x * x, axis=-1, keepdims=True) + EPS)
        xh = x * r
        dxh = dy * g
        return r * (dxh - xh * jnp.mean(dxh * xh, axis=-1, keepdims=True)), jnp.sum(dy * xh, axis=0, keepdims=True)

    def body(dq_ref, dk_ref, dv_ref, c_ref, s_ref, p_ref, dab_ref, wq_ref, wkv_ref, gq_ref, gkv_ref,
             ql_ref, kvl_ref, o_ref, aq_ref, akv_ref):
        @pl.when(pl.program_id(0) == 0)
        def _():
            aq_ref[...] = jnp.zeros_like(aq_ref)
            akv_ref[...] = jnp.zeros_like(akv_ref)

        cos_row, sin_row = c_ref[...], s_ref[...]
        kr = jnp.zeros((tm, 128), F32)
        for h in range(N_HEADS):
            lo = h * HEAD_PAD
            ql_ref[:, lo:lo + 128] = (dq_ref[:, lo:lo + 128].astype(F32) * MLA_SCALE).astype(BF16)
            ql_ref[:, lo + 128:lo + 256] = (_rope_bwd(dq_ref[:, lo + 128:lo + 256].astype(F32), cos_row, sin_row) * MLA_SCALE).astype(BF16)
            kvl_ref[:, h * 128:(h + 1) * 128] = dk_ref[:, lo:lo + 128]
            kr = kr + dk_ref[:, lo + 128:lo + 256].astype(F32)
        kvl_ref[:, 512:] = dv_ref[...]
        dqn = _dot(ql_ref[...], wq_ref[...], NN)
        dkvn = _dot(kvl_ref[...], wkv_ref[...], NT)
        dcq, ggq = norm_bwd(p_ref[:, :Q_LORA].astype(F32), dqn, gq_ref[...])
        dckv, ggkv = norm_bwd(p_ref[:, Q_LORA:640].astype(F32), dkvn, gkv_ref[...])
        aq_ref[...] += ggq
        akv_ref[...] += ggkv
        o_ref[:, :Q_LORA] = dcq.astype(BF16)
        o_ref[:, Q_LORA:640] = dckv.astype(BF16)
        o_ref[:, 640:768] = _rope_bwd(kr, cos_row, sin_row).astype(BF16)
        o_ref[:, 768:896] = dab_ref[...]
        o_ref[:, 896:1024] = jnp.zeros((tm, 128), BF16)

    rowb = lambda c: BS((tm, c), lambda i: (i, 0))
    full = lambda r, c: BS((r, c), lambda i: (0, 0))
    return pl.pallas_call(
        body, name="mla_proj_bwd", grid=(T // tm,),
        in_specs=[rowb(1024), rowb(1024), rowb(512), rowb(128), rowb(128), rowb(1024), rowb(128),
                  full(1024, Q_LORA), full(KV_LORA, 1024), full(1, Q_LORA), full(1, KV_LORA)],
        out_specs=[rowb(1024), rowb(1024), rowb(1024), full(1, Q_LORA), full(1, KV_LORA)],
        out_shape=[jax.ShapeDtypeStruct((T, 1024), BF16)] * 3
        + [jax.ShapeDtypeStruct((1, Q_LORA), F32), jax.ShapeDtypeStruct((1, KV_LORA), F32)],
        compiler_params=_arb(1))(dQ, dK, dV, cos_t, sin_t, P, dab, wq, wkv, gq, gkv)


def _mem_probs(qh, kh):
    s = _dot(qh, kh, NT) * MEM_SCALE
    p = jnp.exp(s - jnp.max(s, axis=1, keepdims=True))
    return p / jnp.sum(p, axis=1, keepdims=True)


def mem_attn_fwd(P, MKV, B, S, M, tq=512):
    T = B * S
    tq = min(tq, S)
    nq = S // tq

    def body(q_ref, kv_ref, o_ref):
        for h in range(N_HEADS):
            sl = slice(h * 128, (h + 1) * 128)
            p = _mem_probs(q_ref[:, sl].astype(BF16), kv_ref[:, sl])
            o_ref[:, sl] = _dot(p.astype(BF16), kv_ref[:, 512 + h * 128:512 + (h + 1) * 128], NN).astype(BF16)

    return pl.pallas_call(
        body, name="mem_attn_fwd", grid=(B, nq),
        in_specs=[BS((tq, 512), lambda b, i: (b * nq + i, OFF_MEMQ // 512)), BS((M, 1024), lambda b, i: (b, 0))],
        out_specs=BS((tq, 512), lambda b, i: (b * nq + i, 0)),
        out_shape=jax.ShapeDtypeStruct((T, 512), BF16), compiler_params=_arb(2))(P, MKV)


def mem_attn_bwd(P, MKV, dO, B, S, M, tq=512):
    T = B * S
    tq = min(tq, S)
    nq = S // tq

    def body(q_ref, kv_ref, do_ref, dq_ref, dkv_ref):
        @pl.when(pl.program_id(1) == 0)
        def _():
            dkv_ref[...] = jnp.zeros_like(dkv_ref)

        for h in range(N_HEADS):
            sl = slice(h * 128, (h + 1) * 128)
            sv = slice(512 + h * 128, 512 + (h + 1) * 128)
            qh = q_ref[:, sl].astype(BF16)
            kh = kv_ref[:, sl]
            do = do_ref[:, sl].astype(BF16)
            p = _mem_probs(qh, kh)
            dkv_ref[:, sv] += _dot(p.astype(BF16), do, TN)
            dp = _dot(do, kv_ref[:, sv], NT)
            ds = (p * (dp - jnp.sum(dp * p, axis=1, keepdims=True)) * MEM_SCALE).astype(BF16)
            dq_ref[:, sl] = _dot(ds, kh, NN).astype(BF16)
            dkv_ref[:, sl] += _dot(ds, qh, TN)

    return pl.pallas_call(
        body, name="mem_attn_bwd", grid=(B, nq),
        in_specs=[BS((tq, 512), lambda b, i: (b * nq + i, OFF_MEMQ // 512)), BS((M, 1024), lambda b, i: (b, 0)),
                  BS((tq, 512), lambda b, i: (b * nq + i, 0))],
        out_specs=[BS((tq, 512), lambda b, i: (b * nq + i, 0)), BS((M, 1024), lambda b, i: (b, 0))],
        out_shape=[jax.ShapeDtypeStruct((T, 512), BF16), jax.ShapeDtypeStruct((B * M, 1024), F32)],
        compiler_params=_arb(2))(P, MKV, dO)


def gain_grad(x, dy, name, tm=256):
    T, n = x.shape
    tm = min(tm, T)

    def body(x_ref, dy_ref, o_ref):
        @pl.when(pl.program_id(0) == 0)
        def _():
            o_ref[...] = jnp.zeros_like(o_ref)

        xv = x_ref[...]
        xh = xv * lax.rsqrt(jnp.mean(xv * xv, axis=-1, keepdims=True) + EPS)
        o_ref[...] += jnp.sum(dy_ref[...] * xh, axis=0, keepdims=True)

    return pl.pallas_call(
        body, name=name, grid=(T // tm,),
        in_specs=[BS((tm, n), lambda i: (i, 0))] * 2, out_specs=BS((1, n), lambda i: (0, 0)),
        out_shape=jax.ShapeDtypeStruct((1, n), F32), compiler_params=_arb(1))(x, dy)


def _conv_silu(x, w, t):
    y = x * w[3:4, :]
    for s in range(1, GDN_CONV):
        y = y + jnp.where(t >= s, pltpu.roll(x, s, 0), 0.0) * w[3 - s:4 - s, :]
    return y, _sigmoid(y)


def gdn_prep_fwd(P, conv_w, B, S):
    T = B * S

    def body(x_ref, w_ref, o_ref):
        kind = pl.program_id(1)
        t = lax.broadcasted_iota(jnp.int32, (S, 1), 0)
        y, sg = _conv_silu(x_ref[...].astype(F32), w_ref[...], t)
        a = y * sg
        scale = jnp.where(kind == 0, GDN_SCALE, 1.0).astype(F32)
        for h in range(N_HEADS):
            sl = slice(h * 128, (h + 1) * 128)
            seg = a[:, sl]
            n = lax.rsqrt(jnp.sum(seg * seg, axis=-1, keepdims=True) + EPS)
            o_ref[:, sl] = jnp.where(kind < 2, seg * (n * scale), seg)

    return pl.pallas_call(
        body, name="gdn_prep_fwd", grid=(B, 3),
        in_specs=[BS((S, 512), lambda b, k: (b, OFF_GDN // 512 + k)), BS((GDN_CONV, 512), lambda b, k: (0, k))],
        out_specs=BS((S, 512), lambda b, k: (b, k)),
        out_shape=jax.ShapeDtypeStruct((T, GDN_QKV), F32), compiler_params=_arb(2))(P, conv_w)


def gdn_prep_bwd(P, dqkv, conv_w, B, S):
    T = B * S

    def body(x_ref, d_ref, w_ref, o_ref, gw_ref):
        kind = pl.program_id(0)

        @pl.when(pl.program_id(1) == 0)
        def _():
            gw_ref[...] = jnp.zeros_like(gw_ref)

        t = lax.broadcasted_iota(jnp.int32, (S, 1), 0)
        x = x_ref[...].astype(F32)
        w = w_ref[...]
        y, sg = _conv_silu(x, w, t)
        a = y * sg
        scale = jnp.where(kind == 0, GDN_SCALE, 1.0).astype(F32)
        das = []
        for h in range(N_HEADS):
            sl = slice(h * 128, (h + 1) * 128)
            seg, dseg = a[:, sl], d_ref[:, sl]
            n = lax.rsqrt(jnp.sum(seg * seg, axis=-1, keepdims=True) + EPS)
            dn = scale * (n * dseg - seg * (n * n * n) * jnp.sum(dseg * seg, axis=-1, keepdims=True))
            das.append(jnp.where(kind < 2, dn, dseg))
        dy = jnp.concatenate(das, axis=1) * (sg * (1.0 + y * (1.0 - sg)))
        dx = dy * w[3:4, :]
        gw_ref[3:4, :] += jnp.sum(dy * x, axis=0, keepdims=True)
        for s in range(1, GDN_CONV):
            dx = dx + jnp.where(t + s < S, pltpu.roll(dy, S - s, 0), 0.0) * w[3 - s:4 - s, :]
            gw_ref[3 - s:4 - s, :] += jnp.sum(dy * jnp.where(t >= s, pltpu.roll(x, s, 0), 0.0), axis=0, keepdims=True)
        o_ref[...] = dx.astype(BF16)

    return pl.pallas_call(
        body, name="gdn_prep_bwd", grid=(3, B),
        in_specs=[BS((S, 512), lambda k, b: (b, OFF_GDN // 512 + k)), BS((S, 512), lambda k, b: (b, k)),
                  BS((GDN_CONV, 512), lambda k, b: (0, k))],
        out_specs=[BS((S, 512), lambda k, b: (b, k)), BS((GDN_CONV, 512), lambda k, b: (0, k))],
        out_shape=[jax.ShapeDtypeStruct((T, GDN_QKV), BF16), jax.ShapeDtypeStruct((GDN_CONV, GDN_QKV), F32)],
        compiler_params=_arb(2))(P, dqkv, conv_w)


def _chunk_row(n_rows):
    return lax.broadcasted_iota(jnp.int32, (n_rows, 1), 0) % CHUNK


def gdn_gate_fwd(P, alog_row, dt_row, B, S):
    T = B * S

    def body(x_ref, al_ref, dt_ref, o_ref):
        x = x_ref[...].astype(F32)
        lane = lax.broadcasted_iota(jnp.int32, (1, 128), 1)
        g = jnp.where(lane < 4, -jnp.exp(al_ref[...]) * _softplus(x + dt_ref[...]), 0.0)
        t = _chunk_row(S)
        for s in (1, 2, 4, 8, 16, 32):
            g = g + jnp.where(t >= s, pltpu.roll(g, s, 0), 0.0)
        o_ref[...] = jnp.where(lane < 4, g, jnp.where(lane < 8, _sigmoid(x), 0.0))

    row = BS((1, 128), lambda b: (0, 0))
    return pl.pallas_call(
        body, name="gdn_gate_fwd", grid=(B,),
        in_specs=[BS((S, 128), lambda b: (b, 768 // 128)), row, row], out_specs=BS((S, 128), lambda b: (b, 0)),
        out_shape=jax.ShapeDtypeStruct((T, 128), F32), compiler_params=_arb(1))(P, alog_row, dt_row)


def gdn_gate_bwd(P, dGB, alog_row, dt_row, B, S):
    T = B * S

    def body(x_ref, d_ref, al_ref, dt_ref, o_ref, acc_ref):
        @pl.when(pl.program_id(0) == 0)
        def _():
            acc_ref[...] = jnp.zeros_like(acc_ref)

        x, d = x_ref[...].astype(F32), d_ref[...]
        lane = lax.broadcasted_iota(jnp.int32, (1, 128), 1)
        z = x + dt_ref[...]
        coef = -jnp.exp(al_ref[...])
        g = coef * _softplus(z)
        da = jnp.where(lane < 4, d * coef * _sigmoid(z), 0.0)
        beta = _sigmoid(x)
        o_ref[...] = jnp.where(lane < 4, da, jnp.where(lane < 8, d * beta * (1.0 - beta), 0.0)).astype(BF16)
        acc_ref[0:1, :] += jnp.sum(jnp.where(lane < 4, d * g, 0.0), axis=0, keepdims=True)
        acc_ref[1:2, :] += jnp.sum(da, axis=0, keepdims=True)

    row = BS((1, 128), lambda b: (0, 0))
    return pl.pallas_call(
        body, name="gdn_gate_bwd", grid=(B,),
        in_specs=[BS((S, 128), lambda b: (b, 768 // 128)), BS((S, 128), lambda b: (b, 0)), row, row],
        out_specs=[BS((S, 128), lambda b: (b, 0)), BS((8, 128), lambda b: (0, 0))],
        out_shape=[jax.ShapeDtypeStruct((T, 128), BF16), jax.ShapeDtypeStruct((8, 128), F32)],
        compiler_params=_arb(1))(P, dGB, alog_row, dt_row)


def _chunk_masks(nc):
    r = lax.broadcasted_iota(jnp.int32, (nc, CHUNK, CHUNK), 1)
    c = lax.broadcasted_iota(jnp.int32, (nc, CHUNK, CHUNK), 2)
    return r >= c, r > c


def _chunk_local(q, k, gc, gr, beta, incl, strict):
    decay = jnp.exp(jnp.where(incl, gc - gr, NEG))
    kb = k * beta
    kbf = k.astype(BF16)
    m_kk = _bdot("gcd,gjd->gcj", kb.astype(BF16), kbf)
    l_mat = jnp.where(strict, m_kk * decay, 0.0)
    a_mat = _bdot("gcd,gjd->gcj", q.astype(BF16), kbf) * decay
    return decay, kb, l_mat, a_mat


WY_SPLIT_LEVELS = 2


def _split_bf16(x):
    hi = x.astype(BF16)
    return hi, (x - hi.astype(F32)).astype(BF16)


def _mm_split(ah, al, bh, bl):
    spec = "gij,gjk->gik"
    return _bdot(spec, ah, bh) + (_bdot(spec, ah, bl) + _bdot(spec, al, bh))


def gdn_chunk_fwd(qkv, GB, Grow, B, S, nc=8):
    T = B * S
    N = S // CHUNK
    nc = min(nc, N)
    nb = N // nc
    R = nc * CHUNK

    def body(q_ref, k_ref, v_ref, gb_ref, gr_ref, u_ref, w_ref, t_ref, a_ref):
        incl, strict = _chunk_masks(nc)
        eye = (lax.broadcasted_iota(jnp.int32, (nc, CHUNK, CHUNK), 1)
               == lax.broadcasted_iota(jnp.int32, (nc, CHUNK, CHUNK), 2)).astype(F32)
        for h in range(N_HEADS):
            sl = slice(h * 128, (h + 1) * 128)
            q = q_ref[:, sl].reshape(nc, CHUNK, 128)
            k = k_ref[:, sl].reshape(nc, CHUNK, 128)
            v = v_ref[:, sl].reshape(nc, CHUNK, 128)
            gc = gb_ref[:, h:h + 1].reshape(nc, CHUNK, 1)
            beta = gb_ref[:, 4 + h:5 + h].reshape(nc, CHUNK, 1)
            gr = gr_ref[h][:, None, :]
            _, kb, l_mat, a_mat = _chunk_local(q, k, gc, gr, beta, incl, strict)
            pw = -l_mat
            tinv = eye + pw
            for level in range(5):
                if level < WY_SPLIT_LEVELS:
                    ph, pl_ = _split_bf16(pw)
                    pw = _mm_split(ph, pl_, ph, pl_)
                    ph, pl_ = _split_bf16(pw)
                    th, tl = _split_bf16(tinv)
                    tinv = tinv + _mm_split(th, tl, ph, pl_)
                else:
                    ph = pw.astype(BF16)
                    pw = _bdot("gij,gjk->gik", ph, ph)
                    tinv = tinv + _bdot("gij,gjk->gik", tinv.astype(BF16), pw.astype(BF16))
            tb = tinv.astype(BF16)
            u = _bdot("gcj,gjv->gcv", tb, (v * beta).astype(BF16))
            w = _bdot("gcj,gjk->gck", tb, (kb * jnp.exp(gc)).astype(BF16))
            u_ref[:, sl] = u.reshape(R, 128)
            w_ref[:, sl] = w.reshape(R, 128).astype(BF16)
            t_ref[h] = tb
            a_ref[h] = a_mat.astype(BF16)

    rowb = lambda c, j: BS((R, c), lambda b, n: (b * nb + n, j))
    mat = BS((None, N_HEADS, nc, CHUNK, CHUNK), lambda b, n: (b, 0, n, 0, 0))
    return pl.pallas_call(
        body, name="gdn_chunk_fwd", grid=(B, nb),
        in_specs=[rowb(512, 0), rowb(512, 1), rowb(512, 2), rowb(128, 0),
                  BS((None, N_HEADS, nc, CHUNK), lambda b, n: (b, 0, n, 0))],
        out_specs=[rowb(512, 0), rowb(512, 0), mat, mat],
        out_shape=[jax.ShapeDtypeStruct((T, 512), F32), jax.ShapeDtypeStruct((T, 512), BF16),
                   jax.ShapeDtypeStruct((B, N_HEADS, N, CHUNK, CHUNK), BF16),
                   jax.ShapeDtypeStruct((B, N_HEADS, N, CHUNK, CHUNK), BF16)],
        compiler_params=_arb(2))(qkv, qkv, qkv, GB, Grow)


def gdn_scan_fwd(qkv3, U3, W3, GB3, A, B, S):
    N = S // CHUNK

    def body(q_ref, k_ref, u_ref, w_ref, gb_ref, a_ref, o_ref, vn_ref, st_ref, s_s):
        @pl.when(pl.program_id(0) == 0)
        def _():
            s_s[...] = jnp.zeros_like(s_s)

        for b in range(B):
            for h in range(N_HEADS):
                sl = slice(h * 128, (h + 1) * 128)
                st = s_s[b, h]
                st_ref[b, h] = st
                stb = st.astype(BF16)
                g = gb_ref[b, :, h:h + 1]
                gl = g[CHUNK - 1:CHUNK, :]
                vn = u_ref[b, :, sl] - _dot(w_ref[b, :, sl].astype(BF16), stb, NN)
                vnb = vn.astype(BF16)
                o = _dot((q_ref[b, :, sl] * jnp.exp(g)).astype(BF16), stb, NN) + _dot(a_ref[b, h].astype(BF16), vnb, NN)
                vn_ref[b, :, sl] = vnb
                o_ref[b, :, sl] = o.astype(BF16)
                s_s[b, h] = st * jnp.exp(gl) + _dot((k_ref[b, :, sl] * jnp.exp(gl - g)).astype(BF16), vnb, TN)

    tok = lambda c, j: BS((B, CHUNK, c), lambda n: (0, n, j))
    return pl.pallas_call(
        body, name="gdn_scan_fwd", grid=(N,),
        in_specs=[tok(512, 0), tok(512, 1), tok(512, 0), tok(512, 0), tok(128, 0),
                  BS((B, N_HEADS, None, CHUNK, CHUNK), lambda n: (0, 0, n, 0, 0))],
        out_specs=[tok(512, 0), tok(512, 0), BS((B, N_HEADS, None, 128, 128), lambda n: (0, 0, n, 0, 0))],
        out_shape=[jax.ShapeDtypeStruct((B, S, 512), BF16), jax.ShapeDtypeStruct((B, S, 512), BF16),
                   jax.ShapeDtypeStruct((B, N_HEADS, N, 128, 128), F32)],
        scratch_shapes=[pltpu.VMEM((B, N_HEADS, 128, 128), F32)],
        compiler_params=_arb(1))(qkv3, qkv3, U3, W3, GB3, A)


def gdn_scan_bwd(dO3, qkv3, W3, Vn3, GB3, A, St, B, S):
    N = S // CHUNK

    def body(do_ref, q_ref, k_ref, w_ref, vn_ref, gb_ref, a_ref, st_ref,
             du_ref, dw_ref, dq_ref, dk_ref, da_ref, dg_ref, ds_s):
        @pl.when(pl.program_id(0) == 0)
        def _():
            ds_s[...] = jnp.zeros_like(ds_s)

        lane = lax.broadcasted_iota(jnp.int32, (1, 128), 1)
        last = lax.broadcasted_iota(jnp.int32, (CHUNK, 1), 0) == CHUNK - 1
        for b in range(B):
            dg_all = jnp.zeros((CHUNK, 128), F32)
            for h in range(N_HEADS):
                sl = slice(h * 128, (h + 1) * 128)
                st = st_ref[b, h]
                stb = st.astype(BF16)
                dsn = ds_s[b, h]
                dsnb = dsn.astype(BF16)
                g = gb_ref[b, :, h:h + 1]
                gl = g[CHUNK - 1:CHUNK, :]
                egl = jnp.exp(gl)
                ekd = jnp.exp(gl - g)
                eg = jnp.exp(g)
                q, k = q_ref[b, :, sl], k_ref[b, :, sl]
                kd = k * ekd
                qg = q * eg
                do = do_ref[b, :, sl].astype(BF16)
                vnb = vn_ref[b, :, sl].astype(BF16)
                dvn = _dot(a_ref[b, h].astype(BF16), do, TN) + _dot(kd.astype(BF16), dsnb, NN)
                dvnb = dvn.astype(BF16)
                da_ref[b, h] = _dot(do, vnb, NT)
                dqg = _dot(do, stb, NT)
                dkd = _dot(vnb, dsnb, NT)
                ds_s[b, h] = (_dot(qg.astype(BF16), do, TN) + egl * dsn - _dot(w_ref[b, :, sl].astype(BF16), dvnb, TN))
                du_ref[b, :, sl] = dvnb
                dw_ref[b, :, sl] = (-_dot(dvnb, stb, NT)).astype(BF16)
                dq_ref[b, :, sl] = dqg * eg
                dk_ref[b, :, sl] = dkd * ekd
                ddel = jnp.sum(dkd * kd, axis=1, keepdims=True)
                dgl = jnp.sum(ddel, axis=0, keepdims=True) + jnp.sum(jnp.sum(st * dsn, axis=1, keepdims=True), axis=0, keepdims=True) * egl
                col = jnp.sum(dqg * qg, axis=1, keepdims=True) - ddel + jnp.where(last, dgl, 0.0)
                dg_all = jnp.where(lane == h, col, dg_all)
            dg_ref[b] = dg_all

    tok = lambda c, j: BS((B, CHUNK, c), lambda n: (0, N - 1 - n, j))
    mat = lambda d: BS((B, N_HEADS, None, d, d), lambda n: (0, 0, N - 1 - n, 0, 0))
    return pl.pallas_call(
        body, name="gdn_scan_bwd", grid=(N,),
        in_specs=[tok(512, 0), tok(512, 0), tok(512, 1), tok(512, 0), tok(512, 0), tok(128, 0), mat(CHUNK), mat(128)],
        out_specs=[tok(512, 0), tok(512, 0), tok(512, 0), tok(512, 0), mat(CHUNK), tok(128, 0)],
        out_shape=[jax.ShapeDtypeStruct((B, S, 512), BF16)] * 2 + [jax.ShapeDtypeStruct((B, S, 512), F32)] * 2
        + [jax.ShapeDtypeStruct((B, N_HEADS, N, CHUNK, CHUNK), F32), jax.ShapeDtypeStruct((B, S, 128), F32)],
        scratch_shapes=[pltpu.VMEM((B, N_HEADS, 128, 128), F32)],
        compiler_params=_arb(1))(dO3, qkv3, qkv3, W3, Vn3, GB3, A, St)


def gdn_chunk_bwd(qkv, GB, Grow, Tinv, dA, dU, dW, dQ1, dK1, dG1, B, S, nc=8):
    T = B * S
    N = S // CHUNK
    nc = min(nc, N)
    nb = N // nc
    R = nc * CHUNK

    def body(q_ref, k_ref, v_ref, gb_ref, gr_ref, t_ref, da_ref, du_ref, dw_ref, dq1_ref, dk1_ref, dg1_ref, o_ref, dgb_ref):
        incl, strict = _chunk_masks(nc)
        lane = lax.broadcasted_iota(jnp.int32, (1, 128), 1)
        dg_all = dg1_ref[...]
        db_all = jnp.zeros((R, 128), F32)
        for h in range(N_HEADS):
            sl = slice(h * 128, (h + 1) * 128)
            q = q_ref[:, sl].reshape(nc, CHUNK, 128)
            k = k_ref[:, sl].reshape(nc, CHUNK, 128)
            v = v_ref[:, sl].reshape(nc, CHUNK, 128)
            gc = gb_ref[:, h:h + 1].reshape(nc, CHUNK, 1)
            beta = gb_ref[:, 4 + h:5 + h].reshape(nc, CHUNK, 1)
            gr = gr_ref[h][:, None, :]
            decay, kb, l_mat, a_mat = _chunk_local(q, k, gc, gr, beta, incl, strict)
            eg = jnp.exp(gc)
            kbg = kb * eg
            vb = v * beta
            tb = t_ref[h].astype(BF16)
            du = du_ref[:, sl].reshape(nc, CHUNK, 128).astype(BF16)
            dw = dw_ref[:, sl].reshape(nc, CHUNK, 128).astype(BF16)
            dvb = _bdot("gcj,gcv->gjv", tb, du)
            dkbg = _bdot("gcj,gck->gjk", tb, dw)
            dt = _bdot("gcv,gjv->gcj", du, vb.astype(BF16)) + _bdot("gck,gjk->gcj", dw, kbg.astype(BF16))
            tmp = _bdot("gac,gab->gcb", tb, dt.astype(BF16))
            dl = jnp.where(strict, -_bdot("gcb,gdb->gcd", tmp.astype(BF16), tb), 0.0)
            da = da_ref[h]
            dm = (dl * decay).astype(BF16)
            dqk = (da * decay).astype(BF16)
            kbf = k.astype(BF16)
            dkb = _bdot("gcj,gjd->gcd", dm, kbf) + dkbg * eg
            dk = (_bdot("gcj,gcd->gjd", dm, kb.astype(BF16)) + _bdot("gcj,gcd->gjd", dqk, q.astype(BF16))
                  + dk1_ref[:, sl].reshape(nc, CHUNK, 128) + dkb * beta)
            dq = _bdot("gcj,gjd->gcd", dqk, kbf) + dq1_ref[:, sl].reshape(nc, CHUNK, 128)
            e = dl * l_mat + da * a_mat
            dgc = (jnp.sum(e, axis=2, keepdims=True) - jnp.sum(jnp.swapaxes(e, 1, 2), axis=2, keepdims=True)
                   + jnp.sum(dkbg * kbg, axis=2, keepdims=True))
            dbeta = jnp.sum(dkb * k, axis=2, keepdims=True) + jnp.sum(dvb * v, axis=2, keepdims=True)
            o_ref[:, sl] = dq.reshape(R, 128)
            o_ref[:, 512 + h * 128:512 + (h + 1) * 128] = dk.reshape(R, 128)
            o_ref[:, 1024 + h * 128:1024 + (h + 1) * 128] = (dvb * beta).reshape(R, 128)
            dg_all = dg_all + jnp.where(lane == h, dgc.reshape(R, 1), 0.0)
            db_all = jnp.where(lane == 4 + h, dbeta.reshape(R, 1), db_all)
        t = _chunk_row(R)
        for s in (1, 2, 4, 8, 16, 32):
            dg_all = dg_all + jnp.where(t + s < CHUNK, pltpu.roll(dg_all, R - s, 0), 0.0)
        dgb_ref[...] = jnp.where(lane < 4, dg_all, db_all)

    rowb = lambda c, j: BS((R, c), lambda b, n: (b * nb + n, j))
    mat = BS((None, N_HEADS, nc, CHUNK, CHUNK), lambda b, n: (b, 0, n, 0, 0))
    return pl.pallas_call(
        body, name="gdn_chunk_bwd", grid=(B, nb),
        in_specs=[rowb(512, 0), rowb(512, 1), rowb(512, 2), rowb(128, 0),
                  BS((None, N_HEADS, nc, CHUNK), lambda b, n: (b, 0, n, 0)), mat, mat,
                  rowb(512, 0), rowb(512, 0), rowb(512, 0), rowb(512, 0), rowb(128, 0)],
        out_specs=[rowb(GDN_QKV, 0), rowb(128, 0)],
        out_shape=[jax.ShapeDtypeStruct((T, GDN_QKV), F32), jax.ShapeDtypeStruct((T, 128), F32)],
        compiler_params=_arb(2))(qkv, qkv, qkv, GB, Grow, Tinv, dA, dU, dW, dQ1, dK1, dG1)


def _gdn_out_norm(og, gg):
    outs, xhs, rs = [], [], []
    for h in range(N_HEADS):
        seg = og[:, h * 128:(h + 1) * 128]
        r = lax.rsqrt(jnp.mean(seg * seg, axis=-1, keepdims=True) + EPS)
        xh = seg * r
        outs.append(xh * gg)
        xhs.append(xh)
        rs.append(r)
    return outs, xhs, rs


def merge_fwd(o_mla, o_gdn, o_mem, P, x, tgt, w_out, g_gdn, g_fin, tm=256):
    T = x.shape[0]
    tm = min(tm, T)

    def body(om_ref, og_ref, oc_ref, gate_ref, x_ref, t_ref, w_ref, gg_ref, gf_ref, mix_ref, dx_ref, dxb_ref, sq_ref, gnf_ref):
        @pl.when(pl.program_id(0) == 0)
        def _():
            sq_ref[...] = jnp.zeros_like(sq_ref)
            gnf_ref[...] = jnp.zeros_like(gnf_ref)

        ogn, _, _ = _gdn_out_norm(og_ref[...].astype(F32), gg_ref[...])
        cat = jnp.concatenate([om_ref[...].astype(F32)] + ogn + [oc_ref[...].astype(F32)], axis=1)
        gt = gate_ref[...].astype(F32)
        mixed = (cat * (gt * _sigmoid(gt))).astype(BF16)
        mix_ref[...] = mixed
        x2 = x_ref[...] + _dot(mixed, w_ref[...], NN)
        r2 = lax.rsqrt(jnp.mean(x2 * x2, axis=-1, keepdims=True) + EPS)
        xh = x2 * r2
        gf = gf_ref[...]
        diff = xh * gf - t_ref[...]
        sq_ref[...] += jnp.sum(diff * diff, axis=0, keepdims=True)
        dy = diff * (1.0 / D_MODEL)
        gnf_ref[...] += jnp.sum(dy * xh, axis=0, keepdims=True)
        dxh = dy * gf
        dx = r2 * (dxh - xh * jnp.mean(dxh * xh, axis=-1, keepdims=True))
        dx_ref[...] = dx
        dxb_ref[...] = dx.astype(BF16)

    rowb = lambda c, j=0: BS((tm, c), lambda i: (i, j))
    full = lambda r, c: BS((r, c), lambda i: (0, 0))
    return pl.pallas_call(
        body, name="merge_fwd", grid=(T // tm,),
        in_specs=[rowb(512), rowb(512), rowb(512), rowb(D_MIX, OFF_GATE // D_MIX), rowb(D_MODEL), rowb(D_MODEL),
                  full(D_MIX, D_MODEL), full(1, 128), full(1, D_MODEL)],
        out_specs=[rowb(D_MIX), rowb(D_MODEL), rowb(D_MODEL), full(1, D_MODEL), full(1, D_MODEL)],
        out_shape=[jax.ShapeDtypeStruct((T, D_MIX), BF16), jax.ShapeDtypeStruct((T, D_MODEL), F32),
                   jax.ShapeDtypeStruct((T, D_MODEL), BF16),
                   jax.ShapeDtypeStruct((1, D_MODEL), F32), jax.ShapeDtypeStruct((1, D_MODEL), F32)],
        compiler_params=_arb(1))(o_mla, o_gdn, o_mem, P, x, tgt, w_out, g_gdn, g_fin)


def merge_bwd(dx2, o_mla, o_gdn, o_mem, P, w_out, g_gdn, tm=256):
    T = dx2.shape[0]
    tm = min(tm, T)

    def body(dx_ref, om_ref, og_ref, oc_ref, gate_ref, w_ref, gg_ref, dgate_ref, dom_ref, dog_ref, doc_ref, ggn_ref):
        @pl.when(pl.program_id(0) == 0)
        def _():
            ggn_ref[...] = jnp.zeros_like(ggn_ref)

        gg = gg_ref[...]
        dmix = _dot(dx_ref[...].astype(BF16), w_ref[...], NT)
        ogn, xhs, rs = _gdn_out_norm(og_ref[...].astype(F32), gg)
        cat = jnp.concatenate([om_ref[...].astype(F32)] + ogn + [oc_ref[...].astype(F32)], axis=1)
        gt = gate_ref[...].astype(F32)
        sg = _sigmoid(gt)
        dgate_ref[...] = (dmix * cat * (sg * (1.0 + gt * (1.0 - sg)))).astype(BF16)
        dcat = dmix * (gt * sg)
        dom_ref[...] = dcat[:, :512].astype(BF16)
        doc_ref[...] = dcat[:, 1024:].astype(BF16)
        acc = jnp.zeros((1, 128), F32)
        for h in range(N_HEADS):
            dseg = dcat[:, 512 + h * 128:512 + (h + 1) * 128]
            acc = acc + jnp.sum(dseg * xhs[h], axis=0, keepdims=True)
            dxh = dseg * gg
            dog_ref[:, h * 128:(h + 1) * 128] = (rs[h] * (dxh - xhs[h] * jnp.mean(dxh * xhs[h], axis=-1, keepdims=True))).astype(BF16)
        ggn_ref[...] += acc

    rowb = lambda c, j=0: BS((tm, c), lambda i: (i, j))
    full = lambda r, c: BS((r, c), lambda i: (0, 0))
    return pl.pallas_call(
        body, name="merge_bwd", grid=(T // tm,),
        in_specs=[rowb(D_MODEL), rowb(512), rowb(512), rowb(512), rowb(D_MIX, OFF_GATE // D_MIX),
                  full(D_MIX, D_MODEL), full(1, 128)],
        out_specs=[rowb(D_MIX), rowb(512), rowb(512), rowb(512), full(1, 128)],
        out_shape=[jax.ShapeDtypeStruct((T, D_MIX), BF16)] + [jax.ShapeDtypeStruct((T, 512), BF16)] * 3
        + [jax.ShapeDtypeStruct((1, 128), F32)],
        compiler_params=_arb(1))(dx2, o_mla, o_gdn, o_mem, P, w_out, g_gdn)


def in_proj_bwd(dP, wp, x, dx2, gain, after, tm=512):
    T, n = x.shape
    tm = min(tm, T)
    k = len(dP)
    widths = [p.shape[1] for p in dP]
    offs = [sum(widths[:i]) for i in range(k)]

    def body(*refs):
        w_ref, x_ref, dx2_ref, g_ref = refs[k:k + 4]
        o_ref, acc_ref = refs[-2:]

        @pl.when(pl.program_id(0) == 0)
        def _():
            acc_ref[...] = jnp.zeros_like(acc_ref)

        dy = None
        for a_ref, off, w in zip(refs[:k], offs, widths):
            d = _dot(a_ref[...], w_ref[off:off + w, :], NN)
            dy = d if dy is None else dy + d
        xv = x_ref[...]
        r = lax.rsqrt(jnp.mean(xv * xv, axis=-1, keepdims=True) + EPS)
        xh = xv * r
        acc_ref[...] += jnp.sum(dy * xh, axis=0, keepdims=True)
        dxh = dy * g_ref[...]
        o_ref[...] = dx2_ref[...] + r * (dxh - xh * jnp.mean(dxh * xh, axis=-1, keepdims=True))

    rowb = BS((tm, n), lambda i: (i, 0))
    full = BS((1, n), lambda i: (0, 0))
    return pl.pallas_call(
        body, name="in_proj_bwd", grid=(T // tm,),
        in_specs=[BS((tm, w), lambda i: (i, 0)) for w in widths]
        + [BS(wp.shape, lambda i: (0, 0), pipeline_mode=pl.Buffered(1)), rowb, rowb, full, BS(memory_space=pl.ANY)],
        out_specs=[rowb, full], out_shape=[jax.ShapeDtypeStruct((T, n), F32), jax.ShapeDtypeStruct((1, n), F32)],
        compiler_params=_arb(1))(*dP, wp, x, dx2, gain, after)


W_IN_SHARD = D_IN // 4
_GDN0 = Q_LORA + KV_LORA + MLA_ROPE
_AB0 = _GDN0 + GDN_QKV
_MEMQ0 = _AB0 + 2 * N_HEADS
_GATE0 = _MEMQ0 + N_HEADS * MEM_DH


def _w_in_row_map():
    a, m, gt = _AB0 - 2 * W_IN_SHARD, _MEMQ0 - 2 * W_IN_SHARD, _GATE0 - 2 * W_IN_SHARD
    e0 = OFF_GDN + W_IN_SHARD - _GDN0
    e1 = e0 + W_IN_SHARD
    e2 = OFF_GATE + W_IN_SHARD - gt
    return [(0, 0, 0, 672), (0, 672, 704, 32), (2, a, 768, m - a), (2, m, OFF_MEMQ, gt - m), (0, _GDN0, OFF_GDN, W_IN_SHARD - _GDN0),
            (1, 0, e0, W_IN_SHARD), (2, 0, e1, a), (2, gt, OFF_GATE, W_IN_SHARD - gt), (3, 0, e2, W_IN_SHARD)]


_W_IN_ZERO_ROWS = [(672, 32), (736, 32), (776, 248)]
W_IN_LANES = 256


def pad_w_in_t(shards):
    per_half = shards.shape[3] // W_IN_LANES

    def body(s_ref, o_ref):
        for r0, n in _W_IN_ZERO_ROWS:
            o_ref[r0:r0 + n, :] = jnp.zeros((n, W_IN_LANES), o_ref.dtype)
        for q, src, dst, n in _w_in_row_map():
            o_ref[dst:dst + n, :] = s_ref[q, src:src + n, :]

    return pl.pallas_call(
        body, name="pad_w_in_t", grid=(D_MODEL // W_IN_LANES,),
        in_specs=[BS((N_CHIPS, None, W_IN_SHARD, W_IN_LANES), lambda j: (0, j // per_half, 0, j % per_half))],
        out_specs=BS((N_PAD, W_IN_LANES), lambda j: (0, j)),
        out_shape=jax.ShapeDtypeStruct((N_PAD, D_MODEL), shards.dtype), compiler_params=_arb(1))(shards)


def unpad_w_in_t(g):
    def body(g_ref, o_ref):
        for q, src, dst, n in _w_in_row_map():
            o_ref[q, src:src + n, :] = g_ref[dst:dst + n, :]

    return pl.pallas_call(
        body, name="unpad_w_in_t", grid=(D_MODEL // W_IN_LANES,),
        in_specs=[BS((N_PAD, W_IN_LANES), lambda j: (0, j))], out_specs=BS((N_CHIPS, W_IN_SHARD, W_IN_LANES), lambda j: (0, 0, j)),
        out_shape=jax.ShapeDtypeStruct((N_CHIPS, W_IN_SHARD, D_MODEL), g.dtype), compiler_params=_arb(1))(g)


def _perm_w_kv_b(s):
    return jnp.concatenate([s[h, :, :128] for h in range(N_HEADS)] + [s[h, :, 128:] for h in range(N_HEADS)], axis=1)


def _unperm_w_kv_b(g):
    return jnp.stack([jnp.concatenate([g[:, h * 128:(h + 1) * 128], g[:, 512 + h * 128:512 + (h + 1) * 128]], axis=1)
                      for h in range(N_HEADS)])


def _lane_row(v4):
    return jnp.pad(v4.reshape(1, -1).astype(F32), ((0, 0), (0, 128 - v4.size)))


N_CHIPS = 4
MESH = pl.DeviceIdType.MESH
ANY = BS(memory_space=pl.ANY)


def _place():
    return lax.axis_index("x"), lax.axis_index("y"), lax.axis_index("c")


def _other_chips(x, y):
    return [(1 - x, y), (x, 1 - y), (1 - x, 1 - y)]


def _half(split, which):
    axis, size = split
    ds = pl.ds(pl.multiple_of(which * size, 16 if axis == 0 else 128), size)
    return (ds, slice(None)) if axis == 0 else (slice(None), ds)


SEM = BS(memory_space=pltpu.SEMAPHORE)
HBM = BS(memory_space=pltpu.HBM)
_IN_HBM = lambda a: pltpu.with_memory_space_constraint(a, pltpu.HBM)
_SIDE_EFFECT = pltpu.SideEffectType.DATAFLOW_SIDE_EFFECTING


def _late_gather_copies(s_refs, l_refs, send_sems, recv_sems, local_sems, with_arrivals):
    x, y, c = _place()
    sends, recvs, locals_ = [], [], []
    for i, (s_ref, l_ref) in enumerate(zip(s_refs, l_refs)):
        locals_.append(pltpu.make_async_copy(s_ref, l_ref.at[2 * x + y], local_sems.at[i]))
        for j, (px, py) in enumerate(_other_chips(x, y)):
            k = 3 * i + j
            sends.append(pltpu.make_async_remote_copy(src_ref=s_ref, dst_ref=l_ref.at[2 * x + y], send_sem=send_sems.at[k],
                                                      recv_sem=recv_sems.at[k], device_id=(px, py, c), device_id_type=MESH))
            if with_arrivals:
                recvs.append(pltpu.make_async_remote_copy(src_ref=s_ref, dst_ref=l_ref.at[2 * px + py], send_sem=send_sems.at[k],
                                                          recv_sem=recv_sems.at[k], device_id=(px, py, c), device_id_type=MESH))
    return sends, recvs, locals_


def late_gather_start(shards, after, name):
    n = len(shards)

    def body(*refs):
        s_refs, l_refs = refs[:n], refs[n:2 * n]
        send_sems, recv_sems, local_sems = refs[2 * n + 1:2 * n + 4]
        token = refs[-1]
        sends, _, locals_ = _late_gather_copies(s_refs, l_refs, send_sems, recv_sems, local_sems, False)
        for cp in locals_ + sends:
            cp.start()
        token[...] = jnp.zeros_like(token)

    lands = [lax.empty((N_CHIPS,) + s.shape, s.dtype) for s in shards]
    hbm_like = lambda a: pltpu.HBM(a.shape, a.dtype)
    out = pl.pallas_call(
        body, name=name,
        out_shape=[pltpu.SemaphoreType.DMA((3 * n,)), pltpu.SemaphoreType.DMA((3 * n,)), pltpu.SemaphoreType.DMA((n,))]
        + [hbm_like(s) for s in shards] + [hbm_like(l) for l in lands] + [jax.ShapeDtypeStruct((8, 128), F32)],
        in_specs=[HBM] * (2 * n) + [BS(memory_space=pl.ANY)], out_specs=[SEM] * 3 + [HBM] * (2 * n) + [BS(memory_space=pltpu.VMEM)],
        input_output_aliases={i: 3 + i for i in range(2 * n)},
        compiler_params=pltpu.CompilerParams(has_side_effects=_SIDE_EFFECT))(
            *[_IN_HBM(s) for s in shards], *[_IN_HBM(l) for l in lands], after)
    return out[:3], out[3:3 + n], out[3 + n:3 + 2 * n], out[-1]


def late_gather_wait(sems, shards, lands, after, name):
    n = len(shards)

    def body(*refs):
        s_refs, l_refs = refs[:n], refs[n:2 * n]
        send_sems, recv_sems, local_sems = refs[2 * n:2 * n + 3]
        sends, recvs, locals_ = _late_gather_copies(s_refs, l_refs, send_sems, recv_sems, local_sems, True)
        for cp in locals_:
            cp.wait()
        for cp in sends:
            cp.wait_send()
        for cp in recvs:
            cp.wait_recv()

    hbm_like = lambda a: pltpu.HBM(a.shape, a.dtype)
    out = pl.pallas_call(
        body, name=name, out_shape=[hbm_like(s) for s in shards] + [hbm_like(l) for l in lands],
        in_specs=[HBM] * (2 * n) + [SEM] * 3 + [BS(memory_space=pl.ANY)], out_specs=[HBM] * (2 * n),
        input_output_aliases={i: i for i in range(2 * n)},
        compiler_params=pltpu.CompilerParams(has_side_effects=_SIDE_EFFECT))(*shards, *lands, *sems, after)
    return out[n:]


def _half_gather_copies(s_ref, l_ref, send_sems, recv_sems, with_arrivals):
    x, y, c = _place()
    sends, recvs = [], []
    for j, (px, py) in enumerate(_other_chips(x, y)):
        sends.append(pltpu.make_async_remote_copy(src_ref=s_ref.at[c], dst_ref=l_ref.at[2 * x + y, c], send_sem=send_sems.at[j],
                                                  recv_sem=recv_sems.at[j], device_id=(px, py, c), device_id_type=MESH))
        if with_arrivals:
            recvs.append(pltpu.make_async_remote_copy(src_ref=s_ref.at[c], dst_ref=l_ref.at[2 * px + py, c], send_sem=send_sems.at[j],
                                                      recv_sem=recv_sems.at[j], device_id=(px, py, c), device_id_type=MESH))
    return sends, recvs


def half_gather_start(shard, name):
    def body(s_ref, l_ref, send_sems, recv_sems, local_sem, s_thru, l_thru, token):
        x, y, _ = _place()
        pltpu.make_async_copy(s_ref, l_ref.at[2 * x + y], local_sem.at[0]).start()
        for cp in _half_gather_copies(s_ref, l_ref, send_sems, recv_sems, False)[0]:
            cp.start()
        token[...] = jnp.zeros_like(token)

    land = lax.empty((N_CHIPS,) + shard.shape, shard.dtype)
    out = pl.pallas_call(
        body, name=name,
        out_shape=[pltpu.SemaphoreType.DMA((3,)), pltpu.SemaphoreType.DMA((3,)), pltpu.SemaphoreType.DMA((1,)),
                   pltpu.HBM(shard.shape, shard.dtype), pltpu.HBM(land.shape, land.dtype), jax.ShapeDtypeStruct((8, 128), F32)],
        in_specs=[HBM, HBM], out_specs=[SEM] * 3 + [HBM, HBM, BS(memory_space=pltpu.VMEM)],
        input_output_aliases={0: 3, 1: 4},
        compiler_params=pltpu.CompilerParams(has_side_effects=_SIDE_EFFECT))(_IN_HBM(shard), _IN_HBM(land))
    return out[:3], out[3], out[4], out[5]


def half_gather_wait(sems, shard, land, after, name):
    def body(s_ref, l_ref, send_sems, recv_sems, local_sem, *rest):
        x, y, _ = _place()
        pltpu.make_async_copy(s_ref, l_ref.at[2 * x + y], local_sem.at[0]).wait()
        sends, recvs = _half_gather_copies(s_ref, l_ref, send_sems, recv_sems, True)
        for cp in sends:
            cp.wait_send()
        for cp in recvs:
            cp.wait_recv()

    out = pl.pallas_call(
        body, name=name, out_shape=[pltpu.HBM(shard.shape, shard.dtype), pltpu.HBM(land.shape, land.dtype)],
        in_specs=[HBM, HBM] + [SEM] * 3 + [BS(memory_space=pl.ANY)] * len(after), out_specs=[HBM, HBM],
        input_output_aliases={0: 0, 1: 1},
        compiler_params=pltpu.CompilerParams(has_side_effects=_SIDE_EFFECT))(shard, land, *sems, *after)
    return out[1]


def pass_halves_to_sibling(land, name):
    def body(l_in, l_ref, send_sems, recv_sems):
        x, y, c = _place()
        copies = []
        for j, (px, py) in enumerate(_other_chips(x, y)):
            q = 2 * px + py
            give = pltpu.make_async_remote_copy(src_ref=l_ref.at[q, c], dst_ref=l_ref.at[q, c], send_sem=send_sems.at[j],
                                                recv_sem=recv_sems.at[j], device_id=(x, y, 1 - c), device_id_type=MESH)
            take = pltpu.make_async_remote_copy(src_ref=l_ref.at[q, c], dst_ref=l_ref.at[q, 1 - c], send_sem=send_sems.at[j],
                                                recv_sem=recv_sems.at[j], device_id=(x, y, 1 - c), device_id_type=MESH)
            give.start()
            copies.append((give, take))
        for give, take in copies:
            take.wait_recv()
            give.wait_send()

    return pl.pallas_call(
        body, name=name, in_specs=[ANY], out_specs=ANY, out_shape=jax.ShapeDtypeStruct(land.shape, land.dtype),
        input_output_aliases={0: 0},
        scratch_shapes=[pltpu.SemaphoreType.DMA((3,)), pltpu.SemaphoreType.DMA((3,))])(land)


def allgather_devices(block, name):
    R, C = block.shape

    def body(b_ref, o_ref, send_sems, recv_sems, local_sem):
        x, y, c = _place()
        me = 4 * x + 2 * y + c
        own = pltpu.make_async_copy(b_ref, o_ref.at[me], local_sem)
        own.start()
        copies = []
        for r in range(1, 8):
            px = 1 - x if r & 4 else x
            py = 1 - y if r & 2 else y
            pc = 1 - c if r & 1 else c
            send = pltpu.make_async_remote_copy(src_ref=b_ref, dst_ref=o_ref.at[me], send_sem=send_sems.at[r - 1],
                                                recv_sem=recv_sems.at[r - 1], device_id=(px, py, pc), device_id_type=MESH)
            recv = pltpu.make_async_remote_copy(src_ref=b_ref, dst_ref=o_ref.at[4 * px + 2 * py + pc], send_sem=send_sems.at[r - 1],
                                                recv_sem=recv_sems.at[r - 1], device_id=(px, py, pc), device_id_type=MESH)
            send.start()
            copies.append((send, recv))
        for send, recv in copies:
            recv.wait_recv()
            send.wait_send()
        own.wait()

    return pl.pallas_call(
        body, name=name, in_specs=[ANY], out_specs=ANY, out_shape=jax.ShapeDtypeStruct((8, R, C), block.dtype),
        scratch_shapes=[pltpu.SemaphoreType.DMA((7,)), pltpu.SemaphoreType.DMA((7,)), pltpu.SemaphoreType.DMA(())])(block)


def swap_sibling(arrs, name, splits=None):
    n = len(arrs)

    def sent(a_ref, i, c):
        return a_ref if splits is None else a_ref.at[(slice(None),) + _half(splits[i], 1 - c)]

    def out_shape(a, i):
        if splits is None:
            return a.shape
        axis, size = splits[i]
        return (a.shape[0], size, a.shape[2]) if axis == 0 else (a.shape[0], a.shape[1], size)

    def body(*refs):
        a_refs, o_refs = refs[:n], refs[n:2 * n]
        send_sems, recv_sems = refs[2 * n:]
        x, y, c = _place()
        copies = [pltpu.make_async_remote_copy(src_ref=sent(a_ref, i, c), dst_ref=o_ref, send_sem=send_sems.at[i],
                                               recv_sem=recv_sems.at[i], device_id=(x, y, 1 - c), device_id_type=MESH)
                  for i, (a_ref, o_ref) in enumerate(zip(a_refs, o_refs))]
        for cp in copies:
            cp.start()
        for cp in copies:
            cp.wait()

    return pl.pallas_call(
        body, name=name, in_specs=[ANY] * n, out_specs=[ANY] * n,
        out_shape=[jax.ShapeDtypeStruct(out_shape(a, i), a.dtype) for i, a in enumerate(arrs)],
        scratch_shapes=[pltpu.SemaphoreType.DMA((n,)), pltpu.SemaphoreType.DMA((n,))])(*arrs)


def _exchange_copies(p_refs, l_refs, send_sems, recv_sems):
    x, y, c = _place()
    return [pltpu.make_async_remote_copy(src_ref=p_ref.at[2 * px + py], dst_ref=l_ref.at[j], send_sem=send_sems.at[3 * i + j],
                                         recv_sem=recv_sems.at[3 * i + j], device_id=(px, py, c), device_id_type=MESH)
            for i, (p_ref, l_ref) in enumerate(zip(p_refs, l_refs)) for j, (px, py) in enumerate(_other_chips(x, y))]


def exchange_chips_start(parts, name):
    n = len(parts)

    def body(*refs):
        send_sems, recv_sems = refs[2 * n:2 * n + 2]
        for cp in _exchange_copies(refs[:n], refs[n:2 * n], send_sems, recv_sems):
            cp.start()
        refs[-1][...] = jnp.zeros_like(refs[-1])

    lands = [lax.empty((3,) + p.shape[1:], p.dtype) for p in parts]
    hbm_like = lambda a: pltpu.HBM(a.shape, a.dtype)
    out = pl.pallas_call(
        body, name=name,
        out_shape=[pltpu.SemaphoreType.DMA((3 * n,)), pltpu.SemaphoreType.DMA((3 * n,))]
        + [hbm_like(p) for p in parts] + [hbm_like(l) for l in lands] + [jax.ShapeDtypeStruct((8, 128), F32)],
        in_specs=[HBM] * (2 * n), out_specs=[SEM] * 2 + [HBM] * (2 * n) + [BS(memory_space=pltpu.VMEM)],
        input_output_aliases={i: 2 + i for i in range(2 * n)},
        compiler_params=pltpu.CompilerParams(has_side_effects=_SIDE_EFFECT))(*[_IN_HBM(p) for p in parts], *[_IN_HBM(l) for l in lands])
    return out[:2], out[2:2 + n], out[2 + n:2 + 2 * n], out[-1]


def exchange_chips_wait(sems, parts, lands, after, name):
    n = len(parts)

    def body(*refs):
        send_sems, recv_sems = refs[2 * n:2 * n + 2]
        for cp in _exchange_copies(refs[:n], refs[n:2 * n], send_sems, recv_sems):
            cp.wait_send()
            cp.wait_recv()

    hbm_like = lambda a: pltpu.HBM(a.shape, a.dtype)
    out = pl.pallas_call(
        body, name=name, out_shape=[hbm_like(p) for p in parts] + [hbm_like(l) for l in lands],
        in_specs=[HBM] * (2 * n) + [SEM] * 2 + [BS(memory_space=pl.ANY)], out_specs=[HBM] * (2 * n),
        input_output_aliases={i: i for i in range(2 * n)},
        compiler_params=pltpu.CompilerParams(has_side_effects=_SIDE_EFFECT))(*parts, *lands, *sems, after)
    return out[n:]


def _half_block(shape2, split):
    axis, size = split
    return (size, shape2[1]) if axis == 0 else (shape2[0], size)


def add_pairs(parts, halves, splits, core, name):
    n = len(parts)

    def body(s_ref, *refs):
        for a_ref, b_ref, o_ref in zip(refs[:n], refs[n:2 * n], refs[2 * n:]):
            o_ref[...] = (a_ref[...].astype(F32) + b_ref[...].astype(F32)).astype(BF16)

    def mine(i):
        blk = (None,) + _half_block(parts[i].shape[1:], splits[i])
        if splits[i][0] == 0:
            return BS(blk, lambda q, s: (q, s[0], 0))
        return BS(blk, lambda q, s: (q, 0, s[0]))

    half_specs = [BS((None,) + h.shape[1:], lambda q, s: (q, 0, 0)) for h in halves]
    return pl.pallas_call(
        body, name=name,
        grid_spec=pltpu.PrefetchScalarGridSpec(num_scalar_prefetch=1, grid=(N_CHIPS,),
                                               in_specs=[mine(i) for i in range(n)] + half_specs, out_specs=half_specs),
        out_shape=[jax.ShapeDtypeStruct(h.shape, BF16) for h in halves], compiler_params=_arb(1))(core, *parts, *halves)


def add_fives(parts, halves, from_chips, splits, chip_core, name):
    n = len(parts)

    def body(s_ref, *refs):
        for a_ref, b_ref, p_ref, o_ref in zip(refs[:n], refs[n:2 * n], refs[2 * n:3 * n], refs[3 * n:]):
            s = a_ref[...].astype(F32) + b_ref[...].astype(F32)
            for j in range(3):
                s = s + p_ref[j].astype(F32)
            o_ref[...] = s

    def mine(i):
        blk = (None,) + _half_block(parts[i].shape[1:], splits[i])
        if splits[i][0] == 0:
            return BS(blk, lambda g, s: (s[0], s[1], 0))
        return BS(blk, lambda g, s: (s[0], 0, s[1]))

    half_specs = [BS((None,) + h.shape[1:], lambda g, s: (s[0], 0, 0)) for h in halves]
    chip_specs = [BS(p.shape, lambda g, s: (0, 0, 0)) for p in from_chips]
    out_specs = [BS(h.shape[1:], lambda g, s: (0, 0)) for h in halves]
    return pl.pallas_call(
        body, name=name,
        grid_spec=pltpu.PrefetchScalarGridSpec(num_scalar_prefetch=1, grid=(1,),
                                               in_specs=[mine(i) for i in range(n)] + half_specs + chip_specs, out_specs=out_specs),
        out_shape=[jax.ShapeDtypeStruct(h.shape[1:], F32) for h in halves], compiler_params=_arb(1))(chip_core, *parts, *halves, *from_chips)


def sum_leading(a, name):
    def body(a_ref, o_ref):
        s = a_ref[0]
        for j in range(1, a.shape[0]):
            s = s + a_ref[j]
        o_ref[...] = s

    return pl.pallas_call(body, name=name, out_shape=jax.ShapeDtypeStruct(a.shape[1:], a.dtype))(a)


def _adamw_math(w, g, m, v):
    mn = ADAM_B1 * m + (1.0 - ADAM_B1) * g
    vn = ADAM_B2 * v + (1.0 - ADAM_B2) * (g * g)
    m_hat = mn / (1.0 - ADAM_B1 ** ADAM_STEP)
    v_hat = vn / (1.0 - ADAM_B2 ** ADAM_STEP)
    return -ADAM_LR * (m_hat / (jnp.sqrt(v_hat) + ADAM_EPS) + ADAM_WD * w), mn, vn


def adamw(w, g, m, v, name):
    R, C = g.shape
    lead = (None,) * (w.ndim - 2)

    def body(w_ref, g_ref, m_ref, v_ref, d_ref, mo_ref, vo_ref):
        d_ref[...], mo_ref[...], vo_ref[...] = _adamw_math(w_ref[...], g_ref[...], m_ref[...], v_ref[...])

    wblk = BS(lead + (R, C), lambda i: (0,) * w.ndim)
    gblk = BS((R, C), lambda i: (0, 0))
    return pl.pallas_call(
        body, name=name, grid=(1,), in_specs=[wblk, gblk, wblk, wblk], out_specs=[wblk] * 3,
        out_shape=[jax.ShapeDtypeStruct(w.shape, F32)] * 3, compiler_params=_arb(1))(w, g, m, v)


SMALL_ROWS = 16
CONV_ROW0 = 8
LOSS_ROW = 14


def pack_small(small_grads, g_ab, g_conv, sq):
    present = [a for a in small_grads if a is not None]

    def body(*refs):
        ab_ref, conv_ref, sq_ref, o_ref = refs[len(present):]
        o_ref[...] = jnp.zeros_like(o_ref)
        it = iter(refs[:len(present)])
        for i, a in enumerate(small_grads):
            if a is not None:
                o_ref[i:i + 1, 0:a.shape[1]] = next(it)[...]
        o_ref[3:4, 0:128] = ab_ref[0:1, :]
        o_ref[4:5, 0:128] = ab_ref[1:2, :]
        half = 512
        for k in range(GDN_CONV * GDN_QKV // half):
            src_r, src_c = (k * half) // GDN_QKV, (k * half) % GDN_QKV
            dst_r, dst_c = CONV_ROW0 + (k * half) // 1024, (k * half) % 1024
            o_ref[dst_r:dst_r + 1, dst_c:dst_c + half] = conv_ref[src_r:src_r + 1, src_c:src_c + half]
        o_ref[LOSS_ROW:LOSS_ROW + 1, :] = sq_ref[...]

    return pl.pallas_call(body, name="pack_small", out_shape=jax.ShapeDtypeStruct((SMALL_ROWS, 1024), F32))(
        *present, g_ab, g_conv, sq)


def adamw_small(block, ws, ms, vs):
    k = len(ws)

    def body(b_ref, *refs):
        outs = refs[3 * k:]
        for i in range(k):
            n = ws[i].shape[1]
            g = b_ref[i:i + 1, 0:n]
            d, mn, vn = _adamw_math(refs[i][...], g, refs[k + i][...], refs[2 * k + i][...])
            outs[4 * i][...], outs[4 * i + 1][...], outs[4 * i + 2][...], outs[4 * i + 3][...] = g, d, mn, vn

    out = pl.pallas_call(
        body, name="adamw_small",
        out_shape=[jax.ShapeDtypeStruct(w.shape, F32) for w in ws for _ in range(4)])(block, *ws, *ms, *vs)
    return [out[4 * i:4 * i + 4] for i in range(k)]


def adamw_halves(w, mine, other, m, v, split, core, name):
    R, C = w.shape[-2:]
    axis, size = split
    lead = (None,) * (w.ndim - 2)
    zeros = (0,) * (w.ndim - 2)
    if axis == 0:
        tr = size if size <= 256 else next(t for t in range(256, 7, -1) if size % t == 0 and t % 8 == 0)
        nb = size // tr
        whole = BS(lead + (tr, C), lambda hi, j, s: zeros + (hi * nb + j, 0))
        part = BS((tr, C), lambda hi, j, s: (j, 0))
    else:
        nb = size // 128
        whole = BS(lead + (R, 128), lambda hi, j, s: zeros + (0, hi * nb + j))
        part = BS((R, 128), lambda hi, j, s: (0, j))

    def body(s_ref, w_ref, a_ref, b_ref, m_ref, v_ref, g_ref, d_ref, mo_ref, vo_ref):
        g = jnp.where(pl.program_id(0) == s_ref[0], a_ref[...], b_ref[...])
        g_ref[...] = g
        d_ref[...], mo_ref[...], vo_ref[...] = _adamw_math(w_ref[...], g, m_ref[...], v_ref[...])

    return pl.pallas_call(
        body, name=name,
        grid_spec=pltpu.PrefetchScalarGridSpec(num_scalar_prefetch=1, grid=(2, nb),
                                               in_specs=[whole, part, part, whole, whole], out_specs=[whole] * 4),
        out_shape=[jax.ShapeDtypeStruct(w.shape, F32)] * 4, compiler_params=_arb(2))(core, w, mine, other, m, v)


def dense_bf16(w3, name):
    R, _, K = w3.shape
    kh = K // 2

    def body(w_hbm, o_ref, buf, sem):
        cp = pltpu.make_async_copy(w_hbm.at[:, 0], buf, sem)
        cp.start()
        cp.wait()
        o_ref[0] = buf[:, :kh].astype(BF16)
        o_ref[1] = buf[:, kh:].astype(BF16)

    return pl.pallas_call(
        body, name=name, in_specs=[ANY], out_specs=BS(memory_space=pltpu.VMEM), out_shape=jax.ShapeDtypeStruct((2, R, kh), BF16),
        scratch_shapes=[pltpu.VMEM((R, K), F32), pltpu.SemaphoreType.DMA(())])(w3)


ROW_BLOCK = 184


def adamw_untiled_rows(w3, mine, other, m3, v3, name):
    R, _, K = w3.shape
    kh = K // 2
    starts = list(range(0, R, ROW_BLOCK))
    sizes = [min(ROW_BLOCK, R - s) for s in starts]
    nblk = len(starts)

    def body(w_hbm, a_ref, b_ref, m_hbm, v_hbm, g_hbm, d_hbm, mo_hbm, vo_hbm,
             wbuf, mbuf, vbuf, gbuf, dbuf, mobuf, vobuf, in_sems, out_sems):
        first = lax.axis_index("c") == 0
        ins = []
        for k, (r0, n) in enumerate(zip(starts, sizes)):
            rows = pl.ds(r0, n)
            cps = [pltpu.make_async_copy(src.at[rows, 0], dst.at[rows], in_sems.at[3 * k + i])
                   for i, (src, dst) in enumerate(((w_hbm, wbuf), (m_hbm, mbuf), (v_hbm, vbuf)))]
            for cp in cps:
                cp.start()
            ins.append(cps)

        def update(rows):
            a, b = a_ref[rows, :], b_ref[rows, :]
            g = jnp.concatenate([jnp.where(first, a, b), jnp.where(first, b, a)], axis=1)
            gbuf[rows, :] = g
            dbuf[rows, :], mobuf[rows, :], vobuf[rows, :] = _adamw_math(wbuf[rows, :], g, mbuf[rows, :], vbuf[rows, :])

        outs = []
        for k, (r0, n) in enumerate(zip(starts, sizes)):
            for cp in ins[k]:
                cp.wait()
            groups, tail = n // 8, n % 8

            def group(i, carry, r0=r0):
                update(pl.ds(pl.multiple_of(r0 + i * 8, 8), 8))
                return carry

            lax.fori_loop(0, groups, group, 0)
            if tail:
                update(pl.ds(r0 + groups * 8, tail))
            rows = pl.ds(r0, n)
            cps = [pltpu.make_async_copy(src.at[rows], dst.at[rows, 0], out_sems.at[4 * k + i])
                   for i, (src, dst) in enumerate(((gbuf, g_hbm), (dbuf, d_hbm), (mobuf, mo_hbm), (vobuf, vo_hbm)))]
            for cp in cps:
                cp.start()
            outs += cps
        for cp in outs:
            cp.wait()

    vmem = BS(memory_space=pltpu.VMEM)
    return pl.pallas_call(
        body, name=name, in_specs=[ANY, vmem, vmem, ANY, ANY], out_specs=[ANY] * 4,
        out_shape=[jax.ShapeDtypeStruct(w3.shape, F32)] * 4,
        scratch_shapes=[pltpu.VMEM((R, K), F32)] * 7 + [pltpu.SemaphoreType.DMA((3 * nblk,)), pltpu.SemaphoreType.DMA((4 * nblk,))])(
            w3, mine, other, m3, v3)


def adamw_w_q_b(w, mine, other, m, v, name):
    def body(w_ref, a_ref, b_ref, m_ref, v_ref, g_ref, d_ref, mo_ref, vo_ref):
        first = lax.axis_index("c") == 0
        lo = jnp.where(first, a_ref[...], b_ref[...])
        hi = jnp.where(first, b_ref[...], a_ref[...])
        g = jnp.concatenate([lo, hi[0:32], hi[64:96]], axis=0)
        g_ref[...] = g
        d_ref[...], mo_ref[...], vo_ref[...] = _adamw_math(w_ref[...], g, m_ref[...], v_ref[...])

    return pl.pallas_call(body, name=name, out_shape=[jax.ShapeDtypeStruct(w.shape, F32)] * 4)(w, mine, other, m, v)


def local_step(x, mem, positions, tgt, norm_in, weights, big_grads_ready, q_a_norm, kv_a_norm, gdn_conv, gdn_a_log,
               gdn_dt_bias, gdn_norm, mem_norm, norm_final):
    B, S, D = x.shape
    M = mem.shape[1]
    T = B * S
    N = S // CHUNK
    x2d = x.reshape(T, D)
    mem2d = mem.reshape(B * M, D)
    tgt2d = tgt.reshape(T, D)

    alog_row, dt_row = _lane_row(gdn_a_log), _lane_row(gdn_dt_bias)

    half = MLA_ROPE // 2
    inv_freq = 1.0 / (ROPE_THETA ** (jnp.arange(half, dtype=F32) / half))
    z32 = jnp.zeros((half,), F32)
    o32 = jnp.ones((half,), F32)
    inv_row = jnp.concatenate([inv_freq, z32, inv_freq, z32]).reshape(1, 128)
    sgn_row = jnp.concatenate([-o32, z32, o32, z32]).reshape(1, 128)
    msk_row = jnp.concatenate([o32, z32, o32, z32]).reshape(1, 128)
    cos_t, sin_t = rope_tables(positions.reshape(T, 1), inv_row, sgn_row, msk_row, after=weights[3])

    h = rms_fwd(x2d, norm_in, "rms_in", after=weights[3])
    wp, behind = weights[0]((h, cos_t))
    P = mm(h, wp, "nt", BF16, "in_proj", bm=512, bn=1536, n_outer=True, after=behind)
    wq, wkv = weights[1](P)
    Q, K, V, qn, kvn = mla_prep(P, q_a_norm, kv_a_norm, wq, wkv, cos_t, sin_t)
    o_mla, lse = mla_attn_fwd(Q, K, V, B, S)
    qkv = gdn_prep_fwd(P, gdn_conv, B, S)
    GB = gdn_gate_fwd(P, alog_row, dt_row, B, S)
    Grow = jnp.transpose(GB[:, :N_HEADS].reshape(B, N, CHUNK, N_HEADS), (0, 3, 1, 2))
    U, W, Tinv, A = gdn_chunk_fwd(qkv, GB, Grow, B, S)
    qkv3, GB3 = qkv.reshape(B, S, GDN_QKV), GB.reshape(B, S, 128)
    W3 = W.reshape(B, S, 512)
    o_gdn3, Vn3, St = gdn_scan_fwd(qkv3, U.reshape(B, S, 512), W3, GB3, A, B, S)
    o_gdn = o_gdn3.reshape(T, 512)
    w_mem_kv, w_out = weights[2](o_gdn)
    memn = rms_fwd(mem2d, mem_norm, "rms_mem")
    MKV = mm(memn, w_mem_kv, "nn", BF16, "mem_kv_proj")
    o_mem = mem_attn_fwd(P, MKV, B, S, M)
    mixed, dx2, dx2b, sq, g_norm_final = merge_fwd(o_mla, o_gdn, o_mem, P, x2d, tgt2d, w_out, gdn_norm, norm_final.reshape(1, D))

    g_w_out = mm(mixed, dx2b, "tn", BF16, "grad_w_out")
    dgate, do_mla, do_gdn, do_mem, g_gdn_norm = merge_bwd(dx2b, o_mla, o_gdn, o_mem, P, w_out, gdn_norm)

    dmemq, dMKV = mem_attn_bwd(P, MKV, do_mem, B, S, M)
    g_w_mem_kv = mm(memn, dMKV, "tn", BF16, "grad_w_mem_kv")
    started_early = big_grads_ready(dict(w_mem_kv=g_w_mem_kv, w_out=g_w_out), "early")
    dmemn = mm(dMKV, w_mem_kv, "nt", F32, "d_memn", after=(started_early,))
    g_mem_norm = gain_grad(mem2d, dmemn, "grad_mem_norm")

    dU3, dW3, dQ13, dK13, dA, dG13 = gdn_scan_bwd(do_gdn.reshape(B, S, 512), qkv3, W3, Vn3, GB3, A, St, B, S)
    r2 = lambda a: a.reshape(T, a.shape[-1])
    dqkv, dGB = gdn_chunk_bwd(qkv, GB, Grow, Tinv, dA, r2(dU3), r2(dW3), r2(dQ13), r2(dK13), r2(dG13), B, S)
    dPg, g_conv = gdn_prep_bwd(P, dqkv, gdn_conv, B, S)
    dab, g_ab = gdn_gate_bwd(P, dGB, alog_row, dt_row, B, S)

    dQ, dK, dV = mla_attn_bwd(Q, K, V, o_mla, do_mla, lse, B, S)
    dq_lin, dkv_lin, dPm, g_q_a_norm, g_kv_a_norm = mla_proj_bwd(dQ, dK, dV, cos_t, sin_t, P, dab, wq, wkv, q_a_norm, kv_a_norm)
    g_wq = mm(dq_lin, qn, "tn", BF16, "grad_w_q_b")
    g_wkv = mm(kvn, dkv_lin, "tn", BF16, "grad_w_kv_b")

    dP = [dPm, dmemq, dPg, dgate]
    g_wp = mm_cols_tn(dP, h, BF16, "grad_w_in")
    started = big_grads_ready(dict(w_in=g_wp, w_q_b=g_wq, w_kv_b=g_wkv), "late")
    grad_x, g_norm_in = in_proj_bwd(dP, wp, x2d, dx2, norm_in, started)

    grads = dict(
        norm_in=g_norm_in, q_a_norm=g_q_a_norm, kv_a_norm=g_kv_a_norm, gdn_conv=g_conv,
        gdn_a_log_dt_bias=g_ab, gdn_norm=g_gdn_norm,
        mem_norm=g_mem_norm, norm_final=g_norm_final)
    return sq, grad_x.reshape(B, S, D), grads


def kernel(x, mem, positions, norm_in, w_in, q_a_norm, w_q_b, kv_a_norm, w_kv_b, gdn_conv, gdn_a_log, gdn_dt_bias, gdn_norm, mem_norm, w_mem_kv, w_out, norm_final, loss_target, m_norm_in, m_w_in, m_q_a_norm, m_w_q_b, m_kv_a_norm, m_w_kv_b, m_gdn_conv, m_gdn_a_log, m_gdn_dt_bias, m_gdn_norm, m_mem_norm, m_w_mem_kv, m_w_out, m_norm_final, v_norm_in, v_w_in, v_q_a_norm, v_w_q_b, v_kv_a_norm, v_w_kv_b, v_gdn_conv, v_gdn_a_log, v_gdn_dt_bias, v_gdn_norm, v_mem_norm, v_w_mem_kv, v_w_out, v_norm_final):
    B = x.shape[0]
    cx, cy, cc = lax.axis_index("x"), lax.axis_index("y"), lax.axis_index("c")
    chip = 2 * cx + cy

    big_names = ("w_in", "w_q_b", "w_kv_b", "w_mem_kv", "w_out")
    rows_major = lambda a: jnp.transpose(a, (2, 0, 1))
    w_in3, m_in3, v_in3 = rows_major(w_in), rows_major(m_w_in), rows_major(v_w_in)
    w_qb_t, m_qb_t, v_qb_t = jnp.transpose(w_q_b[0]), jnp.transpose(m_w_q_b[0]), jnp.transpose(v_w_q_b[0])
    z32 = jnp.zeros((32, Q_LORA), BF16)
    qb_bf = w_qb_t.astype(BF16)
    qb_padded = jnp.concatenate([qb_bf[:160], z32, qb_bf[160:], z32])
    shards = [dense_bf16(w_in3, "w_in_bf16"), qb_padded, w_kv_b[0].astype(BF16), w_mem_kv[0].astype(BF16), w_out[0].astype(BF16)]
    splits = [(1, D_MODEL // 2)] + [(0, s.shape[0] // 2) for s in shards[1:]]
    *w_in_flight, w_in_started = half_gather_start(shards[0], "w_in_gather_start")
    conv_all = allgather_devices(gdn_conv[0], "allgather_conv")
    conv_cols = gdn_conv.shape[2]
    conv_full = jnp.transpose(conv_all[0::2], (1, 0, 2)).reshape(GDN_CONV, N_CHIPS * conv_cols)
    late_shapes = [(N_CHIPS,) + s.shape for s in shards[1:]]
    late = {}

    def w_in_ready(after):
        g_in = pass_halves_to_sibling(half_gather_wait(*w_in_flight, after, "w_in_gather_wait"), "w_in_gather_sibling")
        *late["a"], started_a = late_gather_start(shards[1:3], g_in, "late_gather_qkv_start")
        *late["b"], started_b = late_gather_start(shards[3:], started_a, "late_gather_mem_out_start")
        return pad_w_in_t(g_in), (started_a, started_b)

    def late_qkv(after):
        g_qb, g_kvb = late_gather_wait(*late["a"], after, "late_gather_qkv_wait")
        return g_qb.reshape(-1, Q_LORA), _perm_w_kv_b(g_kvb)

    def late_mem_out(after):
        g_mem, g_out_w = late_gather_wait(*late["b"], after, "late_gather_mem_out_wait")
        return g_mem.reshape(-1, g_mem.shape[2]), g_out_w.reshape(-1, g_out_w.shape[2])

    weights = (w_in_ready, late_qkv, late_mem_out, (w_in_started,))

    core = jnp.stack([cc]).astype(jnp.int32)
    chip_core = jnp.stack([chip, cc]).astype(jnp.int32)
    exchanges = {}
    by_chip = dict(w_in=unpad_w_in_t, w_q_b=lambda a: a.reshape(late_shapes[0]), w_kv_b=_unperm_w_kv_b,
                   w_mem_kv=lambda a: a.reshape(late_shapes[2]), w_out=lambda a: a.reshape(late_shapes[3]))

    def big_grads_ready(gb, group):
        idx = [big_names.index(n) for n in gb]
        parts = [by_chip[n](a) for n, a in gb.items()]
        sp = [splits[i] for i in idx]
        from_sibling = swap_sibling(parts, "rs_sibling_partial_" + group, sp)
        chip_sums = add_pairs(parts, from_sibling, sp, core, "rs_add_sibling_" + group)
        sems, sums_thru, lands, token = exchange_chips_start(chip_sums, "rs_exchange_start_" + group)
        exchanges[group] = dict(idx=idx, parts=parts, from_sibling=from_sibling, sems=sems, sums=sums_thru, lands=lands)
        return token

    sq, grad_x, g = local_step(x, mem, positions, loss_target, norm_in, weights, big_grads_ready, q_a_norm, kv_a_norm, conv_full,
                               gdn_a_log, gdn_dt_bias, gdn_norm, mem_norm, norm_final)

    small_names = ("norm_in", "q_a_norm", "kv_a_norm", "gdn_a_log", "gdn_dt_bias", "gdn_norm", "mem_norm", "norm_final")
    small = dict(norm_in=norm_in, q_a_norm=q_a_norm, kv_a_norm=kv_a_norm, gdn_a_log=gdn_a_log, gdn_dt_bias=gdn_dt_bias,
                 gdn_norm=gdn_norm, mem_norm=mem_norm, norm_final=norm_final)
    m_small = dict(norm_in=m_norm_in, q_a_norm=m_q_a_norm, kv_a_norm=m_kv_a_norm, gdn_a_log=m_gdn_a_log,
                   gdn_dt_bias=m_gdn_dt_bias, gdn_norm=m_gdn_norm, mem_norm=m_mem_norm, norm_final=m_norm_final)
    v_small = dict(norm_in=v_norm_in, q_a_norm=v_q_a_norm, kv_a_norm=v_kv_a_norm, gdn_a_log=v_gdn_a_log,
                   gdn_dt_bias=v_gdn_dt_bias, gdn_norm=v_gdn_norm, mem_norm=v_mem_norm, norm_final=v_norm_final)
    conv_rows = GDN_CONV * GDN_QKV // 1024
    g_block = pack_small([g.get(n) for n in small_names], g["gdn_a_log_dt_bias"], g["gdn_conv"], sq)
    g_block = sum_leading(allgather_devices(g_block, "allgather_small_grads"), "sum_small_grads")
    loss = 0.5 * jnp.sum(g_block[LOSS_ROW]) / D_MODEL
    g_conv = lax.dynamic_slice_in_dim(g_block[CONV_ROW0:CONV_ROW0 + conv_rows].reshape(GDN_CONV, GDN_QKV), chip * conv_cols,
                                      conv_cols, axis=1)
    as_row = lambda a: a.reshape(1, -1)
    updated = adamw_small(g_block, [as_row(small[n]) for n in small_names], [as_row(m_small[n]) for n in small_names],
                          [as_row(v_small[n]) for n in small_names])
    g_out, d_out, m_out, v_out = ({n: u[i].reshape(small[n].shape) for n, u in zip(small_names, updated)} for i in range(4))
    d_s = d_out["norm_in"]

    my_half = [None] * len(big_names)
    for group, e in exchanges.items():
        from_chips = exchange_chips_wait(e["sems"], e["sums"], e["lands"], d_s, "rs_exchange_wait_" + group)
        halves = add_fives(e["parts"], e["from_sibling"], from_chips, [splits[i] for i in e["idx"]], chip_core, "rs_add_chips_" + group)
        for i, a in zip(e["idx"], halves):
            my_half[i] = a
    other_half = swap_sibling(my_half, "rs_sibling_final")

    d_out["gdn_conv"], m_out["gdn_conv"], v_out["gdn_conv"] = adamw(gdn_conv, g_conv, m_gdn_conv, v_gdn_conv, "adamw_gdn_conv")
    g_out["gdn_conv"] = g_conv[None]
    res = adamw_untiled_rows(w_in3, my_half[0], other_half[0], m_in3, v_in3, "adamw_w_in")
    g_out["w_in"], d_out["w_in"], m_out["w_in"], v_out["w_in"] = [jnp.transpose(r, (1, 2, 0)) for r in res]
    res = adamw_w_q_b(w_qb_t, my_half[1], other_half[1], m_qb_t, v_qb_t, "adamw_w_q_b")
    g_out["w_q_b"], d_out["w_q_b"], m_out["w_q_b"], v_out["w_q_b"] = [jnp.transpose(r)[None] for r in res]
    rest = dict(w_kv_b=(w_kv_b, m_w_kv_b, v_w_kv_b), w_mem_kv=(w_mem_kv, m_w_mem_kv, v_w_mem_kv), w_out=(w_out, m_w_out, v_w_out))
    for i, n in enumerate(big_names):
        if n in rest:
            w_n, m_n, v_n = rest[n]
            g_out[n], d_out[n], m_out[n], v_out[n] = adamw_halves(w_n, my_half[i], other_half[i], m_n, v_n, splits[i], core, "adamw_" + n)

    order = ("norm_in", "w_in", "q_a_norm", "w_q_b", "kv_a_norm", "w_kv_b", "gdn_conv", "gdn_a_log", "gdn_dt_bias",
             "gdn_norm", "mem_norm", "w_mem_kv", "w_out", "norm_final")
    return (loss, grad_x, *[g_out[n] for n in order], *[d_out[n] for n in order], *[m_out[n] for n in order],
            *[v_out[n] for n in order])
```

```python
import jax
import jax.numpy as jnp
from jax import lax
from jax.experimental import pallas as pl
from jax.experimental.pallas import tpu as pltpu

F32 = jnp.float32
BF16 = jnp.bfloat16
BS = pl.BlockSpec

D_MODEL = 1024
N_HEADS = 4
MLA_NOPE, MLA_ROPE, MLA_V = 128, 64, 128
Q_LORA, KV_LORA = 384, 256
ROPE_THETA = 10000.0
GDN_DK = GDN_DV = 128
GDN_CONV = 4
CHUNK = 64
MEM_DH = 128
D_MIX = 1536
GDN_QKV = 1536
D_IN = 4296
EPS = 1e-6
ADAM_LR, ADAM_B1, ADAM_B2, ADAM_EPS, ADAM_WD, ADAM_STEP = 0.001, 0.9, 0.999, 1e-08, 0.01, 10

OFF_MLA = 0
OFF_MEMQ = 1024
OFF_GDN = 1536
OFF_GATE = 3072
N_PAD = 4608
HEAD_PAD = 256
MLA_SCALE = (MLA_NOPE + MLA_ROPE) ** -0.5
MEM_SCALE = MEM_DH ** -0.5
GDN_SCALE = GDN_DK ** -0.5
NEG = -1e30

NN = ((1,), (0,))
NT = ((1,), (1,))
TN = ((0,), (0,))


def _dot(a, b, dims):
    return lax.dot_general(a, b, (dims, ((), ())), preferred_element_type=F32)


def _bdot(spec, a, b, precision=None):
    return jnp.einsum(spec, a, b, preferred_element_type=F32, precision=precision)


def _arb(n):
    return pltpu.CompilerParams(dimension_semantics=("arbitrary",) * n)


def _sigmoid(x):
    return 1.0 / (1.0 + jnp.exp(-x))


def _softplus(z):
    return jnp.maximum(z, 0.0) + jnp.log(1.0 + jnp.exp(-jnp.abs(z)))


def _rope(t, cos_row, sin_row):
    return t * cos_row + pltpu.roll(t, 64, 1) * sin_row


def _rope_bwd(d, cos_row, sin_row):
    return d * cos_row + pltpu.roll(d * sin_row, 64, 1)


def rms_fwd(x, gain, name, tm=512, after=()):
    T, n = x.shape
    tm = min(tm, T)

    def body(x_ref, g_ref, *rest):
        xv = x_ref[...]
        r = lax.rsqrt(jnp.mean(xv * xv, axis=-1, keepdims=True) + EPS)
        rest[-1][...] = (xv * r * g_ref[...]).astype(BF16)

    return pl.pallas_call(
        body, name=name, grid=(T // tm,),
        in_specs=[BS((tm, n), lambda i: (i, 0)), BS((1, n), lambda i: (0, 0))] + [BS(memory_space=pl.ANY)] * len(after),
        out_specs=BS((tm, n), lambda i: (i, 0)),
        out_shape=jax.ShapeDtypeStruct((T, n), BF16), compiler_params=_arb(1))(x, gain, *after)


def mm(a, b, kind, out_dtype, name, bm=512, bn=None, n_outer=False, after=()):
    if kind == "nn":
        (M, K), (_, N) = a.shape, b.shape
    elif kind == "nt":
        (M, K), (N, _) = a.shape, b.shape
    else:
        (K, M), (_, N) = a.shape, b.shape
    bm, bn = min(bm, M), min(bn or N, N)
    assert M % bm == 0 and N % bn == 0, (name, M, N, K)
    ij = (lambda g0, g1: (g1, g0)) if n_outer else (lambda g0, g1: (g0, g1))
    a_spec = BS((K, bm), lambda g0, g1: (0, ij(g0, g1)[0])) if kind == "tn" else BS((bm, K), lambda g0, g1: (ij(g0, g1)[0], 0))
    once = dict(pipeline_mode=pl.Buffered(1)) if bn == N else {}
    b_spec = (BS((bn, K), lambda g0, g1: (ij(g0, g1)[1], 0), **once) if kind == "nt"
              else BS((K, bn), lambda g0, g1: (0, ij(g0, g1)[1]), **once))
    dims = {"nn": NN, "nt": NT, "tn": TN}[kind]

    def body(a_ref, b_ref, *rest):
        rest[-1][...] = _dot(a_ref[...].astype(BF16), b_ref[...].astype(BF16), dims).astype(out_dtype)

    grid = (N // bn, M // bm) if n_outer else (M // bm, N // bn)
    return pl.pallas_call(
        body, name=name, grid=grid, in_specs=[a_spec, b_spec] + [BS(memory_space=pl.ANY)] * len(after),
        out_specs=BS((bm, bn), lambda g0, g1: ij(g0, g1)),
        out_shape=jax.ShapeDtypeStruct((M, N), out_dtype), compiler_params=_arb(2))(a, b, *after)


def mm_cols_tn(pieces, b, out_dtype, name, bm=512):
    K, N = b.shape
    tiles = [p.shape[1] // bm for p in pieces]
    firsts = [sum(tiles[:i]) for i in range(len(tiles))]

    def body(*refs):
        b_ref, o_ref = refs[-2], refs[-1]
        i = pl.program_id(0)
        for a_ref, t0, n in zip(refs[:-2], firsts, tiles):
            @pl.when((i >= t0) & (i < t0 + n))
            def _(a_ref=a_ref):
                o_ref[...] = _dot(a_ref[...], b_ref[...], TN).astype(out_dtype)

    a_specs = [BS((K, bm), lambda i, t0=t0, n=n: (0, jnp.clip(i - t0, 0, n - 1))) for t0, n in zip(firsts, tiles)]
    return pl.pallas_call(
        body, name=name, grid=(sum(tiles),),
        in_specs=a_specs + [BS(b.shape, lambda i: (0, 0), pipeline_mode=pl.Buffered(1))],
        out_specs=BS((bm, N), lambda i: (i, 0)), out_shape=jax.ShapeDtypeStruct((sum(tiles) * bm, N), out_dtype),
        compiler_params=_arb(1))(*pieces, b)


def rope_tables(pos_col, inv_row, sgn_row, msk_row, tm=512, after=()):
    T = pos_col.shape[0]
    tm = min(tm, T)

    def body(p_ref, inv_ref, sgn_ref, msk_ref, *rest):
        c_ref, s_ref = rest[-2:]
        ang = p_ref[...].astype(F32) * inv_ref[...]
        c_ref[...] = jnp.cos(ang) * msk_ref[...]
        s_ref[...] = jnp.sin(ang) * sgn_ref[...]

    row = BS((1, 128), lambda i: (0, 0))
    return pl.pallas_call(
        body, name="rope_tables", grid=(T // tm,),
        in_specs=[BS((tm, 1), lambda i: (i, 0)), row, row, row] + [BS(memory_space=pl.ANY)] * len(after),
        out_specs=[BS((tm, 128), lambda i: (i, 0))] * 2,
        out_shape=[jax.ShapeDtypeStruct((T, 128), F32)] * 2, compiler_params=_arb(1))(pos_col, inv_row, sgn_row, msk_row, *after)


def mla_prep(P, gq, gkv, wq, wkv, cos_t, sin_t, tm=512):
    T = P.shape[0]
    tm = min(tm, T)

    def body(p_ref, gq_ref, gkv_ref, wq_ref, wkv_ref, c_ref, s_ref, q_ref, k_ref, v_ref, qn_ref, kvn_ref):
        p = p_ref[...].astype(F32)
        cq, ckv, kr = p[:, :Q_LORA], p[:, Q_LORA:Q_LORA + KV_LORA], p[:, 640:768]
        qn = (cq * lax.rsqrt(jnp.mean(cq * cq, axis=-1, keepdims=True) + EPS) * gq_ref[...]).astype(BF16)
        kvn = (ckv * lax.rsqrt(jnp.mean(ckv * ckv, axis=-1, keepdims=True) + EPS) * gkv_ref[...]).astype(BF16)
        qn_ref[...] = qn
        kvn_ref[...] = kvn
        q = _dot(qn, wq_ref[...], NT)
        kv = _dot(kvn, wkv_ref[...], NN)
        cos_row, sin_row = c_ref[...], s_ref[...]
        krr = _rope(kr, cos_row, sin_row).astype(BF16)
        for h in range(N_HEADS):
            lo = h * HEAD_PAD
            q_ref[:, lo:lo + 128] = (q[:, lo:lo + 128] * MLA_SCALE).astype(BF16)
            q_ref[:, lo + 128:lo + 256] = (_rope(q[:, lo + 128:lo + 256], cos_row, sin_row) * MLA_SCALE).astype(BF16)
            k_ref[:, lo:lo + 128] = kv[:, h * 128:(h + 1) * 128].astype(BF16)
            k_ref[:, lo + 128:lo + 256] = krr
            v_ref[:, lo:lo + 128] = kv[:, 512 + h * 128:512 + (h + 1) * 128].astype(BF16)
            v_ref[:, lo + 128:lo + 256] = jnp.ones((tm, 128), BF16)

    full = lambda r, c: BS((r, c), lambda i: (0, 0))
    rowb = lambda c: BS((tm, c), lambda i: (i, 0))
    return pl.pallas_call(
        body, name="mla_prep", grid=(T // tm,),
        in_specs=[rowb(1024), full(1, Q_LORA), full(1, KV_LORA), full(1024, Q_LORA), full(KV_LORA, 1024), rowb(128), rowb(128)],
        out_specs=[rowb(1024), rowb(1024), rowb(1024), rowb(Q_LORA), rowb(KV_LORA)],
        out_shape=[jax.ShapeDtypeStruct((T, 1024), BF16), jax.ShapeDtypeStruct((T, 1024), BF16),
                   jax.ShapeDtypeStruct((T, 1024), BF16), jax.ShapeDtypeStruct((T, Q_LORA), BF16),
                   jax.ShapeDtypeStruct((T, KV_LORA), BF16)],
        compiler_params=_arb(1))(P, gq, gkv, wq, wkv, cos_t, sin_t)


ATTN_HEADS_PER_STEP = 2
ATTN_STRIP = 32


def mla_attn_fwd(Q, K, V, B, S, tq=512, hp=ATTN_HEADS_PER_STEP):
    T = B * S
    tq = min(tq, S)
    nq = S // tq

    rs = min(ATTN_STRIP, tq)

    def body(q_ref, k_ref, v_ref, o_ref, lse_ref, m_s, acc_s, s_s, p_s, a_s):
        i = pl.program_id(2)
        m_s[...] = jnp.full_like(m_s, NEG)
        acc_s[...] = jnp.zeros_like(acc_s)

        def blk(j, masked):
            rows = pl.ds(pl.multiple_of(j * tq, tq), tq)
            for h in range(hp):
                hq = slice(h * HEAD_PAD, (h + 1) * HEAD_PAD)
                s_s[h] = _dot(q_ref[:, hq], k_ref[rows, hq], NT)
            for h in range(hp):
                for r0 in range(0, tq, rs):
                    rr = slice(r0, r0 + rs)
                    sv = s_s[h, rr, :]
                    if masked:
                        r = r0 + lax.broadcasted_iota(jnp.int32, (rs, tq), 0)
                        c = lax.broadcasted_iota(jnp.int32, (rs, tq), 1)
                        sv = jnp.where(r >= c, sv, NEG)
                    m_prev = m_s[h, rr, :]
                    m_new = jnp.maximum(m_prev, jnp.max(sv, axis=1, keepdims=True))
                    p_s[h, rr, :] = jnp.exp(sv - m_new).astype(BF16)
                    a_s[h, rr, :] = jnp.exp(m_prev - m_new)
                    m_s[h, rr, :] = m_new
            for h in range(hp):
                hq = slice(h * HEAD_PAD, (h + 1) * HEAD_PAD)
                acc_s[h] = a_s[h] * acc_s[h] + _dot(p_s[h], v_ref[rows, hq], NN)

        def loop(j, c):
            blk(j, False)
            return c

        lax.fori_loop(0, i, loop, 0)
        blk(i, True)
        for h in range(hp):
            den = acc_s[h, :, 128:256]
            o_ref[:, h * 128:(h + 1) * 128] = (acc_s[h, :, 0:128] / den).astype(BF16)
            lse_ref[h] = m_s[h] + jnp.log(den[:, 0:1])

    return pl.pallas_call(
        body, name="mla_attn_fwd", grid=(B, N_HEADS // hp, nq),
        in_specs=[BS((tq, hp * HEAD_PAD), lambda b, h, i: (b * nq + i, h)),
                  BS((S, hp * HEAD_PAD), lambda b, h, i: (b, h)),
                  BS((S, hp * HEAD_PAD), lambda b, h, i: (b, h))],
        out_specs=[BS((tq, hp * 128), lambda b, h, i: (b * nq + i, h)),
                   BS((hp, tq, 1), lambda b, h, i: (h, b * nq + i, 0))],
        out_shape=[jax.ShapeDtypeStruct((T, 512), BF16), jax.ShapeDtypeStruct((N_HEADS, T, 1), F32)],
        scratch_shapes=[pltpu.VMEM((hp, tq, 1), F32), pltpu.VMEM((hp, tq, HEAD_PAD), F32), pltpu.VMEM((hp, tq, tq), F32),
                        pltpu.VMEM((hp, tq, tq), BF16), pltpu.VMEM((hp, tq, 1), F32)],
        compiler_params=_arb(3))(Q, K, V)


def mla_attn_bwd(Q, K, V, O, dO, LSE, B, S, tq=512, hp=ATTN_HEADS_PER_STEP):
    T = B * S
    tq = min(tq, S)
    nq = S // tq

    rs = min(ATTN_STRIP, tq)

    def body(q_ref, k_ref, v_ref, o_ref, do_ref, lse_ref, dq_ref, dk_ref, dv_ref, delta_s, dq_s, dk_s, dv_s, s_s, dp_s, p_s, ds_s):
        j = pl.program_id(2)

        @pl.when(j == 0)
        def _():
            dq_s[...] = jnp.zeros_like(dq_s)
            for h in range(hp):
                sl = slice(h * 128, (h + 1) * 128)
                delta_s[h] = jnp.sum(do_ref[:, sl] * o_ref[:, sl].astype(F32), axis=1, keepdims=True)

        dk_s[...] = jnp.zeros_like(dk_s)
        dv_s[...] = jnp.zeros_like(dv_s)

        def step(i, c):
            rows = pl.ds(pl.multiple_of(i * tq, tq), tq)
            for h in range(hp):
                sq, sv = slice(h * HEAD_PAD, (h + 1) * HEAD_PAD), slice(h * 128, (h + 1) * 128)
                s_s[h] = _dot(q_ref[rows, sq], k_ref[:, sq], NT)
                dp_s[h] = _dot(do_ref[rows, sv].astype(BF16), v_ref[:, h * HEAD_PAD:h * HEAD_PAD + 128], NT)
            for h in range(hp):
                for r0 in range(0, tq, rs):
                    rr = slice(r0, r0 + rs)
                    seq_rows = pl.ds(pl.multiple_of(i * tq + r0, rs), rs)
                    r = i * tq + r0 + lax.broadcasted_iota(jnp.int32, (rs, tq), 0)
                    cc = j * tq + lax.broadcasted_iota(jnp.int32, (rs, tq), 1)
                    p = jnp.where(r >= cc, jnp.exp(s_s[h, rr, :] - lse_ref[h, seq_rows, :]), 0.0)
                    p_s[h, rr, :] = p.astype(BF16)
                    ds_s[h, rr, :] = (p * (dp_s[h, rr, :] - delta_s[h, seq_rows, :])).astype(BF16)
            for h in range(hp):
                sq, sv = slice(h * HEAD_PAD, (h + 1) * HEAD_PAD), slice(h * 128, (h + 1) * 128)
                dv_s[:, sv] += _dot(p_s[h], do_ref[rows, sv].astype(BF16), TN)
                dk_s[:, sq] += _dot(ds_s[h], q_ref[rows, sq], TN)
                dq_s[rows, sq] += _dot(ds_s[h], k_ref[:, sq], NN)
            return c

        lax.fori_loop(j, nq, step, 0)
        dk_ref[...] = dk_s[...].astype(BF16)
        dv_ref[...] = dv_s[...].astype(BF16)

        @pl.when(j == nq - 1)
        def _():
            dq_ref[...] = dq_s[...].astype(BF16)

    seq = lambda c: BS((S, c), lambda b, h, j: (b, h))
    blk = lambda c: BS((tq, c), lambda b, h, j: (b * nq + j, h))
    return pl.pallas_call(
        body, name="mla_attn_bwd", grid=(B, N_HEADS // hp, nq),
        in_specs=[seq(hp * HEAD_PAD), blk(hp * HEAD_PAD), blk(hp * HEAD_PAD), seq(hp * 128), seq(hp * 128),
                  BS((hp, S, 1), lambda b, h, j: (h, b, 0))],
        out_specs=[seq(hp * HEAD_PAD), blk(hp * HEAD_PAD), blk(hp * 128)],
        out_shape=[jax.ShapeDtypeStruct((T, 1024), BF16), jax.ShapeDtypeStruct((T, 1024), BF16),
                   jax.ShapeDtypeStruct((T, 512), BF16)],
        scratch_shapes=[pltpu.VMEM((hp, S, 1), F32), pltpu.VMEM((S, hp * HEAD_PAD), F32),
                        pltpu.VMEM((tq, hp * HEAD_PAD), F32), pltpu.VMEM((tq, hp * 128), F32),
                        pltpu.VMEM((hp, tq, tq), F32), pltpu.VMEM((hp, tq, tq), F32),
                        pltpu.VMEM((hp, tq, tq), BF16), pltpu.VMEM((hp, tq, tq), BF16)],
        compiler_params=_arb(3))(Q, K, V, O, dO, LSE)


def mla_proj_bwd(dQ, dK, dV, cos_t, sin_t, P, dab, wq, wkv, gq, gkv, tm=512):
    T = P.shape[0]
    tm = min(tm, T)

    def norm_bwd(x, dy, g):
        r = lax.rsqrt(jnp.mean(x * x, axis=-1, keepdims=True) + EPS)
        xh = x * r
        dxh = dy * g
        return r * (dxh - xh * jnp.mean(dxh * xh, axis=-1, keepdims=True)), jnp.sum(dy * xh, axis=0, keepdims=True)

    def body(dq_ref, dk_ref, dv_ref, c_ref, s_ref, p_ref, dab_ref, wq_ref, wkv_ref, gq_ref, gkv_ref,
             ql_ref, kvl_ref, o_ref, aq_ref, akv_ref):
        @pl.when(pl.program_id(0) == 0)
        def _():
            aq_ref[...] = jnp.zeros_like(aq_ref)
            akv_ref[...] = jnp.zeros_like(akv_ref)

        cos_row, sin_row = c_ref[...], s_ref[...]
        kr = jnp.zeros((tm, 128), F32)
        for h in range(N_HEADS):
            lo = h * HEAD_PAD
            ql_ref[:, lo:lo + 128] = (dq_ref[:, lo:lo + 128].astype(F32) * MLA_SCALE).astype(BF16)
            ql_ref[:, lo + 128:lo + 256] = (_rope_bwd(dq_ref[:, lo + 128:lo + 256].astype(F32), cos_row, sin_row) * MLA_SCALE).astype(BF16)
            kvl_ref[:, h * 128:(h + 1) * 128] = dk_ref[:, lo:lo + 128]
            kr = kr + dk_ref[:, lo + 128:lo + 256].astype(F32)
        kvl_ref[:, 512:] = dv_ref[...]
        dqn = _dot(ql_ref[...], wq_ref[...], NN)
        dkvn = _dot(kvl_ref[...], wkv_ref[...], NT)
        dcq, ggq = norm_bwd(p_ref[:, :Q_LORA].astype(F32), dqn, gq_ref[...])
        dckv, ggkv = norm_bwd(p_ref[:, Q_LORA:640].astype(F32), dkvn, gkv_ref[...])
        aq_ref[...] += ggq
        akv_ref[...] += ggkv
        o_ref[:, :Q_LORA] = dcq.astype(BF16)
        o_ref[:, Q_LORA:640] = dckv.astype(BF16)
        o_ref[:, 640:768] = _rope_bwd(kr, cos_row, sin_row).astype(BF16)
        o_ref[:, 768:896] = dab_ref[...]
        o_ref[:, 896:1024] = jnp.zeros((tm, 128), BF16)

    rowb = lambda c: BS((tm, c), lambda i: (i, 0))
    full = lambda r, c: BS((r, c), lambda i: (0, 0))
    return pl.pallas_call(
        body, name="mla_proj_bwd", grid=(T // tm,),
        in_specs=[rowb(1024), rowb(1024), rowb(512), rowb(128), rowb(128), rowb(1024), rowb(128),
                  full(1024, Q_LORA), full(KV_LORA, 1024), full(1, Q_LORA), full(1, KV_LORA)],
        out_specs=[rowb(1024), rowb(1024), rowb(1024), full(1, Q_LORA), full(1, KV_LORA)],
        out_shape=[jax.ShapeDtypeStruct((T, 1024), BF16)] * 3
        + [jax.ShapeDtypeStruct((1, Q_LORA), F32), jax.ShapeDtypeStruct((1, KV_LORA), F32)],
        compiler_params=_arb(1))(dQ, dK, dV, cos_t, sin_t, P, dab, wq, wkv, gq, gkv)


def _mem_probs(qh, kh):
    s = _dot(qh, kh, NT) * MEM_SCALE
    p = jnp.exp(s - jnp.max(s, axis=1, keepdims=True))
    return p / jnp.sum(p, axis=1, keepdims=True)


def mem_attn_fwd(P, MKV, B, S, M, tq=512):
    T = B * S
    tq = min(tq, S)
    nq = S // tq

    def body(q_ref, kv_ref, o_ref):
        for h in range(N_HEADS):
            sl = slice(h * 128, (h + 1) * 128)
            p = _mem_probs(q_ref[:, sl].astype(BF16), kv_ref[:, sl])
            o_ref[:, sl] = _dot(p.astype(BF16), kv_ref[:, 512 + h * 128:512 + (h + 1) * 128], NN).astype(BF16)

    return pl.pallas_call(
        body, name="mem_attn_fwd", grid=(B, nq),
        in_specs=[BS((tq, 512), lambda b, i: (b * nq + i, OFF_MEMQ // 512)), BS((M, 1024), lambda b, i: (b, 0))],
        out_specs=BS((tq, 512), lambda b, i: (b * nq + i, 0)),
        out_shape=jax.ShapeDtypeStruct((T, 512), BF16), compiler_params=_arb(2))(P, MKV)


def mem_attn_bwd(P, MKV, dO, B, S, M, tq=512):
    T = B * S
    tq = min(tq, S)
    nq = S // tq

    def body(q_ref, kv_ref, do_ref, dq_ref, dkv_ref):
        @pl.when(pl.program_id(1) == 0)
        def _():
            dkv_ref[...] = jnp.zeros_like(dkv_ref)

        for h in range(N_HEADS):
            sl = slice(h * 128, (h + 1) * 128)
            sv = slice(512 + h * 128, 512 + (h + 1) * 128)
            qh = q_ref[:, sl].astype(BF16)
            kh = kv_ref[:, sl]
            do = do_ref[:, sl].astype(BF16)
            p = _mem_probs(qh, kh)
            dkv_ref[:, sv] += _dot(p.astype(BF16), do, TN)
            dp = _dot(do, kv_ref[:, sv], NT)
            ds = (p * (dp - jnp.sum(dp * p, axis=1, keepdims=True)) * MEM_SCALE).astype(BF16)
            dq_ref[:, sl] = _dot(ds, kh, NN).astype(BF16)
            dkv_ref[:, sl] += _dot(ds, qh, TN)

    return pl.pallas_call(
        body, name="mem_attn_bwd", grid=(B, nq),
        in_specs=[BS((tq, 512), lambda b, i: (b * nq + i, OFF_MEMQ // 512)), BS((M, 1024), lambda b, i: (b, 0)),
                  BS((tq, 512), lambda b, i: (b * nq + i, 0))],
        out_specs=[BS((tq, 512), lambda b, i: (b * nq + i, 0)), BS((M, 1024), lambda b, i: (b, 0))],
        out_shape=[jax.ShapeDtypeStruct((T, 512), BF16), jax.ShapeDtypeStruct((B * M, 1024), F32)],
        compiler_params=_arb(2))(P, MKV, dO)


def gain_grad(x, dy, name, tm=256):
    T, n = x.shape
    tm = min(tm, T)

    def body(x_ref, dy_ref, o_ref):
        @pl.when(pl.program_id(0) == 0)
        def _():
            o_ref[...] = jnp.zeros_like(o_ref)

        xv = x_ref[...]
        xh = xv * lax.rsqrt(jnp.mean(xv * xv, axis=-1, keepdims=True) + EPS)
        o_ref[...] += jnp.sum(dy_ref[...] * xh, axis=0, keepdims=True)

    return pl.pallas_call(
        body, name=name, grid=(T // tm,),
        in_specs=[BS((tm, n), lambda i: (i, 0))] * 2, out_specs=BS((1, n), lambda i: (0, 0)),
        out_shape=jax.ShapeDtypeStruct((1, n), F32), compiler_params=_arb(1))(x, dy)


def _conv_silu(x, w, t):
    y = x * w[3:4, :]
    for s in range(1, GDN_CONV):
        y = y + jnp.where(t >= s, pltpu.roll(x, s, 0), 0.0) * w[3 - s:4 - s, :]
    return y, _sigmoid(y)


def gdn_prep_fwd(P, conv_w, B, S):
    T = B * S

    def body(x_ref, w_ref, o_ref):
        kind = pl.program_id(1)
        t = lax.broadcasted_iota(jnp.int32, (S, 1), 0)
        y, sg = _conv_silu(x_ref[...].astype(F32), w_ref[...], t)
        a = y * sg
        scale = jnp.where(kind == 0, GDN_SCALE, 1.0).astype(F32)
        for h in range(N_HEADS):
            sl = slice(h * 128, (h + 1) * 128)
            seg = a[:, sl]
            n = lax.rsqrt(jnp.sum(seg * seg, axis=-1, keepdims=True) + EPS)
            o_ref[:, sl] = jnp.where(kind < 2, seg * (n * scale), seg)

    return pl.pallas_call(
        body, name="gdn_prep_fwd", grid=(B, 3),
        in_specs=[BS((S, 512), lambda b, k: (b, OFF_GDN // 512 + k)), BS((GDN_CONV, 512), lambda b, k: (0, k))],
        out_specs=BS((S, 512), lambda b, k: (b, k)),
        out_shape=jax.ShapeDtypeStruct((T, GDN_QKV), F32), compiler_params=_arb(2))(P, conv_w)


def gdn_prep_bwd(P, dqkv, conv_w, B, S):
    T = B * S

    def body(x_ref, d_ref, w_ref, o_ref, gw_ref):
        kind = pl.program_id(0)

        @pl.when(pl.program_id(1) == 0)
        def _():
            gw_ref[...] = jnp.zeros_like(gw_ref)

        t = lax.broadcasted_iota(jnp.int32, (S, 1), 0)
        x = x_ref[...].astype(F32)
        w = w_ref[...]
        y, sg = _conv_silu(x, w, t)
        a = y * sg
        scale = jnp.where(kind == 0, GDN_SCALE, 1.0).astype(F32)
        das = []
        for h in range(N_HEADS):
            sl = slice(h * 128, (h + 1) * 128)
            seg, dseg = a[:, sl], d_ref[:, sl]
            n = lax.rsqrt(jnp.sum(seg * seg, axis=-1, keepdims=True) + EPS)
            dn = scale * (n * dseg - seg * (n * n * n) * jnp.sum(dseg * seg, axis=-1, keepdims=True))
            das.append(jnp.where(kind < 2, dn, dseg))
        dy = jnp.concatenate(das, axis=1) * (sg * (1.0 + y * (1.0 - sg)))
        dx = dy * w[3:4, :]
        gw_ref[3:4, :] += jnp.sum(dy * x, axis=0, keepdims=True)
        for s in range(1, GDN_CONV):
            dx = dx + jnp.where(t + s < S, pltpu.roll(dy, S - s, 0), 0.0) * w[3 - s:4 - s, :]
            gw_ref[3 - s:4 - s, :] += jnp.sum(dy * jnp.where(t >= s, pltpu.roll(x, s, 0), 0.0), axis=0, keepdims=True)
        o_ref[...] = dx.astype(BF16)

    return pl.pallas_call(
        body, name="gdn_prep_bwd", grid=(3, B),
        in_specs=[BS((S, 512), lambda k, b: (b, OFF_GDN // 512 + k)), BS((S, 512), lambda k, b: (b, k)),
                  BS((GDN_CONV, 512), lambda k, b: (0, k))],
        out_specs=[BS((S, 512), lambda k, b: (b, k)), BS((GDN_CONV, 512), lambda k, b: (0, k))],
        out_shape=[jax.ShapeDtypeStruct((T, GDN_QKV), BF16), jax.ShapeDtypeStruct((GDN_CONV, GDN_QKV), F32)],
        compiler_params=_arb(2))(P, dqkv, conv_w)


def _chunk_row(n_rows):
    return lax.broadcasted_iota(jnp.int32, (n_rows, 1), 0) % CHUNK


def gdn_gate_fwd(P, alog_row, dt_row, B, S):
    T = B * S

    def body(x_ref, al_ref, dt_ref, o_ref):
        x = x_ref[...].astype(F32)
        lane = lax.broadcasted_iota(jnp.int32, (1, 128), 1)
        g = jnp.where(lane < 4, -jnp.exp(al_ref[...]) * _softplus(x + dt_ref[...]), 0.0)
        t = _chunk_row(S)
        for s in (1, 2, 4, 8, 16, 32):
            g = g + jnp.where(t >= s, pltpu.roll(g, s, 0), 0.0)
        o_ref[...] = jnp.where(lane < 4, g, jnp.where(lane < 8, _sigmoid(x), 0.0))

    row = BS((1, 128), lambda b: (0, 0))
    return pl.pallas_call(
        body, name="gdn_gate_fwd", grid=(B,),
        in_specs=[BS((S, 128), lambda b: (b, 768 // 128)), row, row], out_specs=BS((S, 128), lambda b: (b, 0)),
        out_shape=jax.ShapeDtypeStruct((T, 128), F32), compiler_params=_arb(1))(P, alog_row, dt_row)


def gdn_gate_bwd(P, dGB, alog_row, dt_row, B, S):
    T = B * S

    def body(x_ref, d_ref, al_ref, dt_ref, o_ref, acc_ref):
        @pl.when(pl.program_id(0) == 0)
        def _():
            acc_ref[...] = jnp.zeros_like(acc_ref)

        x, d = x_ref[...].astype(F32), d_ref[...]
        lane = lax.broadcasted_iota(jnp.int32, (1, 128), 1)
        z = x + dt_ref[...]
        coef = -jnp.exp(al_ref[...])
        g = coef * _softplus(z)
        da = jnp.where(lane < 4, d * coef * _sigmoid(z), 0.0)
        beta = _sigmoid(x)
        o_ref[...] = jnp.where(lane < 4, da, jnp.where(lane < 8, d * beta * (1.0 - beta), 0.0)).astype(BF16)
        acc_ref[0:1, :] += jnp.sum(jnp.where(lane < 4, d * g, 0.0), axis=0, keepdims=True)
        acc_ref[1:2, :] += jnp.sum(da, axis=0, keepdims=True)

    row = BS((1, 128), lambda b: (0, 0))
    return pl.pallas_call(
        body, name="gdn_gate_bwd", grid=(B,),
        in_specs=[BS((S, 128), lambda b: (b, 768 // 128)), BS((S, 128), lambda b: (b, 0)), row, row],
        out_specs=[BS((S, 128), lambda b: (b, 0)), BS((8, 128), lambda b: (0, 0))],
        out_shape=[jax.ShapeDtypeStruct((T, 128), BF16), jax.ShapeDtypeStruct((8, 128), F32)],
        compiler_params=_arb(1))(P, dGB, alog_row, dt_row)


def _chunk_masks(nc):
    r = lax.broadcasted_iota(jnp.int32, (nc, CHUNK, CHUNK), 1)
    c = lax.broadcasted_iota(jnp.int32, (nc, CHUNK, CHUNK), 2)
    return r >= c, r > c


def _chunk_local(q, k, gc, gr, beta, incl, strict):
    decay = jnp.exp(jnp.where(incl, gc - gr, NEG))
    kb = k * beta
    kbf = k.astype(BF16)
    m_kk = _bdot("gcd,gjd->gcj", kb.astype(BF16), kbf)
    l_mat = jnp.where(strict, m_kk * decay, 0.0)
    a_mat = _bdot("gcd,gjd->gcj", q.astype(BF16), kbf) * decay
    return decay, kb, l_mat, a_mat


WY_SPLIT_LEVELS = 2


def _split_bf16(x):
    hi = x.astype(BF16)
    return hi, (x - hi.astype(F32)).astype(BF16)


def _mm_split(ah, al, bh, bl):
    spec = "gij,gjk->gik"
    return _bdot(spec, ah, bh) + (_bdot(spec, ah, bl) + _bdot(spec, al, bh))


def gdn_chunk_fwd(qkv, GB, Grow, B, S, nc=8):
    T = B * S
    N = S // CHUNK
    nc = min(nc, N)
    nb = N // nc
    R = nc * CHUNK

    def body(q_ref, k_ref, v_ref, gb_ref, gr_ref, u_ref, w_ref, t_ref, a_ref):
        incl, strict = _chunk_masks(nc)
        eye = (lax.broadcasted_iota(jnp.int32, (nc, CHUNK, CHUNK), 1)
               == lax.broadcasted_iota(jnp.int32, (nc, CHUNK, CHUNK), 2)).astype(F32)
        for h in range(N_HEADS):
            sl = slice(h * 128, (h + 1) * 128)
            q = q_ref[:, sl].reshape(nc, CHUNK, 128)
            k = k_ref[:, sl].reshape(nc, CHUNK, 128)
            v = v_ref[:, sl].reshape(nc, CHUNK, 128)
            gc = gb_ref[:, h:h + 1].reshape(nc, CHUNK, 1)
            beta = gb_ref[:, 4 + h:5 + h].reshape(nc, CHUNK, 1)
            gr = gr_ref[h][:, None, :]
            _, kb, l_mat, a_mat = _chunk_local(q, k, gc, gr, beta, incl, strict)
            pw = -l_mat
            tinv = eye + pw
            for level in range(5):
                if level < WY_SPLIT_LEVELS:
                    ph, pl_ = _split_bf16(pw)
                    pw = _mm_split(ph, pl_, ph, pl_)
                    ph, pl_ = _split_bf16(pw)
                    th, tl = _split_bf16(tinv)
                    tinv = tinv + _mm_split(th, tl, ph, pl_)
                else:
                    ph = pw.astype(BF16)
                    pw = _bdot("gij,gjk->gik", ph, ph)
                    tinv = tinv + _bdot("gij,gjk->gik", tinv.astype(BF16), pw.astype(BF16))
            tb = tinv.astype(BF16)
            u = _bdot("gcj,gjv->gcv", tb, (v * beta).astype(BF16))
            w = _bdot("gcj,gjk->gck", tb, (kb * jnp.exp(gc)).astype(BF16))
            u_ref[:, sl] = u.reshape(R, 128)
            w_ref[:, sl] = w.reshape(R, 128).astype(BF16)
            t_ref[h] = jnp.swapaxes(tinv, 1, 2).astype(BF16)
            a_ref[h] = a_mat.astype(BF16)

    rowb = lambda c, j: BS((R, c), lambda b, n: (b * nb + n, j))
    mat = BS((None, N_HEADS, nc, CHUNK, CHUNK), lambda b, n: (b, 0, n, 0, 0))
    return pl.pallas_call(
        body, name="gdn_chunk_fwd", grid=(B, nb),
        in_specs=[rowb(512, 0), rowb(512, 1), rowb(512, 2), rowb(128, 0),
                  BS((None, N_HEADS, nc, CHUNK), lambda b, n: (b, 0, n, 0))],
        out_specs=[rowb(512, 0), rowb(512, 0), mat, mat],
        out_shape=[jax.ShapeDtypeStruct((T, 512), F32), jax.ShapeDtypeStruct((T, 512), BF16),
                   jax.ShapeDtypeStruct((B, N_HEADS, N, CHUNK, CHUNK), BF16),
                   jax.ShapeDtypeStruct((B, N_HEADS, N, CHUNK, CHUNK), BF16)],
        compiler_params=_arb(2))(qkv, qkv, qkv, GB, Grow)


def gdn_scan_fwd(qkv3, U3, W3, GB3, A, B, S):
    N = S // CHUNK

    def body(q_ref, k_ref, u_ref, w_ref, gb_ref, a_ref, o_ref, vn_ref, st_ref, s_s):
        @pl.when(pl.program_id(0) == 0)
        def _():
            s_s[...] = jnp.zeros_like(s_s)

        for b in range(B):
            for h in range(N_HEADS):
                sl = slice(h * 128, (h + 1) * 128)
                st = s_s[b, h]
                st_ref[b, h] = st
                stb = st.astype(BF16)
                g = gb_ref[b, :, h:h + 1]
                gl = g[CHUNK - 1:CHUNK, :]
                qg = (q_ref[b, :, sl] * jnp.exp(g)).astype(BF16)
                on_state = _dot(jnp.concatenate([w_ref[b, :, sl].astype(BF16), qg], axis=0), stb, NN)
                vn = u_ref[b, :, sl] - on_state[:CHUNK]
                vnb = vn.astype(BF16)
                kd_t = jnp.transpose(k_ref[b, :, sl] * jnp.exp(gl - g)).astype(BF16)
                on_vn = _dot(jnp.concatenate([a_ref[b, h].astype(BF16), kd_t], axis=0), vnb, NN)
                vn_ref[b, :, sl] = vnb
                o_ref[b, :, sl] = (on_state[CHUNK:] + on_vn[:CHUNK]).astype(BF16)
                s_s[b, h] = st * jnp.exp(gl) + on_vn[CHUNK:]

    tok = lambda c, j: BS((B, CHUNK, c), lambda n: (0, n, j))
    return pl.pallas_call(
        body, name="gdn_scan_fwd", grid=(N,),
        in_specs=[tok(512, 0), tok(512, 1), tok(512, 0), tok(512, 0), tok(128, 0),
                  BS((B, N_HEADS, None, CHUNK, CHUNK), lambda n: (0, 0, n, 0, 0))],
        out_specs=[tok(512, 0), tok(512, 0), BS((B, N_HEADS, None, 128, 128), lambda n: (0, 0, n, 0, 0))],
        out_shape=[jax.ShapeDtypeStruct((B, S, 512), BF16), jax.ShapeDtypeStruct((B, S, 512), BF16),
                   jax.ShapeDtypeStruct((B, N_HEADS, N, 128, 128), F32)],
        scratch_shapes=[pltpu.VMEM((B, N_HEADS, 128, 128), F32)],
        compiler_params=_arb(1))(qkv3, qkv3, U3, W3, GB3, A)


def gdn_scan_bwd(dO3, qkv3, W3, Vn3, GB3, A, St, B, S):
    N = S // CHUNK

    def body(do_ref, q_ref, k_ref, w_ref, vn_ref, gb_ref, a_ref, st_ref,
             du_ref, dw_ref, dq_ref, dk_ref, da_ref, dg_ref, ds_s):
        @pl.when(pl.program_id(0) == 0)
        def _():
            ds_s[...] = jnp.zeros_like(ds_s)

        lane = lax.broadcasted_iota(jnp.int32, (1, 128), 1)
        last = lax.broadcasted_iota(jnp.int32, (CHUNK, 1), 0) == CHUNK - 1
        for b in range(B):
            dg_all = jnp.zeros((CHUNK, 128), F32)
            for h in range(N_HEADS):
                sl = slice(h * 128, (h + 1) * 128)
                st = st_ref[b, h]
                stb = st.astype(BF16)
                dsn = ds_s[b, h]
                dsnb = dsn.astype(BF16)
                g = gb_ref[b, :, h:h + 1]
                gl = g[CHUNK - 1:CHUNK, :]
                egl = jnp.exp(gl)
                ekd = jnp.exp(gl - g)
                eg = jnp.exp(g)
                q, k = q_ref[b, :, sl], k_ref[b, :, sl]
                kd = k * ekd
                qg = q * eg
                do = do_ref[b, :, sl].astype(BF16)
                vnb = vn_ref[b, :, sl].astype(BF16)
                dvn = _dot(a_ref[b, h].astype(BF16), do, TN) + _dot(kd.astype(BF16), dsnb, NN)
                dvnb = dvn.astype(BF16)
                do_on = _dot(do, jnp.concatenate([stb, vnb], axis=0), NT)
                dqg = do_on[:, :128]
                da_ref[b, h] = do_on[:, 128:]
                dkd = _dot(vnb, dsnb, NT)
                ds_s[b, h] = (_dot(qg.astype(BF16), do, TN) + egl * dsn - _dot(w_ref[b, :, sl].astype(BF16), dvnb, TN))
                du_ref[b, :, sl] = dvnb
                dw_ref[b, :, sl] = (-_dot(dvnb, stb, NT)).astype(BF16)
                dq_ref[b, :, sl] = dqg * eg
                dk_ref[b, :, sl] = dkd * ekd
                ddel = jnp.sum(dkd * kd, axis=1, keepdims=True)
                dgl = jnp.sum(ddel, axis=0, keepdims=True) + jnp.sum(jnp.sum(st * dsn, axis=1, keepdims=True), axis=0, keepdims=True) * egl
                col = jnp.sum(dqg * qg, axis=1, keepdims=True) - ddel + jnp.where(last, dgl, 0.0)
                dg_all = jnp.where(lane == h, col, dg_all)
            dg_ref[b] = dg_all

    tok = lambda c, j: BS((B, CHUNK, c), lambda n: (0, N - 1 - n, j))
    mat = lambda d: BS((B, N_HEADS, None, d, d), lambda n: (0, 0, N - 1 - n, 0, 0))
    return pl.pallas_call(
        body, name="gdn_scan_bwd", grid=(N,),
        in_specs=[tok(512, 0), tok(512, 0), tok(512, 1), tok(512, 0), tok(512, 0), tok(128, 0), mat(CHUNK), mat(128)],
        out_specs=[tok(512, 0), tok(512, 0), tok(512, 0), tok(512, 0), mat(CHUNK), tok(128, 0)],
        out_shape=[jax.ShapeDtypeStruct((B, S, 512), BF16)] * 2 + [jax.ShapeDtypeStruct((B, S, 512), F32)] * 2
        + [jax.ShapeDtypeStruct((B, N_HEADS, N, CHUNK, CHUNK), F32), jax.ShapeDtypeStruct((B, S, 128), F32)],
        scratch_shapes=[pltpu.VMEM((B, N_HEADS, 128, 128), F32)],
        compiler_params=_arb(1))(dO3, qkv3, qkv3, W3, Vn3, GB3, A, St)


def gdn_chunk_bwd(qkv, GB, Grow, Tinv, dA, dU, dW, dQ1, dK1, dG1, B, S, nc=8):
    T = B * S
    N = S // CHUNK
    nc = min(nc, N)
    nb = N // nc
    R = nc * CHUNK

    def body(q_ref, k_ref, v_ref, gb_ref, gr_ref, t_ref, da_ref, du_ref, dw_ref, dq1_ref, dk1_ref, dg1_ref, o_ref, dgb_ref):
        incl, strict = _chunk_masks(nc)
        lane = lax.broadcasted_iota(jnp.int32, (1, 128), 1)
        dg_all = dg1_ref[...]
        db_all = jnp.zeros((R, 128), F32)
        for h in range(N_HEADS):
            sl = slice(h * 128, (h + 1) * 128)
            q = q_ref[:, sl].reshape(nc, CHUNK, 128)
            k = k_ref[:, sl].reshape(nc, CHUNK, 128)
            v = v_ref[:, sl].reshape(nc, CHUNK, 128)
            gc = gb_ref[:, h:h + 1].reshape(nc, CHUNK, 1)
            beta = gb_ref[:, 4 + h:5 + h].reshape(nc, CHUNK, 1)
            gr = gr_ref[h][:, None, :]
            decay, kb, l_mat, a_mat = _chunk_local(q, k, gc, gr, beta, incl, strict)
            eg = jnp.exp(gc)
            kbg = kb * eg
            vb = v * beta
            tt = t_ref[h].astype(BF16)
            du = du_ref[:, sl].reshape(nc, CHUNK, 128).astype(BF16)
            dw = dw_ref[:, sl].reshape(nc, CHUNK, 128).astype(BF16)
            dvb = _bdot("gjc,gcv->gjv", tt, du)
            dkbg = _bdot("gjc,gck->gjk", tt, dw)
            dt = _bdot("gcv,gjv->gcj", du, vb.astype(BF16)) + _bdot("gck,gjk->gcj", dw, kbg.astype(BF16))
            tmp = _bdot("gca,gab->gcb", tt, dt.astype(BF16))
            dl = jnp.where(strict, -_bdot("gcb,gbd->gcd", tmp.astype(BF16), tt), 0.0)
            da = da_ref[h]
            dm = (dl * decay).astype(BF16)
            dqk = (da * decay).astype(BF16)
            kbf = k.astype(BF16)
            dkb = _bdot("gcj,gjd->gcd", dm, kbf) + dkbg * eg
            dk = (_bdot("gcj,gcd->gjd", dm, kb.astype(BF16)) + _bdot("gcj,gcd->gjd", dqk, q.astype(BF16))
                  + dk1_ref[:, sl].reshape(nc, CHUNK, 128) + dkb * beta)
            dq = _bdot("gcj,gjd->gcd", dqk, kbf) + dq1_ref[:, sl].reshape(nc, CHUNK, 128)
            e = dl * l_mat + da * a_mat
            dgc = (jnp.sum(e, axis=2, keepdims=True) - jnp.sum(jnp.swapaxes(e, 1, 2), axis=2, keepdims=True)
                   + jnp.sum(dkbg * kbg, axis=2, keepdims=True))
            dbeta = jnp.sum(dkb * k, axis=2, keepdims=True) + jnp.sum(dvb * v, axis=2, keepdims=True)
            o_ref[:, sl] = dq.reshape(R, 128)
            o_ref[:, 512 + h * 128:512 + (h + 1) * 128] = dk.reshape(R, 128)
            o_ref[:, 1024 + h * 128:1024 + (h + 1) * 128] = (dvb * beta).reshape(R, 128)
            dg_all = dg_all + jnp.where(lane == h, dgc.reshape(R, 1), 0.0)
            db_all = jnp.where(lane == 4 + h, dbeta.reshape(R, 1), db_all)
        t = _chunk_row(R)
        for s in (1, 2, 4, 8, 16, 32):
            dg_all = dg_all + jnp.where(t + s < CHUNK, pltpu.roll(dg_all, R - s, 0), 0.0)
        dgb_ref[...] = jnp.where(lane < 4, dg_all, db_all)

    rowb = lambda c, j: BS((R, c), lambda b, n: (b * nb + n, j))
    mat = BS((None, N_HEADS, nc, CHUNK, CHUNK), lambda b, n: (b, 0, n, 0, 0))
    return pl.pallas_call(
        body, name="gdn_chunk_bwd", grid=(B, nb),
        in_specs=[rowb(512, 0), rowb(512, 1), rowb(512, 2), rowb(128, 0),
                  BS((None, N_HEADS, nc, CHUNK), lambda b, n: (b, 0, n, 0)), mat, mat,
                  rowb(512, 0), rowb(512, 0), rowb(512, 0), rowb(512, 0), rowb(128, 0)],
        out_specs=[rowb(GDN_QKV, 0), rowb(128, 0)],
        out_shape=[jax.ShapeDtypeStruct((T, GDN_QKV), F32), jax.ShapeDtypeStruct((T, 128), F32)],
        compiler_params=_arb(2))(qkv, qkv, qkv, GB, Grow, Tinv, dA, dU, dW, dQ1, dK1, dG1)


def _gdn_out_norm(og, gg):
    outs, xhs, rs = [], [], []
    for h in range(N_HEADS):
        seg = og[:, h * 128:(h + 1) * 128]
        r = lax.rsqrt(jnp.mean(seg * seg, axis=-1, keepdims=True) + EPS)
        xh = seg * r
        outs.append(xh * gg)
        xhs.append(xh)
        rs.append(r)
    return outs, xhs, rs


def merge_fwd(o_mla, o_gdn, o_mem, P, x, tgt, w_out, g_gdn, g_fin, tm=256):
    T = x.shape[0]
    tm = min(tm, T)

    def body(om_ref, og_ref, oc_ref, gate_ref, x_ref, t_ref, w_ref, gg_ref, gf_ref, mix_ref, dx_ref, dxb_ref, sq_ref, gnf_ref):
        @pl.when(pl.program_id(0) == 0)
        def _():
            sq_ref[...] = jnp.zeros_like(sq_ref)
            gnf_ref[...] = jnp.zeros_like(gnf_ref)

        ogn, _, _ = _gdn_out_norm(og_ref[...].astype(F32), gg_ref[...])
        cat = jnp.concatenate([om_ref[...].astype(F32)] + ogn + [oc_ref[...].astype(F32)], axis=1)
        gt = gate_ref[...].astype(F32)
        mixed = (cat * (gt * _sigmoid(gt))).astype(BF16)
        mix_ref[...] = mixed
        x2 = x_ref[...] + _dot(mixed, w_ref[...], NN)
        r2 = lax.rsqrt(jnp.mean(x2 * x2, axis=-1, keepdims=True) + EPS)
        xh = x2 * r2
        gf = gf_ref[...]
        diff = xh * gf - t_ref[...]
        sq_ref[...] += jnp.sum(diff * diff, axis=0, keepdims=True)
        dy = diff * (1.0 / D_MODEL)
        gnf_ref[...] += jnp.sum(dy * xh, axis=0, keepdims=True)
        dxh = dy * gf
        dx = r2 * (dxh - xh * jnp.mean(dxh * xh, axis=-1, keepdims=True))
        dx_ref[...] = dx
        dxb_ref[...] = dx.astype(BF16)

    rowb = lambda c, j=0: BS((tm, c), lambda i: (i, j))
    full = lambda r, c: BS((r, c), lambda i: (0, 0))
    return pl.pallas_call(
        body, name="merge_fwd", grid=(T // tm,),
        in_specs=[rowb(512), rowb(512), rowb(512), rowb(D_MIX, OFF_GATE // D_MIX), rowb(D_MODEL), rowb(D_MODEL),
                  full(D_MIX, D_MODEL), full(1, 128), full(1, D_MODEL)],
        out_specs=[rowb(D_MIX), rowb(D_MODEL), rowb(D_MODEL), full(1, D_MODEL), full(1, D_MODEL)],
        out_shape=[jax.ShapeDtypeStruct((T, D_MIX), BF16), jax.ShapeDtypeStruct((T, D_MODEL), F32),
                   jax.ShapeDtypeStruct((T, D_MODEL), BF16),
                   jax.ShapeDtypeStruct((1, D_MODEL), F32), jax.ShapeDtypeStruct((1, D_MODEL), F32)],
        compiler_params=_arb(1))(o_mla, o_gdn, o_mem, P, x, tgt, w_out, g_gdn, g_fin)


def merge_bwd(dx2, o_mla, o_gdn, o_mem, P, w_out, g_gdn, tm=256):
    T = dx2.shape[0]
    tm = min(tm, T)

    def body(dx_ref, om_ref, og_ref, oc_ref, gate_ref, w_ref, gg_ref, dgate_ref, dom_ref, dog_ref, doc_ref, ggn_ref):
        @pl.when(pl.program_id(0) == 0)
        def _():
            ggn_ref[...] = jnp.zeros_like(ggn_ref)

        gg = gg_ref[...]
        dmix = _dot(dx_ref[...].astype(BF16), w_ref[...], NT)
        ogn, xhs, rs = _gdn_out_norm(og_ref[...].astype(F32), gg)
        cat = jnp.concatenate([om_ref[...].astype(F32)] + ogn + [oc_ref[...].astype(F32)], axis=1)
        gt = gate_ref[...].astype(F32)
        sg = _sigmoid(gt)
        dgate_ref[...] = (dmix * cat * (sg * (1.0 + gt * (1.0 - sg)))).astype(BF16)
        dcat = dmix * (gt * sg)
        dom_ref[...] = dcat[:, :512].astype(BF16)
        doc_ref[...] = dcat[:, 1024:].astype(BF16)
        acc = jnp.zeros((1, 128), F32)
        for h in range(N_HEADS):
            dseg = dcat[:, 512 + h * 128:512 + (h + 1) * 128]
            acc = acc + jnp.sum(dseg * xhs[h], axis=0, keepdims=True)
            dxh = dseg * gg
            dog_ref[:, h * 128:(h + 1) * 128] = (rs[h] * (dxh - xhs[h] * jnp.mean(dxh * xhs[h], axis=-1, keepdims=True))).astype(BF16)
        ggn_ref[...] += acc

    rowb = lambda c, j=0: BS((tm, c), lambda i: (i, j))
    full = lambda r, c: BS((r, c), lambda i: (0, 0))
    return pl.pallas_call(
        body, name="merge_bwd", grid=(T // tm,),
        in_specs=[rowb(D_MODEL), rowb(512), rowb(512), rowb(512), rowb(D_MIX, OFF_GATE // D_MIX),
                  full(D_MIX, D_MODEL), full(1, 128)],
        out_specs=[rowb(D_MIX), rowb(512), rowb(512), rowb(512), full(1, 128)],
        out_shape=[jax.ShapeDtypeStruct((T, D_MIX), BF16)] + [jax.ShapeDtypeStruct((T, 512), BF16)] * 3
        + [jax.ShapeDtypeStruct((1, 128), F32)],
        compiler_params=_arb(1))(dx2, o_mla, o_gdn, o_mem, P, w_out, g_gdn)


def in_proj_bwd(dP, wp, x, dx2, gain, after, tm=512):
    T, n = x.shape
    tm = min(tm, T)
    k = len(dP)
    widths = [p.shape[1] for p in dP]
    offs = [sum(widths[:i]) for i in range(k)]

    def body(*refs):
        w_ref, x_ref, dx2_ref, g_ref = refs[k:k + 4]
        o_ref, acc_ref = refs[-2:]

        @pl.when(pl.program_id(0) == 0)
        def _():
            acc_ref[...] = jnp.zeros_like(acc_ref)

        dy = None
        for a_ref, off, w in zip(refs[:k], offs, widths):
            d = _dot(a_ref[...], w_ref[off:off + w, :], NN)
            dy = d if dy is None else dy + d
        xv = x_ref[...]
        r = lax.rsqrt(jnp.mean(xv * xv, axis=-1, keepdims=True) + EPS)
        xh = xv * r
        acc_ref[...] += jnp.sum(dy * xh, axis=0, keepdims=True)
        dxh = dy * g_ref[...]
        o_ref[...] = dx2_ref[...] + r * (dxh - xh * jnp.mean(dxh * xh, axis=-1, keepdims=True))

    rowb = BS((tm, n), lambda i: (i, 0))
    full = BS((1, n), lambda i: (0, 0))
    return pl.pallas_call(
        body, name="in_proj_bwd", grid=(T // tm,),
        in_specs=[BS((tm, w), lambda i: (i, 0)) for w in widths]
        + [BS(wp.shape, lambda i: (0, 0), pipeline_mode=pl.Buffered(1)), rowb, rowb, full, BS(memory_space=pl.ANY)],
        out_specs=[rowb, full], out_shape=[jax.ShapeDtypeStruct((T, n), F32), jax.ShapeDtypeStruct((1, n), F32)],
        compiler_params=_arb(1))(*dP, wp, x, dx2, gain, after)


W_IN_SHARD = D_IN // 4
_GDN0 = Q_LORA + KV_LORA + MLA_ROPE
_AB0 = _GDN0 + GDN_QKV
_MEMQ0 = _AB0 + 2 * N_HEADS
_GATE0 = _MEMQ0 + N_HEADS * MEM_DH


def _w_in_row_map():
    a, m, gt = _AB0 - 2 * W_IN_SHARD, _MEMQ0 - 2 * W_IN_SHARD, _GATE0 - 2 * W_IN_SHARD
    e0 = OFF_GDN + W_IN_SHARD - _GDN0
    e1 = e0 + W_IN_SHARD
    e2 = OFF_GATE + W_IN_SHARD - gt
    return [(0, 0, 0, 672), (0, 672, 704, 32), (2, a, 768, m - a), (2, m, OFF_MEMQ, gt - m), (0, _GDN0, OFF_GDN, W_IN_SHARD - _GDN0),
            (1, 0, e0, W_IN_SHARD), (2, 0, e1, a), (2, gt, OFF_GATE, W_IN_SHARD - gt), (3, 0, e2, W_IN_SHARD)]


_W_IN_ZERO_ROWS = [(672, 32), (736, 32), (776, 248)]
W_IN_LANES = 256


def pad_w_in_t(shards):
    per_half = shards.shape[3] // W_IN_LANES

    def body(s_ref, o_ref):
        for r0, n in _W_IN_ZERO_ROWS:
            o_ref[r0:r0 + n, :] = jnp.zeros((n, W_IN_LANES), o_ref.dtype)
        for q, src, dst, n in _w_in_row_map():
            o_ref[dst:dst + n, :] = s_ref[q, src:src + n, :]

    return pl.pallas_call(
        body, name="pad_w_in_t", grid=(D_MODEL // W_IN_LANES,),
        in_specs=[BS((N_CHIPS, None, W_IN_SHARD, W_IN_LANES), lambda j: (0, j // per_half, 0, j % per_half))],
        out_specs=BS((N_PAD, W_IN_LANES), lambda j: (0, j)),
        out_shape=jax.ShapeDtypeStruct((N_PAD, D_MODEL), shards.dtype), compiler_params=_arb(1))(shards)


def unpad_w_in_t(g):
    def body(g_ref, o_ref):
        for q, src, dst, n in _w_in_row_map():
            o_ref[q, src:src + n, :] = g_ref[dst:dst + n, :]

    return pl.pallas_call(
        body, name="unpad_w_in_t", grid=(D_MODEL // W_IN_LANES,),
        in_specs=[BS((N_PAD, W_IN_LANES), lambda j: (0, j))], out_specs=BS((N_CHIPS, W_IN_SHARD, W_IN_LANES), lambda j: (0, 0, j)),
        out_shape=jax.ShapeDtypeStruct((N_CHIPS, W_IN_SHARD, D_MODEL), g.dtype), compiler_params=_arb(1))(g)


def _perm_w_kv_b(s):
    return jnp.concatenate([s[h, :, :128] for h in range(N_HEADS)] + [s[h, :, 128:] for h in range(N_HEADS)], axis=1)


def _unperm_w_kv_b(g):
    return jnp.stack([jnp.concatenate([g[:, h * 128:(h + 1) * 128], g[:, 512 + h * 128:512 + (h + 1) * 128]], axis=1)
                      for h in range(N_HEADS)])


def _lane_row(v4):
    return jnp.pad(v4.reshape(1, -1).astype(F32), ((0, 0), (0, 128 - v4.size)))


N_CHIPS = 4
MESH = pl.DeviceIdType.MESH
ANY = BS(memory_space=pl.ANY)


def _place():
    return lax.axis_index("x"), lax.axis_index("y"), lax.axis_index("c")


def _other_chips(x, y):
    return [(1 - x, y), (x, 1 - y), (1 - x, 1 - y)]


def _half(split, which):
    axis, size = split
    ds = pl.ds(pl.multiple_of(which * size, 16 if axis == 0 else 128), size)
    return (ds, slice(None)) if axis == 0 else (slice(None), ds)


SEM = BS(memory_space=pltpu.SEMAPHORE)
HBM = BS(memory_space=pltpu.HBM)
_IN_HBM = lambda a: pltpu.with_memory_space_constraint(a, pltpu.HBM)
_SIDE_EFFECT = pltpu.SideEffectType.DATAFLOW_SIDE_EFFECTING


def _late_gather_copies(s_refs, l_refs, send_sems, recv_sems, local_sems, with_arrivals):
    x, y, c = _place()
    sends, recvs, locals_ = [], [], []
    for i, (s_ref, l_ref) in enumerate(zip(s_refs, l_refs)):
        locals_.append(pltpu.make_async_copy(s_ref, l_ref.at[2 * x + y], local_sems.at[i]))
        for j, (px, py) in enumerate(_other_chips(x, y)):
            k = 3 * i + j
            sends.append(pltpu.make_async_remote_copy(src_ref=s_ref, dst_ref=l_ref.at[2 * x + y], send_sem=send_sems.at[k],
                                                      recv_sem=recv_sems.at[k], device_id=(px, py, c), device_id_type=MESH))
            if with_arrivals:
                recvs.append(pltpu.make_async_remote_copy(src_ref=s_ref, dst_ref=l_ref.at[2 * px + py], send_sem=send_sems.at[k],
                                                          recv_sem=recv_sems.at[k], device_id=(px, py, c), device_id_type=MESH))
    return sends, recvs, locals_


def late_gather_start(shards, after, name):
    n = len(shards)

    def body(*refs):
        s_refs, l_refs = refs[:n], refs[n:2 * n]
        send_sems, recv_sems, local_sems = refs[2 * n + 1:2 * n + 4]
        token = refs[-1]
        sends, _, locals_ = _late_gather_copies(s_refs, l_refs, send_sems, recv_sems, local_sems, False)
        for cp in locals_ + sends:
            cp.start()
        token[...] = jnp.zeros_like(token)

    lands = [lax.empty((N_CHIPS,) + s.shape, s.dtype) for s in shards]
    hbm_like = lambda a: pltpu.HBM(a.shape, a.dtype)
    out = pl.pallas_call(
        body, name=name,
        out_shape=[pltpu.SemaphoreType.DMA((3 * n,)), pltpu.SemaphoreType.DMA((3 * n,)), pltpu.SemaphoreType.DMA((n,))]
        + [hbm_like(s) for s in shards] + [hbm_like(l) for l in lands] + [jax.ShapeDtypeStruct((8, 128), F32)],
        in_specs=[HBM] * (2 * n) + [BS(memory_space=pl.ANY)], out_specs=[SEM] * 3 + [HBM] * (2 * n) + [BS(memory_space=pltpu.VMEM)],
        input_output_aliases={i: 3 + i for i in range(2 * n)},
        compiler_params=pltpu.CompilerParams(has_side_effects=_SIDE_EFFECT))(
            *[_IN_HBM(s) for s in shards], *[_IN_HBM(l) for l in lands], after)
    return out[:3], out[3:3 + n], out[3 + n:3 + 2 * n], out[-1]


def late_gather_wait(sems, shards, lands, after, name):
    n = len(shards)

    def body(*refs):
        s_refs, l_refs = refs[:n], refs[n:2 * n]
        send_sems, recv_sems, local_sems = refs[2 * n:2 * n + 3]
        sends, recvs, locals_ = _late_gather_copies(s_refs, l_refs, send_sems, recv_sems, local_sems, True)
        for cp in locals_:
            cp.wait()
        for cp in sends:
            cp.wait_send()
        for cp in recvs:
            cp.wait_recv()

    hbm_like = lambda a: pltpu.HBM(a.shape, a.dtype)
    out = pl.pallas_call(
        body, name=name, out_shape=[hbm_like(s) for s in shards] + [hbm_like(l) for l in lands],
        in_specs=[HBM] * (2 * n) + [SEM] * 3 + [BS(memory_space=pl.ANY)], out_specs=[HBM] * (2 * n),
        input_output_aliases={i: i for i in range(2 * n)},
        compiler_params=pltpu.CompilerParams(has_side_effects=_SIDE_EFFECT))(*shards, *lands, *sems, after)
    return out[n:]


def _half_gather_copies(s_ref, l_ref, send_sems, recv_sems, with_arrivals):
    x, y, c = _place()
    sends, recvs = [], []
    for j, (px, py) in enumerate(_other_chips(x, y)):
        sends.append(pltpu.make_async_remote_copy(src_ref=s_ref.at[c], dst_ref=l_ref.at[2 * x + y, c], send_sem=send_sems.at[j],
                                                  recv_sem=recv_sems.at[j], device_id=(px, py, c), device_id_type=MESH))
        if with_arrivals:
            recvs.append(pltpu.make_async_remote_copy(src_ref=s_ref.at[c], dst_ref=l_ref.at[2 * px + py, c], send_sem=send_sems.at[j],
                                                      recv_sem=recv_sems.at[j], device_id=(px, py, c), device_id_type=MESH))
    return sends, recvs


def half_gather_start(shard, name):
    def body(s_ref, l_ref, send_sems, recv_sems, local_sem, s_thru, l_thru, token):
        x, y, _ = _place()
        pltpu.make_async_copy(s_ref, l_ref.at[2 * x + y], local_sem.at[0]).start()
        for cp in _half_gather_copies(s_ref, l_ref, send_sems, recv_sems, False)[0]:
            cp.start()
        token[...] = jnp.zeros_like(token)

    land = lax.empty((N_CHIPS,) + shard.shape, shard.dtype)
    out = pl.pallas_call(
        body, name=name,
        out_shape=[pltpu.SemaphoreType.DMA((3,)), pltpu.SemaphoreType.DMA((3,)), pltpu.SemaphoreType.DMA((1,)),
                   pltpu.HBM(shard.shape, shard.dtype), pltpu.HBM(land.shape, land.dtype), jax.ShapeDtypeStruct((8, 128), F32)],
        in_specs=[HBM, HBM], out_specs=[SEM] * 3 + [HBM, HBM, BS(memory_space=pltpu.VMEM)],
        input_output_aliases={0: 3, 1: 4},
        compiler_params=pltpu.CompilerParams(has_side_effects=_SIDE_EFFECT))(_IN_HBM(shard), _IN_HBM(land))
    return out[:3], out[3], out[4], out[5]


def half_gather_wait(sems, shard, land, after, name):
    def body(s_ref, l_ref, send_sems, recv_sems, local_sem, *rest):
        x, y, _ = _place()
        pltpu.make_async_copy(s_ref, l_ref.at[2 * x + y], local_sem.at[0]).wait()
        sends, recvs = _half_gather_copies(s_ref, l_ref, send_sems, recv_sems, True)
        for cp in sends:
            cp.wait_send()
        for cp in recvs:
            cp.wait_recv()

    out = pl.pallas_call(
        body, name=name, out_shape=[pltpu.HBM(shard.shape, shard.dtype), pltpu.HBM(land.shape, land.dtype)],
        in_specs=[HBM, HBM] + [SEM] * 3 + [BS(memory_space=pl.ANY)] * len(after), out_specs=[HBM, HBM],
        input_output_aliases={0: 0, 1: 1},
        compiler_params=pltpu.CompilerParams(has_side_effects=_SIDE_EFFECT))(shard, land, *sems, *after)
    return out[1]


def pass_halves_to_sibling(land, name):
    def body(l_in, l_ref, send_sems, recv_sems):
        x, y, c = _place()
        copies = []
        for j, (px, py) in enumerate(_other_chips(x, y)):
            q = 2 * px + py
            give = pltpu.make_async_remote_copy(src_ref=l_ref.at[q, c], dst_ref=l_ref.at[q, c], send_sem=send_sems.at[j],
                                                recv_sem=recv_sems.at[j], device_id=(x, y, 1 - c), device_id_type=MESH)
            take = pltpu.make_async_remote_copy(src_ref=l_ref.at[q, c], dst_ref=l_ref.at[q, 1 - c], send_sem=send_sems.at[j],
                                                recv_sem=recv_sems.at[j], device_id=(x, y, 1 - c), device_id_type=MESH)
            give.start()
            copies.append((give, take))
        for give, take in copies:
            take.wait_recv()
            give.wait_send()

    return pl.pallas_call(
        body, name=name, in_specs=[ANY], out_specs=ANY, out_shape=jax.ShapeDtypeStruct(land.shape, land.dtype),
        input_output_aliases={0: 0},
        scratch_shapes=[pltpu.SemaphoreType.DMA((3,)), pltpu.SemaphoreType.DMA((3,))])(land)


def allgather_devices(block, name):
    R, C = block.shape

    def body(b_ref, o_ref, send_sems, recv_sems, local_sem):
        x, y, c = _place()
        me = 4 * x + 2 * y + c
        own = pltpu.make_async_copy(b_ref, o_ref.at[me], local_sem)
        own.start()
        copies = []
        for r in range(1, 8):
            px = 1 - x if r & 4 else x
            py = 1 - y if r & 2 else y
            pc = 1 - c if r & 1 else c
            send = pltpu.make_async_remote_copy(src_ref=b_ref, dst_ref=o_ref.at[me], send_sem=send_sems.at[r - 1],
                                                recv_sem=recv_sems.at[r - 1], device_id=(px, py, pc), device_id_type=MESH)
            recv = pltpu.make_async_remote_copy(src_ref=b_ref, dst_ref=o_ref.at[4 * px + 2 * py + pc], send_sem=send_sems.at[r - 1],
                                                recv_sem=recv_sems.at[r - 1], device_id=(px, py, pc), device_id_type=MESH)
            send.start()
            copies.append((send, recv))
        for send, recv in copies:
            recv.wait_recv()
            send.wait_send()
        own.wait()

    return pl.pallas_call(
        body, name=name, in_specs=[ANY], out_specs=ANY, out_shape=jax.ShapeDtypeStruct((8, R, C), block.dtype),
        scratch_shapes=[pltpu.SemaphoreType.DMA((7,)), pltpu.SemaphoreType.DMA((7,)), pltpu.SemaphoreType.DMA(())])(block)


def swap_sibling(arrs, name, splits=None):
    n = len(arrs)

    def sent(a_ref, i, c):
        return a_ref if splits is None else a_ref.at[(slice(None),) + _half(splits[i], 1 - c)]

    def out_shape(a, i):
        if splits is None:
            return a.shape
        axis, size = splits[i]
        return (a.shape[0], size, a.shape[2]) if axis == 0 else (a.shape[0], a.shape[1], size)

    def body(*refs):
        a_refs, o_refs = refs[:n], refs[n:2 * n]
        send_sems, recv_sems = refs[2 * n:]
        x, y, c = _place()
        copies = [pltpu.make_async_remote_copy(src_ref=sent(a_ref, i, c), dst_ref=o_ref, send_sem=send_sems.at[i],
                                               recv_sem=recv_sems.at[i], device_id=(x, y, 1 - c), device_id_type=MESH)
                  for i, (a_ref, o_ref) in enumerate(zip(a_refs, o_refs))]
        for cp in copies:
            cp.start()
        for cp in copies:
            cp.wait()

    return pl.pallas_call(
        body, name=name, in_specs=[ANY] * n, out_specs=[ANY] * n,
        out_shape=[jax.ShapeDtypeStruct(out_shape(a, i), a.dtype) for i, a in enumerate(arrs)],
        scratch_shapes=[pltpu.SemaphoreType.DMA((n,)), pltpu.SemaphoreType.DMA((n,))])(*arrs)


def _exchange_copies(p_refs, l_refs, send_sems, recv_sems):
    x, y, c = _place()
    return [pltpu.make_async_remote_copy(src_ref=p_ref.at[2 * px + py], dst_ref=l_ref.at[j], send_sem=send_sems.at[3 * i + j],
                                         recv_sem=recv_sems.at[3 * i + j], device_id=(px, py, c), device_id_type=MESH)
            for i, (p_ref, l_ref) in enumerate(zip(p_refs, l_refs)) for j, (px, py) in enumerate(_other_chips(x, y))]


def exchange_chips_start(parts, name):
    n = len(parts)

    def body(*refs):
        send_sems, recv_sems = refs[2 * n:2 * n + 2]
        for cp in _exchange_copies(refs[:n], refs[n:2 * n], send_sems, recv_sems):
            cp.start()
        refs[-1][...] = jnp.zeros_like(refs[-1])

    lands = [lax.empty((3,) + p.shape[1:], p.dtype) for p in parts]
    hbm_like = lambda a: pltpu.HBM(a.shape, a.dtype)
    out = pl.pallas_call(
        body, name=name,
        out_shape=[pltpu.SemaphoreType.DMA((3 * n,)), pltpu.SemaphoreType.DMA((3 * n,))]
        + [hbm_like(p) for p in parts] + [hbm_like(l) for l in lands] + [jax.ShapeDtypeStruct((8, 128), F32)],
        in_specs=[HBM] * (2 * n), out_specs=[SEM] * 2 + [HBM] * (2 * n) + [BS(memory_space=pltpu.VMEM)],
        input_output_aliases={i: 2 + i for i in range(2 * n)},
        compiler_params=pltpu.CompilerParams(has_side_effects=_SIDE_EFFECT))(*[_IN_HBM(p) for p in parts], *[_IN_HBM(l) for l in lands])
    return out[:2], out[2:2 + n], out[2 + n:2 + 2 * n], out[-1]


def exchange_chips_wait(sems, parts, lands, after, name):
    n = len(parts)

    def body(*refs):
        send_sems, recv_sems = refs[2 * n:2 * n + 2]
        for cp in _exchange_copies(refs[:n], refs[n:2 * n], send_sems, recv_sems):
            cp.wait_send()
            cp.wait_recv()

    hbm_like = lambda a: pltpu.HBM(a.shape, a.dtype)
    out = pl.pallas_call(
        body, name=name, out_shape=[hbm_like(p) for p in parts] + [hbm_like(l) for l in lands],
        in_specs=[HBM] * (2 * n) + [SEM] * 2 + [BS(memory_space=pl.ANY)], out_specs=[HBM] * (2 * n),
        input_output_aliases={i: i for i in range(2 * n)},
        compiler_params=pltpu.CompilerParams(has_side_effects=_SIDE_EFFECT))(*parts, *lands, *sems, after)
    return out[n:]


def _half_block(shape2, split):
    axis, size = split
    return (size, shape2[1]) if axis == 0 else (shape2[0], size)


def add_pairs(parts, halves, splits, core, name):
    n = len(parts)

    def body(s_ref, *refs):
        for a_ref, b_ref, o_ref in zip(refs[:n], refs[n:2 * n], refs[2 * n:]):
            o_ref[...] = (a_ref[...].astype(F32) + b_ref[...].astype(F32)).astype(BF16)

    def mine(i):
        blk = (None,) + _half_block(parts[i].shape[1:], splits[i])
        if splits[i][0] == 0:
            return BS(blk, lambda q, s: (q, s[0], 0))
        return BS(blk, lambda q, s: (q, 0, s[0]))

    half_specs = [BS((None,) + h.shape[1:], lambda q, s: (q, 0, 0)) for h in halves]
    return pl.pallas_call(
        body, name=name,
        grid_spec=pltpu.PrefetchScalarGridSpec(num_scalar_prefetch=1, grid=(N_CHIPS,),
                                               in_specs=[mine(i) for i in range(n)] + half_specs, out_specs=half_specs),
        out_shape=[jax.ShapeDtypeStruct(h.shape, BF16) for h in halves], compiler_params=_arb(1))(core, *parts, *halves)


def add_fives(parts, halves, from_chips, splits, chip_core, name):
    n = len(parts)

    def body(s_ref, *refs):
        for a_ref, b_ref, p_ref, o_ref in zip(refs[:n], refs[n:2 * n], refs[2 * n:3 * n], refs[3 * n:]):
            s = a_ref[...].astype(F32) + b_ref[...].astype(F32)
            for j in range(3):
                s = s + p_ref[j].astype(F32)
            o_ref[...] = s

    def mine(i):
        blk = (None,) + _half_block(parts[i].shape[1:], splits[i])
        if splits[i][0] == 0:
            return BS(blk, lambda g, s: (s[0], s[1], 0))
        return BS(blk, lambda g, s: (s[0], 0, s[1]))

    half_specs = [BS((None,) + h.shape[1:], lambda g, s: (s[0], 0, 0)) for h in halves]
    chip_specs = [BS(p.shape, lambda g, s: (0, 0, 0)) for p in from_chips]
    out_specs = [BS(h.shape[1:], lambda g, s: (0, 0)) for h in halves]
    return pl.pallas_call(
        body, name=name,
        grid_spec=pltpu.PrefetchScalarGridSpec(num_scalar_prefetch=1, grid=(1,),
                                               in_specs=[mine(i) for i in range(n)] + half_specs + chip_specs, out_specs=out_specs),
        out_shape=[jax.ShapeDtypeStruct(h.shape[1:], F32) for h in halves], compiler_params=_arb(1))(chip_core, *parts, *halves, *from_chips)


def sum_leading(a, name):
    def body(a_ref, o_ref):
        s = a_ref[0]
        for j in range(1, a.shape[0]):
            s = s + a_ref[j]
        o_ref[...] = s

    return pl.pallas_call(body, name=name, out_shape=jax.ShapeDtypeStruct(a.shape[1:], a.dtype))(a)


def _adamw_math(w, g, m, v):
    mn = ADAM_B1 * m + (1.0 - ADAM_B1) * g
    vn = ADAM_B2 * v + (1.0 - ADAM_B2) * (g * g)
    m_hat = mn / (1.0 - ADAM_B1 ** ADAM_STEP)
    v_hat = vn / (1.0 - ADAM_B2 ** ADAM_STEP)
    return -ADAM_LR * (m_hat / (jnp.sqrt(v_hat) + ADAM_EPS) + ADAM_WD * w), mn, vn


def adamw(w, g, m, v, name):
    R, C = g.shape
    lead = (None,) * (w.ndim - 2)

    def body(w_ref, g_ref, m_ref, v_ref, d_ref, mo_ref, vo_ref):
        d_ref[...], mo_ref[...], vo_ref[...] = _adamw_math(w_ref[...], g_ref[...], m_ref[...], v_ref[...])

    wblk = BS(lead + (R, C), lambda i: (0,) * w.ndim)
    gblk = BS((R, C), lambda i: (0, 0))
    return pl.pallas_call(
        body, name=name, grid=(1,), in_specs=[wblk, gblk, wblk, wblk], out_specs=[wblk] * 3,
        out_shape=[jax.ShapeDtypeStruct(w.shape, F32)] * 3, compiler_params=_arb(1))(w, g, m, v)


SMALL_ROWS = 16
CONV_ROW0 = 8
LOSS_ROW = 14


def pack_small(small_grads, g_ab, g_conv, sq):
    present = [a for a in small_grads if a is not None]

    def body(*refs):
        ab_ref, conv_ref, sq_ref, o_ref = refs[len(present):]
        o_ref[...] = jnp.zeros_like(o_ref)
        it = iter(refs[:len(present)])
        for i, a in enumerate(small_grads):
            if a is not None:
                o_ref[i:i + 1, 0:a.shape[1]] = next(it)[...]
        o_ref[3:4, 0:128] = ab_ref[0:1, :]
        o_ref[4:5, 0:128] = ab_ref[1:2, :]
        half = 512
        for k in range(GDN_CONV * GDN_QKV // half):
            src_r, src_c = (k * half) // GDN_QKV, (k * half) % GDN_QKV
            dst_r, dst_c = CONV_ROW0 + (k * half) // 1024, (k * half) % 1024
            o_ref[dst_r:dst_r + 1, dst_c:dst_c + half] = conv_ref[src_r:src_r + 1, src_c:src_c + half]
        o_ref[LOSS_ROW:LOSS_ROW + 1, :] = sq_ref[...]

    return pl.pallas_call(body, name="pack_small", out_shape=jax.ShapeDtypeStruct((SMALL_ROWS, 1024), F32))(
        *present, g_ab, g_conv, sq)


def adamw_small(block, ws, ms, vs):
    k = len(ws)

    def body(b_ref, *refs):
        outs = refs[3 * k:]
        for i in range(k):
            n = ws[i].shape[1]
            g = b_ref[i:i + 1, 0:n]
            d, mn, vn = _adamw_math(refs[i][...], g, refs[k + i][...], refs[2 * k + i][...])
            outs[4 * i][...], outs[4 * i + 1][...], outs[4 * i + 2][...], outs[4 * i + 3][...] = g, d, mn, vn

    out = pl.pallas_call(
        body, name="adamw_small",
        out_shape=[jax.ShapeDtypeStruct(w.shape, F32) for w in ws for _ in range(4)])(block, *ws, *ms, *vs)
    return [out[4 * i:4 * i + 4] for i in range(k)]


def adamw_halves(w, mine, other, m, v, split, core, name):
    R, C = w.shape[-2:]
    axis, size = split
    lead = (None,) * (w.ndim - 2)
    zeros = (0,) * (w.ndim - 2)
    if axis == 0:
        tr = size if size <= 256 else next(t for t in range(256, 7, -1) if size % t == 0 and t % 8 == 0)
        nb = size // tr
        whole = BS(lead + (tr, C), lambda hi, j, s: zeros + (hi * nb + j, 0))
        part = BS((tr, C), lambda hi, j, s: (j, 0))
    else:
        nb = size // 128
        whole = BS(lead + (R, 128), lambda hi, j, s: zeros + (0, hi * nb + j))
        part = BS((R, 128), lambda hi, j, s: (0, j))

    def body(s_ref, w_ref, a_ref, b_ref, m_ref, v_ref, g_ref, d_ref, mo_ref, vo_ref):
        g = jnp.where(pl.program_id(0) == s_ref[0], a_ref[...], b_ref[...])
        g_ref[...] = g
        d_ref[...], mo_ref[...], vo_ref[...] = _adamw_math(w_ref[...], g, m_ref[...], v_ref[...])

    return pl.pallas_call(
        body, name=name,
        grid_spec=pltpu.PrefetchScalarGridSpec(num_scalar_prefetch=1, grid=(2, nb),
                                               in_specs=[whole, part, part, whole, whole], out_specs=[whole] * 4),
        out_shape=[jax.ShapeDtypeStruct(w.shape, F32)] * 4, compiler_params=_arb(2))(core, w, mine, other, m, v)


def dense_bf16(w3, name):
    R, _, K = w3.shape
    kh = K // 2

    def body(w_hbm, o_ref, buf, sem):
        cp = pltpu.make_async_copy(w_hbm.at[:, 0], buf, sem)
        cp.start()
        cp.wait()
        o_ref[0] = buf[:, :kh].astype(BF16)
        o_ref[1] = buf[:, kh:].astype(BF16)

    return pl.pallas_call(
        body, name=name, in_specs=[ANY], out_specs=BS(memory_space=pltpu.VMEM), out_shape=jax.ShapeDtypeStruct((2, R, kh), BF16),
        scratch_shapes=[pltpu.VMEM((R, K), F32), pltpu.SemaphoreType.DMA(())])(w3)


ROW_BLOCK = 184


def adamw_untiled_rows(w3, mine, other, m3, v3, name):
    R, _, K = w3.shape
    kh = K // 2
    starts = list(range(0, R, ROW_BLOCK))
    sizes = [min(ROW_BLOCK, R - s) for s in starts]
    nblk = len(starts)

    def body(w_hbm, a_ref, b_ref, m_hbm, v_hbm, g_hbm, d_hbm, mo_hbm, vo_hbm,
             wbuf, mbuf, vbuf, gbuf, dbuf, mobuf, vobuf, in_sems, out_sems):
        first = lax.axis_index("c") == 0
        ins = []
        for k, (r0, n) in enumerate(zip(starts, sizes)):
            rows = pl.ds(r0, n)
            cps = [pltpu.make_async_copy(src.at[rows, 0], dst.at[rows], in_sems.at[3 * k + i])
                   for i, (src, dst) in enumerate(((w_hbm, wbuf), (m_hbm, mbuf), (v_hbm, vbuf)))]
            for cp in cps:
                cp.start()
            ins.append(cps)

        def update(rows):
            a, b = a_ref[rows, :], b_ref[rows, :]
            g = jnp.concatenate([jnp.where(first, a, b), jnp.where(first, b, a)], axis=1)
            gbuf[rows, :] = g
            dbuf[rows, :], mobuf[rows, :], vobuf[rows, :] = _adamw_math(wbuf[rows, :], g, mbuf[rows, :], vbuf[rows, :])

        outs = []
        for k, (r0, n) in enumerate(zip(starts, sizes)):
            for cp in ins[k]:
                cp.wait()
            groups, tail = n // 8, n % 8

            def group(i, carry, r0=r0):
                update(pl.ds(pl.multiple_of(r0 + i * 8, 8), 8))
                return carry

            lax.fori_loop(0, groups, group, 0)
            if tail:
                update(pl.ds(r0 + groups * 8, tail))
            rows = pl.ds(r0, n)
            cps = [pltpu.make_async_copy(src.at[rows], dst.at[rows, 0], out_sems.at[4 * k + i])
                   for i, (src, dst) in enumerate(((gbuf, g_hbm), (dbuf, d_hbm), (mobuf, mo_hbm), (vobuf, vo_hbm)))]
            for cp in cps:
                cp.start()
            outs += cps
        for cp in outs:
            cp.wait()

    vmem = BS(memory_space=pltpu.VMEM)
    return pl.pallas_call(
        body, name=name, in_specs=[ANY, vmem, vmem, ANY, ANY], out_specs=[ANY] * 4,
        out_shape=[jax.ShapeDtypeStruct(w3.shape, F32)] * 4,
        scratch_shapes=[pltpu.VMEM((R, K), F32)] * 7 + [pltpu.SemaphoreType.DMA((3 * nblk,)), pltpu.SemaphoreType.DMA((4 * nblk,))])(
            w3, mine, other, m3, v3)


def adamw_w_q_b(w, mine, other, m, v, name):
    def body(w_ref, a_ref, b_ref, m_ref, v_ref, g_ref, d_ref, mo_ref, vo_ref):
        first = lax.axis_index("c") == 0
        lo = jnp.where(first, a_ref[...], b_ref[...])
        hi = jnp.where(first, b_ref[...], a_ref[...])
        g = jnp.concatenate([lo, hi[0:32], hi[64:96]], axis=0)
        g_ref[...] = g
        d_ref[...], mo_ref[...], vo_ref[...] = _adamw_math(w_ref[...], g, m_ref[...], v_ref[...])

    return pl.pallas_call(body, name=name, out_shape=[jax.ShapeDtypeStruct(w.shape, F32)] * 4)(w, mine, other, m, v)


def local_step(x, mem, positions, tgt, norm_in, weights, big_grads_ready, q_a_norm, kv_a_norm, gdn_conv, gdn_a_log,
               gdn_dt_bias, gdn_norm, mem_norm, norm_final):
    B, S, D = x.shape
    M = mem.shape[1]
    T = B * S
    N = S // CHUNK
    x2d = x.reshape(T, D)
    mem2d = mem.reshape(B * M, D)
    tgt2d = tgt.reshape(T, D)

    alog_row, dt_row = _lane_row(gdn_a_log), _lane_row(gdn_dt_bias)

    half = MLA_ROPE // 2
    inv_freq = 1.0 / (ROPE_THETA ** (jnp.arange(half, dtype=F32) / half))
    z32 = jnp.zeros((half,), F32)
    o32 = jnp.ones((half,), F32)
    inv_row = jnp.concatenate([inv_freq, z32, inv_freq, z32]).reshape(1, 128)
    sgn_row = jnp.concatenate([-o32, z32, o32, z32]).reshape(1, 128)
    msk_row = jnp.concatenate([o32, z32, o32, z32]).reshape(1, 128)
    cos_t, sin_t = rope_tables(positions.reshape(T, 1), inv_row, sgn_row, msk_row, after=weights[3])

    h = rms_fwd(x2d, norm_in, "rms_in", after=weights[3])
    wp, behind = weights[0]((h, cos_t))
    P = mm(h, wp, "nt", BF16, "in_proj", bm=512, bn=1536, n_outer=True, after=behind)
    wq, wkv = weights[1](P)
    Q, K, V, qn, kvn = mla_prep(P, q_a_norm, kv_a_norm, wq, wkv, cos_t, sin_t)
    o_mla, lse = mla_attn_fwd(Q, K, V, B, S)
    qkv = gdn_prep_fwd(P, gdn_conv, B, S)
    GB = gdn_gate_fwd(P, alog_row, dt_row, B, S)
    Grow = jnp.transpose(GB[:, :N_HEADS].reshape(B, N, CHUNK, N_HEADS), (0, 3, 1, 2))
    U, W, Tinv, A = gdn_chunk_fwd(qkv, GB, Grow, B, S)
    qkv3, GB3 = qkv.reshape(B, S, GDN_QKV), GB.reshape(B, S, 128)
    W3 = W.reshape(B, S, 512)
    o_gdn3, Vn3, St = gdn_scan_fwd(qkv3, U.reshape(B, S, 512), W3, GB3, A, B, S)
    o_gdn = o_gdn3.reshape(T, 512)
    w_mem_kv, w_out = weights[2](o_gdn)
    memn = rms_fwd(mem2d, mem_norm, "rms_mem")
    MKV = mm(memn, w_mem_kv, "nn", BF16, "mem_kv_proj")
    o_mem = mem_attn_fwd(P, MKV, B, S, M)
    mixed, dx2, dx2b, sq, g_norm_final = merge_fwd(o_mla, o_gdn, o_mem, P, x2d, tgt2d, w_out, gdn_norm, norm_final.reshape(1, D))

    g_w_out = mm(mixed, dx2b, "tn", BF16, "grad_w_out")
    dgate, do_mla, do_gdn, do_mem, g_gdn_norm = merge_bwd(dx2b, o_mla, o_gdn, o_mem, P, w_out, gdn_norm)

    dmemq, dMKV = mem_attn_bwd(P, MKV, do_mem, B, S, M)
    g_w_mem_kv = mm(memn, dMKV, "tn", BF16, "grad_w_mem_kv")
    started_early = big_grads_ready(dict(w_mem_kv=g_w_mem_kv, w_out=g_w_out), "early")
    dmemn = mm(dMKV, w_mem_kv, "nt", F32, "d_memn", after=(started_early,))
    g_mem_norm = gain_grad(mem2d, dmemn, "grad_mem_norm")

    dU3, dW3, dQ13, dK13, dA, dG13 = gdn_scan_bwd(do_gdn.reshape(B, S, 512), qkv3, W3, Vn3, GB3, A, St, B, S)
    r2 = lambda a: a.reshape(T, a.shape[-1])
    dqkv, dGB = gdn_chunk_bwd(qkv, GB, Grow, Tinv, dA, r2(dU3), r2(dW3), r2(dQ13), r2(dK13), r2(dG13), B, S)
    dPg, g_conv = gdn_prep_bwd(P, dqkv, gdn_conv, B, S)
    dab, g_ab = gdn_gate_bwd(P, dGB, alog_row, dt_row, B, S)

    dQ, dK, dV = mla_attn_bwd(Q, K, V, o_mla, do_mla, lse, B, S)
    dq_lin, dkv_lin, dPm, g_q_a_norm, g_kv_a_norm = mla_proj_bwd(dQ, dK, dV, cos_t, sin_t, P, dab, wq, wkv, q_a_norm, kv_a_norm)
    g_wq = mm(dq_lin, qn, "tn", BF16, "grad_w_q_b")
    g_wkv = mm(kvn, dkv_lin, "tn", BF16, "grad_w_kv_b")

    dP = [dPm, dmemq, dPg, dgate]
    g_wp = mm_cols_tn(dP, h, BF16, "grad_w_in")
    started = big_grads_ready(dict(w_in=g_wp, w_q_b=g_wq, w_kv_b=g_wkv), "late")
    grad_x, g_norm_in = in_proj_bwd(dP, wp, x2d, dx2, norm_in, started)

    grads = dict(
        norm_in=g_norm_in, q_a_norm=g_q_a_norm, kv_a_norm=g_kv_a_norm, gdn_conv=g_conv,
        gdn_a_log_dt_bias=g_ab, gdn_norm=g_gdn_norm,
        mem_norm=g_mem_norm, norm_final=g_norm_final)
    return sq, grad_x.reshape(B, S, D), grads


def kernel(x, mem, positions, norm_in, w_in, q_a_norm, w_q_b, kv_a_norm, w_kv_b, gdn_conv, gdn_a_log, gdn_dt_bias, gdn_norm, mem_norm, w_mem_kv, w_out, norm_final, loss_target, m_norm_in, m_w_in, m_q_a_norm, m_w_q_b, m_kv_a_norm, m_w_kv_b, m_gdn_conv, m_gdn_a_log, m_gdn_dt_bias, m_gdn_norm, m_mem_norm, m_w_mem_kv, m_w_out, m_norm_final, v_norm_in, v_w_in, v_q_a_norm, v_w_q_b, v_kv_a_norm, v_w_kv_b, v_gdn_conv, v_gdn_a_log, v_gdn_dt_bias, v_gdn_norm, v_mem_norm, v_w_mem_kv, v_w_out, v_norm_final):
    B = x.shape[0]
    cx, cy, cc = lax.axis_index("x"), lax.axis_index("y"), lax.axis_index("c")
    chip = 2 * cx + cy

    big_names = ("w_in", "w_q_b", "w_kv_b", "w_mem_kv", "w_out")
    rows_major = lambda a: jnp.transpose(a, (2, 0, 1))
    w_in3, m_in3, v_in3 = rows_major(w_in), rows_major(m_w_in), rows_major(v_w_in)
    w_qb_t, m_qb_t, v_qb_t = jnp.transpose(w_q_b[0]), jnp.transpose(m_w_q_b[0]), jnp.transpose(v_w_q_b[0])
    z32 = jnp.zeros((32, Q_LORA), BF16)
    qb_bf = w_qb_t.astype(BF16)
    qb_padded = jnp.concatenate([qb_bf[:160], z32, qb_bf[160:], z32])
    shards = [dense_bf16(w_in3, "w_in_bf16"), qb_padded, w_kv_b[0].astype(BF16), w_mem_kv[0].astype(BF16), w_out[0].astype(BF16)]
    splits = [(1, D_MODEL // 2)] + [(0, s.shape[0] // 2) for s in shards[1:]]
    *w_in_flight, w_in_started = half_gather_start(shards[0], "w_in_gather_start")
    conv_all = allgather_devices(gdn_conv[0], "allgather_conv")
    conv_cols = gdn_conv.shape[2]
    conv_full = jnp.transpose(conv_all[0::2], (1, 0, 2)).reshape(GDN_CONV, N_CHIPS * conv_cols)
    late_shapes = [(N_CHIPS,) + s.shape for s in shards[1:]]
    late = {}

    def w_in_ready(after):
        g_in = pass_halves_to_sibling(half_gather_wait(*w_in_flight, after, "w_in_gather_wait"), "w_in_gather_sibling")
        *late["a"], started_a = late_gather_start(shards[1:3], g_in, "late_gather_qkv_start")
        *late["b"], started_b = late_gather_start(shards[3:], started_a, "late_gather_mem_out_start")
        return pad_w_in_t(g_in), (started_a, started_b)

    def late_qkv(after):
        g_qb, g_kvb = late_gather_wait(*late["a"], after, "late_gather_qkv_wait")
        return g_qb.reshape(-1, Q_LORA), _perm_w_kv_b(g_kvb)

    def late_mem_out(after):
        g_mem, g_out_w = late_gather_wait(*late["b"], after, "late_gather_mem_out_wait")
        return g_mem.reshape(-1, g_mem.shape[2]), g_out_w.reshape(-1, g_out_w.shape[2])

    weights = (w_in_ready, late_qkv, late_mem_out, (w_in_started,))

    core = jnp.stack([cc]).astype(jnp.int32)
    chip_core = jnp.stack([chip, cc]).astype(jnp.int32)
    exchanges = {}
    by_chip = dict(w_in=unpad_w_in_t, w_q_b=lambda a: a.reshape(late_shapes[0]), w_kv_b=_unperm_w_kv_b,
                   w_mem_kv=lambda a: a.reshape(late_shapes[2]), w_out=lambda a: a.reshape(late_shapes[3]))

    def big_grads_ready(gb, group):
        idx = [big_names.index(n) for n in gb]
        parts = [by_chip[n](a) for n, a in gb.items()]
        sp = [splits[i] for i in idx]
        from_sibling = swap_sibling(parts, "rs_sibling_partial_" + group, sp)
        chip_sums = add_pairs(parts, from_sibling, sp, core, "rs_add_sibling_" + group)
        sems, sums_thru, lands, token = exchange_chips_start(chip_sums, "rs_exchange_start_" + group)
        exchanges[group] = dict(idx=idx, parts=parts, from_sibling=from_sibling, sems=sems, sums=sums_thru, lands=lands)
        return token

    sq, grad_x, g = local_step(x, mem, positions, loss_target, norm_in, weights, big_grads_ready, q_a_norm, kv_a_norm, conv_full,
                               gdn_a_log, gdn_dt_bias, gdn_norm, mem_norm, norm_final)

    small_names = ("norm_in", "q_a_norm", "kv_a_norm", "gdn_a_log", "gdn_dt_bias", "gdn_norm", "mem_norm", "norm_final")
    small = dict(norm_in=norm_in, q_a_norm=q_a_norm, kv_a_norm=kv_a_norm, gdn_a_log=gdn_a_log, gdn_dt_bias=gdn_dt_bias,
                 gdn_norm=gdn_norm, mem_norm=mem_norm, norm_final=norm_final)
    m_small = dict(norm_in=m_norm_in, q_a_norm=m_q_a_norm, kv_a_norm=m_kv_a_norm, gdn_a_log=m_gdn_a_log,
                   gdn_dt_bias=m_gdn_dt_bias, gdn_norm=m_gdn_norm, mem_norm=m_mem_norm, norm_final=m_norm_final)
    v_small = dict(norm_in=v_norm_in, q_a_norm=v_q_a_norm, kv_a_norm=v_kv_a_norm, gdn_a_log=v_gdn_a_log,
                   gdn_dt_bias=v_gdn_dt_bias, gdn_norm=v_gdn_norm, mem_norm=v_mem_norm, norm_final=v_norm_final)
    conv_rows = GDN_CONV * GDN_QKV // 1024
    g_block = pack_small([g.get(n) for n in small_names], g["gdn_a_log_dt_bias"], g["gdn_conv"], sq)
    g_block = sum_leading(allgather_devices(g_block, "allgather_small_grads"), "sum_small_grads")
    loss = 0.5 * jnp.sum(g_block[LOSS_ROW]) / D_MODEL
    g_conv = lax.dynamic_slice_in_dim(g_block[CONV_ROW0:CONV_ROW0 + conv_rows].reshape(GDN_CONV, GDN_QKV), chip * conv_cols,
                                      conv_cols, axis=1)
    as_row = lambda a: a.reshape(1, -1)
    updated = adamw_small(g_block, [as_row(small[n]) for n in small_names], [as_row(m_small[n]) for n in small_names],
                          [as_row(v_small[n]) for n in small_names])
    g_out, d_out, m_out, v_out = ({n: u[i].reshape(small[n].shape) for n, u in zip(small_names, updated)} for i in range(4))
    d_s = d_out["norm_in"]

    my_half = [None] * len(big_names)
    for group, e in exchanges.items():
        from_chips = exchange_chips_wait(e["sems"], e["sums"], e["lands"], d_s, "rs_exchange_wait_" + group)
        halves = add_fives(e["parts"], e["from_sibling"], from_chips, [splits[i] for i in e["idx"]], chip_core, "rs_add_chips_" + group)
        for i, a in zip(e["idx"], halves):
            my_half[i] = a
    other_half = swap_sibling(my_half, "rs_sibling_final")

    d_out["gdn_conv"], m_out["gdn_conv"], v_out["gdn_conv"] = adamw(gdn_conv, g_conv, m_gdn_conv, v_gdn_conv, "adamw_gdn_conv")
    g_out["gdn_conv"] = g_conv[None]
    res = adamw_untiled_rows(w_in3, my_half[0], other_half[0], m_in3, v_in3, "adamw_w_in")
    g_out["w_in"], d_out["w_in"], m_out["w_in"], v_out["w_in"] = [jnp.transpose(r, (1, 2, 0)) for r in res]
    res = adamw_w_q_b(w_qb_t, my_half[1], other_half[1], m_qb_t, v_qb_t, "adamw_w_q_b")
    g_out["w_q_b"], d_out["w_q_b"], m_out["w_q_b"], v_out["w_q_b"] = [jnp.transpose(r)[None] for r in res]
    rest = dict(w_kv_b=(w_kv_b, m_w_kv_b, v_w_kv_b), w_mem_kv=(w_mem_kv, m_w_mem_kv, v_w_mem_kv), w_out=(w_out, m_w_out, v_w_out))
    for i, n in enumerate(big_names):
        if n in rest:
            w_n, m_n, v_n = rest[n]
            g_out[n], d_out[n], m_out[n], v_out[n] = adamw_halves(w_n, my_half[i], other_half[i], m_n, v_n, splits[i], core, "adamw_" + n)

    order = ("norm_in", "w_in", "q_a_norm", "w_q_b", "kv_a_norm", "w_kv_b", "gdn_conv", "gdn_a_log", "gdn_dt_bias",
             "gdn_norm", "mem_norm", "w_mem_kv", "w_out", "norm_final")
    return (loss, grad_x, *[g_out[n] for n in order], *[d_out[n] for n in order], *[m_out[n] for n in order],
            *[v_out[n] for n in order])
```

```python
import jax
import jax.numpy as jnp
from jax import lax
from jax.experimental import pallas as pl
from jax.experimental.pallas import tpu as pltpu

F32 = jnp.float32
BF16 = jnp.bfloat16
BS = pl.BlockSpec

D_MODEL = 1024
N_HEADS = 4
MLA_NOPE, MLA_ROPE, MLA_V = 128, 64, 128
Q_LORA, KV_LORA = 384, 256
ROPE_THETA = 10000.0
GDN_DK = GDN_DV = 128
GDN_CONV = 4
CHUNK = 64
MEM_DH = 128
D_MIX = 1536
GDN_QKV = 1536
D_IN = 4296
EPS = 1e-6
ADAM_LR, ADAM_B1, ADAM_B2, ADAM_EPS, ADAM_WD, ADAM_STEP = 0.001, 0.9, 0.999, 1e-08, 0.01, 10

OFF_MLA = 0
OFF_MEMQ = 1024
OFF_GDN = 1536
OFF_GATE = 3072
N_PAD = 4608
HEAD_PAD = 256
MLA_SCALE = (MLA_NOPE + MLA_ROPE) ** -0.5
MEM_SCALE = MEM_DH ** -0.5
GDN_SCALE = GDN_DK ** -0.5
NEG = -1e30

NN = ((1,), (0,))
NT = ((1,), (1,))
TN = ((0,), (0,))


def _dot(a, b, dims):
    return lax.dot_general(a, b, (dims, ((), ())), preferred_element_type=F32)


def _bdot(spec, a, b, precision=None):
    return jnp.einsum(spec, a, b, preferred_element_type=F32, precision=precision)


def _arb(n):
    return pltpu.CompilerParams(dimension_semantics=("arbitrary",) * n)


def _sigmoid(x):
    return 1.0 / (1.0 + jnp.exp(-x))


def _softplus(z):
    return jnp.maximum(z, 0.0) + jnp.log(1.0 + jnp.exp(-jnp.abs(z)))


def _rope(t, cos_row, sin_row):
    return t * cos_row + pltpu.roll(t, 64, 1) * sin_row


def _rope_bwd(d, cos_row, sin_row):
    return d * cos_row + pltpu.roll(d * sin_row, 64, 1)


def rms_fwd(x, gain, name, tm=512, after=()):
    T, n = x.shape
    tm = min(tm, T)

    def body(x_ref, g_ref, *rest):
        xv = x_ref[...]
        r = lax.rsqrt(jnp.mean(xv * xv, axis=-1, keepdims=True) + EPS)
        rest[-1][...] = (xv * r * g_ref[...]).astype(BF16)

    return pl.pallas_call(
        body, name=name, grid=(T // tm,),
        in_specs=[BS((tm, n), lambda i: (i, 0)), BS((1, n), lambda i: (0, 0))] + [BS(memory_space=pl.ANY)] * len(after),
        out_specs=BS((tm, n), lambda i: (i, 0)),
        out_shape=jax.ShapeDtypeStruct((T, n), BF16), compiler_params=_arb(1))(x, gain, *after)


def mm(a, b, kind, out_dtype, name, bm=512, bn=None, n_outer=False, after=()):
    if kind == "nn":
        (M, K), (_, N) = a.shape, b.shape
    elif kind == "nt":
        (M, K), (N, _) = a.shape, b.shape
    else:
        (K, M), (_, N) = a.shape, b.shape
    bm, bn = min(bm, M), min(bn or N, N)
    assert M % bm == 0 and N % bn == 0, (name, M, N, K)
    ij = (lambda g0, g1: (g1, g0)) if n_outer else (lambda g0, g1: (g0, g1))
    a_spec = BS((K, bm), lambda g0, g1: (0, ij(g0, g1)[0])) if kind == "tn" else BS((bm, K), lambda g0, g1: (ij(g0, g1)[0], 0))
    once = dict(pipeline_mode=pl.Buffered(1)) if bn == N else {}
    b_spec = (BS((bn, K), lambda g0, g1: (ij(g0, g1)[1], 0), **once) if kind == "nt"
              else BS((K, bn), lambda g0, g1: (0, ij(g0, g1)[1]), **once))
    dims = {"nn": NN, "nt": NT, "tn": TN}[kind]

    def body(a_ref, b_ref, *rest):
        rest[-1][...] = _dot(a_ref[...].astype(BF16), b_ref[...].astype(BF16), dims).astype(out_dtype)

    grid = (N // bn, M // bm) if n_outer else (M // bm, N // bn)
    return pl.pallas_call(
        body, name=name, grid=grid, in_specs=[a_spec, b_spec] + [BS(memory_space=pl.ANY)] * len(after),
        out_specs=BS((bm, bn), lambda g0, g1: ij(g0, g1)),
        out_shape=jax.ShapeDtypeStruct((M, N), out_dtype), compiler_params=_arb(2))(a, b, *after)


def mm_cols_tn(pieces, b, out_dtype, name, bm=512):
    K, N = b.shape
    tiles = [p.shape[1] // bm for p in pieces]
    firsts = [sum(tiles[:i]) for i in range(len(tiles))]

    def body(*refs):
        b_ref, o_ref = refs[-2], refs[-1]
        i = pl.program_id(0)
        for a_ref, t0, n in zip(refs[:-2], firsts, tiles):
            @pl.when((i >= t0) & (i < t0 + n))
            def _(a_ref=a_ref):
                o_ref[...] = _dot(a_ref[...], b_ref[...], TN).astype(out_dtype)

    a_specs = [BS((K, bm), lambda i, t0=t0, n=n: (0, jnp.clip(i - t0, 0, n - 1))) for t0, n in zip(firsts, tiles)]
    return pl.pallas_call(
        body, name=name, grid=(sum(tiles),),
        in_specs=a_specs + [BS(b.shape, lambda i: (0, 0), pipeline_mode=pl.Buffered(1))],
        out_specs=BS((bm, N), lambda i: (i, 0)), out_shape=jax.ShapeDtypeStruct((sum(tiles) * bm, N), out_dtype),
        compiler_params=_arb(1))(*pieces, b)


def rope_tables(pos_col, inv_row, sgn_row, msk_row, tm=512, after=()):
    T = pos_col.shape[0]
    tm = min(tm, T)

    def body(p_ref, inv_ref, sgn_ref, msk_ref, *rest):
        c_ref, s_ref = rest[-2:]
        ang = p_ref[...].astype(F32) * inv_ref[...]
        c_ref[...] = jnp.cos(ang) * msk_ref[...]
        s_ref[...] = jnp.sin(ang) * sgn_ref[...]

    row = BS((1, 128), lambda i: (0, 0))
    return pl.pallas_call(
        body, name="rope_tables", grid=(T // tm,),
        in_specs=[BS((tm, 1), lambda i: (i, 0)), row, row, row] + [BS(memory_space=pl.ANY)] * len(after),
        out_specs=[BS((tm, 128), lambda i: (i, 0))] * 2,
        out_shape=[jax.ShapeDtypeStruct((T, 128), F32)] * 2, compiler_params=_arb(1))(pos_col, inv_row, sgn_row, msk_row, *after)


def mla_prep(P, gq, gkv, wq, wkv, cos_t, sin_t, tm=512):
    T = P.shape[0]
    tm = min(tm, T)

    def body(p_ref, gq_ref, gkv_ref, wq_ref, wkv_ref, c_ref, s_ref, q_ref, k_ref, v_ref, qn_ref, kvn_ref):
        p = p_ref[...].astype(F32)
        cq, ckv, kr = p[:, :Q_LORA], p[:, Q_LORA:Q_LORA + KV_LORA], p[:, 640:768]
        qn = (cq * lax.rsqrt(jnp.mean(cq * cq, axis=-1, keepdims=True) + EPS) * gq_ref[...]).astype(BF16)
        kvn = (ckv * lax.rsqrt(jnp.mean(ckv * ckv, axis=-1, keepdims=True) + EPS) * gkv_ref[...]).astype(BF16)
        qn_ref[...] = qn
        kvn_ref[...] = kvn
        q = _dot(qn, wq_ref[...], NT)
        kv = _dot(kvn, wkv_ref[...], NN)
        cos_row, sin_row = c_ref[...], s_ref[...]
        krr = _rope(kr, cos_row, sin_row).astype(BF16)
        for h in range(N_HEADS):
            lo = h * HEAD_PAD
            q_ref[:, lo:lo + 128] = (q[:, lo:lo + 128] * MLA_SCALE).astype(BF16)
            q_ref[:, lo + 128:lo + 256] = (_rope(q[:, lo + 128:lo + 256], cos_row, sin_row) * MLA_SCALE).astype(BF16)
            k_ref[:, lo:lo + 128] = kv[:, h * 128:(h + 1) * 128].astype(BF16)
            k_ref[:, lo + 128:lo + 256] = krr
            v_ref[:, lo:lo + 128] = kv[:, 512 + h * 128:512 + (h + 1) * 128].astype(BF16)
            v_ref[:, lo + 128:lo + 256] = jnp.ones((tm, 128), BF16)

    full = lambda r, c: BS((r, c), lambda i: (0, 0))
    rowb = lambda c: BS((tm, c), lambda i: (i, 0))
    return pl.pallas_call(
        body, name="mla_prep", grid=(T // tm,),
        in_specs=[rowb(1024), full(1, Q_LORA), full(1, KV_LORA), full(1024, Q_LORA), full(KV_LORA, 1024), rowb(128), rowb(128)],
        out_specs=[rowb(1024), rowb(1024), rowb(1024), rowb(Q_LORA), rowb(KV_LORA)],
        out_shape=[jax.ShapeDtypeStruct((T, 1024), BF16), jax.ShapeDtypeStruct((T, 1024), BF16),
                   jax.ShapeDtypeStruct((T, 1024), BF16), jax.ShapeDtypeStruct((T, Q_LORA), BF16),
                   jax.ShapeDtypeStruct((T, KV_LORA), BF16)],
        compiler_params=_arb(1))(P, gq, gkv, wq, wkv, cos_t, sin_t)


ATTN_HEADS_PER_STEP = 2
ATTN_STRIP = 32


def mla_attn_fwd(Q, K, V, B, S, tq=512, hp=ATTN_HEADS_PER_STEP):
    T = B * S
    tq = min(tq, S)
    nq = S // tq

    rs = min(ATTN_STRIP, tq)

    def body(q_ref, k_ref, v_ref, o_ref, lse_ref, m_s, acc_s, s_s, p_s, a_s):
        i = pl.program_id(2)
        m_s[...] = jnp.full_like(m_s, NEG)
        acc_s[...] = jnp.zeros_like(acc_s)

        def blk(j, masked):
            rows = pl.ds(pl.multiple_of(j * tq, tq), tq)
            for h in range(hp):
                hq = slice(h * HEAD_PAD, (h + 1) * HEAD_PAD)
                s_s[h] = _dot(q_ref[:, hq], k_ref[rows, hq], NT)
            for h in range(hp):
                for r0 in range(0, tq, rs):
                    rr = slice(r0, r0 + rs)
                    sv = s_s[h, rr, :]
                    if masked:
                        r = r0 + lax.broadcasted_iota(jnp.int32, (rs, tq), 0)
                        c = lax.broadcasted_iota(jnp.int32, (rs, tq), 1)
                        sv = jnp.where(r >= c, sv, NEG)
                    m_prev = m_s[h, rr, :]
                    m_new = jnp.maximum(m_prev, jnp.max(sv, axis=1, keepdims=True))
                    p_s[h, rr, :] = jnp.exp(sv - m_new).astype(BF16)
                    a_s[h, rr, :] = jnp.exp(m_prev - m_new)
                    m_s[h, rr, :] = m_new
            for h in range(hp):
                hq = slice(h * HEAD_PAD, (h + 1) * HEAD_PAD)
                acc_s[h] = a_s[h] * acc_s[h] + _dot(p_s[h], v_ref[rows, hq], NN)

        def loop(j, c):
            blk(j, False)
            return c

        lax.fori_loop(0, i, loop, 0)
        blk(i, True)
        for h in range(hp):
            den = acc_s[h, :, 128:256]
            o_ref[:, h * 128:(h + 1) * 128] = (acc_s[h, :, 0:128] / den).astype(BF16)
            lse_ref[h] = m_s[h] + jnp.log(den[:, 0:1])

    return pl.pallas_call(
        body, name="mla_attn_fwd", grid=(B, N_HEADS // hp, nq),
        in_specs=[BS((tq, hp * HEAD_PAD), lambda b, h, i: (b * nq + i, h)),
                  BS((S, hp * HEAD_PAD), lambda b, h, i: (b, h)),
                  BS((S, hp * HEAD_PAD), lambda b, h, i: (b, h))],
        out_specs=[BS((tq, hp * 128), lambda b, h, i: (b * nq + i, h)),
                   BS((hp, tq, 1), lambda b, h, i: (h, b * nq + i, 0))],
        out_shape=[jax.ShapeDtypeStruct((T, 512), BF16), jax.ShapeDtypeStruct((N_HEADS, T, 1), F32)],
        scratch_shapes=[pltpu.VMEM((hp, tq, 1), F32), pltpu.VMEM((hp, tq, HEAD_PAD), F32), pltpu.VMEM((hp, tq, tq), F32),
                        pltpu.VMEM((hp, tq, tq), BF16), pltpu.VMEM((hp, tq, 1), F32)],
        compiler_params=_arb(3))(Q, K, V)


def mla_attn_bwd(Q, K, V, O, dO, LSE, B, S, tq=512, hp=ATTN_HEADS_PER_STEP):
    T = B * S
    tq = min(tq, S)
    nq = S // tq

    rs = min(ATTN_STRIP, tq)

    def body(q_ref, k_ref, v_ref, o_ref, do_ref, lse_ref, dq_ref, dk_ref, dv_ref, delta_s, dq_s, dk_s, dv_s, s_s, dp_s, p_s, ds_s):
        j = pl.program_id(2)

        @pl.when(j == 0)
        def _():
            dq_s[...] = jnp.zeros_like(dq_s)
            for h in range(hp):
                sl = slice(h * 128, (h + 1) * 128)
                delta_s[h] = jnp.sum(do_ref[:, sl] * o_ref[:, sl].astype(F32), axis=1, keepdims=True)

        dk_s[...] = jnp.zeros_like(dk_s)
        dv_s[...] = jnp.zeros_like(dv_s)

        def step(i, c):
            rows = pl.ds(pl.multiple_of(i * tq, tq), tq)
            for h in range(hp):
                sq, sv = slice(h * HEAD_PAD, (h + 1) * HEAD_PAD), slice(h * 128, (h + 1) * 128)
                s_s[h] = _dot(q_ref[rows, sq], k_ref[:, sq], NT)
                dp_s[h] = _dot(do_ref[rows, sv].astype(BF16), v_ref[:, h * HEAD_PAD:h * HEAD_PAD + 128], NT)
            for h in range(hp):
                for r0 in range(0, tq, rs):
                    rr = slice(r0, r0 + rs)
                    seq_rows = pl.ds(pl.multiple_of(i * tq + r0, rs), rs)
                    r = i * tq + r0 + lax.broadcasted_iota(jnp.int32, (rs, tq), 0)
                    cc = j * tq + lax.broadcasted_iota(jnp.int32, (rs, tq), 1)
                    p = jnp.where(r >= cc, jnp.exp(s_s[h, rr, :] - lse_ref[h, seq_rows, :]), 0.0)
                    p_s[h, rr, :] = p.astype(BF16)
                    ds_s[h, rr, :] = (p * (dp_s[h, rr, :] - delta_s[h, seq_rows, :])).astype(BF16)
            for h in range(hp):
                sq, sv = slice(h * HEAD_PAD, (h + 1) * HEAD_PAD), slice(h * 128, (h + 1) * 128)
                dv_s[:, sv] += _dot(p_s[h], do_ref[rows, sv].astype(BF16), TN)
                dk_s[:, sq] += _dot(ds_s[h], q_ref[rows, sq], TN)
                dq_s[rows, sq] += _dot(ds_s[h], k_ref[:, sq], NN)
            return c

        lax.fori_loop(j, nq, step, 0)
        dk_ref[...] = dk_s[...].astype(BF16)
        dv_ref[...] = dv_s[...].astype(BF16)

        @pl.when(j == nq - 1)
        def _():
            dq_ref[...] = dq_s[...].astype(BF16)

    seq = lambda c: BS((S, c), lambda b, h, j: (b, h))
    blk = lambda c: BS((tq, c), lambda b, h, j: (b * nq + j, h))
    return pl.pallas_call(
        body, name="mla_attn_bwd", grid=(B, N_HEADS // hp, nq),
        in_specs=[seq(hp * HEAD_PAD), blk(hp * HEAD_PAD), blk(hp * HEAD_PAD), seq(hp * 128), seq(hp * 128),
                  BS((hp, S, 1), lambda b, h, j: (h, b, 0))],
        out_specs=[seq(hp * HEAD_PAD), blk(hp * HEAD_PAD), blk(hp * 128)],
        out_shape=[jax.ShapeDtypeStruct((T, 1024), BF16), jax.ShapeDtypeStruct((T, 1024), BF16),
                   jax.ShapeDtypeStruct((T, 512), BF16)],
        scratch_shapes=[pltpu.VMEM((hp, S, 1), F32), pltpu.VMEM((S, hp * HEAD_PAD), F32),
                        pltpu.VMEM((tq, hp * HEAD_PAD), F32), pltpu.VMEM((tq, hp * 128), F32),
                        pltpu.VMEM((hp, tq, tq), F32), pltpu.VMEM((hp, tq, tq), F32),
                        pltpu.VMEM((hp, tq, tq), BF16), pltpu.VMEM((hp, tq, tq), BF16)],
        compiler_params=_arb(3))(Q, K, V, O, dO, LSE)


def mla_proj_bwd(dQ, dK, dV, cos_t, sin_t, P, dab, wq, wkv, gq, gkv, tm=512):
    T = P.shape[0]
    tm = min(tm, T)

    def norm_bwd(x, dy, g):
        r = lax.rsqrt(jnp.mean(x * x, axis=-1, keepdims=True) + EPS)
        xh = x * r
        dxh = dy * g
        return r * (dxh - xh * jnp.mean(dxh * xh, axis=-1, keepdims=True)), jnp.sum(dy * xh, axis=0, keepdims=True)

    def body(dq_ref, dk_ref, dv_ref, c_ref, s_ref, p_ref, dab_ref, wq_ref, wkv_ref, gq_ref, gkv_ref,
             ql_ref, kvl_ref, o_ref, aq_ref, akv_ref):
        @pl.when(pl.program_id(0) == 0)
        def _():
            aq_ref[...] = jnp.zeros_like(aq_ref)
            akv_ref[...] = jnp.zeros_like(akv_ref)

        cos_row, sin_row = c_ref[...], s_ref[...]
        kr = jnp.zeros((tm, 128), F32)
        for h in range(N_HEADS):
            lo = h * HEAD_PAD
            ql_ref[:, lo:lo + 128] = (dq_ref[:, lo:lo + 128].astype(F32) * MLA_SCALE).astype(BF16)
            ql_ref[:, lo + 128:lo + 256] = (_rope_bwd(dq_ref[:, lo + 128:lo + 256].astype(F32), cos_row, sin_row) * MLA_SCALE).astype(BF16)
            kvl_ref[:, h * 128:(h + 1) * 128] = dk_ref[:, lo:lo + 128]
            kr = kr + dk_ref[:, lo + 128:lo + 256].astype(F32)
        kvl_ref[:, 512:] = dv_ref[...]
        dqn = _dot(ql_ref[...], wq_ref[...], NN)
        dkvn = _dot(kvl_ref[...], wkv_ref[...], NT)
        dcq, ggq = norm_bwd(p_ref[:, :Q_LORA].astype(F32), dqn, gq_ref[...])
        dckv, ggkv = norm_bwd(p_ref[:, Q_LORA:640].astype(F32), dkvn, gkv_ref[...])
        aq_ref[...] += ggq
        akv_ref[...] += ggkv
        o_ref[:, :Q_LORA] = dcq.astype(BF16)
        o_ref[:, Q_LORA:640] = dckv.astype(BF16)
        o_ref[:, 640:768] = _rope_bwd(kr, cos_row, sin_row).astype(BF16)
        o_ref[:, 768:896] = dab_ref[...]
        o_ref[:, 896:1024] = jnp.zeros((tm, 128), BF16)

    rowb = lambda c: BS((tm, c), lambda i: (i, 0))
    full = lambda r, c: BS((r, c), lambda i: (0, 0))
    return pl.pallas_call(
        body, name="mla_proj_bwd", grid=(T // tm,),
        in_specs=[rowb(1024), rowb(1024), rowb(512), rowb(128), rowb(128), rowb(1024), rowb(128),
                  full(1024, Q_LORA), full(KV_LORA, 1024), full(1, Q_LORA), full(1, KV_LORA)],
        out_specs=[rowb(1024), rowb(1024), rowb(1024), full(1, Q_LORA), full(1, KV_LORA)],
        out_shape=[jax.ShapeDtypeStruct((T, 1024), BF16)] * 3
        + [jax.ShapeDtypeStruct((1, Q_LORA), F32), jax.ShapeDtypeStruct((1, KV_LORA), F32)],
        compiler_params=_arb(1))(dQ, dK, dV, cos_t, sin_t, P, dab, wq, wkv, gq, gkv)


def _mem_probs(qh, kh):
    s = _dot(qh, kh, NT) * MEM_SCALE
    p = jnp.exp(s - jnp.max(s, axis=1, keepdims=True))
    return p / jnp.sum(p, axis=1, keepdims=True)


def mem_attn_fwd(P, MKV, B, S, M, tq=512):
    T = B * S
    tq = min(tq, S)
    nq = S // tq

    def body(q_ref, kv_ref, o_ref):
        for h in range(N_HEADS):
            sl = slice(h * 128, (h + 1) * 128)
            p = _mem_probs(q_ref[:, sl].astype(BF16), kv_ref[:, sl])
            o_ref[:, sl] = _dot(p.astype(BF16), kv_ref[:, 512 + h * 128:512 + (h + 1) * 128], NN).astype(BF16)

    return pl.pallas_call(
        body, name="mem_attn_fwd", grid=(B, nq),
        in_specs=[BS((tq, 512), lambda b, i: (b * nq + i, OFF_MEMQ // 512)), BS((M, 1024), lambda b, i: (b, 0))],
        out_specs=BS((tq, 512), lambda b, i: (b * nq + i, 0)),
        out_shape=jax.ShapeDtypeStruct((T, 512), BF16), compiler_params=_arb(2))(P, MKV)


def mem_attn_bwd(P, MKV, dO, B, S, M, tq=512):
    T = B * S
    tq = min(tq, S)
    nq = S // tq

    def body(q_ref, kv_ref, do_ref, dq_ref, dkv_ref):
        @pl.when(pl.program_id(1) == 0)
        def _():
            dkv_ref[...] = jnp.zeros_like(dkv_ref)

        for h in range(N_HEADS):
            sl = slice(h * 128, (h + 1) * 128)
            sv = slice(512 + h * 128, 512 + (h + 1) * 128)
            qh = q_ref[:, sl].astype(BF16)
            kh = kv_ref[:, sl]
            do = do_ref[:, sl].astype(BF16)
            p = _mem_probs(qh, kh)
            dkv_ref[:, sv] += _dot(p.astype(BF16), do, TN)
            dp = _dot(do, kv_ref[:, sv], NT)
            ds = (p * (dp - jnp.sum(dp * p, axis=1, keepdims=True)) * MEM_SCALE).astype(BF16)
            dq_ref[:, sl] = _dot(ds, kh, NN).astype(BF16)
            dkv_ref[:, sl] += _dot(ds, qh, TN)

    return pl.pallas_call(
        body, name="mem_attn_bwd", grid=(B, nq),
        in_specs=[BS((tq, 512), lambda b, i: (b * nq + i, OFF_MEMQ // 512)), BS((M, 1024), lambda b, i: (b, 0)),
                  BS((tq, 512), lambda b, i: (b * nq + i, 0))],
        out_specs=[BS((tq, 512), lambda b, i: (b * nq + i, 0)), BS((M, 1024), lambda b, i: (b, 0))],
        out_shape=[jax.ShapeDtypeStruct((T, 512), BF16), jax.ShapeDtypeStruct((B * M, 1024), F32)],
        compiler_params=_arb(2))(P, MKV, dO)


def gain_grad(x, dy, name, tm=256):
    T, n = x.shape
    tm = min(tm, T)

    def body(x_ref, dy_ref, o_ref):
        @pl.when(pl.program_id(0) == 0)
        def _():
            o_ref[...] = jnp.zeros_like(o_ref)

        xv = x_ref[...]
        xh = xv * lax.rsqrt(jnp.mean(xv * xv, axis=-1, keepdims=True) + EPS)
        o_ref[...] += jnp.sum(dy_ref[...] * xh, axis=0, keepdims=True)

    return pl.pallas_call(
        body, name=name, grid=(T // tm,),
        in_specs=[BS((tm, n), lambda i: (i, 0))] * 2, out_specs=BS((1, n), lambda i: (0, 0)),
        out_shape=jax.ShapeDtypeStruct((1, n), F32), compiler_params=_arb(1))(x, dy)


def _conv_silu(x, w, t):
    y = x * w[3:4, :]
    for s in range(1, GDN_CONV):
        y = y + jnp.where(t >= s, pltpu.roll(x, s, 0), 0.0) * w[3 - s:4 - s, :]
    return y, _sigmoid(y)


def gdn_prep_fwd(P, conv_w, B, S):
    T = B * S

    def body(x_ref, w_ref, o_ref):
        kind = pl.program_id(1)
        t = lax.broadcasted_iota(jnp.int32, (S, 1), 0)
        y, sg = _conv_silu(x_ref[...].astype(F32), w_ref[...], t)
        a = y * sg
        scale = jnp.where(kind == 0, GDN_SCALE, 1.0).astype(F32)
        for h in range(N_HEADS):
            sl = slice(h * 128, (h + 1) * 128)
            seg = a[:, sl]
            n = lax.rsqrt(jnp.sum(seg * seg, axis=-1, keepdims=True) + EPS)
            o_ref[:, sl] = jnp.where(kind < 2, seg * (n * scale), seg)

    return pl.pallas_call(
        body, name="gdn_prep_fwd", grid=(B, 3),
        in_specs=[BS((S, 512), lambda b, k: (b, OFF_GDN // 512 + k)), BS((GDN_CONV, 512), lambda b, k: (0, k))],
        out_specs=BS((S, 512), lambda b, k: (b, k)),
        out_shape=jax.ShapeDtypeStruct((T, GDN_QKV), F32), compiler_params=_arb(2))(P, conv_w)


def gdn_prep_bwd(P, dqkv, conv_w, B, S):
    T = B * S

    def body(x_ref, d_ref, w_ref, o_ref, gw_ref):
        kind = pl.program_id(0)

        @pl.when(pl.program_id(1) == 0)
        def _():
            gw_ref[...] = jnp.zeros_like(gw_ref)

        t = lax.broadcasted_iota(jnp.int32, (S, 1), 0)
        x = x_ref[...].astype(F32)
        w = w_ref[...]
        y, sg = _conv_silu(x, w, t)
        a = y * sg
        scale = jnp.where(kind == 0, GDN_SCALE, 1.0).astype(F32)
        das = []
        for h in range(N_HEADS):
            sl = slice(h * 128, (h + 1) * 128)
            seg, dseg = a[:, sl], d_ref[:, sl]
            n = lax.rsqrt(jnp.sum(seg * seg, axis=-1, keepdims=True) + EPS)
            dn = scale * (n * dseg - seg * (n * n * n) * jnp.sum(dseg * seg, axis=-1, keepdims=True))
            das.append(jnp.where(kind < 2, dn, dseg))
        dy = jnp.concatenate(das, axis=1) * (sg * (1.0 + y * (1.0 - sg)))
        dx = dy * w[3:4, :]
        gw_ref[3:4, :] += jnp.sum(dy * x, axis=0, keepdims=True)
        for s in range(1, GDN_CONV):
            dx = dx + jnp.where(t + s < S, pltpu.roll(dy, S - s, 0), 0.0) * w[3 - s:4 - s, :]
            gw_ref[3 - s:4 - s, :] += jnp.sum(dy * jnp.where(t >= s, pltpu.roll(x, s, 0), 0.0), axis=0, keepdims=True)
        o_ref[...] = dx.astype(BF16)

    return pl.pallas_call(
        body, name="gdn_prep_bwd", grid=(3, B),
        in_specs=[BS((S, 512), lambda k, b: (b, OFF_GDN // 512 + k)), BS((S, 512), lambda k, b: (b, k)),
                  BS((GDN_CONV, 512), lambda k, b: (0, k))],
        out_specs=[BS((S, 512), lambda k, b: (b, k)), BS((GDN_CONV, 512), lambda k, b: (0, k))],
        out_shape=[jax.ShapeDtypeStruct((T, GDN_QKV), BF16), jax.ShapeDtypeStruct((GDN_CONV, GDN_QKV), F32)],
        compiler_params=_arb(2))(P, dqkv, conv_w)


def _chunk_row(n_rows):
    return lax.broadcasted_iota(jnp.int32, (n_rows, 1), 0) % CHUNK


def gdn_gate_fwd(P, alog_row, dt_row, B, S):
    T = B * S

    def body(x_ref, al_ref, dt_ref, o_ref):
        x = x_ref[...].astype(F32)
        lane = lax.broadcasted_iota(jnp.int32, (1, 128), 1)
        g = jnp.where(lane < 4, -jnp.exp(al_ref[...]) * _softplus(x + dt_ref[...]), 0.0)
        t = _chunk_row(S)
        for s in (1, 2, 4, 8, 16, 32):
            g = g + jnp.where(t >= s, pltpu.roll(g, s, 0), 0.0)
        o_ref[...] = jnp.where(lane < 4, g, jnp.where(lane < 8, _sigmoid(x), 0.0))

    row = BS((1, 128), lambda b: (0, 0))
    return pl.pallas_call(
        body, name="gdn_gate_fwd", grid=(B,),
        in_specs=[BS((S, 128), lambda b: (b, 768 // 128)), row, row], out_specs=BS((S, 128), lambda b: (b, 0)),
        out_shape=jax.ShapeDtypeStruct((T, 128), F32), compiler_params=_arb(1))(P, alog_row, dt_row)


def gdn_gate_bwd(P, dGB, alog_row, dt_row, B, S):
    T = B * S

    def body(x_ref, d_ref, al_ref, dt_ref, o_ref, acc_ref):
        @pl.when(pl.program_id(0) == 0)
        def _():
            acc_ref[...] = jnp.zeros_like(acc_ref)

        x, d = x_ref[...].astype(F32), d_ref[...]
        lane = lax.broadcasted_iota(jnp.int32, (1, 128), 1)
        z = x + dt_ref[...]
        coef = -jnp.exp(al_ref[...])
        g = coef * _softplus(z)
        da = jnp.where(lane < 4, d * coef * _sigmoid(z), 0.0)
        beta = _sigmoid(x)
        o_ref[...] = jnp.where(lane < 4, da, jnp.where(lane < 8, d * beta * (1.0 - beta), 0.0)).astype(BF16)
        acc_ref[0:1, :] += jnp.sum(jnp.where(lane < 4, d * g, 0.0), axis=0, keepdims=True)
        acc_ref[1:2, :] += jnp.sum(da, axis=0, keepdims=True)

    row = BS((1, 128), lambda b: (0, 0))
    return pl.pallas_call(
        body, name="gdn_gate_bwd", grid=(B,),
        in_specs=[BS((S, 128), lambda b: (b, 768 // 128)), BS((S, 128), lambda b: (b, 0)), row, row],
        out_specs=[BS((S, 128), lambda b: (b, 0)), BS((8, 128), lambda b: (0, 0))],
        out_shape=[jax.ShapeDtypeStruct((T, 128), BF16), jax.ShapeDtypeStruct((8, 128), F32)],
        compiler_params=_arb(1))(P, dGB, alog_row, dt_row)


def _chunk_masks(nc):
    r = lax.broadcasted_iota(jnp.int32, (nc, CHUNK, CHUNK), 1)
    c = lax.broadcasted_iota(jnp.int32, (nc, CHUNK, CHUNK), 2)
    return r >= c, r > c


def _chunk_local(q, k, gc, gr, beta, incl, strict):
    decay = jnp.exp(jnp.where(incl, gc - gr, NEG))
    kb = k * beta
    kbf = k.astype(BF16)
    m_kk = _bdot("gcd,gjd->gcj", kb.astype(BF16), kbf)
    l_mat = jnp.where(strict, m_kk * decay, 0.0)
    a_mat = _bdot("gcd,gjd->gcj", q.astype(BF16), kbf) * decay
    return decay, kb, l_mat, a_mat


WY_SPLIT_LEVELS = 2


def _split_bf16(x):
    hi = x.astype(BF16)
    return hi, (x - hi.astype(F32)).astype(BF16)


def _mm_split(ah, al, bh, bl):
    spec = "gij,gjk->gik"
    return _bdot(spec, ah, bh) + (_bdot(spec, ah, bl) + _bdot(spec, al, bh))


def gdn_chunk_fwd(qkv, GB, Grow, B, S, nc=8):
    T = B * S
    N = S // CHUNK
    nc = min(nc, N)
    nb = N // nc
    R = nc * CHUNK

    def body(q_ref, k_ref, v_ref, gb_ref, gr_ref, u_ref, w_ref, t_ref, a_ref):
        incl, strict = _chunk_masks(nc)
        eye = (lax.broadcasted_iota(jnp.int32, (nc, CHUNK, CHUNK), 1)
               == lax.broadcasted_iota(jnp.int32, (nc, CHUNK, CHUNK), 2)).astype(F32)
        for h in range(N_HEADS):
            sl = slice(h * 128, (h + 1) * 128)
            q = q_ref[:, sl].reshape(nc, CHUNK, 128)
            k = k_ref[:, sl].reshape(nc, CHUNK, 128)
            v = v_ref[:, sl].reshape(nc, CHUNK, 128)
            gc = gb_ref[:, h:h + 1].reshape(nc, CHUNK, 1)
            beta = gb_ref[:, 4 + h:5 + h].reshape(nc, CHUNK, 1)
            gr = gr_ref[h][:, None, :]
            _, kb, l_mat, a_mat = _chunk_local(q, k, gc, gr, beta, incl, strict)
            pw = -l_mat
            tinv = eye + pw
            for level in range(5):
                if level < WY_SPLIT_LEVELS:
                    ph, pl_ = _split_bf16(pw)
                    pw = _mm_split(ph, pl_, ph, pl_)
                    ph, pl_ = _split_bf16(pw)
                    th, tl = _split_bf16(tinv)
                    tinv = tinv + _mm_split(th, tl, ph, pl_)
                else:
                    ph = pw.astype(BF16)
                    pw = _bdot("gij,gjk->gik", ph, ph)
                    tinv = tinv + _bdot("gij,gjk->gik", tinv.astype(BF16), pw.astype(BF16))
            tb = tinv.astype(BF16)
            u = _bdot("gcj,gjv->gcv", tb, (v * beta).astype(BF16))
            w = _bdot("gcj,gjk->gck", tb, (kb * jnp.exp(gc)).astype(BF16))
            u_ref[:, sl] = u.reshape(R, 128)
            w_ref[:, sl] = w.reshape(R, 128).astype(BF16)
            t_ref[h] = jnp.swapaxes(tinv, 1, 2).astype(BF16)
            a_ref[h] = a_mat.astype(BF16)

    rowb = lambda c, j: BS((R, c), lambda b, n: (b * nb + n, j))
    mat = BS((None, N_HEADS, nc, CHUNK, CHUNK), lambda b, n: (b, 0, n, 0, 0))
    return pl.pallas_call(
        body, name="gdn_chunk_fwd", grid=(B, nb),
        in_specs=[rowb(512, 0), rowb(512, 1), rowb(512, 2), rowb(128, 0),
                  BS((None, N_HEADS, nc, CHUNK), lambda b, n: (b, 0, n, 0))],
        out_specs=[rowb(512, 0), rowb(512, 0), mat, mat],
        out_shape=[jax.ShapeDtypeStruct((T, 512), F32), jax.ShapeDtypeStruct((T, 512), BF16),
                   jax.ShapeDtypeStruct((B, N_HEADS, N, CHUNK, CHUNK), BF16),
                   jax.ShapeDtypeStruct((B, N_HEADS, N, CHUNK, CHUNK), BF16)],
        compiler_params=_arb(2))(qkv, qkv, qkv, GB, Grow)


SCAN_CHUNKS = 2


def gdn_scan_fwd(qkv3, U3, W3, GB3, A, B, S):
    N = S // CHUNK
    cps = SCAN_CHUNKS if N % SCAN_CHUNKS == 0 else 1

    def body(q_ref, k_ref, u_ref, w_ref, gb_ref, a_ref, o_ref, vn_ref, st_ref, s_s):
        @pl.when(pl.program_id(0) == 0)
        def _():
            s_s[...] = jnp.zeros_like(s_s)

        for c in range(cps):
            rows = slice(c * CHUNK, (c + 1) * CHUNK)
            for b in range(B):
                for h in range(N_HEADS):
                    sl = slice(h * 128, (h + 1) * 128)
                    st = s_s[b, h]
                    st_ref[b, h, c] = st
                    stb = st.astype(BF16)
                    g = gb_ref[b, rows, h:h + 1]
                    gl = g[CHUNK - 1:CHUNK, :]
                    qg = (q_ref[b, rows, sl] * jnp.exp(g)).astype(BF16)
                    on_state = _dot(jnp.concatenate([w_ref[b, rows, sl].astype(BF16), qg], axis=0), stb, NN)
                    vn = u_ref[b, rows, sl] - on_state[:CHUNK]
                    vnb = vn.astype(BF16)
                    kd_t = jnp.transpose(k_ref[b, rows, sl] * jnp.exp(gl - g)).astype(BF16)
                    on_vn = _dot(jnp.concatenate([a_ref[b, h, c].astype(BF16), kd_t], axis=0), vnb, NN)
                    vn_ref[b, rows, sl] = vnb
                    o_ref[b, rows, sl] = (on_state[CHUNK:] + on_vn[:CHUNK]).astype(BF16)
                    s_s[b, h] = st * jnp.exp(gl) + on_vn[CHUNK:]

    tok = lambda c, j: BS((B, cps * CHUNK, c), lambda n: (0, n, j))
    return pl.pallas_call(
        body, name="gdn_scan_fwd", grid=(N // cps,),
        in_specs=[tok(512, 0), tok(512, 1), tok(512, 0), tok(512, 0), tok(128, 0),
                  BS((B, N_HEADS, cps, CHUNK, CHUNK), lambda n: (0, 0, n, 0, 0))],
        out_specs=[tok(512, 0), tok(512, 0), BS((B, N_HEADS, cps, 128, 128), lambda n: (0, 0, n, 0, 0))],
        out_shape=[jax.ShapeDtypeStruct((B, S, 512), BF16), jax.ShapeDtypeStruct((B, S, 512), BF16),
                   jax.ShapeDtypeStruct((B, N_HEADS, N, 128, 128), F32)],
        scratch_shapes=[pltpu.VMEM((B, N_HEADS, 128, 128), F32)],
        compiler_params=_arb(1))(qkv3, qkv3, U3, W3, GB3, A)


def gdn_scan_bwd(dO3, qkv3, W3, Vn3, GB3, A, St, B, S):
    N = S // CHUNK
    cps = SCAN_CHUNKS if N % SCAN_CHUNKS == 0 else 1

    def body(do_ref, q_ref, k_ref, w_ref, vn_ref, gb_ref, a_ref, st_ref,
             du_ref, dw_ref, dq_ref, dk_ref, da_ref, dg_ref, ds_s):
        @pl.when(pl.program_id(0) == 0)
        def _():
            ds_s[...] = jnp.zeros_like(ds_s)

        lane = lax.broadcasted_iota(jnp.int32, (1, 128), 1)
        last = lax.broadcasted_iota(jnp.int32, (CHUNK, 1), 0) == CHUNK - 1
        for c in reversed(range(cps)):
            rows = slice(c * CHUNK, (c + 1) * CHUNK)
            for b in range(B):
                dg_all = jnp.zeros((CHUNK, 128), F32)
                for h in range(N_HEADS):
                    sl = slice(h * 128, (h + 1) * 128)
                    st = st_ref[b, h, c]
                    stb = st.astype(BF16)
                    dsn = ds_s[b, h]
                    dsnb = dsn.astype(BF16)
                    g = gb_ref[b, rows, h:h + 1]
                    gl = g[CHUNK - 1:CHUNK, :]
                    egl = jnp.exp(gl)
                    ekd = jnp.exp(gl - g)
                    eg = jnp.exp(g)
                    q, k = q_ref[b, rows, sl], k_ref[b, rows, sl]
                    kd = k * ekd
                    qg = q * eg
                    do = do_ref[b, rows, sl].astype(BF16)
                    vnb = vn_ref[b, rows, sl].astype(BF16)
                    dvn = _dot(a_ref[b, h, c].astype(BF16), do, TN) + _dot(kd.astype(BF16), dsnb, NN)
                    dvnb = dvn.astype(BF16)
                    do_on = _dot(do, jnp.concatenate([stb, vnb], axis=0), NT)
                    dqg = do_on[:, :128]
                    da_ref[b, h, c] = do_on[:, 128:]
                    dkd = _dot(vnb, dsnb, NT)
                    ds_s[b, h] = (_dot(qg.astype(BF16), do, TN) + egl * dsn - _dot(w_ref[b, rows, sl].astype(BF16), dvnb, TN))
                    du_ref[b, rows, sl] = dvnb
                    dw_ref[b, rows, sl] = (-_dot(dvnb, stb, NT)).astype(BF16)
                    dq_ref[b, rows, sl] = dqg * eg
                    dk_ref[b, rows, sl] = dkd * ekd
                    ddel = jnp.sum(dkd * kd, axis=1, keepdims=True)
                    dgl = jnp.sum(ddel, axis=0, keepdims=True) + jnp.sum(jnp.sum(st * dsn, axis=1, keepdims=True), axis=0, keepdims=True) * egl
                    col = jnp.sum(dqg * qg, axis=1, keepdims=True) - ddel + jnp.where(last, dgl, 0.0)
                    dg_all = jnp.where(lane == h, col, dg_all)
                dg_ref[b, rows, :] = dg_all

    steps = N // cps
    tok = lambda c, j: BS((B, cps * CHUNK, c), lambda n: (0, steps - 1 - n, j))
    mat = lambda d: BS((B, N_HEADS, cps, d, d), lambda n: (0, 0, steps - 1 - n, 0, 0))
    return pl.pallas_call(
        body, name="gdn_scan_bwd", grid=(steps,),
        in_specs=[tok(512, 0), tok(512, 0), tok(512, 1), tok(512, 0), tok(512, 0), tok(128, 0), mat(CHUNK), mat(128)],
        out_specs=[tok(512, 0), tok(512, 0), tok(512, 0), tok(512, 0), mat(CHUNK), tok(128, 0)],
        out_shape=[jax.ShapeDtypeStruct((B, S, 512), BF16)] * 2 + [jax.ShapeDtypeStruct((B, S, 512), F32)] * 2
        + [jax.ShapeDtypeStruct((B, N_HEADS, N, CHUNK, CHUNK), F32), jax.ShapeDtypeStruct((B, S, 128), F32)],
        scratch_shapes=[pltpu.VMEM((B, N_HEADS, 128, 128), F32)],
        compiler_params=_arb(1))(dO3, qkv3, qkv3, W3, Vn3, GB3, A, St)


def gdn_chunk_bwd(qkv, GB, Grow, Tinv, dA, dU, dW, dQ1, dK1, dG1, B, S, nc=8):
    T = B * S
    N = S // CHUNK
    nc = min(nc, N)
    nb = N // nc
    R = nc * CHUNK

    def body(q_ref, k_ref, v_ref, gb_ref, gr_ref, t_ref, da_ref, du_ref, dw_ref, dq1_ref, dk1_ref, dg1_ref, o_ref, dgb_ref):
        incl, strict = _chunk_masks(nc)
        lane = lax.broadcasted_iota(jnp.int32, (1, 128), 1)
        dg_all = dg1_ref[...]
        db_all = jnp.zeros((R, 128), F32)
        for h in range(N_HEADS):
            sl = slice(h * 128, (h + 1) * 128)
            q = q_ref[:, sl].reshape(nc, CHUNK, 128)
            k = k_ref[:, sl].reshape(nc, CHUNK, 128)
            v = v_ref[:, sl].reshape(nc, CHUNK, 128)
            gc = gb_ref[:, h:h + 1].reshape(nc, CHUNK, 1)
            beta = gb_ref[:, 4 + h:5 + h].reshape(nc, CHUNK, 1)
            gr = gr_ref[h][:, None, :]
            decay, kb, l_mat, a_mat = _chunk_local(q, k, gc, gr, beta, incl, strict)
            eg = jnp.exp(gc)
            kbg = kb * eg
            vb = v * beta
            tt = t_ref[h].astype(BF16)
            du = du_ref[:, sl].reshape(nc, CHUNK, 128).astype(BF16)
            dw = dw_ref[:, sl].reshape(nc, CHUNK, 128).astype(BF16)
            dvb = _bdot("gjc,gcv->gjv", tt, du)
            dkbg = _bdot("gjc,gck->gjk", tt, dw)
            dt = _bdot("gcv,gjv->gcj", du, vb.astype(BF16)) + _bdot("gck,gjk->gcj", dw, kbg.astype(BF16))
            tmp = _bdot("gca,gab->gcb", tt, dt.astype(BF16))
            dl = jnp.where(strict, -_bdot("gcb,gbd->gcd", tmp.astype(BF16), tt), 0.0)
            da = da_ref[h]
            dm = (dl * decay).astype(BF16)
            dqk = (da * decay).astype(BF16)
            kbf = k.astype(BF16)
            dkb = _bdot("gcj,gjd->gcd", dm, kbf) + dkbg * eg
            dk = (_bdot("gcj,gcd->gjd", dm, kb.astype(BF16)) + _bdot("gcj,gcd->gjd", dqk, q.astype(BF16))
                  + dk1_ref[:, sl].reshape(nc, CHUNK, 128) + dkb * beta)
            dq = _bdot("gcj,gjd->gcd", dqk, kbf) + dq1_ref[:, sl].reshape(nc, CHUNK, 128)
            e = dl * l_mat + da * a_mat
            dgc = (jnp.sum(e, axis=2, keepdims=True) - jnp.sum(jnp.swapaxes(e, 1, 2), axis=2, keepdims=True)
                   + jnp.sum(dkbg * kbg, axis=2, keepdims=True))
            dbeta = jnp.sum(dkb * k, axis=2, keepdims=True) + jnp.sum(dvb * v, axis=2, keepdims=True)
            o_ref[:, sl] = dq.reshape(R, 128)
            o_ref[:, 512 + h * 128:512 + (h + 1) * 128] = dk.reshape(R, 128)
            o_ref[:, 1024 + h * 128:1024 + (h + 1) * 128] = (dvb * beta).reshape(R, 128)
            dg_all = dg_all + jnp.where(lane == h, dgc.reshape(R, 1), 0.0)
            db_all = jnp.where(lane == 4 + h, dbeta.reshape(R, 1), db_all)
        t = _chunk_row(R)
        for s in (1, 2, 4, 8, 16, 32):
            dg_all = dg_all + jnp.where(t + s < CHUNK, pltpu.roll(dg_all, R - s, 0), 0.0)
        dgb_ref[...] = jnp.where(lane < 4, dg_all, db_all)

    rowb = lambda c, j: BS((R, c), lambda b, n: (b * nb + n, j))
    mat = BS((None, N_HEADS, nc, CHUNK, CHUNK), lambda b, n: (b, 0, n, 0, 0))
    return pl.pallas_call(
        body, name="gdn_chunk_bwd", grid=(B, nb),
        in_specs=[rowb(512, 0), rowb(512, 1), rowb(512, 2), rowb(128, 0),
                  BS((None, N_HEADS, nc, CHUNK), lambda b, n: (b, 0, n, 0)), mat, mat,
                  rowb(512, 0), rowb(512, 0), rowb(512, 0), rowb(512, 0), rowb(128, 0)],
        out_specs=[rowb(GDN_QKV, 0), rowb(128, 0)],
        out_shape=[jax.ShapeDtypeStruct((T, GDN_QKV), F32), jax.ShapeDtypeStruct((T, 128), F32)],
        compiler_params=_arb(2))(qkv, qkv, qkv, GB, Grow, Tinv, dA, dU, dW, dQ1, dK1, dG1)


def _gdn_out_norm(og, gg):
    outs, xhs, rs = [], [], []
    for h in range(N_HEADS):
        seg = og[:, h * 128:(h + 1) * 128]
        r = lax.rsqrt(jnp.mean(seg * seg, axis=-1, keepdims=True) + EPS)
        xh = seg * r
        outs.append(xh * gg)
        xhs.append(xh)
        rs.append(r)
    return outs, xhs, rs


def merge_fwd(o_mla, o_gdn, o_mem, P, x, tgt, w_out, g_gdn, g_fin, tm=256):
    T = x.shape[0]
    tm = min(tm, T)

    def body(om_ref, og_ref, oc_ref, gate_ref, x_ref, t_ref, w_ref, gg_ref, gf_ref, mix_ref, dx_ref, dxb_ref, sq_ref, gnf_ref):
        @pl.when(pl.program_id(0) == 0)
        def _():
            sq_ref[...] = jnp.zeros_like(sq_ref)
            gnf_ref[...] = jnp.zeros_like(gnf_ref)

        ogn, _, _ = _gdn_out_norm(og_ref[...].astype(F32), gg_ref[...])
        cat = jnp.concatenate([om_ref[...].astype(F32)] + ogn + [oc_ref[...].astype(F32)], axis=1)
        gt = gate_ref[...].astype(F32)
        mixed = (cat * (gt * _sigmoid(gt))).astype(BF16)
        mix_ref[...] = mixed
        x2 = x_ref[...] + _dot(mixed, w_ref[...], NN)
        r2 = lax.rsqrt(jnp.mean(x2 * x2, axis=-1, keepdims=True) + EPS)
        xh = x2 * r2
        gf = gf_ref[...]
        diff = xh * gf - t_ref[...]
        sq_ref[...] += jnp.sum(diff * diff, axis=0, keepdims=True)
        dy = diff * (1.0 / D_MODEL)
        gnf_ref[...] += jnp.sum(dy * xh, axis=0, keepdims=True)
        dxh = dy * gf
        dx = r2 * (dxh - xh * jnp.mean(dxh * xh, axis=-1, keepdims=True))
        dx_ref[...] = dx
        dxb_ref[...] = dx.astype(BF16)

    rowb = lambda c, j=0: BS((tm, c), lambda i: (i, j))
    full = lambda r, c: BS((r, c), lambda i: (0, 0))
    return pl.pallas_call(
        body, name="merge_fwd", grid=(T // tm,),
        in_specs=[rowb(512), rowb(512), rowb(512), rowb(D_MIX, OFF_GATE // D_MIX), rowb(D_MODEL), rowb(D_MODEL),
                  full(D_MIX, D_MODEL), full(1, 128), full(1, D_MODEL)],
        out_specs=[rowb(D_MIX), rowb(D_MODEL), rowb(D_MODEL), full(1, D_MODEL), full(1, D_MODEL)],
        out_shape=[jax.ShapeDtypeStruct((T, D_MIX), BF16), jax.ShapeDtypeStruct((T, D_MODEL), F32),
                   jax.ShapeDtypeStruct((T, D_MODEL), BF16),
                   jax.ShapeDtypeStruct((1, D_MODEL), F32), jax.ShapeDtypeStruct((1, D_MODEL), F32)],
        compiler_params=_arb(1))(o_mla, o_gdn, o_mem, P, x, tgt, w_out, g_gdn, g_fin)


def merge_bwd(dx2, o_mla, o_gdn, o_mem, P, w_out, g_gdn, tm=256):
    T = dx2.shape[0]
    tm = min(tm, T)

    def body(dx_ref, om_ref, og_ref, oc_ref, gate_ref, w_ref, gg_ref, dgate_ref, dom_ref, dog_ref, doc_ref, ggn_ref):
        @pl.when(pl.program_id(0) == 0)
        def _():
            ggn_ref[...] = jnp.zeros_like(ggn_ref)

        gg = gg_ref[...]
        dmix = _dot(dx_ref[...].astype(BF16), w_ref[...], NT)
        ogn, xhs, rs = _gdn_out_norm(og_ref[...].astype(F32), gg)
        cat = jnp.concatenate([om_ref[...].astype(F32)] + ogn + [oc_ref[...].astype(F32)], axis=1)
        gt = gate_ref[...].astype(F32)
        sg = _sigmoid(gt)
        dgate_ref[...] = (dmix * cat * (sg * (1.0 + gt * (1.0 - sg)))).astype(BF16)
        dcat = dmix * (gt * sg)
        dom_ref[...] = dcat[:, :512].astype(BF16)
        doc_ref[...] = dcat[:, 1024:].astype(BF16)
        acc = jnp.zeros((1, 128), F32)
        for h in range(N_HEADS):
            dseg = dcat[:, 512 + h * 128:512 + (h + 1) * 128]
            acc = acc + jnp.sum(dseg * xhs[h], axis=0, keepdims=True)
            dxh = dseg * gg
            dog_ref[:, h * 128:(h + 1) * 128] = (rs[h] * (dxh - xhs[h] * jnp.mean(dxh * xhs[h], axis=-1, keepdims=True))).astype(BF16)
        ggn_ref[...] += acc

    rowb = lambda c, j=0: BS((tm, c), lambda i: (i, j))
    full = lambda r, c: BS((r, c), lambda i: (0, 0))
    return pl.pallas_call(
        body, name="merge_bwd", grid=(T // tm,),
        in_specs=[rowb(D_MODEL), rowb(512), rowb(512), rowb(512), rowb(D_MIX, OFF_GATE // D_MIX),
                  full(D_MIX, D_MODEL), full(1, 128)],
        out_specs=[rowb(D_MIX), rowb(512), rowb(512), rowb(512), full(1, 128)],
        out_shape=[jax.ShapeDtypeStruct((T, D_MIX), BF16)] + [jax.ShapeDtypeStruct((T, 512), BF16)] * 3
        + [jax.ShapeDtypeStruct((1, 128), F32)],
        compiler_params=_arb(1))(dx2, o_mla, o_gdn, o_mem, P, w_out, g_gdn)


def in_proj_bwd(dP, wp, x, dx2, gain, after, tm=512):
    T, n = x.shape
    tm = min(tm, T)
    k = len(dP)
    widths = [p.shape[1] for p in dP]
    offs = [sum(widths[:i]) for i in range(k)]

    def body(*refs):
        w_ref, x_ref, dx2_ref, g_ref = refs[k:k + 4]
        o_ref, acc_ref = refs[-2:]

        @pl.when(pl.program_id(0) == 0)
        def _():
            acc_ref[...] = jnp.zeros_like(acc_ref)

        dy = None
        for a_ref, off, w in zip(refs[:k], offs, widths):
            d = _dot(a_ref[...], w_ref[off:off + w, :], NN)
            dy = d if dy is None else dy + d
        xv = x_ref[...]
        r = lax.rsqrt(jnp.mean(xv * xv, axis=-1, keepdims=True) + EPS)
        xh = xv * r
        acc_ref[...] += jnp.sum(dy * xh, axis=0, keepdims=True)
        dxh = dy * g_ref[...]
        o_ref[...] = dx2_ref[...] + r * (dxh - xh * jnp.mean(dxh * xh, axis=-1, keepdims=True))

    rowb = BS((tm, n), lambda i: (i, 0))
    full = BS((1, n), lambda i: (0, 0))
    return pl.pallas_call(
        body, name="in_proj_bwd", grid=(T // tm,),
        in_specs=[BS((tm, w), lambda i: (i, 0)) for w in widths]
        + [BS(wp.shape, lambda i: (0, 0), pipeline_mode=pl.Buffered(1)), rowb, rowb, full, BS(memory_space=pl.ANY)],
        out_specs=[rowb, full], out_shape=[jax.ShapeDtypeStruct((T, n), F32), jax.ShapeDtypeStruct((1, n), F32)],
        compiler_params=_arb(1))(*dP, wp, x, dx2, gain, after)


W_IN_SHARD = D_IN // 4
_GDN0 = Q_LORA + KV_LORA + MLA_ROPE
_AB0 = _GDN0 + GDN_QKV
_MEMQ0 = _AB0 + 2 * N_HEADS
_GATE0 = _MEMQ0 + N_HEADS * MEM_DH


def _w_in_row_map():
    a, m, gt = _AB0 - 2 * W_IN_SHARD, _MEMQ0 - 2 * W_IN_SHARD, _GATE0 - 2 * W_IN_SHARD
    e0 = OFF_GDN + W_IN_SHARD - _GDN0
    e1 = e0 + W_IN_SHARD
    e2 = OFF_GATE + W_IN_SHARD - gt
    return [(0, 0, 0, 672), (0, 672, 704, 32), (2, a, 768, m - a), (2, m, OFF_MEMQ, gt - m), (0, _GDN0, OFF_GDN, W_IN_SHARD - _GDN0),
            (1, 0, e0, W_IN_SHARD), (2, 0, e1, a), (2, gt, OFF_GATE, W_IN_SHARD - gt), (3, 0, e2, W_IN_SHARD)]


_W_IN_ZERO_ROWS = [(672, 32), (736, 32), (776, 248)]
W_IN_LANES = 256


def pad_w_in_t(shards):
    per_half = shards.shape[3] // W_IN_LANES

    def body(s_ref, o_ref):
        for r0, n in _W_IN_ZERO_ROWS:
            o_ref[r0:r0 + n, :] = jnp.zeros((n, W_IN_LANES), o_ref.dtype)
        for q, src, dst, n in _w_in_row_map():
            o_ref[dst:dst + n, :] = s_ref[q, src:src + n, :]

    return pl.pallas_call(
        body, name="pad_w_in_t", grid=(D_MODEL // W_IN_LANES,),
        in_specs=[BS((N_CHIPS, None, W_IN_SHARD, W_IN_LANES), lambda j: (0, j // per_half, 0, j % per_half))],
        out_specs=BS((N_PAD, W_IN_LANES), lambda j: (0, j)),
        out_shape=jax.ShapeDtypeStruct((N_PAD, D_MODEL), shards.dtype), compiler_params=_arb(1))(shards)


def unpad_w_in_t(g):
    def body(g_ref, o_ref):
        for q, src, dst, n in _w_in_row_map():
            o_ref[q, src:src + n, :] = g_ref[dst:dst + n, :]

    return pl.pallas_call(
        body, name="unpad_w_in_t", grid=(D_MODEL // W_IN_LANES,),
        in_specs=[BS((N_PAD, W_IN_LANES), lambda j: (0, j))], out_specs=BS((N_CHIPS, W_IN_SHARD, W_IN_LANES), lambda j: (0, 0, j)),
        out_shape=jax.ShapeDtypeStruct((N_CHIPS, W_IN_SHARD, D_MODEL), g.dtype), compiler_params=_arb(1))(g)


def _perm_w_kv_b(s):
    return jnp.concatenate([s[h, :, :128] for h in range(N_HEADS)] + [s[h, :, 128:] for h in range(N_HEADS)], axis=1)


def _unperm_w_kv_b(g):
    return jnp.stack([jnp.concatenate([g[:, h * 128:(h + 1) * 128], g[:, 512 + h * 128:512 + (h + 1) * 128]], axis=1)
                      for h in range(N_HEADS)])


def _lane_row(v4):
    return jnp.pad(v4.reshape(1, -1).astype(F32), ((0, 0), (0, 128 - v4.size)))


N_CHIPS = 4
MESH = pl.DeviceIdType.MESH
ANY = BS(memory_space=pl.ANY)


def _place():
    return lax.axis_index("x"), lax.axis_index("y"), lax.axis_index("c")


def _other_chips(x, y):
    return [(1 - x, y), (x, 1 - y), (1 - x, 1 - y)]


def _half(split, which):
    axis, size = split
    ds = pl.ds(pl.multiple_of(which * size, 16 if axis == 0 else 128), size)
    return (ds, slice(None)) if axis == 0 else (slice(None), ds)


SEM = BS(memory_space=pltpu.SEMAPHORE)
HBM = BS(memory_space=pltpu.HBM)
_IN_HBM = lambda a: pltpu.with_memory_space_constraint(a, pltpu.HBM)
_SIDE_EFFECT = pltpu.SideEffectType.DATAFLOW_SIDE_EFFECTING


def _late_gather_copies(s_refs, l_refs, send_sems, recv_sems, local_sems, with_arrivals):
    x, y, c = _place()
    sends, recvs, locals_ = [], [], []
    for i, (s_ref, l_ref) in enumerate(zip(s_refs, l_refs)):
        locals_.append(pltpu.make_async_copy(s_ref, l_ref.at[2 * x + y], local_sems.at[i]))
        for j, (px, py) in enumerate(_other_chips(x, y)):
            k = 3 * i + j
            sends.append(pltpu.make_async_remote_copy(src_ref=s_ref, dst_ref=l_ref.at[2 * x + y], send_sem=send_sems.at[k],
                                                      recv_sem=recv_sems.at[k], device_id=(px, py, c), device_id_type=MESH))
            if with_arrivals:
                recvs.append(pltpu.make_async_remote_copy(src_ref=s_ref, dst_ref=l_ref.at[2 * px + py], send_sem=send_sems.at[k],
                                                          recv_sem=recv_sems.at[k], device_id=(px, py, c), device_id_type=MESH))
    return sends, recvs, locals_


def late_gather_start(shards, after, name):
    n = len(shards)

    def body(*refs):
        s_refs, l_refs = refs[:n], refs[n:2 * n]
        send_sems, recv_sems, local_sems = refs[2 * n + 1:2 * n + 4]
        token = refs[-1]
        sends, _, locals_ = _late_gather_copies(s_refs, l_refs, send_sems, recv_sems, local_sems, False)
        for cp in locals_ + sends:
            cp.start()
        token[...] = jnp.zeros_like(token)

    lands = [lax.empty((N_CHIPS,) + s.shape, s.dtype) for s in shards]
    hbm_like = lambda a: pltpu.HBM(a.shape, a.dtype)
    out = pl.pallas_call(
        body, name=name,
        out_shape=[pltpu.SemaphoreType.DMA((3 * n,)), pltpu.SemaphoreType.DMA((3 * n,)), pltpu.SemaphoreType.DMA((n,))]
        + [hbm_like(s) for s in shards] + [hbm_like(l) for l in lands] + [jax.ShapeDtypeStruct((8, 128), F32)],
        in_specs=[HBM] * (2 * n) + [BS(memory_space=pl.ANY)], out_specs=[SEM] * 3 + [HBM] * (2 * n) + [BS(memory_space=pltpu.VMEM)],
        input_output_aliases={i: 3 + i for i in range(2 * n)},
        compiler_params=pltpu.CompilerParams(has_side_effects=_SIDE_EFFECT))(
            *[_IN_HBM(s) for s in shards], *[_IN_HBM(l) for l in lands], after)
    return out[:3], out[3:3 + n], out[3 + n:3 + 2 * n], out[-1]


def late_gather_wait(sems, shards, lands, after, name):
    n = len(shards)

    def body(*refs):
        s_refs, l_refs = refs[:n], refs[n:2 * n]
        send_sems, recv_sems, local_sems = refs[2 * n:2 * n + 3]
        sends, recvs, locals_ = _late_gather_copies(s_refs, l_refs, send_sems, recv_sems, local_sems, True)
        for cp in locals_:
            cp.wait()
        for cp in sends:
            cp.wait_send()
        for cp in recvs:
            cp.wait_recv()

    hbm_like = lambda a: pltpu.HBM(a.shape, a.dtype)
    out = pl.pallas_call(
        body, name=name, out_shape=[hbm_like(s) for s in shards] + [hbm_like(l) for l in lands],
        in_specs=[HBM] * (2 * n) + [SEM] * 3 + [BS(memory_space=pl.ANY)], out_specs=[HBM] * (2 * n),
        input_output_aliases={i: i for i in range(2 * n)},
        compiler_params=pltpu.CompilerParams(has_side_effects=_SIDE_EFFECT))(*shards, *lands, *sems, after)
    return out[n:]


def _half_gather_copies(s_ref, l_ref, send_sems, recv_sems, with_arrivals):
    x, y, c = _place()
    sends, recvs = [], []
    for j, (px, py) in enumerate(_other_chips(x, y)):
        sends.append(pltpu.make_async_remote_copy(src_ref=s_ref.at[c], dst_ref=l_ref.at[2 * x + y, c], send_sem=send_sems.at[j],
                                                  recv_sem=recv_sems.at[j], device_id=(px, py, c), device_id_type=MESH))
        if with_arrivals:
            recvs.append(pltpu.make_async_remote_copy(src_ref=s_ref.at[c], dst_ref=l_ref.at[2 * px + py, c], send_sem=send_sems.at[j],
                                                      recv_sem=recv_sems.at[j], device_id=(px, py, c), device_id_type=MESH))
    return sends, recvs


def half_gather_start(shard, name):
    def body(s_ref, l_ref, send_sems, recv_sems, local_sem, s_thru, l_thru, token):
        x, y, _ = _place()
        pltpu.make_async_copy(s_ref, l_ref.at[2 * x + y], local_sem.at[0]).start()
        for cp in _half_gather_copies(s_ref, l_ref, send_sems, recv_sems, False)[0]:
            cp.start()
        token[...] = jnp.zeros_like(token)

    land = lax.empty((N_CHIPS,) + shard.shape, shard.dtype)
    out = pl.pallas_call(
        body, name=name,
        out_shape=[pltpu.SemaphoreType.DMA((3,)), pltpu.SemaphoreType.DMA((3,)), pltpu.SemaphoreType.DMA((1,)),
                   pltpu.HBM(shard.shape, shard.dtype), pltpu.HBM(land.shape, land.dtype), jax.ShapeDtypeStruct((8, 128), F32)],
        in_specs=[HBM, HBM], out_specs=[SEM] * 3 + [HBM, HBM, BS(memory_space=pltpu.VMEM)],
        input_output_aliases={0: 3, 1: 4},
        compiler_params=pltpu.CompilerParams(has_side_effects=_SIDE_EFFECT))(_IN_HBM(shard), _IN_HBM(land))
    return out[:3], out[3], out[4], out[5]


def half_gather_wait(sems, shard, land, after, name):
    def body(s_ref, l_ref, send_sems, recv_sems, local_sem, *rest):
        x, y, _ = _place()
        pltpu.make_async_copy(s_ref, l_ref.at[2 * x + y], local_sem.at[0]).wait()
        sends, recvs = _half_gather_copies(s_ref, l_ref, send_sems, recv_sems, True)
        for cp in sends:
            cp.wait_send()
        for cp in recvs:
            cp.wait_recv()

    out = pl.pallas_call(
        body, name=name, out_shape=[pltpu.HBM(shard.shape, shard.dtype), pltpu.HBM(land.shape, land.dtype)],
        in_specs=[HBM, HBM] + [SEM] * 3 + [BS(memory_space=pl.ANY)] * len(after), out_specs=[HBM, HBM],
        input_output_aliases={0: 0, 1: 1},
        compiler_params=pltpu.CompilerParams(has_side_effects=_SIDE_EFFECT))(shard, land, *sems, *after)
    return out[1]


def pass_halves_to_sibling(land, name):
    def body(l_in, l_ref, send_sems, recv_sems):
        x, y, c = _place()
        copies = []
        for j, (px, py) in enumerate(_other_chips(x, y)):
            q = 2 * px + py
            give = pltpu.make_async_remote_copy(src_ref=l_ref.at[q, c], dst_ref=l_ref.at[q, c], send_sem=send_sems.at[j],
                                                recv_sem=recv_sems.at[j], device_id=(x, y, 1 - c), device_id_type=MESH)
            take = pltpu.make_async_remote_copy(src_ref=l_ref.at[q, c], dst_ref=l_ref.at[q, 1 - c], send_sem=send_sems.at[j],
                                                recv_sem=recv_sems.at[j], device_id=(x, y, 1 - c), device_id_type=MESH)
            give.start()
            copies.append((give, take))
        for give, take in copies:
            take.wait_recv()
            give.wait_send()

    return pl.pallas_call(
        body, name=name, in_specs=[ANY], out_specs=ANY, out_shape=jax.ShapeDtypeStruct(land.shape, land.dtype),
        input_output_aliases={0: 0},
        scratch_shapes=[pltpu.SemaphoreType.DMA((3,)), pltpu.SemaphoreType.DMA((3,))])(land)


def allgather_devices(block, name):
    R, C = block.shape

    def body(b_ref, o_ref, send_sems, recv_sems, local_sem):
        x, y, c = _place()
        me = 4 * x + 2 * y + c
        own = pltpu.make_async_copy(b_ref, o_ref.at[me], local_sem)
        own.start()
        copies = []
        for r in range(1, 8):
            px = 1 - x if r & 4 else x
            py = 1 - y if r & 2 else y
            pc = 1 - c if r & 1 else c
            send = pltpu.make_async_remote_copy(src_ref=b_ref, dst_ref=o_ref.at[me], send_sem=send_sems.at[r - 1],
                                                recv_sem=recv_sems.at[r - 1], device_id=(px, py, pc), device_id_type=MESH)
            recv = pltpu.make_async_remote_copy(src_ref=b_ref, dst_ref=o_ref.at[4 * px + 2 * py + pc], send_sem=send_sems.at[r - 1],
                                                recv_sem=recv_sems.at[r - 1], device_id=(px, py, pc), device_id_type=MESH)
            send.start()
            copies.append((send, recv))
        for send, recv in copies:
            recv.wait_recv()
            send.wait_send()
        own.wait()

    return pl.pallas_call(
        body, name=name, in_specs=[ANY], out_specs=ANY, out_shape=jax.ShapeDtypeStruct((8, R, C), block.dtype),
        scratch_shapes=[pltpu.SemaphoreType.DMA((7,)), pltpu.SemaphoreType.DMA((7,)), pltpu.SemaphoreType.DMA(())])(block)


def swap_sibling(arrs, name, splits=None):
    n = len(arrs)

    def sent(a_ref, i, c):
        return a_ref if splits is None else a_ref.at[(slice(None),) + _half(splits[i], 1 - c)]

    def out_shape(a, i):
        if splits is None:
            return a.shape
        axis, size = splits[i]
        return (a.shape[0], size, a.shape[2]) if axis == 0 else (a.shape[0], a.shape[1], size)

    def body(*refs):
        a_refs, o_refs = refs[:n], refs[n:2 * n]
        send_sems, recv_sems = refs[2 * n:]
        x, y, c = _place()
        copies = [pltpu.make_async_remote_copy(src_ref=sent(a_ref, i, c), dst_ref=o_ref, send_sem=send_sems.at[i],
                                               recv_sem=recv_sems.at[i], device_id=(x, y, 1 - c), device_id_type=MESH)
                  for i, (a_ref, o_ref) in enumerate(zip(a_refs, o_refs))]
        for cp in copies:
            cp.start()
        for cp in copies:
            cp.wait()

    return pl.pallas_call(
        body, name=name, in_specs=[ANY] * n, out_specs=[ANY] * n,
        out_shape=[jax.ShapeDtypeStruct(out_shape(a, i), a.dtype) for i, a in enumerate(arrs)],
        scratch_shapes=[pltpu.SemaphoreType.DMA((n,)), pltpu.SemaphoreType.DMA((n,))])(*arrs)


def _exchange_copies(p_refs, l_refs, send_sems, recv_sems):
    x, y, c = _place()
    return [pltpu.make_async_remote_copy(src_ref=p_ref.at[2 * px + py], dst_ref=l_ref.at[j], send_sem=send_sems.at[3 * i + j],
                                         recv_sem=recv_sems.at[3 * i + j], device_id=(px, py, c), device_id_type=MESH)
            for i, (p_ref, l_ref) in enumerate(zip(p_refs, l_refs)) for j, (px, py) in enumerate(_other_chips(x, y))]


def exchange_chips_start(parts, name):
    n = len(parts)

    def body(*refs):
        send_sems, recv_sems = refs[2 * n:2 * n + 2]
        for cp in _exchange_copies(refs[:n], refs[n:2 * n], send_sems, recv_sems):
            cp.start()
        refs[-1][...] = jnp.zeros_like(refs[-1])

    lands = [lax.empty((3,) + p.shape[1:], p.dtype) for p in parts]
    hbm_like = lambda a: pltpu.HBM(a.shape, a.dtype)
    out = pl.pallas_call(
        body, name=name,
        out_shape=[pltpu.SemaphoreType.DMA((3 * n,)), pltpu.SemaphoreType.DMA((3 * n,))]
        + [hbm_like(p) for p in parts] + [hbm_like(l) for l in lands] + [jax.ShapeDtypeStruct((8, 128), F32)],
        in_specs=[HBM] * (2 * n), out_specs=[SEM] * 2 + [HBM] * (2 * n) + [BS(memory_space=pltpu.VMEM)],
        input_output_aliases={i: 2 + i for i in range(2 * n)},
        compiler_params=pltpu.CompilerParams(has_side_effects=_SIDE_EFFECT))(*[_IN_HBM(p) for p in parts], *[_IN_HBM(l) for l in lands])
    return out[:2], out[2:2 + n], out[2 + n:2 + 2 * n], out[-1]


def exchange_chips_wait(sems, parts, lands, after, name):
    n = len(parts)

    def body(*refs):
        send_sems, recv_sems = refs[2 * n:2 * n + 2]
        for cp in _exchange_copies(refs[:n], refs[n:2 * n], send_sems, recv_sems):
            cp.wait_send()
            cp.wait_recv()

    hbm_like = lambda a: pltpu.HBM(a.shape, a.dtype)
    out = pl.pallas_call(
        body, name=name, out_shape=[hbm_like(p) for p in parts] + [hbm_like(l) for l in lands],
        in_specs=[HBM] * (2 * n) + [SEM] * 2 + [BS(memory_space=pl.ANY)], out_specs=[HBM] * (2 * n),
        input_output_aliases={i: i for i in range(2 * n)},
        compiler_params=pltpu.CompilerParams(has_side_effects=_SIDE_EFFECT))(*parts, *lands, *sems, after)
    return out[n:]


def _half_block(shape2, split):
    axis, size = split
    return (size, shape2[1]) if axis == 0 else (shape2[0], size)


def add_pairs(parts, halves, splits, core, name):
    n = len(parts)

    def body(s_ref, *refs):
        for a_ref, b_ref, o_ref in zip(refs[:n], refs[n:2 * n], refs[2 * n:]):
            o_ref[...] = (a_ref[...].astype(F32) + b_ref[...].astype(F32)).astype(BF16)

    def mine(i):
        blk = (None,) + _half_block(parts[i].shape[1:], splits[i])
        if splits[i][0] == 0:
            return BS(blk, lambda q, s: (q, s[0], 0))
        return BS(blk, lambda q, s: (q, 0, s[0]))

    half_specs = [BS((None,) + h.shape[1:], lambda q, s: (q, 0, 0)) for h in halves]
    return pl.pallas_call(
        body, name=name,
        grid_spec=pltpu.PrefetchScalarGridSpec(num_scalar_prefetch=1, grid=(N_CHIPS,),
                                               in_specs=[mine(i) for i in range(n)] + half_specs, out_specs=half_specs),
        out_shape=[jax.ShapeDtypeStruct(h.shape, BF16) for h in halves], compiler_params=_arb(1))(core, *parts, *halves)


def add_fives(parts, halves, from_chips, splits, chip_core, name):
    n = len(parts)

    def body(s_ref, *refs):
        for a_ref, b_ref, p_ref, o_ref in zip(refs[:n], refs[n:2 * n], refs[2 * n:3 * n], refs[3 * n:]):
            s = a_ref[...].astype(F32) + b_ref[...].astype(F32)
            for j in range(3):
                s = s + p_ref[j].astype(F32)
            o_ref[...] = s

    def mine(i):
        blk = (None,) + _half_block(parts[i].shape[1:], splits[i])
        if splits[i][0] == 0:
            return BS(blk, lambda g, s: (s[0], s[1], 0))
        return BS(blk, lambda g, s: (s[0], 0, s[1]))

    half_specs = [BS((None,) + h.shape[1:], lambda g, s: (s[0], 0, 0)) for h in halves]
    chip_specs = [BS(p.shape, lambda g, s: (0, 0, 0)) for p in from_chips]
    out_specs = [BS(h.shape[1:], lambda g, s: (0, 0)) for h in halves]
    return pl.pallas_call(
        body, name=name,
        grid_spec=pltpu.PrefetchScalarGridSpec(num_scalar_prefetch=1, grid=(1,),
                                               in_specs=[mine(i) for i in range(n)] + half_specs + chip_specs, out_specs=out_specs),
        out_shape=[jax.ShapeDtypeStruct(h.shape[1:], F32) for h in halves], compiler_params=_arb(1))(chip_core, *parts, *halves, *from_chips)


def sum_leading(a, name):
    def body(a_ref, o_ref):
        s = a_ref[0]
        for j in range(1, a.shape[0]):
            s = s + a_ref[j]
        o_ref[...] = s

    return pl.pallas_call(body, name=name, out_shape=jax.ShapeDtypeStruct(a.shape[1:], a.dtype))(a)


def _adamw_math(w, g, m, v):
    mn = ADAM_B1 * m + (1.0 - ADAM_B1) * g
    vn = ADAM_B2 * v + (1.0 - ADAM_B2) * (g * g)
    m_hat = mn / (1.0 - ADAM_B1 ** ADAM_STEP)
    v_hat = vn / (1.0 - ADAM_B2 ** ADAM_STEP)
    return -ADAM_LR * (m_hat / (jnp.sqrt(v_hat) + ADAM_EPS) + ADAM_WD * w), mn, vn


def adamw(w, g, m, v, name):
    R, C = g.shape
    lead = (None,) * (w.ndim - 2)

    def body(w_ref, g_ref, m_ref, v_ref, d_ref, mo_ref, vo_ref):
        d_ref[...], mo_ref[...], vo_ref[...] = _adamw_math(w_ref[...], g_ref[...], m_ref[...], v_ref[...])

    wblk = BS(lead + (R, C), lambda i: (0,) * w.ndim)
    gblk = BS((R, C), lambda i: (0, 0))
    return pl.pallas_call(
        body, name=name, grid=(1,), in_specs=[wblk, gblk, wblk, wblk], out_specs=[wblk] * 3,
        out_shape=[jax.ShapeDtypeStruct(w.shape, F32)] * 3, compiler_params=_arb(1))(w, g, m, v)


SMALL_ROWS = 16
CONV_ROW0 = 8
LOSS_ROW = 14


def pack_small(small_grads, g_ab, g_conv, sq):
    present = [a for a in small_grads if a is not None]

    def body(*refs):
        ab_ref, conv_ref, sq_ref, o_ref = refs[len(present):]
        o_ref[...] = jnp.zeros_like(o_ref)
        it = iter(refs[:len(present)])
        for i, a in enumerate(small_grads):
            if a is not None:
                o_ref[i:i + 1, 0:a.shape[1]] = next(it)[...]
        o_ref[3:4, 0:128] = ab_ref[0:1, :]
        o_ref[4:5, 0:128] = ab_ref[1:2, :]
        half = 512
        for k in range(GDN_CONV * GDN_QKV // half):
            src_r, src_c = (k * half) // GDN_QKV, (k * half) % GDN_QKV
            dst_r, dst_c = CONV_ROW0 + (k * half) // 1024, (k * half) % 1024
            o_ref[dst_r:dst_r + 1, dst_c:dst_c + half] = conv_ref[src_r:src_r + 1, src_c:src_c + half]
        o_ref[LOSS_ROW:LOSS_ROW + 1, :] = sq_ref[...]

    return pl.pallas_call(body, name="pack_small", out_shape=jax.ShapeDtypeStruct((SMALL_ROWS, 1024), F32))(
        *present, g_ab, g_conv, sq)


def adamw_small(block, ws, ms, vs):
    k = len(ws)

    def body(b_ref, *refs):
        outs = refs[3 * k:]
        for i in range(k):
            n = ws[i].shape[1]
            g = b_ref[i:i + 1, 0:n]
            d, mn, vn = _adamw_math(refs[i][...], g, refs[k + i][...], refs[2 * k + i][...])
            outs[4 * i][...], outs[4 * i + 1][...], outs[4 * i + 2][...], outs[4 * i + 3][...] = g, d, mn, vn

    out = pl.pallas_call(
        body, name="adamw_small",
        out_shape=[jax.ShapeDtypeStruct(w.shape, F32) for w in ws for _ in range(4)])(block, *ws, *ms, *vs)
    return [out[4 * i:4 * i + 4] for i in range(k)]


def adamw_halves(w, mine, other, m, v, split, core, name):
    R, C = w.shape[-2:]
    axis, size = split
    lead = (None,) * (w.ndim - 2)
    zeros = (0,) * (w.ndim - 2)
    if axis == 0:
        tr = size if size <= 256 else next(t for t in range(256, 7, -1) if size % t == 0 and t % 8 == 0)
        nb = size // tr
        whole = BS(lead + (tr, C), lambda hi, j, s: zeros + (hi * nb + j, 0))
        part = BS((tr, C), lambda hi, j, s: (j, 0))
    else:
        nb = size // 128
        whole = BS(lead + (R, 128), lambda hi, j, s: zeros + (0, hi * nb + j))
        part = BS((R, 128), lambda hi, j, s: (0, j))

    def body(s_ref, w_ref, a_ref, b_ref, m_ref, v_ref, g_ref, d_ref, mo_ref, vo_ref):
        g = jnp.where(pl.program_id(0) == s_ref[0], a_ref[...], b_ref[...])
        g_ref[...] = g
        d_ref[...], mo_ref[...], vo_ref[...] = _adamw_math(w_ref[...], g, m_ref[...], v_ref[...])

    return pl.pallas_call(
        body, name=name,
        grid_spec=pltpu.PrefetchScalarGridSpec(num_scalar_prefetch=1, grid=(2, nb),
                                               in_specs=[whole, part, part, whole, whole], out_specs=[whole] * 4),
        out_shape=[jax.ShapeDtypeStruct(w.shape, F32)] * 4, compiler_params=_arb(2))(core, w, mine, other, m, v)


def dense_bf16(w3, name):
    R, _, K = w3.shape
    kh = K // 2

    def body(w_hbm, o_ref, buf, sem):
        cp = pltpu.make_async_copy(w_hbm.at[:, 0], buf, sem)
        cp.start()
        cp.wait()
        o_ref[0] = buf[:, :kh].astype(BF16)
        o_ref[1] = buf[:, kh:].astype(BF16)

    return pl.pallas_call(
        body, name=name, in_specs=[ANY], out_specs=BS(memory_space=pltpu.VMEM), out_shape=jax.ShapeDtypeStruct((2, R, kh), BF16),
        scratch_shapes=[pltpu.VMEM((R, K), F32), pltpu.SemaphoreType.DMA(())])(w3)


ROW_BLOCK = 184


def adamw_untiled_rows(w3, mine, other, m3, v3, name):
    R, _, K = w3.shape
    kh = K // 2
    starts = list(range(0, R, ROW_BLOCK))
    sizes = [min(ROW_BLOCK, R - s) for s in starts]
    nblk = len(starts)

    def body(w_hbm, a_ref, b_ref, m_hbm, v_hbm, g_hbm, d_hbm, mo_hbm, vo_hbm,
             wbuf, mbuf, vbuf, gbuf, dbuf, mobuf, vobuf, in_sems, out_sems):
        first = lax.axis_index("c") == 0
        ins = []
        for k, (r0, n) in enumerate(zip(starts, sizes)):
            rows = pl.ds(r0, n)
            cps = [pltpu.make_async_copy(src.at[rows, 0], dst.at[rows], in_sems.at[3 * k + i])
                   for i, (src, dst) in enumerate(((w_hbm, wbuf), (m_hbm, mbuf), (v_hbm, vbuf)))]
            for cp in cps:
                cp.start()
            ins.append(cps)

        def update(rows):
            a, b = a_ref[rows, :], b_ref[rows, :]
            g = jnp.concatenate([jnp.where(first, a, b), jnp.where(first, b, a)], axis=1)
            gbuf[rows, :] = g
            dbuf[rows, :], mobuf[rows, :], vobuf[rows, :] = _adamw_math(wbuf[rows, :], g, mbuf[rows, :], vbuf[rows, :])

        outs = []
        for k, (r0, n) in enumerate(zip(starts, sizes)):
            for cp in ins[k]:
                cp.wait()
            groups, tail = n // 8, n % 8

            def group(i, carry, r0=r0):
                update(pl.ds(pl.multiple_of(r0 + i * 8, 8), 8))
                return carry

            lax.fori_loop(0, groups, group, 0)
            if tail:
                update(pl.ds(r0 + groups * 8, tail))
            rows = pl.ds(r0, n)
            cps = [pltpu.make_async_copy(src.at[rows], dst.at[rows, 0], out_sems.at[4 * k + i])
                   for i, (src, dst) in enumerate(((gbuf, g_hbm), (dbuf, d_hbm), (mobuf, mo_hbm), (vobuf, vo_hbm)))]
            for cp in cps:
                cp.start()
            outs += cps
        for cp in outs:
            cp.wait()

    vmem = BS(memory_space=pltpu.VMEM)
    return pl.pallas_call(
        body, name=name, in_specs=[ANY, vmem, vmem, ANY, ANY], out_specs=[ANY] * 4,
        out_shape=[jax.ShapeDtypeStruct(w3.shape, F32)] * 4,
        scratch_shapes=[pltpu.VMEM((R, K), F32)] * 7 + [pltpu.SemaphoreType.DMA((3 * nblk,)), pltpu.SemaphoreType.DMA((4 * nblk,))])(
            w3, mine, other, m3, v3)


def adamw_w_q_b(w, mine, other, m, v, name):
    def body(w_ref, a_ref, b_ref, m_ref, v_ref, g_ref, d_ref, mo_ref, vo_ref):
        first = lax.axis_index("c") == 0
        lo = jnp.where(first, a_ref[...], b_ref[...])
        hi = jnp.where(first, b_ref[...], a_ref[...])
        g = jnp.concatenate([lo, hi[0:32], hi[64:96]], axis=0)
        g_ref[...] = g
        d_ref[...], mo_ref[...], vo_ref[...] = _adamw_math(w_ref[...], g, m_ref[...], v_ref[...])

    return pl.pallas_call(body, name=name, out_shape=[jax.ShapeDtypeStruct(w.shape, F32)] * 4)(w, mine, other, m, v)


def local_step(x, mem, positions, tgt, norm_in, weights, big_grads_ready, q_a_norm, kv_a_norm, gdn_conv, gdn_a_log,
               gdn_dt_bias, gdn_norm, mem_norm, norm_final):
    B, S, D = x.shape
    M = mem.shape[1]
    T = B * S
    N = S // CHUNK
    x2d = x.reshape(T, D)
    mem2d = mem.reshape(B * M, D)
    tgt2d = tgt.reshape(T, D)

    alog_row, dt_row = _lane_row(gdn_a_log), _lane_row(gdn_dt_bias)

    half = MLA_ROPE // 2
    inv_freq = 1.0 / (ROPE_THETA ** (jnp.arange(half, dtype=F32) / half))
    z32 = jnp.zeros((half,), F32)
    o32 = jnp.ones((half,), F32)
    inv_row = jnp.concatenate([inv_freq, z32, inv_freq, z32]).reshape(1, 128)
    sgn_row = jnp.concatenate([-o32, z32, o32, z32]).reshape(1, 128)
    msk_row = jnp.concatenate([o32, z32, o32, z32]).reshape(1, 128)
    cos_t, sin_t = rope_tables(positions.reshape(T, 1), inv_row, sgn_row, msk_row, after=weights[3])

    h = rms_fwd(x2d, norm_in, "rms_in", after=weights[3])
    wp, behind = weights[0]((h, cos_t))
    P = mm(h, wp, "nt", BF16, "in_proj", bm=512, bn=1536, n_outer=True, after=behind)
    wq, wkv = weights[1](P)
    Q, K, V, qn, kvn = mla_prep(P, q_a_norm, kv_a_norm, wq, wkv, cos_t, sin_t)
    o_mla, lse = mla_attn_fwd(Q, K, V, B, S)
    qkv = gdn_prep_fwd(P, gdn_conv, B, S)
    GB = gdn_gate_fwd(P, alog_row, dt_row, B, S)
    Grow = jnp.transpose(GB[:, :N_HEADS].reshape(B, N, CHUNK, N_HEADS), (0, 3, 1, 2))
    U, W, Tinv, A = gdn_chunk_fwd(qkv, GB, Grow, B, S)
    qkv3, GB3 = qkv.reshape(B, S, GDN_QKV), GB.reshape(B, S, 128)
    W3 = W.reshape(B, S, 512)
    o_gdn3, Vn3, St = gdn_scan_fwd(qkv3, U.reshape(B, S, 512), W3, GB3, A, B, S)
    o_gdn = o_gdn3.reshape(T, 512)
    w_mem_kv, w_out = weights[2](o_gdn)
    memn = rms_fwd(mem2d, mem_norm, "rms_mem")
    MKV = mm(memn, w_mem_kv, "nn", BF16, "mem_kv_proj")
    o_mem = mem_attn_fwd(P, MKV, B, S, M)
    mixed, dx2, dx2b, sq, g_norm_final = merge_fwd(o_mla, o_gdn, o_mem, P, x2d, tgt2d, w_out, gdn_norm, norm_final.reshape(1, D))

    g_w_out = mm(mixed, dx2b, "tn", BF16, "grad_w_out")
    dgate, do_mla, do_gdn, do_mem, g_gdn_norm = merge_bwd(dx2b, o_mla, o_gdn, o_mem, P, w_out, gdn_norm)

    dmemq, dMKV = mem_attn_bwd(P, MKV, do_mem, B, S, M)
    g_w_mem_kv = mm(memn, dMKV, "tn", BF16, "grad_w_mem_kv")
    started_early = big_grads_ready(dict(w_mem_kv=g_w_mem_kv, w_out=g_w_out), "early")
    dmemn = mm(dMKV, w_mem_kv, "nt", F32, "d_memn", after=(started_early,))
    g_mem_norm = gain_grad(mem2d, dmemn, "grad_mem_norm")

    dU3, dW3, dQ13, dK13, dA, dG13 = gdn_scan_bwd(do_gdn.reshape(B, S, 512), qkv3, W3, Vn3, GB3, A, St, B, S)
    r2 = lambda a: a.reshape(T, a.shape[-1])
    dqkv, dGB = gdn_chunk_bwd(qkv, GB, Grow, Tinv, dA, r2(dU3), r2(dW3), r2(dQ13), r2(dK13), r2(dG13), B, S)
    dPg, g_conv = gdn_prep_bwd(P, dqkv, gdn_conv, B, S)
    dab, g_ab = gdn_gate_bwd(P, dGB, alog_row, dt_row, B, S)

    dQ, dK, dV = mla_attn_bwd(Q, K, V, o_mla, do_mla, lse, B, S)
    dq_lin, dkv_lin, dPm, g_q_a_norm, g_kv_a_norm = mla_proj_bwd(dQ, dK, dV, cos_t, sin_t, P, dab, wq, wkv, q_a_norm, kv_a_norm)
    g_wq = mm(dq_lin, qn, "tn", BF16, "grad_w_q_b")
    g_wkv = mm(kvn, dkv_lin, "tn", BF16, "grad_w_kv_b")

    dP = [dPm, dmemq, dPg, dgate]
    g_wp = mm_cols_tn(dP, h, BF16, "grad_w_in")
    started = big_grads_ready(dict(w_in=g_wp, w_q_b=g_wq, w_kv_b=g_wkv), "late")
    grad_x, g_norm_in = in_proj_bwd(dP, wp, x2d, dx2, norm_in, started)

    grads = dict(
        norm_in=g_norm_in, q_a_norm=g_q_a_norm, kv_a_norm=g_kv_a_norm, gdn_conv=g_conv,
        gdn_a_log_dt_bias=g_ab, gdn_norm=g_gdn_norm,
        mem_norm=g_mem_norm, norm_final=g_norm_final)
    return sq, grad_x.reshape(B, S, D), grads


def kernel(x, mem, positions, norm_in, w_in, q_a_norm, w_q_b, kv_a_norm, w_kv_b, gdn_conv, gdn_a_log, gdn_dt_bias, gdn_norm, mem_norm, w_mem_kv, w_out, norm_final, loss_target, m_norm_in, m_w_in, m_q_a_norm, m_w_q_b, m_kv_a_norm, m_w_kv_b, m_gdn_conv, m_gdn_a_log, m_gdn_dt_bias, m_gdn_norm, m_mem_norm, m_w_mem_kv, m_w_out, m_norm_final, v_norm_in, v_w_in, v_q_a_norm, v_w_q_b, v_kv_a_norm, v_w_kv_b, v_gdn_conv, v_gdn_a_log, v_gdn_dt_bias, v_gdn_norm, v_mem_norm, v_w_mem_kv, v_w_out, v_norm_final):
    B = x.shape[0]
    cx, cy, cc = lax.axis_index("x"), lax.axis_index("y"), lax.axis_index("c")
    chip = 2 * cx + cy

    big_names = ("w_in", "w_q_b", "w_kv_b", "w_mem_kv", "w_out")
    rows_major = lambda a: jnp.transpose(a, (2, 0, 1))
    w_in3, m_in3, v_in3 = rows_major(w_in), rows_major(m_w_in), rows_major(v_w_in)
    w_qb_t, m_qb_t, v_qb_t = jnp.transpose(w_q_b[0]), jnp.transpose(m_w_q_b[0]), jnp.transpose(v_w_q_b[0])
    z32 = jnp.zeros((32, Q_LORA), BF16)
    qb_bf = w_qb_t.astype(BF16)
    qb_padded = jnp.concatenate([qb_bf[:160], z32, qb_bf[160:], z32])
    shards = [dense_bf16(w_in3, "w_in_bf16"), qb_padded, w_kv_b[0].astype(BF16), w_mem_kv[0].astype(BF16), w_out[0].astype(BF16)]
    splits = [(1, D_MODEL // 2)] + [(0, s.shape[0] // 2) for s in shards[1:]]
    *w_in_flight, w_in_started = half_gather_start(shards[0], "w_in_gather_start")
    conv_all = allgather_devices(gdn_conv[0], "allgather_conv")
    conv_cols = gdn_conv.shape[2]
    conv_full = jnp.transpose(conv_all[0::2], (1, 0, 2)).reshape(GDN_CONV, N_CHIPS * conv_cols)
    late_shapes = [(N_CHIPS,) + s.shape for s in shards[1:]]
    late = {}

    def w_in_ready(after):
        g_in = pass_halves_to_sibling(half_gather_wait(*w_in_flight, after, "w_in_gather_wait"), "w_in_gather_sibling")
        *late["a"], started_a = late_gather_start(shards[1:3], g_in, "late_gather_qkv_start")
        *late["b"], started_b = late_gather_start(shards[3:], started_a, "late_gather_mem_out_start")
        return pad_w_in_t(g_in), (started_a, started_b)

    def late_qkv(after):
        g_qb, g_kvb = late_gather_wait(*late["a"], after, "late_gather_qkv_wait")
        return g_qb.reshape(-1, Q_LORA), _perm_w_kv_b(g_kvb)

    def late_mem_out(after):
        g_mem, g_out_w = late_gather_wait(*late["b"], after, "late_gather_mem_out_wait")
        return g_mem.reshape(-1, g_mem.shape[2]), g_out_w.reshape(-1, g_out_w.shape[2])

    weights = (w_in_ready, late_qkv, late_mem_out, (w_in_started,))

    core = jnp.stack([cc]).astype(jnp.int32)
    chip_core = jnp.stack([chip, cc]).astype(jnp.int32)
    exchanges = {}
    by_chip = dict(w_in=unpad_w_in_t, w_q_b=lambda a: a.reshape(late_shapes[0]), w_kv_b=_unperm_w_kv_b,
                   w_mem_kv=lambda a: a.reshape(late_shapes[2]), w_out=lambda a: a.reshape(late_shapes[3]))

    def big_grads_ready(gb, group):
        idx = [big_names.index(n) for n in gb]
        parts = [by_chip[n](a) for n, a in gb.items()]
        sp = [splits[i] for i in idx]
        from_sibling = swap_sibling(parts, "rs_sibling_partial_" + group, sp)
        chip_sums = add_pairs(parts, from_sibling, sp, core, "rs_add_sibling_" + group)
        sems, sums_thru, lands, token = exchange_chips_start(chip_sums, "rs_exchange_start_" + group)
        exchanges[group] = dict(idx=idx, parts=parts, from_sibling=from_sibling, sems=sems, sums=sums_thru, lands=lands)
        return token

    sq, grad_x, g = local_step(x, mem, positions, loss_target, norm_in, weights, big_grads_ready, q_a_norm, kv_a_norm, conv_full,
                               gdn_a_log, gdn_dt_bias, gdn_norm, mem_norm, norm_final)

    small_names = ("norm_in", "q_a_norm", "kv_a_norm", "gdn_a_log", "gdn_dt_bias", "gdn_norm", "mem_norm", "norm_final")
    small = dict(norm_in=norm_in, q_a_norm=q_a_norm, kv_a_norm=kv_a_norm, gdn_a_log=gdn_a_log, gdn_dt_bias=gdn_dt_bias,
                 gdn_norm=gdn_norm, mem_norm=mem_norm, norm_final=norm_final)
    m_small = dict(norm_in=m_norm_in, q_a_norm=m_q_a_norm, kv_a_norm=m_kv_a_norm, gdn_a_log=m_gdn_a_log,
                   gdn_dt_bias=m_gdn_dt_bias, gdn_norm=m_gdn_norm, mem_norm=m_mem_norm, norm_final=m_norm_final)
    v_small = dict(norm_in=v_norm_in, q_a_norm=v_q_a_norm, kv_a_norm=v_kv_a_norm, gdn_a_log=v_gdn_a_log,
                   gdn_dt_bias=v_gdn_dt_bias, gdn_norm=v_gdn_norm, mem_norm=v_mem_norm, norm_final=v_norm_final)
    conv_rows = GDN_CONV * GDN_QKV // 1024
    g_block = pack_small([g.get(n) for n in small_names], g["gdn_a_log_dt_bias"], g["gdn_conv"], sq)
    g_block = sum_leading(allgather_devices(g_block, "allgather_small_grads"), "sum_small_grads")
    loss = 0.5 * jnp.sum(g_block[LOSS_ROW]) / D_MODEL
    g_conv = lax.dynamic_slice_in_dim(g_block[CONV_ROW0:CONV_ROW0 + conv_rows].reshape(GDN_CONV, GDN_QKV), chip * conv_cols,
                                      conv_cols, axis=1)
    as_row = lambda a: a.reshape(1, -1)
    updated = adamw_small(g_block, [as_row(small[n]) for n in small_names], [as_row(m_small[n]) for n in small_names],
                          [as_row(v_small[n]) for n in small_names])
    g_out, d_out, m_out, v_out = ({n: u[i].reshape(small[n].shape) for n, u in zip(small_names, updated)} for i in range(4))
    d_s = d_out["norm_in"]

    my_half = [None] * len(big_names)
    for group, e in exchanges.items():
        from_chips = exchange_chips_wait(e["sems"], e["sums"], e["lands"], d_s, "rs_exchange_wait_" + group)
        halves = add_fives(e["parts"], e["from_sibling"], from_chips, [splits[i] for i in e["idx"]], chip_core, "rs_add_chips_" + group)
        for i, a in zip(e["idx"], halves):
            my_half[i] = a
    other_half = swap_sibling(my_half, "rs_sibling_final")

    d_out["gdn_conv"], m_out["gdn_conv"], v_out["gdn_conv"] = adamw(gdn_conv, g_conv, m_gdn_conv, v_gdn_conv, "adamw_gdn_conv")
    g_out["gdn_conv"] = g_conv[None]
    res = adamw_untiled_rows(w_in3, my_half[0], other_half[0], m_in3, v_in3, "adamw_w_in")
    g_out["w_in"], d_out["w_in"], m_out["w_in"], v_out["w_in"] = [jnp.transpose(r, (1, 2, 0)) for r in res]
    res = adamw_w_q_b(w_qb_t, my_half[1], other_half[1], m_qb_t, v_qb_t, "adamw_w_q_b")
    g_out["w_q_b"], d_out["w_q_b"], m_out["w_q_b"], v_out["w_q_b"] = [jnp.transpose(r)[None] for r in res]
    rest = dict(w_kv_b=(w_kv_b, m_w_kv_b, v_w_kv_b), w_mem_kv=(w_mem_kv, m_w_mem_kv, v_w_mem_kv), w_out=(w_out, m_w_out, v_w_out))
    for i, n in enumerate(big_names):
        if n in rest:
            w_n, m_n, v_n = rest[n]
            g_out[n], d_out[n], m_out[n], v_out[n] = adamw_halves(w_n, my_half[i], other_half[i], m_n, v_n, splits[i], core, "adamw_" + n)

    order = ("norm_in", "w_in", "q_a_norm", "w_q_b", "kv_a_norm", "w_kv_b", "gdn_conv", "gdn_a_log", "gdn_dt_bias",
             "gdn_norm", "mem_norm", "w_mem_kv", "w_out", "norm_final")
    return (loss, grad_x, *[g_out[n] for n in order], *[d_out[n] for n in order], *[m_out[n] for n in order],
            *[v_out[n] for n in order])
```

```python
import jax
import jax.numpy as jnp
from jax import lax
from jax.experimental import pallas as pl
from jax.experimental.pallas import tpu as pltpu

F32 = jnp.float32
BF16 = jnp.bfloat16
BS = pl.BlockSpec

D_MODEL = 1024
N_HEADS = 4
MLA_NOPE, MLA_ROPE, MLA_V = 128, 64, 128
Q_LORA, KV_LORA = 384, 256
ROPE_THETA = 10000.0
GDN_DK = GDN_DV = 128
GDN_CONV = 4
CHUNK = 64
MEM_DH = 128
D_MIX = 1536
GDN_QKV = 1536
D_IN = 4296
EPS = 1e-6
ADAM_LR, ADAM_B1, ADAM_B2, ADAM_EPS, ADAM_WD, ADAM_STEP = 0.001, 0.9, 0.999, 1e-08, 0.01, 10

OFF_MLA = 0
OFF_MEMQ = 1024
OFF_GDN = 1536
OFF_GATE = 3072
N_PAD = 4608
HEAD_PAD = 256
MLA_SCALE = (MLA_NOPE + MLA_ROPE) ** -0.5
MEM_SCALE = MEM_DH ** -0.5
GDN_SCALE = GDN_DK ** -0.5
NEG = -1e30

NN = ((1,), (0,))
NT = ((1,), (1,))
TN = ((0,), (0,))


def _dot(a, b, dims):
    return lax.dot_general(a, b, (dims, ((), ())), preferred_element_type=F32)


def _bdot(spec, a, b, precision=None):
    return jnp.einsum(spec, a, b, preferred_element_type=F32, precision=precision)


def _arb(n):
    return pltpu.CompilerParams(dimension_semantics=("arbitrary",) * n)


def _sigmoid(x):
    return 1.0 / (1.0 + jnp.exp(-x))


def _softplus(z):
    return jnp.maximum(z, 0.0) + jnp.log(1.0 + jnp.exp(-jnp.abs(z)))


def _rope(t, cos_row, sin_row):
    return t * cos_row + pltpu.roll(t, 64, 1) * sin_row


def _rope_bwd(d, cos_row, sin_row):
    return d * cos_row + pltpu.roll(d * sin_row, 64, 1)


def rms_fwd(x, gain, name, tm=512, after=()):
    T, n = x.shape
    tm = min(tm, T)

    def body(x_ref, g_ref, *rest):
        xv = x_ref[...]
        r = lax.rsqrt(jnp.mean(xv * xv, axis=-1, keepdims=True) + EPS)
        rest[-1][...] = (xv * r * g_ref[...]).astype(BF16)

    return pl.pallas_call(
        body, name=name, grid=(T // tm,),
        in_specs=[BS((tm, n), lambda i: (i, 0)), BS((1, n), lambda i: (0, 0))] + [BS(memory_space=pl.ANY)] * len(after),
        out_specs=BS((tm, n), lambda i: (i, 0)),
        out_shape=jax.ShapeDtypeStruct((T, n), BF16), compiler_params=_arb(1))(x, gain, *after)


def mm(a, b, kind, out_dtype, name, bm=512, bn=None, n_outer=False, after=()):
    if kind == "nn":
        (M, K), (_, N) = a.shape, b.shape
    elif kind == "nt":
        (M, K), (N, _) = a.shape, b.shape
    else:
        (K, M), (_, N) = a.shape, b.shape
    bm, bn = min(bm, M), min(bn or N, N)
    assert M % bm == 0 and N % bn == 0, (name, M, N, K)
    ij = (lambda g0, g1: (g1, g0)) if n_outer else (lambda g0, g1: (g0, g1))
    a_spec = BS((K, bm), lambda g0, g1: (0, ij(g0, g1)[0])) if kind == "tn" else BS((bm, K), lambda g0, g1: (ij(g0, g1)[0], 0))
    once = dict(pipeline_mode=pl.Buffered(1)) if bn == N else {}
    b_spec = (BS((bn, K), lambda g0, g1: (ij(g0, g1)[1], 0), **once) if kind == "nt"
              else BS((K, bn), lambda g0, g1: (0, ij(g0, g1)[1]), **once))
    dims = {"nn": NN, "nt": NT, "tn": TN}[kind]

    def body(a_ref, b_ref, *rest):
        rest[-1][...] = _dot(a_ref[...].astype(BF16), b_ref[...].astype(BF16), dims).astype(out_dtype)

    grid = (N // bn, M // bm) if n_outer else (M // bm, N // bn)
    return pl.pallas_call(
        body, name=name, grid=grid, in_specs=[a_spec, b_spec] + [BS(memory_space=pl.ANY)] * len(after),
        out_specs=BS((bm, bn), lambda g0, g1: ij(g0, g1)),
        out_shape=jax.ShapeDtypeStruct((M, N), out_dtype), compiler_params=_arb(2))(a, b, *after)


def mm_cols_tn(pieces, b, out_dtype, name, bm=512):
    K, N = b.shape
    tiles = [p.shape[1] // bm for p in pieces]
    firsts = [sum(tiles[:i]) for i in range(len(tiles))]

    def body(*refs):
        b_ref, o_ref = refs[-2], refs[-1]
        i = pl.program_id(0)
        for a_ref, t0, n in zip(refs[:-2], firsts, tiles):
            @pl.when((i >= t0) & (i < t0 + n))
            def _(a_ref=a_ref):
                o_ref[...] = _dot(a_ref[...], b_ref[...], TN).astype(out_dtype)

    a_specs = [BS((K, bm), lambda i, t0=t0, n=n: (0, jnp.clip(i - t0, 0, n - 1))) for t0, n in zip(firsts, tiles)]
    return pl.pallas_call(
        body, name=name, grid=(sum(tiles),),
        in_specs=a_specs + [BS(b.shape, lambda i: (0, 0), pipeline_mode=pl.Buffered(1))],
        out_specs=BS((bm, N), lambda i: (i, 0)), out_shape=jax.ShapeDtypeStruct((sum(tiles) * bm, N), out_dtype),
        compiler_params=_arb(1))(*pieces, b)


def rope_tables(pos_col, inv_row, sgn_row, msk_row, tm=512, after=()):
    T = pos_col.shape[0]
    tm = min(tm, T)

    def body(p_ref, inv_ref, sgn_ref, msk_ref, *rest):
        c_ref, s_ref = rest[-2:]
        ang = p_ref[...].astype(F32) * inv_ref[...]
        c_ref[...] = jnp.cos(ang) * msk_ref[...]
        s_ref[...] = jnp.sin(ang) * sgn_ref[...]

    row = BS((1, 128), lambda i: (0, 0))
    return pl.pallas_call(
        body, name="rope_tables", grid=(T // tm,),
        in_specs=[BS((tm, 1), lambda i: (i, 0)), row, row, row] + [BS(memory_space=pl.ANY)] * len(after),
        out_specs=[BS((tm, 128), lambda i: (i, 0))] * 2,
        out_shape=[jax.ShapeDtypeStruct((T, 128), F32)] * 2, compiler_params=_arb(1))(pos_col, inv_row, sgn_row, msk_row, *after)


def mla_prep(P, gq, gkv, wq, wkv, cos_t, sin_t, tm=512):
    T = P.shape[0]
    tm = min(tm, T)

    def body(p_ref, gq_ref, gkv_ref, wq_ref, wkv_ref, c_ref, s_ref, q_ref, k_ref, v_ref, qn_ref, kvn_ref):
        p = p_ref[...].astype(F32)
        cq, ckv, kr = p[:, :Q_LORA], p[:, Q_LORA:Q_LORA + KV_LORA], p[:, 640:768]
        qn = (cq * lax.rsqrt(jnp.mean(cq * cq, axis=-1, keepdims=True) + EPS) * gq_ref[...]).astype(BF16)
        kvn = (ckv * lax.rsqrt(jnp.mean(ckv * ckv, axis=-1, keepdims=True) + EPS) * gkv_ref[...]).astype(BF16)
        qn_ref[...] = qn
        kvn_ref[...] = kvn
        q = _dot(qn, wq_ref[...], NT)
        kv = _dot(kvn, wkv_ref[...], NN)
        cos_row, sin_row = c_ref[...], s_ref[...]
        krr = _rope(kr, cos_row, sin_row).astype(BF16)
        for h in range(N_HEADS):
            lo = h * HEAD_PAD
            q_ref[:, lo:lo + 128] = (q[:, lo:lo + 128] * MLA_SCALE).astype(BF16)
            q_ref[:, lo + 128:lo + 256] = (_rope(q[:, lo + 128:lo + 256], cos_row, sin_row) * MLA_SCALE).astype(BF16)
            k_ref[:, lo:lo + 128] = kv[:, h * 128:(h + 1) * 128].astype(BF16)
            k_ref[:, lo + 128:lo + 256] = krr
            v_ref[:, lo:lo + 128] = kv[:, 512 + h * 128:512 + (h + 1) * 128].astype(BF16)
            v_ref[:, lo + 128:lo + 256] = jnp.ones((tm, 128), BF16)

    full = lambda r, c: BS((r, c), lambda i: (0, 0))
    rowb = lambda c: BS((tm, c), lambda i: (i, 0))
    return pl.pallas_call(
        body, name="mla_prep", grid=(T // tm,),
        in_specs=[rowb(1024), full(1, Q_LORA), full(1, KV_LORA), full(1024, Q_LORA), full(KV_LORA, 1024), rowb(128), rowb(128)],
        out_specs=[rowb(1024), rowb(1024), rowb(1024), rowb(Q_LORA), rowb(KV_LORA)],
        out_shape=[jax.ShapeDtypeStruct((T, 1024), BF16), jax.ShapeDtypeStruct((T, 1024), BF16),
                   jax.ShapeDtypeStruct((T, 1024), BF16), jax.ShapeDtypeStruct((T, Q_LORA), BF16),
                   jax.ShapeDtypeStruct((T, KV_LORA), BF16)],
        compiler_params=_arb(1))(P, gq, gkv, wq, wkv, cos_t, sin_t)


ATTN_HEADS_PER_STEP = 2
ATTN_STRIP = 32


def mla_attn_fwd(Q, K, V, B, S, tq=512, hp=ATTN_HEADS_PER_STEP):
    T = B * S
    tq = min(tq, S)
    nq = S // tq

    rs = min(ATTN_STRIP, tq)

    def body(q_ref, k_ref, v_ref, o_ref, lse_ref, m_s, acc_s, s_s, p_s, a_s):
        i = pl.program_id(2)
        m_s[...] = jnp.full_like(m_s, NEG)
        acc_s[...] = jnp.zeros_like(acc_s)

        def blk(j, masked):
            rows = pl.ds(pl.multiple_of(j * tq, tq), tq)
            for h in range(hp):
                hq = slice(h * HEAD_PAD, (h + 1) * HEAD_PAD)
                s_s[h] = _dot(q_ref[:, hq], k_ref[rows, hq], NT)
            for h in range(hp):
                for r0 in range(0, tq, rs):
                    rr = slice(r0, r0 + rs)
                    sv = s_s[h, rr, :]
                    if masked:
                        r = r0 + lax.broadcasted_iota(jnp.int32, (rs, tq), 0)
                        c = lax.broadcasted_iota(jnp.int32, (rs, tq), 1)
                        sv = jnp.where(r >= c, sv, NEG)
                    m_prev = m_s[h, rr, :]
                    m_new = jnp.maximum(m_prev, jnp.max(sv, axis=1, keepdims=True))
                    p_s[h, rr, :] = jnp.exp(sv - m_new).astype(BF16)
                    a_s[h, rr, :] = jnp.exp(m_prev - m_new)
                    m_s[h, rr, :] = m_new
            for h in range(hp):
                hq = slice(h * HEAD_PAD, (h + 1) * HEAD_PAD)
                acc_s[h] = a_s[h] * acc_s[h] + _dot(p_s[h], v_ref[rows, hq], NN)

        def loop(j, c):
            blk(j, False)
            return c

        lax.fori_loop(0, i, loop, 0)
        blk(i, True)
        for h in range(hp):
            den = acc_s[h, :, 128:256]
            o_ref[:, h * 128:(h + 1) * 128] = (acc_s[h, :, 0:128] / den).astype(BF16)
            lse_ref[h] = m_s[h] + jnp.log(den[:, 0:1])

    return pl.pallas_call(
        body, name="mla_attn_fwd", grid=(B, N_HEADS // hp, nq),
        in_specs=[BS((tq, hp * HEAD_PAD), lambda b, h, i: (b * nq + i, h)),
                  BS((S, hp * HEAD_PAD), lambda b, h, i: (b, h)),
                  BS((S, hp * HEAD_PAD), lambda b, h, i: (b, h))],
        out_specs=[BS((tq, hp * 128), lambda b, h, i: (b * nq + i, h)),
                   BS((hp, tq, 1), lambda b, h, i: (h, b * nq + i, 0))],
        out_shape=[jax.ShapeDtypeStruct((T, 512), BF16), jax.ShapeDtypeStruct((N_HEADS, T, 1), F32)],
        scratch_shapes=[pltpu.VMEM((hp, tq, 1), F32), pltpu.VMEM((hp, tq, HEAD_PAD), F32), pltpu.VMEM((hp, tq, tq), F32),
                        pltpu.VMEM((hp, tq, tq), BF16), pltpu.VMEM((hp, tq, 1), F32)],
        compiler_params=_arb(3))(Q, K, V)


def mla_attn_bwd(Q, K, V, O, dO, LSE, B, S, tq=512, hp=ATTN_HEADS_PER_STEP):
    T = B * S
    tq = min(tq, S)
    nq = S // tq

    rs = min(ATTN_STRIP, tq)

    def body(q_ref, k_ref, v_ref, o_ref, do_ref, lse_ref, dq_ref, dk_ref, dv_ref, delta_s, dq_s, dk_s, dv_s, s_s, dp_s, p_s, ds_s):
        j = pl.program_id(2)

        @pl.when(j == 0)
        def _():
            dq_s[...] = jnp.zeros_like(dq_s)
            for h in range(hp):
                sl = slice(h * 128, (h + 1) * 128)
                delta_s[h] = jnp.sum(do_ref[:, sl] * o_ref[:, sl].astype(F32), axis=1, keepdims=True)

        dk_s[...] = jnp.zeros_like(dk_s)
        dv_s[...] = jnp.zeros_like(dv_s)

        def step(i, c):
            rows = pl.ds(pl.multiple_of(i * tq, tq), tq)
            for h in range(hp):
                sq, sv = slice(h * HEAD_PAD, (h + 1) * HEAD_PAD), slice(h * 128, (h + 1) * 128)
                s_s[h] = _dot(q_ref[rows, sq], k_ref[:, sq], NT)
                dp_s[h] = _dot(do_ref[rows, sv].astype(BF16), v_ref[:, h * HEAD_PAD:h * HEAD_PAD + 128], NT)
            for h in range(hp):
                for r0 in range(0, tq, rs):
                    rr = slice(r0, r0 + rs)
                    seq_rows = pl.ds(pl.multiple_of(i * tq + r0, rs), rs)
                    r = i * tq + r0 + lax.broadcasted_iota(jnp.int32, (rs, tq), 0)
                    cc = j * tq + lax.broadcasted_iota(jnp.int32, (rs, tq), 1)
                    p = jnp.where(r >= cc, jnp.exp(s_s[h, rr, :] - lse_ref[h, seq_rows, :]), 0.0)
                    p_s[h, rr, :] = p.astype(BF16)
                    ds_s[h, rr, :] = (p * (dp_s[h, rr, :] - delta_s[h, seq_rows, :])).astype(BF16)
            for h in range(hp):
                sq, sv = slice(h * HEAD_PAD, (h + 1) * HEAD_PAD), slice(h * 128, (h + 1) * 128)
                dv_s[:, sv] += _dot(p_s[h], do_ref[rows, sv].astype(BF16), TN)
                dk_s[:, sq] += _dot(ds_s[h], q_ref[rows, sq], TN)
                dq_s[rows, sq] += _dot(ds_s[h], k_ref[:, sq], NN)
            return c

        lax.fori_loop(j, nq, step, 0)
        dk_ref[...] = dk_s[...].astype(BF16)
        dv_ref[...] = dv_s[...].astype(BF16)

        @pl.when(j == nq - 1)
        def _():
            dq_ref[...] = dq_s[...].astype(BF16)

    seq = lambda c: BS((S, c), lambda b, h, j: (b, h))
    blk = lambda c: BS((tq, c), lambda b, h, j: (b * nq + j, h))
    return pl.pallas_call(
        body, name="mla_attn_bwd", grid=(B, N_HEADS // hp, nq),
        in_specs=[seq(hp * HEAD_PAD), blk(hp * HEAD_PAD), blk(hp * HEAD_PAD), seq(hp * 128), seq(hp * 128),
                  BS((hp, S, 1), lambda b, h, j: (h, b, 0))],
        out_specs=[seq(hp * HEAD_PAD), blk(hp * HEAD_PAD), blk(hp * 128)],
        out_shape=[jax.ShapeDtypeStruct((T, 1024), BF16), jax.ShapeDtypeStruct((T, 1024), BF16),
                   jax.ShapeDtypeStruct((T, 512), BF16)],
        scratch_shapes=[pltpu.VMEM((hp, S, 1), F32), pltpu.VMEM((S, hp * HEAD_PAD), F32),
                        pltpu.VMEM((tq, hp * HEAD_PAD), F32), pltpu.VMEM((tq, hp * 128), F32),
                        pltpu.VMEM((hp, tq, tq), F32), pltpu.VMEM((hp, tq, tq), F32),
                        pltpu.VMEM((hp, tq, tq), BF16), pltpu.VMEM((hp, tq, tq), BF16)],
        compiler_params=_arb(3))(Q, K, V, O, dO, LSE)


def mla_proj_bwd(dQ, dK, dV, cos_t, sin_t, P, dab, wq, wkv, gq, gkv, tm=512):
    T = P.shape[0]
    tm = min(tm, T)

    def norm_bwd(x, dy, g):
        r = lax.rsqrt(jnp.mean(x * x, axis=-1, keepdims=True) + EPS)
        xh = x * r
        dxh = dy * g
        return r * (dxh - xh * jnp.mean(dxh * xh, axis=-1, keepdims=True)), jnp.sum(dy * xh, axis=0, keepdims=True)

    def body(dq_ref, dk_ref, dv_ref, c_ref, s_ref, p_ref, dab_ref, wq_ref, wkv_ref, gq_ref, gkv_ref,
             ql_ref, kvl_ref, o_ref, aq_ref, akv_ref):
        @pl.when(pl.program_id(0) == 0)
        def _():
            aq_ref[...] = jnp.zeros_like(aq_ref)
            akv_ref[...] = jnp.zeros_like(akv_ref)

        cos_row, sin_row = c_ref[...], s_ref[...]
        kr = jnp.zeros((tm, 128), F32)
        for h in range(N_HEADS):
            lo = h * HEAD_PAD
            ql_ref[:, lo:lo + 128] = (dq_ref[:, lo:lo + 128].astype(F32) * MLA_SCALE).astype(BF16)
            ql_ref[:, lo + 128:lo + 256] = (_rope_bwd(dq_ref[:, lo + 128:lo + 256].astype(F32), cos_row, sin_row) * MLA_SCALE).astype(BF16)
            kvl_ref[:, h * 128:(h + 1) * 128] = dk_ref[:, lo:lo + 128]
            kr = kr + dk_ref[:, lo + 128:lo + 256].astype(F32)
        kvl_ref[:, 512:] = dv_ref[...]
        dqn = _dot(ql_ref[...], wq_ref[...], NN)
        dkvn = _dot(kvl_ref[...], wkv_ref[...], NT)
        dcq, ggq = norm_bwd(p_ref[:, :Q_LORA].astype(F32), dqn, gq_ref[...])
        dckv, ggkv = norm_bwd(p_ref[:, Q_LORA:640].astype(F32), dkvn, gkv_ref[...])
        aq_ref[...] += ggq
        akv_ref[...] += ggkv
        o_ref[:, :Q_LORA] = dcq.astype(BF16)
        o_ref[:, Q_LORA:640] = dckv.astype(BF16)
        o_ref[:, 640:768] = _rope_bwd(kr, cos_row, sin_row).astype(BF16)
        o_ref[:, 768:896] = dab_ref[...]
        o_ref[:, 896:1024] = jnp.zeros((tm, 128), BF16)

    rowb = lambda c: BS((tm, c), lambda i: (i, 0))
    full = lambda r, c: BS((r, c), lambda i: (0, 0))
    return pl.pallas_call(
        body, name="mla_proj_bwd", grid=(T // tm,),
        in_specs=[rowb(1024), rowb(1024), rowb(512), rowb(128), rowb(128), rowb(1024), rowb(128),
                  full(1024, Q_LORA), full(KV_LORA, 1024), full(1, Q_LORA), full(1, KV_LORA)],
        out_specs=[rowb(1024), rowb(1024), rowb(1024), full(1, Q_LORA), full(1, KV_LORA)],
        out_shape=[jax.ShapeDtypeStruct((T, 1024), BF16)] * 3
        + [jax.ShapeDtypeStruct((1, Q_LORA), F32), jax.ShapeDtypeStruct((1, KV_LORA), F32)],
        compiler_params=_arb(1))(dQ, dK, dV, cos_t, sin_t, P, dab, wq, wkv, gq, gkv)


def _mem_probs(qh, kh):
    s = _dot(qh, kh, NT) * MEM_SCALE
    p = jnp.exp(s - jnp.max(s, axis=1, keepdims=True))
    return p / jnp.sum(p, axis=1, keepdims=True)


def mem_attn_fwd(P, MKV, B, S, M, tq=512):
    T = B * S
    tq = min(tq, S)
    nq = S // tq

    def body(q_ref, kv_ref, o_ref):
        for h in range(N_HEADS):
            sl = slice(h * 128, (h + 1) * 128)
            p = _mem_probs(q_ref[:, sl].astype(BF16), kv_ref[:, sl])
            o_ref[:, sl] = _dot(p.astype(BF16), kv_ref[:, 512 + h * 128:512 + (h + 1) * 128], NN).astype(BF16)

    return pl.pallas_call(
        body, name="mem_attn_fwd", grid=(B, nq),
        in_specs=[BS((tq, 512), lambda b, i: (b * nq + i, OFF_MEMQ // 512)), BS((M, 1024), lambda b, i: (b, 0))],
        out_specs=BS((tq, 512), lambda b, i: (b * nq + i, 0)),
        out_shape=jax.ShapeDtypeStruct((T, 512), BF16), compiler_params=_arb(2))(P, MKV)


def mem_attn_bwd(P, MKV, dO, B, S, M, tq=512):
    T = B * S
    tq = min(tq, S)
    nq = S // tq

    def body(q_ref, kv_ref, do_ref, dq_ref, dkv_ref):
        @pl.when(pl.program_id(1) == 0)
        def _():
            dkv_ref[...] = jnp.zeros_like(dkv_ref)

        for h in range(N_HEADS):
            sl = slice(h * 128, (h + 1) * 128)
            sv = slice(512 + h * 128, 512 + (h + 1) * 128)
            qh = q_ref[:, sl].astype(BF16)
            kh = kv_ref[:, sl]
            do = do_ref[:, sl].astype(BF16)
            p = _mem_probs(qh, kh)
            dkv_ref[:, sv] += _dot(p.astype(BF16), do, TN)
            dp = _dot(do, kv_ref[:, sv], NT)
            ds = (p * (dp - jnp.sum(dp * p, axis=1, keepdims=True)) * MEM_SCALE).astype(BF16)
            dq_ref[:, sl] = _dot(ds, kh, NN).astype(BF16)
            dkv_ref[:, sl] += _dot(ds, qh, TN)

    return pl.pallas_call(
        body, name="mem_attn_bwd", grid=(B, nq),
        in_specs=[BS((tq, 512), lambda b, i: (b * nq + i, OFF_MEMQ // 512)), BS((M, 1024), lambda b, i: (b, 0)),
                  BS((tq, 512), lambda b, i: (b * nq + i, 0))],
        out_specs=[BS((tq, 512), lambda b, i: (b * nq + i, 0)), BS((M, 1024), lambda b, i: (b, 0))],
        out_shape=[jax.ShapeDtypeStruct((T, 512), BF16), jax.ShapeDtypeStruct((B * M, 1024), F32)],
        compiler_params=_arb(2))(P, MKV, dO)


def gain_grad(x, dy, name, tm=256):
    T, n = x.shape
    tm = min(tm, T)

    def body(x_ref, dy_ref, o_ref):
        @pl.when(pl.program_id(0) == 0)
        def _():
            o_ref[...] = jnp.zeros_like(o_ref)

        xv = x_ref[...]
        xh = xv * lax.rsqrt(jnp.mean(xv * xv, axis=-1, keepdims=True) + EPS)
        o_ref[...] += jnp.sum(dy_ref[...] * xh, axis=0, keepdims=True)

    return pl.pallas_call(
        body, name=name, grid=(T // tm,),
        in_specs=[BS((tm, n), lambda i: (i, 0))] * 2, out_specs=BS((1, n), lambda i: (0, 0)),
        out_shape=jax.ShapeDtypeStruct((1, n), F32), compiler_params=_arb(1))(x, dy)


def _conv_silu(x, w, t):
    y = x * w[3:4, :]
    for s in range(1, GDN_CONV):
        y = y + jnp.where(t >= s, pltpu.roll(x, s, 0), 0.0) * w[3 - s:4 - s, :]
    return y, _sigmoid(y)


def gdn_prep_fwd(P, conv_w, B, S):
    T = B * S

    def body(x_ref, w_ref, o_ref):
        kind = pl.program_id(1)
        t = lax.broadcasted_iota(jnp.int32, (S, 1), 0)
        y, sg = _conv_silu(x_ref[...].astype(F32), w_ref[...], t)
        a = y * sg
        scale = jnp.where(kind == 0, GDN_SCALE, 1.0).astype(F32)
        for h in range(N_HEADS):
            sl = slice(h * 128, (h + 1) * 128)
            seg = a[:, sl]
            n = lax.rsqrt(jnp.sum(seg * seg, axis=-1, keepdims=True) + EPS)
            o_ref[:, sl] = jnp.where(kind < 2, seg * (n * scale), seg)

    return pl.pallas_call(
        body, name="gdn_prep_fwd", grid=(B, 3),
        in_specs=[BS((S, 512), lambda b, k: (b, OFF_GDN // 512 + k)), BS((GDN_CONV, 512), lambda b, k: (0, k))],
        out_specs=BS((S, 512), lambda b, k: (b, k)),
        out_shape=jax.ShapeDtypeStruct((T, GDN_QKV), F32), compiler_params=_arb(2))(P, conv_w)


def gdn_prep_bwd(P, dqkv, conv_w, B, S):
    T = B * S

    def body(x_ref, d_ref, w_ref, o_ref, gw_ref):
        kind = pl.program_id(0)

        @pl.when(pl.program_id(1) == 0)
        def _():
            gw_ref[...] = jnp.zeros_like(gw_ref)

        t = lax.broadcasted_iota(jnp.int32, (S, 1), 0)
        x = x_ref[...].astype(F32)
        w = w_ref[...]
        y, sg = _conv_silu(x, w, t)
        a = y * sg
        scale = jnp.where(kind == 0, GDN_SCALE, 1.0).astype(F32)
        das = []
        for h in range(N_HEADS):
            sl = slice(h * 128, (h + 1) * 128)
            seg, dseg = a[:, sl], d_ref[:, sl]
            n = lax.rsqrt(jnp.sum(seg * seg, axis=-1, keepdims=True) + EPS)
            dn = scale * (n * dseg - seg * (n * n * n) * jnp.sum(dseg * seg, axis=-1, keepdims=True))
            das.append(jnp.where(kind < 2, dn, dseg))
        dy = jnp.concatenate(das, axis=1) * (sg * (1.0 + y * (1.0 - sg)))
        dx = dy * w[3:4, :]
        gw_ref[3:4, :] += jnp.sum(dy * x, axis=0, keepdims=True)
        for s in range(1, GDN_CONV):
            dx = dx + jnp.where(t + s < S, pltpu.roll(dy, S - s, 0), 0.0) * w[3 - s:4 - s, :]
            gw_ref[3 - s:4 - s, :] += jnp.sum(dy * jnp.where(t >= s, pltpu.roll(x, s, 0), 0.0), axis=0, keepdims=True)
        o_ref[...] = dx.astype(BF16)

    return pl.pallas_call(
        body, name="gdn_prep_bwd", grid=(3, B),
        in_specs=[BS((S, 512), lambda k, b: (b, OFF_GDN // 512 + k)), BS((S, 512), lambda k, b: (b, k)),
                  BS((GDN_CONV, 512), lambda k, b: (0, k))],
        out_specs=[BS((S, 512), lambda k, b: (b, k)), BS((GDN_CONV, 512), lambda k, b: (0, k))],
        out_shape=[jax.ShapeDtypeStruct((T, GDN_QKV), BF16), jax.ShapeDtypeStruct((GDN_CONV, GDN_QKV), F32)],
        compiler_params=_arb(2))(P, dqkv, conv_w)


def _chunk_row(n_rows):
    return lax.broadcasted_iota(jnp.int32, (n_rows, 1), 0) % CHUNK


def gdn_gate_fwd(P, alog_row, dt_row, B, S):
    T = B * S

    def body(x_ref, al_ref, dt_ref, o_ref):
        x = x_ref[...].astype(F32)
        lane = lax.broadcasted_iota(jnp.int32, (1, 128), 1)
        g = jnp.where(lane < 4, -jnp.exp(al_ref[...]) * _softplus(x + dt_ref[...]), 0.0)
        t = _chunk_row(S)
        for s in (1, 2, 4, 8, 16, 32):
            g = g + jnp.where(t >= s, pltpu.roll(g, s, 0), 0.0)
        o_ref[...] = jnp.where(lane < 4, g, jnp.where(lane < 8, _sigmoid(x), 0.0))

    row = BS((1, 128), lambda b: (0, 0))
    return pl.pallas_call(
        body, name="gdn_gate_fwd", grid=(B,),
        in_specs=[BS((S, 128), lambda b: (b, 768 // 128)), row, row], out_specs=BS((S, 128), lambda b: (b, 0)),
        out_shape=jax.ShapeDtypeStruct((T, 128), F32), compiler_params=_arb(1))(P, alog_row, dt_row)


def gdn_gate_bwd(P, dGB, alog_row, dt_row, B, S):
    T = B * S

    def body(x_ref, d_ref, al_ref, dt_ref, o_ref, acc_ref):
        @pl.when(pl.program_id(0) == 0)
        def _():
            acc_ref[...] = jnp.zeros_like(acc_ref)

        x, d = x_ref[...].astype(F32), d_ref[...]
        lane = lax.broadcasted_iota(jnp.int32, (1, 128), 1)
        z = x + dt_ref[...]
        coef = -jnp.exp(al_ref[...])
        g = coef * _softplus(z)
        da = jnp.where(lane < 4, d * coef * _sigmoid(z), 0.0)
        beta = _sigmoid(x)
        o_ref[...] = jnp.where(lane < 4, da, jnp.where(lane < 8, d * beta * (1.0 - beta), 0.0)).astype(BF16)
        acc_ref[0:1, :] += jnp.sum(jnp.where(lane < 4, d * g, 0.0), axis=0, keepdims=True)
        acc_ref[1:2, :] += jnp.sum(da, axis=0, keepdims=True)

    row = BS((1, 128), lambda b: (0, 0))
    return pl.pallas_call(
        body, name="gdn_gate_bwd", grid=(B,),
        in_specs=[BS((S, 128), lambda b: (b, 768 // 128)), BS((S, 128), lambda b: (b, 0)), row, row],
        out_specs=[BS((S, 128), lambda b: (b, 0)), BS((8, 128), lambda b: (0, 0))],
        out_shape=[jax.ShapeDtypeStruct((T, 128), BF16), jax.ShapeDtypeStruct((8, 128), F32)],
        compiler_params=_arb(1))(P, dGB, alog_row, dt_row)


def _chunk_masks(nc):
    r = lax.broadcasted_iota(jnp.int32, (nc, CHUNK, CHUNK), 1)
    c = lax.broadcasted_iota(jnp.int32, (nc, CHUNK, CHUNK), 2)
    return r >= c, r > c


def _chunk_local(q, k, gc, gr, beta, incl, strict):
    decay = jnp.exp(jnp.where(incl, gc - gr, NEG))
    kb = k * beta
    kbf = k.astype(BF16)
    m_kk = _bdot("gcd,gjd->gcj", kb.astype(BF16), kbf)
    l_mat = jnp.where(strict, m_kk * decay, 0.0)
    a_mat = _bdot("gcd,gjd->gcj", q.astype(BF16), kbf) * decay
    return decay, kb, l_mat, a_mat


WY_SPLIT_LEVELS = 2


def _split_bf16(x):
    hi = x.astype(BF16)
    return hi, (x - hi.astype(F32)).astype(BF16)


def _mm_split(ah, al, bh, bl):
    spec = "gij,gjk->gik"
    return _bdot(spec, ah, bh) + (_bdot(spec, ah, bl) + _bdot(spec, al, bh))


def gdn_chunk_fwd(qkv, GB, Grow, B, S, nc=8):
    T = B * S
    N = S // CHUNK
    nc = min(nc, N)
    nb = N // nc
    R = nc * CHUNK

    def body(q_ref, k_ref, v_ref, gb_ref, gr_ref, u_ref, w_ref, t_ref, a_ref):
        incl, strict = _chunk_masks(nc)
        eye = (lax.broadcasted_iota(jnp.int32, (nc, CHUNK, CHUNK), 1)
               == lax.broadcasted_iota(jnp.int32, (nc, CHUNK, CHUNK), 2)).astype(F32)
        for h in range(N_HEADS):
            sl = slice(h * 128, (h + 1) * 128)
            q = q_ref[:, sl].reshape(nc, CHUNK, 128)
            k = k_ref[:, sl].reshape(nc, CHUNK, 128)
            v = v_ref[:, sl].reshape(nc, CHUNK, 128)
            gc = gb_ref[:, h:h + 1].reshape(nc, CHUNK, 1)
            beta = gb_ref[:, 4 + h:5 + h].reshape(nc, CHUNK, 1)
            gr = gr_ref[h][:, None, :]
            _, kb, l_mat, a_mat = _chunk_local(q, k, gc, gr, beta, incl, strict)
            pw = -l_mat
            tinv = eye + pw
            for level in range(5):
                if level < WY_SPLIT_LEVELS:
                    ph, pl_ = _split_bf16(pw)
                    pw = _mm_split(ph, pl_, ph, pl_)
                    ph, pl_ = _split_bf16(pw)
                    th, tl = _split_bf16(tinv)
                    tinv = tinv + _mm_split(th, tl, ph, pl_)
                else:
                    ph = pw.astype(BF16)
                    pw = _bdot("gij,gjk->gik", ph, ph)
                    tinv = tinv + _bdot("gij,gjk->gik", tinv.astype(BF16), pw.astype(BF16))
            tb = tinv.astype(BF16)
            u = _bdot("gcj,gjv->gcv", tb, (v * beta).astype(BF16))
            w = _bdot("gcj,gjk->gck", tb, (kb * jnp.exp(gc)).astype(BF16))
            u_ref[:, sl] = u.reshape(R, 128)
            w_ref[:, sl] = w.reshape(R, 128).astype(BF16)
            t_ref[h] = jnp.swapaxes(tinv, 1, 2).astype(BF16)
            a_ref[h] = a_mat.astype(BF16)

    rowb = lambda c, j: BS((R, c), lambda b, n: (b * nb + n, j))
    mat = BS((None, N_HEADS, nc, CHUNK, CHUNK), lambda b, n: (b, 0, n, 0, 0))
    return pl.pallas_call(
        body, name="gdn_chunk_fwd", grid=(B, nb),
        in_specs=[rowb(512, 0), rowb(512, 1), rowb(512, 2), rowb(128, 0),
                  BS((None, N_HEADS, nc, CHUNK), lambda b, n: (b, 0, n, 0))],
        out_specs=[rowb(512, 0), rowb(512, 0), mat, mat],
        out_shape=[jax.ShapeDtypeStruct((T, 512), F32), jax.ShapeDtypeStruct((T, 512), BF16),
                   jax.ShapeDtypeStruct((B, N_HEADS, N, CHUNK, CHUNK), BF16),
                   jax.ShapeDtypeStruct((B, N_HEADS, N, CHUNK, CHUNK), BF16)],
        compiler_params=_arb(2))(qkv, qkv, qkv, GB, Grow)


SCAN_CHUNKS = 4


def gdn_scan_fwd(qkv3, U3, W3, GB3, A, B, S):
    N = S // CHUNK
    cps = SCAN_CHUNKS if N % SCAN_CHUNKS == 0 else 1

    def body(q_ref, k_ref, u_ref, w_ref, gb_ref, a_ref, o_ref, vn_ref, st_ref, s_s):
        @pl.when(pl.program_id(0) == 0)
        def _():
            s_s[...] = jnp.zeros_like(s_s)

        for c in range(cps):
            rows = slice(c * CHUNK, (c + 1) * CHUNK)
            for b in range(B):
                for h in range(N_HEADS):
                    sl = slice(h * 128, (h + 1) * 128)
                    st = s_s[b, h]
                    st_ref[b, h, c] = st
                    stb = st.astype(BF16)
                    g = gb_ref[b, rows, h:h + 1]
                    gl = g[CHUNK - 1:CHUNK, :]
                    qg = (q_ref[b, rows, sl] * jnp.exp(g)).astype(BF16)
                    on_state = _dot(jnp.concatenate([w_ref[b, rows, sl].astype(BF16), qg], axis=0), stb, NN)
                    vn = u_ref[b, rows, sl] - on_state[:CHUNK]
                    vnb = vn.astype(BF16)
                    kd_t = jnp.transpose(k_ref[b, rows, sl] * jnp.exp(gl - g)).astype(BF16)
                    on_vn = _dot(jnp.concatenate([a_ref[b, h, c].astype(BF16), kd_t], axis=0), vnb, NN)
                    vn_ref[b, rows, sl] = vnb
                    o_ref[b, rows, sl] = (on_state[CHUNK:] + on_vn[:CHUNK]).astype(BF16)
                    s_s[b, h] = st * jnp.exp(gl) + on_vn[CHUNK:]

    tok = lambda c, j: BS((B, cps * CHUNK, c), lambda n: (0, n, j))
    return pl.pallas_call(
        body, name="gdn_scan_fwd", grid=(N // cps,),
        in_specs=[tok(512, 0), tok(512, 1), tok(512, 0), tok(512, 0), tok(128, 0),
                  BS((B, N_HEADS, cps, CHUNK, CHUNK), lambda n: (0, 0, n, 0, 0))],
        out_specs=[tok(512, 0), tok(512, 0), BS((B, N_HEADS, cps, 128, 128), lambda n: (0, 0, n, 0, 0))],
        out_shape=[jax.ShapeDtypeStruct((B, S, 512), BF16), jax.ShapeDtypeStruct((B, S, 512), BF16),
                   jax.ShapeDtypeStruct((B, N_HEADS, N, 128, 128), F32)],
        scratch_shapes=[pltpu.VMEM((B, N_HEADS, 128, 128), F32)],
        compiler_params=_arb(1))(qkv3, qkv3, U3, W3, GB3, A)


def gdn_scan_bwd(dO3, qkv3, W3, Vn3, GB3, A, St, B, S):
    N = S // CHUNK
    cps = SCAN_CHUNKS if N % SCAN_CHUNKS == 0 else 1

    def body(do_ref, q_ref, k_ref, w_ref, vn_ref, gb_ref, a_ref, st_ref,
             du_ref, dw_ref, dq_ref, dk_ref, da_ref, dg_ref, ds_s):
        @pl.when(pl.program_id(0) == 0)
        def _():
            ds_s[...] = jnp.zeros_like(ds_s)

        lane = lax.broadcasted_iota(jnp.int32, (1, 128), 1)
        last = lax.broadcasted_iota(jnp.int32, (CHUNK, 1), 0) == CHUNK - 1
        for c in reversed(range(cps)):
            rows = slice(c * CHUNK, (c + 1) * CHUNK)
            for b in range(B):
                dg_all = jnp.zeros((CHUNK, 128), F32)
                for h in range(N_HEADS):
                    sl = slice(h * 128, (h + 1) * 128)
                    st = st_ref[b, h, c]
                    stb = st.astype(BF16)
                    dsn = ds_s[b, h]
                    dsnb = dsn.astype(BF16)
                    g = gb_ref[b, rows, h:h + 1]
                    gl = g[CHUNK - 1:CHUNK, :]
                    egl = jnp.exp(gl)
                    ekd = jnp.exp(gl - g)
                    eg = jnp.exp(g)
                    q, k = q_ref[b, rows, sl], k_ref[b, rows, sl]
                    kd = k * ekd
                    qg = q * eg
                    do = do_ref[b, rows, sl].astype(BF16)
                    vnb = vn_ref[b, rows, sl].astype(BF16)
                    dvn = _dot(a_ref[b, h, c].astype(BF16), do, TN) + _dot(kd.astype(BF16), dsnb, NN)
                    dvnb = dvn.astype(BF16)
                    do_on = _dot(do, jnp.concatenate([stb, vnb], axis=0), NT)
                    dqg = do_on[:, :128]
                    da_ref[b, h, c] = do_on[:, 128:]
                    dkd = _dot(vnb, dsnb, NT)
                    ds_s[b, h] = (_dot(qg.astype(BF16), do, TN) + egl * dsn - _dot(w_ref[b, rows, sl].astype(BF16), dvnb, TN))
                    du_ref[b, rows, sl] = dvnb
                    dw_ref[b, rows, sl] = (-_dot(dvnb, stb, NT)).astype(BF16)
                    dq_ref[b, rows, sl] = dqg * eg
                    dk_ref[b, rows, sl] = dkd * ekd
                    ddel = jnp.sum(dkd * kd, axis=1, keepdims=True)
                    dgl = jnp.sum(ddel, axis=0, keepdims=True) + jnp.sum(jnp.sum(st * dsn, axis=1, keepdims=True), axis=0, keepdims=True) * egl
                    col = jnp.sum(dqg * qg, axis=1, keepdims=True) - ddel + jnp.where(last, dgl, 0.0)
                    dg_all = jnp.where(lane == h, col, dg_all)
                dg_ref[b, rows, :] = dg_all

    steps = N // cps
    tok = lambda c, j: BS((B, cps * CHUNK, c), lambda n: (0, steps - 1 - n, j))
    mat = lambda d: BS((B, N_HEADS, cps, d, d), lambda n: (0, 0, steps - 1 - n, 0, 0))
    return pl.pallas_call(
        body, name="gdn_scan_bwd", grid=(steps,),
        in_specs=[tok(512, 0), tok(512, 0), tok(512, 1), tok(512, 0), tok(512, 0), tok(128, 0), mat(CHUNK), mat(128)],
        out_specs=[tok(512, 0), tok(512, 0), tok(512, 0), tok(512, 0), mat(CHUNK), tok(128, 0)],
        out_shape=[jax.ShapeDtypeStruct((B, S, 512), BF16)] * 2 + [jax.ShapeDtypeStruct((B, S, 512), F32)] * 2
        + [jax.ShapeDtypeStruct((B, N_HEADS, N, CHUNK, CHUNK), F32), jax.ShapeDtypeStruct((B, S, 128), F32)],
        scratch_shapes=[pltpu.VMEM((B, N_HEADS, 128, 128), F32)],
        compiler_params=_arb(1))(dO3, qkv3, qkv3, W3, Vn3, GB3, A, St)


def gdn_chunk_bwd(qkv, GB, Grow, Tinv, dA, dU, dW, dQ1, dK1, dG1, B, S, nc=8):
    T = B * S
    N = S // CHUNK
    nc = min(nc, N)
    nb = N // nc
    R = nc * CHUNK

    def body(q_ref, k_ref, v_ref, gb_ref, gr_ref, t_ref, da_ref, du_ref, dw_ref, dq1_ref, dk1_ref, dg1_ref, o_ref, dgb_ref):
        incl, strict = _chunk_masks(nc)
        lane = lax.broadcasted_iota(jnp.int32, (1, 128), 1)
        dg_all = dg1_ref[...]
        db_all = jnp.zeros((R, 128), F32)
        for h in range(N_HEADS):
            sl = slice(h * 128, (h + 1) * 128)
            q = q_ref[:, sl].reshape(nc, CHUNK, 128)
            k = k_ref[:, sl].reshape(nc, CHUNK, 128)
            v = v_ref[:, sl].reshape(nc, CHUNK, 128)
            gc = gb_ref[:, h:h + 1].reshape(nc, CHUNK, 1)
            beta = gb_ref[:, 4 + h:5 + h].reshape(nc, CHUNK, 1)
            gr = gr_ref[h][:, None, :]
            decay, kb, l_mat, a_mat = _chunk_local(q, k, gc, gr, beta, incl, strict)
            eg = jnp.exp(gc)
            kbg = kb * eg
            vb = v * beta
            tt = t_ref[h].astype(BF16)
            du = du_ref[:, sl].reshape(nc, CHUNK, 128).astype(BF16)
            dw = dw_ref[:, sl].reshape(nc, CHUNK, 128).astype(BF16)
            dvb = _bdot("gjc,gcv->gjv", tt, du)
            dkbg = _bdot("gjc,gck->gjk", tt, dw)
            dt = _bdot("gcv,gjv->gcj", du, vb.astype(BF16)) + _bdot("gck,gjk->gcj", dw, kbg.astype(BF16))
            tmp = _bdot("gca,gab->gcb", tt, dt.astype(BF16))
            dl = jnp.where(strict, -_bdot("gcb,gbd->gcd", tmp.astype(BF16), tt), 0.0)
            da = da_ref[h]
            dm = (dl * decay).astype(BF16)
            dqk = (da * decay).astype(BF16)
            kbf = k.astype(BF16)
            dkb = _bdot("gcj,gjd->gcd", dm, kbf) + dkbg * eg
            dk = (_bdot("gcj,gcd->gjd", dm, kb.astype(BF16)) + _bdot("gcj,gcd->gjd", dqk, q.astype(BF16))
                  + dk1_ref[:, sl].reshape(nc, CHUNK, 128) + dkb * beta)
            dq = _bdot("gcj,gjd->gcd", dqk, kbf) + dq1_ref[:, sl].reshape(nc, CHUNK, 128)
            e = dl * l_mat + da * a_mat
            dgc = (jnp.sum(e, axis=2, keepdims=True) - jnp.sum(jnp.swapaxes(e, 1, 2), axis=2, keepdims=True)
                   + jnp.sum(dkbg * kbg, axis=2, keepdims=True))
            dbeta = jnp.sum(dkb * k, axis=2, keepdims=True) + jnp.sum(dvb * v, axis=2, keepdims=True)
            o_ref[:, sl] = dq.reshape(R, 128)
            o_ref[:, 512 + h * 128:512 + (h + 1) * 128] = dk.reshape(R, 128)
            o_ref[:, 1024 + h * 128:1024 + (h + 1) * 128] = (dvb * beta).reshape(R, 128)
            dg_all = dg_all + jnp.where(lane == h, dgc.reshape(R, 1), 0.0)
            db_all = jnp.where(lane == 4 + h, dbeta.reshape(R, 1), db_all)
        t = _chunk_row(R)
        for s in (1, 2, 4, 8, 16, 32):
            dg_all = dg_all + jnp.where(t + s < CHUNK, pltpu.roll(dg_all, R - s, 0), 0.0)
        dgb_ref[...] = jnp.where(lane < 4, dg_all, db_all)

    rowb = lambda c, j: BS((R, c), lambda b, n: (b * nb + n, j))
    mat = BS((None, N_HEADS, nc, CHUNK, CHUNK), lambda b, n: (b, 0, n, 0, 0))
    return pl.pallas_call(
        body, name="gdn_chunk_bwd", grid=(B, nb),
        in_specs=[rowb(512, 0), rowb(512, 1), rowb(512, 2), rowb(128, 0),
                  BS((None, N_HEADS, nc, CHUNK), lambda b, n: (b, 0, n, 0)), mat, mat,
                  rowb(512, 0), rowb(512, 0), rowb(512, 0), rowb(512, 0), rowb(128, 0)],
        out_specs=[rowb(GDN_QKV, 0), rowb(128, 0)],
        out_shape=[jax.ShapeDtypeStruct((T, GDN_QKV), F32), jax.ShapeDtypeStruct((T, 128), F32)],
        compiler_params=_arb(2))(qkv, qkv, qkv, GB, Grow, Tinv, dA, dU, dW, dQ1, dK1, dG1)


def _gdn_out_norm(og, gg):
    outs, xhs, rs = [], [], []
    for h in range(N_HEADS):
        seg = og[:, h * 128:(h + 1) * 128]
        r = lax.rsqrt(jnp.mean(seg * seg, axis=-1, keepdims=True) + EPS)
        xh = seg * r
        outs.append(xh * gg)
        xhs.append(xh)
        rs.append(r)
    return outs, xhs, rs


def merge_fwd(o_mla, o_gdn, o_mem, P, x, tgt, w_out, g_gdn, g_fin, tm=512):
    T = x.shape[0]
    tm = min(tm, T)

    def body(om_ref, og_ref, oc_ref, gate_ref, x_ref, t_ref, w_ref, gg_ref, gf_ref, mix_ref, dx_ref, dxb_ref, sq_ref, gnf_ref):
        @pl.when(pl.program_id(0) == 0)
        def _():
            sq_ref[...] = jnp.zeros_like(sq_ref)
            gnf_ref[...] = jnp.zeros_like(gnf_ref)

        ogn, _, _ = _gdn_out_norm(og_ref[...].astype(F32), gg_ref[...])
        cat = jnp.concatenate([om_ref[...].astype(F32)] + ogn + [oc_ref[...].astype(F32)], axis=1)
        gt = gate_ref[...].astype(F32)
        mixed = (cat * (gt * _sigmoid(gt))).astype(BF16)
        mix_ref[...] = mixed
        x2 = x_ref[...] + _dot(mixed, w_ref[...], NN)
        r2 = lax.rsqrt(jnp.mean(x2 * x2, axis=-1, keepdims=True) + EPS)
        xh = x2 * r2
        gf = gf_ref[...]
        diff = xh * gf - t_ref[...]
        sq_ref[...] += jnp.sum(diff * diff, axis=0, keepdims=True)
        dy = diff * (1.0 / D_MODEL)
        gnf_ref[...] += jnp.sum(dy * xh, axis=0, keepdims=True)
        dxh = dy * gf
        dx = r2 * (dxh - xh * jnp.mean(dxh * xh, axis=-1, keepdims=True))
        dx_ref[...] = dx
        dxb_ref[...] = dx.astype(BF16)

    rowb = lambda c, j=0: BS((tm, c), lambda i: (i, j))
    full = lambda r, c: BS((r, c), lambda i: (0, 0))
    return pl.pallas_call(
        body, name="merge_fwd", grid=(T // tm,),
        in_specs=[rowb(512), rowb(512), rowb(512), rowb(D_MIX, OFF_GATE // D_MIX), rowb(D_MODEL), rowb(D_MODEL),
                  full(D_MIX, D_MODEL), full(1, 128), full(1, D_MODEL)],
        out_specs=[rowb(D_MIX), rowb(D_MODEL), rowb(D_MODEL), full(1, D_MODEL), full(1, D_MODEL)],
        out_shape=[jax.ShapeDtypeStruct((T, D_MIX), BF16), jax.ShapeDtypeStruct((T, D_MODEL), F32),
                   jax.ShapeDtypeStruct((T, D_MODEL), BF16),
                   jax.ShapeDtypeStruct((1, D_MODEL), F32), jax.ShapeDtypeStruct((1, D_MODEL), F32)],
        compiler_params=_arb(1))(o_mla, o_gdn, o_mem, P, x, tgt, w_out, g_gdn, g_fin)


def merge_bwd(dx2, o_mla, o_gdn, o_mem, P, w_out, g_gdn, tm=512):
    T = dx2.shape[0]
    tm = min(tm, T)

    def body(dx_ref, om_ref, og_ref, oc_ref, gate_ref, w_ref, gg_ref, dgate_ref, dom_ref, dog_ref, doc_ref, ggn_ref):
        @pl.when(pl.program_id(0) == 0)
        def _():
            ggn_ref[...] = jnp.zeros_like(ggn_ref)

        gg = gg_ref[...]
        dmix = _dot(dx_ref[...].astype(BF16), w_ref[...], NT)
        ogn, xhs, rs = _gdn_out_norm(og_ref[...].astype(F32), gg)
        cat = jnp.concatenate([om_ref[...].astype(F32)] + ogn + [oc_ref[...].astype(F32)], axis=1)
        gt = gate_ref[...].astype(F32)
        sg = _sigmoid(gt)
        dgate_ref[...] = (dmix * cat * (sg * (1.0 + gt * (1.0 - sg)))).astype(BF16)
        dcat = dmix * (gt * sg)
        dom_ref[...] = dcat[:, :512].astype(BF16)
        doc_ref[...] = dcat[:, 1024:].astype(BF16)
        acc = jnp.zeros((1, 128), F32)
        for h in range(N_HEADS):
            dseg = dcat[:, 512 + h * 128:512 + (h + 1) * 128]
            acc = acc + jnp.sum(dseg * xhs[h], axis=0, keepdims=True)
            dxh = dseg * gg
            dog_ref[:, h * 128:(h + 1) * 128] = (rs[h] * (dxh - xhs[h] * jnp.mean(dxh * xhs[h], axis=-1, keepdims=True))).astype(BF16)
        ggn_ref[...] += acc

    rowb = lambda c, j=0: BS((tm, c), lambda i: (i, j))
    full = lambda r, c: BS((r, c), lambda i: (0, 0))
    return pl.pallas_call(
        body, name="merge_bwd", grid=(T // tm,),
        in_specs=[rowb(D_MODEL), rowb(512), rowb(512), rowb(512), rowb(D_MIX, OFF_GATE // D_MIX),
                  full(D_MIX, D_MODEL), full(1, 128)],
        out_specs=[rowb(D_MIX), rowb(512), rowb(512), rowb(512), full(1, 128)],
        out_shape=[jax.ShapeDtypeStruct((T, D_MIX), BF16)] + [jax.ShapeDtypeStruct((T, 512), BF16)] * 3
        + [jax.ShapeDtypeStruct((1, 128), F32)],
        compiler_params=_arb(1))(dx2, o_mla, o_gdn, o_mem, P, w_out, g_gdn)


def in_proj_bwd(dP, wp, x, dx2, gain, after, tm=512):
    T, n = x.shape
    tm = min(tm, T)
    k = len(dP)
    widths = [p.shape[1] for p in dP]
    offs = [sum(widths[:i]) for i in range(k)]

    def body(*refs):
        w_ref, x_ref, dx2_ref, g_ref = refs[k:k + 4]
        o_ref, acc_ref = refs[-2:]

        @pl.when(pl.program_id(0) == 0)
        def _():
            acc_ref[...] = jnp.zeros_like(acc_ref)

        dy = None
        for a_ref, off, w in zip(refs[:k], offs, widths):
            d = _dot(a_ref[...], w_ref[off:off + w, :], NN)
            dy = d if dy is None else dy + d
        xv = x_ref[...]
        r = lax.rsqrt(jnp.mean(xv * xv, axis=-1, keepdims=True) + EPS)
        xh = xv * r
        acc_ref[...] += jnp.sum(dy * xh, axis=0, keepdims=True)
        dxh = dy * g_ref[...]
        o_ref[...] = dx2_ref[...] + r * (dxh - xh * jnp.mean(dxh * xh, axis=-1, keepdims=True))

    rowb = BS((tm, n), lambda i: (i, 0))
    full = BS((1, n), lambda i: (0, 0))
    return pl.pallas_call(
        body, name="in_proj_bwd", grid=(T // tm,),
        in_specs=[BS((tm, w), lambda i: (i, 0)) for w in widths]
        + [BS(wp.shape, lambda i: (0, 0), pipeline_mode=pl.Buffered(1)), rowb, rowb, full, BS(memory_space=pl.ANY)],
        out_specs=[rowb, full], out_shape=[jax.ShapeDtypeStruct((T, n), F32), jax.ShapeDtypeStruct((1, n), F32)],
        compiler_params=_arb(1))(*dP, wp, x, dx2, gain, after)


W_IN_SHARD = D_IN // 4
_GDN0 = Q_LORA + KV_LORA + MLA_ROPE
_AB0 = _GDN0 + GDN_QKV
_MEMQ0 = _AB0 + 2 * N_HEADS
_GATE0 = _MEMQ0 + N_HEADS * MEM_DH


def _w_in_row_map():
    a, m, gt = _AB0 - 2 * W_IN_SHARD, _MEMQ0 - 2 * W_IN_SHARD, _GATE0 - 2 * W_IN_SHARD
    e0 = OFF_GDN + W_IN_SHARD - _GDN0
    e1 = e0 + W_IN_SHARD
    e2 = OFF_GATE + W_IN_SHARD - gt
    return [(0, 0, 0, 672), (0, 672, 704, 32), (2, a, 768, m - a), (2, m, OFF_MEMQ, gt - m), (0, _GDN0, OFF_GDN, W_IN_SHARD - _GDN0),
            (1, 0, e0, W_IN_SHARD), (2, 0, e1, a), (2, gt, OFF_GATE, W_IN_SHARD - gt), (3, 0, e2, W_IN_SHARD)]


_W_IN_ZERO_ROWS = [(672, 32), (736, 32), (776, 248)]
W_IN_LANES = 256


def pad_w_in_t(shards):
    per_half = shards.shape[3] // W_IN_LANES

    def body(s_ref, o_ref):
        for r0, n in _W_IN_ZERO_ROWS:
            o_ref[r0:r0 + n, :] = jnp.zeros((n, W_IN_LANES), o_ref.dtype)
        for q, src, dst, n in _w_in_row_map():
            o_ref[dst:dst + n, :] = s_ref[q, src:src + n, :]

    return pl.pallas_call(
        body, name="pad_w_in_t", grid=(D_MODEL // W_IN_LANES,),
        in_specs=[BS((N_CHIPS, None, W_IN_SHARD, W_IN_LANES), lambda j: (0, j // per_half, 0, j % per_half))],
        out_specs=BS((N_PAD, W_IN_LANES), lambda j: (0, j)),
        out_shape=jax.ShapeDtypeStruct((N_PAD, D_MODEL), shards.dtype), compiler_params=_arb(1))(shards)


def unpad_w_in_t(g):
    def body(g_ref, o_ref):
        for q, src, dst, n in _w_in_row_map():
            o_ref[q, src:src + n, :] = g_ref[dst:dst + n, :]

    return pl.pallas_call(
        body, name="unpad_w_in_t", grid=(D_MODEL // W_IN_LANES,),
        in_specs=[BS((N_PAD, W_IN_LANES), lambda j: (0, j))], out_specs=BS((N_CHIPS, W_IN_SHARD, W_IN_LANES), lambda j: (0, 0, j)),
        out_shape=jax.ShapeDtypeStruct((N_CHIPS, W_IN_SHARD, D_MODEL), g.dtype), compiler_params=_arb(1))(g)


def _perm_w_kv_b(s):
    return jnp.concatenate([s[h, :, :128] for h in range(N_HEADS)] + [s[h, :, 128:] for h in range(N_HEADS)], axis=1)


def _unperm_w_kv_b(g):
    return jnp.stack([jnp.concatenate([g[:, h * 128:(h + 1) * 128], g[:, 512 + h * 128:512 + (h + 1) * 128]], axis=1)
                      for h in range(N_HEADS)])


def _lane_row(v4):
    return jnp.pad(v4.reshape(1, -1).astype(F32), ((0, 0), (0, 128 - v4.size)))


N_CHIPS = 4
MESH = pl.DeviceIdType.MESH
ANY = BS(memory_space=pl.ANY)


def _place():
    return lax.axis_index("x"), lax.axis_index("y"), lax.axis_index("c")


def _other_chips(x, y):
    return [(1 - x, y), (x, 1 - y), (1 - x, 1 - y)]


def _half(split, which):
    axis, size = split
    ds = pl.ds(pl.multiple_of(which * size, 16 if axis == 0 else 128), size)
    return (ds, slice(None)) if axis == 0 else (slice(None), ds)


SEM = BS(memory_space=pltpu.SEMAPHORE)
HBM = BS(memory_space=pltpu.HBM)
_IN_HBM = lambda a: pltpu.with_memory_space_constraint(a, pltpu.HBM)
_SIDE_EFFECT = pltpu.SideEffectType.DATAFLOW_SIDE_EFFECTING


def _late_gather_copies(s_refs, l_refs, send_sems, recv_sems, local_sems, with_arrivals):
    x, y, c = _place()
    sends, recvs, locals_ = [], [], []
    for i, (s_ref, l_ref) in enumerate(zip(s_refs, l_refs)):
        locals_.append(pltpu.make_async_copy(s_ref, l_ref.at[2 * x + y], local_sems.at[i]))
        for j, (px, py) in enumerate(_other_chips(x, y)):
            k = 3 * i + j
            sends.append(pltpu.make_async_remote_copy(src_ref=s_ref, dst_ref=l_ref.at[2 * x + y], send_sem=send_sems.at[k],
                                                      recv_sem=recv_sems.at[k], device_id=(px, py, c), device_id_type=MESH))
            if with_arrivals:
                recvs.append(pltpu.make_async_remote_copy(src_ref=s_ref, dst_ref=l_ref.at[2 * px + py], send_sem=send_sems.at[k],
                                                          recv_sem=recv_sems.at[k], device_id=(px, py, c), device_id_type=MESH))
    return sends, recvs, locals_


def late_gather_start(shards, after, name):
    n = len(shards)

    def body(*refs):
        s_refs, l_refs = refs[:n], refs[n:2 * n]
        send_sems, recv_sems, local_sems = refs[2 * n + 1:2 * n + 4]
        token = refs[-1]
        sends, _, locals_ = _late_gather_copies(s_refs, l_refs, send_sems, recv_sems, local_sems, False)
        for cp in locals_ + sends:
            cp.start()
        token[...] = jnp.zeros_like(token)

    lands = [lax.empty((N_CHIPS,) + s.shape, s.dtype) for s in shards]
    hbm_like = lambda a: pltpu.HBM(a.shape, a.dtype)
    out = pl.pallas_call(
        body, name=name,
        out_shape=[pltpu.SemaphoreType.DMA((3 * n,)), pltpu.SemaphoreType.DMA((3 * n,)), pltpu.SemaphoreType.DMA((n,))]
        + [hbm_like(s) for s in shards] + [hbm_like(l) for l in lands] + [jax.ShapeDtypeStruct((8, 128), F32)],
        in_specs=[HBM] * (2 * n) + [BS(memory_space=pl.ANY)], out_specs=[SEM] * 3 + [HBM] * (2 * n) + [BS(memory_space=pltpu.VMEM)],
        input_output_aliases={i: 3 + i for i in range(2 * n)},
        compiler_params=pltpu.CompilerParams(has_side_effects=_SIDE_EFFECT))(
            *[_IN_HBM(s) for s in shards], *[_IN_HBM(l) for l in lands], after)
    return out[:3], out[3:3 + n], out[3 + n:3 + 2 * n], out[-1]


def late_gather_wait(sems, shards, lands, after, name):
    n = len(shards)

    def body(*refs):
        s_refs, l_refs = refs[:n], refs[n:2 * n]
        send_sems, recv_sems, local_sems = refs[2 * n:2 * n + 3]
        sends, recvs, locals_ = _late_gather_copies(s_refs, l_refs, send_sems, recv_sems, local_sems, True)
        for cp in locals_:
            cp.wait()
        for cp in sends:
            cp.wait_send()
        for cp in recvs:
            cp.wait_recv()

    hbm_like = lambda a: pltpu.HBM(a.shape, a.dtype)
    out = pl.pallas_call(
        body, name=name, out_shape=[hbm_like(s) for s in shards] + [hbm_like(l) for l in lands],
        in_specs=[HBM] * (2 * n) + [SEM] * 3 + [BS(memory_space=pl.ANY)], out_specs=[HBM] * (2 * n),
        input_output_aliases={i: i for i in range(2 * n)},
        compiler_params=pltpu.CompilerParams(has_side_effects=_SIDE_EFFECT))(*shards, *lands, *sems, after)
    return out[n:]


def _half_gather_copies(s_ref, l_ref, send_sems, recv_sems, with_arrivals):
    x, y, c = _place()
    sends, recvs = [], []
    for j, (px, py) in enumerate(_other_chips(x, y)):
        sends.append(pltpu.make_async_remote_copy(src_ref=s_ref.at[c], dst_ref=l_ref.at[2 * x + y, c], send_sem=send_sems.at[j],
                                                  recv_sem=recv_sems.at[j], device_id=(px, py, c), device_id_type=MESH))
        if with_arrivals:
            recvs.append(pltpu.make_async_remote_copy(src_ref=s_ref.at[c], dst_ref=l_ref.at[2 * px + py, c], send_sem=send_sems.at[j],
                                                      recv_sem=recv_sems.at[j], device_id=(px, py, c), device_id_type=MESH))
    return sends, recvs


def half_gather_start(shard, name):
    def body(s_ref, l_ref, send_sems, recv_sems, local_sem, s_thru, l_thru, token):
        x, y, _ = _place()
        pltpu.make_async_copy(s_ref, l_ref.at[2 * x + y], local_sem.at[0]).start()
        for cp in _half_gather_copies(s_ref, l_ref, send_sems, recv_sems, False)[0]:
            cp.start()
        token[...] = jnp.zeros_like(token)

    land = lax.empty((N_CHIPS,) + shard.shape, shard.dtype)
    out = pl.pallas_call(
        body, name=name,
        out_shape=[pltpu.SemaphoreType.DMA((3,)), pltpu.SemaphoreType.DMA((3,)), pltpu.SemaphoreType.DMA((1,)),
                   pltpu.HBM(shard.shape, shard.dtype), pltpu.HBM(land.shape, land.dtype), jax.ShapeDtypeStruct((8, 128), F32)],
        in_specs=[HBM, HBM], out_specs=[SEM] * 3 + [HBM, HBM, BS(memory_space=pltpu.VMEM)],
        input_output_aliases={0: 3, 1: 4},
        compiler_params=pltpu.CompilerParams(has_side_effects=_SIDE_EFFECT))(_IN_HBM(shard), _IN_HBM(land))
    return out[:3], out[3], out[4], out[5]


def half_gather_wait(sems, shard, land, after, name):
    def body(s_ref, l_ref, send_sems, recv_sems, local_sem, *rest):
        x, y, _ = _place()
        pltpu.make_async_copy(s_ref, l_ref.at[2 * x + y], local_sem.at[0]).wait()
        sends, recvs = _half_gather_copies(s_ref, l_ref, send_sems, recv_sems, True)
        for cp in sends:
            cp.wait_send()
        for cp in recvs:
            cp.wait_recv()

    out = pl.pallas_call(
        body, name=name, out_shape=[pltpu.HBM(shard.shape, shard.dtype), pltpu.HBM(land.shape, land.dtype)],
        in_specs=[HBM, HBM] + [SEM] * 3 + [BS(memory_space=pl.ANY)] * len(after), out_specs=[HBM, HBM],
        input_output_aliases={0: 0, 1: 1},
        compiler_params=pltpu.CompilerParams(has_side_effects=_SIDE_EFFECT))(shard, land, *sems, *after)
    return out[1]


def pass_halves_to_sibling(land, name):
    def body(l_in, l_ref, send_sems, recv_sems):
        x, y, c = _place()
        copies = []
        for j, (px, py) in enumerate(_other_chips(x, y)):
            q = 2 * px + py
            give = pltpu.make_async_remote_copy(src_ref=l_ref.at[q, c], dst_ref=l_ref.at[q, c], send_sem=send_sems.at[j],
                                                recv_sem=recv_sems.at[j], device_id=(x, y, 1 - c), device_id_type=MESH)
            take = pltpu.make_async_remote_copy(src_ref=l_ref.at[q, c], dst_ref=l_ref.at[q, 1 - c], send_sem=send_sems.at[j],
                                                recv_sem=recv_sems.at[j], device_id=(x, y, 1 - c), device_id_type=MESH)
            give.start()
            copies.append((give, take))
        for give, take in copies:
            take.wait_recv()
            give.wait_send()

    return pl.pallas_call(
        body, name=name, in_specs=[ANY], out_specs=ANY, out_shape=jax.ShapeDtypeStruct(land.shape, land.dtype),
        input_output_aliases={0: 0},
        scratch_shapes=[pltpu.SemaphoreType.DMA((3,)), pltpu.SemaphoreType.DMA((3,))])(land)


def allgather_devices(block, name):
    R, C = block.shape

    def body(b_ref, o_ref, send_sems, recv_sems, local_sem):
        x, y, c = _place()
        me = 4 * x + 2 * y + c
        own = pltpu.make_async_copy(b_ref, o_ref.at[me], local_sem)
        own.start()
        copies = []
        for r in range(1, 8):
            px = 1 - x if r & 4 else x
            py = 1 - y if r & 2 else y
            pc = 1 - c if r & 1 else c
            send = pltpu.make_async_remote_copy(src_ref=b_ref, dst_ref=o_ref.at[me], send_sem=send_sems.at[r - 1],
                                                recv_sem=recv_sems.at[r - 1], device_id=(px, py, pc), device_id_type=MESH)
            recv = pltpu.make_async_remote_copy(src_ref=b_ref, dst_ref=o_ref.at[4 * px + 2 * py + pc], send_sem=send_sems.at[r - 1],
                                                recv_sem=recv_sems.at[r - 1], device_id=(px, py, pc), device_id_type=MESH)
            send.start()
            copies.append((send, recv))
        for send, recv in copies:
            recv.wait_recv()
            send.wait_send()
        own.wait()

    return pl.pallas_call(
        body, name=name, in_specs=[ANY], out_specs=ANY, out_shape=jax.ShapeDtypeStruct((8, R, C), block.dtype),
        scratch_shapes=[pltpu.SemaphoreType.DMA((7,)), pltpu.SemaphoreType.DMA((7,)), pltpu.SemaphoreType.DMA(())])(block)


def swap_sibling(arrs, name, splits=None):
    n = len(arrs)

    def sent(a_ref, i, c):
        return a_ref if splits is None else a_ref.at[(slice(None),) + _half(splits[i], 1 - c)]

    def out_shape(a, i):
        if splits is None:
            return a.shape
        axis, size = splits[i]
        return (a.shape[0], size, a.shape[2]) if axis == 0 else (a.shape[0], a.shape[1], size)

    def body(*refs):
        a_refs, o_refs = refs[:n], refs[n:2 * n]
        send_sems, recv_sems = refs[2 * n:]
        x, y, c = _place()
        copies = [pltpu.make_async_remote_copy(src_ref=sent(a_ref, i, c), dst_ref=o_ref, send_sem=send_sems.at[i],
                                               recv_sem=recv_sems.at[i], device_id=(x, y, 1 - c), device_id_type=MESH)
                  for i, (a_ref, o_ref) in enumerate(zip(a_refs, o_refs))]
        for cp in copies:
            cp.start()
        for cp in copies:
            cp.wait()

    return pl.pallas_call(
        body, name=name, in_specs=[ANY] * n, out_specs=[ANY] * n,
        out_shape=[jax.ShapeDtypeStruct(out_shape(a, i), a.dtype) for i, a in enumerate(arrs)],
        scratch_shapes=[pltpu.SemaphoreType.DMA((n,)), pltpu.SemaphoreType.DMA((n,))])(*arrs)


def _exchange_copies(p_refs, l_refs, send_sems, recv_sems):
    x, y, c = _place()
    return [pltpu.make_async_remote_copy(src_ref=p_ref.at[2 * px + py], dst_ref=l_ref.at[j], send_sem=send_sems.at[3 * i + j],
                                         recv_sem=recv_sems.at[3 * i + j], device_id=(px, py, c), device_id_type=MESH)
            for i, (p_ref, l_ref) in enumerate(zip(p_refs, l_refs)) for j, (px, py) in enumerate(_other_chips(x, y))]


def exchange_chips_start(parts, name):
    n = len(parts)

    def body(*refs):
        send_sems, recv_sems = refs[2 * n:2 * n + 2]
        for cp in _exchange_copies(refs[:n], refs[n:2 * n], send_sems, recv_sems):
            cp.start()
        refs[-1][...] = jnp.zeros_like(refs[-1])

    lands = [lax.empty((3,) + p.shape[1:], p.dtype) for p in parts]
    hbm_like = lambda a: pltpu.HBM(a.shape, a.dtype)
    out = pl.pallas_call(
        body, name=name,
        out_shape=[pltpu.SemaphoreType.DMA((3 * n,)), pltpu.SemaphoreType.DMA((3 * n,))]
        + [hbm_like(p) for p in parts] + [hbm_like(l) for l in lands] + [jax.ShapeDtypeStruct((8, 128), F32)],
        in_specs=[HBM] * (2 * n), out_specs=[SEM] * 2 + [HBM] * (2 * n) + [BS(memory_space=pltpu.VMEM)],
        input_output_aliases={i: 2 + i for i in range(2 * n)},
        compiler_params=pltpu.CompilerParams(has_side_effects=_SIDE_EFFECT))(*[_IN_HBM(p) for p in parts], *[_IN_HBM(l) for l in lands])
    return out[:2], out[2:2 + n], out[2 + n:2 + 2 * n], out[-1]


def exchange_chips_wait(sems, parts, lands, after, name):
    n = len(parts)

    def body(*refs):
        send_sems, recv_sems = refs[2 * n:2 * n + 2]
        for cp in _exchange_copies(refs[:n], refs[n:2 * n], send_sems, recv_sems):
            cp.wait_send()
            cp.wait_recv()

    hbm_like = lambda a: pltpu.HBM(a.shape, a.dtype)
    out = pl.pallas_call(
        body, name=name, out_shape=[hbm_like(p) for p in parts] + [hbm_like(l) for l in lands],
        in_specs=[HBM] * (2 * n) + [SEM] * 2 + [BS(memory_space=pl.ANY)], out_specs=[HBM] * (2 * n),
        input_output_aliases={i: i for i in range(2 * n)},
        compiler_params=pltpu.CompilerParams(has_side_effects=_SIDE_EFFECT))(*parts, *lands, *sems, after)
    return out[n:]


def _half_block(shape2, split):
    axis, size = split
    return (size, shape2[1]) if axis == 0 else (shape2[0], size)


def add_pairs(parts, halves, splits, core, name):
    n = len(parts)

    def body(s_ref, *refs):
        for a_ref, b_ref, o_ref in zip(refs[:n], refs[n:2 * n], refs[2 * n:]):
            o_ref[...] = (a_ref[...].astype(F32) + b_ref[...].astype(F32)).astype(BF16)

    def mine(i):
        blk = (None,) + _half_block(parts[i].shape[1:], splits[i])
        if splits[i][0] == 0:
            return BS(blk, lambda q, s: (q, s[0], 0))
        return BS(blk, lambda q, s: (q, 0, s[0]))

    half_specs = [BS((None,) + h.shape[1:], lambda q, s: (q, 0, 0)) for h in halves]
    return pl.pallas_call(
        body, name=name,
        grid_spec=pltpu.PrefetchScalarGridSpec(num_scalar_prefetch=1, grid=(N_CHIPS,),
                                               in_specs=[mine(i) for i in range(n)] + half_specs, out_specs=half_specs),
        out_shape=[jax.ShapeDtypeStruct(h.shape, BF16) for h in halves], compiler_params=_arb(1))(core, *parts, *halves)


def add_fives(parts, halves, from_chips, splits, chip_core, name):
    n = len(parts)

    def body(s_ref, *refs):
        for a_ref, b_ref, p_ref, o_ref in zip(refs[:n], refs[n:2 * n], refs[2 * n:3 * n], refs[3 * n:]):
            s = a_ref[...].astype(F32) + b_ref[...].astype(F32)
            for j in range(3):
                s = s + p_ref[j].astype(F32)
            o_ref[...] = s

    def mine(i):
        blk = (None,) + _half_block(parts[i].shape[1:], splits[i])
        if splits[i][0] == 0:
            return BS(blk, lambda g, s: (s[0], s[1], 0))
        return BS(blk, lambda g, s: (s[0], 0, s[1]))

    half_specs = [BS((None,) + h.shape[1:], lambda g, s: (s[0], 0, 0)) for h in halves]
    chip_specs = [BS(p.shape, lambda g, s: (0, 0, 0)) for p in from_chips]
    out_specs = [BS(h.shape[1:], lambda g, s: (0, 0)) for h in halves]
    return pl.pallas_call(
        body, name=name,
        grid_spec=pltpu.PrefetchScalarGridSpec(num_scalar_prefetch=1, grid=(1,),
                                               in_specs=[mine(i) for i in range(n)] + half_specs + chip_specs, out_specs=out_specs),
        out_shape=[jax.ShapeDtypeStruct(h.shape[1:], F32) for h in halves], compiler_params=_arb(1))(chip_core, *parts, *halves, *from_chips)


def sum_leading(a, name):
    def body(a_ref, o_ref):
        s = a_ref[0]
        for j in range(1, a.shape[0]):
            s = s + a_ref[j]
        o_ref[...] = s

    return pl.pallas_call(body, name=name, out_shape=jax.ShapeDtypeStruct(a.shape[1:], a.dtype))(a)


def _adamw_math(w, g, m, v):
    mn = ADAM_B1 * m + (1.0 - ADAM_B1) * g
    vn = ADAM_B2 * v + (1.0 - ADAM_B2) * (g * g)
    m_hat = mn / (1.0 - ADAM_B1 ** ADAM_STEP)
    v_hat = vn / (1.0 - ADAM_B2 ** ADAM_STEP)
    return -ADAM_LR * (m_hat / (jnp.sqrt(v_hat) + ADAM_EPS) + ADAM_WD * w), mn, vn


def adamw(w, g, m, v, name):
    R, C = g.shape
    lead = (None,) * (w.ndim - 2)

    def body(w_ref, g_ref, m_ref, v_ref, d_ref, mo_ref, vo_ref):
        d_ref[...], mo_ref[...], vo_ref[...] = _adamw_math(w_ref[...], g_ref[...], m_ref[...], v_ref[...])

    wblk = BS(lead + (R, C), lambda i: (0,) * w.ndim)
    gblk = BS((R, C), lambda i: (0, 0))
    return pl.pallas_call(
        body, name=name, grid=(1,), in_specs=[wblk, gblk, wblk, wblk], out_specs=[wblk] * 3,
        out_shape=[jax.ShapeDtypeStruct(w.shape, F32)] * 3, compiler_params=_arb(1))(w, g, m, v)


SMALL_ROWS = 16
CONV_ROW0 = 8
LOSS_ROW = 14


def pack_small(small_grads, g_ab, g_conv, sq):
    present = [a for a in small_grads if a is not None]

    def body(*refs):
        ab_ref, conv_ref, sq_ref, o_ref = refs[len(present):]
        o_ref[...] = jnp.zeros_like(o_ref)
        it = iter(refs[:len(present)])
        for i, a in enumerate(small_grads):
            if a is not None:
                o_ref[i:i + 1, 0:a.shape[1]] = next(it)[...]
        o_ref[3:4, 0:128] = ab_ref[0:1, :]
        o_ref[4:5, 0:128] = ab_ref[1:2, :]
        half = 512
        for k in range(GDN_CONV * GDN_QKV // half):
            src_r, src_c = (k * half) // GDN_QKV, (k * half) % GDN_QKV
            dst_r, dst_c = CONV_ROW0 + (k * half) // 1024, (k * half) % 1024
            o_ref[dst_r:dst_r + 1, dst_c:dst_c + half] = conv_ref[src_r:src_r + 1, src_c:src_c + half]
        o_ref[LOSS_ROW:LOSS_ROW + 1, :] = sq_ref[...]

    return pl.pallas_call(body, name="pack_small", out_shape=jax.ShapeDtypeStruct((SMALL_ROWS, 1024), F32))(
        *present, g_ab, g_conv, sq)


def adamw_small(block, ws, ms, vs):
    k = len(ws)

    def body(b_ref, *refs):
        outs = refs[3 * k:]
        for i in range(k):
            n = ws[i].shape[1]
            g = b_ref[i:i + 1, 0:n]
            d, mn, vn = _adamw_math(refs[i][...], g, refs[k + i][...], refs[2 * k + i][...])
            outs[4 * i][...], outs[4 * i + 1][...], outs[4 * i + 2][...], outs[4 * i + 3][...] = g, d, mn, vn

    out = pl.pallas_call(
        body, name="adamw_small",
        out_shape=[jax.ShapeDtypeStruct(w.shape, F32) for w in ws for _ in range(4)])(block, *ws, *ms, *vs)
    return [out[4 * i:4 * i + 4] for i in range(k)]


def adamw_halves(w, mine, other, m, v, split, core, name):
    R, C = w.shape[-2:]
    axis, size = split
    lead = (None,) * (w.ndim - 2)
    zeros = (0,) * (w.ndim - 2)
    if axis == 0:
        tr = size if size <= 256 else next(t for t in range(256, 7, -1) if size % t == 0 and t % 8 == 0)
        nb = size // tr
        whole = BS(lead + (tr, C), lambda hi, j, s: zeros + (hi * nb + j, 0))
        part = BS((tr, C), lambda hi, j, s: (j, 0))
    else:
        nb = size // 128
        whole = BS(lead + (R, 128), lambda hi, j, s: zeros + (0, hi * nb + j))
        part = BS((R, 128), lambda hi, j, s: (0, j))

    def body(s_ref, w_ref, a_ref, b_ref, m_ref, v_ref, g_ref, d_ref, mo_ref, vo_ref):
        g = jnp.where(pl.program_id(0) == s_ref[0], a_ref[...], b_ref[...])
        g_ref[...] = g
        d_ref[...], mo_ref[...], vo_ref[...] = _adamw_math(w_ref[...], g, m_ref[...], v_ref[...])

    return pl.pallas_call(
        body, name=name,
        grid_spec=pltpu.PrefetchScalarGridSpec(num_scalar_prefetch=1, grid=(2, nb),
                                               in_specs=[whole, part, part, whole, whole], out_specs=[whole] * 4),
        out_shape=[jax.ShapeDtypeStruct(w.shape, F32)] * 4, compiler_params=_arb(2))(core, w, mine, other, m, v)


def dense_bf16(w3, name):
    R, _, K = w3.shape
    kh = K // 2

    def body(w_hbm, o_ref, buf, sem):
        cp = pltpu.make_async_copy(w_hbm.at[:, 0], buf, sem)
        cp.start()
        cp.wait()
        o_ref[0] = buf[:, :kh].astype(BF16)
        o_ref[1] = buf[:, kh:].astype(BF16)

    return pl.pallas_call(
        body, name=name, in_specs=[ANY], out_specs=BS(memory_space=pltpu.VMEM), out_shape=jax.ShapeDtypeStruct((2, R, kh), BF16),
        scratch_shapes=[pltpu.VMEM((R, K), F32), pltpu.SemaphoreType.DMA(())])(w3)


ROW_BLOCK = 184


def adamw_untiled_rows(w3, mine, other, m3, v3, name):
    R, _, K = w3.shape
    kh = K // 2
    starts = list(range(0, R, ROW_BLOCK))
    sizes = [min(ROW_BLOCK, R - s) for s in starts]
    nblk = len(starts)

    def body(w_hbm, a_ref, b_ref, m_hbm, v_hbm, g_hbm, d_hbm, mo_hbm, vo_hbm,
             wbuf, mbuf, vbuf, gbuf, dbuf, mobuf, vobuf, in_sems, out_sems):
        first = lax.axis_index("c") == 0
        ins = []
        for k, (r0, n) in enumerate(zip(starts, sizes)):
            rows = pl.ds(r0, n)
            cps = [pltpu.make_async_copy(src.at[rows, 0], dst.at[rows], in_sems.at[3 * k + i])
                   for i, (src, dst) in enumerate(((w_hbm, wbuf), (m_hbm, mbuf), (v_hbm, vbuf)))]
            for cp in cps:
                cp.start()
            ins.append(cps)

        def update(rows):
            a, b = a_ref[rows, :], b_ref[rows, :]
            g = jnp.concatenate([jnp.where(first, a, b), jnp.where(first, b, a)], axis=1)
            gbuf[rows, :] = g
            dbuf[rows, :], mobuf[rows, :], vobuf[rows, :] = _adamw_math(wbuf[rows, :], g, mbuf[rows, :], vbuf[rows, :])

        outs = []
        for k, (r0, n) in enumerate(zip(starts, sizes)):
            for cp in ins[k]:
                cp.wait()
            groups, tail = n // 8, n % 8

            def group(i, carry, r0=r0):
                update(pl.ds(pl.multiple_of(r0 + i * 8, 8), 8))
                return carry

            lax.fori_loop(0, groups, group, 0)
            if tail:
                update(pl.ds(r0 + groups * 8, tail))
            rows = pl.ds(r0, n)
            cps = [pltpu.make_async_copy(src.at[rows], dst.at[rows, 0], out_sems.at[4 * k + i])
                   for i, (src, dst) in enumerate(((gbuf, g_hbm), (dbuf, d_hbm), (mobuf, mo_hbm), (vobuf, vo_hbm)))]
            for cp in cps:
                cp.start()
            outs += cps
        for cp in outs:
            cp.wait()

    vmem = BS(memory_space=pltpu.VMEM)
    return pl.pallas_call(
        body, name=name, in_specs=[ANY, vmem, vmem, ANY, ANY], out_specs=[ANY] * 4,
        out_shape=[jax.ShapeDtypeStruct(w3.shape, F32)] * 4,
        scratch_shapes=[pltpu.VMEM((R, K), F32)] * 7 + [pltpu.SemaphoreType.DMA((3 * nblk,)), pltpu.SemaphoreType.DMA((4 * nblk,))])(
            w3, mine, other, m3, v3)


def adamw_w_q_b(w, mine, other, m, v, name):
    def body(w_ref, a_ref, b_ref, m_ref, v_ref, g_ref, d_ref, mo_ref, vo_ref):
        first = lax.axis_index("c") == 0
        lo = jnp.where(first, a_ref[...], b_ref[...])
        hi = jnp.where(first, b_ref[...], a_ref[...])
        g = jnp.concatenate([lo, hi[0:32], hi[64:96]], axis=0)
        g_ref[...] = g
        d_ref[...], mo_ref[...], vo_ref[...] = _adamw_math(w_ref[...], g, m_ref[...], v_ref[...])

    return pl.pallas_call(body, name=name, out_shape=[jax.ShapeDtypeStruct(w.shape, F32)] * 4)(w, mine, other, m, v)


def local_step(x, mem, positions, tgt, norm_in, weights, big_grads_ready, q_a_norm, kv_a_norm, gdn_conv, gdn_a_log,
               gdn_dt_bias, gdn_norm, mem_norm, norm_final):
    B, S, D = x.shape
    M = mem.shape[1]
    T = B * S
    N = S // CHUNK
    x2d = x.reshape(T, D)
    mem2d = mem.reshape(B * M, D)
    tgt2d = tgt.reshape(T, D)

    alog_row, dt_row = _lane_row(gdn_a_log), _lane_row(gdn_dt_bias)

    half = MLA_ROPE // 2
    inv_freq = 1.0 / (ROPE_THETA ** (jnp.arange(half, dtype=F32) / half))
    z32 = jnp.zeros((half,), F32)
    o32 = jnp.ones((half,), F32)
    inv_row = jnp.concatenate([inv_freq, z32, inv_freq, z32]).reshape(1, 128)
    sgn_row = jnp.concatenate([-o32, z32, o32, z32]).reshape(1, 128)
    msk_row = jnp.concatenate([o32, z32, o32, z32]).reshape(1, 128)
    cos_t, sin_t = rope_tables(positions.reshape(T, 1), inv_row, sgn_row, msk_row, after=weights[3])

    h = rms_fwd(x2d, norm_in, "rms_in", after=weights[3])
    wp, behind = weights[0]((h, cos_t))
    P = mm(h, wp, "nt", BF16, "in_proj", bm=512, bn=1536, n_outer=True, after=behind)
    wq, wkv = weights[1](P)
    Q, K, V, qn, kvn = mla_prep(P, q_a_norm, kv_a_norm, wq, wkv, cos_t, sin_t)
    o_mla, lse = mla_attn_fwd(Q, K, V, B, S)
    qkv = gdn_prep_fwd(P, gdn_conv, B, S)
    GB = gdn_gate_fwd(P, alog_row, dt_row, B, S)
    Grow = jnp.transpose(GB[:, :N_HEADS].reshape(B, N, CHUNK, N_HEADS), (0, 3, 1, 2))
    U, W, Tinv, A = gdn_chunk_fwd(qkv, GB, Grow, B, S)
    qkv3, GB3 = qkv.reshape(B, S, GDN_QKV), GB.reshape(B, S, 128)
    W3 = W.reshape(B, S, 512)
    o_gdn3, Vn3, St = gdn_scan_fwd(qkv3, U.reshape(B, S, 512), W3, GB3, A, B, S)
    o_gdn = o_gdn3.reshape(T, 512)
    w_mem_kv, w_out = weights[2](o_gdn)
    memn = rms_fwd(mem2d, mem_norm, "rms_mem")
    MKV = mm(memn, w_mem_kv, "nn", BF16, "mem_kv_proj")
    o_mem = mem_attn_fwd(P, MKV, B, S, M)
    mixed, dx2, dx2b, sq, g_norm_final = merge_fwd(o_mla, o_gdn, o_mem, P, x2d, tgt2d, w_out, gdn_norm, norm_final.reshape(1, D))

    g_w_out = mm(mixed, dx2b, "tn", BF16, "grad_w_out")
    dgate, do_mla, do_gdn, do_mem, g_gdn_norm = merge_bwd(dx2b, o_mla, o_gdn, o_mem, P, w_out, gdn_norm)

    dmemq, dMKV = mem_attn_bwd(P, MKV, do_mem, B, S, M)
    g_w_mem_kv = mm(memn, dMKV, "tn", BF16, "grad_w_mem_kv")
    started_early = big_grads_ready(dict(w_mem_kv=g_w_mem_kv, w_out=g_w_out), "early")
    dmemn = mm(dMKV, w_mem_kv, "nt", F32, "d_memn", after=(started_early,))
    g_mem_norm = gain_grad(mem2d, dmemn, "grad_mem_norm")

    dU3, dW3, dQ13, dK13, dA, dG13 = gdn_scan_bwd(do_gdn.reshape(B, S, 512), qkv3, W3, Vn3, GB3, A, St, B, S)
    r2 = lambda a: a.reshape(T, a.shape[-1])
    dqkv, dGB = gdn_chunk_bwd(qkv, GB, Grow, Tinv, dA, r2(dU3), r2(dW3), r2(dQ13), r2(dK13), r2(dG13), B, S)
    dPg, g_conv = gdn_prep_bwd(P, dqkv, gdn_conv, B, S)
    dab, g_ab = gdn_gate_bwd(P, dGB, alog_row, dt_row, B, S)

    dQ, dK, dV = mla_attn_bwd(Q, K, V, o_mla, do_mla, lse, B, S)
    dq_lin, dkv_lin, dPm, g_q_a_norm, g_kv_a_norm = mla_proj_bwd(dQ, dK, dV, cos_t, sin_t, P, dab, wq, wkv, q_a_norm, kv_a_norm)
    g_wq = mm(dq_lin, qn, "tn", BF16, "grad_w_q_b")
    g_wkv = mm(kvn, dkv_lin, "tn", BF16, "grad_w_kv_b")

    dP = [dPm, dmemq, dPg, dgate]
    g_wp = mm_cols_tn(dP, h, BF16, "grad_w_in")
    started = big_grads_ready(dict(w_in=g_wp, w_q_b=g_wq, w_kv_b=g_wkv), "late")
    grad_x, g_norm_in = in_proj_bwd(dP, wp, x2d, dx2, norm_in, started)

    grads = dict(
        norm_in=g_norm_in, q_a_norm=g_q_a_norm, kv_a_norm=g_kv_a_norm, gdn_conv=g_conv,
        gdn_a_log_dt_bias=g_ab, gdn_norm=g_gdn_norm,
        mem_norm=g_mem_norm, norm_final=g_norm_final)
    return sq, grad_x.reshape(B, S, D), grads


def kernel(x, mem, positions, norm_in, w_in, q_a_norm, w_q_b, kv_a_norm, w_kv_b, gdn_conv, gdn_a_log, gdn_dt_bias, gdn_norm, mem_norm, w_mem_kv, w_out, norm_final, loss_target, m_norm_in, m_w_in, m_q_a_norm, m_w_q_b, m_kv_a_norm, m_w_kv_b, m_gdn_conv, m_gdn_a_log, m_gdn_dt_bias, m_gdn_norm, m_mem_norm, m_w_mem_kv, m_w_out, m_norm_final, v_norm_in, v_w_in, v_q_a_norm, v_w_q_b, v_kv_a_norm, v_w_kv_b, v_gdn_conv, v_gdn_a_log, v_gdn_dt_bias, v_gdn_norm, v_mem_norm, v_w_mem_kv, v_w_out, v_norm_final):
    B = x.shape[0]
    cx, cy, cc = lax.axis_index("x"), lax.axis_index("y"), lax.axis_index("c")
    chip = 2 * cx + cy

    big_names = ("w_in", "w_q_b", "w_kv_b", "w_mem_kv", "w_out")
    rows_major = lambda a: jnp.transpose(a, (2, 0, 1))
    w_in3, m_in3, v_in3 = rows_major(w_in), rows_major(m_w_in), rows_major(v_w_in)
    w_qb_t, m_qb_t, v_qb_t = jnp.transpose(w_q_b[0]), jnp.transpose(m_w_q_b[0]), jnp.transpose(v_w_q_b[0])
    z32 = jnp.zeros((32, Q_LORA), BF16)
    qb_bf = w_qb_t.astype(BF16)
    qb_padded = jnp.concatenate([qb_bf[:160], z32, qb_bf[160:], z32])
    shards = [dense_bf16(w_in3, "w_in_bf16"), qb_padded, w_kv_b[0].astype(BF16), w_mem_kv[0].astype(BF16), w_out[0].astype(BF16)]
    splits = [(1, D_MODEL // 2)] + [(0, s.shape[0] // 2) for s in shards[1:]]
    *w_in_flight, w_in_started = half_gather_start(shards[0], "w_in_gather_start")
    conv_all = allgather_devices(gdn_conv[0], "allgather_conv")
    conv_cols = gdn_conv.shape[2]
    conv_full = jnp.transpose(conv_all[0::2], (1, 0, 2)).reshape(GDN_CONV, N_CHIPS * conv_cols)
    late_shapes = [(N_CHIPS,) + s.shape for s in shards[1:]]
    late = {}

    def w_in_ready(after):
        g_in = pass_halves_to_sibling(half_gather_wait(*w_in_flight, after, "w_in_gather_wait"), "w_in_gather_sibling")
        *late["a"], started_a = late_gather_start(shards[1:3], g_in, "late_gather_qkv_start")
        *late["b"], started_b = late_gather_start(shards[3:], started_a, "late_gather_mem_out_start")
        return pad_w_in_t(g_in), (started_a, started_b)

    def late_qkv(after):
        g_qb, g_kvb = late_gather_wait(*late["a"], after, "late_gather_qkv_wait")
        return g_qb.reshape(-1, Q_LORA), _perm_w_kv_b(g_kvb)

    def late_mem_out(after):
        g_mem, g_out_w = late_gather_wait(*late["b"], after, "late_gather_mem_out_wait")
        return g_mem.reshape(-1, g_mem.shape[2]), g_out_w.reshape(-1, g_out_w.shape[2])

    weights = (w_in_ready, late_qkv, late_mem_out, (w_in_started,))

    core = jnp.stack([cc]).astype(jnp.int32)
    chip_core = jnp.stack([chip, cc]).astype(jnp.int32)
    exchanges = {}
    by_chip = dict(w_in=unpad_w_in_t, w_q_b=lambda a: a.reshape(late_shapes[0]), w_kv_b=_unperm_w_kv_b,
                   w_mem_kv=lambda a: a.reshape(late_shapes[2]), w_out=lambda a: a.reshape(late_shapes[3]))

    def big_grads_ready(gb, group):
        idx = [big_names.index(n) for n in gb]
        parts = [by_chip[n](a) for n, a in gb.items()]
        sp = [splits[i] for i in idx]
        from_sibling = swap_sibling(parts, "rs_sibling_partial_" + group, sp)
        chip_sums = add_pairs(parts, from_sibling, sp, core, "rs_add_sibling_" + group)
        sems, sums_thru, lands, token = exchange_chips_start(chip_sums, "rs_exchange_start_" + group)
        exchanges[group] = dict(idx=idx, parts=parts, from_sibling=from_sibling, sems=sems, sums=sums_thru, lands=lands)
        return token

    sq, grad_x, g = local_step(x, mem, positions, loss_target, norm_in, weights, big_grads_ready, q_a_norm, kv_a_norm, conv_full,
                               gdn_a_log, gdn_dt_bias, gdn_norm, mem_norm, norm_final)

    small_names = ("norm_in", "q_a_norm", "kv_a_norm", "gdn_a_log", "gdn_dt_bias", "gdn_norm", "mem_norm", "norm_final")
    small = dict(norm_in=norm_in, q_a_norm=q_a_norm, kv_a_norm=kv_a_norm, gdn_a_log=gdn_a_log, gdn_dt_bias=gdn_dt_bias,
                 gdn_norm=gdn_norm, mem_norm=mem_norm, norm_final=norm_final)
    m_small = dict(norm_in=m_norm_in, q_a_norm=m_q_a_norm, kv_a_norm=m_kv_a_norm, gdn_a_log=m_gdn_a_log,
                   gdn_dt_bias=m_gdn_dt_bias, gdn_norm=m_gdn_norm, mem_norm=m_mem_norm, norm_final=m_norm_final)
    v_small = dict(norm_in=v_norm_in, q_a_norm=v_q_a_norm, kv_a_norm=v_kv_a_norm, gdn_a_log=v_gdn_a_log,
                   gdn_dt_bias=v_gdn_dt_bias, gdn_norm=v_gdn_norm, mem_norm=v_mem_norm, norm_final=v_norm_final)
    conv_rows = GDN_CONV * GDN_QKV // 1024
    g_block = pack_small([g.get(n) for n in small_names], g["gdn_a_log_dt_bias"], g["gdn_conv"], sq)
    g_block = sum_leading(allgather_devices(g_block, "allgather_small_grads"), "sum_small_grads")
    loss = 0.5 * jnp.sum(g_block[LOSS_ROW]) / D_MODEL
    g_conv = lax.dynamic_slice_in_dim(g_block[CONV_ROW0:CONV_ROW0 + conv_rows].reshape(GDN_CONV, GDN_QKV), chip * conv_cols,
                                      conv_cols, axis=1)
    as_row = lambda a: a.reshape(1, -1)
    updated = adamw_small(g_block, [as_row(small[n]) for n in small_names], [as_row(m_small[n]) for n in small_names],
                          [as_row(v_small[n]) for n in small_names])
    g_out, d_out, m_out, v_out = ({n: u[i].reshape(small[n].shape) for n, u in zip(small_names, updated)} for i in range(4))
    d_s = d_out["norm_in"]

    my_half = [None] * len(big_names)
    for group, e in exchanges.items():
        from_chips = exchange_chips_wait(e["sems"], e["sums"], e["lands"], d_s, "rs_exchange_wait_" + group)
        halves = add_fives(e["parts"], e["from_sibling"], from_chips, [splits[i] for i in e["idx"]], chip_core, "rs_add_chips_" + group)
        for i, a in zip(e["idx"], halves):
            my_half[i] = a
    other_half = swap_sibling(my_half, "rs_sibling_final")

    d_out["gdn_conv"], m_out["gdn_conv"], v_out["gdn_conv"] = adamw(gdn_conv, g_conv, m_gdn_conv, v_gdn_conv, "adamw_gdn_conv")
    g_out["gdn_conv"] = g_conv[None]
    res = adamw_untiled_rows(w_in3, my_half[0], other_half[0], m_in3, v_in3, "adamw_w_in")
    g_out["w_in"], d_out["w_in"], m_out["w_in"], v_out["w_in"] = [jnp.transpose(r, (1, 2, 0)) for r in res]
    res = adamw_w_q_b(w_qb_t, my_half[1], other_half[1], m_qb_t, v_qb_t, "adamw_w_q_b")
    g_out["w_q_b"], d_out["w_q_b"], m_out["w_q_b"], v_out["w_q_b"] = [jnp.transpose(r)[None] for r in res]
    rest = dict(w_kv_b=(w_kv_b, m_w_kv_b, v_w_kv_b), w_mem_kv=(w_mem_kv, m_w_mem_kv, v_w_mem_kv), w_out=(w_out, m_w_out, v_w_out))
    for i, n in enumerate(big_names):
        if n in rest:
            w_n, m_n, v_n = rest[n]
            g_out[n], d_out[n], m_out[n], v_out[n] = adamw_halves(w_n, my_half[i], other_half[i], m_n, v_n, splits[i], core, "adamw_" + n)

    order = ("norm_in", "w_in", "q_a_norm", "w_q_b", "kv_a_norm", "w_kv_b", "gdn_conv", "gdn_a_log", "gdn_dt_bias",
             "gdn_norm", "mem_norm", "w_mem_kv", "w_out", "norm_final")
    return (loss, grad_x, *[g_out[n] for n in order], *[d_out[n] for n in order], *[m_out[n] for n in order],
            *[v_out[n] for n in order])
```

```python
import jax
import jax.numpy as jnp
from jax import lax
from jax.experimental import pallas as pl
from jax.experimental.pallas import tpu as pltpu

F32 = jnp.float32
BF16 = jnp.bfloat16
BS = pl.BlockSpec

D_MODEL = 1024
N_HEADS = 4
MLA_NOPE, MLA_ROPE, MLA_V = 128, 64, 128
Q_LORA, KV_LORA = 384, 256
ROPE_THETA = 10000.0
GDN_DK = GDN_DV = 128
GDN_CONV = 4
CHUNK = 64
MEM_DH = 128
D_MIX = 1536
GDN_QKV = 1536
D_IN = 4296
EPS = 1e-6
ADAM_LR, ADAM_B1, ADAM_B2, ADAM_EPS, ADAM_WD, ADAM_STEP = 0.001, 0.9, 0.999, 1e-08, 0.01, 10

OFF_MLA = 0
OFF_MEMQ = 1024
OFF_GDN = 1536
OFF_GATE = 3072
N_PAD = 4608
HEAD_PAD = 256
MLA_SCALE = (MLA_NOPE + MLA_ROPE) ** -0.5
MEM_SCALE = MEM_DH ** -0.5
GDN_SCALE = GDN_DK ** -0.5
NEG = -1e30

NN = ((1,), (0,))
NT = ((1,), (1,))
TN = ((0,), (0,))


def _dot(a, b, dims):
    return lax.dot_general(a, b, (dims, ((), ())), preferred_element_type=F32)


def _bdot(spec, a, b, precision=None):
    return jnp.einsum(spec, a, b, preferred_element_type=F32, precision=precision)


def _arb(n):
    return pltpu.CompilerParams(dimension_semantics=("arbitrary",) * n)


def _sigmoid(x):
    return 1.0 / (1.0 + jnp.exp(-x))


def _softplus(z):
    return jnp.maximum(z, 0.0) + jnp.log(1.0 + jnp.exp(-jnp.abs(z)))


def _rope(t, cos_row, sin_row):
    return t * cos_row + pltpu.roll(t, 64, 1) * sin_row


def _rope_bwd(d, cos_row, sin_row):
    return d * cos_row + pltpu.roll(d * sin_row, 64, 1)


def rms_fwd(x, gain, name, tm=512, after=()):
    T, n = x.shape
    tm = min(tm, T)

    def body(x_ref, g_ref, *rest):
        xv = x_ref[...]
        r = lax.rsqrt(jnp.mean(xv * xv, axis=-1, keepdims=True) + EPS)
        rest[-1][...] = (xv * r * g_ref[...]).astype(BF16)

    return pl.pallas_call(
        body, name=name, grid=(T // tm,),
        in_specs=[BS((tm, n), lambda i: (i, 0)), BS((1, n), lambda i: (0, 0))] + [BS(memory_space=pl.ANY)] * len(after),
        out_specs=BS((tm, n), lambda i: (i, 0)),
        out_shape=jax.ShapeDtypeStruct((T, n), BF16), compiler_params=_arb(1))(x, gain, *after)


def mm(a, b, kind, out_dtype, name, bm=512, bn=None, n_outer=False, after=()):
    if kind == "nn":
        (M, K), (_, N) = a.shape, b.shape
    elif kind == "nt":
        (M, K), (N, _) = a.shape, b.shape
    else:
        (K, M), (_, N) = a.shape, b.shape
    bm, bn = min(bm, M), min(bn or N, N)
    assert M % bm == 0 and N % bn == 0, (name, M, N, K)
    ij = (lambda g0, g1: (g1, g0)) if n_outer else (lambda g0, g1: (g0, g1))
    a_spec = BS((K, bm), lambda g0, g1: (0, ij(g0, g1)[0])) if kind == "tn" else BS((bm, K), lambda g0, g1: (ij(g0, g1)[0], 0))
    once = dict(pipeline_mode=pl.Buffered(1)) if bn == N else {}
    b_spec = (BS((bn, K), lambda g0, g1: (ij(g0, g1)[1], 0), **once) if kind == "nt"
              else BS((K, bn), lambda g0, g1: (0, ij(g0, g1)[1]), **once))
    dims = {"nn": NN, "nt": NT, "tn": TN}[kind]

    def body(a_ref, b_ref, *rest):
        rest[-1][...] = _dot(a_ref[...].astype(BF16), b_ref[...].astype(BF16), dims).astype(out_dtype)

    grid = (N // bn, M // bm) if n_outer else (M // bm, N // bn)
    return pl.pallas_call(
        body, name=name, grid=grid, in_specs=[a_spec, b_spec] + [BS(memory_space=pl.ANY)] * len(after),
        out_specs=BS((bm, bn), lambda g0, g1: ij(g0, g1)),
        out_shape=jax.ShapeDtypeStruct((M, N), out_dtype), compiler_params=_arb(2))(a, b, *after)


def mm_cols_tn(pieces, b, out_dtype, name, bm=512):
    K, N = b.shape
    tiles = [p.shape[1] // bm for p in pieces]
    firsts = [sum(tiles[:i]) for i in range(len(tiles))]

    def body(*refs):
        b_ref, o_ref = refs[-2], refs[-1]
        i = pl.program_id(0)
        for a_ref, t0, n in zip(refs[:-2], firsts, tiles):
            @pl.when((i >= t0) & (i < t0 + n))
            def _(a_ref=a_ref):
                o_ref[...] = _dot(a_ref[...], b_ref[...], TN).astype(out_dtype)

    a_specs = [BS((K, bm), lambda i, t0=t0, n=n: (0, jnp.clip(i - t0, 0, n - 1))) for t0, n in zip(firsts, tiles)]
    return pl.pallas_call(
        body, name=name, grid=(sum(tiles),),
        in_specs=a_specs + [BS(b.shape, lambda i: (0, 0), pipeline_mode=pl.Buffered(1))],
        out_specs=BS((bm, N), lambda i: (i, 0)), out_shape=jax.ShapeDtypeStruct((sum(tiles) * bm, N), out_dtype),
        compiler_params=_arb(1))(*pieces, b)


def rope_tables(pos_col, inv_row, sgn_row, msk_row, tm=512, after=()):
    T = pos_col.shape[0]
    tm = min(tm, T)

    def body(p_ref, inv_ref, sgn_ref, msk_ref, *rest):
        c_ref, s_ref = rest[-2:]
        ang = p_ref[...].astype(F32) * inv_ref[...]
        c_ref[...] = jnp.cos(ang) * msk_ref[...]
        s_ref[...] = jnp.sin(ang) * sgn_ref[...]

    row = BS((1, 128), lambda i: (0, 0))
    return pl.pallas_call(
        body, name="rope_tables", grid=(T // tm,),
        in_specs=[BS((tm, 1), lambda i: (i, 0)), row, row, row] + [BS(memory_space=pl.ANY)] * len(after),
        out_specs=[BS((tm, 128), lambda i: (i, 0))] * 2,
        out_shape=[jax.ShapeDtypeStruct((T, 128), F32)] * 2, compiler_params=_arb(1))(pos_col, inv_row, sgn_row, msk_row, *after)


def mla_prep(P, gq, gkv, wq, wkv, cos_t, sin_t, tm=512):
    T = P.shape[0]
    tm = min(tm, T)

    def body(p_ref, gq_ref, gkv_ref, wq_ref, wkv_ref, c_ref, s_ref, q_ref, k_ref, v_ref, qn_ref, kvn_ref):
        p = p_ref[...].astype(F32)
        cq, ckv, kr = p[:, :Q_LORA], p[:, Q_LORA:Q_LORA + KV_LORA], p[:, 640:768]
        qn = (cq * lax.rsqrt(jnp.mean(cq * cq, axis=-1, keepdims=True) + EPS) * gq_ref[...]).astype(BF16)
        kvn = (ckv * lax.rsqrt(jnp.mean(ckv * ckv, axis=-1, keepdims=True) + EPS) * gkv_ref[...]).astype(BF16)
        qn_ref[...] = qn
        kvn_ref[...] = kvn
        q = _dot(qn, wq_ref[...], NT)
        kv = _dot(kvn, wkv_ref[...], NN)
        cos_row, sin_row = c_ref[...], s_ref[...]
        krr = _rope(kr, cos_row, sin_row).astype(BF16)
        for h in range(N_HEADS):
            lo = h * HEAD_PAD
            q_ref[:, lo:lo + 128] = (q[:, lo:lo + 128] * MLA_SCALE).astype(BF16)
            q_ref[:, lo + 128:lo + 256] = (_rope(q[:, lo + 128:lo + 256], cos_row, sin_row) * MLA_SCALE).astype(BF16)
            k_ref[:, lo:lo + 128] = kv[:, h * 128:(h + 1) * 128].astype(BF16)
            k_ref[:, lo + 128:lo + 256] = krr
            v_ref[:, lo:lo + 128] = kv[:, 512 + h * 128:512 + (h + 1) * 128].astype(BF16)
            v_ref[:, lo + 128:lo + 256] = jnp.ones((tm, 128), BF16)

    full = lambda r, c: BS((r, c), lambda i: (0, 0))
    rowb = lambda c: BS((tm, c), lambda i: (i, 0))
    return pl.pallas_call(
        body, name="mla_prep", grid=(T // tm,),
        in_specs=[rowb(1024), full(1, Q_LORA), full(1, KV_LORA), full(1024, Q_LORA), full(KV_LORA, 1024), rowb(128), rowb(128)],
        out_specs=[rowb(1024), rowb(1024), rowb(1024), rowb(Q_LORA), rowb(KV_LORA)],
        out_shape=[jax.ShapeDtypeStruct((T, 1024), BF16), jax.ShapeDtypeStruct((T, 1024), BF16),
                   jax.ShapeDtypeStruct((T, 1024), BF16), jax.ShapeDtypeStruct((T, Q_LORA), BF16),
                   jax.ShapeDtypeStruct((T, KV_LORA), BF16)],
        compiler_params=_arb(1))(P, gq, gkv, wq, wkv, cos_t, sin_t)


ATTN_HEADS_PER_STEP = 2
ATTN_STRIP = 32


def mla_attn_fwd(Q, K, V, B, S, tq=512, hp=ATTN_HEADS_PER_STEP):
    T = B * S
    tq = min(tq, S)
    nq = S // tq

    rs = min(ATTN_STRIP, tq)

    def body(q_ref, k_ref, v_ref, o_ref, lse_ref, m_s, acc_s, s_s, p_s, a_s):
        i = pl.program_id(2)
        m_s[...] = jnp.full_like(m_s, NEG)
        acc_s[...] = jnp.zeros_like(acc_s)

        def blk(j, masked):
            rows = pl.ds(pl.multiple_of(j * tq, tq), tq)
            for h in range(hp):
                hq = slice(h * HEAD_PAD, (h + 1) * HEAD_PAD)
                s_s[h] = _dot(q_ref[:, hq], k_ref[rows, hq], NT)
            for h in range(hp):
                for r0 in range(0, tq, rs):
                    rr = slice(r0, r0 + rs)
                    sv = s_s[h, rr, :]
                    if masked:
                        r = r0 + lax.broadcasted_iota(jnp.int32, (rs, tq), 0)
                        c = lax.broadcasted_iota(jnp.int32, (rs, tq), 1)
                        sv = jnp.where(r >= c, sv, NEG)
                    m_prev = m_s[h, rr, :]
                    m_new = jnp.maximum(m_prev, jnp.max(sv, axis=1, keepdims=True))
                    p_s[h, rr, :] = jnp.exp(sv - m_new).astype(BF16)
                    a_s[h, rr, :] = jnp.exp(m_prev - m_new)
                    m_s[h, rr, :] = m_new
            for h in range(hp):
                hq = slice(h * HEAD_PAD, (h + 1) * HEAD_PAD)
                acc_s[h] = a_s[h] * acc_s[h] + _dot(p_s[h], v_ref[rows, hq], NN)

        def loop(j, c):
            blk(j, False)
            return c

        lax.fori_loop(0, i, loop, 0)
        blk(i, True)
        for h in range(hp):
            den = acc_s[h, :, 128:256]
            o_ref[:, h * 128:(h + 1) * 128] = (acc_s[h, :, 0:128] / den).astype(BF16)
            lse_ref[h] = m_s[h] + jnp.log(den[:, 0:1])

    return pl.pallas_call(
        body, name="mla_attn_fwd", grid=(B, N_HEADS // hp, nq),
        in_specs=[BS((tq, hp * HEAD_PAD), lambda b, h, i: (b * nq + i, h)),
                  BS((S, hp * HEAD_PAD), lambda b, h, i: (b, h)),
                  BS((S, hp * HEAD_PAD), lambda b, h, i: (b, h))],
        out_specs=[BS((tq, hp * 128), lambda b, h, i: (b * nq + i, h)),
                   BS((hp, tq, 1), lambda b, h, i: (h, b * nq + i, 0))],
        out_shape=[jax.ShapeDtypeStruct((T, 512), BF16), jax.ShapeDtypeStruct((N_HEADS, T, 1), F32)],
        scratch_shapes=[pltpu.VMEM((hp, tq, 1), F32), pltpu.VMEM((hp, tq, HEAD_PAD), F32), pltpu.VMEM((hp, tq, tq), F32),
                        pltpu.VMEM((hp, tq, tq), BF16), pltpu.VMEM((hp, tq, 1), F32)],
        compiler_params=_arb(3))(Q, K, V)


def mla_attn_bwd(Q, K, V, O, dO, LSE, B, S, tq=512, hp=ATTN_HEADS_PER_STEP):
    T = B * S
    tq = min(tq, S)
    nq = S // tq

    rs = min(ATTN_STRIP, tq)

    def body(q_ref, k_ref, v_ref, o_ref, do_ref, lse_ref, dq_ref, dk_ref, dv_ref, delta_s, dq_s, dk_s, dv_s, s_s, dp_s, p_s, ds_s):
        j = pl.program_id(2)

        @pl.when(j == 0)
        def _():
            dq_s[...] = jnp.zeros_like(dq_s)
            for h in range(hp):
                sl = slice(h * 128, (h + 1) * 128)
                delta_s[h] = jnp.sum(do_ref[:, sl] * o_ref[:, sl].astype(F32), axis=1, keepdims=True)

        dk_s[...] = jnp.zeros_like(dk_s)
        dv_s[...] = jnp.zeros_like(dv_s)

        def step(i, c):
            rows = pl.ds(pl.multiple_of(i * tq, tq), tq)
            for h in range(hp):
                sq, sv = slice(h * HEAD_PAD, (h + 1) * HEAD_PAD), slice(h * 128, (h + 1) * 128)
                s_s[h] = _dot(q_ref[rows, sq], k_ref[:, sq], NT)
                dp_s[h] = _dot(do_ref[rows, sv].astype(BF16), v_ref[:, h * HEAD_PAD:h * HEAD_PAD + 128], NT)
            for h in range(hp):
                for r0 in range(0, tq, rs):
                    rr = slice(r0, r0 + rs)
                    seq_rows = pl.ds(pl.multiple_of(i * tq + r0, rs), rs)
                    r = i * tq + r0 + lax.broadcasted_iota(jnp.int32, (rs, tq), 0)
                    cc = j * tq + lax.broadcasted_iota(jnp.int32, (rs, tq), 1)
                    p = jnp.where(r >= cc, jnp.exp(s_s[h, rr, :] - lse_ref[h, seq_rows, :]), 0.0)
                    p_s[h, rr, :] = p.astype(BF16)
                    ds_s[h, rr, :] = (p * (dp_s[h, rr, :] - delta_s[h, seq_rows, :])).astype(BF16)
            for h in range(hp):
                sq, sv = slice(h * HEAD_PAD, (h + 1) * HEAD_PAD), slice(h * 128, (h + 1) * 128)
                dv_s[:, sv] += _dot(p_s[h], do_ref[rows, sv].astype(BF16), TN)
                dk_s[:, sq] += _dot(ds_s[h], q_ref[rows, sq], TN)
                dq_s[rows, sq] += _dot(ds_s[h], k_ref[:, sq], NN)
            return c

        lax.fori_loop(j, nq, step, 0)
        dk_ref[...] = dk_s[...].astype(BF16)
        dv_ref[...] = dv_s[...].astype(BF16)

        @pl.when(j == nq - 1)
        def _():
            dq_ref[...] = dq_s[...].astype(BF16)

    seq = lambda c: BS((S, c), lambda b, h, j: (b, h))
    blk = lambda c: BS((tq, c), lambda b, h, j: (b * nq + j, h))
    return pl.pallas_call(
        body, name="mla_attn_bwd", grid=(B, N_HEADS // hp, nq),
        in_specs=[seq(hp * HEAD_PAD), blk(hp * HEAD_PAD), blk(hp * HEAD_PAD), seq(hp * 128), seq(hp * 128),
                  BS((hp, S, 1), lambda b, h, j: (h, b, 0))],
        out_specs=[seq(hp * HEAD_PAD), blk(hp * HEAD_PAD), blk(hp * 128)],
        out_shape=[jax.ShapeDtypeStruct((T, 1024), BF16), jax.ShapeDtypeStruct((T, 1024), BF16),
                   jax.ShapeDtypeStruct((T, 512), BF16)],
        scratch_shapes=[pltpu.VMEM((hp, S, 1), F32), pltpu.VMEM((S, hp * HEAD_PAD), F32),
                        pltpu.VMEM((tq, hp * HEAD_PAD), F32), pltpu.VMEM((tq, hp * 128), F32),
                        pltpu.VMEM((hp, tq, tq), F32), pltpu.VMEM((hp, tq, tq), F32),
                        pltpu.VMEM((hp, tq, tq), BF16), pltpu.VMEM((hp, tq, tq), BF16)],
        compiler_params=_arb(3))(Q, K, V, O, dO, LSE)


def mla_proj_bwd(dQ, dK, dV, cos_t, sin_t, P, dab, wq, wkv, gq, gkv, tm=512):
    T = P.shape[0]
    tm = min(tm, T)

    def norm_bwd(x, dy, g):
        r = lax.rsqrt(jnp.mean(x * x, axis=-1, keepdims=True) + EPS)
        xh = x * r
        dxh = dy * g
        return r * (dxh - xh * jnp.mean(dxh * xh, axis=-1, keepdims=True)), jnp.sum(dy * xh, axis=0, keepdims=True)

    def body(dq_ref, dk_ref, dv_ref, c_ref, s_ref, p_ref, dab_ref, wq_ref, wkv_ref, gq_ref, gkv_ref,
             ql_ref, kvl_ref, o_ref, aq_ref, akv_ref):
        @pl.when(pl.program_id(0) == 0)
        def _():
            aq_ref[...] = jnp.zeros_like(aq_ref)
            akv_ref[...] = jnp.zeros_like(akv_ref)

        cos_row, sin_row = c_ref[...], s_ref[...]
        kr = jnp.zeros((tm, 128), F32)
        for h in range(N_HEADS):
            lo = h * HEAD_PAD
            ql_ref[:, lo:lo + 128] = (dq_ref[:, lo:lo + 128].astype(F32) * MLA_SCALE).astype(BF16)
            ql_ref[:, lo + 128:lo + 256] = (_rope_bwd(dq_ref[:, lo + 128:lo + 256].astype(F32), cos_row, sin_row) * MLA_SCALE).astype(BF16)
            kvl_ref[:, h * 128:(h + 1) * 128] = dk_ref[:, lo:lo + 128]
            kr = kr + dk_ref[:, lo + 128:lo + 256].astype(F32)
        kvl_ref[:, 512:] = dv_ref[...]
        dqn = _dot(ql_ref[...], wq_ref[...], NN)
        dkvn = _dot(kvl_ref[...], wkv_ref[...], NT)
        dcq, ggq = norm_bwd(p_ref[:, :Q_LORA].astype(F32), dqn, gq_ref[...])
        dckv, ggkv = norm_bwd(p_ref[:, Q_LORA:640].astype(F32), dkvn, gkv_ref[...])
        aq_ref[...] += ggq
        akv_ref[...] += ggkv
        o_ref[:, :Q_LORA] = dcq.astype(BF16)
        o_ref[:, Q_LORA:640] = dckv.astype(BF16)
        o_ref[:, 640:768] = _rope_bwd(kr, cos_row, sin_row).astype(BF16)
        o_ref[:, 768:896] = dab_ref[...]
        o_ref[:, 896:1024] = jnp.zeros((tm, 128), BF16)

    rowb = lambda c: BS((tm, c), lambda i: (i, 0))
    full = lambda r, c: BS((r, c), lambda i: (0, 0))
    return pl.pallas_call(
        body, name="mla_proj_bwd", grid=(T // tm,),
        in_specs=[rowb(1024), rowb(1024), rowb(512), rowb(128), rowb(128), rowb(1024), rowb(128),
                  full(1024, Q_LORA), full(KV_LORA, 1024), full(1, Q_LORA), full(1, KV_LORA)],
        out_specs=[rowb(1024), rowb(1024), rowb(1024), full(1, Q_LORA), full(1, KV_LORA)],
        out_shape=[jax.ShapeDtypeStruct((T, 1024), BF16)] * 3
        + [jax.ShapeDtypeStruct((1, Q_LORA), F32), jax.ShapeDtypeStruct((1, KV_LORA), F32)],
        compiler_params=_arb(1))(dQ, dK, dV, cos_t, sin_t, P, dab, wq, wkv, gq, gkv)


def _mem_probs(qh, kh):
    s = _dot(qh, kh, NT) * MEM_SCALE
    p = jnp.exp(s - jnp.max(s, axis=1, keepdims=True))
    return p / jnp.sum(p, axis=1, keepdims=True)


def mem_attn_fwd(P, MKV, B, S, M, tq=512):
    T = B * S
    tq = min(tq, S)
    nq = S // tq

    def body(q_ref, kv_ref, o_ref):
        for h in range(N_HEADS):
            sl = slice(h * 128, (h + 1) * 128)
            p = _mem_probs(q_ref[:, sl].astype(BF16), kv_ref[:, sl])
            o_ref[:, sl] = _dot(p.astype(BF16), kv_ref[:, 512 + h * 128:512 + (h + 1) * 128], NN).astype(BF16)

    return pl.pallas_call(
        body, name="mem_attn_fwd", grid=(B, nq),
        in_specs=[BS((tq, 512), lambda b, i: (b * nq + i, OFF_MEMQ // 512)), BS((M, 1024), lambda b, i: (b, 0))],
        out_specs=BS((tq, 512), lambda b, i: (b * nq + i, 0)),
        out_shape=jax.ShapeDtypeStruct((T, 512), BF16), compiler_params=_arb(2))(P, MKV)


def mem_attn_bwd(P, MKV, dO, B, S, M, tq=512):
    T = B * S
    tq = min(tq, S)
    nq = S // tq

    def body(q_ref, kv_ref, do_ref, dq_ref, dkv_ref):
        @pl.when(pl.program_id(1) == 0)
        def _():
            dkv_ref[...] = jnp.zeros_like(dkv_ref)

        for h in range(N_HEADS):
            sl = slice(h * 128, (h + 1) * 128)
            sv = slice(512 + h * 128, 512 + (h + 1) * 128)
            qh = q_ref[:, sl].astype(BF16)
            kh = kv_ref[:, sl]
            do = do_ref[:, sl].astype(BF16)
            p = _mem_probs(qh, kh)
            dkv_ref[:, sv] += _dot(p.astype(BF16), do, TN)
            dp = _dot(do, kv_ref[:, sv], NT)
            ds = (p * (dp - jnp.sum(dp * p, axis=1, keepdims=True)) * MEM_SCALE).astype(BF16)
            dq_ref[:, sl] = _dot(ds, kh, NN).astype(BF16)
            dkv_ref[:, sl] += _dot(ds, qh, TN)

    return pl.pallas_call(
        body, name="mem_attn_bwd", grid=(B, nq),
        in_specs=[BS((tq, 512), lambda b, i: (b * nq + i, OFF_MEMQ // 512)), BS((M, 1024), lambda b, i: (b, 0)),
                  BS((tq, 512), lambda b, i: (b * nq + i, 0))],
        out_specs=[BS((tq, 512), lambda b, i: (b * nq + i, 0)), BS((M, 1024), lambda b, i: (b, 0))],
        out_shape=[jax.ShapeDtypeStruct((T, 512), BF16), jax.ShapeDtypeStruct((B * M, 1024), F32)],
        compiler_params=_arb(2))(P, MKV, dO)


def gain_grad(x, dy, name, tm=256):
    T, n = x.shape
    tm = min(tm, T)

    def body(x_ref, dy_ref, o_ref):
        @pl.when(pl.program_id(0) == 0)
        def _():
            o_ref[...] = jnp.zeros_like(o_ref)

        xv = x_ref[...]
        xh = xv * lax.rsqrt(jnp.mean(xv * xv, axis=-1, keepdims=True) + EPS)
        o_ref[...] += jnp.sum(dy_ref[...] * xh, axis=0, keepdims=True)

    return pl.pallas_call(
        body, name=name, grid=(T // tm,),
        in_specs=[BS((tm, n), lambda i: (i, 0))] * 2, out_specs=BS((1, n), lambda i: (0, 0)),
        out_shape=jax.ShapeDtypeStruct((1, n), F32), compiler_params=_arb(1))(x, dy)


def _conv_silu(x, w, t):
    y = x * w[3:4, :]
    for s in range(1, GDN_CONV):
        y = y + jnp.where(t >= s, pltpu.roll(x, s, 0), 0.0) * w[3 - s:4 - s, :]
    return y, _sigmoid(y)


def gdn_prep_fwd(P, conv_w, B, S):
    T = B * S

    def body(x_ref, w_ref, o_ref):
        kind = pl.program_id(1)
        t = lax.broadcasted_iota(jnp.int32, (S, 1), 0)
        y, sg = _conv_silu(x_ref[...].astype(F32), w_ref[...], t)
        a = y * sg
        scale = jnp.where(kind == 0, GDN_SCALE, 1.0).astype(F32)
        for h in range(N_HEADS):
            sl = slice(h * 128, (h + 1) * 128)
            seg = a[:, sl]
            n = lax.rsqrt(jnp.sum(seg * seg, axis=-1, keepdims=True) + EPS)
            o_ref[:, sl] = jnp.where(kind < 2, seg * (n * scale), seg)

    return pl.pallas_call(
        body, name="gdn_prep_fwd", grid=(B, 3),
        in_specs=[BS((S, 512), lambda b, k: (b, OFF_GDN // 512 + k)), BS((GDN_CONV, 512), lambda b, k: (0, k))],
        out_specs=BS((S, 512), lambda b, k: (b, k)),
        out_shape=jax.ShapeDtypeStruct((T, GDN_QKV), F32), compiler_params=_arb(2))(P, conv_w)


def gdn_prep_bwd(P, dqkv, conv_w, B, S):
    T = B * S

    def body(x_ref, d_ref, w_ref, o_ref, gw_ref):
        kind = pl.program_id(0)

        @pl.when(pl.program_id(1) == 0)
        def _():
            gw_ref[...] = jnp.zeros_like(gw_ref)

        t = lax.broadcasted_iota(jnp.int32, (S, 1), 0)
        x = x_ref[...].astype(F32)
        w = w_ref[...]
        y, sg = _conv_silu(x, w, t)
        a = y * sg
        scale = jnp.where(kind == 0, GDN_SCALE, 1.0).astype(F32)
        das = []
        for h in range(N_HEADS):
            sl = slice(h * 128, (h + 1) * 128)
            seg, dseg = a[:, sl], d_ref[:, sl]
            n = lax.rsqrt(jnp.sum(seg * seg, axis=-1, keepdims=True) + EPS)
            dn = scale * (n * dseg - seg * (n * n * n) * jnp.sum(dseg * seg, axis=-1, keepdims=True))
            das.append(jnp.where(kind < 2, dn, dseg))
        dy = jnp.concatenate(das, axis=1) * (sg * (1.0 + y * (1.0 - sg)))
        dx = dy * w[3:4, :]
        gw_ref[3:4, :] += jnp.sum(dy * x, axis=0, keepdims=True)
        for s in range(1, GDN_CONV):
            dx = dx + jnp.where(t + s < S, pltpu.roll(dy, S - s, 0), 0.0) * w[3 - s:4 - s, :]
            gw_ref[3 - s:4 - s, :] += jnp.sum(dy * jnp.where(t >= s, pltpu.roll(x, s, 0), 0.0), axis=0, keepdims=True)
        o_ref[...] = dx.astype(BF16)

    return pl.pallas_call(
        body, name="gdn_prep_bwd", grid=(3, B),
        in_specs=[BS((S, 512), lambda k, b: (b, OFF_GDN // 512 + k)), BS((S, 512), lambda k, b: (b, k)),
                  BS((GDN_CONV, 512), lambda k, b: (0, k))],
        out_specs=[BS((S, 512), lambda k, b: (b, k)), BS((GDN_CONV, 512), lambda k, b: (0, k))],
        out_shape=[jax.ShapeDtypeStruct((T, GDN_QKV), BF16), jax.ShapeDtypeStruct((GDN_CONV, GDN_QKV), F32)],
        compiler_params=_arb(2))(P, dqkv, conv_w)


def _chunk_row(n_rows):
    return lax.broadcasted_iota(jnp.int32, (n_rows, 1), 0) % CHUNK


def gdn_gate_fwd(P, alog_row, dt_row, B, S):
    T = B * S

    def body(x_ref, al_ref, dt_ref, o_ref):
        x = x_ref[...].astype(F32)
        lane = lax.broadcasted_iota(jnp.int32, (1, 128), 1)
        g = jnp.where(lane < 4, -jnp.exp(al_ref[...]) * _softplus(x + dt_ref[...]), 0.0)
        t = _chunk_row(S)
        for s in (1, 2, 4, 8, 16, 32):
            g = g + jnp.where(t >= s, pltpu.roll(g, s, 0), 0.0)
        o_ref[...] = jnp.where(lane < 4, g, jnp.where(lane < 8, _sigmoid(x), 0.0))

    row = BS((1, 128), lambda b: (0, 0))
    return pl.pallas_call(
        body, name="gdn_gate_fwd", grid=(B,),
        in_specs=[BS((S, 128), lambda b: (b, 0)), row, row], out_specs=BS((S, 128), lambda b: (b, 0)),
        out_shape=jax.ShapeDtypeStruct((T, 128), F32), compiler_params=_arb(1))(P, alog_row, dt_row)


def gdn_gate_bwd(P, dGB, alog_row, dt_row, B, S):
    T = B * S

    def body(x_ref, d_ref, al_ref, dt_ref, o_ref, acc_ref):
        @pl.when(pl.program_id(0) == 0)
        def _():
            acc_ref[...] = jnp.zeros_like(acc_ref)

        x, d = x_ref[...].astype(F32), d_ref[...]
        lane = lax.broadcasted_iota(jnp.int32, (1, 128), 1)
        z = x + dt_ref[...]
        coef = -jnp.exp(al_ref[...])
        g = coef * _softplus(z)
        da = jnp.where(lane < 4, d * coef * _sigmoid(z), 0.0)
        beta = _sigmoid(x)
        o_ref[...] = jnp.where(lane < 4, da, jnp.where(lane < 8, d * beta * (1.0 - beta), 0.0)).astype(BF16)
        acc_ref[0:1, :] += jnp.sum(jnp.where(lane < 4, d * g, 0.0), axis=0, keepdims=True)
        acc_ref[1:2, :] += jnp.sum(da, axis=0, keepdims=True)

    row = BS((1, 128), lambda b: (0, 0))
    return pl.pallas_call(
        body, name="gdn_gate_bwd", grid=(B,),
        in_specs=[BS((S, 128), lambda b: (b, 0)), BS((S, 128), lambda b: (b, 0)), row, row],
        out_specs=[BS((S, 128), lambda b: (b, 0)), BS((8, 128), lambda b: (0, 0))],
        out_shape=[jax.ShapeDtypeStruct((T, 128), BF16), jax.ShapeDtypeStruct((8, 128), F32)],
        compiler_params=_arb(1))(P, dGB, alog_row, dt_row)


def _chunk_masks(nc):
    r = lax.broadcasted_iota(jnp.int32, (nc, CHUNK, CHUNK), 1)
    c = lax.broadcasted_iota(jnp.int32, (nc, CHUNK, CHUNK), 2)
    return r >= c, r > c


def _chunk_local(q, k, gc, gr, beta, incl, strict):
    decay = jnp.exp(jnp.where(incl, gc - gr, NEG))
    kb = k * beta
    kbf = k.astype(BF16)
    m_kk = _bdot("gcd,gjd->gcj", kb.astype(BF16), kbf)
    l_mat = jnp.where(strict, m_kk * decay, 0.0)
    a_mat = _bdot("gcd,gjd->gcj", q.astype(BF16), kbf) * decay
    return decay, kb, l_mat, a_mat


WY_SPLIT_LEVELS = 2


def _split_bf16(x):
    hi = x.astype(BF16)
    return hi, (x - hi.astype(F32)).astype(BF16)


def _mm_split(ah, al, bh, bl):
    spec = "gij,gjk->gik"
    return _bdot(spec, ah, bh) + (_bdot(spec, ah, bl) + _bdot(spec, al, bh))


def gdn_chunk_fwd(qkv, GB, Grow, B, S, nc=8):
    T = B * S
    N = S // CHUNK
    nc = min(nc, N)
    nb = N // nc
    R = nc * CHUNK

    def body(q_ref, k_ref, v_ref, gb_ref, gr_ref, u_ref, w_ref, t_ref, a_ref):
        incl, strict = _chunk_masks(nc)
        eye = (lax.broadcasted_iota(jnp.int32, (nc, CHUNK, CHUNK), 1)
               == lax.broadcasted_iota(jnp.int32, (nc, CHUNK, CHUNK), 2)).astype(F32)
        for h in range(N_HEADS):
            sl = slice(h * 128, (h + 1) * 128)
            q = q_ref[:, sl].reshape(nc, CHUNK, 128)
            k = k_ref[:, sl].reshape(nc, CHUNK, 128)
            v = v_ref[:, sl].reshape(nc, CHUNK, 128)
            gc = gb_ref[:, h:h + 1].reshape(nc, CHUNK, 1)
            beta = gb_ref[:, 4 + h:5 + h].reshape(nc, CHUNK, 1)
            gr = gr_ref[h][:, None, :]
            _, kb, l_mat, a_mat = _chunk_local(q, k, gc, gr, beta, incl, strict)
            pw = -l_mat
            tinv = eye + pw
            for level in range(5):
                if level < WY_SPLIT_LEVELS:
                    ph, pl_ = _split_bf16(pw)
                    pw = _mm_split(ph, pl_, ph, pl_)
                    ph, pl_ = _split_bf16(pw)
                    th, tl = _split_bf16(tinv)
                    tinv = tinv + _mm_split(th, tl, ph, pl_)
                else:
                    ph = pw.astype(BF16)
                    pw = _bdot("gij,gjk->gik", ph, ph)
                    tinv = tinv + _bdot("gij,gjk->gik", tinv.astype(BF16), pw.astype(BF16))
            tb = tinv.astype(BF16)
            u = _bdot("gcj,gjv->gcv", tb, (v * beta).astype(BF16))
            w = _bdot("gcj,gjk->gck", tb, (kb * jnp.exp(gc)).astype(BF16))
            u_ref[:, sl] = u.reshape(R, 128)
            w_ref[:, sl] = w.reshape(R, 128).astype(BF16)
            t_ref[h] = jnp.swapaxes(tinv, 1, 2).astype(BF16)
            a_ref[h] = a_mat.astype(BF16)

    rowb = lambda c, j: BS((R, c), lambda b, n: (b * nb + n, j))
    mat = BS((None, N_HEADS, nc, CHUNK, CHUNK), lambda b, n: (b, 0, n, 0, 0))
    return pl.pallas_call(
        body, name="gdn_chunk_fwd", grid=(B, nb),
        in_specs=[rowb(512, 0), rowb(512, 1), rowb(512, 2), rowb(128, 0),
                  BS((None, N_HEADS, nc, CHUNK), lambda b, n: (b, 0, n, 0))],
        out_specs=[rowb(512, 0), rowb(512, 0), mat, mat],
        out_shape=[jax.ShapeDtypeStruct((T, 512), F32), jax.ShapeDtypeStruct((T, 512), BF16),
                   jax.ShapeDtypeStruct((B, N_HEADS, N, CHUNK, CHUNK), BF16),
                   jax.ShapeDtypeStruct((B, N_HEADS, N, CHUNK, CHUNK), BF16)],
        compiler_params=_arb(2))(qkv, qkv, qkv, GB, Grow)


SCAN_CHUNKS = 4


def gdn_scan_fwd(qkv3, U3, W3, GB3, A, B, S):
    N = S // CHUNK
    cps = SCAN_CHUNKS if N % SCAN_CHUNKS == 0 else 1

    def body(q_ref, k_ref, u_ref, w_ref, gb_ref, a_ref, o_ref, vn_ref, st_ref, s_s):
        @pl.when(pl.program_id(0) == 0)
        def _():
            s_s[...] = jnp.zeros_like(s_s)

        for c in range(cps):
            rows = slice(c * CHUNK, (c + 1) * CHUNK)
            for b in range(B):
                for h in range(N_HEADS):
                    sl = slice(h * 128, (h + 1) * 128)
                    st = s_s[b, h]
                    st_ref[b, h, c] = st
                    stb = st.astype(BF16)
                    g = gb_ref[b, rows, h:h + 1]
                    gl = g[CHUNK - 1:CHUNK, :]
                    qg = (q_ref[b, rows, sl] * jnp.exp(g)).astype(BF16)
                    on_state = _dot(jnp.concatenate([w_ref[b, rows, sl].astype(BF16), qg], axis=0), stb, NN)
                    vn = u_ref[b, rows, sl] - on_state[:CHUNK]
                    vnb = vn.astype(BF16)
                    kd_t = jnp.transpose(k_ref[b, rows, sl] * jnp.exp(gl - g)).astype(BF16)
                    on_vn = _dot(jnp.concatenate([a_ref[b, h, c].astype(BF16), kd_t], axis=0), vnb, NN)
                    vn_ref[b, rows, sl] = vnb
                    o_ref[b, rows, sl] = (on_state[CHUNK:] + on_vn[:CHUNK]).astype(BF16)
                    s_s[b, h] = st * jnp.exp(gl) + on_vn[CHUNK:]

    tok = lambda c, j: BS((B, cps * CHUNK, c), lambda n: (0, n, j))
    return pl.pallas_call(
        body, name="gdn_scan_fwd", grid=(N // cps,),
        in_specs=[tok(512, 0), tok(512, 1), tok(512, 0), tok(512, 0), tok(128, 0),
                  BS((B, N_HEADS, cps, CHUNK, CHUNK), lambda n: (0, 0, n, 0, 0))],
        out_specs=[tok(512, 0), tok(512, 0), BS((B, N_HEADS, cps, 128, 128), lambda n: (0, 0, n, 0, 0))],
        out_shape=[jax.ShapeDtypeStruct((B, S, 512), BF16), jax.ShapeDtypeStruct((B, S, 512), BF16),
                   jax.ShapeDtypeStruct((B, N_HEADS, N, 128, 128), F32)],
        scratch_shapes=[pltpu.VMEM((B, N_HEADS, 128, 128), F32)],
        compiler_params=_arb(1))(qkv3, qkv3, U3, W3, GB3, A)


def gdn_scan_bwd(dO3, qkv3, W3, Vn3, GB3, A, St, B, S):
    N = S // CHUNK
    cps = SCAN_CHUNKS if N % SCAN_CHUNKS == 0 else 1

    def body(do_ref, q_ref, k_ref, w_ref, vn_ref, gb_ref, a_ref, st_ref,
             du_ref, dw_ref, dq_ref, dk_ref, da_ref, dg_ref, ds_s):
        @pl.when(pl.program_id(0) == 0)
        def _():
            ds_s[...] = jnp.zeros_like(ds_s)

        lane = lax.broadcasted_iota(jnp.int32, (1, 128), 1)
        last = lax.broadcasted_iota(jnp.int32, (CHUNK, 1), 0) == CHUNK - 1
        for c in reversed(range(cps)):
            rows = slice(c * CHUNK, (c + 1) * CHUNK)
            for b in range(B):
                dg_all = jnp.zeros((CHUNK, 128), F32)
                for h in range(N_HEADS):
                    sl = slice(h * 128, (h + 1) * 128)
                    st = st_ref[b, h, c]
                    stb = st.astype(BF16)
                    dsn = ds_s[b, h]
                    dsnb = dsn.astype(BF16)
                    g = gb_ref[b, rows, h:h + 1]
                    gl = g[CHUNK - 1:CHUNK, :]
                    egl = jnp.exp(gl)
                    ekd = jnp.exp(gl - g)
                    eg = jnp.exp(g)
                    q, k = q_ref[b, rows, sl], k_ref[b, rows, sl]
                    kd = k * ekd
                    qg = q * eg
                    do = do_ref[b, rows, sl].astype(BF16)
                    vnb = vn_ref[b, rows, sl].astype(BF16)
                    dvn = _dot(a_ref[b, h, c].astype(BF16), do, TN) + _dot(kd.astype(BF16), dsnb, NN)
                    dvnb = dvn.astype(BF16)
                    do_on = _dot(do, jnp.concatenate([stb, vnb], axis=0), NT)
                    dqg = do_on[:, :128]
                    da_ref[b, h, c] = do_on[:, 128:]
                    dkd = _dot(vnb, dsnb, NT)
                    ds_s[b, h] = (_dot(qg.astype(BF16), do, TN) + egl * dsn - _dot(w_ref[b, rows, sl].astype(BF16), dvnb, TN))
                    du_ref[b, rows, sl] = dvnb
                    dw_ref[b, rows, sl] = (-_dot(dvnb, stb, NT)).astype(BF16)
                    dq_ref[b, rows, sl] = dqg * eg
                    dk_ref[b, rows, sl] = dkd * ekd
                    ddel = jnp.sum(dkd * kd, axis=1, keepdims=True)
                    dgl = jnp.sum(ddel, axis=0, keepdims=True) + jnp.sum(jnp.sum(st * dsn, axis=1, keepdims=True), axis=0, keepdims=True) * egl
                    col = jnp.sum(dqg * qg, axis=1, keepdims=True) - ddel + jnp.where(last, dgl, 0.0)
                    dg_all = jnp.where(lane == h, col, dg_all)
                dg_ref[b, rows, :] = dg_all

    steps = N // cps
    tok = lambda c, j: BS((B, cps * CHUNK, c), lambda n: (0, steps - 1 - n, j))
    mat = lambda d: BS((B, N_HEADS, cps, d, d), lambda n: (0, 0, steps - 1 - n, 0, 0))
    return pl.pallas_call(
        body, name="gdn_scan_bwd", grid=(steps,),
        in_specs=[tok(512, 0), tok(512, 0), tok(512, 1), tok(512, 0), tok(512, 0), tok(128, 0), mat(CHUNK), mat(128)],
        out_specs=[tok(512, 0), tok(512, 0), tok(512, 0), tok(512, 0), mat(CHUNK), tok(128, 0)],
        out_shape=[jax.ShapeDtypeStruct((B, S, 512), BF16)] * 2 + [jax.ShapeDtypeStruct((B, S, 512), F32)] * 2
        + [jax.ShapeDtypeStruct((B, N_HEADS, N, CHUNK, CHUNK), F32), jax.ShapeDtypeStruct((B, S, 128), F32)],
        scratch_shapes=[pltpu.VMEM((B, N_HEADS, 128, 128), F32)],
        compiler_params=_arb(1))(dO3, qkv3, qkv3, W3, Vn3, GB3, A, St)


def gdn_chunk_bwd(qkv, GB, Grow, Tinv, dA, dU, dW, dQ1, dK1, dG1, B, S, nc=8):
    T = B * S
    N = S // CHUNK
    nc = min(nc, N)
    nb = N // nc
    R = nc * CHUNK

    def body(q_ref, k_ref, v_ref, gb_ref, gr_ref, t_ref, da_ref, du_ref, dw_ref, dq1_ref, dk1_ref, dg1_ref, o_ref, dgb_ref):
        incl, strict = _chunk_masks(nc)
        lane = lax.broadcasted_iota(jnp.int32, (1, 128), 1)
        dg_all = dg1_ref[...]
        db_all = jnp.zeros((R, 128), F32)
        for h in range(N_HEADS):
            sl = slice(h * 128, (h + 1) * 128)
            q = q_ref[:, sl].reshape(nc, CHUNK, 128)
            k = k_ref[:, sl].reshape(nc, CHUNK, 128)
            v = v_ref[:, sl].reshape(nc, CHUNK, 128)
            gc = gb_ref[:, h:h + 1].reshape(nc, CHUNK, 1)
            beta = gb_ref[:, 4 + h:5 + h].reshape(nc, CHUNK, 1)
            gr = gr_ref[h][:, None, :]
            decay, kb, l_mat, a_mat = _chunk_local(q, k, gc, gr, beta, incl, strict)
            eg = jnp.exp(gc)
            kbg = kb * eg
            vb = v * beta
            tt = t_ref[h].astype(BF16)
            du = du_ref[:, sl].reshape(nc, CHUNK, 128).astype(BF16)
            dw = dw_ref[:, sl].reshape(nc, CHUNK, 128).astype(BF16)
            dvb = _bdot("gjc,gcv->gjv", tt, du)
            dkbg = _bdot("gjc,gck->gjk", tt, dw)
            dt = _bdot("gcv,gjv->gcj", du, vb.astype(BF16)) + _bdot("gck,gjk->gcj", dw, kbg.astype(BF16))
            tmp = _bdot("gca,gab->gcb", tt, dt.astype(BF16))
            dl = jnp.where(strict, -_bdot("gcb,gbd->gcd", tmp.astype(BF16), tt), 0.0)
            da = da_ref[h]
            dm = (dl * decay).astype(BF16)
            dqk = (da * decay).astype(BF16)
            kbf = k.astype(BF16)
            dkb = _bdot("gcj,gjd->gcd", dm, kbf) + dkbg * eg
            dk = (_bdot("gcj,gcd->gjd", dm, kb.astype(BF16)) + _bdot("gcj,gcd->gjd", dqk, q.astype(BF16))
                  + dk1_ref[:, sl].reshape(nc, CHUNK, 128) + dkb * beta)
            dq = _bdot("gcj,gjd->gcd", dqk, kbf) + dq1_ref[:, sl].reshape(nc, CHUNK, 128)
            e = dl * l_mat + da * a_mat
            dgc = (jnp.sum(e, axis=2, keepdims=True) - jnp.sum(jnp.swapaxes(e, 1, 2), axis=2, keepdims=True)
                   + jnp.sum(dkbg * kbg, axis=2, keepdims=True))
            dbeta = jnp.sum(dkb * k, axis=2, keepdims=True) + jnp.sum(dvb * v, axis=2, keepdims=True)
            o_ref[:, sl] = dq.reshape(R, 128)
            o_ref[:, 512 + h * 128:512 + (h + 1) * 128] = dk.reshape(R, 128)
            o_ref[:, 1024 + h * 128:1024 + (h + 1) * 128] = (dvb * beta).reshape(R, 128)
            dg_all = dg_all + jnp.where(lane == h, dgc.reshape(R, 1), 0.0)
            db_all = jnp.where(lane == 4 + h, dbeta.reshape(R, 1), db_all)
        t = _chunk_row(R)
        for s in (1, 2, 4, 8, 16, 32):
            dg_all = dg_all + jnp.where(t + s < CHUNK, pltpu.roll(dg_all, R - s, 0), 0.0)
        dgb_ref[...] = jnp.where(lane < 4, dg_all, db_all)

    rowb = lambda c, j: BS((R, c), lambda b, n: (b * nb + n, j))
    mat = BS((None, N_HEADS, nc, CHUNK, CHUNK), lambda b, n: (b, 0, n, 0, 0))
    return pl.pallas_call(
        body, name="gdn_chunk_bwd", grid=(B, nb),
        in_specs=[rowb(512, 0), rowb(512, 1), rowb(512, 2), rowb(128, 0),
                  BS((None, N_HEADS, nc, CHUNK), lambda b, n: (b, 0, n, 0)), mat, mat,
                  rowb(512, 0), rowb(512, 0), rowb(512, 0), rowb(512, 0), rowb(128, 0)],
        out_specs=[rowb(GDN_QKV, 0), rowb(128, 0)],
        out_shape=[jax.ShapeDtypeStruct((T, GDN_QKV), F32), jax.ShapeDtypeStruct((T, 128), F32)],
        compiler_params=_arb(2))(qkv, qkv, qkv, GB, Grow, Tinv, dA, dU, dW, dQ1, dK1, dG1)


def _gdn_out_norm(og, gg):
    outs, xhs, rs = [], [], []
    for h in range(N_HEADS):
        seg = og[:, h * 128:(h + 1) * 128]
        r = lax.rsqrt(jnp.mean(seg * seg, axis=-1, keepdims=True) + EPS)
        xh = seg * r
        outs.append(xh * gg)
        xhs.append(xh)
        rs.append(r)
    return outs, xhs, rs


def merge_fwd(o_mla, o_gdn, o_mem, P, x, tgt, w_out, g_gdn, g_fin, tm=512):
    T = x.shape[0]
    tm = min(tm, T)

    def body(om_ref, og_ref, oc_ref, gate_ref, x_ref, t_ref, w_ref, gg_ref, gf_ref, mix_ref, dx_ref, dxb_ref, sq_ref, gnf_ref):
        @pl.when(pl.program_id(0) == 0)
        def _():
            sq_ref[...] = jnp.zeros_like(sq_ref)
            gnf_ref[...] = jnp.zeros_like(gnf_ref)

        ogn, _, _ = _gdn_out_norm(og_ref[...].astype(F32), gg_ref[...])
        cat = jnp.concatenate([om_ref[...].astype(F32)] + ogn + [oc_ref[...].astype(F32)], axis=1)
        gt = gate_ref[...].astype(F32)
        mixed = (cat * (gt * _sigmoid(gt))).astype(BF16)
        mix_ref[...] = mixed
        x2 = x_ref[...] + _dot(mixed, w_ref[...], NN)
        r2 = lax.rsqrt(jnp.mean(x2 * x2, axis=-1, keepdims=True) + EPS)
        xh = x2 * r2
        gf = gf_ref[...]
        diff = xh * gf - t_ref[...]
        sq_ref[...] += jnp.sum(diff * diff, axis=0, keepdims=True)
        dy = diff * (1.0 / D_MODEL)
        gnf_ref[...] += jnp.sum(dy * xh, axis=0, keepdims=True)
        dxh = dy * gf
        dx = r2 * (dxh - xh * jnp.mean(dxh * xh, axis=-1, keepdims=True))
        dx_ref[...] = dx
        dxb_ref[...] = dx.astype(BF16)

    rowb = lambda c, j=0: BS((tm, c), lambda i: (i, j))
    full = lambda r, c: BS((r, c), lambda i: (0, 0))
    return pl.pallas_call(
        body, name="merge_fwd", grid=(T // tm,),
        in_specs=[rowb(512), rowb(512), rowb(512), rowb(D_MIX, OFF_GATE // D_MIX), rowb(D_MODEL), rowb(D_MODEL),
                  full(D_MIX, D_MODEL), full(1, 128), full(1, D_MODEL)],
        out_specs=[rowb(D_MIX), rowb(D_MODEL), rowb(D_MODEL), full(1, D_MODEL), full(1, D_MODEL)],
        out_shape=[jax.ShapeDtypeStruct((T, D_MIX), BF16), jax.ShapeDtypeStruct((T, D_MODEL), F32),
                   jax.ShapeDtypeStruct((T, D_MODEL), BF16),
                   jax.ShapeDtypeStruct((1, D_MODEL), F32), jax.ShapeDtypeStruct((1, D_MODEL), F32)],
        compiler_params=_arb(1))(o_mla, o_gdn, o_mem, P, x, tgt, w_out, g_gdn, g_fin)


def merge_bwd(dx2, o_mla, o_gdn, o_mem, P, w_out, g_gdn, tm=512):
    T = dx2.shape[0]
    tm = min(tm, T)

    def body(dx_ref, om_ref, og_ref, oc_ref, gate_ref, w_ref, gg_ref, dgate_ref, dom_ref, dog_ref, doc_ref, ggn_ref):
        @pl.when(pl.program_id(0) == 0)
        def _():
            ggn_ref[...] = jnp.zeros_like(ggn_ref)

        gg = gg_ref[...]
        dmix = _dot(dx_ref[...].astype(BF16), w_ref[...], NT)
        ogn, xhs, rs = _gdn_out_norm(og_ref[...].astype(F32), gg)
        cat = jnp.concatenate([om_ref[...].astype(F32)] + ogn + [oc_ref[...].astype(F32)], axis=1)
        gt = gate_ref[...].astype(F32)
        sg = _sigmoid(gt)
        dgate_ref[...] = (dmix * cat * (sg * (1.0 + gt * (1.0 - sg)))).astype(BF16)
        dcat = dmix * (gt * sg)
        dom_ref[...] = dcat[:, :512].astype(BF16)
        doc_ref[...] = dcat[:, 1024:].astype(BF16)
        acc = jnp.zeros((1, 128), F32)
        for h in range(N_HEADS):
            dseg = dcat[:, 512 + h * 128:512 + (h + 1) * 128]
            acc = acc + jnp.sum(dseg * xhs[h], axis=0, keepdims=True)
            dxh = dseg * gg
            dog_ref[:, h * 128:(h + 1) * 128] = (rs[h] * (dxh - xhs[h] * jnp.mean(dxh * xhs[h], axis=-1, keepdims=True))).astype(BF16)
        ggn_ref[...] += acc

    rowb = lambda c, j=0: BS((tm, c), lambda i: (i, j))
    full = lambda r, c: BS((r, c), lambda i: (0, 0))
    return pl.pallas_call(
        body, name="merge_bwd", grid=(T // tm,),
        in_specs=[rowb(D_MODEL), rowb(512), rowb(512), rowb(512), rowb(D_MIX, OFF_GATE // D_MIX),
                  full(D_MIX, D_MODEL), full(1, 128)],
        out_specs=[rowb(D_MIX), rowb(512), rowb(512), rowb(512), full(1, 128)],
        out_shape=[jax.ShapeDtypeStruct((T, D_MIX), BF16)] + [jax.ShapeDtypeStruct((T, 512), BF16)] * 3
        + [jax.ShapeDtypeStruct((1, 128), F32)],
        compiler_params=_arb(1))(dx2, o_mla, o_gdn, o_mem, P, w_out, g_gdn)


def in_proj_bwd(dP, wp, x, dx2, gain, after, tm=512):
    T, n = x.shape
    tm = min(tm, T)
    k = len(dP)
    widths = [p.shape[1] for p in dP]
    offs = [sum(widths[:i]) for i in range(k)]

    def body(*refs):
        w_ref, x_ref, dx2_ref, g_ref = refs[k:k + 4]
        o_ref, acc_ref = refs[-2:]

        @pl.when(pl.program_id(0) == 0)
        def _():
            acc_ref[...] = jnp.zeros_like(acc_ref)

        dy = None
        for a_ref, off, w in zip(refs[:k], offs, widths):
            d = _dot(a_ref[...], w_ref[off:off + w, :], NN)
            dy = d if dy is None else dy + d
        xv = x_ref[...]
        r = lax.rsqrt(jnp.mean(xv * xv, axis=-1, keepdims=True) + EPS)
        xh = xv * r
        acc_ref[...] += jnp.sum(dy * xh, axis=0, keepdims=True)
        dxh = dy * g_ref[...]
        o_ref[...] = dx2_ref[...] + r * (dxh - xh * jnp.mean(dxh * xh, axis=-1, keepdims=True))

    rowb = BS((tm, n), lambda i: (i, 0))
    full = BS((1, n), lambda i: (0, 0))
    return pl.pallas_call(
        body, name="in_proj_bwd", grid=(T // tm,),
        in_specs=[BS((tm, w), lambda i: (i, 0)) for w in widths]
        + [BS(wp.shape, lambda i: (0, 0), pipeline_mode=pl.Buffered(1)), rowb, rowb, full, BS(memory_space=pl.ANY)],
        out_specs=[rowb, full], out_shape=[jax.ShapeDtypeStruct((T, n), F32), jax.ShapeDtypeStruct((1, n), F32)],
        compiler_params=_arb(1))(*dP, wp, x, dx2, gain, after)


W_IN_SHARD = D_IN // 4
_GDN0 = Q_LORA + KV_LORA + MLA_ROPE
_AB0 = _GDN0 + GDN_QKV
_MEMQ0 = _AB0 + 2 * N_HEADS
_GATE0 = _MEMQ0 + N_HEADS * MEM_DH


def _w_in_row_map():
    a, m, gt = _AB0 - 2 * W_IN_SHARD, _MEMQ0 - 2 * W_IN_SHARD, _GATE0 - 2 * W_IN_SHARD
    e0 = OFF_GDN + W_IN_SHARD - _GDN0
    e1 = e0 + W_IN_SHARD
    e2 = OFF_GATE + W_IN_SHARD - gt
    return [(0, 0, 0, 672), (0, 672, 704, 32), (2, a, 768, m - a), (2, m, OFF_MEMQ, gt - m), (0, _GDN0, OFF_GDN, W_IN_SHARD - _GDN0),
            (1, 0, e0, W_IN_SHARD), (2, 0, e1, a), (2, gt, OFF_GATE, W_IN_SHARD - gt), (3, 0, e2, W_IN_SHARD)]


_W_IN_ZERO_ROWS = [(672, 32), (736, 32), (776, 248)]
W_IN_LANES = 256


def pad_w_in_t(shards):
    per_half = shards.shape[3] // W_IN_LANES

    def body(s_ref, o_ref):
        for r0, n in _W_IN_ZERO_ROWS:
            o_ref[r0:r0 + n, :] = jnp.zeros((n, W_IN_LANES), o_ref.dtype)
        for q, src, dst, n in _w_in_row_map():
            o_ref[dst:dst + n, :] = s_ref[q, src:src + n, :]

    return pl.pallas_call(
        body, name="pad_w_in_t", grid=(D_MODEL // W_IN_LANES,),
        in_specs=[BS((N_CHIPS, None, W_IN_SHARD, W_IN_LANES), lambda j: (0, j // per_half, 0, j % per_half))],
        out_specs=BS((N_PAD, W_IN_LANES), lambda j: (0, j)),
        out_shape=jax.ShapeDtypeStruct((N_PAD, D_MODEL), shards.dtype), compiler_params=_arb(1))(shards)


def unpad_w_in_t(g):
    def body(g_ref, o_ref):
        for q, src, dst, n in _w_in_row_map():
            o_ref[q, src:src + n, :] = g_ref[dst:dst + n, :]

    return pl.pallas_call(
        body, name="unpad_w_in_t", grid=(D_MODEL // W_IN_LANES,),
        in_specs=[BS((N_PAD, W_IN_LANES), lambda j: (0, j))], out_specs=BS((N_CHIPS, W_IN_SHARD, W_IN_LANES), lambda j: (0, 0, j)),
        out_shape=jax.ShapeDtypeStruct((N_CHIPS, W_IN_SHARD, D_MODEL), g.dtype), compiler_params=_arb(1))(g)


def _perm_w_kv_b(s):
    return jnp.concatenate([s[h, :, :128] for h in range(N_HEADS)] + [s[h, :, 128:] for h in range(N_HEADS)], axis=1)


def _unperm_w_kv_b(g):
    return jnp.stack([jnp.concatenate([g[:, h * 128:(h + 1) * 128], g[:, 512 + h * 128:512 + (h + 1) * 128]], axis=1)
                      for h in range(N_HEADS)])


def _lane_row(v4):
    return jnp.pad(v4.reshape(1, -1).astype(F32), ((0, 0), (0, 128 - v4.size)))


N_CHIPS = 4
MESH = pl.DeviceIdType.MESH
ANY = BS(memory_space=pl.ANY)


def _place():
    return lax.axis_index("x"), lax.axis_index("y"), lax.axis_index("c")


def _other_chips(x, y):
    return [(1 - x, y), (x, 1 - y), (1 - x, 1 - y)]


def _half(split, which):
    axis, size = split
    ds = pl.ds(pl.multiple_of(which * size, 16 if axis == 0 else 128), size)
    return (ds, slice(None)) if axis == 0 else (slice(None), ds)


SEM = BS(memory_space=pltpu.SEMAPHORE)
HBM = BS(memory_space=pltpu.HBM)
_IN_HBM = lambda a: pltpu.with_memory_space_constraint(a, pltpu.HBM)
_SIDE_EFFECT = pltpu.SideEffectType.DATAFLOW_SIDE_EFFECTING


def _late_gather_copies(s_refs, l_refs, send_sems, recv_sems, local_sems, with_arrivals):
    x, y, c = _place()
    sends, recvs, locals_ = [], [], []
    for i, (s_ref, l_ref) in enumerate(zip(s_refs, l_refs)):
        locals_.append(pltpu.make_async_copy(s_ref, l_ref.at[2 * x + y], local_sems.at[i]))
        for j, (px, py) in enumerate(_other_chips(x, y)):
            k = 3 * i + j
            sends.append(pltpu.make_async_remote_copy(src_ref=s_ref, dst_ref=l_ref.at[2 * x + y], send_sem=send_sems.at[k],
                                                      recv_sem=recv_sems.at[k], device_id=(px, py, c), device_id_type=MESH))
            if with_arrivals:
                recvs.append(pltpu.make_async_remote_copy(src_ref=s_ref, dst_ref=l_ref.at[2 * px + py], send_sem=send_sems.at[k],
                                                          recv_sem=recv_sems.at[k], device_id=(px, py, c), device_id_type=MESH))
    return sends, recvs, locals_


def late_gather_start(shards, after, name):
    n = len(shards)

    def body(*refs):
        s_refs, l_refs = refs[:n], refs[n:2 * n]
        send_sems, recv_sems, local_sems = refs[2 * n + 1:2 * n + 4]
        token = refs[-1]
        sends, _, locals_ = _late_gather_copies(s_refs, l_refs, send_sems, recv_sems, local_sems, False)
        for cp in locals_ + sends:
            cp.start()
        token[...] = jnp.zeros_like(token)

    lands = [lax.empty((N_CHIPS,) + s.shape, s.dtype) for s in shards]
    hbm_like = lambda a: pltpu.HBM(a.shape, a.dtype)
    out = pl.pallas_call(
        body, name=name,
        out_shape=[pltpu.SemaphoreType.DMA((3 * n,)), pltpu.SemaphoreType.DMA((3 * n,)), pltpu.SemaphoreType.DMA((n,))]
        + [hbm_like(s) for s in shards] + [hbm_like(l) for l in lands] + [jax.ShapeDtypeStruct((8, 128), F32)],
        in_specs=[HBM] * (2 * n) + [BS(memory_space=pl.ANY)], out_specs=[SEM] * 3 + [HBM] * (2 * n) + [BS(memory_space=pltpu.VMEM)],
        input_output_aliases={i: 3 + i for i in range(2 * n)},
        compiler_params=pltpu.CompilerParams(has_side_effects=_SIDE_EFFECT))(
            *[_IN_HBM(s) for s in shards], *[_IN_HBM(l) for l in lands], after)
    return out[:3], out[3:3 + n], out[3 + n:3 + 2 * n], out[-1]


def late_gather_wait(sems, shards, lands, after, name):
    n = len(shards)

    def body(*refs):
        s_refs, l_refs = refs[:n], refs[n:2 * n]
        send_sems, recv_sems, local_sems = refs[2 * n:2 * n + 3]
        sends, recvs, locals_ = _late_gather_copies(s_refs, l_refs, send_sems, recv_sems, local_sems, True)
        for cp in locals_:
            cp.wait()
        for cp in sends:
            cp.wait_send()
        for cp in recvs:
            cp.wait_recv()

    hbm_like = lambda a: pltpu.HBM(a.shape, a.dtype)
    out = pl.pallas_call(
        body, name=name, out_shape=[hbm_like(s) for s in shards] + [hbm_like(l) for l in lands],
        in_specs=[HBM] * (2 * n) + [SEM] * 3 + [BS(memory_space=pl.ANY)], out_specs=[HBM] * (2 * n),
        input_output_aliases={i: i for i in range(2 * n)},
        compiler_params=pltpu.CompilerParams(has_side_effects=_SIDE_EFFECT))(*shards, *lands, *sems, after)
    return out[n:]


def _half_gather_copies(s_ref, l_ref, send_sems, recv_sems, with_arrivals):
    x, y, c = _place()
    sends, recvs = [], []
    for j, (px, py) in enumerate(_other_chips(x, y)):
        sends.append(pltpu.make_async_remote_copy(src_ref=s_ref.at[c], dst_ref=l_ref.at[2 * x + y, c], send_sem=send_sems.at[j],
                                                  recv_sem=recv_sems.at[j], device_id=(px, py, c), device_id_type=MESH))
        if with_arrivals:
            recvs.append(pltpu.make_async_remote_copy(src_ref=s_ref.at[c], dst_ref=l_ref.at[2 * px + py, c], send_sem=send_sems.at[j],
                                                      recv_sem=recv_sems.at[j], device_id=(px, py, c), device_id_type=MESH))
    return sends, recvs


def half_gather_start(shard, name):
    def body(s_ref, l_ref, send_sems, recv_sems, local_sem, s_thru, l_thru, token):
        x, y, _ = _place()
        pltpu.make_async_copy(s_ref, l_ref.at[2 * x + y], local_sem.at[0]).start()
        for cp in _half_gather_copies(s_ref, l_ref, send_sems, recv_sems, False)[0]:
            cp.start()
        token[...] = jnp.zeros_like(token)

    land = lax.empty((N_CHIPS,) + shard.shape, shard.dtype)
    out = pl.pallas_call(
        body, name=name,
        out_shape=[pltpu.SemaphoreType.DMA((3,)), pltpu.SemaphoreType.DMA((3,)), pltpu.SemaphoreType.DMA((1,)),
                   pltpu.HBM(shard.shape, shard.dtype), pltpu.HBM(land.shape, land.dtype), jax.ShapeDtypeStruct((8, 128), F32)],
        in_specs=[HBM, HBM], out_specs=[SEM] * 3 + [HBM, HBM, BS(memory_space=pltpu.VMEM)],
        input_output_aliases={0: 3, 1: 4},
        compiler_params=pltpu.CompilerParams(has_side_effects=_SIDE_EFFECT))(_IN_HBM(shard), _IN_HBM(land))
    return out[:3], out[3], out[4], out[5]


def half_gather_wait(sems, shard, land, after, name):
    def body(s_ref, l_ref, send_sems, recv_sems, local_sem, *rest):
        x, y, _ = _place()
        pltpu.make_async_copy(s_ref, l_ref.at[2 * x + y], local_sem.at[0]).wait()
        sends, recvs = _half_gather_copies(s_ref, l_ref, send_sems, recv_sems, True)
        for cp in sends:
            cp.wait_send()
        for cp in recvs:
            cp.wait_recv()

    out = pl.pallas_call(
        body, name=name, out_shape=[pltpu.HBM(shard.shape, shard.dtype), pltpu.HBM(land.shape, land.dtype)],
        in_specs=[HBM, HBM] + [SEM] * 3 + [BS(memory_space=pl.ANY)] * len(after), out_specs=[HBM, HBM],
        input_output_aliases={0: 0, 1: 1},
        compiler_params=pltpu.CompilerParams(has_side_effects=_SIDE_EFFECT))(shard, land, *sems, *after)
    return out[1]


def pass_halves_to_sibling(land, name):
    def body(l_in, l_ref, send_sems, recv_sems):
        x, y, c = _place()
        copies = []
        for j, (px, py) in enumerate(_other_chips(x, y)):
            q = 2 * px + py
            give = pltpu.make_async_remote_copy(src_ref=l_ref.at[q, c], dst_ref=l_ref.at[q, c], send_sem=send_sems.at[j],
                                                recv_sem=recv_sems.at[j], device_id=(x, y, 1 - c), device_id_type=MESH)
            take = pltpu.make_async_remote_copy(src_ref=l_ref.at[q, c], dst_ref=l_ref.at[q, 1 - c], send_sem=send_sems.at[j],
                                                recv_sem=recv_sems.at[j], device_id=(x, y, 1 - c), device_id_type=MESH)
            give.start()
            copies.append((give, take))
        for give, take in copies:
            take.wait_recv()
            give.wait_send()

    return pl.pallas_call(
        body, name=name, in_specs=[ANY], out_specs=ANY, out_shape=jax.ShapeDtypeStruct(land.shape, land.dtype),
        input_output_aliases={0: 0},
        scratch_shapes=[pltpu.SemaphoreType.DMA((3,)), pltpu.SemaphoreType.DMA((3,))])(land)


def allgather_devices(block, name):
    R, C = block.shape

    def body(b_ref, o_ref, send_sems, recv_sems, local_sem):
        x, y, c = _place()
        me = 4 * x + 2 * y + c
        own = pltpu.make_async_copy(b_ref, o_ref.at[me], local_sem)
        own.start()
        copies = []
        for r in range(1, 8):
            px = 1 - x if r & 4 else x
            py = 1 - y if r & 2 else y
            pc = 1 - c if r & 1 else c
            send = pltpu.make_async_remote_copy(src_ref=b_ref, dst_ref=o_ref.at[me], send_sem=send_sems.at[r - 1],
                                                recv_sem=recv_sems.at[r - 1], device_id=(px, py, pc), device_id_type=MESH)
            recv = pltpu.make_async_remote_copy(src_ref=b_ref, dst_ref=o_ref.at[4 * px + 2 * py + pc], send_sem=send_sems.at[r - 1],
                                                recv_sem=recv_sems.at[r - 1], device_id=(px, py, pc), device_id_type=MESH)
            send.start()
            copies.append((send, recv))
        for send, recv in copies:
            recv.wait_recv()
            send.wait_send()
        own.wait()

    return pl.pallas_call(
        body, name=name, in_specs=[ANY], out_specs=ANY, out_shape=jax.ShapeDtypeStruct((8, R, C), block.dtype),
        scratch_shapes=[pltpu.SemaphoreType.DMA((7,)), pltpu.SemaphoreType.DMA((7,)), pltpu.SemaphoreType.DMA(())])(block)


def swap_sibling(arrs, name, splits=None):
    n = len(arrs)

    def sent(a_ref, i, c):
        return a_ref if splits is None else a_ref.at[(slice(None),) + _half(splits[i], 1 - c)]

    def out_shape(a, i):
        if splits is None:
            return a.shape
        axis, size = splits[i]
        return (a.shape[0], size, a.shape[2]) if axis == 0 else (a.shape[0], a.shape[1], size)

    def body(*refs):
        a_refs, o_refs = refs[:n], refs[n:2 * n]
        send_sems, recv_sems = refs[2 * n:]
        x, y, c = _place()
        copies = [pltpu.make_async_remote_copy(src_ref=sent(a_ref, i, c), dst_ref=o_ref, send_sem=send_sems.at[i],
                                               recv_sem=recv_sems.at[i], device_id=(x, y, 1 - c), device_id_type=MESH)
                  for i, (a_ref, o_ref) in enumerate(zip(a_refs, o_refs))]
        for cp in copies:
            cp.start()
        for cp in copies:
            cp.wait()

    return pl.pallas_call(
        body, name=name, in_specs=[ANY] * n, out_specs=[ANY] * n,
        out_shape=[jax.ShapeDtypeStruct(out_shape(a, i), a.dtype) for i, a in enumerate(arrs)],
        scratch_shapes=[pltpu.SemaphoreType.DMA((n,)), pltpu.SemaphoreType.DMA((n,))])(*arrs)


def _exchange_copies(p_refs, l_refs, send_sems, recv_sems):
    x, y, c = _place()
    return [pltpu.make_async_remote_copy(src_ref=p_ref.at[2 * px + py], dst_ref=l_ref.at[j], send_sem=send_sems.at[3 * i + j],
                                         recv_sem=recv_sems.at[3 * i + j], device_id=(px, py, c), device_id_type=MESH)
            for i, (p_ref, l_ref) in enumerate(zip(p_refs, l_refs)) for j, (px, py) in enumerate(_other_chips(x, y))]


def exchange_chips_start(parts, name):
    n = len(parts)

    def body(*refs):
        send_sems, recv_sems = refs[2 * n:2 * n + 2]
        for cp in _exchange_copies(refs[:n], refs[n:2 * n], send_sems, recv_sems):
            cp.start()
        refs[-1][...] = jnp.zeros_like(refs[-1])

    lands = [lax.empty((3,) + p.shape[1:], p.dtype) for p in parts]
    hbm_like = lambda a: pltpu.HBM(a.shape, a.dtype)
    out = pl.pallas_call(
        body, name=name,
        out_shape=[pltpu.SemaphoreType.DMA((3 * n,)), pltpu.SemaphoreType.DMA((3 * n,))]
        + [hbm_like(p) for p in parts] + [hbm_like(l) for l in lands] + [jax.ShapeDtypeStruct((8, 128), F32)],
        in_specs=[HBM] * (2 * n), out_specs=[SEM] * 2 + [HBM] * (2 * n) + [BS(memory_space=pltpu.VMEM)],
        input_output_aliases={i: 2 + i for i in range(2 * n)},
        compiler_params=pltpu.CompilerParams(has_side_effects=_SIDE_EFFECT))(*[_IN_HBM(p) for p in parts], *[_IN_HBM(l) for l in lands])
    return out[:2], out[2:2 + n], out[2 + n:2 + 2 * n], out[-1]


def exchange_chips_wait(sems, parts, lands, after, name):
    n = len(parts)

    def body(*refs):
        send_sems, recv_sems = refs[2 * n:2 * n + 2]
        for cp in _exchange_copies(refs[:n], refs[n:2 * n], send_sems, recv_sems):
            cp.wait_send()
            cp.wait_recv()

    hbm_like = lambda a: pltpu.HBM(a.shape, a.dtype)
    out = pl.pallas_call(
        body, name=name, out_shape=[hbm_like(p) for p in parts] + [hbm_like(l) for l in lands],
        in_specs=[HBM] * (2 * n) + [SEM] * 2 + [BS(memory_space=pl.ANY)], out_specs=[HBM] * (2 * n),
        input_output_aliases={i: i for i in range(2 * n)},
        compiler_params=pltpu.CompilerParams(has_side_effects=_SIDE_EFFECT))(*parts, *lands, *sems, after)
    return out[n:]


def _half_block(shape2, split):
    axis, size = split
    return (size, shape2[1]) if axis == 0 else (shape2[0], size)


def add_pairs(parts, halves, splits, core, name):
    n = len(parts)

    def body(s_ref, *refs):
        for a_ref, b_ref, o_ref in zip(refs[:n], refs[n:2 * n], refs[2 * n:]):
            o_ref[...] = (a_ref[...].astype(F32) + b_ref[...].astype(F32)).astype(BF16)

    def mine(i):
        blk = (None,) + _half_block(parts[i].shape[1:], splits[i])
        if splits[i][0] == 0:
            return BS(blk, lambda q, s: (q, s[0], 0))
        return BS(blk, lambda q, s: (q, 0, s[0]))

    half_specs = [BS((None,) + h.shape[1:], lambda q, s: (q, 0, 0)) for h in halves]
    return pl.pallas_call(
        body, name=name,
        grid_spec=pltpu.PrefetchScalarGridSpec(num_scalar_prefetch=1, grid=(N_CHIPS,),
                                               in_specs=[mine(i) for i in range(n)] + half_specs, out_specs=half_specs),
        out_shape=[jax.ShapeDtypeStruct(h.shape, BF16) for h in halves], compiler_params=_arb(1))(core, *parts, *halves)


def add_fives(parts, halves, from_chips, splits, chip_core, name):
    n = len(parts)

    def body(s_ref, *refs):
        for a_ref, b_ref, p_ref, o_ref in zip(refs[:n], refs[n:2 * n], refs[2 * n:3 * n], refs[3 * n:]):
            s = a_ref[...].astype(F32) + b_ref[...].astype(F32)
            for j in range(3):
                s = s + p_ref[j].astype(F32)
            o_ref[...] = s

    def mine(i):
        blk = (None,) + _half_block(parts[i].shape[1:], splits[i])
        if splits[i][0] == 0:
            return BS(blk, lambda g, s: (s[0], s[1], 0))
        return BS(blk, lambda g, s: (s[0], 0, s[1]))

    half_specs = [BS((None,) + h.shape[1:], lambda g, s: (s[0], 0, 0)) for h in halves]
    chip_specs = [BS(p.shape, lambda g, s: (0, 0, 0)) for p in from_chips]
    out_specs = [BS(h.shape[1:], lambda g, s: (0, 0)) for h in halves]
    return pl.pallas_call(
        body, name=name,
        grid_spec=pltpu.PrefetchScalarGridSpec(num_scalar_prefetch=1, grid=(1,),
                                               in_specs=[mine(i) for i in range(n)] + half_specs + chip_specs, out_specs=out_specs),
        out_shape=[jax.ShapeDtypeStruct(h.shape[1:], F32) for h in halves], compiler_params=_arb(1))(chip_core, *parts, *halves, *from_chips)


def sum_leading(a, name):
    def body(a_ref, o_ref):
        s = a_ref[0]
        for j in range(1, a.shape[0]):
            s = s + a_ref[j]
        o_ref[...] = s

    return pl.pallas_call(body, name=name, out_shape=jax.ShapeDtypeStruct(a.shape[1:], a.dtype))(a)


def _adamw_math(w, g, m, v):
    mn = ADAM_B1 * m + (1.0 - ADAM_B1) * g
    vn = ADAM_B2 * v + (1.0 - ADAM_B2) * (g * g)
    m_hat = mn / (1.0 - ADAM_B1 ** ADAM_STEP)
    v_hat = vn / (1.0 - ADAM_B2 ** ADAM_STEP)
    return -ADAM_LR * (m_hat / (jnp.sqrt(v_hat) + ADAM_EPS) + ADAM_WD * w), mn, vn


def adamw(w, g, m, v, name):
    R, C = g.shape
    lead = (None,) * (w.ndim - 2)

    def body(w_ref, g_ref, m_ref, v_ref, d_ref, mo_ref, vo_ref):
        d_ref[...], mo_ref[...], vo_ref[...] = _adamw_math(w_ref[...], g_ref[...], m_ref[...], v_ref[...])

    wblk = BS(lead + (R, C), lambda i: (0,) * w.ndim)
    gblk = BS((R, C), lambda i: (0, 0))
    return pl.pallas_call(
        body, name=name, grid=(1,), in_specs=[wblk, gblk, wblk, wblk], out_specs=[wblk] * 3,
        out_shape=[jax.ShapeDtypeStruct(w.shape, F32)] * 3, compiler_params=_arb(1))(w, g, m, v)


SMALL_ROWS = 16
CONV_ROW0 = 8
LOSS_ROW = 14


def pack_small(small_grads, g_ab, g_conv, sq):
    present = [a for a in small_grads if a is not None]

    def body(*refs):
        ab_ref, conv_ref, sq_ref, o_ref = refs[len(present):]
        o_ref[...] = jnp.zeros_like(o_ref)
        it = iter(refs[:len(present)])
        for i, a in enumerate(small_grads):
            if a is not None:
                o_ref[i:i + 1, 0:a.shape[1]] = next(it)[...]
        o_ref[3:4, 0:128] = ab_ref[0:1, :]
        o_ref[4:5, 0:128] = ab_ref[1:2, :]
        half = 512
        for k in range(GDN_CONV * GDN_QKV // half):
            src_r, src_c = (k * half) // GDN_QKV, (k * half) % GDN_QKV
            dst_r, dst_c = CONV_ROW0 + (k * half) // 1024, (k * half) % 1024
            o_ref[dst_r:dst_r + 1, dst_c:dst_c + half] = conv_ref[src_r:src_r + 1, src_c:src_c + half]
        o_ref[LOSS_ROW:LOSS_ROW + 1, :] = sq_ref[...]

    return pl.pallas_call(body, name="pack_small", out_shape=jax.ShapeDtypeStruct((SMALL_ROWS, 1024), F32))(
        *present, g_ab, g_conv, sq)


def adamw_small(block, ws, ms, vs):
    k = len(ws)

    def body(b_ref, *refs):
        outs = refs[3 * k:]
        for i in range(k):
            n = ws[i].shape[1]
            g = b_ref[i:i + 1, 0:n]
            d, mn, vn = _adamw_math(refs[i][...], g, refs[k + i][...], refs[2 * k + i][...])
            outs[4 * i][...], outs[4 * i + 1][...], outs[4 * i + 2][...], outs[4 * i + 3][...] = g, d, mn, vn

    out = pl.pallas_call(
        body, name="adamw_small",
        out_shape=[jax.ShapeDtypeStruct(w.shape, F32) for w in ws for _ in range(4)])(block, *ws, *ms, *vs)
    return [out[4 * i:4 * i + 4] for i in range(k)]


def adamw_halves(w, mine, other, m, v, split, core, name):
    R, C = w.shape[-2:]
    axis, size = split
    lead = (None,) * (w.ndim - 2)
    zeros = (0,) * (w.ndim - 2)
    if axis == 0:
        tr = size if size <= 256 else next(t for t in range(256, 7, -1) if size % t == 0 and t % 8 == 0)
        nb = size // tr
        whole = BS(lead + (tr, C), lambda hi, j, s: zeros + (hi * nb + j, 0))
        part = BS((tr, C), lambda hi, j, s: (j, 0))
    else:
        nb = size // 128
        whole = BS(lead + (R, 128), lambda hi, j, s: zeros + (0, hi * nb + j))
        part = BS((R, 128), lambda hi, j, s: (0, j))

    def body(s_ref, w_ref, a_ref, b_ref, m_ref, v_ref, g_ref, d_ref, mo_ref, vo_ref):
        g = jnp.where(pl.program_id(0) == s_ref[0], a_ref[...], b_ref[...])
        g_ref[...] = g
        d_ref[...], mo_ref[...], vo_ref[...] = _adamw_math(w_ref[...], g, m_ref[...], v_ref[...])

    return pl.pallas_call(
        body, name=name,
        grid_spec=pltpu.PrefetchScalarGridSpec(num_scalar_prefetch=1, grid=(2, nb),
                                               in_specs=[whole, part, part, whole, whole], out_specs=[whole] * 4),
        out_shape=[jax.ShapeDtypeStruct(w.shape, F32)] * 4, compiler_params=_arb(2))(core, w, mine, other, m, v)


def dense_bf16(w3, name):
    R, _, K = w3.shape
    kh = K // 2

    def body(w_hbm, o_ref, buf, sem):
        cp = pltpu.make_async_copy(w_hbm.at[:, 0], buf, sem)
        cp.start()
        cp.wait()
        o_ref[0] = buf[:, :kh].astype(BF16)
        o_ref[1] = buf[:, kh:].astype(BF16)

    return pl.pallas_call(
        body, name=name, in_specs=[ANY], out_specs=BS(memory_space=pltpu.VMEM), out_shape=jax.ShapeDtypeStruct((2, R, kh), BF16),
        scratch_shapes=[pltpu.VMEM((R, K), F32), pltpu.SemaphoreType.DMA(())])(w3)


ROW_BLOCK = 184


def adamw_untiled_rows(w3, mine, other, m3, v3, name):
    R, _, K = w3.shape
    kh = K // 2
    starts = list(range(0, R, ROW_BLOCK))
    sizes = [min(ROW_BLOCK, R - s) for s in starts]
    nblk = len(starts)

    def body(w_hbm, a_ref, b_ref, m_hbm, v_hbm, g_hbm, d_hbm, mo_hbm, vo_hbm,
             wbuf, mbuf, vbuf, gbuf, dbuf, mobuf, vobuf, in_sems, out_sems):
        first = lax.axis_index("c") == 0
        ins = []
        for k, (r0, n) in enumerate(zip(starts, sizes)):
            rows = pl.ds(r0, n)
            cps = [pltpu.make_async_copy(src.at[rows, 0], dst.at[rows], in_sems.at[3 * k + i])
                   for i, (src, dst) in enumerate(((w_hbm, wbuf), (m_hbm, mbuf), (v_hbm, vbuf)))]
            for cp in cps:
                cp.start()
            ins.append(cps)

        def update(rows):
            a, b = a_ref[rows, :], b_ref[rows, :]
            g = jnp.concatenate([jnp.where(first, a, b), jnp.where(first, b, a)], axis=1)
            gbuf[rows, :] = g
            dbuf[rows, :], mobuf[rows, :], vobuf[rows, :] = _adamw_math(wbuf[rows, :], g, mbuf[rows, :], vbuf[rows, :])

        outs = []
        for k, (r0, n) in enumerate(zip(starts, sizes)):
            for cp in ins[k]:
                cp.wait()
            groups, tail = n // 8, n % 8

            def group(i, carry, r0=r0):
                update(pl.ds(pl.multiple_of(r0 + i * 8, 8), 8))
                return carry

            lax.fori_loop(0, groups, group, 0)
            if tail:
                update(pl.ds(r0 + groups * 8, tail))
            rows = pl.ds(r0, n)
            cps = [pltpu.make_async_copy(src.at[rows], dst.at[rows, 0], out_sems.at[4 * k + i])
                   for i, (src, dst) in enumerate(((gbuf, g_hbm), (dbuf, d_hbm), (mobuf, mo_hbm), (vobuf, vo_hbm)))]
            for cp in cps:
                cp.start()
            outs += cps
        for cp in outs:
            cp.wait()

    vmem = BS(memory_space=pltpu.VMEM)
    return pl.pallas_call(
        body, name=name, in_specs=[ANY, vmem, vmem, ANY, ANY], out_specs=[ANY] * 4,
        out_shape=[jax.ShapeDtypeStruct(w3.shape, F32)] * 4,
        scratch_shapes=[pltpu.VMEM((R, K), F32)] * 7 + [pltpu.SemaphoreType.DMA((3 * nblk,)), pltpu.SemaphoreType.DMA((4 * nblk,))])(
            w3, mine, other, m3, v3)


def adamw_w_q_b(w, mine, other, m, v, name):
    def body(w_ref, a_ref, b_ref, m_ref, v_ref, g_ref, d_ref, mo_ref, vo_ref):
        first = lax.axis_index("c") == 0
        lo = jnp.where(first, a_ref[...], b_ref[...])
        hi = jnp.where(first, b_ref[...], a_ref[...])
        g = jnp.concatenate([lo, hi[0:32], hi[64:96]], axis=0)
        g_ref[...] = g
        d_ref[...], mo_ref[...], vo_ref[...] = _adamw_math(w_ref[...], g, m_ref[...], v_ref[...])

    return pl.pallas_call(body, name=name, out_shape=[jax.ShapeDtypeStruct(w.shape, F32)] * 4)(w, mine, other, m, v)


def local_step(x, mem, positions, tgt, norm_in, weights, big_grads_ready, q_a_norm, kv_a_norm, gdn_conv, gdn_a_log,
               gdn_dt_bias, gdn_norm, mem_norm, norm_final):
    B, S, D = x.shape
    M = mem.shape[1]
    T = B * S
    N = S // CHUNK
    x2d = x.reshape(T, D)
    mem2d = mem.reshape(B * M, D)
    tgt2d = tgt.reshape(T, D)

    alog_row, dt_row = _lane_row(gdn_a_log), _lane_row(gdn_dt_bias)

    half = MLA_ROPE // 2
    inv_freq = 1.0 / (ROPE_THETA ** (jnp.arange(half, dtype=F32) / half))
    z32 = jnp.zeros((half,), F32)
    o32 = jnp.ones((half,), F32)
    inv_row = jnp.concatenate([inv_freq, z32, inv_freq, z32]).reshape(1, 128)
    sgn_row = jnp.concatenate([-o32, z32, o32, z32]).reshape(1, 128)
    msk_row = jnp.concatenate([o32, z32, o32, z32]).reshape(1, 128)
    cos_t, sin_t = rope_tables(positions.reshape(T, 1), inv_row, sgn_row, msk_row, after=weights[3])

    h = rms_fwd(x2d, norm_in, "rms_in", after=weights[3])
    wp, behind = weights[0]((h, cos_t))
    P = mm(h, wp, "nt", BF16, "in_proj", bm=512, bn=1536, n_outer=True, after=behind)
    P_ab = mm(h, wp[768:896], "nt", F32, "in_proj_ab")
    wq, wkv = weights[1](P)
    Q, K, V, qn, kvn = mla_prep(P, q_a_norm, kv_a_norm, wq, wkv, cos_t, sin_t)
    o_mla, lse = mla_attn_fwd(Q, K, V, B, S)
    qkv = gdn_prep_fwd(P, gdn_conv, B, S)
    GB = gdn_gate_fwd(P_ab, alog_row, dt_row, B, S)
    Grow = jnp.transpose(GB[:, :N_HEADS].reshape(B, N, CHUNK, N_HEADS), (0, 3, 1, 2))
    U, W, Tinv, A = gdn_chunk_fwd(qkv, GB, Grow, B, S)
    qkv3, GB3 = qkv.reshape(B, S, GDN_QKV), GB.reshape(B, S, 128)
    W3 = W.reshape(B, S, 512)
    o_gdn3, Vn3, St = gdn_scan_fwd(qkv3, U.reshape(B, S, 512), W3, GB3, A, B, S)
    o_gdn = o_gdn3.reshape(T, 512)
    w_mem_kv, w_out = weights[2](o_gdn)
    memn = rms_fwd(mem2d, mem_norm, "rms_mem")
    MKV = mm(memn, w_mem_kv, "nn", BF16, "mem_kv_proj")
    o_mem = mem_attn_fwd(P, MKV, B, S, M)
    mixed, dx2, dx2b, sq, g_norm_final = merge_fwd(o_mla, o_gdn, o_mem, P, x2d, tgt2d, w_out, gdn_norm, norm_final.reshape(1, D))

    g_w_out = mm(mixed, dx2b, "tn", BF16, "grad_w_out")
    dgate, do_mla, do_gdn, do_mem, g_gdn_norm = merge_bwd(dx2b, o_mla, o_gdn, o_mem, P, w_out, gdn_norm)

    dmemq, dMKV = mem_attn_bwd(P, MKV, do_mem, B, S, M)
    g_w_mem_kv = mm(memn, dMKV, "tn", BF16, "grad_w_mem_kv")
    started_early = big_grads_ready(dict(w_mem_kv=g_w_mem_kv, w_out=g_w_out), "early")
    dmemn = mm(dMKV, w_mem_kv, "nt", F32, "d_memn", after=(started_early,))
    g_mem_norm = gain_grad(mem2d, dmemn, "grad_mem_norm")

    dU3, dW3, dQ13, dK13, dA, dG13 = gdn_scan_bwd(do_gdn.reshape(B, S, 512), qkv3, W3, Vn3, GB3, A, St, B, S)
    r2 = lambda a: a.reshape(T, a.shape[-1])
    dqkv, dGB = gdn_chunk_bwd(qkv, GB, Grow, Tinv, dA, r2(dU3), r2(dW3), r2(dQ13), r2(dK13), r2(dG13), B, S)
    dPg, g_conv = gdn_prep_bwd(P, dqkv, gdn_conv, B, S)
    dab, g_ab = gdn_gate_bwd(P_ab, dGB, alog_row, dt_row, B, S)

    dQ, dK, dV = mla_attn_bwd(Q, K, V, o_mla, do_mla, lse, B, S)
    dq_lin, dkv_lin, dPm, g_q_a_norm, g_kv_a_norm = mla_proj_bwd(dQ, dK, dV, cos_t, sin_t, P, dab, wq, wkv, q_a_norm, kv_a_norm)
    g_wq = mm(dq_lin, qn, "tn", BF16, "grad_w_q_b")
    g_wkv = mm(kvn, dkv_lin, "tn", BF16, "grad_w_kv_b")

    dP = [dPm, dmemq, dPg, dgate]
    g_wp = mm_cols_tn(dP, h, BF16, "grad_w_in")
    started = big_grads_ready(dict(w_in=g_wp, w_q_b=g_wq, w_kv_b=g_wkv), "late")
    grad_x, g_norm_in = in_proj_bwd(dP, wp, x2d, dx2, norm_in, started)

    grads = dict(
        norm_in=g_norm_in, q_a_norm=g_q_a_norm, kv_a_norm=g_kv_a_norm, gdn_conv=g_conv,
        gdn_a_log_dt_bias=g_ab, gdn_norm=g_gdn_norm,
        mem_norm=g_mem_norm, norm_final=g_norm_final)
    return sq, grad_x.reshape(B, S, D), grads


def kernel(x, mem, positions, norm_in, w_in, q_a_norm, w_q_b, kv_a_norm, w_kv_b, gdn_conv, gdn_a_log, gdn_dt_bias, gdn_norm, mem_norm, w_mem_kv, w_out, norm_final, loss_target, m_norm_in, m_w_in, m_q_a_norm, m_w_q_b, m_kv_a_norm, m_w_kv_b, m_gdn_conv, m_gdn_a_log, m_gdn_dt_bias, m_gdn_norm, m_mem_norm, m_w_mem_kv, m_w_out, m_norm_final, v_norm_in, v_w_in, v_q_a_norm, v_w_q_b, v_kv_a_norm, v_w_kv_b, v_gdn_conv, v_gdn_a_log, v_gdn_dt_bias, v_gdn_norm, v_mem_norm, v_w_mem_kv, v_w_out, v_norm_final):
    B = x.shape[0]
    cx, cy, cc = lax.axis_index("x"), lax.axis_index("y"), lax.axis_index("c")
    chip = 2 * cx + cy

    big_names = ("w_in", "w_q_b", "w_kv_b", "w_mem_kv", "w_out")
    rows_major = lambda a: jnp.transpose(a, (2, 0, 1))
    w_in3, m_in3, v_in3 = rows_major(w_in), rows_major(m_w_in), rows_major(v_w_in)
    w_qb_t, m_qb_t, v_qb_t = jnp.transpose(w_q_b[0]), jnp.transpose(m_w_q_b[0]), jnp.transpose(v_w_q_b[0])
    z32 = jnp.zeros((32, Q_LORA), BF16)
    qb_bf = w_qb_t.astype(BF16)
    qb_padded = jnp.concatenate([qb_bf[:160], z32, qb_bf[160:], z32])
    shards = [dense_bf16(w_in3, "w_in_bf16"), qb_padded, w_kv_b[0].astype(BF16), w_mem_kv[0].astype(BF16), w_out[0].astype(BF16)]
    splits = [(1, D_MODEL // 2)] + [(0, s.shape[0] // 2) for s in shards[1:]]
    *w_in_flight, w_in_started = half_gather_start(shards[0], "w_in_gather_start")
    conv_all = allgather_devices(gdn_conv[0], "allgather_conv")
    conv_cols = gdn_conv.shape[2]
    conv_full = jnp.transpose(conv_all[0::2], (1, 0, 2)).reshape(GDN_CONV, N_CHIPS * conv_cols)
    late_shapes = [(N_CHIPS,) + s.shape for s in shards[1:]]
    late = {}

    def w_in_ready(after):
        g_in = pass_halves_to_sibling(half_gather_wait(*w_in_flight, after, "w_in_gather_wait"), "w_in_gather_sibling")
        *late["a"], started_a = late_gather_start(shards[1:3], g_in, "late_gather_qkv_start")
        *late["b"], started_b = late_gather_start(shards[3:], started_a, "late_gather_mem_out_start")
        return pad_w_in_t(g_in), (started_a, started_b)

    def late_qkv(after):
        g_qb, g_kvb = late_gather_wait(*late["a"], after, "late_gather_qkv_wait")
        return g_qb.reshape(-1, Q_LORA), _perm_w_kv_b(g_kvb)

    def late_mem_out(after):
        g_mem, g_out_w = late_gather_wait(*late["b"], after, "late_gather_mem_out_wait")
        return g_mem.reshape(-1, g_mem.shape[2]), g_out_w.reshape(-1, g_out_w.shape[2])

    weights = (w_in_ready, late_qkv, late_mem_out, (w_in_started,))

    core = jnp.stack([cc]).astype(jnp.int32)
    chip_core = jnp.stack([chip, cc]).astype(jnp.int32)
    exchanges = {}
    by_chip = dict(w_in=unpad_w_in_t, w_q_b=lambda a: a.reshape(late_shapes[0]), w_kv_b=_unperm_w_kv_b,
                   w_mem_kv=lambda a: a.reshape(late_shapes[2]), w_out=lambda a: a.reshape(late_shapes[3]))

    def big_grads_ready(gb, group):
        idx = [big_names.index(n) for n in gb]
        parts = [by_chip[n](a) for n, a in gb.items()]
        sp = [splits[i] for i in idx]
        from_sibling = swap_sibling(parts, "rs_sibling_partial_" + group, sp)
        chip_sums = add_pairs(parts, from_sibling, sp, core, "rs_add_sibling_" + group)
        sems, sums_thru, lands, token = exchange_chips_start(chip_sums, "rs_exchange_start_" + group)
        exchanges[group] = dict(idx=idx, parts=parts, from_sibling=from_sibling, sems=sems, sums=sums_thru, lands=lands)
        return token

    sq, grad_x, g = local_step(x, mem, positions, loss_target, norm_in, weights, big_grads_ready, q_a_norm, kv_a_norm, conv_full,
                               gdn_a_log, gdn_dt_bias, gdn_norm, mem_norm, norm_final)

    small_names = ("norm_in", "q_a_norm", "kv_a_norm", "gdn_a_log", "gdn_dt_bias", "gdn_norm", "mem_norm", "norm_final")
    small = dict(norm_in=norm_in, q_a_norm=q_a_norm, kv_a_norm=kv_a_norm, gdn_a_log=gdn_a_log, gdn_dt_bias=gdn_dt_bias,
                 gdn_norm=gdn_norm, mem_norm=mem_norm, norm_final=norm_final)
    m_small = dict(norm_in=m_norm_in, q_a_norm=m_q_a_norm, kv_a_norm=m_kv_a_norm, gdn_a_log=m_gdn_a_log,
                   gdn_dt_bias=m_gdn_dt_bias, gdn_norm=m_gdn_norm, mem_norm=m_mem_norm, norm_final=m_norm_final)
    v_small = dict(norm_in=v_norm_in, q_a_norm=v_q_a_norm, kv_a_norm=v_kv_a_norm, gdn_a_log=v_gdn_a_log,
                   gdn_dt_bias=v_gdn_dt_bias, gdn_norm=v_gdn_norm, mem_norm=v_mem_norm, norm_final=v_norm_final)
    conv_rows = GDN_CONV * GDN_QKV // 1024
    g_block = pack_small([g.get(n) for n in small_names], g["gdn_a_log_dt_bias"], g["gdn_conv"], sq)
    g_block = sum_leading(allgather_devices(g_block, "allgather_small_grads"), "sum_small_grads")
    loss = 0.5 * jnp.sum(g_block[LOSS_ROW]) / D_MODEL
    g_conv = lax.dynamic_slice_in_dim(g_block[CONV_ROW0:CONV_ROW0 + conv_rows].reshape(GDN_CONV, GDN_QKV), chip * conv_cols,
                                      conv_cols, axis=1)
    as_row = lambda a: a.reshape(1, -1)
    updated = adamw_small(g_block, [as_row(small[n]) for n in small_names], [as_row(m_small[n]) for n in small_names],
                          [as_row(v_small[n]) for n in small_names])
    g_out, d_out, m_out, v_out = ({n: u[i].reshape(small[n].shape) for n, u in zip(small_names, updated)} for i in range(4))
    d_s = d_out["norm_in"]

    my_half = [None] * len(big_names)
    for group, e in exchanges.items():
        from_chips = exchange_chips_wait(e["sems"], e["sums"], e["lands"], d_s, "rs_exchange_wait_" + group)
        halves = add_fives(e["parts"], e["from_sibling"], from_chips, [splits[i] for i in e["idx"]], chip_core, "rs_add_chips_" + group)
        for i, a in zip(e["idx"], halves):
            my_half[i] = a
    other_half = swap_sibling(my_half, "rs_sibling_final")

    d_out["gdn_conv"], m_out["gdn_conv"], v_out["gdn_conv"] = adamw(gdn_conv, g_conv, m_gdn_conv, v_gdn_conv, "adamw_gdn_conv")
    g_out["gdn_conv"] = g_conv[None]
    res = adamw_untiled_rows(w_in3, my_half[0], other_half[0], m_in3, v_in3, "adamw_w_in")
    g_out["w_in"], d_out["w_in"], m_out["w_in"], v_out["w_in"] = [jnp.transpose(r, (1, 2, 0)) for r in res]
    res = adamw_w_q_b(w_qb_t, my_half[1], other_half[1], m_qb_t, v_qb_t, "adamw_w_q_b")
    g_out["w_q_b"], d_out["w_q_b"], m_out["w_q_b"], v_out["w_q_b"] = [jnp.transpose(r)[None] for r in res]
    rest = dict(w_kv_b=(w_kv_b, m_w_kv_b, v_w_kv_b), w_mem_kv=(w_mem_kv, m_w_mem_kv, v_w_mem_kv), w_out=(w_out, m_w_out, v_w_out))
    for i, n in enumerate(big_names):
        if n in rest:
            w_n, m_n, v_n = rest[n]
            g_out[n], d_out[n], m_out[n], v_out[n] = adamw_halves(w_n, my_half[i], other_half[i], m_n, v_n, splits[i], core, "adamw_" + n)

    order = ("norm_in", "w_in", "q_a_norm", "w_q_b", "kv_a_norm", "w_kv_b", "gdn_conv", "gdn_a_log", "gdn_dt_bias",
             "gdn_norm", "mem_norm", "w_mem_kv", "w_out", "norm_final")
    return (loss, grad_x, *[g_out[n] for n in order], *[d_out[n] for n in order], *[m_out[n] for n in order],
            *[v_out[n] for n in order])
```

```python
import jax
import jax.numpy as jnp
from jax import lax
from jax.experimental import pallas as pl
from jax.experimental.pallas import tpu as pltpu

F32 = jnp.float32
BF16 = jnp.bfloat16
BS = pl.BlockSpec

D_MODEL = 1024
N_HEADS = 4
MLA_NOPE, MLA_ROPE, MLA_V = 128, 64, 128
Q_LORA, KV_LORA = 384, 256
ROPE_THETA = 10000.0
GDN_DK = GDN_DV = 128
GDN_CONV = 4
CHUNK = 64
MEM_DH = 128
D_MIX = 1536
GDN_QKV = 1536
D_IN = 4296
EPS = 1e-6
ADAM_LR, ADAM_B1, ADAM_B2, ADAM_EPS, ADAM_WD, ADAM_STEP = 0.001, 0.9, 0.999, 1e-08, 0.01, 10

OFF_MLA = 0
OFF_MEMQ = 1024
OFF_GDN = 1536
OFF_GATE = 3072
N_PAD = 4608
HEAD_PAD = 256
MLA_SCALE = (MLA_NOPE + MLA_ROPE) ** -0.5
MEM_SCALE = MEM_DH ** -0.5
GDN_SCALE = GDN_DK ** -0.5
NEG = -1e30

NN = ((1,), (0,))
NT = ((1,), (1,))
TN = ((0,), (0,))


def _dot(a, b, dims):
    return lax.dot_general(a, b, (dims, ((), ())), preferred_element_type=F32)


def _bdot(spec, a, b, precision=None):
    return jnp.einsum(spec, a, b, preferred_element_type=F32, precision=precision)


def _arb(n):
    return pltpu.CompilerParams(dimension_semantics=("arbitrary",) * n)


def _sigmoid(x):
    return 1.0 / (1.0 + jnp.exp(-x))


def _softplus(z):
    return jnp.maximum(z, 0.0) + jnp.log(1.0 + jnp.exp(-jnp.abs(z)))


def _rope(t, cos_row, sin_row):
    return t * cos_row + pltpu.roll(t, 64, 1) * sin_row


def _rope_bwd(d, cos_row, sin_row):
    return d * cos_row + pltpu.roll(d * sin_row, 64, 1)


def rms_fwd(x, gain, name, tm=512, after=()):
    T, n = x.shape
    tm = min(tm, T)

    def body(x_ref, g_ref, *rest):
        xv = x_ref[...]
        r = lax.rsqrt(jnp.mean(xv * xv, axis=-1, keepdims=True) + EPS)
        rest[-1][...] = (xv * r * g_ref[...]).astype(BF16)

    return pl.pallas_call(
        body, name=name, grid=(T // tm,),
        in_specs=[BS((tm, n), lambda i: (i, 0)), BS((1, n), lambda i: (0, 0))] + [BS(memory_space=pl.ANY)] * len(after),
        out_specs=BS((tm, n), lambda i: (i, 0)),
        out_shape=jax.ShapeDtypeStruct((T, n), BF16), compiler_params=_arb(1))(x, gain, *after)


def mm(a, b, kind, out_dtype, name, bm=512, bn=None, n_outer=False, after=()):
    if kind == "nn":
        (M, K), (_, N) = a.shape, b.shape
    elif kind == "nt":
        (M, K), (N, _) = a.shape, b.shape
    else:
        (K, M), (_, N) = a.shape, b.shape
    bm, bn = min(bm, M), min(bn or N, N)
    assert M % bm == 0 and N % bn == 0, (name, M, N, K)
    ij = (lambda g0, g1: (g1, g0)) if n_outer else (lambda g0, g1: (g0, g1))
    a_spec = BS((K, bm), lambda g0, g1: (0, ij(g0, g1)[0])) if kind == "tn" else BS((bm, K), lambda g0, g1: (ij(g0, g1)[0], 0))
    once = dict(pipeline_mode=pl.Buffered(1)) if bn == N else {}
    b_spec = (BS((bn, K), lambda g0, g1: (ij(g0, g1)[1], 0), **once) if kind == "nt"
              else BS((K, bn), lambda g0, g1: (0, ij(g0, g1)[1]), **once))
    dims = {"nn": NN, "nt": NT, "tn": TN}[kind]

    def body(a_ref, b_ref, *rest):
        rest[-1][...] = _dot(a_ref[...].astype(BF16), b_ref[...].astype(BF16), dims).astype(out_dtype)

    grid = (N // bn, M // bm) if n_outer else (M // bm, N // bn)
    return pl.pallas_call(
        body, name=name, grid=grid, in_specs=[a_spec, b_spec] + [BS(memory_space=pl.ANY)] * len(after),
        out_specs=BS((bm, bn), lambda g0, g1: ij(g0, g1)),
        out_shape=jax.ShapeDtypeStruct((M, N), out_dtype), compiler_params=_arb(2))(a, b, *after)


def mm_cols_tn(pieces, b, out_dtype, name, bm=512):
    K, N = b.shape
    tiles = [p.shape[1] // bm for p in pieces]
    firsts = [sum(tiles[:i]) for i in range(len(tiles))]

    def body(*refs):
        b_ref, o_ref = refs[-2], refs[-1]
        i = pl.program_id(0)
        for a_ref, t0, n in zip(refs[:-2], firsts, tiles):
            @pl.when((i >= t0) & (i < t0 + n))
            def _(a_ref=a_ref):
                o_ref[...] = _dot(a_ref[...], b_ref[...], TN).astype(out_dtype)

    a_specs = [BS((K, bm), lambda i, t0=t0, n=n: (0, jnp.clip(i - t0, 0, n - 1))) for t0, n in zip(firsts, tiles)]
    return pl.pallas_call(
        body, name=name, grid=(sum(tiles),),
        in_specs=a_specs + [BS(b.shape, lambda i: (0, 0), pipeline_mode=pl.Buffered(1))],
        out_specs=BS((bm, N), lambda i: (i, 0)), out_shape=jax.ShapeDtypeStruct((sum(tiles) * bm, N), out_dtype),
        compiler_params=_arb(1))(*pieces, b)


def rope_tables(pos_col, inv_row, sgn_row, msk_row, tm=512, after=()):
    T = pos_col.shape[0]
    tm = min(tm, T)

    def body(p_ref, inv_ref, sgn_ref, msk_ref, *rest):
        c_ref, s_ref = rest[-2:]
        ang = p_ref[...].astype(F32) * inv_ref[...]
        c_ref[...] = jnp.cos(ang) * msk_ref[...]
        s_ref[...] = jnp.sin(ang) * sgn_ref[...]

    row = BS((1, 128), lambda i: (0, 0))
    return pl.pallas_call(
        body, name="rope_tables", grid=(T // tm,),
        in_specs=[BS((tm, 1), lambda i: (i, 0)), row, row, row] + [BS(memory_space=pl.ANY)] * len(after),
        out_specs=[BS((tm, 128), lambda i: (i, 0))] * 2,
        out_shape=[jax.ShapeDtypeStruct((T, 128), F32)] * 2, compiler_params=_arb(1))(pos_col, inv_row, sgn_row, msk_row, *after)


def mla_prep(P, gq, gkv, wq, wkv, cos_t, sin_t, tm=512):
    T = P.shape[0]
    tm = min(tm, T)

    def body(p_ref, gq_ref, gkv_ref, wq_ref, wkv_ref, c_ref, s_ref, q_ref, k_ref, v_ref, qn_ref, kvn_ref):
        p = p_ref[...].astype(F32)
        cq, ckv, kr = p[:, :Q_LORA], p[:, Q_LORA:Q_LORA + KV_LORA], p[:, 640:768]
        qn = (cq * lax.rsqrt(jnp.mean(cq * cq, axis=-1, keepdims=True) + EPS) * gq_ref[...]).astype(BF16)
        kvn = (ckv * lax.rsqrt(jnp.mean(ckv * ckv, axis=-1, keepdims=True) + EPS) * gkv_ref[...]).astype(BF16)
        qn_ref[...] = qn
        kvn_ref[...] = kvn
        q = _dot(qn, wq_ref[...], NT)
        kv = _dot(kvn, wkv_ref[...], NN)
        cos_row, sin_row = c_ref[...], s_ref[...]
        krr = _rope(kr, cos_row, sin_row).astype(BF16)
        for h in range(N_HEADS):
            lo = h * HEAD_PAD
            q_ref[:, lo:lo + 128] = (q[:, lo:lo + 128] * MLA_SCALE).astype(BF16)
            q_ref[:, lo + 128:lo + 256] = (_rope(q[:, lo + 128:lo + 256], cos_row, sin_row) * MLA_SCALE).astype(BF16)
            k_ref[:, lo:lo + 128] = kv[:, h * 128:(h + 1) * 128].astype(BF16)
            k_ref[:, lo + 128:lo + 256] = krr
            v_ref[:, lo:lo + 128] = kv[:, 512 + h * 128:512 + (h + 1) * 128].astype(BF16)
            v_ref[:, lo + 128:lo + 256] = jnp.ones((tm, 128), BF16)

    full = lambda r, c: BS((r, c), lambda i: (0, 0))
    rowb = lambda c: BS((tm, c), lambda i: (i, 0))
    return pl.pallas_call(
        body, name="mla_prep", grid=(T // tm,),
        in_specs=[rowb(1024), full(1, Q_LORA), full(1, KV_LORA), full(1024, Q_LORA), full(KV_LORA, 1024), rowb(128), rowb(128)],
        out_specs=[rowb(1024), rowb(1024), rowb(1024), rowb(Q_LORA), rowb(KV_LORA)],
        out_shape=[jax.ShapeDtypeStruct((T, 1024), BF16), jax.ShapeDtypeStruct((T, 1024), BF16),
                   jax.ShapeDtypeStruct((T, 1024), BF16), jax.ShapeDtypeStruct((T, Q_LORA), BF16),
                   jax.ShapeDtypeStruct((T, KV_LORA), BF16)],
        compiler_params=_arb(1))(P, gq, gkv, wq, wkv, cos_t, sin_t)


ATTN_HEADS_PER_STEP = 2
ATTN_STRIP = 32


def mla_attn_fwd(Q, K, V, B, S, tq=512, hp=ATTN_HEADS_PER_STEP):
    T = B * S
    tq = min(tq, S)
    nq = S // tq

    rs = min(ATTN_STRIP, tq)

    def body(q_ref, k_ref, v_ref, o_ref, lse_ref, m_s, acc_s, s_s, p_s, a_s):
        i = pl.program_id(2)
        m_s[...] = jnp.full_like(m_s, NEG)
        acc_s[...] = jnp.zeros_like(acc_s)

        def blk(j, masked):
            rows = pl.ds(pl.multiple_of(j * tq, tq), tq)
            for h in range(hp):
                hq = slice(h * HEAD_PAD, (h + 1) * HEAD_PAD)
                s_s[h] = _dot(q_ref[:, hq], k_ref[rows, hq], NT)
            for h in range(hp):
                for r0 in range(0, tq, rs):
                    rr = slice(r0, r0 + rs)
                    sv = s_s[h, rr, :]
                    if masked:
                        r = r0 + lax.broadcasted_iota(jnp.int32, (rs, tq), 0)
                        c = lax.broadcasted_iota(jnp.int32, (rs, tq), 1)
                        sv = jnp.where(r >= c, sv, NEG)
                    m_prev = m_s[h, rr, :]
                    m_new = jnp.maximum(m_prev, jnp.max(sv, axis=1, keepdims=True))
                    p_s[h, rr, :] = jnp.exp(sv - m_new).astype(BF16)
                    a_s[h, rr, :] = jnp.exp(m_prev - m_new)
                    m_s[h, rr, :] = m_new
            for h in range(hp):
                hq = slice(h * HEAD_PAD, (h + 1) * HEAD_PAD)
                acc_s[h] = a_s[h] * acc_s[h] + _dot(p_s[h], v_ref[rows, hq], NN)

        def loop(j, c):
            blk(j, False)
            return c

        lax.fori_loop(0, i, loop, 0)
        blk(i, True)
        for h in range(hp):
            den = acc_s[h, :, 128:256]
            o_ref[:, h * 128:(h + 1) * 128] = (acc_s[h, :, 0:128] / den).astype(BF16)
            lse_ref[h] = m_s[h] + jnp.log(den[:, 0:1])

    return pl.pallas_call(
        body, name="mla_attn_fwd", grid=(B, N_HEADS // hp, nq),
        in_specs=[BS((tq, hp * HEAD_PAD), lambda b, h, i: (b * nq + i, h)),
                  BS((S, hp * HEAD_PAD), lambda b, h, i: (b, h)),
                  BS((S, hp * HEAD_PAD), lambda b, h, i: (b, h))],
        out_specs=[BS((tq, hp * 128), lambda b, h, i: (b * nq + i, h)),
                   BS((hp, tq, 1), lambda b, h, i: (h, b * nq + i, 0))],
        out_shape=[jax.ShapeDtypeStruct((T, 512), BF16), jax.ShapeDtypeStruct((N_HEADS, T, 1), F32)],
        scratch_shapes=[pltpu.VMEM((hp, tq, 1), F32), pltpu.VMEM((hp, tq, HEAD_PAD), F32), pltpu.VMEM((hp, tq, tq), F32),
                        pltpu.VMEM((hp, tq, tq), BF16), pltpu.VMEM((hp, tq, 1), F32)],
        compiler_params=_arb(3))(Q, K, V)


def mla_attn_bwd(Q, K, V, O, dO, LSE, B, S, tq=512, hp=ATTN_HEADS_PER_STEP):
    T = B * S
    tq = min(tq, S)
    nq = S // tq

    rs = min(ATTN_STRIP, tq)

    def body(q_ref, k_ref, v_ref, o_ref, do_ref, lse_ref, dq_ref, dk_ref, dv_ref, delta_s, dq_s, dk_s, dv_s, s_s, dp_s, p_s, ds_s):
        j = pl.program_id(2)

        @pl.when(j == 0)
        def _():
            dq_s[...] = jnp.zeros_like(dq_s)
            for h in range(hp):
                sl = slice(h * 128, (h + 1) * 128)
                delta_s[h] = jnp.sum(do_ref[:, sl] * o_ref[:, sl].astype(F32), axis=1, keepdims=True)

        dk_s[...] = jnp.zeros_like(dk_s)
        dv_s[...] = jnp.zeros_like(dv_s)

        def step(i, c):
            rows = pl.ds(pl.multiple_of(i * tq, tq), tq)
            for h in range(hp):
                sq, sv = slice(h * HEAD_PAD, (h + 1) * HEAD_PAD), slice(h * 128, (h + 1) * 128)
                s_s[h] = _dot(q_ref[rows, sq], k_ref[:, sq], NT)
                dp_s[h] = _dot(do_ref[rows, sv].astype(BF16), v_ref[:, h * HEAD_PAD:h * HEAD_PAD + 128], NT)
            for h in range(hp):
                for r0 in range(0, tq, rs):
                    rr = slice(r0, r0 + rs)
                    seq_rows = pl.ds(pl.multiple_of(i * tq + r0, rs), rs)
                    r = i * tq + r0 + lax.broadcasted_iota(jnp.int32, (rs, tq), 0)
                    cc = j * tq + lax.broadcasted_iota(jnp.int32, (rs, tq), 1)
                    p = jnp.where(r >= cc, jnp.exp(s_s[h, rr, :] - lse_ref[h, seq_rows, :]), 0.0)
                    p_s[h, rr, :] = p.astype(BF16)
                    ds_s[h, rr, :] = (p * (dp_s[h, rr, :] - delta_s[h, seq_rows, :])).astype(BF16)
            for h in range(hp):
                sq, sv = slice(h * HEAD_PAD, (h + 1) * HEAD_PAD), slice(h * 128, (h + 1) * 128)
                dv_s[:, sv] += _dot(p_s[h], do_ref[rows, sv].astype(BF16), TN)
                dk_s[:, sq] += _dot(ds_s[h], q_ref[rows, sq], TN)
                dq_s[rows, sq] += _dot(ds_s[h], k_ref[:, sq], NN)
            return c

        lax.fori_loop(j, nq, step, 0)
        dk_ref[...] = dk_s[...].astype(BF16)
        dv_ref[...] = dv_s[...].astype(BF16)

        @pl.when(j == nq - 1)
        def _():
            dq_ref[...] = dq_s[...].astype(BF16)

    seq = lambda c: BS((S, c), lambda b, h, j: (b, h))
    blk = lambda c: BS((tq, c), lambda b, h, j: (b * nq + j, h))
    return pl.pallas_call(
        body, name="mla_attn_bwd", grid=(B, N_HEADS // hp, nq),
        in_specs=[seq(hp * HEAD_PAD), blk(hp * HEAD_PAD), blk(hp * HEAD_PAD), seq(hp * 128), seq(hp * 128),
                  BS((hp, S, 1), lambda b, h, j: (h, b, 0))],
        out_specs=[seq(hp * HEAD_PAD), blk(hp * HEAD_PAD), blk(hp * 128)],
        out_shape=[jax.ShapeDtypeStruct((T, 1024), BF16), jax.ShapeDtypeStruct((T, 1024), BF16),
                   jax.ShapeDtypeStruct((T, 512), BF16)],
        scratch_shapes=[pltpu.VMEM((hp, S, 1), F32), pltpu.VMEM((S, hp * HEAD_PAD), F32),
                        pltpu.VMEM((tq, hp * HEAD_PAD), F32), pltpu.VMEM((tq, hp * 128), F32),
                        pltpu.VMEM((hp, tq, tq), F32), pltpu.VMEM((hp, tq, tq), F32),
                        pltpu.VMEM((hp, tq, tq), BF16), pltpu.VMEM((hp, tq, tq), BF16)],
        compiler_params=_arb(3))(Q, K, V, O, dO, LSE)


def mla_proj_bwd(dQ, dK, dV, cos_t, sin_t, P, dab, wq, wkv, gq, gkv, tm=512):
    T = P.shape[0]
    tm = min(tm, T)

    def norm_bwd(x, dy, g):
        r = lax.rsqrt(jnp.mean(x * x, axis=-1, keepdims=True) + EPS)
        xh = x * r
        dxh = dy * g
        return r * (dxh - xh * jnp.mean(dxh * xh, axis=-1, keepdims=True)), jnp.sum(dy * xh, axis=0, keepdims=True)

    def body(dq_ref, dk_ref, dv_ref, c_ref, s_ref, p_ref, dab_ref, wq_ref, wkv_ref, gq_ref, gkv_ref,
             ql_ref, kvl_ref, o_ref, aq_ref, akv_ref):
        @pl.when(pl.program_id(0) == 0)
        def _():
            aq_ref[...] = jnp.zeros_like(aq_ref)
            akv_ref[...] = jnp.zeros_like(akv_ref)

        cos_row, sin_row = c_ref[...], s_ref[...]
        kr = jnp.zeros((tm, 128), F32)
        for h in range(N_HEADS):
            lo = h * HEAD_PAD
            ql_ref[:, lo:lo + 128] = (dq_ref[:, lo:lo + 128].astype(F32) * MLA_SCALE).astype(BF16)
            ql_ref[:, lo + 128:lo + 256] = (_rope_bwd(dq_ref[:, lo + 128:lo + 256].astype(F32), cos_row, sin_row) * MLA_SCALE).astype(BF16)
            kvl_ref[:, h * 128:(h + 1) * 128] = dk_ref[:, lo:lo + 128]
            kr = kr + dk_ref[:, lo + 128:lo + 256].astype(F32)
        kvl_ref[:, 512:] = dv_ref[...]
        dqn = _dot(ql_ref[...], wq_ref[...], NN)
        dkvn = _dot(kvl_ref[...], wkv_ref[...], NT)
        dcq, ggq = norm_bwd(p_ref[:, :Q_LORA].astype(F32), dqn, gq_ref[...])
        dckv, ggkv = norm_bwd(p_ref[:, Q_LORA:640].astype(F32), dkvn, gkv_ref[...])
        aq_ref[...] += ggq
        akv_ref[...] += ggkv
        o_ref[:, :Q_LORA] = dcq.astype(BF16)
        o_ref[:, Q_LORA:640] = dckv.astype(BF16)
        o_ref[:, 640:768] = _rope_bwd(kr, cos_row, sin_row).astype(BF16)
        o_ref[:, 768:896] = dab_ref[...]
        o_ref[:, 896:1024] = jnp.zeros((tm, 128), BF16)

    rowb = lambda c: BS((tm, c), lambda i: (i, 0))
    full = lambda r, c: BS((r, c), lambda i: (0, 0))
    return pl.pallas_call(
        body, name="mla_proj_bwd", grid=(T // tm,),
        in_specs=[rowb(1024), rowb(1024), rowb(512), rowb(128), rowb(128), rowb(1024), rowb(128),
                  full(1024, Q_LORA), full(KV_LORA, 1024), full(1, Q_LORA), full(1, KV_LORA)],
        out_specs=[rowb(1024), rowb(1024), rowb(1024), full(1, Q_LORA), full(1, KV_LORA)],
        out_shape=[jax.ShapeDtypeStruct((T, 1024), BF16)] * 3
        + [jax.ShapeDtypeStruct((1, Q_LORA), F32), jax.ShapeDtypeStruct((1, KV_LORA), F32)],
        compiler_params=_arb(1))(dQ, dK, dV, cos_t, sin_t, P, dab, wq, wkv, gq, gkv)


def _mem_probs(qh, kh):
    s = _dot(qh, kh, NT) * MEM_SCALE
    p = jnp.exp(s - jnp.max(s, axis=1, keepdims=True))
    return p / jnp.sum(p, axis=1, keepdims=True)


def mem_attn_fwd(P, MKV, B, S, M, tq=512):
    T = B * S
    tq = min(tq, S)
    nq = S // tq

    def body(q_ref, kv_ref, o_ref):
        for h in range(N_HEADS):
            sl = slice(h * 128, (h + 1) * 128)
            p = _mem_probs(q_ref[:, sl].astype(BF16), kv_ref[:, sl])
            o_ref[:, sl] = _dot(p.astype(BF16), kv_ref[:, 512 + h * 128:512 + (h + 1) * 128], NN).astype(BF16)

    return pl.pallas_call(
        body, name="mem_attn_fwd", grid=(B, nq),
        in_specs=[BS((tq, 512), lambda b, i: (b * nq + i, OFF_MEMQ // 512)), BS((M, 1024), lambda b, i: (b, 0))],
        out_specs=BS((tq, 512), lambda b, i: (b * nq + i, 0)),
        out_shape=jax.ShapeDtypeStruct((T, 512), BF16), compiler_params=_arb(2))(P, MKV)


def mem_attn_bwd(P, MKV, dO, B, S, M, tq=512):
    T = B * S
    tq = min(tq, S)
    nq = S // tq

    def body(q_ref, kv_ref, do_ref, dq_ref, dkv_ref):
        @pl.when(pl.program_id(1) == 0)
        def _():
            dkv_ref[...] = jnp.zeros_like(dkv_ref)

        for h in range(N_HEADS):
            sl = slice(h * 128, (h + 1) * 128)
            sv = slice(512 + h * 128, 512 + (h + 1) * 128)
            qh = q_ref[:, sl].astype(BF16)
            kh = kv_ref[:, sl]
            do = do_ref[:, sl].astype(BF16)
            p = _mem_probs(qh, kh)
            dkv_ref[:, sv] += _dot(p.astype(BF16), do, TN)
            dp = _dot(do, kv_ref[:, sv], NT)
            ds = (p * (dp - jnp.sum(dp * p, axis=1, keepdims=True)) * MEM_SCALE).astype(BF16)
            dq_ref[:, sl] = _dot(ds, kh, NN).astype(BF16)
            dkv_ref[:, sl] += _dot(ds, qh, TN)

    return pl.pallas_call(
        body, name="mem_attn_bwd", grid=(B, nq),
        in_specs=[BS((tq, 512), lambda b, i: (b * nq + i, OFF_MEMQ // 512)), BS((M, 1024), lambda b, i: (b, 0)),
                  BS((tq, 512), lambda b, i: (b * nq + i, 0))],
        out_specs=[BS((tq, 512), lambda b, i: (b * nq + i, 0)), BS((M, 1024), lambda b, i: (b, 0))],
        out_shape=[jax.ShapeDtypeStruct((T, 512), BF16), jax.ShapeDtypeStruct((B * M, 1024), F32)],
        compiler_params=_arb(2))(P, MKV, dO)


def gain_grad(x, dy, name, tm=256):
    T, n = x.shape
    tm = min(tm, T)

    def body(x_ref, dy_ref, o_ref):
        @pl.when(pl.program_id(0) == 0)
        def _():
            o_ref[...] = jnp.zeros_like(o_ref)

        xv = x_ref[...]
        xh = xv * lax.rsqrt(jnp.mean(xv * xv, axis=-1, keepdims=True) + EPS)
        o_ref[...] += jnp.sum(dy_ref[...] * xh, axis=0, keepdims=True)

    return pl.pallas_call(
        body, name=name, grid=(T // tm,),
        in_specs=[BS((tm, n), lambda i: (i, 0))] * 2, out_specs=BS((1, n), lambda i: (0, 0)),
        out_shape=jax.ShapeDtypeStruct((1, n), F32), compiler_params=_arb(1))(x, dy)


def _conv_silu(x, w, t):
    y = x * w[3:4, :]
    for s in range(1, GDN_CONV):
        y = y + jnp.where(t >= s, pltpu.roll(x, s, 0), 0.0) * w[3 - s:4 - s, :]
    return y, _sigmoid(y)


def gdn_prep_fwd(P, conv_w, B, S):
    T = B * S

    def body(x_ref, w_ref, o_ref):
        kind = pl.program_id(1)
        t = lax.broadcasted_iota(jnp.int32, (S, 1), 0)
        y, sg = _conv_silu(x_ref[...].astype(F32), w_ref[...], t)
        a = y * sg
        scale = jnp.where(kind == 0, GDN_SCALE, 1.0).astype(F32)
        for h in range(N_HEADS):
            sl = slice(h * 128, (h + 1) * 128)
            seg = a[:, sl]
            n = lax.rsqrt(jnp.sum(seg * seg, axis=-1, keepdims=True) + EPS)
            o_ref[:, sl] = jnp.where(kind < 2, seg * (n * scale), seg)

    return pl.pallas_call(
        body, name="gdn_prep_fwd", grid=(B, 3),
        in_specs=[BS((S, 512), lambda b, k: (b, OFF_GDN // 512 + k)), BS((GDN_CONV, 512), lambda b, k: (0, k))],
        out_specs=BS((S, 512), lambda b, k: (b, k)),
        out_shape=jax.ShapeDtypeStruct((T, GDN_QKV), F32), compiler_params=_arb(2))(P, conv_w)


def gdn_prep_bwd(P, dqkv, conv_w, B, S):
    T = B * S

    def body(x_ref, d_ref, w_ref, o_ref, gw_ref):
        kind = pl.program_id(0)

        @pl.when(pl.program_id(1) == 0)
        def _():
            gw_ref[...] = jnp.zeros_like(gw_ref)

        t = lax.broadcasted_iota(jnp.int32, (S, 1), 0)
        x = x_ref[...].astype(F32)
        w = w_ref[...]
        y, sg = _conv_silu(x, w, t)
        a = y * sg
        scale = jnp.where(kind == 0, GDN_SCALE, 1.0).astype(F32)
        das = []
        for h in range(N_HEADS):
            sl = slice(h * 128, (h + 1) * 128)
            seg, dseg = a[:, sl], d_ref[:, sl]
            n = lax.rsqrt(jnp.sum(seg * seg, axis=-1, keepdims=True) + EPS)
            dn = scale * (n * dseg - seg * (n * n * n) * jnp.sum(dseg * seg, axis=-1, keepdims=True))
            das.append(jnp.where(kind < 2, dn, dseg))
        dy = jnp.concatenate(das, axis=1) * (sg * (1.0 + y * (1.0 - sg)))
        dx = dy * w[3:4, :]
        gw_ref[3:4, :] += jnp.sum(dy * x, axis=0, keepdims=True)
        for s in range(1, GDN_CONV):
            dx = dx + jnp.where(t + s < S, pltpu.roll(dy, S - s, 0), 0.0) * w[3 - s:4 - s, :]
            gw_ref[3 - s:4 - s, :] += jnp.sum(dy * jnp.where(t >= s, pltpu.roll(x, s, 0), 0.0), axis=0, keepdims=True)
        o_ref[...] = dx.astype(BF16)

    return pl.pallas_call(
        body, name="gdn_prep_bwd", grid=(3, B),
        in_specs=[BS((S, 512), lambda k, b: (b, OFF_GDN // 512 + k)), BS((S, 512), lambda k, b: (b, k)),
                  BS((GDN_CONV, 512), lambda k, b: (0, k))],
        out_specs=[BS((S, 512), lambda k, b: (b, k)), BS((GDN_CONV, 512), lambda k, b: (0, k))],
        out_shape=[jax.ShapeDtypeStruct((T, GDN_QKV), BF16), jax.ShapeDtypeStruct((GDN_CONV, GDN_QKV), F32)],
        compiler_params=_arb(2))(P, dqkv, conv_w)


def _chunk_row(n_rows):
    return lax.broadcasted_iota(jnp.int32, (n_rows, 1), 0) % CHUNK


def gdn_gate_fwd(P, alog_row, dt_row, B, S):
    T = B * S

    def body(x_ref, al_ref, dt_ref, o_ref):
        x = x_ref[...].astype(F32)
        lane = lax.broadcasted_iota(jnp.int32, (1, 128), 1)
        g = jnp.where(lane < 4, -jnp.exp(al_ref[...]) * _softplus(x + dt_ref[...]), 0.0)
        t = _chunk_row(S)
        for s in (1, 2, 4, 8, 16, 32):
            g = g + jnp.where(t >= s, pltpu.roll(g, s, 0), 0.0)
        o_ref[...] = jnp.where(lane < 4, g, jnp.where(lane < 8, _sigmoid(x), 0.0))

    row = BS((1, 128), lambda b: (0, 0))
    return pl.pallas_call(
        body, name="gdn_gate_fwd", grid=(B,),
        in_specs=[BS((S, 128), lambda b: (b, 768 // 128)), row, row], out_specs=BS((S, 128), lambda b: (b, 0)),
        out_shape=jax.ShapeDtypeStruct((T, 128), F32), compiler_params=_arb(1))(P, alog_row, dt_row)


def gdn_gate_bwd(P, dGB, alog_row, dt_row, B, S):
    T = B * S

    def body(x_ref, d_ref, al_ref, dt_ref, o_ref, acc_ref):
        @pl.when(pl.program_id(0) == 0)
        def _():
            acc_ref[...] = jnp.zeros_like(acc_ref)

        x, d = x_ref[...].astype(F32), d_ref[...]
        lane = lax.broadcasted_iota(jnp.int32, (1, 128), 1)
        z = x + dt_ref[...]
        coef = -jnp.exp(al_ref[...])
        g = coef * _softplus(z)
        da = jnp.where(lane < 4, d * coef * _sigmoid(z), 0.0)
        beta = _sigmoid(x)
        o_ref[...] = jnp.where(lane < 4, da, jnp.where(lane < 8, d * beta * (1.0 - beta), 0.0)).astype(BF16)
        acc_ref[0:1, :] += jnp.sum(jnp.where(lane < 4, d * g, 0.0), axis=0, keepdims=True)
        acc_ref[1:2, :] += jnp.sum(da, axis=0, keepdims=True)

    row = BS((1, 128), lambda b: (0, 0))
    return pl.pallas_call(
        body, name="gdn_gate_bwd", grid=(B,),
        in_specs=[BS((S, 128), lambda b: (b, 768 // 128)), BS((S, 128), lambda b: (b, 0)), row, row],
        out_specs=[BS((S, 128), lambda b: (b, 0)), BS((8, 128), lambda b: (0, 0))],
        out_shape=[jax.ShapeDtypeStruct((T, 128), BF16), jax.ShapeDtypeStruct((8, 128), F32)],
        compiler_params=_arb(1))(P, dGB, alog_row, dt_row)


def _chunk_masks(nc):
    r = lax.broadcasted_iota(jnp.int32, (nc, CHUNK, CHUNK), 1)
    c = lax.broadcasted_iota(jnp.int32, (nc, CHUNK, CHUNK), 2)
    return r >= c, r > c


def _chunk_local(q, k, gc, gr, beta, incl, strict):
    decay = jnp.exp(jnp.where(incl, gc - gr, NEG))
    kb = k * beta
    kbf = k.astype(BF16)
    m_kk = _bdot("gcd,gjd->gcj", kb.astype(BF16), kbf)
    l_mat = jnp.where(strict, m_kk * decay, 0.0)
    a_mat = _bdot("gcd,gjd->gcj", q.astype(BF16), kbf) * decay
    return decay, kb, l_mat, a_mat


WY_SPLIT_LEVELS = 2


def _split_bf16(x):
    hi = x.astype(BF16)
    return hi, (x - hi.astype(F32)).astype(BF16)


def _mm_split(ah, al, bh, bl):
    spec = "gij,gjk->gik"
    return _bdot(spec, ah, bh) + (_bdot(spec, ah, bl) + _bdot(spec, al, bh))


def gdn_chunk_fwd(qkv, GB, Grow, B, S, nc=8):
    T = B * S
    N = S // CHUNK
    nc = min(nc, N)
    nb = N // nc
    R = nc * CHUNK

    def body(q_ref, k_ref, v_ref, gb_ref, gr_ref, u_ref, w_ref, t_ref, a_ref):
        incl, strict = _chunk_masks(nc)
        eye = (lax.broadcasted_iota(jnp.int32, (nc, CHUNK, CHUNK), 1)
               == lax.broadcasted_iota(jnp.int32, (nc, CHUNK, CHUNK), 2)).astype(F32)
        for h in range(N_HEADS):
            sl = slice(h * 128, (h + 1) * 128)
            q = q_ref[:, sl].reshape(nc, CHUNK, 128)
            k = k_ref[:, sl].reshape(nc, CHUNK, 128)
            v = v_ref[:, sl].reshape(nc, CHUNK, 128)
            gc = gb_ref[:, h:h + 1].reshape(nc, CHUNK, 1)
            beta = gb_ref[:, 4 + h:5 + h].reshape(nc, CHUNK, 1)
            gr = gr_ref[h][:, None, :]
            _, kb, l_mat, a_mat = _chunk_local(q, k, gc, gr, beta, incl, strict)
            pw = -l_mat
            tinv = eye + pw
            for level in range(5):
                if level < WY_SPLIT_LEVELS:
                    ph, pl_ = _split_bf16(pw)
                    pw = _mm_split(ph, pl_, ph, pl_)
                    ph, pl_ = _split_bf16(pw)
                    th, tl = _split_bf16(tinv)
                    tinv = tinv + _mm_split(th, tl, ph, pl_)
                else:
                    ph = pw.astype(BF16)
                    pw = _bdot("gij,gjk->gik", ph, ph)
                    tinv = tinv + _bdot("gij,gjk->gik", tinv.astype(BF16), pw.astype(BF16))
            tb = tinv.astype(BF16)
            u = _bdot("gcj,gjv->gcv", tb, (v * beta).astype(BF16))
            w = _bdot("gcj,gjk->gck", tb, (kb * jnp.exp(gc)).astype(BF16))
            u_ref[:, sl] = u.reshape(R, 128)
            w_ref[:, sl] = w.reshape(R, 128).astype(BF16)
            t_ref[h] = jnp.swapaxes(tinv, 1, 2).astype(BF16)
            a_ref[h] = a_mat.astype(BF16)

    rowb = lambda c, j: BS((R, c), lambda b, n: (b * nb + n, j))
    mat = BS((None, N_HEADS, nc, CHUNK, CHUNK), lambda b, n: (b, 0, n, 0, 0))
    return pl.pallas_call(
        body, name="gdn_chunk_fwd", grid=(B, nb),
        in_specs=[rowb(512, 0), rowb(512, 1), rowb(512, 2), rowb(128, 0),
                  BS((None, N_HEADS, nc, CHUNK), lambda b, n: (b, 0, n, 0))],
        out_specs=[rowb(512, 0), rowb(512, 0), mat, mat],
        out_shape=[jax.ShapeDtypeStruct((T, 512), F32), jax.ShapeDtypeStruct((T, 512), BF16),
                   jax.ShapeDtypeStruct((B, N_HEADS, N, CHUNK, CHUNK), BF16),
                   jax.ShapeDtypeStruct((B, N_HEADS, N, CHUNK, CHUNK), BF16)],
        compiler_params=_arb(2))(qkv, qkv, qkv, GB, Grow)


SCAN_CHUNKS = 4


def gdn_scan_fwd(qkv3, U3, W3, GB3, A, B, S):
    N = S // CHUNK
    cps = SCAN_CHUNKS if N % SCAN_CHUNKS == 0 else 1

    def body(q_ref, k_ref, u_ref, w_ref, gb_ref, a_ref, o_ref, vn_ref, st_ref, s_s):
        @pl.when(pl.program_id(0) == 0)
        def _():
            s_s[...] = jnp.zeros_like(s_s)

        for c in range(cps):
            rows = slice(c * CHUNK, (c + 1) * CHUNK)
            for b in range(B):
                for h in range(N_HEADS):
                    sl = slice(h * 128, (h + 1) * 128)
                    st = s_s[b, h]
                    st_ref[b, h, c] = st
                    stb = st.astype(BF16)
                    g = gb_ref[b, rows, h:h + 1]
                    gl = g[CHUNK - 1:CHUNK, :]
                    qg = (q_ref[b, rows, sl] * jnp.exp(g)).astype(BF16)
                    on_state = _dot(jnp.concatenate([w_ref[b, rows, sl].astype(BF16), qg], axis=0), stb, NN)
                    vn = u_ref[b, rows, sl] - on_state[:CHUNK]
                    vnb = vn.astype(BF16)
                    kd_t = jnp.transpose(k_ref[b, rows, sl] * jnp.exp(gl - g)).astype(BF16)
                    on_vn = _dot(jnp.concatenate([a_ref[b, h, c].astype(BF16), kd_t], axis=0), vnb, NN)
                    vn_ref[b, rows, sl] = vnb
                    o_ref[b, rows, sl] = (on_state[CHUNK:] + on_vn[:CHUNK]).astype(BF16)
                    s_s[b, h] = st * jnp.exp(gl) + on_vn[CHUNK:]

    tok = lambda c, j: BS((B, cps * CHUNK, c), lambda n: (0, n, j))
    return pl.pallas_call(
        body, name="gdn_scan_fwd", grid=(N // cps,),
        in_specs=[tok(512, 0), tok(512, 1), tok(512, 0), tok(512, 0), tok(128, 0),
                  BS((B, N_HEADS, cps, CHUNK, CHUNK), lambda n: (0, 0, n, 0, 0))],
        out_specs=[tok(512, 0), tok(512, 0), BS((B, N_HEADS, cps, 128, 128), lambda n: (0, 0, n, 0, 0))],
        out_shape=[jax.ShapeDtypeStruct((B, S, 512), BF16), jax.ShapeDtypeStruct((B, S, 512), BF16),
                   jax.ShapeDtypeStruct((B, N_HEADS, N, 128, 128), F32)],
        scratch_shapes=[pltpu.VMEM((B, N_HEADS, 128, 128), F32)],
        compiler_params=_arb(1))(qkv3, qkv3, U3, W3, GB3, A)


def gdn_scan_bwd(dO3, qkv3, W3, Vn3, GB3, A, St, B, S):
    N = S // CHUNK
    cps = SCAN_CHUNKS if N % SCAN_CHUNKS == 0 else 1

    def body(do_ref, q_ref, k_ref, w_ref, vn_ref, gb_ref, a_ref, st_ref,
             du_ref, dw_ref, dq_ref, dk_ref, da_ref, dg_ref, ds_s):
        @pl.when(pl.program_id(0) == 0)
        def _():
            ds_s[...] = jnp.zeros_like(ds_s)

        lane = lax.broadcasted_iota(jnp.int32, (1, 128), 1)
        last = lax.broadcasted_iota(jnp.int32, (CHUNK, 1), 0) == CHUNK - 1
        for c in reversed(range(cps)):
            rows = slice(c * CHUNK, (c + 1) * CHUNK)
            for b in range(B):
                dg_all = jnp.zeros((CHUNK, 128), F32)
                for h in range(N_HEADS):
                    sl = slice(h * 128, (h + 1) * 128)
                    st = st_ref[b, h, c]
                    stb = st.astype(BF16)
                    dsn = ds_s[b, h]
                    dsnb = dsn.astype(BF16)
                    g = gb_ref[b, rows, h:h + 1]
                    gl = g[CHUNK - 1:CHUNK, :]
                    egl = jnp.exp(gl)
                    ekd = jnp.exp(gl - g)
                    eg = jnp.exp(g)
                    q, k = q_ref[b, rows, sl], k_ref[b, rows, sl]
                    kd = k * ekd
                    qg = q * eg
                    do = do_ref[b, rows, sl].astype(BF16)
                    vnb = vn_ref[b, rows, sl].astype(BF16)
                    dvn = _dot(a_ref[b, h, c].astype(BF16), do, TN) + _dot(kd.astype(BF16), dsnb, NN)
                    dvnb = dvn.astype(BF16)
                    do_on = _dot(do, jnp.concatenate([stb, vnb], axis=0), NT)
                    dqg = do_on[:, :128]
                    da_ref[b, h, c] = do_on[:, 128:]
                    dkd = _dot(vnb, dsnb, NT)
                    ds_s[b, h] = (_dot(qg.astype(BF16), do, TN) + egl * dsn - _dot(w_ref[b, rows, sl].astype(BF16), dvnb, TN))
                    du_ref[b, rows, sl] = dvnb
                    dw_ref[b, rows, sl] = (-_dot(dvnb, stb, NT)).astype(BF16)
                    dq_ref[b, rows, sl] = dqg * eg
                    dk_ref[b, rows, sl] = dkd * ekd
                    ddel = jnp.sum(dkd * kd, axis=1, keepdims=True)
                    dgl = jnp.sum(ddel, axis=0, keepdims=True) + jnp.sum(jnp.sum(st * dsn, axis=1, keepdims=True), axis=0, keepdims=True) * egl
                    col = jnp.sum(dqg * qg, axis=1, keepdims=True) - ddel + jnp.where(last, dgl, 0.0)
                    dg_all = jnp.where(lane == h, col, dg_all)
                dg_ref[b, rows, :] = dg_all

    steps = N // cps
    tok = lambda c, j: BS((B, cps * CHUNK, c), lambda n: (0, steps - 1 - n, j))
    mat = lambda d: BS((B, N_HEADS, cps, d, d), lambda n: (0, 0, steps - 1 - n, 0, 0))
    return pl.pallas_call(
        body, name="gdn_scan_bwd", grid=(steps,),
        in_specs=[tok(512, 0), tok(512, 0), tok(512, 1), tok(512, 0), tok(512, 0), tok(128, 0), mat(CHUNK), mat(128)],
        out_specs=[tok(512, 0), tok(512, 0), tok(512, 0), tok(512, 0), mat(CHUNK), tok(128, 0)],
        out_shape=[jax.ShapeDtypeStruct((B, S, 512), BF16)] * 2 + [jax.ShapeDtypeStruct((B, S, 512), F32)] * 2
        + [jax.ShapeDtypeStruct((B, N_HEADS, N, CHUNK, CHUNK), F32), jax.ShapeDtypeStruct((B, S, 128), F32)],
        scratch_shapes=[pltpu.VMEM((B, N_HEADS, 128, 128), F32)],
        compiler_params=_arb(1))(dO3, qkv3, qkv3, W3, Vn3, GB3, A, St)


def gdn_chunk_bwd(qkv, GB, Grow, Tinv, dA, dU, dW, dQ1, dK1, dG1, B, S, nc=8):
    T = B * S
    N = S // CHUNK
    nc = min(nc, N)
    nb = N // nc
    R = nc * CHUNK

    def body(q_ref, k_ref, v_ref, gb_ref, gr_ref, t_ref, da_ref, du_ref, dw_ref, dq1_ref, dk1_ref, dg1_ref, o_ref, dgb_ref):
        incl, strict = _chunk_masks(nc)
        lane = lax.broadcasted_iota(jnp.int32, (1, 128), 1)
        dg_all = dg1_ref[...]
        db_all = jnp.zeros((R, 128), F32)
        for h in range(N_HEADS):
            sl = slice(h * 128, (h + 1) * 128)
            q = q_ref[:, sl].reshape(nc, CHUNK, 128)
            k = k_ref[:, sl].reshape(nc, CHUNK, 128)
            v = v_ref[:, sl].reshape(nc, CHUNK, 128)
            gc = gb_ref[:, h:h + 1].reshape(nc, CHUNK, 1)
            beta = gb_ref[:, 4 + h:5 + h].reshape(nc, CHUNK, 1)
            gr = gr_ref[h][:, None, :]
            decay, kb, l_mat, a_mat = _chunk_local(q, k, gc, gr, beta, incl, strict)
            eg = jnp.exp(gc)
            kbg = kb * eg
            vb = v * beta
            tt = t_ref[h].astype(BF16)
            du = du_ref[:, sl].reshape(nc, CHUNK, 128).astype(BF16)
            dw = dw_ref[:, sl].reshape(nc, CHUNK, 128).astype(BF16)
            dvb = _bdot("gjc,gcv->gjv", tt, du)
            dkbg = _bdot("gjc,gck->gjk", tt, dw)
            dt = _bdot("gcv,gjv->gcj", du, vb.astype(BF16)) + _bdot("gck,gjk->gcj", dw, kbg.astype(BF16))
            tmp = _bdot("gca,gab->gcb", tt, dt.astype(BF16))
            dl = jnp.where(strict, -_bdot("gcb,gbd->gcd", tmp.astype(BF16), tt), 0.0)
            da = da_ref[h]
            dm = (dl * decay).astype(BF16)
            dqk = (da * decay).astype(BF16)
            kbf = k.astype(BF16)
            dkb = _bdot("gcj,gjd->gcd", dm, kbf) + dkbg * eg
            dk = (_bdot("gcj,gcd->gjd", dm, kb.astype(BF16)) + _bdot("gcj,gcd->gjd", dqk, q.astype(BF16))
                  + dk1_ref[:, sl].reshape(nc, CHUNK, 128) + dkb * beta)
            dq = _bdot("gcj,gjd->gcd", dqk, kbf) + dq1_ref[:, sl].reshape(nc, CHUNK, 128)
            e = dl * l_mat + da * a_mat
            dgc = (jnp.sum(e, axis=2, keepdims=True) - jnp.sum(jnp.swapaxes(e, 1, 2), axis=2, keepdims=True)
                   + jnp.sum(dkbg * kbg, axis=2, keepdims=True))
            dbeta = jnp.sum(dkb * k, axis=2, keepdims=True) + jnp.sum(dvb * v, axis=2, keepdims=True)
            o_ref[:, sl] = dq.reshape(R, 128)
            o_ref[:, 512 + h * 128:512 + (h + 1) * 128] = dk.reshape(R, 128)
            o_ref[:, 1024 + h * 128:1024 + (h + 1) * 128] = (dvb * beta).reshape(R, 128)
            dg_all = dg_all + jnp.where(lane == h, dgc.reshape(R, 1), 0.0)
            db_all = jnp.where(lane == 4 + h, dbeta.reshape(R, 1), db_all)
        t = _chunk_row(R)
        for s in (1, 2, 4, 8, 16, 32):
            dg_all = dg_all + jnp.where(t + s < CHUNK, pltpu.roll(dg_all, R - s, 0), 0.0)
        dgb_ref[...] = jnp.where(lane < 4, dg_all, db_all)

    rowb = lambda c, j: BS((R, c), lambda b, n: (b * nb + n, j))
    mat = BS((None, N_HEADS, nc, CHUNK, CHUNK), lambda b, n: (b, 0, n, 0, 0))
    return pl.pallas_call(
        body, name="gdn_chunk_bwd", grid=(B, nb),
        in_specs=[rowb(512, 0), rowb(512, 1), rowb(512, 2), rowb(128, 0),
                  BS((None, N_HEADS, nc, CHUNK), lambda b, n: (b, 0, n, 0)), mat, mat,
                  rowb(512, 0), rowb(512, 0), rowb(512, 0), rowb(512, 0), rowb(128, 0)],
        out_specs=[rowb(GDN_QKV, 0), rowb(128, 0)],
        out_shape=[jax.ShapeDtypeStruct((T, GDN_QKV), F32), jax.ShapeDtypeStruct((T, 128), F32)],
        compiler_params=_arb(2))(qkv, qkv, qkv, GB, Grow, Tinv, dA, dU, dW, dQ1, dK1, dG1)


def _gdn_out_norm(og, gg):
    outs, xhs, rs = [], [], []
    for h in range(N_HEADS):
        seg = og[:, h * 128:(h + 1) * 128]
        r = lax.rsqrt(jnp.mean(seg * seg, axis=-1, keepdims=True) + EPS)
        xh = seg * r
        outs.append(xh * gg)
        xhs.append(xh)
        rs.append(r)
    return outs, xhs, rs


def merge_fwd(o_mla, o_gdn, o_mem, P, x, tgt, w_out, g_gdn, g_fin, tm=512):
    T = x.shape[0]
    tm = min(tm, T)

    def body(om_ref, og_ref, oc_ref, gate_ref, x_ref, t_ref, w_ref, gg_ref, gf_ref, mix_ref, dx_ref, dxb_ref, sq_ref, gnf_ref):
        @pl.when(pl.program_id(0) == 0)
        def _():
            sq_ref[...] = jnp.zeros_like(sq_ref)
            gnf_ref[...] = jnp.zeros_like(gnf_ref)

        ogn, _, _ = _gdn_out_norm(og_ref[...].astype(F32), gg_ref[...])
        cat = jnp.concatenate([om_ref[...].astype(F32)] + ogn + [oc_ref[...].astype(F32)], axis=1)
        gt = gate_ref[...].astype(F32)
        mixed = (cat * (gt * _sigmoid(gt))).astype(BF16)
        mix_ref[...] = mixed
        x2 = x_ref[...] + _dot(mixed, w_ref[...], NN)
        r2 = lax.rsqrt(jnp.mean(x2 * x2, axis=-1, keepdims=True) + EPS)
        xh = x2 * r2
        gf = gf_ref[...]
        diff = xh * gf - t_ref[...]
        sq_ref[...] += jnp.sum(diff * diff, axis=0, keepdims=True)
        dy = diff * (1.0 / D_MODEL)
        gnf_ref[...] += jnp.sum(dy * xh, axis=0, keepdims=True)
        dxh = dy * gf
        dx = r2 * (dxh - xh * jnp.mean(dxh * xh, axis=-1, keepdims=True))
        dx_ref[...] = dx
        dxb_ref[...] = dx.astype(BF16)

    rowb = lambda c, j=0: BS((tm, c), lambda i: (i, j))
    full = lambda r, c: BS((r, c), lambda i: (0, 0))
    return pl.pallas_call(
        body, name="merge_fwd", grid=(T // tm,),
        in_specs=[rowb(512), rowb(512), rowb(512), rowb(D_MIX, OFF_GATE // D_MIX), rowb(D_MODEL), rowb(D_MODEL),
                  full(D_MIX, D_MODEL), full(1, 128), full(1, D_MODEL)],
        out_specs=[rowb(D_MIX), rowb(D_MODEL), rowb(D_MODEL), full(1, D_MODEL), full(1, D_MODEL)],
        out_shape=[jax.ShapeDtypeStruct((T, D_MIX), BF16), jax.ShapeDtypeStruct((T, D_MODEL), F32),
                   jax.ShapeDtypeStruct((T, D_MODEL), BF16),
                   jax.ShapeDtypeStruct((1, D_MODEL), F32), jax.ShapeDtypeStruct((1, D_MODEL), F32)],
        compiler_params=_arb(1))(o_mla, o_gdn, o_mem, P, x, tgt, w_out, g_gdn, g_fin)


def merge_bwd(dx2, o_mla, o_gdn, o_mem, P, w_out, g_gdn, tm=512):
    T = dx2.shape[0]
    tm = min(tm, T)

    def body(dx_ref, om_ref, og_ref, oc_ref, gate_ref, w_ref, gg_ref, dgate_ref, dom_ref, dog_ref, doc_ref, ggn_ref):
        @pl.when(pl.program_id(0) == 0)
        def _():
            ggn_ref[...] = jnp.zeros_like(ggn_ref)

        gg = gg_ref[...]
        dmix = _dot(dx_ref[...].astype(BF16), w_ref[...], NT)
        ogn, xhs, rs = _gdn_out_norm(og_ref[...].astype(F32), gg)
        cat = jnp.concatenate([om_ref[...].astype(F32)] + ogn + [oc_ref[...].astype(F32)], axis=1)
        gt = gate_ref[...].astype(F32)
        sg = _sigmoid(gt)
        dgate_ref[...] = (dmix * cat * (sg * (1.0 + gt * (1.0 - sg)))).astype(BF16)
        dcat = dmix * (gt * sg)
        dom_ref[...] = dcat[:, :512].astype(BF16)
        doc_ref[...] = dcat[:, 1024:].astype(BF16)
        acc = jnp.zeros((1, 128), F32)
        for h in range(N_HEADS):
            dseg = dcat[:, 512 + h * 128:512 + (h + 1) * 128]
            acc = acc + jnp.sum(dseg * xhs[h], axis=0, keepdims=True)
            dxh = dseg * gg
            dog_ref[:, h * 128:(h + 1) * 128] = (rs[h] * (dxh - xhs[h] * jnp.mean(dxh * xhs[h], axis=-1, keepdims=True))).astype(BF16)
        ggn_ref[...] += acc

    rowb = lambda c, j=0: BS((tm, c), lambda i: (i, j))
    full = lambda r, c: BS((r, c), lambda i: (0, 0))
    return pl.pallas_call(
        body, name="merge_bwd", grid=(T // tm,),
        in_specs=[rowb(D_MODEL), rowb(512), rowb(512), rowb(512), rowb(D_MIX, OFF_GATE // D_MIX),
                  full(D_MIX, D_MODEL), full(1, 128)],
        out_specs=[rowb(D_MIX), rowb(512), rowb(512), rowb(512), full(1, 128)],
        out_shape=[jax.ShapeDtypeStruct((T, D_MIX), BF16)] + [jax.ShapeDtypeStruct((T, 512), BF16)] * 3
        + [jax.ShapeDtypeStruct((1, 128), F32)],
        compiler_params=_arb(1))(dx2, o_mla, o_gdn, o_mem, P, w_out, g_gdn)


def in_proj_bwd(dP, wp, x, dx2, gain, after, tm=512):
    T, n = x.shape
    tm = min(tm, T)
    k = len(dP)
    widths = [p.shape[1] for p in dP]
    offs = [sum(widths[:i]) for i in range(k)]

    def body(*refs):
        w_ref, x_ref, dx2_ref, g_ref = refs[k:k + 4]
        o_ref, acc_ref = refs[-2:]

        @pl.when(pl.program_id(0) == 0)
        def _():
            acc_ref[...] = jnp.zeros_like(acc_ref)

        dy = None
        for a_ref, off, w in zip(refs[:k], offs, widths):
            d = _dot(a_ref[...], w_ref[off:off + w, :], NN)
            dy = d if dy is None else dy + d
        xv = x_ref[...]
        r = lax.rsqrt(jnp.mean(xv * xv, axis=-1, keepdims=True) + EPS)
        xh = xv * r
        acc_ref[...] += jnp.sum(dy * xh, axis=0, keepdims=True)
        dxh = dy * g_ref[...]
        o_ref[...] = dx2_ref[...] + r * (dxh - xh * jnp.mean(dxh * xh, axis=-1, keepdims=True))

    rowb = BS((tm, n), lambda i: (i, 0))
    full = BS((1, n), lambda i: (0, 0))
    return pl.pallas_call(
        body, name="in_proj_bwd", grid=(T // tm,),
        in_specs=[BS((tm, w), lambda i: (i, 0)) for w in widths]
        + [BS(wp.shape, lambda i: (0, 0), pipeline_mode=pl.Buffered(1)), rowb, rowb, full, BS(memory_space=pl.ANY)],
        out_specs=[rowb, full], out_shape=[jax.ShapeDtypeStruct((T, n), F32), jax.ShapeDtypeStruct((1, n), F32)],
        compiler_params=_arb(1))(*dP, wp, x, dx2, gain, after)


W_IN_SHARD = D_IN // 4
_GDN0 = Q_LORA + KV_LORA + MLA_ROPE
_AB0 = _GDN0 + GDN_QKV
_MEMQ0 = _AB0 + 2 * N_HEADS
_GATE0 = _MEMQ0 + N_HEADS * MEM_DH


def _w_in_row_map():
    a, m, gt = _AB0 - 2 * W_IN_SHARD, _MEMQ0 - 2 * W_IN_SHARD, _GATE0 - 2 * W_IN_SHARD
    e0 = OFF_GDN + W_IN_SHARD - _GDN0
    e1 = e0 + W_IN_SHARD
    e2 = OFF_GATE + W_IN_SHARD - gt
    return [(0, 0, 0, 672), (0, 672, 704, 32), (2, a, 768, m - a), (2, m, OFF_MEMQ, gt - m), (0, _GDN0, OFF_GDN, W_IN_SHARD - _GDN0),
            (1, 0, e0, W_IN_SHARD), (2, 0, e1, a), (2, gt, OFF_GATE, W_IN_SHARD - gt), (3, 0, e2, W_IN_SHARD)]


_W_IN_ZERO_ROWS = [(672, 32), (736, 32), (776, 248)]
W_IN_LANES = 256


def pad_w_in_t(shards):
    per_half = shards.shape[3] // W_IN_LANES

    def body(s_ref, o_ref):
        for r0, n in _W_IN_ZERO_ROWS:
            o_ref[r0:r0 + n, :] = jnp.zeros((n, W_IN_LANES), o_ref.dtype)
        for q, src, dst, n in _w_in_row_map():
            o_ref[dst:dst + n, :] = s_ref[q, src:src + n, :]

    return pl.pallas_call(
        body, name="pad_w_in_t", grid=(D_MODEL // W_IN_LANES,),
        in_specs=[BS((N_CHIPS, None, W_IN_SHARD, W_IN_LANES), lambda j: (0, j // per_half, 0, j % per_half))],
        out_specs=BS((N_PAD, W_IN_LANES), lambda j: (0, j)),
        out_shape=jax.ShapeDtypeStruct((N_PAD, D_MODEL), shards.dtype), compiler_params=_arb(1))(shards)


def unpad_w_in_t(g):
    def body(g_ref, o_ref):
        for q, src, dst, n in _w_in_row_map():
            o_ref[q, src:src + n, :] = g_ref[dst:dst + n, :]

    return pl.pallas_call(
        body, name="unpad_w_in_t", grid=(D_MODEL // W_IN_LANES,),
        in_specs=[BS((N_PAD, W_IN_LANES), lambda j: (0, j))], out_specs=BS((N_CHIPS, W_IN_SHARD, W_IN_LANES), lambda j: (0, 0, j)),
        out_shape=jax.ShapeDtypeStruct((N_CHIPS, W_IN_SHARD, D_MODEL), g.dtype), compiler_params=_arb(1))(g)


def _perm_w_kv_b(s):
    return jnp.concatenate([s[h, :, :128] for h in range(N_HEADS)] + [s[h, :, 128:] for h in range(N_HEADS)], axis=1)


def _unperm_w_kv_b(g):
    return jnp.stack([jnp.concatenate([g[:, h * 128:(h + 1) * 128], g[:, 512 + h * 128:512 + (h + 1) * 128]], axis=1)
                      for h in range(N_HEADS)])


def _lane_row(v4):
    return jnp.pad(v4.reshape(1, -1).astype(F32), ((0, 0), (0, 128 - v4.size)))


N_CHIPS = 4
MESH = pl.DeviceIdType.MESH
ANY = BS(memory_space=pl.ANY)


def _place():
    return lax.axis_index("x"), lax.axis_index("y"), lax.axis_index("c")


def _other_chips(x, y):
    return [(1 - x, y), (x, 1 - y), (1 - x, 1 - y)]


def _half(split, which):
    axis, size = split
    ds = pl.ds(pl.multiple_of(which * size, 16 if axis == 0 else 128), size)
    return (ds, slice(None)) if axis == 0 else (slice(None), ds)


SEM = BS(memory_space=pltpu.SEMAPHORE)
HBM = BS(memory_space=pltpu.HBM)
_IN_HBM = lambda a: pltpu.with_memory_space_constraint(a, pltpu.HBM)
_SIDE_EFFECT = pltpu.SideEffectType.DATAFLOW_SIDE_EFFECTING


def _late_gather_copies(s_refs, l_refs, send_sems, recv_sems, local_sems, with_arrivals):
    x, y, c = _place()
    sends, recvs, locals_ = [], [], []
    for i, (s_ref, l_ref) in enumerate(zip(s_refs, l_refs)):
        locals_.append(pltpu.make_async_copy(s_ref, l_ref.at[2 * x + y], local_sems.at[i]))
        for j, (px, py) in enumerate(_other_chips(x, y)):
            k = 3 * i + j
            sends.append(pltpu.make_async_remote_copy(src_ref=s_ref, dst_ref=l_ref.at[2 * x + y], send_sem=send_sems.at[k],
                                                      recv_sem=recv_sems.at[k], device_id=(px, py, c), device_id_type=MESH))
            if with_arrivals:
                recvs.append(pltpu.make_async_remote_copy(src_ref=s_ref, dst_ref=l_ref.at[2 * px + py], send_sem=send_sems.at[k],
                                                          recv_sem=recv_sems.at[k], device_id=(px, py, c), device_id_type=MESH))
    return sends, recvs, locals_


def late_gather_start(shards, after, name):
    n = len(shards)

    def body(*refs):
        s_refs, l_refs = refs[:n], refs[n:2 * n]
        send_sems, recv_sems, local_sems = refs[2 * n + 1:2 * n + 4]
        token = refs[-1]
        sends, _, locals_ = _late_gather_copies(s_refs, l_refs, send_sems, recv_sems, local_sems, False)
        for cp in locals_ + sends:
            cp.start()
        token[...] = jnp.zeros_like(token)

    lands = [lax.empty((N_CHIPS,) + s.shape, s.dtype) for s in shards]
    hbm_like = lambda a: pltpu.HBM(a.shape, a.dtype)
    out = pl.pallas_call(
        body, name=name,
        out_shape=[pltpu.SemaphoreType.DMA((3 * n,)), pltpu.SemaphoreType.DMA((3 * n,)), pltpu.SemaphoreType.DMA((n,))]
        + [hbm_like(s) for s in shards] + [hbm_like(l) for l in lands] + [jax.ShapeDtypeStruct((8, 128), F32)],
        in_specs=[HBM] * (2 * n) + [BS(memory_space=pl.ANY)], out_specs=[SEM] * 3 + [HBM] * (2 * n) + [BS(memory_space=pltpu.VMEM)],
        input_output_aliases={i: 3 + i for i in range(2 * n)},
        compiler_params=pltpu.CompilerParams(has_side_effects=_SIDE_EFFECT))(
            *[_IN_HBM(s) for s in shards], *[_IN_HBM(l) for l in lands], after)
    return out[:3], out[3:3 + n], out[3 + n:3 + 2 * n], out[-1]


def late_gather_wait(sems, shards, lands, after, name):
    n = len(shards)

    def body(*refs):
        s_refs, l_refs = refs[:n], refs[n:2 * n]
        send_sems, recv_sems, local_sems = refs[2 * n:2 * n + 3]
        sends, recvs, locals_ = _late_gather_copies(s_refs, l_refs, send_sems, recv_sems, local_sems, True)
        for cp in locals_:
            cp.wait()
        for cp in sends:
            cp.wait_send()
        for cp in recvs:
            cp.wait_recv()

    hbm_like = lambda a: pltpu.HBM(a.shape, a.dtype)
    out = pl.pallas_call(
        body, name=name, out_shape=[hbm_like(s) for s in shards] + [hbm_like(l) for l in lands],
        in_specs=[HBM] * (2 * n) + [SEM] * 3 + [BS(memory_space=pl.ANY)], out_specs=[HBM] * (2 * n),
        input_output_aliases={i: i for i in range(2 * n)},
        compiler_params=pltpu.CompilerParams(has_side_effects=_SIDE_EFFECT))(*shards, *lands, *sems, after)
    return out[n:]


def _half_gather_copies(s_ref, l_ref, send_sems, recv_sems, with_arrivals):
    x, y, c = _place()
    sends, recvs = [], []
    for j, (px, py) in enumerate(_other_chips(x, y)):
        sends.append(pltpu.make_async_remote_copy(src_ref=s_ref.at[c], dst_ref=l_ref.at[2 * x + y, c], send_sem=send_sems.at[j],
                                                  recv_sem=recv_sems.at[j], device_id=(px, py, c), device_id_type=MESH))
        if with_arrivals:
            recvs.append(pltpu.make_async_remote_copy(src_ref=s_ref.at[c], dst_ref=l_ref.at[2 * px + py, c], send_sem=send_sems.at[j],
                                                      recv_sem=recv_sems.at[j], device_id=(px, py, c), device_id_type=MESH))
    return sends, recvs


def half_gather_start(shard, name):
    def body(s_ref, l_ref, send_sems, recv_sems, local_sem, s_thru, l_thru, token):
        x, y, _ = _place()
        pltpu.make_async_copy(s_ref, l_ref.at[2 * x + y], local_sem.at[0]).start()
        for cp in _half_gather_copies(s_ref, l_ref, send_sems, recv_sems, False)[0]:
            cp.start()
        token[...] = jnp.zeros_like(token)

    land = lax.empty((N_CHIPS,) + shard.shape, shard.dtype)
    out = pl.pallas_call(
        body, name=name,
        out_shape=[pltpu.SemaphoreType.DMA((3,)), pltpu.SemaphoreType.DMA((3,)), pltpu.SemaphoreType.DMA((1,)),
                   pltpu.HBM(shard.shape, shard.dtype), pltpu.HBM(land.shape, land.dtype), jax.ShapeDtypeStruct((8, 128), F32)],
        in_specs=[HBM, HBM], out_specs=[SEM] * 3 + [HBM, HBM, BS(memory_space=pltpu.VMEM)],
        input_output_aliases={0: 3, 1: 4},
        compiler_params=pltpu.CompilerParams(has_side_effects=_SIDE_EFFECT))(_IN_HBM(shard), _IN_HBM(land))
    return out[:3], out[3], out[4], out[5]


def half_gather_wait(sems, shard, land, after, name):
    def body(s_ref, l_ref, send_sems, recv_sems, local_sem, *rest):
        x, y, _ = _place()
        pltpu.make_async_copy(s_ref, l_ref.at[2 * x + y], local_sem.at[0]).wait()
        sends, recvs = _half_gather_copies(s_ref, l_ref, send_sems, recv_sems, True)
        for cp in sends:
            cp.wait_send()
        for cp in recvs:
            cp.wait_recv()

    out = pl.pallas_call(
        body, name=name, out_shape=[pltpu.HBM(shard.shape, shard.dtype), pltpu.HBM(land.shape, land.dtype)],
        in_specs=[HBM, HBM] + [SEM] * 3 + [BS(memory_space=pl.ANY)] * len(after), out_specs=[HBM, HBM],
        input_output_aliases={0: 0, 1: 1},
        compiler_params=pltpu.CompilerParams(has_side_effects=_SIDE_EFFECT))(shard, land, *sems, *after)
    return out[1]


def pass_halves_to_sibling(land, name):
    def body(l_in, l_ref, send_sems, recv_sems):
        x, y, c = _place()
        copies = []
        for j, (px, py) in enumerate(_other_chips(x, y)):
            q = 2 * px + py
            give = pltpu.make_async_remote_copy(src_ref=l_ref.at[q, c], dst_ref=l_ref.at[q, c], send_sem=send_sems.at[j],
                                                recv_sem=recv_sems.at[j], device_id=(x, y, 1 - c), device_id_type=MESH)
            take = pltpu.make_async_remote_copy(src_ref=l_ref.at[q, c], dst_ref=l_ref.at[q, 1 - c], send_sem=send_sems.at[j],
                                                recv_sem=recv_sems.at[j], device_id=(x, y, 1 - c), device_id_type=MESH)
            give.start()
            copies.append((give, take))
        for give, take in copies:
            take.wait_recv()
            give.wait_send()

    return pl.pallas_call(
        body, name=name, in_specs=[ANY], out_specs=ANY, out_shape=jax.ShapeDtypeStruct(land.shape, land.dtype),
        input_output_aliases={0: 0},
        scratch_shapes=[pltpu.SemaphoreType.DMA((3,)), pltpu.SemaphoreType.DMA((3,))])(land)


def allgather_devices(block, name):
    R, C = block.shape

    def body(b_ref, o_ref, send_sems, recv_sems, local_sem):
        x, y, c = _place()
        me = 4 * x + 2 * y + c
        own = pltpu.make_async_copy(b_ref, o_ref.at[me], local_sem)
        own.start()
        copies = []
        for r in range(1, 8):
            px = 1 - x if r & 4 else x
            py = 1 - y if r & 2 else y
            pc = 1 - c if r & 1 else c
            send = pltpu.make_async_remote_copy(src_ref=b_ref, dst_ref=o_ref.at[me], send_sem=send_sems.at[r - 1],
                                                recv_sem=recv_sems.at[r - 1], device_id=(px, py, pc), device_id_type=MESH)
            recv = pltpu.make_async_remote_copy(src_ref=b_ref, dst_ref=o_ref.at[4 * px + 2 * py + pc], send_sem=send_sems.at[r - 1],
                                                recv_sem=recv_sems.at[r - 1], device_id=(px, py, pc), device_id_type=MESH)
            send.start()
            copies.append((send, recv))
        for send, recv in copies:
            recv.wait_recv()
            send.wait_send()
        own.wait()

    return pl.pallas_call(
        body, name=name, in_specs=[ANY], out_specs=ANY, out_shape=jax.ShapeDtypeStruct((8, R, C), block.dtype),
        scratch_shapes=[pltpu.SemaphoreType.DMA((7,)), pltpu.SemaphoreType.DMA((7,)), pltpu.SemaphoreType.DMA(())])(block)


def swap_sibling(arrs, name, splits=None):
    n = len(arrs)

    def sent(a_ref, i, c):
        return a_ref if splits is None else a_ref.at[(slice(None),) + _half(splits[i], 1 - c)]

    def out_shape(a, i):
        if splits is None:
            return a.shape
        axis, size = splits[i]
        return (a.shape[0], size, a.shape[2]) if axis == 0 else (a.shape[0], a.shape[1], size)

    def body(*refs):
        a_refs, o_refs = refs[:n], refs[n:2 * n]
        send_sems, recv_sems = refs[2 * n:]
        x, y, c = _place()
        copies = [pltpu.make_async_remote_copy(src_ref=sent(a_ref, i, c), dst_ref=o_ref, send_sem=send_sems.at[i],
                                               recv_sem=recv_sems.at[i], device_id=(x, y, 1 - c), device_id_type=MESH)
                  for i, (a_ref, o_ref) in enumerate(zip(a_refs, o_refs))]
        for cp in copies:
            cp.start()
        for cp in copies:
            cp.wait()

    return pl.pallas_call(
        body, name=name, in_specs=[ANY] * n, out_specs=[ANY] * n,
        out_shape=[jax.ShapeDtypeStruct(out_shape(a, i), a.dtype) for i, a in enumerate(arrs)],
        scratch_shapes=[pltpu.SemaphoreType.DMA((n,)), pltpu.SemaphoreType.DMA((n,))])(*arrs)


def _exchange_copies(p_refs, l_refs, send_sems, recv_sems):
    x, y, c = _place()
    return [pltpu.make_async_remote_copy(src_ref=p_ref.at[2 * px + py], dst_ref=l_ref.at[j], send_sem=send_sems.at[3 * i + j],
                                         recv_sem=recv_sems.at[3 * i + j], device_id=(px, py, c), device_id_type=MESH)
            for i, (p_ref, l_ref) in enumerate(zip(p_refs, l_refs)) for j, (px, py) in enumerate(_other_chips(x, y))]


def exchange_chips_start(parts, name):
    n = len(parts)

    def body(*refs):
        send_sems, recv_sems = refs[2 * n:2 * n + 2]
        for cp in _exchange_copies(refs[:n], refs[n:2 * n], send_sems, recv_sems):
            cp.start()
        refs[-1][...] = jnp.zeros_like(refs[-1])

    lands = [lax.empty((3,) + p.shape[1:], p.dtype) for p in parts]
    hbm_like = lambda a: pltpu.HBM(a.shape, a.dtype)
    out = pl.pallas_call(
        body, name=name,
        out_shape=[pltpu.SemaphoreType.DMA((3 * n,)), pltpu.SemaphoreType.DMA((3 * n,))]
        + [hbm_like(p) for p in parts] + [hbm_like(l) for l in lands] + [jax.ShapeDtypeStruct((8, 128), F32)],
        in_specs=[HBM] * (2 * n), out_specs=[SEM] * 2 + [HBM] * (2 * n) + [BS(memory_space=pltpu.VMEM)],
        input_output_aliases={i: 2 + i for i in range(2 * n)},
        compiler_params=pltpu.CompilerParams(has_side_effects=_SIDE_EFFECT))(*[_IN_HBM(p) for p in parts], *[_IN_HBM(l) for l in lands])
    return out[:2], out[2:2 + n], out[2 + n:2 + 2 * n], out[-1]


def exchange_chips_wait(sems, parts, lands, after, name):
    n = len(parts)

    def body(*refs):
        send_sems, recv_sems = refs[2 * n:2 * n + 2]
        for cp in _exchange_copies(refs[:n], refs[n:2 * n], send_sems, recv_sems):
            cp.wait_send()
            cp.wait_recv()

    hbm_like = lambda a: pltpu.HBM(a.shape, a.dtype)
    out = pl.pallas_call(
        body, name=name, out_shape=[hbm_like(p) for p in parts] + [hbm_like(l) for l in lands],
        in_specs=[HBM] * (2 * n) + [SEM] * 2 + [BS(memory_space=pl.ANY)], out_specs=[HBM] * (2 * n),
        input_output_aliases={i: i for i in range(2 * n)},
        compiler_params=pltpu.CompilerParams(has_side_effects=_SIDE_EFFECT))(*parts, *lands, *sems, after)
    return out[n:]


def _half_block(shape2, split):
    axis, size = split
    return (size, shape2[1]) if axis == 0 else (shape2[0], size)


def add_pairs(parts, halves, splits, core, name):
    n = len(parts)

    def body(s_ref, *refs):
        for a_ref, b_ref, o_ref in zip(refs[:n], refs[n:2 * n], refs[2 * n:]):
            o_ref[...] = (a_ref[...].astype(F32) + b_ref[...].astype(F32)).astype(BF16)

    def mine(i):
        blk = (None,) + _half_block(parts[i].shape[1:], splits[i])
        if splits[i][0] == 0:
            return BS(blk, lambda q, s: (q, s[0], 0))
        return BS(blk, lambda q, s: (q, 0, s[0]))

    half_specs = [BS((None,) + h.shape[1:], lambda q, s: (q, 0, 0)) for h in halves]
    return pl.pallas_call(
        body, name=name,
        grid_spec=pltpu.PrefetchScalarGridSpec(num_scalar_prefetch=1, grid=(N_CHIPS,),
                                               in_specs=[mine(i) for i in range(n)] + half_specs, out_specs=half_specs),
        out_shape=[jax.ShapeDtypeStruct(h.shape, BF16) for h in halves], compiler_params=_arb(1))(core, *parts, *halves)


def add_fives(parts, halves, from_chips, splits, chip_core, name):
    n = len(parts)

    def body(s_ref, *refs):
        for a_ref, b_ref, p_ref, o_ref in zip(refs[:n], refs[n:2 * n], refs[2 * n:3 * n], refs[3 * n:]):
            s = a_ref[...].astype(F32) + b_ref[...].astype(F32)
            for j in range(3):
                s = s + p_ref[j].astype(F32)
            o_ref[...] = s

    def mine(i):
        blk = (None,) + _half_block(parts[i].shape[1:], splits[i])
        if splits[i][0] == 0:
            return BS(blk, lambda g, s: (s[0], s[1], 0))
        return BS(blk, lambda g, s: (s[0], 0, s[1]))

    half_specs = [BS((None,) + h.shape[1:], lambda g, s: (s[0], 0, 0)) for h in halves]
    chip_specs = [BS(p.shape, lambda g, s: (0, 0, 0)) for p in from_chips]
    out_specs = [BS(h.shape[1:], lambda g, s: (0, 0)) for h in halves]
    return pl.pallas_call(
        body, name=name,
        grid_spec=pltpu.PrefetchScalarGridSpec(num_scalar_prefetch=1, grid=(1,),
                                               in_specs=[mine(i) for i in range(n)] + half_specs + chip_specs, out_specs=out_specs),
        out_shape=[jax.ShapeDtypeStruct(h.shape[1:], F32) for h in halves], compiler_params=_arb(1))(chip_core, *parts, *halves, *from_chips)


def sum_leading(a, name):
    def body(a_ref, o_ref):
        s = a_ref[0]
        for j in range(1, a.shape[0]):
            s = s + a_ref[j]
        o_ref[...] = s

    return pl.pallas_call(body, name=name, out_shape=jax.ShapeDtypeStruct(a.shape[1:], a.dtype))(a)


def _adamw_math(w, g, m, v):
    mn = ADAM_B1 * m + (1.0 - ADAM_B1) * g
    vn = ADAM_B2 * v + (1.0 - ADAM_B2) * (g * g)
    m_hat = mn / (1.0 - ADAM_B1 ** ADAM_STEP)
    v_hat = vn / (1.0 - ADAM_B2 ** ADAM_STEP)
    return -ADAM_LR * (m_hat / (jnp.sqrt(v_hat) + ADAM_EPS) + ADAM_WD * w), mn, vn


def adamw(w, g, m, v, name):
    R, C = g.shape
    lead = (None,) * (w.ndim - 2)

    def body(w_ref, g_ref, m_ref, v_ref, d_ref, mo_ref, vo_ref):
        d_ref[...], mo_ref[...], vo_ref[...] = _adamw_math(w_ref[...], g_ref[...], m_ref[...], v_ref[...])

    wblk = BS(lead + (R, C), lambda i: (0,) * w.ndim)
    gblk = BS((R, C), lambda i: (0, 0))
    return pl.pallas_call(
        body, name=name, grid=(1,), in_specs=[wblk, gblk, wblk, wblk], out_specs=[wblk] * 3,
        out_shape=[jax.ShapeDtypeStruct(w.shape, F32)] * 3, compiler_params=_arb(1))(w, g, m, v)


SMALL_ROWS = 16
CONV_ROW0 = 8
LOSS_ROW = 14


def pack_small(small_grads, g_ab, g_conv, sq):
    present = [a for a in small_grads if a is not None]

    def body(*refs):
        ab_ref, conv_ref, sq_ref, o_ref = refs[len(present):]
        o_ref[...] = jnp.zeros_like(o_ref)
        it = iter(refs[:len(present)])
        for i, a in enumerate(small_grads):
            if a is not None:
                o_ref[i:i + 1, 0:a.shape[1]] = next(it)[...]
        o_ref[3:4, 0:128] = ab_ref[0:1, :]
        o_ref[4:5, 0:128] = ab_ref[1:2, :]
        half = 512
        for k in range(GDN_CONV * GDN_QKV // half):
            src_r, src_c = (k * half) // GDN_QKV, (k * half) % GDN_QKV
            dst_r, dst_c = CONV_ROW0 + (k * half) // 1024, (k * half) % 1024
            o_ref[dst_r:dst_r + 1, dst_c:dst_c + half] = conv_ref[src_r:src_r + 1, src_c:src_c + half]
        o_ref[LOSS_ROW:LOSS_ROW + 1, :] = sq_ref[...]

    return pl.pallas_call(body, name="pack_small", out_shape=jax.ShapeDtypeStruct((SMALL_ROWS, 1024), F32))(
        *present, g_ab, g_conv, sq)


def adamw_small(block, ws, ms, vs):
    k = len(ws)

    def body(b_ref, *refs):
        outs = refs[3 * k:]
        for i in range(k):
            n = ws[i].shape[1]
            g = b_ref[i:i + 1, 0:n]
            d, mn, vn = _adamw_math(refs[i][...], g, refs[k + i][...], refs[2 * k + i][...])
            outs[4 * i][...], outs[4 * i + 1][...], outs[4 * i + 2][...], outs[4 * i + 3][...] = g, d, mn, vn

    out = pl.pallas_call(
        body, name="adamw_small",
        out_shape=[jax.ShapeDtypeStruct(w.shape, F32) for w in ws for _ in range(4)])(block, *ws, *ms, *vs)
    return [out[4 * i:4 * i + 4] for i in range(k)]


def adamw_halves(w, mine, other, m, v, split, core, name):
    R, C = w.shape[-2:]
    axis, size = split
    lead = (None,) * (w.ndim - 2)
    zeros = (0,) * (w.ndim - 2)
    if axis == 0:
        tr = size if size <= 256 else next(t for t in range(256, 7, -1) if size % t == 0 and t % 8 == 0)
        nb = size // tr
        whole = BS(lead + (tr, C), lambda hi, j, s: zeros + (hi * nb + j, 0))
        part = BS((tr, C), lambda hi, j, s: (j, 0))
    else:
        nb = size // 128
        whole = BS(lead + (R, 128), lambda hi, j, s: zeros + (0, hi * nb + j))
        part = BS((R, 128), lambda hi, j, s: (0, j))

    def body(s_ref, w_ref, a_ref, b_ref, m_ref, v_ref, g_ref, d_ref, mo_ref, vo_ref):
        g = jnp.where(pl.program_id(0) == s_ref[0], a_ref[...], b_ref[...])
        g_ref[...] = g
        d_ref[...], mo_ref[...], vo_ref[...] = _adamw_math(w_ref[...], g, m_ref[...], v_ref[...])

    return pl.pallas_call(
        body, name=name,
        grid_spec=pltpu.PrefetchScalarGridSpec(num_scalar_prefetch=1, grid=(2, nb),
                                               in_specs=[whole, part, part, whole, whole], out_specs=[whole] * 4),
        out_shape=[jax.ShapeDtypeStruct(w.shape, F32)] * 4, compiler_params=_arb(2))(core, w, mine, other, m, v)


def dense_bf16(w3, name):
    R, _, K = w3.shape
    kh = K // 2

    def body(w_hbm, o_ref, buf, sem):
        cp = pltpu.make_async_copy(w_hbm.at[:, 0], buf, sem)
        cp.start()
        cp.wait()
        o_ref[0] = buf[:, :kh].astype(BF16)
        o_ref[1] = buf[:, kh:].astype(BF16)

    return pl.pallas_call(
        body, name=name, in_specs=[ANY], out_specs=BS(memory_space=pltpu.VMEM), out_shape=jax.ShapeDtypeStruct((2, R, kh), BF16),
        scratch_shapes=[pltpu.VMEM((R, K), F32), pltpu.SemaphoreType.DMA(())])(w3)


ROW_BLOCK = 184


def adamw_untiled_rows(w3, mine, other, m3, v3, name):
    R, _, K = w3.shape
    kh = K // 2
    starts = list(range(0, R, ROW_BLOCK))
    sizes = [min(ROW_BLOCK, R - s) for s in starts]
    nblk = len(starts)

    def body(w_hbm, a_ref, b_ref, m_hbm, v_hbm, g_hbm, d_hbm, mo_hbm, vo_hbm,
             wbuf, mbuf, vbuf, gbuf, dbuf, mobuf, vobuf, in_sems, out_sems):
        first = lax.axis_index("c") == 0
        ins = []
        for k, (r0, n) in enumerate(zip(starts, sizes)):
            rows = pl.ds(r0, n)
            cps = [pltpu.make_async_copy(src.at[rows, 0], dst.at[rows], in_sems.at[3 * k + i])
                   for i, (src, dst) in enumerate(((w_hbm, wbuf), (m_hbm, mbuf), (v_hbm, vbuf)))]
            for cp in cps:
                cp.start()
            ins.append(cps)

        def update(rows):
            a, b = a_ref[rows, :], b_ref[rows, :]
            g = jnp.concatenate([jnp.where(first, a, b), jnp.where(first, b, a)], axis=1)
            gbuf[rows, :] = g
            dbuf[rows, :], mobuf[rows, :], vobuf[rows, :] = _adamw_math(wbuf[rows, :], g, mbuf[rows, :], vbuf[rows, :])

        outs = []
        for k, (r0, n) in enumerate(zip(starts, sizes)):
            for cp in ins[k]:
                cp.wait()
            groups, tail = n // 8, n % 8

            def group(i, carry, r0=r0):
                update(pl.ds(pl.multiple_of(r0 + i * 8, 8), 8))
                return carry

            lax.fori_loop(0, groups, group, 0)
            if tail:
                update(pl.ds(r0 + groups * 8, tail))
            rows = pl.ds(r0, n)
            cps = [pltpu.make_async_copy(src.at[rows], dst.at[rows, 0], out_sems.at[4 * k + i])
                   for i, (src, dst) in enumerate(((gbuf, g_hbm), (dbuf, d_hbm), (mobuf, mo_hbm), (vobuf, vo_hbm)))]
            for cp in cps:
                cp.start()
            outs += cps
        for cp in outs:
            cp.wait()

    vmem = BS(memory_space=pltpu.VMEM)
    return pl.pallas_call(
        body, name=name, in_specs=[ANY, vmem, vmem, ANY, ANY], out_specs=[ANY] * 4,
        out_shape=[jax.ShapeDtypeStruct(w3.shape, F32)] * 4,
        scratch_shapes=[pltpu.VMEM((R, K), F32)] * 7 + [pltpu.SemaphoreType.DMA((3 * nblk,)), pltpu.SemaphoreType.DMA((4 * nblk,))])(
            w3, mine, other, m3, v3)


def adamw_w_q_b(w, mine, other, m, v, name):
    def body(w_ref, a_ref, b_ref, m_ref, v_ref, g_ref, d_ref, mo_ref, vo_ref):
        first = lax.axis_index("c") == 0
        lo = jnp.where(first, a_ref[...], b_ref[...])
        hi = jnp.where(first, b_ref[...], a_ref[...])
        g = jnp.concatenate([lo, hi[0:32], hi[64:96]], axis=0)
        g_ref[...] = g
        d_ref[...], mo_ref[...], vo_ref[...] = _adamw_math(w_ref[...], g, m_ref[...], v_ref[...])

    return pl.pallas_call(body, name=name, out_shape=[jax.ShapeDtypeStruct(w.shape, F32)] * 4)(w, mine, other, m, v)


def local_step(x, mem, positions, tgt, norm_in, weights, big_grads_ready, q_a_norm, kv_a_norm, gdn_conv, gdn_a_log,
               gdn_dt_bias, gdn_norm, mem_norm, norm_final):
    B, S, D = x.shape
    M = mem.shape[1]
    T = B * S
    N = S // CHUNK
    x2d = x.reshape(T, D)
    mem2d = mem.reshape(B * M, D)
    tgt2d = tgt.reshape(T, D)

    alog_row, dt_row = _lane_row(gdn_a_log), _lane_row(gdn_dt_bias)

    half = MLA_ROPE // 2
    inv_freq = 1.0 / (ROPE_THETA ** (jnp.arange(half, dtype=F32) / half))
    z32 = jnp.zeros((half,), F32)
    o32 = jnp.ones((half,), F32)
    inv_row = jnp.concatenate([inv_freq, z32, inv_freq, z32]).reshape(1, 128)
    sgn_row = jnp.concatenate([-o32, z32, o32, z32]).reshape(1, 128)
    msk_row = jnp.concatenate([o32, z32, o32, z32]).reshape(1, 128)
    cos_t, sin_t = rope_tables(positions.reshape(T, 1), inv_row, sgn_row, msk_row, after=weights[3])

    h = rms_fwd(x2d, norm_in, "rms_in", after=weights[3])
    wp, behind = weights[0]((h, cos_t))
    P = mm(h, wp, "nt", F32, "in_proj", bm=1024, bn=1536, n_outer=True, after=behind)
    wq, wkv = weights[1](P)
    Q, K, V, qn, kvn = mla_prep(P, q_a_norm, kv_a_norm, wq, wkv, cos_t, sin_t)
    o_mla, lse = mla_attn_fwd(Q, K, V, B, S)
    qkv = gdn_prep_fwd(P, gdn_conv, B, S)
    GB = gdn_gate_fwd(P, alog_row, dt_row, B, S)
    Grow = jnp.transpose(GB[:, :N_HEADS].reshape(B, N, CHUNK, N_HEADS), (0, 3, 1, 2))
    U, W, Tinv, A = gdn_chunk_fwd(qkv, GB, Grow, B, S)
    qkv3, GB3 = qkv.reshape(B, S, GDN_QKV), GB.reshape(B, S, 128)
    W3 = W.reshape(B, S, 512)
    o_gdn3, Vn3, St = gdn_scan_fwd(qkv3, U.reshape(B, S, 512), W3, GB3, A, B, S)
    o_gdn = o_gdn3.reshape(T, 512)
    w_mem_kv, w_out = weights[2](o_gdn)
    memn = rms_fwd(mem2d, mem_norm, "rms_mem")
    MKV = mm(memn, w_mem_kv, "nn", BF16, "mem_kv_proj")
    o_mem = mem_attn_fwd(P, MKV, B, S, M)
    mixed, dx2, dx2b, sq, g_norm_final = merge_fwd(o_mla, o_gdn, o_mem, P, x2d, tgt2d, w_out, gdn_norm, norm_final.reshape(1, D))

    g_w_out = mm(mixed, dx2b, "tn", BF16, "grad_w_out")
    dgate, do_mla, do_gdn, do_mem, g_gdn_norm = merge_bwd(dx2b, o_mla, o_gdn, o_mem, P, w_out, gdn_norm)

    dmemq, dMKV = mem_attn_bwd(P, MKV, do_mem, B, S, M)
    g_w_mem_kv = mm(memn, dMKV, "tn", BF16, "grad_w_mem_kv")
    started_early = big_grads_ready(dict(w_mem_kv=g_w_mem_kv, w_out=g_w_out), "early")
    dmemn = mm(dMKV, w_mem_kv, "nt", F32, "d_memn", after=(started_early,))
    g_mem_norm = gain_grad(mem2d, dmemn, "grad_mem_norm")

    dU3, dW3, dQ13, dK13, dA, dG13 = gdn_scan_bwd(do_gdn.reshape(B, S, 512), qkv3, W3, Vn3, GB3, A, St, B, S)
    r2 = lambda a: a.reshape(T, a.shape[-1])
    dqkv, dGB = gdn_chunk_bwd(qkv, GB, Grow, Tinv, dA, r2(dU3), r2(dW3), r2(dQ13), r2(dK13), r2(dG13), B, S)
    dPg, g_conv = gdn_prep_bwd(P, dqkv, gdn_conv, B, S)
    dab, g_ab = gdn_gate_bwd(P, dGB, alog_row, dt_row, B, S)

    dQ, dK, dV = mla_attn_bwd(Q, K, V, o_mla, do_mla, lse, B, S)
    dq_lin, dkv_lin, dPm, g_q_a_norm, g_kv_a_norm = mla_proj_bwd(dQ, dK, dV, cos_t, sin_t, P, dab, wq, wkv, q_a_norm, kv_a_norm)
    g_wq = mm(dq_lin, qn, "tn", BF16, "grad_w_q_b")
    g_wkv = mm(kvn, dkv_lin, "tn", BF16, "grad_w_kv_b")

    dP = [dPm, dmemq, dPg, dgate]
    g_wp = mm_cols_tn(dP, h, BF16, "grad_w_in")
    started = big_grads_ready(dict(w_in=g_wp, w_q_b=g_wq, w_kv_b=g_wkv), "late")
    grad_x, g_norm_in = in_proj_bwd(dP, wp, x2d, dx2, norm_in, started)

    grads = dict(
        norm_in=g_norm_in, q_a_norm=g_q_a_norm, kv_a_norm=g_kv_a_norm, gdn_conv=g_conv,
        gdn_a_log_dt_bias=g_ab, gdn_norm=g_gdn_norm,
        mem_norm=g_mem_norm, norm_final=g_norm_final)
    return sq, grad_x.reshape(B, S, D), grads


def kernel(x, mem, positions, norm_in, w_in, q_a_norm, w_q_b, kv_a_norm, w_kv_b, gdn_conv, gdn_a_log, gdn_dt_bias, gdn_norm, mem_norm, w_mem_kv, w_out, norm_final, loss_target, m_norm_in, m_w_in, m_q_a_norm, m_w_q_b, m_kv_a_norm, m_w_kv_b, m_gdn_conv, m_gdn_a_log, m_gdn_dt_bias, m_gdn_norm, m_mem_norm, m_w_mem_kv, m_w_out, m_norm_final, v_norm_in, v_w_in, v_q_a_norm, v_w_q_b, v_kv_a_norm, v_w_kv_b, v_gdn_conv, v_gdn_a_log, v_gdn_dt_bias, v_gdn_norm, v_mem_norm, v_w_mem_kv, v_w_out, v_norm_final):
    B = x.shape[0]
    cx, cy, cc = lax.axis_index("x"), lax.axis_index("y"), lax.axis_index("c")
    chip = 2 * cx + cy

    big_names = ("w_in", "w_q_b", "w_kv_b", "w_mem_kv", "w_out")
    rows_major = lambda a: jnp.transpose(a, (2, 0, 1))
    w_in3, m_in3, v_in3 = rows_major(w_in), rows_major(m_w_in), rows_major(v_w_in)
    w_qb_t, m_qb_t, v_qb_t = jnp.transpose(w_q_b[0]), jnp.transpose(m_w_q_b[0]), jnp.transpose(v_w_q_b[0])
    z32 = jnp.zeros((32, Q_LORA), BF16)
    qb_bf = w_qb_t.astype(BF16)
    qb_padded = jnp.concatenate([qb_bf[:160], z32, qb_bf[160:], z32])
    shards = [dense_bf16(w_in3, "w_in_bf16"), qb_padded, w_kv_b[0].astype(BF16), w_mem_kv[0].astype(BF16), w_out[0].astype(BF16)]
    splits = [(1, D_MODEL // 2)] + [(0, s.shape[0] // 2) for s in shards[1:]]
    *w_in_flight, w_in_started = half_gather_start(shards[0], "w_in_gather_start")
    conv_all = allgather_devices(gdn_conv[0], "allgather_conv")
    conv_cols = gdn_conv.shape[2]
    conv_full = jnp.transpose(conv_all[0::2], (1, 0, 2)).reshape(GDN_CONV, N_CHIPS * conv_cols)
    late_shapes = [(N_CHIPS,) + s.shape for s in shards[1:]]
    late = {}

    def w_in_ready(after):
        g_in = pass_halves_to_sibling(half_gather_wait(*w_in_flight, after, "w_in_gather_wait"), "w_in_gather_sibling")
        *late["a"], started_a = late_gather_start(shards[1:3], g_in, "late_gather_qkv_start")
        *late["b"], started_b = late_gather_start(shards[3:], started_a, "late_gather_mem_out_start")
        return pad_w_in_t(g_in), (started_a, started_b)

    def late_qkv(after):
        g_qb, g_kvb = late_gather_wait(*late["a"], after, "late_gather_qkv_wait")
        return g_qb.reshape(-1, Q_LORA), _perm_w_kv_b(g_kvb)

    def late_mem_out(after):
        g_mem, g_out_w = late_gather_wait(*late["b"], after, "late_gather_mem_out_wait")
        return g_mem.reshape(-1, g_mem.shape[2]), g_out_w.reshape(-1, g_out_w.shape[2])

    weights = (w_in_ready, late_qkv, late_mem_out, (w_in_started,))

    core = jnp.stack([cc]).astype(jnp.int32)
    chip_core = jnp.stack([chip, cc]).astype(jnp.int32)
    exchanges = {}
    by_chip = dict(w_in=unpad_w_in_t, w_q_b=lambda a: a.reshape(late_shapes[0]), w_kv_b=_unperm_w_kv_b,
                   w_mem_kv=lambda a: a.reshape(late_shapes[2]), w_out=lambda a: a.reshape(late_shapes[3]))

    def big_grads_ready(gb, group):
        idx = [big_names.index(n) for n in gb]
        parts = [by_chip[n](a) for n, a in gb.items()]
        sp = [splits[i] for i in idx]
        from_sibling = swap_sibling(parts, "rs_sibling_partial_" + group, sp)
        chip_sums = add_pairs(parts, from_sibling, sp, core, "rs_add_sibling_" + group)
        sems, sums_thru, lands, token = exchange_chips_start(chip_sums, "rs_exchange_start_" + group)
        exchanges[group] = dict(idx=idx, parts=parts, from_sibling=from_sibling, sems=sems, sums=sums_thru, lands=lands)
        return token

    sq, grad_x, g = local_step(x, mem, positions, loss_target, norm_in, weights, big_grads_ready, q_a_norm, kv_a_norm, conv_full,
                               gdn_a_log, gdn_dt_bias, gdn_norm, mem_norm, norm_final)

    small_names = ("norm_in", "q_a_norm", "kv_a_norm", "gdn_a_log", "gdn_dt_bias", "gdn_norm", "mem_norm", "norm_final")
    small = dict(norm_in=norm_in, q_a_norm=q_a_norm, kv_a_norm=kv_a_norm, gdn_a_log=gdn_a_log, gdn_dt_bias=gdn_dt_bias,
                 gdn_norm=gdn_norm, mem_norm=mem_norm, norm_final=norm_final)
    m_small = dict(norm_in=m_norm_in, q_a_norm=m_q_a_norm, kv_a_norm=m_kv_a_norm, gdn_a_log=m_gdn_a_log,
                   gdn_dt_bias=m_gdn_dt_bias, gdn_norm=m_gdn_norm, mem_norm=m_mem_norm, norm_final=m_norm_final)
    v_small = dict(norm_in=v_norm_in, q_a_norm=v_q_a_norm, kv_a_norm=v_kv_a_norm, gdn_a_log=v_gdn_a_log,
                   gdn_dt_bias=v_gdn_dt_bias, gdn_norm=v_gdn_norm, mem_norm=v_mem_norm, norm_final=v_norm_final)
    conv_rows = GDN_CONV * GDN_QKV // 1024
    g_block = pack_small([g.get(n) for n in small_names], g["gdn_a_log_dt_bias"], g["gdn_conv"], sq)
    g_block = sum_leading(allgather_devices(g_block, "allgather_small_grads"), "sum_small_grads")
    loss = 0.5 * jnp.sum(g_block[LOSS_ROW]) / D_MODEL
    g_conv = lax.dynamic_slice_in_dim(g_block[CONV_ROW0:CONV_ROW0 + conv_rows].reshape(GDN_CONV, GDN_QKV), chip * conv_cols,
                                      conv_cols, axis=1)
    as_row = lambda a: a.reshape(1, -1)
    updated = adamw_small(g_block, [as_row(small[n]) for n in small_names], [as_row(m_small[n]) for n in small_names],
                          [as_row(v_small[n]) for n in small_names])
    g_out, d_out, m_out, v_out = ({n: u[i].reshape(small[n].shape) for n, u in zip(small_names, updated)} for i in range(4))
    d_s = d_out["norm_in"]

    my_half = [None] * len(big_names)
    for group, e in exchanges.items():
        from_chips = exchange_chips_wait(e["sems"], e["sums"], e["lands"], d_s, "rs_exchange_wait_" + group)
        halves = add_fives(e["parts"], e["from_sibling"], from_chips, [splits[i] for i in e["idx"]], chip_core, "rs_add_chips_" + group)
        for i, a in zip(e["idx"], halves):
            my_half[i] = a
    other_half = swap_sibling(my_half, "rs_sibling_final")

    d_out["gdn_conv"], m_out["gdn_conv"], v_out["gdn_conv"] = adamw(gdn_conv, g_conv, m_gdn_conv, v_gdn_conv, "adamw_gdn_conv")
    g_out["gdn_conv"] = g_conv[None]
    res = adamw_untiled_rows(w_in3, my_half[0], other_half[0], m_in3, v_in3, "adamw_w_in")
    g_out["w_in"], d_out["w_in"], m_out["w_in"], v_out["w_in"] = [jnp.transpose(r, (1, 2, 0)) for r in res]
    res = adamw_w_q_b(w_qb_t, my_half[1], other_half[1], m_qb_t, v_qb_t, "adamw_w_q_b")
    g_out["w_q_b"], d_out["w_q_b"], m_out["w_q_b"], v_out["w_q_b"] = [jnp.transpose(r)[None] for r in res]
    rest = dict(w_kv_b=(w_kv_b, m_w_kv_b, v_w_kv_b), w_mem_kv=(w_mem_kv, m_w_mem_kv, v_w_mem_kv), w_out=(w_out, m_w_out, v_w_out))
    for i, n in enumerate(big_names):
        if n in rest:
            w_n, m_n, v_n = rest[n]
            g_out[n], d_out[n], m_out[n], v_out[n] = adamw_halves(w_n, my_half[i], other_half[i], m_n, v_n, splits[i], core, "adamw_" + n)

    order = ("norm_in", "w_in", "q_a_norm", "w_q_b", "kv_a_norm", "w_kv_b", "gdn_conv", "gdn_a_log", "gdn_dt_bias",
             "gdn_norm", "mem_norm", "w_mem_kv", "w_out", "norm_final")
    return (loss, grad_x, *[g_out[n] for n in order], *[d_out[n] for n in order], *[m_out[n] for n in order],
            *[v_out[n] for n in order])
```

```python
import jax
import jax.numpy as jnp
from jax import lax
from jax.experimental import pallas as pl
from jax.experimental.pallas import tpu as pltpu

F32 = jnp.float32
BF16 = jnp.bfloat16
BS = pl.BlockSpec

D_MODEL = 1024
N_HEADS = 4
MLA_NOPE, MLA_ROPE, MLA_V = 128, 64, 128
Q_LORA, KV_LORA = 384, 256
ROPE_THETA = 10000.0
GDN_DK = GDN_DV = 128
GDN_CONV = 4
CHUNK = 64
MEM_DH = 128
D_MIX = 1536
GDN_QKV = 1536
D_IN = 4296
EPS = 1e-6
ADAM_LR, ADAM_B1, ADAM_B2, ADAM_EPS, ADAM_WD, ADAM_STEP = 0.001, 0.9, 0.999, 1e-08, 0.01, 10

OFF_MLA = 0
OFF_MEMQ = 1024
OFF_GDN = 1536
OFF_GATE = 3072
N_PAD = 4608
HEAD_PAD = 256
MLA_SCALE = (MLA_NOPE + MLA_ROPE) ** -0.5
MEM_SCALE = MEM_DH ** -0.5
GDN_SCALE = GDN_DK ** -0.5
NEG = -1e30

NN = ((1,), (0,))
NT = ((1,), (1,))
TN = ((0,), (0,))


def _dot(a, b, dims):
    return lax.dot_general(a, b, (dims, ((), ())), preferred_element_type=F32)


def _bdot(spec, a, b, precision=None):
    return jnp.einsum(spec, a, b, preferred_element_type=F32, precision=precision)


def _arb(n):
    return pltpu.CompilerParams(dimension_semantics=("arbitrary",) * n)


def _sigmoid(x):
    return 1.0 / (1.0 + jnp.exp(-x))


def _softplus(z):
    return jnp.maximum(z, 0.0) + jnp.log(1.0 + jnp.exp(-jnp.abs(z)))


def _rope(t, cos_row, sin_row):
    return t * cos_row + pltpu.roll(t, 64, 1) * sin_row


def _rope_bwd(d, cos_row, sin_row):
    return d * cos_row + pltpu.roll(d * sin_row, 64, 1)


def rms_fwd(x, gain, name, tm=512, after=()):
    T, n = x.shape
    tm = min(tm, T)

    def body(x_ref, g_ref, *rest):
        xv = x_ref[...]
        r = lax.rsqrt(jnp.mean(xv * xv, axis=-1, keepdims=True) + EPS)
        rest[-1][...] = (xv * r * g_ref[...]).astype(BF16)

    return pl.pallas_call(
        body, name=name, grid=(T // tm,),
        in_specs=[BS((tm, n), lambda i: (i, 0)), BS((1, n), lambda i: (0, 0))] + [BS(memory_space=pl.ANY)] * len(after),
        out_specs=BS((tm, n), lambda i: (i, 0)),
        out_shape=jax.ShapeDtypeStruct((T, n), BF16), compiler_params=_arb(1))(x, gain, *after)


def mm(a, b, kind, out_dtype, name, bm=512, bn=None, n_outer=False, after=(), col_tiles=None, into=None):
    if kind == "nn":
        (M, K), (_, N) = a.shape, b.shape
    elif kind == "nt":
        (M, K), (N, _) = a.shape, b.shape
    else:
        (K, M), (_, N) = a.shape, b.shape
    bm, bn = min(bm, M), min(bn or N, N)
    assert M % bm == 0 and N % bn == 0, (name, M, N, K)
    lo, hi = col_tiles or (0, N // bn)
    ij = (lambda g0, g1: (g1, g0 + lo)) if n_outer else (lambda g0, g1: (g0, g1 + lo))
    a_spec = BS((K, bm), lambda g0, g1: (0, ij(g0, g1)[0])) if kind == "tn" else BS((bm, K), lambda g0, g1: (ij(g0, g1)[0], 0))
    once = dict(pipeline_mode=pl.Buffered(1)) if hi - lo == 1 else {}
    b_spec = (BS((bn, K), lambda g0, g1: (ij(g0, g1)[1], 0), **once) if kind == "nt"
              else BS((K, bn), lambda g0, g1: (0, ij(g0, g1)[1]), **once))
    dims = {"nn": NN, "nt": NT, "tn": TN}[kind]
    extra = tuple(after) + (() if into is None else (into,))

    def body(a_ref, b_ref, *rest):
        rest[-1][...] = _dot(a_ref[...].astype(BF16), b_ref[...].astype(BF16), dims).astype(out_dtype)

    grid = (hi - lo, M // bm) if n_outer else (M // bm, hi - lo)
    return pl.pallas_call(
        body, name=name, grid=grid, in_specs=[a_spec, b_spec] + [BS(memory_space=pl.ANY)] * len(extra),
        out_specs=BS((bm, bn), lambda g0, g1: ij(g0, g1)),
        input_output_aliases={} if into is None else {1 + len(extra): 0},
        out_shape=jax.ShapeDtypeStruct((M, N), out_dtype), compiler_params=_arb(2))(a, b, *extra)


def mm_cols_tn(pieces, b, out_dtype, name, bm=512):
    K, N = b.shape
    tiles = [p.shape[1] // bm for p in pieces]
    firsts = [sum(tiles[:i]) for i in range(len(tiles))]

    def body(*refs):
        b_ref, o_ref = refs[-2], refs[-1]
        i = pl.program_id(0)
        for a_ref, t0, n in zip(refs[:-2], firsts, tiles):
            @pl.when((i >= t0) & (i < t0 + n))
            def _(a_ref=a_ref):
                o_ref[...] = _dot(a_ref[...], b_ref[...], TN).astype(out_dtype)

    a_specs = [BS((K, bm), lambda i, t0=t0, n=n: (0, jnp.clip(i - t0, 0, n - 1))) for t0, n in zip(firsts, tiles)]
    return pl.pallas_call(
        body, name=name, grid=(sum(tiles),),
        in_specs=a_specs + [BS(b.shape, lambda i: (0, 0), pipeline_mode=pl.Buffered(1))],
        out_specs=BS((bm, N), lambda i: (i, 0)), out_shape=jax.ShapeDtypeStruct((sum(tiles) * bm, N), out_dtype),
        compiler_params=_arb(1))(*pieces, b)


def rope_tables(pos_col, inv_row, sgn_row, msk_row, tm=512, after=()):
    T = pos_col.shape[0]
    tm = min(tm, T)

    def body(p_ref, inv_ref, sgn_ref, msk_ref, *rest):
        c_ref, s_ref = rest[-2:]
        ang = p_ref[...].astype(F32) * inv_ref[...]
        c_ref[...] = jnp.cos(ang) * msk_ref[...]
        s_ref[...] = jnp.sin(ang) * sgn_ref[...]

    row = BS((1, 128), lambda i: (0, 0))
    return pl.pallas_call(
        body, name="rope_tables", grid=(T // tm,),
        in_specs=[BS((tm, 1), lambda i: (i, 0)), row, row, row] + [BS(memory_space=pl.ANY)] * len(after),
        out_specs=[BS((tm, 128), lambda i: (i, 0))] * 2,
        out_shape=[jax.ShapeDtypeStruct((T, 128), F32)] * 2, compiler_params=_arb(1))(pos_col, inv_row, sgn_row, msk_row, *after)


def mla_prep(P, gq, gkv, wq, wkv, cos_t, sin_t, tm=512):
    T = P.shape[0]
    tm = min(tm, T)

    def body(p_ref, gq_ref, gkv_ref, wq_ref, wkv_ref, c_ref, s_ref, q_ref, k_ref, v_ref, qn_ref, kvn_ref):
        p = p_ref[...].astype(F32)
        cq, ckv, kr = p[:, :Q_LORA], p[:, Q_LORA:Q_LORA + KV_LORA], p[:, 640:768]
        qn = (cq * lax.rsqrt(jnp.mean(cq * cq, axis=-1, keepdims=True) + EPS) * gq_ref[...]).astype(BF16)
        kvn = (ckv * lax.rsqrt(jnp.mean(ckv * ckv, axis=-1, keepdims=True) + EPS) * gkv_ref[...]).astype(BF16)
        qn_ref[...] = qn
        kvn_ref[...] = kvn
        q = _dot(qn, wq_ref[...], NT)
        kv = _dot(kvn, wkv_ref[...], NN)
        cos_row, sin_row = c_ref[...], s_ref[...]
        krr = _rope(kr, cos_row, sin_row).astype(BF16)
        for h in range(N_HEADS):
            lo = h * HEAD_PAD
            q_ref[:, lo:lo + 128] = (q[:, lo:lo + 128] * MLA_SCALE).astype(BF16)
            q_ref[:, lo + 128:lo + 256] = (_rope(q[:, lo + 128:lo + 256], cos_row, sin_row) * MLA_SCALE).astype(BF16)
            k_ref[:, lo:lo + 128] = kv[:, h * 128:(h + 1) * 128].astype(BF16)
            k_ref[:, lo + 128:lo + 256] = krr
            v_ref[:, lo:lo + 128] = kv[:, 512 + h * 128:512 + (h + 1) * 128].astype(BF16)
            v_ref[:, lo + 128:lo + 256] = jnp.ones((tm, 128), BF16)

    full = lambda r, c: BS((r, c), lambda i: (0, 0))
    rowb = lambda c: BS((tm, c), lambda i: (i, 0))
    return pl.pallas_call(
        body, name="mla_prep", grid=(T // tm,),
        in_specs=[rowb(1024), full(1, Q_LORA), full(1, KV_LORA), full(1024, Q_LORA), full(KV_LORA, 1024), rowb(128), rowb(128)],
        out_specs=[rowb(1024), rowb(1024), rowb(1024), rowb(Q_LORA), rowb(KV_LORA)],
        out_shape=[jax.ShapeDtypeStruct((T, 1024), BF16), jax.ShapeDtypeStruct((T, 1024), BF16),
                   jax.ShapeDtypeStruct((T, 1024), BF16), jax.ShapeDtypeStruct((T, Q_LORA), BF16),
                   jax.ShapeDtypeStruct((T, KV_LORA), BF16)],
        compiler_params=_arb(1))(P, gq, gkv, wq, wkv, cos_t, sin_t)


ATTN_HEADS_PER_STEP = 2
ATTN_STRIP = 32


def mla_attn_fwd(Q, K, V, B, S, tq=512, hp=ATTN_HEADS_PER_STEP):
    T = B * S
    tq = min(tq, S)
    nq = S // tq

    rs = min(ATTN_STRIP, tq)

    def body(q_ref, k_ref, v_ref, o_ref, lse_ref, m_s, acc_s, s_s, p_s, a_s):
        i = pl.program_id(2)
        m_s[...] = jnp.full_like(m_s, NEG)
        acc_s[...] = jnp.zeros_like(acc_s)

        def blk(j, masked):
            rows = pl.ds(pl.multiple_of(j * tq, tq), tq)
            for h in range(hp):
                hq = slice(h * HEAD_PAD, (h + 1) * HEAD_PAD)
                s_s[h] = _dot(q_ref[:, hq], k_ref[rows, hq], NT)
            for h in range(hp):
                for r0 in range(0, tq, rs):
                    rr = slice(r0, r0 + rs)
                    sv = s_s[h, rr, :]
                    if masked:
                        r = r0 + lax.broadcasted_iota(jnp.int32, (rs, tq), 0)
                        c = lax.broadcasted_iota(jnp.int32, (rs, tq), 1)
                        sv = jnp.where(r >= c, sv, NEG)
                    m_prev = m_s[h, rr, :]
                    m_new = jnp.maximum(m_prev, jnp.max(sv, axis=1, keepdims=True))
                    p_s[h, rr, :] = jnp.exp(sv - m_new).astype(BF16)
                    a_s[h, rr, :] = jnp.exp(m_prev - m_new)
                    m_s[h, rr, :] = m_new
            for h in range(hp):
                hq = slice(h * HEAD_PAD, (h + 1) * HEAD_PAD)
                acc_s[h] = a_s[h] * acc_s[h] + _dot(p_s[h], v_ref[rows, hq], NN)

        def loop(j, c):
            blk(j, False)
            return c

        lax.fori_loop(0, i, loop, 0)
        blk(i, True)
        for h in range(hp):
            den = acc_s[h, :, 128:256]
            o_ref[:, h * 128:(h + 1) * 128] = (acc_s[h, :, 0:128] / den).astype(BF16)
            lse_ref[h] = m_s[h] + jnp.log(den[:, 0:1])

    return pl.pallas_call(
        body, name="mla_attn_fwd", grid=(B, N_HEADS // hp, nq),
        in_specs=[BS((tq, hp * HEAD_PAD), lambda b, h, i: (b * nq + i, h)),
                  BS((S, hp * HEAD_PAD), lambda b, h, i: (b, h)),
                  BS((S, hp * HEAD_PAD), lambda b, h, i: (b, h))],
        out_specs=[BS((tq, hp * 128), lambda b, h, i: (b * nq + i, h)),
                   BS((hp, tq, 1), lambda b, h, i: (h, b * nq + i, 0))],
        out_shape=[jax.ShapeDtypeStruct((T, 512), BF16), jax.ShapeDtypeStruct((N_HEADS, T, 1), F32)],
        scratch_shapes=[pltpu.VMEM((hp, tq, 1), F32), pltpu.VMEM((hp, tq, HEAD_PAD), F32), pltpu.VMEM((hp, tq, tq), F32),
                        pltpu.VMEM((hp, tq, tq), BF16), pltpu.VMEM((hp, tq, 1), F32)],
        compiler_params=_arb(3))(Q, K, V)


def mla_attn_bwd(Q, K, V, O, dO, LSE, B, S, tq=512, hp=ATTN_HEADS_PER_STEP):
    T = B * S
    tq = min(tq, S)
    nq = S // tq

    rs = min(ATTN_STRIP, tq)

    def body(q_ref, k_ref, v_ref, o_ref, do_ref, lse_ref, dq_ref, dk_ref, dv_ref, delta_s, dq_s, dk_s, dv_s, s_s, dp_s, p_s, ds_s):
        j = pl.program_id(2)

        @pl.when(j == 0)
        def _():
            dq_s[...] = jnp.zeros_like(dq_s)
            for h in range(hp):
                sl = slice(h * 128, (h + 1) * 128)
                delta_s[h] = jnp.sum(do_ref[:, sl] * o_ref[:, sl].astype(F32), axis=1, keepdims=True)

        dk_s[...] = jnp.zeros_like(dk_s)
        dv_s[...] = jnp.zeros_like(dv_s)

        def step(i, c):
            rows = pl.ds(pl.multiple_of(i * tq, tq), tq)
            for h in range(hp):
                sq, sv = slice(h * HEAD_PAD, (h + 1) * HEAD_PAD), slice(h * 128, (h + 1) * 128)
                s_s[h] = _dot(q_ref[rows, sq], k_ref[:, sq], NT)
                dp_s[h] = _dot(do_ref[rows, sv].astype(BF16), v_ref[:, h * HEAD_PAD:h * HEAD_PAD + 128], NT)
            for h in range(hp):
                for r0 in range(0, tq, rs):
                    rr = slice(r0, r0 + rs)
                    seq_rows = pl.ds(pl.multiple_of(i * tq + r0, rs), rs)
                    r = i * tq + r0 + lax.broadcasted_iota(jnp.int32, (rs, tq), 0)
                    cc = j * tq + lax.broadcasted_iota(jnp.int32, (rs, tq), 1)
                    p = jnp.where(r >= cc, jnp.exp(s_s[h, rr, :] - lse_ref[h, seq_rows, :]), 0.0)
                    p_s[h, rr, :] = p.astype(BF16)
                    ds_s[h, rr, :] = (p * (dp_s[h, rr, :] - delta_s[h, seq_rows, :])).astype(BF16)
            for h in range(hp):
                sq, sv = slice(h * HEAD_PAD, (h + 1) * HEAD_PAD), slice(h * 128, (h + 1) * 128)
                dv_s[:, sv] += _dot(p_s[h], do_ref[rows, sv].astype(BF16), TN)
                dk_s[:, sq] += _dot(ds_s[h], q_ref[rows, sq], TN)
                dq_s[rows, sq] += _dot(ds_s[h], k_ref[:, sq], NN)
            return c

        lax.fori_loop(j, nq, step, 0)
        dk_ref[...] = dk_s[...].astype(BF16)
        dv_ref[...] = dv_s[...].astype(BF16)

        @pl.when(j == nq - 1)
        def _():
            dq_ref[...] = dq_s[...].astype(BF16)

    seq = lambda c: BS((S, c), lambda b, h, j: (b, h))
    blk = lambda c: BS((tq, c), lambda b, h, j: (b * nq + j, h))
    return pl.pallas_call(
        body, name="mla_attn_bwd", grid=(B, N_HEADS // hp, nq),
        in_specs=[seq(hp * HEAD_PAD), blk(hp * HEAD_PAD), blk(hp * HEAD_PAD), seq(hp * 128), seq(hp * 128),
                  BS((hp, S, 1), lambda b, h, j: (h, b, 0))],
        out_specs=[seq(hp * HEAD_PAD), blk(hp * HEAD_PAD), blk(hp * 128)],
        out_shape=[jax.ShapeDtypeStruct((T, 1024), BF16), jax.ShapeDtypeStruct((T, 1024), BF16),
                   jax.ShapeDtypeStruct((T, 512), BF16)],
        scratch_shapes=[pltpu.VMEM((hp, S, 1), F32), pltpu.VMEM((S, hp * HEAD_PAD), F32),
                        pltpu.VMEM((tq, hp * HEAD_PAD), F32), pltpu.VMEM((tq, hp * 128), F32),
                        pltpu.VMEM((hp, tq, tq), F32), pltpu.VMEM((hp, tq, tq), F32),
                        pltpu.VMEM((hp, tq, tq), BF16), pltpu.VMEM((hp, tq, tq), BF16)],
        compiler_params=_arb(3))(Q, K, V, O, dO, LSE)


def mla_proj_bwd(dQ, dK, dV, cos_t, sin_t, P, dab, wq, wkv, gq, gkv, tm=512):
    T = P.shape[0]
    tm = min(tm, T)

    def norm_bwd(x, dy, g):
        r = lax.rsqrt(jnp.mean(x * x, axis=-1, keepdims=True) + EPS)
        xh = x * r
        dxh = dy * g
        return r * (dxh - xh * jnp.mean(dxh * xh, axis=-1, keepdims=True)), jnp.sum(dy * xh, axis=0, keepdims=True)

    def body(dq_ref, dk_ref, dv_ref, c_ref, s_ref, p_ref, dab_ref, wq_ref, wkv_ref, gq_ref, gkv_ref,
             ql_ref, kvl_ref, o_ref, aq_ref, akv_ref):
        @pl.when(pl.program_id(0) == 0)
        def _():
            aq_ref[...] = jnp.zeros_like(aq_ref)
            akv_ref[...] = jnp.zeros_like(akv_ref)

        cos_row, sin_row = c_ref[...], s_ref[...]
        kr = jnp.zeros((tm, 128), F32)
        for h in range(N_HEADS):
            lo = h * HEAD_PAD
            ql_ref[:, lo:lo + 128] = (dq_ref[:, lo:lo + 128].astype(F32) * MLA_SCALE).astype(BF16)
            ql_ref[:, lo + 128:lo + 256] = (_rope_bwd(dq_ref[:, lo + 128:lo + 256].astype(F32), cos_row, sin_row) * MLA_SCALE).astype(BF16)
            kvl_ref[:, h * 128:(h + 1) * 128] = dk_ref[:, lo:lo + 128]
            kr = kr + dk_ref[:, lo + 128:lo + 256].astype(F32)
        kvl_ref[:, 512:] = dv_ref[...]
        dqn = _dot(ql_ref[...], wq_ref[...], NN)
        dkvn = _dot(kvl_ref[...], wkv_ref[...], NT)
        dcq, ggq = norm_bwd(p_ref[:, :Q_LORA].astype(F32), dqn, gq_ref[...])
        dckv, ggkv = norm_bwd(p_ref[:, Q_LORA:640].astype(F32), dkvn, gkv_ref[...])
        aq_ref[...] += ggq
        akv_ref[...] += ggkv
        o_ref[:, :Q_LORA] = dcq.astype(BF16)
        o_ref[:, Q_LORA:640] = dckv.astype(BF16)
        o_ref[:, 640:768] = _rope_bwd(kr, cos_row, sin_row).astype(BF16)
        o_ref[:, 768:896] = dab_ref[...]
        o_ref[:, 896:1024] = jnp.zeros((tm, 128), BF16)

    rowb = lambda c: BS((tm, c), lambda i: (i, 0))
    full = lambda r, c: BS((r, c), lambda i: (0, 0))
    return pl.pallas_call(
        body, name="mla_proj_bwd", grid=(T // tm,),
        in_specs=[rowb(1024), rowb(1024), rowb(512), rowb(128), rowb(128), rowb(1024), rowb(128),
                  full(1024, Q_LORA), full(KV_LORA, 1024), full(1, Q_LORA), full(1, KV_LORA)],
        out_specs=[rowb(1024), rowb(1024), rowb(1024), full(1, Q_LORA), full(1, KV_LORA)],
        out_shape=[jax.ShapeDtypeStruct((T, 1024), BF16)] * 3
        + [jax.ShapeDtypeStruct((1, Q_LORA), F32), jax.ShapeDtypeStruct((1, KV_LORA), F32)],
        compiler_params=_arb(1))(dQ, dK, dV, cos_t, sin_t, P, dab, wq, wkv, gq, gkv)


def _mem_probs(qh, kh):
    s = _dot(qh, kh, NT) * MEM_SCALE
    p = jnp.exp(s - jnp.max(s, axis=1, keepdims=True))
    return p / jnp.sum(p, axis=1, keepdims=True)


def mem_attn_fwd(P, MKV, B, S, M, tq=512):
    T = B * S
    tq = min(tq, S)
    nq = S // tq

    def body(q_ref, kv_ref, o_ref):
        for h in range(N_HEADS):
            sl = slice(h * 128, (h + 1) * 128)
            p = _mem_probs(q_ref[:, sl].astype(BF16), kv_ref[:, sl])
            o_ref[:, sl] = _dot(p.astype(BF16), kv_ref[:, 512 + h * 128:512 + (h + 1) * 128], NN).astype(BF16)

    return pl.pallas_call(
        body, name="mem_attn_fwd", grid=(B, nq),
        in_specs=[BS((tq, 512), lambda b, i: (b * nq + i, OFF_MEMQ // 512)), BS((M, 1024), lambda b, i: (b, 0))],
        out_specs=BS((tq, 512), lambda b, i: (b * nq + i, 0)),
        out_shape=jax.ShapeDtypeStruct((T, 512), BF16), compiler_params=_arb(2))(P, MKV)


def mem_attn_bwd(P, MKV, dO, B, S, M, tq=512):
    T = B * S
    tq = min(tq, S)
    nq = S // tq

    def body(q_ref, kv_ref, do_ref, dq_ref, dkv_ref):
        @pl.when(pl.program_id(1) == 0)
        def _():
            dkv_ref[...] = jnp.zeros_like(dkv_ref)

        for h in range(N_HEADS):
            sl = slice(h * 128, (h + 1) * 128)
            sv = slice(512 + h * 128, 512 + (h + 1) * 128)
            qh = q_ref[:, sl].astype(BF16)
            kh = kv_ref[:, sl]
            do = do_ref[:, sl].astype(BF16)
            p = _mem_probs(qh, kh)
            dkv_ref[:, sv] += _dot(p.astype(BF16), do, TN)
            dp = _dot(do, kv_ref[:, sv], NT)
            ds = (p * (dp - jnp.sum(dp * p, axis=1, keepdims=True)) * MEM_SCALE).astype(BF16)
            dq_ref[:, sl] = _dot(ds, kh, NN).astype(BF16)
            dkv_ref[:, sl] += _dot(ds, qh, TN)

    return pl.pallas_call(
        body, name="mem_attn_bwd", grid=(B, nq),
        in_specs=[BS((tq, 512), lambda b, i: (b * nq + i, OFF_MEMQ // 512)), BS((M, 1024), lambda b, i: (b, 0)),
                  BS((tq, 512), lambda b, i: (b * nq + i, 0))],
        out_specs=[BS((tq, 512), lambda b, i: (b * nq + i, 0)), BS((M, 1024), lambda b, i: (b, 0))],
        out_shape=[jax.ShapeDtypeStruct((T, 512), BF16), jax.ShapeDtypeStruct((B * M, 1024), F32)],
        compiler_params=_arb(2))(P, MKV, dO)


def gain_grad(x, dy, name, tm=256):
    T, n = x.shape
    tm = min(tm, T)

    def body(x_ref, dy_ref, o_ref):
        @pl.when(pl.program_id(0) == 0)
        def _():
            o_ref[...] = jnp.zeros_like(o_ref)

        xv = x_ref[...]
        xh = xv * lax.rsqrt(jnp.mean(xv * xv, axis=-1, keepdims=True) + EPS)
        o_ref[...] += jnp.sum(dy_ref[...] * xh, axis=0, keepdims=True)

    return pl.pallas_call(
        body, name=name, grid=(T // tm,),
        in_specs=[BS((tm, n), lambda i: (i, 0))] * 2, out_specs=BS((1, n), lambda i: (0, 0)),
        out_shape=jax.ShapeDtypeStruct((1, n), F32), compiler_params=_arb(1))(x, dy)


def _conv_silu(x, w, t):
    y = x * w[3:4, :]
    for s in range(1, GDN_CONV):
        y = y + jnp.where(t >= s, pltpu.roll(x, s, 0), 0.0) * w[3 - s:4 - s, :]
    return y, _sigmoid(y)


def gdn_prep_fwd(P, conv_w, B, S):
    T = B * S

    def body(x_ref, w_ref, o_ref):
        kind = pl.program_id(1)
        t = lax.broadcasted_iota(jnp.int32, (S, 1), 0)
        y, sg = _conv_silu(x_ref[...].astype(F32), w_ref[...], t)
        a = y * sg
        scale = jnp.where(kind == 0, GDN_SCALE, 1.0).astype(F32)
        for h in range(N_HEADS):
            sl = slice(h * 128, (h + 1) * 128)
            seg = a[:, sl]
            n = lax.rsqrt(jnp.sum(seg * seg, axis=-1, keepdims=True) + EPS)
            o_ref[:, sl] = jnp.where(kind < 2, seg * (n * scale), seg)

    return pl.pallas_call(
        body, name="gdn_prep_fwd", grid=(B, 3),
        in_specs=[BS((S, 512), lambda b, k: (b, OFF_GDN // 512 + k)), BS((GDN_CONV, 512), lambda b, k: (0, k))],
        out_specs=BS((S, 512), lambda b, k: (b, k)),
        out_shape=jax.ShapeDtypeStruct((T, GDN_QKV), F32), compiler_params=_arb(2))(P, conv_w)


def gdn_prep_bwd(P, dqkv, conv_w, B, S):
    T = B * S

    def body(x_ref, d_ref, w_ref, o_ref, gw_ref):
        kind = pl.program_id(0)

        @pl.when(pl.program_id(1) == 0)
        def _():
            gw_ref[...] = jnp.zeros_like(gw_ref)

        t = lax.broadcasted_iota(jnp.int32, (S, 1), 0)
        x = x_ref[...].astype(F32)
        w = w_ref[...]
        y, sg = _conv_silu(x, w, t)
        a = y * sg
        scale = jnp.where(kind == 0, GDN_SCALE, 1.0).astype(F32)
        das = []
        for h in range(N_HEADS):
            sl = slice(h * 128, (h + 1) * 128)
            seg, dseg = a[:, sl], d_ref[:, sl]
            n = lax.rsqrt(jnp.sum(seg * seg, axis=-1, keepdims=True) + EPS)
            dn = scale * (n * dseg - seg * (n * n * n) * jnp.sum(dseg * seg, axis=-1, keepdims=True))
            das.append(jnp.where(kind < 2, dn, dseg))
        dy = jnp.concatenate(das, axis=1) * (sg * (1.0 + y * (1.0 - sg)))
        dx = dy * w[3:4, :]
        gw_ref[3:4, :] += jnp.sum(dy * x, axis=0, keepdims=True)
        for s in range(1, GDN_CONV):
            dx = dx + jnp.where(t + s < S, pltpu.roll(dy, S - s, 0), 0.0) * w[3 - s:4 - s, :]
            gw_ref[3 - s:4 - s, :] += jnp.sum(dy * jnp.where(t >= s, pltpu.roll(x, s, 0), 0.0), axis=0, keepdims=True)
        o_ref[...] = dx.astype(BF16)

    return pl.pallas_call(
        body, name="gdn_prep_bwd", grid=(3, B),
        in_specs=[BS((S, 512), lambda k, b: (b, OFF_GDN // 512 + k)), BS((S, 512), lambda k, b: (b, k)),
                  BS((GDN_CONV, 512), lambda k, b: (0, k))],
        out_specs=[BS((S, 512), lambda k, b: (b, k)), BS((GDN_CONV, 512), lambda k, b: (0, k))],
        out_shape=[jax.ShapeDtypeStruct((T, GDN_QKV), BF16), jax.ShapeDtypeStruct((GDN_CONV, GDN_QKV), F32)],
        compiler_params=_arb(2))(P, dqkv, conv_w)


def _chunk_row(n_rows):
    return lax.broadcasted_iota(jnp.int32, (n_rows, 1), 0) % CHUNK


def gdn_gate_fwd(P, alog_row, dt_row, B, S):
    T = B * S

    def body(x_ref, al_ref, dt_ref, o_ref):
        x = x_ref[...].astype(F32)
        lane = lax.broadcasted_iota(jnp.int32, (1, 128), 1)
        g = jnp.where(lane < 4, -jnp.exp(al_ref[...]) * _softplus(x + dt_ref[...]), 0.0)
        t = _chunk_row(S)
        for s in (1, 2, 4, 8, 16, 32):
            g = g + jnp.where(t >= s, pltpu.roll(g, s, 0), 0.0)
        o_ref[...] = jnp.where(lane < 4, g, jnp.where(lane < 8, _sigmoid(x), 0.0))

    row = BS((1, 128), lambda b: (0, 0))
    return pl.pallas_call(
        body, name="gdn_gate_fwd", grid=(B,),
        in_specs=[BS((S, 128), lambda b: (b, 768 // 128)), row, row], out_specs=BS((S, 128), lambda b: (b, 0)),
        out_shape=jax.ShapeDtypeStruct((T, 128), F32), compiler_params=_arb(1))(P, alog_row, dt_row)


def gdn_gate_bwd(P, dGB, alog_row, dt_row, B, S):
    T = B * S

    def body(x_ref, d_ref, al_ref, dt_ref, o_ref, acc_ref):
        @pl.when(pl.program_id(0) == 0)
        def _():
            acc_ref[...] = jnp.zeros_like(acc_ref)

        x, d = x_ref[...].astype(F32), d_ref[...]
        lane = lax.broadcasted_iota(jnp.int32, (1, 128), 1)
        z = x + dt_ref[...]
        coef = -jnp.exp(al_ref[...])
        g = coef * _softplus(z)
        da = jnp.where(lane < 4, d * coef * _sigmoid(z), 0.0)
        beta = _sigmoid(x)
        o_ref[...] = jnp.where(lane < 4, da, jnp.where(lane < 8, d * beta * (1.0 - beta), 0.0)).astype(BF16)
        acc_ref[0:1, :] += jnp.sum(jnp.where(lane < 4, d * g, 0.0), axis=0, keepdims=True)
        acc_ref[1:2, :] += jnp.sum(da, axis=0, keepdims=True)

    row = BS((1, 128), lambda b: (0, 0))
    return pl.pallas_call(
        body, name="gdn_gate_bwd", grid=(B,),
        in_specs=[BS((S, 128), lambda b: (b, 768 // 128)), BS((S, 128), lambda b: (b, 0)), row, row],
        out_specs=[BS((S, 128), lambda b: (b, 0)), BS((8, 128), lambda b: (0, 0))],
        out_shape=[jax.ShapeDtypeStruct((T, 128), BF16), jax.ShapeDtypeStruct((8, 128), F32)],
        compiler_params=_arb(1))(P, dGB, alog_row, dt_row)


def _chunk_masks(nc):
    r = lax.broadcasted_iota(jnp.int32, (nc, CHUNK, CHUNK), 1)
    c = lax.broadcasted_iota(jnp.int32, (nc, CHUNK, CHUNK), 2)
    return r >= c, r > c


def _chunk_local(q, k, gc, gr, beta, incl, strict):
    decay = jnp.exp(jnp.where(incl, gc - gr, NEG))
    kb = k * beta
    kbf = k.astype(BF16)
    m_kk = _bdot("gcd,gjd->gcj", kb.astype(BF16), kbf)
    l_mat = jnp.where(strict, m_kk * decay, 0.0)
    a_mat = _bdot("gcd,gjd->gcj", q.astype(BF16), kbf) * decay
    return decay, kb, l_mat, a_mat


WY_SPLIT_LEVELS = 2


def _split_bf16(x):
    hi = x.astype(BF16)
    return hi, (x - hi.astype(F32)).astype(BF16)


def _mm_split(ah, al, bh, bl):
    spec = "gij,gjk->gik"
    return _bdot(spec, ah, bh) + (_bdot(spec, ah, bl) + _bdot(spec, al, bh))


def gdn_chunk_fwd(qkv, GB, Grow, B, S, nc=8):
    T = B * S
    N = S // CHUNK
    nc = min(nc, N)
    nb = N // nc
    R = nc * CHUNK

    def body(q_ref, k_ref, v_ref, gb_ref, gr_ref, u_ref, w_ref, t_ref, a_ref):
        incl, strict = _chunk_masks(nc)
        eye = (lax.broadcasted_iota(jnp.int32, (nc, CHUNK, CHUNK), 1)
               == lax.broadcasted_iota(jnp.int32, (nc, CHUNK, CHUNK), 2)).astype(F32)
        for h in range(N_HEADS):
            sl = slice(h * 128, (h + 1) * 128)
            q = q_ref[:, sl].reshape(nc, CHUNK, 128)
            k = k_ref[:, sl].reshape(nc, CHUNK, 128)
            v = v_ref[:, sl].reshape(nc, CHUNK, 128)
            gc = gb_ref[:, h:h + 1].reshape(nc, CHUNK, 1)
            beta = gb_ref[:, 4 + h:5 + h].reshape(nc, CHUNK, 1)
            gr = gr_ref[h][:, None, :]
            _, kb, l_mat, a_mat = _chunk_local(q, k, gc, gr, beta, incl, strict)
            pw = -l_mat
            tinv = eye + pw
            for level in range(5):
                if level < WY_SPLIT_LEVELS:
                    ph, pl_ = _split_bf16(pw)
                    pw = _mm_split(ph, pl_, ph, pl_)
                    ph, pl_ = _split_bf16(pw)
                    th, tl = _split_bf16(tinv)
                    tinv = tinv + _mm_split(th, tl, ph, pl_)
                else:
                    ph = pw.astype(BF16)
                    pw = _bdot("gij,gjk->gik", ph, ph)
                    tinv = tinv + _bdot("gij,gjk->gik", tinv.astype(BF16), pw.astype(BF16))
            tb = tinv.astype(BF16)
            u = _bdot("gcj,gjv->gcv", tb, (v * beta).astype(BF16))
            w = _bdot("gcj,gjk->gck", tb, (kb * jnp.exp(gc)).astype(BF16))
            u_ref[:, sl] = u.reshape(R, 128)
            w_ref[:, sl] = w.reshape(R, 128).astype(BF16)
            t_ref[h] = jnp.swapaxes(tinv, 1, 2).astype(BF16)
            a_ref[h] = a_mat.astype(BF16)

    rowb = lambda c, j: BS((R, c), lambda b, n: (b * nb + n, j))
    mat = BS((None, N_HEADS, nc, CHUNK, CHUNK), lambda b, n: (b, 0, n, 0, 0))
    return pl.pallas_call(
        body, name="gdn_chunk_fwd", grid=(B, nb),
        in_specs=[rowb(512, 0), rowb(512, 1), rowb(512, 2), rowb(128, 0),
                  BS((None, N_HEADS, nc, CHUNK), lambda b, n: (b, 0, n, 0))],
        out_specs=[rowb(512, 0), rowb(512, 0), mat, mat],
        out_shape=[jax.ShapeDtypeStruct((T, 512), F32), jax.ShapeDtypeStruct((T, 512), BF16),
                   jax.ShapeDtypeStruct((B, N_HEADS, N, CHUNK, CHUNK), BF16),
                   jax.ShapeDtypeStruct((B, N_HEADS, N, CHUNK, CHUNK), BF16)],
        compiler_params=_arb(2))(qkv, qkv, qkv, GB, Grow)


SCAN_CHUNKS = 4


def gdn_scan_fwd(qkv3, U3, W3, GB3, A, B, S):
    N = S // CHUNK
    cps = SCAN_CHUNKS if N % SCAN_CHUNKS == 0 else 1

    def body(q_ref, k_ref, u_ref, w_ref, gb_ref, a_ref, o_ref, vn_ref, st_ref, s_s):
        @pl.when(pl.program_id(0) == 0)
        def _():
            s_s[...] = jnp.zeros_like(s_s)

        for c in range(cps):
            rows = slice(c * CHUNK, (c + 1) * CHUNK)
            for b in range(B):
                for h in range(N_HEADS):
                    sl = slice(h * 128, (h + 1) * 128)
                    st = s_s[b, h]
                    st_ref[b, h, c] = st
                    stb = st.astype(BF16)
                    g = gb_ref[b, rows, h:h + 1]
                    gl = g[CHUNK - 1:CHUNK, :]
                    qg = (q_ref[b, rows, sl] * jnp.exp(g)).astype(BF16)
                    on_state = _dot(jnp.concatenate([w_ref[b, rows, sl].astype(BF16), qg], axis=0), stb, NN)
                    vn = u_ref[b, rows, sl] - on_state[:CHUNK]
                    vnb = vn.astype(BF16)
                    kd_t = jnp.transpose(k_ref[b, rows, sl] * jnp.exp(gl - g)).astype(BF16)
                    on_vn = _dot(jnp.concatenate([a_ref[b, h, c].astype(BF16), kd_t], axis=0), vnb, NN)
                    vn_ref[b, rows, sl] = vnb
                    o_ref[b, rows, sl] = (on_state[CHUNK:] + on_vn[:CHUNK]).astype(BF16)
                    s_s[b, h] = st * jnp.exp(gl) + on_vn[CHUNK:]

    tok = lambda c, j: BS((B, cps * CHUNK, c), lambda n: (0, n, j))
    return pl.pallas_call(
        body, name="gdn_scan_fwd", grid=(N // cps,),
        in_specs=[tok(512, 0), tok(512, 1), tok(512, 0), tok(512, 0), tok(128, 0),
                  BS((B, N_HEADS, cps, CHUNK, CHUNK), lambda n: (0, 0, n, 0, 0))],
        out_specs=[tok(512, 0), tok(512, 0), BS((B, N_HEADS, cps, 128, 128), lambda n: (0, 0, n, 0, 0))],
        out_shape=[jax.ShapeDtypeStruct((B, S, 512), BF16), jax.ShapeDtypeStruct((B, S, 512), BF16),
                   jax.ShapeDtypeStruct((B, N_HEADS, N, 128, 128), F32)],
        scratch_shapes=[pltpu.VMEM((B, N_HEADS, 128, 128), F32)],
        compiler_params=_arb(1))(qkv3, qkv3, U3, W3, GB3, A)


def gdn_scan_bwd(dO3, qkv3, W3, Vn3, GB3, A, St, B, S):
    N = S // CHUNK
    cps = SCAN_CHUNKS if N % SCAN_CHUNKS == 0 else 1

    def body(do_ref, q_ref, k_ref, w_ref, vn_ref, gb_ref, a_ref, st_ref,
             du_ref, dw_ref, dq_ref, dk_ref, da_ref, dg_ref, ds_s):
        @pl.when(pl.program_id(0) == 0)
        def _():
            ds_s[...] = jnp.zeros_like(ds_s)

        lane = lax.broadcasted_iota(jnp.int32, (1, 128), 1)
        last = lax.broadcasted_iota(jnp.int32, (CHUNK, 1), 0) == CHUNK - 1
        for c in reversed(range(cps)):
            rows = slice(c * CHUNK, (c + 1) * CHUNK)
            for b in range(B):
                dg_all = jnp.zeros((CHUNK, 128), F32)
                for h in range(N_HEADS):
                    sl = slice(h * 128, (h + 1) * 128)
                    st = st_ref[b, h, c]
                    stb = st.astype(BF16)
                    dsn = ds_s[b, h]
                    dsnb = dsn.astype(BF16)
                    g = gb_ref[b, rows, h:h + 1]
                    gl = g[CHUNK - 1:CHUNK, :]
                    egl = jnp.exp(gl)
                    ekd = jnp.exp(gl - g)
                    eg = jnp.exp(g)
                    q, k = q_ref[b, rows, sl], k_ref[b, rows, sl]
                    kd = k * ekd
                    qg = q * eg
                    do = do_ref[b, rows, sl].astype(BF16)
                    vnb = vn_ref[b, rows, sl].astype(BF16)
                    dvn = _dot(a_ref[b, h, c].astype(BF16), do, TN) + _dot(kd.astype(BF16), dsnb, NN)
                    dvnb = dvn.astype(BF16)
                    do_on = _dot(do, jnp.concatenate([stb, vnb], axis=0), NT)
                    dqg = do_on[:, :128]
                    da_ref[b, h, c] = do_on[:, 128:]
                    dkd = _dot(vnb, dsnb, NT)
                    ds_s[b, h] = (_dot(qg.astype(BF16), do, TN) + egl * dsn - _dot(w_ref[b, rows, sl].astype(BF16), dvnb, TN))
                    du_ref[b, rows, sl] = dvnb
                    dw_ref[b, rows, sl] = (-_dot(dvnb, stb, NT)).astype(BF16)
                    dq_ref[b, rows, sl] = dqg * eg
                    dk_ref[b, rows, sl] = dkd * ekd
                    ddel = jnp.sum(dkd * kd, axis=1, keepdims=True)
                    dgl = jnp.sum(ddel, axis=0, keepdims=True) + jnp.sum(jnp.sum(st * dsn, axis=1, keepdims=True), axis=0, keepdims=True) * egl
                    col = jnp.sum(dqg * qg, axis=1, keepdims=True) - ddel + jnp.where(last, dgl, 0.0)
                    dg_all = jnp.where(lane == h, col, dg_all)
                dg_ref[b, rows, :] = dg_all

    steps = N // cps
    tok = lambda c, j: BS((B, cps * CHUNK, c), lambda n: (0, steps - 1 - n, j))
    mat = lambda d: BS((B, N_HEADS, cps, d, d), lambda n: (0, 0, steps - 1 - n, 0, 0))
    return pl.pallas_call(
        body, name="gdn_scan_bwd", grid=(steps,),
        in_specs=[tok(512, 0), tok(512, 0), tok(512, 1), tok(512, 0), tok(512, 0), tok(128, 0), mat(CHUNK), mat(128)],
        out_specs=[tok(512, 0), tok(512, 0), tok(512, 0), tok(512, 0), mat(CHUNK), tok(128, 0)],
        out_shape=[jax.ShapeDtypeStruct((B, S, 512), BF16)] * 2 + [jax.ShapeDtypeStruct((B, S, 512), F32)] * 2
        + [jax.ShapeDtypeStruct((B, N_HEADS, N, CHUNK, CHUNK), F32), jax.ShapeDtypeStruct((B, S, 128), F32)],
        scratch_shapes=[pltpu.VMEM((B, N_HEADS, 128, 128), F32)],
        compiler_params=_arb(1))(dO3, qkv3, qkv3, W3, Vn3, GB3, A, St)


def gdn_chunk_bwd(qkv, GB, Grow, Tinv, dA, dU, dW, dQ1, dK1, dG1, B, S, nc=8):
    T = B * S
    N = S // CHUNK
    nc = min(nc, N)
    nb = N // nc
    R = nc * CHUNK

    def body(q_ref, k_ref, v_ref, gb_ref, gr_ref, t_ref, da_ref, du_ref, dw_ref, dq1_ref, dk1_ref, dg1_ref, o_ref, dgb_ref):
        incl, strict = _chunk_masks(nc)
        lane = lax.broadcasted_iota(jnp.int32, (1, 128), 1)
        dg_all = dg1_ref[...]
        db_all = jnp.zeros((R, 128), F32)
        for h in range(N_HEADS):
            sl = slice(h * 128, (h + 1) * 128)
            q = q_ref[:, sl].reshape(nc, CHUNK, 128)
            k = k_ref[:, sl].reshape(nc, CHUNK, 128)
            v = v_ref[:, sl].reshape(nc, CHUNK, 128)
            gc = gb_ref[:, h:h + 1].reshape(nc, CHUNK, 1)
            beta = gb_ref[:, 4 + h:5 + h].reshape(nc, CHUNK, 1)
            gr = gr_ref[h][:, None, :]
            decay, kb, l_mat, a_mat = _chunk_local(q, k, gc, gr, beta, incl, strict)
            eg = jnp.exp(gc)
            kbg = kb * eg
            vb = v * beta
            tt = t_ref[h].astype(BF16)
            du = du_ref[:, sl].reshape(nc, CHUNK, 128).astype(BF16)
            dw = dw_ref[:, sl].reshape(nc, CHUNK, 128).astype(BF16)
            dvb = _bdot("gjc,gcv->gjv", tt, du)
            dkbg = _bdot("gjc,gck->gjk", tt, dw)
            dt = _bdot("gcv,gjv->gcj", du, vb.astype(BF16)) + _bdot("gck,gjk->gcj", dw, kbg.astype(BF16))
            tmp = _bdot("gca,gab->gcb", tt, dt.astype(BF16))
            dl = jnp.where(strict, -_bdot("gcb,gbd->gcd", tmp.astype(BF16), tt), 0.0)
            da = da_ref[h]
            dm = (dl * decay).astype(BF16)
            dqk = (da * decay).astype(BF16)
            kbf = k.astype(BF16)
            dkb = _bdot("gcj,gjd->gcd", dm, kbf) + dkbg * eg
            dk = (_bdot("gcj,gcd->gjd", dm, kb.astype(BF16)) + _bdot("gcj,gcd->gjd", dqk, q.astype(BF16))
                  + dk1_ref[:, sl].reshape(nc, CHUNK, 128) + dkb * beta)
            dq = _bdot("gcj,gjd->gcd", dqk, kbf) + dq1_ref[:, sl].reshape(nc, CHUNK, 128)
            e = dl * l_mat + da * a_mat
            dgc = (jnp.sum(e, axis=2, keepdims=True) - jnp.sum(jnp.swapaxes(e, 1, 2), axis=2, keepdims=True)
                   + jnp.sum(dkbg * kbg, axis=2, keepdims=True))
            dbeta = jnp.sum(dkb * k, axis=2, keepdims=True) + jnp.sum(dvb * v, axis=2, keepdims=True)
            o_ref[:, sl] = dq.reshape(R, 128)
            o_ref[:, 512 + h * 128:512 + (h + 1) * 128] = dk.reshape(R, 128)
            o_ref[:, 1024 + h * 128:1024 + (h + 1) * 128] = (dvb * beta).reshape(R, 128)
            dg_all = dg_all + jnp.where(lane == h, dgc.reshape(R, 1), 0.0)
            db_all = jnp.where(lane == 4 + h, dbeta.reshape(R, 1), db_all)
        t = _chunk_row(R)
        for s in (1, 2, 4, 8, 16, 32):
            dg_all = dg_all + jnp.where(t + s < CHUNK, pltpu.roll(dg_all, R - s, 0), 0.0)
        dgb_ref[...] = jnp.where(lane < 4, dg_all, db_all)

    rowb = lambda c, j: BS((R, c), lambda b, n: (b * nb + n, j))
    mat = BS((None, N_HEADS, nc, CHUNK, CHUNK), lambda b, n: (b, 0, n, 0, 0))
    return pl.pallas_call(
        body, name="gdn_chunk_bwd", grid=(B, nb),
        in_specs=[rowb(512, 0), rowb(512, 1), rowb(512, 2), rowb(128, 0),
                  BS((None, N_HEADS, nc, CHUNK), lambda b, n: (b, 0, n, 0)), mat, mat,
                  rowb(512, 0), rowb(512, 0), rowb(512, 0), rowb(512, 0), rowb(128, 0)],
        out_specs=[rowb(GDN_QKV, 0), rowb(128, 0)],
        out_shape=[jax.ShapeDtypeStruct((T, GDN_QKV), F32), jax.ShapeDtypeStruct((T, 128), F32)],
        compiler_params=_arb(2))(qkv, qkv, qkv, GB, Grow, Tinv, dA, dU, dW, dQ1, dK1, dG1)


def _gdn_out_norm(og, gg):
    outs, xhs, rs = [], [], []
    for h in range(N_HEADS):
        seg = og[:, h * 128:(h + 1) * 128]
        r = lax.rsqrt(jnp.mean(seg * seg, axis=-1, keepdims=True) + EPS)
        xh = seg * r
        outs.append(xh * gg)
        xhs.append(xh)
        rs.append(r)
    return outs, xhs, rs


def merge_fwd(o_mla, o_gdn, o_mem, P, x, tgt, w_out, g_gdn, g_fin, tm=512):
    T = x.shape[0]
    tm = min(tm, T)

    def body(om_ref, og_ref, oc_ref, gate_ref, x_ref, t_ref, w_ref, gg_ref, gf_ref, mix_ref, dx_ref, dxb_ref, sq_ref, gnf_ref):
        @pl.when(pl.program_id(0) == 0)
        def _():
            sq_ref[...] = jnp.zeros_like(sq_ref)
            gnf_ref[...] = jnp.zeros_like(gnf_ref)

        ogn, _, _ = _gdn_out_norm(og_ref[...].astype(F32), gg_ref[...])
        cat = jnp.concatenate([om_ref[...].astype(F32)] + ogn + [oc_ref[...].astype(F32)], axis=1)
        gt = gate_ref[...].astype(F32)
        mixed = (cat * (gt * _sigmoid(gt))).astype(BF16)
        mix_ref[...] = mixed
        x2 = x_ref[...] + _dot(mixed, w_ref[...], NN)
        r2 = lax.rsqrt(jnp.mean(x2 * x2, axis=-1, keepdims=True) + EPS)
        xh = x2 * r2
        gf = gf_ref[...]
        diff = xh * gf - t_ref[...]
        sq_ref[...] += jnp.sum(diff * diff, axis=0, keepdims=True)
        dy = diff * (1.0 / D_MODEL)
        gnf_ref[...] += jnp.sum(dy * xh, axis=0, keepdims=True)
        dxh = dy * gf
        dx = r2 * (dxh - xh * jnp.mean(dxh * xh, axis=-1, keepdims=True))
        dx_ref[...] = dx
        dxb_ref[...] = dx.astype(BF16)

    rowb = lambda c, j=0: BS((tm, c), lambda i: (i, j))
    full = lambda r, c: BS((r, c), lambda i: (0, 0))
    return pl.pallas_call(
        body, name="merge_fwd", grid=(T // tm,),
        in_specs=[rowb(512), rowb(512), rowb(512), rowb(D_MIX, OFF_GATE // D_MIX), rowb(D_MODEL), rowb(D_MODEL),
                  full(D_MIX, D_MODEL), full(1, 128), full(1, D_MODEL)],
        out_specs=[rowb(D_MIX), rowb(D_MODEL), rowb(D_MODEL), full(1, D_MODEL), full(1, D_MODEL)],
        out_shape=[jax.ShapeDtypeStruct((T, D_MIX), BF16), jax.ShapeDtypeStruct((T, D_MODEL), F32),
                   jax.ShapeDtypeStruct((T, D_MODEL), BF16),
                   jax.ShapeDtypeStruct((1, D_MODEL), F32), jax.ShapeDtypeStruct((1, D_MODEL), F32)],
        compiler_params=_arb(1))(o_mla, o_gdn, o_mem, P, x, tgt, w_out, g_gdn, g_fin)


def merge_bwd(dx2, o_mla, o_gdn, o_mem, P, w_out, g_gdn, tm=512):
    T = dx2.shape[0]
    tm = min(tm, T)

    def body(dx_ref, om_ref, og_ref, oc_ref, gate_ref, w_ref, gg_ref, dgate_ref, dom_ref, dog_ref, doc_ref, ggn_ref):
        @pl.when(pl.program_id(0) == 0)
        def _():
            ggn_ref[...] = jnp.zeros_like(ggn_ref)

        gg = gg_ref[...]
        dmix = _dot(dx_ref[...].astype(BF16), w_ref[...], NT)
        ogn, xhs, rs = _gdn_out_norm(og_ref[...].astype(F32), gg)
        cat = jnp.concatenate([om_ref[...].astype(F32)] + ogn + [oc_ref[...].astype(F32)], axis=1)
        gt = gate_ref[...].astype(F32)
        sg = _sigmoid(gt)
        dgate_ref[...] = (dmix * cat * (sg * (1.0 + gt * (1.0 - sg)))).astype(BF16)
        dcat = dmix * (gt * sg)
        dom_ref[...] = dcat[:, :512].astype(BF16)
        doc_ref[...] = dcat[:, 1024:].astype(BF16)
        acc = jnp.zeros((1, 128), F32)
        for h in range(N_HEADS):
            dseg = dcat[:, 512 + h * 128:512 + (h + 1) * 128]
            acc = acc + jnp.sum(dseg * xhs[h], axis=0, keepdims=True)
            dxh = dseg * gg
            dog_ref[:, h * 128:(h + 1) * 128] = (rs[h] * (dxh - xhs[h] * jnp.mean(dxh * xhs[h], axis=-1, keepdims=True))).astype(BF16)
        ggn_ref[...] += acc

    rowb = lambda c, j=0: BS((tm, c), lambda i: (i, j))
    full = lambda r, c: BS((r, c), lambda i: (0, 0))
    return pl.pallas_call(
        body, name="merge_bwd", grid=(T // tm,),
        in_specs=[rowb(D_MODEL), rowb(512), rowb(512), rowb(512), rowb(D_MIX, OFF_GATE // D_MIX),
                  full(D_MIX, D_MODEL), full(1, 128)],
        out_specs=[rowb(D_MIX), rowb(512), rowb(512), rowb(512), full(1, 128)],
        out_shape=[jax.ShapeDtypeStruct((T, D_MIX), BF16)] + [jax.ShapeDtypeStruct((T, 512), BF16)] * 3
        + [jax.ShapeDtypeStruct((1, 128), F32)],
        compiler_params=_arb(1))(dx2, o_mla, o_gdn, o_mem, P, w_out, g_gdn)


def in_proj_bwd(dP, wp, x, dx2, gain, after, tm=512):
    T, n = x.shape
    tm = min(tm, T)
    k = len(dP)
    widths = [p.shape[1] for p in dP]
    offs = [sum(widths[:i]) for i in range(k)]

    def body(*refs):
        w_ref, x_ref, dx2_ref, g_ref = refs[k:k + 4]
        o_ref, acc_ref = refs[-2:]

        @pl.when(pl.program_id(0) == 0)
        def _():
            acc_ref[...] = jnp.zeros_like(acc_ref)

        dy = None
        for a_ref, off, w in zip(refs[:k], offs, widths):
            d = _dot(a_ref[...], w_ref[off:off + w, :], NN)
            dy = d if dy is None else dy + d
        xv = x_ref[...]
        r = lax.rsqrt(jnp.mean(xv * xv, axis=-1, keepdims=True) + EPS)
        xh = xv * r
        acc_ref[...] += jnp.sum(dy * xh, axis=0, keepdims=True)
        dxh = dy * g_ref[...]
        o_ref[...] = dx2_ref[...] + r * (dxh - xh * jnp.mean(dxh * xh, axis=-1, keepdims=True))

    rowb = BS((tm, n), lambda i: (i, 0))
    full = BS((1, n), lambda i: (0, 0))
    return pl.pallas_call(
        body, name="in_proj_bwd", grid=(T // tm,),
        in_specs=[BS((tm, w), lambda i: (i, 0)) for w in widths]
        + [BS(wp.shape, lambda i: (0, 0), pipeline_mode=pl.Buffered(1)), rowb, rowb, full, BS(memory_space=pl.ANY)],
        out_specs=[rowb, full], out_shape=[jax.ShapeDtypeStruct((T, n), F32), jax.ShapeDtypeStruct((1, n), F32)],
        compiler_params=_arb(1))(*dP, wp, x, dx2, gain, after)


W_IN_SHARD = D_IN // 4
_GDN0 = Q_LORA + KV_LORA + MLA_ROPE
_AB0 = _GDN0 + GDN_QKV
_MEMQ0 = _AB0 + 2 * N_HEADS
_GATE0 = _MEMQ0 + N_HEADS * MEM_DH


def _w_in_row_map():
    a, m, gt = _AB0 - 2 * W_IN_SHARD, _MEMQ0 - 2 * W_IN_SHARD, _GATE0 - 2 * W_IN_SHARD
    e0 = OFF_GDN + W_IN_SHARD - _GDN0
    e1 = e0 + W_IN_SHARD
    e2 = OFF_GATE + W_IN_SHARD - gt
    return [(0, 0, 0, 672), (0, 672, 704, 32), (2, a, 768, m - a), (2, m, OFF_MEMQ, gt - m), (0, _GDN0, OFF_GDN, W_IN_SHARD - _GDN0),
            (1, 0, e0, W_IN_SHARD), (2, 0, e1, a), (2, gt, OFF_GATE, W_IN_SHARD - gt), (3, 0, e2, W_IN_SHARD)]


_W_IN_ZERO_ROWS = [(672, 32), (736, 32), (776, 248)]
W_IN_LANES = 256


EARLY_ROWS = 704
EARLY_ROW0 = (0, 80)


def pad_w_in_t_early(early):
    per_half = early.shape[3] // W_IN_LANES

    def body(s_ref, o_ref):
        for r0, n in _W_IN_ZERO_ROWS:
            o_ref[r0:r0 + n, :] = jnp.zeros((n, W_IN_LANES), o_ref.dtype)
        for q, src, dst, n in _w_in_row_map():
            if dst < OFF_GDN:
                first = src - EARLY_ROW0[q // 2]
                o_ref[dst:dst + n, :] = s_ref[q // 2, first:first + n, :]

    return pl.pallas_call(
        body, name="pad_w_in_t_early", grid=(D_MODEL // W_IN_LANES,),
        in_specs=[BS((2, None, EARLY_ROWS, W_IN_LANES), lambda j: (0, j // per_half, 0, j % per_half))],
        out_specs=BS((OFF_GDN, W_IN_LANES), lambda j: (0, j)),
        out_shape=jax.ShapeDtypeStruct((N_PAD, D_MODEL), early.dtype), compiler_params=_arb(1))(early)


def pad_w_in_t_rest(shards, wp):
    per_half = shards.shape[3] // W_IN_LANES
    blocks = (OFF_GDN, OFF_GATE)
    assert OFF_GATE - OFF_GDN == OFF_GDN and N_PAD - OFF_GATE == OFF_GDN

    def body(s_ref, w_in, o_ref):
        for i, b0 in enumerate(blocks):
            @pl.when(pl.program_id(1) == i)
            def _(b0=b0):
                for q, src, dst, n in _w_in_row_map():
                    if b0 <= dst < b0 + OFF_GDN:
                        o_ref[dst - b0:dst - b0 + n, :] = s_ref[q, src:src + n, :]

    return pl.pallas_call(
        body, name="pad_w_in_t_rest", grid=(D_MODEL // W_IN_LANES, len(blocks)),
        in_specs=[BS((N_CHIPS, None, W_IN_SHARD, W_IN_LANES), lambda j, i: (0, j // per_half, 0, j % per_half)), ANY],
        out_specs=BS((OFF_GDN, W_IN_LANES), lambda j, i: (1 + i, j)), input_output_aliases={1: 0},
        out_shape=jax.ShapeDtypeStruct((N_PAD, D_MODEL), shards.dtype), compiler_params=_arb(2))(shards, wp)


def unpad_w_in_t(g):
    def body(g_ref, o_ref):
        for q, src, dst, n in _w_in_row_map():
            o_ref[q, src:src + n, :] = g_ref[dst:dst + n, :]

    return pl.pallas_call(
        body, name="unpad_w_in_t", grid=(D_MODEL // W_IN_LANES,),
        in_specs=[BS((N_PAD, W_IN_LANES), lambda j: (0, j))], out_specs=BS((N_CHIPS, W_IN_SHARD, W_IN_LANES), lambda j: (0, 0, j)),
        out_shape=jax.ShapeDtypeStruct((N_CHIPS, W_IN_SHARD, D_MODEL), g.dtype), compiler_params=_arb(1))(g)


def _perm_w_kv_b(s):
    return jnp.concatenate([s[h, :, :128] for h in range(N_HEADS)] + [s[h, :, 128:] for h in range(N_HEADS)], axis=1)


def _unperm_w_kv_b(g):
    return jnp.stack([jnp.concatenate([g[:, h * 128:(h + 1) * 128], g[:, 512 + h * 128:512 + (h + 1) * 128]], axis=1)
                      for h in range(N_HEADS)])


def _lane_row(v4):
    return jnp.pad(v4.reshape(1, -1).astype(F32), ((0, 0), (0, 128 - v4.size)))


N_CHIPS = 4
MESH = pl.DeviceIdType.MESH
ANY = BS(memory_space=pl.ANY)


def _place():
    return lax.axis_index("x"), lax.axis_index("y"), lax.axis_index("c")


def _other_chips(x, y):
    return [(1 - x, y), (x, 1 - y), (1 - x, 1 - y)]


def _half(split, which):
    axis, size = split
    ds = pl.ds(pl.multiple_of(which * size, 16 if axis == 0 else 128), size)
    return (ds, slice(None)) if axis == 0 else (slice(None), ds)


SEM = BS(memory_space=pltpu.SEMAPHORE)
HBM = BS(memory_space=pltpu.HBM)
_IN_HBM = lambda a: pltpu.with_memory_space_constraint(a, pltpu.HBM)
_SIDE_EFFECT = pltpu.SideEffectType.DATAFLOW_SIDE_EFFECTING


def _late_gather_copies(s_refs, l_refs, send_sems, recv_sems, local_sems, with_arrivals):
    x, y, c = _place()
    sends, recvs, locals_ = [], [], []
    for i, (s_ref, l_ref) in enumerate(zip(s_refs, l_refs)):
        locals_.append(pltpu.make_async_copy(s_ref, l_ref.at[2 * x + y], local_sems.at[i]))
        for j, (px, py) in enumerate(_other_chips(x, y)):
            k = 3 * i + j
            sends.append(pltpu.make_async_remote_copy(src_ref=s_ref, dst_ref=l_ref.at[2 * x + y], send_sem=send_sems.at[k],
                                                      recv_sem=recv_sems.at[k], device_id=(px, py, c), device_id_type=MESH))
            if with_arrivals:
                recvs.append(pltpu.make_async_remote_copy(src_ref=s_ref, dst_ref=l_ref.at[2 * px + py], send_sem=send_sems.at[k],
                                                          recv_sem=recv_sems.at[k], device_id=(px, py, c), device_id_type=MESH))
    return sends, recvs, locals_


def late_gather_start(shards, after, name):
    n = len(shards)

    def body(*refs):
        s_refs, l_refs = refs[:n], refs[n:2 * n]
        send_sems, recv_sems, local_sems = refs[2 * n + 1:2 * n + 4]
        token = refs[-1]
        sends, _, locals_ = _late_gather_copies(s_refs, l_refs, send_sems, recv_sems, local_sems, False)
        for cp in locals_ + sends:
            cp.start()
        token[...] = jnp.zeros_like(token)

    lands = [lax.empty((N_CHIPS,) + s.shape, s.dtype) for s in shards]
    hbm_like = lambda a: pltpu.HBM(a.shape, a.dtype)
    out = pl.pallas_call(
        body, name=name,
        out_shape=[pltpu.SemaphoreType.DMA((3 * n,)), pltpu.SemaphoreType.DMA((3 * n,)), pltpu.SemaphoreType.DMA((n,))]
        + [hbm_like(s) for s in shards] + [hbm_like(l) for l in lands] + [jax.ShapeDtypeStruct((8, 128), F32)],
        in_specs=[HBM] * (2 * n) + [BS(memory_space=pl.ANY)], out_specs=[SEM] * 3 + [HBM] * (2 * n) + [BS(memory_space=pltpu.VMEM)],
        input_output_aliases={i: 3 + i for i in range(2 * n)},
        compiler_params=pltpu.CompilerParams(has_side_effects=_SIDE_EFFECT))(
            *[_IN_HBM(s) for s in shards], *[_IN_HBM(l) for l in lands], after)
    return out[:3], out[3:3 + n], out[3 + n:3 + 2 * n], out[-1]


def late_gather_wait(sems, shards, lands, after, name):
    n = len(shards)

    def body(*refs):
        s_refs, l_refs = refs[:n], refs[n:2 * n]
        send_sems, recv_sems, local_sems = refs[2 * n:2 * n + 3]
        sends, recvs, locals_ = _late_gather_copies(s_refs, l_refs, send_sems, recv_sems, local_sems, True)
        for cp in locals_:
            cp.wait()
        for cp in sends:
            cp.wait_send()
        for cp in recvs:
            cp.wait_recv()

    hbm_like = lambda a: pltpu.HBM(a.shape, a.dtype)
    out = pl.pallas_call(
        body, name=name, out_shape=[hbm_like(s) for s in shards] + [hbm_like(l) for l in lands],
        in_specs=[HBM] * (2 * n) + [SEM] * 3 + [BS(memory_space=pl.ANY)], out_specs=[HBM] * (2 * n),
        input_output_aliases={i: i for i in range(2 * n)},
        compiler_params=pltpu.CompilerParams(has_side_effects=_SIDE_EFFECT))(*shards, *lands, *sems, after)
    return out[n:]


def _sends(x, y, early):
    return (y == 0) if early else None


def _block(x, y, early):
    return x if early else 2 * x + y


def _if(cond, fn):
    if cond is None:
        fn()
    else:
        pl.when(cond)(fn)


def _half_gather_copies(s_ref, l_ref, send_sems, recv_sems, with_arrivals, early):
    x, y, c = _place()
    sends, recvs, peer_sends = [], [], []
    for j, (px, py) in enumerate(_other_chips(x, y)):
        sends.append(pltpu.make_async_remote_copy(src_ref=s_ref.at[c], dst_ref=l_ref.at[_block(x, y, early), c],
                                                  send_sem=send_sems.at[j], recv_sem=recv_sems.at[j], device_id=(px, py, c),
                                                  device_id_type=MESH))
        if with_arrivals:
            recvs.append(pltpu.make_async_remote_copy(src_ref=s_ref.at[c], dst_ref=l_ref.at[_block(px, py, early), c],
                                                      send_sem=send_sems.at[j], recv_sem=recv_sems.at[j], device_id=(px, py, c),
                                                      device_id_type=MESH))
            peer_sends.append(_sends(px, py, early))
    return sends, recvs, peer_sends


def half_gather_start(shard, name, early=False, after=()):
    def body(s_ref, l_ref, *rest):
        send_sems, recv_sems, local_sem, s_thru, l_thru, token = rest[len(after):]
        x, y, _ = _place()

        def go():
            pltpu.make_async_copy(s_ref, l_ref.at[_block(x, y, early)], local_sem.at[0]).start()
            for cp in _half_gather_copies(s_ref, l_ref, send_sems, recv_sems, False, early)[0]:
                cp.start()

        _if(_sends(x, y, early), go)
        token[...] = jnp.zeros_like(token)

    land = lax.empty((2 if early else N_CHIPS,) + shard.shape, shard.dtype)
    out = pl.pallas_call(
        body, name=name,
        out_shape=[pltpu.SemaphoreType.DMA((3,)), pltpu.SemaphoreType.DMA((3,)), pltpu.SemaphoreType.DMA((1,)),
                   pltpu.HBM(shard.shape, shard.dtype), pltpu.HBM(land.shape, land.dtype), jax.ShapeDtypeStruct((8, 128), F32)],
        in_specs=[HBM, HBM] + [BS(memory_space=pl.ANY)] * len(after),
        out_specs=[SEM] * 3 + [HBM, HBM, BS(memory_space=pltpu.VMEM)],
        input_output_aliases={0: 3, 1: 4},
        compiler_params=pltpu.CompilerParams(has_side_effects=_SIDE_EFFECT))(_IN_HBM(shard), _IN_HBM(land), *after)
    return out[:3], out[3], out[4], out[5]


def half_gather_wait(sems, shard, land, after, name, early=False):
    def body(s_ref, l_ref, send_sems, recv_sems, local_sem, *rest):
        x, y, _ = _place()
        sends, recvs, peer_sends = _half_gather_copies(s_ref, l_ref, send_sems, recv_sems, True, early)

        def sent():
            pltpu.make_async_copy(s_ref, l_ref.at[_block(x, y, early)], local_sem.at[0]).wait()
            for cp in sends:
                cp.wait_send()

        _if(_sends(x, y, early), sent)
        for cp, cond in zip(recvs, peer_sends):
            _if(cond, cp.wait_recv)

    out = pl.pallas_call(
        body, name=name, out_shape=[pltpu.HBM(shard.shape, shard.dtype), pltpu.HBM(land.shape, land.dtype)],
        in_specs=[HBM, HBM] + [SEM] * 3 + [BS(memory_space=pl.ANY)] * len(after), out_specs=[HBM, HBM],
        input_output_aliases={0: 0, 1: 1},
        compiler_params=pltpu.CompilerParams(has_side_effects=_SIDE_EFFECT))(shard, land, *sems, *after)
    return out[1]


def pass_halves_to_sibling(land, name, early=False):
    def body(l_in, l_ref, send_sems, recv_sems):
        x, y, c = _place()
        copies = []
        for j, (px, py) in enumerate(_other_chips(x, y)):
            q = _block(px, py, early)
            give = pltpu.make_async_remote_copy(src_ref=l_ref.at[q, c], dst_ref=l_ref.at[q, c], send_sem=send_sems.at[j],
                                                recv_sem=recv_sems.at[j], device_id=(x, y, 1 - c), device_id_type=MESH)
            take = pltpu.make_async_remote_copy(src_ref=l_ref.at[q, c], dst_ref=l_ref.at[q, 1 - c], send_sem=send_sems.at[j],
                                                recv_sem=recv_sems.at[j], device_id=(x, y, 1 - c), device_id_type=MESH)
            _if(_sends(px, py, early), give.start)
            copies.append((give, take, _sends(px, py, early)))
        for give, take, cond in copies:
            def done(give=give, take=take):
                take.wait_recv()
                give.wait_send()

            _if(cond, done)

    return pl.pallas_call(
        body, name=name, in_specs=[ANY], out_specs=ANY, out_shape=jax.ShapeDtypeStruct(land.shape, land.dtype),
        input_output_aliases={0: 0},
        scratch_shapes=[pltpu.SemaphoreType.DMA((3,)), pltpu.SemaphoreType.DMA((3,))])(land)


def allgather_devices(block, name):
    R, C = block.shape

    def body(b_ref, o_ref, send_sems, recv_sems, local_sem):
        x, y, c = _place()
        me = 4 * x + 2 * y + c
        own = pltpu.make_async_copy(b_ref, o_ref.at[me], local_sem)
        own.start()
        copies = []
        for r in range(1, 8):
            px = 1 - x if r & 4 else x
            py = 1 - y if r & 2 else y
            pc = 1 - c if r & 1 else c
            send = pltpu.make_async_remote_copy(src_ref=b_ref, dst_ref=o_ref.at[me], send_sem=send_sems.at[r - 1],
                                                recv_sem=recv_sems.at[r - 1], device_id=(px, py, pc), device_id_type=MESH)
            recv = pltpu.make_async_remote_copy(src_ref=b_ref, dst_ref=o_ref.at[4 * px + 2 * py + pc], send_sem=send_sems.at[r - 1],
                                                recv_sem=recv_sems.at[r - 1], device_id=(px, py, pc), device_id_type=MESH)
            send.start()
            copies.append((send, recv))
        for send, recv in copies:
            recv.wait_recv()
            send.wait_send()
        own.wait()

    return pl.pallas_call(
        body, name=name, in_specs=[ANY], out_specs=ANY, out_shape=jax.ShapeDtypeStruct((8, R, C), block.dtype),
        scratch_shapes=[pltpu.SemaphoreType.DMA((7,)), pltpu.SemaphoreType.DMA((7,)), pltpu.SemaphoreType.DMA(())])(block)


def swap_sibling(arrs, name, splits=None):
    n = len(arrs)

    def sent(a_ref, i, c):
        return a_ref if splits is None else a_ref.at[(slice(None),) + _half(splits[i], 1 - c)]

    def out_shape(a, i):
        if splits is None:
            return a.shape
        axis, size = splits[i]
        return (a.shape[0], size, a.shape[2]) if axis == 0 else (a.shape[0], a.shape[1], size)

    def body(*refs):
        a_refs, o_refs = refs[:n], refs[n:2 * n]
        send_sems, recv_sems = refs[2 * n:]
        x, y, c = _place()
        copies = [pltpu.make_async_remote_copy(src_ref=sent(a_ref, i, c), dst_ref=o_ref, send_sem=send_sems.at[i],
                                               recv_sem=recv_sems.at[i], device_id=(x, y, 1 - c), device_id_type=MESH)
                  for i, (a_ref, o_ref) in enumerate(zip(a_refs, o_refs))]
        for cp in copies:
            cp.start()
        for cp in copies:
            cp.wait()

    return pl.pallas_call(
        body, name=name, in_specs=[ANY] * n, out_specs=[ANY] * n,
        out_shape=[jax.ShapeDtypeStruct(out_shape(a, i), a.dtype) for i, a in enumerate(arrs)],
        scratch_shapes=[pltpu.SemaphoreType.DMA((n,)), pltpu.SemaphoreType.DMA((n,))])(*arrs)


def _exchange_copies(p_refs, l_refs, send_sems, recv_sems):
    x, y, c = _place()
    return [pltpu.make_async_remote_copy(src_ref=p_ref.at[2 * px + py], dst_ref=l_ref.at[j], send_sem=send_sems.at[3 * i + j],
                                         recv_sem=recv_sems.at[3 * i + j], device_id=(px, py, c), device_id_type=MESH)
            for i, (p_ref, l_ref) in enumerate(zip(p_refs, l_refs)) for j, (px, py) in enumerate(_other_chips(x, y))]


def exchange_chips_start(parts, name):
    n = len(parts)

    def body(*refs):
        send_sems, recv_sems = refs[2 * n:2 * n + 2]
        for cp in _exchange_copies(refs[:n], refs[n:2 * n], send_sems, recv_sems):
            cp.start()
        refs[-1][...] = jnp.zeros_like(refs[-1])

    lands = [lax.empty((3,) + p.shape[1:], p.dtype) for p in parts]
    hbm_like = lambda a: pltpu.HBM(a.shape, a.dtype)
    out = pl.pallas_call(
        body, name=name,
        out_shape=[pltpu.SemaphoreType.DMA((3 * n,)), pltpu.SemaphoreType.DMA((3 * n,))]
        + [hbm_like(p) for p in parts] + [hbm_like(l) for l in lands] + [jax.ShapeDtypeStruct((8, 128), F32)],
        in_specs=[HBM] * (2 * n), out_specs=[SEM] * 2 + [HBM] * (2 * n) + [BS(memory_space=pltpu.VMEM)],
        input_output_aliases={i: 2 + i for i in range(2 * n)},
        compiler_params=pltpu.CompilerParams(has_side_effects=_SIDE_EFFECT))(*[_IN_HBM(p) for p in parts], *[_IN_HBM(l) for l in lands])
    return out[:2], out[2:2 + n], out[2 + n:2 + 2 * n], out[-1]


def exchange_chips_wait(sems, parts, lands, after, name):
    n = len(parts)

    def body(*refs):
        send_sems, recv_sems = refs[2 * n:2 * n + 2]
        for cp in _exchange_copies(refs[:n], refs[n:2 * n], send_sems, recv_sems):
            cp.wait_send()
            cp.wait_recv()

    hbm_like = lambda a: pltpu.HBM(a.shape, a.dtype)
    out = pl.pallas_call(
        body, name=name, out_shape=[hbm_like(p) for p in parts] + [hbm_like(l) for l in lands],
        in_specs=[HBM] * (2 * n) + [SEM] * 2 + [BS(memory_space=pl.ANY)], out_specs=[HBM] * (2 * n),
        input_output_aliases={i: i for i in range(2 * n)},
        compiler_params=pltpu.CompilerParams(has_side_effects=_SIDE_EFFECT))(*parts, *lands, *sems, after)
    return out[n:]


def _half_block(shape2, split):
    axis, size = split
    return (size, shape2[1]) if axis == 0 else (shape2[0], size)


def add_pairs(parts, halves, splits, core, name):
    n = len(parts)

    def body(s_ref, *refs):
        for a_ref, b_ref, o_ref in zip(refs[:n], refs[n:2 * n], refs[2 * n:]):
            o_ref[...] = (a_ref[...].astype(F32) + b_ref[...].astype(F32)).astype(BF16)

    def mine(i):
        blk = (None,) + _half_block(parts[i].shape[1:], splits[i])
        if splits[i][0] == 0:
            return BS(blk, lambda q, s: (q, s[0], 0))
        return BS(blk, lambda q, s: (q, 0, s[0]))

    half_specs = [BS((None,) + h.shape[1:], lambda q, s: (q, 0, 0)) for h in halves]
    return pl.pallas_call(
        body, name=name,
        grid_spec=pltpu.PrefetchScalarGridSpec(num_scalar_prefetch=1, grid=(N_CHIPS,),
                                               in_specs=[mine(i) for i in range(n)] + half_specs, out_specs=half_specs),
        out_shape=[jax.ShapeDtypeStruct(h.shape, BF16) for h in halves], compiler_params=_arb(1))(core, *parts, *halves)


def add_fives(parts, halves, from_chips, splits, chip_core, name):
    n = len(parts)

    def body(s_ref, *refs):
        for a_ref, b_ref, p_ref, o_ref in zip(refs[:n], refs[n:2 * n], refs[2 * n:3 * n], refs[3 * n:]):
            s = a_ref[...].astype(F32) + b_ref[...].astype(F32)
            for j in range(3):
                s = s + p_ref[j].astype(F32)
            o_ref[...] = s

    def mine(i):
        blk = (None,) + _half_block(parts[i].shape[1:], splits[i])
        if splits[i][0] == 0:
            return BS(blk, lambda g, s: (s[0], s[1], 0))
        return BS(blk, lambda g, s: (s[0], 0, s[1]))

    half_specs = [BS((None,) + h.shape[1:], lambda g, s: (s[0], 0, 0)) for h in halves]
    chip_specs = [BS(p.shape, lambda g, s: (0, 0, 0)) for p in from_chips]
    out_specs = [BS(h.shape[1:], lambda g, s: (0, 0)) for h in halves]
    return pl.pallas_call(
        body, name=name,
        grid_spec=pltpu.PrefetchScalarGridSpec(num_scalar_prefetch=1, grid=(1,),
                                               in_specs=[mine(i) for i in range(n)] + half_specs + chip_specs, out_specs=out_specs),
        out_shape=[jax.ShapeDtypeStruct(h.shape[1:], F32) for h in halves], compiler_params=_arb(1))(chip_core, *parts, *halves, *from_chips)


def sum_leading(a, name):
    def body(a_ref, o_ref):
        s = a_ref[0]
        for j in range(1, a.shape[0]):
            s = s + a_ref[j]
        o_ref[...] = s

    return pl.pallas_call(body, name=name, out_shape=jax.ShapeDtypeStruct(a.shape[1:], a.dtype))(a)


def _adamw_math(w, g, m, v):
    mn = ADAM_B1 * m + (1.0 - ADAM_B1) * g
    vn = ADAM_B2 * v + (1.0 - ADAM_B2) * (g * g)
    m_hat = mn / (1.0 - ADAM_B1 ** ADAM_STEP)
    v_hat = vn / (1.0 - ADAM_B2 ** ADAM_STEP)
    return -ADAM_LR * (m_hat / (jnp.sqrt(v_hat) + ADAM_EPS) + ADAM_WD * w), mn, vn


def adamw(w, g, m, v, name):
    R, C = g.shape
    lead = (None,) * (w.ndim - 2)

    def body(w_ref, g_ref, m_ref, v_ref, d_ref, mo_ref, vo_ref):
        d_ref[...], mo_ref[...], vo_ref[...] = _adamw_math(w_ref[...], g_ref[...], m_ref[...], v_ref[...])

    wblk = BS(lead + (R, C), lambda i: (0,) * w.ndim)
    gblk = BS((R, C), lambda i: (0, 0))
    return pl.pallas_call(
        body, name=name, grid=(1,), in_specs=[wblk, gblk, wblk, wblk], out_specs=[wblk] * 3,
        out_shape=[jax.ShapeDtypeStruct(w.shape, F32)] * 3, compiler_params=_arb(1))(w, g, m, v)


SMALL_ROWS = 16
CONV_ROW0 = 8
LOSS_ROW = 14


def pack_small(small_grads, g_ab, g_conv, sq):
    present = [a for a in small_grads if a is not None]

    def body(*refs):
        ab_ref, conv_ref, sq_ref, o_ref = refs[len(present):]
        o_ref[...] = jnp.zeros_like(o_ref)
        it = iter(refs[:len(present)])
        for i, a in enumerate(small_grads):
            if a is not None:
                o_ref[i:i + 1, 0:a.shape[1]] = next(it)[...]
        o_ref[3:4, 0:128] = ab_ref[0:1, :]
        o_ref[4:5, 0:128] = ab_ref[1:2, :]
        half = 512
        for k in range(GDN_CONV * GDN_QKV // half):
            src_r, src_c = (k * half) // GDN_QKV, (k * half) % GDN_QKV
            dst_r, dst_c = CONV_ROW0 + (k * half) // 1024, (k * half) % 1024
            o_ref[dst_r:dst_r + 1, dst_c:dst_c + half] = conv_ref[src_r:src_r + 1, src_c:src_c + half]
        o_ref[LOSS_ROW:LOSS_ROW + 1, :] = sq_ref[...]

    return pl.pallas_call(body, name="pack_small", out_shape=jax.ShapeDtypeStruct((SMALL_ROWS, 1024), F32))(
        *present, g_ab, g_conv, sq)


def adamw_small(block, ws, ms, vs):
    k = len(ws)

    def body(b_ref, *refs):
        outs = refs[3 * k:]
        for i in range(k):
            n = ws[i].shape[1]
            g = b_ref[i:i + 1, 0:n]
            d, mn, vn = _adamw_math(refs[i][...], g, refs[k + i][...], refs[2 * k + i][...])
            outs[4 * i][...], outs[4 * i + 1][...], outs[4 * i + 2][...], outs[4 * i + 3][...] = g, d, mn, vn

    out = pl.pallas_call(
        body, name="adamw_small",
        out_shape=[jax.ShapeDtypeStruct(w.shape, F32) for w in ws for _ in range(4)])(block, *ws, *ms, *vs)
    return [out[4 * i:4 * i + 4] for i in range(k)]


def adamw_halves(w, mine, other, m, v, split, core, name):
    R, C = w.shape[-2:]
    axis, size = split
    lead = (None,) * (w.ndim - 2)
    zeros = (0,) * (w.ndim - 2)
    if axis == 0:
        tr = size if size <= 256 else next(t for t in range(256, 7, -1) if size % t == 0 and t % 8 == 0)
        nb = size // tr
        whole = BS(lead + (tr, C), lambda hi, j, s: zeros + (hi * nb + j, 0))
        part = BS((tr, C), lambda hi, j, s: (j, 0))
    else:
        nb = size // 128
        whole = BS(lead + (R, 128), lambda hi, j, s: zeros + (0, hi * nb + j))
        part = BS((R, 128), lambda hi, j, s: (0, j))

    def body(s_ref, w_ref, a_ref, b_ref, m_ref, v_ref, g_ref, d_ref, mo_ref, vo_ref):
        g = jnp.where(pl.program_id(0) == s_ref[0], a_ref[...], b_ref[...])
        g_ref[...] = g
        d_ref[...], mo_ref[...], vo_ref[...] = _adamw_math(w_ref[...], g, m_ref[...], v_ref[...])

    return pl.pallas_call(
        body, name=name,
        grid_spec=pltpu.PrefetchScalarGridSpec(num_scalar_prefetch=1, grid=(2, nb),
                                               in_specs=[whole, part, part, whole, whole], out_specs=[whole] * 4),
        out_shape=[jax.ShapeDtypeStruct(w.shape, F32)] * 4, compiler_params=_arb(2))(core, w, mine, other, m, v)


def dense_bf16(w3, name):
    R, _, K = w3.shape
    kh = K // 2

    def body(w_hbm, o_ref, buf, sem):
        cp = pltpu.make_async_copy(w_hbm.at[:, 0], buf, sem)
        cp.start()
        cp.wait()
        o_ref[0] = buf[:, :kh].astype(BF16)
        o_ref[1] = buf[:, kh:].astype(BF16)

    return pl.pallas_call(
        body, name=name, in_specs=[ANY], out_specs=BS(memory_space=pltpu.VMEM), out_shape=jax.ShapeDtypeStruct((2, R, kh), BF16),
        scratch_shapes=[pltpu.VMEM((R, K), F32), pltpu.SemaphoreType.DMA(())])(w3)


ROW_BLOCK = 184


def adamw_untiled_rows(w3, mine, other, m3, v3, name):
    R, _, K = w3.shape
    kh = K // 2
    starts = list(range(0, R, ROW_BLOCK))
    sizes = [min(ROW_BLOCK, R - s) for s in starts]
    nblk = len(starts)

    def body(w_hbm, a_ref, b_ref, m_hbm, v_hbm, g_hbm, d_hbm, mo_hbm, vo_hbm,
             wbuf, mbuf, vbuf, gbuf, dbuf, mobuf, vobuf, in_sems, out_sems):
        first = lax.axis_index("c") == 0
        ins = []
        for k, (r0, n) in enumerate(zip(starts, sizes)):
            rows = pl.ds(r0, n)
            cps = [pltpu.make_async_copy(src.at[rows, 0], dst.at[rows], in_sems.at[3 * k + i])
                   for i, (src, dst) in enumerate(((w_hbm, wbuf), (m_hbm, mbuf), (v_hbm, vbuf)))]
            for cp in cps:
                cp.start()
            ins.append(cps)

        def update(rows):
            a, b = a_ref[rows, :], b_ref[rows, :]
            g = jnp.concatenate([jnp.where(first, a, b), jnp.where(first, b, a)], axis=1)
            gbuf[rows, :] = g
            dbuf[rows, :], mobuf[rows, :], vobuf[rows, :] = _adamw_math(wbuf[rows, :], g, mbuf[rows, :], vbuf[rows, :])

        outs = []
        for k, (r0, n) in enumerate(zip(starts, sizes)):
            for cp in ins[k]:
                cp.wait()
            groups, tail = n // 8, n % 8

            def group(i, carry, r0=r0):
                update(pl.ds(pl.multiple_of(r0 + i * 8, 8), 8))
                return carry

            lax.fori_loop(0, groups, group, 0)
            if tail:
                update(pl.ds(r0 + groups * 8, tail))
            rows = pl.ds(r0, n)
            cps = [pltpu.make_async_copy(src.at[rows], dst.at[rows, 0], out_sems.at[4 * k + i])
                   for i, (src, dst) in enumerate(((gbuf, g_hbm), (dbuf, d_hbm), (mobuf, mo_hbm), (vobuf, vo_hbm)))]
            for cp in cps:
                cp.start()
            outs += cps
        for cp in outs:
            cp.wait()

    vmem = BS(memory_space=pltpu.VMEM)
    return pl.pallas_call(
        body, name=name, in_specs=[ANY, vmem, vmem, ANY, ANY], out_specs=[ANY] * 4,
        out_shape=[jax.ShapeDtypeStruct(w3.shape, F32)] * 4,
        scratch_shapes=[pltpu.VMEM((R, K), F32)] * 7 + [pltpu.SemaphoreType.DMA((3 * nblk,)), pltpu.SemaphoreType.DMA((4 * nblk,))])(
            w3, mine, other, m3, v3)


def adamw_w_q_b(w, mine, other, m, v, name):
    def body(w_ref, a_ref, b_ref, m_ref, v_ref, g_ref, d_ref, mo_ref, vo_ref):
        first = lax.axis_index("c") == 0
        lo = jnp.where(first, a_ref[...], b_ref[...])
        hi = jnp.where(first, b_ref[...], a_ref[...])
        g = jnp.concatenate([lo, hi[0:32], hi[64:96]], axis=0)
        g_ref[...] = g
        d_ref[...], mo_ref[...], vo_ref[...] = _adamw_math(w_ref[...], g, m_ref[...], v_ref[...])

    return pl.pallas_call(body, name=name, out_shape=[jax.ShapeDtypeStruct(w.shape, F32)] * 4)(w, mine, other, m, v)


def local_step(x, mem, positions, tgt, norm_in, weights, big_grads_ready, q_a_norm, kv_a_norm, gdn_conv, gdn_a_log,
               gdn_dt_bias, gdn_norm, mem_norm, norm_final):
    B, S, D = x.shape
    M = mem.shape[1]
    T = B * S
    N = S // CHUNK
    x2d = x.reshape(T, D)
    mem2d = mem.reshape(B * M, D)
    tgt2d = tgt.reshape(T, D)

    alog_row, dt_row = _lane_row(gdn_a_log), _lane_row(gdn_dt_bias)

    half = MLA_ROPE // 2
    inv_freq = 1.0 / (ROPE_THETA ** (jnp.arange(half, dtype=F32) / half))
    z32 = jnp.zeros((half,), F32)
    o32 = jnp.ones((half,), F32)
    inv_row = jnp.concatenate([inv_freq, z32, inv_freq, z32]).reshape(1, 128)
    sgn_row = jnp.concatenate([-o32, z32, o32, z32]).reshape(1, 128)
    msk_row = jnp.concatenate([o32, z32, o32, z32]).reshape(1, 128)
    cos_t, sin_t = rope_tables(positions.reshape(T, 1), inv_row, sgn_row, msk_row, after=weights[3])

    h = rms_fwd(x2d, norm_in, "rms_in", after=weights[3])
    wp_early = weights[0]((h, cos_t))
    P_early = mm(h, wp_early, "nt", F32, "in_proj_early", bm=1024, bn=OFF_GDN, n_outer=True, col_tiles=(0, 1))
    wq, wkv = weights[1](P_early)
    Q, K, V, qn, kvn = mla_prep(P_early, q_a_norm, kv_a_norm, wq, wkv, cos_t, sin_t)
    o_mla, lse = mla_attn_fwd(Q, K, V, B, S)
    wp = weights[4]((o_mla, P_early), wp_early)
    P = mm(h, wp, "nt", F32, "in_proj", bm=1024, bn=OFF_GDN, n_outer=True, col_tiles=(1, N_PAD // OFF_GDN), into=P_early)
    qkv = gdn_prep_fwd(P, gdn_conv, B, S)
    GB = gdn_gate_fwd(P, alog_row, dt_row, B, S)
    Grow = jnp.transpose(GB[:, :N_HEADS].reshape(B, N, CHUNK, N_HEADS), (0, 3, 1, 2))
    U, W, Tinv, A = gdn_chunk_fwd(qkv, GB, Grow, B, S)
    qkv3, GB3 = qkv.reshape(B, S, GDN_QKV), GB.reshape(B, S, 128)
    W3 = W.reshape(B, S, 512)
    o_gdn3, Vn3, St = gdn_scan_fwd(qkv3, U.reshape(B, S, 512), W3, GB3, A, B, S)
    o_gdn = o_gdn3.reshape(T, 512)
    w_mem_kv, w_out = weights[2](o_gdn)
    memn = rms_fwd(mem2d, mem_norm, "rms_mem")
    MKV = mm(memn, w_mem_kv, "nn", BF16, "mem_kv_proj")
    o_mem = mem_attn_fwd(P, MKV, B, S, M)
    mixed, dx2, dx2b, sq, g_norm_final = merge_fwd(o_mla, o_gdn, o_mem, P, x2d, tgt2d, w_out, gdn_norm, norm_final.reshape(1, D))

    g_w_out = mm(mixed, dx2b, "tn", BF16, "grad_w_out")
    dgate, do_mla, do_gdn, do_mem, g_gdn_norm = merge_bwd(dx2b, o_mla, o_gdn, o_mem, P, w_out, gdn_norm)

    dmemq, dMKV = mem_attn_bwd(P, MKV, do_mem, B, S, M)
    g_w_mem_kv = mm(memn, dMKV, "tn", BF16, "grad_w_mem_kv")
    started_early = big_grads_ready(dict(w_mem_kv=g_w_mem_kv, w_out=g_w_out), "early")
    dmemn = mm(dMKV, w_mem_kv, "nt", F32, "d_memn", after=(started_early,))
    g_mem_norm = gain_grad(mem2d, dmemn, "grad_mem_norm")

    dU3, dW3, dQ13, dK13, dA, dG13 = gdn_scan_bwd(do_gdn.reshape(B, S, 512), qkv3, W3, Vn3, GB3, A, St, B, S)
    r2 = lambda a: a.reshape(T, a.shape[-1])
    dqkv, dGB = gdn_chunk_bwd(qkv, GB, Grow, Tinv, dA, r2(dU3), r2(dW3), r2(dQ13), r2(dK13), r2(dG13), B, S)
    dPg, g_conv = gdn_prep_bwd(P, dqkv, gdn_conv, B, S)
    dab, g_ab = gdn_gate_bwd(P, dGB, alog_row, dt_row, B, S)

    dQ, dK, dV = mla_attn_bwd(Q, K, V, o_mla, do_mla, lse, B, S)
    dq_lin, dkv_lin, dPm, g_q_a_norm, g_kv_a_norm = mla_proj_bwd(dQ, dK, dV, cos_t, sin_t, P, dab, wq, wkv, q_a_norm, kv_a_norm)
    g_wq = mm(dq_lin, qn, "tn", BF16, "grad_w_q_b")
    g_wkv = mm(kvn, dkv_lin, "tn", BF16, "grad_w_kv_b")

    dP = [dPm, dmemq, dPg, dgate]
    g_wp = mm_cols_tn(dP, h, BF16, "grad_w_in")
    started = big_grads_ready(dict(w_in=g_wp, w_q_b=g_wq, w_kv_b=g_wkv), "late")
    grad_x, g_norm_in = in_proj_bwd(dP, wp, x2d, dx2, norm_in, started)

    grads = dict(
        norm_in=g_norm_in, q_a_norm=g_q_a_norm, kv_a_norm=g_kv_a_norm, gdn_conv=g_conv,
        gdn_a_log_dt_bias=g_ab, gdn_norm=g_gdn_norm,
        mem_norm=g_mem_norm, norm_final=g_norm_final)
    return sq, grad_x.reshape(B, S, D), grads


def kernel(x, mem, positions, norm_in, w_in, q_a_norm, w_q_b, kv_a_norm, w_kv_b, gdn_conv, gdn_a_log, gdn_dt_bias, gdn_norm, mem_norm, w_mem_kv, w_out, norm_final, loss_target, m_norm_in, m_w_in, m_q_a_norm, m_w_q_b, m_kv_a_norm, m_w_kv_b, m_gdn_conv, m_gdn_a_log, m_gdn_dt_bias, m_gdn_norm, m_mem_norm, m_w_mem_kv, m_w_out, m_norm_final, v_norm_in, v_w_in, v_q_a_norm, v_w_q_b, v_kv_a_norm, v_w_kv_b, v_gdn_conv, v_gdn_a_log, v_gdn_dt_bias, v_gdn_norm, v_mem_norm, v_w_mem_kv, v_w_out, v_norm_final):
    B = x.shape[0]
    cx, cy, cc = lax.axis_index("x"), lax.axis_index("y"), lax.axis_index("c")
    chip = 2 * cx + cy

    big_names = ("w_in", "w_q_b", "w_kv_b", "w_mem_kv", "w_out")
    rows_major = lambda a: jnp.transpose(a, (2, 0, 1))
    w_in3, m_in3, v_in3 = rows_major(w_in), rows_major(m_w_in), rows_major(v_w_in)
    w_qb_t, m_qb_t, v_qb_t = jnp.transpose(w_q_b[0]), jnp.transpose(m_w_q_b[0]), jnp.transpose(v_w_q_b[0])
    z32 = jnp.zeros((32, Q_LORA), BF16)
    qb_bf = w_qb_t.astype(BF16)
    qb_padded = jnp.concatenate([qb_bf[:160], z32, qb_bf[160:], z32])
    shards = [dense_bf16(w_in3, "w_in_bf16"), qb_padded, w_kv_b[0].astype(BF16), w_mem_kv[0].astype(BF16), w_out[0].astype(BF16)]
    splits = [(1, D_MODEL // 2)] + [(0, s.shape[0] // 2) for s in shards[1:]]
    early_row0 = jnp.where(chip == 2, EARLY_ROW0[1], EARLY_ROW0[0])
    *early_flight, early_started = half_gather_start(lax.dynamic_slice_in_dim(shards[0], early_row0, EARLY_ROWS, axis=1),
                                                     "w_in_early_gather_start", early=True)
    *late_a, started_a = late_gather_start(shards[1:3], early_started, "late_gather_qkv_start")
    *w_in_flight, w_in_started = half_gather_start(shards[0], "w_in_gather_start", after=(started_a,))
    *late_b, started_b = late_gather_start(shards[3:], w_in_started, "late_gather_mem_out_start")
    conv_all = allgather_devices(gdn_conv[0], "allgather_conv")
    conv_cols = gdn_conv.shape[2]
    conv_full = jnp.transpose(conv_all[0::2], (1, 0, 2)).reshape(GDN_CONV, N_CHIPS * conv_cols)
    late_shapes = [(N_CHIPS,) + s.shape for s in shards[1:]]

    def w_in_early_ready(after):
        g_early = pass_halves_to_sibling(half_gather_wait(*early_flight, after, "w_in_early_gather_wait", early=True),
                                         "w_in_early_gather_sibling", early=True)
        return pad_w_in_t_early(g_early)

    def w_in_ready(after, wp):
        g_in = pass_halves_to_sibling(half_gather_wait(*w_in_flight, after, "w_in_gather_wait"), "w_in_gather_sibling")
        return pad_w_in_t_rest(g_in, wp)

    def late_qkv(after):
        g_qb, g_kvb = late_gather_wait(*late_a, after, "late_gather_qkv_wait")
        return g_qb.reshape(-1, Q_LORA), _perm_w_kv_b(g_kvb)

    def late_mem_out(after):
        g_mem, g_out_w = late_gather_wait(*late_b, after, "late_gather_mem_out_wait")
        return g_mem.reshape(-1, g_mem.shape[2]), g_out_w.reshape(-1, g_out_w.shape[2])

    weights = (w_in_early_ready, late_qkv, late_mem_out, (started_b,), w_in_ready)

    core = jnp.stack([cc]).astype(jnp.int32)
    chip_core = jnp.stack([chip, cc]).astype(jnp.int32)
    exchanges = {}
    by_chip = dict(w_in=unpad_w_in_t, w_q_b=lambda a: a.reshape(late_shapes[0]), w_kv_b=_unperm_w_kv_b,
                   w_mem_kv=lambda a: a.reshape(late_shapes[2]), w_out=lambda a: a.reshape(late_shapes[3]))

    def big_grads_ready(gb, group):
        idx = [big_names.index(n) for n in gb]
        parts = [by_chip[n](a) for n, a in gb.items()]
        sp = [splits[i] for i in idx]
        from_sibling = swap_sibling(parts, "rs_sibling_partial_" + group, sp)
        chip_sums = add_pairs(parts, from_sibling, sp, core, "rs_add_sibling_" + group)
        sems, sums_thru, lands, token = exchange_chips_start(chip_sums, "rs_exchange_start_" + group)
        exchanges[group] = dict(idx=idx, parts=parts, from_sibling=from_sibling, sems=sems, sums=sums_thru, lands=lands)
        return token

    sq, grad_x, g = local_step(x, mem, positions, loss_target, norm_in, weights, big_grads_ready, q_a_norm, kv_a_norm, conv_full,
                               gdn_a_log, gdn_dt_bias, gdn_norm, mem_norm, norm_final)

    small_names = ("norm_in", "q_a_norm", "kv_a_norm", "gdn_a_log", "gdn_dt_bias", "gdn_norm", "mem_norm", "norm_final")
    small = dict(norm_in=norm_in, q_a_norm=q_a_norm, kv_a_norm=kv_a_norm, gdn_a_log=gdn_a_log, gdn_dt_bias=gdn_dt_bias,
                 gdn_norm=gdn_norm, mem_norm=mem_norm, norm_final=norm_final)
    m_small = dict(norm_in=m_norm_in, q_a_norm=m_q_a_norm, kv_a_norm=m_kv_a_norm, gdn_a_log=m_gdn_a_log,
                   gdn_dt_bias=m_gdn_dt_bias, gdn_norm=m_gdn_norm, mem_norm=m_mem_norm, norm_final=m_norm_final)
    v_small = dict(norm_in=v_norm_in, q_a_norm=v_q_a_norm, kv_a_norm=v_kv_a_norm, gdn_a_log=v_gdn_a_log,
                   gdn_dt_bias=v_gdn_dt_bias, gdn_norm=v_gdn_norm, mem_norm=v_mem_norm, norm_final=v_norm_final)
    conv_rows = GDN_CONV * GDN_QKV // 1024
    g_block = pack_small([g.get(n) for n in small_names], g["gdn_a_log_dt_bias"], g["gdn_conv"], sq)
    g_block = sum_leading(allgather_devices(g_block, "allgather_small_grads"), "sum_small_grads")
    loss = 0.5 * jnp.sum(g_block[LOSS_ROW]) / D_MODEL
    g_conv = lax.dynamic_slice_in_dim(g_block[CONV_ROW0:CONV_ROW0 + conv_rows].reshape(GDN_CONV, GDN_QKV), chip * conv_cols,
                                      conv_cols, axis=1)
    as_row = lambda a: a.reshape(1, -1)
    updated = adamw_small(g_block, [as_row(small[n]) for n in small_names], [as_row(m_small[n]) for n in small_names],
                          [as_row(v_small[n]) for n in small_names])
    g_out, d_out, m_out, v_out = ({n: u[i].reshape(small[n].shape) for n, u in zip(small_names, updated)} for i in range(4))
    d_s = d_out["norm_in"]

    my_half = [None] * len(big_names)
    for group, e in exchanges.items():
        from_chips = exchange_chips_wait(e["sems"], e["sums"], e["lands"], d_s, "rs_exchange_wait_" + group)
        halves = add_fives(e["parts"], e["from_sibling"], from_chips, [splits[i] for i in e["idx"]], chip_core, "rs_add_chips_" + group)
        for i, a in zip(e["idx"], halves):
            my_half[i] = a
    other_half = swap_sibling(my_half, "rs_sibling_final")

    d_out["gdn_conv"], m_out["gdn_conv"], v_out["gdn_conv"] = adamw(gdn_conv, g_conv, m_gdn_conv, v_gdn_conv, "adamw_gdn_conv")
    g_out["gdn_conv"] = g_conv[None]
    res = adamw_untiled_rows(w_in3, my_half[0], other_half[0], m_in3, v_in3, "adamw_w_in")
    g_out["w_in"], d_out["w_in"], m_out["w_in"], v_out["w_in"] = [jnp.transpose(r, (1, 2, 0)) for r in res]
    res = adamw_w_q_b(w_qb_t, my_half[1], other_half[1], m_qb_t, v_qb_t, "adamw_w_q_b")
    g_out["w_q_b"], d_out["w_q_b"], m_out["w_q_b"], v_out["w_q_b"] = [jnp.transpose(r)[None] for r in res]
    rest = dict(w_kv_b=(w_kv_b, m_w_kv_b, v_w_kv_b), w_mem_kv=(w_mem_kv, m_w_mem_kv, v_w_mem_kv), w_out=(w_out, m_w_out, v_w_out))
    for i, n in enumerate(big_names):
        if n in rest:
            w_n, m_n, v_n = rest[n]
            g_out[n], d_out[n], m_out[n], v_out[n] = adamw_halves(w_n, my_half[i], other_half[i], m_n, v_n, splits[i], core, "adamw_" + n)

    order = ("norm_in", "w_in", "q_a_norm", "w_q_b", "kv_a_norm", "w_kv_b", "gdn_conv", "gdn_a_log", "gdn_dt_bias",
             "gdn_norm", "mem_norm", "w_mem_kv", "w_out", "norm_final")
    return (loss, grad_x, *[g_out[n] for n in order], *[d_out[n] for n in order], *[m_out[n] for n in order],
            *[v_out[n] for n in order])
```

```python
import jax
import jax.numpy as jnp
from jax import lax
from jax.experimental import pallas as pl
from jax.experimental.pallas import tpu as pltpu

F32 = jnp.float32
BF16 = jnp.bfloat16
BS = pl.BlockSpec

D_MODEL = 1024
N_HEADS = 4
MLA_NOPE, MLA_ROPE, MLA_V = 128, 64, 128
Q_LORA, KV_LORA = 384, 256
ROPE_THETA = 10000.0
GDN_DK = GDN_DV = 128
GDN_CONV = 4
CHUNK = 64
MEM_DH = 128
D_MIX = 1536
GDN_QKV = 1536
D_IN = 4296
EPS = 1e-6
ADAM_LR, ADAM_B1, ADAM_B2, ADAM_EPS, ADAM_WD, ADAM_STEP = 0.001, 0.9, 0.999, 1e-08, 0.01, 10

OFF_MLA = 0
OFF_MEMQ = 1024
OFF_GDN = 1536
OFF_GATE = 3072
N_PAD = 4608
HEAD_PAD = 256
MLA_SCALE = (MLA_NOPE + MLA_ROPE) ** -0.5
MEM_SCALE = MEM_DH ** -0.5
GDN_SCALE = GDN_DK ** -0.5
NEG = -1e30

NN = ((1,), (0,))
NT = ((1,), (1,))
TN = ((0,), (0,))


def _dot(a, b, dims):
    return lax.dot_general(a, b, (dims, ((), ())), preferred_element_type=F32)


def _bdot(spec, a, b, precision=None):
    return jnp.einsum(spec, a, b, preferred_element_type=F32, precision=precision)


def _arb(n):
    return pltpu.CompilerParams(dimension_semantics=("arbitrary",) * n)


def _sigmoid(x):
    return 1.0 / (1.0 + jnp.exp(-x))


def _softplus(z):
    return jnp.maximum(z, 0.0) + jnp.log(1.0 + jnp.exp(-jnp.abs(z)))


def _rope(t, cos_row, sin_row):
    return t * cos_row + pltpu.roll(t, 64, 1) * sin_row


def _rope_bwd(d, cos_row, sin_row):
    return d * cos_row + pltpu.roll(d * sin_row, 64, 1)


def rms_fwd(x, gain, name, tm=512, after=()):
    T, n = x.shape
    tm = min(tm, T)

    def body(x_ref, g_ref, *rest):
        xv = x_ref[...]
        r = lax.rsqrt(jnp.mean(xv * xv, axis=-1, keepdims=True) + EPS)
        rest[-1][...] = (xv * r * g_ref[...]).astype(BF16)

    return pl.pallas_call(
        body, name=name, grid=(T // tm,),
        in_specs=[BS((tm, n), lambda i: (i, 0)), BS((1, n), lambda i: (0, 0))] + [BS(memory_space=pl.ANY)] * len(after),
        out_specs=BS((tm, n), lambda i: (i, 0)),
        out_shape=jax.ShapeDtypeStruct((T, n), BF16), compiler_params=_arb(1))(x, gain, *after)


def mm(a, b, kind, out_dtype, name, bm=512, bn=None, n_outer=False, after=(), col_tiles=None, into=None):
    if kind == "nn":
        (M, K), (_, N) = a.shape, b.shape
    elif kind == "nt":
        (M, K), (N, _) = a.shape, b.shape
    else:
        (K, M), (_, N) = a.shape, b.shape
    bm, bn = min(bm, M), min(bn or N, N)
    assert M % bm == 0 and N % bn == 0, (name, M, N, K)
    lo, hi = col_tiles or (0, N // bn)
    ij = (lambda g0, g1: (g1, g0 + lo)) if n_outer else (lambda g0, g1: (g0, g1 + lo))
    a_spec = BS((K, bm), lambda g0, g1: (0, ij(g0, g1)[0])) if kind == "tn" else BS((bm, K), lambda g0, g1: (ij(g0, g1)[0], 0))
    once = dict(pipeline_mode=pl.Buffered(1)) if hi - lo == 1 else {}
    b_spec = (BS((bn, K), lambda g0, g1: (ij(g0, g1)[1], 0), **once) if kind == "nt"
              else BS((K, bn), lambda g0, g1: (0, ij(g0, g1)[1]), **once))
    dims = {"nn": NN, "nt": NT, "tn": TN}[kind]
    extra = tuple(after) + (() if into is None else (into,))

    def body(a_ref, b_ref, *rest):
        rest[-1][...] = _dot(a_ref[...].astype(BF16), b_ref[...].astype(BF16), dims).astype(out_dtype)

    grid = (hi - lo, M // bm) if n_outer else (M // bm, hi - lo)
    return pl.pallas_call(
        body, name=name, grid=grid, in_specs=[a_spec, b_spec] + [BS(memory_space=pl.ANY)] * len(extra),
        out_specs=BS((bm, bn), lambda g0, g1: ij(g0, g1)),
        input_output_aliases={} if into is None else {1 + len(extra): 0},
        out_shape=jax.ShapeDtypeStruct((M, N), out_dtype), compiler_params=_arb(2))(a, b, *extra)


def mm_cols_tn(pieces, b, out_dtype, name, bm=512):
    K, N = b.shape
    tiles = [p.shape[1] // bm for p in pieces]
    firsts = [sum(tiles[:i]) for i in range(len(tiles))]

    def body(*refs):
        b_ref, o_ref = refs[-2], refs[-1]
        i = pl.program_id(0)
        for a_ref, t0, n in zip(refs[:-2], firsts, tiles):
            @pl.when((i >= t0) & (i < t0 + n))
            def _(a_ref=a_ref):
                o_ref[...] = _dot(a_ref[...], b_ref[...], TN).astype(out_dtype)

    a_specs = [BS((K, bm), lambda i, t0=t0, n=n: (0, jnp.clip(i - t0, 0, n - 1))) for t0, n in zip(firsts, tiles)]
    return pl.pallas_call(
        body, name=name, grid=(sum(tiles),),
        in_specs=a_specs + [BS(b.shape, lambda i: (0, 0), pipeline_mode=pl.Buffered(1))],
        out_specs=BS((bm, N), lambda i: (i, 0)), out_shape=jax.ShapeDtypeStruct((sum(tiles) * bm, N), out_dtype),
        compiler_params=_arb(1))(*pieces, b)


def rope_tables(pos_col, inv_row, sgn_row, msk_row, tm=512, after=()):
    T = pos_col.shape[0]
    tm = min(tm, T)

    def body(p_ref, inv_ref, sgn_ref, msk_ref, *rest):
        c_ref, s_ref = rest[-2:]
        ang = p_ref[...].astype(F32) * inv_ref[...]
        c_ref[...] = jnp.cos(ang) * msk_ref[...]
        s_ref[...] = jnp.sin(ang) * sgn_ref[...]

    row = BS((1, 128), lambda i: (0, 0))
    return pl.pallas_call(
        body, name="rope_tables", grid=(T // tm,),
        in_specs=[BS((tm, 1), lambda i: (i, 0)), row, row, row] + [BS(memory_space=pl.ANY)] * len(after),
        out_specs=[BS((tm, 128), lambda i: (i, 0))] * 2,
        out_shape=[jax.ShapeDtypeStruct((T, 128), F32)] * 2, compiler_params=_arb(1))(pos_col, inv_row, sgn_row, msk_row, *after)


def mla_prep(P, gq, gkv, wq, wkv, cos_t, sin_t, tm=512):
    T = P.shape[0]
    tm = min(tm, T)

    def body(p_ref, gq_ref, gkv_ref, wq_ref, wkv_ref, c_ref, s_ref, q_ref, k_ref, v_ref, qn_ref, kvn_ref):
        p = p_ref[...].astype(F32)
        cq, ckv, kr = p[:, :Q_LORA], p[:, Q_LORA:Q_LORA + KV_LORA], p[:, 640:768]
        qn = (cq * lax.rsqrt(jnp.mean(cq * cq, axis=-1, keepdims=True) + EPS) * gq_ref[...]).astype(BF16)
        kvn = (ckv * lax.rsqrt(jnp.mean(ckv * ckv, axis=-1, keepdims=True) + EPS) * gkv_ref[...]).astype(BF16)
        qn_ref[...] = qn
        kvn_ref[...] = kvn
        q = _dot(qn, wq_ref[...], NT)
        kv = _dot(kvn, wkv_ref[...], NN)
        cos_row, sin_row = c_ref[...], s_ref[...]
        krr = _rope(kr, cos_row, sin_row).astype(BF16)
        for h in range(N_HEADS):
            lo = h * HEAD_PAD
            q_ref[:, lo:lo + 128] = (q[:, lo:lo + 128] * MLA_SCALE).astype(BF16)
            q_ref[:, lo + 128:lo + 256] = (_rope(q[:, lo + 128:lo + 256], cos_row, sin_row) * MLA_SCALE).astype(BF16)
            k_ref[:, lo:lo + 128] = kv[:, h * 128:(h + 1) * 128].astype(BF16)
            k_ref[:, lo + 128:lo + 256] = krr
            v_ref[:, lo:lo + 128] = kv[:, 512 + h * 128:512 + (h + 1) * 128].astype(BF16)
            v_ref[:, lo + 128:lo + 256] = jnp.ones((tm, 128), BF16)

    full = lambda r, c: BS((r, c), lambda i: (0, 0))
    rowb = lambda c: BS((tm, c), lambda i: (i, 0))
    return pl.pallas_call(
        body, name="mla_prep", grid=(T // tm,),
        in_specs=[rowb(1024), full(1, Q_LORA), full(1, KV_LORA), full(1024, Q_LORA), full(KV_LORA, 1024), rowb(128), rowb(128)],
        out_specs=[rowb(1024), rowb(1024), rowb(1024), rowb(Q_LORA), rowb(KV_LORA)],
        out_shape=[jax.ShapeDtypeStruct((T, 1024), BF16), jax.ShapeDtypeStruct((T, 1024), BF16),
                   jax.ShapeDtypeStruct((T, 1024), BF16), jax.ShapeDtypeStruct((T, Q_LORA), BF16),
                   jax.ShapeDtypeStruct((T, KV_LORA), BF16)],
        compiler_params=_arb(1))(P, gq, gkv, wq, wkv, cos_t, sin_t)


ATTN_HEADS_PER_STEP = 2
ATTN_STRIP = 32


def mla_attn_fwd(Q, K, V, B, S, tq=512, hp=ATTN_HEADS_PER_STEP):
    T = B * S
    tq = min(tq, S)
    nq = S // tq

    rs = min(ATTN_STRIP, tq)

    def body(q_ref, k_ref, v_ref, o_ref, lse_ref, m_s, acc_s, s_s, p_s, a_s):
        i = pl.program_id(2)
        m_s[...] = jnp.full_like(m_s, NEG)
        acc_s[...] = jnp.zeros_like(acc_s)

        def blk(j, masked):
            rows = pl.ds(pl.multiple_of(j * tq, tq), tq)
            for h in range(hp):
                hq = slice(h * HEAD_PAD, (h + 1) * HEAD_PAD)
                s_s[h] = _dot(q_ref[:, hq], k_ref[rows, hq], NT)
            for h in range(hp):
                for r0 in range(0, tq, rs):
                    rr = slice(r0, r0 + rs)
                    sv = s_s[h, rr, :]
                    if masked:
                        r = r0 + lax.broadcasted_iota(jnp.int32, (rs, tq), 0)
                        c = lax.broadcasted_iota(jnp.int32, (rs, tq), 1)
                        sv = jnp.where(r >= c, sv, NEG)
                    m_prev = m_s[h, rr, :]
                    m_new = jnp.maximum(m_prev, jnp.max(sv, axis=1, keepdims=True))
                    p_s[h, rr, :] = jnp.exp(sv - m_new).astype(BF16)
                    a_s[h, rr, :] = jnp.exp(m_prev - m_new)
                    m_s[h, rr, :] = m_new
            for h in range(hp):
                hq = slice(h * HEAD_PAD, (h + 1) * HEAD_PAD)
                acc_s[h] = a_s[h] * acc_s[h] + _dot(p_s[h], v_ref[rows, hq], NN)

        def loop(j, c):
            blk(j, False)
            return c

        lax.fori_loop(0, i, loop, 0)
        blk(i, True)
        for h in range(hp):
            den = acc_s[h, :, 128:256]
            o_ref[:, h * 128:(h + 1) * 128] = (acc_s[h, :, 0:128] / den).astype(BF16)
            lse_ref[h] = m_s[h] + jnp.log(den[:, 0:1])

    return pl.pallas_call(
        body, name="mla_attn_fwd", grid=(B, N_HEADS // hp, nq),
        in_specs=[BS((tq, hp * HEAD_PAD), lambda b, h, i: (b * nq + i, h)),
                  BS((S, hp * HEAD_PAD), lambda b, h, i: (b, h)),
                  BS((S, hp * HEAD_PAD), lambda b, h, i: (b, h))],
        out_specs=[BS((tq, hp * 128), lambda b, h, i: (b * nq + i, h)),
                   BS((hp, tq, 1), lambda b, h, i: (h, b * nq + i, 0))],
        out_shape=[jax.ShapeDtypeStruct((T, 512), BF16), jax.ShapeDtypeStruct((N_HEADS, T, 1), F32)],
        scratch_shapes=[pltpu.VMEM((hp, tq, 1), F32), pltpu.VMEM((hp, tq, HEAD_PAD), F32), pltpu.VMEM((hp, tq, tq), F32),
                        pltpu.VMEM((hp, tq, tq), BF16), pltpu.VMEM((hp, tq, 1), F32)],
        compiler_params=_arb(3))(Q, K, V)


def mla_attn_bwd(Q, K, V, O, dO, LSE, B, S, tq=512, hp=ATTN_HEADS_PER_STEP):
    T = B * S
    tq = min(tq, S)
    nq = S // tq

    rs = min(ATTN_STRIP, tq)

    def body(q_ref, k_ref, v_ref, o_ref, do_ref, lse_ref, dq_ref, dk_ref, dv_ref, delta_s, dq_s, dk_s, dv_s, s_s, dp_s, p_s, ds_s):
        j = pl.program_id(2)

        @pl.when(j == 0)
        def _():
            dq_s[...] = jnp.zeros_like(dq_s)
            for h in range(hp):
                sl = slice(h * 128, (h + 1) * 128)
                delta_s[h] = jnp.sum(do_ref[:, sl] * o_ref[:, sl].astype(F32), axis=1, keepdims=True)

        dk_s[...] = jnp.zeros_like(dk_s)
        dv_s[...] = jnp.zeros_like(dv_s)

        def step(i, c):
            rows = pl.ds(pl.multiple_of(i * tq, tq), tq)
            for h in range(hp):
                sq, sv = slice(h * HEAD_PAD, (h + 1) * HEAD_PAD), slice(h * 128, (h + 1) * 128)
                s_s[h] = _dot(q_ref[rows, sq], k_ref[:, sq], NT)
                dp_s[h] = _dot(do_ref[rows, sv].astype(BF16), v_ref[:, h * HEAD_PAD:h * HEAD_PAD + 128], NT)
            for h in range(hp):
                for r0 in range(0, tq, rs):
                    rr = slice(r0, r0 + rs)
                    seq_rows = pl.ds(pl.multiple_of(i * tq + r0, rs), rs)
                    r = i * tq + r0 + lax.broadcasted_iota(jnp.int32, (rs, tq), 0)
                    cc = j * tq + lax.broadcasted_iota(jnp.int32, (rs, tq), 1)
                    p = jnp.where(r >= cc, jnp.exp(s_s[h, rr, :] - lse_ref[h, seq_rows, :]), 0.0)
                    p_s[h, rr, :] = p.astype(BF16)
                    ds_s[h, rr, :] = (p * (dp_s[h, rr, :] - delta_s[h, seq_rows, :])).astype(BF16)
            for h in range(hp):
                sq, sv = slice(h * HEAD_PAD, (h + 1) * HEAD_PAD), slice(h * 128, (h + 1) * 128)
                dv_s[:, sv] += _dot(p_s[h], do_ref[rows, sv].astype(BF16), TN)
                dk_s[:, sq] += _dot(ds_s[h], q_ref[rows, sq], TN)
                dq_s[rows, sq] += _dot(ds_s[h], k_ref[:, sq], NN)
            return c

        lax.fori_loop(j, nq, step, 0)
        dk_ref[...] = dk_s[...].astype(BF16)
        dv_ref[...] = dv_s[...].astype(BF16)

        @pl.when(j == nq - 1)
        def _():
            dq_ref[...] = dq_s[...].astype(BF16)

    seq = lambda c: BS((S, c), lambda b, h, j: (b, h))
    blk = lambda c: BS((tq, c), lambda b, h, j: (b * nq + j, h))
    return pl.pallas_call(
        body, name="mla_attn_bwd", grid=(B, N_HEADS // hp, nq),
        in_specs=[seq(hp * HEAD_PAD), blk(hp * HEAD_PAD), blk(hp * HEAD_PAD), seq(hp * 128), seq(hp * 128),
                  BS((hp, S, 1), lambda b, h, j: (h, b, 0))],
        out_specs=[seq(hp * HEAD_PAD), blk(hp * HEAD_PAD), blk(hp * 128)],
        out_shape=[jax.ShapeDtypeStruct((T, 1024), BF16), jax.ShapeDtypeStruct((T, 1024), BF16),
                   jax.ShapeDtypeStruct((T, 512), BF16)],
        scratch_shapes=[pltpu.VMEM((hp, S, 1), F32), pltpu.VMEM((S, hp * HEAD_PAD), F32),
                        pltpu.VMEM((tq, hp * HEAD_PAD), F32), pltpu.VMEM((tq, hp * 128), F32),
                        pltpu.VMEM((hp, tq, tq), F32), pltpu.VMEM((hp, tq, tq), F32),
                        pltpu.VMEM((hp, tq, tq), BF16), pltpu.VMEM((hp, tq, tq), BF16)],
        compiler_params=_arb(3))(Q, K, V, O, dO, LSE)


def mla_proj_bwd(dQ, dK, dV, cos_t, sin_t, P, dab, wq, wkv, gq, gkv, tm=512):
    T = P.shape[0]
    tm = min(tm, T)

    def norm_bwd(x, dy, g):
        r = lax.rsqrt(jnp.mean(x * x, axis=-1, keepdims=True) + EPS)
        xh = x * r
        dxh = dy * g
        return r * (dxh - xh * jnp.mean(dxh * xh, axis=-1, keepdims=True)), jnp.sum(dy * xh, axis=0, keepdims=True)

    def body(dq_ref, dk_ref, dv_ref, c_ref, s_ref, p_ref, dab_ref, wq_ref, wkv_ref, gq_ref, gkv_ref,
             ql_ref, kvl_ref, o_ref, aq_ref, akv_ref):
        @pl.when(pl.program_id(0) == 0)
        def _():
            aq_ref[...] = jnp.zeros_like(aq_ref)
            akv_ref[...] = jnp.zeros_like(akv_ref)

        cos_row, sin_row = c_ref[...], s_ref[...]
        kr = jnp.zeros((tm, 128), F32)
        for h in range(N_HEADS):
            lo = h * HEAD_PAD
            ql_ref[:, lo:lo + 128] = (dq_ref[:, lo:lo + 128].astype(F32) * MLA_SCALE).astype(BF16)
            ql_ref[:, lo + 128:lo + 256] = (_rope_bwd(dq_ref[:, lo + 128:lo + 256].astype(F32), cos_row, sin_row) * MLA_SCALE).astype(BF16)
            kvl_ref[:, h * 128:(h + 1) * 128] = dk_ref[:, lo:lo + 128]
            kr = kr + dk_ref[:, lo + 128:lo + 256].astype(F32)
        kvl_ref[:, 512:] = dv_ref[...]
        dqn = _dot(ql_ref[...], wq_ref[...], NN)
        dkvn = _dot(kvl_ref[...], wkv_ref[...], NT)
        dcq, ggq = norm_bwd(p_ref[:, :Q_LORA].astype(F32), dqn, gq_ref[...])
        dckv, ggkv = norm_bwd(p_ref[:, Q_LORA:640].astype(F32), dkvn, gkv_ref[...])
        aq_ref[...] += ggq
        akv_ref[...] += ggkv
        o_ref[:, :Q_LORA] = dcq.astype(BF16)
        o_ref[:, Q_LORA:640] = dckv.astype(BF16)
        o_ref[:, 640:768] = _rope_bwd(kr, cos_row, sin_row).astype(BF16)
        o_ref[:, 768:896] = dab_ref[...]
        o_ref[:, 896:1024] = jnp.zeros((tm, 128), BF16)

    rowb = lambda c: BS((tm, c), lambda i: (i, 0))
    full = lambda r, c: BS((r, c), lambda i: (0, 0))
    return pl.pallas_call(
        body, name="mla_proj_bwd", grid=(T // tm,),
        in_specs=[rowb(1024), rowb(1024), rowb(512), rowb(128), rowb(128), rowb(1024), rowb(128),
                  full(1024, Q_LORA), full(KV_LORA, 1024), full(1, Q_LORA), full(1, KV_LORA)],
        out_specs=[rowb(1024), rowb(1024), rowb(1024), full(1, Q_LORA), full(1, KV_LORA)],
        out_shape=[jax.ShapeDtypeStruct((T, 1024), BF16)] * 3
        + [jax.ShapeDtypeStruct((1, Q_LORA), F32), jax.ShapeDtypeStruct((1, KV_LORA), F32)],
        compiler_params=_arb(1))(dQ, dK, dV, cos_t, sin_t, P, dab, wq, wkv, gq, gkv)


def _mem_probs(qh, kh):
    s = _dot(qh, kh, NT) * MEM_SCALE
    p = jnp.exp(s - jnp.max(s, axis=1, keepdims=True))
    return p / jnp.sum(p, axis=1, keepdims=True)


def mem_attn_fwd(P, MKV, B, S, M, tq=512):
    T = B * S
    tq = min(tq, S)
    nq = S // tq

    def body(q_ref, kv_ref, o_ref):
        for h in range(N_HEADS):
            sl = slice(h * 128, (h + 1) * 128)
            p = _mem_probs(q_ref[:, sl].astype(BF16), kv_ref[:, sl])
            o_ref[:, sl] = _dot(p.astype(BF16), kv_ref[:, 512 + h * 128:512 + (h + 1) * 128], NN).astype(BF16)

    return pl.pallas_call(
        body, name="mem_attn_fwd", grid=(B, nq),
        in_specs=[BS((tq, 512), lambda b, i: (b * nq + i, OFF_MEMQ // 512)), BS((M, 1024), lambda b, i: (b, 0))],
        out_specs=BS((tq, 512), lambda b, i: (b * nq + i, 0)),
        out_shape=jax.ShapeDtypeStruct((T, 512), BF16), compiler_params=_arb(2))(P, MKV)


def mem_attn_bwd(P, MKV, dO, B, S, M, tq=512):
    T = B * S
    tq = min(tq, S)
    nq = S // tq

    def body(q_ref, kv_ref, do_ref, dq_ref, dkv_ref):
        @pl.when(pl.program_id(1) == 0)
        def _():
            dkv_ref[...] = jnp.zeros_like(dkv_ref)

        for h in range(N_HEADS):
            sl = slice(h * 128, (h + 1) * 128)
            sv = slice(512 + h * 128, 512 + (h + 1) * 128)
            qh = q_ref[:, sl].astype(BF16)
            kh = kv_ref[:, sl]
            do = do_ref[:, sl].astype(BF16)
            p = _mem_probs(qh, kh)
            dkv_ref[:, sv] += _dot(p.astype(BF16), do, TN)
            dp = _dot(do, kv_ref[:, sv], NT)
            ds = (p * (dp - jnp.sum(dp * p, axis=1, keepdims=True)) * MEM_SCALE).astype(BF16)
            dq_ref[:, sl] = _dot(ds, kh, NN).astype(BF16)
            dkv_ref[:, sl] += _dot(ds, qh, TN)

    return pl.pallas_call(
        body, name="mem_attn_bwd", grid=(B, nq),
        in_specs=[BS((tq, 512), lambda b, i: (b * nq + i, OFF_MEMQ // 512)), BS((M, 1024), lambda b, i: (b, 0)),
                  BS((tq, 512), lambda b, i: (b * nq + i, 0))],
        out_specs=[BS((tq, 512), lambda b, i: (b * nq + i, 0)), BS((M, 1024), lambda b, i: (b, 0))],
        out_shape=[jax.ShapeDtypeStruct((T, 512), BF16), jax.ShapeDtypeStruct((B * M, 1024), F32)],
        compiler_params=_arb(2))(P, MKV, dO)


def gain_grad(x, dy, name, tm=256):
    T, n = x.shape
    tm = min(tm, T)

    def body(x_ref, dy_ref, o_ref):
        @pl.when(pl.program_id(0) == 0)
        def _():
            o_ref[...] = jnp.zeros_like(o_ref)

        xv = x_ref[...]
        xh = xv * lax.rsqrt(jnp.mean(xv * xv, axis=-1, keepdims=True) + EPS)
        o_ref[...] += jnp.sum(dy_ref[...] * xh, axis=0, keepdims=True)

    return pl.pallas_call(
        body, name=name, grid=(T // tm,),
        in_specs=[BS((tm, n), lambda i: (i, 0))] * 2, out_specs=BS((1, n), lambda i: (0, 0)),
        out_shape=jax.ShapeDtypeStruct((1, n), F32), compiler_params=_arb(1))(x, dy)


def _conv_silu(x, w, t):
    y = x * w[3:4, :]
    for s in range(1, GDN_CONV):
        y = y + jnp.where(t >= s, pltpu.roll(x, s, 0), 0.0) * w[3 - s:4 - s, :]
    return y, _sigmoid(y)


def gdn_prep_fwd(P, conv_w, B, S):
    T = B * S

    def body(x_ref, w_ref, o_ref):
        kind = pl.program_id(1)
        t = lax.broadcasted_iota(jnp.int32, (S, 1), 0)
        y, sg = _conv_silu(x_ref[...].astype(F32), w_ref[...], t)
        a = y * sg
        scale = jnp.where(kind == 0, GDN_SCALE, 1.0).astype(F32)
        for h in range(N_HEADS):
            sl = slice(h * 128, (h + 1) * 128)
            seg = a[:, sl]
            n = lax.rsqrt(jnp.sum(seg * seg, axis=-1, keepdims=True) + EPS)
            o_ref[:, sl] = jnp.where(kind < 2, seg * (n * scale), seg)

    return pl.pallas_call(
        body, name="gdn_prep_fwd", grid=(B, 3),
        in_specs=[BS((S, 512), lambda b, k: (b, OFF_GDN // 512 + k)), BS((GDN_CONV, 512), lambda b, k: (0, k))],
        out_specs=BS((S, 512), lambda b, k: (b, k)),
        out_shape=jax.ShapeDtypeStruct((T, GDN_QKV), F32), compiler_params=_arb(2))(P, conv_w)


def gdn_prep_bwd(P, dqkv, conv_w, B, S):
    T = B * S

    def body(x_ref, d_ref, w_ref, o_ref, gw_ref):
        kind = pl.program_id(0)

        @pl.when(pl.program_id(1) == 0)
        def _():
            gw_ref[...] = jnp.zeros_like(gw_ref)

        t = lax.broadcasted_iota(jnp.int32, (S, 1), 0)
        x = x_ref[...].astype(F32)
        w = w_ref[...]
        y, sg = _conv_silu(x, w, t)
        a = y * sg
        scale = jnp.where(kind == 0, GDN_SCALE, 1.0).astype(F32)
        das = []
        for h in range(N_HEADS):
            sl = slice(h * 128, (h + 1) * 128)
            seg, dseg = a[:, sl], d_ref[:, sl]
            n = lax.rsqrt(jnp.sum(seg * seg, axis=-1, keepdims=True) + EPS)
            dn = scale * (n * dseg - seg * (n * n * n) * jnp.sum(dseg * seg, axis=-1, keepdims=True))
            das.append(jnp.where(kind < 2, dn, dseg))
        dy = jnp.concatenate(das, axis=1) * (sg * (1.0 + y * (1.0 - sg)))
        dx = dy * w[3:4, :]
        gw_ref[3:4, :] += jnp.sum(dy * x, axis=0, keepdims=True)
        for s in range(1, GDN_CONV):
            dx = dx + jnp.where(t + s < S, pltpu.roll(dy, S - s, 0), 0.0) * w[3 - s:4 - s, :]
            gw_ref[3 - s:4 - s, :] += jnp.sum(dy * jnp.where(t >= s, pltpu.roll(x, s, 0), 0.0), axis=0, keepdims=True)
        o_ref[...] = dx.astype(BF16)

    return pl.pallas_call(
        body, name="gdn_prep_bwd", grid=(3, B),
        in_specs=[BS((S, 512), lambda k, b: (b, OFF_GDN // 512 + k)), BS((S, 512), lambda k, b: (b, k)),
                  BS((GDN_CONV, 512), lambda k, b: (0, k))],
        out_specs=[BS((S, 512), lambda k, b: (b, k)), BS((GDN_CONV, 512), lambda k, b: (0, k))],
        out_shape=[jax.ShapeDtypeStruct((T, GDN_QKV), BF16), jax.ShapeDtypeStruct((GDN_CONV, GDN_QKV), F32)],
        compiler_params=_arb(2))(P, dqkv, conv_w)


def _chunk_row(n_rows):
    return lax.broadcasted_iota(jnp.int32, (n_rows, 1), 0) % CHUNK


def gdn_gate_fwd(P, alog_row, dt_row, B, S):
    T = B * S

    def body(x_ref, al_ref, dt_ref, o_ref):
        x = x_ref[...].astype(F32)
        lane = lax.broadcasted_iota(jnp.int32, (1, 128), 1)
        g = jnp.where(lane < 4, -jnp.exp(al_ref[...]) * _softplus(x + dt_ref[...]), 0.0)
        t = _chunk_row(S)
        for s in (1, 2, 4, 8, 16, 32):
            g = g + jnp.where(t >= s, pltpu.roll(g, s, 0), 0.0)
        o_ref[...] = jnp.where(lane < 4, g, jnp.where(lane < 8, _sigmoid(x), 0.0))

    row = BS((1, 128), lambda b: (0, 0))
    return pl.pallas_call(
        body, name="gdn_gate_fwd", grid=(B,),
        in_specs=[BS((S, 128), lambda b: (b, 768 // 128)), row, row], out_specs=BS((S, 128), lambda b: (b, 0)),
        out_shape=jax.ShapeDtypeStruct((T, 128), F32), compiler_params=_arb(1))(P, alog_row, dt_row)


def gdn_gate_bwd(P, dGB, alog_row, dt_row, B, S):
    T = B * S

    def body(x_ref, d_ref, al_ref, dt_ref, o_ref, acc_ref):
        @pl.when(pl.program_id(0) == 0)
        def _():
            acc_ref[...] = jnp.zeros_like(acc_ref)

        x, d = x_ref[...].astype(F32), d_ref[...]
        lane = lax.broadcasted_iota(jnp.int32, (1, 128), 1)
        z = x + dt_ref[...]
        coef = -jnp.exp(al_ref[...])
        g = coef * _softplus(z)
        da = jnp.where(lane < 4, d * coef * _sigmoid(z), 0.0)
        beta = _sigmoid(x)
        o_ref[...] = jnp.where(lane < 4, da, jnp.where(lane < 8, d * beta * (1.0 - beta), 0.0)).astype(BF16)
        acc_ref[0:1, :] += jnp.sum(jnp.where(lane < 4, d * g, 0.0), axis=0, keepdims=True)
        acc_ref[1:2, :] += jnp.sum(da, axis=0, keepdims=True)

    row = BS((1, 128), lambda b: (0, 0))
    return pl.pallas_call(
        body, name="gdn_gate_bwd", grid=(B,),
        in_specs=[BS((S, 128), lambda b: (b, 768 // 128)), BS((S, 128), lambda b: (b, 0)), row, row],
        out_specs=[BS((S, 128), lambda b: (b, 0)), BS((8, 128), lambda b: (0, 0))],
        out_shape=[jax.ShapeDtypeStruct((T, 128), BF16), jax.ShapeDtypeStruct((8, 128), F32)],
        compiler_params=_arb(1))(P, dGB, alog_row, dt_row)


def _chunk_masks(nc):
    r = lax.broadcasted_iota(jnp.int32, (nc, CHUNK, CHUNK), 1)
    c = lax.broadcasted_iota(jnp.int32, (nc, CHUNK, CHUNK), 2)
    return r >= c, r > c


def _chunk_local(q, k, gc, gr, beta, incl, strict):
    decay = jnp.exp(jnp.where(incl, gc - gr, NEG))
    kb = k * beta
    kbf = k.astype(BF16)
    m_kk = _bdot("gcd,gjd->gcj", kb.astype(BF16), kbf)
    l_mat = jnp.where(strict, m_kk * decay, 0.0)
    a_mat = _bdot("gcd,gjd->gcj", q.astype(BF16), kbf) * decay
    return decay, kb, l_mat, a_mat


WY_SPLIT_LEVELS = 2


def _split_bf16(x):
    hi = x.astype(BF16)
    return hi, (x - hi.astype(F32)).astype(BF16)


def _mm_split(ah, al, bh, bl):
    spec = "gij,gjk->gik"
    return _bdot(spec, ah, bh) + (_bdot(spec, ah, bl) + _bdot(spec, al, bh))


def gdn_chunk_fwd(qkv, GB, Grow, B, S, nc=8):
    T = B * S
    N = S // CHUNK
    nc = min(nc, N)
    nb = N // nc
    R = nc * CHUNK

    def body(q_ref, k_ref, v_ref, gb_ref, gr_ref, u_ref, w_ref, t_ref, a_ref):
        incl, strict = _chunk_masks(nc)
        eye = (lax.broadcasted_iota(jnp.int32, (nc, CHUNK, CHUNK), 1)
               == lax.broadcasted_iota(jnp.int32, (nc, CHUNK, CHUNK), 2)).astype(F32)
        for h in range(N_HEADS):
            sl = slice(h * 128, (h + 1) * 128)
            q = q_ref[:, sl].reshape(nc, CHUNK, 128)
            k = k_ref[:, sl].reshape(nc, CHUNK, 128)
            v = v_ref[:, sl].reshape(nc, CHUNK, 128)
            gc = gb_ref[:, h:h + 1].reshape(nc, CHUNK, 1)
            beta = gb_ref[:, 4 + h:5 + h].reshape(nc, CHUNK, 1)
            gr = gr_ref[h][:, None, :]
            _, kb, l_mat, a_mat = _chunk_local(q, k, gc, gr, beta, incl, strict)
            pw = -l_mat
            tinv = eye + pw
            for level in range(5):
                if level < WY_SPLIT_LEVELS:
                    ph, pl_ = _split_bf16(pw)
                    pw = _mm_split(ph, pl_, ph, pl_)
                    ph, pl_ = _split_bf16(pw)
                    th, tl = _split_bf16(tinv)
                    tinv = tinv + _mm_split(th, tl, ph, pl_)
                else:
                    ph = pw.astype(BF16)
                    pw = _bdot("gij,gjk->gik", ph, ph)
                    tinv = tinv + _bdot("gij,gjk->gik", tinv.astype(BF16), pw.astype(BF16))
            tb = tinv.astype(BF16)
            u = _bdot("gcj,gjv->gcv", tb, (v * beta).astype(BF16))
            w = _bdot("gcj,gjk->gck", tb, (kb * jnp.exp(gc)).astype(BF16))
            u_ref[:, sl] = u.reshape(R, 128)
            w_ref[:, sl] = w.reshape(R, 128).astype(BF16)
            t_ref[h] = jnp.swapaxes(tinv, 1, 2).astype(BF16)
            a_ref[h] = a_mat.astype(BF16)

    rowb = lambda c, j: BS((R, c), lambda b, n: (b * nb + n, j))
    mat = BS((None, N_HEADS, nc, CHUNK, CHUNK), lambda b, n: (b, 0, n, 0, 0))
    return pl.pallas_call(
        body, name="gdn_chunk_fwd", grid=(B, nb),
        in_specs=[rowb(512, 0), rowb(512, 1), rowb(512, 2), rowb(128, 0),
                  BS((None, N_HEADS, nc, CHUNK), lambda b, n: (b, 0, n, 0))],
        out_specs=[rowb(512, 0), rowb(512, 0), mat, mat],
        out_shape=[jax.ShapeDtypeStruct((T, 512), F32), jax.ShapeDtypeStruct((T, 512), BF16),
                   jax.ShapeDtypeStruct((B, N_HEADS, N, CHUNK, CHUNK), BF16),
                   jax.ShapeDtypeStruct((B, N_HEADS, N, CHUNK, CHUNK), BF16)],
        compiler_params=_arb(2))(qkv, qkv, qkv, GB, Grow)


SCAN_CHUNKS = 4


def gdn_scan_fwd(qkv3, U3, W3, GB3, A, B, S):
    N = S // CHUNK
    cps = SCAN_CHUNKS if N % SCAN_CHUNKS == 0 else 1

    def body(q_ref, k_ref, u_ref, w_ref, gb_ref, a_ref, o_ref, vn_ref, st_ref, s_s):
        @pl.when(pl.program_id(0) == 0)
        def _():
            s_s[...] = jnp.zeros_like(s_s)

        for c in range(cps):
            rows = slice(c * CHUNK, (c + 1) * CHUNK)
            for b in range(B):
                for h in range(N_HEADS):
                    sl = slice(h * 128, (h + 1) * 128)
                    st = s_s[b, h]
                    st_ref[b, h, c] = st
                    stb = st.astype(BF16)
                    g = gb_ref[b, rows, h:h + 1]
                    gl = g[CHUNK - 1:CHUNK, :]
                    qg = (q_ref[b, rows, sl] * jnp.exp(g)).astype(BF16)
                    on_state = _dot(jnp.concatenate([w_ref[b, rows, sl].astype(BF16), qg], axis=0), stb, NN)
                    vn = u_ref[b, rows, sl] - on_state[:CHUNK]
                    vnb = vn.astype(BF16)
                    kd_t = jnp.transpose(k_ref[b, rows, sl] * jnp.exp(gl - g)).astype(BF16)
                    on_vn = _dot(jnp.concatenate([a_ref[b, h, c].astype(BF16), kd_t], axis=0), vnb, NN)
                    vn_ref[b, rows, sl] = vnb
                    o_ref[b, rows, sl] = (on_state[CHUNK:] + on_vn[:CHUNK]).astype(BF16)
                    s_s[b, h] = st * jnp.exp(gl) + on_vn[CHUNK:]

    tok = lambda c, j: BS((B, cps * CHUNK, c), lambda n: (0, n, j))
    return pl.pallas_call(
        body, name="gdn_scan_fwd", grid=(N // cps,),
        in_specs=[tok(512, 0), tok(512, 1), tok(512, 0), tok(512, 0), tok(128, 0),
                  BS((B, N_HEADS, cps, CHUNK, CHUNK), lambda n: (0, 0, n, 0, 0))],
        out_specs=[tok(512, 0), tok(512, 0), BS((B, N_HEADS, cps, 128, 128), lambda n: (0, 0, n, 0, 0))],
        out_shape=[jax.ShapeDtypeStruct((B, S, 512), BF16), jax.ShapeDtypeStruct((B, S, 512), BF16),
                   jax.ShapeDtypeStruct((B, N_HEADS, N, 128, 128), F32)],
        scratch_shapes=[pltpu.VMEM((B, N_HEADS, 128, 128), F32)],
        compiler_params=_arb(1))(qkv3, qkv3, U3, W3, GB3, A)


def gdn_scan_bwd(dO3, qkv3, W3, Vn3, GB3, A, St, B, S):
    N = S // CHUNK
    cps = SCAN_CHUNKS if N % SCAN_CHUNKS == 0 else 1

    def body(do_ref, q_ref, k_ref, w_ref, vn_ref, gb_ref, a_ref, st_ref,
             du_ref, dw_ref, dq_ref, dk_ref, da_ref, dg_ref, ds_s):
        @pl.when(pl.program_id(0) == 0)
        def _():
            ds_s[...] = jnp.zeros_like(ds_s)

        lane = lax.broadcasted_iota(jnp.int32, (1, 128), 1)
        last = lax.broadcasted_iota(jnp.int32, (CHUNK, 1), 0) == CHUNK - 1
        for c in reversed(range(cps)):
            rows = slice(c * CHUNK, (c + 1) * CHUNK)
            for b in range(B):
                dg_all = jnp.zeros((CHUNK, 128), F32)
                for h in range(N_HEADS):
                    sl = slice(h * 128, (h + 1) * 128)
                    st = st_ref[b, h, c]
                    stb = st.astype(BF16)
                    dsn = ds_s[b, h]
                    dsnb = dsn.astype(BF16)
                    g = gb_ref[b, rows, h:h + 1]
                    gl = g[CHUNK - 1:CHUNK, :]
                    egl = jnp.exp(gl)
                    ekd = jnp.exp(gl - g)
                    eg = jnp.exp(g)
                    q, k = q_ref[b, rows, sl], k_ref[b, rows, sl]
                    kd = k * ekd
                    qg = q * eg
                    do = do_ref[b, rows, sl].astype(BF16)
                    vnb = vn_ref[b, rows, sl].astype(BF16)
                    dvn = _dot(a_ref[b, h, c].astype(BF16), do, TN) + _dot(kd.astype(BF16), dsnb, NN)
                    dvnb = dvn.astype(BF16)
                    do_on = _dot(do, jnp.concatenate([stb, vnb], axis=0), NT)
                    dqg = do_on[:, :128]
                    da_ref[b, h, c] = do_on[:, 128:]
                    dkd = _dot(vnb, dsnb, NT)
                    ds_s[b, h] = (_dot(qg.astype(BF16), do, TN) + egl * dsn - _dot(w_ref[b, rows, sl].astype(BF16), dvnb, TN))
                    du_ref[b, rows, sl] = dvnb
                    dw_ref[b, rows, sl] = (-_dot(dvnb, stb, NT)).astype(BF16)
                    dq_ref[b, rows, sl] = dqg * eg
                    dk_ref[b, rows, sl] = dkd * ekd
                    ddel = jnp.sum(dkd * kd, axis=1, keepdims=True)
                    dgl = jnp.sum(ddel, axis=0, keepdims=True) + jnp.sum(jnp.sum(st * dsn, axis=1, keepdims=True), axis=0, keepdims=True) * egl
                    col = jnp.sum(dqg * qg, axis=1, keepdims=True) - ddel + jnp.where(last, dgl, 0.0)
                    dg_all = jnp.where(lane == h, col, dg_all)
                dg_ref[b, rows, :] = dg_all

    steps = N // cps
    tok = lambda c, j: BS((B, cps * CHUNK, c), lambda n: (0, steps - 1 - n, j))
    mat = lambda d: BS((B, N_HEADS, cps, d, d), lambda n: (0, 0, steps - 1 - n, 0, 0))
    return pl.pallas_call(
        body, name="gdn_scan_bwd", grid=(steps,),
        in_specs=[tok(512, 0), tok(512, 0), tok(512, 1), tok(512, 0), tok(512, 0), tok(128, 0), mat(CHUNK), mat(128)],
        out_specs=[tok(512, 0), tok(512, 0), tok(512, 0), tok(512, 0), mat(CHUNK), tok(128, 0)],
        out_shape=[jax.ShapeDtypeStruct((B, S, 512), BF16)] * 2 + [jax.ShapeDtypeStruct((B, S, 512), F32)] * 2
        + [jax.ShapeDtypeStruct((B, N_HEADS, N, CHUNK, CHUNK), F32), jax.ShapeDtypeStruct((B, S, 128), F32)],
        scratch_shapes=[pltpu.VMEM((B, N_HEADS, 128, 128), F32)],
        compiler_params=_arb(1))(dO3, qkv3, qkv3, W3, Vn3, GB3, A, St)


def gdn_chunk_bwd(qkv, GB, Grow, Tinv, dA, dU, dW, dQ1, dK1, dG1, B, S, nc=8):
    T = B * S
    N = S // CHUNK
    nc = min(nc, N)
    nb = N // nc
    R = nc * CHUNK

    def body(q_ref, k_ref, v_ref, gb_ref, gr_ref, t_ref, da_ref, du_ref, dw_ref, dq1_ref, dk1_ref, dg1_ref, o_ref, dgb_ref):
        incl, strict = _chunk_masks(nc)
        lane = lax.broadcasted_iota(jnp.int32, (1, 128), 1)
        dg_all = dg1_ref[...]
        db_all = jnp.zeros((R, 128), F32)
        for h in range(N_HEADS):
            sl = slice(h * 128, (h + 1) * 128)
            q = q_ref[:, sl].reshape(nc, CHUNK, 128)
            k = k_ref[:, sl].reshape(nc, CHUNK, 128)
            v = v_ref[:, sl].reshape(nc, CHUNK, 128)
            gc = gb_ref[:, h:h + 1].reshape(nc, CHUNK, 1)
            beta = gb_ref[:, 4 + h:5 + h].reshape(nc, CHUNK, 1)
            gr = gr_ref[h][:, None, :]
            decay, kb, l_mat, a_mat = _chunk_local(q, k, gc, gr, beta, incl, strict)
            eg = jnp.exp(gc)
            kbg = kb * eg
            vb = v * beta
            tt = t_ref[h].astype(BF16)
            du = du_ref[:, sl].reshape(nc, CHUNK, 128).astype(BF16)
            dw = dw_ref[:, sl].reshape(nc, CHUNK, 128).astype(BF16)
            dvb = _bdot("gjc,gcv->gjv", tt, du)
            dkbg = _bdot("gjc,gck->gjk", tt, dw)
            dt = _bdot("gcv,gjv->gcj", du, vb.astype(BF16)) + _bdot("gck,gjk->gcj", dw, kbg.astype(BF16))
            tmp = _bdot("gca,gab->gcb", tt, dt.astype(BF16))
            dl = jnp.where(strict, -_bdot("gcb,gbd->gcd", tmp.astype(BF16), tt), 0.0)
            da = da_ref[h]
            dm = (dl * decay).astype(BF16)
            dqk = (da * decay).astype(BF16)
            kbf = k.astype(BF16)
            dkb = _bdot("gcj,gjd->gcd", dm, kbf) + dkbg * eg
            dk = (_bdot("gcj,gcd->gjd", dm, kb.astype(BF16)) + _bdot("gcj,gcd->gjd", dqk, q.astype(BF16))
                  + dk1_ref[:, sl].reshape(nc, CHUNK, 128) + dkb * beta)
            dq = _bdot("gcj,gjd->gcd", dqk, kbf) + dq1_ref[:, sl].reshape(nc, CHUNK, 128)
            e = dl * l_mat + da * a_mat
            dgc = (jnp.sum(e, axis=2, keepdims=True) - jnp.sum(jnp.swapaxes(e, 1, 2), axis=2, keepdims=True)
                   + jnp.sum(dkbg * kbg, axis=2, keepdims=True))
            dbeta = jnp.sum(dkb * k, axis=2, keepdims=True) + jnp.sum(dvb * v, axis=2, keepdims=True)
            o_ref[:, sl] = dq.reshape(R, 128)
            o_ref[:, 512 + h * 128:512 + (h + 1) * 128] = dk.reshape(R, 128)
            o_ref[:, 1024 + h * 128:1024 + (h + 1) * 128] = (dvb * beta).reshape(R, 128)
            dg_all = dg_all + jnp.where(lane == h, dgc.reshape(R, 1), 0.0)
            db_all = jnp.where(lane == 4 + h, dbeta.reshape(R, 1), db_all)
        t = _chunk_row(R)
        for s in (1, 2, 4, 8, 16, 32):
            dg_all = dg_all + jnp.where(t + s < CHUNK, pltpu.roll(dg_all, R - s, 0), 0.0)
        dgb_ref[...] = jnp.where(lane < 4, dg_all, db_all)

    rowb = lambda c, j: BS((R, c), lambda b, n: (b * nb + n, j))
    mat = BS((None, N_HEADS, nc, CHUNK, CHUNK), lambda b, n: (b, 0, n, 0, 0))
    return pl.pallas_call(
        body, name="gdn_chunk_bwd", grid=(B, nb),
        in_specs=[rowb(512, 0), rowb(512, 1), rowb(512, 2), rowb(128, 0),
                  BS((None, N_HEADS, nc, CHUNK), lambda b, n: (b, 0, n, 0)), mat, mat,
                  rowb(512, 0), rowb(512, 0), rowb(512, 0), rowb(512, 0), rowb(128, 0)],
        out_specs=[rowb(GDN_QKV, 0), rowb(128, 0)],
        out_shape=[jax.ShapeDtypeStruct((T, GDN_QKV), F32), jax.ShapeDtypeStruct((T, 128), F32)],
        compiler_params=_arb(2))(qkv, qkv, qkv, GB, Grow, Tinv, dA, dU, dW, dQ1, dK1, dG1)


def _gdn_out_norm(og, gg):
    outs, xhs, rs = [], [], []
    for h in range(N_HEADS):
        seg = og[:, h * 128:(h + 1) * 128]
        r = lax.rsqrt(jnp.mean(seg * seg, axis=-1, keepdims=True) + EPS)
        xh = seg * r
        outs.append(xh * gg)
        xhs.append(xh)
        rs.append(r)
    return outs, xhs, rs


def merge_fwd(o_mla, o_gdn, o_mem, P, x, tgt, w_out, g_gdn, g_fin, tm=512):
    T = x.shape[0]
    tm = min(tm, T)

    def body(om_ref, og_ref, oc_ref, gate_ref, x_ref, t_ref, w_ref, gg_ref, gf_ref, mix_ref, dx_ref, dxb_ref, sq_ref, gnf_ref):
        @pl.when(pl.program_id(0) == 0)
        def _():
            sq_ref[...] = jnp.zeros_like(sq_ref)
            gnf_ref[...] = jnp.zeros_like(gnf_ref)

        ogn, _, _ = _gdn_out_norm(og_ref[...].astype(F32), gg_ref[...])
        cat = jnp.concatenate([om_ref[...].astype(F32)] + ogn + [oc_ref[...].astype(F32)], axis=1)
        gt = gate_ref[...].astype(F32)
        mixed = (cat * (gt * _sigmoid(gt))).astype(BF16)
        mix_ref[...] = mixed
        x2 = x_ref[...] + _dot(mixed, w_ref[...], NN)
        r2 = lax.rsqrt(jnp.mean(x2 * x2, axis=-1, keepdims=True) + EPS)
        xh = x2 * r2
        gf = gf_ref[...]
        diff = xh * gf - t_ref[...]
        sq_ref[...] += jnp.sum(diff * diff, axis=0, keepdims=True)
        dy = diff * (1.0 / D_MODEL)
        gnf_ref[...] += jnp.sum(dy * xh, axis=0, keepdims=True)
        dxh = dy * gf
        dx = r2 * (dxh - xh * jnp.mean(dxh * xh, axis=-1, keepdims=True))
        dx_ref[...] = dx
        dxb_ref[...] = dx.astype(BF16)

    rowb = lambda c, j=0: BS((tm, c), lambda i: (i, j))
    full = lambda r, c: BS((r, c), lambda i: (0, 0))
    return pl.pallas_call(
        body, name="merge_fwd", grid=(T // tm,),
        in_specs=[rowb(512), rowb(512), rowb(512), rowb(D_MIX, OFF_GATE // D_MIX), rowb(D_MODEL), rowb(D_MODEL),
                  full(D_MIX, D_MODEL), full(1, 128), full(1, D_MODEL)],
        out_specs=[rowb(D_MIX), rowb(D_MODEL), rowb(D_MODEL), full(1, D_MODEL), full(1, D_MODEL)],
        out_shape=[jax.ShapeDtypeStruct((T, D_MIX), BF16), jax.ShapeDtypeStruct((T, D_MODEL), F32),
                   jax.ShapeDtypeStruct((T, D_MODEL), BF16),
                   jax.ShapeDtypeStruct((1, D_MODEL), F32), jax.ShapeDtypeStruct((1, D_MODEL), F32)],
        compiler_params=_arb(1))(o_mla, o_gdn, o_mem, P, x, tgt, w_out, g_gdn, g_fin)


def merge_bwd(dx2, o_mla, o_gdn, o_mem, P, w_out, g_gdn, tm=512):
    T = dx2.shape[0]
    tm = min(tm, T)

    def body(dx_ref, om_ref, og_ref, oc_ref, gate_ref, w_ref, gg_ref, dgate_ref, dom_ref, dog_ref, doc_ref, ggn_ref):
        @pl.when(pl.program_id(0) == 0)
        def _():
            ggn_ref[...] = jnp.zeros_like(ggn_ref)

        gg = gg_ref[...]
        dmix = _dot(dx_ref[...].astype(BF16), w_ref[...], NT)
        ogn, xhs, rs = _gdn_out_norm(og_ref[...].astype(F32), gg)
        cat = jnp.concatenate([om_ref[...].astype(F32)] + ogn + [oc_ref[...].astype(F32)], axis=1)
        gt = gate_ref[...].astype(F32)
        sg = _sigmoid(gt)
        dgate_ref[...] = (dmix * cat * (sg * (1.0 + gt * (1.0 - sg)))).astype(BF16)
        dcat = dmix * (gt * sg)
        dom_ref[...] = dcat[:, :512].astype(BF16)
        doc_ref[...] = dcat[:, 1024:].astype(BF16)
        acc = jnp.zeros((1, 128), F32)
        for h in range(N_HEADS):
            dseg = dcat[:, 512 + h * 128:512 + (h + 1) * 128]
            acc = acc + jnp.sum(dseg * xhs[h], axis=0, keepdims=True)
            dxh = dseg * gg
            dog_ref[:, h * 128:(h + 1) * 128] = (rs[h] * (dxh - xhs[h] * jnp.mean(dxh * xhs[h], axis=-1, keepdims=True))).astype(BF16)
        ggn_ref[...] += acc

    rowb = lambda c, j=0: BS((tm, c), lambda i: (i, j))
    full = lambda r, c: BS((r, c), lambda i: (0, 0))
    return pl.pallas_call(
        body, name="merge_bwd", grid=(T // tm,),
        in_specs=[rowb(D_MODEL), rowb(512), rowb(512), rowb(512), rowb(D_MIX, OFF_GATE // D_MIX),
                  full(D_MIX, D_MODEL), full(1, 128)],
        out_specs=[rowb(D_MIX), rowb(512), rowb(512), rowb(512), full(1, 128)],
        out_shape=[jax.ShapeDtypeStruct((T, D_MIX), BF16)] + [jax.ShapeDtypeStruct((T, 512), BF16)] * 3
        + [jax.ShapeDtypeStruct((1, 128), F32)],
        compiler_params=_arb(1))(dx2, o_mla, o_gdn, o_mem, P, w_out, g_gdn)


def in_proj_bwd(dP, wp, x, dx2, gain, after, tm=512):
    T, n = x.shape
    tm = min(tm, T)
    k = len(dP)
    widths = [p.shape[1] for p in dP]
    offs = [sum(widths[:i]) for i in range(k)]

    def body(*refs):
        w_ref, x_ref, dx2_ref, g_ref = refs[k:k + 4]
        o_ref, acc_ref = refs[-2:]

        @pl.when(pl.program_id(0) == 0)
        def _():
            acc_ref[...] = jnp.zeros_like(acc_ref)

        dy = None
        for a_ref, off, w in zip(refs[:k], offs, widths):
            d = _dot(a_ref[...], w_ref[off:off + w, :], NN)
            dy = d if dy is None else dy + d
        xv = x_ref[...]
        r = lax.rsqrt(jnp.mean(xv * xv, axis=-1, keepdims=True) + EPS)
        xh = xv * r
        acc_ref[...] += jnp.sum(dy * xh, axis=0, keepdims=True)
        dxh = dy * g_ref[...]
        o_ref[...] = dx2_ref[...] + r * (dxh - xh * jnp.mean(dxh * xh, axis=-1, keepdims=True))

    rowb = BS((tm, n), lambda i: (i, 0))
    full = BS((1, n), lambda i: (0, 0))
    return pl.pallas_call(
        body, name="in_proj_bwd", grid=(T // tm,),
        in_specs=[BS((tm, w), lambda i: (i, 0)) for w in widths]
        + [BS(wp.shape, lambda i: (0, 0), pipeline_mode=pl.Buffered(1)), rowb, rowb, full, BS(memory_space=pl.ANY)],
        out_specs=[rowb, full], out_shape=[jax.ShapeDtypeStruct((T, n), F32), jax.ShapeDtypeStruct((1, n), F32)],
        compiler_params=_arb(1))(*dP, wp, x, dx2, gain, after)


W_IN_SHARD = D_IN // 4
_GDN0 = Q_LORA + KV_LORA + MLA_ROPE
_AB0 = _GDN0 + GDN_QKV
_MEMQ0 = _AB0 + 2 * N_HEADS
_GATE0 = _MEMQ0 + N_HEADS * MEM_DH


def _w_in_row_map():
    a, m, gt = _AB0 - 2 * W_IN_SHARD, _MEMQ0 - 2 * W_IN_SHARD, _GATE0 - 2 * W_IN_SHARD
    e0 = OFF_GDN + W_IN_SHARD - _GDN0
    e1 = e0 + W_IN_SHARD
    e2 = OFF_GATE + W_IN_SHARD - gt
    return [(0, 0, 0, 672), (0, 672, 704, 32), (2, a, 768, m - a), (2, m, OFF_MEMQ, gt - m), (0, _GDN0, OFF_GDN, W_IN_SHARD - _GDN0),
            (1, 0, e0, W_IN_SHARD), (2, 0, e1, a), (2, gt, OFF_GATE, W_IN_SHARD - gt), (3, 0, e2, W_IN_SHARD)]


_W_IN_ZERO_ROWS = [(672, 32), (736, 32), (776, 248)]
W_IN_LANES = 256


EARLY_WINDOWS = ((0, 0, 704), (2, 80, 32))
W_IN_ROW_BLOCK = 512


def pad_w_in_t_early(windows):
    per_half = windows[0].shape[3] // W_IN_LANES

    def body(*refs):
        o_ref = refs[-1]
        for r0, n in _W_IN_ZERO_ROWS:
            o_ref[r0:r0 + n, :] = jnp.zeros((n, W_IN_LANES), o_ref.dtype)
        for q, src, dst, n in _w_in_row_map():
            if dst < OFF_MEMQ:
                (s_ref, row0), = [(r, w[1]) for r, w in zip(refs, EARLY_WINDOWS) if w[0] == q]
                o_ref[dst:dst + n, :] = s_ref[src - row0:src - row0 + n, :]

    return pl.pallas_call(
        body, name="pad_w_in_t_early", grid=(D_MODEL // W_IN_LANES,),
        in_specs=[BS((None, None, w.shape[2], W_IN_LANES), lambda j: (0, j // per_half, 0, j % per_half)) for w in windows],
        out_specs=BS((OFF_MEMQ, W_IN_LANES), lambda j: (0, j)),
        out_shape=jax.ShapeDtypeStruct((N_PAD, D_MODEL), windows[0].dtype), compiler_params=_arb(1))(*windows)


def pad_w_in_t_rest(shards, wp):
    per_half = shards.shape[3] // W_IN_LANES
    steps = (N_PAD - OFF_MEMQ) // W_IN_ROW_BLOCK

    def body(s_ref, w_in, o_ref):
        for i in range(steps):
            @pl.when(pl.program_id(1) == i)
            def _(b0=OFF_MEMQ + i * W_IN_ROW_BLOCK):
                for q, src, dst, n in _w_in_row_map():
                    lo, hi = max(dst, b0), min(dst + n, b0 + W_IN_ROW_BLOCK)
                    if lo < hi:
                        o_ref[lo - b0:hi - b0, :] = s_ref[q, src + lo - dst:src + hi - dst, :]

    return pl.pallas_call(
        body, name="pad_w_in_t_rest", grid=(D_MODEL // W_IN_LANES, steps),
        in_specs=[BS((N_CHIPS, None, W_IN_SHARD, W_IN_LANES), lambda j, i: (0, j // per_half, 0, j % per_half)), ANY],
        out_specs=BS((W_IN_ROW_BLOCK, W_IN_LANES), lambda j, i: (OFF_MEMQ // W_IN_ROW_BLOCK + i, j)), input_output_aliases={1: 0},
        out_shape=jax.ShapeDtypeStruct((N_PAD, D_MODEL), shards.dtype), compiler_params=_arb(2))(shards, wp)


def unpad_w_in_t(g):
    def body(g_ref, o_ref):
        for q, src, dst, n in _w_in_row_map():
            o_ref[q, src:src + n, :] = g_ref[dst:dst + n, :]

    return pl.pallas_call(
        body, name="unpad_w_in_t", grid=(D_MODEL // W_IN_LANES,),
        in_specs=[BS((N_PAD, W_IN_LANES), lambda j: (0, j))], out_specs=BS((N_CHIPS, W_IN_SHARD, W_IN_LANES), lambda j: (0, 0, j)),
        out_shape=jax.ShapeDtypeStruct((N_CHIPS, W_IN_SHARD, D_MODEL), g.dtype), compiler_params=_arb(1))(g)


def _perm_w_kv_b(s):
    return jnp.concatenate([s[h, :, :128] for h in range(N_HEADS)] + [s[h, :, 128:] for h in range(N_HEADS)], axis=1)


def _unperm_w_kv_b(g):
    return jnp.stack([jnp.concatenate([g[:, h * 128:(h + 1) * 128], g[:, 512 + h * 128:512 + (h + 1) * 128]], axis=1)
                      for h in range(N_HEADS)])


def _lane_row(v4):
    return jnp.pad(v4.reshape(1, -1).astype(F32), ((0, 0), (0, 128 - v4.size)))


N_CHIPS = 4
MESH = pl.DeviceIdType.MESH
ANY = BS(memory_space=pl.ANY)


def _place():
    return lax.axis_index("x"), lax.axis_index("y"), lax.axis_index("c")


def _other_chips(x, y):
    return [(1 - x, y), (x, 1 - y), (1 - x, 1 - y)]


def _half(split, which):
    axis, size = split
    ds = pl.ds(pl.multiple_of(which * size, 16 if axis == 0 else 128), size)
    return (ds, slice(None)) if axis == 0 else (slice(None), ds)


SEM = BS(memory_space=pltpu.SEMAPHORE)
HBM = BS(memory_space=pltpu.HBM)
_IN_HBM = lambda a: pltpu.with_memory_space_constraint(a, pltpu.HBM)
_SIDE_EFFECT = pltpu.SideEffectType.DATAFLOW_SIDE_EFFECTING


def _late_gather_copies(s_refs, l_refs, send_sems, recv_sems, local_sems, with_arrivals):
    x, y, c = _place()
    sends, recvs, locals_ = [], [], []
    for i, (s_ref, l_ref) in enumerate(zip(s_refs, l_refs)):
        locals_.append(pltpu.make_async_copy(s_ref, l_ref.at[2 * x + y], local_sems.at[i]))
        for j, (px, py) in enumerate(_other_chips(x, y)):
            k = 3 * i + j
            sends.append(pltpu.make_async_remote_copy(src_ref=s_ref, dst_ref=l_ref.at[2 * x + y], send_sem=send_sems.at[k],
                                                      recv_sem=recv_sems.at[k], device_id=(px, py, c), device_id_type=MESH))
            if with_arrivals:
                recvs.append(pltpu.make_async_remote_copy(src_ref=s_ref, dst_ref=l_ref.at[2 * px + py], send_sem=send_sems.at[k],
                                                          recv_sem=recv_sems.at[k], device_id=(px, py, c), device_id_type=MESH))
    return sends, recvs, locals_


def late_gather_start(shards, after, name):
    n = len(shards)

    def body(*refs):
        s_refs, l_refs = refs[:n], refs[n:2 * n]
        send_sems, recv_sems, local_sems = refs[2 * n + 1:2 * n + 4]
        token = refs[-1]
        sends, _, locals_ = _late_gather_copies(s_refs, l_refs, send_sems, recv_sems, local_sems, False)
        for cp in locals_ + sends:
            cp.start()
        token[...] = jnp.zeros_like(token)

    lands = [lax.empty((N_CHIPS,) + s.shape, s.dtype) for s in shards]
    hbm_like = lambda a: pltpu.HBM(a.shape, a.dtype)
    out = pl.pallas_call(
        body, name=name,
        out_shape=[pltpu.SemaphoreType.DMA((3 * n,)), pltpu.SemaphoreType.DMA((3 * n,)), pltpu.SemaphoreType.DMA((n,))]
        + [hbm_like(s) for s in shards] + [hbm_like(l) for l in lands] + [jax.ShapeDtypeStruct((8, 128), F32)],
        in_specs=[HBM] * (2 * n) + [BS(memory_space=pl.ANY)], out_specs=[SEM] * 3 + [HBM] * (2 * n) + [BS(memory_space=pltpu.VMEM)],
        input_output_aliases={i: 3 + i for i in range(2 * n)},
        compiler_params=pltpu.CompilerParams(has_side_effects=_SIDE_EFFECT))(
            *[_IN_HBM(s) for s in shards], *[_IN_HBM(l) for l in lands], after)
    return out[:3], out[3:3 + n], out[3 + n:3 + 2 * n], out[-1]


def late_gather_wait(sems, shards, lands, after, name):
    n = len(shards)

    def body(*refs):
        s_refs, l_refs = refs[:n], refs[n:2 * n]
        send_sems, recv_sems, local_sems = refs[2 * n:2 * n + 3]
        sends, recvs, locals_ = _late_gather_copies(s_refs, l_refs, send_sems, recv_sems, local_sems, True)
        for cp in locals_:
            cp.wait()
        for cp in sends:
            cp.wait_send()
        for cp in recvs:
            cp.wait_recv()

    hbm_like = lambda a: pltpu.HBM(a.shape, a.dtype)
    out = pl.pallas_call(
        body, name=name, out_shape=[hbm_like(s) for s in shards] + [hbm_like(l) for l in lands],
        in_specs=[HBM] * (2 * n) + [SEM] * 3 + [BS(memory_space=pl.ANY)], out_specs=[HBM] * (2 * n),
        input_output_aliases={i: i for i in range(2 * n)},
        compiler_params=pltpu.CompilerParams(has_side_effects=_SIDE_EFFECT))(*shards, *lands, *sems, after)
    return out[n:]


def _sends(x, y, only):
    return None if only is None else (2 * x + y == only)


def _block(x, y, only):
    return 2 * x + y if only is None else 0


def _if(cond, fn):
    if cond is None:
        fn()
    else:
        pl.when(cond)(fn)


def _half_gather_copies(s_refs, l_refs, only, send_sems, recv_sems, with_arrivals):
    x, y, c = _place()
    out = []
    for i, (s_ref, l_ref, who) in enumerate(zip(s_refs, l_refs, only)):
        sends, recvs, peer_sends = [], [], []
        for j, (px, py) in enumerate(_other_chips(x, y)):
            k = 3 * i + j
            sends.append(pltpu.make_async_remote_copy(src_ref=s_ref.at[c], dst_ref=l_ref.at[_block(x, y, who), c],
                                                      send_sem=send_sems.at[k], recv_sem=recv_sems.at[k], device_id=(px, py, c),
                                                      device_id_type=MESH))
            if with_arrivals:
                recvs.append(pltpu.make_async_remote_copy(src_ref=s_ref.at[c], dst_ref=l_ref.at[_block(px, py, who), c],
                                                          send_sem=send_sems.at[k], recv_sem=recv_sems.at[k],
                                                          device_id=(px, py, c), device_id_type=MESH))
                peer_sends.append(_sends(px, py, who))
        out.append((sends, recvs, peer_sends))
    return out


def half_gather_start(shards, only, name, after=()):
    n = len(shards)

    def body(*refs):
        s_refs, l_refs = refs[:n], refs[n:2 * n]
        send_sems, recv_sems, local_sems = refs[2 * n + len(after):2 * n + len(after) + 3]
        token = refs[-1]
        x, y, _ = _place()
        copies = _half_gather_copies(s_refs, l_refs, only, send_sems, recv_sems, False)
        for i, who in enumerate(only):
            def go(i=i, who=who):
                pltpu.make_async_copy(s_refs[i], l_refs[i].at[_block(x, y, who)], local_sems.at[i]).start()
                for cp in copies[i][0]:
                    cp.start()

            _if(_sends(x, y, who), go)
        token[...] = jnp.zeros_like(token)

    lands = [lax.empty((N_CHIPS if who is None else 1,) + s.shape, s.dtype) for s, who in zip(shards, only)]
    hbm_like = lambda a: pltpu.HBM(a.shape, a.dtype)
    out = pl.pallas_call(
        body, name=name,
        out_shape=[pltpu.SemaphoreType.DMA((3 * n,)), pltpu.SemaphoreType.DMA((3 * n,)), pltpu.SemaphoreType.DMA((n,))]
        + [hbm_like(s) for s in shards] + [hbm_like(l) for l in lands] + [jax.ShapeDtypeStruct((8, 128), F32)],
        in_specs=[HBM] * (2 * n) + [BS(memory_space=pl.ANY)] * len(after),
        out_specs=[SEM] * 3 + [HBM] * (2 * n) + [BS(memory_space=pltpu.VMEM)],
        input_output_aliases={i: 3 + i for i in range(2 * n)},
        compiler_params=pltpu.CompilerParams(has_side_effects=_SIDE_EFFECT))(
            *[_IN_HBM(s) for s in shards], *[_IN_HBM(l) for l in lands], *after)
    return out[:3], out[3:3 + n], out[3 + n:3 + 2 * n], out[-1]


def half_gather_wait(sems, shards, lands, only, after, name):
    n = len(shards)

    def body(*refs):
        s_refs, l_refs = refs[:n], refs[n:2 * n]
        send_sems, recv_sems, local_sems = refs[2 * n:2 * n + 3]
        x, y, _ = _place()
        copies = _half_gather_copies(s_refs, l_refs, only, send_sems, recv_sems, True)
        for i, who in enumerate(only):
            sends, recvs, peer_sends = copies[i]

            def sent(i=i, who=who, sends=sends):
                pltpu.make_async_copy(s_refs[i], l_refs[i].at[_block(x, y, who)], local_sems.at[i]).wait()
                for cp in sends:
                    cp.wait_send()

            _if(_sends(x, y, who), sent)
            for cp, cond in zip(recvs, peer_sends):
                _if(cond, cp.wait_recv)

    hbm_like = lambda a: pltpu.HBM(a.shape, a.dtype)
    out = pl.pallas_call(
        body, name=name, out_shape=[hbm_like(s) for s in shards] + [hbm_like(l) for l in lands],
        in_specs=[HBM] * (2 * n) + [SEM] * 3 + [BS(memory_space=pl.ANY)] * len(after), out_specs=[HBM] * (2 * n),
        input_output_aliases={i: i for i in range(2 * n)},
        compiler_params=pltpu.CompilerParams(has_side_effects=_SIDE_EFFECT))(*shards, *lands, *sems, *after)
    return out[n:]


def pass_halves_to_sibling(lands, only, name):
    n = len(lands)

    def body(*refs):
        l_refs = refs[n:2 * n]
        send_sems, recv_sems = refs[2 * n:]
        x, y, c = _place()
        copies = []
        for i, who in enumerate(only):
            for j, (px, py) in enumerate(_other_chips(x, y)):
                k, q = 3 * i + j, _block(px, py, who)
                give = pltpu.make_async_remote_copy(src_ref=l_refs[i].at[q, c], dst_ref=l_refs[i].at[q, c], send_sem=send_sems.at[k],
                                                    recv_sem=recv_sems.at[k], device_id=(x, y, 1 - c), device_id_type=MESH)
                take = pltpu.make_async_remote_copy(src_ref=l_refs[i].at[q, c], dst_ref=l_refs[i].at[q, 1 - c],
                                                    send_sem=send_sems.at[k], recv_sem=recv_sems.at[k], device_id=(x, y, 1 - c),
                                                    device_id_type=MESH)
                _if(_sends(px, py, who), give.start)
                copies.append((give, take, _sends(px, py, who)))
        for give, take, cond in copies:
            def done(give=give, take=take):
                take.wait_recv()
                give.wait_send()

            _if(cond, done)

    return pl.pallas_call(
        body, name=name, in_specs=[ANY] * n, out_specs=[ANY] * n,
        out_shape=[jax.ShapeDtypeStruct(l.shape, l.dtype) for l in lands], input_output_aliases={i: i for i in range(n)},
        scratch_shapes=[pltpu.SemaphoreType.DMA((3 * n,)), pltpu.SemaphoreType.DMA((3 * n,))])(*lands)


def allgather_devices(block, name):
    R, C = block.shape

    def body(b_ref, o_ref, send_sems, recv_sems, local_sem):
        x, y, c = _place()
        me = 4 * x + 2 * y + c
        own = pltpu.make_async_copy(b_ref, o_ref.at[me], local_sem)
        own.start()
        copies = []
        for r in range(1, 8):
            px = 1 - x if r & 4 else x
            py = 1 - y if r & 2 else y
            pc = 1 - c if r & 1 else c
            send = pltpu.make_async_remote_copy(src_ref=b_ref, dst_ref=o_ref.at[me], send_sem=send_sems.at[r - 1],
                                                recv_sem=recv_sems.at[r - 1], device_id=(px, py, pc), device_id_type=MESH)
            recv = pltpu.make_async_remote_copy(src_ref=b_ref, dst_ref=o_ref.at[4 * px + 2 * py + pc], send_sem=send_sems.at[r - 1],
                                                recv_sem=recv_sems.at[r - 1], device_id=(px, py, pc), device_id_type=MESH)
            send.start()
            copies.append((send, recv))
        for send, recv in copies:
            recv.wait_recv()
            send.wait_send()
        own.wait()

    return pl.pallas_call(
        body, name=name, in_specs=[ANY], out_specs=ANY, out_shape=jax.ShapeDtypeStruct((8, R, C), block.dtype),
        scratch_shapes=[pltpu.SemaphoreType.DMA((7,)), pltpu.SemaphoreType.DMA((7,)), pltpu.SemaphoreType.DMA(())])(block)


def swap_sibling(arrs, name, splits=None):
    n = len(arrs)

    def sent(a_ref, i, c):
        return a_ref if splits is None else a_ref.at[(slice(None),) + _half(splits[i], 1 - c)]

    def out_shape(a, i):
        if splits is None:
            return a.shape
        axis, size = splits[i]
        return (a.shape[0], size, a.shape[2]) if axis == 0 else (a.shape[0], a.shape[1], size)

    def body(*refs):
        a_refs, o_refs = refs[:n], refs[n:2 * n]
        send_sems, recv_sems = refs[2 * n:]
        x, y, c = _place()
        copies = [pltpu.make_async_remote_copy(src_ref=sent(a_ref, i, c), dst_ref=o_ref, send_sem=send_sems.at[i],
                                               recv_sem=recv_sems.at[i], device_id=(x, y, 1 - c), device_id_type=MESH)
                  for i, (a_ref, o_ref) in enumerate(zip(a_refs, o_refs))]
        for cp in copies:
            cp.start()
        for cp in copies:
            cp.wait()

    return pl.pallas_call(
        body, name=name, in_specs=[ANY] * n, out_specs=[ANY] * n,
        out_shape=[jax.ShapeDtypeStruct(out_shape(a, i), a.dtype) for i, a in enumerate(arrs)],
        scratch_shapes=[pltpu.SemaphoreType.DMA((n,)), pltpu.SemaphoreType.DMA((n,))])(*arrs)


def _exchange_copies(p_refs, l_refs, send_sems, recv_sems):
    x, y, c = _place()
    return [pltpu.make_async_remote_copy(src_ref=p_ref.at[2 * px + py], dst_ref=l_ref.at[j], send_sem=send_sems.at[3 * i + j],
                                         recv_sem=recv_sems.at[3 * i + j], device_id=(px, py, c), device_id_type=MESH)
            for i, (p_ref, l_ref) in enumerate(zip(p_refs, l_refs)) for j, (px, py) in enumerate(_other_chips(x, y))]


def exchange_chips_start(parts, name):
    n = len(parts)

    def body(*refs):
        send_sems, recv_sems = refs[2 * n:2 * n + 2]
        for cp in _exchange_copies(refs[:n], refs[n:2 * n], send_sems, recv_sems):
            cp.start()
        refs[-1][...] = jnp.zeros_like(refs[-1])

    lands = [lax.empty((3,) + p.shape[1:], p.dtype) for p in parts]
    hbm_like = lambda a: pltpu.HBM(a.shape, a.dtype)
    out = pl.pallas_call(
        body, name=name,
        out_shape=[pltpu.SemaphoreType.DMA((3 * n,)), pltpu.SemaphoreType.DMA((3 * n,))]
        + [hbm_like(p) for p in parts] + [hbm_like(l) for l in lands] + [jax.ShapeDtypeStruct((8, 128), F32)],
        in_specs=[HBM] * (2 * n), out_specs=[SEM] * 2 + [HBM] * (2 * n) + [BS(memory_space=pltpu.VMEM)],
        input_output_aliases={i: 2 + i for i in range(2 * n)},
        compiler_params=pltpu.CompilerParams(has_side_effects=_SIDE_EFFECT))(*[_IN_HBM(p) for p in parts], *[_IN_HBM(l) for l in lands])
    return out[:2], out[2:2 + n], out[2 + n:2 + 2 * n], out[-1]


def exchange_chips_wait(sems, parts, lands, after, name):
    n = len(parts)

    def body(*refs):
        send_sems, recv_sems = refs[2 * n:2 * n + 2]
        for cp in _exchange_copies(refs[:n], refs[n:2 * n], send_sems, recv_sems):
            cp.wait_send()
            cp.wait_recv()

    hbm_like = lambda a: pltpu.HBM(a.shape, a.dtype)
    out = pl.pallas_call(
        body, name=name, out_shape=[hbm_like(p) for p in parts] + [hbm_like(l) for l in lands],
        in_specs=[HBM] * (2 * n) + [SEM] * 2 + [BS(memory_space=pl.ANY)], out_specs=[HBM] * (2 * n),
        input_output_aliases={i: i for i in range(2 * n)},
        compiler_params=pltpu.CompilerParams(has_side_effects=_SIDE_EFFECT))(*parts, *lands, *sems, after)
    return out[n:]


def _half_block(shape2, split):
    axis, size = split
    return (size, shape2[1]) if axis == 0 else (shape2[0], size)


def add_pairs(parts, halves, splits, core, name):
    n = len(parts)

    def body(s_ref, *refs):
        for a_ref, b_ref, o_ref in zip(refs[:n], refs[n:2 * n], refs[2 * n:]):
            o_ref[...] = (a_ref[...].astype(F32) + b_ref[...].astype(F32)).astype(BF16)

    def mine(i):
        blk = (None,) + _half_block(parts[i].shape[1:], splits[i])
        if splits[i][0] == 0:
            return BS(blk, lambda q, s: (q, s[0], 0))
        return BS(blk, lambda q, s: (q, 0, s[0]))

    half_specs = [BS((None,) + h.shape[1:], lambda q, s: (q, 0, 0)) for h in halves]
    return pl.pallas_call(
        body, name=name,
        grid_spec=pltpu.PrefetchScalarGridSpec(num_scalar_prefetch=1, grid=(N_CHIPS,),
                                               in_specs=[mine(i) for i in range(n)] + half_specs, out_specs=half_specs),
        out_shape=[jax.ShapeDtypeStruct(h.shape, BF16) for h in halves], compiler_params=_arb(1))(core, *parts, *halves)


def add_fives(parts, halves, from_chips, splits, chip_core, name):
    n = len(parts)

    def body(s_ref, *refs):
        for a_ref, b_ref, p_ref, o_ref in zip(refs[:n], refs[n:2 * n], refs[2 * n:3 * n], refs[3 * n:]):
            s = a_ref[...].astype(F32) + b_ref[...].astype(F32)
            for j in range(3):
                s = s + p_ref[j].astype(F32)
            o_ref[...] = s

    def mine(i):
        blk = (None,) + _half_block(parts[i].shape[1:], splits[i])
        if splits[i][0] == 0:
            return BS(blk, lambda g, s: (s[0], s[1], 0))
        return BS(blk, lambda g, s: (s[0], 0, s[1]))

    half_specs = [BS((None,) + h.shape[1:], lambda g, s: (s[0], 0, 0)) for h in halves]
    chip_specs = [BS(p.shape, lambda g, s: (0, 0, 0)) for p in from_chips]
    out_specs = [BS(h.shape[1:], lambda g, s: (0, 0)) for h in halves]
    return pl.pallas_call(
        body, name=name,
        grid_spec=pltpu.PrefetchScalarGridSpec(num_scalar_prefetch=1, grid=(1,),
                                               in_specs=[mine(i) for i in range(n)] + half_specs + chip_specs, out_specs=out_specs),
        out_shape=[jax.ShapeDtypeStruct(h.shape[1:], F32) for h in halves], compiler_params=_arb(1))(chip_core, *parts, *halves, *from_chips)


def sum_leading(a, name):
    def body(a_ref, o_ref):
        s = a_ref[0]
        for j in range(1, a.shape[0]):
            s = s + a_ref[j]
        o_ref[...] = s

    return pl.pallas_call(body, name=name, out_shape=jax.ShapeDtypeStruct(a.shape[1:], a.dtype))(a)


def _adamw_math(w, g, m, v):
    mn = ADAM_B1 * m + (1.0 - ADAM_B1) * g
    vn = ADAM_B2 * v + (1.0 - ADAM_B2) * (g * g)
    m_hat = mn / (1.0 - ADAM_B1 ** ADAM_STEP)
    v_hat = vn / (1.0 - ADAM_B2 ** ADAM_STEP)
    return -ADAM_LR * (m_hat / (jnp.sqrt(v_hat) + ADAM_EPS) + ADAM_WD * w), mn, vn


def adamw(w, g, m, v, name):
    R, C = g.shape
    lead = (None,) * (w.ndim - 2)

    def body(w_ref, g_ref, m_ref, v_ref, d_ref, mo_ref, vo_ref):
        d_ref[...], mo_ref[...], vo_ref[...] = _adamw_math(w_ref[...], g_ref[...], m_ref[...], v_ref[...])

    wblk = BS(lead + (R, C), lambda i: (0,) * w.ndim)
    gblk = BS((R, C), lambda i: (0, 0))
    return pl.pallas_call(
        body, name=name, grid=(1,), in_specs=[wblk, gblk, wblk, wblk], out_specs=[wblk] * 3,
        out_shape=[jax.ShapeDtypeStruct(w.shape, F32)] * 3, compiler_params=_arb(1))(w, g, m, v)


SMALL_ROWS = 16
CONV_ROW0 = 8
LOSS_ROW = 14


def pack_small(small_grads, g_ab, g_conv, sq):
    present = [a for a in small_grads if a is not None]

    def body(*refs):
        ab_ref, conv_ref, sq_ref, o_ref = refs[len(present):]
        o_ref[...] = jnp.zeros_like(o_ref)
        it = iter(refs[:len(present)])
        for i, a in enumerate(small_grads):
            if a is not None:
                o_ref[i:i + 1, 0:a.shape[1]] = next(it)[...]
        o_ref[3:4, 0:128] = ab_ref[0:1, :]
        o_ref[4:5, 0:128] = ab_ref[1:2, :]
        half = 512
        for k in range(GDN_CONV * GDN_QKV // half):
            src_r, src_c = (k * half) // GDN_QKV, (k * half) % GDN_QKV
            dst_r, dst_c = CONV_ROW0 + (k * half) // 1024, (k * half) % 1024
            o_ref[dst_r:dst_r + 1, dst_c:dst_c + half] = conv_ref[src_r:src_r + 1, src_c:src_c + half]
        o_ref[LOSS_ROW:LOSS_ROW + 1, :] = sq_ref[...]

    return pl.pallas_call(body, name="pack_small", out_shape=jax.ShapeDtypeStruct((SMALL_ROWS, 1024), F32))(
        *present, g_ab, g_conv, sq)


def adamw_small(block, ws, ms, vs):
    k = len(ws)

    def body(b_ref, *refs):
        outs = refs[3 * k:]
        for i in range(k):
            n = ws[i].shape[1]
            g = b_ref[i:i + 1, 0:n]
            d, mn, vn = _adamw_math(refs[i][...], g, refs[k + i][...], refs[2 * k + i][...])
            outs[4 * i][...], outs[4 * i + 1][...], outs[4 * i + 2][...], outs[4 * i + 3][...] = g, d, mn, vn

    out = pl.pallas_call(
        body, name="adamw_small",
        out_shape=[jax.ShapeDtypeStruct(w.shape, F32) for w in ws for _ in range(4)])(block, *ws, *ms, *vs)
    return [out[4 * i:4 * i + 4] for i in range(k)]


def adamw_halves(w, mine, other, m, v, split, core, name):
    R, C = w.shape[-2:]
    axis, size = split
    lead = (None,) * (w.ndim - 2)
    zeros = (0,) * (w.ndim - 2)
    if axis == 0:
        tr = size if size <= 256 else next(t for t in range(256, 7, -1) if size % t == 0 and t % 8 == 0)
        nb = size // tr
        whole = BS(lead + (tr, C), lambda hi, j, s: zeros + (hi * nb + j, 0))
        part = BS((tr, C), lambda hi, j, s: (j, 0))
    else:
        nb = size // 128
        whole = BS(lead + (R, 128), lambda hi, j, s: zeros + (0, hi * nb + j))
        part = BS((R, 128), lambda hi, j, s: (0, j))

    def body(s_ref, w_ref, a_ref, b_ref, m_ref, v_ref, g_ref, d_ref, mo_ref, vo_ref):
        g = jnp.where(pl.program_id(0) == s_ref[0], a_ref[...], b_ref[...])
        g_ref[...] = g
        d_ref[...], mo_ref[...], vo_ref[...] = _adamw_math(w_ref[...], g, m_ref[...], v_ref[...])

    return pl.pallas_call(
        body, name=name,
        grid_spec=pltpu.PrefetchScalarGridSpec(num_scalar_prefetch=1, grid=(2, nb),
                                               in_specs=[whole, part, part, whole, whole], out_specs=[whole] * 4),
        out_shape=[jax.ShapeDtypeStruct(w.shape, F32)] * 4, compiler_params=_arb(2))(core, w, mine, other, m, v)


def dense_bf16(w3, name):
    R, _, K = w3.shape
    kh = K // 2

    def body(w_hbm, o_ref, buf, sem):
        cp = pltpu.make_async_copy(w_hbm.at[:, 0], buf, sem)
        cp.start()
        cp.wait()
        o_ref[0] = buf[:, :kh].astype(BF16)
        o_ref[1] = buf[:, kh:].astype(BF16)

    return pl.pallas_call(
        body, name=name, in_specs=[ANY], out_specs=BS(memory_space=pltpu.VMEM), out_shape=jax.ShapeDtypeStruct((2, R, kh), BF16),
        scratch_shapes=[pltpu.VMEM((R, K), F32), pltpu.SemaphoreType.DMA(())])(w3)


ROW_BLOCK = 184


def adamw_untiled_rows(w3, mine, other, m3, v3, name):
    R, _, K = w3.shape
    kh = K // 2
    starts = list(range(0, R, ROW_BLOCK))
    sizes = [min(ROW_BLOCK, R - s) for s in starts]
    nblk = len(starts)

    def body(w_hbm, a_ref, b_ref, m_hbm, v_hbm, g_hbm, d_hbm, mo_hbm, vo_hbm,
             wbuf, mbuf, vbuf, gbuf, dbuf, mobuf, vobuf, in_sems, out_sems):
        first = lax.axis_index("c") == 0
        ins = []
        for k, (r0, n) in enumerate(zip(starts, sizes)):
            rows = pl.ds(r0, n)
            cps = [pltpu.make_async_copy(src.at[rows, 0], dst.at[rows], in_sems.at[3 * k + i])
                   for i, (src, dst) in enumerate(((w_hbm, wbuf), (m_hbm, mbuf), (v_hbm, vbuf)))]
            for cp in cps:
                cp.start()
            ins.append(cps)

        def update(rows):
            a, b = a_ref[rows, :], b_ref[rows, :]
            g = jnp.concatenate([jnp.where(first, a, b), jnp.where(first, b, a)], axis=1)
            gbuf[rows, :] = g
            dbuf[rows, :], mobuf[rows, :], vobuf[rows, :] = _adamw_math(wbuf[rows, :], g, mbuf[rows, :], vbuf[rows, :])

        outs = []
        for k, (r0, n) in enumerate(zip(starts, sizes)):
            for cp in ins[k]:
                cp.wait()
            groups, tail = n // 8, n % 8

            def group(i, carry, r0=r0):
                update(pl.ds(pl.multiple_of(r0 + i * 8, 8), 8))
                return carry

            lax.fori_loop(0, groups, group, 0)
            if tail:
                update(pl.ds(r0 + groups * 8, tail))
            rows = pl.ds(r0, n)
            cps = [pltpu.make_async_copy(src.at[rows], dst.at[rows, 0], out_sems.at[4 * k + i])
                   for i, (src, dst) in enumerate(((gbuf, g_hbm), (dbuf, d_hbm), (mobuf, mo_hbm), (vobuf, vo_hbm)))]
            for cp in cps:
                cp.start()
            outs += cps
        for cp in outs:
            cp.wait()

    vmem = BS(memory_space=pltpu.VMEM)
    return pl.pallas_call(
        body, name=name, in_specs=[ANY, vmem, vmem, ANY, ANY], out_specs=[ANY] * 4,
        out_shape=[jax.ShapeDtypeStruct(w3.shape, F32)] * 4,
        scratch_shapes=[pltpu.VMEM((R, K), F32)] * 7 + [pltpu.SemaphoreType.DMA((3 * nblk,)), pltpu.SemaphoreType.DMA((4 * nblk,))])(
            w3, mine, other, m3, v3)


def adamw_w_q_b(w, mine, other, m, v, name):
    def body(w_ref, a_ref, b_ref, m_ref, v_ref, g_ref, d_ref, mo_ref, vo_ref):
        first = lax.axis_index("c") == 0
        lo = jnp.where(first, a_ref[...], b_ref[...])
        hi = jnp.where(first, b_ref[...], a_ref[...])
        g = jnp.concatenate([lo, hi[0:32], hi[64:96]], axis=0)
        g_ref[...] = g
        d_ref[...], mo_ref[...], vo_ref[...] = _adamw_math(w_ref[...], g, m_ref[...], v_ref[...])

    return pl.pallas_call(body, name=name, out_shape=[jax.ShapeDtypeStruct(w.shape, F32)] * 4)(w, mine, other, m, v)


def local_step(x, mem, positions, tgt, norm_in, weights, big_grads_ready, q_a_norm, kv_a_norm, gdn_conv, gdn_a_log,
               gdn_dt_bias, gdn_norm, mem_norm, norm_final):
    B, S, D = x.shape
    M = mem.shape[1]
    T = B * S
    N = S // CHUNK
    x2d = x.reshape(T, D)
    mem2d = mem.reshape(B * M, D)
    tgt2d = tgt.reshape(T, D)

    alog_row, dt_row = _lane_row(gdn_a_log), _lane_row(gdn_dt_bias)

    half = MLA_ROPE // 2
    inv_freq = 1.0 / (ROPE_THETA ** (jnp.arange(half, dtype=F32) / half))
    z32 = jnp.zeros((half,), F32)
    o32 = jnp.ones((half,), F32)
    inv_row = jnp.concatenate([inv_freq, z32, inv_freq, z32]).reshape(1, 128)
    sgn_row = jnp.concatenate([-o32, z32, o32, z32]).reshape(1, 128)
    msk_row = jnp.concatenate([o32, z32, o32, z32]).reshape(1, 128)
    cos_t, sin_t = rope_tables(positions.reshape(T, 1), inv_row, sgn_row, msk_row, after=weights[3])

    h = rms_fwd(x2d, norm_in, "rms_in", after=weights[3])
    tiles = dict(bm=2048, bn=W_IN_ROW_BLOCK, n_outer=True)
    wp_early = weights[0]((h, cos_t))
    P_early = mm(h, wp_early, "nt", F32, "in_proj_early", col_tiles=(0, OFF_MEMQ // W_IN_ROW_BLOCK), **tiles)
    wq, wkv = weights[1](P_early)
    Q, K, V, qn, kvn = mla_prep(P_early, q_a_norm, kv_a_norm, wq, wkv, cos_t, sin_t)
    o_mla, lse = mla_attn_fwd(Q, K, V, B, S)
    wp = weights[4]((o_mla, P_early), wp_early)
    P = mm(h, wp, "nt", F32, "in_proj", col_tiles=(OFF_MEMQ // W_IN_ROW_BLOCK, N_PAD // W_IN_ROW_BLOCK), into=P_early, **tiles)
    qkv = gdn_prep_fwd(P, gdn_conv, B, S)
    GB = gdn_gate_fwd(P, alog_row, dt_row, B, S)
    Grow = jnp.transpose(GB[:, :N_HEADS].reshape(B, N, CHUNK, N_HEADS), (0, 3, 1, 2))
    U, W, Tinv, A = gdn_chunk_fwd(qkv, GB, Grow, B, S)
    qkv3, GB3 = qkv.reshape(B, S, GDN_QKV), GB.reshape(B, S, 128)
    W3 = W.reshape(B, S, 512)
    o_gdn3, Vn3, St = gdn_scan_fwd(qkv3, U.reshape(B, S, 512), W3, GB3, A, B, S)
    o_gdn = o_gdn3.reshape(T, 512)
    w_mem_kv, w_out = weights[2](o_gdn)
    memn = rms_fwd(mem2d, mem_norm, "rms_mem")
    MKV = mm(memn, w_mem_kv, "nn", BF16, "mem_kv_proj")
    o_mem = mem_attn_fwd(P, MKV, B, S, M)
    mixed, dx2, dx2b, sq, g_norm_final = merge_fwd(o_mla, o_gdn, o_mem, P, x2d, tgt2d, w_out, gdn_norm, norm_final.reshape(1, D))

    g_w_out = mm(mixed, dx2b, "tn", BF16, "grad_w_out")
    dgate, do_mla, do_gdn, do_mem, g_gdn_norm = merge_bwd(dx2b, o_mla, o_gdn, o_mem, P, w_out, gdn_norm)

    dmemq, dMKV = mem_attn_bwd(P, MKV, do_mem, B, S, M)
    g_w_mem_kv = mm(memn, dMKV, "tn", BF16, "grad_w_mem_kv")
    started_early = big_grads_ready(dict(w_mem_kv=g_w_mem_kv, w_out=g_w_out), "early")
    dmemn = mm(dMKV, w_mem_kv, "nt", F32, "d_memn", after=(started_early,))
    g_mem_norm = gain_grad(mem2d, dmemn, "grad_mem_norm")

    dU3, dW3, dQ13, dK13, dA, dG13 = gdn_scan_bwd(do_gdn.reshape(B, S, 512), qkv3, W3, Vn3, GB3, A, St, B, S)
    r2 = lambda a: a.reshape(T, a.shape[-1])
    dqkv, dGB = gdn_chunk_bwd(qkv, GB, Grow, Tinv, dA, r2(dU3), r2(dW3), r2(dQ13), r2(dK13), r2(dG13), B, S)
    dPg, g_conv = gdn_prep_bwd(P, dqkv, gdn_conv, B, S)
    dab, g_ab = gdn_gate_bwd(P, dGB, alog_row, dt_row, B, S)

    dQ, dK, dV = mla_attn_bwd(Q, K, V, o_mla, do_mla, lse, B, S)
    dq_lin, dkv_lin, dPm, g_q_a_norm, g_kv_a_norm = mla_proj_bwd(dQ, dK, dV, cos_t, sin_t, P, dab, wq, wkv, q_a_norm, kv_a_norm)
    g_wq = mm(dq_lin, qn, "tn", BF16, "grad_w_q_b")
    g_wkv = mm(kvn, dkv_lin, "tn", BF16, "grad_w_kv_b")

    dP = [dPm, dmemq, dPg, dgate]
    g_wp = mm_cols_tn(dP, h, BF16, "grad_w_in")
    started = big_grads_ready(dict(w_in=g_wp, w_q_b=g_wq, w_kv_b=g_wkv), "late")
    grad_x, g_norm_in = in_proj_bwd(dP, wp, x2d, dx2, norm_in, started)

    grads = dict(
        norm_in=g_norm_in, q_a_norm=g_q_a_norm, kv_a_norm=g_kv_a_norm, gdn_conv=g_conv,
        gdn_a_log_dt_bias=g_ab, gdn_norm=g_gdn_norm,
        mem_norm=g_mem_norm, norm_final=g_norm_final)
    return sq, grad_x.reshape(B, S, D), grads


def kernel(x, mem, positions, norm_in, w_in, q_a_norm, w_q_b, kv_a_norm, w_kv_b, gdn_conv, gdn_a_log, gdn_dt_bias, gdn_norm, mem_norm, w_mem_kv, w_out, norm_final, loss_target, m_norm_in, m_w_in, m_q_a_norm, m_w_q_b, m_kv_a_norm, m_w_kv_b, m_gdn_conv, m_gdn_a_log, m_gdn_dt_bias, m_gdn_norm, m_mem_norm, m_w_mem_kv, m_w_out, m_norm_final, v_norm_in, v_w_in, v_q_a_norm, v_w_q_b, v_kv_a_norm, v_w_kv_b, v_gdn_conv, v_gdn_a_log, v_gdn_dt_bias, v_gdn_norm, v_mem_norm, v_w_mem_kv, v_w_out, v_norm_final):
    B = x.shape[0]
    cx, cy, cc = lax.axis_index("x"), lax.axis_index("y"), lax.axis_index("c")
    chip = 2 * cx + cy

    big_names = ("w_in", "w_q_b", "w_kv_b", "w_mem_kv", "w_out")
    rows_major = lambda a: jnp.transpose(a, (2, 0, 1))
    w_in3, m_in3, v_in3 = rows_major(w_in), rows_major(m_w_in), rows_major(v_w_in)
    w_qb_t, m_qb_t, v_qb_t = jnp.transpose(w_q_b[0]), jnp.transpose(m_w_q_b[0]), jnp.transpose(v_w_q_b[0])
    z32 = jnp.zeros((32, Q_LORA), BF16)
    qb_bf = w_qb_t.astype(BF16)
    qb_padded = jnp.concatenate([qb_bf[:160], z32, qb_bf[160:], z32])
    shards = [dense_bf16(w_in3, "w_in_bf16"), qb_padded, w_kv_b[0].astype(BF16), w_mem_kv[0].astype(BF16), w_out[0].astype(BF16)]
    splits = [(1, D_MODEL // 2)] + [(0, s.shape[0] // 2) for s in shards[1:]]
    early_from = [q for q, _, _ in EARLY_WINDOWS]
    *early_flight, early_started = half_gather_start([shards[0][:, r0:r0 + n] for _, r0, n in EARLY_WINDOWS], early_from,
                                                     "w_in_early_gather_start")
    *late_a, started_a = late_gather_start(shards[1:3], early_started, "late_gather_qkv_start")
    *w_in_flight, w_in_started = half_gather_start([shards[0]], [None], "w_in_gather_start", after=(started_a,))
    *late_b, started_b = late_gather_start(shards[3:], w_in_started, "late_gather_mem_out_start")
    conv_all = allgather_devices(gdn_conv[0], "allgather_conv")
    conv_cols = gdn_conv.shape[2]
    conv_full = jnp.transpose(conv_all[0::2], (1, 0, 2)).reshape(GDN_CONV, N_CHIPS * conv_cols)
    late_shapes = [(N_CHIPS,) + s.shape for s in shards[1:]]

    def w_in_early_ready(after):
        halves = half_gather_wait(*early_flight, early_from, after, "w_in_early_gather_wait")
        return pad_w_in_t_early(pass_halves_to_sibling(halves, early_from, "w_in_early_gather_sibling"))

    def w_in_ready(after, wp):
        halves = half_gather_wait(*w_in_flight, [None], after, "w_in_gather_wait")
        g_in, = pass_halves_to_sibling(halves, [None], "w_in_gather_sibling")
        return pad_w_in_t_rest(g_in, wp)

    def late_qkv(after):
        g_qb, g_kvb = late_gather_wait(*late_a, after, "late_gather_qkv_wait")
        return g_qb.reshape(-1, Q_LORA), _perm_w_kv_b(g_kvb)

    def late_mem_out(after):
        g_mem, g_out_w = late_gather_wait(*late_b, after, "late_gather_mem_out_wait")
        return g_mem.reshape(-1, g_mem.shape[2]), g_out_w.reshape(-1, g_out_w.shape[2])

    weights = (w_in_early_ready, late_qkv, late_mem_out, (started_b,), w_in_ready)

    core = jnp.stack([cc]).astype(jnp.int32)
    chip_core = jnp.stack([chip, cc]).astype(jnp.int32)
    exchanges = {}
    by_chip = dict(w_in=unpad_w_in_t, w_q_b=lambda a: a.reshape(late_shapes[0]), w_kv_b=_unperm_w_kv_b,
                   w_mem_kv=lambda a: a.reshape(late_shapes[2]), w_out=lambda a: a.reshape(late_shapes[3]))

    def big_grads_ready(gb, group):
        idx = [big_names.index(n) for n in gb]
        parts = [by_chip[n](a) for n, a in gb.items()]
        sp = [splits[i] for i in idx]
        from_sibling = swap_sibling(parts, "rs_sibling_partial_" + group, sp)
        chip_sums = add_pairs(parts, from_sibling, sp, core, "rs_add_sibling_" + group)
        sems, sums_thru, lands, token = exchange_chips_start(chip_sums, "rs_exchange_start_" + group)
        exchanges[group] = dict(idx=idx, parts=parts, from_sibling=from_sibling, sems=sems, sums=sums_thru, lands=lands)
        return token

    sq, grad_x, g = local_step(x, mem, positions, loss_target, norm_in, weights, big_grads_ready, q_a_norm, kv_a_norm, conv_full,
                               gdn_a_log, gdn_dt_bias, gdn_norm, mem_norm, norm_final)

    small_names = ("norm_in", "q_a_norm", "kv_a_norm", "gdn_a_log", "gdn_dt_bias", "gdn_norm", "mem_norm", "norm_final")
    small = dict(norm_in=norm_in, q_a_norm=q_a_norm, kv_a_norm=kv_a_norm, gdn_a_log=gdn_a_log, gdn_dt_bias=gdn_dt_bias,
                 gdn_norm=gdn_norm, mem_norm=mem_norm, norm_final=norm_final)
    m_small = dict(norm_in=m_norm_in, q_a_norm=m_q_a_norm, kv_a_norm=m_kv_a_norm, gdn_a_log=m_gdn_a_log,
                   gdn_dt_bias=m_gdn_dt_bias, gdn_norm=m_gdn_norm, mem_norm=m_mem_norm, norm_final=m_norm_final)
    v_small = dict(norm_in=v_norm_in, q_a_norm=v_q_a_norm, kv_a_norm=v_kv_a_norm, gdn_a_log=v_gdn_a_log,
                   gdn_dt_bias=v_gdn_dt_bias, gdn_norm=v_gdn_norm, mem_norm=v_mem_norm, norm_final=v_norm_final)
    conv_rows = GDN_CONV * GDN_QKV // 1024
    g_block = pack_small([g.get(n) for n in small_names], g["gdn_a_log_dt_bias"], g["gdn_conv"], sq)
    g_block = sum_leading(allgather_devices(g_block, "allgather_small_grads"), "sum_small_grads")
    loss = 0.5 * jnp.sum(g_block[LOSS_ROW]) / D_MODEL
    g_conv = lax.dynamic_slice_in_dim(g_block[CONV_ROW0:CONV_ROW0 + conv_rows].reshape(GDN_CONV, GDN_QKV), chip * conv_cols,
                                      conv_cols, axis=1)
    as_row = lambda a: a.reshape(1, -1)
    updated = adamw_small(g_block, [as_row(small[n]) for n in small_names], [as_row(m_small[n]) for n in small_names],
                          [as_row(v_small[n]) for n in small_names])
    g_out, d_out, m_out, v_out = ({n: u[i].reshape(small[n].shape) for n, u in zip(small_names, updated)} for i in range(4))
    d_s = d_out["norm_in"]

    my_half = [None] * len(big_names)
    for group, e in exchanges.items():
        from_chips = exchange_chips_wait(e["sems"], e["sums"], e["lands"], d_s, "rs_exchange_wait_" + group)
        halves = add_fives(e["parts"], e["from_sibling"], from_chips, [splits[i] for i in e["idx"]], chip_core, "rs_add_chips_" + group)
        for i, a in zip(e["idx"], halves):
            my_half[i] = a
    other_half = swap_sibling(my_half, "rs_sibling_final")

    d_out["gdn_conv"], m_out["gdn_conv"], v_out["gdn_conv"] = adamw(gdn_conv, g_conv, m_gdn_conv, v_gdn_conv, "adamw_gdn_conv")
    g_out["gdn_conv"] = g_conv[None]
    res = adamw_untiled_rows(w_in3, my_half[0], other_half[0], m_in3, v_in3, "adamw_w_in")
    g_out["w_in"], d_out["w_in"], m_out["w_in"], v_out["w_in"] = [jnp.transpose(r, (1, 2, 0)) for r in res]
    res = adamw_w_q_b(w_qb_t, my_half[1], other_half[1], m_qb_t, v_qb_t, "adamw_w_q_b")
    g_out["w_q_b"], d_out["w_q_b"], m_out["w_q_b"], v_out["w_q_b"] = [jnp.transpose(r)[None] for r in res]
    rest = dict(w_kv_b=(w_kv_b, m_w_kv_b, v_w_kv_b), w_mem_kv=(w_mem_kv, m_w_mem_kv, v_w_mem_kv), w_out=(w_out, m_w_out, v_w_out))
    for i, n in enumerate(big_names):
        if n in rest:
            w_n, m_n, v_n = rest[n]
            g_out[n], d_out[n], m_out[n], v_out[n] = adamw_halves(w_n, my_half[i], other_half[i], m_n, v_n, splits[i], core, "adamw_" + n)

    order = ("norm_in", "w_in", "q_a_norm", "w_q_b", "kv_a_norm", "w_kv_b", "gdn_conv", "gdn_a_log", "gdn_dt_bias",
             "gdn_norm", "mem_norm", "w_mem_kv", "w_out", "norm_final")
    return (loss, grad_x, *[g_out[n] for n in order], *[d_out[n] for n in order], *[m_out[n] for n in order],
            *[v_out[n] for n in order])
```

```python
import jax
import jax.numpy as jnp
from jax import lax
from jax.experimental import pallas as pl
from jax.experimental.pallas import tpu as pltpu

F32 = jnp.float32
BF16 = jnp.bfloat16
BS = pl.BlockSpec

D_MODEL = 1024
N_HEADS = 4
MLA_NOPE, MLA_ROPE, MLA_V = 128, 64, 128
Q_LORA, KV_LORA = 384, 256
ROPE_THETA = 10000.0
GDN_DK = GDN_DV = 128
GDN_CONV = 4
CHUNK = 64
MEM_DH = 128
D_MIX = 1536
GDN_QKV = 1536
D_IN = 4296
EPS = 1e-6
ADAM_LR, ADAM_B1, ADAM_B2, ADAM_EPS, ADAM_WD, ADAM_STEP = 0.001, 0.9, 0.999, 1e-08, 0.01, 10

OFF_MLA = 0
OFF_MEMQ = 1024
OFF_GDN = 1536
OFF_GATE = 3072
N_PAD = 4608
HEAD_PAD = 256
MLA_SCALE = (MLA_NOPE + MLA_ROPE) ** -0.5
MEM_SCALE = MEM_DH ** -0.5
GDN_SCALE = GDN_DK ** -0.5
NEG = -1e30

NN = ((1,), (0,))
NT = ((1,), (1,))
TN = ((0,), (0,))


def _dot(a, b, dims):
    return lax.dot_general(a, b, (dims, ((), ())), preferred_element_type=F32)


def _bdot(spec, a, b, precision=None):
    return jnp.einsum(spec, a, b, preferred_element_type=F32, precision=precision)


def _arb(n):
    return pltpu.CompilerParams(dimension_semantics=("arbitrary",) * n)


def _sigmoid(x):
    return 1.0 / (1.0 + jnp.exp(-x))


def _softplus(z):
    return jnp.maximum(z, 0.0) + jnp.log(1.0 + jnp.exp(-jnp.abs(z)))


def _rope(t, cos_row, sin_row):
    return t * cos_row + pltpu.roll(t, 64, 1) * sin_row


def _rope_bwd(d, cos_row, sin_row):
    return d * cos_row + pltpu.roll(d * sin_row, 64, 1)


def rms_fwd(x, gain, name, tm=512, after=()):
    T, n = x.shape
    tm = min(tm, T)

    def body(x_ref, g_ref, *rest):
        xv = x_ref[...]
        r = lax.rsqrt(jnp.mean(xv * xv, axis=-1, keepdims=True) + EPS)
        rest[-1][...] = (xv * r * g_ref[...]).astype(BF16)

    return pl.pallas_call(
        body, name=name, grid=(T // tm,),
        in_specs=[BS((tm, n), lambda i: (i, 0)), BS((1, n), lambda i: (0, 0))] + [BS(memory_space=pl.ANY)] * len(after),
        out_specs=BS((tm, n), lambda i: (i, 0)),
        out_shape=jax.ShapeDtypeStruct((T, n), BF16), compiler_params=_arb(1))(x, gain, *after)


def mm(a, b, kind, out_dtype, name, bm=512, bn=None, n_outer=False, after=(), col_tiles=None, into=None):
    if kind == "nn":
        (M, K), (_, N) = a.shape, b.shape
    elif kind == "nt":
        (M, K), (N, _) = a.shape, b.shape
    else:
        (K, M), (_, N) = a.shape, b.shape
    bm, bn = min(bm, M), min(bn or N, N)
    assert M % bm == 0 and N % bn == 0, (name, M, N, K)
    lo, hi = col_tiles or (0, N // bn)
    ij = (lambda g0, g1: (g1, g0 + lo)) if n_outer else (lambda g0, g1: (g0, g1 + lo))
    a_spec = BS((K, bm), lambda g0, g1: (0, ij(g0, g1)[0])) if kind == "tn" else BS((bm, K), lambda g0, g1: (ij(g0, g1)[0], 0))
    once = dict(pipeline_mode=pl.Buffered(1)) if hi - lo == 1 else {}
    b_spec = (BS((bn, K), lambda g0, g1: (ij(g0, g1)[1], 0), **once) if kind == "nt"
              else BS((K, bn), lambda g0, g1: (0, ij(g0, g1)[1]), **once))
    dims = {"nn": NN, "nt": NT, "tn": TN}[kind]
    extra = tuple(after) + (() if into is None else (into,))

    def body(a_ref, b_ref, *rest):
        rest[-1][...] = _dot(a_ref[...].astype(BF16), b_ref[...].astype(BF16), dims).astype(out_dtype)

    grid = (hi - lo, M // bm) if n_outer else (M // bm, hi - lo)
    return pl.pallas_call(
        body, name=name, grid=grid, in_specs=[a_spec, b_spec] + [BS(memory_space=pl.ANY)] * len(extra),
        out_specs=BS((bm, bn), lambda g0, g1: ij(g0, g1)),
        input_output_aliases={} if into is None else {1 + len(extra): 0},
        out_shape=jax.ShapeDtypeStruct((M, N), out_dtype), compiler_params=_arb(2))(a, b, *extra)


def mm_cols_tn(pieces, b, out_dtype, name, bm=512):
    K, N = b.shape
    tiles = [p.shape[1] // bm for p in pieces]
    firsts = [sum(tiles[:i]) for i in range(len(tiles))]

    def body(*refs):
        b_ref, o_ref = refs[-2], refs[-1]
        i = pl.program_id(0)
        for a_ref, t0, n in zip(refs[:-2], firsts, tiles):
            @pl.when((i >= t0) & (i < t0 + n))
            def _(a_ref=a_ref):
                o_ref[...] = _dot(a_ref[...], b_ref[...], TN).astype(out_dtype)

    a_specs = [BS((K, bm), lambda i, t0=t0, n=n: (0, jnp.clip(i - t0, 0, n - 1))) for t0, n in zip(firsts, tiles)]
    return pl.pallas_call(
        body, name=name, grid=(sum(tiles),),
        in_specs=a_specs + [BS(b.shape, lambda i: (0, 0), pipeline_mode=pl.Buffered(1))],
        out_specs=BS((bm, N), lambda i: (i, 0)), out_shape=jax.ShapeDtypeStruct((sum(tiles) * bm, N), out_dtype),
        compiler_params=_arb(1))(*pieces, b)


def rope_tables(pos_col, inv_row, sgn_row, msk_row, tm=512, after=()):
    T = pos_col.shape[0]
    tm = min(tm, T)

    def body(p_ref, inv_ref, sgn_ref, msk_ref, *rest):
        c_ref, s_ref = rest[-2:]
        ang = p_ref[...].astype(F32) * inv_ref[...]
        c_ref[...] = jnp.cos(ang) * msk_ref[...]
        s_ref[...] = jnp.sin(ang) * sgn_ref[...]

    row = BS((1, 128), lambda i: (0, 0))
    return pl.pallas_call(
        body, name="rope_tables", grid=(T // tm,),
        in_specs=[BS((tm, 1), lambda i: (i, 0)), row, row, row] + [BS(memory_space=pl.ANY)] * len(after),
        out_specs=[BS((tm, 128), lambda i: (i, 0))] * 2,
        out_shape=[jax.ShapeDtypeStruct((T, 128), F32)] * 2, compiler_params=_arb(1))(pos_col, inv_row, sgn_row, msk_row, *after)


def mla_prep(P, gq, gkv, wq, wkv, cos_t, sin_t, tm=512):
    T = P.shape[0]
    tm = min(tm, T)

    def body(p_ref, gq_ref, gkv_ref, wq_ref, wkv_ref, c_ref, s_ref, q_ref, k_ref, v_ref, qn_ref, kvn_ref):
        p = p_ref[...].astype(F32)
        cq, ckv, kr = p[:, :Q_LORA], p[:, Q_LORA:Q_LORA + KV_LORA], p[:, 640:768]
        qn = (cq * lax.rsqrt(jnp.mean(cq * cq, axis=-1, keepdims=True) + EPS) * gq_ref[...]).astype(BF16)
        kvn = (ckv * lax.rsqrt(jnp.mean(ckv * ckv, axis=-1, keepdims=True) + EPS) * gkv_ref[...]).astype(BF16)
        qn_ref[...] = qn
        kvn_ref[...] = kvn
        q = _dot(qn, wq_ref[...], NT)
        kv = _dot(kvn, wkv_ref[...], NN)
        cos_row, sin_row = c_ref[...], s_ref[...]
        krr = _rope(kr, cos_row, sin_row).astype(BF16)
        for h in range(N_HEADS):
            lo = h * HEAD_PAD
            q_ref[:, lo:lo + 128] = (q[:, lo:lo + 128] * MLA_SCALE).astype(BF16)
            q_ref[:, lo + 128:lo + 256] = (_rope(q[:, lo + 128:lo + 256], cos_row, sin_row) * MLA_SCALE).astype(BF16)
            k_ref[:, lo:lo + 128] = kv[:, h * 128:(h + 1) * 128].astype(BF16)
            k_ref[:, lo + 128:lo + 256] = krr
            v_ref[:, lo:lo + 128] = kv[:, 512 + h * 128:512 + (h + 1) * 128].astype(BF16)
            v_ref[:, lo + 128:lo + 256] = jnp.ones((tm, 128), BF16)

    full = lambda r, c: BS((r, c), lambda i: (0, 0))
    rowb = lambda c: BS((tm, c), lambda i: (i, 0))
    return pl.pallas_call(
        body, name="mla_prep", grid=(T // tm,),
        in_specs=[rowb(1024), full(1, Q_LORA), full(1, KV_LORA), full(1024, Q_LORA), full(KV_LORA, 1024), rowb(128), rowb(128)],
        out_specs=[rowb(1024), rowb(1024), rowb(1024), rowb(Q_LORA), rowb(KV_LORA)],
        out_shape=[jax.ShapeDtypeStruct((T, 1024), BF16), jax.ShapeDtypeStruct((T, 1024), BF16),
                   jax.ShapeDtypeStruct((T, 1024), BF16), jax.ShapeDtypeStruct((T, Q_LORA), BF16),
                   jax.ShapeDtypeStruct((T, KV_LORA), BF16)],
        compiler_params=_arb(1))(P, gq, gkv, wq, wkv, cos_t, sin_t)


ATTN_HEADS_PER_STEP = 2
ATTN_STRIP = 32


def mla_attn_fwd(Q, K, V, B, S, tq=512, hp=ATTN_HEADS_PER_STEP):
    T = B * S
    tq = min(tq, S)
    nq = S // tq

    rs = min(ATTN_STRIP, tq)

    def body(q_ref, k_ref, v_ref, o_ref, lse_ref, m_s, acc_s, s_s, p_s, a_s):
        i = pl.program_id(2)
        m_s[...] = jnp.full_like(m_s, NEG)
        acc_s[...] = jnp.zeros_like(acc_s)

        def blk(j, masked):
            rows = pl.ds(pl.multiple_of(j * tq, tq), tq)
            for h in range(hp):
                hq = slice(h * HEAD_PAD, (h + 1) * HEAD_PAD)
                s_s[h] = _dot(q_ref[:, hq], k_ref[rows, hq], NT)
            for h in range(hp):
                for r0 in range(0, tq, rs):
                    rr = slice(r0, r0 + rs)
                    sv = s_s[h, rr, :]
                    if masked:
                        r = r0 + lax.broadcasted_iota(jnp.int32, (rs, tq), 0)
                        c = lax.broadcasted_iota(jnp.int32, (rs, tq), 1)
                        sv = jnp.where(r >= c, sv, NEG)
                    m_prev = m_s[h, rr, :]
                    m_new = jnp.maximum(m_prev, jnp.max(sv, axis=1, keepdims=True))
                    p_s[h, rr, :] = jnp.exp(sv - m_new).astype(BF16)
                    a_s[h, rr, :] = jnp.exp(m_prev - m_new)
                    m_s[h, rr, :] = m_new
            for h in range(hp):
                hq = slice(h * HEAD_PAD, (h + 1) * HEAD_PAD)
                acc_s[h] = a_s[h] * acc_s[h] + _dot(p_s[h], v_ref[rows, hq], NN)

        def loop(j, c):
            blk(j, False)
            return c

        lax.fori_loop(0, i, loop, 0)
        blk(i, True)
        for h in range(hp):
            den = acc_s[h, :, 128:256]
            o_ref[:, h * 128:(h + 1) * 128] = (acc_s[h, :, 0:128] / den).astype(BF16)
            lse_ref[h] = m_s[h] + jnp.log(den[:, 0:1])

    return pl.pallas_call(
        body, name="mla_attn_fwd", grid=(B, N_HEADS // hp, nq),
        in_specs=[BS((tq, hp * HEAD_PAD), lambda b, h, i: (b * nq + i, h)),
                  BS((S, hp * HEAD_PAD), lambda b, h, i: (b, h)),
                  BS((S, hp * HEAD_PAD), lambda b, h, i: (b, h))],
        out_specs=[BS((tq, hp * 128), lambda b, h, i: (b * nq + i, h)),
                   BS((hp, tq, 1), lambda b, h, i: (h, b * nq + i, 0))],
        out_shape=[jax.ShapeDtypeStruct((T, 512), BF16), jax.ShapeDtypeStruct((N_HEADS, T, 1), F32)],
        scratch_shapes=[pltpu.VMEM((hp, tq, 1), F32), pltpu.VMEM((hp, tq, HEAD_PAD), F32), pltpu.VMEM((hp, tq, tq), F32),
                        pltpu.VMEM((hp, tq, tq), BF16), pltpu.VMEM((hp, tq, 1), F32)],
        compiler_params=_arb(3))(Q, K, V)


def mla_attn_bwd(Q, K, V, O, dO, LSE, B, S, tq=512, hp=ATTN_HEADS_PER_STEP):
    T = B * S
    tq = min(tq, S)
    nq = S // tq

    rs = min(ATTN_STRIP, tq)

    def body(q_ref, k_ref, v_ref, o_ref, do_ref, lse_ref, dq_ref, dk_ref, dv_ref, delta_s, dq_s, dk_s, dv_s, s_s, dp_s, p_s, ds_s):
        j = pl.program_id(2)

        @pl.when(j == 0)
        def _():
            dq_s[...] = jnp.zeros_like(dq_s)
            for h in range(hp):
                sl = slice(h * 128, (h + 1) * 128)
                delta_s[h] = jnp.sum(do_ref[:, sl] * o_ref[:, sl].astype(F32), axis=1, keepdims=True)

        dk_s[...] = jnp.zeros_like(dk_s)
        dv_s[...] = jnp.zeros_like(dv_s)

        def step(i, c):
            rows = pl.ds(pl.multiple_of(i * tq, tq), tq)
            for h in range(hp):
                sq, sv = slice(h * HEAD_PAD, (h + 1) * HEAD_PAD), slice(h * 128, (h + 1) * 128)
                s_s[h] = _dot(q_ref[rows, sq], k_ref[:, sq], NT)
                dp_s[h] = _dot(do_ref[rows, sv].astype(BF16), v_ref[:, h * HEAD_PAD:h * HEAD_PAD + 128], NT)
            for h in range(hp):
                for r0 in range(0, tq, rs):
                    rr = slice(r0, r0 + rs)
                    seq_rows = pl.ds(pl.multiple_of(i * tq + r0, rs), rs)
                    r = i * tq + r0 + lax.broadcasted_iota(jnp.int32, (rs, tq), 0)
                    cc = j * tq + lax.broadcasted_iota(jnp.int32, (rs, tq), 1)
                    p = jnp.where(r >= cc, jnp.exp(s_s[h, rr, :] - lse_ref[h, seq_rows, :]), 0.0)
                    p_s[h, rr, :] = p.astype(BF16)
                    ds_s[h, rr, :] = (p * (dp_s[h, rr, :] - delta_s[h, seq_rows, :])).astype(BF16)
            for h in range(hp):
                sq, sv = slice(h * HEAD_PAD, (h + 1) * HEAD_PAD), slice(h * 128, (h + 1) * 128)
                dv_s[:, sv] += _dot(p_s[h], do_ref[rows, sv].astype(BF16), TN)
                dk_s[:, sq] += _dot(ds_s[h], q_ref[rows, sq], TN)
                dq_s[rows, sq] += _dot(ds_s[h], k_ref[:, sq], NN)
            return c

        lax.fori_loop(j, nq, step, 0)
        dk_ref[...] = dk_s[...].astype(BF16)
        dv_ref[...] = dv_s[...].astype(BF16)

        @pl.when(j == nq - 1)
        def _():
            dq_ref[...] = dq_s[...].astype(BF16)

    seq = lambda c: BS((S, c), lambda b, h, j: (b, h))
    blk = lambda c: BS((tq, c), lambda b, h, j: (b * nq + j, h))
    return pl.pallas_call(
        body, name="mla_attn_bwd", grid=(B, N_HEADS // hp, nq),
        in_specs=[seq(hp * HEAD_PAD), blk(hp * HEAD_PAD), blk(hp * HEAD_PAD), seq(hp * 128), seq(hp * 128),
                  BS((hp, S, 1), lambda b, h, j: (h, b, 0))],
        out_specs=[seq(hp * HEAD_PAD), blk(hp * HEAD_PAD), blk(hp * 128)],
        out_shape=[jax.ShapeDtypeStruct((T, 1024), BF16), jax.ShapeDtypeStruct((T, 1024), BF16),
                   jax.ShapeDtypeStruct((T, 512), BF16)],
        scratch_shapes=[pltpu.VMEM((hp, S, 1), F32), pltpu.VMEM((S, hp * HEAD_PAD), F32),
                        pltpu.VMEM((tq, hp * HEAD_PAD), F32), pltpu.VMEM((tq, hp * 128), F32),
                        pltpu.VMEM((hp, tq, tq), F32), pltpu.VMEM((hp, tq, tq), F32),
                        pltpu.VMEM((hp, tq, tq), BF16), pltpu.VMEM((hp, tq, tq), BF16)],
        compiler_params=_arb(3))(Q, K, V, O, dO, LSE)


def mla_proj_bwd(dQ, dK, dV, cos_t, sin_t, P, dab, wq, wkv, gq, gkv, tm=512):
    T = P.shape[0]
    tm = min(tm, T)

    def norm_bwd(x, dy, g):
        r = lax.rsqrt(jnp.mean(x * x, axis=-1, keepdims=True) + EPS)
        xh = x * r
        dxh = dy * g
        return r * (dxh - xh * jnp.mean(dxh * xh, axis=-1, keepdims=True)), jnp.sum(dy * xh, axis=0, keepdims=True)

    def body(dq_ref, dk_ref, dv_ref, c_ref, s_ref, p_ref, dab_ref, wq_ref, wkv_ref, gq_ref, gkv_ref,
             ql_ref, kvl_ref, o_ref, aq_ref, akv_ref):
        @pl.when(pl.program_id(0) == 0)
        def _():
            aq_ref[...] = jnp.zeros_like(aq_ref)
            akv_ref[...] = jnp.zeros_like(akv_ref)

        cos_row, sin_row = c_ref[...], s_ref[...]
        kr = jnp.zeros((tm, 128), F32)
        for h in range(N_HEADS):
            lo = h * HEAD_PAD
            ql_ref[:, lo:lo + 128] = (dq_ref[:, lo:lo + 128].astype(F32) * MLA_SCALE).astype(BF16)
            ql_ref[:, lo + 128:lo + 256] = (_rope_bwd(dq_ref[:, lo + 128:lo + 256].astype(F32), cos_row, sin_row) * MLA_SCALE).astype(BF16)
            kvl_ref[:, h * 128:(h + 1) * 128] = dk_ref[:, lo:lo + 128]
            kr = kr + dk_ref[:, lo + 128:lo + 256].astype(F32)
        kvl_ref[:, 512:] = dv_ref[...]
        dqn = _dot(ql_ref[...], wq_ref[...], NN)
        dkvn = _dot(kvl_ref[...], wkv_ref[...], NT)
        dcq, ggq = norm_bwd(p_ref[:, :Q_LORA].astype(F32), dqn, gq_ref[...])
        dckv, ggkv = norm_bwd(p_ref[:, Q_LORA:640].astype(F32), dkvn, gkv_ref[...])
        aq_ref[...] += ggq
        akv_ref[...] += ggkv
        o_ref[:, :Q_LORA] = dcq.astype(BF16)
        o_ref[:, Q_LORA:640] = dckv.astype(BF16)
        o_ref[:, 640:768] = _rope_bwd(kr, cos_row, sin_row).astype(BF16)
        o_ref[:, 768:896] = dab_ref[...]
        o_ref[:, 896:1024] = jnp.zeros((tm, 128), BF16)

    rowb = lambda c: BS((tm, c), lambda i: (i, 0))
    full = lambda r, c: BS((r, c), lambda i: (0, 0))
    return pl.pallas_call(
        body, name="mla_proj_bwd", grid=(T // tm,),
        in_specs=[rowb(1024), rowb(1024), rowb(512), rowb(128), rowb(128), rowb(1024), rowb(128),
                  full(1024, Q_LORA), full(KV_LORA, 1024), full(1, Q_LORA), full(1, KV_LORA)],
        out_specs=[rowb(1024), rowb(1024), rowb(1024), full(1, Q_LORA), full(1, KV_LORA)],
        out_shape=[jax.ShapeDtypeStruct((T, 1024), BF16)] * 3
        + [jax.ShapeDtypeStruct((1, Q_LORA), F32), jax.ShapeDtypeStruct((1, KV_LORA), F32)],
        compiler_params=_arb(1))(dQ, dK, dV, cos_t, sin_t, P, dab, wq, wkv, gq, gkv)


def _mem_probs(qh, kh):
    s = _dot(qh, kh, NT) * MEM_SCALE
    p = jnp.exp(s - jnp.max(s, axis=1, keepdims=True))
    return p / jnp.sum(p, axis=1, keepdims=True)


def mem_attn_fwd(P, MKV, B, S, M, tq=512):
    T = B * S
    tq = min(tq, S)
    nq = S // tq

    def body(q_ref, kv_ref, o_ref):
        for h in range(N_HEADS):
            sl = slice(h * 128, (h + 1) * 128)
            p = _mem_probs(q_ref[:, sl].astype(BF16), kv_ref[:, sl])
            o_ref[:, sl] = _dot(p.astype(BF16), kv_ref[:, 512 + h * 128:512 + (h + 1) * 128], NN).astype(BF16)

    return pl.pallas_call(
        body, name="mem_attn_fwd", grid=(B, nq),
        in_specs=[BS((tq, 512), lambda b, i: (b * nq + i, OFF_MEMQ // 512)), BS((M, 1024), lambda b, i: (b, 0))],
        out_specs=BS((tq, 512), lambda b, i: (b * nq + i, 0)),
        out_shape=jax.ShapeDtypeStruct((T, 512), BF16), compiler_params=_arb(2))(P, MKV)


def mem_attn_bwd(P, MKV, dO, B, S, M, tq=512):
    T = B * S
    tq = min(tq, S)
    nq = S // tq

    def body(q_ref, kv_ref, do_ref, dq_ref, dkv_ref):
        @pl.when(pl.program_id(1) == 0)
        def _():
            dkv_ref[...] = jnp.zeros_like(dkv_ref)

        for h in range(N_HEADS):
            sl = slice(h * 128, (h + 1) * 128)
            sv = slice(512 + h * 128, 512 + (h + 1) * 128)
            qh = q_ref[:, sl].astype(BF16)
            kh = kv_ref[:, sl]
            do = do_ref[:, sl].astype(BF16)
            p = _mem_probs(qh, kh)
            dkv_ref[:, sv] += _dot(p.astype(BF16), do, TN)
            dp = _dot(do, kv_ref[:, sv], NT)
            ds = (p * (dp - jnp.sum(dp * p, axis=1, keepdims=True)) * MEM_SCALE).astype(BF16)
            dq_ref[:, sl] = _dot(ds, kh, NN).astype(BF16)
            dkv_ref[:, sl] += _dot(ds, qh, TN)

    return pl.pallas_call(
        body, name="mem_attn_bwd", grid=(B, nq),
        in_specs=[BS((tq, 512), lambda b, i: (b * nq + i, OFF_MEMQ // 512)), BS((M, 1024), lambda b, i: (b, 0)),
                  BS((tq, 512), lambda b, i: (b * nq + i, 0))],
        out_specs=[BS((tq, 512), lambda b, i: (b * nq + i, 0)), BS((M, 1024), lambda b, i: (b, 0))],
        out_shape=[jax.ShapeDtypeStruct((T, 512), BF16), jax.ShapeDtypeStruct((B * M, 1024), F32)],
        compiler_params=_arb(2))(P, MKV, dO)


def gain_grad(x, dy, name, tm=256):
    T, n = x.shape
    tm = min(tm, T)

    def body(x_ref, dy_ref, o_ref):
        @pl.when(pl.program_id(0) == 0)
        def _():
            o_ref[...] = jnp.zeros_like(o_ref)

        xv = x_ref[...]
        xh = xv * lax.rsqrt(jnp.mean(xv * xv, axis=-1, keepdims=True) + EPS)
        o_ref[...] += jnp.sum(dy_ref[...] * xh, axis=0, keepdims=True)

    return pl.pallas_call(
        body, name=name, grid=(T // tm,),
        in_specs=[BS((tm, n), lambda i: (i, 0))] * 2, out_specs=BS((1, n), lambda i: (0, 0)),
        out_shape=jax.ShapeDtypeStruct((1, n), F32), compiler_params=_arb(1))(x, dy)


def _conv_silu(x, w, t):
    y = x * w[3:4, :]
    for s in range(1, GDN_CONV):
        y = y + jnp.where(t >= s, pltpu.roll(x, s, 0), 0.0) * w[3 - s:4 - s, :]
    return y, _sigmoid(y)


def gdn_prep_fwd(P, conv_w, B, S):
    T = B * S

    def body(x_ref, w_ref, o_ref):
        kind = pl.program_id(1)
        t = lax.broadcasted_iota(jnp.int32, (S, 1), 0)
        y, sg = _conv_silu(x_ref[...].astype(F32), w_ref[...], t)
        a = y * sg
        scale = jnp.where(kind == 0, GDN_SCALE, 1.0).astype(F32)
        for h in range(N_HEADS):
            sl = slice(h * 128, (h + 1) * 128)
            seg = a[:, sl]
            n = lax.rsqrt(jnp.sum(seg * seg, axis=-1, keepdims=True) + EPS)
            o_ref[:, sl] = jnp.where(kind < 2, seg * (n * scale), seg)

    return pl.pallas_call(
        body, name="gdn_prep_fwd", grid=(B, 3),
        in_specs=[BS((S, 512), lambda b, k: (b, OFF_GDN // 512 + k)), BS((GDN_CONV, 512), lambda b, k: (0, k))],
        out_specs=BS((S, 512), lambda b, k: (b, k)),
        out_shape=jax.ShapeDtypeStruct((T, GDN_QKV), F32), compiler_params=_arb(2))(P, conv_w)


def gdn_prep_bwd(P, dqkv, conv_w, B, S):
    T = B * S

    def body(x_ref, d_ref, w_ref, o_ref, gw_ref):
        kind = pl.program_id(0)

        @pl.when(pl.program_id(1) == 0)
        def _():
            gw_ref[...] = jnp.zeros_like(gw_ref)

        t = lax.broadcasted_iota(jnp.int32, (S, 1), 0)
        x = x_ref[...].astype(F32)
        w = w_ref[...]
        y, sg = _conv_silu(x, w, t)
        a = y * sg
        scale = jnp.where(kind == 0, GDN_SCALE, 1.0).astype(F32)
        das = []
        for h in range(N_HEADS):
            sl = slice(h * 128, (h + 1) * 128)
            seg, dseg = a[:, sl], d_ref[:, sl]
            n = lax.rsqrt(jnp.sum(seg * seg, axis=-1, keepdims=True) + EPS)
            dn = scale * (n * dseg - seg * (n * n * n) * jnp.sum(dseg * seg, axis=-1, keepdims=True))
            das.append(jnp.where(kind < 2, dn, dseg))
        dy = jnp.concatenate(das, axis=1) * (sg * (1.0 + y * (1.0 - sg)))
        dx = dy * w[3:4, :]
        gw_ref[3:4, :] += jnp.sum(dy * x, axis=0, keepdims=True)
        for s in range(1, GDN_CONV):
            dx = dx + jnp.where(t + s < S, pltpu.roll(dy, S - s, 0), 0.0) * w[3 - s:4 - s, :]
            gw_ref[3 - s:4 - s, :] += jnp.sum(dy * jnp.where(t >= s, pltpu.roll(x, s, 0), 0.0), axis=0, keepdims=True)
        o_ref[...] = dx.astype(BF16)

    return pl.pallas_call(
        body, name="gdn_prep_bwd", grid=(3, B),
        in_specs=[BS((S, 512), lambda k, b: (b, OFF_GDN // 512 + k)), BS((S, 512), lambda k, b: (b, k)),
                  BS((GDN_CONV, 512), lambda k, b: (0, k))],
        out_specs=[BS((S, 512), lambda k, b: (b, k)), BS((GDN_CONV, 512), lambda k, b: (0, k))],
        out_shape=[jax.ShapeDtypeStruct((T, GDN_QKV), BF16), jax.ShapeDtypeStruct((GDN_CONV, GDN_QKV), F32)],
        compiler_params=_arb(2))(P, dqkv, conv_w)


def _chunk_row(n_rows):
    return lax.broadcasted_iota(jnp.int32, (n_rows, 1), 0) % CHUNK


def gdn_gate_fwd(P, alog_row, dt_row, B, S):
    T = B * S

    def body(x_ref, al_ref, dt_ref, o_ref):
        x = x_ref[...].astype(F32)
        lane = lax.broadcasted_iota(jnp.int32, (1, 128), 1)
        g = jnp.where(lane < 4, -jnp.exp(al_ref[...]) * _softplus(x + dt_ref[...]), 0.0)
        t = _chunk_row(S)
        for s in (1, 2, 4, 8, 16, 32):
            g = g + jnp.where(t >= s, pltpu.roll(g, s, 0), 0.0)
        o_ref[...] = jnp.where(lane < 4, g, jnp.where(lane < 8, _sigmoid(x), 0.0))

    row = BS((1, 128), lambda b: (0, 0))
    return pl.pallas_call(
        body, name="gdn_gate_fwd", grid=(B,),
        in_specs=[BS((S, 128), lambda b: (b, 768 // 128)), row, row], out_specs=BS((S, 128), lambda b: (b, 0)),
        out_shape=jax.ShapeDtypeStruct((T, 128), F32), compiler_params=_arb(1))(P, alog_row, dt_row)


def gdn_gate_bwd(P, dGB, alog_row, dt_row, B, S):
    T = B * S

    def body(x_ref, d_ref, al_ref, dt_ref, o_ref, acc_ref):
        @pl.when(pl.program_id(0) == 0)
        def _():
            acc_ref[...] = jnp.zeros_like(acc_ref)

        x, d = x_ref[...].astype(F32), d_ref[...]
        lane = lax.broadcasted_iota(jnp.int32, (1, 128), 1)
        z = x + dt_ref[...]
        coef = -jnp.exp(al_ref[...])
        g = coef * _softplus(z)
        da = jnp.where(lane < 4, d * coef * _sigmoid(z), 0.0)
        beta = _sigmoid(x)
        o_ref[...] = jnp.where(lane < 4, da, jnp.where(lane < 8, d * beta * (1.0 - beta), 0.0)).astype(BF16)
        acc_ref[0:1, :] += jnp.sum(jnp.where(lane < 4, d * g, 0.0), axis=0, keepdims=True)
        acc_ref[1:2, :] += jnp.sum(da, axis=0, keepdims=True)

    row = BS((1, 128), lambda b: (0, 0))
    return pl.pallas_call(
        body, name="gdn_gate_bwd", grid=(B,),
        in_specs=[BS((S, 128), lambda b: (b, 768 // 128)), BS((S, 128), lambda b: (b, 0)), row, row],
        out_specs=[BS((S, 128), lambda b: (b, 0)), BS((8, 128), lambda b: (0, 0))],
        out_shape=[jax.ShapeDtypeStruct((T, 128), BF16), jax.ShapeDtypeStruct((8, 128), F32)],
        compiler_params=_arb(1))(P, dGB, alog_row, dt_row)


def _chunk_masks(nc):
    r = lax.broadcasted_iota(jnp.int32, (nc, CHUNK, CHUNK), 1)
    c = lax.broadcasted_iota(jnp.int32, (nc, CHUNK, CHUNK), 2)
    return r >= c, r > c


def _chunk_local(q, k, gc, gr, beta, incl, strict):
    decay = jnp.exp(jnp.where(incl, gc - gr, NEG))
    kb = k * beta
    kbf = k.astype(BF16)
    m_kk = _bdot("gcd,gjd->gcj", kb.astype(BF16), kbf)
    l_mat = jnp.where(strict, m_kk * decay, 0.0)
    a_mat = _bdot("gcd,gjd->gcj", q.astype(BF16), kbf) * decay
    return decay, kb, l_mat, a_mat


WY_SPLIT_LEVELS = 2


def _split_bf16(x):
    hi = x.astype(BF16)
    return hi, (x - hi.astype(F32)).astype(BF16)


def _mm_split(ah, al, bh, bl):
    spec = "gij,gjk->gik"
    return _bdot(spec, ah, bh) + (_bdot(spec, ah, bl) + _bdot(spec, al, bh))


def gdn_chunk_fwd(qkv, GB, Grow, B, S, nc=8):
    T = B * S
    N = S // CHUNK
    nc = min(nc, N)
    nb = N // nc
    R = nc * CHUNK

    def body(q_ref, k_ref, v_ref, gb_ref, gr_ref, u_ref, w_ref, t_ref, a_ref):
        incl, strict = _chunk_masks(nc)
        eye = (lax.broadcasted_iota(jnp.int32, (nc, CHUNK, CHUNK), 1)
               == lax.broadcasted_iota(jnp.int32, (nc, CHUNK, CHUNK), 2)).astype(F32)
        for h in range(N_HEADS):
            sl = slice(h * 128, (h + 1) * 128)
            q = q_ref[:, sl].reshape(nc, CHUNK, 128)
            k = k_ref[:, sl].reshape(nc, CHUNK, 128)
            v = v_ref[:, sl].reshape(nc, CHUNK, 128)
            gc = gb_ref[:, h:h + 1].reshape(nc, CHUNK, 1)
            beta = gb_ref[:, 4 + h:5 + h].reshape(nc, CHUNK, 1)
            gr = gr_ref[h][:, None, :]
            _, kb, l_mat, a_mat = _chunk_local(q, k, gc, gr, beta, incl, strict)
            pw = -l_mat
            tinv = eye + pw
            for level in range(5):
                if level < WY_SPLIT_LEVELS:
                    ph, pl_ = _split_bf16(pw)
                    pw = _mm_split(ph, pl_, ph, pl_)
                    ph, pl_ = _split_bf16(pw)
                    th, tl = _split_bf16(tinv)
                    tinv = tinv + _mm_split(th, tl, ph, pl_)
                else:
                    ph = pw.astype(BF16)
                    pw = _bdot("gij,gjk->gik", ph, ph)
                    tinv = tinv + _bdot("gij,gjk->gik", tinv.astype(BF16), pw.astype(BF16))
            tb = tinv.astype(BF16)
            u = _bdot("gcj,gjv->gcv", tb, (v * beta).astype(BF16))
            w = _bdot("gcj,gjk->gck", tb, (kb * jnp.exp(gc)).astype(BF16))
            u_ref[:, sl] = u.reshape(R, 128)
            w_ref[:, sl] = w.reshape(R, 128).astype(BF16)
            t_ref[h] = jnp.swapaxes(tinv, 1, 2).astype(BF16)
            a_ref[h] = a_mat.astype(BF16)

    rowb = lambda c, j: BS((R, c), lambda b, n: (b * nb + n, j))
    mat = BS((None, N_HEADS, nc, CHUNK, CHUNK), lambda b, n: (b, 0, n, 0, 0))
    return pl.pallas_call(
        body, name="gdn_chunk_fwd", grid=(B, nb),
        in_specs=[rowb(512, 0), rowb(512, 1), rowb(512, 2), rowb(128, 0),
                  BS((None, N_HEADS, nc, CHUNK), lambda b, n: (b, 0, n, 0))],
        out_specs=[rowb(512, 0), rowb(512, 0), mat, mat],
        out_shape=[jax.ShapeDtypeStruct((T, 512), F32), jax.ShapeDtypeStruct((T, 512), BF16),
                   jax.ShapeDtypeStruct((B, N_HEADS, N, CHUNK, CHUNK), BF16),
                   jax.ShapeDtypeStruct((B, N_HEADS, N, CHUNK, CHUNK), BF16)],
        compiler_params=_arb(2))(qkv, qkv, qkv, GB, Grow)


SCAN_CHUNKS = 4


def gdn_scan_fwd(qkv3, U3, W3, GB3, A, B, S):
    N = S // CHUNK
    cps = SCAN_CHUNKS if N % SCAN_CHUNKS == 0 else 1

    def body(q_ref, k_ref, u_ref, w_ref, gb_ref, a_ref, o_ref, vn_ref, st_ref, s_s):
        @pl.when(pl.program_id(0) == 0)
        def _():
            s_s[...] = jnp.zeros_like(s_s)

        for c in range(cps):
            rows = slice(c * CHUNK, (c + 1) * CHUNK)
            for b in range(B):
                for h in range(N_HEADS):
                    sl = slice(h * 128, (h + 1) * 128)
                    st = s_s[b, h]
                    st_ref[b, h, c] = st
                    stb = st.astype(BF16)
                    g = gb_ref[b, rows, h:h + 1]
                    gl = g[CHUNK - 1:CHUNK, :]
                    qg = (q_ref[b, rows, sl] * jnp.exp(g)).astype(BF16)
                    on_state = _dot(jnp.concatenate([w_ref[b, rows, sl].astype(BF16), qg], axis=0), stb, NN)
                    vn = u_ref[b, rows, sl] - on_state[:CHUNK]
                    vnb = vn.astype(BF16)
                    kd_t = jnp.transpose(k_ref[b, rows, sl] * jnp.exp(gl - g)).astype(BF16)
                    on_vn = _dot(jnp.concatenate([a_ref[b, h, c].astype(BF16), kd_t], axis=0), vnb, NN)
                    vn_ref[b, rows, sl] = vnb
                    o_ref[b, rows, sl] = (on_state[CHUNK:] + on_vn[:CHUNK]).astype(BF16)
                    s_s[b, h] = st * jnp.exp(gl) + on_vn[CHUNK:]

    tok = lambda c, j: BS((B, cps * CHUNK, c), lambda n: (0, n, j))
    return pl.pallas_call(
        body, name="gdn_scan_fwd", grid=(N // cps,),
        in_specs=[tok(512, 0), tok(512, 1), tok(512, 0), tok(512, 0), tok(128, 0),
                  BS((B, N_HEADS, cps, CHUNK, CHUNK), lambda n: (0, 0, n, 0, 0))],
        out_specs=[tok(512, 0), tok(512, 0), BS((B, N_HEADS, cps, 128, 128), lambda n: (0, 0, n, 0, 0))],
        out_shape=[jax.ShapeDtypeStruct((B, S, 512), BF16), jax.ShapeDtypeStruct((B, S, 512), BF16),
                   jax.ShapeDtypeStruct((B, N_HEADS, N, 128, 128), F32)],
        scratch_shapes=[pltpu.VMEM((B, N_HEADS, 128, 128), F32)],
        compiler_params=_arb(1))(qkv3, qkv3, U3, W3, GB3, A)


def gdn_scan_bwd(dO3, qkv3, W3, Vn3, GB3, A, St, B, S):
    N = S // CHUNK
    cps = SCAN_CHUNKS if N % SCAN_CHUNKS == 0 else 1

    def body(do_ref, q_ref, k_ref, w_ref, vn_ref, gb_ref, a_ref, st_ref,
             du_ref, dw_ref, dq_ref, dk_ref, da_ref, dg_ref, ds_s):
        @pl.when(pl.program_id(0) == 0)
        def _():
            ds_s[...] = jnp.zeros_like(ds_s)

        lane = lax.broadcasted_iota(jnp.int32, (1, 128), 1)
        last = lax.broadcasted_iota(jnp.int32, (CHUNK, 1), 0) == CHUNK - 1
        for c in reversed(range(cps)):
            rows = slice(c * CHUNK, (c + 1) * CHUNK)
            for b in range(B):
                dg_all = jnp.zeros((CHUNK, 128), F32)
                for h in range(N_HEADS):
                    sl = slice(h * 128, (h + 1) * 128)
                    st = st_ref[b, h, c]
                    stb = st.astype(BF16)
                    dsn = ds_s[b, h]
                    dsnb = dsn.astype(BF16)
                    g = gb_ref[b, rows, h:h + 1]
                    gl = g[CHUNK - 1:CHUNK, :]
                    egl = jnp.exp(gl)
                    ekd = jnp.exp(gl - g)
                    eg = jnp.exp(g)
                    q, k = q_ref[b, rows, sl], k_ref[b, rows, sl]
                    kd = k * ekd
                    qg = q * eg
                    do = do_ref[b, rows, sl].astype(BF16)
                    vnb = vn_ref[b, rows, sl].astype(BF16)
                    dvn = _dot(a_ref[b, h, c].astype(BF16), do, TN) + _dot(kd.astype(BF16), dsnb, NN)
                    dvnb = dvn.astype(BF16)
                    do_on = _dot(do, jnp.concatenate([stb, vnb], axis=0), NT)
                    dqg = do_on[:, :128]
                    da_ref[b, h, c] = do_on[:, 128:]
                    dkd = _dot(vnb, dsnb, NT)
                    ds_s[b, h] = (_dot(qg.astype(BF16), do, TN) + egl * dsn - _dot(w_ref[b, rows, sl].astype(BF16), dvnb, TN))
                    du_ref[b, rows, sl] = dvnb
                    dw_ref[b, rows, sl] = (-_dot(dvnb, stb, NT)).astype(BF16)
                    dq_ref[b, rows, sl] = dqg * eg
                    dk_ref[b, rows, sl] = dkd * ekd
                    ddel = jnp.sum(dkd * kd, axis=1, keepdims=True)
                    dgl = jnp.sum(ddel, axis=0, keepdims=True) + jnp.sum(jnp.sum(st * dsn, axis=1, keepdims=True), axis=0, keepdims=True) * egl
                    col = jnp.sum(dqg * qg, axis=1, keepdims=True) - ddel + jnp.where(last, dgl, 0.0)
                    dg_all = jnp.where(lane == h, col, dg_all)
                dg_ref[b, rows, :] = dg_all

    steps = N // cps
    tok = lambda c, j: BS((B, cps * CHUNK, c), lambda n: (0, steps - 1 - n, j))
    mat = lambda d: BS((B, N_HEADS, cps, d, d), lambda n: (0, 0, steps - 1 - n, 0, 0))
    return pl.pallas_call(
        body, name="gdn_scan_bwd", grid=(steps,),
        in_specs=[tok(512, 0), tok(512, 0), tok(512, 1), tok(512, 0), tok(512, 0), tok(128, 0), mat(CHUNK), mat(128)],
        out_specs=[tok(512, 0), tok(512, 0), tok(512, 0), tok(512, 0), mat(CHUNK), tok(128, 0)],
        out_shape=[jax.ShapeDtypeStruct((B, S, 512), BF16)] * 2 + [jax.ShapeDtypeStruct((B, S, 512), F32)] * 2
        + [jax.ShapeDtypeStruct((B, N_HEADS, N, CHUNK, CHUNK), F32), jax.ShapeDtypeStruct((B, S, 128), F32)],
        scratch_shapes=[pltpu.VMEM((B, N_HEADS, 128, 128), F32)],
        compiler_params=_arb(1))(dO3, qkv3, qkv3, W3, Vn3, GB3, A, St)


def gdn_chunk_bwd(qkv, GB, Grow, Tinv, dA, dU, dW, dQ1, dK1, dG1, B, S, nc=8):
    T = B * S
    N = S // CHUNK
    nc = min(nc, N)
    nb = N // nc
    R = nc * CHUNK

    def body(q_ref, k_ref, v_ref, gb_ref, gr_ref, t_ref, da_ref, du_ref, dw_ref, dq1_ref, dk1_ref, dg1_ref, o_ref, dgb_ref):
        incl, strict = _chunk_masks(nc)
        lane = lax.broadcasted_iota(jnp.int32, (1, 128), 1)
        dg_all = dg1_ref[...]
        db_all = jnp.zeros((R, 128), F32)
        for h in range(N_HEADS):
            sl = slice(h * 128, (h + 1) * 128)
            q = q_ref[:, sl].reshape(nc, CHUNK, 128)
            k = k_ref[:, sl].reshape(nc, CHUNK, 128)
            v = v_ref[:, sl].reshape(nc, CHUNK, 128)
            gc = gb_ref[:, h:h + 1].reshape(nc, CHUNK, 1)
            beta = gb_ref[:, 4 + h:5 + h].reshape(nc, CHUNK, 1)
            gr = gr_ref[h][:, None, :]
            decay, kb, l_mat, a_mat = _chunk_local(q, k, gc, gr, beta, incl, strict)
            eg = jnp.exp(gc)
            kbg = kb * eg
            vb = v * beta
            tt = t_ref[h].astype(BF16)
            du = du_ref[:, sl].reshape(nc, CHUNK, 128).astype(BF16)
            dw = dw_ref[:, sl].reshape(nc, CHUNK, 128).astype(BF16)
            dvb = _bdot("gjc,gcv->gjv", tt, du)
            dkbg = _bdot("gjc,gck->gjk", tt, dw)
            dt = _bdot("gcv,gjv->gcj", du, vb.astype(BF16)) + _bdot("gck,gjk->gcj", dw, kbg.astype(BF16))
            tmp = _bdot("gca,gab->gcb", tt, dt.astype(BF16))
            dl = jnp.where(strict, -_bdot("gcb,gbd->gcd", tmp.astype(BF16), tt), 0.0)
            da = da_ref[h]
            dm = (dl * decay).astype(BF16)
            dqk = (da * decay).astype(BF16)
            kbf = k.astype(BF16)
            dkb = _bdot("gcj,gjd->gcd", dm, kbf) + dkbg * eg
            dk = (_bdot("gcj,gcd->gjd", dm, kb.astype(BF16)) + _bdot("gcj,gcd->gjd", dqk, q.astype(BF16))
                  + dk1_ref[:, sl].reshape(nc, CHUNK, 128) + dkb * beta)
            dq = _bdot("gcj,gjd->gcd", dqk, kbf) + dq1_ref[:, sl].reshape(nc, CHUNK, 128)
            e = dl * l_mat + da * a_mat
            dgc = (jnp.sum(e, axis=2, keepdims=True) - jnp.sum(jnp.swapaxes(e, 1, 2), axis=2, keepdims=True)
                   + jnp.sum(dkbg * kbg, axis=2, keepdims=True))
            dbeta = jnp.sum(dkb * k, axis=2, keepdims=True) + jnp.sum(dvb * v, axis=2, keepdims=True)
            o_ref[:, sl] = dq.reshape(R, 128)
            o_ref[:, 512 + h * 128:512 + (h + 1) * 128] = dk.reshape(R, 128)
            o_ref[:, 1024 + h * 128:1024 + (h + 1) * 128] = (dvb * beta).reshape(R, 128)
            dg_all = dg_all + jnp.where(lane == h, dgc.reshape(R, 1), 0.0)
            db_all = jnp.where(lane == 4 + h, dbeta.reshape(R, 1), db_all)
        t = _chunk_row(R)
        for s in (1, 2, 4, 8, 16, 32):
            dg_all = dg_all + jnp.where(t + s < CHUNK, pltpu.roll(dg_all, R - s, 0), 0.0)
        dgb_ref[...] = jnp.where(lane < 4, dg_all, db_all)

    rowb = lambda c, j: BS((R, c), lambda b, n: (b * nb + n, j))
    mat = BS((None, N_HEADS, nc, CHUNK, CHUNK), lambda b, n: (b, 0, n, 0, 0))
    return pl.pallas_call(
        body, name="gdn_chunk_bwd", grid=(B, nb),
        in_specs=[rowb(512, 0), rowb(512, 1), rowb(512, 2), rowb(128, 0),
                  BS((None, N_HEADS, nc, CHUNK), lambda b, n: (b, 0, n, 0)), mat, mat,
                  rowb(512, 0), rowb(512, 0), rowb(512, 0), rowb(512, 0), rowb(128, 0)],
        out_specs=[rowb(GDN_QKV, 0), rowb(128, 0)],
        out_shape=[jax.ShapeDtypeStruct((T, GDN_QKV), F32), jax.ShapeDtypeStruct((T, 128), F32)],
        compiler_params=_arb(2))(qkv, qkv, qkv, GB, Grow, Tinv, dA, dU, dW, dQ1, dK1, dG1)


def _gdn_out_norm(og, gg):
    outs, xhs, rs = [], [], []
    for h in range(N_HEADS):
        seg = og[:, h * 128:(h + 1) * 128]
        r = lax.rsqrt(jnp.mean(seg * seg, axis=-1, keepdims=True) + EPS)
        xh = seg * r
        outs.append(xh * gg)
        xhs.append(xh)
        rs.append(r)
    return outs, xhs, rs


def merge_fwd(o_mla, o_gdn, o_mem, P, x, tgt, w_out, g_gdn, g_fin, tm=512):
    T = x.shape[0]
    tm = min(tm, T)

    def body(om_ref, og_ref, oc_ref, gate_ref, x_ref, t_ref, w_ref, gg_ref, gf_ref, mix_ref, dx_ref, dxb_ref, sq_ref, gnf_ref):
        @pl.when(pl.program_id(0) == 0)
        def _():
            sq_ref[...] = jnp.zeros_like(sq_ref)
            gnf_ref[...] = jnp.zeros_like(gnf_ref)

        ogn, _, _ = _gdn_out_norm(og_ref[...].astype(F32), gg_ref[...])
        cat = jnp.concatenate([om_ref[...].astype(F32)] + ogn + [oc_ref[...].astype(F32)], axis=1)
        gt = gate_ref[...].astype(F32)
        mixed = (cat * (gt * _sigmoid(gt))).astype(BF16)
        mix_ref[...] = mixed
        x2 = x_ref[...] + _dot(mixed, w_ref[...], NN)
        r2 = lax.rsqrt(jnp.mean(x2 * x2, axis=-1, keepdims=True) + EPS)
        xh = x2 * r2
        gf = gf_ref[...]
        diff = xh * gf - t_ref[...]
        sq_ref[...] += jnp.sum(diff * diff, axis=0, keepdims=True)
        dy = diff * (1.0 / D_MODEL)
        gnf_ref[...] += jnp.sum(dy * xh, axis=0, keepdims=True)
        dxh = dy * gf
        dx = r2 * (dxh - xh * jnp.mean(dxh * xh, axis=-1, keepdims=True))
        dx_ref[...] = dx
        dxb_ref[...] = dx.astype(BF16)

    rowb = lambda c, j=0: BS((tm, c), lambda i: (i, j))
    full = lambda r, c: BS((r, c), lambda i: (0, 0))
    return pl.pallas_call(
        body, name="merge_fwd", grid=(T // tm,),
        in_specs=[rowb(512), rowb(512), rowb(512), rowb(D_MIX, OFF_GATE // D_MIX), rowb(D_MODEL), rowb(D_MODEL),
                  full(D_MIX, D_MODEL), full(1, 128), full(1, D_MODEL)],
        out_specs=[rowb(D_MIX), rowb(D_MODEL), rowb(D_MODEL), full(1, D_MODEL), full(1, D_MODEL)],
        out_shape=[jax.ShapeDtypeStruct((T, D_MIX), BF16), jax.ShapeDtypeStruct((T, D_MODEL), F32),
                   jax.ShapeDtypeStruct((T, D_MODEL), BF16),
                   jax.ShapeDtypeStruct((1, D_MODEL), F32), jax.ShapeDtypeStruct((1, D_MODEL), F32)],
        compiler_params=_arb(1))(o_mla, o_gdn, o_mem, P, x, tgt, w_out, g_gdn, g_fin)


def merge_bwd(dx2, o_mla, o_gdn, o_mem, P, w_out, g_gdn, tm=512):
    T = dx2.shape[0]
    tm = min(tm, T)

    def body(dx_ref, om_ref, og_ref, oc_ref, gate_ref, w_ref, gg_ref, dgate_ref, dom_ref, dog_ref, doc_ref, ggn_ref):
        @pl.when(pl.program_id(0) == 0)
        def _():
            ggn_ref[...] = jnp.zeros_like(ggn_ref)

        gg = gg_ref[...]
        dmix = _dot(dx_ref[...].astype(BF16), w_ref[...], NT)
        ogn, xhs, rs = _gdn_out_norm(og_ref[...].astype(F32), gg)
        cat = jnp.concatenate([om_ref[...].astype(F32)] + ogn + [oc_ref[...].astype(F32)], axis=1)
        gt = gate_ref[...].astype(F32)
        sg = _sigmoid(gt)
        dgate_ref[...] = (dmix * cat * (sg * (1.0 + gt * (1.0 - sg)))).astype(BF16)
        dcat = dmix * (gt * sg)
        dom_ref[...] = dcat[:, :512].astype(BF16)
        doc_ref[...] = dcat[:, 1024:].astype(BF16)
        acc = jnp.zeros((1, 128), F32)
        for h in range(N_HEADS):
            dseg = dcat[:, 512 + h * 128:512 + (h + 1) * 128]
            acc = acc + jnp.sum(dseg * xhs[h], axis=0, keepdims=True)
            dxh = dseg * gg
            dog_ref[:, h * 128:(h + 1) * 128] = (rs[h] * (dxh - xhs[h] * jnp.mean(dxh * xhs[h], axis=-1, keepdims=True))).astype(BF16)
        ggn_ref[...] += acc

    rowb = lambda c, j=0: BS((tm, c), lambda i: (i, j))
    full = lambda r, c: BS((r, c), lambda i: (0, 0))
    return pl.pallas_call(
        body, name="merge_bwd", grid=(T // tm,),
        in_specs=[rowb(D_MODEL), rowb(512), rowb(512), rowb(512), rowb(D_MIX, OFF_GATE // D_MIX),
                  full(D_MIX, D_MODEL), full(1, 128)],
        out_specs=[rowb(D_MIX), rowb(512), rowb(512), rowb(512), full(1, 128)],
        out_shape=[jax.ShapeDtypeStruct((T, D_MIX), BF16)] + [jax.ShapeDtypeStruct((T, 512), BF16)] * 3
        + [jax.ShapeDtypeStruct((1, 128), F32)],
        compiler_params=_arb(1))(dx2, o_mla, o_gdn, o_mem, P, w_out, g_gdn)


def in_proj_bwd(dP, wp, x, dx2, gain, after, tm=512):
    T, n = x.shape
    tm = min(tm, T)
    k = len(dP)
    widths = [p.shape[1] for p in dP]
    offs = [sum(widths[:i]) for i in range(k)]

    def body(*refs):
        w_ref, x_ref, dx2_ref, g_ref = refs[k:k + 4]
        o_ref, acc_ref = refs[-2:]

        @pl.when(pl.program_id(0) == 0)
        def _():
            acc_ref[...] = jnp.zeros_like(acc_ref)

        dy = None
        for a_ref, off, w in zip(refs[:k], offs, widths):
            d = _dot(a_ref[...], w_ref[off:off + w, :], NN)
            dy = d if dy is None else dy + d
        xv = x_ref[...]
        r = lax.rsqrt(jnp.mean(xv * xv, axis=-1, keepdims=True) + EPS)
        xh = xv * r
        acc_ref[...] += jnp.sum(dy * xh, axis=0, keepdims=True)
        dxh = dy * g_ref[...]
        o_ref[...] = dx2_ref[...] + r * (dxh - xh * jnp.mean(dxh * xh, axis=-1, keepdims=True))

    rowb = BS((tm, n), lambda i: (i, 0))
    full = BS((1, n), lambda i: (0, 0))
    return pl.pallas_call(
        body, name="in_proj_bwd", grid=(T // tm,),
        in_specs=[BS((tm, w), lambda i: (i, 0)) for w in widths]
        + [BS(wp.shape, lambda i: (0, 0), pipeline_mode=pl.Buffered(1)), rowb, rowb, full, BS(memory_space=pl.ANY)],
        out_specs=[rowb, full], out_shape=[jax.ShapeDtypeStruct((T, n), F32), jax.ShapeDtypeStruct((1, n), F32)],
        compiler_params=_arb(1))(*dP, wp, x, dx2, gain, after)


W_IN_SHARD = D_IN // 4
_GDN0 = Q_LORA + KV_LORA + MLA_ROPE
_AB0 = _GDN0 + GDN_QKV
_MEMQ0 = _AB0 + 2 * N_HEADS
_GATE0 = _MEMQ0 + N_HEADS * MEM_DH


def _w_in_row_map():
    a, m, gt = _AB0 - 2 * W_IN_SHARD, _MEMQ0 - 2 * W_IN_SHARD, _GATE0 - 2 * W_IN_SHARD
    e0 = OFF_GDN + W_IN_SHARD - _GDN0
    e1 = e0 + W_IN_SHARD
    e2 = OFF_GATE + W_IN_SHARD - gt
    return [(0, 0, 0, 672), (0, 672, 704, 32), (2, a, 768, m - a), (2, m, OFF_MEMQ, gt - m), (0, _GDN0, OFF_GDN, W_IN_SHARD - _GDN0),
            (1, 0, e0, W_IN_SHARD), (2, 0, e1, a), (2, gt, OFF_GATE, W_IN_SHARD - gt), (3, 0, e2, W_IN_SHARD)]


_W_IN_ZERO_ROWS = [(672, 32), (736, 32), (776, 248)]
W_IN_LANES = 256


EARLY_ROWS = 704
EARLY_ROW0 = (0, 80)


def pad_w_in_t_early(early):
    per_half = early.shape[3] // W_IN_LANES

    def body(s_ref, o_ref):
        for r0, n in _W_IN_ZERO_ROWS:
            o_ref[r0:r0 + n, :] = jnp.zeros((n, W_IN_LANES), o_ref.dtype)
        for q, src, dst, n in _w_in_row_map():
            if dst < OFF_GDN:
                first = src - EARLY_ROW0[q // 2]
                o_ref[dst:dst + n, :] = s_ref[q // 2, first:first + n, :]

    return pl.pallas_call(
        body, name="pad_w_in_t_early", grid=(D_MODEL // W_IN_LANES,),
        in_specs=[BS((2, None, EARLY_ROWS, W_IN_LANES), lambda j: (0, j // per_half, 0, j % per_half))],
        out_specs=BS((OFF_GDN, W_IN_LANES), lambda j: (0, j)),
        out_shape=jax.ShapeDtypeStruct((N_PAD, D_MODEL), early.dtype), compiler_params=_arb(1))(early)


def pad_w_in_t_rest(shards, wp):
    per_half = shards.shape[3] // W_IN_LANES
    blocks = (OFF_GDN, OFF_GATE)
    assert OFF_GATE - OFF_GDN == OFF_GDN and N_PAD - OFF_GATE == OFF_GDN

    def body(s_ref, w_in, o_ref):
        for i, b0 in enumerate(blocks):
            @pl.when(pl.program_id(1) == i)
            def _(b0=b0):
                for q, src, dst, n in _w_in_row_map():
                    if b0 <= dst < b0 + OFF_GDN:
                        o_ref[dst - b0:dst - b0 + n, :] = s_ref[q, src:src + n, :]

    return pl.pallas_call(
        body, name="pad_w_in_t_rest", grid=(D_MODEL // W_IN_LANES, len(blocks)),
        in_specs=[BS((N_CHIPS, None, W_IN_SHARD, W_IN_LANES), lambda j, i: (0, j // per_half, 0, j % per_half)), ANY],
        out_specs=BS((OFF_GDN, W_IN_LANES), lambda j, i: (1 + i, j)), input_output_aliases={1: 0},
        out_shape=jax.ShapeDtypeStruct((N_PAD, D_MODEL), shards.dtype), compiler_params=_arb(2))(shards, wp)


def unpad_w_in_t(g):
    def body(g_ref, o_ref):
        for q, src, dst, n in _w_in_row_map():
            o_ref[q, src:src + n, :] = g_ref[dst:dst + n, :]

    return pl.pallas_call(
        body, name="unpad_w_in_t", grid=(D_MODEL // W_IN_LANES,),
        in_specs=[BS((N_PAD, W_IN_LANES), lambda j: (0, j))], out_specs=BS((N_CHIPS, W_IN_SHARD, W_IN_LANES), lambda j: (0, 0, j)),
        out_shape=jax.ShapeDtypeStruct((N_CHIPS, W_IN_SHARD, D_MODEL), g.dtype), compiler_params=_arb(1))(g)


def _perm_w_kv_b(s):
    return jnp.concatenate([s[h, :, :128] for h in range(N_HEADS)] + [s[h, :, 128:] for h in range(N_HEADS)], axis=1)


def _unperm_w_kv_b(g):
    return jnp.stack([jnp.concatenate([g[:, h * 128:(h + 1) * 128], g[:, 512 + h * 128:512 + (h + 1) * 128]], axis=1)
                      for h in range(N_HEADS)])


def _lane_row(v4):
    return jnp.pad(v4.reshape(1, -1).astype(F32), ((0, 0), (0, 128 - v4.size)))


N_CHIPS = 4
MESH = pl.DeviceIdType.MESH
ANY = BS(memory_space=pl.ANY)


def _place():
    return lax.axis_index("x"), lax.axis_index("y"), lax.axis_index("c")


def _other_chips(x, y):
    return [(1 - x, y), (x, 1 - y), (1 - x, 1 - y)]


def _half(split, which):
    axis, size = split
    ds = pl.ds(pl.multiple_of(which * size, 16 if axis == 0 else 128), size)
    return (ds, slice(None)) if axis == 0 else (slice(None), ds)


SEM = BS(memory_space=pltpu.SEMAPHORE)
HBM = BS(memory_space=pltpu.HBM)
_IN_HBM = lambda a: pltpu.with_memory_space_constraint(a, pltpu.HBM)
_SIDE_EFFECT = pltpu.SideEffectType.DATAFLOW_SIDE_EFFECTING


def _late_gather_copies(s_refs, l_refs, send_sems, recv_sems, local_sems, with_arrivals):
    x, y, c = _place()
    sends, recvs, locals_ = [], [], []
    for i, (s_ref, l_ref) in enumerate(zip(s_refs, l_refs)):
        locals_.append(pltpu.make_async_copy(s_ref, l_ref.at[2 * x + y], local_sems.at[i]))
        for j, (px, py) in enumerate(_other_chips(x, y)):
            k = 3 * i + j
            sends.append(pltpu.make_async_remote_copy(src_ref=s_ref, dst_ref=l_ref.at[2 * x + y], send_sem=send_sems.at[k],
                                                      recv_sem=recv_sems.at[k], device_id=(px, py, c), device_id_type=MESH))
            if with_arrivals:
                recvs.append(pltpu.make_async_remote_copy(src_ref=s_ref, dst_ref=l_ref.at[2 * px + py], send_sem=send_sems.at[k],
                                                          recv_sem=recv_sems.at[k], device_id=(px, py, c), device_id_type=MESH))
    return sends, recvs, locals_


def late_gather_start(shards, after, name):
    n = len(shards)

    def body(*refs):
        s_refs, l_refs = refs[:n], refs[n:2 * n]
        send_sems, recv_sems, local_sems = refs[2 * n + 1:2 * n + 4]
        token = refs[-1]
        sends, _, locals_ = _late_gather_copies(s_refs, l_refs, send_sems, recv_sems, local_sems, False)
        for cp in locals_ + sends:
            cp.start()
        token[...] = jnp.zeros_like(token)

    lands = [lax.empty((N_CHIPS,) + s.shape, s.dtype) for s in shards]
    hbm_like = lambda a: pltpu.HBM(a.shape, a.dtype)
    out = pl.pallas_call(
        body, name=name,
        out_shape=[pltpu.SemaphoreType.DMA((3 * n,)), pltpu.SemaphoreType.DMA((3 * n,)), pltpu.SemaphoreType.DMA((n,))]
        + [hbm_like(s) for s in shards] + [hbm_like(l) for l in lands] + [jax.ShapeDtypeStruct((8, 128), F32)],
        in_specs=[HBM] * (2 * n) + [BS(memory_space=pl.ANY)], out_specs=[SEM] * 3 + [HBM] * (2 * n) + [BS(memory_space=pltpu.VMEM)],
        input_output_aliases={i: 3 + i for i in range(2 * n)},
        compiler_params=pltpu.CompilerParams(has_side_effects=_SIDE_EFFECT))(
            *[_IN_HBM(s) for s in shards], *[_IN_HBM(l) for l in lands], after)
    return out[:3], out[3:3 + n], out[3 + n:3 + 2 * n], out[-1]


def late_gather_wait(sems, shards, lands, after, name):
    n = len(shards)

    def body(*refs):
        s_refs, l_refs = refs[:n], refs[n:2 * n]
        send_sems, recv_sems, local_sems = refs[2 * n:2 * n + 3]
        sends, recvs, locals_ = _late_gather_copies(s_refs, l_refs, send_sems, recv_sems, local_sems, True)
        for cp in locals_:
            cp.wait()
        for cp in sends:
            cp.wait_send()
        for cp in recvs:
            cp.wait_recv()

    hbm_like = lambda a: pltpu.HBM(a.shape, a.dtype)
    out = pl.pallas_call(
        body, name=name, out_shape=[hbm_like(s) for s in shards] + [hbm_like(l) for l in lands],
        in_specs=[HBM] * (2 * n) + [SEM] * 3 + [BS(memory_space=pl.ANY)], out_specs=[HBM] * (2 * n),
        input_output_aliases={i: i for i in range(2 * n)},
        compiler_params=pltpu.CompilerParams(has_side_effects=_SIDE_EFFECT))(*shards, *lands, *sems, after)
    return out[n:]


def _sends(x, y, early):
    return (y == 0) if early else None


def _block(x, y, early):
    return x if early else 2 * x + y


def _if(cond, fn):
    if cond is None:
        fn()
    else:
        pl.when(cond)(fn)


def _half_gather_copies(s_ref, l_ref, send_sems, recv_sems, with_arrivals, early):
    x, y, c = _place()
    sends, recvs, peer_sends = [], [], []
    for j, (px, py) in enumerate(_other_chips(x, y)):
        sends.append(pltpu.make_async_remote_copy(src_ref=s_ref.at[c], dst_ref=l_ref.at[_block(x, y, early), c],
                                                  send_sem=send_sems.at[j], recv_sem=recv_sems.at[j], device_id=(px, py, c),
                                                  device_id_type=MESH))
        if with_arrivals:
            recvs.append(pltpu.make_async_remote_copy(src_ref=s_ref.at[c], dst_ref=l_ref.at[_block(px, py, early), c],
                                                      send_sem=send_sems.at[j], recv_sem=recv_sems.at[j], device_id=(px, py, c),
                                                      device_id_type=MESH))
            peer_sends.append(_sends(px, py, early))
    return sends, recvs, peer_sends


def half_gather_start(shard, name, early=False, after=()):
    def body(s_ref, l_ref, *rest):
        send_sems, recv_sems, local_sem, s_thru, l_thru, token = rest[len(after):]
        x, y, _ = _place()

        def go():
            pltpu.make_async_copy(s_ref, l_ref.at[_block(x, y, early)], local_sem.at[0]).start()
            for cp in _half_gather_copies(s_ref, l_ref, send_sems, recv_sems, False, early)[0]:
                cp.start()

        _if(_sends(x, y, early), go)
        token[...] = jnp.zeros_like(token)

    land = lax.empty((2 if early else N_CHIPS,) + shard.shape, shard.dtype)
    out = pl.pallas_call(
        body, name=name,
        out_shape=[pltpu.SemaphoreType.DMA((3,)), pltpu.SemaphoreType.DMA((3,)), pltpu.SemaphoreType.DMA((1,)),
                   pltpu.HBM(shard.shape, shard.dtype), pltpu.HBM(land.shape, land.dtype), jax.ShapeDtypeStruct((8, 128), F32)],
        in_specs=[HBM, HBM] + [BS(memory_space=pl.ANY)] * len(after),
        out_specs=[SEM] * 3 + [HBM, HBM, BS(memory_space=pltpu.VMEM)],
        input_output_aliases={0: 3, 1: 4},
        compiler_params=pltpu.CompilerParams(has_side_effects=_SIDE_EFFECT))(_IN_HBM(shard), _IN_HBM(land), *after)
    return out[:3], out[3], out[4], out[5]


def half_gather_wait(sems, shard, land, after, name, early=False):
    def body(s_ref, l_ref, send_sems, recv_sems, local_sem, *rest):
        x, y, _ = _place()
        sends, recvs, peer_sends = _half_gather_copies(s_ref, l_ref, send_sems, recv_sems, True, early)

        def sent():
            pltpu.make_async_copy(s_ref, l_ref.at[_block(x, y, early)], local_sem.at[0]).wait()
            for cp in sends:
                cp.wait_send()

        _if(_sends(x, y, early), sent)
        for cp, cond in zip(recvs, peer_sends):
            _if(cond, cp.wait_recv)

    out = pl.pallas_call(
        body, name=name, out_shape=[pltpu.HBM(shard.shape, shard.dtype), pltpu.HBM(land.shape, land.dtype)],
        in_specs=[HBM, HBM] + [SEM] * 3 + [BS(memory_space=pl.ANY)] * len(after), out_specs=[HBM, HBM],
        input_output_aliases={0: 0, 1: 1},
        compiler_params=pltpu.CompilerParams(has_side_effects=_SIDE_EFFECT))(shard, land, *sems, *after)
    return out[1]


def pass_halves_to_sibling(land, name, early=False):
    def body(l_in, l_ref, send_sems, recv_sems):
        x, y, c = _place()
        copies = []
        for j, (px, py) in enumerate(_other_chips(x, y)):
            q = _block(px, py, early)
            give = pltpu.make_async_remote_copy(src_ref=l_ref.at[q, c], dst_ref=l_ref.at[q, c], send_sem=send_sems.at[j],
                                                recv_sem=recv_sems.at[j], device_id=(x, y, 1 - c), device_id_type=MESH)
            take = pltpu.make_async_remote_copy(src_ref=l_ref.at[q, c], dst_ref=l_ref.at[q, 1 - c], send_sem=send_sems.at[j],
                                                recv_sem=recv_sems.at[j], device_id=(x, y, 1 - c), device_id_type=MESH)
            _if(_sends(px, py, early), give.start)
            copies.append((give, take, _sends(px, py, early)))
        for give, take, cond in copies:
            def done(give=give, take=take):
                take.wait_recv()
                give.wait_send()

            _if(cond, done)

    return pl.pallas_call(
        body, name=name, in_specs=[ANY], out_specs=ANY, out_shape=jax.ShapeDtypeStruct(land.shape, land.dtype),
        input_output_aliases={0: 0},
        scratch_shapes=[pltpu.SemaphoreType.DMA((3,)), pltpu.SemaphoreType.DMA((3,))])(land)


def allgather_devices(block, name):
    R, C = block.shape

    def body(b_ref, o_ref, send_sems, recv_sems, local_sem):
        x, y, c = _place()
        me = 4 * x + 2 * y + c
        own = pltpu.make_async_copy(b_ref, o_ref.at[me], local_sem)
        own.start()
        copies = []
        for r in range(1, 8):
            px = 1 - x if r & 4 else x
            py = 1 - y if r & 2 else y
            pc = 1 - c if r & 1 else c
            send = pltpu.make_async_remote_copy(src_ref=b_ref, dst_ref=o_ref.at[me], send_sem=send_sems.at[r - 1],
                                                recv_sem=recv_sems.at[r - 1], device_id=(px, py, pc), device_id_type=MESH)
            recv = pltpu.make_async_remote_copy(src_ref=b_ref, dst_ref=o_ref.at[4 * px + 2 * py + pc], send_sem=send_sems.at[r - 1],
                                                recv_sem=recv_sems.at[r - 1], device_id=(px, py, pc), device_id_type=MESH)
            send.start()
            copies.append((send, recv))
        for send, recv in copies:
            recv.wait_recv()
            send.wait_send()
        own.wait()

    return pl.pallas_call(
        body, name=name, in_specs=[ANY], out_specs=ANY, out_shape=jax.ShapeDtypeStruct((8, R, C), block.dtype),
        scratch_shapes=[pltpu.SemaphoreType.DMA((7,)), pltpu.SemaphoreType.DMA((7,)), pltpu.SemaphoreType.DMA(())])(block)


def swap_sibling(arrs, name, splits=None):
    n = len(arrs)

    def sent(a_ref, i, c):
        return a_ref if splits is None else a_ref.at[(slice(None),) + _half(splits[i], 1 - c)]

    def out_shape(a, i):
        if splits is None:
            return a.shape
        axis, size = splits[i]
        return (a.shape[0], size, a.shape[2]) if axis == 0 else (a.shape[0], a.shape[1], size)

    def body(*refs):
        a_refs, o_refs = refs[:n], refs[n:2 * n]
        send_sems, recv_sems = refs[2 * n:]
        x, y, c = _place()
        copies = [pltpu.make_async_remote_copy(src_ref=sent(a_ref, i, c), dst_ref=o_ref, send_sem=send_sems.at[i],
                                               recv_sem=recv_sems.at[i], device_id=(x, y, 1 - c), device_id_type=MESH)
                  for i, (a_ref, o_ref) in enumerate(zip(a_refs, o_refs))]
        for cp in copies:
            cp.start()
        for cp in copies:
            cp.wait()

    return pl.pallas_call(
        body, name=name, in_specs=[ANY] * n, out_specs=[ANY] * n,
        out_shape=[jax.ShapeDtypeStruct(out_shape(a, i), a.dtype) for i, a in enumerate(arrs)],
        scratch_shapes=[pltpu.SemaphoreType.DMA((n,)), pltpu.SemaphoreType.DMA((n,))])(*arrs)


def _exchange_copies(p_refs, l_refs, send_sems, recv_sems):
    x, y, c = _place()
    return [pltpu.make_async_remote_copy(src_ref=p_ref.at[2 * px + py], dst_ref=l_ref.at[j], send_sem=send_sems.at[3 * i + j],
                                         recv_sem=recv_sems.at[3 * i + j], device_id=(px, py, c), device_id_type=MESH)
            for i, (p_ref, l_ref) in enumerate(zip(p_refs, l_refs)) for j, (px, py) in enumerate(_other_chips(x, y))]


def exchange_chips_start(parts, name):
    n = len(parts)

    def body(*refs):
        send_sems, recv_sems = refs[2 * n:2 * n + 2]
        for cp in _exchange_copies(refs[:n], refs[n:2 * n], send_sems, recv_sems):
            cp.start()
        refs[-1][...] = jnp.zeros_like(refs[-1])

    lands = [lax.empty((3,) + p.shape[1:], p.dtype) for p in parts]
    hbm_like = lambda a: pltpu.HBM(a.shape, a.dtype)
    out = pl.pallas_call(
        body, name=name,
        out_shape=[pltpu.SemaphoreType.DMA((3 * n,)), pltpu.SemaphoreType.DMA((3 * n,))]
        + [hbm_like(p) for p in parts] + [hbm_like(l) for l in lands] + [jax.ShapeDtypeStruct((8, 128), F32)],
        in_specs=[HBM] * (2 * n), out_specs=[SEM] * 2 + [HBM] * (2 * n) + [BS(memory_space=pltpu.VMEM)],
        input_output_aliases={i: 2 + i for i in range(2 * n)},
        compiler_params=pltpu.CompilerParams(has_side_effects=_SIDE_EFFECT))(*[_IN_HBM(p) for p in parts], *[_IN_HBM(l) for l in lands])
    return out[:2], out[2:2 + n], out[2 + n:2 + 2 * n], out[-1]


def exchange_chips_wait(sems, parts, lands, after, name):
    n = len(parts)

    def body(*refs):
        send_sems, recv_sems = refs[2 * n:2 * n + 2]
        for cp in _exchange_copies(refs[:n], refs[n:2 * n], send_sems, recv_sems):
            cp.wait_send()
            cp.wait_recv()

    hbm_like = lambda a: pltpu.HBM(a.shape, a.dtype)
    out = pl.pallas_call(
        body, name=name, out_shape=[hbm_like(p) for p in parts] + [hbm_like(l) for l in lands],
        in_specs=[HBM] * (2 * n) + [SEM] * 2 + [BS(memory_space=pl.ANY)], out_specs=[HBM] * (2 * n),
        input_output_aliases={i: i for i in range(2 * n)},
        compiler_params=pltpu.CompilerParams(has_side_effects=_SIDE_EFFECT))(*parts, *lands, *sems, after)
    return out[n:]


DIRECT_SLOTS = 7


def _direct_copies(p_refs, l_refs, splits, send_sems, recv_sems):
    x, y, c = _place()
    copies = []
    for i, (p_ref, l_ref) in enumerate(zip(p_refs, l_refs)):
        def copy(q, core, slot, device, p_ref=p_ref, l_ref=l_ref, i=i):
            k = DIRECT_SLOTS * i + slot
            return pltpu.make_async_remote_copy(src_ref=p_ref.at[(q,) + _half(splits[i], core)], dst_ref=l_ref.at[slot],
                                                send_sem=send_sems.at[k], recv_sem=recv_sems.at[k], device_id=device,
                                                device_id_type=MESH)

        for j, (px, py) in enumerate(_other_chips(x, y)):
            copies.append(copy(2 * px + py, c, j, (px, py, c)))
            copies.append(copy(2 * px + py, 1 - c, 3 + j, (px, py, 1 - c)))
        copies.append(copy(2 * x + y, 1 - c, DIRECT_SLOTS - 1, (x, y, 1 - c)))
    return copies


def exchange_direct_start(parts, splits, name):
    n = len(parts)

    def body(*refs):
        send_sems, recv_sems = refs[2 * n:2 * n + 2]
        for cp in _direct_copies(refs[:n], refs[n:2 * n], splits, send_sems, recv_sems):
            cp.start()
        refs[-1][...] = jnp.zeros_like(refs[-1])

    lands = [lax.empty((DIRECT_SLOTS,) + _half_block(p.shape[1:], s), p.dtype) for p, s in zip(parts, splits)]
    hbm_like = lambda a: pltpu.HBM(a.shape, a.dtype)
    out = pl.pallas_call(
        body, name=name,
        out_shape=[pltpu.SemaphoreType.DMA((DIRECT_SLOTS * n,)), pltpu.SemaphoreType.DMA((DIRECT_SLOTS * n,))]
        + [hbm_like(p) for p in parts] + [hbm_like(l) for l in lands] + [jax.ShapeDtypeStruct((8, 128), F32)],
        in_specs=[HBM] * (2 * n), out_specs=[SEM] * 2 + [HBM] * (2 * n) + [BS(memory_space=pltpu.VMEM)],
        input_output_aliases={i: 2 + i for i in range(2 * n)},
        compiler_params=pltpu.CompilerParams(has_side_effects=_SIDE_EFFECT))(*[_IN_HBM(p) for p in parts], *[_IN_HBM(l) for l in lands])
    return out[:2], out[2:2 + n], out[2 + n:2 + 2 * n], out[-1]


def exchange_direct_wait(sems, parts, lands, splits, after, name):
    n = len(parts)

    def body(*refs):
        send_sems, recv_sems = refs[2 * n:2 * n + 2]
        for cp in _direct_copies(refs[:n], refs[n:2 * n], splits, send_sems, recv_sems):
            cp.wait_send()
            cp.wait_recv()

    hbm_like = lambda a: pltpu.HBM(a.shape, a.dtype)
    out = pl.pallas_call(
        body, name=name, out_shape=[hbm_like(p) for p in parts] + [hbm_like(l) for l in lands],
        in_specs=[HBM] * (2 * n) + [SEM] * 2 + [BS(memory_space=pl.ANY)], out_specs=[HBM] * (2 * n),
        input_output_aliases={i: i for i in range(2 * n)},
        compiler_params=pltpu.CompilerParams(has_side_effects=_SIDE_EFFECT))(*parts, *lands, *sems, after)
    return out[:n], out[n:]


def add_direct(parts, arrived, splits, chip_core, name):
    n = len(parts)

    def body(s_ref, *refs):
        for a_ref, p_ref, o_ref in zip(refs[:n], refs[n:2 * n], refs[2 * n:]):
            s = a_ref[...].astype(F32) + p_ref[DIRECT_SLOTS - 1].astype(F32)
            for j in range(DIRECT_SLOTS - 1):
                s = s + p_ref[j].astype(F32)
            o_ref[...] = s

    def mine(i):
        blk = (None,) + _half_block(parts[i].shape[1:], splits[i])
        if splits[i][0] == 0:
            return BS(blk, lambda g, s: (s[0], s[1], 0))
        return BS(blk, lambda g, s: (s[0], 0, s[1]))

    arrived_specs = [BS(p.shape, lambda g, s: (0, 0, 0)) for p in arrived]
    out_specs = [BS(p.shape[1:], lambda g, s: (0, 0)) for p in arrived]
    return pl.pallas_call(
        body, name=name,
        grid_spec=pltpu.PrefetchScalarGridSpec(num_scalar_prefetch=1, grid=(1,),
                                               in_specs=[mine(i) for i in range(n)] + arrived_specs, out_specs=out_specs),
        out_shape=[jax.ShapeDtypeStruct(p.shape[1:], F32) for p in arrived], compiler_params=_arb(1))(chip_core, *parts, *arrived)


def _half_block(shape2, split):
    axis, size = split
    return (size, shape2[1]) if axis == 0 else (shape2[0], size)


def add_pairs(parts, halves, splits, core, name):
    n = len(parts)

    def body(s_ref, *refs):
        for a_ref, b_ref, o_ref in zip(refs[:n], refs[n:2 * n], refs[2 * n:]):
            o_ref[...] = (a_ref[...].astype(F32) + b_ref[...].astype(F32)).astype(BF16)

    def mine(i):
        blk = (None,) + _half_block(parts[i].shape[1:], splits[i])
        if splits[i][0] == 0:
            return BS(blk, lambda q, s: (q, s[0], 0))
        return BS(blk, lambda q, s: (q, 0, s[0]))

    half_specs = [BS((None,) + h.shape[1:], lambda q, s: (q, 0, 0)) for h in halves]
    return pl.pallas_call(
        body, name=name,
        grid_spec=pltpu.PrefetchScalarGridSpec(num_scalar_prefetch=1, grid=(N_CHIPS,),
                                               in_specs=[mine(i) for i in range(n)] + half_specs, out_specs=half_specs),
        out_shape=[jax.ShapeDtypeStruct(h.shape, BF16) for h in halves], compiler_params=_arb(1))(core, *parts, *halves)


def add_fives(parts, halves, from_chips, splits, chip_core, name):
    n = len(parts)

    def body(s_ref, *refs):
        for a_ref, b_ref, p_ref, o_ref in zip(refs[:n], refs[n:2 * n], refs[2 * n:3 * n], refs[3 * n:]):
            s = a_ref[...].astype(F32) + b_ref[...].astype(F32)
            for j in range(3):
                s = s + p_ref[j].astype(F32)
            o_ref[...] = s

    def mine(i):
        blk = (None,) + _half_block(parts[i].shape[1:], splits[i])
        if splits[i][0] == 0:
            return BS(blk, lambda g, s: (s[0], s[1], 0))
        return BS(blk, lambda g, s: (s[0], 0, s[1]))

    half_specs = [BS((None,) + h.shape[1:], lambda g, s: (s[0], 0, 0)) for h in halves]
    chip_specs = [BS(p.shape, lambda g, s: (0, 0, 0)) for p in from_chips]
    out_specs = [BS(h.shape[1:], lambda g, s: (0, 0)) for h in halves]
    return pl.pallas_call(
        body, name=name,
        grid_spec=pltpu.PrefetchScalarGridSpec(num_scalar_prefetch=1, grid=(1,),
                                               in_specs=[mine(i) for i in range(n)] + half_specs + chip_specs, out_specs=out_specs),
        out_shape=[jax.ShapeDtypeStruct(h.shape[1:], F32) for h in halves], compiler_params=_arb(1))(chip_core, *parts, *halves, *from_chips)


def sum_leading(a, name):
    def body(a_ref, o_ref):
        s = a_ref[0]
        for j in range(1, a.shape[0]):
            s = s + a_ref[j]
        o_ref[...] = s

    return pl.pallas_call(body, name=name, out_shape=jax.ShapeDtypeStruct(a.shape[1:], a.dtype))(a)


def _adamw_math(w, g, m, v):
    mn = ADAM_B1 * m + (1.0 - ADAM_B1) * g
    vn = ADAM_B2 * v + (1.0 - ADAM_B2) * (g * g)
    m_hat = mn / (1.0 - ADAM_B1 ** ADAM_STEP)
    v_hat = vn / (1.0 - ADAM_B2 ** ADAM_STEP)
    return -ADAM_LR * (m_hat / (jnp.sqrt(v_hat) + ADAM_EPS) + ADAM_WD * w), mn, vn


def adamw(w, g, m, v, name):
    R, C = g.shape
    lead = (None,) * (w.ndim - 2)

    def body(w_ref, g_ref, m_ref, v_ref, d_ref, mo_ref, vo_ref):
        d_ref[...], mo_ref[...], vo_ref[...] = _adamw_math(w_ref[...], g_ref[...], m_ref[...], v_ref[...])

    wblk = BS(lead + (R, C), lambda i: (0,) * w.ndim)
    gblk = BS((R, C), lambda i: (0, 0))
    return pl.pallas_call(
        body, name=name, grid=(1,), in_specs=[wblk, gblk, wblk, wblk], out_specs=[wblk] * 3,
        out_shape=[jax.ShapeDtypeStruct(w.shape, F32)] * 3, compiler_params=_arb(1))(w, g, m, v)


SMALL_ROWS = 16
CONV_ROW0 = 8
LOSS_ROW = 14


def pack_small(small_grads, g_ab, g_conv, sq):
    present = [a for a in small_grads if a is not None]

    def body(*refs):
        ab_ref, conv_ref, sq_ref, o_ref = refs[len(present):]
        o_ref[...] = jnp.zeros_like(o_ref)
        it = iter(refs[:len(present)])
        for i, a in enumerate(small_grads):
            if a is not None:
                o_ref[i:i + 1, 0:a.shape[1]] = next(it)[...]
        o_ref[3:4, 0:128] = ab_ref[0:1, :]
        o_ref[4:5, 0:128] = ab_ref[1:2, :]
        half = 512
        for k in range(GDN_CONV * GDN_QKV // half):
            src_r, src_c = (k * half) // GDN_QKV, (k * half) % GDN_QKV
            dst_r, dst_c = CONV_ROW0 + (k * half) // 1024, (k * half) % 1024
            o_ref[dst_r:dst_r + 1, dst_c:dst_c + half] = conv_ref[src_r:src_r + 1, src_c:src_c + half]
        o_ref[LOSS_ROW:LOSS_ROW + 1, :] = sq_ref[...]

    return pl.pallas_call(body, name="pack_small", out_shape=jax.ShapeDtypeStruct((SMALL_ROWS, 1024), F32))(
        *present, g_ab, g_conv, sq)


def adamw_small(block, ws, ms, vs):
    k = len(ws)

    def body(b_ref, *refs):
        outs = refs[3 * k:]
        for i in range(k):
            n = ws[i].shape[1]
            g = b_ref[i:i + 1, 0:n]
            d, mn, vn = _adamw_math(refs[i][...], g, refs[k + i][...], refs[2 * k + i][...])
            outs[4 * i][...], outs[4 * i + 1][...], outs[4 * i + 2][...], outs[4 * i + 3][...] = g, d, mn, vn

    out = pl.pallas_call(
        body, name="adamw_small",
        out_shape=[jax.ShapeDtypeStruct(w.shape, F32) for w in ws for _ in range(4)])(block, *ws, *ms, *vs)
    return [out[4 * i:4 * i + 4] for i in range(k)]


def adamw_halves(w, mine, other, m, v, split, core, name):
    R, C = w.shape[-2:]
    axis, size = split
    lead = (None,) * (w.ndim - 2)
    zeros = (0,) * (w.ndim - 2)
    if axis == 0:
        tr = size if size <= 256 else next(t for t in range(256, 7, -1) if size % t == 0 and t % 8 == 0)
        nb = size // tr
        whole = BS(lead + (tr, C), lambda hi, j, s: zeros + (hi * nb + j, 0))
        part = BS((tr, C), lambda hi, j, s: (j, 0))
    else:
        nb = size // 128
        whole = BS(lead + (R, 128), lambda hi, j, s: zeros + (0, hi * nb + j))
        part = BS((R, 128), lambda hi, j, s: (0, j))

    def body(s_ref, w_ref, a_ref, b_ref, m_ref, v_ref, g_ref, d_ref, mo_ref, vo_ref):
        g = jnp.where(pl.program_id(0) == s_ref[0], a_ref[...], b_ref[...])
        g_ref[...] = g
        d_ref[...], mo_ref[...], vo_ref[...] = _adamw_math(w_ref[...], g, m_ref[...], v_ref[...])

    return pl.pallas_call(
        body, name=name,
        grid_spec=pltpu.PrefetchScalarGridSpec(num_scalar_prefetch=1, grid=(2, nb),
                                               in_specs=[whole, part, part, whole, whole], out_specs=[whole] * 4),
        out_shape=[jax.ShapeDtypeStruct(w.shape, F32)] * 4, compiler_params=_arb(2))(core, w, mine, other, m, v)


def dense_bf16(w3, name):
    R, _, K = w3.shape
    kh = K // 2

    def body(w_hbm, o_ref, buf, sem):
        cp = pltpu.make_async_copy(w_hbm.at[:, 0], buf, sem)
        cp.start()
        cp.wait()
        o_ref[0] = buf[:, :kh].astype(BF16)
        o_ref[1] = buf[:, kh:].astype(BF16)

    return pl.pallas_call(
        body, name=name, in_specs=[ANY], out_specs=BS(memory_space=pltpu.VMEM), out_shape=jax.ShapeDtypeStruct((2, R, kh), BF16),
        scratch_shapes=[pltpu.VMEM((R, K), F32), pltpu.SemaphoreType.DMA(())])(w3)


ROW_BLOCK = 184


def adamw_untiled_rows(w3, mine, other, m3, v3, name):
    R, _, K = w3.shape
    kh = K // 2
    starts = list(range(0, R, ROW_BLOCK))
    sizes = [min(ROW_BLOCK, R - s) for s in starts]
    nblk = len(starts)

    def body(w_hbm, a_ref, b_ref, m_hbm, v_hbm, g_hbm, d_hbm, mo_hbm, vo_hbm,
             wbuf, mbuf, vbuf, gbuf, dbuf, mobuf, vobuf, in_sems, out_sems):
        first = lax.axis_index("c") == 0
        ins = []
        for k, (r0, n) in enumerate(zip(starts, sizes)):
            rows = pl.ds(r0, n)
            cps = [pltpu.make_async_copy(src.at[rows, 0], dst.at[rows], in_sems.at[3 * k + i])
                   for i, (src, dst) in enumerate(((w_hbm, wbuf), (m_hbm, mbuf), (v_hbm, vbuf)))]
            for cp in cps:
                cp.start()
            ins.append(cps)

        def update(rows):
            a, b = a_ref[rows, :], b_ref[rows, :]
            g = jnp.concatenate([jnp.where(first, a, b), jnp.where(first, b, a)], axis=1)
            gbuf[rows, :] = g
            dbuf[rows, :], mobuf[rows, :], vobuf[rows, :] = _adamw_math(wbuf[rows, :], g, mbuf[rows, :], vbuf[rows, :])

        outs = []
        for k, (r0, n) in enumerate(zip(starts, sizes)):
            for cp in ins[k]:
                cp.wait()
            groups, tail = n // 8, n % 8

            def group(i, carry, r0=r0):
                update(pl.ds(pl.multiple_of(r0 + i * 8, 8), 8))
                return carry

            lax.fori_loop(0, groups, group, 0)
            if tail:
                update(pl.ds(r0 + groups * 8, tail))
            rows = pl.ds(r0, n)
            cps = [pltpu.make_async_copy(src.at[rows], dst.at[rows, 0], out_sems.at[4 * k + i])
                   for i, (src, dst) in enumerate(((gbuf, g_hbm), (dbuf, d_hbm), (mobuf, mo_hbm), (vobuf, vo_hbm)))]
            for cp in cps:
                cp.start()
            outs += cps
        for cp in outs:
            cp.wait()

    vmem = BS(memory_space=pltpu.VMEM)
    return pl.pallas_call(
        body, name=name, in_specs=[ANY, vmem, vmem, ANY, ANY], out_specs=[ANY] * 4,
        out_shape=[jax.ShapeDtypeStruct(w3.shape, F32)] * 4,
        scratch_shapes=[pltpu.VMEM((R, K), F32)] * 7 + [pltpu.SemaphoreType.DMA((3 * nblk,)), pltpu.SemaphoreType.DMA((4 * nblk,))])(
            w3, mine, other, m3, v3)


def adamw_w_q_b(w, mine, other, m, v, name):
    def body(w_ref, a_ref, b_ref, m_ref, v_ref, g_ref, d_ref, mo_ref, vo_ref):
        first = lax.axis_index("c") == 0
        lo = jnp.where(first, a_ref[...], b_ref[...])
        hi = jnp.where(first, b_ref[...], a_ref[...])
        g = jnp.concatenate([lo, hi[0:32], hi[64:96]], axis=0)
        g_ref[...] = g
        d_ref[...], mo_ref[...], vo_ref[...] = _adamw_math(w_ref[...], g, m_ref[...], v_ref[...])

    return pl.pallas_call(body, name=name, out_shape=[jax.ShapeDtypeStruct(w.shape, F32)] * 4)(w, mine, other, m, v)


def local_step(x, mem, positions, tgt, norm_in, weights, big_grads_ready, q_a_norm, kv_a_norm, gdn_conv, gdn_a_log,
               gdn_dt_bias, gdn_norm, mem_norm, norm_final):
    B, S, D = x.shape
    M = mem.shape[1]
    T = B * S
    N = S // CHUNK
    x2d = x.reshape(T, D)
    mem2d = mem.reshape(B * M, D)
    tgt2d = tgt.reshape(T, D)

    alog_row, dt_row = _lane_row(gdn_a_log), _lane_row(gdn_dt_bias)

    half = MLA_ROPE // 2
    inv_freq = 1.0 / (ROPE_THETA ** (jnp.arange(half, dtype=F32) / half))
    z32 = jnp.zeros((half,), F32)
    o32 = jnp.ones((half,), F32)
    inv_row = jnp.concatenate([inv_freq, z32, inv_freq, z32]).reshape(1, 128)
    sgn_row = jnp.concatenate([-o32, z32, o32, z32]).reshape(1, 128)
    msk_row = jnp.concatenate([o32, z32, o32, z32]).reshape(1, 128)
    cos_t, sin_t = rope_tables(positions.reshape(T, 1), inv_row, sgn_row, msk_row, after=weights[3])

    h = rms_fwd(x2d, norm_in, "rms_in", after=weights[3])
    wp_early = weights[0]((h, cos_t))
    P_early = mm(h, wp_early, "nt", F32, "in_proj_early", bm=1024, bn=OFF_GDN, n_outer=True, col_tiles=(0, 1))
    wq, wkv = weights[1](P_early)
    Q, K, V, qn, kvn = mla_prep(P_early, q_a_norm, kv_a_norm, wq, wkv, cos_t, sin_t)
    o_mla, lse = mla_attn_fwd(Q, K, V, B, S)
    wp = weights[4]((o_mla, P_early), wp_early)
    P = mm(h, wp, "nt", F32, "in_proj", bm=1024, bn=OFF_GDN, n_outer=True, col_tiles=(1, N_PAD // OFF_GDN), into=P_early)
    qkv = gdn_prep_fwd(P, gdn_conv, B, S)
    GB = gdn_gate_fwd(P, alog_row, dt_row, B, S)
    Grow = jnp.transpose(GB[:, :N_HEADS].reshape(B, N, CHUNK, N_HEADS), (0, 3, 1, 2))
    U, W, Tinv, A = gdn_chunk_fwd(qkv, GB, Grow, B, S)
    qkv3, GB3 = qkv.reshape(B, S, GDN_QKV), GB.reshape(B, S, 128)
    W3 = W.reshape(B, S, 512)
    o_gdn3, Vn3, St = gdn_scan_fwd(qkv3, U.reshape(B, S, 512), W3, GB3, A, B, S)
    o_gdn = o_gdn3.reshape(T, 512)
    w_mem_kv, w_out = weights[2](o_gdn)
    memn = rms_fwd(mem2d, mem_norm, "rms_mem")
    MKV = mm(memn, w_mem_kv, "nn", BF16, "mem_kv_proj")
    o_mem = mem_attn_fwd(P, MKV, B, S, M)
    mixed, dx2, dx2b, sq, g_norm_final = merge_fwd(o_mla, o_gdn, o_mem, P, x2d, tgt2d, w_out, gdn_norm, norm_final.reshape(1, D))

    g_w_out = mm(mixed, dx2b, "tn", BF16, "grad_w_out")
    dgate, do_mla, do_gdn, do_mem, g_gdn_norm = merge_bwd(dx2b, o_mla, o_gdn, o_mem, P, w_out, gdn_norm)

    dmemq, dMKV = mem_attn_bwd(P, MKV, do_mem, B, S, M)
    g_w_mem_kv = mm(memn, dMKV, "tn", BF16, "grad_w_mem_kv")
    started_early = big_grads_ready(dict(w_mem_kv=g_w_mem_kv, w_out=g_w_out), "early")
    dmemn = mm(dMKV, w_mem_kv, "nt", F32, "d_memn", after=(started_early,))
    g_mem_norm = gain_grad(mem2d, dmemn, "grad_mem_norm")

    dU3, dW3, dQ13, dK13, dA, dG13 = gdn_scan_bwd(do_gdn.reshape(B, S, 512), qkv3, W3, Vn3, GB3, A, St, B, S)
    r2 = lambda a: a.reshape(T, a.shape[-1])
    dqkv, dGB = gdn_chunk_bwd(qkv, GB, Grow, Tinv, dA, r2(dU3), r2(dW3), r2(dQ13), r2(dK13), r2(dG13), B, S)
    dPg, g_conv = gdn_prep_bwd(P, dqkv, gdn_conv, B, S)
    dab, g_ab = gdn_gate_bwd(P, dGB, alog_row, dt_row, B, S)

    dQ, dK, dV = mla_attn_bwd(Q, K, V, o_mla, do_mla, lse, B, S)
    dq_lin, dkv_lin, dPm, g_q_a_norm, g_kv_a_norm = mla_proj_bwd(dQ, dK, dV, cos_t, sin_t, P, dab, wq, wkv, q_a_norm, kv_a_norm)
    g_wq = mm(dq_lin, qn, "tn", BF16, "grad_w_q_b")
    g_wkv = mm(kvn, dkv_lin, "tn", BF16, "grad_w_kv_b")

    dP = [dPm, dmemq, dPg, dgate]
    g_wp = mm_cols_tn(dP, h, BF16, "grad_w_in")
    started = big_grads_ready(dict(w_in=g_wp, w_q_b=g_wq, w_kv_b=g_wkv), "late")
    grad_x, g_norm_in = in_proj_bwd(dP, wp, x2d, dx2, norm_in, started)

    grads = dict(
        norm_in=g_norm_in, q_a_norm=g_q_a_norm, kv_a_norm=g_kv_a_norm, gdn_conv=g_conv,
        gdn_a_log_dt_bias=g_ab, gdn_norm=g_gdn_norm,
        mem_norm=g_mem_norm, norm_final=g_norm_final)
    return sq, grad_x.reshape(B, S, D), grads


def kernel(x, mem, positions, norm_in, w_in, q_a_norm, w_q_b, kv_a_norm, w_kv_b, gdn_conv, gdn_a_log, gdn_dt_bias, gdn_norm, mem_norm, w_mem_kv, w_out, norm_final, loss_target, m_norm_in, m_w_in, m_q_a_norm, m_w_q_b, m_kv_a_norm, m_w_kv_b, m_gdn_conv, m_gdn_a_log, m_gdn_dt_bias, m_gdn_norm, m_mem_norm, m_w_mem_kv, m_w_out, m_norm_final, v_norm_in, v_w_in, v_q_a_norm, v_w_q_b, v_kv_a_norm, v_w_kv_b, v_gdn_conv, v_gdn_a_log, v_gdn_dt_bias, v_gdn_norm, v_mem_norm, v_w_mem_kv, v_w_out, v_norm_final):
    B = x.shape[0]
    cx, cy, cc = lax.axis_index("x"), lax.axis_index("y"), lax.axis_index("c")
    chip = 2 * cx + cy

    big_names = ("w_in", "w_q_b", "w_kv_b", "w_mem_kv", "w_out")
    rows_major = lambda a: jnp.transpose(a, (2, 0, 1))
    w_in3, m_in3, v_in3 = rows_major(w_in), rows_major(m_w_in), rows_major(v_w_in)
    w_qb_t, m_qb_t, v_qb_t = jnp.transpose(w_q_b[0]), jnp.transpose(m_w_q_b[0]), jnp.transpose(v_w_q_b[0])
    z32 = jnp.zeros((32, Q_LORA), BF16)
    qb_bf = w_qb_t.astype(BF16)
    qb_padded = jnp.concatenate([qb_bf[:160], z32, qb_bf[160:], z32])
    shards = [dense_bf16(w_in3, "w_in_bf16"), qb_padded, w_kv_b[0].astype(BF16), w_mem_kv[0].astype(BF16), w_out[0].astype(BF16)]
    splits = [(1, D_MODEL // 2)] + [(0, s.shape[0] // 2) for s in shards[1:]]
    early_row0 = jnp.where(chip == 2, EARLY_ROW0[1], EARLY_ROW0[0])
    *early_flight, early_started = half_gather_start(lax.dynamic_slice_in_dim(shards[0], early_row0, EARLY_ROWS, axis=1),
                                                     "w_in_early_gather_start", early=True)
    *late_a, started_a = late_gather_start(shards[1:3], early_started, "late_gather_qkv_start")
    *w_in_flight, w_in_started = half_gather_start(shards[0], "w_in_gather_start", after=(started_a,))
    *late_b, started_b = late_gather_start(shards[3:], w_in_started, "late_gather_mem_out_start")
    conv_all = allgather_devices(gdn_conv[0], "allgather_conv")
    conv_cols = gdn_conv.shape[2]
    conv_full = jnp.transpose(conv_all[0::2], (1, 0, 2)).reshape(GDN_CONV, N_CHIPS * conv_cols)
    late_shapes = [(N_CHIPS,) + s.shape for s in shards[1:]]

    def w_in_early_ready(after):
        g_early = pass_halves_to_sibling(half_gather_wait(*early_flight, after, "w_in_early_gather_wait", early=True),
                                         "w_in_early_gather_sibling", early=True)
        return pad_w_in_t_early(g_early)

    def w_in_ready(after, wp):
        g_in = pass_halves_to_sibling(half_gather_wait(*w_in_flight, after, "w_in_gather_wait"), "w_in_gather_sibling")
        return pad_w_in_t_rest(g_in, wp)

    def late_qkv(after):
        g_qb, g_kvb = late_gather_wait(*late_a, after, "late_gather_qkv_wait")
        return g_qb.reshape(-1, Q_LORA), _perm_w_kv_b(g_kvb)

    def late_mem_out(after):
        g_mem, g_out_w = late_gather_wait(*late_b, after, "late_gather_mem_out_wait")
        return g_mem.reshape(-1, g_mem.shape[2]), g_out_w.reshape(-1, g_out_w.shape[2])

    weights = (w_in_early_ready, late_qkv, late_mem_out, (started_b,), w_in_ready)

    core = jnp.stack([cc]).astype(jnp.int32)
    chip_core = jnp.stack([chip, cc]).astype(jnp.int32)
    exchanges = {}
    by_chip = dict(w_in=unpad_w_in_t, w_q_b=lambda a: a.reshape(late_shapes[0]), w_kv_b=_unperm_w_kv_b,
                   w_mem_kv=lambda a: a.reshape(late_shapes[2]), w_out=lambda a: a.reshape(late_shapes[3]))

    def big_grads_ready(gb, group):
        idx = [big_names.index(n) for n in gb]
        parts = [by_chip[n](a) for n, a in gb.items()]
        sp = [splits[i] for i in idx]
        if group == "early":
            sems, parts_thru, lands, token = exchange_direct_start(parts, sp, "rs_exchange_start_" + group)
            exchanges[group] = dict(idx=idx, direct=True, sems=sems, parts=parts_thru, lands=lands)
            return token
        from_sibling = swap_sibling(parts, "rs_sibling_partial_" + group, sp)
        chip_sums = add_pairs(parts, from_sibling, sp, core, "rs_add_sibling_" + group)
        sems, sums_thru, lands, token = exchange_chips_start(chip_sums, "rs_exchange_start_" + group)
        exchanges[group] = dict(idx=idx, parts=parts, from_sibling=from_sibling, sems=sems, sums=sums_thru, lands=lands)
        return token

    sq, grad_x, g = local_step(x, mem, positions, loss_target, norm_in, weights, big_grads_ready, q_a_norm, kv_a_norm, conv_full,
                               gdn_a_log, gdn_dt_bias, gdn_norm, mem_norm, norm_final)

    small_names = ("norm_in", "q_a_norm", "kv_a_norm", "gdn_a_log", "gdn_dt_bias", "gdn_norm", "mem_norm", "norm_final")
    small = dict(norm_in=norm_in, q_a_norm=q_a_norm, kv_a_norm=kv_a_norm, gdn_a_log=gdn_a_log, gdn_dt_bias=gdn_dt_bias,
                 gdn_norm=gdn_norm, mem_norm=mem_norm, norm_final=norm_final)
    m_small = dict(norm_in=m_norm_in, q_a_norm=m_q_a_norm, kv_a_norm=m_kv_a_norm, gdn_a_log=m_gdn_a_log,
                   gdn_dt_bias=m_gdn_dt_bias, gdn_norm=m_gdn_norm, mem_norm=m_mem_norm, norm_final=m_norm_final)
    v_small = dict(norm_in=v_norm_in, q_a_norm=v_q_a_norm, kv_a_norm=v_kv_a_norm, gdn_a_log=v_gdn_a_log,
                   gdn_dt_bias=v_gdn_dt_bias, gdn_norm=v_gdn_norm, mem_norm=v_mem_norm, norm_final=v_norm_final)
    conv_rows = GDN_CONV * GDN_QKV // 1024
    g_block = pack_small([g.get(n) for n in small_names], g["gdn_a_log_dt_bias"], g["gdn_conv"], sq)
    g_block = sum_leading(allgather_devices(g_block, "allgather_small_grads"), "sum_small_grads")
    loss = 0.5 * jnp.sum(g_block[LOSS_ROW]) / D_MODEL
    g_conv = lax.dynamic_slice_in_dim(g_block[CONV_ROW0:CONV_ROW0 + conv_rows].reshape(GDN_CONV, GDN_QKV), chip * conv_cols,
                                      conv_cols, axis=1)
    as_row = lambda a: a.reshape(1, -1)
    updated = adamw_small(g_block, [as_row(small[n]) for n in small_names], [as_row(m_small[n]) for n in small_names],
                          [as_row(v_small[n]) for n in small_names])
    g_out, d_out, m_out, v_out = ({n: u[i].reshape(small[n].shape) for n, u in zip(small_names, updated)} for i in range(4))
    d_s = d_out["norm_in"]

    my_half = [None] * len(big_names)
    for group, e in exchanges.items():
        sp = [splits[i] for i in e["idx"]]
        if e.get("direct"):
            parts, arrived = exchange_direct_wait(e["sems"], e["parts"], e["lands"], sp, d_s, "rs_exchange_wait_" + group)
            halves = add_direct(parts, arrived, sp, chip_core, "rs_add_chips_" + group)
        else:
            from_chips = exchange_chips_wait(e["sems"], e["sums"], e["lands"], d_s, "rs_exchange_wait_" + group)
            halves = add_fives(e["parts"], e["from_sibling"], from_chips, sp, chip_core, "rs_add_chips_" + group)
        for i, a in zip(e["idx"], halves):
            my_half[i] = a
    other_half = swap_sibling(my_half, "rs_sibling_final")

    d_out["gdn_conv"], m_out["gdn_conv"], v_out["gdn_conv"] = adamw(gdn_conv, g_conv, m_gdn_conv, v_gdn_conv, "adamw_gdn_conv")
    g_out["gdn_conv"] = g_conv[None]
    res = adamw_untiled_rows(w_in3, my_half[0], other_half[0], m_in3, v_in3, "adamw_w_in")
    g_out["w_in"], d_out["w_in"], m_out["w_in"], v_out["w_in"] = [jnp.transpose(r, (1, 2, 0)) for r in res]
    res = adamw_w_q_b(w_qb_t, my_half[1], other_half[1], m_qb_t, v_qb_t, "adamw_w_q_b")
    g_out["w_q_b"], d_out["w_q_b"], m_out["w_q_b"], v_out["w_q_b"] = [jnp.transpose(r)[None] for r in res]
    rest = dict(w_kv_b=(w_kv_b, m_w_kv_b, v_w_kv_b), w_mem_kv=(w_mem_kv, m_w_mem_kv, v_w_mem_kv), w_out=(w_out, m_w_out, v_w_out))
    for i, n in enumerate(big_names):
        if n in rest:
            w_n, m_n, v_n = rest[n]
            g_out[n], d_out[n], m_out[n], v_out[n] = adamw_halves(w_n, my_half[i], other_half[i], m_n, v_n, splits[i], core, "adamw_" + n)

    order = ("norm_in", "w_in", "q_a_norm", "w_q_b", "kv_a_norm", "w_kv_b", "gdn_conv", "gdn_a_log", "gdn_dt_bias",
             "gdn_norm", "mem_norm", "w_mem_kv", "w_out", "norm_final")
    return (loss, grad_x, *[g_out[n] for n in order], *[d_out[n] for n in order], *[m_out[n] for n in order],
            *[v_out[n] for n in order])
```

```python
import jax
import jax.numpy as jnp
from jax import lax
from jax.experimental import pallas as pl
from jax.experimental.pallas import tpu as pltpu

F32 = jnp.float32
BF16 = jnp.bfloat16
BS = pl.BlockSpec

D_MODEL = 1024
N_HEADS = 4
MLA_NOPE, MLA_ROPE, MLA_V = 128, 64, 128
Q_LORA, KV_LORA = 384, 256
ROPE_THETA = 10000.0
GDN_DK = GDN_DV = 128
GDN_CONV = 4
CHUNK = 64
MEM_DH = 128
D_MIX = 1536
GDN_QKV = 1536
D_IN = 4296
EPS = 1e-6
ADAM_LR, ADAM_B1, ADAM_B2, ADAM_EPS, ADAM_WD, ADAM_STEP = 0.001, 0.9, 0.999, 1e-08, 0.01, 10

OFF_MLA = 0
OFF_MEMQ = 1024
OFF_GDN = 1536
OFF_GATE = 3072
N_PAD = 4608
HEAD_PAD = 256
MLA_SCALE = (MLA_NOPE + MLA_ROPE) ** -0.5
MEM_SCALE = MEM_DH ** -0.5
GDN_SCALE = GDN_DK ** -0.5
NEG = -1e30

NN = ((1,), (0,))
NT = ((1,), (1,))
TN = ((0,), (0,))


def _dot(a, b, dims):
    return lax.dot_general(a, b, (dims, ((), ())), preferred_element_type=F32)


def _bdot(spec, a, b, precision=None):
    return jnp.einsum(spec, a, b, preferred_element_type=F32, precision=precision)


def _arb(n):
    return pltpu.CompilerParams(dimension_semantics=("arbitrary",) * n)


def _sigmoid(x):
    return 1.0 / (1.0 + jnp.exp(-x))


def _softplus(z):
    return jnp.maximum(z, 0.0) + jnp.log(1.0 + jnp.exp(-jnp.abs(z)))


def _rope(t, cos_row, sin_row):
    return t * cos_row + pltpu.roll(t, 64, 1) * sin_row


def _rope_bwd(d, cos_row, sin_row):
    return d * cos_row + pltpu.roll(d * sin_row, 64, 1)


def rms_fwd(x, gain, name, tm=512, after=()):
    T, n = x.shape
    tm = min(tm, T)

    def body(x_ref, g_ref, *rest):
        xv = x_ref[...]
        r = lax.rsqrt(jnp.mean(xv * xv, axis=-1, keepdims=True) + EPS)
        rest[-1][...] = (xv * r * g_ref[...]).astype(BF16)

    return pl.pallas_call(
        body, name=name, grid=(T // tm,),
        in_specs=[BS((tm, n), lambda i: (i, 0)), BS((1, n), lambda i: (0, 0))] + [BS(memory_space=pl.ANY)] * len(after),
        out_specs=BS((tm, n), lambda i: (i, 0)),
        out_shape=jax.ShapeDtypeStruct((T, n), BF16), compiler_params=_arb(1))(x, gain, *after)


def mm(a, b, kind, out_dtype, name, bm=512, bn=None, n_outer=False, after=(), col_tiles=None, into=None):
    if kind == "nn":
        (M, K), (_, N) = a.shape, b.shape
    elif kind == "nt":
        (M, K), (N, _) = a.shape, b.shape
    else:
        (K, M), (_, N) = a.shape, b.shape
    bm, bn = min(bm, M), min(bn or N, N)
    assert M % bm == 0 and N % bn == 0, (name, M, N, K)
    lo, hi = col_tiles or (0, N // bn)
    ij = (lambda g0, g1: (g1, g0 + lo)) if n_outer else (lambda g0, g1: (g0, g1 + lo))
    a_spec = BS((K, bm), lambda g0, g1: (0, ij(g0, g1)[0])) if kind == "tn" else BS((bm, K), lambda g0, g1: (ij(g0, g1)[0], 0))
    once = dict(pipeline_mode=pl.Buffered(1)) if hi - lo == 1 else {}
    b_spec = (BS((bn, K), lambda g0, g1: (ij(g0, g1)[1], 0), **once) if kind == "nt"
              else BS((K, bn), lambda g0, g1: (0, ij(g0, g1)[1]), **once))
    dims = {"nn": NN, "nt": NT, "tn": TN}[kind]
    extra = tuple(after) + (() if into is None else (into,))

    def body(a_ref, b_ref, *rest):
        rest[-1][...] = _dot(a_ref[...].astype(BF16), b_ref[...].astype(BF16), dims).astype(out_dtype)

    grid = (hi - lo, M // bm) if n_outer else (M // bm, hi - lo)
    return pl.pallas_call(
        body, name=name, grid=grid, in_specs=[a_spec, b_spec] + [BS(memory_space=pl.ANY)] * len(extra),
        out_specs=BS((bm, bn), lambda g0, g1: ij(g0, g1)),
        input_output_aliases={} if into is None else {1 + len(extra): 0},
        out_shape=jax.ShapeDtypeStruct((M, N), out_dtype), compiler_params=_arb(2))(a, b, *extra)


def mm_cols_tn(pieces, b, out_dtype, name, bm=512):
    K, N = b.shape
    tiles = [p.shape[1] // bm for p in pieces]
    firsts = [sum(tiles[:i]) for i in range(len(tiles))]

    def body(*refs):
        b_ref, o_ref = refs[-2], refs[-1]
        i = pl.program_id(0)
        for a_ref, t0, n in zip(refs[:-2], firsts, tiles):
            @pl.when((i >= t0) & (i < t0 + n))
            def _(a_ref=a_ref):
                o_ref[...] = _dot(a_ref[...], b_ref[...], TN).astype(out_dtype)

    a_specs = [BS((K, bm), lambda i, t0=t0, n=n: (0, jnp.clip(i - t0, 0, n - 1))) for t0, n in zip(firsts, tiles)]
    return pl.pallas_call(
        body, name=name, grid=(sum(tiles),),
        in_specs=a_specs + [BS(b.shape, lambda i: (0, 0), pipeline_mode=pl.Buffered(1))],
        out_specs=BS((bm, N), lambda i: (i, 0)), out_shape=jax.ShapeDtypeStruct((sum(tiles) * bm, N), out_dtype),
        compiler_params=_arb(1))(*pieces, b)


def rope_tables(pos_col, inv_row, sgn_row, msk_row, tm=512, after=()):
    T = pos_col.shape[0]
    tm = min(tm, T)

    def body(p_ref, inv_ref, sgn_ref, msk_ref, *rest):
        c_ref, s_ref = rest[-2:]
        ang = p_ref[...].astype(F32) * inv_ref[...]
        c_ref[...] = jnp.cos(ang) * msk_ref[...]
        s_ref[...] = jnp.sin(ang) * sgn_ref[...]

    row = BS((1, 128), lambda i: (0, 0))
    return pl.pallas_call(
        body, name="rope_tables", grid=(T // tm,),
        in_specs=[BS((tm, 1), lambda i: (i, 0)), row, row, row] + [BS(memory_space=pl.ANY)] * len(after),
        out_specs=[BS((tm, 128), lambda i: (i, 0))] * 2,
        out_shape=[jax.ShapeDtypeStruct((T, 128), F32)] * 2, compiler_params=_arb(1))(pos_col, inv_row, sgn_row, msk_row, *after)


def mla_prep(P, gq, gkv, wq, wkv, cos_t, sin_t, tm=512):
    T = P.shape[0]
    tm = min(tm, T)

    def body(p_ref, gq_ref, gkv_ref, wq_ref, wkv_ref, c_ref, s_ref, q_ref, k_ref, v_ref, qn_ref, kvn_ref):
        p = p_ref[...].astype(F32)
        cq, ckv, kr = p[:, :Q_LORA], p[:, Q_LORA:Q_LORA + KV_LORA], p[:, 640:768]
        qn = (cq * lax.rsqrt(jnp.mean(cq * cq, axis=-1, keepdims=True) + EPS) * gq_ref[...]).astype(BF16)
        kvn = (ckv * lax.rsqrt(jnp.mean(ckv * ckv, axis=-1, keepdims=True) + EPS) * gkv_ref[...]).astype(BF16)
        qn_ref[...] = qn
        kvn_ref[...] = kvn
        q = _dot(qn, wq_ref[...], NT)
        kv = _dot(kvn, wkv_ref[...], NN)
        cos_row, sin_row = c_ref[...], s_ref[...]
        krr = _rope(kr, cos_row, sin_row).astype(BF16)
        for h in range(N_HEADS):
            lo = h * HEAD_PAD
            q_ref[:, lo:lo + 128] = (q[:, lo:lo + 128] * MLA_SCALE).astype(BF16)
            q_ref[:, lo + 128:lo + 256] = (_rope(q[:, lo + 128:lo + 256], cos_row, sin_row) * MLA_SCALE).astype(BF16)
            k_ref[:, lo:lo + 128] = kv[:, h * 128:(h + 1) * 128].astype(BF16)
            k_ref[:, lo + 128:lo + 256] = krr
            v_ref[:, lo:lo + 128] = kv[:, 512 + h * 128:512 + (h + 1) * 128].astype(BF16)
            v_ref[:, lo + 128:lo + 256] = jnp.ones((tm, 128), BF16)

    full = lambda r, c: BS((r, c), lambda i: (0, 0))
    rowb = lambda c: BS((tm, c), lambda i: (i, 0))
    return pl.pallas_call(
        body, name="mla_prep", grid=(T // tm,),
        in_specs=[rowb(1024), full(1, Q_LORA), full(1, KV_LORA), full(1024, Q_LORA), full(KV_LORA, 1024), rowb(128), rowb(128)],
        out_specs=[rowb(1024), rowb(1024), rowb(1024), rowb(Q_LORA), rowb(KV_LORA)],
        out_shape=[jax.ShapeDtypeStruct((T, 1024), BF16), jax.ShapeDtypeStruct((T, 1024), BF16),
                   jax.ShapeDtypeStruct((T, 1024), BF16), jax.ShapeDtypeStruct((T, Q_LORA), BF16),
                   jax.ShapeDtypeStruct((T, KV_LORA), BF16)],
        compiler_params=_arb(1))(P, gq, gkv, wq, wkv, cos_t, sin_t)


ATTN_HEADS_PER_STEP = 2
ATTN_STRIP = 32


def mla_attn_fwd(Q, K, V, B, S, tq=512, hp=ATTN_HEADS_PER_STEP):
    T = B * S
    tq = min(tq, S)
    nq = S // tq

    rs = min(ATTN_STRIP, tq)

    def body(q_ref, k_ref, v_ref, o_ref, lse_ref, m_s, acc_s, s_s, p_s, a_s):
        i = pl.program_id(2)
        m_s[...] = jnp.full_like(m_s, NEG)
        acc_s[...] = jnp.zeros_like(acc_s)

        def blk(j, masked):
            rows = pl.ds(pl.multiple_of(j * tq, tq), tq)
            for h in range(hp):
                hq = slice(h * HEAD_PAD, (h + 1) * HEAD_PAD)
                s_s[h] = _dot(q_ref[:, hq], k_ref[rows, hq], NT)
            for h in range(hp):
                for r0 in range(0, tq, rs):
                    rr = slice(r0, r0 + rs)
                    sv = s_s[h, rr, :]
                    if masked:
                        r = r0 + lax.broadcasted_iota(jnp.int32, (rs, tq), 0)
                        c = lax.broadcasted_iota(jnp.int32, (rs, tq), 1)
                        sv = jnp.where(r >= c, sv, NEG)
                    m_prev = m_s[h, rr, :]
                    m_new = jnp.maximum(m_prev, jnp.max(sv, axis=1, keepdims=True))
                    p_s[h, rr, :] = jnp.exp(sv - m_new).astype(BF16)
                    a_s[h, rr, :] = jnp.exp(m_prev - m_new)
                    m_s[h, rr, :] = m_new
            for h in range(hp):
                hq = slice(h * HEAD_PAD, (h + 1) * HEAD_PAD)
                acc_s[h] = a_s[h] * acc_s[h] + _dot(p_s[h], v_ref[rows, hq], NN)

        def loop(j, c):
            blk(j, False)
            return c

        lax.fori_loop(0, i, loop, 0)
        blk(i, True)
        for h in range(hp):
            den = acc_s[h, :, 128:256]
            o_ref[:, h * 128:(h + 1) * 128] = (acc_s[h, :, 0:128] / den).astype(BF16)
            lse_ref[h] = m_s[h] + jnp.log(den[:, 0:1])

    return pl.pallas_call(
        body, name="mla_attn_fwd", grid=(B, N_HEADS // hp, nq),
        in_specs=[BS((tq, hp * HEAD_PAD), lambda b, h, i: (b * nq + i, h)),
                  BS((S, hp * HEAD_PAD), lambda b, h, i: (b, h)),
                  BS((S, hp * HEAD_PAD), lambda b, h, i: (b, h))],
        out_specs=[BS((tq, hp * 128), lambda b, h, i: (b * nq + i, h)),
                   BS((hp, tq, 1), lambda b, h, i: (h, b * nq + i, 0))],
        out_shape=[jax.ShapeDtypeStruct((T, 512), BF16), jax.ShapeDtypeStruct((N_HEADS, T, 1), F32)],
        scratch_shapes=[pltpu.VMEM((hp, tq, 1), F32), pltpu.VMEM((hp, tq, HEAD_PAD), F32), pltpu.VMEM((hp, tq, tq), F32),
                        pltpu.VMEM((hp, tq, tq), BF16), pltpu.VMEM((hp, tq, 1), F32)],
        compiler_params=_arb(3))(Q, K, V)


def mla_attn_bwd(Q, K, V, O, dO, LSE, B, S, tq=512, hp=ATTN_HEADS_PER_STEP):
    T = B * S
    tq = min(tq, S)
    nq = S // tq

    rs = min(ATTN_STRIP, tq)

    def body(q_ref, k_ref, v_ref, o_ref, do_ref, lse_ref, dq_ref, dk_ref, dv_ref, delta_s, dq_s, dk_s, dv_s, s_s, dp_s, p_s, ds_s):
        j = pl.program_id(2)

        @pl.when(j == 0)
        def _():
            dq_s[...] = jnp.zeros_like(dq_s)
            for h in range(hp):
                sl = slice(h * 128, (h + 1) * 128)
                delta_s[h] = jnp.sum(do_ref[:, sl] * o_ref[:, sl].astype(F32), axis=1, keepdims=True)

        dk_s[...] = jnp.zeros_like(dk_s)
        dv_s[...] = jnp.zeros_like(dv_s)

        def step(i, c):
            rows = pl.ds(pl.multiple_of(i * tq, tq), tq)
            for h in range(hp):
                sq, sv = slice(h * HEAD_PAD, (h + 1) * HEAD_PAD), slice(h * 128, (h + 1) * 128)
                s_s[h] = _dot(q_ref[rows, sq], k_ref[:, sq], NT)
                dp_s[h] = _dot(do_ref[rows, sv].astype(BF16), v_ref[:, h * HEAD_PAD:h * HEAD_PAD + 128], NT)
            for h in range(hp):
                for r0 in range(0, tq, rs):
                    rr = slice(r0, r0 + rs)
                    seq_rows = pl.ds(pl.multiple_of(i * tq + r0, rs), rs)
                    r = i * tq + r0 + lax.broadcasted_iota(jnp.int32, (rs, tq), 0)
                    cc = j * tq + lax.broadcasted_iota(jnp.int32, (rs, tq), 1)
                    p = jnp.where(r >= cc, jnp.exp(s_s[h, rr, :] - lse_ref[h, seq_rows, :]), 0.0)
                    p_s[h, rr, :] = p.astype(BF16)
                    ds_s[h, rr, :] = (p * (dp_s[h, rr, :] - delta_s[h, seq_rows, :])).astype(BF16)
            for h in range(hp):
                sq, sv = slice(h * HEAD_PAD, (h + 1) * HEAD_PAD), slice(h * 128, (h + 1) * 128)
                dv_s[:, sv] += _dot(p_s[h], do_ref[rows, sv].astype(BF16), TN)
                dk_s[:, sq] += _dot(ds_s[h], q_ref[rows, sq], TN)
                dq_s[rows, sq] += _dot(ds_s[h], k_ref[:, sq], NN)
            return c

        lax.fori_loop(j, nq, step, 0)
        dk_ref[...] = dk_s[...].astype(BF16)
        dv_ref[...] = dv_s[...].astype(BF16)

        @pl.when(j == nq - 1)
        def _():
            dq_ref[...] = dq_s[...].astype(BF16)

    seq = lambda c: BS((S, c), lambda b, h, j: (b, h))
    blk = lambda c: BS((tq, c), lambda b, h, j: (b * nq + j, h))
    return pl.pallas_call(
        body, name="mla_attn_bwd", grid=(B, N_HEADS // hp, nq),
        in_specs=[seq(hp * HEAD_PAD), blk(hp * HEAD_PAD), blk(hp * HEAD_PAD), seq(hp * 128), seq(hp * 128),
                  BS((hp, S, 1), lambda b, h, j: (h, b, 0))],
        out_specs=[seq(hp * HEAD_PAD), blk(hp * HEAD_PAD), blk(hp * 128)],
        out_shape=[jax.ShapeDtypeStruct((T, 1024), BF16), jax.ShapeDtypeStruct((T, 1024), BF16),
                   jax.ShapeDtypeStruct((T, 512), BF16)],
        scratch_shapes=[pltpu.VMEM((hp, S, 1), F32), pltpu.VMEM((S, hp * HEAD_PAD), F32),
                        pltpu.VMEM((tq, hp * HEAD_PAD), F32), pltpu.VMEM((tq, hp * 128), F32),
                        pltpu.VMEM((hp, tq, tq), F32), pltpu.VMEM((hp, tq, tq), F32),
                        pltpu.VMEM((hp, tq, tq), BF16), pltpu.VMEM((hp, tq, tq), BF16)],
        compiler_params=_arb(3))(Q, K, V, O, dO, LSE)


def mla_proj_bwd(dQ, dK, dV, cos_t, sin_t, P, dab, wq, wkv, gq, gkv, tm=512):
    T = P.shape[0]
    tm = min(tm, T)

    def norm_bwd(x, dy, g):
        r = lax.rsqrt(jnp.mean(x * x, axis=-1, keepdims=True) + EPS)
        xh = x * r
        dxh = dy * g
        return r * (dxh - xh * jnp.mean(dxh * xh, axis=-1, keepdims=True)), jnp.sum(dy * xh, axis=0, keepdims=True)

    def body(dq_ref, dk_ref, dv_ref, c_ref, s_ref, p_ref, dab_ref, wq_ref, wkv_ref, gq_ref, gkv_ref,
             ql_ref, kvl_ref, o_ref, aq_ref, akv_ref):
        @pl.when(pl.program_id(0) == 0)
        def _():
            aq_ref[...] = jnp.zeros_like(aq_ref)
            akv_ref[...] = jnp.zeros_like(akv_ref)

        cos_row, sin_row = c_ref[...], s_ref[...]
        kr = jnp.zeros((tm, 128), F32)
        for h in range(N_HEADS):
            lo = h * HEAD_PAD
            ql_ref[:, lo:lo + 128] = (dq_ref[:, lo:lo + 128].astype(F32) * MLA_SCALE).astype(BF16)
            ql_ref[:, lo + 128:lo + 256] = (_rope_bwd(dq_ref[:, lo + 128:lo + 256].astype(F32), cos_row, sin_row) * MLA_SCALE).astype(BF16)
            kvl_ref[:, h * 128:(h + 1) * 128] = dk_ref[:, lo:lo + 128]
            kr = kr + dk_ref[:, lo + 128:lo + 256].astype(F32)
        kvl_ref[:, 512:] = dv_ref[...]
        dqn = _dot(ql_ref[...], wq_ref[...], NN)
        dkvn = _dot(kvl_ref[...], wkv_ref[...], NT)
        dcq, ggq = norm_bwd(p_ref[:, :Q_LORA].astype(F32), dqn, gq_ref[...])
        dckv, ggkv = norm_bwd(p_ref[:, Q_LORA:640].astype(F32), dkvn, gkv_ref[...])
        aq_ref[...] += ggq
        akv_ref[...] += ggkv
        o_ref[:, :Q_LORA] = dcq.astype(BF16)
        o_ref[:, Q_LORA:640] = dckv.astype(BF16)
        o_ref[:, 640:768] = _rope_bwd(kr, cos_row, sin_row).astype(BF16)
        o_ref[:, 768:896] = dab_ref[...]
        o_ref[:, 896:1024] = jnp.zeros((tm, 128), BF16)

    rowb = lambda c: BS((tm, c), lambda i: (i, 0))
    full = lambda r, c: BS((r, c), lambda i: (0, 0))
    return pl.pallas_call(
        body, name="mla_proj_bwd", grid=(T // tm,),
        in_specs=[rowb(1024), rowb(1024), rowb(512), rowb(128), rowb(128), rowb(1024), rowb(128),
                  full(1024, Q_LORA), full(KV_LORA, 1024), full(1, Q_LORA), full(1, KV_LORA)],
        out_specs=[rowb(1024), rowb(1024), rowb(1024), full(1, Q_LORA), full(1, KV_LORA)],
        out_shape=[jax.ShapeDtypeStruct((T, 1024), BF16)] * 3
        + [jax.ShapeDtypeStruct((1, Q_LORA), F32), jax.ShapeDtypeStruct((1, KV_LORA), F32)],
        compiler_params=_arb(1))(dQ, dK, dV, cos_t, sin_t, P, dab, wq, wkv, gq, gkv)


def _mem_probs(qh, kh):
    s = _dot(qh, kh, NT) * MEM_SCALE
    p = jnp.exp(s - jnp.max(s, axis=1, keepdims=True))
    return p / jnp.sum(p, axis=1, keepdims=True)


def mem_attn_fwd(P, MKV, B, S, M, tq=512):
    T = B * S
    tq = min(tq, S)
    nq = S // tq

    def body(q_ref, kv_ref, o_ref):
        for h in range(N_HEADS):
            sl = slice(h * 128, (h + 1) * 128)
            p = _mem_probs(q_ref[:, sl].astype(BF16), kv_ref[:, sl])
            o_ref[:, sl] = _dot(p.astype(BF16), kv_ref[:, 512 + h * 128:512 + (h + 1) * 128], NN).astype(BF16)

    return pl.pallas_call(
        body, name="mem_attn_fwd", grid=(B, nq),
        in_specs=[BS((tq, 512), lambda b, i: (b * nq + i, OFF_MEMQ // 512)), BS((M, 1024), lambda b, i: (b, 0))],
        out_specs=BS((tq, 512), lambda b, i: (b * nq + i, 0)),
        out_shape=jax.ShapeDtypeStruct((T, 512), BF16), compiler_params=_arb(2))(P, MKV)


def mem_attn_bwd(P, MKV, dO, B, S, M, tq=512):
    T = B * S
    tq = min(tq, S)
    nq = S // tq

    def body(q_ref, kv_ref, do_ref, dq_ref, dkv_ref):
        @pl.when(pl.program_id(1) == 0)
        def _():
            dkv_ref[...] = jnp.zeros_like(dkv_ref)

        for h in range(N_HEADS):
            sl = slice(h * 128, (h + 1) * 128)
            sv = slice(512 + h * 128, 512 + (h + 1) * 128)
            qh = q_ref[:, sl].astype(BF16)
            kh = kv_ref[:, sl]
            do = do_ref[:, sl].astype(BF16)
            p = _mem_probs(qh, kh)
            dkv_ref[:, sv] += _dot(p.astype(BF16), do, TN)
            dp = _dot(do, kv_ref[:, sv], NT)
            ds = (p * (dp - jnp.sum(dp * p, axis=1, keepdims=True)) * MEM_SCALE).astype(BF16)
            dq_ref[:, sl] = _dot(ds, kh, NN).astype(BF16)
            dkv_ref[:, sl] += _dot(ds, qh, TN)

    return pl.pallas_call(
        body, name="mem_attn_bwd", grid=(B, nq),
        in_specs=[BS((tq, 512), lambda b, i: (b * nq + i, OFF_MEMQ // 512)), BS((M, 1024), lambda b, i: (b, 0)),
                  BS((tq, 512), lambda b, i: (b * nq + i, 0))],
        out_specs=[BS((tq, 512), lambda b, i: (b * nq + i, 0)), BS((M, 1024), lambda b, i: (b, 0))],
        out_shape=[jax.ShapeDtypeStruct((T, 512), BF16), jax.ShapeDtypeStruct((B * M, 1024), F32)],
        compiler_params=_arb(2))(P, MKV, dO)


def gain_grad(x, dy, name, tm=256):
    T, n = x.shape
    tm = min(tm, T)

    def body(x_ref, dy_ref, o_ref):
        @pl.when(pl.program_id(0) == 0)
        def _():
            o_ref[...] = jnp.zeros_like(o_ref)

        xv = x_ref[...]
        xh = xv * lax.rsqrt(jnp.mean(xv * xv, axis=-1, keepdims=True) + EPS)
        o_ref[...] += jnp.sum(dy_ref[...] * xh, axis=0, keepdims=True)

    return pl.pallas_call(
        body, name=name, grid=(T // tm,),
        in_specs=[BS((tm, n), lambda i: (i, 0))] * 2, out_specs=BS((1, n), lambda i: (0, 0)),
        out_shape=jax.ShapeDtypeStruct((1, n), F32), compiler_params=_arb(1))(x, dy)


def _conv_silu(x, w, t):
    y = x * w[3:4, :]
    for s in range(1, GDN_CONV):
        y = y + jnp.where(t >= s, pltpu.roll(x, s, 0), 0.0) * w[3 - s:4 - s, :]
    return y, _sigmoid(y)


def gdn_prep_fwd(P, conv_w, B, S):
    T = B * S

    def body(x_ref, w_ref, o_ref):
        kind = pl.program_id(1)
        t = lax.broadcasted_iota(jnp.int32, (S, 1), 0)
        y, sg = _conv_silu(x_ref[...].astype(F32), w_ref[...], t)
        a = y * sg
        scale = jnp.where(kind == 0, GDN_SCALE, 1.0).astype(F32)
        for h in range(N_HEADS):
            sl = slice(h * 128, (h + 1) * 128)
            seg = a[:, sl]
            n = lax.rsqrt(jnp.sum(seg * seg, axis=-1, keepdims=True) + EPS)
            o_ref[:, sl] = jnp.where(kind < 2, seg * (n * scale), seg)

    return pl.pallas_call(
        body, name="gdn_prep_fwd", grid=(B, 3),
        in_specs=[BS((S, 512), lambda b, k: (b, OFF_GDN // 512 + k)), BS((GDN_CONV, 512), lambda b, k: (0, k))],
        out_specs=BS((S, 512), lambda b, k: (b, k)),
        out_shape=jax.ShapeDtypeStruct((T, GDN_QKV), F32), compiler_params=_arb(2))(P, conv_w)


def gdn_prep_bwd(P, dqkv, conv_w, B, S):
    T = B * S

    def body(x_ref, d_ref, w_ref, o_ref, gw_ref):
        kind = pl.program_id(0)

        @pl.when(pl.program_id(1) == 0)
        def _():
            gw_ref[...] = jnp.zeros_like(gw_ref)

        t = lax.broadcasted_iota(jnp.int32, (S, 1), 0)
        x = x_ref[...].astype(F32)
        w = w_ref[...]
        y, sg = _conv_silu(x, w, t)
        a = y * sg
        scale = jnp.where(kind == 0, GDN_SCALE, 1.0).astype(F32)
        das = []
        for h in range(N_HEADS):
            sl = slice(h * 128, (h + 1) * 128)
            seg, dseg = a[:, sl], d_ref[:, sl]
            n = lax.rsqrt(jnp.sum(seg * seg, axis=-1, keepdims=True) + EPS)
            dn = scale * (n * dseg - seg * (n * n * n) * jnp.sum(dseg * seg, axis=-1, keepdims=True))
            das.append(jnp.where(kind < 2, dn, dseg))
        dy = jnp.concatenate(das, axis=1) * (sg * (1.0 + y * (1.0 - sg)))
        dx = dy * w[3:4, :]
        gw_ref[3:4, :] += jnp.sum(dy * x, axis=0, keepdims=True)
        for s in range(1, GDN_CONV):
            dx = dx + jnp.where(t + s < S, pltpu.roll(dy, S - s, 0), 0.0) * w[3 - s:4 - s, :]
            gw_ref[3 - s:4 - s, :] += jnp.sum(dy * jnp.where(t >= s, pltpu.roll(x, s, 0), 0.0), axis=0, keepdims=True)
        o_ref[...] = dx.astype(BF16)

    return pl.pallas_call(
        body, name="gdn_prep_bwd", grid=(3, B),
        in_specs=[BS((S, 512), lambda k, b: (b, OFF_GDN // 512 + k)), BS((S, 512), lambda k, b: (b, k)),
                  BS((GDN_CONV, 512), lambda k, b: (0, k))],
        out_specs=[BS((S, 512), lambda k, b: (b, k)), BS((GDN_CONV, 512), lambda k, b: (0, k))],
        out_shape=[jax.ShapeDtypeStruct((T, GDN_QKV), BF16), jax.ShapeDtypeStruct((GDN_CONV, GDN_QKV), F32)],
        compiler_params=_arb(2))(P, dqkv, conv_w)


def _chunk_row(n_rows):
    return lax.broadcasted_iota(jnp.int32, (n_rows, 1), 0) % CHUNK


def gdn_gate_fwd(P, alog_row, dt_row, B, S):
    T = B * S

    def body(x_ref, al_ref, dt_ref, o_ref):
        x = x_ref[...].astype(F32)
        lane = lax.broadcasted_iota(jnp.int32, (1, 128), 1)
        g = jnp.where(lane < 4, -jnp.exp(al_ref[...]) * _softplus(x + dt_ref[...]), 0.0)
        t = _chunk_row(S)
        for s in (1, 2, 4, 8, 16, 32):
            g = g + jnp.where(t >= s, pltpu.roll(g, s, 0), 0.0)
        o_ref[...] = jnp.where(lane < 4, g, jnp.where(lane < 8, _sigmoid(x), 0.0))

    row = BS((1, 128), lambda b: (0, 0))
    return pl.pallas_call(
        body, name="gdn_gate_fwd", grid=(B,),
        in_specs=[BS((S, 128), lambda b: (b, 768 // 128)), row, row], out_specs=BS((S, 128), lambda b: (b, 0)),
        out_shape=jax.ShapeDtypeStruct((T, 128), F32), compiler_params=_arb(1))(P, alog_row, dt_row)


def gdn_gate_bwd(P, dGB, alog_row, dt_row, B, S):
    T = B * S

    def body(x_ref, d_ref, al_ref, dt_ref, o_ref, acc_ref):
        @pl.when(pl.program_id(0) == 0)
        def _():
            acc_ref[...] = jnp.zeros_like(acc_ref)

        x, d = x_ref[...].astype(F32), d_ref[...]
        lane = lax.broadcasted_iota(jnp.int32, (1, 128), 1)
        z = x + dt_ref[...]
        coef = -jnp.exp(al_ref[...])
        g = coef * _softplus(z)
        da = jnp.where(lane < 4, d * coef * _sigmoid(z), 0.0)
        beta = _sigmoid(x)
        o_ref[...] = jnp.where(lane < 4, da, jnp.where(lane < 8, d * beta * (1.0 - beta), 0.0)).astype(BF16)
        acc_ref[0:1, :] += jnp.sum(jnp.where(lane < 4, d * g, 0.0), axis=0, keepdims=True)
        acc_ref[1:2, :] += jnp.sum(da, axis=0, keepdims=True)

    row = BS((1, 128), lambda b: (0, 0))
    return pl.pallas_call(
        body, name="gdn_gate_bwd", grid=(B,),
        in_specs=[BS((S, 128), lambda b: (b, 768 // 128)), BS((S, 128), lambda b: (b, 0)), row, row],
        out_specs=[BS((S, 128), lambda b: (b, 0)), BS((8, 128), lambda b: (0, 0))],
        out_shape=[jax.ShapeDtypeStruct((T, 128), BF16), jax.ShapeDtypeStruct((8, 128), F32)],
        compiler_params=_arb(1))(P, dGB, alog_row, dt_row)


def _chunk_masks(nc):
    r = lax.broadcasted_iota(jnp.int32, (nc, CHUNK, CHUNK), 1)
    c = lax.broadcasted_iota(jnp.int32, (nc, CHUNK, CHUNK), 2)
    return r >= c, r > c


def _chunk_local(q, k, gc, gr, beta, incl, strict):
    decay = jnp.exp(jnp.where(incl, gc - gr, NEG))
    kb = k * beta
    kbf = k.astype(BF16)
    m_kk = _bdot("gcd,gjd->gcj", kb.astype(BF16), kbf)
    l_mat = jnp.where(strict, m_kk * decay, 0.0)
    a_mat = _bdot("gcd,gjd->gcj", q.astype(BF16), kbf) * decay
    return decay, kb, l_mat, a_mat


WY_SPLIT_LEVELS = 2


def _split_bf16(x):
    hi = x.astype(BF16)
    return hi, (x - hi.astype(F32)).astype(BF16)


def _mm_split(ah, al, bh, bl):
    spec = "gij,gjk->gik"
    return _bdot(spec, ah, bh) + (_bdot(spec, ah, bl) + _bdot(spec, al, bh))


def gdn_chunk_fwd(qkv, GB, Grow, B, S, nc=8):
    T = B * S
    N = S // CHUNK
    nc = min(nc, N)
    nb = N // nc
    R = nc * CHUNK

    def body(q_ref, k_ref, v_ref, gb_ref, gr_ref, u_ref, w_ref, t_ref, a_ref):
        incl, strict = _chunk_masks(nc)
        eye = (lax.broadcasted_iota(jnp.int32, (nc, CHUNK, CHUNK), 1)
               == lax.broadcasted_iota(jnp.int32, (nc, CHUNK, CHUNK), 2)).astype(F32)
        for h in range(N_HEADS):
            sl = slice(h * 128, (h + 1) * 128)
            q = q_ref[:, sl].reshape(nc, CHUNK, 128)
            k = k_ref[:, sl].reshape(nc, CHUNK, 128)
            v = v_ref[:, sl].reshape(nc, CHUNK, 128)
            gc = gb_ref[:, h:h + 1].reshape(nc, CHUNK, 1)
            beta = gb_ref[:, 4 + h:5 + h].reshape(nc, CHUNK, 1)
            gr = gr_ref[h][:, None, :]
            _, kb, l_mat, a_mat = _chunk_local(q, k, gc, gr, beta, incl, strict)
            pw = -l_mat
            tinv = eye + pw
            for level in range(5):
                if level < WY_SPLIT_LEVELS:
                    ph, pl_ = _split_bf16(pw)
                    pw = _mm_split(ph, pl_, ph, pl_)
                    ph, pl_ = _split_bf16(pw)
                    th, tl = _split_bf16(tinv)
                    tinv = tinv + _mm_split(th, tl, ph, pl_)
                else:
                    ph = pw.astype(BF16)
                    pw = _bdot("gij,gjk->gik", ph, ph)
                    tinv = tinv + _bdot("gij,gjk->gik", tinv.astype(BF16), pw.astype(BF16))
            tb = tinv.astype(BF16)
            u = _bdot("gcj,gjv->gcv", tb, (v * beta).astype(BF16))
            w = _bdot("gcj,gjk->gck", tb, (kb * jnp.exp(gc)).astype(BF16))
            u_ref[:, sl] = u.reshape(R, 128)
            w_ref[:, sl] = w.reshape(R, 128).astype(BF16)
            t_ref[h] = jnp.swapaxes(tinv, 1, 2).astype(BF16)
            a_ref[h] = a_mat.astype(BF16)

    rowb = lambda c, j: BS((R, c), lambda b, n: (b * nb + n, j))
    mat = BS((None, N_HEADS, nc, CHUNK, CHUNK), lambda b, n: (b, 0, n, 0, 0))
    return pl.pallas_call(
        body, name="gdn_chunk_fwd", grid=(B, nb),
        in_specs=[rowb(512, 0), rowb(512, 1), rowb(512, 2), rowb(128, 0),
                  BS((None, N_HEADS, nc, CHUNK), lambda b, n: (b, 0, n, 0))],
        out_specs=[rowb(512, 0), rowb(512, 0), mat, mat],
        out_shape=[jax.ShapeDtypeStruct((T, 512), F32), jax.ShapeDtypeStruct((T, 512), BF16),
                   jax.ShapeDtypeStruct((B, N_HEADS, N, CHUNK, CHUNK), BF16),
                   jax.ShapeDtypeStruct((B, N_HEADS, N, CHUNK, CHUNK), BF16)],
        compiler_params=_arb(2))(qkv, qkv, qkv, GB, Grow)


SCAN_CHUNKS = 4


def gdn_scan_fwd(qkv3, U3, W3, GB3, A, B, S):
    N = S // CHUNK
    cps = SCAN_CHUNKS if N % SCAN_CHUNKS == 0 else 1

    def body(q_ref, k_ref, u_ref, w_ref, gb_ref, a_ref, o_ref, vn_ref, st_ref, s_s):
        @pl.when(pl.program_id(0) == 0)
        def _():
            s_s[...] = jnp.zeros_like(s_s)

        for c in range(cps):
            rows = slice(c * CHUNK, (c + 1) * CHUNK)
            for b in range(B):
                for h in range(N_HEADS):
                    sl = slice(h * 128, (h + 1) * 128)
                    st = s_s[b, h]
                    st_ref[b, h, c] = st
                    stb = st.astype(BF16)
                    g = gb_ref[b, rows, h:h + 1]
                    gl = g[CHUNK - 1:CHUNK, :]
                    qg = (q_ref[b, rows, sl] * jnp.exp(g)).astype(BF16)
                    on_state = _dot(jnp.concatenate([w_ref[b, rows, sl].astype(BF16), qg], axis=0), stb, NN)
                    vn = u_ref[b, rows, sl] - on_state[:CHUNK]
                    vnb = vn.astype(BF16)
                    kd_t = jnp.transpose(k_ref[b, rows, sl] * jnp.exp(gl - g)).astype(BF16)
                    on_vn = _dot(jnp.concatenate([a_ref[b, h, c].astype(BF16), kd_t], axis=0), vnb, NN)
                    vn_ref[b, rows, sl] = vnb
                    o_ref[b, rows, sl] = (on_state[CHUNK:] + on_vn[:CHUNK]).astype(BF16)
                    s_s[b, h] = st * jnp.exp(gl) + on_vn[CHUNK:]

    tok = lambda c, j: BS((B, cps * CHUNK, c), lambda n: (0, n, j))
    return pl.pallas_call(
        body, name="gdn_scan_fwd", grid=(N // cps,),
        in_specs=[tok(512, 0), tok(512, 1), tok(512, 0), tok(512, 0), tok(128, 0),
                  BS((B, N_HEADS, cps, CHUNK, CHUNK), lambda n: (0, 0, n, 0, 0))],
        out_specs=[tok(512, 0), tok(512, 0), BS((B, N_HEADS, cps, 128, 128), lambda n: (0, 0, n, 0, 0))],
        out_shape=[jax.ShapeDtypeStruct((B, S, 512), BF16), jax.ShapeDtypeStruct((B, S, 512), BF16),
                   jax.ShapeDtypeStruct((B, N_HEADS, N, 128, 128), F32)],
        scratch_shapes=[pltpu.VMEM((B, N_HEADS, 128, 128), F32)],
        compiler_params=_arb(1))(qkv3, qkv3, U3, W3, GB3, A)


def gdn_scan_bwd(dO3, qkv3, W3, Vn3, GB3, A, St, B, S):
    N = S // CHUNK
    cps = SCAN_CHUNKS if N % SCAN_CHUNKS == 0 else 1

    def body(do_ref, q_ref, k_ref, w_ref, vn_ref, gb_ref, a_ref, st_ref,
             du_ref, dw_ref, dq_ref, dk_ref, da_ref, dg_ref, ds_s):
        @pl.when(pl.program_id(0) == 0)
        def _():
            ds_s[...] = jnp.zeros_like(ds_s)

        lane = lax.broadcasted_iota(jnp.int32, (1, 128), 1)
        last = lax.broadcasted_iota(jnp.int32, (CHUNK, 1), 0) == CHUNK - 1
        for c in reversed(range(cps)):
            rows = slice(c * CHUNK, (c + 1) * CHUNK)
            for b in range(B):
                dg_all = jnp.zeros((CHUNK, 128), F32)
                for h in range(N_HEADS):
                    sl = slice(h * 128, (h + 1) * 128)
                    st = st_ref[b, h, c]
                    stb = st.astype(BF16)
                    dsn = ds_s[b, h]
                    dsnb = dsn.astype(BF16)
                    g = gb_ref[b, rows, h:h + 1]
                    gl = g[CHUNK - 1:CHUNK, :]
                    egl = jnp.exp(gl)
                    ekd = jnp.exp(gl - g)
                    eg = jnp.exp(g)
                    q, k = q_ref[b, rows, sl], k_ref[b, rows, sl]
                    kd = k * ekd
                    qg = q * eg
                    do = do_ref[b, rows, sl].astype(BF16)
                    vnb = vn_ref[b, rows, sl].astype(BF16)
                    dvn = _dot(a_ref[b, h, c].astype(BF16), do, TN) + _dot(kd.astype(BF16), dsnb, NN)
                    dvnb = dvn.astype(BF16)
                    do_on = _dot(do, jnp.concatenate([stb, vnb], axis=0), NT)
                    dqg = do_on[:, :128]
                    da_ref[b, h, c] = do_on[:, 128:]
                    dkd = _dot(vnb, dsnb, NT)
                    ds_s[b, h] = (_dot(qg.astype(BF16), do, TN) + egl * dsn - _dot(w_ref[b, rows, sl].astype(BF16), dvnb, TN))
                    du_ref[b, rows, sl] = dvnb
                    dw_ref[b, rows, sl] = (-_dot(dvnb, stb, NT)).astype(BF16)
                    dq_ref[b, rows, sl] = dqg * eg
                    dk_ref[b, rows, sl] = dkd * ekd
                    ddel = jnp.sum(dkd * kd, axis=1, keepdims=True)
                    dgl = jnp.sum(ddel, axis=0, keepdims=True) + jnp.sum(jnp.sum(st * dsn, axis=1, keepdims=True), axis=0, keepdims=True) * egl
                    col = jnp.sum(dqg * qg, axis=1, keepdims=True) - ddel + jnp.where(last, dgl, 0.0)
                    dg_all = jnp.where(lane == h, col, dg_all)
                dg_ref[b, rows, :] = dg_all

    steps = N // cps
    tok = lambda c, j: BS((B, cps * CHUNK, c), lambda n: (0, steps - 1 - n, j))
    mat = lambda d: BS((B, N_HEADS, cps, d, d), lambda n: (0, 0, steps - 1 - n, 0, 0))
    return pl.pallas_call(
        body, name="gdn_scan_bwd", grid=(steps,),
        in_specs=[tok(512, 0), tok(512, 0), tok(512, 1), tok(512, 0), tok(512, 0), tok(128, 0), mat(CHUNK), mat(128)],
        out_specs=[tok(512, 0), tok(512, 0), tok(512, 0), tok(512, 0), mat(CHUNK), tok(128, 0)],
        out_shape=[jax.ShapeDtypeStruct((B, S, 512), BF16)] * 2 + [jax.ShapeDtypeStruct((B, S, 512), F32)] * 2
        + [jax.ShapeDtypeStruct((B, N_HEADS, N, CHUNK, CHUNK), F32), jax.ShapeDtypeStruct((B, S, 128), F32)],
        scratch_shapes=[pltpu.VMEM((B, N_HEADS, 128, 128), F32)],
        compiler_params=_arb(1))(dO3, qkv3, qkv3, W3, Vn3, GB3, A, St)


def gdn_chunk_bwd(qkv, GB, Grow, Tinv, dA, dU, dW, dQ1, dK1, dG1, B, S, nc=8):
    T = B * S
    N = S // CHUNK
    nc = min(nc, N)
    nb = N // nc
    R = nc * CHUNK

    def body(q_ref, k_ref, v_ref, gb_ref, gr_ref, t_ref, da_ref, du_ref, dw_ref, dq1_ref, dk1_ref, dg1_ref, o_ref, dgb_ref):
        incl, strict = _chunk_masks(nc)
        lane = lax.broadcasted_iota(jnp.int32, (1, 128), 1)
        dg_all = dg1_ref[...]
        db_all = jnp.zeros((R, 128), F32)
        for h in range(N_HEADS):
            sl = slice(h * 128, (h + 1) * 128)
            q = q_ref[:, sl].reshape(nc, CHUNK, 128)
            k = k_ref[:, sl].reshape(nc, CHUNK, 128)
            v = v_ref[:, sl].reshape(nc, CHUNK, 128)
            gc = gb_ref[:, h:h + 1].reshape(nc, CHUNK, 1)
            beta = gb_ref[:, 4 + h:5 + h].reshape(nc, CHUNK, 1)
            gr = gr_ref[h][:, None, :]
            decay, kb, l_mat, a_mat = _chunk_local(q, k, gc, gr, beta, incl, strict)
            eg = jnp.exp(gc)
            kbg = kb * eg
            vb = v * beta
            tt = t_ref[h].astype(BF16)
            du = du_ref[:, sl].reshape(nc, CHUNK, 128).astype(BF16)
            dw = dw_ref[:, sl].reshape(nc, CHUNK, 128).astype(BF16)
            dvb = _bdot("gjc,gcv->gjv", tt, du)
            dkbg = _bdot("gjc,gck->gjk", tt, dw)
            dt = _bdot("gcv,gjv->gcj", du, vb.astype(BF16)) + _bdot("gck,gjk->gcj", dw, kbg.astype(BF16))
            tmp = _bdot("gca,gab->gcb", tt, dt.astype(BF16))
            dl = jnp.where(strict, -_bdot("gcb,gbd->gcd", tmp.astype(BF16), tt), 0.0)
            da = da_ref[h]
            dm = (dl * decay).astype(BF16)
            dqk = (da * decay).astype(BF16)
            kbf = k.astype(BF16)
            dkb = _bdot("gcj,gjd->gcd", dm, kbf) + dkbg * eg
            dk = (_bdot("gcj,gcd->gjd", dm, kb.astype(BF16)) + _bdot("gcj,gcd->gjd", dqk, q.astype(BF16))
                  + dk1_ref[:, sl].reshape(nc, CHUNK, 128) + dkb * beta)
            dq = _bdot("gcj,gjd->gcd", dqk, kbf) + dq1_ref[:, sl].reshape(nc, CHUNK, 128)
            e = dl * l_mat + da * a_mat
            dgc = (jnp.sum(e, axis=2, keepdims=True) - jnp.sum(jnp.swapaxes(e, 1, 2), axis=2, keepdims=True)
                   + jnp.sum(dkbg * kbg, axis=2, keepdims=True))
            dbeta = jnp.sum(dkb * k, axis=2, keepdims=True) + jnp.sum(dvb * v, axis=2, keepdims=True)
            o_ref[:, sl] = dq.reshape(R, 128)
            o_ref[:, 512 + h * 128:512 + (h + 1) * 128] = dk.reshape(R, 128)
            o_ref[:, 1024 + h * 128:1024 + (h + 1) * 128] = (dvb * beta).reshape(R, 128)
            dg_all = dg_all + jnp.where(lane == h, dgc.reshape(R, 1), 0.0)
            db_all = jnp.where(lane == 4 + h, dbeta.reshape(R, 1), db_all)
        t = _chunk_row(R)
        for s in (1, 2, 4, 8, 16, 32):
            dg_all = dg_all + jnp.where(t + s < CHUNK, pltpu.roll(dg_all, R - s, 0), 0.0)
        dgb_ref[...] = jnp.where(lane < 4, dg_all, db_all)

    rowb = lambda c, j: BS((R, c), lambda b, n: (b * nb + n, j))
    mat = BS((None, N_HEADS, nc, CHUNK, CHUNK), lambda b, n: (b, 0, n, 0, 0))
    return pl.pallas_call(
        body, name="gdn_chunk_bwd", grid=(B, nb),
        in_specs=[rowb(512, 0), rowb(512, 1), rowb(512, 2), rowb(128, 0),
                  BS((None, N_HEADS, nc, CHUNK), lambda b, n: (b, 0, n, 0)), mat, mat,
                  rowb(512, 0), rowb(512, 0), rowb(512, 0), rowb(512, 0), rowb(128, 0)],
        out_specs=[rowb(GDN_QKV, 0), rowb(128, 0)],
        out_shape=[jax.ShapeDtypeStruct((T, GDN_QKV), F32), jax.ShapeDtypeStruct((T, 128), F32)],
        compiler_params=_arb(2))(qkv, qkv, qkv, GB, Grow, Tinv, dA, dU, dW, dQ1, dK1, dG1)


def _gdn_out_norm(og, gg):
    outs, xhs, rs = [], [], []
    for h in range(N_HEADS):
        seg = og[:, h * 128:(h + 1) * 128]
        r = lax.rsqrt(jnp.mean(seg * seg, axis=-1, keepdims=True) + EPS)
        xh = seg * r
        outs.append(xh * gg)
        xhs.append(xh)
        rs.append(r)
    return outs, xhs, rs


def merge_fwd(o_mla, o_gdn, o_mem, P, x, tgt, w_out, g_gdn, g_fin, tm=512):
    T = x.shape[0]
    tm = min(tm, T)

    def body(om_ref, og_ref, oc_ref, gate_ref, x_ref, t_ref, w_ref, gg_ref, gf_ref, mix_ref, dx_ref, dxb_ref, sq_ref, gnf_ref):
        @pl.when(pl.program_id(0) == 0)
        def _():
            sq_ref[...] = jnp.zeros_like(sq_ref)
            gnf_ref[...] = jnp.zeros_like(gnf_ref)

        ogn, _, _ = _gdn_out_norm(og_ref[...].astype(F32), gg_ref[...])
        cat = jnp.concatenate([om_ref[...].astype(F32)] + ogn + [oc_ref[...].astype(F32)], axis=1)
        gt = gate_ref[...].astype(F32)
        mixed = (cat * (gt * _sigmoid(gt))).astype(BF16)
        mix_ref[...] = mixed
        x2 = x_ref[...] + _dot(mixed, w_ref[...], NN)
        r2 = lax.rsqrt(jnp.mean(x2 * x2, axis=-1, keepdims=True) + EPS)
        xh = x2 * r2
        gf = gf_ref[...]
        diff = xh * gf - t_ref[...]
        sq_ref[...] += jnp.sum(diff * diff, axis=0, keepdims=True)
        dy = diff * (1.0 / D_MODEL)
        gnf_ref[...] += jnp.sum(dy * xh, axis=0, keepdims=True)
        dxh = dy * gf
        dx = r2 * (dxh - xh * jnp.mean(dxh * xh, axis=-1, keepdims=True))
        dx_ref[...] = dx
        dxb_ref[...] = dx.astype(BF16)

    rowb = lambda c, j=0: BS((tm, c), lambda i: (i, j))
    full = lambda r, c: BS((r, c), lambda i: (0, 0))
    return pl.pallas_call(
        body, name="merge_fwd", grid=(T // tm,),
        in_specs=[rowb(512), rowb(512), rowb(512), rowb(D_MIX, OFF_GATE // D_MIX), rowb(D_MODEL), rowb(D_MODEL),
                  full(D_MIX, D_MODEL), full(1, 128), full(1, D_MODEL)],
        out_specs=[rowb(D_MIX), rowb(D_MODEL), rowb(D_MODEL), full(1, D_MODEL), full(1, D_MODEL)],
        out_shape=[jax.ShapeDtypeStruct((T, D_MIX), BF16), jax.ShapeDtypeStruct((T, D_MODEL), F32),
                   jax.ShapeDtypeStruct((T, D_MODEL), BF16),
                   jax.ShapeDtypeStruct((1, D_MODEL), F32), jax.ShapeDtypeStruct((1, D_MODEL), F32)],
        compiler_params=_arb(1))(o_mla, o_gdn, o_mem, P, x, tgt, w_out, g_gdn, g_fin)


def merge_bwd(dx2, o_mla, o_gdn, o_mem, P, w_out, g_gdn, tm=512):
    T = dx2.shape[0]
    tm = min(tm, T)

    def body(dx_ref, om_ref, og_ref, oc_ref, gate_ref, w_ref, gg_ref, dgate_ref, dom_ref, dog_ref, doc_ref, ggn_ref):
        @pl.when(pl.program_id(0) == 0)
        def _():
            ggn_ref[...] = jnp.zeros_like(ggn_ref)

        gg = gg_ref[...]
        dmix = _dot(dx_ref[...].astype(BF16), w_ref[...], NT)
        ogn, xhs, rs = _gdn_out_norm(og_ref[...].astype(F32), gg)
        cat = jnp.concatenate([om_ref[...].astype(F32)] + ogn + [oc_ref[...].astype(F32)], axis=1)
        gt = gate_ref[...].astype(F32)
        sg = _sigmoid(gt)
        dgate_ref[...] = (dmix * cat * (sg * (1.0 + gt * (1.0 - sg)))).astype(BF16)
        dcat = dmix * (gt * sg)
        dom_ref[...] = dcat[:, :512].astype(BF16)
        doc_ref[...] = dcat[:, 1024:].astype(BF16)
        acc = jnp.zeros((1, 128), F32)
        for h in range(N_HEADS):
            dseg = dcat[:, 512 + h * 128:512 + (h + 1) * 128]
            acc = acc + jnp.sum(dseg * xhs[h], axis=0, keepdims=True)
            dxh = dseg * gg
            dog_ref[:, h * 128:(h + 1) * 128] = (rs[h] * (dxh - xhs[h] * jnp.mean(dxh * xhs[h], axis=-1, keepdims=True))).astype(BF16)
        ggn_ref[...] += acc

    rowb = lambda c, j=0: BS((tm, c), lambda i: (i, j))
    full = lambda r, c: BS((r, c), lambda i: (0, 0))
    return pl.pallas_call(
        body, name="merge_bwd", grid=(T // tm,),
        in_specs=[rowb(D_MODEL), rowb(512), rowb(512), rowb(512), rowb(D_MIX, OFF_GATE // D_MIX),
                  full(D_MIX, D_MODEL), full(1, 128)],
        out_specs=[rowb(D_MIX), rowb(512), rowb(512), rowb(512), full(1, 128)],
        out_shape=[jax.ShapeDtypeStruct((T, D_MIX), BF16)] + [jax.ShapeDtypeStruct((T, 512), BF16)] * 3
        + [jax.ShapeDtypeStruct((1, 128), F32)],
        compiler_params=_arb(1))(dx2, o_mla, o_gdn, o_mem, P, w_out, g_gdn)


def in_proj_bwd(dP, wp, x, dx2, gain, after, tm=512):
    T, n = x.shape
    tm = min(tm, T)
    k = len(dP)
    widths = [p.shape[1] for p in dP]
    offs = [sum(widths[:i]) for i in range(k)]

    def body(*refs):
        w_ref, x_ref, dx2_ref, g_ref = refs[k:k + 4]
        o_ref, acc_ref = refs[-2:]

        @pl.when(pl.program_id(0) == 0)
        def _():
            acc_ref[...] = jnp.zeros_like(acc_ref)

        dy = None
        for a_ref, off, w in zip(refs[:k], offs, widths):
            d = _dot(a_ref[...], w_ref[off:off + w, :], NN)
            dy = d if dy is None else dy + d
        xv = x_ref[...]
        r = lax.rsqrt(jnp.mean(xv * xv, axis=-1, keepdims=True) + EPS)
        xh = xv * r
        acc_ref[...] += jnp.sum(dy * xh, axis=0, keepdims=True)
        dxh = dy * g_ref[...]
        o_ref[...] = dx2_ref[...] + r * (dxh - xh * jnp.mean(dxh * xh, axis=-1, keepdims=True))

    rowb = BS((tm, n), lambda i: (i, 0))
    full = BS((1, n), lambda i: (0, 0))
    return pl.pallas_call(
        body, name="in_proj_bwd", grid=(T // tm,),
        in_specs=[BS((tm, w), lambda i: (i, 0)) for w in widths]
        + [BS(wp.shape, lambda i: (0, 0), pipeline_mode=pl.Buffered(1)), rowb, rowb, full, BS(memory_space=pl.ANY)],
        out_specs=[rowb, full], out_shape=[jax.ShapeDtypeStruct((T, n), F32), jax.ShapeDtypeStruct((1, n), F32)],
        compiler_params=_arb(1))(*dP, wp, x, dx2, gain, after)


W_IN_SHARD = D_IN // 4
_GDN0 = Q_LORA + KV_LORA + MLA_ROPE
_AB0 = _GDN0 + GDN_QKV
_MEMQ0 = _AB0 + 2 * N_HEADS
_GATE0 = _MEMQ0 + N_HEADS * MEM_DH


def _w_in_row_map():
    a, m, gt = _AB0 - 2 * W_IN_SHARD, _MEMQ0 - 2 * W_IN_SHARD, _GATE0 - 2 * W_IN_SHARD
    e0 = OFF_GDN + W_IN_SHARD - _GDN0
    e1 = e0 + W_IN_SHARD
    e2 = OFF_GATE + W_IN_SHARD - gt
    return [(0, 0, 0, 672), (0, 672, 704, 32), (2, a, 768, m - a), (2, m, OFF_MEMQ, gt - m), (0, _GDN0, OFF_GDN, W_IN_SHARD - _GDN0),
            (1, 0, e0, W_IN_SHARD), (2, 0, e1, a), (2, gt, OFF_GATE, W_IN_SHARD - gt), (3, 0, e2, W_IN_SHARD)]


_W_IN_ZERO_ROWS = [(672, 32), (736, 32), (776, 248)]
W_IN_LANES = 256


EARLY_ROWS = 704
EARLY_ROW0 = (0, 80)


def pad_w_in_t_early(early):
    per_half = early.shape[3] // W_IN_LANES

    def body(s_ref, o_ref):
        for r0, n in _W_IN_ZERO_ROWS:
            o_ref[r0:r0 + n, :] = jnp.zeros((n, W_IN_LANES), o_ref.dtype)
        for q, src, dst, n in _w_in_row_map():
            if dst < OFF_GDN:
                first = src - EARLY_ROW0[q // 2]
                o_ref[dst:dst + n, :] = s_ref[q // 2, first:first + n, :]

    return pl.pallas_call(
        body, name="pad_w_in_t_early", grid=(D_MODEL // W_IN_LANES,),
        in_specs=[BS((2, None, EARLY_ROWS, W_IN_LANES), lambda j: (0, j // per_half, 0, j % per_half))],
        out_specs=BS((OFF_GDN, W_IN_LANES), lambda j: (0, j)),
        out_shape=jax.ShapeDtypeStruct((N_PAD, D_MODEL), early.dtype), compiler_params=_arb(1))(early)


def pad_w_in_t_rest(shards, wp):
    per_half = shards.shape[3] // W_IN_LANES
    blocks = (OFF_GDN, OFF_GATE)
    assert OFF_GATE - OFF_GDN == OFF_GDN and N_PAD - OFF_GATE == OFF_GDN

    def body(s_ref, w_in, o_ref):
        for i, b0 in enumerate(blocks):
            @pl.when(pl.program_id(1) == i)
            def _(b0=b0):
                for q, src, dst, n in _w_in_row_map():
                    if b0 <= dst < b0 + OFF_GDN:
                        o_ref[dst - b0:dst - b0 + n, :] = s_ref[q, src:src + n, :]

    return pl.pallas_call(
        body, name="pad_w_in_t_rest", grid=(D_MODEL // W_IN_LANES, len(blocks)),
        in_specs=[BS((N_CHIPS, None, W_IN_SHARD, W_IN_LANES), lambda j, i: (0, j // per_half, 0, j % per_half)), ANY],
        out_specs=BS((OFF_GDN, W_IN_LANES), lambda j, i: (1 + i, j)), input_output_aliases={1: 0},
        out_shape=jax.ShapeDtypeStruct((N_PAD, D_MODEL), shards.dtype), compiler_params=_arb(2))(shards, wp)


def unpad_w_in_t(g):
    def body(g_ref, o_ref):
        for q, src, dst, n in _w_in_row_map():
            o_ref[q, src:src + n, :] = g_ref[dst:dst + n, :]

    return pl.pallas_call(
        body, name="unpad_w_in_t", grid=(D_MODEL // W_IN_LANES,),
        in_specs=[BS((N_PAD, W_IN_LANES), lambda j: (0, j))], out_specs=BS((N_CHIPS, W_IN_SHARD, W_IN_LANES), lambda j: (0, 0, j)),
        out_shape=jax.ShapeDtypeStruct((N_CHIPS, W_IN_SHARD, D_MODEL), g.dtype), compiler_params=_arb(1))(g)


def _perm_w_kv_b(s):
    return jnp.concatenate([s[h, :, :128] for h in range(N_HEADS)] + [s[h, :, 128:] for h in range(N_HEADS)], axis=1)


def _unperm_w_kv_b(g):
    return jnp.stack([jnp.concatenate([g[:, h * 128:(h + 1) * 128], g[:, 512 + h * 128:512 + (h + 1) * 128]], axis=1)
                      for h in range(N_HEADS)])


def _lane_row(v4):
    return jnp.pad(v4.reshape(1, -1).astype(F32), ((0, 0), (0, 128 - v4.size)))


N_CHIPS = 4
MESH = pl.DeviceIdType.MESH
ANY = BS(memory_space=pl.ANY)


def _place():
    return lax.axis_index("x"), lax.axis_index("y"), lax.axis_index("c")


def _other_chips(x, y):
    return [(1 - x, y), (x, 1 - y), (1 - x, 1 - y)]


def _half(split, which):
    axis, size = split
    ds = pl.ds(pl.multiple_of(which * size, 16 if axis == 0 else 128), size)
    return (ds, slice(None)) if axis == 0 else (slice(None), ds)


SEM = BS(memory_space=pltpu.SEMAPHORE)
HBM = BS(memory_space=pltpu.HBM)
_IN_HBM = lambda a: pltpu.with_memory_space_constraint(a, pltpu.HBM)
_SIDE_EFFECT = pltpu.SideEffectType.DATAFLOW_SIDE_EFFECTING


def _late_gather_copies(s_refs, l_refs, send_sems, recv_sems, local_sems, with_arrivals):
    x, y, c = _place()
    sends, recvs, locals_ = [], [], []
    for i, (s_ref, l_ref) in enumerate(zip(s_refs, l_refs)):
        locals_.append(pltpu.make_async_copy(s_ref, l_ref.at[2 * x + y], local_sems.at[i]))
        for j, (px, py) in enumerate(_other_chips(x, y)):
            k = 3 * i + j
            sends.append(pltpu.make_async_remote_copy(src_ref=s_ref, dst_ref=l_ref.at[2 * x + y], send_sem=send_sems.at[k],
                                                      recv_sem=recv_sems.at[k], device_id=(px, py, c), device_id_type=MESH))
            if with_arrivals:
                recvs.append(pltpu.make_async_remote_copy(src_ref=s_ref, dst_ref=l_ref.at[2 * px + py], send_sem=send_sems.at[k],
                                                          recv_sem=recv_sems.at[k], device_id=(px, py, c), device_id_type=MESH))
    return sends, recvs, locals_


def late_gather_start(shards, after, name):
    n = len(shards)

    def body(*refs):
        s_refs, l_refs = refs[:n], refs[n:2 * n]
        send_sems, recv_sems, local_sems = refs[2 * n + 1:2 * n + 4]
        token = refs[-1]
        sends, _, locals_ = _late_gather_copies(s_refs, l_refs, send_sems, recv_sems, local_sems, False)
        for cp in locals_ + sends:
            cp.start()
        token[...] = jnp.zeros_like(token)

    lands = [lax.empty((N_CHIPS,) + s.shape, s.dtype) for s in shards]
    hbm_like = lambda a: pltpu.HBM(a.shape, a.dtype)
    out = pl.pallas_call(
        body, name=name,
        out_shape=[pltpu.SemaphoreType.DMA((3 * n,)), pltpu.SemaphoreType.DMA((3 * n,)), pltpu.SemaphoreType.DMA((n,))]
        + [hbm_like(s) for s in shards] + [hbm_like(l) for l in lands] + [jax.ShapeDtypeStruct((8, 128), F32)],
        in_specs=[HBM] * (2 * n) + [BS(memory_space=pl.ANY)], out_specs=[SEM] * 3 + [HBM] * (2 * n) + [BS(memory_space=pltpu.VMEM)],
        input_output_aliases={i: 3 + i for i in range(2 * n)},
        compiler_params=pltpu.CompilerParams(has_side_effects=_SIDE_EFFECT))(
            *[_IN_HBM(s) for s in shards], *[_IN_HBM(l) for l in lands], after)
    return out[:3], out[3:3 + n], out[3 + n:3 + 2 * n], out[-1]


def late_gather_wait(sems, shards, lands, after, name):
    n = len(shards)

    def body(*refs):
        s_refs, l_refs = refs[:n], refs[n:2 * n]
        send_sems, recv_sems, local_sems = refs[2 * n:2 * n + 3]
        sends, recvs, locals_ = _late_gather_copies(s_refs, l_refs, send_sems, recv_sems, local_sems, True)
        for cp in locals_:
            cp.wait()
        for cp in sends:
            cp.wait_send()
        for cp in recvs:
            cp.wait_recv()

    hbm_like = lambda a: pltpu.HBM(a.shape, a.dtype)
    out = pl.pallas_call(
        body, name=name, out_shape=[hbm_like(s) for s in shards] + [hbm_like(l) for l in lands],
        in_specs=[HBM] * (2 * n) + [SEM] * 3 + [BS(memory_space=pl.ANY)], out_specs=[HBM] * (2 * n),
        input_output_aliases={i: i for i in range(2 * n)},
        compiler_params=pltpu.CompilerParams(has_side_effects=_SIDE_EFFECT))(*shards, *lands, *sems, after)
    return out[n:]


def _sends(x, y, early):
    return (y == 0) if early else None


def _block(x, y, early):
    return x if early else 2 * x + y


def _if(cond, fn):
    if cond is None:
        fn()
    else:
        pl.when(cond)(fn)


def _half_gather_copies(s_ref, l_ref, send_sems, recv_sems, with_arrivals, early):
    x, y, c = _place()
    sends, recvs, peer_sends = [], [], []
    for j, (px, py) in enumerate(_other_chips(x, y)):
        sends.append(pltpu.make_async_remote_copy(src_ref=s_ref.at[c], dst_ref=l_ref.at[_block(x, y, early), c],
                                                  send_sem=send_sems.at[j], recv_sem=recv_sems.at[j], device_id=(px, py, c),
                                                  device_id_type=MESH))
        if with_arrivals:
            recvs.append(pltpu.make_async_remote_copy(src_ref=s_ref.at[c], dst_ref=l_ref.at[_block(px, py, early), c],
                                                      send_sem=send_sems.at[j], recv_sem=recv_sems.at[j], device_id=(px, py, c),
                                                      device_id_type=MESH))
            peer_sends.append(_sends(px, py, early))
    return sends, recvs, peer_sends


def half_gather_start(shard, name, early=False, after=()):
    def body(s_ref, l_ref, *rest):
        send_sems, recv_sems, local_sem, s_thru, l_thru, token = rest[len(after):]
        x, y, _ = _place()

        def go():
            pltpu.make_async_copy(s_ref, l_ref.at[_block(x, y, early)], local_sem.at[0]).start()
            for cp in _half_gather_copies(s_ref, l_ref, send_sems, recv_sems, False, early)[0]:
                cp.start()

        _if(_sends(x, y, early), go)
        token[...] = jnp.zeros_like(token)

    land = lax.empty((2 if early else N_CHIPS,) + shard.shape, shard.dtype)
    out = pl.pallas_call(
        body, name=name,
        out_shape=[pltpu.SemaphoreType.DMA((3,)), pltpu.SemaphoreType.DMA((3,)), pltpu.SemaphoreType.DMA((1,)),
                   pltpu.HBM(shard.shape, shard.dtype), pltpu.HBM(land.shape, land.dtype), jax.ShapeDtypeStruct((8, 128), F32)],
        in_specs=[HBM, HBM] + [BS(memory_space=pl.ANY)] * len(after),
        out_specs=[SEM] * 3 + [HBM, HBM, BS(memory_space=pltpu.VMEM)],
        input_output_aliases={0: 3, 1: 4},
        compiler_params=pltpu.CompilerParams(has_side_effects=_SIDE_EFFECT))(_IN_HBM(shard), _IN_HBM(land), *after)
    return out[:3], out[3], out[4], out[5]


def half_gather_wait(sems, shard, land, after, name, early=False):
    def body(s_ref, l_ref, send_sems, recv_sems, local_sem, *rest):
        x, y, _ = _place()
        sends, recvs, peer_sends = _half_gather_copies(s_ref, l_ref, send_sems, recv_sems, True, early)

        def sent():
            pltpu.make_async_copy(s_ref, l_ref.at[_block(x, y, early)], local_sem.at[0]).wait()
            for cp in sends:
                cp.wait_send()

        _if(_sends(x, y, early), sent)
        for cp, cond in zip(recvs, peer_sends):
            _if(cond, cp.wait_recv)

    out = pl.pallas_call(
        body, name=name, out_shape=[pltpu.HBM(shard.shape, shard.dtype), pltpu.HBM(land.shape, land.dtype)],
        in_specs=[HBM, HBM] + [SEM] * 3 + [BS(memory_space=pl.ANY)] * len(after), out_specs=[HBM, HBM],
        input_output_aliases={0: 0, 1: 1},
        compiler_params=pltpu.CompilerParams(has_side_effects=_SIDE_EFFECT))(shard, land, *sems, *after)
    return out[1]


def pass_halves_to_sibling(land, name, early=False):
    def body(l_in, l_ref, send_sems, recv_sems):
        x, y, c = _place()
        copies = []
        for j, (px, py) in enumerate(_other_chips(x, y)):
            q = _block(px, py, early)
            give = pltpu.make_async_remote_copy(src_ref=l_ref.at[q, c], dst_ref=l_ref.at[q, c], send_sem=send_sems.at[j],
                                                recv_sem=recv_sems.at[j], device_id=(x, y, 1 - c), device_id_type=MESH)
            take = pltpu.make_async_remote_copy(src_ref=l_ref.at[q, c], dst_ref=l_ref.at[q, 1 - c], send_sem=send_sems.at[j],
                                                recv_sem=recv_sems.at[j], device_id=(x, y, 1 - c), device_id_type=MESH)
            _if(_sends(px, py, early), give.start)
            copies.append((give, take, _sends(px, py, early)))
        for give, take, cond in copies:
            def done(give=give, take=take):
                take.wait_recv()
                give.wait_send()

            _if(cond, done)

    return pl.pallas_call(
        body, name=name, in_specs=[ANY], out_specs=ANY, out_shape=jax.ShapeDtypeStruct(land.shape, land.dtype),
        input_output_aliases={0: 0},
        scratch_shapes=[pltpu.SemaphoreType.DMA((3,)), pltpu.SemaphoreType.DMA((3,))])(land)


def allgather_devices(block, name):
    R, C = block.shape

    def body(b_ref, o_ref, send_sems, recv_sems, local_sem):
        x, y, c = _place()
        me = 4 * x + 2 * y + c
        own = pltpu.make_async_copy(b_ref, o_ref.at[me], local_sem)
        own.start()
        copies = []
        for r in range(1, 8):
            px = 1 - x if r & 4 else x
            py = 1 - y if r & 2 else y
            pc = 1 - c if r & 1 else c
            send = pltpu.make_async_remote_copy(src_ref=b_ref, dst_ref=o_ref.at[me], send_sem=send_sems.at[r - 1],
                                                recv_sem=recv_sems.at[r - 1], device_id=(px, py, pc), device_id_type=MESH)
            recv = pltpu.make_async_remote_copy(src_ref=b_ref, dst_ref=o_ref.at[4 * px + 2 * py + pc], send_sem=send_sems.at[r - 1],
                                                recv_sem=recv_sems.at[r - 1], device_id=(px, py, pc), device_id_type=MESH)
            send.start()
            copies.append((send, recv))
        for send, recv in copies:
            recv.wait_recv()
            send.wait_send()
        own.wait()

    return pl.pallas_call(
        body, name=name, in_specs=[ANY], out_specs=ANY, out_shape=jax.ShapeDtypeStruct((8, R, C), block.dtype),
        scratch_shapes=[pltpu.SemaphoreType.DMA((7,)), pltpu.SemaphoreType.DMA((7,)), pltpu.SemaphoreType.DMA(())])(block)


def swap_sibling(arrs, name, splits=None):
    n = len(arrs)

    def sent(a_ref, i, c):
        return a_ref if splits is None else a_ref.at[(slice(None),) + _half(splits[i], 1 - c)]

    def out_shape(a, i):
        if splits is None:
            return a.shape
        axis, size = splits[i]
        return (a.shape[0], size, a.shape[2]) if axis == 0 else (a.shape[0], a.shape[1], size)

    def body(*refs):
        a_refs, o_refs = refs[:n], refs[n:2 * n]
        send_sems, recv_sems = refs[2 * n:]
        x, y, c = _place()
        copies = [pltpu.make_async_remote_copy(src_ref=sent(a_ref, i, c), dst_ref=o_ref, send_sem=send_sems.at[i],
                                               recv_sem=recv_sems.at[i], device_id=(x, y, 1 - c), device_id_type=MESH)
                  for i, (a_ref, o_ref) in enumerate(zip(a_refs, o_refs))]
        for cp in copies:
            cp.start()
        for cp in copies:
            cp.wait()

    return pl.pallas_call(
        body, name=name, in_specs=[ANY] * n, out_specs=[ANY] * n,
        out_shape=[jax.ShapeDtypeStruct(out_shape(a, i), a.dtype) for i, a in enumerate(arrs)],
        scratch_shapes=[pltpu.SemaphoreType.DMA((n,)), pltpu.SemaphoreType.DMA((n,))])(*arrs)


def _exchange_copies(p_refs, l_refs, send_sems, recv_sems):
    x, y, c = _place()
    return [pltpu.make_async_remote_copy(src_ref=p_ref.at[2 * px + py], dst_ref=l_ref.at[j], send_sem=send_sems.at[3 * i + j],
                                         recv_sem=recv_sems.at[3 * i + j], device_id=(px, py, c), device_id_type=MESH)
            for i, (p_ref, l_ref) in enumerate(zip(p_refs, l_refs)) for j, (px, py) in enumerate(_other_chips(x, y))]


def exchange_chips_start(parts, name):
    n = len(parts)

    def body(*refs):
        send_sems, recv_sems = refs[2 * n:2 * n + 2]
        for cp in _exchange_copies(refs[:n], refs[n:2 * n], send_sems, recv_sems):
            cp.start()
        refs[-1][...] = jnp.zeros_like(refs[-1])

    lands = [lax.empty((3,) + p.shape[1:], p.dtype) for p in parts]
    hbm_like = lambda a: pltpu.HBM(a.shape, a.dtype)
    out = pl.pallas_call(
        body, name=name,
        out_shape=[pltpu.SemaphoreType.DMA((3 * n,)), pltpu.SemaphoreType.DMA((3 * n,))]
        + [hbm_like(p) for p in parts] + [hbm_like(l) for l in lands] + [jax.ShapeDtypeStruct((8, 128), F32)],
        in_specs=[HBM] * (2 * n), out_specs=[SEM] * 2 + [HBM] * (2 * n) + [BS(memory_space=pltpu.VMEM)],
        input_output_aliases={i: 2 + i for i in range(2 * n)},
        compiler_params=pltpu.CompilerParams(has_side_effects=_SIDE_EFFECT))(*[_IN_HBM(p) for p in parts], *[_IN_HBM(l) for l in lands])
    return out[:2], out[2:2 + n], out[2 + n:2 + 2 * n], out[-1]


def exchange_chips_wait(sems, parts, lands, after, name):
    n = len(parts)

    def body(*refs):
        send_sems, recv_sems = refs[2 * n:2 * n + 2]
        for cp in _exchange_copies(refs[:n], refs[n:2 * n], send_sems, recv_sems):
            cp.wait_send()
            cp.wait_recv()

    hbm_like = lambda a: pltpu.HBM(a.shape, a.dtype)
    out = pl.pallas_call(
        body, name=name, out_shape=[hbm_like(p) for p in parts] + [hbm_like(l) for l in lands],
        in_specs=[HBM] * (2 * n) + [SEM] * 2 + [BS(memory_space=pl.ANY)], out_specs=[HBM] * (2 * n),
        input_output_aliases={i: i for i in range(2 * n)},
        compiler_params=pltpu.CompilerParams(has_side_effects=_SIDE_EFFECT))(*parts, *lands, *sems, after)
    return out[:n], out[n:]


def add_fours(parts, from_chips, chip_core, name):
    n = len(parts)

    def body(s_ref, *refs):
        for a_ref, p_ref, o_ref in zip(refs[:n], refs[n:2 * n], refs[2 * n:]):
            s = a_ref[...].astype(F32)
            for j in range(3):
                s = s + p_ref[j].astype(F32)
            o_ref[...] = s

    own_specs = [BS((None,) + p.shape[1:], lambda g, s: (s[0], 0, 0)) for p in parts]
    chip_specs = [BS(p.shape, lambda g, s: (0, 0, 0)) for p in from_chips]
    out_specs = [BS(p.shape[1:], lambda g, s: (0, 0)) for p in parts]
    return pl.pallas_call(
        body, name=name,
        grid_spec=pltpu.PrefetchScalarGridSpec(num_scalar_prefetch=1, grid=(1,), in_specs=own_specs + chip_specs,
                                               out_specs=out_specs),
        out_shape=[jax.ShapeDtypeStruct(p.shape[1:], F32) for p in parts], compiler_params=_arb(1))(chip_core, *parts, *from_chips)


def _half_block(shape2, split):
    axis, size = split
    return (size, shape2[1]) if axis == 0 else (shape2[0], size)


def add_pairs(parts, halves, splits, core, name):
    n = len(parts)

    def body(s_ref, *refs):
        for a_ref, b_ref, o_ref in zip(refs[:n], refs[n:2 * n], refs[2 * n:]):
            o_ref[...] = (a_ref[...].astype(F32) + b_ref[...].astype(F32)).astype(BF16)

    def mine(i):
        blk = (None,) + _half_block(parts[i].shape[1:], splits[i])
        if splits[i][0] == 0:
            return BS(blk, lambda q, s: (q, s[0], 0))
        return BS(blk, lambda q, s: (q, 0, s[0]))

    half_specs = [BS((None,) + h.shape[1:], lambda q, s: (q, 0, 0)) for h in halves]
    return pl.pallas_call(
        body, name=name,
        grid_spec=pltpu.PrefetchScalarGridSpec(num_scalar_prefetch=1, grid=(N_CHIPS,),
                                               in_specs=[mine(i) for i in range(n)] + half_specs, out_specs=half_specs),
        out_shape=[jax.ShapeDtypeStruct(h.shape, BF16) for h in halves], compiler_params=_arb(1))(core, *parts, *halves)


def add_fives(parts, halves, from_chips, splits, chip_core, name):
    n = len(parts)

    def body(s_ref, *refs):
        for a_ref, b_ref, p_ref, o_ref in zip(refs[:n], refs[n:2 * n], refs[2 * n:3 * n], refs[3 * n:]):
            s = a_ref[...].astype(F32) + b_ref[...].astype(F32)
            for j in range(3):
                s = s + p_ref[j].astype(F32)
            o_ref[...] = s

    def mine(i):
        blk = (None,) + _half_block(parts[i].shape[1:], splits[i])
        if splits[i][0] == 0:
            return BS(blk, lambda g, s: (s[0], s[1], 0))
        return BS(blk, lambda g, s: (s[0], 0, s[1]))

    half_specs = [BS((None,) + h.shape[1:], lambda g, s: (s[0], 0, 0)) for h in halves]
    chip_specs = [BS(p.shape, lambda g, s: (0, 0, 0)) for p in from_chips]
    out_specs = [BS(h.shape[1:], lambda g, s: (0, 0)) for h in halves]
    return pl.pallas_call(
        body, name=name,
        grid_spec=pltpu.PrefetchScalarGridSpec(num_scalar_prefetch=1, grid=(1,),
                                               in_specs=[mine(i) for i in range(n)] + half_specs + chip_specs, out_specs=out_specs),
        out_shape=[jax.ShapeDtypeStruct(h.shape[1:], F32) for h in halves], compiler_params=_arb(1))(chip_core, *parts, *halves, *from_chips)


def sum_leading(a, name):
    def body(a_ref, o_ref):
        s = a_ref[0]
        for j in range(1, a.shape[0]):
            s = s + a_ref[j]
        o_ref[...] = s

    return pl.pallas_call(body, name=name, out_shape=jax.ShapeDtypeStruct(a.shape[1:], a.dtype))(a)


def _adamw_math(w, g, m, v):
    mn = ADAM_B1 * m + (1.0 - ADAM_B1) * g
    vn = ADAM_B2 * v + (1.0 - ADAM_B2) * (g * g)
    m_hat = mn / (1.0 - ADAM_B1 ** ADAM_STEP)
    v_hat = vn / (1.0 - ADAM_B2 ** ADAM_STEP)
    return -ADAM_LR * (m_hat / (jnp.sqrt(v_hat) + ADAM_EPS) + ADAM_WD * w), mn, vn


def adamw(w, g, m, v, name):
    R, C = g.shape
    lead = (None,) * (w.ndim - 2)

    def body(w_ref, g_ref, m_ref, v_ref, d_ref, mo_ref, vo_ref):
        d_ref[...], mo_ref[...], vo_ref[...] = _adamw_math(w_ref[...], g_ref[...], m_ref[...], v_ref[...])

    wblk = BS(lead + (R, C), lambda i: (0,) * w.ndim)
    gblk = BS((R, C), lambda i: (0, 0))
    return pl.pallas_call(
        body, name=name, grid=(1,), in_specs=[wblk, gblk, wblk, wblk], out_specs=[wblk] * 3,
        out_shape=[jax.ShapeDtypeStruct(w.shape, F32)] * 3, compiler_params=_arb(1))(w, g, m, v)


SMALL_ROWS = 16
CONV_ROW0 = 8
LOSS_ROW = 14


def pack_small(small_grads, g_ab, g_conv, sq):
    present = [a for a in small_grads if a is not None]

    def body(*refs):
        ab_ref, conv_ref, sq_ref, o_ref = refs[len(present):]
        o_ref[...] = jnp.zeros_like(o_ref)
        it = iter(refs[:len(present)])
        for i, a in enumerate(small_grads):
            if a is not None:
                o_ref[i:i + 1, 0:a.shape[1]] = next(it)[...]
        o_ref[3:4, 0:128] = ab_ref[0:1, :]
        o_ref[4:5, 0:128] = ab_ref[1:2, :]
        half = 512
        for k in range(GDN_CONV * GDN_QKV // half):
            src_r, src_c = (k * half) // GDN_QKV, (k * half) % GDN_QKV
            dst_r, dst_c = CONV_ROW0 + (k * half) // 1024, (k * half) % 1024
            o_ref[dst_r:dst_r + 1, dst_c:dst_c + half] = conv_ref[src_r:src_r + 1, src_c:src_c + half]
        o_ref[LOSS_ROW:LOSS_ROW + 1, :] = sq_ref[...]

    return pl.pallas_call(body, name="pack_small", out_shape=jax.ShapeDtypeStruct((SMALL_ROWS, 1024), F32))(
        *present, g_ab, g_conv, sq)


def adamw_small(block, ws, ms, vs):
    k = len(ws)

    def body(b_ref, *refs):
        outs = refs[3 * k:]
        for i in range(k):
            n = ws[i].shape[1]
            g = b_ref[i:i + 1, 0:n]
            d, mn, vn = _adamw_math(refs[i][...], g, refs[k + i][...], refs[2 * k + i][...])
            outs[4 * i][...], outs[4 * i + 1][...], outs[4 * i + 2][...], outs[4 * i + 3][...] = g, d, mn, vn

    out = pl.pallas_call(
        body, name="adamw_small",
        out_shape=[jax.ShapeDtypeStruct(w.shape, F32) for w in ws for _ in range(4)])(block, *ws, *ms, *vs)
    return [out[4 * i:4 * i + 4] for i in range(k)]


def adamw_halves(w, mine, other, m, v, split, core, name):
    R, C = w.shape[-2:]
    axis, size = split
    lead = (None,) * (w.ndim - 2)
    zeros = (0,) * (w.ndim - 2)
    if axis == 0:
        tr = size if size <= 256 else next(t for t in range(256, 7, -1) if size % t == 0 and t % 8 == 0)
        nb = size // tr
        whole = BS(lead + (tr, C), lambda hi, j, s: zeros + (hi * nb + j, 0))
        part = BS((tr, C), lambda hi, j, s: (j, 0))
    else:
        nb = size // 128
        whole = BS(lead + (R, 128), lambda hi, j, s: zeros + (0, hi * nb + j))
        part = BS((R, 128), lambda hi, j, s: (0, j))

    def body(s_ref, w_ref, a_ref, b_ref, m_ref, v_ref, g_ref, d_ref, mo_ref, vo_ref):
        g = jnp.where(pl.program_id(0) == s_ref[0], a_ref[...], b_ref[...])
        g_ref[...] = g
        d_ref[...], mo_ref[...], vo_ref[...] = _adamw_math(w_ref[...], g, m_ref[...], v_ref[...])

    return pl.pallas_call(
        body, name=name,
        grid_spec=pltpu.PrefetchScalarGridSpec(num_scalar_prefetch=1, grid=(2, nb),
                                               in_specs=[whole, part, part, whole, whole], out_specs=[whole] * 4),
        out_shape=[jax.ShapeDtypeStruct(w.shape, F32)] * 4, compiler_params=_arb(2))(core, w, mine, other, m, v)


def adamw_sum(w, mine, other, m, v, name):
    R, C = w.shape[-2:]
    lead = (None,) * (w.ndim - 2)
    zeros = (0,) * (w.ndim - 2)
    tr = next(t for t in range(256, 7, -1) if R % t == 0 and t % 8 == 0)
    whole = BS(lead + (tr, C), lambda j: zeros + (j, 0))
    part = BS((tr, C), lambda j: (j, 0))

    def body(w_ref, a_ref, b_ref, m_ref, v_ref, g_ref, d_ref, mo_ref, vo_ref):
        g = a_ref[...] + b_ref[...]
        g_ref[...] = g
        d_ref[...], mo_ref[...], vo_ref[...] = _adamw_math(w_ref[...], g, m_ref[...], v_ref[...])

    return pl.pallas_call(
        body, name=name, grid=(R // tr,), in_specs=[whole, part, part, whole, whole], out_specs=[whole] * 4,
        out_shape=[jax.ShapeDtypeStruct(w.shape, F32)] * 4, compiler_params=_arb(1))(w, mine, other, m, v)


def dense_bf16(w3, name):
    R, _, K = w3.shape
    kh = K // 2

    def body(w_hbm, o_ref, buf, sem):
        cp = pltpu.make_async_copy(w_hbm.at[:, 0], buf, sem)
        cp.start()
        cp.wait()
        o_ref[0] = buf[:, :kh].astype(BF16)
        o_ref[1] = buf[:, kh:].astype(BF16)

    return pl.pallas_call(
        body, name=name, in_specs=[ANY], out_specs=BS(memory_space=pltpu.VMEM), out_shape=jax.ShapeDtypeStruct((2, R, kh), BF16),
        scratch_shapes=[pltpu.VMEM((R, K), F32), pltpu.SemaphoreType.DMA(())])(w3)


ROW_BLOCK = 184


def adamw_untiled_rows(w3, mine, other, m3, v3, name):
    R, _, K = w3.shape
    kh = K // 2
    starts = list(range(0, R, ROW_BLOCK))
    sizes = [min(ROW_BLOCK, R - s) for s in starts]
    nblk = len(starts)

    def body(w_hbm, a_ref, b_ref, m_hbm, v_hbm, g_hbm, d_hbm, mo_hbm, vo_hbm,
             wbuf, mbuf, vbuf, gbuf, dbuf, mobuf, vobuf, in_sems, out_sems):
        first = lax.axis_index("c") == 0
        ins = []
        for k, (r0, n) in enumerate(zip(starts, sizes)):
            rows = pl.ds(r0, n)
            cps = [pltpu.make_async_copy(src.at[rows, 0], dst.at[rows], in_sems.at[3 * k + i])
                   for i, (src, dst) in enumerate(((w_hbm, wbuf), (m_hbm, mbuf), (v_hbm, vbuf)))]
            for cp in cps:
                cp.start()
            ins.append(cps)

        def update(rows):
            a, b = a_ref[rows, :], b_ref[rows, :]
            g = jnp.concatenate([jnp.where(first, a, b), jnp.where(first, b, a)], axis=1)
            gbuf[rows, :] = g
            dbuf[rows, :], mobuf[rows, :], vobuf[rows, :] = _adamw_math(wbuf[rows, :], g, mbuf[rows, :], vbuf[rows, :])

        outs = []
        for k, (r0, n) in enumerate(zip(starts, sizes)):
            for cp in ins[k]:
                cp.wait()
            groups, tail = n // 8, n % 8

            def group(i, carry, r0=r0):
                update(pl.ds(pl.multiple_of(r0 + i * 8, 8), 8))
                return carry

            lax.fori_loop(0, groups, group, 0)
            if tail:
                update(pl.ds(r0 + groups * 8, tail))
            rows = pl.ds(r0, n)
            cps = [pltpu.make_async_copy(src.at[rows], dst.at[rows, 0], out_sems.at[4 * k + i])
                   for i, (src, dst) in enumerate(((gbuf, g_hbm), (dbuf, d_hbm), (mobuf, mo_hbm), (vobuf, vo_hbm)))]
            for cp in cps:
                cp.start()
            outs += cps
        for cp in outs:
            cp.wait()

    vmem = BS(memory_space=pltpu.VMEM)
    return pl.pallas_call(
        body, name=name, in_specs=[ANY, vmem, vmem, ANY, ANY], out_specs=[ANY] * 4,
        out_shape=[jax.ShapeDtypeStruct(w3.shape, F32)] * 4,
        scratch_shapes=[pltpu.VMEM((R, K), F32)] * 7 + [pltpu.SemaphoreType.DMA((3 * nblk,)), pltpu.SemaphoreType.DMA((4 * nblk,))])(
            w3, mine, other, m3, v3)


def adamw_w_q_b(w, mine, other, m, v, name):
    def body(w_ref, a_ref, b_ref, m_ref, v_ref, g_ref, d_ref, mo_ref, vo_ref):
        first = lax.axis_index("c") == 0
        lo = jnp.where(first, a_ref[...], b_ref[...])
        hi = jnp.where(first, b_ref[...], a_ref[...])
        g = jnp.concatenate([lo, hi[0:32], hi[64:96]], axis=0)
        g_ref[...] = g
        d_ref[...], mo_ref[...], vo_ref[...] = _adamw_math(w_ref[...], g, m_ref[...], v_ref[...])

    return pl.pallas_call(body, name=name, out_shape=[jax.ShapeDtypeStruct(w.shape, F32)] * 4)(w, mine, other, m, v)


def local_step(x, mem, positions, tgt, norm_in, weights, big_grads_ready, q_a_norm, kv_a_norm, gdn_conv, gdn_a_log,
               gdn_dt_bias, gdn_norm, mem_norm, norm_final):
    B, S, D = x.shape
    M = mem.shape[1]
    T = B * S
    N = S // CHUNK
    x2d = x.reshape(T, D)
    mem2d = mem.reshape(B * M, D)
    tgt2d = tgt.reshape(T, D)

    alog_row, dt_row = _lane_row(gdn_a_log), _lane_row(gdn_dt_bias)

    half = MLA_ROPE // 2
    inv_freq = 1.0 / (ROPE_THETA ** (jnp.arange(half, dtype=F32) / half))
    z32 = jnp.zeros((half,), F32)
    o32 = jnp.ones((half,), F32)
    inv_row = jnp.concatenate([inv_freq, z32, inv_freq, z32]).reshape(1, 128)
    sgn_row = jnp.concatenate([-o32, z32, o32, z32]).reshape(1, 128)
    msk_row = jnp.concatenate([o32, z32, o32, z32]).reshape(1, 128)
    cos_t, sin_t = rope_tables(positions.reshape(T, 1), inv_row, sgn_row, msk_row, after=weights[3])

    h = rms_fwd(x2d, norm_in, "rms_in", after=weights[3])
    wp_early = weights[0]((h, cos_t))
    P_early = mm(h, wp_early, "nt", F32, "in_proj_early", bm=1024, bn=OFF_GDN, n_outer=True, col_tiles=(0, 1))
    wq, wkv = weights[1](P_early)
    Q, K, V, qn, kvn = mla_prep(P_early, q_a_norm, kv_a_norm, wq, wkv, cos_t, sin_t)
    o_mla, lse = mla_attn_fwd(Q, K, V, B, S)
    wp = weights[4]((o_mla, P_early), wp_early)
    P = mm(h, wp, "nt", F32, "in_proj", bm=1024, bn=OFF_GDN, n_outer=True, col_tiles=(1, N_PAD // OFF_GDN), into=P_early)
    qkv = gdn_prep_fwd(P, gdn_conv, B, S)
    GB = gdn_gate_fwd(P, alog_row, dt_row, B, S)
    Grow = jnp.transpose(GB[:, :N_HEADS].reshape(B, N, CHUNK, N_HEADS), (0, 3, 1, 2))
    U, W, Tinv, A = gdn_chunk_fwd(qkv, GB, Grow, B, S)
    qkv3, GB3 = qkv.reshape(B, S, GDN_QKV), GB.reshape(B, S, 128)
    W3 = W.reshape(B, S, 512)
    o_gdn3, Vn3, St = gdn_scan_fwd(qkv3, U.reshape(B, S, 512), W3, GB3, A, B, S)
    o_gdn = o_gdn3.reshape(T, 512)
    w_mem_kv, w_out = weights[2](o_gdn)
    memn = rms_fwd(mem2d, mem_norm, "rms_mem")
    MKV = mm(memn, w_mem_kv, "nn", BF16, "mem_kv_proj")
    o_mem = mem_attn_fwd(P, MKV, B, S, M)
    mixed, dx2, dx2b, sq, g_norm_final = merge_fwd(o_mla, o_gdn, o_mem, P, x2d, tgt2d, w_out, gdn_norm, norm_final.reshape(1, D))

    g_w_out = mm(mixed, dx2b, "tn", BF16, "grad_w_out")
    dgate, do_mla, do_gdn, do_mem, g_gdn_norm = merge_bwd(dx2b, o_mla, o_gdn, o_mem, P, w_out, gdn_norm)

    dmemq, dMKV = mem_attn_bwd(P, MKV, do_mem, B, S, M)
    g_w_mem_kv = mm(memn, dMKV, "tn", BF16, "grad_w_mem_kv")
    started_early = big_grads_ready(dict(w_mem_kv=g_w_mem_kv, w_out=g_w_out), "early")
    dmemn = mm(dMKV, w_mem_kv, "nt", F32, "d_memn", after=(started_early,))
    g_mem_norm = gain_grad(mem2d, dmemn, "grad_mem_norm")

    dU3, dW3, dQ13, dK13, dA, dG13 = gdn_scan_bwd(do_gdn.reshape(B, S, 512), qkv3, W3, Vn3, GB3, A, St, B, S)
    r2 = lambda a: a.reshape(T, a.shape[-1])
    dqkv, dGB = gdn_chunk_bwd(qkv, GB, Grow, Tinv, dA, r2(dU3), r2(dW3), r2(dQ13), r2(dK13), r2(dG13), B, S)
    dPg, g_conv = gdn_prep_bwd(P, dqkv, gdn_conv, B, S)
    dab, g_ab = gdn_gate_bwd(P, dGB, alog_row, dt_row, B, S)

    dQ, dK, dV = mla_attn_bwd(Q, K, V, o_mla, do_mla, lse, B, S)
    dq_lin, dkv_lin, dPm, g_q_a_norm, g_kv_a_norm = mla_proj_bwd(dQ, dK, dV, cos_t, sin_t, P, dab, wq, wkv, q_a_norm, kv_a_norm)
    g_wq = mm(dq_lin, qn, "tn", BF16, "grad_w_q_b")
    g_wkv = mm(kvn, dkv_lin, "tn", BF16, "grad_w_kv_b")

    dP = [dPm, dmemq, dPg, dgate]
    g_wp = mm_cols_tn(dP, h, BF16, "grad_w_in")
    started = big_grads_ready(dict(w_in=g_wp, w_q_b=g_wq, w_kv_b=g_wkv), "late")
    grad_x, g_norm_in = in_proj_bwd(dP, wp, x2d, dx2, norm_in, started)

    grads = dict(
        norm_in=g_norm_in, q_a_norm=g_q_a_norm, kv_a_norm=g_kv_a_norm, gdn_conv=g_conv,
        gdn_a_log_dt_bias=g_ab, gdn_norm=g_gdn_norm,
        mem_norm=g_mem_norm, norm_final=g_norm_final)
    return sq, grad_x.reshape(B, S, D), grads


def kernel(x, mem, positions, norm_in, w_in, q_a_norm, w_q_b, kv_a_norm, w_kv_b, gdn_conv, gdn_a_log, gdn_dt_bias, gdn_norm, mem_norm, w_mem_kv, w_out, norm_final, loss_target, m_norm_in, m_w_in, m_q_a_norm, m_w_q_b, m_kv_a_norm, m_w_kv_b, m_gdn_conv, m_gdn_a_log, m_gdn_dt_bias, m_gdn_norm, m_mem_norm, m_w_mem_kv, m_w_out, m_norm_final, v_norm_in, v_w_in, v_q_a_norm, v_w_q_b, v_kv_a_norm, v_w_kv_b, v_gdn_conv, v_gdn_a_log, v_gdn_dt_bias, v_gdn_norm, v_mem_norm, v_w_mem_kv, v_w_out, v_norm_final):
    B = x.shape[0]
    cx, cy, cc = lax.axis_index("x"), lax.axis_index("y"), lax.axis_index("c")
    chip = 2 * cx + cy

    big_names = ("w_in", "w_q_b", "w_kv_b", "w_mem_kv", "w_out")
    rows_major = lambda a: jnp.transpose(a, (2, 0, 1))
    w_in3, m_in3, v_in3 = rows_major(w_in), rows_major(m_w_in), rows_major(v_w_in)
    w_qb_t, m_qb_t, v_qb_t = jnp.transpose(w_q_b[0]), jnp.transpose(m_w_q_b[0]), jnp.transpose(v_w_q_b[0])
    z32 = jnp.zeros((32, Q_LORA), BF16)
    qb_bf = w_qb_t.astype(BF16)
    qb_padded = jnp.concatenate([qb_bf[:160], z32, qb_bf[160:], z32])
    shards = [dense_bf16(w_in3, "w_in_bf16"), qb_padded, w_kv_b[0].astype(BF16), w_mem_kv[0].astype(BF16), w_out[0].astype(BF16)]
    splits = [(1, D_MODEL // 2)] + [(0, s.shape[0] // 2) for s in shards[1:]]
    early_row0 = jnp.where(chip == 2, EARLY_ROW0[1], EARLY_ROW0[0])
    *early_flight, early_started = half_gather_start(lax.dynamic_slice_in_dim(shards[0], early_row0, EARLY_ROWS, axis=1),
                                                     "w_in_early_gather_start", early=True)
    *late_a, started_a = late_gather_start(shards[1:3], early_started, "late_gather_qkv_start")
    *w_in_flight, w_in_started = half_gather_start(shards[0], "w_in_gather_start", after=(started_a,))
    *late_b, started_b = late_gather_start(shards[3:], w_in_started, "late_gather_mem_out_start")
    conv_all = allgather_devices(gdn_conv[0], "allgather_conv")
    conv_cols = gdn_conv.shape[2]
    conv_full = jnp.transpose(conv_all[0::2], (1, 0, 2)).reshape(GDN_CONV, N_CHIPS * conv_cols)
    late_shapes = [(N_CHIPS,) + s.shape for s in shards[1:]]

    def w_in_early_ready(after):
        g_early = pass_halves_to_sibling(half_gather_wait(*early_flight, after, "w_in_early_gather_wait", early=True),
                                         "w_in_early_gather_sibling", early=True)
        return pad_w_in_t_early(g_early)

    def w_in_ready(after, wp):
        g_in = pass_halves_to_sibling(half_gather_wait(*w_in_flight, after, "w_in_gather_wait"), "w_in_gather_sibling")
        return pad_w_in_t_rest(g_in, wp)

    def late_qkv(after):
        g_qb, g_kvb = late_gather_wait(*late_a, after, "late_gather_qkv_wait")
        return g_qb.reshape(-1, Q_LORA), _perm_w_kv_b(g_kvb)

    def late_mem_out(after):
        g_mem, g_out_w = late_gather_wait(*late_b, after, "late_gather_mem_out_wait")
        return g_mem.reshape(-1, g_mem.shape[2]), g_out_w.reshape(-1, g_out_w.shape[2])

    weights = (w_in_early_ready, late_qkv, late_mem_out, (started_b,), w_in_ready)

    core = jnp.stack([cc]).astype(jnp.int32)
    chip_core = jnp.stack([chip, cc]).astype(jnp.int32)
    exchanges = {}
    by_chip = dict(w_in=unpad_w_in_t, w_q_b=lambda a: a.reshape(late_shapes[0]), w_kv_b=_unperm_w_kv_b,
                   w_mem_kv=lambda a: a.reshape(late_shapes[2]), w_out=lambda a: a.reshape(late_shapes[3]))

    def big_grads_ready(gb, group):
        idx = [big_names.index(n) for n in gb]
        parts = [by_chip[n](a) for n, a in gb.items()]
        sp = [splits[i] for i in idx]
        if group == "early":
            sems, parts_thru, lands, token = exchange_chips_start(parts, "rs_exchange_start_" + group)
            exchanges[group] = dict(idx=idx, whole=True, sems=sems, sums=parts_thru, lands=lands)
            return token
        from_sibling = swap_sibling(parts, "rs_sibling_partial_" + group, sp)
        chip_sums = add_pairs(parts, from_sibling, sp, core, "rs_add_sibling_" + group)
        sems, sums_thru, lands, token = exchange_chips_start(chip_sums, "rs_exchange_start_" + group)
        exchanges[group] = dict(idx=idx, parts=parts, from_sibling=from_sibling, sems=sems, sums=sums_thru, lands=lands)
        return token

    sq, grad_x, g = local_step(x, mem, positions, loss_target, norm_in, weights, big_grads_ready, q_a_norm, kv_a_norm, conv_full,
                               gdn_a_log, gdn_dt_bias, gdn_norm, mem_norm, norm_final)

    small_names = ("norm_in", "q_a_norm", "kv_a_norm", "gdn_a_log", "gdn_dt_bias", "gdn_norm", "mem_norm", "norm_final")
    small = dict(norm_in=norm_in, q_a_norm=q_a_norm, kv_a_norm=kv_a_norm, gdn_a_log=gdn_a_log, gdn_dt_bias=gdn_dt_bias,
                 gdn_norm=gdn_norm, mem_norm=mem_norm, norm_final=norm_final)
    m_small = dict(norm_in=m_norm_in, q_a_norm=m_q_a_norm, kv_a_norm=m_kv_a_norm, gdn_a_log=m_gdn_a_log,
                   gdn_dt_bias=m_gdn_dt_bias, gdn_norm=m_gdn_norm, mem_norm=m_mem_norm, norm_final=m_norm_final)
    v_small = dict(norm_in=v_norm_in, q_a_norm=v_q_a_norm, kv_a_norm=v_kv_a_norm, gdn_a_log=v_gdn_a_log,
                   gdn_dt_bias=v_gdn_dt_bias, gdn_norm=v_gdn_norm, mem_norm=v_mem_norm, norm_final=v_norm_final)
    conv_rows = GDN_CONV * GDN_QKV // 1024
    g_block = pack_small([g.get(n) for n in small_names], g["gdn_a_log_dt_bias"], g["gdn_conv"], sq)
    g_block = sum_leading(allgather_devices(g_block, "allgather_small_grads"), "sum_small_grads")
    loss = 0.5 * jnp.sum(g_block[LOSS_ROW]) / D_MODEL
    g_conv = lax.dynamic_slice_in_dim(g_block[CONV_ROW0:CONV_ROW0 + conv_rows].reshape(GDN_CONV, GDN_QKV), chip * conv_cols,
                                      conv_cols, axis=1)
    as_row = lambda a: a.reshape(1, -1)
    updated = adamw_small(g_block, [as_row(small[n]) for n in small_names], [as_row(m_small[n]) for n in small_names],
                          [as_row(v_small[n]) for n in small_names])
    g_out, d_out, m_out, v_out = ({n: u[i].reshape(small[n].shape) for n, u in zip(small_names, updated)} for i in range(4))
    d_s = d_out["norm_in"]

    my_half = [None] * len(big_names)
    for group, e in exchanges.items():
        sp = [splits[i] for i in e["idx"]]
        sent, from_chips = exchange_chips_wait(e["sems"], e["sums"], e["lands"], d_s, "rs_exchange_wait_" + group)
        if e.get("whole"):
            halves = add_fours(sent, from_chips, chip_core, "rs_add_chips_" + group)
        else:
            halves = add_fives(e["parts"], e["from_sibling"], from_chips, sp, chip_core, "rs_add_chips_" + group)
        for i, a in zip(e["idx"], halves):
            my_half[i] = a
    other_half = swap_sibling(my_half, "rs_sibling_final")

    d_out["gdn_conv"], m_out["gdn_conv"], v_out["gdn_conv"] = adamw(gdn_conv, g_conv, m_gdn_conv, v_gdn_conv, "adamw_gdn_conv")
    g_out["gdn_conv"] = g_conv[None]
    res = adamw_untiled_rows(w_in3, my_half[0], other_half[0], m_in3, v_in3, "adamw_w_in")
    g_out["w_in"], d_out["w_in"], m_out["w_in"], v_out["w_in"] = [jnp.transpose(r, (1, 2, 0)) for r in res]
    res = adamw_w_q_b(w_qb_t, my_half[1], other_half[1], m_qb_t, v_qb_t, "adamw_w_q_b")
    g_out["w_q_b"], d_out["w_q_b"], m_out["w_q_b"], v_out["w_q_b"] = [jnp.transpose(r)[None] for r in res]
    rest = dict(w_kv_b=(w_kv_b, m_w_kv_b, v_w_kv_b), w_mem_kv=(w_mem_kv, m_w_mem_kv, v_w_mem_kv), w_out=(w_out, m_w_out, v_w_out))
    for i, n in enumerate(big_names):
        if n in rest:
            w_n, m_n, v_n = rest[n]
            if any(e.get("whole") and i in e["idx"] for e in exchanges.values()):
                g_out[n], d_out[n], m_out[n], v_out[n] = adamw_sum(w_n, my_half[i], other_half[i], m_n, v_n, "adamw_" + n)
            else:
                g_out[n], d_out[n], m_out[n], v_out[n] = adamw_halves(w_n, my_half[i], other_half[i], m_n, v_n, splits[i], core,
                                                                      "adamw_" + n)

    order = ("norm_in", "w_in", "q_a_norm", "w_q_b", "kv_a_norm", "w_kv_b", "gdn_conv", "gdn_a_log", "gdn_dt_bias",
             "gdn_norm", "mem_norm", "w_mem_kv", "w_out", "norm_final")
    return (loss, grad_x, *[g_out[n] for n in order], *[d_out[n] for n in order], *[m_out[n] for n in order],
            *[v_out[n] for n in order])
```

```python
import jax
import jax.numpy as jnp
from jax import lax
from jax.experimental import pallas as pl
from jax.experimental.pallas import tpu as pltpu

F32 = jnp.float32
BF16 = jnp.bfloat16
BS = pl.BlockSpec

D_MODEL = 1024
N_HEADS = 4
MLA_NOPE, MLA_ROPE, MLA_V = 128, 64, 128
Q_LORA, KV_LORA = 384, 256
ROPE_THETA = 10000.0
GDN_DK = GDN_DV = 128
GDN_CONV = 4
CHUNK = 64
MEM_DH = 128
D_MIX = 1536
GDN_QKV = 1536
D_IN = 4296
EPS = 1e-6
ADAM_LR, ADAM_B1, ADAM_B2, ADAM_EPS, ADAM_WD, ADAM_STEP = 0.001, 0.9, 0.999, 1e-08, 0.01, 10

OFF_MLA = 0
OFF_MEMQ = 1024
OFF_GDN = 1536
OFF_GATE = 3072
N_PAD = 4608
HEAD_PAD = 256
MLA_SCALE = (MLA_NOPE + MLA_ROPE) ** -0.5
MEM_SCALE = MEM_DH ** -0.5
GDN_SCALE = GDN_DK ** -0.5
NEG = -1e30

NN = ((1,), (0,))
NT = ((1,), (1,))
TN = ((0,), (0,))


def _dot(a, b, dims):
    return lax.dot_general(a, b, (dims, ((), ())), preferred_element_type=F32)


def _bdot(spec, a, b, precision=None):
    return jnp.einsum(spec, a, b, preferred_element_type=F32, precision=precision)


def _arb(n):
    return pltpu.CompilerParams(dimension_semantics=("arbitrary",) * n)


def _sigmoid(x):
    return 1.0 / (1.0 + jnp.exp(-x))


def _softplus(z):
    return jnp.maximum(z, 0.0) + jnp.log(1.0 + jnp.exp(-jnp.abs(z)))


def _rope(t, cos_row, sin_row):
    return t * cos_row + pltpu.roll(t, 64, 1) * sin_row


def _rope_bwd(d, cos_row, sin_row):
    return d * cos_row + pltpu.roll(d * sin_row, 64, 1)


def rms_fwd(x, gain, name, tm=512, after=()):
    T, n = x.shape
    tm = min(tm, T)

    def body(x_ref, g_ref, *rest):
        xv = x_ref[...]
        r = lax.rsqrt(jnp.mean(xv * xv, axis=-1, keepdims=True) + EPS)
        rest[-1][...] = (xv * r * g_ref[...]).astype(BF16)

    return pl.pallas_call(
        body, name=name, grid=(T // tm,),
        in_specs=[BS((tm, n), lambda i: (i, 0)), BS((1, n), lambda i: (0, 0))] + [BS(memory_space=pl.ANY)] * len(after),
        out_specs=BS((tm, n), lambda i: (i, 0)),
        out_shape=jax.ShapeDtypeStruct((T, n), BF16), compiler_params=_arb(1))(x, gain, *after)


def mm(a, b, kind, out_dtype, name, bm=512, bn=None, n_outer=False, after=(), col_tiles=None, into=None):
    if kind == "nn":
        (M, K), (_, N) = a.shape, b.shape
    elif kind == "nt":
        (M, K), (N, _) = a.shape, b.shape
    else:
        (K, M), (_, N) = a.shape, b.shape
    bm, bn = min(bm, M), min(bn or N, N)
    assert M % bm == 0 and N % bn == 0, (name, M, N, K)
    lo, hi = col_tiles or (0, N // bn)
    ij = (lambda g0, g1: (g1, g0 + lo)) if n_outer else (lambda g0, g1: (g0, g1 + lo))
    a_spec = BS((K, bm), lambda g0, g1: (0, ij(g0, g1)[0])) if kind == "tn" else BS((bm, K), lambda g0, g1: (ij(g0, g1)[0], 0))
    once = dict(pipeline_mode=pl.Buffered(1)) if hi - lo == 1 else {}
    b_spec = (BS((bn, K), lambda g0, g1: (ij(g0, g1)[1], 0), **once) if kind == "nt"
              else BS((K, bn), lambda g0, g1: (0, ij(g0, g1)[1]), **once))
    dims = {"nn": NN, "nt": NT, "tn": TN}[kind]
    extra = tuple(after) + (() if into is None else (into,))

    def body(a_ref, b_ref, *rest):
        rest[-1][...] = _dot(a_ref[...].astype(BF16), b_ref[...].astype(BF16), dims).astype(out_dtype)

    grid = (hi - lo, M // bm) if n_outer else (M // bm, hi - lo)
    return pl.pallas_call(
        body, name=name, grid=grid, in_specs=[a_spec, b_spec] + [BS(memory_space=pl.ANY)] * len(extra),
        out_specs=BS((bm, bn), lambda g0, g1: ij(g0, g1)),
        input_output_aliases={} if into is None else {1 + len(extra): 0},
        out_shape=jax.ShapeDtypeStruct((M, N), out_dtype), compiler_params=_arb(2))(a, b, *extra)


def mm_cols_tn(pieces, b, out_dtype, name, bm=512):
    K, N = b.shape
    tiles = [p.shape[1] // bm for p in pieces]
    firsts = [sum(tiles[:i]) for i in range(len(tiles))]

    def body(*refs):
        b_ref, o_ref = refs[-2], refs[-1]
        i = pl.program_id(0)
        for a_ref, t0, n in zip(refs[:-2], firsts, tiles):
            @pl.when((i >= t0) & (i < t0 + n))
            def _(a_ref=a_ref):
                o_ref[...] = _dot(a_ref[...], b_ref[...], TN).astype(out_dtype)

    a_specs = [BS((K, bm), lambda i, t0=t0, n=n: (0, jnp.clip(i - t0, 0, n - 1))) for t0, n in zip(firsts, tiles)]
    return pl.pallas_call(
        body, name=name, grid=(sum(tiles),),
        in_specs=a_specs + [BS(b.shape, lambda i: (0, 0), pipeline_mode=pl.Buffered(1))],
        out_specs=BS((bm, N), lambda i: (i, 0)), out_shape=jax.ShapeDtypeStruct((sum(tiles) * bm, N), out_dtype),
        compiler_params=_arb(1))(*pieces, b)


def rope_tables(pos_col, inv_row, sgn_row, msk_row, tm=512, after=()):
    T = pos_col.shape[0]
    tm = min(tm, T)

    def body(p_ref, inv_ref, sgn_ref, msk_ref, *rest):
        c_ref, s_ref = rest[-2:]
        ang = p_ref[...].astype(F32) * inv_ref[...]
        c_ref[...] = jnp.cos(ang) * msk_ref[...]
        s_ref[...] = jnp.sin(ang) * sgn_ref[...]

    row = BS((1, 128), lambda i: (0, 0))
    return pl.pallas_call(
        body, name="rope_tables", grid=(T // tm,),
        in_specs=[BS((tm, 1), lambda i: (i, 0)), row, row, row] + [BS(memory_space=pl.ANY)] * len(after),
        out_specs=[BS((tm, 128), lambda i: (i, 0))] * 2,
        out_shape=[jax.ShapeDtypeStruct((T, 128), F32)] * 2, compiler_params=_arb(1))(pos_col, inv_row, sgn_row, msk_row, *after)


def mla_prep(P, gq, gkv, wq, wkv, cos_t, sin_t, tm=512):
    T = P.shape[0]
    tm = min(tm, T)

    def body(p_ref, gq_ref, gkv_ref, wq_ref, wkv_ref, c_ref, s_ref, q_ref, k_ref, v_ref, qn_ref, kvn_ref):
        p = p_ref[...].astype(F32)
        cq, ckv, kr = p[:, :Q_LORA], p[:, Q_LORA:Q_LORA + KV_LORA], p[:, 640:768]
        qn = (cq * lax.rsqrt(jnp.mean(cq * cq, axis=-1, keepdims=True) + EPS) * gq_ref[...]).astype(BF16)
        kvn = (ckv * lax.rsqrt(jnp.mean(ckv * ckv, axis=-1, keepdims=True) + EPS) * gkv_ref[...]).astype(BF16)
        qn_ref[...] = qn
        kvn_ref[...] = kvn
        q = _dot(qn, wq_ref[...], NT)
        kv = _dot(kvn, wkv_ref[...], NN)
        cos_row, sin_row = c_ref[...], s_ref[...]
        krr = _rope(kr, cos_row, sin_row).astype(BF16)
        for h in range(N_HEADS):
            lo = h * HEAD_PAD
            q_ref[:, lo:lo + 128] = (q[:, lo:lo + 128] * MLA_SCALE).astype(BF16)
            q_ref[:, lo + 128:lo + 256] = (_rope(q[:, lo + 128:lo + 256], cos_row, sin_row) * MLA_SCALE).astype(BF16)
            k_ref[:, lo:lo + 128] = kv[:, h * 128:(h + 1) * 128].astype(BF16)
            k_ref[:, lo + 128:lo + 256] = krr
            v_ref[:, lo:lo + 128] = kv[:, 512 + h * 128:512 + (h + 1) * 128].astype(BF16)
            v_ref[:, lo + 128:lo + 256] = jnp.ones((tm, 128), BF16)

    full = lambda r, c: BS((r, c), lambda i: (0, 0))
    rowb = lambda c: BS((tm, c), lambda i: (i, 0))
    return pl.pallas_call(
        body, name="mla_prep", grid=(T // tm,),
        in_specs=[rowb(1024), full(1, Q_LORA), full(1, KV_LORA), full(1024, Q_LORA), full(KV_LORA, 1024), rowb(128), rowb(128)],
        out_specs=[rowb(1024), rowb(1024), rowb(1024), rowb(Q_LORA), rowb(KV_LORA)],
        out_shape=[jax.ShapeDtypeStruct((T, 1024), BF16), jax.ShapeDtypeStruct((T, 1024), BF16),
                   jax.ShapeDtypeStruct((T, 1024), BF16), jax.ShapeDtypeStruct((T, Q_LORA), BF16),
                   jax.ShapeDtypeStruct((T, KV_LORA), BF16)],
        compiler_params=_arb(1))(P, gq, gkv, wq, wkv, cos_t, sin_t)


ATTN_HEADS_PER_STEP = 2
ATTN_STRIP = 32


def mla_attn_fwd(Q, K, V, B, S, tq=512, hp=ATTN_HEADS_PER_STEP):
    T = B * S
    tq = min(tq, S)
    nq = S // tq

    rs = min(ATTN_STRIP, tq)

    def body(q_ref, k_ref, v_ref, o_ref, lse_ref, m_s, acc_s, s_s, p_s, a_s):
        i = pl.program_id(2)
        m_s[...] = jnp.full_like(m_s, NEG)
        acc_s[...] = jnp.zeros_like(acc_s)

        def blk(j, masked):
            rows = pl.ds(pl.multiple_of(j * tq, tq), tq)
            for h in range(hp):
                hq = slice(h * HEAD_PAD, (h + 1) * HEAD_PAD)
                s_s[h] = _dot(q_ref[:, hq], k_ref[rows, hq], NT)
            for h in range(hp):
                for r0 in range(0, tq, rs):
                    rr = slice(r0, r0 + rs)
                    sv = s_s[h, rr, :]
                    if masked:
                        r = r0 + lax.broadcasted_iota(jnp.int32, (rs, tq), 0)
                        c = lax.broadcasted_iota(jnp.int32, (rs, tq), 1)
                        sv = jnp.where(r >= c, sv, NEG)
                    m_prev = m_s[h, rr, :]
                    m_new = jnp.maximum(m_prev, jnp.max(sv, axis=1, keepdims=True))
                    p_s[h, rr, :] = jnp.exp(sv - m_new).astype(BF16)
                    a_s[h, rr, :] = jnp.exp(m_prev - m_new)
                    m_s[h, rr, :] = m_new
            for h in range(hp):
                hq = slice(h * HEAD_PAD, (h + 1) * HEAD_PAD)
                acc_s[h] = a_s[h] * acc_s[h] + _dot(p_s[h], v_ref[rows, hq], NN)

        def loop(j, c):
            blk(j, False)
            return c

        lax.fori_loop(0, i, loop, 0)
        blk(i, True)
        for h in range(hp):
            den = acc_s[h, :, 128:256]
            o_ref[:, h * 128:(h + 1) * 128] = (acc_s[h, :, 0:128] / den).astype(BF16)
            lse_ref[h] = m_s[h] + jnp.log(den[:, 0:1])

    return pl.pallas_call(
        body, name="mla_attn_fwd", grid=(B, N_HEADS // hp, nq),
        in_specs=[BS((tq, hp * HEAD_PAD), lambda b, h, i: (b * nq + i, h)),
                  BS((S, hp * HEAD_PAD), lambda b, h, i: (b, h)),
                  BS((S, hp * HEAD_PAD), lambda b, h, i: (b, h))],
        out_specs=[BS((tq, hp * 128), lambda b, h, i: (b * nq + i, h)),
                   BS((hp, tq, 1), lambda b, h, i: (h, b * nq + i, 0))],
        out_shape=[jax.ShapeDtypeStruct((T, 512), BF16), jax.ShapeDtypeStruct((N_HEADS, T, 1), F32)],
        scratch_shapes=[pltpu.VMEM((hp, tq, 1), F32), pltpu.VMEM((hp, tq, HEAD_PAD), F32), pltpu.VMEM((hp, tq, tq), F32),
                        pltpu.VMEM((hp, tq, tq), BF16), pltpu.VMEM((hp, tq, 1), F32)],
        compiler_params=_arb(3))(Q, K, V)


def mla_attn_bwd(Q, K, V, O, dO, LSE, B, S, tq=512, hp=ATTN_HEADS_PER_STEP):
    T = B * S
    tq = min(tq, S)
    nq = S // tq

    rs = min(ATTN_STRIP, tq)

    def body(q_ref, k_ref, v_ref, o_ref, do_ref, lse_ref, dq_ref, dk_ref, dv_ref, delta_s, dq_s, dk_s, dv_s, s_s, dp_s, p_s, ds_s):
        j = pl.program_id(2)

        @pl.when(j == 0)
        def _():
            dq_s[...] = jnp.zeros_like(dq_s)
            for h in range(hp):
                sl = slice(h * 128, (h + 1) * 128)
                delta_s[h] = jnp.sum(do_ref[:, sl] * o_ref[:, sl].astype(F32), axis=1, keepdims=True)

        dk_s[...] = jnp.zeros_like(dk_s)
        dv_s[...] = jnp.zeros_like(dv_s)

        def step(i, c):
            rows = pl.ds(pl.multiple_of(i * tq, tq), tq)
            for h in range(hp):
                sq, sv = slice(h * HEAD_PAD, (h + 1) * HEAD_PAD), slice(h * 128, (h + 1) * 128)
                s_s[h] = _dot(q_ref[rows, sq], k_ref[:, sq], NT)
                dp_s[h] = _dot(do_ref[rows, sv].astype(BF16), v_ref[:, h * HEAD_PAD:h * HEAD_PAD + 128], NT)
            for h in range(hp):
                for r0 in range(0, tq, rs):
                    rr = slice(r0, r0 + rs)
                    seq_rows = pl.ds(pl.multiple_of(i * tq + r0, rs), rs)
                    r = i * tq + r0 + lax.broadcasted_iota(jnp.int32, (rs, tq), 0)
                    cc = j * tq + lax.broadcasted_iota(jnp.int32, (rs, tq), 1)
                    p = jnp.where(r >= cc, jnp.exp(s_s[h, rr, :] - lse_ref[h, seq_rows, :]), 0.0)
                    p_s[h, rr, :] = p.astype(BF16)
                    ds_s[h, rr, :] = (p * (dp_s[h, rr, :] - delta_s[h, seq_rows, :])).astype(BF16)
            for h in range(hp):
                sq, sv = slice(h * HEAD_PAD, (h + 1) * HEAD_PAD), slice(h * 128, (h + 1) * 128)
                dv_s[:, sv] += _dot(p_s[h], do_ref[rows, sv].astype(BF16), TN)
                dk_s[:, sq] += _dot(ds_s[h], q_ref[rows, sq], TN)
                dq_s[rows, sq] += _dot(ds_s[h], k_ref[:, sq], NN)
            return c

        lax.fori_loop(j, nq, step, 0)
        dk_ref[...] = dk_s[...].astype(BF16)
        dv_ref[...] = dv_s[...].astype(BF16)

        @pl.when(j == nq - 1)
        def _():
            dq_ref[...] = dq_s[...].astype(BF16)

    seq = lambda c: BS((S, c), lambda b, h, j: (b, h))
    blk = lambda c: BS((tq, c), lambda b, h, j: (b * nq + j, h))
    return pl.pallas_call(
        body, name="mla_attn_bwd", grid=(B, N_HEADS // hp, nq),
        in_specs=[seq(hp * HEAD_PAD), blk(hp * HEAD_PAD), blk(hp * HEAD_PAD), seq(hp * 128), seq(hp * 128),
                  BS((hp, S, 1), lambda b, h, j: (h, b, 0))],
        out_specs=[seq(hp * HEAD_PAD), blk(hp * HEAD_PAD), blk(hp * 128)],
        out_shape=[jax.ShapeDtypeStruct((T, 1024), BF16), jax.ShapeDtypeStruct((T, 1024), BF16),
                   jax.ShapeDtypeStruct((T, 512), BF16)],
        scratch_shapes=[pltpu.VMEM((hp, S, 1), F32), pltpu.VMEM((S, hp * HEAD_PAD), F32),
                        pltpu.VMEM((tq, hp * HEAD_PAD), F32), pltpu.VMEM((tq, hp * 128), F32),
                        pltpu.VMEM((hp, tq, tq), F32), pltpu.VMEM((hp, tq, tq), F32),
                        pltpu.VMEM((hp, tq, tq), BF16), pltpu.VMEM((hp, tq, tq), BF16)],
        compiler_params=_arb(3))(Q, K, V, O, dO, LSE)


def mla_proj_bwd(dQ, dK, dV, cos_t, sin_t, P, dab, wq, wkv, gq, gkv, tm=512):
    T = P.shape[0]
    tm = min(tm, T)

    def norm_bwd(x, dy, g):
        r = lax.rsqrt(jnp.mean(x * x, axis=-1, keepdims=True) + EPS)
        xh = x * r
        dxh = dy * g
        return r * (dxh - xh * jnp.mean(dxh * xh, axis=-1, keepdims=True)), jnp.sum(dy * xh, axis=0, keepdims=True)

    def body(dq_ref, dk_ref, dv_ref, c_ref, s_ref, p_ref, dab_ref, wq_ref, wkv_ref, gq_ref, gkv_ref,
             ql_ref, kvl_ref, o_ref, aq_ref, akv_ref):
        @pl.when(pl.program_id(0) == 0)
        def _():
            aq_ref[...] = jnp.zeros_like(aq_ref)
            akv_ref[...] = jnp.zeros_like(akv_ref)

        cos_row, sin_row = c_ref[...], s_ref[...]
        kr = jnp.zeros((tm, 128), F32)
        for h in range(N_HEADS):
            lo = h * HEAD_PAD
            ql_ref[:, lo:lo + 128] = (dq_ref[:, lo:lo + 128].astype(F32) * MLA_SCALE).astype(BF16)
            ql_ref[:, lo + 128:lo + 256] = (_rope_bwd(dq_ref[:, lo + 128:lo + 256].astype(F32), cos_row, sin_row) * MLA_SCALE).astype(BF16)
            kvl_ref[:, h * 128:(h + 1) * 128] = dk_ref[:, lo:lo + 128]
            kr = kr + dk_ref[:, lo + 128:lo + 256].astype(F32)
        kvl_ref[:, 512:] = dv_ref[...]
        dqn = _dot(ql_ref[...], wq_ref[...], NN)
        dkvn = _dot(kvl_ref[...], wkv_ref[...], NT)
        dcq, ggq = norm_bwd(p_ref[:, :Q_LORA].astype(F32), dqn, gq_ref[...])
        dckv, ggkv = norm_bwd(p_ref[:, Q_LORA:640].astype(F32), dkvn, gkv_ref[...])
        aq_ref[...] += ggq
        akv_ref[...] += ggkv
        o_ref[:, :Q_LORA] = dcq.astype(BF16)
        o_ref[:, Q_LORA:640] = dckv.astype(BF16)
        o_ref[:, 640:768] = _rope_bwd(kr, cos_row, sin_row).astype(BF16)
        o_ref[:, 768:896] = dab_ref[...]
        o_ref[:, 896:1024] = jnp.zeros((tm, 128), BF16)

    rowb = lambda c: BS((tm, c), lambda i: (i, 0))
    full = lambda r, c: BS((r, c), lambda i: (0, 0))
    return pl.pallas_call(
        body, name="mla_proj_bwd", grid=(T // tm,),
        in_specs=[rowb(1024), rowb(1024), rowb(512), rowb(128), rowb(128), rowb(1024), rowb(128),
                  full(1024, Q_LORA), full(KV_LORA, 1024), full(1, Q_LORA), full(1, KV_LORA)],
        out_specs=[rowb(1024), rowb(1024), rowb(1024), full(1, Q_LORA), full(1, KV_LORA)],
        out_shape=[jax.ShapeDtypeStruct((T, 1024), BF16)] * 3
        + [jax.ShapeDtypeStruct((1, Q_LORA), F32), jax.ShapeDtypeStruct((1, KV_LORA), F32)],
        compiler_params=_arb(1))(dQ, dK, dV, cos_t, sin_t, P, dab, wq, wkv, gq, gkv)


def _mem_probs(qh, kh):
    s = _dot(qh, kh, NT) * MEM_SCALE
    p = jnp.exp(s - jnp.max(s, axis=1, keepdims=True))
    return p / jnp.sum(p, axis=1, keepdims=True)


def mem_attn_fwd(P, MKV, B, S, M, tq=512):
    T = B * S
    tq = min(tq, S)
    nq = S // tq

    def body(q_ref, kv_ref, o_ref):
        for h in range(N_HEADS):
            sl = slice(h * 128, (h + 1) * 128)
            p = _mem_probs(q_ref[:, sl].astype(BF16), kv_ref[:, sl])
            o_ref[:, sl] = _dot(p.astype(BF16), kv_ref[:, 512 + h * 128:512 + (h + 1) * 128], NN).astype(BF16)

    return pl.pallas_call(
        body, name="mem_attn_fwd", grid=(B, nq),
        in_specs=[BS((tq, 512), lambda b, i: (b * nq + i, OFF_MEMQ // 512)), BS((M, 1024), lambda b, i: (b, 0))],
        out_specs=BS((tq, 512), lambda b, i: (b * nq + i, 0)),
        out_shape=jax.ShapeDtypeStruct((T, 512), BF16), compiler_params=_arb(2))(P, MKV)


def mem_attn_bwd(P, MKV, dO, B, S, M, tq=512):
    T = B * S
    tq = min(tq, S)
    nq = S // tq

    def body(q_ref, kv_ref, do_ref, dq_ref, dkv_ref):
        @pl.when(pl.program_id(1) == 0)
        def _():
            dkv_ref[...] = jnp.zeros_like(dkv_ref)

        for h in range(N_HEADS):
            sl = slice(h * 128, (h + 1) * 128)
            sv = slice(512 + h * 128, 512 + (h + 1) * 128)
            qh = q_ref[:, sl].astype(BF16)
            kh = kv_ref[:, sl]
            do = do_ref[:, sl].astype(BF16)
            p = _mem_probs(qh, kh)
            dkv_ref[:, sv] += _dot(p.astype(BF16), do, TN)
            dp = _dot(do, kv_ref[:, sv], NT)
            ds = (p * (dp - jnp.sum(dp * p, axis=1, keepdims=True)) * MEM_SCALE).astype(BF16)
            dq_ref[:, sl] = _dot(ds, kh, NN).astype(BF16)
            dkv_ref[:, sl] += _dot(ds, qh, TN)

    return pl.pallas_call(
        body, name="mem_attn_bwd", grid=(B, nq),
        in_specs=[BS((tq, 512), lambda b, i: (b * nq + i, OFF_MEMQ // 512)), BS((M, 1024), lambda b, i: (b, 0)),
                  BS((tq, 512), lambda b, i: (b * nq + i, 0))],
        out_specs=[BS((tq, 512), lambda b, i: (b * nq + i, 0)), BS((M, 1024), lambda b, i: (b, 0))],
        out_shape=[jax.ShapeDtypeStruct((T, 512), BF16), jax.ShapeDtypeStruct((B * M, 1024), F32)],
        compiler_params=_arb(2))(P, MKV, dO)


def gain_grad(x, dy, name, tm=256):
    T, n = x.shape
    tm = min(tm, T)

    def body(x_ref, dy_ref, o_ref):
        @pl.when(pl.program_id(0) == 0)
        def _():
            o_ref[...] = jnp.zeros_like(o_ref)

        xv = x_ref[...]
        xh = xv * lax.rsqrt(jnp.mean(xv * xv, axis=-1, keepdims=True) + EPS)
        o_ref[...] += jnp.sum(dy_ref[...] * xh, axis=0, keepdims=True)

    return pl.pallas_call(
        body, name=name, grid=(T // tm,),
        in_specs=[BS((tm, n), lambda i: (i, 0))] * 2, out_specs=BS((1, n), lambda i: (0, 0)),
        out_shape=jax.ShapeDtypeStruct((1, n), F32), compiler_params=_arb(1))(x, dy)


def _conv_silu(x, w, t):
    y = x * w[3:4, :]
    for s in range(1, GDN_CONV):
        y = y + jnp.where(t >= s, pltpu.roll(x, s, 0), 0.0) * w[3 - s:4 - s, :]
    return y, _sigmoid(y)


def gdn_prep_fwd(P, conv_w, B, S):
    T = B * S

    def body(x_ref, w_ref, o_ref):
        kind = pl.program_id(1)
        t = lax.broadcasted_iota(jnp.int32, (S, 1), 0)
        y, sg = _conv_silu(x_ref[...].astype(F32), w_ref[...], t)
        a = y * sg
        scale = jnp.where(kind == 0, GDN_SCALE, 1.0).astype(F32)
        for h in range(N_HEADS):
            sl = slice(h * 128, (h + 1) * 128)
            seg = a[:, sl]
            n = lax.rsqrt(jnp.sum(seg * seg, axis=-1, keepdims=True) + EPS)
            o_ref[:, sl] = jnp.where(kind < 2, seg * (n * scale), seg)

    return pl.pallas_call(
        body, name="gdn_prep_fwd", grid=(B, 3),
        in_specs=[BS((S, 512), lambda b, k: (b, OFF_GDN // 512 + k)), BS((GDN_CONV, 512), lambda b, k: (0, k))],
        out_specs=BS((S, 512), lambda b, k: (b, k)),
        out_shape=jax.ShapeDtypeStruct((T, GDN_QKV), F32), compiler_params=_arb(2))(P, conv_w)


def gdn_prep_bwd(P, dqkv, conv_w, B, S):
    T = B * S

    def body(x_ref, d_ref, w_ref, o_ref, gw_ref):
        kind = pl.program_id(0)

        @pl.when(pl.program_id(1) == 0)
        def _():
            gw_ref[...] = jnp.zeros_like(gw_ref)

        t = lax.broadcasted_iota(jnp.int32, (S, 1), 0)
        x = x_ref[...].astype(F32)
        w = w_ref[...]
        y, sg = _conv_silu(x, w, t)
        a = y * sg
        scale = jnp.where(kind == 0, GDN_SCALE, 1.0).astype(F32)
        das = []
        for h in range(N_HEADS):
            sl = slice(h * 128, (h + 1) * 128)
            seg, dseg = a[:, sl], d_ref[:, sl]
            n = lax.rsqrt(jnp.sum(seg * seg, axis=-1, keepdims=True) + EPS)
            dn = scale * (n * dseg - seg * (n * n * n) * jnp.sum(dseg * seg, axis=-1, keepdims=True))
            das.append(jnp.where(kind < 2, dn, dseg))
        dy = jnp.concatenate(das, axis=1) * (sg * (1.0 + y * (1.0 - sg)))
        dx = dy * w[3:4, :]
        gw_ref[3:4, :] += jnp.sum(dy * x, axis=0, keepdims=True)
        for s in range(1, GDN_CONV):
            dx = dx + jnp.where(t + s < S, pltpu.roll(dy, S - s, 0), 0.0) * w[3 - s:4 - s, :]
            gw_ref[3 - s:4 - s, :] += jnp.sum(dy * jnp.where(t >= s, pltpu.roll(x, s, 0), 0.0), axis=0, keepdims=True)
        o_ref[...] = dx.astype(BF16)

    return pl.pallas_call(
        body, name="gdn_prep_bwd", grid=(3, B),
        in_specs=[BS((S, 512), lambda k, b: (b, OFF_GDN // 512 + k)), BS((S, 512), lambda k, b: (b, k)),
                  BS((GDN_CONV, 512), lambda k, b: (0, k))],
        out_specs=[BS((S, 512), lambda k, b: (b, k)), BS((GDN_CONV, 512), lambda k, b: (0, k))],
        out_shape=[jax.ShapeDtypeStruct((T, GDN_QKV), BF16), jax.ShapeDtypeStruct((GDN_CONV, GDN_QKV), F32)],
        compiler_params=_arb(2))(P, dqkv, conv_w)


def _chunk_row(n_rows):
    return lax.broadcasted_iota(jnp.int32, (n_rows, 1), 0) % CHUNK


def gdn_gate_fwd(P, alog_row, dt_row, B, S):
    T = B * S

    def body(x_ref, al_ref, dt_ref, o_ref):
        x = x_ref[...].astype(F32)
        lane = lax.broadcasted_iota(jnp.int32, (1, 128), 1)
        g = jnp.where(lane < 4, -jnp.exp(al_ref[...]) * _softplus(x + dt_ref[...]), 0.0)
        t = _chunk_row(S)
        for s in (1, 2, 4, 8, 16, 32):
            g = g + jnp.where(t >= s, pltpu.roll(g, s, 0), 0.0)
        o_ref[...] = jnp.where(lane < 4, g, jnp.where(lane < 8, _sigmoid(x), 0.0))

    row = BS((1, 128), lambda b: (0, 0))
    return pl.pallas_call(
        body, name="gdn_gate_fwd", grid=(B,),
        in_specs=[BS((S, 128), lambda b: (b, 768 // 128)), row, row], out_specs=BS((S, 128), lambda b: (b, 0)),
        out_shape=jax.ShapeDtypeStruct((T, 128), F32), compiler_params=_arb(1))(P, alog_row, dt_row)


def gdn_gate_bwd(P, dGB, alog_row, dt_row, B, S):
    T = B * S

    def body(x_ref, d_ref, al_ref, dt_ref, o_ref, acc_ref):
        @pl.when(pl.program_id(0) == 0)
        def _():
            acc_ref[...] = jnp.zeros_like(acc_ref)

        x, d = x_ref[...].astype(F32), d_ref[...]
        lane = lax.broadcasted_iota(jnp.int32, (1, 128), 1)
        z = x + dt_ref[...]
        coef = -jnp.exp(al_ref[...])
        g = coef * _softplus(z)
        da = jnp.where(lane < 4, d * coef * _sigmoid(z), 0.0)
        beta = _sigmoid(x)
        o_ref[...] = jnp.where(lane < 4, da, jnp.where(lane < 8, d * beta * (1.0 - beta), 0.0)).astype(BF16)
        acc_ref[0:1, :] += jnp.sum(jnp.where(lane < 4, d * g, 0.0), axis=0, keepdims=True)
        acc_ref[1:2, :] += jnp.sum(da, axis=0, keepdims=True)

    row = BS((1, 128), lambda b: (0, 0))
    return pl.pallas_call(
        body, name="gdn_gate_bwd", grid=(B,),
        in_specs=[BS((S, 128), lambda b: (b, 768 // 128)), BS((S, 128), lambda b: (b, 0)), row, row],
        out_specs=[BS((S, 128), lambda b: (b, 0)), BS((8, 128), lambda b: (0, 0))],
        out_shape=[jax.ShapeDtypeStruct((T, 128), BF16), jax.ShapeDtypeStruct((8, 128), F32)],
        compiler_params=_arb(1))(P, dGB, alog_row, dt_row)


def _chunk_masks(nc):
    r = lax.broadcasted_iota(jnp.int32, (nc, CHUNK, CHUNK), 1)
    c = lax.broadcasted_iota(jnp.int32, (nc, CHUNK, CHUNK), 2)
    return r >= c, r > c


def _chunk_local(q, k, gc, gr, beta, incl, strict):
    decay = jnp.exp(jnp.where(incl, gc - gr, NEG))
    kb = k * beta
    kbf = k.astype(BF16)
    m_kk = _bdot("gcd,gjd->gcj", kb.astype(BF16), kbf)
    l_mat = jnp.where(strict, m_kk * decay, 0.0)
    a_mat = _bdot("gcd,gjd->gcj", q.astype(BF16), kbf) * decay
    return decay, kb, l_mat, a_mat


WY_SPLIT_LEVELS = 2


def _split_bf16(x):
    hi = x.astype(BF16)
    return hi, (x - hi.astype(F32)).astype(BF16)


def _mm_split(ah, al, bh, bl):
    spec = "gij,gjk->gik"
    return _bdot(spec, ah, bh) + (_bdot(spec, ah, bl) + _bdot(spec, al, bh))


def gdn_chunk_fwd(qkv, GB, Grow, B, S, nc=8):
    T = B * S
    N = S // CHUNK
    nc = min(nc, N)
    nb = N // nc
    R = nc * CHUNK

    def body(q_ref, k_ref, v_ref, gb_ref, gr_ref, u_ref, w_ref, t_ref, a_ref):
        incl, strict = _chunk_masks(nc)
        eye = (lax.broadcasted_iota(jnp.int32, (nc, CHUNK, CHUNK), 1)
               == lax.broadcasted_iota(jnp.int32, (nc, CHUNK, CHUNK), 2)).astype(F32)
        for h in range(N_HEADS):
            sl = slice(h * 128, (h + 1) * 128)
            q = q_ref[:, sl].reshape(nc, CHUNK, 128)
            k = k_ref[:, sl].reshape(nc, CHUNK, 128)
            v = v_ref[:, sl].reshape(nc, CHUNK, 128)
            gc = gb_ref[:, h:h + 1].reshape(nc, CHUNK, 1)
            beta = gb_ref[:, 4 + h:5 + h].reshape(nc, CHUNK, 1)
            gr = gr_ref[h][:, None, :]
            _, kb, l_mat, a_mat = _chunk_local(q, k, gc, gr, beta, incl, strict)
            pw = -l_mat
            tinv = eye + pw
            for level in range(5):
                if level < WY_SPLIT_LEVELS:
                    ph, pl_ = _split_bf16(pw)
                    pw = _mm_split(ph, pl_, ph, pl_)
                    ph, pl_ = _split_bf16(pw)
                    th, tl = _split_bf16(tinv)
                    tinv = tinv + _mm_split(th, tl, ph, pl_)
                else:
                    ph = pw.astype(BF16)
                    pw = _bdot("gij,gjk->gik", ph, ph)
                    tinv = tinv + _bdot("gij,gjk->gik", tinv.astype(BF16), pw.astype(BF16))
            tb = tinv.astype(BF16)
            u = _bdot("gcj,gjv->gcv", tb, (v * beta).astype(BF16))
            w = _bdot("gcj,gjk->gck", tb, (kb * jnp.exp(gc)).astype(BF16))
            u_ref[:, sl] = u.reshape(R, 128)
            w_ref[:, sl] = w.reshape(R, 128).astype(BF16)
            t_ref[h] = jnp.swapaxes(tinv, 1, 2).astype(BF16)
            a_ref[h] = a_mat.astype(BF16)

    rowb = lambda c, j: BS((R, c), lambda b, n: (b * nb + n, j))
    mat = BS((None, N_HEADS, nc, CHUNK, CHUNK), lambda b, n: (b, 0, n, 0, 0))
    return pl.pallas_call(
        body, name="gdn_chunk_fwd", grid=(B, nb),
        in_specs=[rowb(512, 0), rowb(512, 1), rowb(512, 2), rowb(128, 0),
                  BS((None, N_HEADS, nc, CHUNK), lambda b, n: (b, 0, n, 0))],
        out_specs=[rowb(512, 0), rowb(512, 0), mat, mat],
        out_shape=[jax.ShapeDtypeStruct((T, 512), F32), jax.ShapeDtypeStruct((T, 512), BF16),
                   jax.ShapeDtypeStruct((B, N_HEADS, N, CHUNK, CHUNK), BF16),
                   jax.ShapeDtypeStruct((B, N_HEADS, N, CHUNK, CHUNK), BF16)],
        compiler_params=_arb(2))(qkv, qkv, qkv, GB, Grow)


SCAN_CHUNKS = 4


def gdn_scan_fwd(qkv3, U3, W3, GB3, A, B, S):
    N = S // CHUNK
    cps = SCAN_CHUNKS if N % SCAN_CHUNKS == 0 else 1

    def body(q_ref, k_ref, u_ref, w_ref, gb_ref, a_ref, o_ref, vn_ref, st_ref, s_s):
        @pl.when(pl.program_id(0) == 0)
        def _():
            s_s[...] = jnp.zeros_like(s_s)

        for c in range(cps):
            rows = slice(c * CHUNK, (c + 1) * CHUNK)
            for b in range(B):
                for h in range(N_HEADS):
                    sl = slice(h * 128, (h + 1) * 128)
                    st = s_s[b, h]
                    st_ref[b, h, c] = st
                    stb = st.astype(BF16)
                    g = gb_ref[b, rows, h:h + 1]
                    gl = g[CHUNK - 1:CHUNK, :]
                    qg = (q_ref[b, rows, sl] * jnp.exp(g)).astype(BF16)
                    on_state = _dot(jnp.concatenate([w_ref[b, rows, sl].astype(BF16), qg], axis=0), stb, NN)
                    vn = u_ref[b, rows, sl] - on_state[:CHUNK]
                    vnb = vn.astype(BF16)
                    kd_t = jnp.transpose(k_ref[b, rows, sl] * jnp.exp(gl - g)).astype(BF16)
                    on_vn = _dot(jnp.concatenate([a_ref[b, h, c].astype(BF16), kd_t], axis=0), vnb, NN)
                    vn_ref[b, rows, sl] = vnb
                    o_ref[b, rows, sl] = (on_state[CHUNK:] + on_vn[:CHUNK]).astype(BF16)
                    s_s[b, h] = st * jnp.exp(gl) + on_vn[CHUNK:]

    tok = lambda c, j: BS((B, cps * CHUNK, c), lambda n: (0, n, j))
    return pl.pallas_call(
        body, name="gdn_scan_fwd", grid=(N // cps,),
        in_specs=[tok(512, 0), tok(512, 1), tok(512, 0), tok(512, 0), tok(128, 0),
                  BS((B, N_HEADS, cps, CHUNK, CHUNK), lambda n: (0, 0, n, 0, 0))],
        out_specs=[tok(512, 0), tok(512, 0), BS((B, N_HEADS, cps, 128, 128), lambda n: (0, 0, n, 0, 0))],
        out_shape=[jax.ShapeDtypeStruct((B, S, 512), BF16), jax.ShapeDtypeStruct((B, S, 512), BF16),
                   jax.ShapeDtypeStruct((B, N_HEADS, N, 128, 128), F32)],
        scratch_shapes=[pltpu.VMEM((B, N_HEADS, 128, 128), F32)],
        compiler_params=_arb(1))(qkv3, qkv3, U3, W3, GB3, A)


def gdn_scan_bwd(dO3, qkv3, W3, Vn3, GB3, A, St, B, S, after):
    N = S // CHUNK
    cps = SCAN_CHUNKS if N % SCAN_CHUNKS == 0 else 1

    def body(do_ref, q_ref, k_ref, w_ref, vn_ref, gb_ref, a_ref, st_ref, after_ref,
             du_ref, dw_ref, dq_ref, dk_ref, da_ref, dg_ref, ds_s):
        @pl.when(pl.program_id(0) == 0)
        def _():
            ds_s[...] = jnp.zeros_like(ds_s)

        lane = lax.broadcasted_iota(jnp.int32, (1, 128), 1)
        last = lax.broadcasted_iota(jnp.int32, (CHUNK, 1), 0) == CHUNK - 1
        for c in reversed(range(cps)):
            rows = slice(c * CHUNK, (c + 1) * CHUNK)
            for b in range(B):
                dg_all = jnp.zeros((CHUNK, 128), F32)
                for h in range(N_HEADS):
                    sl = slice(h * 128, (h + 1) * 128)
                    st = st_ref[b, h, c]
                    stb = st.astype(BF16)
                    dsn = ds_s[b, h]
                    dsnb = dsn.astype(BF16)
                    g = gb_ref[b, rows, h:h + 1]
                    gl = g[CHUNK - 1:CHUNK, :]
                    egl = jnp.exp(gl)
                    ekd = jnp.exp(gl - g)
                    eg = jnp.exp(g)
                    q, k = q_ref[b, rows, sl], k_ref[b, rows, sl]
                    kd = k * ekd
                    qg = q * eg
                    do = do_ref[b, rows, sl].astype(BF16)
                    vnb = vn_ref[b, rows, sl].astype(BF16)
                    dvn = _dot(a_ref[b, h, c].astype(BF16), do, TN) + _dot(kd.astype(BF16), dsnb, NN)
                    dvnb = dvn.astype(BF16)
                    do_on = _dot(do, jnp.concatenate([stb, vnb], axis=0), NT)
                    dqg = do_on[:, :128]
                    da_ref[b, h, c] = do_on[:, 128:]
                    dkd = _dot(vnb, dsnb, NT)
                    ds_s[b, h] = (_dot(qg.astype(BF16), do, TN) + egl * dsn - _dot(w_ref[b, rows, sl].astype(BF16), dvnb, TN))
                    du_ref[b, rows, sl] = dvnb
                    dw_ref[b, rows, sl] = (-_dot(dvnb, stb, NT)).astype(BF16)
                    dq_ref[b, rows, sl] = dqg * eg
                    dk_ref[b, rows, sl] = dkd * ekd
                    ddel = jnp.sum(dkd * kd, axis=1, keepdims=True)
                    dgl = jnp.sum(ddel, axis=0, keepdims=True) + jnp.sum(jnp.sum(st * dsn, axis=1, keepdims=True), axis=0, keepdims=True) * egl
                    col = jnp.sum(dqg * qg, axis=1, keepdims=True) - ddel + jnp.where(last, dgl, 0.0)
                    dg_all = jnp.where(lane == h, col, dg_all)
                dg_ref[b, rows, :] = dg_all

    steps = N // cps
    tok = lambda c, j: BS((B, cps * CHUNK, c), lambda n: (0, steps - 1 - n, j))
    mat = lambda d: BS((B, N_HEADS, cps, d, d), lambda n: (0, 0, steps - 1 - n, 0, 0))
    return pl.pallas_call(
        body, name="gdn_scan_bwd", grid=(steps,),
        in_specs=[tok(512, 0), tok(512, 0), tok(512, 1), tok(512, 0), tok(512, 0), tok(128, 0), mat(CHUNK), mat(128),
                  BS(memory_space=pl.ANY)],
        out_specs=[tok(512, 0), tok(512, 0), tok(512, 0), tok(512, 0), mat(CHUNK), tok(128, 0)],
        out_shape=[jax.ShapeDtypeStruct((B, S, 512), BF16)] * 2 + [jax.ShapeDtypeStruct((B, S, 512), F32)] * 2
        + [jax.ShapeDtypeStruct((B, N_HEADS, N, CHUNK, CHUNK), F32), jax.ShapeDtypeStruct((B, S, 128), F32)],
        scratch_shapes=[pltpu.VMEM((B, N_HEADS, 128, 128), F32)],
        compiler_params=_arb(1))(dO3, qkv3, qkv3, W3, Vn3, GB3, A, St, after)


def gdn_chunk_bwd(qkv, GB, Grow, Tinv, dA, dU, dW, dQ1, dK1, dG1, B, S, nc=8):
    T = B * S
    N = S // CHUNK
    nc = min(nc, N)
    nb = N // nc
    R = nc * CHUNK

    def body(q_ref, k_ref, v_ref, gb_ref, gr_ref, t_ref, da_ref, du_ref, dw_ref, dq1_ref, dk1_ref, dg1_ref, o_ref, dgb_ref):
        incl, strict = _chunk_masks(nc)
        lane = lax.broadcasted_iota(jnp.int32, (1, 128), 1)
        dg_all = dg1_ref[...]
        db_all = jnp.zeros((R, 128), F32)
        for h in range(N_HEADS):
            sl = slice(h * 128, (h + 1) * 128)
            q = q_ref[:, sl].reshape(nc, CHUNK, 128)
            k = k_ref[:, sl].reshape(nc, CHUNK, 128)
            v = v_ref[:, sl].reshape(nc, CHUNK, 128)
            gc = gb_ref[:, h:h + 1].reshape(nc, CHUNK, 1)
            beta = gb_ref[:, 4 + h:5 + h].reshape(nc, CHUNK, 1)
            gr = gr_ref[h][:, None, :]
            decay, kb, l_mat, a_mat = _chunk_local(q, k, gc, gr, beta, incl, strict)
            eg = jnp.exp(gc)
            kbg = kb * eg
            vb = v * beta
            tt = t_ref[h].astype(BF16)
            du = du_ref[:, sl].reshape(nc, CHUNK, 128).astype(BF16)
            dw = dw_ref[:, sl].reshape(nc, CHUNK, 128).astype(BF16)
            dvb = _bdot("gjc,gcv->gjv", tt, du)
            dkbg = _bdot("gjc,gck->gjk", tt, dw)
            dt = _bdot("gcv,gjv->gcj", du, vb.astype(BF16)) + _bdot("gck,gjk->gcj", dw, kbg.astype(BF16))
            tmp = _bdot("gca,gab->gcb", tt, dt.astype(BF16))
            dl = jnp.where(strict, -_bdot("gcb,gbd->gcd", tmp.astype(BF16), tt), 0.0)
            da = da_ref[h]
            dm = (dl * decay).astype(BF16)
            dqk = (da * decay).astype(BF16)
            kbf = k.astype(BF16)
            dkb = _bdot("gcj,gjd->gcd", dm, kbf) + dkbg * eg
            dk = (_bdot("gcj,gcd->gjd", dm, kb.astype(BF16)) + _bdot("gcj,gcd->gjd", dqk, q.astype(BF16))
                  + dk1_ref[:, sl].reshape(nc, CHUNK, 128) + dkb * beta)
            dq = _bdot("gcj,gjd->gcd", dqk, kbf) + dq1_ref[:, sl].reshape(nc, CHUNK, 128)
            e = dl * l_mat + da * a_mat
            dgc = (jnp.sum(e, axis=2, keepdims=True) - jnp.sum(jnp.swapaxes(e, 1, 2), axis=2, keepdims=True)
                   + jnp.sum(dkbg * kbg, axis=2, keepdims=True))
            dbeta = jnp.sum(dkb * k, axis=2, keepdims=True) + jnp.sum(dvb * v, axis=2, keepdims=True)
            o_ref[:, sl] = dq.reshape(R, 128)
            o_ref[:, 512 + h * 128:512 + (h + 1) * 128] = dk.reshape(R, 128)
            o_ref[:, 1024 + h * 128:1024 + (h + 1) * 128] = (dvb * beta).reshape(R, 128)
            dg_all = dg_all + jnp.where(lane == h, dgc.reshape(R, 1), 0.0)
            db_all = jnp.where(lane == 4 + h, dbeta.reshape(R, 1), db_all)
        t = _chunk_row(R)
        for s in (1, 2, 4, 8, 16, 32):
            dg_all = dg_all + jnp.where(t + s < CHUNK, pltpu.roll(dg_all, R - s, 0), 0.0)
        dgb_ref[...] = jnp.where(lane < 4, dg_all, db_all)

    rowb = lambda c, j: BS((R, c), lambda b, n: (b * nb + n, j))
    mat = BS((None, N_HEADS, nc, CHUNK, CHUNK), lambda b, n: (b, 0, n, 0, 0))
    return pl.pallas_call(
        body, name="gdn_chunk_bwd", grid=(B, nb),
        in_specs=[rowb(512, 0), rowb(512, 1), rowb(512, 2), rowb(128, 0),
                  BS((None, N_HEADS, nc, CHUNK), lambda b, n: (b, 0, n, 0)), mat, mat,
                  rowb(512, 0), rowb(512, 0), rowb(512, 0), rowb(512, 0), rowb(128, 0)],
        out_specs=[rowb(GDN_QKV, 0), rowb(128, 0)],
        out_shape=[jax.ShapeDtypeStruct((T, GDN_QKV), F32), jax.ShapeDtypeStruct((T, 128), F32)],
        compiler_params=_arb(2))(qkv, qkv, qkv, GB, Grow, Tinv, dA, dU, dW, dQ1, dK1, dG1)


def _gdn_out_norm(og, gg):
    outs, xhs, rs = [], [], []
    for h in range(N_HEADS):
        seg = og[:, h * 128:(h + 1) * 128]
        r = lax.rsqrt(jnp.mean(seg * seg, axis=-1, keepdims=True) + EPS)
        xh = seg * r
        outs.append(xh * gg)
        xhs.append(xh)
        rs.append(r)
    return outs, xhs, rs


def merge_fwd(o_mla, o_gdn, o_mem, P, x, tgt, w_out, g_gdn, g_fin, tm=512):
    T = x.shape[0]
    tm = min(tm, T)

    def body(om_ref, og_ref, oc_ref, gate_ref, x_ref, t_ref, w_ref, gg_ref, gf_ref, mix_ref, dx_ref, dxb_ref, sq_ref, gnf_ref):
        @pl.when(pl.program_id(0) == 0)
        def _():
            sq_ref[...] = jnp.zeros_like(sq_ref)
            gnf_ref[...] = jnp.zeros_like(gnf_ref)

        ogn, _, _ = _gdn_out_norm(og_ref[...].astype(F32), gg_ref[...])
        cat = jnp.concatenate([om_ref[...].astype(F32)] + ogn + [oc_ref[...].astype(F32)], axis=1)
        gt = gate_ref[...].astype(F32)
        mixed = (cat * (gt * _sigmoid(gt))).astype(BF16)
        mix_ref[...] = mixed
        x2 = x_ref[...] + _dot(mixed, w_ref[...], NN)
        r2 = lax.rsqrt(jnp.mean(x2 * x2, axis=-1, keepdims=True) + EPS)
        xh = x2 * r2
        gf = gf_ref[...]
        diff = xh * gf - t_ref[...]
        sq_ref[...] += jnp.sum(diff * diff, axis=0, keepdims=True)
        dy = diff * (1.0 / D_MODEL)
        gnf_ref[...] += jnp.sum(dy * xh, axis=0, keepdims=True)
        dxh = dy * gf
        dx = r2 * (dxh - xh * jnp.mean(dxh * xh, axis=-1, keepdims=True))
        dx_ref[...] = dx
        dxb_ref[...] = dx.astype(BF16)

    rowb = lambda c, j=0: BS((tm, c), lambda i: (i, j))
    full = lambda r, c: BS((r, c), lambda i: (0, 0))
    return pl.pallas_call(
        body, name="merge_fwd", grid=(T // tm,),
        in_specs=[rowb(512), rowb(512), rowb(512), rowb(D_MIX, OFF_GATE // D_MIX), rowb(D_MODEL), rowb(D_MODEL),
                  full(D_MIX, D_MODEL), full(1, 128), full(1, D_MODEL)],
        out_specs=[rowb(D_MIX), rowb(D_MODEL), rowb(D_MODEL), full(1, D_MODEL), full(1, D_MODEL)],
        out_shape=[jax.ShapeDtypeStruct((T, D_MIX), BF16), jax.ShapeDtypeStruct((T, D_MODEL), F32),
                   jax.ShapeDtypeStruct((T, D_MODEL), BF16),
                   jax.ShapeDtypeStruct((1, D_MODEL), F32), jax.ShapeDtypeStruct((1, D_MODEL), F32)],
        compiler_params=_arb(1))(o_mla, o_gdn, o_mem, P, x, tgt, w_out, g_gdn, g_fin)


def merge_bwd(dx2, o_mla, o_gdn, o_mem, P, w_out, g_gdn, tm=512):
    T = dx2.shape[0]
    tm = min(tm, T)

    def body(dx_ref, om_ref, og_ref, oc_ref, gate_ref, w_ref, gg_ref, dgate_ref, dom_ref, dog_ref, doc_ref, ggn_ref):
        @pl.when(pl.program_id(0) == 0)
        def _():
            ggn_ref[...] = jnp.zeros_like(ggn_ref)

        gg = gg_ref[...]
        dmix = _dot(dx_ref[...].astype(BF16), w_ref[...], NT)
        ogn, xhs, rs = _gdn_out_norm(og_ref[...].astype(F32), gg)
        cat = jnp.concatenate([om_ref[...].astype(F32)] + ogn + [oc_ref[...].astype(F32)], axis=1)
        gt = gate_ref[...].astype(F32)
        sg = _sigmoid(gt)
        dgate_ref[...] = (dmix * cat * (sg * (1.0 + gt * (1.0 - sg)))).astype(BF16)
        dcat = dmix * (gt * sg)
        dom_ref[...] = dcat[:, :512].astype(BF16)
        doc_ref[...] = dcat[:, 1024:].astype(BF16)
        acc = jnp.zeros((1, 128), F32)
        for h in range(N_HEADS):
            dseg = dcat[:, 512 + h * 128:512 + (h + 1) * 128]
            acc = acc + jnp.sum(dseg * xhs[h], axis=0, keepdims=True)
            dxh = dseg * gg
            dog_ref[:, h * 128:(h + 1) * 128] = (rs[h] * (dxh - xhs[h] * jnp.mean(dxh * xhs[h], axis=-1, keepdims=True))).astype(BF16)
        ggn_ref[...] += acc

    rowb = lambda c, j=0: BS((tm, c), lambda i: (i, j))
    full = lambda r, c: BS((r, c), lambda i: (0, 0))
    return pl.pallas_call(
        body, name="merge_bwd", grid=(T // tm,),
        in_specs=[rowb(D_MODEL), rowb(512), rowb(512), rowb(512), rowb(D_MIX, OFF_GATE // D_MIX),
                  full(D_MIX, D_MODEL), full(1, 128)],
        out_specs=[rowb(D_MIX), rowb(512), rowb(512), rowb(512), full(1, 128)],
        out_shape=[jax.ShapeDtypeStruct((T, D_MIX), BF16)] + [jax.ShapeDtypeStruct((T, 512), BF16)] * 3
        + [jax.ShapeDtypeStruct((1, 128), F32)],
        compiler_params=_arb(1))(dx2, o_mla, o_gdn, o_mem, P, w_out, g_gdn)


def in_proj_bwd(dP, wp, x, dx2, gain, after, tm=512):
    T, n = x.shape
    tm = min(tm, T)
    k = len(dP)
    widths = [p.shape[1] for p in dP]
    offs = [sum(widths[:i]) for i in range(k)]

    def body(*refs):
        w_ref, x_ref, dx2_ref, g_ref = refs[k:k + 4]
        o_ref, acc_ref = refs[-2:]

        @pl.when(pl.program_id(0) == 0)
        def _():
            acc_ref[...] = jnp.zeros_like(acc_ref)

        dy = None
        for a_ref, off, w in zip(refs[:k], offs, widths):
            d = _dot(a_ref[...], w_ref[off:off + w, :], NN)
            dy = d if dy is None else dy + d
        xv = x_ref[...]
        r = lax.rsqrt(jnp.mean(xv * xv, axis=-1, keepdims=True) + EPS)
        xh = xv * r
        acc_ref[...] += jnp.sum(dy * xh, axis=0, keepdims=True)
        dxh = dy * g_ref[...]
        o_ref[...] = dx2_ref[...] + r * (dxh - xh * jnp.mean(dxh * xh, axis=-1, keepdims=True))

    rowb = BS((tm, n), lambda i: (i, 0))
    full = BS((1, n), lambda i: (0, 0))
    return pl.pallas_call(
        body, name="in_proj_bwd", grid=(T // tm,),
        in_specs=[BS((tm, w), lambda i: (i, 0)) for w in widths]
        + [BS(wp.shape, lambda i: (0, 0), pipeline_mode=pl.Buffered(1)), rowb, rowb, full, BS(memory_space=pl.ANY)],
        out_specs=[rowb, full], out_shape=[jax.ShapeDtypeStruct((T, n), F32), jax.ShapeDtypeStruct((1, n), F32)],
        compiler_params=_arb(1))(*dP, wp, x, dx2, gain, after)


W_IN_SHARD = D_IN // 4
_GDN0 = Q_LORA + KV_LORA + MLA_ROPE
_AB0 = _GDN0 + GDN_QKV
_MEMQ0 = _AB0 + 2 * N_HEADS
_GATE0 = _MEMQ0 + N_HEADS * MEM_DH


def _w_in_row_map():
    a, m, gt = _AB0 - 2 * W_IN_SHARD, _MEMQ0 - 2 * W_IN_SHARD, _GATE0 - 2 * W_IN_SHARD
    e0 = OFF_GDN + W_IN_SHARD - _GDN0
    e1 = e0 + W_IN_SHARD
    e2 = OFF_GATE + W_IN_SHARD - gt
    return [(0, 0, 0, 672), (0, 672, 704, 32), (2, a, 768, m - a), (2, m, OFF_MEMQ, gt - m), (0, _GDN0, OFF_GDN, W_IN_SHARD - _GDN0),
            (1, 0, e0, W_IN_SHARD), (2, 0, e1, a), (2, gt, OFF_GATE, W_IN_SHARD - gt), (3, 0, e2, W_IN_SHARD)]


_W_IN_ZERO_ROWS = [(672, 32), (736, 32), (776, 248)]
W_IN_LANES = 256


EARLY_ROWS = 704
EARLY_ROW0 = (0, 80)


def pad_w_in_t_early(early):
    per_half = early.shape[3] // W_IN_LANES

    def body(s_ref, o_ref):
        for r0, n in _W_IN_ZERO_ROWS:
            o_ref[r0:r0 + n, :] = jnp.zeros((n, W_IN_LANES), o_ref.dtype)
        for q, src, dst, n in _w_in_row_map():
            if dst < OFF_GDN:
                first = src - EARLY_ROW0[q // 2]
                o_ref[dst:dst + n, :] = s_ref[q // 2, first:first + n, :]

    return pl.pallas_call(
        body, name="pad_w_in_t_early", grid=(D_MODEL // W_IN_LANES,),
        in_specs=[BS((2, None, EARLY_ROWS, W_IN_LANES), lambda j: (0, j // per_half, 0, j % per_half))],
        out_specs=BS((OFF_GDN, W_IN_LANES), lambda j: (0, j)),
        out_shape=jax.ShapeDtypeStruct((N_PAD, D_MODEL), early.dtype), compiler_params=_arb(1))(early)


def pad_w_in_t_rest(shards, wp):
    per_half = shards.shape[3] // W_IN_LANES
    blocks = (OFF_GDN, OFF_GATE)
    assert OFF_GATE - OFF_GDN == OFF_GDN and N_PAD - OFF_GATE == OFF_GDN

    def body(s_ref, w_in, o_ref):
        for i, b0 in enumerate(blocks):
            @pl.when(pl.program_id(1) == i)
            def _(b0=b0):
                for q, src, dst, n in _w_in_row_map():
                    if b0 <= dst < b0 + OFF_GDN:
                        o_ref[dst - b0:dst - b0 + n, :] = s_ref[q, src:src + n, :]

    return pl.pallas_call(
        body, name="pad_w_in_t_rest", grid=(D_MODEL // W_IN_LANES, len(blocks)),
        in_specs=[BS((N_CHIPS, None, W_IN_SHARD, W_IN_LANES), lambda j, i: (0, j // per_half, 0, j % per_half)), ANY],
        out_specs=BS((OFF_GDN, W_IN_LANES), lambda j, i: (1 + i, j)), input_output_aliases={1: 0},
        out_shape=jax.ShapeDtypeStruct((N_PAD, D_MODEL), shards.dtype), compiler_params=_arb(2))(shards, wp)


def unpad_w_in_t(g):
    def body(g_ref, o_ref):
        for q, src, dst, n in _w_in_row_map():
            o_ref[q, src:src + n, :] = g_ref[dst:dst + n, :]

    return pl.pallas_call(
        body, name="unpad_w_in_t", grid=(D_MODEL // W_IN_LANES,),
        in_specs=[BS((N_PAD, W_IN_LANES), lambda j: (0, j))], out_specs=BS((N_CHIPS, W_IN_SHARD, W_IN_LANES), lambda j: (0, 0, j)),
        out_shape=jax.ShapeDtypeStruct((N_CHIPS, W_IN_SHARD, D_MODEL), g.dtype), compiler_params=_arb(1))(g)


def _perm_w_kv_b(s):
    return jnp.concatenate([s[h, :, :128] for h in range(N_HEADS)] + [s[h, :, 128:] for h in range(N_HEADS)], axis=1)


def _unperm_w_kv_b(g):
    return jnp.stack([jnp.concatenate([g[:, h * 128:(h + 1) * 128], g[:, 512 + h * 128:512 + (h + 1) * 128]], axis=1)
                      for h in range(N_HEADS)])


def _lane_row(v4):
    return jnp.pad(v4.reshape(1, -1).astype(F32), ((0, 0), (0, 128 - v4.size)))


N_CHIPS = 4
MESH = pl.DeviceIdType.MESH
ANY = BS(memory_space=pl.ANY)


def _place():
    return lax.axis_index("x"), lax.axis_index("y"), lax.axis_index("c")


def _other_chips(x, y):
    return [(1 - x, y), (x, 1 - y), (1 - x, 1 - y)]


def _half(split, which):
    axis, size = split
    ds = pl.ds(pl.multiple_of(which * size, 16 if axis == 0 else 128), size)
    return (ds, slice(None)) if axis == 0 else (slice(None), ds)


SEM = BS(memory_space=pltpu.SEMAPHORE)
HBM = BS(memory_space=pltpu.HBM)
_IN_HBM = lambda a: pltpu.with_memory_space_constraint(a, pltpu.HBM)
_SIDE_EFFECT = pltpu.SideEffectType.DATAFLOW_SIDE_EFFECTING


def _late_gather_copies(s_refs, l_refs, send_sems, recv_sems, local_sems, with_arrivals):
    x, y, c = _place()
    sends, recvs, locals_ = [], [], []
    for i, (s_ref, l_ref) in enumerate(zip(s_refs, l_refs)):
        locals_.append(pltpu.make_async_copy(s_ref, l_ref.at[2 * x + y], local_sems.at[i]))
        for j, (px, py) in enumerate(_other_chips(x, y)):
            k = 3 * i + j
            sends.append(pltpu.make_async_remote_copy(src_ref=s_ref, dst_ref=l_ref.at[2 * x + y], send_sem=send_sems.at[k],
                                                      recv_sem=recv_sems.at[k], device_id=(px, py, c), device_id_type=MESH))
            if with_arrivals:
                recvs.append(pltpu.make_async_remote_copy(src_ref=s_ref, dst_ref=l_ref.at[2 * px + py], send_sem=send_sems.at[k],
                                                          recv_sem=recv_sems.at[k], device_id=(px, py, c), device_id_type=MESH))
    return sends, recvs, locals_


def late_gather_start(shards, after, name):
    n = len(shards)

    def body(*refs):
        s_refs, l_refs = refs[:n], refs[n:2 * n]
        send_sems, recv_sems, local_sems = refs[2 * n + 1:2 * n + 4]
        token = refs[-1]
        sends, _, locals_ = _late_gather_copies(s_refs, l_refs, send_sems, recv_sems, local_sems, False)
        for cp in locals_ + sends:
            cp.start()
        token[...] = jnp.zeros_like(token)

    lands = [lax.empty((N_CHIPS,) + s.shape, s.dtype) for s in shards]
    hbm_like = lambda a: pltpu.HBM(a.shape, a.dtype)
    out = pl.pallas_call(
        body, name=name,
        out_shape=[pltpu.SemaphoreType.DMA((3 * n,)), pltpu.SemaphoreType.DMA((3 * n,)), pltpu.SemaphoreType.DMA((n,))]
        + [hbm_like(s) for s in shards] + [hbm_like(l) for l in lands] + [jax.ShapeDtypeStruct((8, 128), F32)],
        in_specs=[HBM] * (2 * n) + [BS(memory_space=pl.ANY)], out_specs=[SEM] * 3 + [HBM] * (2 * n) + [BS(memory_space=pltpu.VMEM)],
        input_output_aliases={i: 3 + i for i in range(2 * n)},
        compiler_params=pltpu.CompilerParams(has_side_effects=_SIDE_EFFECT))(
            *[_IN_HBM(s) for s in shards], *[_IN_HBM(l) for l in lands], after)
    return out[:3], out[3:3 + n], out[3 + n:3 + 2 * n], out[-1]


def late_gather_wait(sems, shards, lands, after, name):
    n = len(shards)

    def body(*refs):
        s_refs, l_refs = refs[:n], refs[n:2 * n]
        send_sems, recv_sems, local_sems = refs[2 * n:2 * n + 3]
        sends, recvs, locals_ = _late_gather_copies(s_refs, l_refs, send_sems, recv_sems, local_sems, True)
        for cp in locals_:
            cp.wait()
        for cp in sends:
            cp.wait_send()
        for cp in recvs:
            cp.wait_recv()

    hbm_like = lambda a: pltpu.HBM(a.shape, a.dtype)
    out = pl.pallas_call(
        body, name=name, out_shape=[hbm_like(s) for s in shards] + [hbm_like(l) for l in lands],
        in_specs=[HBM] * (2 * n) + [SEM] * 3 + [BS(memory_space=pl.ANY)], out_specs=[HBM] * (2 * n),
        input_output_aliases={i: i for i in range(2 * n)},
        compiler_params=pltpu.CompilerParams(has_side_effects=_SIDE_EFFECT))(*shards, *lands, *sems, after)
    return out[n:]


def _sends(x, y, early):
    return (y == 0) if early else None


def _block(x, y, early):
    return x if early else 2 * x + y


def _if(cond, fn):
    if cond is None:
        fn()
    else:
        pl.when(cond)(fn)


def _half_gather_copies(s_ref, l_ref, send_sems, recv_sems, with_arrivals, early):
    x, y, c = _place()
    sends, recvs, peer_sends = [], [], []
    for j, (px, py) in enumerate(_other_chips(x, y)):
        sends.append(pltpu.make_async_remote_copy(src_ref=s_ref.at[c], dst_ref=l_ref.at[_block(x, y, early), c],
                                                  send_sem=send_sems.at[j], recv_sem=recv_sems.at[j], device_id=(px, py, c),
                                                  device_id_type=MESH))
        if with_arrivals:
            recvs.append(pltpu.make_async_remote_copy(src_ref=s_ref.at[c], dst_ref=l_ref.at[_block(px, py, early), c],
                                                      send_sem=send_sems.at[j], recv_sem=recv_sems.at[j], device_id=(px, py, c),
                                                      device_id_type=MESH))
            peer_sends.append(_sends(px, py, early))
    return sends, recvs, peer_sends


def half_gather_start(shard, name, early=False, after=()):
    def body(s_ref, l_ref, *rest):
        send_sems, recv_sems, local_sem, s_thru, l_thru, token = rest[len(after):]
        x, y, _ = _place()

        def go():
            pltpu.make_async_copy(s_ref, l_ref.at[_block(x, y, early)], local_sem.at[0]).start()
            for cp in _half_gather_copies(s_ref, l_ref, send_sems, recv_sems, False, early)[0]:
                cp.start()

        _if(_sends(x, y, early), go)
        token[...] = jnp.zeros_like(token)

    land = lax.empty((2 if early else N_CHIPS,) + shard.shape, shard.dtype)
    out = pl.pallas_call(
        body, name=name,
        out_shape=[pltpu.SemaphoreType.DMA((3,)), pltpu.SemaphoreType.DMA((3,)), pltpu.SemaphoreType.DMA((1,)),
                   pltpu.HBM(shard.shape, shard.dtype), pltpu.HBM(land.shape, land.dtype), jax.ShapeDtypeStruct((8, 128), F32)],
        in_specs=[HBM, HBM] + [BS(memory_space=pl.ANY)] * len(after),
        out_specs=[SEM] * 3 + [HBM, HBM, BS(memory_space=pltpu.VMEM)],
        input_output_aliases={0: 3, 1: 4},
        compiler_params=pltpu.CompilerParams(has_side_effects=_SIDE_EFFECT))(_IN_HBM(shard), _IN_HBM(land), *after)
    return out[:3], out[3], out[4], out[5]


def half_gather_wait(sems, shard, land, after, name, early=False):
    def body(s_ref, l_ref, send_sems, recv_sems, local_sem, *rest):
        x, y, _ = _place()
        sends, recvs, peer_sends = _half_gather_copies(s_ref, l_ref, send_sems, recv_sems, True, early)

        def sent():
            pltpu.make_async_copy(s_ref, l_ref.at[_block(x, y, early)], local_sem.at[0]).wait()
            for cp in sends:
                cp.wait_send()

        _if(_sends(x, y, early), sent)
        for cp, cond in zip(recvs, peer_sends):
            _if(cond, cp.wait_recv)

    out = pl.pallas_call(
        body, name=name, out_shape=[pltpu.HBM(shard.shape, shard.dtype), pltpu.HBM(land.shape, land.dtype)],
        in_specs=[HBM, HBM] + [SEM] * 3 + [BS(memory_space=pl.ANY)] * len(after), out_specs=[HBM, HBM],
        input_output_aliases={0: 0, 1: 1},
        compiler_params=pltpu.CompilerParams(has_side_effects=_SIDE_EFFECT))(shard, land, *sems, *after)
    return out[1]


def pass_halves_to_sibling(land, name, early=False):
    def body(l_in, l_ref, send_sems, recv_sems):
        x, y, c = _place()
        copies = []
        for j, (px, py) in enumerate(_other_chips(x, y)):
            q = _block(px, py, early)
            give = pltpu.make_async_remote_copy(src_ref=l_ref.at[q, c], dst_ref=l_ref.at[q, c], send_sem=send_sems.at[j],
                                                recv_sem=recv_sems.at[j], device_id=(x, y, 1 - c), device_id_type=MESH)
            take = pltpu.make_async_remote_copy(src_ref=l_ref.at[q, c], dst_ref=l_ref.at[q, 1 - c], send_sem=send_sems.at[j],
                                                recv_sem=recv_sems.at[j], device_id=(x, y, 1 - c), device_id_type=MESH)
            _if(_sends(px, py, early), give.start)
            copies.append((give, take, _sends(px, py, early)))
        for give, take, cond in copies:
            def done(give=give, take=take):
                take.wait_recv()
                give.wait_send()

            _if(cond, done)

    return pl.pallas_call(
        body, name=name, in_specs=[ANY], out_specs=ANY, out_shape=jax.ShapeDtypeStruct(land.shape, land.dtype),
        input_output_aliases={0: 0},
        scratch_shapes=[pltpu.SemaphoreType.DMA((3,)), pltpu.SemaphoreType.DMA((3,))])(land)


def allgather_devices(block, name):
    R, C = block.shape

    def body(b_ref, o_ref, send_sems, recv_sems, local_sem):
        x, y, c = _place()
        me = 4 * x + 2 * y + c
        own = pltpu.make_async_copy(b_ref, o_ref.at[me], local_sem)
        own.start()
        copies = []
        for r in range(1, 8):
            px = 1 - x if r & 4 else x
            py = 1 - y if r & 2 else y
            pc = 1 - c if r & 1 else c
            send = pltpu.make_async_remote_copy(src_ref=b_ref, dst_ref=o_ref.at[me], send_sem=send_sems.at[r - 1],
                                                recv_sem=recv_sems.at[r - 1], device_id=(px, py, pc), device_id_type=MESH)
            recv = pltpu.make_async_remote_copy(src_ref=b_ref, dst_ref=o_ref.at[4 * px + 2 * py + pc], send_sem=send_sems.at[r - 1],
                                                recv_sem=recv_sems.at[r - 1], device_id=(px, py, pc), device_id_type=MESH)
            send.start()
            copies.append((send, recv))
        for send, recv in copies:
            recv.wait_recv()
            send.wait_send()
        own.wait()

    return pl.pallas_call(
        body, name=name, in_specs=[ANY], out_specs=ANY, out_shape=jax.ShapeDtypeStruct((8, R, C), block.dtype),
        scratch_shapes=[pltpu.SemaphoreType.DMA((7,)), pltpu.SemaphoreType.DMA((7,)), pltpu.SemaphoreType.DMA(())])(block)


def swap_sibling(arrs, name, splits=None):
    n = len(arrs)

    def sent(a_ref, i, c):
        return a_ref if splits is None else a_ref.at[(slice(None),) + _half(splits[i], 1 - c)]

    def out_shape(a, i):
        if splits is None:
            return a.shape
        axis, size = splits[i]
        return (a.shape[0], size, a.shape[2]) if axis == 0 else (a.shape[0], a.shape[1], size)

    def body(*refs):
        a_refs, o_refs = refs[:n], refs[n:2 * n]
        send_sems, recv_sems = refs[2 * n:]
        x, y, c = _place()
        copies = [pltpu.make_async_remote_copy(src_ref=sent(a_ref, i, c), dst_ref=o_ref, send_sem=send_sems.at[i],
                                               recv_sem=recv_sems.at[i], device_id=(x, y, 1 - c), device_id_type=MESH)
                  for i, (a_ref, o_ref) in enumerate(zip(a_refs, o_refs))]
        for cp in copies:
            cp.start()
        for cp in copies:
            cp.wait()

    return pl.pallas_call(
        body, name=name, in_specs=[ANY] * n, out_specs=[ANY] * n,
        out_shape=[jax.ShapeDtypeStruct(out_shape(a, i), a.dtype) for i, a in enumerate(arrs)],
        scratch_shapes=[pltpu.SemaphoreType.DMA((n,)), pltpu.SemaphoreType.DMA((n,))])(*arrs)


def _exchange_copies(p_refs, l_refs, send_sems, recv_sems):
    x, y, c = _place()
    return [pltpu.make_async_remote_copy(src_ref=p_ref.at[2 * px + py], dst_ref=l_ref.at[j], send_sem=send_sems.at[3 * i + j],
                                         recv_sem=recv_sems.at[3 * i + j], device_id=(px, py, c), device_id_type=MESH)
            for i, (p_ref, l_ref) in enumerate(zip(p_refs, l_refs)) for j, (px, py) in enumerate(_other_chips(x, y))]


def exchange_chips_start(parts, name):
    n = len(parts)

    def body(*refs):
        send_sems, recv_sems = refs[2 * n:2 * n + 2]
        for cp in _exchange_copies(refs[:n], refs[n:2 * n], send_sems, recv_sems):
            cp.start()
        refs[-1][...] = jnp.zeros_like(refs[-1])

    lands = [lax.empty((3,) + p.shape[1:], p.dtype) for p in parts]
    hbm_like = lambda a: pltpu.HBM(a.shape, a.dtype)
    out = pl.pallas_call(
        body, name=name,
        out_shape=[pltpu.SemaphoreType.DMA((3 * n,)), pltpu.SemaphoreType.DMA((3 * n,))]
        + [hbm_like(p) for p in parts] + [hbm_like(l) for l in lands] + [jax.ShapeDtypeStruct((8, 128), F32)],
        in_specs=[HBM] * (2 * n), out_specs=[SEM] * 2 + [HBM] * (2 * n) + [BS(memory_space=pltpu.VMEM)],
        input_output_aliases={i: 2 + i for i in range(2 * n)},
        compiler_params=pltpu.CompilerParams(has_side_effects=_SIDE_EFFECT))(*[_IN_HBM(p) for p in parts], *[_IN_HBM(l) for l in lands])
    return out[:2], out[2:2 + n], out[2 + n:2 + 2 * n], out[-1]


def exchange_chips_wait(sems, parts, lands, after, name):
    n = len(parts)

    def body(*refs):
        send_sems, recv_sems = refs[2 * n:2 * n + 2]
        for cp in _exchange_copies(refs[:n], refs[n:2 * n], send_sems, recv_sems):
            cp.wait_send()
            cp.wait_recv()

    hbm_like = lambda a: pltpu.HBM(a.shape, a.dtype)
    out = pl.pallas_call(
        body, name=name, out_shape=[hbm_like(p) for p in parts] + [hbm_like(l) for l in lands],
        in_specs=[HBM] * (2 * n) + [SEM] * 2 + [BS(memory_space=pl.ANY)], out_specs=[HBM] * (2 * n),
        input_output_aliases={i: i for i in range(2 * n)},
        compiler_params=pltpu.CompilerParams(has_side_effects=_SIDE_EFFECT))(*parts, *lands, *sems, after)
    return out[:n], out[n:]


def add_fours(parts, from_chips, chip_core, name):
    n = len(parts)

    def body(s_ref, *refs):
        for a_ref, p_ref, o_ref in zip(refs[:n], refs[n:2 * n], refs[2 * n:]):
            s = a_ref[...].astype(F32)
            for j in range(3):
                s = s + p_ref[j].astype(F32)
            o_ref[...] = s

    own_specs = [BS((None,) + p.shape[1:], lambda g, s: (s[0], 0, 0)) for p in parts]
    chip_specs = [BS(p.shape, lambda g, s: (0, 0, 0)) for p in from_chips]
    out_specs = [BS(p.shape[1:], lambda g, s: (0, 0)) for p in parts]
    return pl.pallas_call(
        body, name=name,
        grid_spec=pltpu.PrefetchScalarGridSpec(num_scalar_prefetch=1, grid=(1,), in_specs=own_specs + chip_specs,
                                               out_specs=out_specs),
        out_shape=[jax.ShapeDtypeStruct(p.shape[1:], F32) for p in parts], compiler_params=_arb(1))(chip_core, *parts, *from_chips)


def _half_block(shape2, split):
    axis, size = split
    return (size, shape2[1]) if axis == 0 else (shape2[0], size)


def add_pairs(parts, halves, splits, core, name):
    n = len(parts)

    def body(s_ref, *refs):
        for a_ref, b_ref, o_ref in zip(refs[:n], refs[n:2 * n], refs[2 * n:]):
            o_ref[...] = (a_ref[...].astype(F32) + b_ref[...].astype(F32)).astype(BF16)

    def mine(i):
        blk = (None,) + _half_block(parts[i].shape[1:], splits[i])
        if splits[i][0] == 0:
            return BS(blk, lambda q, s: (q, s[0], 0))
        return BS(blk, lambda q, s: (q, 0, s[0]))

    half_specs = [BS((None,) + h.shape[1:], lambda q, s: (q, 0, 0)) for h in halves]
    return pl.pallas_call(
        body, name=name,
        grid_spec=pltpu.PrefetchScalarGridSpec(num_scalar_prefetch=1, grid=(N_CHIPS,),
                                               in_specs=[mine(i) for i in range(n)] + half_specs, out_specs=half_specs),
        out_shape=[jax.ShapeDtypeStruct(h.shape, BF16) for h in halves], compiler_params=_arb(1))(core, *parts, *halves)


def add_fives(parts, halves, from_chips, splits, chip_core, name):
    n = len(parts)

    def body(s_ref, *refs):
        for a_ref, b_ref, p_ref, o_ref in zip(refs[:n], refs[n:2 * n], refs[2 * n:3 * n], refs[3 * n:]):
            s = a_ref[...].astype(F32) + b_ref[...].astype(F32)
            for j in range(3):
                s = s + p_ref[j].astype(F32)
            o_ref[...] = s

    def mine(i):
        blk = (None,) + _half_block(parts[i].shape[1:], splits[i])
        if splits[i][0] == 0:
            return BS(blk, lambda g, s: (s[0], s[1], 0))
        return BS(blk, lambda g, s: (s[0], 0, s[1]))

    half_specs = [BS((None,) + h.shape[1:], lambda g, s: (s[0], 0, 0)) for h in halves]
    chip_specs = [BS(p.shape, lambda g, s: (0, 0, 0)) for p in from_chips]
    out_specs = [BS(h.shape[1:], lambda g, s: (0, 0)) for h in halves]
    return pl.pallas_call(
        body, name=name,
        grid_spec=pltpu.PrefetchScalarGridSpec(num_scalar_prefetch=1, grid=(1,),
                                               in_specs=[mine(i) for i in range(n)] + half_specs + chip_specs, out_specs=out_specs),
        out_shape=[jax.ShapeDtypeStruct(h.shape[1:], F32) for h in halves], compiler_params=_arb(1))(chip_core, *parts, *halves, *from_chips)


def sum_leading(a, name):
    def body(a_ref, o_ref):
        s = a_ref[0]
        for j in range(1, a.shape[0]):
            s = s + a_ref[j]
        o_ref[...] = s

    return pl.pallas_call(body, name=name, out_shape=jax.ShapeDtypeStruct(a.shape[1:], a.dtype))(a)


def _adamw_math(w, g, m, v):
    mn = ADAM_B1 * m + (1.0 - ADAM_B1) * g
    vn = ADAM_B2 * v + (1.0 - ADAM_B2) * (g * g)
    m_hat = mn / (1.0 - ADAM_B1 ** ADAM_STEP)
    v_hat = vn / (1.0 - ADAM_B2 ** ADAM_STEP)
    return -ADAM_LR * (m_hat / (jnp.sqrt(v_hat) + ADAM_EPS) + ADAM_WD * w), mn, vn


def adamw(w, g, m, v, name):
    R, C = g.shape
    lead = (None,) * (w.ndim - 2)

    def body(w_ref, g_ref, m_ref, v_ref, d_ref, mo_ref, vo_ref):
        d_ref[...], mo_ref[...], vo_ref[...] = _adamw_math(w_ref[...], g_ref[...], m_ref[...], v_ref[...])

    wblk = BS(lead + (R, C), lambda i: (0,) * w.ndim)
    gblk = BS((R, C), lambda i: (0, 0))
    return pl.pallas_call(
        body, name=name, grid=(1,), in_specs=[wblk, gblk, wblk, wblk], out_specs=[wblk] * 3,
        out_shape=[jax.ShapeDtypeStruct(w.shape, F32)] * 3, compiler_params=_arb(1))(w, g, m, v)


SMALL_ROWS = 16
CONV_ROW0 = 8
LOSS_ROW = 14


def pack_small(small_grads, g_ab, g_conv, sq):
    present = [a for a in small_grads if a is not None]

    def body(*refs):
        ab_ref, conv_ref, sq_ref, o_ref = refs[len(present):]
        o_ref[...] = jnp.zeros_like(o_ref)
        it = iter(refs[:len(present)])
        for i, a in enumerate(small_grads):
            if a is not None:
                o_ref[i:i + 1, 0:a.shape[1]] = next(it)[...]
        o_ref[3:4, 0:128] = ab_ref[0:1, :]
        o_ref[4:5, 0:128] = ab_ref[1:2, :]
        half = 512
        for k in range(GDN_CONV * GDN_QKV // half):
            src_r, src_c = (k * half) // GDN_QKV, (k * half) % GDN_QKV
            dst_r, dst_c = CONV_ROW0 + (k * half) // 1024, (k * half) % 1024
            o_ref[dst_r:dst_r + 1, dst_c:dst_c + half] = conv_ref[src_r:src_r + 1, src_c:src_c + half]
        o_ref[LOSS_ROW:LOSS_ROW + 1, :] = sq_ref[...]

    return pl.pallas_call(body, name="pack_small", out_shape=jax.ShapeDtypeStruct((SMALL_ROWS, 1024), F32))(
        *present, g_ab, g_conv, sq)


def adamw_small(block, ws, ms, vs):
    k = len(ws)

    def body(b_ref, *refs):
        outs = refs[3 * k:]
        for i in range(k):
            n = ws[i].shape[1]
            g = b_ref[i:i + 1, 0:n]
            d, mn, vn = _adamw_math(refs[i][...], g, refs[k + i][...], refs[2 * k + i][...])
            outs[4 * i][...], outs[4 * i + 1][...], outs[4 * i + 2][...], outs[4 * i + 3][...] = g, d, mn, vn

    out = pl.pallas_call(
        body, name="adamw_small",
        out_shape=[jax.ShapeDtypeStruct(w.shape, F32) for w in ws for _ in range(4)])(block, *ws, *ms, *vs)
    return [out[4 * i:4 * i + 4] for i in range(k)]


def adamw_halves(w, mine, other, m, v, split, core, name):
    R, C = w.shape[-2:]
    axis, size = split
    lead = (None,) * (w.ndim - 2)
    zeros = (0,) * (w.ndim - 2)
    if axis == 0:
        tr = size if size <= 256 else next(t for t in range(256, 7, -1) if size % t == 0 and t % 8 == 0)
        nb = size // tr
        whole = BS(lead + (tr, C), lambda hi, j, s: zeros + (hi * nb + j, 0))
        part = BS((tr, C), lambda hi, j, s: (j, 0))
    else:
        nb = size // 128
        whole = BS(lead + (R, 128), lambda hi, j, s: zeros + (0, hi * nb + j))
        part = BS((R, 128), lambda hi, j, s: (0, j))

    def body(s_ref, w_ref, a_ref, b_ref, m_ref, v_ref, g_ref, d_ref, mo_ref, vo_ref):
        g = jnp.where(pl.program_id(0) == s_ref[0], a_ref[...], b_ref[...])
        g_ref[...] = g
        d_ref[...], mo_ref[...], vo_ref[...] = _adamw_math(w_ref[...], g, m_ref[...], v_ref[...])

    return pl.pallas_call(
        body, name=name,
        grid_spec=pltpu.PrefetchScalarGridSpec(num_scalar_prefetch=1, grid=(2, nb),
                                               in_specs=[whole, part, part, whole, whole], out_specs=[whole] * 4),
        out_shape=[jax.ShapeDtypeStruct(w.shape, F32)] * 4, compiler_params=_arb(2))(core, w, mine, other, m, v)


def adamw_sum(w, mine, other, m, v, name):
    R, C = w.shape[-2:]
    lead = (None,) * (w.ndim - 2)
    zeros = (0,) * (w.ndim - 2)
    tr = next(t for t in range(256, 7, -1) if R % t == 0 and t % 8 == 0)
    whole = BS(lead + (tr, C), lambda j: zeros + (j, 0))
    part = BS((tr, C), lambda j: (j, 0))

    def body(w_ref, a_ref, b_ref, m_ref, v_ref, g_ref, d_ref, mo_ref, vo_ref):
        g = a_ref[...] + b_ref[...]
        g_ref[...] = g
        d_ref[...], mo_ref[...], vo_ref[...] = _adamw_math(w_ref[...], g, m_ref[...], v_ref[...])

    return pl.pallas_call(
        body, name=name, grid=(R // tr,), in_specs=[whole, part, part, whole, whole], out_specs=[whole] * 4,
        out_shape=[jax.ShapeDtypeStruct(w.shape, F32)] * 4, compiler_params=_arb(1))(w, mine, other, m, v)


def dense_bf16(w3, name):
    R, _, K = w3.shape
    kh = K // 2

    def body(w_hbm, o_ref, buf, sem):
        cp = pltpu.make_async_copy(w_hbm.at[:, 0], buf, sem)
        cp.start()
        cp.wait()
        o_ref[0] = buf[:, :kh].astype(BF16)
        o_ref[1] = buf[:, kh:].astype(BF16)

    return pl.pallas_call(
        body, name=name, in_specs=[ANY], out_specs=BS(memory_space=pltpu.VMEM), out_shape=jax.ShapeDtypeStruct((2, R, kh), BF16),
        scratch_shapes=[pltpu.VMEM((R, K), F32), pltpu.SemaphoreType.DMA(())])(w3)


ROW_BLOCK = 184


def adamw_untiled_rows(w3, mine, other, m3, v3, name):
    R, _, K = w3.shape
    kh = K // 2
    starts = list(range(0, R, ROW_BLOCK))
    sizes = [min(ROW_BLOCK, R - s) for s in starts]
    nblk = len(starts)

    def body(w_hbm, a_ref, b_ref, m_hbm, v_hbm, g_hbm, d_hbm, mo_hbm, vo_hbm,
             wbuf, mbuf, vbuf, gbuf, dbuf, mobuf, vobuf, in_sems, out_sems):
        first = lax.axis_index("c") == 0
        ins = []
        for k, (r0, n) in enumerate(zip(starts, sizes)):
            rows = pl.ds(r0, n)
            cps = [pltpu.make_async_copy(src.at[rows, 0], dst.at[rows], in_sems.at[3 * k + i])
                   for i, (src, dst) in enumerate(((w_hbm, wbuf), (m_hbm, mbuf), (v_hbm, vbuf)))]
            for cp in cps:
                cp.start()
            ins.append(cps)

        def update(rows):
            a, b = a_ref[rows, :], b_ref[rows, :]
            g = jnp.concatenate([jnp.where(first, a, b), jnp.where(first, b, a)], axis=1)
            gbuf[rows, :] = g
            dbuf[rows, :], mobuf[rows, :], vobuf[rows, :] = _adamw_math(wbuf[rows, :], g, mbuf[rows, :], vbuf[rows, :])

        outs = []
        for k, (r0, n) in enumerate(zip(starts, sizes)):
            for cp in ins[k]:
                cp.wait()
            groups, tail = n // 8, n % 8

            def group(i, carry, r0=r0):
                update(pl.ds(pl.multiple_of(r0 + i * 8, 8), 8))
                return carry

            lax.fori_loop(0, groups, group, 0)
            if tail:
                update(pl.ds(r0 + groups * 8, tail))
            rows = pl.ds(r0, n)
            cps = [pltpu.make_async_copy(src.at[rows], dst.at[rows, 0], out_sems.at[4 * k + i])
                   for i, (src, dst) in enumerate(((gbuf, g_hbm), (dbuf, d_hbm), (mobuf, mo_hbm), (vobuf, vo_hbm)))]
            for cp in cps:
                cp.start()
            outs += cps
        for cp in outs:
            cp.wait()

    vmem = BS(memory_space=pltpu.VMEM)
    return pl.pallas_call(
        body, name=name, in_specs=[ANY, vmem, vmem, ANY, ANY], out_specs=[ANY] * 4,
        out_shape=[jax.ShapeDtypeStruct(w3.shape, F32)] * 4,
        scratch_shapes=[pltpu.VMEM((R, K), F32)] * 7 + [pltpu.SemaphoreType.DMA((3 * nblk,)), pltpu.SemaphoreType.DMA((4 * nblk,))])(
            w3, mine, other, m3, v3)


def adamw_w_q_b(w, mine, other, m, v, name):
    def body(w_ref, a_ref, b_ref, m_ref, v_ref, g_ref, d_ref, mo_ref, vo_ref):
        first = lax.axis_index("c") == 0
        lo = jnp.where(first, a_ref[...], b_ref[...])
        hi = jnp.where(first, b_ref[...], a_ref[...])
        g = jnp.concatenate([lo, hi[0:32], hi[64:96]], axis=0)
        g_ref[...] = g
        d_ref[...], mo_ref[...], vo_ref[...] = _adamw_math(w_ref[...], g, m_ref[...], v_ref[...])

    return pl.pallas_call(body, name=name, out_shape=[jax.ShapeDtypeStruct(w.shape, F32)] * 4)(w, mine, other, m, v)


def local_step(x, mem, positions, tgt, norm_in, weights, big_grads_ready, q_a_norm, kv_a_norm, gdn_conv, gdn_a_log,
               gdn_dt_bias, gdn_norm, mem_norm, norm_final):
    B, S, D = x.shape
    M = mem.shape[1]
    T = B * S
    N = S // CHUNK
    x2d = x.reshape(T, D)
    mem2d = mem.reshape(B * M, D)
    tgt2d = tgt.reshape(T, D)

    alog_row, dt_row = _lane_row(gdn_a_log), _lane_row(gdn_dt_bias)

    half = MLA_ROPE // 2
    inv_freq = 1.0 / (ROPE_THETA ** (jnp.arange(half, dtype=F32) / half))
    z32 = jnp.zeros((half,), F32)
    o32 = jnp.ones((half,), F32)
    inv_row = jnp.concatenate([inv_freq, z32, inv_freq, z32]).reshape(1, 128)
    sgn_row = jnp.concatenate([-o32, z32, o32, z32]).reshape(1, 128)
    msk_row = jnp.concatenate([o32, z32, o32, z32]).reshape(1, 128)
    cos_t, sin_t = rope_tables(positions.reshape(T, 1), inv_row, sgn_row, msk_row, after=weights[3])

    h = rms_fwd(x2d, norm_in, "rms_in", after=weights[3])
    wp_early = weights[0]((h, cos_t))
    P_early = mm(h, wp_early, "nt", F32, "in_proj_early", bm=1024, bn=OFF_GDN, n_outer=True, col_tiles=(0, 1))
    wq, wkv = weights[1](P_early)
    Q, K, V, qn, kvn = mla_prep(P_early, q_a_norm, kv_a_norm, wq, wkv, cos_t, sin_t)
    o_mla, lse = mla_attn_fwd(Q, K, V, B, S)
    wp = weights[4]((o_mla, P_early), wp_early)
    P = mm(h, wp, "nt", F32, "in_proj", bm=1024, bn=OFF_GDN, n_outer=True, col_tiles=(1, N_PAD // OFF_GDN), into=P_early)
    qkv = gdn_prep_fwd(P, gdn_conv, B, S)
    GB = gdn_gate_fwd(P, alog_row, dt_row, B, S)
    Grow = jnp.transpose(GB[:, :N_HEADS].reshape(B, N, CHUNK, N_HEADS), (0, 3, 1, 2))
    U, W, Tinv, A = gdn_chunk_fwd(qkv, GB, Grow, B, S)
    qkv3, GB3 = qkv.reshape(B, S, GDN_QKV), GB.reshape(B, S, 128)
    W3 = W.reshape(B, S, 512)
    o_gdn3, Vn3, St = gdn_scan_fwd(qkv3, U.reshape(B, S, 512), W3, GB3, A, B, S)
    o_gdn = o_gdn3.reshape(T, 512)
    w_mem_kv, w_out = weights[2](o_gdn)
    memn = rms_fwd(mem2d, mem_norm, "rms_mem")
    MKV = mm(memn, w_mem_kv, "nn", BF16, "mem_kv_proj")
    o_mem = mem_attn_fwd(P, MKV, B, S, M)
    mixed, dx2, dx2b, sq, g_norm_final = merge_fwd(o_mla, o_gdn, o_mem, P, x2d, tgt2d, w_out, gdn_norm, norm_final.reshape(1, D))

    g_w_out = mm(mixed, dx2b, "tn", BF16, "grad_w_out")
    dgate, do_mla, do_gdn, do_mem, g_gdn_norm = merge_bwd(dx2b, o_mla, o_gdn, o_mem, P, w_out, gdn_norm)

    dmemq, dMKV = mem_attn_bwd(P, MKV, do_mem, B, S, M)
    g_w_mem_kv = mm(memn, dMKV, "tn", BF16, "grad_w_mem_kv")
    started_early = big_grads_ready(dict(w_mem_kv=g_w_mem_kv, w_out=g_w_out), "early")
    dmemn = mm(dMKV, w_mem_kv, "nt", F32, "d_memn", after=(started_early,))
    g_mem_norm = gain_grad(mem2d, dmemn, "grad_mem_norm")

    dU3, dW3, dQ13, dK13, dA, dG13 = gdn_scan_bwd(do_gdn.reshape(B, S, 512), qkv3, W3, Vn3, GB3, A, St, B, S, g_mem_norm)
    r2 = lambda a: a.reshape(T, a.shape[-1])
    dqkv, dGB = gdn_chunk_bwd(qkv, GB, Grow, Tinv, dA, r2(dU3), r2(dW3), r2(dQ13), r2(dK13), r2(dG13), B, S)
    dPg, g_conv = gdn_prep_bwd(P, dqkv, gdn_conv, B, S)
    dab, g_ab = gdn_gate_bwd(P, dGB, alog_row, dt_row, B, S)

    dQ, dK, dV = mla_attn_bwd(Q, K, V, o_mla, do_mla, lse, B, S)
    dq_lin, dkv_lin, dPm, g_q_a_norm, g_kv_a_norm = mla_proj_bwd(dQ, dK, dV, cos_t, sin_t, P, dab, wq, wkv, q_a_norm, kv_a_norm)
    g_wq = mm(dq_lin, qn, "tn", BF16, "grad_w_q_b")
    g_wkv = mm(kvn, dkv_lin, "tn", BF16, "grad_w_kv_b")

    dP = [dPm, dmemq, dPg, dgate]
    g_wp = mm_cols_tn(dP, h, BF16, "grad_w_in")
    started = big_grads_ready(dict(w_in=g_wp, w_q_b=g_wq, w_kv_b=g_wkv), "late")
    grad_x, g_norm_in = in_proj_bwd(dP, wp, x2d, dx2, norm_in, started)

    grads = dict(
        norm_in=g_norm_in, q_a_norm=g_q_a_norm, kv_a_norm=g_kv_a_norm, gdn_conv=g_conv,
        gdn_a_log_dt_bias=g_ab, gdn_norm=g_gdn_norm,
        mem_norm=g_mem_norm, norm_final=g_norm_final)
    return sq, grad_x.reshape(B, S, D), grads


def kernel(x, mem, positions, norm_in, w_in, q_a_norm, w_q_b, kv_a_norm, w_kv_b, gdn_conv, gdn_a_log, gdn_dt_bias, gdn_norm, mem_norm, w_mem_kv, w_out, norm_final, loss_target, m_norm_in, m_w_in, m_q_a_norm, m_w_q_b, m_kv_a_norm, m_w_kv_b, m_gdn_conv, m_gdn_a_log, m_gdn_dt_bias, m_gdn_norm, m_mem_norm, m_w_mem_kv, m_w_out, m_norm_final, v_norm_in, v_w_in, v_q_a_norm, v_w_q_b, v_kv_a_norm, v_w_kv_b, v_gdn_conv, v_gdn_a_log, v_gdn_dt_bias, v_gdn_norm, v_mem_norm, v_w_mem_kv, v_w_out, v_norm_final):
    B = x.shape[0]
    cx, cy, cc = lax.axis_index("x"), lax.axis_index("y"), lax.axis_index("c")
    chip = 2 * cx + cy

    big_names = ("w_in", "w_q_b", "w_kv_b", "w_mem_kv", "w_out")
    rows_major = lambda a: jnp.transpose(a, (2, 0, 1))
    w_in3, m_in3, v_in3 = rows_major(w_in), rows_major(m_w_in), rows_major(v_w_in)
    w_qb_t, m_qb_t, v_qb_t = jnp.transpose(w_q_b[0]), jnp.transpose(m_w_q_b[0]), jnp.transpose(v_w_q_b[0])
    z32 = jnp.zeros((32, Q_LORA), BF16)
    qb_bf = w_qb_t.astype(BF16)
    qb_padded = jnp.concatenate([qb_bf[:160], z32, qb_bf[160:], z32])
    shards = [dense_bf16(w_in3, "w_in_bf16"), qb_padded, w_kv_b[0].astype(BF16), w_mem_kv[0].astype(BF16), w_out[0].astype(BF16)]
    splits = [(1, D_MODEL // 2)] + [(0, s.shape[0] // 2) for s in shards[1:]]
    early_row0 = jnp.where(chip == 2, EARLY_ROW0[1], EARLY_ROW0[0])
    *early_flight, early_started = half_gather_start(lax.dynamic_slice_in_dim(shards[0], early_row0, EARLY_ROWS, axis=1),
                                                     "w_in_early_gather_start", early=True)
    *late_a, started_a = late_gather_start(shards[1:3], early_started, "late_gather_qkv_start")
    *w_in_flight, w_in_started = half_gather_start(shards[0], "w_in_gather_start", after=(started_a,))
    *late_b, started_b = late_gather_start(shards[3:], w_in_started, "late_gather_mem_out_start")
    conv_all = allgather_devices(gdn_conv[0], "allgather_conv")
    conv_cols = gdn_conv.shape[2]
    conv_full = jnp.transpose(conv_all[0::2], (1, 0, 2)).reshape(GDN_CONV, N_CHIPS * conv_cols)
    late_shapes = [(N_CHIPS,) + s.shape for s in shards[1:]]

    def w_in_early_ready(after):
        g_early = pass_halves_to_sibling(half_gather_wait(*early_flight, after, "w_in_early_gather_wait", early=True),
                                         "w_in_early_gather_sibling", early=True)
        return pad_w_in_t_early(g_early)

    def w_in_ready(after, wp):
        g_in = pass_halves_to_sibling(half_gather_wait(*w_in_flight, after, "w_in_gather_wait"), "w_in_gather_sibling")
        return pad_w_in_t_rest(g_in, wp)

    def late_qkv(after):
        g_qb, g_kvb = late_gather_wait(*late_a, after, "late_gather_qkv_wait")
        return g_qb.reshape(-1, Q_LORA), _perm_w_kv_b(g_kvb)

    def late_mem_out(after):
        g_mem, g_out_w = late_gather_wait(*late_b, after, "late_gather_mem_out_wait")
        return g_mem.reshape(-1, g_mem.shape[2]), g_out_w.reshape(-1, g_out_w.shape[2])

    weights = (w_in_early_ready, late_qkv, late_mem_out, (started_b,), w_in_ready)

    core = jnp.stack([cc]).astype(jnp.int32)
    chip_core = jnp.stack([chip, cc]).astype(jnp.int32)
    exchanges = {}
    by_chip = dict(w_in=unpad_w_in_t, w_q_b=lambda a: a.reshape(late_shapes[0]), w_kv_b=_unperm_w_kv_b,
                   w_mem_kv=lambda a: a.reshape(late_shapes[2]), w_out=lambda a: a.reshape(late_shapes[3]))

    def big_grads_ready(gb, group):
        idx = [big_names.index(n) for n in gb]
        parts = [by_chip[n](a) for n, a in gb.items()]
        sp = [splits[i] for i in idx]
        if group == "early":
            sems, parts_thru, lands, token = exchange_chips_start(parts, "rs_exchange_start_" + group)
            exchanges[group] = dict(idx=idx, whole=True, sems=sems, sums=parts_thru, lands=lands)
            return token
        from_sibling = swap_sibling(parts, "rs_sibling_partial_" + group, sp)
        chip_sums = add_pairs(parts, from_sibling, sp, core, "rs_add_sibling_" + group)
        sems, sums_thru, lands, token = exchange_chips_start(chip_sums, "rs_exchange_start_" + group)
        exchanges[group] = dict(idx=idx, parts=parts, from_sibling=from_sibling, sems=sems, sums=sums_thru, lands=lands)
        return token

    sq, grad_x, g = local_step(x, mem, positions, loss_target, norm_in, weights, big_grads_ready, q_a_norm, kv_a_norm, conv_full,
                               gdn_a_log, gdn_dt_bias, gdn_norm, mem_norm, norm_final)

    small_names = ("norm_in", "q_a_norm", "kv_a_norm", "gdn_a_log", "gdn_dt_bias", "gdn_norm", "mem_norm", "norm_final")
    small = dict(norm_in=norm_in, q_a_norm=q_a_norm, kv_a_norm=kv_a_norm, gdn_a_log=gdn_a_log, gdn_dt_bias=gdn_dt_bias,
                 gdn_norm=gdn_norm, mem_norm=mem_norm, norm_final=norm_final)
    m_small = dict(norm_in=m_norm_in, q_a_norm=m_q_a_norm, kv_a_norm=m_kv_a_norm, gdn_a_log=m_gdn_a_log,
                   gdn_dt_bias=m_gdn_dt_bias, gdn_norm=m_gdn_norm, mem_norm=m_mem_norm, norm_final=m_norm_final)
    v_small = dict(norm_in=v_norm_in, q_a_norm=v_q_a_norm, kv_a_norm=v_kv_a_norm, gdn_a_log=v_gdn_a_log,
                   gdn_dt_bias=v_gdn_dt_bias, gdn_norm=v_gdn_norm, mem_norm=v_mem_norm, norm_final=v_norm_final)
    conv_rows = GDN_CONV * GDN_QKV // 1024
    g_block = pack_small([g.get(n) for n in small_names], g["gdn_a_log_dt_bias"], g["gdn_conv"], sq)
    g_block = sum_leading(allgather_devices(g_block, "allgather_small_grads"), "sum_small_grads")
    loss = 0.5 * jnp.sum(g_block[LOSS_ROW]) / D_MODEL
    g_conv = lax.dynamic_slice_in_dim(g_block[CONV_ROW0:CONV_ROW0 + conv_rows].reshape(GDN_CONV, GDN_QKV), chip * conv_cols,
                                      conv_cols, axis=1)
    as_row = lambda a: a.reshape(1, -1)
    updated = adamw_small(g_block, [as_row(small[n]) for n in small_names], [as_row(m_small[n]) for n in small_names],
                          [as_row(v_small[n]) for n in small_names])
    g_out, d_out, m_out, v_out = ({n: u[i].reshape(small[n].shape) for n, u in zip(small_names, updated)} for i in range(4))
    d_s = d_out["norm_in"]

    my_half = [None] * len(big_names)
    for group, e in exchanges.items():
        sp = [splits[i] for i in e["idx"]]
        sent, from_chips = exchange_chips_wait(e["sems"], e["sums"], e["lands"], d_s, "rs_exchange_wait_" + group)
        if e.get("whole"):
            halves = add_fours(sent, from_chips, chip_core, "rs_add_chips_" + group)
        else:
            halves = add_fives(e["parts"], e["from_sibling"], from_chips, sp, chip_core, "rs_add_chips_" + group)
        for i, a in zip(e["idx"], halves):
            my_half[i] = a
    other_half = swap_sibling(my_half, "rs_sibling_final")

    d_out["gdn_conv"], m_out["gdn_conv"], v_out["gdn_conv"] = adamw(gdn_conv, g_conv, m_gdn_conv, v_gdn_conv, "adamw_gdn_conv")
    g_out["gdn_conv"] = g_conv[None]
    res = adamw_untiled_rows(w_in3, my_half[0], other_half[0], m_in3, v_in3, "adamw_w_in")
    g_out["w_in"], d_out["w_in"], m_out["w_in"], v_out["w_in"] = [jnp.transpose(r, (1, 2, 0)) for r in res]
    res = adamw_w_q_b(w_qb_t, my_half[1], other_half[1], m_qb_t, v_qb_t, "adamw_w_q_b")
    g_out["w_q_b"], d_out["w_q_b"], m_out["w_q_b"], v_out["w_q_b"] = [jnp.transpose(r)[None] for r in res]
    rest = dict(w_kv_b=(w_kv_b, m_w_kv_b, v_w_kv_b), w_mem_kv=(w_mem_kv, m_w_mem_kv, v_w_mem_kv), w_out=(w_out, m_w_out, v_w_out))
    for i, n in enumerate(big_names):
        if n in rest:
            w_n, m_n, v_n = rest[n]
            if any(e.get("whole") and i in e["idx"] for e in exchanges.values()):
                g_out[n], d_out[n], m_out[n], v_out[n] = adamw_sum(w_n, my_half[i], other_half[i], m_n, v_n, "adamw_" + n)
            else:
                g_out[n], d_out[n], m_out[n], v_out[n] = adamw_halves(w_n, my_half[i], other_half[i], m_n, v_n, splits[i], core,
                                                                      "adamw_" + n)

    order = ("norm_in", "w_in", "q_a_norm", "w_q_b", "kv_a_norm", "w_kv_b", "gdn_conv", "gdn_a_log", "gdn_dt_bias",
             "gdn_norm", "mem_norm", "w_mem_kv", "w_out", "norm_final")
    return (loss, grad_x, *[g_out[n] for n in order], *[d_out[n] for n in order], *[m_out[n] for n in order],
            *[v_out[n] for n in order])
```
